```python
import jax, jax.numpy as jnp
from jax import lax
import numpy as np

D_MODEL = 1024
BATCH = 8
SEQ = 2048
DEPTH = 2

MLA_HEADS = 8
QK_NOPE_DIM = 64
QK_ROPE_DIM = 32
QK_HEAD_DIM = QK_NOPE_DIM + QK_ROPE_DIM
V_HEAD_DIM = 64
Q_LORA_RANK = 384
KV_LORA_RANK = 256
ROPE_THETA = 10000.0
Q_BLOCK = 128
RWKV_HEAD_DIM = 64
RWKV_HEADS = 4
RWKV_WIDTH = RWKV_HEADS * RWKV_HEAD_DIM
DECAY_LORA = 64
AAA_LORA = 64
GATE_LORA = 128
MV_LORA = 32
GN_EPS = 64e-5
CONV_WIDTH = 256
CONV_K = 3
D_FF = 4 * D_MODEL
N_BRANCH = 3
NORM_EPS = 1e-6

GATE_COLS = N_BRANCH * D_MODEL
MLA_COLS = Q_LORA_RANK + KV_LORA_RANK + QK_ROPE_DIM
RWKV_COLS = 3 * RWKV_WIDTH + DECAY_LORA + AAA_LORA + GATE_LORA
CONV_COLS = 3 * CONV_WIDTH
IN_COLS = GATE_COLS + MLA_COLS + RWKV_COLS + CONV_COLS

kernel_name = 'hybrid_mla_rwkv7_shortconv_block'


def _split(x, sizes):
    idx = np.cumsum(sizes)[:-1].tolist()
    return jnp.split(x, idx, axis=-1)


def rms_norm(x, g, eps=NORM_EPS):
    xf = x.astype(jnp.float32)
    y = xf * lax.rsqrt(jnp.mean(xf * xf, axis=-1, keepdims=True) + eps)
    return (y * g.astype(jnp.float32)).astype(x.dtype)


def token_shift(x):
    return jnp.pad(x, ((0, 0), (1, 0), (0, 0)))[:, :-1]


def rope_tables(positions):
    freqs = ROPE_THETA ** (-(jnp.arange(QK_ROPE_DIM // 2, dtype=jnp.float32) * 2.0 / QK_ROPE_DIM))
    ang = positions.astype(jnp.float32)[..., None] * freqs
    return jnp.cos(ang)[:, :, None, :], jnp.sin(ang)[:, :, None, :]


def apply_rope(x, cos, sin):
    x1, x2 = x[..., :QK_ROPE_DIM // 2], x[..., QK_ROPE_DIM // 2:]
    cos, sin = cos.astype(x.dtype), sin.astype(x.dtype)
    return jnp.concatenate([x1 * cos - x2 * sin, x1 * sin + x2 * cos], axis=-1)


def causal_block_attention(q, k, v):
    scale = QK_HEAD_DIM ** -0.5
    qh, kh, vh = (jnp.swapaxes(t, 1, 2) for t in (q, k, v))
    seq = qh.shape[2]
    outs = []
    for start in range(0, seq, Q_BLOCK):
        stop = start + Q_BLOCK
        s = jnp.einsum('bhqd,bhkd->bhqk', qh[:, :, start:stop], kh[:, :, :stop]).astype(jnp.float32) * scale
        causal = (start + jnp.arange(Q_BLOCK))[:, None] >= jnp.arange(stop)[None, :]
        p = jax.nn.softmax(jnp.where(causal, s, -jnp.inf), axis=-1)
        outs.append(jnp.einsum('bhqk,bhkd->bhqd', p.astype(vh.dtype), vh[:, :, :stop]))
    return jnp.swapaxes(jnp.concatenate(outs, axis=2), 1, 2)


def mla_branch(cols, positions, q_a_norm, wq_b, kv_a_norm, wkv_b, q_norm, k_norm):
    bsz, seq, _ = cols.shape
    c_q, c_kv, k_pe = _split(cols, [Q_LORA_RANK, KV_LORA_RANK, QK_ROPE_DIM])
    q = (rms_norm(c_q, q_a_norm) @ wq_b).reshape(bsz, seq, MLA_HEADS, QK_HEAD_DIM)
    kv = (rms_norm(c_kv, kv_a_norm) @ wkv_b).reshape(bsz, seq, MLA_HEADS, QK_NOPE_DIM + V_HEAD_DIM)
    k_nope, v = kv[..., :QK_NOPE_DIM], kv[..., QK_NOPE_DIM:]
    k_pe = jnp.broadcast_to(k_pe[:, :, None, :], (bsz, seq, MLA_HEADS, QK_ROPE_DIM))
    k = jnp.concatenate([k_nope, k_pe], axis=-1)
    q = rms_norm(q, q_norm)
    k = rms_norm(k, k_norm)
    cos, sin = rope_tables(positions)
    q = jnp.concatenate([q[..., :QK_NOPE_DIM], apply_rope(q[..., QK_NOPE_DIM:], cos, sin)], axis=-1)
    k = jnp.concatenate([k[..., :QK_NOPE_DIM], apply_rope(k[..., QK_NOPE_DIM:], cos, sin)], axis=-1)
    o = causal_block_attention(q, k, v)
    return o.reshape(bsz, seq, MLA_HEADS * V_HEAD_DIM)


def wkv7_scan(r, w, k, v, a, b):
    def step(state, inp):
        r_t, w_t, k_t, v_t, a_t, b_t = inp
        sa = jnp.einsum('bhvk,bhk->bhv', state, a_t)
        state = state * w_t[:, :, None, :] + sa[..., None] * b_t[:, :, None, :] + v_t[..., None] * k_t[:, :, None, :]
        return state, jnp.einsum('bhvk,bhk->bhv', state, r_t)
    bsz, _, heads, n = r.shape
    xs = tuple(jnp.moveaxis(t, 1, 0) for t in (r, w, k, v, a, b))
    s0 = jnp.zeros((bsz, heads, n, n), jnp.float32)
    _, ys = lax.scan(step, s0, xs)
    return jnp.moveaxis(ys, 0, 1)


def rwkv7_branch(cols, h, v_first, mu, w0, w2, a0, a2, g2, k_k, k_a, r_k, ln_w, ln_b, vres):
    bsz, seq, _ = cols.shape
    f32 = jnp.float32
    cols = cols + (token_shift(cols) - cols) * mu
    r, k, v, xw, xa, xg = _split(cols, [RWKV_WIDTH, RWKV_WIDTH, RWKV_WIDTH, DECAY_LORA, AAA_LORA, GATE_LORA])
    log_w = -jax.nn.softplus(-(w0 + jnp.tanh(xw) @ w2).astype(f32)) - 0.5
    decay = jnp.exp(-jnp.exp(log_w))
    a = jax.nn.sigmoid(a0 + xa @ a2)
    g = jax.nn.sigmoid(xg) @ g2
    if vres is None:
        v_first = v
    else:
        v1, v_mu, v0, v2 = vres
        xv = h @ v1
        xv = xv + (token_shift(xv) - xv) * v_mu
        v = v + (v_first - v) * jax.nn.sigmoid(v0 + xv @ v2)
    heads = lambda t: t.reshape(bsz, seq, RWKV_HEADS, RWKV_HEAD_DIM).astype(f32)
    kk = heads(k * k_k)
    kk = kk / jnp.maximum(jnp.linalg.norm(kk, axis=-1, keepdims=True), 1e-12)
    k = k * (1 + (a - 1) * k_a)
    rh, kh, vh, ah = heads(r), heads(k), heads(v), heads(a)
    y = wkv7_scan(rh, heads(decay), kh, vh, -kk, kk * ah)
    mean = jnp.mean(y, axis=-1, keepdims=True)
    var = jnp.mean(jnp.square(y - mean), axis=-1, keepdims=True)
    y = ((y - mean) * lax.rsqrt(var + GN_EPS)).reshape(bsz, seq, RWKV_WIDTH)
    y = y * ln_w.astype(f32) + ln_b.astype(f32)
    bonus = jnp.sum(rh * kh * r_k.astype(f32), axis=-1, keepdims=True) * vh
    y = (y + bonus.reshape(bsz, seq, RWKV_WIDTH)).astype(cols.dtype)
    return y * g, v_first


def short_conv_branch(cols, conv_w):
    b_gate, c_gate, xc = _split(cols, [CONV_WIDTH, CONV_WIDTH, CONV_WIDTH])
    u = c_gate * xc
    y = lax.conv_general_dilated(
        u, conv_w.astype(u.dtype)[:, None, :], window_strides=(1,), padding=[(CONV_K - 1, 0)],
        dimension_numbers=('NWC', 'WIO', 'NWC'), feature_group_count=CONV_WIDTH)
    return b_gate * y


def _fwd_setup_inputs(seed: int = 0) -> dict:
    key = jax.random.key(seed)
    ks = iter(jax.random.split(key, 64))
    nrm = lambda shape, scale: jax.random.normal(next(ks), shape, jnp.float32) * scale
    gain = lambda shape: 1.0 + nrm(shape, 0.05)
    unif = lambda shape, lo, hi: jax.random.uniform(next(ks), shape, jnp.float32, lo, hi)
    L, Lv = DEPTH, DEPTH - 1
    x = nrm((BATCH, SEQ, D_MODEL), 1.0)
    offsets = jax.random.randint(next(ks), (BATCH, 1), 0, 4096, dtype=jnp.int32)
    positions = offsets + jnp.arange(SEQ, dtype=jnp.int32)[None, :]
    return {
        'x': x,
        'positions': positions,
        'attn_norm': gain((L, D_MODEL)),
        'w_in': nrm((L, D_MODEL, IN_COLS), D_MODEL ** -0.5),
        'mla_q_a_norm': gain((L, Q_LORA_RANK)),
        'mla_wq_b': nrm((L, Q_LORA_RANK, MLA_HEADS * QK_HEAD_DIM), Q_LORA_RANK ** -0.5),
        'mla_kv_a_norm': gain((L, KV_LORA_RANK)),
        'mla_wkv_b': nrm((L, KV_LORA_RANK, MLA_HEADS * (QK_NOPE_DIM + V_HEAD_DIM)), KV_LORA_RANK ** -0.5),
        'mla_q_norm': gain((L, QK_HEAD_DIM)),
        'mla_k_norm': gain((L, QK_HEAD_DIM)),
        'mla_w_o': nrm((L, MLA_HEADS * V_HEAD_DIM, D_MODEL), (MLA_HEADS * V_HEAD_DIM) ** -0.5),
        'rwkv_mu': unif((L, RWKV_COLS), 0.0, 1.0),
        'rwkv_w0': unif((L, RWKV_WIDTH), -6.0, -1.0),
        'rwkv_w2': nrm((L, DECAY_LORA, RWKV_WIDTH), 0.5 * DECAY_LORA ** -0.5),
        'rwkv_a0': nrm((L, RWKV_WIDTH), 0.1),
        'rwkv_a2': nrm((L, AAA_LORA, RWKV_WIDTH), 0.5 * AAA_LORA ** -0.5),
        'rwkv_g2': nrm((L, GATE_LORA, RWKV_WIDTH), GATE_LORA ** -0.5),
        'rwkv_k_k': 0.85 + nrm((L, RWKV_WIDTH), 0.05),
        'rwkv_k_a': gain((L, RWKV_WIDTH)),
        'rwkv_r_k': nrm((L, RWKV_HEADS, RWKV_HEAD_DIM), 0.1),
        'rwkv_ln_w': gain((L, RWKV_WIDTH)),
        'rwkv_ln_b': nrm((L, RWKV_WIDTH), 0.02),
        'rwkv_w_o': nrm((L, RWKV_WIDTH, D_MODEL), RWKV_WIDTH ** -0.5),
        'rwkv_v1': nrm((Lv, D_MODEL, MV_LORA), D_MODEL ** -0.5),
        'rwkv_v_mu': unif((Lv, MV_LORA), 0.0, 1.0),
        'rwkv_v0': nrm((Lv, RWKV_WIDTH), 0.1),
        'rwkv_v2': nrm((Lv, MV_LORA, RWKV_WIDTH), 0.5 * MV_LORA ** -0.5),
        'conv_w': nrm((L, CONV_K, CONV_WIDTH), CONV_K ** -0.5),
        'conv_w_o': nrm((L, CONV_WIDTH, D_MODEL), CONV_WIDTH ** -0.5),
        'w_out': nrm((L, D_MODEL, D_MODEL), D_MODEL ** -0.5),
        'mlp_norm': gain((L, D_MODEL)),
        'w_up': nrm((L, D_MODEL, D_FF), D_MODEL ** -0.5),
        'w_down': nrm((L, D_FF, D_MODEL), D_FF ** -0.5),
    }


def _fwd_reference(x, positions, attn_norm, w_in, mla_q_a_norm, mla_wq_b, mla_kv_a_norm, mla_wkv_b,
              mla_q_norm, mla_k_norm, mla_w_o, rwkv_mu, rwkv_w0, rwkv_w2, rwkv_a0, rwkv_a2, rwkv_g2,
              rwkv_k_k, rwkv_k_a, rwkv_r_k, rwkv_ln_w, rwkv_ln_b, rwkv_w_o, rwkv_v1, rwkv_v_mu,
              rwkv_v0, rwkv_v2, conv_w, conv_w_o, w_out, mlp_norm, w_up, w_down):
    bsz, seq, _ = x.shape
    v_first = None
    for l in range(DEPTH):
        h = rms_norm(x, attn_norm[l])
        proj = h @ w_in[l]
        gate_cols, mla_cols, rwkv_cols, conv_cols = _split(proj, [GATE_COLS, MLA_COLS, RWKV_COLS, CONV_COLS])
        o_a = mla_branch(mla_cols, positions, mla_q_a_norm[l], mla_wq_b[l], mla_kv_a_norm[l],
                         mla_wkv_b[l], mla_q_norm[l], mla_k_norm[l]) @ mla_w_o[l]
        vres = None if l == 0 else (rwkv_v1[l - 1], rwkv_v_mu[l - 1], rwkv_v0[l - 1], rwkv_v2[l - 1])
        o_b, v_first = rwkv7_branch(rwkv_cols, h, v_first, rwkv_mu[l], rwkv_w0[l], rwkv_w2[l], rwkv_a0[l],
                                    rwkv_a2[l], rwkv_g2[l], rwkv_k_k[l], rwkv_k_a[l], rwkv_r_k[l],
                                    rwkv_ln_w[l], rwkv_ln_b[l], vres)
        o_b = o_b @ rwkv_w_o[l]
        o_c = short_conv_branch(conv_cols, conv_w[l]) @ conv_w_o[l]
        g = jax.nn.sigmoid(gate_cols).reshape(bsz, seq, N_BRANCH, D_MODEL)
        merged = g[:, :, 0] * o_a + g[:, :, 1] * o_b + g[:, :, 2] * o_c
        x = x + merged @ w_out[l]
        h2 = rms_norm(x, mlp_norm[l])
        x = x + jnp.square(jax.nn.relu(h2 @ w_up[l])) @ w_down[l]
    return x


import jax as _jax
import jax.numpy as _jnp

TWIN_FORMAT = 'train_step'
FWD_PARAMS = ['x', 'positions', 'attn_norm', 'w_in', 'mla_q_a_norm', 'mla_wq_b', 'mla_kv_a_norm', 'mla_wkv_b', 'mla_q_norm', 'mla_k_norm', 'mla_w_o', 'rwkv_mu', 'rwkv_w0', 'rwkv_w2', 'rwkv_a0', 'rwkv_a2', 'rwkv_g2', 'rwkv_k_k', 'rwkv_k_a', 'rwkv_r_k', 'rwkv_ln_w', 'rwkv_ln_b', 'rwkv_w_o', 'rwkv_v1', 'rwkv_v_mu', 'rwkv_v0', 'rwkv_v2', 'conv_w', 'conv_w_o', 'w_out', 'mlp_norm', 'w_up', 'w_down']
TWIN_WEIGHTS = ['attn_norm', 'w_in', 'mla_q_a_norm', 'mla_wq_b', 'mla_kv_a_norm', 'mla_wkv_b', 'mla_q_norm', 'mla_k_norm', 'mla_w_o', 'rwkv_mu', 'rwkv_w0', 'rwkv_w2', 'rwkv_a0', 'rwkv_a2', 'rwkv_g2', 'rwkv_k_k', 'rwkv_k_a', 'rwkv_r_k', 'rwkv_ln_w', 'rwkv_ln_b', 'rwkv_w_o', 'rwkv_v1', 'rwkv_v_mu', 'rwkv_v0', 'rwkv_v2', 'conv_w', 'conv_w_o', 'w_out', 'mlp_norm', 'w_up', 'w_down']
TWIN_DIFF_INPUT = 'x'
TWIN_INPUTS = ['x', 'positions', 'attn_norm', 'w_in', 'mla_q_a_norm', 'mla_wq_b', 'mla_kv_a_norm', 'mla_wkv_b', 'mla_q_norm', 'mla_k_norm', 'mla_w_o', 'rwkv_mu', 'rwkv_w0', 'rwkv_w2', 'rwkv_a0', 'rwkv_a2', 'rwkv_g2', 'rwkv_k_k', 'rwkv_k_a', 'rwkv_r_k', 'rwkv_ln_w', 'rwkv_ln_b', 'rwkv_w_o', 'rwkv_v1', 'rwkv_v_mu', 'rwkv_v0', 'rwkv_v2', 'conv_w', 'conv_w_o', 'w_out', 'mlp_norm', 'w_up', 'w_down', 'loss_target', 'm_attn_norm', 'm_w_in', 'm_mla_q_a_norm', 'm_mla_wq_b', 'm_mla_kv_a_norm', 'm_mla_wkv_b', 'm_mla_q_norm', 'm_mla_k_norm', 'm_mla_w_o', 'm_rwkv_mu', 'm_rwkv_w0', 'm_rwkv_w2', 'm_rwkv_a0', 'm_rwkv_a2', 'm_rwkv_g2', 'm_rwkv_k_k', 'm_rwkv_k_a', 'm_rwkv_r_k', 'm_rwkv_ln_w', 'm_rwkv_ln_b', 'm_rwkv_w_o', 'm_rwkv_v1', 'm_rwkv_v_mu', 'm_rwkv_v0', 'm_rwkv_v2', 'm_conv_w', 'm_conv_w_o', 'm_w_out', 'm_mlp_norm', 'm_w_up', 'm_w_down', 'v_attn_norm', 'v_w_in', 'v_mla_q_a_norm', 'v_mla_wq_b', 'v_mla_kv_a_norm', 'v_mla_wkv_b', 'v_mla_q_norm', 'v_mla_k_norm', 'v_mla_w_o', 'v_rwkv_mu', 'v_rwkv_w0', 'v_rwkv_w2', 'v_rwkv_a0', 'v_rwkv_a2', 'v_rwkv_g2', 'v_rwkv_k_k', 'v_rwkv_k_a', 'v_rwkv_r_k', 'v_rwkv_ln_w', 'v_rwkv_ln_b', 'v_rwkv_w_o', 'v_rwkv_v1', 'v_rwkv_v_mu', 'v_rwkv_v0', 'v_rwkv_v2', 'v_conv_w', 'v_conv_w_o', 'v_w_out', 'v_mlp_norm', 'v_w_up', 'v_w_down']
TWIN_OUTPUTS = ['loss', 'grad_x', 'grad_attn_norm', 'grad_w_in', 'grad_mla_q_a_norm', 'grad_mla_wq_b', 'grad_mla_kv_a_norm', 'grad_mla_wkv_b', 'grad_mla_q_norm', 'grad_mla_k_norm', 'grad_mla_w_o', 'grad_rwkv_mu', 'grad_rwkv_w0', 'grad_rwkv_w2', 'grad_rwkv_a0', 'grad_rwkv_a2', 'grad_rwkv_g2', 'grad_rwkv_k_k', 'grad_rwkv_k_a', 'grad_rwkv_r_k', 'grad_rwkv_ln_w', 'grad_rwkv_ln_b', 'grad_rwkv_w_o', 'grad_rwkv_v1', 'grad_rwkv_v_mu', 'grad_rwkv_v0', 'grad_rwkv_v2', 'grad_conv_w', 'grad_conv_w_o', 'grad_w_out', 'grad_mlp_norm', 'grad_w_up', 'grad_w_down', 'delta_attn_norm', 'delta_w_in', 'delta_mla_q_a_norm', 'delta_mla_wq_b', 'delta_mla_kv_a_norm', 'delta_mla_wkv_b', 'delta_mla_q_norm', 'delta_mla_k_norm', 'delta_mla_w_o', 'delta_rwkv_mu', 'delta_rwkv_w0', 'delta_rwkv_w2', 'delta_rwkv_a0', 'delta_rwkv_a2', 'delta_rwkv_g2', 'delta_rwkv_k_k', 'delta_rwkv_k_a', 'delta_rwkv_r_k', 'delta_rwkv_ln_w', 'delta_rwkv_ln_b', 'delta_rwkv_w_o', 'delta_rwkv_v1', 'delta_rwkv_v_mu', 'delta_rwkv_v0', 'delta_rwkv_v2', 'delta_conv_w', 'delta_conv_w_o', 'delta_w_out', 'delta_mlp_norm', 'delta_w_up', 'delta_w_down', 'new_m_attn_norm', 'new_m_w_in', 'new_m_mla_q_a_norm', 'new_m_mla_wq_b', 'new_m_mla_kv_a_norm', 'new_m_mla_wkv_b', 'new_m_mla_q_norm', 'new_m_mla_k_norm', 'new_m_mla_w_o', 'new_m_rwkv_mu', 'new_m_rwkv_w0', 'new_m_rwkv_w2', 'new_m_rwkv_a0', 'new_m_rwkv_a2', 'new_m_rwkv_g2', 'new_m_rwkv_k_k', 'new_m_rwkv_k_a', 'new_m_rwkv_r_k', 'new_m_rwkv_ln_w', 'new_m_rwkv_ln_b', 'new_m_rwkv_w_o', 'new_m_rwkv_v1', 'new_m_rwkv_v_mu', 'new_m_rwkv_v0', 'new_m_rwkv_v2', 'new_m_conv_w', 'new_m_conv_w_o', 'new_m_w_out', 'new_m_mlp_norm', 'new_m_w_up', 'new_m_w_down', 'new_v_attn_norm', 'new_v_w_in', 'new_v_mla_q_a_norm', 'new_v_mla_wq_b', 'new_v_mla_kv_a_norm', 'new_v_mla_wkv_b', 'new_v_mla_q_norm', 'new_v_mla_k_norm', 'new_v_mla_w_o', 'new_v_rwkv_mu', 'new_v_rwkv_w0', 'new_v_rwkv_w2', 'new_v_rwkv_a0', 'new_v_rwkv_a2', 'new_v_rwkv_g2', 'new_v_rwkv_k_k', 'new_v_rwkv_k_a', 'new_v_rwkv_r_k', 'new_v_rwkv_ln_w', 'new_v_rwkv_ln_b', 'new_v_rwkv_w_o', 'new_v_rwkv_v1', 'new_v_rwkv_v_mu', 'new_v_rwkv_v0', 'new_v_rwkv_v2', 'new_v_conv_w', 'new_v_conv_w_o', 'new_v_w_out', 'new_v_mlp_norm', 'new_v_w_up', 'new_v_w_down']
TWIN_LEAF_KINDS = {'loss': 'loss', 'grad_x': 'grad_x', 'grad_attn_norm': 'grad_w', 'grad_w_in': 'grad_w', 'grad_mla_q_a_norm': 'grad_w', 'grad_mla_wq_b': 'grad_w', 'grad_mla_kv_a_norm': 'grad_w', 'grad_mla_wkv_b': 'grad_w', 'grad_mla_q_norm': 'grad_w', 'grad_mla_k_norm': 'grad_w', 'grad_mla_w_o': 'grad_w', 'grad_rwkv_mu': 'grad_w', 'grad_rwkv_w0': 'grad_w', 'grad_rwkv_w2': 'grad_w', 'grad_rwkv_a0': 'grad_w', 'grad_rwkv_a2': 'grad_w', 'grad_rwkv_g2': 'grad_w', 'grad_rwkv_k_k': 'grad_w', 'grad_rwkv_k_a': 'grad_w', 'grad_rwkv_r_k': 'grad_w', 'grad_rwkv_ln_w': 'grad_w', 'grad_rwkv_ln_b': 'grad_w', 'grad_rwkv_w_o': 'grad_w', 'grad_rwkv_v1': 'grad_w', 'grad_rwkv_v_mu': 'grad_w', 'grad_rwkv_v0': 'grad_w', 'grad_rwkv_v2': 'grad_w', 'grad_conv_w': 'grad_w', 'grad_conv_w_o': 'grad_w', 'grad_w_out': 'grad_w', 'grad_mlp_norm': 'grad_w', 'grad_w_up': 'grad_w', 'grad_w_down': 'grad_w', 'delta_attn_norm': 'delta_w', 'delta_w_in': 'delta_w', 'delta_mla_q_a_norm': 'delta_w', 'delta_mla_wq_b': 'delta_w', 'delta_mla_kv_a_norm': 'delta_w', 'delta_mla_wkv_b': 'delta_w', 'delta_mla_q_norm': 'delta_w', 'delta_mla_k_norm': 'delta_w', 'delta_mla_w_o': 'delta_w', 'delta_rwkv_mu': 'delta_w', 'delta_rwkv_w0': 'delta_w', 'delta_rwkv_w2': 'delta_w', 'delta_rwkv_a0': 'delta_w', 'delta_rwkv_a2': 'delta_w', 'delta_rwkv_g2': 'delta_w', 'delta_rwkv_k_k': 'delta_w', 'delta_rwkv_k_a': 'delta_w', 'delta_rwkv_r_k': 'delta_w', 'delta_rwkv_ln_w': 'delta_w', 'delta_rwkv_ln_b': 'delta_w', 'delta_rwkv_w_o': 'delta_w', 'delta_rwkv_v1': 'delta_w', 'delta_rwkv_v_mu': 'delta_w', 'delta_rwkv_v0': 'delta_w', 'delta_rwkv_v2': 'delta_w', 'delta_conv_w': 'delta_w', 'delta_conv_w_o': 'delta_w', 'delta_w_out': 'delta_w', 'delta_mlp_norm': 'delta_w', 'delta_w_up': 'delta_w', 'delta_w_down': 'delta_w', 'new_m_attn_norm': 'new_m', 'new_m_w_in': 'new_m', 'new_m_mla_q_a_norm': 'new_m', 'new_m_mla_wq_b': 'new_m', 'new_m_mla_kv_a_norm': 'new_m', 'new_m_mla_wkv_b': 'new_m', 'new_m_mla_q_norm': 'new_m', 'new_m_mla_k_norm': 'new_m', 'new_m_mla_w_o': 'new_m', 'new_m_rwkv_mu': 'new_m', 'new_m_rwkv_w0': 'new_m', 'new_m_rwkv_w2': 'new_m', 'new_m_rwkv_a0': 'new_m', 'new_m_rwkv_a2': 'new_m', 'new_m_rwkv_g2': 'new_m', 'new_m_rwkv_k_k': 'new_m', 'new_m_rwkv_k_a': 'new_m', 'new_m_rwkv_r_k': 'new_m', 'new_m_rwkv_ln_w': 'new_m', 'new_m_rwkv_ln_b': 'new_m', 'new_m_rwkv_w_o': 'new_m', 'new_m_rwkv_v1': 'new_m', 'new_m_rwkv_v_mu': 'new_m', 'new_m_rwkv_v0': 'new_m', 'new_m_rwkv_v2': 'new_m', 'new_m_conv_w': 'new_m', 'new_m_conv_w_o': 'new_m', 'new_m_w_out': 'new_m', 'new_m_mlp_norm': 'new_m', 'new_m_w_up': 'new_m', 'new_m_w_down': 'new_m', 'new_v_attn_norm': 'new_v', 'new_v_w_in': 'new_v', 'new_v_mla_q_a_norm': 'new_v', 'new_v_mla_wq_b': 'new_v', 'new_v_mla_kv_a_norm': 'new_v', 'new_v_mla_wkv_b': 'new_v', 'new_v_mla_q_norm': 'new_v', 'new_v_mla_k_norm': 'new_v', 'new_v_mla_w_o': 'new_v', 'new_v_rwkv_mu': 'new_v', 'new_v_rwkv_w0': 'new_v', 'new_v_rwkv_w2': 'new_v', 'new_v_rwkv_a0': 'new_v', 'new_v_rwkv_a2': 'new_v', 'new_v_rwkv_g2': 'new_v', 'new_v_rwkv_k_k': 'new_v', 'new_v_rwkv_k_a': 'new_v', 'new_v_rwkv_r_k': 'new_v', 'new_v_rwkv_ln_w': 'new_v', 'new_v_rwkv_ln_b': 'new_v', 'new_v_rwkv_w_o': 'new_v', 'new_v_rwkv_v1': 'new_v', 'new_v_rwkv_v_mu': 'new_v', 'new_v_rwkv_v0': 'new_v', 'new_v_rwkv_v2': 'new_v', 'new_v_conv_w': 'new_v', 'new_v_conv_w_o': 'new_v', 'new_v_w_out': 'new_v', 'new_v_mlp_norm': 'new_v', 'new_v_w_up': 'new_v', 'new_v_w_down': 'new_v'}


def _forward(args):
    return _fwd_reference(*[args[k] for k in FWD_PARAMS])


def _output_shape():
    out = _jax.eval_shape(lambda: _forward(_fwd_setup_inputs(0)))
    return out.shape, out.dtype

N_MICROBATCH = 1
ADAM_LR = 0.001
ADAM_B1 = 0.9
ADAM_B2 = 0.999
ADAM_EPS = 1e-08
ADAM_WD = 0.01
ADAM_STEP = 10
PER_EXAMPLE_BATCH_AXIS = {'x': 0, 'positions': 0, 'loss_target': 0}
SHARED_INPUTS = []
_WEIGHT_DTYPES = {'attn_norm': _jnp.float32, 'w_in': _jnp.float32, 'mla_q_a_norm': _jnp.float32, 'mla_wq_b': _jnp.float32, 'mla_kv_a_norm': _jnp.float32, 'mla_wkv_b': _jnp.float32, 'mla_q_norm': _jnp.float32, 'mla_k_norm': _jnp.float32, 'mla_w_o': _jnp.float32, 'rwkv_mu': _jnp.float32, 'rwkv_w0': _jnp.float32, 'rwkv_w2': _jnp.float32, 'rwkv_a0': _jnp.float32, 'rwkv_a2': _jnp.float32, 'rwkv_g2': _jnp.float32, 'rwkv_k_k': _jnp.float32, 'rwkv_k_a': _jnp.float32, 'rwkv_r_k': _jnp.float32, 'rwkv_ln_w': _jnp.float32, 'rwkv_ln_b': _jnp.float32, 'rwkv_w_o': _jnp.float32, 'rwkv_v1': _jnp.float32, 'rwkv_v_mu': _jnp.float32, 'rwkv_v0': _jnp.float32, 'rwkv_v2': _jnp.float32, 'conv_w': _jnp.float32, 'conv_w_o': _jnp.float32, 'w_out': _jnp.float32, 'mlp_norm': _jnp.float32, 'w_up': _jnp.float32, 'w_down': _jnp.float32}
MOMENT_SCALE = {'attn_norm': 1.440126e+01, 'w_in': 1.022836e+00, 'mla_q_a_norm': 1.257688e-01, 'mla_wq_b': 9.545057e-02, 'mla_kv_a_norm': 5.248304e+00, 'mla_wkv_b': 2.023366e+00, 'mla_q_norm': 4.146874e-01, 'mla_k_norm': 4.142001e-01, 'mla_w_o': 1.915658e+00, 'rwkv_mu': 1.747050e+00, 'rwkv_w0': 3.842915e-01, 'rwkv_w2': 7.701179e-02, 'rwkv_a0': 2.730769e-01, 'rwkv_a2': 2.120788e-01, 'rwkv_g2': 5.105265e+00, 'rwkv_k_k': 3.124523e+00, 'rwkv_k_a': 3.464707e+00, 'rwkv_r_k': 2.747943e+00, 'rwkv_ln_w': 8.112393e+00, 'rwkv_ln_b': 1.068628e+01, 'rwkv_w_o': 5.311784e-01, 'rwkv_v1': 4.058649e-01, 'rwkv_v_mu': 3.118157e-01, 'rwkv_v0': 9.494678e-01, 'rwkv_v2': 2.820132e-01, 'conv_w': 1.050866e+01, 'conv_w_o': 6.941539e-01, 'w_out': 1.903739e+00, 'mlp_norm': 4.953803e+01, 'w_up': 1.931551e+00, 'w_down': 8.225006e+00}


def _to_microbatches(a, axis):
    t = _jnp.moveaxis(a, axis, 0)
    t = t.reshape((N_MICROBATCH, t.shape[0] // N_MICROBATCH) + t.shape[1:])
    return _jnp.moveaxis(t, 1, axis + 1)


def setup_inputs(seed: int = 0) -> dict:
    inp = _fwd_setup_inputs(seed)
    key = _jax.random.fold_in(_jax.random.key(seed), 7919)
    shape, _ = _output_shape()
    out = dict(inp)
    out["loss_target"] = _jax.random.normal(_jax.random.fold_in(key, 0), shape, _jnp.float32)
    for i, name in enumerate(TWIN_WEIGHTS):
        w = inp[name].astype(_jnp.float32)
        if MOMENT_SCALE is None:
            s = _jnp.sqrt(_jnp.mean(_jnp.square(w)) + 1e-30)
        else:
            s = MOMENT_SCALE[name]
        km, kv = _jax.random.split(_jax.random.fold_in(key, i + 1))
        out[name] = w
        out["m_" + name] = s * _jax.random.normal(km, w.shape, _jnp.float32)
        out["v_" + name] = (s * s) * _jax.random.uniform(kv, w.shape, _jnp.float32, 0.5, 1.5)
    if N_MICROBATCH > 1:
        for name, axis in PER_EXAMPLE_BATCH_AXIS.items():
            out[name] = _to_microbatches(out[name], axis)
    return {'x': out['x'], 'positions': out['positions'], 'attn_norm': out['attn_norm'], 'w_in': out['w_in'], 'mla_q_a_norm': out['mla_q_a_norm'], 'mla_wq_b': out['mla_wq_b'], 'mla_kv_a_norm': out['mla_kv_a_norm'], 'mla_wkv_b': out['mla_wkv_b'], 'mla_q_norm': out['mla_q_norm'], 'mla_k_norm': out['mla_k_norm'], 'mla_w_o': out['mla_w_o'], 'rwkv_mu': out['rwkv_mu'], 'rwkv_w0': out['rwkv_w0'], 'rwkv_w2': out['rwkv_w2'], 'rwkv_a0': out['rwkv_a0'], 'rwkv_a2': out['rwkv_a2'], 'rwkv_g2': out['rwkv_g2'], 'rwkv_k_k': out['rwkv_k_k'], 'rwkv_k_a': out['rwkv_k_a'], 'rwkv_r_k': out['rwkv_r_k'], 'rwkv_ln_w': out['rwkv_ln_w'], 'rwkv_ln_b': out['rwkv_ln_b'], 'rwkv_w_o': out['rwkv_w_o'], 'rwkv_v1': out['rwkv_v1'], 'rwkv_v_mu': out['rwkv_v_mu'], 'rwkv_v0': out['rwkv_v0'], 'rwkv_v2': out['rwkv_v2'], 'conv_w': out['conv_w'], 'conv_w_o': out['conv_w_o'], 'w_out': out['w_out'], 'mlp_norm': out['mlp_norm'], 'w_up': out['w_up'], 'w_down': out['w_down'], 'loss_target': out['loss_target'], 'm_attn_norm': out['m_attn_norm'], 'm_w_in': out['m_w_in'], 'm_mla_q_a_norm': out['m_mla_q_a_norm'], 'm_mla_wq_b': out['m_mla_wq_b'], 'm_mla_kv_a_norm': out['m_mla_kv_a_norm'], 'm_mla_wkv_b': out['m_mla_wkv_b'], 'm_mla_q_norm': out['m_mla_q_norm'], 'm_mla_k_norm': out['m_mla_k_norm'], 'm_mla_w_o': out['m_mla_w_o'], 'm_rwkv_mu': out['m_rwkv_mu'], 'm_rwkv_w0': out['m_rwkv_w0'], 'm_rwkv_w2': out['m_rwkv_w2'], 'm_rwkv_a0': out['m_rwkv_a0'], 'm_rwkv_a2': out['m_rwkv_a2'], 'm_rwkv_g2': out['m_rwkv_g2'], 'm_rwkv_k_k': out['m_rwkv_k_k'], 'm_rwkv_k_a': out['m_rwkv_k_a'], 'm_rwkv_r_k': out['m_rwkv_r_k'], 'm_rwkv_ln_w': out['m_rwkv_ln_w'], 'm_rwkv_ln_b': out['m_rwkv_ln_b'], 'm_rwkv_w_o': out['m_rwkv_w_o'], 'm_rwkv_v1': out['m_rwkv_v1'], 'm_rwkv_v_mu': out['m_rwkv_v_mu'], 'm_rwkv_v0': out['m_rwkv_v0'], 'm_rwkv_v2': out['m_rwkv_v2'], 'm_conv_w': out['m_conv_w'], 'm_conv_w_o': out['m_conv_w_o'], 'm_w_out': out['m_w_out'], 'm_mlp_norm': out['m_mlp_norm'], 'm_w_up': out['m_w_up'], 'm_w_down': out['m_w_down'], 'v_attn_norm': out['v_attn_norm'], 'v_w_in': out['v_w_in'], 'v_mla_q_a_norm': out['v_mla_q_a_norm'], 'v_mla_wq_b': out['v_mla_wq_b'], 'v_mla_kv_a_norm': out['v_mla_kv_a_norm'], 'v_mla_wkv_b': out['v_mla_wkv_b'], 'v_mla_q_norm': out['v_mla_q_norm'], 'v_mla_k_norm': out['v_mla_k_norm'], 'v_mla_w_o': out['v_mla_w_o'], 'v_rwkv_mu': out['v_rwkv_mu'], 'v_rwkv_w0': out['v_rwkv_w0'], 'v_rwkv_w2': out['v_rwkv_w2'], 'v_rwkv_a0': out['v_rwkv_a0'], 'v_rwkv_a2': out['v_rwkv_a2'], 'v_rwkv_g2': out['v_rwkv_g2'], 'v_rwkv_k_k': out['v_rwkv_k_k'], 'v_rwkv_k_a': out['v_rwkv_k_a'], 'v_rwkv_r_k': out['v_rwkv_r_k'], 'v_rwkv_ln_w': out['v_rwkv_ln_w'], 'v_rwkv_ln_b': out['v_rwkv_ln_b'], 'v_rwkv_w_o': out['v_rwkv_w_o'], 'v_rwkv_v1': out['v_rwkv_v1'], 'v_rwkv_v_mu': out['v_rwkv_v_mu'], 'v_rwkv_v0': out['v_rwkv_v0'], 'v_rwkv_v2': out['v_rwkv_v2'], 'v_conv_w': out['v_conv_w'], 'v_conv_w_o': out['v_conv_w_o'], 'v_w_out': out['v_w_out'], 'v_mlp_norm': out['v_mlp_norm'], 'v_w_up': out['v_w_up'], 'v_w_down': out['v_w_down']}


def _loss(weights, diff, rest, loss_target):
    with _jax.named_scope("forward"):
        args = {**rest, TWIN_DIFF_INPUT: diff, **{k: w.astype(_WEIGHT_DTYPES[k]) for k, w in weights.items()}}
        y = _forward(args)
    with _jax.named_scope("loss_head"):
        err = _jnp.square(y.astype(_jnp.float32) - loss_target)
        return 0.5 * _jnp.sum(_jnp.mean(err, axis=-1)) if err.ndim else 0.5 * err


def _adamw(w, g, m, v):
    m = ADAM_B1 * m + (1.0 - ADAM_B1) * g
    v = ADAM_B2 * v + (1.0 - ADAM_B2) * _jnp.square(g)
    m_hat = m / (1.0 - ADAM_B1 ** ADAM_STEP)
    v_hat = v / (1.0 - ADAM_B2 ** ADAM_STEP)
    delta = -ADAM_LR * (m_hat / (_jnp.sqrt(v_hat) + ADAM_EPS) + ADAM_WD * w)
    return delta, m, v


def reference(x, positions, attn_norm, w_in, mla_q_a_norm, mla_wq_b, mla_kv_a_norm, mla_wkv_b, mla_q_norm, mla_k_norm, mla_w_o, rwkv_mu, rwkv_w0, rwkv_w2, rwkv_a0, rwkv_a2, rwkv_g2, rwkv_k_k, rwkv_k_a, rwkv_r_k, rwkv_ln_w, rwkv_ln_b, rwkv_w_o, rwkv_v1, rwkv_v_mu, rwkv_v0, rwkv_v2, conv_w, conv_w_o, w_out, mlp_norm, w_up, w_down, loss_target, m_attn_norm, m_w_in, m_mla_q_a_norm, m_mla_wq_b, m_mla_kv_a_norm, m_mla_wkv_b, m_mla_q_norm, m_mla_k_norm, m_mla_w_o, m_rwkv_mu, m_rwkv_w0, m_rwkv_w2, m_rwkv_a0, m_rwkv_a2, m_rwkv_g2, m_rwkv_k_k, m_rwkv_k_a, m_rwkv_r_k, m_rwkv_ln_w, m_rwkv_ln_b, m_rwkv_w_o, m_rwkv_v1, m_rwkv_v_mu, m_rwkv_v0, m_rwkv_v2, m_conv_w, m_conv_w_o, m_w_out, m_mlp_norm, m_w_up, m_w_down, v_attn_norm, v_w_in, v_mla_q_a_norm, v_mla_wq_b, v_mla_kv_a_norm, v_mla_wkv_b, v_mla_q_norm, v_mla_k_norm, v_mla_w_o, v_rwkv_mu, v_rwkv_w0, v_rwkv_w2, v_rwkv_a0, v_rwkv_a2, v_rwkv_g2, v_rwkv_k_k, v_rwkv_k_a, v_rwkv_r_k, v_rwkv_ln_w, v_rwkv_ln_b, v_rwkv_w_o, v_rwkv_v1, v_rwkv_v_mu, v_rwkv_v0, v_rwkv_v2, v_conv_w, v_conv_w_o, v_w_out, v_mlp_norm, v_w_up, v_w_down):
    given = dict(x=x, positions=positions, attn_norm=attn_norm, w_in=w_in, mla_q_a_norm=mla_q_a_norm, mla_wq_b=mla_wq_b, mla_kv_a_norm=mla_kv_a_norm, mla_wkv_b=mla_wkv_b, mla_q_norm=mla_q_norm, mla_k_norm=mla_k_norm, mla_w_o=mla_w_o, rwkv_mu=rwkv_mu, rwkv_w0=rwkv_w0, rwkv_w2=rwkv_w2, rwkv_a0=rwkv_a0, rwkv_a2=rwkv_a2, rwkv_g2=rwkv_g2, rwkv_k_k=rwkv_k_k, rwkv_k_a=rwkv_k_a, rwkv_r_k=rwkv_r_k, rwkv_ln_w=rwkv_ln_w, rwkv_ln_b=rwkv_ln_b, rwkv_w_o=rwkv_w_o, rwkv_v1=rwkv_v1, rwkv_v_mu=rwkv_v_mu, rwkv_v0=rwkv_v0, rwkv_v2=rwkv_v2, conv_w=conv_w, conv_w_o=conv_w_o, w_out=w_out, mlp_norm=mlp_norm, w_up=w_up, w_down=w_down, loss_target=loss_target, m_attn_norm=m_attn_norm, m_w_in=m_w_in, m_mla_q_a_norm=m_mla_q_a_norm, m_mla_wq_b=m_mla_wq_b, m_mla_kv_a_norm=m_mla_kv_a_norm, m_mla_wkv_b=m_mla_wkv_b, m_mla_q_norm=m_mla_q_norm, m_mla_k_norm=m_mla_k_norm, m_mla_w_o=m_mla_w_o, m_rwkv_mu=m_rwkv_mu, m_rwkv_w0=m_rwkv_w0, m_rwkv_w2=m_rwkv_w2, m_rwkv_a0=m_rwkv_a0, m_rwkv_a2=m_rwkv_a2, m_rwkv_g2=m_rwkv_g2, m_rwkv_k_k=m_rwkv_k_k, m_rwkv_k_a=m_rwkv_k_a, m_rwkv_r_k=m_rwkv_r_k, m_rwkv_ln_w=m_rwkv_ln_w, m_rwkv_ln_b=m_rwkv_ln_b, m_rwkv_w_o=m_rwkv_w_o, m_rwkv_v1=m_rwkv_v1, m_rwkv_v_mu=m_rwkv_v_mu, m_rwkv_v0=m_rwkv_v0, m_rwkv_v2=m_rwkv_v2, m_conv_w=m_conv_w, m_conv_w_o=m_conv_w_o, m_w_out=m_w_out, m_mlp_norm=m_mlp_norm, m_w_up=m_w_up, m_w_down=m_w_down, v_attn_norm=v_attn_norm, v_w_in=v_w_in, v_mla_q_a_norm=v_mla_q_a_norm, v_mla_wq_b=v_mla_wq_b, v_mla_kv_a_norm=v_mla_kv_a_norm, v_mla_wkv_b=v_mla_wkv_b, v_mla_q_norm=v_mla_q_norm, v_mla_k_norm=v_mla_k_norm, v_mla_w_o=v_mla_w_o, v_rwkv_mu=v_rwkv_mu, v_rwkv_w0=v_rwkv_w0, v_rwkv_w2=v_rwkv_w2, v_rwkv_a0=v_rwkv_a0, v_rwkv_a2=v_rwkv_a2, v_rwkv_g2=v_rwkv_g2, v_rwkv_k_k=v_rwkv_k_k, v_rwkv_k_a=v_rwkv_k_a, v_rwkv_r_k=v_rwkv_r_k, v_rwkv_ln_w=v_rwkv_ln_w, v_rwkv_ln_b=v_rwkv_ln_b, v_rwkv_w_o=v_rwkv_w_o, v_rwkv_v1=v_rwkv_v1, v_rwkv_v_mu=v_rwkv_v_mu, v_rwkv_v0=v_rwkv_v0, v_rwkv_v2=v_rwkv_v2, v_conv_w=v_conv_w, v_conv_w_o=v_conv_w_o, v_w_out=v_w_out, v_mlp_norm=v_mlp_norm, v_w_up=v_w_up, v_w_down=v_w_down)
    weights = {n: given[n] for n in TWIN_WEIGHTS}
    shared = {n: given[n] for n in SHARED_INPUTS}
    per_example = {n: given[n] for n in ['x', 'positions']}
    grad_fn = _jax.value_and_grad(_loss, argnums=(0, 1))

    def one_microbatch(ex, loss_target):
        ex = dict(ex)
        diff = ex.pop(TWIN_DIFF_INPUT)
        return grad_fn(weights, diff, {**shared, **ex}, loss_target)

    if N_MICROBATCH == 1:
        loss, (grad_w, grad_x) = one_microbatch(per_example, given["loss_target"])
    else:
        def body(carry, xs):
            loss_sum, grad_sum = carry
            l_k, (gw_k, gx_k) = one_microbatch(xs[0], xs[1])
            with _jax.named_scope("update"):
                return (loss_sum + l_k, _jax.tree.map(_jnp.add, grad_sum, gw_k)), gx_k

        init = (_jnp.zeros((), _jnp.float32), _jax.tree.map(_jnp.zeros_like, weights))
        (loss, grad_w), grad_x = _jax.lax.scan(body, init, (per_example, given["loss_target"]))
    with _jax.named_scope("update"):
        delta_w, new_m, new_v = {}, {}, {}
        for n in TWIN_WEIGHTS:
            delta_w[n], new_m[n], new_v[n] = _adamw(weights[n], grad_w[n], given["m_" + n], given["v_" + n])
    return (loss, grad_x, *[grad_w[n] for n in TWIN_WEIGHTS], *[delta_w[n] for n in TWIN_WEIGHTS],
            *[new_m[n] for n in TWIN_WEIGHTS], *[new_v[n] for n in TWIN_WEIGHTS])
```

```python
import functools

import jax
import jax.numpy as jnp
import numpy as np
from jax import lax
from jax.experimental import pallas as pl
from jax.experimental.pallas import tpu as pltpu

f32, bf16 = jnp.float32, jnp.bfloat16
HI = lax.Precision.HIGHEST
MESH = pl.DeviceIdType.MESH

D = 1024
DEPTH = 2
MLA_H, NOPE, ROPE, DQK, DV = 8, 64, 32, 96, 64
QL, KVL = 384, 256
RW, RH, RN = 256, 4, 64
DL, AL, GL, MVL = 64, 64, 128, 32
CW = 256
DFF = 4096
GATE = 3 * D
MLA_COLS = QL + KVL + ROPE
MLA_PAD = 768
NORM_EPS = 1e-6
GN_EPS = 64e-5
ROPE_THETA = 10000.0
LR, B1, B2, EPS, WD, STEP = 0.001, 0.9, 0.999, 1e-08, 0.01, 10

VMEM_LIMIT = 52 * 1024 * 1024
WKV_CHUNK = 32

BIG = ["w_in", "mla_wq_b", "mla_wkv_b", "mla_w_o", "rwkv_w_o", "conv_w_o", "w_out", "w_up", "w_down"]
MED = ["rwkv_w2", "rwkv_a2", "rwkv_g2", "rwkv_v1", "rwkv_v2", "conv_w"]
ROW_SHARDED = {"w_out", "w_down", "rwkv_v1"}
SMALL = ["attn_norm", "mla_q_a_norm", "mla_kv_a_norm", "mla_q_norm", "mla_k_norm", "rwkv_mu", "rwkv_w0",
         "rwkv_a0", "rwkv_k_k", "rwkv_k_a", "rwkv_r_k", "rwkv_ln_w", "rwkv_ln_b", "rwkv_v_mu", "rwkv_v0",
         "mlp_norm"]
WEIGHTS = ["attn_norm", "w_in", "mla_q_a_norm", "mla_wq_b", "mla_kv_a_norm", "mla_wkv_b", "mla_q_norm",
           "mla_k_norm", "mla_w_o", "rwkv_mu", "rwkv_w0", "rwkv_w2", "rwkv_a0", "rwkv_a2", "rwkv_g2",
           "rwkv_k_k", "rwkv_k_a", "rwkv_r_k", "rwkv_ln_w", "rwkv_ln_b", "rwkv_w_o", "rwkv_v1", "rwkv_v_mu",
           "rwkv_v0", "rwkv_v2", "conv_w", "conv_w_o", "w_out", "mlp_norm", "w_up", "w_down"]


def _cparams(sem=None):
    return pltpu.CompilerParams(dimension_semantics=sem, vmem_limit_bytes=VMEM_LIMIT)


def _pick(dim, pref, align):
    if dim <= pref:
        return dim
    t = (pref // align) * align
    while t >= align:
        if dim % t == 0:
            return t
        t -= align
    return dim


def _bdot(a, b, dims):
    return lax.dot_general(a.astype(bf16), b.astype(bf16), (dims, ((), ())), preferred_element_type=f32)


@jax.custom_vjp
def _mm(a, b):
    return _bdot(a, b, ((1,), (0,)))


def _mm_fwd(a, b):
    return _mm(a, b), (a, b)


def _mm_bwd(res, g):
    a, b = res
    return _bdot(g, b, ((1,), (1,))), _bdot(a, g, ((0,), (0,)))


_mm.defvjp(_mm_fwd, _mm_bwd)


@jax.custom_vjp
def _mm_nt(a, b):
    return _bdot(a, b, ((1,), (1,)))


def _mm_nt_fwd(a, b):
    return _mm_nt(a, b), (a, b)


def _mm_nt_bwd(res, g):
    a, b = res
    return _bdot(g, b, ((1,), (0,))), _bdot(g, a, ((0,), (0,)))


_mm_nt.defvjp(_mm_nt_fwd, _mm_nt_bwd)


def _hdot(a, b):
    return jnp.dot(a, b, precision=HI, preferred_element_type=f32)


def _hdot_nt(a, b):
    return lax.dot_general(a, b, (((1,), (1,)), ((), ())), precision=HI, preferred_element_type=f32)


def _hdot_tn(a, b):
    return lax.dot_general(a, b, (((0,), (0,)), ((), ())), precision=HI, preferred_element_type=f32)


def _rms(x, g, eps=NORM_EPS):
    return x * lax.rsqrt(jnp.mean(x * x, axis=-1, keepdims=True) + eps) * g


def _sigmoid(x):
    return 1.0 / (1.0 + jnp.exp(-x))


def _softplus(x):
    return jnp.maximum(x, 0.0) + jnp.log(1.0 + jnp.exp(-jnp.maximum(x, -x)))


def _lane_split(x, sizes):
    bounds = np.cumsum([0] + list(sizes))

    @jax.custom_vjp
    def split(v):
        return tuple(v[:, int(bounds[q]):int(bounds[q + 1])] for q in range(len(sizes)))

    split.defvjp(lambda v: (split(v), None), lambda _, g: (jnp.concatenate(g, axis=-1),))
    return split(x)


def _unstack(x):
    @jax.custom_vjp
    def unstack(v):
        return tuple(v[q] for q in range(v.shape[0]))

    unstack.defvjp(lambda v: (unstack(v), None), lambda _, g: (jnp.stack(g, axis=0),))
    return unstack(x)


def _shift_mats(t, k):
    r = lax.broadcasted_iota(jnp.int32, (t, t), 0)
    c = lax.broadcasted_iota(jnp.int32, (t, t), 1)
    inner = (r - c == k).astype(f32)
    r8 = lax.broadcasted_iota(jnp.int32, (t, 8), 0)
    c8 = lax.broadcasted_iota(jnp.int32, (t, 8), 1)
    edge = (c8 - r8 == 8 - k).astype(f32)
    return inner, edge


def _shift(x, halo, k):
    inner, edge = _shift_mats(x.shape[0], k)
    return _hdot(inner, x) + _hdot(edge, halo)


def mm(a, b, *, name, ta=False, tb=False, a_batched=False, b_batched=False, reduce_batch=False, add=None,
       act_grad=None, out_dtype=f32, tm=512, tn=1024, tk=512):
    ash, bsh = a.shape[-2:], b.shape[-2:]
    (k_, m_) = ash if ta else ash[::-1]
    (k2_, n_) = bsh[::-1] if tb else bsh
    assert k_ == k2_, (a.shape, b.shape, ta, tb)
    hb = a.shape[0] if a_batched else (b.shape[0] if b_batched else 1)
    batched_out = (a_batched or b_batched) and not reduce_batch
    h_out = hb if batched_out else 1
    h_red = hb if reduce_batch else 1
    tm = _pick(m_, tm, 128)
    tn = _pick(n_, tn, 128)
    tk = _pick(k_, tk, 128)
    nm, nn, nk = m_ // tm, n_ // tn, k_ // tk

    def a_map(ho, i, j, hr, kk):
        blk = (kk, i) if ta else (i, kk)
        return ((ho if batched_out else hr),) + blk if a_batched else blk

    def b_map(ho, i, j, hr, kk):
        blk = (j, kk) if tb else (kk, j)
        return ((ho if batched_out else hr),) + blk if b_batched else blk

    a_blk = (tk, tm) if ta else (tm, tk)
    b_blk = (tn, tk) if tb else (tk, tn)
    in_specs = [pl.BlockSpec(((1,) + a_blk) if a_batched else a_blk, a_map),
                pl.BlockSpec(((1,) + b_blk) if b_batched else b_blk, b_map)]
    args = [a, b]
    for extra in (add, act_grad):
        if extra is not None:
            in_specs.append(pl.BlockSpec((tm, tn), lambda ho, i, j, hr, kk: (i, j)))
            args.append(extra)
    if batched_out:
        out_spec = pl.BlockSpec((1, tm, tn), lambda ho, i, j, hr, kk: (ho, i, j))
        out_shape = jax.ShapeDtypeStruct((hb, m_, n_), out_dtype)
    else:
        out_spec = pl.BlockSpec((tm, tn), lambda ho, i, j, hr, kk: (i, j))
        out_shape = jax.ShapeDtypeStruct((m_, n_), out_dtype)
    dims = ((0,) if ta else (1,), (1,) if tb else (0,))
    has_add, has_act = add is not None, act_grad is not None

    def body(*refs):
        a_ref, b_ref = refs[0], refs[1]
        pos = 2
        add_ref = act_ref = None
        if has_add:
            add_ref = refs[pos]
            pos += 1
        if has_act:
            act_ref = refs[pos]
            pos += 1
        o_ref, acc_ref = refs[pos], refs[pos + 1]
        hr, kk = pl.program_id(3), pl.program_id(4)
        first = jnp.logical_and(hr == 0, kk == 0)
        last = jnp.logical_and(hr == h_red - 1, kk == nk - 1)
        av = a_ref[0] if a_batched else a_ref[...]
        bv = b_ref[0] if b_batched else b_ref[...]
        p = _bdot(av, bv, dims)

        @pl.when(first)
        def _():
            acc_ref[...] = p

        @pl.when(jnp.logical_not(first))
        def _():
            acc_ref[...] += p

        @pl.when(last)
        def _():
            r = acc_ref[...]
            if has_act:
                r = r * (2.0 * jnp.maximum(act_ref[...], 0.0))
            if has_add:
                r = r + add_ref[...]
            if batched_out:
                o_ref[0] = r.astype(out_dtype)
            else:
                o_ref[...] = r.astype(out_dtype)

    return pl.pallas_call(
        body, name=name, grid=(h_out, nm, nn, h_red, nk), in_specs=in_specs, out_specs=out_spec,
        out_shape=out_shape, scratch_shapes=[pltpu.VMEM((tm, tn), f32)],
        compiler_params=_cparams(("parallel", "parallel", "parallel", "arbitrary", "arbitrary")),
    )(*args)


def _row_spec(arr, tile, idx):
    if arr.ndim == 2:
        return pl.BlockSpec((tile, arr.shape[1]), lambda i: (idx(i), 0))
    return pl.BlockSpec((arr.shape[0], tile, arr.shape[2]), lambda i: (0, idx(i), 0))


def _halo_spec(arr, tile, idx):
    per = tile // 8
    return pl.BlockSpec((8, arr.shape[1]), lambda i: (jnp.maximum(idx(i) * per - 1, 0), 0))


def _full_spec(arr):
    nd = arr.ndim
    return pl.BlockSpec(arr.shape, lambda i: (0,) * nd)


def rows_fwd(fn, rows, params, consts, out_shapes, *, tile, name, halos=()):
    s_len = rows[0].shape[-2]
    n = s_len // tile
    nr, nh, npar, nc = len(rows), len(halos), len(params), len(consts)
    ident = lambda i: i
    in_specs = ([_row_spec(r, tile, ident) for r in rows] + [_halo_spec(rows[h], tile, ident) for h in halos]
                + [_full_spec(p) for p in params] + [_full_spec(c) for c in consts])
    out_specs = [_row_spec(o, tile, ident) for o in out_shapes]

    def body(*refs):
        i = pl.program_id(0)
        rv = [r[...] for r in refs[:nr]]
        keep = (i > 0).astype(f32)
        hv = [r[...] * keep for r in refs[nr:nr + nh]]
        pv = [r[...] for r in refs[nr + nh:nr + nh + npar]]
        cv = [r[...] for r in refs[nr + nh + npar:nr + nh + npar + nc]]
        outs = fn(rv, hv, pv, cv)
        for o_ref, o in zip(refs[nr + nh + npar + nc:], outs):
            o_ref[...] = o

    return pl.pallas_call(
        body, name=name, grid=(n,), in_specs=in_specs, out_specs=out_specs, out_shape=list(out_shapes),
        compiler_params=_cparams(("arbitrary",)),
    )(*rows, *[rows[h] for h in halos], *params, *consts)


def rows_bwd(fn, rows, params, consts, douts, *, tile, name, halos=(), grad_rows=None, extra=None):
    s_len = rows[0].shape[-2]
    n = s_len // tile
    nr, nh, npar, nc = len(rows), len(halos), len(params), len(consts)
    grad_rows = list(range(nr)) if grad_rows is None else list(grad_rows)
    extra = extra or {}
    assert all(h in grad_rows for h in halos)
    rev = lambda i: n - 1 - i
    dflat = [d for ds in douts for d in ds]
    dcount = [len(ds) for ds in douts]
    eflat = [e for g in grad_rows for e in extra.get(g, [])]
    ecount = [len(extra.get(g, [])) for g in grad_rows]
    in_specs = ([_row_spec(r, tile, rev) for r in rows] + [_halo_spec(rows[h], tile, rev) for h in halos]
                + [_full_spec(p) for p in params] + [_full_spec(c) for c in consts]
                + [_row_spec(d, tile, rev) for d in dflat] + [_row_spec(e, tile, rev) for e in eflat])
    out_shapes = ([jax.ShapeDtypeStruct(rows[g].shape, f32) for g in grad_rows]
                  + [jax.ShapeDtypeStruct(p.shape, f32) for p in params])
    out_specs = [_row_spec(rows[g], tile, rev) for g in grad_rows] + [_full_spec(p) for p in params]
    scratch = [pltpu.VMEM((8, rows[h].shape[1]), f32) for h in halos]
    n_in = nr + nh + npar + nc + len(dflat) + len(eflat)
    n_out = len(grad_rows) + npar

    def body(*refs):
        i = pl.program_id(0)
        rv = [r[...] for r in refs[:nr]]
        keep = (i < n - 1).astype(f32)
        hv = [r[...] * keep for r in refs[nr:nr + nh]]
        pv = [r[...] for r in refs[nr + nh:nr + nh + npar]]
        pos = nr + nh + npar
        cv = [r[...] for r in refs[pos:pos + nc]]
        pos += nc
        dv = []
        for cnt in dcount:
            acc = refs[pos][...]
            for q in range(1, cnt):
                acc = acc + refs[pos + q][...]
            dv.append(acc)
            pos += cnt
        ev = []
        for cnt in ecount:
            ev.append([refs[pos + q][...] for q in range(cnt)])
            pos += cnt
        out_refs = refs[n_in:n_in + n_out]
        carry_refs = refs[n_in + n_out:]

        def f(gr, gh, gp):
            full = list(rv)
            for g, val in zip(grad_rows, gr):
                full[g] = val
            return tuple(fn(full, gh, gp, cv))

        _, vjp = jax.vjp(f, [rv[g] for g in grad_rows], hv, pv)
        d_rows, d_halos, d_params = vjp(tuple(dv))

        @pl.when(i == 0)
        def _():
            for c_ref in carry_refs:
                c_ref[...] = jnp.zeros_like(c_ref)
            for p_ref in out_refs[len(grad_rows):]:
                p_ref[...] = jnp.zeros_like(p_ref)

        for q, g in enumerate(grad_rows):
            val = d_rows[q]
            for e in ev[q]:
                val = val + e
            out_refs[q][...] = val
            if g in halos:
                hq = list(halos).index(g)
                out_refs[q][tile - 8:tile, :] += carry_refs[hq][...]
                carry_refs[hq][...] = d_halos[hq]
        for p_ref, dp in zip(out_refs[len(grad_rows):], d_params):
            p_ref[...] += dp

    res = pl.pallas_call(
        body, name=name, grid=(n,), in_specs=in_specs, out_specs=out_specs, out_shape=out_shapes,
        scratch_shapes=scratch, compiler_params=_cparams(("arbitrary",)),
    )(*rows, *[rows[h] for h in halos], *params, *consts, *dflat, *eflat)
    return list(res[:len(grad_rows)]), list(res[len(grad_rows):])


def _fn_norm(rows, halos, params, consts):
    return (_rms(rows[0], params[0]),)


def _fn_mla_prep(rows, halos, params, consts):
    cq, ckv, kpe = _lane_split(rows[0], (QL, KVL, MLA_PAD - QL - KVL))
    return _rms(cq, params[0]), _rms(ckv, params[1]), kpe


def _rope(x, cos, sin, rot):
    return x * cos + _hdot(x, rot) * sin


def _fn_qk_post(rows, halos, params, consts):
    q_raw, kn_pad, kpe, cos, sin = rows
    q_norm, k_norm = params
    place, rot = consts
    kpe96 = _hdot(kpe, place)
    qs = [_rope(_rms(qh, q_norm), cos, sin, rot) for qh in _unstack(q_raw)]
    ks = [_rope(_rms(kh + kpe96, k_norm), cos, sin, rot) for kh in _unstack(kn_pad)]
    return jnp.stack(qs, axis=0), jnp.stack(ks, axis=0)


def _seg(x, bd):
    return _hdot(x, bd)


def _make_fn_rwkv_prep(vres):
    def fn(rows, halos, params, consts):
        cols = rows[0]
        bd = consts[0]
        mu, w0, w2, a0, a2, g2, k_k, k_a = params[:8]
        prev = _shift(cols, halos[0], 1)
        c = cols + (prev - cols) * mu
        r, k, v, xw, xa, xg = _lane_split(c, (RW, RW, RW, DL, AL, GL))
        log_w = -_softplus(-(w0 + _mm(jnp.tanh(xw), w2))) - 0.5
        ld = -jnp.exp(log_w)
        a = _sigmoid(a0 + _mm(xa, a2))
        g = _mm(_sigmoid(xg), g2)
        if vres:
            hcur, v_first = rows[1], rows[2]
            v1, v_mu, v0, v2 = params[8:12]
            xv = _mm(hcur, v1)
            xv_prev = _shift(xv, _mm(halos[1], v1), 1)
            xv = xv + (xv_prev - xv) * v_mu
            v = v + (v_first - v) * _sigmoid(v0 + _mm(xv, v2))
        kk = k * k_k
        kk = kk / jnp.maximum(jnp.sqrt(_seg(kk * kk, bd)), 1e-12)
        k2 = k * (1.0 + (a - 1.0) * k_a)
        return r, ld, k2, v, -kk, kk * a, g
    return fn


def _fn_rwkv_post(rows, halos, params, consts):
    y, r, k2, v, g = rows
    ln_w, ln_b, r_k = params
    bd = consts[0]
    mean = _seg(y, bd) * (1.0 / RN)
    d = y - mean
    var = _seg(d * d, bd) * (1.0 / RN)
    yn = d * lax.rsqrt(var + GN_EPS) * ln_w + ln_b
    bonus = _seg(r * k2 * r_k, bd) * v
    return ((yn + bonus) * g,)


def _fn_conv(rows, halos, params, consts):
    cols, halo = rows[0], halos[0]
    w0, w1, w2 = params
    b, c, x = _lane_split(cols, (CW, CW, CW))
    _, ch, xh = _lane_split(halo, (CW, CW, CW))
    u, uh = c * x, ch * xh
    return (b * (w0 * _shift(u, uh, 2) + w1 * _shift(u, uh, 1) + w2 * u),)


def _fn_merge(rows, halos, params, consts):
    gate, o_a, o_b, o_c = rows
    g_a, g_b, g_c = _lane_split(gate, (D, D, D))
    return (_sigmoid(g_a) * o_a + _sigmoid(g_b) * o_b + _sigmoid(g_c) * o_c,)


def _fn_relu2(rows, halos, params, consts):
    return (jnp.square(jnp.maximum(rows[0], 0.0)),)


def _attn_block(q, k, v, q0):
    tq, s_len = q.shape[0], k.shape[0]
    s = _mm_nt(q, k) * (DQK ** -0.5)
    row = q0 + lax.broadcasted_iota(jnp.int32, (tq, s_len), 0)
    col = lax.broadcasted_iota(jnp.int32, (tq, s_len), 1)
    s = jnp.where(row >= col, s, -1e30)
    m = lax.stop_gradient(jnp.max(s, axis=-1, keepdims=True))
    e = jnp.exp(s - m)
    p = e / jnp.sum(e, axis=-1, keepdims=True)
    return _mm(p, v)


def attn_fwd(q, k, v, *, tq, name):
    h, s_len, _ = q.shape

    def body(q_ref, k_ref, v_ref, o_ref):
        o_ref[0] = _attn_block(q_ref[0], k_ref[0], v_ref[0], pl.program_id(1) * tq)

    return pl.pallas_call(
        body, name=name, grid=(h, s_len // tq),
        in_specs=[pl.BlockSpec((1, tq, DQK), lambda hh, i: (hh, i, 0)),
                  pl.BlockSpec((1, s_len, DQK), lambda hh, i: (hh, 0, 0)),
                  pl.BlockSpec((1, s_len, DV), lambda hh, i: (hh, 0, 0))],
        out_specs=pl.BlockSpec((1, tq, DV), lambda hh, i: (hh, i, 0)),
        out_shape=jax.ShapeDtypeStruct((h, s_len, DV), f32),
        compiler_params=_cparams(("parallel", "arbitrary")),
    )(q, k, v)


def attn_bwd(q, k, v, do, *, tq, name):
    h, s_len, _ = q.shape

    def body(q_ref, k_ref, v_ref, do_ref, dq_ref, dk_ref, dv_ref):
        i = pl.program_id(1)
        _, vjp = jax.vjp(functools.partial(_attn_block, q0=i * tq), q_ref[0], k_ref[0], v_ref[0])
        dq, dk, dv = vjp(do_ref[0])
        dq_ref[0] = dq

        @pl.when(i == 0)
        def _():
            dk_ref[0] = dk
            dv_ref[0] = dv

        @pl.when(i > 0)
        def _():
            dk_ref[0] += dk
            dv_ref[0] += dv

    return pl.pallas_call(
        body, name=name, grid=(h, s_len // tq),
        in_specs=[pl.BlockSpec((1, tq, DQK), lambda hh, i: (hh, i, 0)),
                  pl.BlockSpec((1, s_len, DQK), lambda hh, i: (hh, 0, 0)),
                  pl.BlockSpec((1, s_len, DV), lambda hh, i: (hh, 0, 0)),
                  pl.BlockSpec((1, tq, DV), lambda hh, i: (hh, i, 0))],
        out_specs=[pl.BlockSpec((1, tq, DQK), lambda hh, i: (hh, i, 0)),
                   pl.BlockSpec((1, s_len, DQK), lambda hh, i: (hh, 0, 0)),
                   pl.BlockSpec((1, s_len, DV), lambda hh, i: (hh, 0, 0))],
        out_shape=[jax.ShapeDtypeStruct((h, s_len, DQK), f32), jax.ShapeDtypeStruct((h, s_len, DQK), f32),
                   jax.ShapeDtypeStruct((h, s_len, DV), f32)],
        compiler_params=_cparams(("parallel", "arbitrary")),
    )(q, k, v, do)


def _wkv_chunk(s0, r, ld, k, v, a, b):
    c = r.shape[0]
    ri = lax.broadcasted_iota(jnp.int32, (c, c), 0)
    ci = lax.broadcasted_iota(jnp.int32, (c, c), 1)
    incl, strict = ri >= ci, ri > ci
    cum = _hdot(incl.astype(f32), ld)
    tot = jnp.sum(ld, axis=0, keepdims=True)
    w_incl, w_excl, w_inv, w_rest = jnp.exp(cum), jnp.exp(cum - ld), jnp.exp(-cum), jnp.exp(tot - cum)
    ab, rb, bb, kb = a * w_excl, r * w_incl, b * w_inv, k * w_inv
    l_ab = jnp.where(strict, _hdot_nt(ab, bb), 0.0)
    l_ak = jnp.where(strict, _hdot_nt(ab, kb), 0.0)
    m_rb = jnp.where(incl, _hdot_nt(rb, bb), 0.0)
    m_rk = jnp.where(incl, _hdot_nt(rb, kb), 0.0)
    u = _hdot_nt(ab, s0) + _hdot(l_ak, v)
    lp, span = l_ab, 1
    while span < c:
        u = u + _hdot(lp, u)
        span *= 2
        if span < c:
            lp = _hdot(lp, lp)
    y = _hdot_nt(rb, s0) + _hdot(m_rb, u) + _hdot(m_rk, v)
    s1 = s0 * jnp.exp(tot) + _hdot_tn(u, b * w_rest) + _hdot_tn(v, k * w_rest)
    return y, s1


def wkv_fwd(r, ld, k, v, a, b, *, name):
    s_len = r.shape[0]
    c = WKV_CHUNK
    n = s_len // c

    def body(r_ref, ld_ref, k_ref, v_ref, a_ref, b_ref, y_ref, st_ref, s_sc):
        @pl.when(pl.program_id(0) == 0)
        def _():
            s_sc[...] = jnp.zeros_like(s_sc)

        ys = []
        for h in range(RH):
            sl = slice(RN * h, RN * (h + 1))
            s0 = s_sc[h]
            st_ref[0, h] = s0
            y, s1 = _wkv_chunk(s0, r_ref[:, sl], ld_ref[:, sl], k_ref[:, sl], v_ref[:, sl], a_ref[:, sl],
                               b_ref[:, sl])
            s_sc[h] = s1
            ys.append(y)
        y_ref[...] = jnp.concatenate(ys, axis=-1)

    bs = pl.BlockSpec((c, RW), lambda i: (i, 0))
    return pl.pallas_call(
        body, name=name, grid=(n,), in_specs=[bs] * 6,
        out_specs=[bs, pl.BlockSpec((1, RH, RN, RN), lambda i: (i, 0, 0, 0))],
        out_shape=[jax.ShapeDtypeStruct((s_len, RW), f32), jax.ShapeDtypeStruct((n, RH, RN, RN), f32)],
        scratch_shapes=[pltpu.VMEM((RH, RN, RN), f32)], compiler_params=_cparams(("arbitrary",)),
    )(r, ld, k, v, a, b)


def wkv_bwd(r, ld, k, v, a, b, states, dy, *, name):
    s_len = r.shape[0]
    c = WKV_CHUNK
    n = s_len // c

    def body(r_ref, ld_ref, k_ref, v_ref, a_ref, b_ref, st_ref, dy_ref, dr, dld, dk, dv, da, db, ds_sc):
        @pl.when(pl.program_id(0) == 0)
        def _():
            ds_sc[...] = jnp.zeros_like(ds_sc)

        outs = [[] for _ in range(6)]
        for h in range(RH):
            sl = slice(RN * h, RN * (h + 1))
            args = (st_ref[0, h], r_ref[:, sl], ld_ref[:, sl], k_ref[:, sl], v_ref[:, sl], a_ref[:, sl],
                    b_ref[:, sl])
            _, vjp = jax.vjp(_wkv_chunk, *args)
            g = vjp((dy_ref[:, sl], ds_sc[h]))
            ds_sc[h] = g[0]
            for j in range(6):
                outs[j].append(g[1 + j])
        for j, o_ref in enumerate((dr, dld, dk, dv, da, db)):
            o_ref[...] = jnp.concatenate(outs[j], axis=-1)

    bs = pl.BlockSpec((c, RW), lambda i: (n - 1 - i, 0))
    return pl.pallas_call(
        body, name=name, grid=(n,),
        in_specs=[bs] * 6 + [pl.BlockSpec((1, RH, RN, RN), lambda i: (n - 1 - i, 0, 0, 0)), bs],
        out_specs=[bs] * 6, out_shape=[jax.ShapeDtypeStruct((s_len, RW), f32)] * 6,
        scratch_shapes=[pltpu.VMEM((RH, RN, RN), f32)], compiler_params=_cparams(("arbitrary",)),
    )(r, ld, k, v, a, b, states, dy)


def loss_head(y, target, *, tile, name):
    s_len, d = y.shape
    n = s_len // tile

    def body(y_ref, t_ref, dy_ref, l_ref):
        err = y_ref[...] - t_ref[...]
        dy_ref[...] = err * (1.0 / d)
        part = 0.5 * jnp.sum(jnp.mean(err * err, axis=-1, keepdims=True), axis=0, keepdims=True)

        @pl.when(pl.program_id(0) == 0)
        def _():
            l_ref[...] = jnp.zeros_like(l_ref)

        l_ref[...] += jnp.broadcast_to(part, l_ref.shape)

    bs = pl.BlockSpec((tile, d), lambda i: (i, 0))
    dy, l = pl.pallas_call(
        body, name=name, grid=(n,), in_specs=[bs, bs],
        out_specs=[bs, pl.BlockSpec((8, 128), lambda i: (0, 0))],
        out_shape=[jax.ShapeDtypeStruct((s_len, d), f32), jax.ShapeDtypeStruct((8, 128), f32)],
        compiler_params=_cparams(("arbitrary",)),
    )(y, target)
    return l[0, 0], dy


def adamw(w, g, m, v, *, name):
    gs = g if isinstance(g, (list, tuple)) else [g]
    rows, cols = w.shape
    tile = _pick(rows, max(8, (2 * 1024 * 1024 // (4 * cols)) // 8 * 8), 8)
    c1 = 1.0 - B1 ** STEP
    c2 = 1.0 - B2 ** STEP
    ng = len(gs)

    def body(*refs):
        w_ref, m_ref, v_ref = refs[0], refs[1 + ng], refs[2 + ng]
        g_out, d_out, m_out, v_out = refs[3 + ng:]
        gv = refs[1][...]
        for q in range(1, ng):
            gv = gv + refs[1 + q][...]
        mn = B1 * m_ref[...] + (1.0 - B1) * gv
        vn = B2 * v_ref[...] + (1.0 - B2) * (gv * gv)
        d_out[...] = -LR * ((mn / c1) / (jnp.sqrt(vn / c2) + EPS) + WD * w_ref[...])
        g_out[...] = gv
        m_out[...] = mn
        v_out[...] = vn

    bs = pl.BlockSpec((tile, cols), lambda i: (i, 0))
    return pl.pallas_call(
        body, name=name, grid=(rows // tile,), in_specs=[bs] * (3 + ng), out_specs=[bs] * 4,
        out_shape=[jax.ShapeDtypeStruct((rows, cols), f32)] * 4, compiler_params=_cparams(("parallel",)),
    )(w, *gs, m, v)


def _place():
    return lax.axis_index("x"), lax.axis_index("y"), lax.axis_index("c")


def chip_exchange(src, *, gather, name):
    shape = src.shape if not gather else (4,) + src.shape

    def body(s_ref, o_ref, send_sems, recv_sems, local_sem):
        x, y, c = _place()
        me = 2 * x + y
        peers = [(1 - x, y), (x, 1 - y), (1 - x, 1 - y)]
        local = pltpu.make_async_copy(s_ref if gather else s_ref.at[me], o_ref.at[me], local_sem)
        local.start()
        sends = []
        for j, (px, py) in enumerate(peers):
            cp = pltpu.make_async_remote_copy(
                src_ref=s_ref if gather else s_ref.at[2 * px + py], dst_ref=o_ref.at[me],
                send_sem=send_sems.at[j], recv_sem=recv_sems.at[j], device_id=(px, py, c), device_id_type=MESH)
            cp.start()
            sends.append(cp)
        for j, (px, py) in enumerate(peers):
            pltpu.make_async_remote_copy(
                src_ref=s_ref if gather else s_ref.at[me], dst_ref=o_ref.at[2 * px + py],
                send_sem=send_sems.at[j], recv_sem=recv_sems.at[j], device_id=(px, py, c),
                device_id_type=MESH).wait_recv()
        for cp in sends:
            cp.wait_send()
        local.wait()

    any_spec = pl.BlockSpec(memory_space=pl.ANY)
    return pl.pallas_call(
        body, name=name, in_specs=[any_spec], out_specs=any_spec,
        out_shape=jax.ShapeDtypeStruct(shape, src.dtype),
        scratch_shapes=[pltpu.SemaphoreType.DMA((3,)), pltpu.SemaphoreType.DMA((3,)), pltpu.SemaphoreType.DMA],
    )(src)


def sibling_swap(src, *, name):
    def body(s_ref, o_ref, send_sem, recv_sem):
        x, y, c = _place()
        cp = pltpu.make_async_remote_copy(src_ref=s_ref, dst_ref=o_ref, send_sem=send_sem, recv_sem=recv_sem,
                                          device_id=(x, y, 1 - c), device_id_type=MESH)
        cp.start()
        cp.wait()

    any_spec = pl.BlockSpec(memory_space=pl.ANY)
    return pl.pallas_call(
        body, name=name, in_specs=[any_spec], out_specs=any_spec,
        out_shape=jax.ShapeDtypeStruct(src.shape, src.dtype),
        scratch_shapes=[pltpu.SemaphoreType.DMA, pltpu.SemaphoreType.DMA],
    )(src)


def all_reduce_small(src, *, name):
    rows, cols = src.shape

    def body(s_ref, o_ref, buf, send_sems, recv_sems):
        x, y, c = _place()
        me = 4 * x + 2 * y + c
        buf[me] = s_ref[...]
        sends = []
        for msk in range(1, 8):
            px = x ^ (msk >> 2)
            py = y ^ ((msk >> 1) & 1)
            pc = c ^ (msk & 1)
            cp = pltpu.make_async_remote_copy(
                src_ref=s_ref, dst_ref=buf.at[me], send_sem=send_sems.at[msk - 1], recv_sem=recv_sems.at[msk - 1],
                device_id=(px, py, pc), device_id_type=MESH)
            cp.start()
            sends.append(cp)
        for msk in range(1, 8):
            px = x ^ (msk >> 2)
            py = y ^ ((msk >> 1) & 1)
            pc = c ^ (msk & 1)
            pltpu.make_async_remote_copy(
                src_ref=s_ref, dst_ref=buf.at[4 * px + 2 * py + pc], send_sem=send_sems.at[msk - 1],
                recv_sem=recv_sems.at[msk - 1], device_id=(px, py, pc), device_id_type=MESH).wait_recv()
        for cp in sends:
            cp.wait_send()
        acc = buf[0]
        for d in range(1, 8):
            acc = acc + buf[d]
        o_ref[...] = acc

    vm = pl.BlockSpec(memory_space=pltpu.VMEM)
    return pl.pallas_call(
        body, name=name, in_specs=[vm], out_specs=vm, out_shape=jax.ShapeDtypeStruct((rows, cols), f32),
        scratch_shapes=[pltpu.VMEM((8, rows, cols), f32), pltpu.SemaphoreType.DMA((7,)),
                        pltpu.SemaphoreType.DMA((7,))],
        compiler_params=pltpu.CompilerParams(vmem_limit_bytes=VMEM_LIMIT),
    )(src)


def sum4(parts, *, name):
    _, rows, cols = parts.shape
    tile = _pick(rows, 512, 8)

    def body(p_ref, o_ref):
        o_ref[...] = ((p_ref[0] + p_ref[1]) + p_ref[2]) + p_ref[3]

    return pl.pallas_call(
        body, name=name, grid=(rows // tile,), in_specs=[pl.BlockSpec((4, tile, cols), lambda i: (0, i, 0))],
        out_specs=pl.BlockSpec((tile, cols), lambda i: (i, 0)), out_shape=jax.ShapeDtypeStruct((rows, cols), f32),
        compiler_params=_cparams(("parallel",)),
    )(parts)


def _consts():
    idx = np.arange(RW)
    bd = (idx[:, None] // RN == idx[None, :] // RN).astype(np.float32)
    place = np.zeros((128, DQK), np.float32)
    place[np.arange(ROPE), NOPE + np.arange(ROPE)] = 1.0
    rot = np.zeros((DQK, DQK), np.float32)
    half = ROPE // 2
    rot[NOPE + half + np.arange(half), NOPE + np.arange(half)] = -1.0
    rot[NOPE + np.arange(half), NOPE + half + np.arange(half)] = 1.0
    return jnp.asarray(bd), jnp.asarray(place), jnp.asarray(rot)


def _rope_tables(positions):
    freqs = ROPE_THETA ** (-(jnp.arange(ROPE // 2, dtype=f32) * 2.0 / ROPE))
    ang = positions.astype(f32)[:, None] * freqs
    cos, sin = jnp.cos(ang), jnp.sin(ang)
    ones = jnp.ones((positions.shape[0], NOPE), f32)
    return (jnp.concatenate([ones, cos, cos], axis=-1), jnp.concatenate([0.0 * ones, sin, sin], axis=-1))


def derive_layer(w, l):
    w_in = w["w_in"][l]
    pad = jnp.zeros((D, MLA_PAD - MLA_COLS), w_in.dtype)
    wkv = w["mla_wkv_b"][l].reshape(KVL, MLA_H, NOPE + DV)
    wk = jnp.concatenate([wkv[:, :, :NOPE], jnp.zeros((KVL, MLA_H, ROPE), wkv.dtype)], axis=-1)
    return dict(
        gate=w_in[:, :GATE], mla=jnp.concatenate([w_in[:, GATE:GATE + MLA_COLS], pad], axis=1),
        rw=w_in[:, GATE + MLA_COLS:GATE + MLA_COLS + 4 * RW], cv=w_in[:, GATE + MLA_COLS + 4 * RW:],
        wq=w["mla_wq_b"][l].reshape(QL, MLA_H, DQK).transpose(1, 0, 2),
        wk=wk.transpose(1, 0, 2), wv=wkv[:, :, NOPE:].transpose(1, 0, 2),
        wo=w["mla_w_o"][l].reshape(MLA_H, DV, D),
        rwo=w["rwkv_w_o"][l], cvo=w["conv_w_o"][l], out=w["w_out"][l], up=w["w_up"][l], down=w["w_down"][l],
    )


def fold_layer_grads(g):
    w_in = jnp.concatenate([g["gate"], g["mla"][:, :MLA_COLS], g["rw"], g["cv"]], axis=1)
    wkv = jnp.concatenate([g["wk"][:, :, :NOPE], g["wv"]], axis=-1).transpose(1, 0, 2).reshape(KVL, MLA_H * (NOPE + DV))
    return dict(
        w_in=w_in, mla_wq_b=g["wq"].transpose(1, 0, 2).reshape(QL, MLA_H * DQK), mla_wkv_b=wkv,
        mla_w_o=g["wo"].reshape(MLA_H * DV, D), rwkv_w_o=g["rwo"], conv_w_o=g["cvo"], w_out=g["out"],
        w_up=g["up"], w_down=g["down"],
    )


def _row(v):
    return v.reshape(1, -1)


def local_step(x, positions, target, w, sm):
    s_len = x.shape[0]
    t_row = _pick(s_len, 256, 8)
    t_wide = _pick(s_len, 128, 8)
    bd, place, rot = _consts()
    cos, sin = _rope_tables(positions)
    sds = lambda *shape: jax.ShapeDtypeStruct(shape, f32)
    saved = []
    v_first = None
    for l in range(DEPTH):
        tag = f"l{l}_"
        lw = derive_layer(w, l)
        vres = l > 0
        p_norm1 = [_row(sm["attn_norm"][l])]
        (h,) = rows_fwd(_fn_norm, [x], p_norm1, [], [sds(s_len, D)], tile=t_row, name=tag + "norm1")
        gate = mm(h, lw["gate"], name=tag + "proj_gate")
        mla = mm(h, lw["mla"], name=tag + "proj_mla")
        rwc = mm(h, lw["rw"], name=tag + "proj_rwkv")
        cvc = mm(h, lw["cv"], name=tag + "proj_conv")
        p_mla = [_row(sm["mla_q_a_norm"][l]), _row(sm["mla_kv_a_norm"][l])]
        qn, kvn, kpe = rows_fwd(_fn_mla_prep, [mla], p_mla, [], [sds(s_len, QL), sds(s_len, KVL), sds(s_len, 128)],
                                tile=t_row, name=tag + "mla_prep")
        q_raw = mm(qn, lw["wq"], b_batched=True, name=tag + "q_proj")
        kn_pad = mm(kvn, lw["wk"], b_batched=True, name=tag + "k_proj")
        vv = mm(kvn, lw["wv"], b_batched=True, name=tag + "v_proj")
        p_qk = [_row(sm["mla_q_norm"][l]), _row(sm["mla_k_norm"][l])]
        q, k = rows_fwd(_fn_qk_post, [q_raw, kn_pad, kpe, cos, sin], p_qk, [place, rot],
                        [sds(MLA_H, s_len, DQK), sds(MLA_H, s_len, DQK)], tile=t_wide, name=tag + "qk_post")
        o = attn_fwd(q, k, vv, tq=_pick(s_len, 256, 8), name=tag + "attn")
        o_a = mm(o, lw["wo"], a_batched=True, b_batched=True, reduce_batch=True, name=tag + "o_a")
        p_rw = [_row(sm["rwkv_mu"][l]), _row(sm["rwkv_w0"][l]), w["rwkv_w2"][l], _row(sm["rwkv_a0"][l]),
                w["rwkv_a2"][l], w["rwkv_g2"][l], _row(sm["rwkv_k_k"][l]), _row(sm["rwkv_k_a"][l])]
        rw_rows, rw_halos = [rwc], (0,)
        if vres:
            p_rw += [w["rwkv_v1"][l - 1], _row(sm["rwkv_v_mu"][l - 1]), _row(sm["rwkv_v0"][l - 1]), w["rwkv_v2"][l - 1]]
            rw_rows, rw_halos = [rwc, h, v_first], (0, 1)
        fn_prep = _make_fn_rwkv_prep(vres)
        r, ld, k2, v, an, bn, g = rows_fwd(fn_prep, rw_rows, p_rw, [bd], [sds(s_len, RW)] * 7, tile=t_row,
                                           name=tag + "rwkv_prep", halos=rw_halos)
        if not vres:
            v_first = v
        y, states = wkv_fwd(r, ld, k2, v, an, bn, name=tag + "wkv")
        p_post = [_row(sm["rwkv_ln_w"][l]), _row(sm["rwkv_ln_b"][l]), _row(sm["rwkv_r_k"][l])]
        (yb,) = rows_fwd(_fn_rwkv_post, [y, r, k2, v, g], p_post, [bd], [sds(s_len, RW)], tile=t_row,
                         name=tag + "rwkv_post")
        o_b = mm(yb, lw["rwo"], name=tag + "o_b")
        p_cv = [w["conv_w"][l][q:q + 1] for q in range(3)]
        (yc,) = rows_fwd(_fn_conv, [cvc], p_cv, [], [sds(s_len, CW)], tile=t_row, name=tag + "conv", halos=(0,))
        o_c = mm(yc, lw["cvo"], name=tag + "o_c")
        (merged,) = rows_fwd(_fn_merge, [gate, o_a, o_b, o_c], [], [], [sds(s_len, D)], tile=t_wide,
                             name=tag + "merge")
        x1 = mm(merged, lw["out"], add=x, name=tag + "out_proj")
        p_norm2 = [_row(sm["mlp_norm"][l])]
        (h2,) = rows_fwd(_fn_norm, [x1], p_norm2, [], [sds(s_len, D)], tile=t_row, name=tag + "norm2")
        up = mm(h2, lw["up"], name=tag + "up")
        (act,) = rows_fwd(_fn_relu2, [up], [], [], [sds(s_len, DFF)], tile=t_wide, name=tag + "relu2")
        x2 = mm(act, lw["down"], add=x1, name=tag + "down")
        saved.append(dict(lw=lw, x=x, h=h, gate=gate, mla=mla, rwc=rwc, cvc=cvc, qn=qn, kvn=kvn, kpe=kpe,
                          q_raw=q_raw, kn_pad=kn_pad, vv=vv, q=q, k=k, o=o, o_a=o_a, r=r, ld=ld, k2=k2, v=v,
                          an=an, bn=bn, g=g, y=y, states=states, yb=yb, o_b=o_b, yc=yc, o_c=o_c, merged=merged,
                          x1=x1, h2=h2, up=up, act=act, p_norm1=p_norm1, p_mla=p_mla, p_qk=p_qk, p_rw=p_rw,
                          p_post=p_post, p_cv=p_cv, p_norm2=p_norm2, rw_rows=rw_rows, rw_halos=rw_halos,
                          fn_prep=fn_prep, v_first=v_first if vres else None))
        x = x2

    loss, dx = loss_head(x, target, tile=t_row, name="loss_head")

    grads = {n: [None] * (DEPTH - 1 if n in ("rwkv_v1", "rwkv_v_mu", "rwkv_v0", "rwkv_v2") else DEPTH)
             for n in WEIGHTS}
    dv_first = None
    for l in reversed(range(DEPTH)):
        tag = f"b{l}_"
        sv = saved[l]
        lw = sv["lw"]
        vres = l > 0
        g_down = mm(sv["act"], dx, ta=True, name=tag + "g_down")
        dup = mm(dx, lw["down"], tb=True, act_grad=sv["up"], name=tag + "d_up")
        g_up = mm(sv["h2"], dup, ta=True, name=tag + "g_up")
        dh2 = mm(dup, lw["up"], tb=True, name=tag + "d_h2")
        (dx1,), (g_n2,) = rows_bwd(_fn_norm, [sv["x1"]], sv["p_norm2"], [], [[dh2]], tile=t_row,
                                   name=tag + "norm2", extra={0: [dx]})
        g_out = mm(sv["merged"], dx1, ta=True, name=tag + "g_out")
        dmerged = mm(dx1, lw["out"], tb=True, name=tag + "d_merged")
        (dgate, do_a, do_b, do_c), _ = rows_bwd(_fn_merge, [sv["gate"], sv["o_a"], sv["o_b"], sv["o_c"]], [], [],
                                                [[dmerged]], tile=t_wide, name=tag + "merge")
        g_cvo = mm(sv["yc"], do_c, ta=True, name=tag + "g_cvo")
        dyc = mm(do_c, lw["cvo"], tb=True, name=tag + "d_yc")
        (dcvc,), g_cw = rows_bwd(_fn_conv, [sv["cvc"]], sv["p_cv"], [], [[dyc]], tile=t_row, name=tag + "conv",
                                    halos=(0,))
        g_rwo = mm(sv["yb"], do_b, ta=True, name=tag + "g_rwo")
        dyb = mm(do_b, lw["rwo"], tb=True, name=tag + "d_yb")
        (dy, dr_p, dk_p, dv_p, dg), g_post = rows_bwd(
            _fn_rwkv_post, [sv["y"], sv["r"], sv["k2"], sv["v"], sv["g"]], sv["p_post"], [bd], [[dyb]], tile=t_row,
            name=tag + "rwkv_post")
        dr_s, dld, dk_s, dv_s, dan, dbn = wkv_bwd(sv["r"], sv["ld"], sv["k2"], sv["v"], sv["an"], sv["bn"],
                                                  sv["states"], dy, name=tag + "wkv")
        dv_list = [dv_s, dv_p] + ([dv_first] if (not vres and dv_first is not None) else [])
        d_prep, g_prep = rows_bwd(
            sv["fn_prep"], sv["rw_rows"], sv["p_rw"], [bd],
            [[dr_s, dr_p], [dld], [dk_s, dk_p], dv_list, [dan], [dbn], [dg]], tile=t_row, name=tag + "rwkv_prep",
            halos=sv["rw_halos"])
        drwc = d_prep[0]
        dh_extra = []
        if vres:
            dh_extra = [d_prep[1]]
            dv_first = d_prep[2]
        g_wo = mm(sv["o"], do_a, ta=True, a_batched=True, name=tag + "g_wo")
        do = mm(do_a, lw["wo"], tb=True, b_batched=True, name=tag + "d_o")
        dq, dk, dvv = attn_bwd(sv["q"], sv["k"], sv["vv"], do, tq=_pick(s_len, 256, 8), name=tag + "attn")
        (dq_raw, dkn_pad, dkpe), g_qk = rows_bwd(
            _fn_qk_post, [sv["q_raw"], sv["kn_pad"], sv["kpe"], cos, sin], sv["p_qk"], [place, rot], [[dq], [dk]],
            tile=t_wide, name=tag + "qk_post", grad_rows=[0, 1, 2])
        g_wq = mm(sv["qn"], dq_raw, ta=True, b_batched=True, name=tag + "g_wq")
        g_wk = mm(sv["kvn"], dkn_pad, ta=True, b_batched=True, name=tag + "g_wk")
        g_wv = mm(sv["kvn"], dvv, ta=True, b_batched=True, name=tag + "g_wv")
        dqn = mm(dq_raw, lw["wq"], tb=True, a_batched=True, b_batched=True, reduce_batch=True, name=tag + "d_qn")
        dkvn = mm(dkn_pad, lw["wk"], tb=True, a_batched=True, b_batched=True, reduce_batch=True, name=tag + "d_kvn_k")
        dkvn = mm(dvv, lw["wv"], tb=True, a_batched=True, b_batched=True, reduce_batch=True, add=dkvn,
                  name=tag + "d_kvn_v")
        (dmla,), g_mla = rows_bwd(_fn_mla_prep, [sv["mla"]], sv["p_mla"], [], [[dqn], [dkvn], [dkpe]], tile=t_row,
                                  name=tag + "mla_prep")
        g_gate = mm(sv["h"], dgate, ta=True, name=tag + "g_gate")
        g_mlaw = mm(sv["h"], dmla, ta=True, name=tag + "g_mla")
        g_rw = mm(sv["h"], drwc, ta=True, name=tag + "g_rw")
        g_cv = mm(sv["h"], dcvc, ta=True, name=tag + "g_cv")
        dh = mm(dgate, lw["gate"], tb=True, name=tag + "d_h_gate")
        dh = mm(dmla, lw["mla"], tb=True, add=dh, name=tag + "d_h_mla")
        dh = mm(drwc, lw["rw"], tb=True, add=dh, name=tag + "d_h_rw")
        dh = mm(dcvc, lw["cv"], tb=True, add=dh, name=tag + "d_h_cv")
        (dx,), (g_n1,) = rows_bwd(_fn_norm, [sv["x"]], sv["p_norm1"], [], [[dh] + dh_extra], tile=t_row,
                                  name=tag + "norm1", extra={0: [dx1]})
        folded = fold_layer_grads(dict(gate=g_gate, mla=g_mlaw, rw=g_rw, cv=g_cv, wq=g_wq, wk=g_wk, wv=g_wv, wo=g_wo,
                                       rwo=g_rwo, cvo=g_cvo, out=g_out, up=g_up, down=g_down))
        for n, val in folded.items():
            grads[n][l] = val
        grads["attn_norm"][l], grads["mlp_norm"][l] = g_n1, g_n2
        grads["mla_q_a_norm"][l], grads["mla_kv_a_norm"][l] = g_mla
        grads["mla_q_norm"][l], grads["mla_k_norm"][l] = g_qk
        grads["rwkv_ln_w"][l], grads["rwkv_ln_b"][l], grads["rwkv_r_k"][l] = g_post
        for n, val in zip(["rwkv_mu", "rwkv_w0", "rwkv_w2", "rwkv_a0", "rwkv_a2", "rwkv_g2", "rwkv_k_k", "rwkv_k_a"],
                          g_prep[:8]):
            grads[n][l] = val
        if vres:
            for n, val in zip(["rwkv_v1", "rwkv_v_mu", "rwkv_v0", "rwkv_v2"], g_prep[8:12]):
                grads[n][l - 1] = val
        grads["conv_w"][l] = jnp.concatenate(g_cw, axis=0)
    return loss, dx, grads


def _split3(a):
    hi = a.astype(bf16)
    r1 = a - hi.astype(f32)
    mid = r1.astype(bf16)
    lo = (r1 - mid.astype(f32)).astype(bf16)
    return hi, mid, lo


def _shard_axis(name):
    return 1 if name in ROW_SHARDED else 2


def _pack(pieces, width, dtype, row_align):
    flat = jnp.concatenate([p.reshape(-1).astype(dtype) for p in pieces])
    rows = -(-flat.shape[0] // width)
    rows = -(-rows // row_align) * row_align
    return jnp.pad(flat, (0, rows * width - flat.shape[0])).reshape(rows, width)


def _unpack(flat2d, shapes):
    flat = flat2d.reshape(-1)
    out, off = [], 0
    for shp in shapes:
        n = int(np.prod(shp))
        out.append(flat[off:off + n].reshape(shp))
        off += n
    return out


def kernel(x, positions, attn_norm, w_in, mla_q_a_norm, mla_wq_b, mla_kv_a_norm, mla_wkv_b, mla_q_norm, mla_k_norm, mla_w_o, rwkv_mu, rwkv_w0, rwkv_w2, rwkv_a0, rwkv_a2, rwkv_g2, rwkv_k_k, rwkv_k_a, rwkv_r_k, rwkv_ln_w, rwkv_ln_b, rwkv_w_o, rwkv_v1, rwkv_v_mu, rwkv_v0, rwkv_v2, conv_w, conv_w_o, w_out, mlp_norm, w_up, w_down, loss_target, m_attn_norm, m_w_in, m_mla_q_a_norm, m_mla_wq_b, m_mla_kv_a_norm, m_mla_wkv_b, m_mla_q_norm, m_mla_k_norm, m_mla_w_o, m_rwkv_mu, m_rwkv_w0, m_rwkv_w2, m_rwkv_a0, m_rwkv_a2, m_rwkv_g2, m_rwkv_k_k, m_rwkv_k_a, m_rwkv_r_k, m_rwkv_ln_w, m_rwkv_ln_b, m_rwkv_w_o, m_rwkv_v1, m_rwkv_v_mu, m_rwkv_v0, m_rwkv_v2, m_conv_w, m_conv_w_o, m_w_out, m_mlp_norm, m_w_up, m_w_down, v_attn_norm, v_w_in, v_mla_q_a_norm, v_mla_wq_b, v_mla_kv_a_norm, v_mla_wkv_b, v_mla_q_norm, v_mla_k_norm, v_mla_w_o, v_rwkv_mu, v_rwkv_w0, v_rwkv_w2, v_rwkv_a0, v_rwkv_a2, v_rwkv_g2, v_rwkv_k_k, v_rwkv_k_a, v_rwkv_r_k, v_rwkv_ln_w, v_rwkv_ln_b, v_rwkv_w_o, v_rwkv_v1, v_rwkv_v_mu, v_rwkv_v0, v_rwkv_v2, v_conv_w, v_conv_w_o, v_w_out, v_mlp_norm, v_w_up, v_w_down):
    args = dict(locals())
    wts = {n: args[n] for n in WEIGHTS}
    mom = {n: args["m_" + n] for n in WEIGHTS}
    var = {n: args["v_" + n] for n in WEIGHTS}
    chip = 2 * lax.axis_index("x") + lax.axis_index("y")

    gathered_names = BIG + [n for n in MED if n != "conv_w"]
    pieces = [wts[n] for n in gathered_names] + list(_split3(wts["conv_w"]))
    piece_shapes = [p.shape for p in pieces]
    send = _pack(pieces, 1024, bf16, 16)
    got = chip_exchange(send, gather=True, name="gather_weights")
    per_chip = [_unpack(got[j], piece_shapes) for j in range(4)]
    whole = {}
    for q, n in enumerate(gathered_names):
        whole[n] = jnp.concatenate([per_chip[j][q] for j in range(4)], axis=_shard_axis(n))
        if n in MED:
            whole[n] = whole[n].astype(f32)
    base = len(gathered_names)
    cw_parts = [jnp.concatenate([per_chip[j][base + t] for j in range(4)], axis=2).astype(f32) for t in range(3)]
    whole["conv_w"] = (cw_parts[0] + cw_parts[1]) + cw_parts[2]
    small = {n: wts[n] for n in SMALL}
    small["rwkv_r_k"] = wts["rwkv_r_k"].reshape(DEPTH, RW)

    loss, grad_x, grads = local_step(x[0], positions[0], loss_target[0], whole, small)
    loss = lax.psum(loss, ("x", "y", "c"))

    sm_names = SMALL + MED
    sm_grads = [jnp.stack(grads[n]) for n in sm_names]
    sm_shapes = [g.shape for g in sm_grads]
    sm_sum = _unpack(all_reduce_small(_pack(sm_grads, 128, f32, 8), name="reduce_small"), sm_shapes)
    gsum = {}
    for n, g in zip(sm_names, sm_sum):
        if n in MED:
            ax = _shard_axis(n)
            width = wts[n].shape[ax]
            g = lax.dynamic_slice_in_dim(g, chip * width, width, axis=ax)
        gsum[n] = g.reshape(wts[n].shape)
    big_full = [jnp.stack(grads[n]) for n in BIG]
    shard_shapes = [wts[n].shape for n in BIG]
    outgoing = []
    for j in range(4):
        blocks = []
        for n, g in zip(BIG, big_full):
            ax = _shard_axis(n)
            width = wts[n].shape[ax]
            blocks.append(lax.slice_in_dim(g, j * width, (j + 1) * width, axis=ax))
        outgoing.append(_pack(blocks, 1024, f32, 8))
    incoming = chip_exchange(jnp.stack(outgoing), gather=False, name="scatter_grads")
    mine = sum4(incoming, name="sum_chips")
    theirs = sibling_swap(mine, name="swap_cores")
    mine_parts = _unpack(mine, shard_shapes)
    their_parts = _unpack(theirs, shard_shapes)

    out_g, out_d, out_m, out_v = {}, {}, {}, {}
    for q, n in enumerate(BIG):
        shp = wts[n].shape
        as2d = lambda a: a.reshape(-1, shp[-1])
        res = adamw(as2d(wts[n]), [as2d(mine_parts[q]), as2d(their_parts[q])], as2d(mom[n]), as2d(var[n]),
                    name="adamw_" + n)
        out_g[n], out_d[n], out_m[n], out_v[n] = [r.reshape(shp) for r in res]
    sm_all = SMALL + MED
    sm_shapes2 = [wts[n].shape for n in sm_all]
    res = adamw(_pack([wts[n] for n in sm_all], 128, f32, 8), _pack([gsum[n] for n in sm_all], 128, f32, 8),
                _pack([mom[n] for n in sm_all], 128, f32, 8), _pack([var[n] for n in sm_all], 128, f32, 8),
                name="adamw_small")
    for tgt, flat in zip((out_g, out_d, out_m, out_v), res):
        for n, val in zip(sm_all, _unpack(flat, sm_shapes2)):
            tgt[n] = val
    return (loss, grad_x[None], *[out_g[n] for n in WEIGHTS], *[out_d[n] for n in WEIGHTS],
            *[out_m[n] for n in WEIGHTS], *[out_v[n] for n in WEIGHTS])
```

```python
import functools

import jax
import jax.numpy as jnp
import numpy as np
from jax import lax
from jax.experimental import pallas as pl
from jax.experimental.pallas import tpu as pltpu

f32, bf16 = jnp.float32, jnp.bfloat16
HI = lax.Precision.HIGHEST
MESH = pl.DeviceIdType.MESH

D = 1024
DEPTH = 2
MLA_H, NOPE, ROPE, DQK, DV = 8, 64, 32, 96, 64
QL, KVL = 384, 256
RW, RH, RN = 256, 4, 64
DL, AL, GL, MVL = 64, 64, 128, 32
CW = 256
DFF = 4096
GATE = 3 * D
MLA_COLS = QL + KVL + ROPE
MLA_PAD = 768
NORM_EPS = 1e-6
GN_EPS = 64e-5
ROPE_THETA = 10000.0
LR, B1, B2, EPS, WD, STEP = 0.001, 0.9, 0.999, 1e-08, 0.01, 10

VMEM_LIMIT = 52 * 1024 * 1024
WKV_CHUNK = 32

BIG = ["w_in", "mla_wq_b", "mla_wkv_b", "mla_w_o", "rwkv_w_o", "conv_w_o", "w_out", "w_up", "w_down"]
MED = ["rwkv_w2", "rwkv_a2", "rwkv_g2", "rwkv_v1", "rwkv_v2", "conv_w"]
ROW_SHARDED = {"w_out", "w_down", "rwkv_v1"}
SMALL = ["attn_norm", "mla_q_a_norm", "mla_kv_a_norm", "mla_q_norm", "mla_k_norm", "rwkv_mu", "rwkv_w0",
         "rwkv_a0", "rwkv_k_k", "rwkv_k_a", "rwkv_r_k", "rwkv_ln_w", "rwkv_ln_b", "rwkv_v_mu", "rwkv_v0",
         "mlp_norm"]
WEIGHTS = ["attn_norm", "w_in", "mla_q_a_norm", "mla_wq_b", "mla_kv_a_norm", "mla_wkv_b", "mla_q_norm",
           "mla_k_norm", "mla_w_o", "rwkv_mu", "rwkv_w0", "rwkv_w2", "rwkv_a0", "rwkv_a2", "rwkv_g2",
           "rwkv_k_k", "rwkv_k_a", "rwkv_r_k", "rwkv_ln_w", "rwkv_ln_b", "rwkv_w_o", "rwkv_v1", "rwkv_v_mu",
           "rwkv_v0", "rwkv_v2", "conv_w", "conv_w_o", "w_out", "mlp_norm", "w_up", "w_down"]


def _cparams(sem=None):
    return pltpu.CompilerParams(dimension_semantics=sem, vmem_limit_bytes=VMEM_LIMIT)


def _pick(dim, pref, align):
    if dim <= pref:
        return dim
    t = (pref // align) * align
    while t >= align:
        if dim % t == 0:
            return t
        t -= align
    return dim


def _bdot(a, b, dims):
    return lax.dot_general(a.astype(bf16), b.astype(bf16), (dims, ((), ())), preferred_element_type=f32)


@jax.custom_vjp
def _mm(a, b):
    return _bdot(a, b, ((1,), (0,)))


def _mm_fwd(a, b):
    return _mm(a, b), (a, b)


def _mm_bwd(res, g):
    a, b = res
    return _bdot(g, b, ((1,), (1,))), _bdot(a, g, ((0,), (0,)))


_mm.defvjp(_mm_fwd, _mm_bwd)


@jax.custom_vjp
def _mm_nt(a, b):
    return _bdot(a, b, ((1,), (1,)))


def _mm_nt_fwd(a, b):
    return _mm_nt(a, b), (a, b)


def _mm_nt_bwd(res, g):
    a, b = res
    return _bdot(g, b, ((1,), (0,))), _bdot(g, a, ((0,), (0,)))


_mm_nt.defvjp(_mm_nt_fwd, _mm_nt_bwd)


def _hdot(a, b):
    return jnp.dot(a, b, precision=HI, preferred_element_type=f32)


def _hdot_nt(a, b):
    return lax.dot_general(a, b, (((1,), (1,)), ((), ())), precision=HI, preferred_element_type=f32)


def _hdot_tn(a, b):
    return lax.dot_general(a, b, (((0,), (0,)), ((), ())), precision=HI, preferred_element_type=f32)


def _rms(x, g, eps=NORM_EPS):
    return x * lax.rsqrt(jnp.mean(x * x, axis=-1, keepdims=True) + eps) * g


def _sigmoid(x):
    return 1.0 / (1.0 + jnp.exp(-x))


def _softplus(x):
    return jnp.maximum(x, 0.0) + jnp.log(1.0 + jnp.exp(-jnp.maximum(x, -x)))


def _lane_split(x, sizes):
    bounds = np.cumsum([0] + list(sizes))

    @jax.custom_vjp
    def split(v):
        return tuple(v[:, int(bounds[q]):int(bounds[q + 1])] for q in range(len(sizes)))

    split.defvjp(lambda v: (split(v), None), lambda _, g: (jnp.concatenate(g, axis=-1),))
    return split(x)


def _unstack(x):
    @jax.custom_vjp
    def unstack(v):
        return tuple(v[q] for q in range(v.shape[0]))

    unstack.defvjp(lambda v: (unstack(v), None), lambda _, g: (jnp.stack(g, axis=0),))
    return unstack(x)


def _shift_mats(t, k):
    r = lax.broadcasted_iota(jnp.int32, (t, t), 0)
    c = lax.broadcasted_iota(jnp.int32, (t, t), 1)
    inner = (r - c == k).astype(f32)
    r8 = lax.broadcasted_iota(jnp.int32, (t, 8), 0)
    c8 = lax.broadcasted_iota(jnp.int32, (t, 8), 1)
    edge = (c8 - r8 == 8 - k).astype(f32)
    return inner, edge


def _shift(x, halo, k):
    inner, edge = _shift_mats(x.shape[0], k)
    return _hdot(inner, x) + _hdot(edge, halo)


def mm(a, b, *, name, ta=False, tb=False, a_batched=False, b_batched=False, reduce_batch=False, add=None,
       act_grad=None, n_split=1, out_dtype=f32, tm=512, tn=1024, tk=512):
    ash, bsh = a.shape[-2:], b.shape[-2:]
    (k_, m_) = ash if ta else ash[::-1]
    (k2_, n_) = bsh[::-1] if tb else bsh
    assert k_ == k2_, (a.shape, b.shape, ta, tb)
    hb = a.shape[0] if a_batched else (b.shape[0] if b_batched else 1)
    batched_out = (a_batched or b_batched) and not reduce_batch
    h_out = hb if batched_out else 1
    h_red = hb if reduce_batch else 1
    tm = _pick(m_, tm, 128)
    tn = _pick(n_ // n_split, tn, 128)
    tk = _pick(k_, tk, 128)
    nm, nn, nk = m_ // tm, n_ // tn, k_ // tk

    def a_map(ho, i, j, hr, kk):
        blk = (kk, i) if ta else (i, kk)
        return ((ho if batched_out else hr),) + blk if a_batched else blk

    def b_map(ho, i, j, hr, kk):
        blk = (j, kk) if tb else (kk, j)
        return ((ho if batched_out else hr),) + blk if b_batched else blk

    a_blk = (tk, tm) if ta else (tm, tk)
    b_blk = (tn, tk) if tb else (tk, tn)
    in_specs = [pl.BlockSpec(((1,) + a_blk) if a_batched else a_blk, a_map),
                pl.BlockSpec(((1,) + b_blk) if b_batched else b_blk, b_map)]
    args = [a, b]
    for extra in (add, act_grad):
        if extra is not None:
            in_specs.append(pl.BlockSpec((tm, tn), lambda ho, i, j, hr, kk: (i, j)))
            args.append(extra)
    if n_split > 1:
        per = n_ // n_split // tn
        if batched_out:
            out_spec = pl.BlockSpec((1, 1, tm, tn), lambda ho, i, j, hr, kk: (j // per, ho, i, j % per))
            out_shape = jax.ShapeDtypeStruct((n_split, hb, m_, n_ // n_split), out_dtype)
        else:
            out_spec = pl.BlockSpec((1, tm, tn), lambda ho, i, j, hr, kk: (j // per, i, j % per))
            out_shape = jax.ShapeDtypeStruct((n_split, m_, n_ // n_split), out_dtype)
    elif batched_out:
        out_spec = pl.BlockSpec((1, tm, tn), lambda ho, i, j, hr, kk: (ho, i, j))
        out_shape = jax.ShapeDtypeStruct((hb, m_, n_), out_dtype)
    else:
        out_spec = pl.BlockSpec((tm, tn), lambda ho, i, j, hr, kk: (i, j))
        out_shape = jax.ShapeDtypeStruct((m_, n_), out_dtype)
    lead = (0,) * (int(batched_out) + int(n_split > 1))
    dims = ((0,) if ta else (1,), (1,) if tb else (0,))
    has_add, has_act = add is not None, act_grad is not None

    def body(*refs):
        a_ref, b_ref = refs[0], refs[1]
        pos = 2
        add_ref = act_ref = None
        if has_add:
            add_ref = refs[pos]
            pos += 1
        if has_act:
            act_ref = refs[pos]
            pos += 1
        o_ref, acc_ref = refs[pos], refs[pos + 1]
        hr, kk = pl.program_id(3), pl.program_id(4)
        first = jnp.logical_and(hr == 0, kk == 0)
        last = jnp.logical_and(hr == h_red - 1, kk == nk - 1)
        av = a_ref[0] if a_batched else a_ref[...]
        bv = b_ref[0] if b_batched else b_ref[...]
        p = _bdot(av, bv, dims)

        @pl.when(first)
        def _():
            acc_ref[...] = p

        @pl.when(jnp.logical_not(first))
        def _():
            acc_ref[...] += p

        @pl.when(last)
        def _():
            r = acc_ref[...]
            if has_act:
                r = r * (2.0 * jnp.maximum(act_ref[...], 0.0))
            if has_add:
                r = r + add_ref[...]
            if lead:
                o_ref[lead] = r.astype(out_dtype)
            else:
                o_ref[...] = r.astype(out_dtype)

    return pl.pallas_call(
        body, name=name, grid=(h_out, nm, nn, h_red, nk), in_specs=in_specs, out_specs=out_spec,
        out_shape=out_shape, scratch_shapes=[pltpu.VMEM((tm, tn), f32)],
        compiler_params=_cparams(("parallel", "parallel", "parallel", "arbitrary", "arbitrary")),
    )(*args)


def _row_spec(arr, tile, idx):
    if arr.ndim == 2:
        return pl.BlockSpec((tile, arr.shape[1]), lambda i: (idx(i), 0))
    return pl.BlockSpec((arr.shape[0], tile, arr.shape[2]), lambda i: (0, idx(i), 0))


def _halo_spec(arr, tile, idx):
    per = tile // 8
    return pl.BlockSpec((8, arr.shape[1]), lambda i: (jnp.maximum(idx(i) * per - 1, 0), 0))


def _full_spec(arr):
    nd = arr.ndim
    return pl.BlockSpec(arr.shape, lambda i: (0,) * nd)


def rows_fwd(fn, rows, params, consts, out_shapes, *, tile, name, halos=()):
    s_len = rows[0].shape[-2]
    n = s_len // tile
    nr, nh, npar, nc = len(rows), len(halos), len(params), len(consts)
    ident = lambda i: i
    in_specs = ([_row_spec(r, tile, ident) for r in rows] + [_halo_spec(rows[h], tile, ident) for h in halos]
                + [_full_spec(p) for p in params] + [_full_spec(c) for c in consts])
    out_specs = [_row_spec(o, tile, ident) for o in out_shapes]

    def body(*refs):
        i = pl.program_id(0)
        rv = [r[...] for r in refs[:nr]]
        keep = (i > 0).astype(f32)
        hv = [r[...] * keep for r in refs[nr:nr + nh]]
        pv = [r[...] for r in refs[nr + nh:nr + nh + npar]]
        cv = [r[...] for r in refs[nr + nh + npar:nr + nh + npar + nc]]
        outs = fn(rv, hv, pv, cv)
        for o_ref, o in zip(refs[nr + nh + npar + nc:], outs):
            o_ref[...] = o

    return pl.pallas_call(
        body, name=name, grid=(n,), in_specs=in_specs, out_specs=out_specs, out_shape=list(out_shapes),
        compiler_params=_cparams(("arbitrary",)),
    )(*rows, *[rows[h] for h in halos], *params, *consts)


def rows_bwd(fn, rows, params, consts, douts, *, tile, name, halos=(), grad_rows=None, extra=None):
    s_len = rows[0].shape[-2]
    n = s_len // tile
    nr, nh, npar, nc = len(rows), len(halos), len(params), len(consts)
    grad_rows = list(range(nr)) if grad_rows is None else list(grad_rows)
    extra = extra or {}
    assert all(h in grad_rows for h in halos)
    rev = lambda i: n - 1 - i
    dflat = [d for ds in douts for d in ds]
    dcount = [len(ds) for ds in douts]
    eflat = [e for g in grad_rows for e in extra.get(g, [])]
    ecount = [len(extra.get(g, [])) for g in grad_rows]
    in_specs = ([_row_spec(r, tile, rev) for r in rows] + [_halo_spec(rows[h], tile, rev) for h in halos]
                + [_full_spec(p) for p in params] + [_full_spec(c) for c in consts]
                + [_row_spec(d, tile, rev) for d in dflat] + [_row_spec(e, tile, rev) for e in eflat])
    out_shapes = ([jax.ShapeDtypeStruct(rows[g].shape, f32) for g in grad_rows]
                  + [jax.ShapeDtypeStruct(p.shape, f32) for p in params])
    out_specs = [_row_spec(rows[g], tile, rev) for g in grad_rows] + [_full_spec(p) for p in params]
    scratch = [pltpu.VMEM((8, rows[h].shape[1]), f32) for h in halos]
    n_in = nr + nh + npar + nc + len(dflat) + len(eflat)
    n_out = len(grad_rows) + npar

    def body(*refs):
        i = pl.program_id(0)
        rv = [r[...] for r in refs[:nr]]
        keep = (i < n - 1).astype(f32)
        hv = [r[...] * keep for r in refs[nr:nr + nh]]
        pv = [r[...] for r in refs[nr + nh:nr + nh + npar]]
        pos = nr + nh + npar
        cv = [r[...] for r in refs[pos:pos + nc]]
        pos += nc
        dv = []
        for cnt in dcount:
            acc = refs[pos][...]
            for q in range(1, cnt):
                acc = acc + refs[pos + q][...]
            dv.append(acc)
            pos += cnt
        ev = []
        for cnt in ecount:
            ev.append([refs[pos + q][...] for q in range(cnt)])
            pos += cnt
        out_refs = refs[n_in:n_in + n_out]
        carry_refs = refs[n_in + n_out:]

        def f(gr, gh, gp):
            full = list(rv)
            for g, val in zip(grad_rows, gr):
                full[g] = val
            return tuple(fn(full, gh, gp, cv))

        _, vjp = jax.vjp(f, [rv[g] for g in grad_rows], hv, pv)
        d_rows, d_halos, d_params = vjp(tuple(dv))

        @pl.when(i == 0)
        def _():
            for c_ref in carry_refs:
                c_ref[...] = jnp.zeros_like(c_ref)
            for p_ref in out_refs[len(grad_rows):]:
                p_ref[...] = jnp.zeros_like(p_ref)

        for q, g in enumerate(grad_rows):
            val = d_rows[q]
            for e in ev[q]:
                val = val + e
            out_refs[q][...] = val
            if g in halos:
                hq = list(halos).index(g)
                out_refs[q][tile - 8:tile, :] += carry_refs[hq][...]
                carry_refs[hq][...] = d_halos[hq]
        for p_ref, dp in zip(out_refs[len(grad_rows):], d_params):
            p_ref[...] += dp

    res = pl.pallas_call(
        body, name=name, grid=(n,), in_specs=in_specs, out_specs=out_specs, out_shape=out_shapes,
        scratch_shapes=scratch, compiler_params=_cparams(("arbitrary",)),
    )(*rows, *[rows[h] for h in halos], *params, *consts, *dflat, *eflat)
    return list(res[:len(grad_rows)]), list(res[len(grad_rows):])


def _fn_norm(rows, halos, params, consts):
    return (_rms(rows[0], params[0]),)


def _fn_mla_prep(rows, halos, params, consts):
    cq, ckv, kpe = _lane_split(rows[0], (QL, KVL, MLA_PAD - QL - KVL))
    return _rms(cq, params[0]), _rms(ckv, params[1]), kpe


def _rope(x, cos, sin, rot):
    return x * cos + _hdot(x, rot) * sin


def _fn_qk_post(rows, halos, params, consts):
    q_raw, kn_pad, kpe, cos, sin = rows
    q_norm, k_norm = params
    place, rot = consts
    kpe96 = _hdot(kpe, place)
    qs = [_rope(_rms(qh, q_norm), cos, sin, rot) for qh in _unstack(q_raw)]
    ks = [_rope(_rms(kh + kpe96, k_norm), cos, sin, rot) for kh in _unstack(kn_pad)]
    return jnp.stack(qs, axis=0), jnp.stack(ks, axis=0)


def _seg(x, bd):
    return _hdot(x, bd)


def _make_fn_rwkv_prep(vres):
    def fn(rows, halos, params, consts):
        cols = rows[0]
        bd = consts[0]
        mu, w0, w2, a0, a2, g2, k_k, k_a = params[:8]
        prev = _shift(cols, halos[0], 1)
        c = cols + (prev - cols) * mu
        r, k, v, xw, xa, xg = _lane_split(c, (RW, RW, RW, DL, AL, GL))
        log_w = -_softplus(-(w0 + _mm(jnp.tanh(xw), w2))) - 0.5
        ld = -jnp.exp(log_w)
        a = _sigmoid(a0 + _mm(xa, a2))
        g = _mm(_sigmoid(xg), g2)
        if vres:
            hcur, v_first = rows[1], rows[2]
            v1, v_mu, v0, v2 = params[8:12]
            xv = _mm(hcur, v1)
            xv_prev = _shift(xv, _mm(halos[1], v1), 1)
            xv = xv + (xv_prev - xv) * v_mu
            v = v + (v_first - v) * _sigmoid(v0 + _mm(xv, v2))
        kk = k * k_k
        kk = kk / jnp.maximum(jnp.sqrt(_seg(kk * kk, bd)), 1e-12)
        k2 = k * (1.0 + (a - 1.0) * k_a)
        return r, ld, k2, v, -kk, kk * a, g
    return fn


def _fn_rwkv_post(rows, halos, params, consts):
    y, r, k2, v, g = rows
    ln_w, ln_b, r_k = params
    bd = consts[0]
    mean = _seg(y, bd) * (1.0 / RN)
    d = y - mean
    var = _seg(d * d, bd) * (1.0 / RN)
    yn = d * lax.rsqrt(var + GN_EPS) * ln_w + ln_b
    bonus = _seg(r * k2 * r_k, bd) * v
    return ((yn + bonus) * g,)


def _fn_conv(rows, halos, params, consts):
    cols, halo = rows[0], halos[0]
    w0, w1, w2 = params
    b, c, x = _lane_split(cols, (CW, CW, CW))
    _, ch, xh = _lane_split(halo, (CW, CW, CW))
    u, uh = c * x, ch * xh
    return (b * (w0 * _shift(u, uh, 2) + w1 * _shift(u, uh, 1) + w2 * u),)


def _fn_merge(rows, halos, params, consts):
    gate, o_a, o_b, o_c = rows
    g_a, g_b, g_c = _lane_split(gate, (D, D, D))
    return (_sigmoid(g_a) * o_a + _sigmoid(g_b) * o_b + _sigmoid(g_c) * o_c,)


def _fn_relu2(rows, halos, params, consts):
    return (jnp.square(jnp.maximum(rows[0], 0.0)),)


def _attn_block(q, k, v, q0):
    tq, s_len = q.shape[0], k.shape[0]
    s = _mm_nt(q, k) * (DQK ** -0.5)
    row = q0 + lax.broadcasted_iota(jnp.int32, (tq, s_len), 0)
    col = lax.broadcasted_iota(jnp.int32, (tq, s_len), 1)
    s = jnp.where(row >= col, s, -1e30)
    m = lax.stop_gradient(jnp.max(s, axis=-1, keepdims=True))
    e = jnp.exp(s - m)
    p = e / jnp.sum(e, axis=-1, keepdims=True)
    return _mm(p, v)


def attn_fwd(q, k, v, *, tq, name):
    h, s_len, _ = q.shape

    def body(q_ref, k_ref, v_ref, o_ref):
        o_ref[0] = _attn_block(q_ref[0], k_ref[0], v_ref[0], pl.program_id(1) * tq)

    return pl.pallas_call(
        body, name=name, grid=(h, s_len // tq),
        in_specs=[pl.BlockSpec((1, tq, DQK), lambda hh, i: (hh, i, 0)),
                  pl.BlockSpec((1, s_len, DQK), lambda hh, i: (hh, 0, 0)),
                  pl.BlockSpec((1, s_len, DV), lambda hh, i: (hh, 0, 0))],
        out_specs=pl.BlockSpec((1, tq, DV), lambda hh, i: (hh, i, 0)),
        out_shape=jax.ShapeDtypeStruct((h, s_len, DV), f32),
        compiler_params=_cparams(("parallel", "arbitrary")),
    )(q, k, v)


def attn_bwd(q, k, v, do, *, tq, name):
    h, s_len, _ = q.shape

    def body(q_ref, k_ref, v_ref, do_ref, dq_ref, dk_ref, dv_ref):
        i = pl.program_id(1)
        _, vjp = jax.vjp(functools.partial(_attn_block, q0=i * tq), q_ref[0], k_ref[0], v_ref[0])
        dq, dk, dv = vjp(do_ref[0])
        dq_ref[0] = dq

        @pl.when(i == 0)
        def _():
            dk_ref[0] = dk
            dv_ref[0] = dv

        @pl.when(i > 0)
        def _():
            dk_ref[0] += dk
            dv_ref[0] += dv

    return pl.pallas_call(
        body, name=name, grid=(h, s_len // tq),
        in_specs=[pl.BlockSpec((1, tq, DQK), lambda hh, i: (hh, i, 0)),
                  pl.BlockSpec((1, s_len, DQK), lambda hh, i: (hh, 0, 0)),
                  pl.BlockSpec((1, s_len, DV), lambda hh, i: (hh, 0, 0)),
                  pl.BlockSpec((1, tq, DV), lambda hh, i: (hh, i, 0))],
        out_specs=[pl.BlockSpec((1, tq, DQK), lambda hh, i: (hh, i, 0)),
                   pl.BlockSpec((1, s_len, DQK), lambda hh, i: (hh, 0, 0)),
                   pl.BlockSpec((1, s_len, DV), lambda hh, i: (hh, 0, 0))],
        out_shape=[jax.ShapeDtypeStruct((h, s_len, DQK), f32), jax.ShapeDtypeStruct((h, s_len, DQK), f32),
                   jax.ShapeDtypeStruct((h, s_len, DV), f32)],
        compiler_params=_cparams(("parallel", "arbitrary")),
    )(q, k, v, do)


def _wkv_chunk(s0, r, ld, k, v, a, b):
    c = r.shape[0]
    ri = lax.broadcasted_iota(jnp.int32, (c, c), 0)
    ci = lax.broadcasted_iota(jnp.int32, (c, c), 1)
    incl, strict = ri >= ci, ri > ci
    cum = _hdot(incl.astype(f32), ld)
    tot = jnp.sum(ld, axis=0, keepdims=True)
    w_incl, w_excl, w_inv, w_rest = jnp.exp(cum), jnp.exp(cum - ld), jnp.exp(-cum), jnp.exp(tot - cum)
    ab, rb, bb, kb = a * w_excl, r * w_incl, b * w_inv, k * w_inv
    l_ab = jnp.where(strict, _hdot_nt(ab, bb), 0.0)
    l_ak = jnp.where(strict, _hdot_nt(ab, kb), 0.0)
    m_rb = jnp.where(incl, _hdot_nt(rb, bb), 0.0)
    m_rk = jnp.where(incl, _hdot_nt(rb, kb), 0.0)
    u = _hdot_nt(ab, s0) + _hdot(l_ak, v)
    lp, span = l_ab, 1
    while span < c:
        u = u + _hdot(lp, u)
        span *= 2
        if span < c:
            lp = _hdot(lp, lp)
    y = _hdot_nt(rb, s0) + _hdot(m_rb, u) + _hdot(m_rk, v)
    s1 = s0 * jnp.exp(tot) + _hdot_tn(u, b * w_rest) + _hdot_tn(v, k * w_rest)
    return y, s1


def wkv_fwd(r, ld, k, v, a, b, *, name):
    s_len = r.shape[0]
    c = WKV_CHUNK
    n = s_len // c

    def body(r_ref, ld_ref, k_ref, v_ref, a_ref, b_ref, y_ref, st_ref, s_sc):
        @pl.when(pl.program_id(0) == 0)
        def _():
            s_sc[...] = jnp.zeros_like(s_sc)

        ys = []
        for h in range(RH):
            sl = slice(RN * h, RN * (h + 1))
            s0 = s_sc[h]
            st_ref[0, h] = s0
            y, s1 = _wkv_chunk(s0, r_ref[:, sl], ld_ref[:, sl], k_ref[:, sl], v_ref[:, sl], a_ref[:, sl],
                               b_ref[:, sl])
            s_sc[h] = s1
            ys.append(y)
        y_ref[...] = jnp.concatenate(ys, axis=-1)

    bs = pl.BlockSpec((c, RW), lambda i: (i, 0))
    return pl.pallas_call(
        body, name=name, grid=(n,), in_specs=[bs] * 6,
        out_specs=[bs, pl.BlockSpec((1, RH, RN, RN), lambda i: (i, 0, 0, 0))],
        out_shape=[jax.ShapeDtypeStruct((s_len, RW), f32), jax.ShapeDtypeStruct((n, RH, RN, RN), f32)],
        scratch_shapes=[pltpu.VMEM((RH, RN, RN), f32)], compiler_params=_cparams(("arbitrary",)),
    )(r, ld, k, v, a, b)


def wkv_bwd(r, ld, k, v, a, b, states, dy, *, name):
    s_len = r.shape[0]
    c = WKV_CHUNK
    n = s_len // c

    def body(r_ref, ld_ref, k_ref, v_ref, a_ref, b_ref, st_ref, dy_ref, dr, dld, dk, dv, da, db, ds_sc):
        @pl.when(pl.program_id(0) == 0)
        def _():
            ds_sc[...] = jnp.zeros_like(ds_sc)

        outs = [[] for _ in range(6)]
        for h in range(RH):
            sl = slice(RN * h, RN * (h + 1))
            args = (st_ref[0, h], r_ref[:, sl], ld_ref[:, sl], k_ref[:, sl], v_ref[:, sl], a_ref[:, sl],
                    b_ref[:, sl])
            _, vjp = jax.vjp(_wkv_chunk, *args)
            g = vjp((dy_ref[:, sl], ds_sc[h]))
            ds_sc[h] = g[0]
            for j in range(6):
                outs[j].append(g[1 + j])
        for j, o_ref in enumerate((dr, dld, dk, dv, da, db)):
            o_ref[...] = jnp.concatenate(outs[j], axis=-1)

    bs = pl.BlockSpec((c, RW), lambda i: (n - 1 - i, 0))
    return pl.pallas_call(
        body, name=name, grid=(n,),
        in_specs=[bs] * 6 + [pl.BlockSpec((1, RH, RN, RN), lambda i: (n - 1 - i, 0, 0, 0)), bs],
        out_specs=[bs] * 6, out_shape=[jax.ShapeDtypeStruct((s_len, RW), f32)] * 6,
        scratch_shapes=[pltpu.VMEM((RH, RN, RN), f32)], compiler_params=_cparams(("arbitrary",)),
    )(r, ld, k, v, a, b, states, dy)


def loss_head(y, target, *, tile, name):
    s_len, d = y.shape
    n = s_len // tile

    def body(y_ref, t_ref, dy_ref, l_ref):
        err = y_ref[...] - t_ref[...]
        dy_ref[...] = err * (1.0 / d)
        part = 0.5 * jnp.sum(jnp.mean(err * err, axis=-1, keepdims=True), axis=0, keepdims=True)

        @pl.when(pl.program_id(0) == 0)
        def _():
            l_ref[...] = jnp.zeros_like(l_ref)

        l_ref[...] += jnp.broadcast_to(part, l_ref.shape)

    bs = pl.BlockSpec((tile, d), lambda i: (i, 0))
    dy, l = pl.pallas_call(
        body, name=name, grid=(n,), in_specs=[bs, bs],
        out_specs=[bs, pl.BlockSpec((8, 128), lambda i: (0, 0))],
        out_shape=[jax.ShapeDtypeStruct((s_len, d), f32), jax.ShapeDtypeStruct((8, 128), f32)],
        compiler_params=_cparams(("arbitrary",)),
    )(y, target)
    return l[0, 0], dy


def adamw(w, g, m, v, *, name):
    gs = g if isinstance(g, (list, tuple)) else [g]
    rows, cols = w.shape
    tile = _pick(rows, max(8, (2 * 1024 * 1024 // (4 * cols)) // 8 * 8), 8)
    c1 = 1.0 - B1 ** STEP
    c2 = 1.0 - B2 ** STEP
    ng = len(gs)

    def body(*refs):
        w_ref, m_ref, v_ref = refs[0], refs[1 + ng], refs[2 + ng]
        g_out, d_out, m_out, v_out = refs[3 + ng:]
        gv = refs[1][...]
        for q in range(1, ng):
            gv = gv + refs[1 + q][...]
        mn = B1 * m_ref[...] + (1.0 - B1) * gv
        vn = B2 * v_ref[...] + (1.0 - B2) * (gv * gv)
        d_out[...] = -LR * ((mn / c1) / (jnp.sqrt(vn / c2) + EPS) + WD * w_ref[...])
        g_out[...] = gv
        m_out[...] = mn
        v_out[...] = vn

    bs = pl.BlockSpec((tile, cols), lambda i: (i, 0))
    return pl.pallas_call(
        body, name=name, grid=(rows // tile,), in_specs=[bs] * (3 + ng), out_specs=[bs] * 4,
        out_shape=[jax.ShapeDtypeStruct((rows, cols), f32)] * 4, compiler_params=_cparams(("parallel",)),
    )(w, *gs, m, v)


def _place():
    return lax.axis_index("x"), lax.axis_index("y"), lax.axis_index("c")


_ANY = pl.BlockSpec(memory_space=pl.ANY)


def _peer_chips(x, y):
    return [(1 - x, y), (x, 1 - y), (1 - x, 1 - y)]


def gather_weights(shards, *, name):
    nk = len(shards)

    def body(*refs):
        srcs, outs = refs[:nk], refs[nk:2 * nk]
        ici_send, ici_recv, d2d_send, d2d_recv, local_sems = refs[2 * nk:]
        x, y, c = _place()
        me = 2 * x + y
        peers = _peer_chips(x, y)
        pending, locals_ = [], []
        for k in range(nk):
            half = srcs[k].shape[0] // 2
            mine = pl.ds(c * half, half)
            local = pltpu.make_async_copy(srcs[k], outs[k].at[me], local_sems.at[k])
            local.start()
            locals_.append(local)
            for p, (px, py) in enumerate(peers):
                cp = pltpu.make_async_remote_copy(
                    src_ref=srcs[k].at[mine], dst_ref=outs[k].at[me, mine], send_sem=ici_send.at[k, p],
                    recv_sem=ici_recv.at[k, p], device_id=(px, py, c), device_id_type=MESH)
                cp.start()
                pending.append(cp)
        for k in range(nk):
            half = srcs[k].shape[0] // 2
            mine = pl.ds(c * half, half)
            for p, (px, py) in enumerate(peers):
                landed = outs[k].at[2 * px + py, mine]
                pltpu.make_async_remote_copy(
                    src_ref=srcs[k].at[mine], dst_ref=landed, send_sem=ici_send.at[k, p], recv_sem=ici_recv.at[k, p],
                    device_id=(px, py, c), device_id_type=MESH).wait_recv()
                fwd = pltpu.make_async_remote_copy(
                    src_ref=landed, dst_ref=landed, send_sem=d2d_send.at[k, p], recv_sem=d2d_recv.at[k, p],
                    device_id=(x, y, 1 - c), device_id_type=MESH)
                fwd.start()
                pending.append(fwd)
        for k in range(nk):
            half = srcs[k].shape[0] // 2
            other = pl.ds((1 - c) * half, half)
            for p, (px, py) in enumerate(peers):
                theirs = outs[k].at[2 * px + py, other]
                pltpu.make_async_remote_copy(
                    src_ref=theirs, dst_ref=theirs, send_sem=d2d_send.at[k, p], recv_sem=d2d_recv.at[k, p],
                    device_id=(x, y, 1 - c), device_id_type=MESH).wait_recv()
        for cp in pending:
            cp.wait_send()
        for cp in locals_:
            cp.wait()

    sem = lambda *shape: pltpu.SemaphoreType.DMA(shape)
    return pl.pallas_call(
        body, name=name, in_specs=[_ANY] * nk, out_specs=[_ANY] * nk,
        out_shape=[jax.ShapeDtypeStruct((4,) + s.shape, s.dtype) for s in shards],
        scratch_shapes=[sem(nk, 3), sem(nk, 3), sem(nk, 3), sem(nk, 3), sem(nk)],
    )(*shards)


def grads_to_sibling(parts, *, name):
    nk = len(parts)

    def body(*refs):
        srcs, outs = refs[:nk], refs[nk:2 * nk]
        send_sems, recv_sems = refs[2 * nk:]
        x, y, c = _place()
        sends = []
        for k in range(nk):
            half = srcs[k].shape[1] // 2
            cp = pltpu.make_async_remote_copy(
                src_ref=srcs[k].at[:, pl.ds((1 - c) * half, half), :], dst_ref=outs[k], send_sem=send_sems.at[k],
                recv_sem=recv_sems.at[k], device_id=(x, y, 1 - c), device_id_type=MESH)
            cp.start()
            sends.append(cp)
        for cp in sends:
            cp.wait_recv()
        for cp in sends:
            cp.wait_send()

    return pl.pallas_call(
        body, name=name, in_specs=[_ANY] * nk, out_specs=[_ANY] * nk,
        out_shape=[jax.ShapeDtypeStruct((4, p.shape[1] // 2, p.shape[2]), p.dtype) for p in parts],
        scratch_shapes=[pltpu.SemaphoreType.DMA((nk,)), pltpu.SemaphoreType.DMA((nk,))],
    )(*parts)


def pair_sum(part, theirs, core, *, name):
    _, rows, cols = part.shape
    half = rows // 2
    tile = _pick(half, max(16, (1 << 20) // (4 * cols) // 16 * 16), 16)
    per = half // tile

    def body(c_ref, p_ref, t_ref, o_ref):
        o_ref[...] = (p_ref[...] + t_ref[...]).astype(bf16)

    grid_spec = pltpu.PrefetchScalarGridSpec(
        num_scalar_prefetch=1, grid=(4, per),
        in_specs=[pl.BlockSpec((1, tile, cols), lambda j, i, c_ref: (j, c_ref[0] * per + i, 0)),
                  pl.BlockSpec((1, tile, cols), lambda j, i, c_ref: (j, i, 0))],
        out_specs=pl.BlockSpec((1, tile, cols), lambda j, i, c_ref: (j, i, 0)))
    return pl.pallas_call(
        body, name=name, grid_spec=grid_spec, out_shape=jax.ShapeDtypeStruct((4, half, cols), bf16),
        compiler_params=_cparams(("parallel", "parallel")),
    )(core, part, theirs)


def chips_all_to_all(parts, *, name):
    nk = len(parts)

    def body(*refs):
        srcs, outs = refs[:nk], refs[nk:2 * nk]
        send_sems, recv_sems, local_sems = refs[2 * nk:]
        x, y, c = _place()
        me = 2 * x + y
        peers = _peer_chips(x, y)
        sends, locals_ = [], []
        for k in range(nk):
            local = pltpu.make_async_copy(srcs[k].at[me], outs[k].at[me], local_sems.at[k])
            local.start()
            locals_.append(local)
            for p, (px, py) in enumerate(peers):
                cp = pltpu.make_async_remote_copy(
                    src_ref=srcs[k].at[2 * px + py], dst_ref=outs[k].at[me], send_sem=send_sems.at[k, p],
                    recv_sem=recv_sems.at[k, p], device_id=(px, py, c), device_id_type=MESH)
                cp.start()
                sends.append(cp)
        for k in range(nk):
            for p, (px, py) in enumerate(peers):
                pltpu.make_async_remote_copy(
                    src_ref=srcs[k].at[me], dst_ref=outs[k].at[2 * px + py], send_sem=send_sems.at[k, p],
                    recv_sem=recv_sems.at[k, p], device_id=(px, py, c), device_id_type=MESH).wait_recv()
        for cp in sends:
            cp.wait_send()
        for cp in locals_:
            cp.wait()

    sem = lambda *shape: pltpu.SemaphoreType.DMA(shape)
    return pl.pallas_call(
        body, name=name, in_specs=[_ANY] * nk, out_specs=[_ANY] * nk,
        out_shape=[jax.ShapeDtypeStruct(p.shape, p.dtype) for p in parts],
        scratch_shapes=[sem(nk, 3), sem(nk, 3), sem(nk)],
    )(*parts)


def join_halves(halves, layout, *, name):
    nk = len(halves)
    n_out = 1 + max(o for o, _, _ in layout)
    out_rows = [0] * n_out
    out_cols = [0] * n_out
    for (o, off, rows), hv in zip(layout, halves):
        out_rows[o] = max(out_rows[o], off + rows)
        out_cols[o] = hv.shape[1]

    def body(*refs):
        srcs, outs = refs[:nk], refs[nk:nk + n_out]
        send_sems, recv_sems, local_sems = refs[nk + n_out:]
        x, y, c = _place()
        pending, locals_ = [], []
        for k, (o, off, rows) in enumerate(layout):
            half = rows // 2
            mine = outs[o].at[pl.ds(off + c * half, half), :]
            local = pltpu.make_async_copy(srcs[k], mine, local_sems.at[k])
            local.start()
            locals_.append(local)
            cp = pltpu.make_async_remote_copy(
                src_ref=srcs[k], dst_ref=mine, send_sem=send_sems.at[k], recv_sem=recv_sems.at[k],
                device_id=(x, y, 1 - c), device_id_type=MESH)
            cp.start()
            pending.append(cp)
        for k, (o, off, rows) in enumerate(layout):
            half = rows // 2
            theirs = outs[o].at[pl.ds(off + (1 - c) * half, half), :]
            pltpu.make_async_remote_copy(
                src_ref=srcs[k], dst_ref=theirs, send_sem=send_sems.at[k], recv_sem=recv_sems.at[k],
                device_id=(x, y, 1 - c), device_id_type=MESH).wait_recv()
        for cp in pending:
            cp.wait_send()
        for cp in locals_:
            cp.wait()

    sem = lambda *shape: pltpu.SemaphoreType.DMA(shape)
    return pl.pallas_call(
        body, name=name, in_specs=[_ANY] * nk, out_specs=[_ANY] * n_out,
        out_shape=[jax.ShapeDtypeStruct((out_rows[o], out_cols[o]), f32) for o in range(n_out)],
        scratch_shapes=[sem(nk), sem(nk), sem(nk)],
    )(*halves)


def all_reduce_small(src, *, name):
    rows, cols = src.shape

    def body(s_ref, o_ref, buf, send_sems, recv_sems):
        x, y, c = _place()
        me = 4 * x + 2 * y + c
        buf[me] = s_ref[...]
        sends = []
        for msk in range(1, 8):
            px = x ^ (msk >> 2)
            py = y ^ ((msk >> 1) & 1)
            pc = c ^ (msk & 1)
            cp = pltpu.make_async_remote_copy(
                src_ref=s_ref, dst_ref=buf.at[me], send_sem=send_sems.at[msk - 1], recv_sem=recv_sems.at[msk - 1],
                device_id=(px, py, pc), device_id_type=MESH)
            cp.start()
            sends.append(cp)
        for msk in range(1, 8):
            px = x ^ (msk >> 2)
            py = y ^ ((msk >> 1) & 1)
            pc = c ^ (msk & 1)
            pltpu.make_async_remote_copy(
                src_ref=s_ref, dst_ref=buf.at[4 * px + 2 * py + pc], send_sem=send_sems.at[msk - 1],
                recv_sem=recv_sems.at[msk - 1], device_id=(px, py, pc), device_id_type=MESH).wait_recv()
        for cp in sends:
            cp.wait_send()
        acc = buf[0]
        for d in range(1, 8):
            acc = acc + buf[d]
        o_ref[...] = acc

    vm = pl.BlockSpec(memory_space=pltpu.VMEM)
    return pl.pallas_call(
        body, name=name, in_specs=[vm], out_specs=vm, out_shape=jax.ShapeDtypeStruct((rows, cols), f32),
        scratch_shapes=[pltpu.VMEM((8, rows, cols), f32), pltpu.SemaphoreType.DMA((7,)),
                        pltpu.SemaphoreType.DMA((7,))],
        compiler_params=pltpu.CompilerParams(vmem_limit_bytes=VMEM_LIMIT),
    )(src)


def sum4(parts, *, name):
    _, rows, cols = parts.shape
    tile = _pick(rows, max(16, (1 << 20) // (4 * cols) // 16 * 16), 16)

    def body(p_ref, o_ref):
        p = [p_ref[j].astype(f32) for j in range(4)]
        o_ref[...] = ((p[0] + p[1]) + p[2]) + p[3]

    return pl.pallas_call(
        body, name=name, grid=(rows // tile,), in_specs=[pl.BlockSpec((4, tile, cols), lambda i: (0, i, 0))],
        out_specs=pl.BlockSpec((tile, cols), lambda i: (i, 0)), out_shape=jax.ShapeDtypeStruct((rows, cols), f32),
        compiler_params=_cparams(("parallel",)),
    )(parts)


def _consts():
    idx = np.arange(RW)
    bd = (idx[:, None] // RN == idx[None, :] // RN).astype(np.float32)
    place = np.zeros((128, DQK), np.float32)
    place[np.arange(ROPE), NOPE + np.arange(ROPE)] = 1.0
    rot = np.zeros((DQK, DQK), np.float32)
    half = ROPE // 2
    rot[NOPE + half + np.arange(half), NOPE + np.arange(half)] = -1.0
    rot[NOPE + np.arange(half), NOPE + half + np.arange(half)] = 1.0
    return jnp.asarray(bd), jnp.asarray(place), jnp.asarray(rot)


def _rope_tables(positions):
    freqs = ROPE_THETA ** (-(jnp.arange(ROPE // 2, dtype=f32) * 2.0 / ROPE))
    ang = positions.astype(f32)[:, None] * freqs
    cos, sin = jnp.cos(ang), jnp.sin(ang)
    ones = jnp.ones((positions.shape[0], NOPE), f32)
    return (jnp.concatenate([ones, cos, cos], axis=-1), jnp.concatenate([0.0 * ones, sin, sin], axis=-1))


def derive_layer(w, l):
    w_in = w["w_in"][l]
    pad = jnp.zeros((D, MLA_PAD - MLA_COLS), w_in.dtype)
    wkv = w["mla_wkv_b"][l].reshape(KVL, MLA_H, NOPE + DV)
    wk = jnp.concatenate([wkv[:, :, :NOPE], jnp.zeros((KVL, MLA_H, ROPE), wkv.dtype)], axis=-1)
    return dict(
        gate=w_in[:, :GATE], mla=jnp.concatenate([w_in[:, GATE:GATE + MLA_COLS], pad], axis=1),
        rw=w_in[:, GATE + MLA_COLS:GATE + MLA_COLS + 4 * RW], cv=w_in[:, GATE + MLA_COLS + 4 * RW:],
        wq=w["mla_wq_b"][l].reshape(QL, MLA_H, DQK).transpose(1, 0, 2),
        wk=wk.transpose(1, 0, 2), wv=wkv[:, :, NOPE:].transpose(1, 0, 2),
        wo=w["mla_w_o"][l].reshape(MLA_H, DV, D),
        rwo=w["rwkv_w_o"][l], cvo=w["conv_w_o"][l], out=w["w_out"][l], up=w["w_up"][l], down=w["w_down"][l],
    )


W_IN_WINDOW_TILE = (0, 10, 21, 31)
W_IN_WINDOW = 1664
W_IN_SHARD = 1384


def w_in_window_cols(win, chip):
    gap = MLA_PAD - MLA_COLS
    branches = []
    for j in range(4):
        lo, hi = W_IN_SHARD * j, W_IN_SHARD * (j + 1)
        base = 128 * W_IN_WINDOW_TILE[j]
        cut = GATE + MLA_COLS
        if hi <= cut:
            branches.append(lambda w, a=lo - base: w[:, a:a + W_IN_SHARD])
        elif lo >= cut:
            branches.append(lambda w, a=lo + gap - base: w[:, a:a + W_IN_SHARD])
        else:
            branches.append(lambda w, a=lo - base, n1=cut - lo, b=cut + gap - base, n2=hi - cut:
                            jnp.concatenate([w[:, a:a + n1], w[:, b:b + n2]], axis=1))
    return lax.switch(chip, branches, win)


def chip_major_grads(g):
    padded = jnp.concatenate([g["gate"], g["mla"], g["rw"], g["cv"]], axis=1)
    w_in = jnp.stack([padded[:, 128 * t:128 * t + W_IN_WINDOW] for t in W_IN_WINDOW_TILE])
    heads = MLA_H // 4
    wq = g["wq"].reshape(4, heads, QL, DQK).transpose(0, 2, 1, 3).reshape(4, QL, heads * DQK)
    wkv = jnp.concatenate([g["wk"][:, :, :NOPE], g["wv"]], axis=-1)
    wkv = wkv.reshape(4, heads, KVL, NOPE + DV).transpose(0, 2, 1, 3).reshape(4, KVL, heads * (NOPE + DV))
    return dict(
        w_in=w_in, mla_wq_b=wq, mla_wkv_b=wkv, mla_w_o=g["wo"].reshape(4, MLA_H * DV, D // 4),
        rwkv_w_o=g["rwo"], conv_w_o=g["cvo"], w_out=g["out"].reshape(4, D // 4, D), w_up=g["up"],
        w_down=g["down"].reshape(4, DFF // 4, D),
    )


def _row(v):
    return v.reshape(1, -1)


def local_step(x, positions, target, w, sm):
    s_len = x.shape[0]
    t_row = _pick(s_len, 256, 8)
    t_wide = _pick(s_len, 128, 8)
    bd, place, rot = _consts()
    cos, sin = _rope_tables(positions)
    sds = lambda *shape: jax.ShapeDtypeStruct(shape, f32)
    saved = []
    v_first = None
    for l in range(DEPTH):
        tag = f"l{l}_"
        lw = derive_layer(w, l)
        vres = l > 0
        p_norm1 = [_row(sm["attn_norm"][l])]
        (h,) = rows_fwd(_fn_norm, [x], p_norm1, [], [sds(s_len, D)], tile=t_row, name=tag + "norm1")
        gate = mm(h, lw["gate"], name=tag + "proj_gate")
        mla = mm(h, lw["mla"], name=tag + "proj_mla")
        rwc = mm(h, lw["rw"], name=tag + "proj_rwkv")
        cvc = mm(h, lw["cv"], name=tag + "proj_conv")
        p_mla = [_row(sm["mla_q_a_norm"][l]), _row(sm["mla_kv_a_norm"][l])]
        qn, kvn, kpe = rows_fwd(_fn_mla_prep, [mla], p_mla, [], [sds(s_len, QL), sds(s_len, KVL), sds(s_len, 128)],
                                tile=t_row, name=tag + "mla_prep")
        q_raw = mm(qn, lw["wq"], b_batched=True, name=tag + "q_proj")
        kn_pad = mm(kvn, lw["wk"], b_batched=True, name=tag + "k_proj")
        vv = mm(kvn, lw["wv"], b_batched=True, name=tag + "v_proj")
        p_qk = [_row(sm["mla_q_norm"][l]), _row(sm["mla_k_norm"][l])]
        q, k = rows_fwd(_fn_qk_post, [q_raw, kn_pad, kpe, cos, sin], p_qk, [place, rot],
                        [sds(MLA_H, s_len, DQK), sds(MLA_H, s_len, DQK)], tile=t_wide, name=tag + "qk_post")
        o = attn_fwd(q, k, vv, tq=_pick(s_len, 256, 8), name=tag + "attn")
        o_a = mm(o, lw["wo"], a_batched=True, b_batched=True, reduce_batch=True, name=tag + "o_a")
        p_rw = [_row(sm["rwkv_mu"][l]), _row(sm["rwkv_w0"][l]), w["rwkv_w2"][l], _row(sm["rwkv_a0"][l]),
                w["rwkv_a2"][l], w["rwkv_g2"][l], _row(sm["rwkv_k_k"][l]), _row(sm["rwkv_k_a"][l])]
        rw_rows, rw_halos = [rwc], (0,)
        if vres:
            p_rw += [w["rwkv_v1"][l - 1], _row(sm["rwkv_v_mu"][l - 1]), _row(sm["rwkv_v0"][l - 1]), w["rwkv_v2"][l - 1]]
            rw_rows, rw_halos = [rwc, h, v_first], (0, 1)
        fn_prep = _make_fn_rwkv_prep(vres)
        r, ld, k2, v, an, bn, g = rows_fwd(fn_prep, rw_rows, p_rw, [bd], [sds(s_len, RW)] * 7, tile=t_row,
                                           name=tag + "rwkv_prep", halos=rw_halos)
        if not vres:
            v_first = v
        y, states = wkv_fwd(r, ld, k2, v, an, bn, name=tag + "wkv")
        p_post = [_row(sm["rwkv_ln_w"][l]), _row(sm["rwkv_ln_b"][l]), _row(sm["rwkv_r_k"][l])]
        (yb,) = rows_fwd(_fn_rwkv_post, [y, r, k2, v, g], p_post, [bd], [sds(s_len, RW)], tile=t_row,
                         name=tag + "rwkv_post")
        o_b = mm(yb, lw["rwo"], name=tag + "o_b")
        p_cv = [w["conv_w"][l][q:q + 1] for q in range(3)]
        (yc,) = rows_fwd(_fn_conv, [cvc], p_cv, [], [sds(s_len, CW)], tile=t_row, name=tag + "conv", halos=(0,))
        o_c = mm(yc, lw["cvo"], name=tag + "o_c")
        (merged,) = rows_fwd(_fn_merge, [gate, o_a, o_b, o_c], [], [], [sds(s_len, D)], tile=t_wide,
                             name=tag + "merge")
        x1 = mm(merged, lw["out"], add=x, name=tag + "out_proj")
        p_norm2 = [_row(sm["mlp_norm"][l])]
        (h2,) = rows_fwd(_fn_norm, [x1], p_norm2, [], [sds(s_len, D)], tile=t_row, name=tag + "norm2")
        up = mm(h2, lw["up"], name=tag + "up")
        (act,) = rows_fwd(_fn_relu2, [up], [], [], [sds(s_len, DFF)], tile=t_wide, name=tag + "relu2")
        x2 = mm(act, lw["down"], add=x1, name=tag + "down")
        saved.append(dict(lw=lw, x=x, h=h, gate=gate, mla=mla, rwc=rwc, cvc=cvc, qn=qn, kvn=kvn, kpe=kpe,
                          q_raw=q_raw, kn_pad=kn_pad, vv=vv, q=q, k=k, o=o, o_a=o_a, r=r, ld=ld, k2=k2, v=v,
                          an=an, bn=bn, g=g, y=y, states=states, yb=yb, o_b=o_b, yc=yc, o_c=o_c, merged=merged,
                          x1=x1, h2=h2, up=up, act=act, p_norm1=p_norm1, p_mla=p_mla, p_qk=p_qk, p_rw=p_rw,
                          p_post=p_post, p_cv=p_cv, p_norm2=p_norm2, rw_rows=rw_rows, rw_halos=rw_halos,
                          fn_prep=fn_prep, v_first=v_first if vres else None))
        x = x2

    loss, dx = loss_head(x, target, tile=t_row, name="loss_head")

    grads = {n: [None] * (DEPTH - 1 if n in ("rwkv_v1", "rwkv_v_mu", "rwkv_v0", "rwkv_v2") else DEPTH)
             for n in WEIGHTS}
    dv_first = None
    for l in reversed(range(DEPTH)):
        tag = f"b{l}_"
        sv = saved[l]
        lw = sv["lw"]
        vres = l > 0
        g_down = mm(sv["act"], dx, ta=True, name=tag + "g_down")
        dup = mm(dx, lw["down"], tb=True, act_grad=sv["up"], name=tag + "d_up")
        g_up = mm(sv["h2"], dup, ta=True, n_split=4, name=tag + "g_up")
        dh2 = mm(dup, lw["up"], tb=True, name=tag + "d_h2")
        (dx1,), (g_n2,) = rows_bwd(_fn_norm, [sv["x1"]], sv["p_norm2"], [], [[dh2]], tile=t_row,
                                   name=tag + "norm2", extra={0: [dx]})
        g_out = mm(sv["merged"], dx1, ta=True, name=tag + "g_out")
        dmerged = mm(dx1, lw["out"], tb=True, name=tag + "d_merged")
        (dgate, do_a, do_b, do_c), _ = rows_bwd(_fn_merge, [sv["gate"], sv["o_a"], sv["o_b"], sv["o_c"]], [], [],
                                                [[dmerged]], tile=t_wide, name=tag + "merge")
        g_cvo = mm(sv["yc"], do_c, ta=True, n_split=4, name=tag + "g_cvo")
        dyc = mm(do_c, lw["cvo"], tb=True, name=tag + "d_yc")
        (dcvc,), g_cw = rows_bwd(_fn_conv, [sv["cvc"]], sv["p_cv"], [], [[dyc]], tile=t_row, name=tag + "conv",
                                    halos=(0,))
        g_rwo = mm(sv["yb"], do_b, ta=True, n_split=4, name=tag + "g_rwo")
        dyb = mm(do_b, lw["rwo"], tb=True, name=tag + "d_yb")
        (dy, dr_p, dk_p, dv_p, dg), g_post = rows_bwd(
            _fn_rwkv_post, [sv["y"], sv["r"], sv["k2"], sv["v"], sv["g"]], sv["p_post"], [bd], [[dyb]], tile=t_row,
            name=tag + "rwkv_post")
        dr_s, dld, dk_s, dv_s, dan, dbn = wkv_bwd(sv["r"], sv["ld"], sv["k2"], sv["v"], sv["an"], sv["bn"],
                                                  sv["states"], dy, name=tag + "wkv")
        dv_list = [dv_s, dv_p] + ([dv_first] if (not vres and dv_first is not None) else [])
        d_prep, g_prep = rows_bwd(
            sv["fn_prep"], sv["rw_rows"], sv["p_rw"], [bd],
            [[dr_s, dr_p], [dld], [dk_s, dk_p], dv_list, [dan], [dbn], [dg]], tile=t_row, name=tag + "rwkv_prep",
            halos=sv["rw_halos"])
        drwc = d_prep[0]
        dh_extra = []
        if vres:
            dh_extra = [d_prep[1]]
            dv_first = d_prep[2]
        g_wo = mm(sv["o"], do_a, ta=True, a_batched=True, n_split=4, name=tag + "g_wo")
        do = mm(do_a, lw["wo"], tb=True, b_batched=True, name=tag + "d_o")
        dq, dk, dvv = attn_bwd(sv["q"], sv["k"], sv["vv"], do, tq=_pick(s_len, 256, 8), name=tag + "attn")
        (dq_raw, dkn_pad, dkpe), g_qk = rows_bwd(
            _fn_qk_post, [sv["q_raw"], sv["kn_pad"], sv["kpe"], cos, sin], sv["p_qk"], [place, rot], [[dq], [dk]],
            tile=t_wide, name=tag + "qk_post", grad_rows=[0, 1, 2])
        g_wq = mm(sv["qn"], dq_raw, ta=True, b_batched=True, name=tag + "g_wq")
        g_wk = mm(sv["kvn"], dkn_pad, ta=True, b_batched=True, name=tag + "g_wk")
        g_wv = mm(sv["kvn"], dvv, ta=True, b_batched=True, name=tag + "g_wv")
        dqn = mm(dq_raw, lw["wq"], tb=True, a_batched=True, b_batched=True, reduce_batch=True, name=tag + "d_qn")
        dkvn = mm(dkn_pad, lw["wk"], tb=True, a_batched=True, b_batched=True, reduce_batch=True, name=tag + "d_kvn_k")
        dkvn = mm(dvv, lw["wv"], tb=True, a_batched=True, b_batched=True, reduce_batch=True, add=dkvn,
                  name=tag + "d_kvn_v")
        (dmla,), g_mla = rows_bwd(_fn_mla_prep, [sv["mla"]], sv["p_mla"], [], [[dqn], [dkvn], [dkpe]], tile=t_row,
                                  name=tag + "mla_prep")
        g_gate = mm(sv["h"], dgate, ta=True, name=tag + "g_gate")
        g_mlaw = mm(sv["h"], dmla, ta=True, name=tag + "g_mla")
        g_rw = mm(sv["h"], drwc, ta=True, name=tag + "g_rw")
        g_cv = mm(sv["h"], dcvc, ta=True, name=tag + "g_cv")
        dh = mm(dgate, lw["gate"], tb=True, name=tag + "d_h_gate")
        dh = mm(dmla, lw["mla"], tb=True, add=dh, name=tag + "d_h_mla")
        dh = mm(drwc, lw["rw"], tb=True, add=dh, name=tag + "d_h_rw")
        dh = mm(dcvc, lw["cv"], tb=True, add=dh, name=tag + "d_h_cv")
        (dx,), (g_n1,) = rows_bwd(_fn_norm, [sv["x"]], sv["p_norm1"], [], [[dh] + dh_extra], tile=t_row,
                                  name=tag + "norm1", extra={0: [dx1]})
        slabs = chip_major_grads(dict(gate=g_gate, mla=g_mlaw, rw=g_rw, cv=g_cv, wq=g_wq, wk=g_wk, wv=g_wv, wo=g_wo,
                                      rwo=g_rwo, cvo=g_cvo, out=g_out, up=g_up, down=g_down))
        for n, val in slabs.items():
            grads[n][l] = val
        grads["attn_norm"][l], grads["mlp_norm"][l] = g_n1, g_n2
        grads["mla_q_a_norm"][l], grads["mla_kv_a_norm"][l] = g_mla
        grads["mla_q_norm"][l], grads["mla_k_norm"][l] = g_qk
        grads["rwkv_ln_w"][l], grads["rwkv_ln_b"][l], grads["rwkv_r_k"][l] = g_post
        for n, val in zip(["rwkv_mu", "rwkv_w0", "rwkv_w2", "rwkv_a0", "rwkv_a2", "rwkv_g2", "rwkv_k_k", "rwkv_k_a"],
                          g_prep[:8]):
            grads[n][l] = val
        if vres:
            for n, val in zip(["rwkv_v1", "rwkv_v_mu", "rwkv_v0", "rwkv_v2"], g_prep[8:12]):
                grads[n][l - 1] = val
        grads["conv_w"][l] = jnp.concatenate(g_cw, axis=0)
    return loss, dx, grads


def _split3(a):
    hi = a.astype(bf16)
    r1 = a - hi.astype(f32)
    mid = r1.astype(bf16)
    lo = (r1 - mid.astype(f32)).astype(bf16)
    return hi, mid, lo


def _shard_axis(name):
    return 1 if name in ROW_SHARDED else 2


def _pack(pieces, width, dtype, row_align):
    flat = jnp.concatenate([p.reshape(-1).astype(dtype) for p in pieces])
    rows = -(-flat.shape[0] // width)
    rows = -(-rows // row_align) * row_align
    return jnp.pad(flat, (0, rows * width - flat.shape[0])).reshape(rows, width)


def _unpack(flat2d, shapes):
    flat = flat2d.reshape(-1)
    out, off = [], 0
    for shp in shapes:
        n = int(np.prod(shp))
        out.append(flat[off:off + n].reshape(shp))
        off += n
    return out


def kernel(x, positions, attn_norm, w_in, mla_q_a_norm, mla_wq_b, mla_kv_a_norm, mla_wkv_b, mla_q_norm, mla_k_norm, mla_w_o, rwkv_mu, rwkv_w0, rwkv_w2, rwkv_a0, rwkv_a2, rwkv_g2, rwkv_k_k, rwkv_k_a, rwkv_r_k, rwkv_ln_w, rwkv_ln_b, rwkv_w_o, rwkv_v1, rwkv_v_mu, rwkv_v0, rwkv_v2, conv_w, conv_w_o, w_out, mlp_norm, w_up, w_down, loss_target, m_attn_norm, m_w_in, m_mla_q_a_norm, m_mla_wq_b, m_mla_kv_a_norm, m_mla_wkv_b, m_mla_q_norm, m_mla_k_norm, m_mla_w_o, m_rwkv_mu, m_rwkv_w0, m_rwkv_w2, m_rwkv_a0, m_rwkv_a2, m_rwkv_g2, m_rwkv_k_k, m_rwkv_k_a, m_rwkv_r_k, m_rwkv_ln_w, m_rwkv_ln_b, m_rwkv_w_o, m_rwkv_v1, m_rwkv_v_mu, m_rwkv_v0, m_rwkv_v2, m_conv_w, m_conv_w_o, m_w_out, m_mlp_norm, m_w_up, m_w_down, v_attn_norm, v_w_in, v_mla_q_a_norm, v_mla_wq_b, v_mla_kv_a_norm, v_mla_wkv_b, v_mla_q_norm, v_mla_k_norm, v_mla_w_o, v_rwkv_mu, v_rwkv_w0, v_rwkv_w2, v_rwkv_a0, v_rwkv_a2, v_rwkv_g2, v_rwkv_k_k, v_rwkv_k_a, v_rwkv_r_k, v_rwkv_ln_w, v_rwkv_ln_b, v_rwkv_w_o, v_rwkv_v1, v_rwkv_v_mu, v_rwkv_v0, v_rwkv_v2, v_conv_w, v_conv_w_o, v_w_out, v_mlp_norm, v_w_up, v_w_down):
    args = dict(locals())
    wts = {n: args[n] for n in WEIGHTS}
    mom = {n: args["m_" + n] for n in WEIGHTS}
    var = {n: args["v_" + n] for n in WEIGHTS}
    chip = 2 * lax.axis_index("x") + lax.axis_index("y")
    core = lax.axis_index("c").astype(jnp.int32).reshape(1)

    med_names = [n for n in MED if n != "conv_w"]
    med_pieces = [wts[n] for n in med_names] + list(_split3(wts["conv_w"]))
    med_shapes = [p.shape for p in med_pieces]
    shards = [wts[n].astype(bf16).reshape(-1, wts[n].shape[-1]) for n in BIG] + [_pack(med_pieces, 128, bf16, 32)]
    got = gather_weights(shards, name="gather_weights")
    whole = {}
    for q, n in enumerate(BIG):
        depth, rows, cols = wts[n].shape
        by_chip = got[q].reshape(4, depth, rows, cols)
        if n in ROW_SHARDED:
            whole[n] = by_chip.transpose(1, 0, 2, 3).reshape(depth, 4 * rows, cols)
        else:
            whole[n] = by_chip.transpose(1, 2, 0, 3).reshape(depth, rows, 4 * cols)
    per_chip = [_unpack(got[len(BIG)][j], med_shapes) for j in range(4)]
    for q, n in enumerate(med_names):
        whole[n] = jnp.concatenate([per_chip[j][q] for j in range(4)], axis=_shard_axis(n)).astype(f32)
    base = len(med_names)
    cw_parts = [jnp.concatenate([per_chip[j][base + t] for j in range(4)], axis=2).astype(f32) for t in range(3)]
    whole["conv_w"] = (cw_parts[0] + cw_parts[1]) + cw_parts[2]
    small = {n: wts[n] for n in SMALL}
    small["rwkv_r_k"] = wts["rwkv_r_k"].reshape(DEPTH, RW)

    loss, grad_x, grads = local_step(x[0], positions[0], loss_target[0], whole, small)
    loss = lax.psum(loss, ("x", "y", "c"))

    sm_names = SMALL + MED
    sm_grads = [jnp.stack(grads[n]) for n in sm_names]
    sm_shapes = [g.shape for g in sm_grads]
    sm_sum = _unpack(all_reduce_small(_pack(sm_grads, 128, f32, 8), name="reduce_small"), sm_shapes)
    gsum = {}
    for n, g in zip(sm_names, sm_sum):
        if n in MED:
            ax = _shard_axis(n)
            width = wts[n].shape[ax]
            g = lax.dynamic_slice_in_dim(g, chip * width, width, axis=ax)
        gsum[n] = g.reshape(wts[n].shape)
    slabs = [grads[n][l] for n in BIG for l in range(DEPTH)]
    labels = [f"{n}_{l}" for n in BIG for l in range(DEPTH)]
    from_sibling = grads_to_sibling(slabs, name="grads_to_sibling")
    chip_sums = [pair_sum(s, t, core, name="pair_sum_" + lb) for s, t, lb in zip(slabs, from_sibling, labels)]
    arrived = chips_all_to_all(chip_sums, name="scatter_grads")
    halves = [sum4(a, name="sum_chips_" + lb) for a, lb in zip(arrived, labels)]
    layout = [(q, l * slabs[q * DEPTH].shape[1], slabs[q * DEPTH].shape[1]) for q in range(len(BIG))
              for l in range(DEPTH)]
    reduced = join_halves(halves, layout, name="join_halves")

    out_g, out_d, out_m, out_v = {}, {}, {}, {}
    for q, n in enumerate(BIG):
        shp = wts[n].shape
        as2d = lambda a: a.reshape(-1, shp[-1])
        g2d = w_in_window_cols(reduced[q], chip) if n == "w_in" else reduced[q]
        res = adamw(as2d(wts[n]), g2d, as2d(mom[n]), as2d(var[n]), name="adamw_" + n)
        out_g[n], out_d[n], out_m[n], out_v[n] = [r.reshape(shp) for r in res]
    sm_all = SMALL + MED
    sm_shapes2 = [wts[n].shape for n in sm_all]
    res = adamw(_pack([wts[n] for n in sm_all], 128, f32, 8), _pack([gsum[n] for n in sm_all], 128, f32, 8),
                _pack([mom[n] for n in sm_all], 128, f32, 8), _pack([var[n] for n in sm_all], 128, f32, 8),
                name="adamw_small")
    for tgt, flat in zip((out_g, out_d, out_m, out_v), res):
        for n, val in zip(sm_all, _unpack(flat, sm_shapes2)):
            tgt[n] = val
    return (loss, grad_x[None], *[out_g[n] for n in WEIGHTS], *[out_d[n] for n in WEIGHTS],
            *[out_m[n] for n in WEIGHTS], *[out_v[n] for n in WEIGHTS])
```

```python
import functools

import jax
import jax.numpy as jnp
import numpy as np
from jax import lax
from jax.experimental import pallas as pl
from jax.experimental.pallas import tpu as pltpu

f32, bf16 = jnp.float32, jnp.bfloat16
HI = lax.Precision.HIGHEST
MESH = pl.DeviceIdType.MESH

D = 1024
DEPTH = 2
MLA_H, NOPE, ROPE, DQK, DV = 8, 64, 32, 96, 64
QL, KVL = 384, 256
RW, RH, RN = 256, 4, 64
DL, AL, GL, MVL = 64, 64, 128, 32
CW = 256
DFF = 4096
GATE = 3 * D
MLA_COLS = QL + KVL + ROPE
MLA_PAD = 768
NORM_EPS = 1e-6
GN_EPS = 64e-5
ROPE_THETA = 10000.0
LR, B1, B2, EPS, WD, STEP = 0.001, 0.9, 0.999, 1e-08, 0.01, 10

VMEM_LIMIT = 52 * 1024 * 1024
WKV_CHUNK = 32
WKV_CHUNKS_PER_STEP = 2

BIG = ["w_in", "mla_wq_b", "mla_wkv_b", "mla_w_o", "rwkv_w_o", "conv_w_o", "w_out", "w_up", "w_down"]
MED = ["rwkv_w2", "rwkv_a2", "rwkv_g2", "rwkv_v1", "rwkv_v2", "conv_w"]
ROW_SHARDED = {"w_out", "w_down", "rwkv_v1"}
SMALL = ["attn_norm", "mla_q_a_norm", "mla_kv_a_norm", "mla_q_norm", "mla_k_norm", "rwkv_mu", "rwkv_w0",
         "rwkv_a0", "rwkv_k_k", "rwkv_k_a", "rwkv_r_k", "rwkv_ln_w", "rwkv_ln_b", "rwkv_v_mu", "rwkv_v0",
         "mlp_norm"]
WEIGHTS = ["attn_norm", "w_in", "mla_q_a_norm", "mla_wq_b", "mla_kv_a_norm", "mla_wkv_b", "mla_q_norm",
           "mla_k_norm", "mla_w_o", "rwkv_mu", "rwkv_w0", "rwkv_w2", "rwkv_a0", "rwkv_a2", "rwkv_g2",
           "rwkv_k_k", "rwkv_k_a", "rwkv_r_k", "rwkv_ln_w", "rwkv_ln_b", "rwkv_w_o", "rwkv_v1", "rwkv_v_mu",
           "rwkv_v0", "rwkv_v2", "conv_w", "conv_w_o", "w_out", "mlp_norm", "w_up", "w_down"]


def _cparams(sem=None):
    return pltpu.CompilerParams(dimension_semantics=sem, vmem_limit_bytes=VMEM_LIMIT)


def _pick(dim, pref, align):
    if dim <= pref:
        return dim
    t = (pref // align) * align
    while t >= align:
        if dim % t == 0:
            return t
        t -= align
    return dim


def _bdot(a, b, dims):
    return lax.dot_general(a.astype(bf16), b.astype(bf16), (dims, ((), ())), preferred_element_type=f32)


@jax.custom_vjp
def _mm(a, b):
    return _bdot(a, b, ((1,), (0,)))


def _mm_fwd(a, b):
    return _mm(a, b), (a, b)


def _mm_bwd(res, g):
    a, b = res
    return _bdot(g, b, ((1,), (1,))), _bdot(a, g, ((0,), (0,)))


_mm.defvjp(_mm_fwd, _mm_bwd)


@jax.custom_vjp
def _mm_nt(a, b):
    return _bdot(a, b, ((1,), (1,)))


def _mm_nt_fwd(a, b):
    return _mm_nt(a, b), (a, b)


def _mm_nt_bwd(res, g):
    a, b = res
    return _bdot(g, b, ((1,), (0,))), _bdot(g, a, ((0,), (0,)))


_mm_nt.defvjp(_mm_nt_fwd, _mm_nt_bwd)


def _hdot(a, b):
    return jnp.dot(a, b, precision=HI, preferred_element_type=f32)


def _hdot_nt(a, b):
    return lax.dot_general(a, b, (((1,), (1,)), ((), ())), precision=HI, preferred_element_type=f32)


def _hdot_tn(a, b):
    return lax.dot_general(a, b, (((0,), (0,)), ((), ())), precision=HI, preferred_element_type=f32)


def _rms(x, g, eps=NORM_EPS):
    return x * lax.rsqrt(jnp.mean(x * x, axis=-1, keepdims=True) + eps) * g


def _sigmoid(x):
    return 1.0 / (1.0 + jnp.exp(-x))


def _softplus(x):
    return jnp.maximum(x, 0.0) + jnp.log(1.0 + jnp.exp(-jnp.maximum(x, -x)))


def _lane_split(x, sizes):
    bounds = np.cumsum([0] + list(sizes))

    @jax.custom_vjp
    def split(v):
        return tuple(v[:, int(bounds[q]):int(bounds[q + 1])] for q in range(len(sizes)))

    split.defvjp(lambda v: (split(v), None), lambda _, g: (jnp.concatenate(g, axis=-1),))
    return split(x)


def _unstack(x):
    @jax.custom_vjp
    def unstack(v):
        return tuple(v[q] for q in range(v.shape[0]))

    unstack.defvjp(lambda v: (unstack(v), None), lambda _, g: (jnp.stack(g, axis=0),))
    return unstack(x)


def _shift_mats(t, k):
    r = lax.broadcasted_iota(jnp.int32, (t, t), 0)
    c = lax.broadcasted_iota(jnp.int32, (t, t), 1)
    inner = (r - c == k).astype(f32)
    r8 = lax.broadcasted_iota(jnp.int32, (t, 8), 0)
    c8 = lax.broadcasted_iota(jnp.int32, (t, 8), 1)
    edge = (c8 - r8 == 8 - k).astype(f32)
    return inner, edge


def _shift(x, halo, k):
    inner, edge = _shift_mats(x.shape[0], k)
    return _hdot(inner, x) + _hdot(edge, halo)


def mm(a, b, *, name, ta=False, tb=False, a_batched=False, b_batched=False, reduce_batch=False, add=None,
       act_grad=None, n_split=1, out_dtype=f32, tm=512, tn=1024, tk=512):
    ash, bsh = a.shape[-2:], b.shape[-2:]
    (k_, m_) = ash if ta else ash[::-1]
    (k2_, n_) = bsh[::-1] if tb else bsh
    assert k_ == k2_, (a.shape, b.shape, ta, tb)
    hb = a.shape[0] if a_batched else (b.shape[0] if b_batched else 1)
    batched_out = (a_batched or b_batched) and not reduce_batch
    h_out = hb if batched_out else 1
    h_red = hb if reduce_batch else 1
    tm = _pick(m_, tm, 128)
    tn = _pick(n_ // n_split, tn, 128)
    tk = _pick(k_, tk, 128)
    nm, nn, nk = m_ // tm, n_ // tn, k_ // tk

    def a_map(ho, i, j, hr, kk):
        blk = (kk, i) if ta else (i, kk)
        return ((ho if batched_out else hr),) + blk if a_batched else blk

    def b_map(ho, i, j, hr, kk):
        blk = (j, kk) if tb else (kk, j)
        return ((ho if batched_out else hr),) + blk if b_batched else blk

    a_blk = (tk, tm) if ta else (tm, tk)
    b_blk = (tn, tk) if tb else (tk, tn)
    in_specs = [pl.BlockSpec(((1,) + a_blk) if a_batched else a_blk, a_map),
                pl.BlockSpec(((1,) + b_blk) if b_batched else b_blk, b_map)]
    args = [a, b]
    for extra in (add, act_grad):
        if extra is not None:
            in_specs.append(pl.BlockSpec((tm, tn), lambda ho, i, j, hr, kk: (i, j)))
            args.append(extra)
    if n_split > 1:
        per = n_ // n_split // tn
        if batched_out:
            out_spec = pl.BlockSpec((1, 1, tm, tn), lambda ho, i, j, hr, kk: (j // per, ho, i, j % per))
            out_shape = jax.ShapeDtypeStruct((n_split, hb, m_, n_ // n_split), out_dtype)
        else:
            out_spec = pl.BlockSpec((1, tm, tn), lambda ho, i, j, hr, kk: (j // per, i, j % per))
            out_shape = jax.ShapeDtypeStruct((n_split, m_, n_ // n_split), out_dtype)
    elif batched_out:
        out_spec = pl.BlockSpec((1, tm, tn), lambda ho, i, j, hr, kk: (ho, i, j))
        out_shape = jax.ShapeDtypeStruct((hb, m_, n_), out_dtype)
    else:
        out_spec = pl.BlockSpec((tm, tn), lambda ho, i, j, hr, kk: (i, j))
        out_shape = jax.ShapeDtypeStruct((m_, n_), out_dtype)
    lead = (0,) * (int(batched_out) + int(n_split > 1))
    dims = ((0,) if ta else (1,), (1,) if tb else (0,))
    has_add, has_act = add is not None, act_grad is not None

    def body(*refs):
        a_ref, b_ref = refs[0], refs[1]
        pos = 2
        add_ref = act_ref = None
        if has_add:
            add_ref = refs[pos]
            pos += 1
        if has_act:
            act_ref = refs[pos]
            pos += 1
        o_ref, acc_ref = refs[pos], refs[pos + 1]
        hr, kk = pl.program_id(3), pl.program_id(4)
        first = jnp.logical_and(hr == 0, kk == 0)
        last = jnp.logical_and(hr == h_red - 1, kk == nk - 1)
        av = a_ref[0] if a_batched else a_ref[...]
        bv = b_ref[0] if b_batched else b_ref[...]
        p = _bdot(av, bv, dims)

        @pl.when(first)
        def _():
            acc_ref[...] = p

        @pl.when(jnp.logical_not(first))
        def _():
            acc_ref[...] += p

        @pl.when(last)
        def _():
            r = acc_ref[...]
            if has_act:
                r = r * (2.0 * jnp.maximum(act_ref[...], 0.0))
            if has_add:
                r = r + add_ref[...]
            if lead:
                o_ref[lead] = r.astype(out_dtype)
            else:
                o_ref[...] = r.astype(out_dtype)

    return pl.pallas_call(
        body, name=name, grid=(h_out, nm, nn, h_red, nk), in_specs=in_specs, out_specs=out_spec,
        out_shape=out_shape, scratch_shapes=[pltpu.VMEM((tm, tn), f32)],
        compiler_params=_cparams(("parallel", "parallel", "parallel", "arbitrary", "arbitrary")),
    )(*args)


def _row_spec(arr, tile, idx):
    if arr.ndim == 2:
        return pl.BlockSpec((tile, arr.shape[1]), lambda i: (idx(i), 0))
    return pl.BlockSpec((arr.shape[0], tile, arr.shape[2]), lambda i: (0, idx(i), 0))


def _halo_spec(arr, tile, idx):
    per = tile // 8
    return pl.BlockSpec((8, arr.shape[1]), lambda i: (jnp.maximum(idx(i) * per - 1, 0), 0))


def _full_spec(arr):
    nd = arr.ndim
    return pl.BlockSpec(arr.shape, lambda i: (0,) * nd)


def rows_fwd(fn, rows, params, consts, out_shapes, *, tile, name, halos=()):
    s_len = rows[0].shape[-2]
    n = s_len // tile
    nr, nh, npar, nc = len(rows), len(halos), len(params), len(consts)
    ident = lambda i: i
    in_specs = ([_row_spec(r, tile, ident) for r in rows] + [_halo_spec(rows[h], tile, ident) for h in halos]
                + [_full_spec(p) for p in params] + [_full_spec(c) for c in consts])
    out_specs = [_row_spec(o, tile, ident) for o in out_shapes]

    def body(*refs):
        i = pl.program_id(0)
        rv = [r[...] for r in refs[:nr]]
        keep = (i > 0).astype(f32)
        hv = [r[...] * keep for r in refs[nr:nr + nh]]
        pv = [r[...] for r in refs[nr + nh:nr + nh + npar]]
        cv = [r[...] for r in refs[nr + nh + npar:nr + nh + npar + nc]]
        outs = fn(rv, hv, pv, cv)
        for o_ref, o in zip(refs[nr + nh + npar + nc:], outs):
            o_ref[...] = o

    return pl.pallas_call(
        body, name=name, grid=(n,), in_specs=in_specs, out_specs=out_specs, out_shape=list(out_shapes),
        compiler_params=_cparams(("arbitrary",)),
    )(*rows, *[rows[h] for h in halos], *params, *consts)


def rows_bwd(fn, rows, params, consts, douts, *, tile, name, halos=(), grad_rows=None, extra=None):
    s_len = rows[0].shape[-2]
    n = s_len // tile
    nr, nh, npar, nc = len(rows), len(halos), len(params), len(consts)
    grad_rows = list(range(nr)) if grad_rows is None else list(grad_rows)
    extra = extra or {}
    assert all(h in grad_rows for h in halos)
    rev = lambda i: n - 1 - i
    dflat = [d for ds in douts for d in ds]
    dcount = [len(ds) for ds in douts]
    eflat = [e for g in grad_rows for e in extra.get(g, [])]
    ecount = [len(extra.get(g, [])) for g in grad_rows]
    in_specs = ([_row_spec(r, tile, rev) for r in rows] + [_halo_spec(rows[h], tile, rev) for h in halos]
                + [_full_spec(p) for p in params] + [_full_spec(c) for c in consts]
                + [_row_spec(d, tile, rev) for d in dflat] + [_row_spec(e, tile, rev) for e in eflat])
    out_shapes = ([jax.ShapeDtypeStruct(rows[g].shape, f32) for g in grad_rows]
                  + [jax.ShapeDtypeStruct(p.shape, f32) for p in params])
    out_specs = [_row_spec(rows[g], tile, rev) for g in grad_rows] + [_full_spec(p) for p in params]
    scratch = [pltpu.VMEM((8, rows[h].shape[1]), f32) for h in halos]
    n_in = nr + nh + npar + nc + len(dflat) + len(eflat)
    n_out = len(grad_rows) + npar

    def body(*refs):
        i = pl.program_id(0)
        rv = [r[...] for r in refs[:nr]]
        keep = (i < n - 1).astype(f32)
        hv = [r[...] * keep for r in refs[nr:nr + nh]]
        pv = [r[...] for r in refs[nr + nh:nr + nh + npar]]
        pos = nr + nh + npar
        cv = [r[...] for r in refs[pos:pos + nc]]
        pos += nc
        dv = []
        for cnt in dcount:
            acc = refs[pos][...]
            for q in range(1, cnt):
                acc = acc + refs[pos + q][...]
            dv.append(acc)
            pos += cnt
        ev = []
        for cnt in ecount:
            ev.append([refs[pos + q][...] for q in range(cnt)])
            pos += cnt
        out_refs = refs[n_in:n_in + n_out]
        carry_refs = refs[n_in + n_out:]

        def f(gr, gh, gp):
            full = list(rv)
            for g, val in zip(grad_rows, gr):
                full[g] = val
            return tuple(fn(full, gh, gp, cv))

        _, vjp = jax.vjp(f, [rv[g] for g in grad_rows], hv, pv)
        d_rows, d_halos, d_params = vjp(tuple(dv))

        @pl.when(i == 0)
        def _():
            for c_ref in carry_refs:
                c_ref[...] = jnp.zeros_like(c_ref)
            for p_ref in out_refs[len(grad_rows):]:
                p_ref[...] = jnp.zeros_like(p_ref)

        for q, g in enumerate(grad_rows):
            val = d_rows[q]
            for e in ev[q]:
                val = val + e
            out_refs[q][...] = val
            if g in halos:
                hq = list(halos).index(g)
                out_refs[q][tile - 8:tile, :] += carry_refs[hq][...]
                carry_refs[hq][...] = d_halos[hq]
        for p_ref, dp in zip(out_refs[len(grad_rows):], d_params):
            p_ref[...] += dp

    res = pl.pallas_call(
        body, name=name, grid=(n,), in_specs=in_specs, out_specs=out_specs, out_shape=out_shapes,
        scratch_shapes=scratch, compiler_params=_cparams(("arbitrary",)),
    )(*rows, *[rows[h] for h in halos], *params, *consts, *dflat, *eflat)
    return list(res[:len(grad_rows)]), list(res[len(grad_rows):])


def _fn_norm(rows, halos, params, consts):
    return (_rms(rows[0], params[0]),)


def _fn_mla_prep(rows, halos, params, consts):
    cq, ckv, kpe = _lane_split(rows[0], (QL, KVL, MLA_PAD - QL - KVL))
    return _rms(cq, params[0]), _rms(ckv, params[1]), kpe


def _rope(x, cos, sin, rot):
    return x * cos + _hdot(x, rot) * sin


def _fn_qk_post(rows, halos, params, consts):
    q_raw, kn_pad, kpe, cos, sin = rows
    q_norm, k_norm = params
    place, rot = consts
    kpe96 = _hdot(kpe, place)
    qs = [_rope(_rms(qh, q_norm), cos, sin, rot) for qh in _unstack(q_raw)]
    ks = [_rope(_rms(kh + kpe96, k_norm), cos, sin, rot) for kh in _unstack(kn_pad)]
    return jnp.stack(qs, axis=0), jnp.stack(ks, axis=0)


def _seg(x, bd):
    return _hdot(x, bd)


def _make_fn_rwkv_prep(vres):
    def fn(rows, halos, params, consts):
        cols = rows[0]
        bd = consts[0]
        mu, w0, w2, a0, a2, g2, k_k, k_a = params[:8]
        prev = _shift(cols, halos[0], 1)
        c = cols + (prev - cols) * mu
        r, k, v, xw, xa, xg = _lane_split(c, (RW, RW, RW, DL, AL, GL))
        log_w = -_softplus(-(w0 + _mm(jnp.tanh(xw), w2))) - 0.5
        ld = -jnp.exp(log_w)
        a = _sigmoid(a0 + _mm(xa, a2))
        g = _mm(_sigmoid(xg), g2)
        if vres:
            hcur, v_first = rows[1], rows[2]
            v1, v_mu, v0, v2 = params[8:12]
            xv = _mm(hcur, v1)
            xv_prev = _shift(xv, _mm(halos[1], v1), 1)
            xv = xv + (xv_prev - xv) * v_mu
            v = v + (v_first - v) * _sigmoid(v0 + _mm(xv, v2))
        kk = k * k_k
        kk = kk / jnp.maximum(jnp.sqrt(_seg(kk * kk, bd)), 1e-12)
        k2 = k * (1.0 + (a - 1.0) * k_a)
        return r, ld, k2, v, -kk, kk * a, g
    return fn


def _fn_rwkv_post(rows, halos, params, consts):
    y, r, k2, v, g = rows
    ln_w, ln_b, r_k = params
    bd = consts[0]
    mean = _seg(y, bd) * (1.0 / RN)
    d = y - mean
    var = _seg(d * d, bd) * (1.0 / RN)
    yn = d * lax.rsqrt(var + GN_EPS) * ln_w + ln_b
    bonus = _seg(r * k2 * r_k, bd) * v
    return ((yn + bonus) * g,)


def _fn_conv(rows, halos, params, consts):
    cols, halo = rows[0], halos[0]
    w0, w1, w2 = params
    b, c, x = _lane_split(cols, (CW, CW, CW))
    _, ch, xh = _lane_split(halo, (CW, CW, CW))
    u, uh = c * x, ch * xh
    return (b * (w0 * _shift(u, uh, 2) + w1 * _shift(u, uh, 1) + w2 * u),)


def _fn_merge(rows, halos, params, consts):
    gate, o_a, o_b, o_c = rows
    g_a, g_b, g_c = _lane_split(gate, (D, D, D))
    return (_sigmoid(g_a) * o_a + _sigmoid(g_b) * o_b + _sigmoid(g_c) * o_c,)


def _fn_relu2(rows, halos, params, consts):
    return (jnp.square(jnp.maximum(rows[0], 0.0)),)


def _attn_block(q, k, v, q0):
    tq, s_len = q.shape[0], k.shape[0]
    s = _mm_nt(q, k) * (DQK ** -0.5)
    row = q0 + lax.broadcasted_iota(jnp.int32, (tq, s_len), 0)
    col = lax.broadcasted_iota(jnp.int32, (tq, s_len), 1)
    s = jnp.where(row >= col, s, -1e30)
    m = lax.stop_gradient(jnp.max(s, axis=-1, keepdims=True))
    e = jnp.exp(s - m)
    p = e / jnp.sum(e, axis=-1, keepdims=True)
    return _mm(p, v)


def attn_fwd(q, k, v, *, tq, name):
    h, s_len, _ = q.shape

    def body(q_ref, k_ref, v_ref, o_ref):
        o_ref[0] = _attn_block(q_ref[0], k_ref[0], v_ref[0], pl.program_id(1) * tq)

    return pl.pallas_call(
        body, name=name, grid=(h, s_len // tq),
        in_specs=[pl.BlockSpec((1, tq, DQK), lambda hh, i: (hh, i, 0)),
                  pl.BlockSpec((1, s_len, DQK), lambda hh, i: (hh, 0, 0)),
                  pl.BlockSpec((1, s_len, DV), lambda hh, i: (hh, 0, 0))],
        out_specs=pl.BlockSpec((1, tq, DV), lambda hh, i: (hh, i, 0)),
        out_shape=jax.ShapeDtypeStruct((h, s_len, DV), f32),
        compiler_params=_cparams(("parallel", "arbitrary")),
    )(q, k, v)


def attn_bwd(q, k, v, do, *, tq, name):
    h, s_len, _ = q.shape

    def body(q_ref, k_ref, v_ref, do_ref, dq_ref, dk_ref, dv_ref):
        i = pl.program_id(1)
        _, vjp = jax.vjp(functools.partial(_attn_block, q0=i * tq), q_ref[0], k_ref[0], v_ref[0])
        dq, dk, dv = vjp(do_ref[0])
        dq_ref[0] = dq

        @pl.when(i == 0)
        def _():
            dk_ref[0] = dk
            dv_ref[0] = dv

        @pl.when(i > 0)
        def _():
            dk_ref[0] += dk
            dv_ref[0] += dv

    return pl.pallas_call(
        body, name=name, grid=(h, s_len // tq),
        in_specs=[pl.BlockSpec((1, tq, DQK), lambda hh, i: (hh, i, 0)),
                  pl.BlockSpec((1, s_len, DQK), lambda hh, i: (hh, 0, 0)),
                  pl.BlockSpec((1, s_len, DV), lambda hh, i: (hh, 0, 0)),
                  pl.BlockSpec((1, tq, DV), lambda hh, i: (hh, i, 0))],
        out_specs=[pl.BlockSpec((1, tq, DQK), lambda hh, i: (hh, i, 0)),
                   pl.BlockSpec((1, s_len, DQK), lambda hh, i: (hh, 0, 0)),
                   pl.BlockSpec((1, s_len, DV), lambda hh, i: (hh, 0, 0))],
        out_shape=[jax.ShapeDtypeStruct((h, s_len, DQK), f32), jax.ShapeDtypeStruct((h, s_len, DQK), f32),
                   jax.ShapeDtypeStruct((h, s_len, DV), f32)],
        compiler_params=_cparams(("parallel", "arbitrary")),
    )(q, k, v, do)


def _wkv_local(r, ld, k, v, a, b):
    c, n = r.shape
    ri = lax.broadcasted_iota(jnp.int32, (c, c), 0)
    ci = lax.broadcasted_iota(jnp.int32, (c, c), 1)
    incl, strict = ri >= ci, ri > ci
    cum = _hdot(incl.astype(f32), ld)
    tot = jnp.sum(ld, axis=0, keepdims=True)
    w_incl, w_excl, w_inv, w_rest = jnp.exp(cum), jnp.exp(cum - ld), jnp.exp(-cum), jnp.exp(tot - cum)
    ab, rb, bb, kb = a * w_excl, r * w_incl, b * w_inv, k * w_inv
    bw, kw = b * w_rest, k * w_rest
    l_ab = jnp.where(strict, _hdot_nt(ab, bb), 0.0)
    l_ak = jnp.where(strict, _hdot_nt(ab, kb), 0.0)
    m_rb = jnp.where(incl, _hdot_nt(rb, bb), 0.0)
    m_rk = jnp.where(incl, _hdot_nt(rb, kb), 0.0)
    x = jnp.concatenate([ab, _hdot(l_ak, v)], axis=-1)
    lp, span = l_ab, 1
    while span < c:
        x = x + _hdot(lp, x)
        span *= 2
        if span < c:
            lp = _hdot(lp, lp)
    a_hat, u0 = _lane_split(x, (n, n))
    r_hat = rb + _hdot(m_rb, a_hat)
    y0 = _hdot(m_rb, u0) + _hdot(m_rk, v)
    eye = lax.broadcasted_iota(jnp.int32, (n, n), 0) == lax.broadcasted_iota(jnp.int32, (n, n), 1)
    g = jnp.where(eye, jnp.exp(tot), 0.0) + _hdot_tn(a_hat, bw)
    z = _hdot_tn(u0, bw) + _hdot_tn(v, kw)
    return r_hat, y0, g, z


def _head(h):
    return slice(RN * h, RN * (h + 1))


def wkv_fwd(r, ld, k, v, a, b, *, name):
    s_len = r.shape[0]
    c, per = WKV_CHUNK, WKV_CHUNKS_PER_STEP
    n = s_len // c
    rows = pl.BlockSpec((c * per, RW), lambda i: (i, 0))
    mats = pl.BlockSpec((per, RH, RN, RN), lambda i: (i, 0, 0, 0))
    rows_t, mats_t = jax.ShapeDtypeStruct((s_len, RW), f32), jax.ShapeDtypeStruct((n, RH, RN, RN), f32)

    def local_body(r_ref, ld_ref, k_ref, v_ref, a_ref, b_ref, rh_ref, y0_ref, g_ref, z_ref):
        for q in range(per):
            tok = slice(c * q, c * (q + 1))
            rh, y0 = [], []
            for h in range(RH):
                sl = _head(h)
                out = _wkv_local(r_ref[tok, sl], ld_ref[tok, sl], k_ref[tok, sl], v_ref[tok, sl], a_ref[tok, sl],
                                 b_ref[tok, sl])
                rh.append(out[0])
                y0.append(out[1])
                g_ref[q, h] = out[2]
                z_ref[q, h] = out[3]
            rh_ref[tok, :] = jnp.concatenate(rh, axis=-1)
            y0_ref[tok, :] = jnp.concatenate(y0, axis=-1)

    r_hat, y0, g, z = pl.pallas_call(
        local_body, name=name + "_local", grid=(n // per,), in_specs=[rows] * 6, out_specs=[rows, rows, mats, mats],
        out_shape=[rows_t, rows_t, mats_t, mats_t], compiler_params=_cparams(("parallel",)),
    )(r, ld, k, v, a, b)

    def scan_body(g_ref, z_ref, st_ref, s_sc):
        s_sc[...] = jnp.zeros_like(s_sc)

        @pl.loop(0, n)
        def _(i):
            for h in range(RH):
                s0 = s_sc[h]
                st_ref[i, h] = s0
                s_sc[h] = _hdot(s0, g_ref[i, h]) + z_ref[i, h]

    vm = pl.BlockSpec(memory_space=pltpu.VMEM)
    states = pl.pallas_call(
        scan_body, name=name + "_scan", in_specs=[vm, vm], out_specs=vm, out_shape=mats_t,
        scratch_shapes=[pltpu.VMEM((RH, RN, RN), f32)],
        compiler_params=pltpu.CompilerParams(vmem_limit_bytes=VMEM_LIMIT),
    )(g, z)

    def out_body(rh_ref, y0_ref, st_ref, y_ref):
        for q in range(per):
            tok = slice(c * q, c * (q + 1))
            ys = [_hdot_nt(rh_ref[tok, _head(h)], st_ref[q, h]) for h in range(RH)]
            y_ref[tok, :] = jnp.concatenate(ys, axis=-1) + y0_ref[tok, :]

    y = pl.pallas_call(
        out_body, name=name + "_out", grid=(n // per,), in_specs=[rows, rows, mats], out_specs=rows,
        out_shape=rows_t, compiler_params=_cparams(("parallel",)),
    )(r_hat, y0, states)
    return y, dict(r_hat=r_hat, g=g, states=states)


def wkv_bwd(r, ld, k, v, a, b, saved, dy, *, name):
    s_len = r.shape[0]
    c, per = WKV_CHUNK, WKV_CHUNKS_PER_STEP
    n = s_len // c
    rows = pl.BlockSpec((c * per, RW), lambda i: (i, 0))
    mats = pl.BlockSpec((per, RH, RN, RN), lambda i: (i, 0, 0, 0))
    rows_t, mats_t = jax.ShapeDtypeStruct((s_len, RW), f32), jax.ShapeDtypeStruct((n, RH, RN, RN), f32)

    def out_body(dy_ref, rh_ref, st_ref, drh_ref, dsy_ref):
        for q in range(per):
            tok = slice(c * q, c * (q + 1))
            drh = []
            for h in range(RH):
                dyh = dy_ref[tok, _head(h)]
                drh.append(_hdot(dyh, st_ref[q, h]))
                dsy_ref[q, h] = _hdot_tn(dyh, rh_ref[tok, _head(h)])
            drh_ref[tok, :] = jnp.concatenate(drh, axis=-1)

    d_rhat, ds_y = pl.pallas_call(
        out_body, name=name + "_out", grid=(n // per,), in_specs=[rows, rows, mats], out_specs=[rows, mats],
        out_shape=[rows_t, mats_t], compiler_params=_cparams(("parallel",)),
    )(dy, saved["r_hat"], saved["states"])

    def scan_body(dsy_ref, g_ref, st_ref, dg_ref, dz_ref, ds_sc):
        ds_sc[...] = jnp.zeros_like(ds_sc)

        @pl.loop(0, n)
        def _(i):
            cidx = n - 1 - i
            for h in range(RH):
                ds_next = ds_sc[h]
                dz_ref[cidx, h] = ds_next
                dg_ref[cidx, h] = _hdot_tn(st_ref[cidx, h], ds_next)
                ds_sc[h] = dsy_ref[cidx, h] + _hdot_nt(ds_next, g_ref[cidx, h])

    vm = pl.BlockSpec(memory_space=pltpu.VMEM)
    d_g, d_z = pl.pallas_call(
        scan_body, name=name + "_scan", in_specs=[vm, vm, vm], out_specs=[vm, vm], out_shape=[mats_t, mats_t],
        scratch_shapes=[pltpu.VMEM((RH, RN, RN), f32)],
        compiler_params=pltpu.CompilerParams(vmem_limit_bytes=VMEM_LIMIT),
    )(ds_y, saved["g"], saved["states"])

    def local_body(r_ref, ld_ref, k_ref, v_ref, a_ref, b_ref, drh_ref, dy_ref, dg_ref, dz_ref, *out_refs):
        for q in range(per):
            tok = slice(c * q, c * (q + 1))
            outs = [[] for _ in range(6)]
            for h in range(RH):
                sl = _head(h)
                _, vjp = jax.vjp(_wkv_local, r_ref[tok, sl], ld_ref[tok, sl], k_ref[tok, sl], v_ref[tok, sl],
                                 a_ref[tok, sl], b_ref[tok, sl])
                grads = vjp((drh_ref[tok, sl], dy_ref[tok, sl], dg_ref[q, h], dz_ref[q, h]))
                for j in range(6):
                    outs[j].append(grads[j])
            for j, o_ref in enumerate(out_refs):
                o_ref[tok, :] = jnp.concatenate(outs[j], axis=-1)

    return pl.pallas_call(
        local_body, name=name + "_local", grid=(n // per,), in_specs=[rows] * 8 + [mats, mats], out_specs=[rows] * 6,
        out_shape=[rows_t] * 6, compiler_params=_cparams(("parallel",)),
    )(r, ld, k, v, a, b, d_rhat, dy, d_g, d_z)


def loss_head(y, target, *, tile, name):
    s_len, d = y.shape
    n = s_len // tile

    def body(y_ref, t_ref, dy_ref, l_ref):
        err = y_ref[...] - t_ref[...]
        dy_ref[...] = err * (1.0 / d)
        part = 0.5 * jnp.sum(jnp.mean(err * err, axis=-1, keepdims=True), axis=0, keepdims=True)

        @pl.when(pl.program_id(0) == 0)
        def _():
            l_ref[...] = jnp.zeros_like(l_ref)

        l_ref[...] += jnp.broadcast_to(part, l_ref.shape)

    bs = pl.BlockSpec((tile, d), lambda i: (i, 0))
    dy, l = pl.pallas_call(
        body, name=name, grid=(n,), in_specs=[bs, bs],
        out_specs=[bs, pl.BlockSpec((8, 128), lambda i: (0, 0))],
        out_shape=[jax.ShapeDtypeStruct((s_len, d), f32), jax.ShapeDtypeStruct((8, 128), f32)],
        compiler_params=_cparams(("arbitrary",)),
    )(y, target)
    return l[0, 0], dy


def adamw(w, g, m, v, *, name):
    gs = g if isinstance(g, (list, tuple)) else [g]
    rows, cols = w.shape
    tile = _pick(rows, max(8, (2 * 1024 * 1024 // (4 * cols)) // 8 * 8), 8)
    c1 = 1.0 - B1 ** STEP
    c2 = 1.0 - B2 ** STEP
    ng = len(gs)

    def body(*refs):
        w_ref, m_ref, v_ref = refs[0], refs[1 + ng], refs[2 + ng]
        g_out, d_out, m_out, v_out = refs[3 + ng:]
        gv = refs[1][...]
        for q in range(1, ng):
            gv = gv + refs[1 + q][...]
        mn = B1 * m_ref[...] + (1.0 - B1) * gv
        vn = B2 * v_ref[...] + (1.0 - B2) * (gv * gv)
        d_out[...] = -LR * ((mn / c1) / (jnp.sqrt(vn / c2) + EPS) + WD * w_ref[...])
        g_out[...] = gv
        m_out[...] = mn
        v_out[...] = vn

    bs = pl.BlockSpec((tile, cols), lambda i: (i, 0))
    return pl.pallas_call(
        body, name=name, grid=(rows // tile,), in_specs=[bs] * (3 + ng), out_specs=[bs] * 4,
        out_shape=[jax.ShapeDtypeStruct((rows, cols), f32)] * 4, compiler_params=_cparams(("parallel",)),
    )(w, *gs, m, v)


def _place():
    return lax.axis_index("x"), lax.axis_index("y"), lax.axis_index("c")


_ANY = pl.BlockSpec(memory_space=pl.ANY)


def _peer_chips(x, y):
    return [(1 - x, y), (x, 1 - y), (1 - x, 1 - y)]


def gather_weights(shards, *, name):
    nk = len(shards)

    def body(*refs):
        srcs, outs = refs[:nk], refs[nk:2 * nk]
        ici_send, ici_recv, d2d_send, d2d_recv = refs[2 * nk:]
        x, y, c = _place()
        me = 2 * x + y
        peers = _peer_chips(x, y)
        pending = []
        for k in range(nk):
            half = srcs[k].shape[0] // 2
            mine = pl.ds(c * half, half)
            for p, (px, py) in enumerate(peers):
                cp = pltpu.make_async_remote_copy(
                    src_ref=srcs[k].at[mine], dst_ref=outs[k].at[me, mine], send_sem=ici_send.at[k, p],
                    recv_sem=ici_recv.at[k, p], device_id=(px, py, c), device_id_type=MESH)
                cp.start()
                pending.append(cp)
        for k in range(nk):
            half = srcs[k].shape[0] // 2
            mine = pl.ds(c * half, half)
            for p, (px, py) in enumerate(peers):
                landed = outs[k].at[2 * px + py, mine]
                pltpu.make_async_remote_copy(
                    src_ref=srcs[k].at[mine], dst_ref=landed, send_sem=ici_send.at[k, p], recv_sem=ici_recv.at[k, p],
                    device_id=(px, py, c), device_id_type=MESH).wait_recv()
                fwd = pltpu.make_async_remote_copy(
                    src_ref=landed, dst_ref=landed, send_sem=d2d_send.at[k, p], recv_sem=d2d_recv.at[k, p],
                    device_id=(x, y, 1 - c), device_id_type=MESH)
                fwd.start()
                pending.append(fwd)
        for k in range(nk):
            half = srcs[k].shape[0] // 2
            other = pl.ds((1 - c) * half, half)
            for p, (px, py) in enumerate(peers):
                theirs = outs[k].at[2 * px + py, other]
                pltpu.make_async_remote_copy(
                    src_ref=theirs, dst_ref=theirs, send_sem=d2d_send.at[k, p], recv_sem=d2d_recv.at[k, p],
                    device_id=(x, y, 1 - c), device_id_type=MESH).wait_recv()
        for cp in pending:
            cp.wait_send()

    sem = lambda *shape: pltpu.SemaphoreType.DMA(shape)
    return pl.pallas_call(
        body, name=name, in_specs=[_ANY] * nk, out_specs=[_ANY] * nk,
        out_shape=[jax.ShapeDtypeStruct((4,) + s.shape, s.dtype) for s in shards],
        scratch_shapes=[sem(nk, 3), sem(nk, 3), sem(nk, 3), sem(nk, 3)],
    )(*shards)


def grads_to_sibling(parts, *, name):
    nk = len(parts)

    def body(*refs):
        srcs, outs = refs[:nk], refs[nk:2 * nk]
        send_sems, recv_sems = refs[2 * nk:]
        x, y, c = _place()
        sends = []
        for k in range(nk):
            half = srcs[k].shape[1] // 2
            cp = pltpu.make_async_remote_copy(
                src_ref=srcs[k].at[:, pl.ds((1 - c) * half, half), :], dst_ref=outs[k], send_sem=send_sems.at[k],
                recv_sem=recv_sems.at[k], device_id=(x, y, 1 - c), device_id_type=MESH)
            cp.start()
            sends.append(cp)
        for cp in sends:
            cp.wait_recv()
        for cp in sends:
            cp.wait_send()

    return pl.pallas_call(
        body, name=name, in_specs=[_ANY] * nk, out_specs=[_ANY] * nk,
        out_shape=[jax.ShapeDtypeStruct((4, p.shape[1] // 2, p.shape[2]), p.dtype) for p in parts],
        scratch_shapes=[pltpu.SemaphoreType.DMA((nk,)), pltpu.SemaphoreType.DMA((nk,))],
    )(*parts)


def pair_sum(part, theirs, core, *, name):
    _, rows, cols = part.shape
    half = rows // 2
    tile = _pick(half, max(16, (1 << 20) // (4 * cols) // 16 * 16), 16)
    per = half // tile

    def body(c_ref, p_ref, t_ref, o_ref):
        o_ref[...] = (p_ref[...] + t_ref[...]).astype(bf16)

    grid_spec = pltpu.PrefetchScalarGridSpec(
        num_scalar_prefetch=1, grid=(4, per),
        in_specs=[pl.BlockSpec((1, tile, cols), lambda j, i, c_ref: (j, c_ref[0] * per + i, 0)),
                  pl.BlockSpec((1, tile, cols), lambda j, i, c_ref: (j, i, 0))],
        out_specs=pl.BlockSpec((1, tile, cols), lambda j, i, c_ref: (j, i, 0)))
    return pl.pallas_call(
        body, name=name, grid_spec=grid_spec, out_shape=jax.ShapeDtypeStruct((4, half, cols), bf16),
        compiler_params=_cparams(("parallel", "parallel")),
    )(core, part, theirs)


def chips_all_to_all(parts, *, name):
    nk = len(parts)

    def body(*refs):
        srcs, outs = refs[:nk], refs[nk:2 * nk]
        send_sems, recv_sems, local_sems = refs[2 * nk:]
        x, y, c = _place()
        me = 2 * x + y
        peers = _peer_chips(x, y)
        sends, locals_ = [], []
        for k in range(nk):
            local = pltpu.make_async_copy(srcs[k].at[me], outs[k].at[me], local_sems.at[k])
            local.start()
            locals_.append(local)
            for p, (px, py) in enumerate(peers):
                cp = pltpu.make_async_remote_copy(
                    src_ref=srcs[k].at[2 * px + py], dst_ref=outs[k].at[me], send_sem=send_sems.at[k, p],
                    recv_sem=recv_sems.at[k, p], device_id=(px, py, c), device_id_type=MESH)
                cp.start()
                sends.append(cp)
        for k in range(nk):
            for p, (px, py) in enumerate(peers):
                pltpu.make_async_remote_copy(
                    src_ref=srcs[k].at[me], dst_ref=outs[k].at[2 * px + py], send_sem=send_sems.at[k, p],
                    recv_sem=recv_sems.at[k, p], device_id=(px, py, c), device_id_type=MESH).wait_recv()
        for cp in sends:
            cp.wait_send()
        for cp in locals_:
            cp.wait()

    sem = lambda *shape: pltpu.SemaphoreType.DMA(shape)
    return pl.pallas_call(
        body, name=name, in_specs=[_ANY] * nk, out_specs=[_ANY] * nk,
        out_shape=[jax.ShapeDtypeStruct(p.shape, p.dtype) for p in parts],
        scratch_shapes=[sem(nk, 3), sem(nk, 3), sem(nk)],
    )(*parts)


def join_halves(bufs, layout, *, name):
    nk, nb = len(layout), len(bufs)

    def body(*refs):
        outs = refs[nb:2 * nb]
        send_sems, recv_sems = refs[2 * nb:]
        x, y, c = _place()
        pending = []
        for k, (o, off, rows) in enumerate(layout):
            half = rows // 2
            mine = outs[o].at[pl.ds(off + c * half, half), :]
            cp = pltpu.make_async_remote_copy(
                src_ref=mine, dst_ref=mine, send_sem=send_sems.at[k], recv_sem=recv_sems.at[k],
                device_id=(x, y, 1 - c), device_id_type=MESH)
            cp.start()
            pending.append(cp)
        for k, (o, off, rows) in enumerate(layout):
            half = rows // 2
            theirs = outs[o].at[pl.ds(off + (1 - c) * half, half), :]
            pltpu.make_async_remote_copy(
                src_ref=theirs, dst_ref=theirs, send_sem=send_sems.at[k], recv_sem=recv_sems.at[k],
                device_id=(x, y, 1 - c), device_id_type=MESH).wait_recv()
        for cp in pending:
            cp.wait_send()

    return pl.pallas_call(
        body, name=name, in_specs=[_ANY] * nb, out_specs=[_ANY] * nb,
        out_shape=[jax.ShapeDtypeStruct(b.shape, b.dtype) for b in bufs],
        input_output_aliases={o: o for o in range(nb)},
        scratch_shapes=[pltpu.SemaphoreType.DMA((nk,)), pltpu.SemaphoreType.DMA((nk,))],
    )(*bufs)


def place_slab(dest, src, index, *, name):
    rows, cols = src.shape
    tile = _pick(rows, max(16, (1 << 20) // (src.dtype.itemsize * cols) // 16 * 16), 16)

    def body(i_ref, s_ref, d_ref, o_ref):
        del i_ref, d_ref
        o_ref[0] = s_ref[...]

    grid_spec = pltpu.PrefetchScalarGridSpec(
        num_scalar_prefetch=1, grid=(rows // tile,),
        in_specs=[pl.BlockSpec((tile, cols), lambda i, idx: (i, 0)), _ANY],
        out_specs=pl.BlockSpec((1, tile, cols), lambda i, idx: (idx[0], i, 0)))
    return pl.pallas_call(
        body, name=name, grid_spec=grid_spec, out_shape=jax.ShapeDtypeStruct(dest.shape, dest.dtype),
        input_output_aliases={2: 0}, compiler_params=_cparams(("parallel",)),
    )(index, src, dest)


def all_reduce_small(src, *, name):
    rows, cols = src.shape

    def body(s_ref, o_ref, buf, send_sems, recv_sems):
        x, y, c = _place()
        me = 4 * x + 2 * y + c
        buf[me] = s_ref[...]
        sends = []
        for msk in range(1, 8):
            px = x ^ (msk >> 2)
            py = y ^ ((msk >> 1) & 1)
            pc = c ^ (msk & 1)
            cp = pltpu.make_async_remote_copy(
                src_ref=s_ref, dst_ref=buf.at[me], send_sem=send_sems.at[msk - 1], recv_sem=recv_sems.at[msk - 1],
                device_id=(px, py, pc), device_id_type=MESH)
            cp.start()
            sends.append(cp)
        for msk in range(1, 8):
            px = x ^ (msk >> 2)
            py = y ^ ((msk >> 1) & 1)
            pc = c ^ (msk & 1)
            pltpu.make_async_remote_copy(
                src_ref=s_ref, dst_ref=buf.at[4 * px + 2 * py + pc], send_sem=send_sems.at[msk - 1],
                recv_sem=recv_sems.at[msk - 1], device_id=(px, py, pc), device_id_type=MESH).wait_recv()
        for cp in sends:
            cp.wait_send()
        acc = buf[0]
        for d in range(1, 8):
            acc = acc + buf[d]
        o_ref[...] = acc

    vm = pl.BlockSpec(memory_space=pltpu.VMEM)
    return pl.pallas_call(
        body, name=name, in_specs=[vm], out_specs=vm, out_shape=jax.ShapeDtypeStruct((rows, cols), f32),
        scratch_shapes=[pltpu.VMEM((8, rows, cols), f32), pltpu.SemaphoreType.DMA((7,)),
                        pltpu.SemaphoreType.DMA((7,))],
        compiler_params=pltpu.CompilerParams(vmem_limit_bytes=VMEM_LIMIT),
    )(src)


def sum4_into(parts, dest, core, *, layer, total_rows, name):
    _, rows, cols = parts.shape
    tile = _pick(rows, max(16, (1 << 20) // (4 * cols) // 16 * 16), 16)
    per = rows // tile

    def body(c_ref, p_ref, *rest):
        del c_ref
        p = [p_ref[j].astype(f32) for j in range(4)]
        rest[-1][...] = ((p[0] + p[1]) + p[2]) + p[3]

    grid_spec = pltpu.PrefetchScalarGridSpec(
        num_scalar_prefetch=1, grid=(per,),
        in_specs=[pl.BlockSpec((4, tile, cols), lambda i, c_ref: (0, i, 0))] + ([] if dest is None else [_ANY]),
        out_specs=pl.BlockSpec((tile, cols), lambda i, c_ref: ((2 * layer + c_ref[0]) * per + i, 0)))
    return pl.pallas_call(
        body, name=name, grid_spec=grid_spec, out_shape=jax.ShapeDtypeStruct((total_rows, cols), f32),
        input_output_aliases={} if dest is None else {2: 0}, compiler_params=_cparams(("parallel",)),
    )(core, parts, *([] if dest is None else [dest]))


def _consts():
    idx = np.arange(RW)
    bd = (idx[:, None] // RN == idx[None, :] // RN).astype(np.float32)
    place = np.zeros((128, DQK), np.float32)
    place[np.arange(ROPE), NOPE + np.arange(ROPE)] = 1.0
    rot = np.zeros((DQK, DQK), np.float32)
    half = ROPE // 2
    rot[NOPE + half + np.arange(half), NOPE + np.arange(half)] = -1.0
    rot[NOPE + np.arange(half), NOPE + half + np.arange(half)] = 1.0
    return jnp.asarray(bd), jnp.asarray(place), jnp.asarray(rot)


def _rope_tables(positions):
    freqs = ROPE_THETA ** (-(jnp.arange(ROPE // 2, dtype=f32) * 2.0 / ROPE))
    ang = positions.astype(f32)[:, None] * freqs
    cos, sin = jnp.cos(ang), jnp.sin(ang)
    ones = jnp.ones((positions.shape[0], NOPE), f32)
    return (jnp.concatenate([ones, cos, cos], axis=-1), jnp.concatenate([0.0 * ones, sin, sin], axis=-1))


def derive_layer(w, l):
    w_in = w["w_in"][l]
    pad = jnp.zeros((D, MLA_PAD - MLA_COLS), w_in.dtype)
    wkv = w["mla_wkv_b"][l].reshape(KVL, MLA_H, NOPE + DV)
    wk = jnp.concatenate([wkv[:, :, :NOPE], jnp.zeros((KVL, MLA_H, ROPE), wkv.dtype)], axis=-1)
    return dict(
        gate=w_in[:, :GATE], mla=jnp.concatenate([w_in[:, GATE:GATE + MLA_COLS], pad], axis=1),
        rw=w_in[:, GATE + MLA_COLS:GATE + MLA_COLS + 4 * RW], cv=w_in[:, GATE + MLA_COLS + 4 * RW:],
        wq=w["mla_wq_b"][l].reshape(QL, MLA_H, DQK).transpose(1, 0, 2),
        wk=wk.transpose(1, 0, 2), wv=wkv[:, :, NOPE:].transpose(1, 0, 2),
        wo=w["mla_w_o"][l].reshape(MLA_H, DV, D),
        rwo=w["rwkv_w_o"][l], cvo=w["conv_w_o"][l], out=w["w_out"][l], up=w["w_up"][l], down=w["w_down"][l],
    )


W_IN_WINDOW_TILE = (0, 10, 21, 31)
W_IN_WINDOW = 1664
W_IN_SHARD = 1384


def w_in_window_cols(win, chip):
    gap = MLA_PAD - MLA_COLS
    branches = []
    for j in range(4):
        lo, hi = W_IN_SHARD * j, W_IN_SHARD * (j + 1)
        base = 128 * W_IN_WINDOW_TILE[j]
        cut = GATE + MLA_COLS
        if hi <= cut:
            branches.append(lambda w, a=lo - base: w[:, a:a + W_IN_SHARD])
        elif lo >= cut:
            branches.append(lambda w, a=lo + gap - base: w[:, a:a + W_IN_SHARD])
        else:
            branches.append(lambda w, a=lo - base, n1=cut - lo, b=cut + gap - base, n2=hi - cut:
                            jnp.concatenate([w[:, a:a + n1], w[:, b:b + n2]], axis=1))
    return lax.switch(chip, branches, win)


def chip_major_grads(g):
    padded = jnp.concatenate([g["gate"], g["mla"], g["rw"], g["cv"]], axis=1)
    w_in = jnp.stack([padded[:, 128 * t:128 * t + W_IN_WINDOW] for t in W_IN_WINDOW_TILE])
    heads = MLA_H // 4
    wq = g["wq"].reshape(4, heads, QL, DQK).transpose(0, 2, 1, 3).reshape(4, QL, heads * DQK)
    wkv = jnp.concatenate([g["wk"][:, :, :NOPE], g["wv"]], axis=-1)
    wkv = wkv.reshape(4, heads, KVL, NOPE + DV).transpose(0, 2, 1, 3).reshape(4, KVL, heads * (NOPE + DV))
    return dict(
        w_in=w_in, mla_wq_b=wq, mla_wkv_b=wkv, mla_w_o=g["wo"].reshape(4, MLA_H * DV, D // 4),
        rwkv_w_o=g["rwo"], conv_w_o=g["cvo"], w_out=g["out"].reshape(4, D // 4, D), w_up=g["up"],
        w_down=g["down"].reshape(4, DFF // 4, D),
    )


def _row(v):
    return v.reshape(1, -1)


def local_step(x, positions, target, w, sm):
    s_len = x.shape[0]
    t_row = _pick(s_len, 256, 8)
    t_wide = _pick(s_len, 128, 8)
    bd, place, rot = _consts()
    cos, sin = _rope_tables(positions)
    sds = lambda *shape: jax.ShapeDtypeStruct(shape, f32)
    saved = []
    v_first = None
    for l in range(DEPTH):
        tag = f"l{l}_"
        lw = derive_layer(w, l)
        vres = l > 0
        p_norm1 = [_row(sm["attn_norm"][l])]
        (h,) = rows_fwd(_fn_norm, [x], p_norm1, [], [sds(s_len, D)], tile=t_row, name=tag + "norm1")
        gate = mm(h, lw["gate"], name=tag + "proj_gate")
        mla = mm(h, lw["mla"], name=tag + "proj_mla")
        rwc = mm(h, lw["rw"], name=tag + "proj_rwkv")
        cvc = mm(h, lw["cv"], name=tag + "proj_conv")
        p_mla = [_row(sm["mla_q_a_norm"][l]), _row(sm["mla_kv_a_norm"][l])]
        qn, kvn, kpe = rows_fwd(_fn_mla_prep, [mla], p_mla, [], [sds(s_len, QL), sds(s_len, KVL), sds(s_len, 128)],
                                tile=t_row, name=tag + "mla_prep")
        q_raw = mm(qn, lw["wq"], b_batched=True, name=tag + "q_proj")
        kn_pad = mm(kvn, lw["wk"], b_batched=True, name=tag + "k_proj")
        vv = mm(kvn, lw["wv"], b_batched=True, name=tag + "v_proj")
        p_qk = [_row(sm["mla_q_norm"][l]), _row(sm["mla_k_norm"][l])]
        q, k = rows_fwd(_fn_qk_post, [q_raw, kn_pad, kpe, cos, sin], p_qk, [place, rot],
                        [sds(MLA_H, s_len, DQK), sds(MLA_H, s_len, DQK)], tile=t_wide, name=tag + "qk_post")
        o = attn_fwd(q, k, vv, tq=_pick(s_len, 256, 8), name=tag + "attn")
        o_a = mm(o, lw["wo"], a_batched=True, b_batched=True, reduce_batch=True, name=tag + "o_a")
        p_rw = [_row(sm["rwkv_mu"][l]), _row(sm["rwkv_w0"][l]), w["rwkv_w2"][l], _row(sm["rwkv_a0"][l]),
                w["rwkv_a2"][l], w["rwkv_g2"][l], _row(sm["rwkv_k_k"][l]), _row(sm["rwkv_k_a"][l])]
        rw_rows, rw_halos = [rwc], (0,)
        if vres:
            p_rw += [w["rwkv_v1"][l - 1], _row(sm["rwkv_v_mu"][l - 1]), _row(sm["rwkv_v0"][l - 1]), w["rwkv_v2"][l - 1]]
            rw_rows, rw_halos = [rwc, h, v_first], (0, 1)
        fn_prep = _make_fn_rwkv_prep(vres)
        r, ld, k2, v, an, bn, g = rows_fwd(fn_prep, rw_rows, p_rw, [bd], [sds(s_len, RW)] * 7, tile=t_row,
                                           name=tag + "rwkv_prep", halos=rw_halos)
        if not vres:
            v_first = v
        y, states = wkv_fwd(r, ld, k2, v, an, bn, name=tag + "wkv")
        p_post = [_row(sm["rwkv_ln_w"][l]), _row(sm["rwkv_ln_b"][l]), _row(sm["rwkv_r_k"][l])]
        (yb,) = rows_fwd(_fn_rwkv_post, [y, r, k2, v, g], p_post, [bd], [sds(s_len, RW)], tile=t_row,
                         name=tag + "rwkv_post")
        o_b = mm(yb, lw["rwo"], name=tag + "o_b")
        p_cv = [w["conv_w"][l][q:q + 1] for q in range(3)]
        (yc,) = rows_fwd(_fn_conv, [cvc], p_cv, [], [sds(s_len, CW)], tile=t_row, name=tag + "conv", halos=(0,))
        o_c = mm(yc, lw["cvo"], name=tag + "o_c")
        (merged,) = rows_fwd(_fn_merge, [gate, o_a, o_b, o_c], [], [], [sds(s_len, D)], tile=t_wide,
                             name=tag + "merge")
        x1 = mm(merged, lw["out"], add=x, name=tag + "out_proj")
        p_norm2 = [_row(sm["mlp_norm"][l])]
        (h2,) = rows_fwd(_fn_norm, [x1], p_norm2, [], [sds(s_len, D)], tile=t_row, name=tag + "norm2")
        up = mm(h2, lw["up"], name=tag + "up")
        (act,) = rows_fwd(_fn_relu2, [up], [], [], [sds(s_len, DFF)], tile=t_wide, name=tag + "relu2")
        x2 = mm(act, lw["down"], add=x1, name=tag + "down")
        saved.append(dict(lw=lw, x=x, h=h, gate=gate, mla=mla, rwc=rwc, cvc=cvc, qn=qn, kvn=kvn, kpe=kpe,
                          q_raw=q_raw, kn_pad=kn_pad, vv=vv, q=q, k=k, o=o, o_a=o_a, r=r, ld=ld, k2=k2, v=v,
                          an=an, bn=bn, g=g, y=y, states=states, yb=yb, o_b=o_b, yc=yc, o_c=o_c, merged=merged,
                          x1=x1, h2=h2, up=up, act=act, p_norm1=p_norm1, p_mla=p_mla, p_qk=p_qk, p_rw=p_rw,
                          p_post=p_post, p_cv=p_cv, p_norm2=p_norm2, rw_rows=rw_rows, rw_halos=rw_halos,
                          fn_prep=fn_prep, v_first=v_first if vres else None))
        x = x2

    loss, dx = loss_head(x, target, tile=t_row, name="loss_head")

    grads = {n: [None] * (DEPTH - 1 if n in ("rwkv_v1", "rwkv_v_mu", "rwkv_v0", "rwkv_v2") else DEPTH)
             for n in WEIGHTS}
    dv_first = None
    for l in reversed(range(DEPTH)):
        tag = f"b{l}_"
        sv = saved[l]
        lw = sv["lw"]
        vres = l > 0
        g_down = mm(sv["act"], dx, ta=True, name=tag + "g_down")
        dup = mm(dx, lw["down"], tb=True, act_grad=sv["up"], name=tag + "d_up")
        g_up = mm(sv["h2"], dup, ta=True, n_split=4, name=tag + "g_up")
        dh2 = mm(dup, lw["up"], tb=True, name=tag + "d_h2")
        (dx1,), (g_n2,) = rows_bwd(_fn_norm, [sv["x1"]], sv["p_norm2"], [], [[dh2]], tile=t_row,
                                   name=tag + "norm2", extra={0: [dx]})
        g_out = mm(sv["merged"], dx1, ta=True, name=tag + "g_out")
        dmerged = mm(dx1, lw["out"], tb=True, name=tag + "d_merged")
        (dgate, do_a, do_b, do_c), _ = rows_bwd(_fn_merge, [sv["gate"], sv["o_a"], sv["o_b"], sv["o_c"]], [], [],
                                                [[dmerged]], tile=t_wide, name=tag + "merge")
        g_cvo = mm(sv["yc"], do_c, ta=True, n_split=4, name=tag + "g_cvo")
        dyc = mm(do_c, lw["cvo"], tb=True, name=tag + "d_yc")
        (dcvc,), g_cw = rows_bwd(_fn_conv, [sv["cvc"]], sv["p_cv"], [], [[dyc]], tile=t_row, name=tag + "conv",
                                    halos=(0,))
        g_rwo = mm(sv["yb"], do_b, ta=True, n_split=4, name=tag + "g_rwo")
        dyb = mm(do_b, lw["rwo"], tb=True, name=tag + "d_yb")
        (dy, dr_p, dk_p, dv_p, dg), g_post = rows_bwd(
            _fn_rwkv_post, [sv["y"], sv["r"], sv["k2"], sv["v"], sv["g"]], sv["p_post"], [bd], [[dyb]], tile=t_row,
            name=tag + "rwkv_post")
        dr_s, dld, dk_s, dv_s, dan, dbn = wkv_bwd(sv["r"], sv["ld"], sv["k2"], sv["v"], sv["an"], sv["bn"],
                                                  sv["states"], dy, name=tag + "wkv")
        dv_list = [dv_s, dv_p] + ([dv_first] if (not vres and dv_first is not None) else [])
        d_prep, g_prep = rows_bwd(
            sv["fn_prep"], sv["rw_rows"], sv["p_rw"], [bd],
            [[dr_s, dr_p], [dld], [dk_s, dk_p], dv_list, [dan], [dbn], [dg]], tile=t_row, name=tag + "rwkv_prep",
            halos=sv["rw_halos"])
        drwc = d_prep[0]
        dh_extra = []
        if vres:
            dh_extra = [d_prep[1]]
            dv_first = d_prep[2]
        g_wo = mm(sv["o"], do_a, ta=True, a_batched=True, n_split=4, name=tag + "g_wo")
        do = mm(do_a, lw["wo"], tb=True, b_batched=True, name=tag + "d_o")
        dq, dk, dvv = attn_bwd(sv["q"], sv["k"], sv["vv"], do, tq=_pick(s_len, 256, 8), name=tag + "attn")
        (dq_raw, dkn_pad, dkpe), g_qk = rows_bwd(
            _fn_qk_post, [sv["q_raw"], sv["kn_pad"], sv["kpe"], cos, sin], sv["p_qk"], [place, rot], [[dq], [dk]],
            tile=t_wide, name=tag + "qk_post", grad_rows=[0, 1, 2])
        g_wq = mm(sv["qn"], dq_raw, ta=True, b_batched=True, name=tag + "g_wq")
        g_wk = mm(sv["kvn"], dkn_pad, ta=True, b_batched=True, name=tag + "g_wk")
        g_wv = mm(sv["kvn"], dvv, ta=True, b_batched=True, name=tag + "g_wv")
        dqn = mm(dq_raw, lw["wq"], tb=True, a_batched=True, b_batched=True, reduce_batch=True, name=tag + "d_qn")
        dkvn = mm(dkn_pad, lw["wk"], tb=True, a_batched=True, b_batched=True, reduce_batch=True, name=tag + "d_kvn_k")
        dkvn = mm(dvv, lw["wv"], tb=True, a_batched=True, b_batched=True, reduce_batch=True, add=dkvn,
                  name=tag + "d_kvn_v")
        (dmla,), g_mla = rows_bwd(_fn_mla_prep, [sv["mla"]], sv["p_mla"], [], [[dqn], [dkvn], [dkpe]], tile=t_row,
                                  name=tag + "mla_prep")
        g_gate = mm(sv["h"], dgate, ta=True, name=tag + "g_gate")
        g_mlaw = mm(sv["h"], dmla, ta=True, name=tag + "g_mla")
        g_rw = mm(sv["h"], drwc, ta=True, name=tag + "g_rw")
        g_cv = mm(sv["h"], dcvc, ta=True, name=tag + "g_cv")
        dh = mm(dgate, lw["gate"], tb=True, name=tag + "d_h_gate")
        dh = mm(dmla, lw["mla"], tb=True, add=dh, name=tag + "d_h_mla")
        dh = mm(drwc, lw["rw"], tb=True, add=dh, name=tag + "d_h_rw")
        dh = mm(dcvc, lw["cv"], tb=True, add=dh, name=tag + "d_h_cv")
        (dx,), (g_n1,) = rows_bwd(_fn_norm, [sv["x"]], sv["p_norm1"], [], [[dh] + dh_extra], tile=t_row,
                                  name=tag + "norm1", extra={0: [dx1]})
        slabs = chip_major_grads(dict(gate=g_gate, mla=g_mlaw, rw=g_rw, cv=g_cv, wq=g_wq, wk=g_wk, wv=g_wv, wo=g_wo,
                                      rwo=g_rwo, cvo=g_cvo, out=g_out, up=g_up, down=g_down))
        for n, val in slabs.items():
            grads[n][l] = val
        grads["attn_norm"][l], grads["mlp_norm"][l] = g_n1, g_n2
        grads["mla_q_a_norm"][l], grads["mla_kv_a_norm"][l] = g_mla
        grads["mla_q_norm"][l], grads["mla_k_norm"][l] = g_qk
        grads["rwkv_ln_w"][l], grads["rwkv_ln_b"][l], grads["rwkv_r_k"][l] = g_post
        for n, val in zip(["rwkv_mu", "rwkv_w0", "rwkv_w2", "rwkv_a0", "rwkv_a2", "rwkv_g2", "rwkv_k_k", "rwkv_k_a"],
                          g_prep[:8]):
            grads[n][l] = val
        if vres:
            for n, val in zip(["rwkv_v1", "rwkv_v_mu", "rwkv_v0", "rwkv_v2"], g_prep[8:12]):
                grads[n][l - 1] = val
        grads["conv_w"][l] = jnp.concatenate(g_cw, axis=0)
    return loss, dx, grads


def _split3(a):
    hi = a.astype(bf16)
    r1 = a - hi.astype(f32)
    mid = r1.astype(bf16)
    lo = (r1 - mid.astype(f32)).astype(bf16)
    return hi, mid, lo


def _shard_axis(name):
    return 1 if name in ROW_SHARDED else 2


def _pack(pieces, width, dtype, row_align):
    flat = jnp.concatenate([p.reshape(-1).astype(dtype) for p in pieces])
    rows = -(-flat.shape[0] // width)
    rows = -(-rows // row_align) * row_align
    return jnp.pad(flat, (0, rows * width - flat.shape[0])).reshape(rows, width)


def _unpack(flat2d, shapes):
    flat = flat2d.reshape(-1)
    out, off = [], 0
    for shp in shapes:
        n = int(np.prod(shp))
        out.append(flat[off:off + n].reshape(shp))
        off += n
    return out


def kernel(x, positions, attn_norm, w_in, mla_q_a_norm, mla_wq_b, mla_kv_a_norm, mla_wkv_b, mla_q_norm, mla_k_norm, mla_w_o, rwkv_mu, rwkv_w0, rwkv_w2, rwkv_a0, rwkv_a2, rwkv_g2, rwkv_k_k, rwkv_k_a, rwkv_r_k, rwkv_ln_w, rwkv_ln_b, rwkv_w_o, rwkv_v1, rwkv_v_mu, rwkv_v0, rwkv_v2, conv_w, conv_w_o, w_out, mlp_norm, w_up, w_down, loss_target, m_attn_norm, m_w_in, m_mla_q_a_norm, m_mla_wq_b, m_mla_kv_a_norm, m_mla_wkv_b, m_mla_q_norm, m_mla_k_norm, m_mla_w_o, m_rwkv_mu, m_rwkv_w0, m_rwkv_w2, m_rwkv_a0, m_rwkv_a2, m_rwkv_g2, m_rwkv_k_k, m_rwkv_k_a, m_rwkv_r_k, m_rwkv_ln_w, m_rwkv_ln_b, m_rwkv_w_o, m_rwkv_v1, m_rwkv_v_mu, m_rwkv_v0, m_rwkv_v2, m_conv_w, m_conv_w_o, m_w_out, m_mlp_norm, m_w_up, m_w_down, v_attn_norm, v_w_in, v_mla_q_a_norm, v_mla_wq_b, v_mla_kv_a_norm, v_mla_wkv_b, v_mla_q_norm, v_mla_k_norm, v_mla_w_o, v_rwkv_mu, v_rwkv_w0, v_rwkv_w2, v_rwkv_a0, v_rwkv_a2, v_rwkv_g2, v_rwkv_k_k, v_rwkv_k_a, v_rwkv_r_k, v_rwkv_ln_w, v_rwkv_ln_b, v_rwkv_w_o, v_rwkv_v1, v_rwkv_v_mu, v_rwkv_v0, v_rwkv_v2, v_conv_w, v_conv_w_o, v_w_out, v_mlp_norm, v_w_up, v_w_down):
    args = dict(locals())
    wts = {n: args[n] for n in WEIGHTS}
    mom = {n: args["m_" + n] for n in WEIGHTS}
    var = {n: args["v_" + n] for n in WEIGHTS}
    chip = 2 * lax.axis_index("x") + lax.axis_index("y")
    core = lax.axis_index("c").astype(jnp.int32).reshape(1)

    med_names = [n for n in MED if n != "conv_w"]
    med_pieces = [wts[n] for n in med_names] + list(_split3(wts["conv_w"]))
    med_shapes = [p.shape for p in med_pieces]
    shards = [wts[n].astype(bf16).reshape(-1, wts[n].shape[-1]) for n in BIG] + [_pack(med_pieces, 128, bf16, 32)]
    got = gather_weights(shards, name="gather_weights")
    chip_idx = chip.astype(jnp.int32).reshape(1)
    got = [place_slab(g, s, chip_idx, name=f"place_own_{q}") for q, (g, s) in enumerate(zip(got, shards))]
    whole = {}
    for q, n in enumerate(BIG):
        depth, rows, cols = wts[n].shape
        by_chip = got[q].reshape(4, depth, rows, cols)
        if n in ROW_SHARDED:
            whole[n] = by_chip.transpose(1, 0, 2, 3).reshape(depth, 4 * rows, cols)
        else:
            whole[n] = by_chip.transpose(1, 2, 0, 3).reshape(depth, rows, 4 * cols)
    per_chip = [_unpack(got[len(BIG)][j], med_shapes) for j in range(4)]
    for q, n in enumerate(med_names):
        whole[n] = jnp.concatenate([per_chip[j][q] for j in range(4)], axis=_shard_axis(n)).astype(f32)
    base = len(med_names)
    cw_parts = [jnp.concatenate([per_chip[j][base + t] for j in range(4)], axis=2).astype(f32) for t in range(3)]
    whole["conv_w"] = (cw_parts[0] + cw_parts[1]) + cw_parts[2]
    small = {n: wts[n] for n in SMALL}
    small["rwkv_r_k"] = wts["rwkv_r_k"].reshape(DEPTH, RW)

    loss, grad_x, grads = local_step(x[0], positions[0], loss_target[0], whole, small)
    loss = lax.psum(loss, ("x", "y", "c"))

    sm_names = SMALL + MED
    sm_grads = [jnp.stack(grads[n]) for n in sm_names]
    sm_shapes = [g.shape for g in sm_grads]
    sm_sum = _unpack(all_reduce_small(_pack(sm_grads, 128, f32, 8), name="reduce_small"), sm_shapes)
    gsum = {}
    for n, g in zip(sm_names, sm_sum):
        if n in MED:
            ax = _shard_axis(n)
            width = wts[n].shape[ax]
            g = lax.dynamic_slice_in_dim(g, chip * width, width, axis=ax)
        gsum[n] = g.reshape(wts[n].shape)
    slabs = [grads[n][l] for n in BIG for l in range(DEPTH)]
    labels = [f"{n}_{l}" for n in BIG for l in range(DEPTH)]
    from_sibling = grads_to_sibling(slabs, name="grads_to_sibling")
    chip_sums = [pair_sum(s, t, core, name="pair_sum_" + lb) for s, t, lb in zip(slabs, from_sibling, labels)]
    arrived = chips_all_to_all(chip_sums, name="scatter_grads")
    bufs, layout = [], []
    for q in range(len(BIG)):
        rows = slabs[q * DEPTH].shape[1]
        buf = None
        for l in range(DEPTH):
            buf = sum4_into(arrived[q * DEPTH + l], buf, core, layer=l, total_rows=DEPTH * rows,
                            name="sum_chips_" + labels[q * DEPTH + l])
            layout.append((q, l * rows, rows))
        bufs.append(buf)
    reduced = join_halves(bufs, layout, name="join_halves")

    out_g, out_d, out_m, out_v = {}, {}, {}, {}
    for q, n in enumerate(BIG):
        shp = wts[n].shape
        as2d = lambda a: a.reshape(-1, shp[-1])
        g2d = w_in_window_cols(reduced[q], chip) if n == "w_in" else reduced[q]
        res = adamw(as2d(wts[n]), g2d, as2d(mom[n]), as2d(var[n]), name="adamw_" + n)
        out_g[n], out_d[n], out_m[n], out_v[n] = [r.reshape(shp) for r in res]
    sm_all = SMALL + MED
    sm_shapes2 = [wts[n].shape for n in sm_all]
    res = adamw(_pack([wts[n] for n in sm_all], 128, f32, 8), _pack([gsum[n] for n in sm_all], 128, f32, 8),
                _pack([mom[n] for n in sm_all], 128, f32, 8), _pack([var[n] for n in sm_all], 128, f32, 8),
                name="adamw_small")
    for tgt, flat in zip((out_g, out_d, out_m, out_v), res):
        for n, val in zip(sm_all, _unpack(flat, sm_shapes2)):
            tgt[n] = val
    return (loss, grad_x[None], *[out_g[n] for n in WEIGHTS], *[out_d[n] for n in WEIGHTS],
            *[out_m[n] for n in WEIGHTS], *[out_v[n] for n in WEIGHTS])
```

```python
import functools

import jax
import jax.numpy as jnp
import numpy as np
from jax import lax
from jax.experimental import pallas as pl
from jax.experimental.pallas import tpu as pltpu

f32, bf16 = jnp.float32, jnp.bfloat16
HI = lax.Precision.HIGHEST
MESH = pl.DeviceIdType.MESH

D = 1024
DEPTH = 2
MLA_H, NOPE, ROPE, DQK, DV = 8, 64, 32, 96, 64
QL, KVL = 384, 256
RW, RH, RN = 256, 4, 64
DL, AL, GL, MVL = 64, 64, 128, 32
CW = 256
DFF = 4096
GATE = 3 * D
MLA_COLS = QL + KVL + ROPE
MLA_PAD = 768
NORM_EPS = 1e-6
GN_EPS = 64e-5
ROPE_THETA = 10000.0
LR, B1, B2, EPS, WD, STEP = 0.001, 0.9, 0.999, 1e-08, 0.01, 10

VMEM_LIMIT = 52 * 1024 * 1024
WKV_CHUNK = 64
WKV_CHUNKS_PER_STEP = 2

BIG = ["w_in", "mla_wq_b", "mla_wkv_b", "mla_w_o", "rwkv_w_o", "conv_w_o", "w_out", "w_up", "w_down"]
MED = ["rwkv_w2", "rwkv_a2", "rwkv_g2", "rwkv_v1", "rwkv_v2", "conv_w"]
ROW_SHARDED = {"w_out", "w_down", "rwkv_v1"}
SMALL = ["attn_norm", "mla_q_a_norm", "mla_kv_a_norm", "mla_q_norm", "mla_k_norm", "rwkv_mu", "rwkv_w0",
         "rwkv_a0", "rwkv_k_k", "rwkv_k_a", "rwkv_r_k", "rwkv_ln_w", "rwkv_ln_b", "rwkv_v_mu", "rwkv_v0",
         "mlp_norm"]
WEIGHTS = ["attn_norm", "w_in", "mla_q_a_norm", "mla_wq_b", "mla_kv_a_norm", "mla_wkv_b", "mla_q_norm",
           "mla_k_norm", "mla_w_o", "rwkv_mu", "rwkv_w0", "rwkv_w2", "rwkv_a0", "rwkv_a2", "rwkv_g2",
           "rwkv_k_k", "rwkv_k_a", "rwkv_r_k", "rwkv_ln_w", "rwkv_ln_b", "rwkv_w_o", "rwkv_v1", "rwkv_v_mu",
           "rwkv_v0", "rwkv_v2", "conv_w", "conv_w_o", "w_out", "mlp_norm", "w_up", "w_down"]


def _cparams(sem=None):
    return pltpu.CompilerParams(dimension_semantics=sem, vmem_limit_bytes=VMEM_LIMIT)


def _pick(dim, pref, align):
    if dim <= pref:
        return dim
    t = (pref // align) * align
    while t >= align:
        if dim % t == 0:
            return t
        t -= align
    return dim


def _bdot(a, b, dims):
    return lax.dot_general(a.astype(bf16), b.astype(bf16), (dims, ((), ())), preferred_element_type=f32)


@jax.custom_vjp
def _mm(a, b):
    return _bdot(a, b, ((1,), (0,)))


def _mm_fwd(a, b):
    return _mm(a, b), (a, b)


def _mm_bwd(res, g):
    a, b = res
    return _bdot(g, b, ((1,), (1,))), _bdot(a, g, ((0,), (0,)))


_mm.defvjp(_mm_fwd, _mm_bwd)


@jax.custom_vjp
def _mm_nt(a, b):
    return _bdot(a, b, ((1,), (1,)))


def _mm_nt_fwd(a, b):
    return _mm_nt(a, b), (a, b)


def _mm_nt_bwd(res, g):
    a, b = res
    return _bdot(g, b, ((1,), (0,))), _bdot(g, a, ((0,), (0,)))


_mm_nt.defvjp(_mm_nt_fwd, _mm_nt_bwd)


_NN, _NT, _TN = ((1,), (0,)), ((1,), (1,)), ((0,), (0,))


def _dg(a, b, dims):
    return lax.dot_general(a, b, (dims, ((), ())), preferred_element_type=f32)


def _bf16_pieces(x, count):
    out, rest = [], x
    for q in range(count):
        piece = rest.astype(bf16)
        out.append(piece)
        if q + 1 < count:
            rest = rest - piece.astype(f32)
    return out


def _dot3(a, b, dims):
    (ah, al), (bh, bl) = _bf16_pieces(a, 2), _bf16_pieces(b, 2)
    return _dg(ah, bh, dims) + (_dg(ah, bl, dims) + _dg(al, bh, dims))


@jax.custom_vjp
def _hdot(a, b):
    return _dot3(a, b, _NN)


@jax.custom_vjp
def _hdot_nt(a, b):
    return _dot3(a, b, _NT)


@jax.custom_vjp
def _hdot_tn(a, b):
    return _dot3(a, b, _TN)


_hdot.defvjp(lambda a, b: (_hdot(a, b), (a, b)), lambda res, g: (_hdot_nt(g, res[1]), _hdot_tn(res[0], g)))
_hdot_nt.defvjp(lambda a, b: (_hdot_nt(a, b), (a, b)), lambda res, g: (_hdot(g, res[1]), _hdot_tn(g, res[0])))
_hdot_tn.defvjp(lambda a, b: (_hdot_tn(a, b), (a, b)), lambda res, g: (_hdot_nt(res[1], g), _hdot(res[0], g)))


@functools.partial(jax.custom_vjp, nondiff_argnums=(2,))
def _exact_l(m, x, transposed):
    mb = m.astype(bf16)
    hi, mid, lo = _bf16_pieces(x, 3)
    dims = _TN if transposed else _NN
    return (_dg(mb, hi, dims) + _dg(mb, mid, dims)) + _dg(mb, lo, dims)


_exact_l.defvjp(lambda m, x, transposed: (_exact_l(m, x, transposed), m),
                lambda transposed, m, g: (jnp.zeros_like(m), _exact_l(m, g, not transposed)))


@functools.partial(jax.custom_vjp, nondiff_argnums=(2,))
def _exact_r(x, m, transposed):
    mb = m.astype(bf16)
    hi, mid, lo = _bf16_pieces(x, 3)
    dims = _NT if transposed else _NN
    return (_dg(hi, mb, dims) + _dg(mid, mb, dims)) + _dg(lo, mb, dims)


_exact_r.defvjp(lambda x, m, transposed: (_exact_r(x, m, transposed), m),
                lambda transposed, m, g: (_exact_r(g, m, not transposed), jnp.zeros_like(m)))


def _rms(x, g, eps=NORM_EPS):
    return x * lax.rsqrt(jnp.mean(x * x, axis=-1, keepdims=True) + eps) * g


def _sigmoid(x):
    return 1.0 / (1.0 + jnp.exp(-x))


def _softplus(x):
    return jnp.maximum(x, 0.0) + jnp.log(1.0 + jnp.exp(-jnp.maximum(x, -x)))


def _lane_split(x, sizes):
    bounds = np.cumsum([0] + list(sizes))

    @jax.custom_vjp
    def split(v):
        return tuple(v[:, int(bounds[q]):int(bounds[q + 1])] for q in range(len(sizes)))

    split.defvjp(lambda v: (split(v), None), lambda _, g: (jnp.concatenate(g, axis=-1),))
    return split(x)


def _row_split(x, sizes):
    bounds = np.cumsum([0] + list(sizes))

    @jax.custom_vjp
    def split(v):
        return tuple(v[int(bounds[q]):int(bounds[q + 1]), :] for q in range(len(sizes)))

    split.defvjp(lambda v: (split(v), None), lambda _, g: (jnp.concatenate(g, axis=0),))
    return split(x)


def _unstack(x):
    @jax.custom_vjp
    def unstack(v):
        return tuple(v[q] for q in range(v.shape[0]))

    unstack.defvjp(lambda v: (unstack(v), None), lambda _, g: (jnp.stack(g, axis=0),))
    return unstack(x)


def _shift_mats(t, k):
    r = lax.broadcasted_iota(jnp.int32, (t, t), 0)
    c = lax.broadcasted_iota(jnp.int32, (t, t), 1)
    inner = (r - c == k).astype(f32)
    r8 = lax.broadcasted_iota(jnp.int32, (t, 8), 0)
    c8 = lax.broadcasted_iota(jnp.int32, (t, 8), 1)
    edge = (c8 - r8 == 8 - k).astype(f32)
    return inner, edge


def _shift(x, halo, k):
    inner, edge = _shift_mats(x.shape[0], k)
    return _exact_l(inner, x, False) + jnp.dot(edge, halo, precision=HI, preferred_element_type=f32)


def mm(a, b, *, name, ta=False, tb=False, a_batched=False, b_batched=False, reduce_batch=False, add=None,
       act_grad=None, n_split=1, out_dtype=f32, tm=1024, tn=1024, tk=1024):
    ash, bsh = a.shape[-2:], b.shape[-2:]
    (k_, m_) = ash if ta else ash[::-1]
    (k2_, n_) = bsh[::-1] if tb else bsh
    assert k_ == k2_, (a.shape, b.shape, ta, tb)
    hb = a.shape[0] if a_batched else (b.shape[0] if b_batched else 1)
    batched_out = (a_batched or b_batched) and not reduce_batch
    h_out = hb if batched_out else 1
    h_red = hb if reduce_batch else 1
    tm = _pick(m_, tm, 128)
    tn = _pick(n_ // n_split, tn, 128)
    tk = _pick(k_, tk, 128)
    nm, nn, nk = m_ // tm, n_ // tn, k_ // tk

    def a_map(ho, i, j, hr, kk):
        blk = (kk, i) if ta else (i, kk)
        return ((ho if batched_out else hr),) + blk if a_batched else blk

    def b_map(ho, i, j, hr, kk):
        blk = (j, kk) if tb else (kk, j)
        return ((ho if batched_out else hr),) + blk if b_batched else blk

    a_blk = (tk, tm) if ta else (tm, tk)
    b_blk = (tn, tk) if tb else (tk, tn)
    in_specs = [pl.BlockSpec(((1,) + a_blk) if a_batched else a_blk, a_map),
                pl.BlockSpec(((1,) + b_blk) if b_batched else b_blk, b_map)]
    args = [a, b]
    for extra in (add, act_grad):
        if extra is not None:
            in_specs.append(pl.BlockSpec((tm, tn), lambda ho, i, j, hr, kk: (i, j)))
            args.append(extra)
    if n_split > 1:
        per = n_ // n_split // tn
        if batched_out:
            out_spec = pl.BlockSpec((1, 1, tm, tn), lambda ho, i, j, hr, kk: (j // per, ho, i, j % per))
            out_shape = jax.ShapeDtypeStruct((n_split, hb, m_, n_ // n_split), out_dtype)
        else:
            out_spec = pl.BlockSpec((1, tm, tn), lambda ho, i, j, hr, kk: (j // per, i, j % per))
            out_shape = jax.ShapeDtypeStruct((n_split, m_, n_ // n_split), out_dtype)
    elif batched_out:
        out_spec = pl.BlockSpec((1, tm, tn), lambda ho, i, j, hr, kk: (ho, i, j))
        out_shape = jax.ShapeDtypeStruct((hb, m_, n_), out_dtype)
    else:
        out_spec = pl.BlockSpec((tm, tn), lambda ho, i, j, hr, kk: (i, j))
        out_shape = jax.ShapeDtypeStruct((m_, n_), out_dtype)
    lead = (0,) * (int(batched_out) + int(n_split > 1))
    dims = ((0,) if ta else (1,), (1,) if tb else (0,))
    has_add, has_act = add is not None, act_grad is not None

    def body(*refs):
        a_ref, b_ref = refs[0], refs[1]
        pos = 2
        add_ref = act_ref = None
        if has_add:
            add_ref = refs[pos]
            pos += 1
        if has_act:
            act_ref = refs[pos]
            pos += 1
        o_ref, acc_ref = refs[pos], refs[pos + 1]
        hr, kk = pl.program_id(3), pl.program_id(4)
        first = jnp.logical_and(hr == 0, kk == 0)
        last = jnp.logical_and(hr == h_red - 1, kk == nk - 1)
        av = a_ref[0] if a_batched else a_ref[...]
        bv = b_ref[0] if b_batched else b_ref[...]
        p = _bdot(av, bv, dims)

        @pl.when(first)
        def _():
            acc_ref[...] = p

        @pl.when(jnp.logical_not(first))
        def _():
            acc_ref[...] += p

        @pl.when(last)
        def _():
            r = acc_ref[...]
            if has_act:
                r = r * (2.0 * jnp.maximum(act_ref[...], 0.0))
            if has_add:
                r = r + add_ref[...]
            if lead:
                o_ref[lead] = r.astype(out_dtype)
            else:
                o_ref[...] = r.astype(out_dtype)

    return pl.pallas_call(
        body, name=name, grid=(h_out, nm, nn, h_red, nk), in_specs=in_specs, out_specs=out_spec,
        out_shape=out_shape, scratch_shapes=[pltpu.VMEM((tm, tn), f32)],
        compiler_params=_cparams(("parallel", "parallel", "parallel", "arbitrary", "arbitrary")),
    )(*args)


def _row_spec(arr, tile, idx):
    if arr.ndim == 2:
        return pl.BlockSpec((tile, arr.shape[1]), lambda i: (idx(i), 0))
    return pl.BlockSpec((arr.shape[0], tile, arr.shape[2]), lambda i: (0, idx(i), 0))


def _halo_spec(arr, tile, idx):
    per = tile // 8
    return pl.BlockSpec((8, arr.shape[1]), lambda i: (jnp.maximum(idx(i) * per - 1, 0), 0))


def _full_spec(arr):
    nd = arr.ndim
    return pl.BlockSpec(arr.shape, lambda i: (0,) * nd)


def rows_fwd(fn, rows, params, consts, out_shapes, *, tile, name, halos=()):
    s_len = rows[0].shape[-2]
    n = s_len // tile
    nr, nh, npar, nc = len(rows), len(halos), len(params), len(consts)
    ident = lambda i: i
    in_specs = ([_row_spec(r, tile, ident) for r in rows] + [_halo_spec(rows[h], tile, ident) for h in halos]
                + [_full_spec(p) for p in params] + [_full_spec(c) for c in consts])
    out_specs = [_row_spec(o, tile, ident) for o in out_shapes]

    def body(*refs):
        i = pl.program_id(0)
        rv = [r[...] for r in refs[:nr]]
        keep = (i > 0).astype(f32)
        hv = [r[...] * keep for r in refs[nr:nr + nh]]
        pv = [r[...] for r in refs[nr + nh:nr + nh + npar]]
        cv = [r[...] for r in refs[nr + nh + npar:nr + nh + npar + nc]]
        outs = fn(rv, hv, pv, cv)
        for o_ref, o in zip(refs[nr + nh + npar + nc:], outs):
            o_ref[...] = o

    return pl.pallas_call(
        body, name=name, grid=(n,), in_specs=in_specs, out_specs=out_specs, out_shape=list(out_shapes),
        compiler_params=_cparams(("arbitrary",)),
    )(*rows, *[rows[h] for h in halos], *params, *consts)


def rows_bwd(fn, rows, params, consts, douts, *, tile, name, halos=(), grad_rows=None, extra=None):
    s_len = rows[0].shape[-2]
    n = s_len // tile
    nr, nh, npar, nc = len(rows), len(halos), len(params), len(consts)
    grad_rows = list(range(nr)) if grad_rows is None else list(grad_rows)
    extra = extra or {}
    assert all(h in grad_rows for h in halos)
    rev = lambda i: n - 1 - i
    dflat = [d for ds in douts for d in ds]
    dcount = [len(ds) for ds in douts]
    eflat = [e for g in grad_rows for e in extra.get(g, [])]
    ecount = [len(extra.get(g, [])) for g in grad_rows]
    in_specs = ([_row_spec(r, tile, rev) for r in rows] + [_halo_spec(rows[h], tile, rev) for h in halos]
                + [_full_spec(p) for p in params] + [_full_spec(c) for c in consts]
                + [_row_spec(d, tile, rev) for d in dflat] + [_row_spec(e, tile, rev) for e in eflat])
    out_shapes = ([jax.ShapeDtypeStruct(rows[g].shape, f32) for g in grad_rows]
                  + [jax.ShapeDtypeStruct(p.shape, f32) for p in params])
    out_specs = [_row_spec(rows[g], tile, rev) for g in grad_rows] + [_full_spec(p) for p in params]
    scratch = [pltpu.VMEM((8, rows[h].shape[1]), f32) for h in halos]
    n_in = nr + nh + npar + nc + len(dflat) + len(eflat)
    n_out = len(grad_rows) + npar

    def body(*refs):
        i = pl.program_id(0)
        rv = [r[...] for r in refs[:nr]]
        keep = (i < n - 1).astype(f32)
        hv = [r[...] * keep for r in refs[nr:nr + nh]]
        pv = [r[...] for r in refs[nr + nh:nr + nh + npar]]
        pos = nr + nh + npar
        cv = [r[...] for r in refs[pos:pos + nc]]
        pos += nc
        dv = []
        for cnt in dcount:
            acc = refs[pos][...]
            for q in range(1, cnt):
                acc = acc + refs[pos + q][...]
            dv.append(acc)
            pos += cnt
        ev = []
        for cnt in ecount:
            ev.append([refs[pos + q][...] for q in range(cnt)])
            pos += cnt
        out_refs = refs[n_in:n_in + n_out]
        carry_refs = refs[n_in + n_out:]

        def f(gr, gh, gp):
            full = list(rv)
            for g, val in zip(grad_rows, gr):
                full[g] = val
            return tuple(fn(full, gh, gp, cv))

        _, vjp = jax.vjp(f, [rv[g] for g in grad_rows], hv, pv)
        d_rows, d_halos, d_params = vjp(tuple(dv))

        @pl.when(i == 0)
        def _():
            for c_ref in carry_refs:
                c_ref[...] = jnp.zeros_like(c_ref)
            for p_ref in out_refs[len(grad_rows):]:
                p_ref[...] = jnp.zeros_like(p_ref)

        for q, g in enumerate(grad_rows):
            val = d_rows[q]
            for e in ev[q]:
                val = val + e
            out_refs[q][...] = val
            if g in halos:
                hq = list(halos).index(g)
                out_refs[q][tile - 8:tile, :] += carry_refs[hq][...]
                carry_refs[hq][...] = d_halos[hq]
        for p_ref, dp in zip(out_refs[len(grad_rows):], d_params):
            p_ref[...] += dp

    res = pl.pallas_call(
        body, name=name, grid=(n,), in_specs=in_specs, out_specs=out_specs, out_shape=out_shapes,
        scratch_shapes=scratch, compiler_params=_cparams(("arbitrary",)),
    )(*rows, *[rows[h] for h in halos], *params, *consts, *dflat, *eflat)
    return list(res[:len(grad_rows)]), list(res[len(grad_rows):])


def _fn_norm(rows, halos, params, consts):
    return (_rms(rows[0], params[0]),)


def _fn_mla_prep(rows, halos, params, consts):
    cq, ckv, kpe = _lane_split(rows[0], (QL, KVL, MLA_PAD - QL - KVL))
    return _rms(cq, params[0]), _rms(ckv, params[1]), kpe


def _rope(x, cos, sin, rot):
    return x * cos + _exact_r(x, rot, False) * sin


def _fn_qk_post(rows, halos, params, consts):
    q_raw, kn_pad, kpe, cos, sin = rows
    q_norm, k_norm = params
    place, rot = consts
    kpe96 = _exact_r(kpe, place, False)
    qs = [_rope(_rms(qh, q_norm), cos, sin, rot) for qh in _unstack(q_raw)]
    ks = [_rope(_rms(kh + kpe96, k_norm), cos, sin, rot) for kh in _unstack(kn_pad)]
    return jnp.stack(qs, axis=0), jnp.stack(ks, axis=0)


def _seg(x, bd):
    return _exact_r(x, bd, False)


def _make_fn_rwkv_prep(vres):
    def fn(rows, halos, params, consts):
        cols = rows[0]
        bd = consts[0]
        mu, w0, w2, a0, a2, g2, k_k, k_a = params[:8]
        prev = _shift(cols, halos[0], 1)
        c = cols + (prev - cols) * mu
        r, k, v, xw, xa, xg = _lane_split(c, (RW, RW, RW, DL, AL, GL))
        log_w = -_softplus(-(w0 + _mm(jnp.tanh(xw), w2))) - 0.5
        ld = -jnp.exp(log_w)
        a = _sigmoid(a0 + _mm(xa, a2))
        g = _mm(_sigmoid(xg), g2)
        if vres:
            hcur, v_first = rows[1], rows[2]
            v1, v_mu, v0, v2 = params[8:12]
            xv = _mm(hcur, v1)
            xv_prev = _shift(xv, _mm(halos[1], v1), 1)
            xv = xv + (xv_prev - xv) * v_mu
            v = v + (v_first - v) * _sigmoid(v0 + _mm(xv, v2))
        kk = k * k_k
        kk = kk / jnp.maximum(jnp.sqrt(_seg(kk * kk, bd)), 1e-12)
        k2 = k * (1.0 + (a - 1.0) * k_a)
        return r, ld, k2, v, -kk, kk * a, g
    return fn


def _fn_rwkv_post(rows, halos, params, consts):
    y, r, k2, v, g = rows
    ln_w, ln_b, r_k = params
    bd = consts[0]
    mean = _seg(y, bd) * (1.0 / RN)
    d = y - mean
    var = _seg(d * d, bd) * (1.0 / RN)
    yn = d * lax.rsqrt(var + GN_EPS) * ln_w + ln_b
    bonus = _seg(r * k2 * r_k, bd) * v
    return ((yn + bonus) * g,)


def _fn_conv(rows, halos, params, consts):
    cols, halo = rows[0], halos[0]
    w0, w1, w2 = params
    b, c, x = _lane_split(cols, (CW, CW, CW))
    _, ch, xh = _lane_split(halo, (CW, CW, CW))
    u, uh = c * x, ch * xh
    return (b * (w0 * _shift(u, uh, 2) + w1 * _shift(u, uh, 1) + w2 * u),)


def _fn_merge(rows, halos, params, consts):
    gate, o_a, o_b, o_c = rows
    g_a, g_b, g_c = _lane_split(gate, (D, D, D))
    return (_sigmoid(g_a) * o_a + _sigmoid(g_b) * o_b + _sigmoid(g_c) * o_c,)


def _fn_relu2(rows, halos, params, consts):
    return (jnp.square(jnp.maximum(rows[0], 0.0)),)


def _attn_block(q, k, v, q0):
    tq, s_len = q.shape[0], k.shape[0]
    s = _mm_nt(q, k) * (DQK ** -0.5)
    row = q0 + lax.broadcasted_iota(jnp.int32, (tq, s_len), 0)
    col = lax.broadcasted_iota(jnp.int32, (tq, s_len), 1)
    s = jnp.where(row >= col, s, -1e30)
    m = lax.stop_gradient(jnp.max(s, axis=-1, keepdims=True))
    e = jnp.exp(s - m)
    p = e / jnp.sum(e, axis=-1, keepdims=True)
    return _mm(p, v)


def attn_fwd(q, k, v, *, tq, name):
    h, s_len, _ = q.shape

    def body(q_ref, k_ref, v_ref, o_ref):
        o_ref[0] = _attn_block(q_ref[0], k_ref[0], v_ref[0], pl.program_id(1) * tq)

    return pl.pallas_call(
        body, name=name, grid=(h, s_len // tq),
        in_specs=[pl.BlockSpec((1, tq, DQK), lambda hh, i: (hh, i, 0)),
                  pl.BlockSpec((1, s_len, DQK), lambda hh, i: (hh, 0, 0)),
                  pl.BlockSpec((1, s_len, DV), lambda hh, i: (hh, 0, 0))],
        out_specs=pl.BlockSpec((1, tq, DV), lambda hh, i: (hh, i, 0)),
        out_shape=jax.ShapeDtypeStruct((h, s_len, DV), f32),
        compiler_params=_cparams(("parallel", "arbitrary")),
    )(q, k, v)


def attn_bwd(q, k, v, do, *, tq, name):
    h, s_len, _ = q.shape

    def body(q_ref, k_ref, v_ref, do_ref, dq_ref, dk_ref, dv_ref):
        i = pl.program_id(1)
        _, vjp = jax.vjp(functools.partial(_attn_block, q0=i * tq), q_ref[0], k_ref[0], v_ref[0])
        dq, dk, dv = vjp(do_ref[0])
        dq_ref[0] = dq

        @pl.when(i == 0)
        def _():
            dk_ref[0] = dk
            dv_ref[0] = dv

        @pl.when(i > 0)
        def _():
            dk_ref[0] += dk
            dv_ref[0] += dv

    return pl.pallas_call(
        body, name=name, grid=(h, s_len // tq),
        in_specs=[pl.BlockSpec((1, tq, DQK), lambda hh, i: (hh, i, 0)),
                  pl.BlockSpec((1, s_len, DQK), lambda hh, i: (hh, 0, 0)),
                  pl.BlockSpec((1, s_len, DV), lambda hh, i: (hh, 0, 0)),
                  pl.BlockSpec((1, tq, DV), lambda hh, i: (hh, i, 0))],
        out_specs=[pl.BlockSpec((1, tq, DQK), lambda hh, i: (hh, i, 0)),
                   pl.BlockSpec((1, s_len, DQK), lambda hh, i: (hh, 0, 0)),
                   pl.BlockSpec((1, s_len, DV), lambda hh, i: (hh, 0, 0))],
        out_shape=[jax.ShapeDtypeStruct((h, s_len, DQK), f32), jax.ShapeDtypeStruct((h, s_len, DQK), f32),
                   jax.ShapeDtypeStruct((h, s_len, DV), f32)],
        compiler_params=_cparams(("parallel", "arbitrary")),
    )(q, k, v, do)


def _wkv_local(r, ld, k, v, a, b):
    c, n = r.shape
    ri = lax.broadcasted_iota(jnp.int32, (c, c), 0)
    ci = lax.broadcasted_iota(jnp.int32, (c, c), 1)
    cum = _exact_l((ri >= ci).astype(f32), ld, False)
    tot = jnp.sum(ld, axis=0, keepdims=True)
    w_incl, w_excl, w_inv, w_rest = jnp.exp(cum), jnp.exp(cum - ld), jnp.exp(-cum), jnp.exp(tot - cum)
    ab, rb, bb, kb = a * w_excl, r * w_incl, b * w_inv, k * w_inv
    bw, kw = b * w_rest, k * w_rest
    r2 = lax.broadcasted_iota(jnp.int32, (2 * c, 2 * c), 0)
    c2 = lax.broadcasted_iota(jnp.int32, (2 * c, 2 * c), 1)
    t_of, s_of = jnp.where(r2 >= c, r2 - c, r2), jnp.where(c2 >= c, c2 - c, c2)
    keep = jnp.logical_or(t_of > s_of, jnp.logical_and(r2 >= c, t_of == s_of))
    pair = jnp.where(keep, _hdot_nt(jnp.concatenate([ab, rb], axis=0), jnp.concatenate([bb, kb], axis=0)), 0.0)
    on_b, on_k = _lane_split(pair, (c, c))
    l_ab, m_rb = _row_split(on_b, (c, c))
    l_ak_v, m_rk_v = _row_split(_hdot(on_k, v), (c, c))
    x = jnp.concatenate([ab, l_ak_v], axis=-1)
    lp, span = l_ab, 1
    while span < c:
        x = x + _hdot(lp, x)
        span *= 2
        if span < c:
            lp = _hdot(lp, lp)
    via_b_r, via_b_y = _lane_split(_hdot(m_rb, x), (n, n))
    r_hat = rb + via_b_r
    y0 = via_b_y + m_rk_v
    from_b_g, from_b_z = _row_split(_hdot_tn(x, bw), (n, n))
    eye = lax.broadcasted_iota(jnp.int32, (n, n), 0) == lax.broadcasted_iota(jnp.int32, (n, n), 1)
    g = jnp.where(eye, jnp.exp(tot), 0.0) + from_b_g
    z = from_b_z + _hdot_tn(v, kw)
    return r_hat, y0, g, z


def _head(h):
    return slice(RN * h, RN * (h + 1))


def wkv_fwd(r, ld, k, v, a, b, *, name):
    s_len = r.shape[0]
    c, per = WKV_CHUNK, WKV_CHUNKS_PER_STEP
    n = s_len // c
    rows = pl.BlockSpec((c * per, RW), lambda i: (i, 0))
    mats = pl.BlockSpec((per, RH, RN, RN), lambda i: (i, 0, 0, 0))
    rows_t, mats_t = jax.ShapeDtypeStruct((s_len, RW), f32), jax.ShapeDtypeStruct((n, RH, RN, RN), f32)

    def local_body(r_ref, ld_ref, k_ref, v_ref, a_ref, b_ref, rh_ref, y0_ref, g_ref, z_ref):
        for q in range(per):
            tok = slice(c * q, c * (q + 1))
            rh, y0 = [], []
            for h in range(RH):
                sl = _head(h)
                out = _wkv_local(r_ref[tok, sl], ld_ref[tok, sl], k_ref[tok, sl], v_ref[tok, sl], a_ref[tok, sl],
                                 b_ref[tok, sl])
                rh.append(out[0])
                y0.append(out[1])
                g_ref[q, h] = out[2]
                z_ref[q, h] = out[3]
            rh_ref[tok, :] = jnp.concatenate(rh, axis=-1)
            y0_ref[tok, :] = jnp.concatenate(y0, axis=-1)

    r_hat, y0, g, z = pl.pallas_call(
        local_body, name=name + "_local", grid=(n // per,), in_specs=[rows] * 6, out_specs=[rows, rows, mats, mats],
        out_shape=[rows_t, rows_t, mats_t, mats_t], compiler_params=_cparams(("parallel",)),
    )(r, ld, k, v, a, b)

    def scan_body(g_ref, z_ref, st_ref, s_sc):
        s_sc[...] = jnp.zeros_like(s_sc)

        @pl.loop(0, n)
        def _(i):
            for h in range(RH):
                s0 = s_sc[h]
                st_ref[i, h] = s0
                s_sc[h] = _hdot(s0, g_ref[i, h]) + z_ref[i, h]

    vm = pl.BlockSpec(memory_space=pltpu.VMEM)
    states = pl.pallas_call(
        scan_body, name=name + "_scan", in_specs=[vm, vm], out_specs=vm, out_shape=mats_t,
        scratch_shapes=[pltpu.VMEM((RH, RN, RN), f32)],
        compiler_params=pltpu.CompilerParams(vmem_limit_bytes=VMEM_LIMIT),
    )(g, z)

    def out_body(rh_ref, y0_ref, st_ref, y_ref):
        for q in range(per):
            tok = slice(c * q, c * (q + 1))
            ys = [_hdot_nt(rh_ref[tok, _head(h)], st_ref[q, h]) for h in range(RH)]
            y_ref[tok, :] = jnp.concatenate(ys, axis=-1) + y0_ref[tok, :]

    y = pl.pallas_call(
        out_body, name=name + "_out", grid=(n // per,), in_specs=[rows, rows, mats], out_specs=rows,
        out_shape=rows_t, compiler_params=_cparams(("parallel",)),
    )(r_hat, y0, states)
    return y, dict(r_hat=r_hat, g=g, states=states)


def wkv_bwd(r, ld, k, v, a, b, saved, dy, *, name):
    s_len = r.shape[0]
    c, per = WKV_CHUNK, WKV_CHUNKS_PER_STEP
    n = s_len // c
    rows = pl.BlockSpec((c * per, RW), lambda i: (i, 0))
    mats = pl.BlockSpec((per, RH, RN, RN), lambda i: (i, 0, 0, 0))
    rows_t, mats_t = jax.ShapeDtypeStruct((s_len, RW), f32), jax.ShapeDtypeStruct((n, RH, RN, RN), f32)

    def out_body(dy_ref, rh_ref, st_ref, drh_ref, dsy_ref):
        for q in range(per):
            tok = slice(c * q, c * (q + 1))
            drh = []
            for h in range(RH):
                dyh = dy_ref[tok, _head(h)]
                drh.append(_hdot(dyh, st_ref[q, h]))
                dsy_ref[q, h] = _hdot_tn(dyh, rh_ref[tok, _head(h)])
            drh_ref[tok, :] = jnp.concatenate(drh, axis=-1)

    d_rhat, ds_y = pl.pallas_call(
        out_body, name=name + "_out", grid=(n // per,), in_specs=[rows, rows, mats], out_specs=[rows, mats],
        out_shape=[rows_t, mats_t], compiler_params=_cparams(("parallel",)),
    )(dy, saved["r_hat"], saved["states"])

    def scan_body(dsy_ref, g_ref, st_ref, dg_ref, dz_ref, ds_sc):
        ds_sc[...] = jnp.zeros_like(ds_sc)

        @pl.loop(0, n)
        def _(i):
            cidx = n - 1 - i
            for h in range(RH):
                ds_next = ds_sc[h]
                dz_ref[cidx, h] = ds_next
                dg_ref[cidx, h] = _hdot_tn(st_ref[cidx, h], ds_next)
                ds_sc[h] = dsy_ref[cidx, h] + _hdot_nt(ds_next, g_ref[cidx, h])

    vm = pl.BlockSpec(memory_space=pltpu.VMEM)
    d_g, d_z = pl.pallas_call(
        scan_body, name=name + "_scan", in_specs=[vm, vm, vm], out_specs=[vm, vm], out_shape=[mats_t, mats_t],
        scratch_shapes=[pltpu.VMEM((RH, RN, RN), f32)],
        compiler_params=pltpu.CompilerParams(vmem_limit_bytes=VMEM_LIMIT),
    )(ds_y, saved["g"], saved["states"])

    def local_body(r_ref, ld_ref, k_ref, v_ref, a_ref, b_ref, drh_ref, dy_ref, dg_ref, dz_ref, *out_refs):
        for q in range(per):
            tok = slice(c * q, c * (q + 1))
            outs = [[] for _ in range(6)]
            for h in range(RH):
                sl = _head(h)
                _, vjp = jax.vjp(_wkv_local, r_ref[tok, sl], ld_ref[tok, sl], k_ref[tok, sl], v_ref[tok, sl],
                                 a_ref[tok, sl], b_ref[tok, sl])
                grads = vjp((drh_ref[tok, sl], dy_ref[tok, sl], dg_ref[q, h], dz_ref[q, h]))
                for j in range(6):
                    outs[j].append(grads[j])
            for j, o_ref in enumerate(out_refs):
                o_ref[tok, :] = jnp.concatenate(outs[j], axis=-1)

    return pl.pallas_call(
        local_body, name=name + "_local", grid=(n // per,), in_specs=[rows] * 8 + [mats, mats], out_specs=[rows] * 6,
        out_shape=[rows_t] * 6, compiler_params=_cparams(("parallel",)),
    )(r, ld, k, v, a, b, d_rhat, dy, d_g, d_z)


def loss_head(y, target, *, tile, name):
    s_len, d = y.shape
    n = s_len // tile

    def body(y_ref, t_ref, dy_ref, l_ref):
        err = y_ref[...] - t_ref[...]
        dy_ref[...] = err * (1.0 / d)
        part = 0.5 * jnp.sum(jnp.mean(err * err, axis=-1, keepdims=True), axis=0, keepdims=True)

        @pl.when(pl.program_id(0) == 0)
        def _():
            l_ref[...] = jnp.zeros_like(l_ref)

        l_ref[...] += jnp.broadcast_to(part, l_ref.shape)

    bs = pl.BlockSpec((tile, d), lambda i: (i, 0))
    dy, l = pl.pallas_call(
        body, name=name, grid=(n,), in_specs=[bs, bs],
        out_specs=[bs, pl.BlockSpec((8, 128), lambda i: (0, 0))],
        out_shape=[jax.ShapeDtypeStruct((s_len, d), f32), jax.ShapeDtypeStruct((8, 128), f32)],
        compiler_params=_cparams(("arbitrary",)),
    )(y, target)
    return l[0, 0], dy


def adamw(w, g, m, v, *, name):
    gs = g if isinstance(g, (list, tuple)) else [g]
    rows, cols = w.shape
    tile = _pick(rows, max(8, (2 * 1024 * 1024 // (4 * cols)) // 8 * 8), 8)
    c1 = 1.0 - B1 ** STEP
    c2 = 1.0 - B2 ** STEP
    ng = len(gs)

    def body(*refs):
        w_ref, m_ref, v_ref = refs[0], refs[1 + ng], refs[2 + ng]
        g_out, d_out, m_out, v_out = refs[3 + ng:]
        gv = refs[1][...]
        for q in range(1, ng):
            gv = gv + refs[1 + q][...]
        mn = B1 * m_ref[...] + (1.0 - B1) * gv
        vn = B2 * v_ref[...] + (1.0 - B2) * (gv * gv)
        d_out[...] = -LR * ((mn / c1) / (jnp.sqrt(vn / c2) + EPS) + WD * w_ref[...])
        g_out[...] = gv
        m_out[...] = mn
        v_out[...] = vn

    bs = pl.BlockSpec((tile, cols), lambda i: (i, 0))
    return pl.pallas_call(
        body, name=name, grid=(rows // tile,), in_specs=[bs] * (3 + ng), out_specs=[bs] * 4,
        out_shape=[jax.ShapeDtypeStruct((rows, cols), f32)] * 4, compiler_params=_cparams(("parallel",)),
    )(w, *gs, m, v)


def _place():
    return lax.axis_index("x"), lax.axis_index("y"), lax.axis_index("c")


_ANY = pl.BlockSpec(memory_space=pl.ANY)


def _peer_chips(x, y):
    return [(1 - x, y), (x, 1 - y), (1 - x, 1 - y)]


def gather_weights(shards, *, name):
    nk = len(shards)

    def body(*refs):
        srcs, outs = refs[:nk], refs[nk:2 * nk]
        ici_send, ici_recv, d2d_send, d2d_recv = refs[2 * nk:]
        x, y, c = _place()
        me = 2 * x + y
        peers = _peer_chips(x, y)
        pending = []
        for k in range(nk):
            half = srcs[k].shape[0] // 2
            mine = pl.ds(c * half, half)
            for p, (px, py) in enumerate(peers):
                cp = pltpu.make_async_remote_copy(
                    src_ref=srcs[k].at[mine], dst_ref=outs[k].at[me, mine], send_sem=ici_send.at[k, p],
                    recv_sem=ici_recv.at[k, p], device_id=(px, py, c), device_id_type=MESH)
                cp.start()
                pending.append(cp)
        for k in range(nk):
            half = srcs[k].shape[0] // 2
            mine = pl.ds(c * half, half)
            for p, (px, py) in enumerate(peers):
                landed = outs[k].at[2 * px + py, mine]
                pltpu.make_async_remote_copy(
                    src_ref=srcs[k].at[mine], dst_ref=landed, send_sem=ici_send.at[k, p], recv_sem=ici_recv.at[k, p],
                    device_id=(px, py, c), device_id_type=MESH).wait_recv()
                fwd = pltpu.make_async_remote_copy(
                    src_ref=landed, dst_ref=landed, send_sem=d2d_send.at[k, p], recv_sem=d2d_recv.at[k, p],
                    device_id=(x, y, 1 - c), device_id_type=MESH)
                fwd.start()
                pending.append(fwd)
        for k in range(nk):
            half = srcs[k].shape[0] // 2
            other = pl.ds((1 - c) * half, half)
            for p, (px, py) in enumerate(peers):
                theirs = outs[k].at[2 * px + py, other]
                pltpu.make_async_remote_copy(
                    src_ref=theirs, dst_ref=theirs, send_sem=d2d_send.at[k, p], recv_sem=d2d_recv.at[k, p],
                    device_id=(x, y, 1 - c), device_id_type=MESH).wait_recv()
        for cp in pending:
            cp.wait_send()

    sem = lambda *shape: pltpu.SemaphoreType.DMA(shape)
    return pl.pallas_call(
        body, name=name, in_specs=[_ANY] * nk, out_specs=[_ANY] * nk,
        out_shape=[jax.ShapeDtypeStruct((4,) + s.shape, s.dtype) for s in shards],
        scratch_shapes=[sem(nk, 3), sem(nk, 3), sem(nk, 3), sem(nk, 3)],
    )(*shards)


def grads_to_sibling(parts, *, name):
    nk = len(parts)

    def body(*refs):
        srcs, outs = refs[:nk], refs[nk:2 * nk]
        send_sems, recv_sems = refs[2 * nk:]
        x, y, c = _place()
        sends = []
        for k in range(nk):
            half = srcs[k].shape[1] // 2
            cp = pltpu.make_async_remote_copy(
                src_ref=srcs[k].at[:, pl.ds((1 - c) * half, half), :], dst_ref=outs[k], send_sem=send_sems.at[k],
                recv_sem=recv_sems.at[k], device_id=(x, y, 1 - c), device_id_type=MESH)
            cp.start()
            sends.append(cp)
        for cp in sends:
            cp.wait_recv()
        for cp in sends:
            cp.wait_send()

    return pl.pallas_call(
        body, name=name, in_specs=[_ANY] * nk, out_specs=[_ANY] * nk,
        out_shape=[jax.ShapeDtypeStruct((4, p.shape[1] // 2, p.shape[2]), p.dtype) for p in parts],
        scratch_shapes=[pltpu.SemaphoreType.DMA((nk,)), pltpu.SemaphoreType.DMA((nk,))],
    )(*parts)


def pair_sum(part, theirs, core, *, name):
    _, rows, cols = part.shape
    half = rows // 2
    tile = _pick(half, max(16, (1 << 20) // (4 * cols) // 16 * 16), 16)
    per = half // tile

    def body(c_ref, p_ref, t_ref, o_ref):
        o_ref[...] = (p_ref[...] + t_ref[...]).astype(bf16)

    grid_spec = pltpu.PrefetchScalarGridSpec(
        num_scalar_prefetch=1, grid=(4, per),
        in_specs=[pl.BlockSpec((1, tile, cols), lambda j, i, c_ref: (j, c_ref[0] * per + i, 0)),
                  pl.BlockSpec((1, tile, cols), lambda j, i, c_ref: (j, i, 0))],
        out_specs=pl.BlockSpec((1, tile, cols), lambda j, i, c_ref: (j, i, 0)))
    return pl.pallas_call(
        body, name=name, grid_spec=grid_spec, out_shape=jax.ShapeDtypeStruct((4, half, cols), bf16),
        compiler_params=_cparams(("parallel", "parallel")),
    )(core, part, theirs)


def chips_all_to_all(parts, *, name):
    nk = len(parts)

    def body(*refs):
        srcs, outs = refs[:nk], refs[nk:2 * nk]
        send_sems, recv_sems, local_sems = refs[2 * nk:]
        x, y, c = _place()
        me = 2 * x + y
        peers = _peer_chips(x, y)
        sends, locals_ = [], []
        for k in range(nk):
            local = pltpu.make_async_copy(srcs[k].at[me], outs[k].at[me], local_sems.at[k])
            local.start()
            locals_.append(local)
            for p, (px, py) in enumerate(peers):
                cp = pltpu.make_async_remote_copy(
                    src_ref=srcs[k].at[2 * px + py], dst_ref=outs[k].at[me], send_sem=send_sems.at[k, p],
                    recv_sem=recv_sems.at[k, p], device_id=(px, py, c), device_id_type=MESH)
                cp.start()
                sends.append(cp)
        for k in range(nk):
            for p, (px, py) in enumerate(peers):
                pltpu.make_async_remote_copy(
                    src_ref=srcs[k].at[me], dst_ref=outs[k].at[2 * px + py], send_sem=send_sems.at[k, p],
                    recv_sem=recv_sems.at[k, p], device_id=(px, py, c), device_id_type=MESH).wait_recv()
        for cp in sends:
            cp.wait_send()
        for cp in locals_:
            cp.wait()

    sem = lambda *shape: pltpu.SemaphoreType.DMA(shape)
    return pl.pallas_call(
        body, name=name, in_specs=[_ANY] * nk, out_specs=[_ANY] * nk,
        out_shape=[jax.ShapeDtypeStruct(p.shape, p.dtype) for p in parts],
        scratch_shapes=[sem(nk, 3), sem(nk, 3), sem(nk)],
    )(*parts)


def join_halves(bufs, layout, *, name):
    nk, nb = len(layout), len(bufs)

    def body(*refs):
        outs = refs[nb:2 * nb]
        send_sems, recv_sems = refs[2 * nb:]
        x, y, c = _place()
        pending = []
        for k, (o, off, rows) in enumerate(layout):
            half = rows // 2
            mine = outs[o].at[pl.ds(off + c * half, half), :]
            cp = pltpu.make_async_remote_copy(
                src_ref=mine, dst_ref=mine, send_sem=send_sems.at[k], recv_sem=recv_sems.at[k],
                device_id=(x, y, 1 - c), device_id_type=MESH)
            cp.start()
            pending.append(cp)
        for k, (o, off, rows) in enumerate(layout):
            half = rows // 2
            theirs = outs[o].at[pl.ds(off + (1 - c) * half, half), :]
            pltpu.make_async_remote_copy(
                src_ref=theirs, dst_ref=theirs, send_sem=send_sems.at[k], recv_sem=recv_sems.at[k],
                device_id=(x, y, 1 - c), device_id_type=MESH).wait_recv()
        for cp in pending:
            cp.wait_send()

    return pl.pallas_call(
        body, name=name, in_specs=[_ANY] * nb, out_specs=[_ANY] * nb,
        out_shape=[jax.ShapeDtypeStruct(b.shape, b.dtype) for b in bufs],
        input_output_aliases={o: o for o in range(nb)},
        scratch_shapes=[pltpu.SemaphoreType.DMA((nk,)), pltpu.SemaphoreType.DMA((nk,))],
    )(*bufs)


def place_slab(dest, src, index, *, name):
    rows, cols = src.shape
    tile = _pick(rows, max(16, (1 << 20) // (src.dtype.itemsize * cols) // 16 * 16), 16)

    def body(i_ref, s_ref, d_ref, o_ref):
        del i_ref, d_ref
        o_ref[0] = s_ref[...]

    grid_spec = pltpu.PrefetchScalarGridSpec(
        num_scalar_prefetch=1, grid=(rows // tile,),
        in_specs=[pl.BlockSpec((tile, cols), lambda i, idx: (i, 0)), _ANY],
        out_specs=pl.BlockSpec((1, tile, cols), lambda i, idx: (idx[0], i, 0)))
    return pl.pallas_call(
        body, name=name, grid_spec=grid_spec, out_shape=jax.ShapeDtypeStruct(dest.shape, dest.dtype),
        input_output_aliases={2: 0}, compiler_params=_cparams(("parallel",)),
    )(index, src, dest)


def all_reduce_small(src, *, name):
    rows, cols = src.shape

    def body(s_ref, o_ref, buf, send_sems, recv_sems):
        x, y, c = _place()
        me = 4 * x + 2 * y + c
        buf[me] = s_ref[...]
        sends = []
        for msk in range(1, 8):
            px = x ^ (msk >> 2)
            py = y ^ ((msk >> 1) & 1)
            pc = c ^ (msk & 1)
            cp = pltpu.make_async_remote_copy(
                src_ref=s_ref, dst_ref=buf.at[me], send_sem=send_sems.at[msk - 1], recv_sem=recv_sems.at[msk - 1],
                device_id=(px, py, pc), device_id_type=MESH)
            cp.start()
            sends.append(cp)
        for msk in range(1, 8):
            px = x ^ (msk >> 2)
            py = y ^ ((msk >> 1) & 1)
            pc = c ^ (msk & 1)
            pltpu.make_async_remote_copy(
                src_ref=s_ref, dst_ref=buf.at[4 * px + 2 * py + pc], send_sem=send_sems.at[msk - 1],
                recv_sem=recv_sems.at[msk - 1], device_id=(px, py, pc), device_id_type=MESH).wait_recv()
        for cp in sends:
            cp.wait_send()
        acc = buf[0]
        for d in range(1, 8):
            acc = acc + buf[d]
        o_ref[...] = acc

    vm = pl.BlockSpec(memory_space=pltpu.VMEM)
    return pl.pallas_call(
        body, name=name, in_specs=[vm], out_specs=vm, out_shape=jax.ShapeDtypeStruct((rows, cols), f32),
        scratch_shapes=[pltpu.VMEM((8, rows, cols), f32), pltpu.SemaphoreType.DMA((7,)),
                        pltpu.SemaphoreType.DMA((7,))],
        compiler_params=pltpu.CompilerParams(vmem_limit_bytes=VMEM_LIMIT),
    )(src)


def sum4_into(parts, dest, core, *, layer, total_rows, name):
    _, rows, cols = parts.shape
    tile = _pick(rows, max(16, (1 << 20) // (4 * cols) // 16 * 16), 16)
    per = rows // tile

    def body(c_ref, p_ref, *rest):
        del c_ref
        p = [p_ref[j].astype(f32) for j in range(4)]
        rest[-1][...] = ((p[0] + p[1]) + p[2]) + p[3]

    grid_spec = pltpu.PrefetchScalarGridSpec(
        num_scalar_prefetch=1, grid=(per,),
        in_specs=[pl.BlockSpec((4, tile, cols), lambda i, c_ref: (0, i, 0))] + ([] if dest is None else [_ANY]),
        out_specs=pl.BlockSpec((tile, cols), lambda i, c_ref: ((2 * layer + c_ref[0]) * per + i, 0)))
    return pl.pallas_call(
        body, name=name, grid_spec=grid_spec, out_shape=jax.ShapeDtypeStruct((total_rows, cols), f32),
        input_output_aliases={} if dest is None else {2: 0}, compiler_params=_cparams(("parallel",)),
    )(core, parts, *([] if dest is None else [dest]))


def _consts():
    idx = np.arange(RW)
    bd = (idx[:, None] // RN == idx[None, :] // RN).astype(np.float32)
    place = np.zeros((128, DQK), np.float32)
    place[np.arange(ROPE), NOPE + np.arange(ROPE)] = 1.0
    rot = np.zeros((DQK, DQK), np.float32)
    half = ROPE // 2
    rot[NOPE + half + np.arange(half), NOPE + np.arange(half)] = -1.0
    rot[NOPE + np.arange(half), NOPE + half + np.arange(half)] = 1.0
    return jnp.asarray(bd), jnp.asarray(place), jnp.asarray(rot)


def _rope_tables(positions):
    freqs = ROPE_THETA ** (-(jnp.arange(ROPE // 2, dtype=f32) * 2.0 / ROPE))
    ang = positions.astype(f32)[:, None] * freqs
    cos, sin = jnp.cos(ang), jnp.sin(ang)
    ones = jnp.ones((positions.shape[0], NOPE), f32)
    return (jnp.concatenate([ones, cos, cos], axis=-1), jnp.concatenate([0.0 * ones, sin, sin], axis=-1))


def derive_layer(w, l):
    w_in = w["w_in"][l]
    pad = jnp.zeros((D, MLA_PAD - MLA_COLS), w_in.dtype)
    wkv = w["mla_wkv_b"][l].reshape(KVL, MLA_H, NOPE + DV)
    wk = jnp.concatenate([wkv[:, :, :NOPE], jnp.zeros((KVL, MLA_H, ROPE), wkv.dtype)], axis=-1)
    return dict(
        gate=w_in[:, :GATE], mla=jnp.concatenate([w_in[:, GATE:GATE + MLA_COLS], pad], axis=1),
        rw=w_in[:, GATE + MLA_COLS:GATE + MLA_COLS + 4 * RW], cv=w_in[:, GATE + MLA_COLS + 4 * RW:],
        wq=w["mla_wq_b"][l].reshape(QL, MLA_H, DQK).transpose(1, 0, 2),
        wk=wk.transpose(1, 0, 2), wv=wkv[:, :, NOPE:].transpose(1, 0, 2),
        wo=w["mla_w_o"][l].reshape(MLA_H, DV, D),
        rwo=w["rwkv_w_o"][l], cvo=w["conv_w_o"][l], out=w["w_out"][l], up=w["w_up"][l], down=w["w_down"][l],
    )


W_IN_WINDOW_TILE = (0, 10, 21, 31)
W_IN_WINDOW = 1664
W_IN_SHARD = 1384


def w_in_window_cols(win, chip):
    gap = MLA_PAD - MLA_COLS
    branches = []
    for j in range(4):
        lo, hi = W_IN_SHARD * j, W_IN_SHARD * (j + 1)
        base = 128 * W_IN_WINDOW_TILE[j]
        cut = GATE + MLA_COLS
        if hi <= cut:
            branches.append(lambda w, a=lo - base: w[:, a:a + W_IN_SHARD])
        elif lo >= cut:
            branches.append(lambda w, a=lo + gap - base: w[:, a:a + W_IN_SHARD])
        else:
            branches.append(lambda w, a=lo - base, n1=cut - lo, b=cut + gap - base, n2=hi - cut:
                            jnp.concatenate([w[:, a:a + n1], w[:, b:b + n2]], axis=1))
    return lax.switch(chip, branches, win)


def chip_major_grads(g):
    padded = jnp.concatenate([g["gate"], g["mla"], g["rw"], g["cv"]], axis=1)
    w_in = jnp.stack([padded[:, 128 * t:128 * t + W_IN_WINDOW] for t in W_IN_WINDOW_TILE])
    heads = MLA_H // 4
    wq = g["wq"].reshape(4, heads, QL, DQK).transpose(0, 2, 1, 3).reshape(4, QL, heads * DQK)
    wkv = jnp.concatenate([g["wk"][:, :, :NOPE], g["wv"]], axis=-1)
    wkv = wkv.reshape(4, heads, KVL, NOPE + DV).transpose(0, 2, 1, 3).reshape(4, KVL, heads * (NOPE + DV))
    return dict(
        w_in=w_in, mla_wq_b=wq, mla_wkv_b=wkv, mla_w_o=g["wo"].reshape(4, MLA_H * DV, D // 4),
        rwkv_w_o=g["rwo"], conv_w_o=g["cvo"], w_out=g["out"].reshape(4, D // 4, D), w_up=g["up"],
        w_down=g["down"].reshape(4, DFF // 4, D),
    )


def _row(v):
    return v.reshape(1, -1)


def local_step(x, positions, target, w, sm):
    s_len = x.shape[0]
    t_row = _pick(s_len, 256, 8)
    t_wide = _pick(s_len, 128, 8)
    bd, place, rot = _consts()
    cos, sin = _rope_tables(positions)
    sds = lambda *shape: jax.ShapeDtypeStruct(shape, f32)
    saved = []
    v_first = None
    for l in range(DEPTH):
        tag = f"l{l}_"
        lw = derive_layer(w, l)
        vres = l > 0
        p_norm1 = [_row(sm["attn_norm"][l])]
        (h,) = rows_fwd(_fn_norm, [x], p_norm1, [], [sds(s_len, D)], tile=t_row, name=tag + "norm1")
        gate = mm(h, lw["gate"], name=tag + "proj_gate")
        mla = mm(h, lw["mla"], name=tag + "proj_mla")
        rwc = mm(h, lw["rw"], name=tag + "proj_rwkv")
        cvc = mm(h, lw["cv"], name=tag + "proj_conv")
        p_mla = [_row(sm["mla_q_a_norm"][l]), _row(sm["mla_kv_a_norm"][l])]
        qn, kvn, kpe = rows_fwd(_fn_mla_prep, [mla], p_mla, [], [sds(s_len, QL), sds(s_len, KVL), sds(s_len, 128)],
                                tile=t_row, name=tag + "mla_prep")
        q_raw = mm(qn, lw["wq"], b_batched=True, name=tag + "q_proj")
        kn_pad = mm(kvn, lw["wk"], b_batched=True, name=tag + "k_proj")
        vv = mm(kvn, lw["wv"], b_batched=True, name=tag + "v_proj")
        p_qk = [_row(sm["mla_q_norm"][l]), _row(sm["mla_k_norm"][l])]
        q, k = rows_fwd(_fn_qk_post, [q_raw, kn_pad, kpe, cos, sin], p_qk, [place, rot],
                        [sds(MLA_H, s_len, DQK), sds(MLA_H, s_len, DQK)], tile=t_wide, name=tag + "qk_post")
        o = attn_fwd(q, k, vv, tq=_pick(s_len, 256, 8), name=tag + "attn")
        o_a = mm(o, lw["wo"], a_batched=True, b_batched=True, reduce_batch=True, name=tag + "o_a")
        p_rw = [_row(sm["rwkv_mu"][l]), _row(sm["rwkv_w0"][l]), w["rwkv_w2"][l], _row(sm["rwkv_a0"][l]),
                w["rwkv_a2"][l], w["rwkv_g2"][l], _row(sm["rwkv_k_k"][l]), _row(sm["rwkv_k_a"][l])]
        rw_rows, rw_halos = [rwc], (0,)
        if vres:
            p_rw += [w["rwkv_v1"][l - 1], _row(sm["rwkv_v_mu"][l - 1]), _row(sm["rwkv_v0"][l - 1]), w["rwkv_v2"][l - 1]]
            rw_rows, rw_halos = [rwc, h, v_first], (0, 1)
        fn_prep = _make_fn_rwkv_prep(vres)
        r, ld, k2, v, an, bn, g = rows_fwd(fn_prep, rw_rows, p_rw, [bd], [sds(s_len, RW)] * 7, tile=t_row,
                                           name=tag + "rwkv_prep", halos=rw_halos)
        if not vres:
            v_first = v
        y, states = wkv_fwd(r, ld, k2, v, an, bn, name=tag + "wkv")
        p_post = [_row(sm["rwkv_ln_w"][l]), _row(sm["rwkv_ln_b"][l]), _row(sm["rwkv_r_k"][l])]
        (yb,) = rows_fwd(_fn_rwkv_post, [y, r, k2, v, g], p_post, [bd], [sds(s_len, RW)], tile=t_row,
                         name=tag + "rwkv_post")
        o_b = mm(yb, lw["rwo"], name=tag + "o_b")
        p_cv = [w["conv_w"][l][q:q + 1] for q in range(3)]
        (yc,) = rows_fwd(_fn_conv, [cvc], p_cv, [], [sds(s_len, CW)], tile=t_row, name=tag + "conv", halos=(0,))
        o_c = mm(yc, lw["cvo"], name=tag + "o_c")
        (merged,) = rows_fwd(_fn_merge, [gate, o_a, o_b, o_c], [], [], [sds(s_len, D)], tile=t_wide,
                             name=tag + "merge")
        x1 = mm(merged, lw["out"], add=x, name=tag + "out_proj")
        p_norm2 = [_row(sm["mlp_norm"][l])]
        (h2,) = rows_fwd(_fn_norm, [x1], p_norm2, [], [sds(s_len, D)], tile=t_row, name=tag + "norm2")
        up = mm(h2, lw["up"], name=tag + "up")
        (act,) = rows_fwd(_fn_relu2, [up], [], [], [sds(s_len, DFF)], tile=t_wide, name=tag + "relu2")
        x2 = mm(act, lw["down"], add=x1, name=tag + "down")
        saved.append(dict(lw=lw, x=x, h=h, gate=gate, mla=mla, rwc=rwc, cvc=cvc, qn=qn, kvn=kvn, kpe=kpe,
                          q_raw=q_raw, kn_pad=kn_pad, vv=vv, q=q, k=k, o=o, o_a=o_a, r=r, ld=ld, k2=k2, v=v,
                          an=an, bn=bn, g=g, y=y, states=states, yb=yb, o_b=o_b, yc=yc, o_c=o_c, merged=merged,
                          x1=x1, h2=h2, up=up, act=act, p_norm1=p_norm1, p_mla=p_mla, p_qk=p_qk, p_rw=p_rw,
                          p_post=p_post, p_cv=p_cv, p_norm2=p_norm2, rw_rows=rw_rows, rw_halos=rw_halos,
                          fn_prep=fn_prep, v_first=v_first if vres else None))
        x = x2

    loss, dx = loss_head(x, target, tile=t_row, name="loss_head")

    grads = {n: [None] * (DEPTH - 1 if n in ("rwkv_v1", "rwkv_v_mu", "rwkv_v0", "rwkv_v2") else DEPTH)
             for n in WEIGHTS}
    dv_first = None
    for l in reversed(range(DEPTH)):
        tag = f"b{l}_"
        sv = saved[l]
        lw = sv["lw"]
        vres = l > 0
        g_down = mm(sv["act"], dx, ta=True, name=tag + "g_down")
        dup = mm(dx, lw["down"], tb=True, act_grad=sv["up"], name=tag + "d_up")
        g_up = mm(sv["h2"], dup, ta=True, n_split=4, name=tag + "g_up")
        dh2 = mm(dup, lw["up"], tb=True, name=tag + "d_h2")
        (dx1,), (g_n2,) = rows_bwd(_fn_norm, [sv["x1"]], sv["p_norm2"], [], [[dh2]], tile=t_row,
                                   name=tag + "norm2", extra={0: [dx]})
        g_out = mm(sv["merged"], dx1, ta=True, name=tag + "g_out")
        dmerged = mm(dx1, lw["out"], tb=True, name=tag + "d_merged")
        (dgate, do_a, do_b, do_c), _ = rows_bwd(_fn_merge, [sv["gate"], sv["o_a"], sv["o_b"], sv["o_c"]], [], [],
                                                [[dmerged]], tile=t_wide, name=tag + "merge")
        g_cvo = mm(sv["yc"], do_c, ta=True, n_split=4, name=tag + "g_cvo")
        dyc = mm(do_c, lw["cvo"], tb=True, name=tag + "d_yc")
        (dcvc,), g_cw = rows_bwd(_fn_conv, [sv["cvc"]], sv["p_cv"], [], [[dyc]], tile=t_row, name=tag + "conv",
                                    halos=(0,))
        g_rwo = mm(sv["yb"], do_b, ta=True, n_split=4, name=tag + "g_rwo")
        dyb = mm(do_b, lw["rwo"], tb=True, name=tag + "d_yb")
        (dy, dr_p, dk_p, dv_p, dg), g_post = rows_bwd(
            _fn_rwkv_post, [sv["y"], sv["r"], sv["k2"], sv["v"], sv["g"]], sv["p_post"], [bd], [[dyb]], tile=t_row,
            name=tag + "rwkv_post")
        dr_s, dld, dk_s, dv_s, dan, dbn = wkv_bwd(sv["r"], sv["ld"], sv["k2"], sv["v"], sv["an"], sv["bn"],
                                                  sv["states"], dy, name=tag + "wkv")
        dv_list = [dv_s, dv_p] + ([dv_first] if (not vres and dv_first is not None) else [])
        d_prep, g_prep = rows_bwd(
            sv["fn_prep"], sv["rw_rows"], sv["p_rw"], [bd],
            [[dr_s, dr_p], [dld], [dk_s, dk_p], dv_list, [dan], [dbn], [dg]], tile=t_row, name=tag + "rwkv_prep",
            halos=sv["rw_halos"])
        drwc = d_prep[0]
        dh_extra = []
        if vres:
            dh_extra = [d_prep[1]]
            dv_first = d_prep[2]
        g_wo = mm(sv["o"], do_a, ta=True, a_batched=True, n_split=4, name=tag + "g_wo")
        do = mm(do_a, lw["wo"], tb=True, b_batched=True, name=tag + "d_o")
        dq, dk, dvv = attn_bwd(sv["q"], sv["k"], sv["vv"], do, tq=_pick(s_len, 256, 8), name=tag + "attn")
        (dq_raw, dkn_pad, dkpe), g_qk = rows_bwd(
            _fn_qk_post, [sv["q_raw"], sv["kn_pad"], sv["kpe"], cos, sin], sv["p_qk"], [place, rot], [[dq], [dk]],
            tile=t_wide, name=tag + "qk_post", grad_rows=[0, 1, 2])
        g_wq = mm(sv["qn"], dq_raw, ta=True, b_batched=True, name=tag + "g_wq")
        g_wk = mm(sv["kvn"], dkn_pad, ta=True, b_batched=True, name=tag + "g_wk")
        g_wv = mm(sv["kvn"], dvv, ta=True, b_batched=True, name=tag + "g_wv")
        dqn = mm(dq_raw, lw["wq"], tb=True, a_batched=True, b_batched=True, reduce_batch=True, name=tag + "d_qn")
        dkvn = mm(dkn_pad, lw["wk"], tb=True, a_batched=True, b_batched=True, reduce_batch=True, name=tag + "d_kvn_k")
        dkvn = mm(dvv, lw["wv"], tb=True, a_batched=True, b_batched=True, reduce_batch=True, add=dkvn,
                  name=tag + "d_kvn_v")
        (dmla,), g_mla = rows_bwd(_fn_mla_prep, [sv["mla"]], sv["p_mla"], [], [[dqn], [dkvn], [dkpe]], tile=t_row,
                                  name=tag + "mla_prep")
        g_gate = mm(sv["h"], dgate, ta=True, name=tag + "g_gate")
        g_mlaw = mm(sv["h"], dmla, ta=True, name=tag + "g_mla")
        g_rw = mm(sv["h"], drwc, ta=True, name=tag + "g_rw")
        g_cv = mm(sv["h"], dcvc, ta=True, name=tag + "g_cv")
        dh = mm(dgate, lw["gate"], tb=True, name=tag + "d_h_gate")
        dh = mm(dmla, lw["mla"], tb=True, add=dh, name=tag + "d_h_mla")
        dh = mm(drwc, lw["rw"], tb=True, add=dh, name=tag + "d_h_rw")
        dh = mm(dcvc, lw["cv"], tb=True, add=dh, name=tag + "d_h_cv")
        (dx,), (g_n1,) = rows_bwd(_fn_norm, [sv["x"]], sv["p_norm1"], [], [[dh] + dh_extra], tile=t_row,
                                  name=tag + "norm1", extra={0: [dx1]})
        slabs = chip_major_grads(dict(gate=g_gate, mla=g_mlaw, rw=g_rw, cv=g_cv, wq=g_wq, wk=g_wk, wv=g_wv, wo=g_wo,
                                      rwo=g_rwo, cvo=g_cvo, out=g_out, up=g_up, down=g_down))
        for n, val in slabs.items():
            grads[n][l] = val
        grads["attn_norm"][l], grads["mlp_norm"][l] = g_n1, g_n2
        grads["mla_q_a_norm"][l], grads["mla_kv_a_norm"][l] = g_mla
        grads["mla_q_norm"][l], grads["mla_k_norm"][l] = g_qk
        grads["rwkv_ln_w"][l], grads["rwkv_ln_b"][l], grads["rwkv_r_k"][l] = g_post
        for n, val in zip(["rwkv_mu", "rwkv_w0", "rwkv_w2", "rwkv_a0", "rwkv_a2", "rwkv_g2", "rwkv_k_k", "rwkv_k_a"],
                          g_prep[:8]):
            grads[n][l] = val
        if vres:
            for n, val in zip(["rwkv_v1", "rwkv_v_mu", "rwkv_v0", "rwkv_v2"], g_prep[8:12]):
                grads[n][l - 1] = val
        grads["conv_w"][l] = jnp.concatenate(g_cw, axis=0)
    return loss, dx, grads


def _split3(a):
    hi = a.astype(bf16)
    r1 = a - hi.astype(f32)
    mid = r1.astype(bf16)
    lo = (r1 - mid.astype(f32)).astype(bf16)
    return hi, mid, lo


def _shard_axis(name):
    return 1 if name in ROW_SHARDED else 2


def _pack(pieces, width, dtype, row_align):
    flat = jnp.concatenate([p.reshape(-1).astype(dtype) for p in pieces])
    rows = -(-flat.shape[0] // width)
    rows = -(-rows // row_align) * row_align
    return jnp.pad(flat, (0, rows * width - flat.shape[0])).reshape(rows, width)


def _unpack(flat2d, shapes):
    flat = flat2d.reshape(-1)
    out, off = [], 0
    for shp in shapes:
        n = int(np.prod(shp))
        out.append(flat[off:off + n].reshape(shp))
        off += n
    return out


def kernel(x, positions, attn_norm, w_in, mla_q_a_norm, mla_wq_b, mla_kv_a_norm, mla_wkv_b, mla_q_norm, mla_k_norm, mla_w_o, rwkv_mu, rwkv_w0, rwkv_w2, rwkv_a0, rwkv_a2, rwkv_g2, rwkv_k_k, rwkv_k_a, rwkv_r_k, rwkv_ln_w, rwkv_ln_b, rwkv_w_o, rwkv_v1, rwkv_v_mu, rwkv_v0, rwkv_v2, conv_w, conv_w_o, w_out, mlp_norm, w_up, w_down, loss_target, m_attn_norm, m_w_in, m_mla_q_a_norm, m_mla_wq_b, m_mla_kv_a_norm, m_mla_wkv_b, m_mla_q_norm, m_mla_k_norm, m_mla_w_o, m_rwkv_mu, m_rwkv_w0, m_rwkv_w2, m_rwkv_a0, m_rwkv_a2, m_rwkv_g2, m_rwkv_k_k, m_rwkv_k_a, m_rwkv_r_k, m_rwkv_ln_w, m_rwkv_ln_b, m_rwkv_w_o, m_rwkv_v1, m_rwkv_v_mu, m_rwkv_v0, m_rwkv_v2, m_conv_w, m_conv_w_o, m_w_out, m_mlp_norm, m_w_up, m_w_down, v_attn_norm, v_w_in, v_mla_q_a_norm, v_mla_wq_b, v_mla_kv_a_norm, v_mla_wkv_b, v_mla_q_norm, v_mla_k_norm, v_mla_w_o, v_rwkv_mu, v_rwkv_w0, v_rwkv_w2, v_rwkv_a0, v_rwkv_a2, v_rwkv_g2, v_rwkv_k_k, v_rwkv_k_a, v_rwkv_r_k, v_rwkv_ln_w, v_rwkv_ln_b, v_rwkv_w_o, v_rwkv_v1, v_rwkv_v_mu, v_rwkv_v0, v_rwkv_v2, v_conv_w, v_conv_w_o, v_w_out, v_mlp_norm, v_w_up, v_w_down):
    args = dict(locals())
    wts = {n: args[n] for n in WEIGHTS}
    mom = {n: args["m_" + n] for n in WEIGHTS}
    var = {n: args["v_" + n] for n in WEIGHTS}
    chip = 2 * lax.axis_index("x") + lax.axis_index("y")
    core = lax.axis_index("c").astype(jnp.int32).reshape(1)

    med_names = [n for n in MED if n != "conv_w"]
    med_pieces = [wts[n] for n in med_names] + list(_split3(wts["conv_w"]))
    med_shapes = [p.shape for p in med_pieces]
    shards = [wts[n].astype(bf16).reshape(-1, wts[n].shape[-1]) for n in BIG] + [_pack(med_pieces, 128, bf16, 32)]
    got = gather_weights(shards, name="gather_weights")
    chip_idx = chip.astype(jnp.int32).reshape(1)
    got = [place_slab(g, s, chip_idx, name=f"place_own_{q}") for q, (g, s) in enumerate(zip(got, shards))]
    whole = {}
    for q, n in enumerate(BIG):
        depth, rows, cols = wts[n].shape
        by_chip = got[q].reshape(4, depth, rows, cols)
        if n in ROW_SHARDED:
            whole[n] = by_chip.transpose(1, 0, 2, 3).reshape(depth, 4 * rows, cols)
        else:
            whole[n] = by_chip.transpose(1, 2, 0, 3).reshape(depth, rows, 4 * cols)
    per_chip = [_unpack(got[len(BIG)][j], med_shapes) for j in range(4)]
    for q, n in enumerate(med_names):
        whole[n] = jnp.concatenate([per_chip[j][q] for j in range(4)], axis=_shard_axis(n)).astype(f32)
    base = len(med_names)
    cw_parts = [jnp.concatenate([per_chip[j][base + t] for j in range(4)], axis=2).astype(f32) for t in range(3)]
    whole["conv_w"] = (cw_parts[0] + cw_parts[1]) + cw_parts[2]
    small = {n: wts[n] for n in SMALL}
    small["rwkv_r_k"] = wts["rwkv_r_k"].reshape(DEPTH, RW)

    loss, grad_x, grads = local_step(x[0], positions[0], loss_target[0], whole, small)
    loss = lax.psum(loss, ("x", "y", "c"))

    sm_names = SMALL + MED
    sm_grads = [jnp.stack(grads[n]) for n in sm_names]
    sm_shapes = [g.shape for g in sm_grads]
    sm_sum = _unpack(all_reduce_small(_pack(sm_grads, 128, f32, 8), name="reduce_small"), sm_shapes)
    gsum = {}
    for n, g in zip(sm_names, sm_sum):
        if n in MED:
            ax = _shard_axis(n)
            width = wts[n].shape[ax]
            g = lax.dynamic_slice_in_dim(g, chip * width, width, axis=ax)
        gsum[n] = g.reshape(wts[n].shape)
    slabs = [grads[n][l] for n in BIG for l in range(DEPTH)]
    labels = [f"{n}_{l}" for n in BIG for l in range(DEPTH)]
    from_sibling = grads_to_sibling(slabs, name="grads_to_sibling")
    chip_sums = [pair_sum(s, t, core, name="pair_sum_" + lb) for s, t, lb in zip(slabs, from_sibling, labels)]
    arrived = chips_all_to_all(chip_sums, name="scatter_grads")
    bufs, layout = [], []
    for q in range(len(BIG)):
        rows = slabs[q * DEPTH].shape[1]
        buf = None
        for l in range(DEPTH):
            buf = sum4_into(arrived[q * DEPTH + l], buf, core, layer=l, total_rows=DEPTH * rows,
                            name="sum_chips_" + labels[q * DEPTH + l])
            layout.append((q, l * rows, rows))
        bufs.append(buf)
    reduced = join_halves(bufs, layout, name="join_halves")

    out_g, out_d, out_m, out_v = {}, {}, {}, {}
    for q, n in enumerate(BIG):
        shp = wts[n].shape
        as2d = lambda a: a.reshape(-1, shp[-1])
        g2d = w_in_window_cols(reduced[q], chip) if n == "w_in" else reduced[q]
        res = adamw(as2d(wts[n]), g2d, as2d(mom[n]), as2d(var[n]), name="adamw_" + n)
        out_g[n], out_d[n], out_m[n], out_v[n] = [r.reshape(shp) for r in res]
    sm_all = SMALL + MED
    sm_shapes2 = [wts[n].shape for n in sm_all]
    res = adamw(_pack([wts[n] for n in sm_all], 128, f32, 8), _pack([gsum[n] for n in sm_all], 128, f32, 8),
                _pack([mom[n] for n in sm_all], 128, f32, 8), _pack([var[n] for n in sm_all], 128, f32, 8),
                name="adamw_small")
    for tgt, flat in zip((out_g, out_d, out_m, out_v), res):
        for n, val in zip(sm_all, _unpack(flat, sm_shapes2)):
            tgt[n] = val
    return (loss, grad_x[None], *[out_g[n] for n in WEIGHTS], *[out_d[n] for n in WEIGHTS],
            *[out_m[n] for n in WEIGHTS], *[out_v[n] for n in WEIGHTS])
```

```python
import functools

import jax
import jax.numpy as jnp
import numpy as np
from jax import lax
from jax.experimental import pallas as pl
from jax.experimental.pallas import tpu as pltpu

f32, bf16 = jnp.float32, jnp.bfloat16
HI = lax.Precision.HIGHEST
MESH = pl.DeviceIdType.MESH

D = 1024
DEPTH = 2
MLA_H, NOPE, ROPE, DQK, DV = 8, 64, 32, 96, 64
QL, KVL = 384, 256
RW, RH, RN = 256, 4, 64
DL, AL, GL, MVL = 64, 64, 128, 32
CW = 256
DFF = 4096
GATE = 3 * D
MLA_COLS = QL + KVL + ROPE
MLA_PAD = 768
NORM_EPS = 1e-6
GN_EPS = 64e-5
ROPE_THETA = 10000.0
LR, B1, B2, EPS, WD, STEP = 0.001, 0.9, 0.999, 1e-08, 0.01, 10

VMEM_LIMIT = 52 * 1024 * 1024
WKV_CHUNK = 64
WKV_CHUNKS_PER_STEP = 2

BIG = ["w_in", "mla_wq_b", "mla_wkv_b", "mla_w_o", "rwkv_w_o", "conv_w_o", "w_out", "w_up", "w_down"]
MED = ["rwkv_w2", "rwkv_a2", "rwkv_g2", "rwkv_v1", "rwkv_v2", "conv_w"]
ROW_SHARDED = {"w_out", "w_down", "rwkv_v1"}
SMALL = ["attn_norm", "mla_q_a_norm", "mla_kv_a_norm", "mla_q_norm", "mla_k_norm", "rwkv_mu", "rwkv_w0",
         "rwkv_a0", "rwkv_k_k", "rwkv_k_a", "rwkv_r_k", "rwkv_ln_w", "rwkv_ln_b", "rwkv_v_mu", "rwkv_v0",
         "mlp_norm"]
WEIGHTS = ["attn_norm", "w_in", "mla_q_a_norm", "mla_wq_b", "mla_kv_a_norm", "mla_wkv_b", "mla_q_norm",
           "mla_k_norm", "mla_w_o", "rwkv_mu", "rwkv_w0", "rwkv_w2", "rwkv_a0", "rwkv_a2", "rwkv_g2",
           "rwkv_k_k", "rwkv_k_a", "rwkv_r_k", "rwkv_ln_w", "rwkv_ln_b", "rwkv_w_o", "rwkv_v1", "rwkv_v_mu",
           "rwkv_v0", "rwkv_v2", "conv_w", "conv_w_o", "w_out", "mlp_norm", "w_up", "w_down"]


def _cparams(sem=None):
    return pltpu.CompilerParams(dimension_semantics=sem, vmem_limit_bytes=VMEM_LIMIT)


def _pick(dim, pref, align):
    if dim <= pref:
        return dim
    t = (pref // align) * align
    while t >= align:
        if dim % t == 0:
            return t
        t -= align
    return dim


def _bdot(a, b, dims):
    return lax.dot_general(a.astype(bf16), b.astype(bf16), (dims, ((), ())), preferred_element_type=f32)


@jax.custom_vjp
def _mm(a, b):
    return _bdot(a, b, ((1,), (0,)))


def _mm_fwd(a, b):
    return _mm(a, b), (a, b)


def _mm_bwd(res, g):
    a, b = res
    return _bdot(g, b, ((1,), (1,))), _bdot(a, g, ((0,), (0,)))


_mm.defvjp(_mm_fwd, _mm_bwd)


@jax.custom_vjp
def _mm_nt(a, b):
    return _bdot(a, b, ((1,), (1,)))


def _mm_nt_fwd(a, b):
    return _mm_nt(a, b), (a, b)


def _mm_nt_bwd(res, g):
    a, b = res
    return _bdot(g, b, ((1,), (0,))), _bdot(g, a, ((0,), (0,)))


_mm_nt.defvjp(_mm_nt_fwd, _mm_nt_bwd)


_NN, _NT, _TN = ((1,), (0,)), ((1,), (1,)), ((0,), (0,))


def _dg(a, b, dims):
    return lax.dot_general(a, b, (dims, ((), ())), preferred_element_type=f32)


def _bf16_pieces(x, count):
    out, rest = [], x
    for q in range(count):
        piece = rest.astype(bf16)
        out.append(piece)
        if q + 1 < count:
            rest = rest - piece.astype(f32)
    return out


def _dot3(a, b, dims):
    (ah, al), (bh, bl) = _bf16_pieces(a, 2), _bf16_pieces(b, 2)
    return _dg(ah, bh, dims) + (_dg(ah, bl, dims) + _dg(al, bh, dims))


@jax.custom_vjp
def _hdot(a, b):
    return _dot3(a, b, _NN)


@jax.custom_vjp
def _hdot_nt(a, b):
    return _dot3(a, b, _NT)


@jax.custom_vjp
def _hdot_tn(a, b):
    return _dot3(a, b, _TN)


_hdot.defvjp(lambda a, b: (_hdot(a, b), (a, b)), lambda res, g: (_hdot_nt(g, res[1]), _hdot_tn(res[0], g)))
_hdot_nt.defvjp(lambda a, b: (_hdot_nt(a, b), (a, b)), lambda res, g: (_hdot(g, res[1]), _hdot_tn(g, res[0])))
_hdot_tn.defvjp(lambda a, b: (_hdot_tn(a, b), (a, b)), lambda res, g: (_hdot_nt(res[1], g), _hdot(res[0], g)))


_BNN, _BNT, _BTN = ((2,), (1,)), ((2,), (2,)), ((1,), (1,))


def _bdg(a, b, dims):
    return lax.dot_general(a, b, (dims, ((0,), (0,))), preferred_element_type=f32)


def _bdot3(a, b, dims):
    (ah, al), (bh, bl) = _bf16_pieces(a, 2), _bf16_pieces(b, 2)
    return _bdg(ah, bh, dims) + (_bdg(ah, bl, dims) + _bdg(al, bh, dims))


@jax.custom_vjp
def _hbnn(a, b):
    return _bdot3(a, b, _BNN)


@jax.custom_vjp
def _hbnt(a, b):
    return _bdot3(a, b, _BNT)


@jax.custom_vjp
def _hbtn(a, b):
    return _bdot3(a, b, _BTN)


_hbnn.defvjp(lambda a, b: (_hbnn(a, b), (a, b)), lambda res, g: (_hbnt(g, res[1]), _hbtn(res[0], g)))
_hbnt.defvjp(lambda a, b: (_hbnt(a, b), (a, b)), lambda res, g: (_hbnn(g, res[1]), _hbtn(g, res[0])))
_hbtn.defvjp(lambda a, b: (_hbtn(a, b), (a, b)), lambda res, g: (_hbnt(res[1], g), _hbnn(res[0], g)))


@functools.partial(jax.custom_vjp, nondiff_argnums=(2,))
def _exact_bl(m, x, transposed):
    mb = m.astype(bf16)
    hi, mid, lo = _bf16_pieces(x, 3)
    dims = _BTN if transposed else _BNN
    return (_bdg(mb, hi, dims) + _bdg(mb, mid, dims)) + _bdg(mb, lo, dims)


_exact_bl.defvjp(lambda m, x, transposed: (_exact_bl(m, x, transposed), m),
                 lambda transposed, m, g: (jnp.zeros_like(m), _exact_bl(m, g, not transposed)))


@functools.partial(jax.custom_vjp, nondiff_argnums=(2,))
def _exact_l(m, x, transposed):
    mb = m.astype(bf16)
    hi, mid, lo = _bf16_pieces(x, 3)
    dims = _TN if transposed else _NN
    return (_dg(mb, hi, dims) + _dg(mb, mid, dims)) + _dg(mb, lo, dims)


_exact_l.defvjp(lambda m, x, transposed: (_exact_l(m, x, transposed), m),
                lambda transposed, m, g: (jnp.zeros_like(m), _exact_l(m, g, not transposed)))


@functools.partial(jax.custom_vjp, nondiff_argnums=(2,))
def _exact_r(x, m, transposed):
    mb = m.astype(bf16)
    hi, mid, lo = _bf16_pieces(x, 3)
    dims = _NT if transposed else _NN
    return (_dg(hi, mb, dims) + _dg(mid, mb, dims)) + _dg(lo, mb, dims)


_exact_r.defvjp(lambda x, m, transposed: (_exact_r(x, m, transposed), m),
                lambda transposed, m, g: (_exact_r(g, m, not transposed), jnp.zeros_like(m)))


def _rms(x, g, eps=NORM_EPS):
    return x * lax.rsqrt(jnp.mean(x * x, axis=-1, keepdims=True) + eps) * g


def _sigmoid(x):
    return 1.0 / (1.0 + jnp.exp(-x))


def _softplus(x):
    return jnp.maximum(x, 0.0) + jnp.log(1.0 + jnp.exp(-jnp.maximum(x, -x)))


def _lane_split(x, sizes):
    bounds = np.cumsum([0] + list(sizes))

    @jax.custom_vjp
    def split(v):
        return tuple(v[..., int(bounds[q]):int(bounds[q + 1])] for q in range(len(sizes)))

    split.defvjp(lambda v: (split(v), None), lambda _, g: (jnp.concatenate(g, axis=-1),))
    return split(x)


def _row_split(x, sizes):
    bounds = np.cumsum([0] + list(sizes))

    @jax.custom_vjp
    def split(v):
        return tuple(v[..., int(bounds[q]):int(bounds[q + 1]), :] for q in range(len(sizes)))

    split.defvjp(lambda v: (split(v), None), lambda _, g: (jnp.concatenate(g, axis=-2),))
    return split(x)


def _unstack(x):
    @jax.custom_vjp
    def unstack(v):
        return tuple(v[q] for q in range(v.shape[0]))

    unstack.defvjp(lambda v: (unstack(v), None), lambda _, g: (jnp.stack(g, axis=0),))
    return unstack(x)


def _shift_mats(t, k):
    r = lax.broadcasted_iota(jnp.int32, (t, t), 0)
    c = lax.broadcasted_iota(jnp.int32, (t, t), 1)
    inner = (r - c == k).astype(f32)
    r8 = lax.broadcasted_iota(jnp.int32, (t, 8), 0)
    c8 = lax.broadcasted_iota(jnp.int32, (t, 8), 1)
    edge = (c8 - r8 == 8 - k).astype(f32)
    return inner, edge


def _shift(x, halo, k):
    inner, edge = _shift_mats(x.shape[0], k)
    return _exact_l(inner, x, False) + jnp.dot(edge, halo, precision=HI, preferred_element_type=f32)


def mm(a, b, *, name, ta=False, tb=False, a_batched=False, b_batched=False, reduce_batch=False, add=None,
       act_grad=None, relu2_out=False, n_split=1, out_dtype=f32, tm=1024, tn=1024, tk=1024):
    ash, bsh = a.shape[-2:], b.shape[-2:]
    (k_, m_) = ash if ta else ash[::-1]
    (k2_, n_) = bsh[::-1] if tb else bsh
    assert k_ == k2_, (a.shape, b.shape, ta, tb)
    hb = a.shape[0] if a_batched else (b.shape[0] if b_batched else 1)
    batched_out = (a_batched or b_batched) and not reduce_batch
    h_out = hb if batched_out else 1
    h_red = hb if reduce_batch else 1
    tm = _pick(m_, tm, 128)
    tn = _pick(n_ // n_split, tn, 128)
    tk = _pick(k_, tk, 128)
    nm, nn, nk = m_ // tm, n_ // tn, k_ // tk

    def a_map(i, j, ho, hr, kk):
        blk = (kk, i) if ta else (i, kk)
        return ((ho if batched_out else hr),) + blk if a_batched else blk

    def b_map(i, j, ho, hr, kk):
        blk = (j, kk) if tb else (kk, j)
        return ((ho if batched_out else hr),) + blk if b_batched else blk

    a_blk = (tk, tm) if ta else (tm, tk)
    b_blk = (tn, tk) if tb else (tk, tn)
    in_specs = [pl.BlockSpec(((1,) + a_blk) if a_batched else a_blk, a_map),
                pl.BlockSpec(((1,) + b_blk) if b_batched else b_blk, b_map)]
    args = [a, b]
    for extra in (add, act_grad):
        if extra is not None:
            in_specs.append(pl.BlockSpec((tm, tn), lambda i, j, ho, hr, kk: (i, j)))
            args.append(extra)
    if n_split > 1:
        per = n_ // n_split // tn
        if batched_out:
            out_spec = pl.BlockSpec((1, 1, tm, tn), lambda i, j, ho, hr, kk: (j // per, ho, i, j % per))
            out_shape = jax.ShapeDtypeStruct((n_split, hb, m_, n_ // n_split), out_dtype)
        else:
            out_spec = pl.BlockSpec((1, tm, tn), lambda i, j, ho, hr, kk: (j // per, i, j % per))
            out_shape = jax.ShapeDtypeStruct((n_split, m_, n_ // n_split), out_dtype)
    elif batched_out:
        out_spec = pl.BlockSpec((1, tm, tn), lambda i, j, ho, hr, kk: (ho, i, j))
        out_shape = jax.ShapeDtypeStruct((hb, m_, n_), out_dtype)
    else:
        out_spec = pl.BlockSpec((tm, tn), lambda i, j, ho, hr, kk: (i, j))
        out_shape = jax.ShapeDtypeStruct((m_, n_), out_dtype)
    lead = (0,) * (int(batched_out) + int(n_split > 1))
    dims = ((0,) if ta else (1,), (1,) if tb else (0,))
    has_add, has_act = add is not None, act_grad is not None

    def body(*refs):
        a_ref, b_ref = refs[0], refs[1]
        pos = 2
        add_ref = act_ref = None
        if has_add:
            add_ref = refs[pos]
            pos += 1
        if has_act:
            act_ref = refs[pos]
            pos += 1
        o_ref, acc_ref = refs[pos], refs[-1]
        hr, kk = pl.program_id(3), pl.program_id(4)
        first = jnp.logical_and(hr == 0, kk == 0)
        last = jnp.logical_and(hr == h_red - 1, kk == nk - 1)
        av = a_ref[0] if a_batched else a_ref[...]
        bv = b_ref[0] if b_batched else b_ref[...]
        p = _bdot(av, bv, dims)

        @pl.when(first)
        def _():
            acc_ref[...] = p

        @pl.when(jnp.logical_not(first))
        def _():
            acc_ref[...] += p

        @pl.when(last)
        def _():
            r = acc_ref[...]
            if has_act:
                r = r * (2.0 * jnp.maximum(act_ref[...], 0.0))
            if has_add:
                r = r + add_ref[...]
            if lead:
                o_ref[lead] = r.astype(out_dtype)
            else:
                o_ref[...] = r.astype(out_dtype)
            if relu2_out:
                refs[pos + 1][...] = jnp.square(jnp.maximum(r, 0.0)).astype(bf16)

    if relu2_out:
        assert not lead
        out_spec = [out_spec, out_spec]
        out_shape = [out_shape, jax.ShapeDtypeStruct(out_shape.shape, bf16)]
    return pl.pallas_call(
        body, name=name, grid=(nm, nn, h_out, h_red, nk), in_specs=in_specs, out_specs=out_spec,
        out_shape=out_shape, scratch_shapes=[pltpu.VMEM((tm, tn), f32)],
        compiler_params=_cparams(("parallel", "parallel", "parallel", "arbitrary", "arbitrary")),
    )(*args)


def _row_spec(arr, tile, idx):
    if arr.ndim == 2:
        return pl.BlockSpec((tile, arr.shape[1]), lambda i: (idx(i), 0))
    return pl.BlockSpec((arr.shape[0], tile, arr.shape[2]), lambda i: (0, idx(i), 0))


def _halo_spec(arr, tile, idx):
    per = tile // 8
    return pl.BlockSpec((8, arr.shape[1]), lambda i: (jnp.maximum(idx(i) * per - 1, 0), 0))


def _full_spec(arr):
    nd = arr.ndim
    return pl.BlockSpec(arr.shape, lambda i: (0,) * nd)


def _load_f32(ref):
    val = ref[...]
    return val.astype(f32) if val.dtype == bf16 else val


def rows_fwd(fn, rows, params, consts, out_shapes, *, tile, name, halos=()):
    s_len = rows[0].shape[-2]
    n = s_len // tile
    nr, nh, npar, nc = len(rows), len(halos), len(params), len(consts)
    ident = lambda i: i
    in_specs = ([_row_spec(r, tile, ident) for r in rows] + [_halo_spec(rows[h], tile, ident) for h in halos]
                + [_full_spec(p) for p in params] + [_full_spec(c) for c in consts])
    out_specs = [_row_spec(o, tile, ident) for o in out_shapes]

    def body(*refs):
        i = pl.program_id(0)
        rv = [_load_f32(r) for r in refs[:nr]]
        keep = (i > 0).astype(f32)
        hv = [r[...] * keep for r in refs[nr:nr + nh]]
        pv = [r[...] for r in refs[nr + nh:nr + nh + npar]]
        cv = [r[...] for r in refs[nr + nh + npar:nr + nh + npar + nc]]
        outs = fn(rv, hv, pv, cv)
        for o_ref, o in zip(refs[nr + nh + npar + nc:], outs):
            o_ref[...] = o.astype(o_ref.dtype)

    return pl.pallas_call(
        body, name=name, grid=(n,), in_specs=in_specs, out_specs=out_specs, out_shape=list(out_shapes),
        compiler_params=_cparams(("arbitrary",)),
    )(*rows, *[rows[h] for h in halos], *params, *consts)


def rows_bwd(fn, rows, params, consts, douts, *, tile, name, halos=(), grad_rows=None, extra=None,
             grad_dtypes=None):
    s_len = rows[0].shape[-2]
    n = s_len // tile
    nr, nh, npar, nc = len(rows), len(halos), len(params), len(consts)
    grad_rows = list(range(nr)) if grad_rows is None else list(grad_rows)
    extra = extra or {}
    assert all(h in grad_rows for h in halos)
    rev = lambda i: n - 1 - i
    dflat = [d for ds in douts for d in ds]
    dcount = [len(ds) for ds in douts]
    eflat = [e for g in grad_rows for e in extra.get(g, [])]
    ecount = [len(extra.get(g, [])) for g in grad_rows]
    in_specs = ([_row_spec(r, tile, rev) for r in rows] + [_halo_spec(rows[h], tile, rev) for h in halos]
                + [_full_spec(p) for p in params] + [_full_spec(c) for c in consts]
                + [_row_spec(d, tile, rev) for d in dflat] + [_row_spec(e, tile, rev) for e in eflat])
    grad_dtypes = [f32] * len(grad_rows) if grad_dtypes is None else list(grad_dtypes)
    assert all(grad_dtypes[q] == f32 for q, g in enumerate(grad_rows) if g in halos)
    out_shapes = ([jax.ShapeDtypeStruct(rows[g].shape, dt) for g, dt in zip(grad_rows, grad_dtypes)]
                  + [jax.ShapeDtypeStruct(p.shape, f32) for p in params])
    out_specs = [_row_spec(rows[g], tile, rev) for g in grad_rows] + [_full_spec(p) for p in params]
    scratch = [pltpu.VMEM((8, rows[h].shape[1]), f32) for h in halos]
    n_in = nr + nh + npar + nc + len(dflat) + len(eflat)
    n_out = len(grad_rows) + npar

    def body(*refs):
        i = pl.program_id(0)
        rv = [_load_f32(r) for r in refs[:nr]]
        keep = (i < n - 1).astype(f32)
        hv = [r[...] * keep for r in refs[nr:nr + nh]]
        pv = [r[...] for r in refs[nr + nh:nr + nh + npar]]
        pos = nr + nh + npar
        cv = [r[...] for r in refs[pos:pos + nc]]
        pos += nc
        dv = []
        for cnt in dcount:
            acc = _load_f32(refs[pos])
            for q in range(1, cnt):
                acc = acc + _load_f32(refs[pos + q])
            dv.append(acc)
            pos += cnt
        ev = []
        for cnt in ecount:
            ev.append([_load_f32(refs[pos + q]) for q in range(cnt)])
            pos += cnt
        out_refs = refs[n_in:n_in + n_out]
        carry_refs = refs[n_in + n_out:]

        def f(gr, gh, gp):
            full = list(rv)
            for g, val in zip(grad_rows, gr):
                full[g] = val
            return tuple(fn(full, gh, gp, cv))

        _, vjp = jax.vjp(f, [rv[g] for g in grad_rows], hv, pv)
        d_rows, d_halos, d_params = vjp(tuple(dv))

        @pl.when(i == 0)
        def _():
            for c_ref in carry_refs:
                c_ref[...] = jnp.zeros_like(c_ref)
            for p_ref in out_refs[len(grad_rows):]:
                p_ref[...] = jnp.zeros_like(p_ref)

        for q, g in enumerate(grad_rows):
            val = d_rows[q]
            for e in ev[q]:
                val = val + e
            out_refs[q][...] = val.astype(out_refs[q].dtype)
            if g in halos:
                hq = list(halos).index(g)
                out_refs[q][tile - 8:tile, :] += carry_refs[hq][...]
                carry_refs[hq][...] = d_halos[hq]
        for p_ref, dp in zip(out_refs[len(grad_rows):], d_params):
            p_ref[...] += dp

    res = pl.pallas_call(
        body, name=name, grid=(n,), in_specs=in_specs, out_specs=out_specs, out_shape=out_shapes,
        scratch_shapes=scratch, compiler_params=_cparams(("arbitrary",)),
    )(*rows, *[rows[h] for h in halos], *params, *consts, *dflat, *eflat)
    return list(res[:len(grad_rows)]), list(res[len(grad_rows):])


def _fn_norm(rows, halos, params, consts):
    return (_rms(rows[0], params[0]),)


def _fn_mla_prep(rows, halos, params, consts):
    cq, ckv, kpe = _lane_split(rows[0], (QL, KVL, MLA_PAD - QL - KVL))
    return _rms(cq, params[0]), _rms(ckv, params[1]), kpe


def _rope(x, cos, sin, rot):
    return x * cos + _exact_r(x, rot, False) * sin


def _fn_qk_post(rows, halos, params, consts):
    q_raw, kn_pad, kpe, cos, sin = rows
    q_norm, k_norm = params
    place, rot = consts
    kpe96 = _exact_r(kpe, place, False)
    qs = [_rope(_rms(qh, q_norm), cos, sin, rot) for qh in _unstack(q_raw)]
    ks = [_rope(_rms(kh + kpe96, k_norm), cos, sin, rot) for kh in _unstack(kn_pad)]
    return jnp.stack(qs, axis=0), jnp.stack(ks, axis=0)


def _seg(x, bd):
    return _exact_r(x, bd, False)


def _make_fn_rwkv_prep(vres):
    def fn(rows, halos, params, consts):
        cols = rows[0]
        bd = consts[0]
        mu, w0, w2, a0, a2, g2, k_k, k_a = params[:8]
        prev = _shift(cols, halos[0], 1)
        c = cols + (prev - cols) * mu
        r, k, v, xw, xa, xg = _lane_split(c, (RW, RW, RW, DL, AL, GL))
        log_w = -_softplus(-(w0 + _mm(jnp.tanh(xw), w2))) - 0.5
        ld = -jnp.exp(log_w)
        a = _sigmoid(a0 + _mm(xa, a2))
        g = _mm(_sigmoid(xg), g2)
        if vres:
            hcur, v_first = rows[1], rows[2]
            v1, v_mu, v0, v2 = params[8:12]
            xv = _mm(hcur, v1)
            xv_prev = _shift(xv, _mm(halos[1], v1), 1)
            xv = xv + (xv_prev - xv) * v_mu
            v = v + (v_first - v) * _sigmoid(v0 + _mm(xv, v2))
        kk = k * k_k
        kk = kk / jnp.maximum(jnp.sqrt(_seg(kk * kk, bd)), 1e-12)
        k2 = k * (1.0 + (a - 1.0) * k_a)
        return r, ld, k2, v, -kk, kk * a, g
    return fn


def _fn_rwkv_post(rows, halos, params, consts):
    y, r, k2, v, g = rows
    ln_w, ln_b, r_k = params
    bd = consts[0]
    mean = _seg(y, bd) * (1.0 / RN)
    d = y - mean
    var = _seg(d * d, bd) * (1.0 / RN)
    yn = d * lax.rsqrt(var + GN_EPS) * ln_w + ln_b
    bonus = _seg(r * k2 * r_k, bd) * v
    return ((yn + bonus) * g,)


def _fn_conv(rows, halos, params, consts):
    cols, halo = rows[0], halos[0]
    w0, w1, w2 = params
    b, c, x = _lane_split(cols, (CW, CW, CW))
    _, ch, xh = _lane_split(halo, (CW, CW, CW))
    u, uh = c * x, ch * xh
    return (b * (w0 * _shift(u, uh, 2) + w1 * _shift(u, uh, 1) + w2 * u),)


def _fn_merge(rows, halos, params, consts):
    gate, o_a, o_b, o_c = rows
    g_a, g_b, g_c = _lane_split(gate, (D, D, D))
    return (_sigmoid(g_a) * o_a + _sigmoid(g_b) * o_b + _sigmoid(g_c) * o_c,)


def _attn_block(q, k, v, q0):
    tq, s_len = q.shape[0], k.shape[0]
    s = _mm_nt(q, k) * (DQK ** -0.5)
    row = q0 + lax.broadcasted_iota(jnp.int32, (tq, s_len), 0)
    col = lax.broadcasted_iota(jnp.int32, (tq, s_len), 1)
    s = jnp.where(row >= col, s, -1e30)
    m = lax.stop_gradient(jnp.max(s, axis=-1, keepdims=True))
    e = jnp.exp(s - m)
    p = e / jnp.sum(e, axis=-1, keepdims=True)
    return _mm(p, v)


def attn_fwd(q, k, v, *, tq, name):
    h, s_len, _ = q.shape

    def body(q_ref, k_ref, v_ref, o_ref):
        o_ref[0] = _attn_block(q_ref[0], k_ref[0], v_ref[0], pl.program_id(1) * tq).astype(o_ref.dtype)

    return pl.pallas_call(
        body, name=name, grid=(h, s_len // tq),
        in_specs=[pl.BlockSpec((1, tq, DQK), lambda hh, i: (hh, i, 0)),
                  pl.BlockSpec((1, s_len, DQK), lambda hh, i: (hh, 0, 0)),
                  pl.BlockSpec((1, s_len, DV), lambda hh, i: (hh, 0, 0))],
        out_specs=pl.BlockSpec((1, tq, DV), lambda hh, i: (hh, i, 0)),
        out_shape=jax.ShapeDtypeStruct((h, s_len, DV), bf16),
        compiler_params=_cparams(("parallel", "arbitrary")),
    )(q, k, v)


def attn_bwd(q, k, v, do, *, tq, name):
    h, s_len, _ = q.shape

    def body(q_ref, k_ref, v_ref, do_ref, dq_ref, dk_ref, dv_ref):
        i = pl.program_id(1)
        _, vjp = jax.vjp(functools.partial(_attn_block, q0=i * tq), q_ref[0], k_ref[0], v_ref[0])
        dq, dk, dv = vjp(do_ref[0])
        dq_ref[0] = dq

        @pl.when(i == 0)
        def _():
            dk_ref[0] = dk
            dv_ref[0] = dv

        @pl.when(i > 0)
        def _():
            dk_ref[0] += dk
            dv_ref[0] += dv

    return pl.pallas_call(
        body, name=name, grid=(h, s_len // tq),
        in_specs=[pl.BlockSpec((1, tq, DQK), lambda hh, i: (hh, i, 0)),
                  pl.BlockSpec((1, s_len, DQK), lambda hh, i: (hh, 0, 0)),
                  pl.BlockSpec((1, s_len, DV), lambda hh, i: (hh, 0, 0)),
                  pl.BlockSpec((1, tq, DV), lambda hh, i: (hh, i, 0))],
        out_specs=[pl.BlockSpec((1, tq, DQK), lambda hh, i: (hh, i, 0)),
                   pl.BlockSpec((1, s_len, DQK), lambda hh, i: (hh, 0, 0)),
                   pl.BlockSpec((1, s_len, DV), lambda hh, i: (hh, 0, 0))],
        out_shape=[jax.ShapeDtypeStruct((h, s_len, DQK), f32), jax.ShapeDtypeStruct((h, s_len, DQK), f32),
                   jax.ShapeDtypeStruct((h, s_len, DV), f32)],
        compiler_params=_cparams(("parallel", "arbitrary")),
    )(q, k, v, do)


def _wkv_local(r, ld, k, v, a, b):
    nb, c, n = r.shape
    ri = lax.broadcasted_iota(jnp.int32, (c, c), 0)
    ci = lax.broadcasted_iota(jnp.int32, (c, c), 1)
    tri = jnp.broadcast_to((ri >= ci).astype(f32)[None], (nb, c, c))
    cum = _exact_bl(tri, ld, False)
    tot = jnp.sum(ld, axis=1, keepdims=True)
    w_incl, w_excl, w_inv, w_rest = jnp.exp(cum), jnp.exp(cum - ld), jnp.exp(-cum), jnp.exp(tot - cum)
    ab, rb, bb, kb = a * w_excl, r * w_incl, b * w_inv, k * w_inv
    bw, kw = b * w_rest, k * w_rest
    r2 = lax.broadcasted_iota(jnp.int32, (2 * c, 2 * c), 0)
    c2 = lax.broadcasted_iota(jnp.int32, (2 * c, 2 * c), 1)
    t_of, s_of = jnp.where(r2 >= c, r2 - c, r2), jnp.where(c2 >= c, c2 - c, c2)
    keep = jnp.logical_or(t_of > s_of, jnp.logical_and(r2 >= c, t_of == s_of))
    pair = jnp.where(keep[None], _hbnt(jnp.concatenate([ab, rb], axis=1), jnp.concatenate([bb, kb], axis=1)), 0.0)
    on_b, on_k = _lane_split(pair, (c, c))
    l_ab, m_rb = _row_split(on_b, (c, c))
    l_ak_v, m_rk_v = _row_split(_hbnn(on_k, v), (c, c))
    x = jnp.concatenate([ab, l_ak_v], axis=-1)
    lp, span = l_ab, 1
    while span < c:
        x = x + _hbnn(lp, x)
        span *= 2
        if span < c:
            lp = _hbnn(lp, lp)
    via_b_r, via_b_y = _lane_split(_hbnn(m_rb, x), (n, n))
    r_hat = rb + via_b_r
    y0 = via_b_y + m_rk_v
    from_b_g, from_b_z = _row_split(_hbtn(x, bw), (n, n))
    eye = lax.broadcasted_iota(jnp.int32, (n, n), 0) == lax.broadcasted_iota(jnp.int32, (n, n), 1)
    g = jnp.where(eye[None], jnp.exp(tot), 0.0) + from_b_g
    z = from_b_z + _hbtn(v, kw)
    return r_hat, y0, g, z


def _head(h):
    return slice(RN * h, RN * (h + 1))


def _load_chunk_heads(ref, c, per):
    return jnp.stack([ref[c * q:c * (q + 1), _head(h)] for q in range(per) for h in range(RH)], axis=0)


def _store_chunk_heads(ref, val, c, per):
    for q in range(per):
        ref[c * q:c * (q + 1), :] = jnp.concatenate([val[q * RH + h] for h in range(RH)], axis=-1)


def wkv_fwd(r, ld, k, v, a, b, *, name):
    s_len = r.shape[0]
    c, per = WKV_CHUNK, WKV_CHUNKS_PER_STEP
    n = s_len // c
    rows = pl.BlockSpec((c * per, RW), lambda i: (i, 0))
    mats = pl.BlockSpec((per, RH, RN, RN), lambda i: (i, 0, 0, 0))
    rows_t, mats_t = jax.ShapeDtypeStruct((s_len, RW), f32), jax.ShapeDtypeStruct((n, RH, RN, RN), f32)

    def local_body(r_ref, ld_ref, k_ref, v_ref, a_ref, b_ref, rh_ref, y0_ref, g_ref, z_ref):
        r_hat, y0, g, z = _wkv_local(*[_load_chunk_heads(ref, c, per)
                                       for ref in (r_ref, ld_ref, k_ref, v_ref, a_ref, b_ref)])
        _store_chunk_heads(rh_ref, r_hat, c, per)
        _store_chunk_heads(y0_ref, y0, c, per)
        g_ref[...] = g.reshape(per, RH, RN, RN)
        z_ref[...] = z.reshape(per, RH, RN, RN)

    r_hat, y0, g, z = pl.pallas_call(
        local_body, name=name + "_local", grid=(n // per,), in_specs=[rows] * 6, out_specs=[rows, rows, mats, mats],
        out_shape=[rows_t, rows_t, mats_t, mats_t], compiler_params=_cparams(("parallel",)),
    )(r, ld, k, v, a, b)

    def scan_body(g_ref, z_ref, st_ref, s_sc):
        s_sc[...] = jnp.zeros_like(s_sc)

        @pl.loop(0, n)
        def _(i):
            s0 = s_sc[...]
            st_ref[i] = s0
            s_sc[...] = _hbnn(s0, g_ref[i]) + z_ref[i]

    vm = pl.BlockSpec(memory_space=pltpu.VMEM)
    states = pl.pallas_call(
        scan_body, name=name + "_scan", in_specs=[vm, vm], out_specs=vm, out_shape=mats_t,
        scratch_shapes=[pltpu.VMEM((RH, RN, RN), f32)],
        compiler_params=pltpu.CompilerParams(vmem_limit_bytes=VMEM_LIMIT),
    )(g, z)

    def out_body(rh_ref, y0_ref, st_ref, y_ref):
        y = _hbnt(_load_chunk_heads(rh_ref, c, per), st_ref[...].reshape(per * RH, RN, RN))
        _store_chunk_heads(y_ref, y, c, per)
        y_ref[...] += y0_ref[...]

    y = pl.pallas_call(
        out_body, name=name + "_out", grid=(n // per,), in_specs=[rows, rows, mats], out_specs=rows,
        out_shape=rows_t, compiler_params=_cparams(("parallel",)),
    )(r_hat, y0, states)
    return y, dict(r_hat=r_hat, g=g, states=states)


def wkv_bwd(r, ld, k, v, a, b, saved, dy, *, name):
    s_len = r.shape[0]
    c, per = WKV_CHUNK, WKV_CHUNKS_PER_STEP
    n = s_len // c
    rows = pl.BlockSpec((c * per, RW), lambda i: (i, 0))
    mats = pl.BlockSpec((per, RH, RN, RN), lambda i: (i, 0, 0, 0))
    rows_t, mats_t = jax.ShapeDtypeStruct((s_len, RW), f32), jax.ShapeDtypeStruct((n, RH, RN, RN), f32)

    def out_body(dy_ref, rh_ref, st_ref, drh_ref, dsy_ref):
        dyb = _load_chunk_heads(dy_ref, c, per)
        _store_chunk_heads(drh_ref, _hbnn(dyb, st_ref[...].reshape(per * RH, RN, RN)), c, per)
        dsy_ref[...] = _hbtn(dyb, _load_chunk_heads(rh_ref, c, per)).reshape(per, RH, RN, RN)

    d_rhat, ds_y = pl.pallas_call(
        out_body, name=name + "_out", grid=(n // per,), in_specs=[rows, rows, mats], out_specs=[rows, mats],
        out_shape=[rows_t, mats_t], compiler_params=_cparams(("parallel",)),
    )(dy, saved["r_hat"], saved["states"])

    def scan_body(dsy_ref, g_ref, st_ref, dg_ref, dz_ref, ds_sc):
        ds_sc[...] = jnp.zeros_like(ds_sc)

        @pl.loop(0, n)
        def _(i):
            cidx = n - 1 - i
            ds_next = ds_sc[...]
            dz_ref[cidx] = ds_next
            dg_ref[cidx] = _hbtn(st_ref[cidx], ds_next)
            ds_sc[...] = dsy_ref[cidx] + _hbnt(ds_next, g_ref[cidx])

    vm = pl.BlockSpec(memory_space=pltpu.VMEM)
    d_g, d_z = pl.pallas_call(
        scan_body, name=name + "_scan", in_specs=[vm, vm, vm], out_specs=[vm, vm], out_shape=[mats_t, mats_t],
        scratch_shapes=[pltpu.VMEM((RH, RN, RN), f32)],
        compiler_params=pltpu.CompilerParams(vmem_limit_bytes=VMEM_LIMIT),
    )(ds_y, saved["g"], saved["states"])

    def local_body(r_ref, ld_ref, k_ref, v_ref, a_ref, b_ref, drh_ref, dy_ref, dg_ref, dz_ref, *out_refs):
        _, vjp = jax.vjp(_wkv_local, *[_load_chunk_heads(ref, c, per)
                                       for ref in (r_ref, ld_ref, k_ref, v_ref, a_ref, b_ref)])
        grads = vjp((_load_chunk_heads(drh_ref, c, per), _load_chunk_heads(dy_ref, c, per),
                     dg_ref[...].reshape(per * RH, RN, RN), dz_ref[...].reshape(per * RH, RN, RN)))
        for o_ref, val in zip(out_refs, grads):
            _store_chunk_heads(o_ref, val, c, per)

    return pl.pallas_call(
        local_body, name=name + "_local", grid=(n // per,), in_specs=[rows] * 8 + [mats, mats], out_specs=[rows] * 6,
        out_shape=[rows_t] * 6, compiler_params=_cparams(("parallel",)),
    )(r, ld, k, v, a, b, d_rhat, dy, d_g, d_z)


def loss_head(y, target, *, tile, name):
    s_len, d = y.shape
    n = s_len // tile

    def body(y_ref, t_ref, dy_ref, l_ref):
        err = y_ref[...] - t_ref[...]
        dy_ref[...] = err * (1.0 / d)
        part = 0.5 * jnp.sum(jnp.mean(err * err, axis=-1, keepdims=True), axis=0, keepdims=True)

        @pl.when(pl.program_id(0) == 0)
        def _():
            l_ref[...] = jnp.zeros_like(l_ref)

        l_ref[...] += jnp.broadcast_to(part, l_ref.shape)

    bs = pl.BlockSpec((tile, d), lambda i: (i, 0))
    dy, l = pl.pallas_call(
        body, name=name, grid=(n,), in_specs=[bs, bs],
        out_specs=[bs, pl.BlockSpec((8, 128), lambda i: (0, 0))],
        out_shape=[jax.ShapeDtypeStruct((s_len, d), f32), jax.ShapeDtypeStruct((8, 128), f32)],
        compiler_params=_cparams(("arbitrary",)),
    )(y, target)
    return l[0, 0], dy


def adamw(w, g, m, v, *, name):
    gs = g if isinstance(g, (list, tuple)) else [g]
    rows, cols = w.shape
    tile = _pick(rows, max(8, (2 * 1024 * 1024 // (4 * cols)) // 8 * 8), 8)
    c1 = 1.0 - B1 ** STEP
    c2 = 1.0 - B2 ** STEP
    ng = len(gs)

    def body(*refs):
        w_ref, m_ref, v_ref = refs[0], refs[1 + ng], refs[2 + ng]
        g_out, d_out, m_out, v_out = refs[3 + ng:]
        gv = refs[1][...]
        for q in range(1, ng):
            gv = gv + refs[1 + q][...]
        mn = B1 * m_ref[...] + (1.0 - B1) * gv
        vn = B2 * v_ref[...] + (1.0 - B2) * (gv * gv)
        d_out[...] = -LR * ((mn / c1) / (jnp.sqrt(vn / c2) + EPS) + WD * w_ref[...])
        g_out[...] = gv
        m_out[...] = mn
        v_out[...] = vn

    bs = pl.BlockSpec((tile, cols), lambda i: (i, 0))
    return pl.pallas_call(
        body, name=name, grid=(rows // tile,), in_specs=[bs] * (3 + ng), out_specs=[bs] * 4,
        out_shape=[jax.ShapeDtypeStruct((rows, cols), f32)] * 4, compiler_params=_cparams(("parallel",)),
    )(w, *gs, m, v)


def _place():
    return lax.axis_index("x"), lax.axis_index("y"), lax.axis_index("c")


_ANY = pl.BlockSpec(memory_space=pl.ANY)


def _peer_chips(x, y):
    return [(1 - x, y), (x, 1 - y), (1 - x, 1 - y)]


def gather_weights(shards, *, name):
    nk = len(shards)

    def body(*refs):
        srcs, outs = refs[:nk], refs[nk:2 * nk]
        ici_send, ici_recv, d2d_send, d2d_recv = refs[2 * nk:]
        x, y, c = _place()
        me = 2 * x + y
        peers = _peer_chips(x, y)
        pending = []
        for k in range(nk):
            half = srcs[k].shape[0] // 2
            mine = pl.ds(c * half, half)
            for p, (px, py) in enumerate(peers):
                cp = pltpu.make_async_remote_copy(
                    src_ref=srcs[k].at[mine], dst_ref=outs[k].at[me, mine], send_sem=ici_send.at[k, p],
                    recv_sem=ici_recv.at[k, p], device_id=(px, py, c), device_id_type=MESH)
                cp.start()
                pending.append(cp)
        for k in range(nk):
            half = srcs[k].shape[0] // 2
            mine = pl.ds(c * half, half)
            for p, (px, py) in enumerate(peers):
                landed = outs[k].at[2 * px + py, mine]
                pltpu.make_async_remote_copy(
                    src_ref=srcs[k].at[mine], dst_ref=landed, send_sem=ici_send.at[k, p], recv_sem=ici_recv.at[k, p],
                    device_id=(px, py, c), device_id_type=MESH).wait_recv()
                fwd = pltpu.make_async_remote_copy(
                    src_ref=landed, dst_ref=landed, send_sem=d2d_send.at[k, p], recv_sem=d2d_recv.at[k, p],
                    device_id=(x, y, 1 - c), device_id_type=MESH)
                fwd.start()
                pending.append(fwd)
        for k in range(nk):
            half = srcs[k].shape[0] // 2
            other = pl.ds((1 - c) * half, half)
            for p, (px, py) in enumerate(peers):
                theirs = outs[k].at[2 * px + py, other]
                pltpu.make_async_remote_copy(
                    src_ref=theirs, dst_ref=theirs, send_sem=d2d_send.at[k, p], recv_sem=d2d_recv.at[k, p],
                    device_id=(x, y, 1 - c), device_id_type=MESH).wait_recv()
        for cp in pending:
            cp.wait_send()

    sem = lambda *shape: pltpu.SemaphoreType.DMA(shape)
    return pl.pallas_call(
        body, name=name, in_specs=[_ANY] * nk, out_specs=[_ANY] * nk,
        out_shape=[jax.ShapeDtypeStruct((4,) + s.shape, s.dtype) for s in shards],
        scratch_shapes=[sem(nk, 3), sem(nk, 3), sem(nk, 3), sem(nk, 3)],
    )(*shards)


def grads_to_sibling(parts, *, name):
    nk = len(parts)

    def body(*refs):
        srcs, outs = refs[:nk], refs[nk:2 * nk]
        send_sems, recv_sems = refs[2 * nk:]
        x, y, c = _place()
        sends = []
        for k in range(nk):
            half = srcs[k].shape[1] // 2
            cp = pltpu.make_async_remote_copy(
                src_ref=srcs[k].at[:, pl.ds((1 - c) * half, half), :], dst_ref=outs[k], send_sem=send_sems.at[k],
                recv_sem=recv_sems.at[k], device_id=(x, y, 1 - c), device_id_type=MESH)
            cp.start()
            sends.append(cp)
        for cp in sends:
            cp.wait_recv()
        for cp in sends:
            cp.wait_send()

    return pl.pallas_call(
        body, name=name, in_specs=[_ANY] * nk, out_specs=[_ANY] * nk,
        out_shape=[jax.ShapeDtypeStruct((4, p.shape[1] // 2, p.shape[2]), p.dtype) for p in parts],
        scratch_shapes=[pltpu.SemaphoreType.DMA((nk,)), pltpu.SemaphoreType.DMA((nk,))],
    )(*parts)


def pair_sum(part, theirs, core, *, name):
    _, rows, cols = part.shape
    half = rows // 2
    tile = _pick(half, max(16, (1 << 20) // (4 * cols) // 16 * 16), 16)
    per = half // tile

    def body(c_ref, p_ref, t_ref, o_ref):
        o_ref[...] = (p_ref[...] + t_ref[...]).astype(bf16)

    grid_spec = pltpu.PrefetchScalarGridSpec(
        num_scalar_prefetch=1, grid=(4, per),
        in_specs=[pl.BlockSpec((1, tile, cols), lambda j, i, c_ref: (j, c_ref[0] * per + i, 0)),
                  pl.BlockSpec((1, tile, cols), lambda j, i, c_ref: (j, i, 0))],
        out_specs=pl.BlockSpec((1, tile, cols), lambda j, i, c_ref: (j, i, 0)))
    return pl.pallas_call(
        body, name=name, grid_spec=grid_spec, out_shape=jax.ShapeDtypeStruct((4, half, cols), bf16),
        compiler_params=_cparams(("parallel", "parallel")),
    )(core, part, theirs)


def chips_all_to_all(parts, *, name):
    nk = len(parts)

    def body(*refs):
        srcs, outs = refs[:nk], refs[nk:2 * nk]
        send_sems, recv_sems, local_sems = refs[2 * nk:]
        x, y, c = _place()
        me = 2 * x + y
        peers = _peer_chips(x, y)
        sends, locals_ = [], []
        for k in range(nk):
            local = pltpu.make_async_copy(srcs[k].at[me], outs[k].at[me], local_sems.at[k])
            local.start()
            locals_.append(local)
            for p, (px, py) in enumerate(peers):
                cp = pltpu.make_async_remote_copy(
                    src_ref=srcs[k].at[2 * px + py], dst_ref=outs[k].at[me], send_sem=send_sems.at[k, p],
                    recv_sem=recv_sems.at[k, p], device_id=(px, py, c), device_id_type=MESH)
                cp.start()
                sends.append(cp)
        for k in range(nk):
            for p, (px, py) in enumerate(peers):
                pltpu.make_async_remote_copy(
                    src_ref=srcs[k].at[me], dst_ref=outs[k].at[2 * px + py], send_sem=send_sems.at[k, p],
                    recv_sem=recv_sems.at[k, p], device_id=(px, py, c), device_id_type=MESH).wait_recv()
        for cp in sends:
            cp.wait_send()
        for cp in locals_:
            cp.wait()

    sem = lambda *shape: pltpu.SemaphoreType.DMA(shape)
    return pl.pallas_call(
        body, name=name, in_specs=[_ANY] * nk, out_specs=[_ANY] * nk,
        out_shape=[jax.ShapeDtypeStruct(p.shape, p.dtype) for p in parts],
        scratch_shapes=[sem(nk, 3), sem(nk, 3), sem(nk)],
    )(*parts)


def join_halves(bufs, layout, *, name):
    nk, nb = len(layout), len(bufs)

    def body(*refs):
        outs = refs[nb:2 * nb]
        send_sems, recv_sems = refs[2 * nb:]
        x, y, c = _place()
        pending = []
        for k, (o, off, rows) in enumerate(layout):
            half = rows // 2
            mine = outs[o].at[pl.ds(off + c * half, half), :]
            cp = pltpu.make_async_remote_copy(
                src_ref=mine, dst_ref=mine, send_sem=send_sems.at[k], recv_sem=recv_sems.at[k],
                device_id=(x, y, 1 - c), device_id_type=MESH)
            cp.start()
            pending.append(cp)
        for k, (o, off, rows) in enumerate(layout):
            half = rows // 2
            theirs = outs[o].at[pl.ds(off + (1 - c) * half, half), :]
            pltpu.make_async_remote_copy(
                src_ref=theirs, dst_ref=theirs, send_sem=send_sems.at[k], recv_sem=recv_sems.at[k],
                device_id=(x, y, 1 - c), device_id_type=MESH).wait_recv()
        for cp in pending:
            cp.wait_send()

    return pl.pallas_call(
        body, name=name, in_specs=[_ANY] * nb, out_specs=[_ANY] * nb,
        out_shape=[jax.ShapeDtypeStruct(b.shape, b.dtype) for b in bufs],
        input_output_aliases={o: o for o in range(nb)},
        scratch_shapes=[pltpu.SemaphoreType.DMA((nk,)), pltpu.SemaphoreType.DMA((nk,))],
    )(*bufs)


def place_slab(dest, src, index, *, name):
    rows, cols = src.shape
    tile = _pick(rows, max(16, (1 << 20) // (src.dtype.itemsize * cols) // 16 * 16), 16)

    def body(i_ref, s_ref, d_ref, o_ref):
        del i_ref, d_ref
        o_ref[0] = s_ref[...]

    grid_spec = pltpu.PrefetchScalarGridSpec(
        num_scalar_prefetch=1, grid=(rows // tile,),
        in_specs=[pl.BlockSpec((tile, cols), lambda i, idx: (i, 0)), _ANY],
        out_specs=pl.BlockSpec((1, tile, cols), lambda i, idx: (idx[0], i, 0)))
    return pl.pallas_call(
        body, name=name, grid_spec=grid_spec, out_shape=jax.ShapeDtypeStruct(dest.shape, dest.dtype),
        input_output_aliases={2: 0}, compiler_params=_cparams(("parallel",)),
    )(index, src, dest)


def all_reduce_small(src, *, name):
    rows, cols = src.shape

    def body(s_ref, o_ref, buf, send_sems, recv_sems):
        x, y, c = _place()
        me = 4 * x + 2 * y + c
        buf[me] = s_ref[...]
        sends = []
        for msk in range(1, 8):
            px = x ^ (msk >> 2)
            py = y ^ ((msk >> 1) & 1)
            pc = c ^ (msk & 1)
            cp = pltpu.make_async_remote_copy(
                src_ref=s_ref, dst_ref=buf.at[me], send_sem=send_sems.at[msk - 1], recv_sem=recv_sems.at[msk - 1],
                device_id=(px, py, pc), device_id_type=MESH)
            cp.start()
            sends.append(cp)
        for msk in range(1, 8):
            px = x ^ (msk >> 2)
            py = y ^ ((msk >> 1) & 1)
            pc = c ^ (msk & 1)
            pltpu.make_async_remote_copy(
                src_ref=s_ref, dst_ref=buf.at[4 * px + 2 * py + pc], send_sem=send_sems.at[msk - 1],
                recv_sem=recv_sems.at[msk - 1], device_id=(px, py, pc), device_id_type=MESH).wait_recv()
        for cp in sends:
            cp.wait_send()
        acc = buf[0]
        for d in range(1, 8):
            acc = acc + buf[d]
        o_ref[...] = acc

    vm = pl.BlockSpec(memory_space=pltpu.VMEM)
    return pl.pallas_call(
        body, name=name, in_specs=[vm], out_specs=vm, out_shape=jax.ShapeDtypeStruct((rows, cols), f32),
        scratch_shapes=[pltpu.VMEM((8, rows, cols), f32), pltpu.SemaphoreType.DMA((7,)),
                        pltpu.SemaphoreType.DMA((7,))],
        compiler_params=pltpu.CompilerParams(vmem_limit_bytes=VMEM_LIMIT),
    )(src)


def sum4_into(parts, dest, core, *, layer, total_rows, name):
    _, rows, cols = parts.shape
    tile = _pick(rows, max(16, (1 << 20) // (4 * cols) // 16 * 16), 16)
    per = rows // tile

    def body(c_ref, p_ref, *rest):
        del c_ref
        p = [p_ref[j].astype(f32) for j in range(4)]
        rest[-1][...] = ((p[0] + p[1]) + p[2]) + p[3]

    grid_spec = pltpu.PrefetchScalarGridSpec(
        num_scalar_prefetch=1, grid=(per,),
        in_specs=[pl.BlockSpec((4, tile, cols), lambda i, c_ref: (0, i, 0))] + ([] if dest is None else [_ANY]),
        out_specs=pl.BlockSpec((tile, cols), lambda i, c_ref: ((2 * layer + c_ref[0]) * per + i, 0)))
    return pl.pallas_call(
        body, name=name, grid_spec=grid_spec, out_shape=jax.ShapeDtypeStruct((total_rows, cols), f32),
        input_output_aliases={} if dest is None else {2: 0}, compiler_params=_cparams(("parallel",)),
    )(core, parts, *([] if dest is None else [dest]))


def _consts():
    idx = np.arange(RW)
    bd = (idx[:, None] // RN == idx[None, :] // RN).astype(np.float32)
    place = np.zeros((128, DQK), np.float32)
    place[np.arange(ROPE), NOPE + np.arange(ROPE)] = 1.0
    rot = np.zeros((DQK, DQK), np.float32)
    half = ROPE // 2
    rot[NOPE + half + np.arange(half), NOPE + np.arange(half)] = -1.0
    rot[NOPE + np.arange(half), NOPE + half + np.arange(half)] = 1.0
    return jnp.asarray(bd), jnp.asarray(place), jnp.asarray(rot)


def _rope_tables(positions):
    freqs = ROPE_THETA ** (-(jnp.arange(ROPE // 2, dtype=f32) * 2.0 / ROPE))
    ang = positions.astype(f32)[:, None] * freqs
    cos, sin = jnp.cos(ang), jnp.sin(ang)
    ones = jnp.ones((positions.shape[0], NOPE), f32)
    return (jnp.concatenate([ones, cos, cos], axis=-1), jnp.concatenate([0.0 * ones, sin, sin], axis=-1))


def derive_layer(w, l):
    w_in = w["w_in"][l]
    pad = jnp.zeros((D, MLA_PAD - MLA_COLS), w_in.dtype)
    wkv = w["mla_wkv_b"][l].reshape(KVL, MLA_H, NOPE + DV)
    wk = jnp.concatenate([wkv[:, :, :NOPE], jnp.zeros((KVL, MLA_H, ROPE), wkv.dtype)], axis=-1)
    return dict(
        gate=w_in[:, :GATE], mla=jnp.concatenate([w_in[:, GATE:GATE + MLA_COLS], pad], axis=1),
        rw=w_in[:, GATE + MLA_COLS:GATE + MLA_COLS + 4 * RW], cv=w_in[:, GATE + MLA_COLS + 4 * RW:],
        wq=w["mla_wq_b"][l].reshape(QL, MLA_H, DQK).transpose(1, 0, 2),
        wk=wk.transpose(1, 0, 2), wv=wkv[:, :, NOPE:].transpose(1, 0, 2),
        wo=w["mla_w_o"][l].reshape(MLA_H, DV, D),
        rwo=w["rwkv_w_o"][l], cvo=w["conv_w_o"][l], out=w["w_out"][l], up=w["w_up"][l], down=w["w_down"][l],
    )


W_IN_WINDOW_TILE = (0, 10, 21, 31)
W_IN_WINDOW = 1664
W_IN_SHARD = 1384


def w_in_window_cols(win, chip):
    gap = MLA_PAD - MLA_COLS
    branches = []
    for j in range(4):
        lo, hi = W_IN_SHARD * j, W_IN_SHARD * (j + 1)
        base = 128 * W_IN_WINDOW_TILE[j]
        cut = GATE + MLA_COLS
        if hi <= cut:
            branches.append(lambda w, a=lo - base: w[:, a:a + W_IN_SHARD])
        elif lo >= cut:
            branches.append(lambda w, a=lo + gap - base: w[:, a:a + W_IN_SHARD])
        else:
            branches.append(lambda w, a=lo - base, n1=cut - lo, b=cut + gap - base, n2=hi - cut:
                            jnp.concatenate([w[:, a:a + n1], w[:, b:b + n2]], axis=1))
    return lax.switch(chip, branches, win)


def chip_major_grads(g):
    padded = jnp.concatenate([g["gate"], g["mla"], g["rw"], g["cv"]], axis=1)
    w_in = jnp.stack([padded[:, 128 * t:128 * t + W_IN_WINDOW] for t in W_IN_WINDOW_TILE])
    heads = MLA_H // 4
    wq = g["wq"].reshape(4, heads, QL, DQK).transpose(0, 2, 1, 3).reshape(4, QL, heads * DQK)
    wkv = jnp.concatenate([g["wk"][:, :, :NOPE], g["wv"]], axis=-1)
    wkv = wkv.reshape(4, heads, KVL, NOPE + DV).transpose(0, 2, 1, 3).reshape(4, KVL, heads * (NOPE + DV))
    return dict(
        w_in=w_in, mla_wq_b=wq, mla_wkv_b=wkv, mla_w_o=g["wo"].reshape(4, MLA_H * DV, D // 4),
        rwkv_w_o=g["rwo"], conv_w_o=g["cvo"], w_out=g["out"].reshape(4, D // 4, D), w_up=g["up"],
        w_down=g["down"].reshape(4, DFF // 4, D),
    )


def _row(v):
    return v.reshape(1, -1)


def local_step(x, positions, target, w, sm):
    s_len = x.shape[0]
    t_row = _pick(s_len, 256, 8)
    t_wide = _pick(s_len, 128, 8)
    bd, place, rot = _consts()
    cos, sin = _rope_tables(positions)
    sds = lambda *shape: jax.ShapeDtypeStruct(shape, f32)
    sdb = lambda *shape: jax.ShapeDtypeStruct(shape, bf16)
    saved = []
    v_first = None
    for l in range(DEPTH):
        tag = f"l{l}_"
        lw = derive_layer(w, l)
        vres = l > 0
        p_norm1 = [_row(sm["attn_norm"][l])]
        (h,) = rows_fwd(_fn_norm, [x], p_norm1, [], [sds(s_len, D)], tile=t_row, name=tag + "norm1")
        gate = mm(h, lw["gate"], name=tag + "proj_gate")
        mla = mm(h, lw["mla"], name=tag + "proj_mla")
        rwc = mm(h, lw["rw"], name=tag + "proj_rwkv")
        cvc = mm(h, lw["cv"], name=tag + "proj_conv")
        p_mla = [_row(sm["mla_q_a_norm"][l]), _row(sm["mla_kv_a_norm"][l])]
        qn, kvn, kpe = rows_fwd(_fn_mla_prep, [mla], p_mla, [], [sdb(s_len, QL), sdb(s_len, KVL), sds(s_len, 128)],
                                tile=t_row, name=tag + "mla_prep")
        q_raw = mm(qn, lw["wq"], b_batched=True, name=tag + "q_proj")
        kn_pad = mm(kvn, lw["wk"], b_batched=True, name=tag + "k_proj")
        vv = mm(kvn, lw["wv"], b_batched=True, name=tag + "v_proj")
        p_qk = [_row(sm["mla_q_norm"][l]), _row(sm["mla_k_norm"][l])]
        q, k = rows_fwd(_fn_qk_post, [q_raw, kn_pad, kpe, cos, sin], p_qk, [place, rot],
                        [sds(MLA_H, s_len, DQK), sds(MLA_H, s_len, DQK)], tile=t_wide, name=tag + "qk_post")
        o = attn_fwd(q, k, vv, tq=_pick(s_len, 256, 8), name=tag + "attn")
        o_a = mm(o, lw["wo"], a_batched=True, b_batched=True, reduce_batch=True, name=tag + "o_a")
        p_rw = [_row(sm["rwkv_mu"][l]), _row(sm["rwkv_w0"][l]), w["rwkv_w2"][l], _row(sm["rwkv_a0"][l]),
                w["rwkv_a2"][l], w["rwkv_g2"][l], _row(sm["rwkv_k_k"][l]), _row(sm["rwkv_k_a"][l])]
        rw_rows, rw_halos = [rwc], (0,)
        if vres:
            p_rw += [w["rwkv_v1"][l - 1], _row(sm["rwkv_v_mu"][l - 1]), _row(sm["rwkv_v0"][l - 1]), w["rwkv_v2"][l - 1]]
            rw_rows, rw_halos = [rwc, h, v_first], (0, 1)
        fn_prep = _make_fn_rwkv_prep(vres)
        r, ld, k2, v, an, bn, g = rows_fwd(fn_prep, rw_rows, p_rw, [bd], [sds(s_len, RW)] * 7, tile=t_row,
                                           name=tag + "rwkv_prep", halos=rw_halos)
        if not vres:
            v_first = v
        y, states = wkv_fwd(r, ld, k2, v, an, bn, name=tag + "wkv")
        p_post = [_row(sm["rwkv_ln_w"][l]), _row(sm["rwkv_ln_b"][l]), _row(sm["rwkv_r_k"][l])]
        (yb,) = rows_fwd(_fn_rwkv_post, [y, r, k2, v, g], p_post, [bd], [sdb(s_len, RW)], tile=t_row,
                         name=tag + "rwkv_post")
        o_b = mm(yb, lw["rwo"], name=tag + "o_b")
        p_cv = [w["conv_w"][l][q:q + 1] for q in range(3)]
        (yc,) = rows_fwd(_fn_conv, [cvc], p_cv, [], [sdb(s_len, CW)], tile=t_row, name=tag + "conv", halos=(0,))
        o_c = mm(yc, lw["cvo"], name=tag + "o_c")
        (merged,) = rows_fwd(_fn_merge, [gate, o_a, o_b, o_c], [], [], [sdb(s_len, D)], tile=t_wide,
                             name=tag + "merge")
        x1 = mm(merged, lw["out"], add=x, name=tag + "out_proj")
        p_norm2 = [_row(sm["mlp_norm"][l])]
        (h2,) = rows_fwd(_fn_norm, [x1], p_norm2, [], [sdb(s_len, D)], tile=t_row, name=tag + "norm2")
        up, act = mm(h2, lw["up"], relu2_out=True, name=tag + "up")
        x2 = mm(act, lw["down"], add=x1, name=tag + "down")
        saved.append(dict(lw=lw, x=x, h=h, gate=gate, mla=mla, rwc=rwc, cvc=cvc, qn=qn, kvn=kvn, kpe=kpe,
                          q_raw=q_raw, kn_pad=kn_pad, vv=vv, q=q, k=k, o=o, o_a=o_a, r=r, ld=ld, k2=k2, v=v,
                          an=an, bn=bn, g=g, y=y, states=states, yb=yb, o_b=o_b, yc=yc, o_c=o_c, merged=merged,
                          x1=x1, h2=h2, up=up, act=act, p_norm1=p_norm1, p_mla=p_mla, p_qk=p_qk, p_rw=p_rw,
                          p_post=p_post, p_cv=p_cv, p_norm2=p_norm2, rw_rows=rw_rows, rw_halos=rw_halos,
                          fn_prep=fn_prep, v_first=v_first if vres else None))
        x = x2

    loss, dx = loss_head(x, target, tile=t_row, name="loss_head")

    grads = {n: [None] * (DEPTH - 1 if n in ("rwkv_v1", "rwkv_v_mu", "rwkv_v0", "rwkv_v2") else DEPTH)
             for n in WEIGHTS}
    dv_first = None
    for l in reversed(range(DEPTH)):
        tag = f"b{l}_"
        sv = saved[l]
        lw = sv["lw"]
        vres = l > 0
        g_down = mm(sv["act"], dx, ta=True, name=tag + "g_down")
        dup = mm(dx, lw["down"], tb=True, act_grad=sv["up"], out_dtype=bf16, name=tag + "d_up")
        g_up = mm(sv["h2"], dup, ta=True, n_split=4, name=tag + "g_up")
        dh2 = mm(dup, lw["up"], tb=True, name=tag + "d_h2")
        (dx1,), (g_n2,) = rows_bwd(_fn_norm, [sv["x1"]], sv["p_norm2"], [], [[dh2]], tile=t_row,
                                   name=tag + "norm2", extra={0: [dx]})
        g_out = mm(sv["merged"], dx1, ta=True, name=tag + "g_out")
        dmerged = mm(dx1, lw["out"], tb=True, name=tag + "d_merged")
        (dgate, do_a, do_b, do_c), _ = rows_bwd(_fn_merge, [sv["gate"], sv["o_a"], sv["o_b"], sv["o_c"]], [], [],
                                                [[dmerged]], tile=t_wide, name=tag + "merge",
                                                grad_dtypes=[bf16] * 4)
        g_cvo = mm(sv["yc"], do_c, ta=True, n_split=4, name=tag + "g_cvo")
        dyc = mm(do_c, lw["cvo"], tb=True, name=tag + "d_yc")
        (dcvc,), g_cw = rows_bwd(_fn_conv, [sv["cvc"]], sv["p_cv"], [], [[dyc]], tile=t_row, name=tag + "conv",
                                    halos=(0,))
        g_rwo = mm(sv["yb"], do_b, ta=True, n_split=4, name=tag + "g_rwo")
        dyb = mm(do_b, lw["rwo"], tb=True, name=tag + "d_yb")
        (dy, dr_p, dk_p, dv_p, dg), g_post = rows_bwd(
            _fn_rwkv_post, [sv["y"], sv["r"], sv["k2"], sv["v"], sv["g"]], sv["p_post"], [bd], [[dyb]], tile=t_row,
            name=tag + "rwkv_post")
        dr_s, dld, dk_s, dv_s, dan, dbn = wkv_bwd(sv["r"], sv["ld"], sv["k2"], sv["v"], sv["an"], sv["bn"],
                                                  sv["states"], dy, name=tag + "wkv")
        dv_list = [dv_s, dv_p] + ([dv_first] if (not vres and dv_first is not None) else [])
        d_prep, g_prep = rows_bwd(
            sv["fn_prep"], sv["rw_rows"], sv["p_rw"], [bd],
            [[dr_s, dr_p], [dld], [dk_s, dk_p], dv_list, [dan], [dbn], [dg]], tile=t_row, name=tag + "rwkv_prep",
            halos=sv["rw_halos"])
        drwc = d_prep[0]
        dh_extra = []
        if vres:
            dh_extra = [d_prep[1]]
            dv_first = d_prep[2]
        g_wo = mm(sv["o"], do_a, ta=True, a_batched=True, n_split=4, tk=s_len, name=tag + "g_wo")
        do = mm(do_a, lw["wo"], tb=True, b_batched=True, name=tag + "d_o")
        dq, dk, dvv = attn_bwd(sv["q"], sv["k"], sv["vv"], do, tq=_pick(s_len, 256, 8), name=tag + "attn")
        (dq_raw, dkn_pad, dkpe), g_qk = rows_bwd(
            _fn_qk_post, [sv["q_raw"], sv["kn_pad"], sv["kpe"], cos, sin], sv["p_qk"], [place, rot], [[dq], [dk]],
            tile=t_wide, name=tag + "qk_post", grad_rows=[0, 1, 2], grad_dtypes=[bf16, bf16, f32])
        g_wq = mm(sv["qn"], dq_raw, ta=True, b_batched=True, tk=s_len, name=tag + "g_wq")
        g_wk = mm(sv["kvn"], dkn_pad, ta=True, b_batched=True, tk=s_len, name=tag + "g_wk")
        g_wv = mm(sv["kvn"], dvv, ta=True, b_batched=True, tk=s_len, name=tag + "g_wv")
        dqn = mm(dq_raw, lw["wq"], tb=True, a_batched=True, b_batched=True, reduce_batch=True, name=tag + "d_qn")
        dkvn = mm(dkn_pad, lw["wk"], tb=True, a_batched=True, b_batched=True, reduce_batch=True, name=tag + "d_kvn_k")
        dkvn = mm(dvv, lw["wv"], tb=True, a_batched=True, b_batched=True, reduce_batch=True, add=dkvn,
                  name=tag + "d_kvn_v")
        (dmla,), g_mla = rows_bwd(_fn_mla_prep, [sv["mla"]], sv["p_mla"], [], [[dqn], [dkvn], [dkpe]], tile=t_row,
                                  name=tag + "mla_prep", grad_dtypes=[bf16])
        g_gate = mm(sv["h"], dgate, ta=True, name=tag + "g_gate")
        g_mlaw = mm(sv["h"], dmla, ta=True, name=tag + "g_mla")
        g_rw = mm(sv["h"], drwc, ta=True, name=tag + "g_rw")
        g_cv = mm(sv["h"], dcvc, ta=True, name=tag + "g_cv")
        dh = mm(dgate, lw["gate"], tb=True, name=tag + "d_h_gate")
        dh = mm(dmla, lw["mla"], tb=True, add=dh, name=tag + "d_h_mla")
        dh = mm(drwc, lw["rw"], tb=True, add=dh, name=tag + "d_h_rw")
        dh = mm(dcvc, lw["cv"], tb=True, add=dh, name=tag + "d_h_cv")
        (dx,), (g_n1,) = rows_bwd(_fn_norm, [sv["x"]], sv["p_norm1"], [], [[dh] + dh_extra], tile=t_row,
                                  name=tag + "norm1", extra={0: [dx1]})
        slabs = chip_major_grads(dict(gate=g_gate, mla=g_mlaw, rw=g_rw, cv=g_cv, wq=g_wq, wk=g_wk, wv=g_wv, wo=g_wo,
                                      rwo=g_rwo, cvo=g_cvo, out=g_out, up=g_up, down=g_down))
        for n, val in slabs.items():
            grads[n][l] = val
        grads["attn_norm"][l], grads["mlp_norm"][l] = g_n1, g_n2
        grads["mla_q_a_norm"][l], grads["mla_kv_a_norm"][l] = g_mla
        grads["mla_q_norm"][l], grads["mla_k_norm"][l] = g_qk
        grads["rwkv_ln_w"][l], grads["rwkv_ln_b"][l], grads["rwkv_r_k"][l] = g_post
        for n, val in zip(["rwkv_mu", "rwkv_w0", "rwkv_w2", "rwkv_a0", "rwkv_a2", "rwkv_g2", "rwkv_k_k", "rwkv_k_a"],
                          g_prep[:8]):
            grads[n][l] = val
        if vres:
            for n, val in zip(["rwkv_v1", "rwkv_v_mu", "rwkv_v0", "rwkv_v2"], g_prep[8:12]):
                grads[n][l - 1] = val
        grads["conv_w"][l] = jnp.concatenate(g_cw, axis=0)
    return loss, dx, grads


def _split3(a):
    hi = a.astype(bf16)
    r1 = a - hi.astype(f32)
    mid = r1.astype(bf16)
    lo = (r1 - mid.astype(f32)).astype(bf16)
    return hi, mid, lo


def _shard_axis(name):
    return 1 if name in ROW_SHARDED else 2


def _pack(pieces, width, dtype, row_align):
    flat = jnp.concatenate([p.reshape(-1).astype(dtype) for p in pieces])
    rows = -(-flat.shape[0] // width)
    rows = -(-rows // row_align) * row_align
    return jnp.pad(flat, (0, rows * width - flat.shape[0])).reshape(rows, width)


def _unpack(flat2d, shapes):
    flat = flat2d.reshape(-1)
    out, off = [], 0
    for shp in shapes:
        n = int(np.prod(shp))
        out.append(flat[off:off + n].reshape(shp))
        off += n
    return out


def kernel(x, positions, attn_norm, w_in, mla_q_a_norm, mla_wq_b, mla_kv_a_norm, mla_wkv_b, mla_q_norm, mla_k_norm, mla_w_o, rwkv_mu, rwkv_w0, rwkv_w2, rwkv_a0, rwkv_a2, rwkv_g2, rwkv_k_k, rwkv_k_a, rwkv_r_k, rwkv_ln_w, rwkv_ln_b, rwkv_w_o, rwkv_v1, rwkv_v_mu, rwkv_v0, rwkv_v2, conv_w, conv_w_o, w_out, mlp_norm, w_up, w_down, loss_target, m_attn_norm, m_w_in, m_mla_q_a_norm, m_mla_wq_b, m_mla_kv_a_norm, m_mla_wkv_b, m_mla_q_norm, m_mla_k_norm, m_mla_w_o, m_rwkv_mu, m_rwkv_w0, m_rwkv_w2, m_rwkv_a0, m_rwkv_a2, m_rwkv_g2, m_rwkv_k_k, m_rwkv_k_a, m_rwkv_r_k, m_rwkv_ln_w, m_rwkv_ln_b, m_rwkv_w_o, m_rwkv_v1, m_rwkv_v_mu, m_rwkv_v0, m_rwkv_v2, m_conv_w, m_conv_w_o, m_w_out, m_mlp_norm, m_w_up, m_w_down, v_attn_norm, v_w_in, v_mla_q_a_norm, v_mla_wq_b, v_mla_kv_a_norm, v_mla_wkv_b, v_mla_q_norm, v_mla_k_norm, v_mla_w_o, v_rwkv_mu, v_rwkv_w0, v_rwkv_w2, v_rwkv_a0, v_rwkv_a2, v_rwkv_g2, v_rwkv_k_k, v_rwkv_k_a, v_rwkv_r_k, v_rwkv_ln_w, v_rwkv_ln_b, v_rwkv_w_o, v_rwkv_v1, v_rwkv_v_mu, v_rwkv_v0, v_rwkv_v2, v_conv_w, v_conv_w_o, v_w_out, v_mlp_norm, v_w_up, v_w_down):
    args = dict(locals())
    wts = {n: args[n] for n in WEIGHTS}
    mom = {n: args["m_" + n] for n in WEIGHTS}
    var = {n: args["v_" + n] for n in WEIGHTS}
    chip = 2 * lax.axis_index("x") + lax.axis_index("y")
    core = lax.axis_index("c").astype(jnp.int32).reshape(1)

    med_names = [n for n in MED if n != "conv_w"]
    med_pieces = [wts[n] for n in med_names] + list(_split3(wts["conv_w"]))
    med_shapes = [p.shape for p in med_pieces]
    shards = [wts[n].astype(bf16).reshape(-1, wts[n].shape[-1]) for n in BIG] + [_pack(med_pieces, 128, bf16, 32)]
    got = gather_weights(shards, name="gather_weights")
    chip_idx = chip.astype(jnp.int32).reshape(1)
    got = [place_slab(g, s, chip_idx, name=f"place_own_{q}") for q, (g, s) in enumerate(zip(got, shards))]
    whole = {}
    for q, n in enumerate(BIG):
        depth, rows, cols = wts[n].shape
        by_chip = got[q].reshape(4, depth, rows, cols)
        if n in ROW_SHARDED:
            whole[n] = by_chip.transpose(1, 0, 2, 3).reshape(depth, 4 * rows, cols)
        else:
            whole[n] = by_chip.transpose(1, 2, 0, 3).reshape(depth, rows, 4 * cols)
    per_chip = [_unpack(got[len(BIG)][j], med_shapes) for j in range(4)]
    for q, n in enumerate(med_names):
        whole[n] = jnp.concatenate([per_chip[j][q] for j in range(4)], axis=_shard_axis(n)).astype(f32)
    base = len(med_names)
    cw_parts = [jnp.concatenate([per_chip[j][base + t] for j in range(4)], axis=2).astype(f32) for t in range(3)]
    whole["conv_w"] = (cw_parts[0] + cw_parts[1]) + cw_parts[2]
    small = {n: wts[n] for n in SMALL}
    small["rwkv_r_k"] = wts["rwkv_r_k"].reshape(DEPTH, RW)

    loss, grad_x, grads = local_step(x[0], positions[0], loss_target[0], whole, small)
    loss = lax.psum(loss, ("x", "y", "c"))

    sm_names = SMALL + MED
    sm_grads = [jnp.stack(grads[n]) for n in sm_names]
    sm_shapes = [g.shape for g in sm_grads]
    sm_sum = _unpack(all_reduce_small(_pack(sm_grads, 128, f32, 8), name="reduce_small"), sm_shapes)
    gsum = {}
    for n, g in zip(sm_names, sm_sum):
        if n in MED:
            ax = _shard_axis(n)
            width = wts[n].shape[ax]
            g = lax.dynamic_slice_in_dim(g, chip * width, width, axis=ax)
        gsum[n] = g.reshape(wts[n].shape)
    slabs = [grads[n][l] for n in BIG for l in range(DEPTH)]
    labels = [f"{n}_{l}" for n in BIG for l in range(DEPTH)]
    from_sibling = grads_to_sibling(slabs, name="grads_to_sibling")
    chip_sums = [pair_sum(s, t, core, name="pair_sum_" + lb) for s, t, lb in zip(slabs, from_sibling, labels)]
    arrived = chips_all_to_all(chip_sums, name="scatter_grads")
    bufs, layout = [], []
    for q in range(len(BIG)):
        rows = slabs[q * DEPTH].shape[1]
        buf = None
        for l in range(DEPTH):
            buf = sum4_into(arrived[q * DEPTH + l], buf, core, layer=l, total_rows=DEPTH * rows,
                            name="sum_chips_" + labels[q * DEPTH + l])
            layout.append((q, l * rows, rows))
        bufs.append(buf)
    reduced = join_halves(bufs, layout, name="join_halves")

    out_g, out_d, out_m, out_v = {}, {}, {}, {}
    for q, n in enumerate(BIG):
        shp = wts[n].shape
        as2d = lambda a: a.reshape(-1, shp[-1])
        g2d = w_in_window_cols(reduced[q], chip) if n == "w_in" else reduced[q]
        res = adamw(as2d(wts[n]), g2d, as2d(mom[n]), as2d(var[n]), name="adamw_" + n)
        out_g[n], out_d[n], out_m[n], out_v[n] = [r.reshape(shp) for r in res]
    sm_all = SMALL + MED
    sm_shapes2 = [wts[n].shape for n in sm_all]
    res = adamw(_pack([wts[n] for n in sm_all], 128, f32, 8), _pack([gsum[n] for n in sm_all], 128, f32, 8),
                _pack([mom[n] for n in sm_all], 128, f32, 8), _pack([var[n] for n in sm_all], 128, f32, 8),
                name="adamw_small")
    for tgt, flat in zip((out_g, out_d, out_m, out_v), res):
        for n, val in zip(sm_all, _unpack(flat, sm_shapes2)):
            tgt[n] = val
    return (loss, grad_x[None], *[out_g[n] for n in WEIGHTS], *[out_d[n] for n in WEIGHTS],
            *[out_m[n] for n in WEIGHTS], *[out_v[n] for n in WEIGHTS])
```

```python
import functools

import jax
import jax.numpy as jnp
import numpy as np
from jax import lax
from jax.experimental import pallas as pl
from jax.experimental.pallas import tpu as pltpu

f32, bf16 = jnp.float32, jnp.bfloat16
HI = lax.Precision.HIGHEST
MESH = pl.DeviceIdType.MESH

D = 1024
DEPTH = 2
MLA_H, NOPE, ROPE, DQK, DV = 8, 64, 32, 96, 64
QL, KVL = 384, 256
RW, RH, RN = 256, 4, 64
DL, AL, GL, MVL = 64, 64, 128, 32
CW = 256
DFF = 4096
GATE = 3 * D
MLA_COLS = QL + KVL + ROPE
MLA_PAD = 768
NORM_EPS = 1e-6
GN_EPS = 64e-5
ROPE_THETA = 10000.0
LR, B1, B2, EPS, WD, STEP = 0.001, 0.9, 0.999, 1e-08, 0.01, 10

VMEM_LIMIT = 52 * 1024 * 1024
WKV_CHUNK = 64
WKV_CHUNKS_PER_STEP = 2
ATTN_SEGMENTS = 4

BIG = ["w_in", "mla_wq_b", "mla_wkv_b", "mla_w_o", "rwkv_w_o", "conv_w_o", "w_out", "w_up", "w_down"]
MED = ["rwkv_w2", "rwkv_a2", "rwkv_g2", "rwkv_v1", "rwkv_v2", "conv_w"]
ROW_SHARDED = {"w_out", "w_down", "rwkv_v1"}
SMALL = ["attn_norm", "mla_q_a_norm", "mla_kv_a_norm", "mla_q_norm", "mla_k_norm", "rwkv_mu", "rwkv_w0",
         "rwkv_a0", "rwkv_k_k", "rwkv_k_a", "rwkv_r_k", "rwkv_ln_w", "rwkv_ln_b", "rwkv_v_mu", "rwkv_v0",
         "mlp_norm"]
WEIGHTS = ["attn_norm", "w_in", "mla_q_a_norm", "mla_wq_b", "mla_kv_a_norm", "mla_wkv_b", "mla_q_norm",
           "mla_k_norm", "mla_w_o", "rwkv_mu", "rwkv_w0", "rwkv_w2", "rwkv_a0", "rwkv_a2", "rwkv_g2",
           "rwkv_k_k", "rwkv_k_a", "rwkv_r_k", "rwkv_ln_w", "rwkv_ln_b", "rwkv_w_o", "rwkv_v1", "rwkv_v_mu",
           "rwkv_v0", "rwkv_v2", "conv_w", "conv_w_o", "w_out", "mlp_norm", "w_up", "w_down"]


def _cparams(sem=None):
    return pltpu.CompilerParams(dimension_semantics=sem, vmem_limit_bytes=VMEM_LIMIT)


def _pick(dim, pref, align):
    if dim <= pref:
        return dim
    t = (pref // align) * align
    while t >= align:
        if dim % t == 0:
            return t
        t -= align
    return dim


def _bdot(a, b, dims):
    return lax.dot_general(a.astype(bf16), b.astype(bf16), (dims, ((), ())), preferred_element_type=f32)


@jax.custom_vjp
def _mm(a, b):
    return _bdot(a, b, ((1,), (0,)))


def _mm_fwd(a, b):
    return _mm(a, b), (a, b)


def _mm_bwd(res, g):
    a, b = res
    return _bdot(g, b, ((1,), (1,))), _bdot(a, g, ((0,), (0,)))


_mm.defvjp(_mm_fwd, _mm_bwd)


@jax.custom_vjp
def _mm_nt(a, b):
    return _bdot(a, b, ((1,), (1,)))


def _mm_nt_fwd(a, b):
    return _mm_nt(a, b), (a, b)


def _mm_nt_bwd(res, g):
    a, b = res
    return _bdot(g, b, ((1,), (0,))), _bdot(g, a, ((0,), (0,)))


_mm_nt.defvjp(_mm_nt_fwd, _mm_nt_bwd)


_NN, _NT, _TN = ((1,), (0,)), ((1,), (1,)), ((0,), (0,))


def _dg(a, b, dims):
    return lax.dot_general(a, b, (dims, ((), ())), preferred_element_type=f32)


def _bf16_pieces(x, count):
    out, rest = [], x
    for q in range(count):
        piece = rest.astype(bf16)
        out.append(piece)
        if q + 1 < count:
            rest = rest - piece.astype(f32)
    return out


def _dot3(a, b, dims):
    (ah, al), (bh, bl) = _bf16_pieces(a, 2), _bf16_pieces(b, 2)
    return _dg(ah, bh, dims) + (_dg(ah, bl, dims) + _dg(al, bh, dims))


@jax.custom_vjp
def _hdot(a, b):
    return _dot3(a, b, _NN)


@jax.custom_vjp
def _hdot_nt(a, b):
    return _dot3(a, b, _NT)


@jax.custom_vjp
def _hdot_tn(a, b):
    return _dot3(a, b, _TN)


_hdot.defvjp(lambda a, b: (_hdot(a, b), (a, b)), lambda res, g: (_hdot_nt(g, res[1]), _hdot_tn(res[0], g)))
_hdot_nt.defvjp(lambda a, b: (_hdot_nt(a, b), (a, b)), lambda res, g: (_hdot(g, res[1]), _hdot_tn(g, res[0])))
_hdot_tn.defvjp(lambda a, b: (_hdot_tn(a, b), (a, b)), lambda res, g: (_hdot_nt(res[1], g), _hdot(res[0], g)))


_BNN, _BNT, _BTN = ((2,), (1,)), ((2,), (2,)), ((1,), (1,))


def _bdg(a, b, dims):
    return lax.dot_general(a, b, (dims, ((0,), (0,))), preferred_element_type=f32)


def _bdot3(a, b, dims):
    (ah, al), (bh, bl) = _bf16_pieces(a, 2), _bf16_pieces(b, 2)
    return _bdg(ah, bh, dims) + (_bdg(ah, bl, dims) + _bdg(al, bh, dims))


@jax.custom_vjp
def _hbnn(a, b):
    return _bdot3(a, b, _BNN)


@jax.custom_vjp
def _hbnt(a, b):
    return _bdot3(a, b, _BNT)


@jax.custom_vjp
def _hbtn(a, b):
    return _bdot3(a, b, _BTN)


_hbnn.defvjp(lambda a, b: (_hbnn(a, b), (a, b)), lambda res, g: (_hbnt(g, res[1]), _hbtn(res[0], g)))
_hbnt.defvjp(lambda a, b: (_hbnt(a, b), (a, b)), lambda res, g: (_hbnn(g, res[1]), _hbtn(g, res[0])))
_hbtn.defvjp(lambda a, b: (_hbtn(a, b), (a, b)), lambda res, g: (_hbnt(res[1], g), _hbnn(res[0], g)))


@functools.partial(jax.custom_vjp, nondiff_argnums=(2,))
def _exact_bl(m, x, transposed):
    mb = m.astype(bf16)
    hi, mid, lo = _bf16_pieces(x, 3)
    dims = _BTN if transposed else _BNN
    return (_bdg(mb, hi, dims) + _bdg(mb, mid, dims)) + _bdg(mb, lo, dims)


_exact_bl.defvjp(lambda m, x, transposed: (_exact_bl(m, x, transposed), m),
                 lambda transposed, m, g: (jnp.zeros_like(m), _exact_bl(m, g, not transposed)))


@functools.partial(jax.custom_vjp, nondiff_argnums=(2,))
def _exact_l(m, x, transposed):
    mb = m.astype(bf16)
    hi, mid, lo = _bf16_pieces(x, 3)
    dims = _TN if transposed else _NN
    return (_dg(mb, hi, dims) + _dg(mb, mid, dims)) + _dg(mb, lo, dims)


_exact_l.defvjp(lambda m, x, transposed: (_exact_l(m, x, transposed), m),
                lambda transposed, m, g: (jnp.zeros_like(m), _exact_l(m, g, not transposed)))


@functools.partial(jax.custom_vjp, nondiff_argnums=(2,))
def _exact_r(x, m, transposed):
    mb = m.astype(bf16)
    hi, mid, lo = _bf16_pieces(x, 3)
    dims = _NT if transposed else _NN
    return (_dg(hi, mb, dims) + _dg(mid, mb, dims)) + _dg(lo, mb, dims)


_exact_r.defvjp(lambda x, m, transposed: (_exact_r(x, m, transposed), m),
                lambda transposed, m, g: (_exact_r(g, m, not transposed), jnp.zeros_like(m)))


def _rms(x, g, eps=NORM_EPS):
    return x * lax.rsqrt(jnp.mean(x * x, axis=-1, keepdims=True) + eps) * g


def _sigmoid(x):
    return 1.0 / (1.0 + jnp.exp(-x))


def _softplus(x):
    return jnp.maximum(x, 0.0) + jnp.log(1.0 + jnp.exp(-jnp.maximum(x, -x)))


def _lane_split(x, sizes):
    bounds = np.cumsum([0] + list(sizes))

    @jax.custom_vjp
    def split(v):
        return tuple(v[..., int(bounds[q]):int(bounds[q + 1])] for q in range(len(sizes)))

    split.defvjp(lambda v: (split(v), None), lambda _, g: (jnp.concatenate(g, axis=-1),))
    return split(x)


def _row_split(x, sizes):
    bounds = np.cumsum([0] + list(sizes))

    @jax.custom_vjp
    def split(v):
        return tuple(v[..., int(bounds[q]):int(bounds[q + 1]), :] for q in range(len(sizes)))

    split.defvjp(lambda v: (split(v), None), lambda _, g: (jnp.concatenate(g, axis=-2),))
    return split(x)


def _unstack(x):
    @jax.custom_vjp
    def unstack(v):
        return tuple(v[q] for q in range(v.shape[0]))

    unstack.defvjp(lambda v: (unstack(v), None), lambda _, g: (jnp.stack(g, axis=0),))
    return unstack(x)


def _shift_mats(t, k):
    r = lax.broadcasted_iota(jnp.int32, (t, t), 0)
    c = lax.broadcasted_iota(jnp.int32, (t, t), 1)
    inner = (r - c == k).astype(f32)
    r8 = lax.broadcasted_iota(jnp.int32, (t, 8), 0)
    c8 = lax.broadcasted_iota(jnp.int32, (t, 8), 1)
    edge = (c8 - r8 == 8 - k).astype(f32)
    return inner, edge


def _shift(x, halo, k):
    inner, edge = _shift_mats(x.shape[0], k)
    return _exact_l(inner, x, False) + jnp.dot(edge, halo, precision=HI, preferred_element_type=f32)


def mm(a, b, *, name, ta=False, tb=False, a_batched=False, b_batched=False, reduce_batch=False, add=None,
       act_grad=None, relu2_out=False, n_split=1, out_dtype=f32, tm=1024, tn=1024, tk=1024):
    ash, bsh = a.shape[-2:], b.shape[-2:]
    (k_, m_) = ash if ta else ash[::-1]
    (k2_, n_) = bsh[::-1] if tb else bsh
    assert k_ == k2_, (a.shape, b.shape, ta, tb)
    hb = a.shape[0] if a_batched else (b.shape[0] if b_batched else 1)
    batched_out = (a_batched or b_batched) and not reduce_batch
    h_out = hb if batched_out else 1
    h_red = hb if reduce_batch else 1
    tm = _pick(m_, tm, 128)
    tn = _pick(n_ // n_split, tn, 128)
    tk = _pick(k_, tk, 128)
    nm, nn, nk = m_ // tm, n_ // tn, k_ // tk

    def a_map(i, j, ho, hr, kk):
        blk = (kk, i) if ta else (i, kk)
        return ((ho if batched_out else hr),) + blk if a_batched else blk

    def b_map(i, j, ho, hr, kk):
        blk = (j, kk) if tb else (kk, j)
        return ((ho if batched_out else hr),) + blk if b_batched else blk

    a_blk = (tk, tm) if ta else (tm, tk)
    b_blk = (tn, tk) if tb else (tk, tn)
    in_specs = [pl.BlockSpec(((1,) + a_blk) if a_batched else a_blk, a_map),
                pl.BlockSpec(((1,) + b_blk) if b_batched else b_blk, b_map)]
    args = [a, b]
    for extra in (add, act_grad):
        if extra is not None:
            in_specs.append(pl.BlockSpec((tm, tn), lambda i, j, ho, hr, kk: (i, j)))
            args.append(extra)
    if n_split > 1:
        per = n_ // n_split // tn
        if batched_out:
            out_spec = pl.BlockSpec((1, 1, tm, tn), lambda i, j, ho, hr, kk: (j // per, ho, i, j % per))
            out_shape = jax.ShapeDtypeStruct((n_split, hb, m_, n_ // n_split), out_dtype)
        else:
            out_spec = pl.BlockSpec((1, tm, tn), lambda i, j, ho, hr, kk: (j // per, i, j % per))
            out_shape = jax.ShapeDtypeStruct((n_split, m_, n_ // n_split), out_dtype)
    elif batched_out:
        out_spec = pl.BlockSpec((1, tm, tn), lambda i, j, ho, hr, kk: (ho, i, j))
        out_shape = jax.ShapeDtypeStruct((hb, m_, n_), out_dtype)
    else:
        out_spec = pl.BlockSpec((tm, tn), lambda i, j, ho, hr, kk: (i, j))
        out_shape = jax.ShapeDtypeStruct((m_, n_), out_dtype)
    lead = (0,) * (int(batched_out) + int(n_split > 1))
    dims = ((0,) if ta else (1,), (1,) if tb else (0,))
    has_add, has_act = add is not None, act_grad is not None

    def body(*refs):
        a_ref, b_ref = refs[0], refs[1]
        pos = 2
        add_ref = act_ref = None
        if has_add:
            add_ref = refs[pos]
            pos += 1
        if has_act:
            act_ref = refs[pos]
            pos += 1
        o_ref, acc_ref = refs[pos], refs[-1]
        hr, kk = pl.program_id(3), pl.program_id(4)
        first = jnp.logical_and(hr == 0, kk == 0)
        last = jnp.logical_and(hr == h_red - 1, kk == nk - 1)
        av = a_ref[0] if a_batched else a_ref[...]
        bv = b_ref[0] if b_batched else b_ref[...]
        p = _bdot(av, bv, dims)

        @pl.when(first)
        def _():
            acc_ref[...] = p

        @pl.when(jnp.logical_not(first))
        def _():
            acc_ref[...] += p

        @pl.when(last)
        def _():
            r = acc_ref[...]
            if has_act:
                r = r * (2.0 * jnp.maximum(act_ref[...], 0.0))
            if has_add:
                r = r + add_ref[...]
            if lead:
                o_ref[lead] = r.astype(out_dtype)
            else:
                o_ref[...] = r.astype(out_dtype)
            if relu2_out:
                refs[pos + 1][...] = jnp.square(jnp.maximum(r, 0.0)).astype(bf16)

    if relu2_out:
        assert not lead
        out_spec = [out_spec, out_spec]
        out_shape = [out_shape, jax.ShapeDtypeStruct(out_shape.shape, bf16)]
    return pl.pallas_call(
        body, name=name, grid=(nm, nn, h_out, h_red, nk), in_specs=in_specs, out_specs=out_spec,
        out_shape=out_shape, scratch_shapes=[pltpu.VMEM((tm, tn), f32)],
        compiler_params=_cparams(("parallel", "parallel", "parallel", "arbitrary", "arbitrary")),
    )(*args)


def _row_spec(arr, tile, idx):
    if arr.ndim == 2:
        return pl.BlockSpec((tile, arr.shape[1]), lambda i: (idx(i), 0))
    return pl.BlockSpec((arr.shape[0], tile, arr.shape[2]), lambda i: (0, idx(i), 0))


def _halo_spec(arr, tile, idx):
    per = tile // 8
    return pl.BlockSpec((8, arr.shape[1]), lambda i: (jnp.maximum(idx(i) * per - 1, 0), 0))


def _full_spec(arr):
    nd = arr.ndim
    return pl.BlockSpec(arr.shape, lambda i: (0,) * nd)


def _load_f32(ref):
    val = ref[...]
    return val.astype(f32) if val.dtype == bf16 else val


def rows_fwd(fn, rows, params, consts, out_shapes, *, tile, name, halos=()):
    s_len = rows[0].shape[-2]
    n = s_len // tile
    nr, nh, npar, nc = len(rows), len(halos), len(params), len(consts)
    ident = lambda i: i
    in_specs = ([_row_spec(r, tile, ident) for r in rows] + [_halo_spec(rows[h], tile, ident) for h in halos]
                + [_full_spec(p) for p in params] + [_full_spec(c) for c in consts])
    out_specs = [_row_spec(o, tile, ident) for o in out_shapes]

    def body(*refs):
        i = pl.program_id(0)
        rv = [_load_f32(r) for r in refs[:nr]]
        keep = (i > 0).astype(f32)
        hv = [r[...] * keep for r in refs[nr:nr + nh]]
        pv = [r[...] for r in refs[nr + nh:nr + nh + npar]]
        cv = [r[...] for r in refs[nr + nh + npar:nr + nh + npar + nc]]
        outs = fn(rv, hv, pv, cv)
        for o_ref, o in zip(refs[nr + nh + npar + nc:], outs):
            o_ref[...] = o.astype(o_ref.dtype)

    return pl.pallas_call(
        body, name=name, grid=(n,), in_specs=in_specs, out_specs=out_specs, out_shape=list(out_shapes),
        compiler_params=_cparams(("arbitrary",)),
    )(*rows, *[rows[h] for h in halos], *params, *consts)


def rows_bwd(fn, rows, params, consts, douts, *, tile, name, halos=(), grad_rows=None, extra=None,
             grad_dtypes=None):
    s_len = rows[0].shape[-2]
    n = s_len // tile
    nr, nh, npar, nc = len(rows), len(halos), len(params), len(consts)
    grad_rows = list(range(nr)) if grad_rows is None else list(grad_rows)
    extra = extra or {}
    assert all(h in grad_rows for h in halos)
    rev = lambda i: n - 1 - i
    dflat = [d for ds in douts for d in ds]
    dcount = [len(ds) for ds in douts]
    eflat = [e for g in grad_rows for e in extra.get(g, [])]
    ecount = [len(extra.get(g, [])) for g in grad_rows]
    in_specs = ([_row_spec(r, tile, rev) for r in rows] + [_halo_spec(rows[h], tile, rev) for h in halos]
                + [_full_spec(p) for p in params] + [_full_spec(c) for c in consts]
                + [_row_spec(d, tile, rev) for d in dflat] + [_row_spec(e, tile, rev) for e in eflat])
    grad_dtypes = [f32] * len(grad_rows) if grad_dtypes is None else list(grad_dtypes)
    assert all(grad_dtypes[q] == f32 for q, g in enumerate(grad_rows) if g in halos)
    out_shapes = ([jax.ShapeDtypeStruct(rows[g].shape, dt) for g, dt in zip(grad_rows, grad_dtypes)]
                  + [jax.ShapeDtypeStruct(p.shape, f32) for p in params])
    out_specs = [_row_spec(rows[g], tile, rev) for g in grad_rows] + [_full_spec(p) for p in params]
    scratch = [pltpu.VMEM((8, rows[h].shape[1]), f32) for h in halos]
    n_in = nr + nh + npar + nc + len(dflat) + len(eflat)
    n_out = len(grad_rows) + npar

    def body(*refs):
        i = pl.program_id(0)
        rv = [_load_f32(r) for r in refs[:nr]]
        keep = (i < n - 1).astype(f32)
        hv = [r[...] * keep for r in refs[nr:nr + nh]]
        pv = [r[...] for r in refs[nr + nh:nr + nh + npar]]
        pos = nr + nh + npar
        cv = [r[...] for r in refs[pos:pos + nc]]
        pos += nc
        dv = []
        for cnt in dcount:
            acc = _load_f32(refs[pos])
            for q in range(1, cnt):
                acc = acc + _load_f32(refs[pos + q])
            dv.append(acc)
            pos += cnt
        ev = []
        for cnt in ecount:
            ev.append([_load_f32(refs[pos + q]) for q in range(cnt)])
            pos += cnt
        out_refs = refs[n_in:n_in + n_out]
        carry_refs = refs[n_in + n_out:]

        def f(gr, gh, gp):
            full = list(rv)
            for g, val in zip(grad_rows, gr):
                full[g] = val
            return tuple(fn(full, gh, gp, cv))

        _, vjp = jax.vjp(f, [rv[g] for g in grad_rows], hv, pv)
        d_rows, d_halos, d_params = vjp(tuple(dv))

        @pl.when(i == 0)
        def _():
            for c_ref in carry_refs:
                c_ref[...] = jnp.zeros_like(c_ref)
            for p_ref in out_refs[len(grad_rows):]:
                p_ref[...] = jnp.zeros_like(p_ref)

        for q, g in enumerate(grad_rows):
            val = d_rows[q]
            for e in ev[q]:
                val = val + e
            out_refs[q][...] = val.astype(out_refs[q].dtype)
            if g in halos:
                hq = list(halos).index(g)
                out_refs[q][tile - 8:tile, :] += carry_refs[hq][...]
                carry_refs[hq][...] = d_halos[hq]
        for p_ref, dp in zip(out_refs[len(grad_rows):], d_params):
            p_ref[...] += dp

    res = pl.pallas_call(
        body, name=name, grid=(n,), in_specs=in_specs, out_specs=out_specs, out_shape=out_shapes,
        scratch_shapes=scratch, compiler_params=_cparams(("arbitrary",)),
    )(*rows, *[rows[h] for h in halos], *params, *consts, *dflat, *eflat)
    return list(res[:len(grad_rows)]), list(res[len(grad_rows):])


def _fn_norm(rows, halos, params, consts):
    return (_rms(rows[0], params[0]),)


def _fn_mla_prep(rows, halos, params, consts):
    cq, ckv, kpe = _lane_split(rows[0], (QL, KVL, MLA_PAD - QL - KVL))
    return _rms(cq, params[0]), _rms(ckv, params[1]), kpe


def _rope(x, cos, sin, rot):
    return x * cos + _exact_r(x, rot, False) * sin


def _fn_qk_post(rows, halos, params, consts):
    q_raw, kn_pad, kpe, cos, sin = rows
    q_norm, k_norm = params
    place, rot = consts
    kpe96 = _exact_r(kpe, place, False)
    qs = [_rope(_rms(qh, q_norm), cos, sin, rot) for qh in _unstack(q_raw)]
    ks = [_rope(_rms(kh + kpe96, k_norm), cos, sin, rot) for kh in _unstack(kn_pad)]
    return jnp.stack(qs, axis=0), jnp.stack(ks, axis=0)


def _seg(x, bd):
    return _exact_r(x, bd, False)


def _make_fn_rwkv_prep(vres):
    def fn(rows, halos, params, consts):
        cols = rows[0]
        bd = consts[0]
        mu, w0, w2, a0, a2, g2, k_k, k_a = params[:8]
        prev = _shift(cols, halos[0], 1)
        c = cols + (prev - cols) * mu
        r, k, v, xw, xa, xg = _lane_split(c, (RW, RW, RW, DL, AL, GL))
        log_w = -_softplus(-(w0 + _mm(jnp.tanh(xw), w2))) - 0.5
        ld = -jnp.exp(log_w)
        a = _sigmoid(a0 + _mm(xa, a2))
        g = _mm(_sigmoid(xg), g2)
        if vres:
            hcur, v_first = rows[1], rows[2]
            v1, v_mu, v0, v2 = params[8:12]
            xv = _mm(hcur, v1)
            xv_prev = _shift(xv, _mm(halos[1], v1), 1)
            xv = xv + (xv_prev - xv) * v_mu
            v = v + (v_first - v) * _sigmoid(v0 + _mm(xv, v2))
        kk = k * k_k
        kk = kk / jnp.maximum(jnp.sqrt(_seg(kk * kk, bd)), 1e-12)
        k2 = k * (1.0 + (a - 1.0) * k_a)
        return r, ld, k2, v, -kk, kk * a, g
    return fn


def _fn_rwkv_post(rows, halos, params, consts):
    y, r, k2, v, g = rows
    ln_w, ln_b, r_k = params
    bd = consts[0]
    mean = _seg(y, bd) * (1.0 / RN)
    d = y - mean
    var = _seg(d * d, bd) * (1.0 / RN)
    yn = d * lax.rsqrt(var + GN_EPS) * ln_w + ln_b
    bonus = _seg(r * k2 * r_k, bd) * v
    return ((yn + bonus) * g,)


def _fn_conv(rows, halos, params, consts):
    cols, halo = rows[0], halos[0]
    w0, w1, w2 = params
    b, c, x = _lane_split(cols, (CW, CW, CW))
    _, ch, xh = _lane_split(halo, (CW, CW, CW))
    u, uh = c * x, ch * xh
    return (b * (w0 * _shift(u, uh, 2) + w1 * _shift(u, uh, 1) + w2 * u),)


def _fn_merge(rows, halos, params, consts):
    gate, o_a, o_b, o_c = rows
    g_a, g_b, g_c = _lane_split(gate, (D, D, D))
    return (_sigmoid(g_a) * o_a + _sigmoid(g_b) * o_b + _sigmoid(g_c) * o_c,)


def _attn_block(q, k, v, q0):
    tq, s_len = q.shape[0], k.shape[0]
    s = _mm_nt(q, k) * (DQK ** -0.5)
    row = q0 + lax.broadcasted_iota(jnp.int32, (tq, s_len), 0)
    col = lax.broadcasted_iota(jnp.int32, (tq, s_len), 1)
    s = jnp.where(row >= col, s, -1e30)
    m = lax.stop_gradient(jnp.max(s, axis=-1, keepdims=True))
    e = jnp.exp(s - m)
    p = e / jnp.sum(e, axis=-1, keepdims=True)
    return _mm(p, v)


def _attn_segments(s_len, tq):
    per = max(1, s_len // tq // ATTN_SEGMENTS)
    return [(first, per, (first + per) * tq) for first in range(0, s_len // tq, per)]


def attn_fwd(q, k, v, *, tq, name):
    h, s_len, _ = q.shape
    outs = []
    for seg, (first, nq, kend) in enumerate(_attn_segments(s_len, tq)):
        def body(q_ref, k_ref, v_ref, o_ref, first=first):
            q0 = (first + pl.program_id(1)) * tq
            o_ref[0] = _attn_block(q_ref[0], k_ref[0], v_ref[0], q0).astype(o_ref.dtype)

        outs.append(pl.pallas_call(
            body, name=f"{name}_{seg}", grid=(h, nq),
            in_specs=[pl.BlockSpec((1, tq, DQK), lambda hh, i, first=first: (hh, first + i, 0)),
                      pl.BlockSpec((1, kend, DQK), lambda hh, i: (hh, 0, 0)),
                      pl.BlockSpec((1, kend, DV), lambda hh, i: (hh, 0, 0))],
            out_specs=pl.BlockSpec((1, tq, DV), lambda hh, i: (hh, i, 0)),
            out_shape=jax.ShapeDtypeStruct((h, nq * tq, DV), bf16),
            compiler_params=_cparams(("parallel", "arbitrary")),
        )(q, k, v))
    return jnp.concatenate(outs, axis=1)


def attn_bwd(q, k, v, do, *, tq, name):
    h, s_len, _ = q.shape
    dqs, dk_acc, dv_acc = [], None, None
    for seg, (first, nq, kend) in reversed(list(enumerate(_attn_segments(s_len, tq)))):
        carried = dk_acc is not None

        def body(*refs, first=first, carried=carried):
            q_ref, k_ref, v_ref, do_ref = refs[:4]
            dq_ref, dk_ref, dv_ref = refs[-3:]
            i = pl.program_id(1)
            _, vjp = jax.vjp(functools.partial(_attn_block, q0=(first + i) * tq), q_ref[0], k_ref[0], v_ref[0])
            dq, dk, dv = vjp(do_ref[0])
            dq_ref[0] = dq

            @pl.when(i == 0)
            def _():
                dk_ref[0] = dk + refs[4][0] if carried else dk
                dv_ref[0] = dv + refs[5][0] if carried else dv

            @pl.when(i > 0)
            def _():
                dk_ref[0] += dk
                dv_ref[0] += dv

        key_specs = [pl.BlockSpec((1, kend, DQK), lambda hh, i: (hh, 0, 0)),
                     pl.BlockSpec((1, kend, DV), lambda hh, i: (hh, 0, 0))]
        dq, dk_acc, dv_acc = pl.pallas_call(
            body, name=f"{name}_{seg}", grid=(h, nq),
            in_specs=[pl.BlockSpec((1, tq, DQK), lambda hh, i, first=first: (hh, first + i, 0))] + key_specs
            + [pl.BlockSpec((1, tq, DV), lambda hh, i, first=first: (hh, first + i, 0))]
            + (key_specs if carried else []),
            out_specs=[pl.BlockSpec((1, tq, DQK), lambda hh, i: (hh, i, 0))] + key_specs,
            out_shape=[jax.ShapeDtypeStruct((h, nq * tq, DQK), f32), jax.ShapeDtypeStruct((h, s_len, DQK), f32),
                       jax.ShapeDtypeStruct((h, s_len, DV), f32)],
            input_output_aliases={4: 1, 5: 2} if carried else {},
            compiler_params=_cparams(("parallel", "arbitrary")),
        )(q, k, v, do, *([dk_acc, dv_acc] if carried else []))
        dqs.append(dq)
    return jnp.concatenate(dqs[::-1], axis=1), dk_acc, dv_acc


def _wkv_local(r, ld, k, v, a, b):
    nb, c, n = r.shape
    ri = lax.broadcasted_iota(jnp.int32, (c, c), 0)
    ci = lax.broadcasted_iota(jnp.int32, (c, c), 1)
    tri = jnp.broadcast_to((ri >= ci).astype(f32)[None], (nb, c, c))
    cum = _exact_bl(tri, ld, False)
    tot = jnp.sum(ld, axis=1, keepdims=True)
    w_incl, w_excl, w_inv, w_rest = jnp.exp(cum), jnp.exp(cum - ld), jnp.exp(-cum), jnp.exp(tot - cum)
    ab, rb, bb, kb = a * w_excl, r * w_incl, b * w_inv, k * w_inv
    bw, kw = b * w_rest, k * w_rest
    r2 = lax.broadcasted_iota(jnp.int32, (2 * c, 2 * c), 0)
    c2 = lax.broadcasted_iota(jnp.int32, (2 * c, 2 * c), 1)
    t_of, s_of = jnp.where(r2 >= c, r2 - c, r2), jnp.where(c2 >= c, c2 - c, c2)
    keep = jnp.logical_or(t_of > s_of, jnp.logical_and(r2 >= c, t_of == s_of))
    pair = jnp.where(keep[None], _hbnt(jnp.concatenate([ab, rb], axis=1), jnp.concatenate([bb, kb], axis=1)), 0.0)
    on_b, on_k = _lane_split(pair, (c, c))
    l_ab, m_rb = _row_split(on_b, (c, c))
    l_ak_v, m_rk_v = _row_split(_hbnn(on_k, v), (c, c))
    x = jnp.concatenate([ab, l_ak_v], axis=-1)
    lp, span = l_ab, 1
    while span < c:
        x = x + _hbnn(lp, x)
        span *= 2
        if span < c:
            lp = _hbnn(lp, lp)
    via_b_r, via_b_y = _lane_split(_hbnn(m_rb, x), (n, n))
    r_hat = rb + via_b_r
    y0 = via_b_y + m_rk_v
    from_b_g, from_b_z = _row_split(_hbtn(x, bw), (n, n))
    eye = lax.broadcasted_iota(jnp.int32, (n, n), 0) == lax.broadcasted_iota(jnp.int32, (n, n), 1)
    g = jnp.where(eye[None], jnp.exp(tot), 0.0) + from_b_g
    z = from_b_z + _hbtn(v, kw)
    return r_hat, y0, g, z


def _head(h):
    return slice(RN * h, RN * (h + 1))


def _load_chunk_heads(ref, c, per):
    return jnp.stack([ref[c * q:c * (q + 1), _head(h)] for q in range(per) for h in range(RH)], axis=0)


def _store_chunk_heads(ref, val, c, per):
    for q in range(per):
        ref[c * q:c * (q + 1), :] = jnp.concatenate([val[q * RH + h] for h in range(RH)], axis=-1)


def wkv_fwd(r, ld, k, v, a, b, *, name):
    s_len = r.shape[0]
    c, per = WKV_CHUNK, WKV_CHUNKS_PER_STEP
    n = s_len // c
    rows = pl.BlockSpec((c * per, RW), lambda i: (i, 0))
    mats = pl.BlockSpec((per, RH, RN, RN), lambda i: (i, 0, 0, 0))
    rows_t, mats_t = jax.ShapeDtypeStruct((s_len, RW), f32), jax.ShapeDtypeStruct((n, RH, RN, RN), f32)

    def local_body(r_ref, ld_ref, k_ref, v_ref, a_ref, b_ref, rh_ref, y0_ref, g_ref, z_ref):
        r_hat, y0, g, z = _wkv_local(*[_load_chunk_heads(ref, c, per)
                                       for ref in (r_ref, ld_ref, k_ref, v_ref, a_ref, b_ref)])
        _store_chunk_heads(rh_ref, r_hat, c, per)
        _store_chunk_heads(y0_ref, y0, c, per)
        g_ref[...] = g.reshape(per, RH, RN, RN)
        z_ref[...] = z.reshape(per, RH, RN, RN)

    r_hat, y0, g, z = pl.pallas_call(
        local_body, name=name + "_local", grid=(n // per,), in_specs=[rows] * 6, out_specs=[rows, rows, mats, mats],
        out_shape=[rows_t, rows_t, mats_t, mats_t], compiler_params=_cparams(("parallel",)),
    )(r, ld, k, v, a, b)

    def scan_body(g_ref, z_ref, st_ref, s_sc):
        s_sc[...] = jnp.zeros_like(s_sc)

        @pl.loop(0, n)
        def _(i):
            s0 = s_sc[...]
            st_ref[i] = s0
            s_sc[...] = _hbnn(s0, g_ref[i]) + z_ref[i]

    vm = pl.BlockSpec(memory_space=pltpu.VMEM)
    states = pl.pallas_call(
        scan_body, name=name + "_scan", in_specs=[vm, vm], out_specs=vm, out_shape=mats_t,
        scratch_shapes=[pltpu.VMEM((RH, RN, RN), f32)],
        compiler_params=pltpu.CompilerParams(vmem_limit_bytes=VMEM_LIMIT),
    )(g, z)

    def out_body(rh_ref, y0_ref, st_ref, y_ref):
        y = _hbnt(_load_chunk_heads(rh_ref, c, per), st_ref[...].reshape(per * RH, RN, RN))
        _store_chunk_heads(y_ref, y, c, per)
        y_ref[...] += y0_ref[...]

    y = pl.pallas_call(
        out_body, name=name + "_out", grid=(n // per,), in_specs=[rows, rows, mats], out_specs=rows,
        out_shape=rows_t, compiler_params=_cparams(("parallel",)),
    )(r_hat, y0, states)
    return y, dict(r_hat=r_hat, g=g, states=states)


def wkv_bwd(r, ld, k, v, a, b, saved, dy, *, name):
    s_len = r.shape[0]
    c, per = WKV_CHUNK, WKV_CHUNKS_PER_STEP
    n = s_len // c
    rows = pl.BlockSpec((c * per, RW), lambda i: (i, 0))
    mats = pl.BlockSpec((per, RH, RN, RN), lambda i: (i, 0, 0, 0))
    rows_t, mats_t = jax.ShapeDtypeStruct((s_len, RW), f32), jax.ShapeDtypeStruct((n, RH, RN, RN), f32)

    def out_body(dy_ref, rh_ref, st_ref, drh_ref, dsy_ref):
        dyb = _load_chunk_heads(dy_ref, c, per)
        _store_chunk_heads(drh_ref, _hbnn(dyb, st_ref[...].reshape(per * RH, RN, RN)), c, per)
        dsy_ref[...] = _hbtn(dyb, _load_chunk_heads(rh_ref, c, per)).reshape(per, RH, RN, RN)

    d_rhat, ds_y = pl.pallas_call(
        out_body, name=name + "_out", grid=(n // per,), in_specs=[rows, rows, mats], out_specs=[rows, mats],
        out_shape=[rows_t, mats_t], compiler_params=_cparams(("parallel",)),
    )(dy, saved["r_hat"], saved["states"])

    def scan_body(dsy_ref, g_ref, st_ref, dg_ref, dz_ref, ds_sc):
        ds_sc[...] = jnp.zeros_like(ds_sc)

        @pl.loop(0, n)
        def _(i):
            cidx = n - 1 - i
            ds_next = ds_sc[...]
            dz_ref[cidx] = ds_next
            dg_ref[cidx] = _hbtn(st_ref[cidx], ds_next)
            ds_sc[...] = dsy_ref[cidx] + _hbnt(ds_next, g_ref[cidx])

    vm = pl.BlockSpec(memory_space=pltpu.VMEM)
    d_g, d_z = pl.pallas_call(
        scan_body, name=name + "_scan", in_specs=[vm, vm, vm], out_specs=[vm, vm], out_shape=[mats_t, mats_t],
        scratch_shapes=[pltpu.VMEM((RH, RN, RN), f32)],
        compiler_params=pltpu.CompilerParams(vmem_limit_bytes=VMEM_LIMIT),
    )(ds_y, saved["g"], saved["states"])

    def local_body(r_ref, ld_ref, k_ref, v_ref, a_ref, b_ref, drh_ref, dy_ref, dg_ref, dz_ref, *out_refs):
        _, vjp = jax.vjp(_wkv_local, *[_load_chunk_heads(ref, c, per)
                                       for ref in (r_ref, ld_ref, k_ref, v_ref, a_ref, b_ref)])
        grads = vjp((_load_chunk_heads(drh_ref, c, per), _load_chunk_heads(dy_ref, c, per),
                     dg_ref[...].reshape(per * RH, RN, RN), dz_ref[...].reshape(per * RH, RN, RN)))
        for o_ref, val in zip(out_refs, grads):
            _store_chunk_heads(o_ref, val, c, per)

    return pl.pallas_call(
        local_body, name=name + "_local", grid=(n // per,), in_specs=[rows] * 8 + [mats, mats], out_specs=[rows] * 6,
        out_shape=[rows_t] * 6, compiler_params=_cparams(("parallel",)),
    )(r, ld, k, v, a, b, d_rhat, dy, d_g, d_z)


def loss_head(y, target, *, tile, name):
    s_len, d = y.shape
    n = s_len // tile

    def body(y_ref, t_ref, dy_ref, l_ref):
        err = y_ref[...] - t_ref[...]
        dy_ref[...] = err * (1.0 / d)
        part = 0.5 * jnp.sum(jnp.mean(err * err, axis=-1, keepdims=True), axis=0, keepdims=True)

        @pl.when(pl.program_id(0) == 0)
        def _():
            l_ref[...] = jnp.zeros_like(l_ref)

        l_ref[...] += jnp.broadcast_to(part, l_ref.shape)

    bs = pl.BlockSpec((tile, d), lambda i: (i, 0))
    dy, l = pl.pallas_call(
        body, name=name, grid=(n,), in_specs=[bs, bs],
        out_specs=[bs, pl.BlockSpec((8, 128), lambda i: (0, 0))],
        out_shape=[jax.ShapeDtypeStruct((s_len, d), f32), jax.ShapeDtypeStruct((8, 128), f32)],
        compiler_params=_cparams(("arbitrary",)),
    )(y, target)
    return l[0, 0], dy


def adamw(w, g, m, v, *, name):
    gs = g if isinstance(g, (list, tuple)) else [g]
    rows, cols = w.shape
    tile = _pick(rows, max(8, (2 * 1024 * 1024 // (4 * cols)) // 8 * 8), 8)
    c1 = 1.0 - B1 ** STEP
    c2 = 1.0 - B2 ** STEP
    ng = len(gs)

    def body(*refs):
        w_ref, m_ref, v_ref = refs[0], refs[1 + ng], refs[2 + ng]
        g_out, d_out, m_out, v_out = refs[3 + ng:]
        gv = refs[1][...]
        for q in range(1, ng):
            gv = gv + refs[1 + q][...]
        mn = B1 * m_ref[...] + (1.0 - B1) * gv
        vn = B2 * v_ref[...] + (1.0 - B2) * (gv * gv)
        d_out[...] = -LR * ((mn / c1) / (jnp.sqrt(vn / c2) + EPS) + WD * w_ref[...])
        g_out[...] = gv
        m_out[...] = mn
        v_out[...] = vn

    bs = pl.BlockSpec((tile, cols), lambda i: (i, 0))
    return pl.pallas_call(
        body, name=name, grid=(rows // tile,), in_specs=[bs] * (3 + ng), out_specs=[bs] * 4,
        out_shape=[jax.ShapeDtypeStruct((rows, cols), f32)] * 4, compiler_params=_cparams(("parallel",)),
    )(w, *gs, m, v)


def _place():
    return lax.axis_index("x"), lax.axis_index("y"), lax.axis_index("c")


_ANY = pl.BlockSpec(memory_space=pl.ANY)


def _peer_chips(x, y):
    return [(1 - x, y), (x, 1 - y), (1 - x, 1 - y)]


def gather_weights(shards, *, name):
    nk = len(shards)

    def body(*refs):
        srcs, outs = refs[:nk], refs[nk:2 * nk]
        ici_send, ici_recv, d2d_send, d2d_recv = refs[2 * nk:]
        x, y, c = _place()
        me = 2 * x + y
        peers = _peer_chips(x, y)
        pending = []
        for k in range(nk):
            half = srcs[k].shape[0] // 2
            mine = pl.ds(c * half, half)
            for p, (px, py) in enumerate(peers):
                cp = pltpu.make_async_remote_copy(
                    src_ref=srcs[k].at[mine], dst_ref=outs[k].at[me, mine], send_sem=ici_send.at[k, p],
                    recv_sem=ici_recv.at[k, p], device_id=(px, py, c), device_id_type=MESH)
                cp.start()
                pending.append(cp)
        for k in range(nk):
            half = srcs[k].shape[0] // 2
            mine = pl.ds(c * half, half)
            for p, (px, py) in enumerate(peers):
                landed = outs[k].at[2 * px + py, mine]
                pltpu.make_async_remote_copy(
                    src_ref=srcs[k].at[mine], dst_ref=landed, send_sem=ici_send.at[k, p], recv_sem=ici_recv.at[k, p],
                    device_id=(px, py, c), device_id_type=MESH).wait_recv()
                fwd = pltpu.make_async_remote_copy(
                    src_ref=landed, dst_ref=landed, send_sem=d2d_send.at[k, p], recv_sem=d2d_recv.at[k, p],
                    device_id=(x, y, 1 - c), device_id_type=MESH)
                fwd.start()
                pending.append(fwd)
        for k in range(nk):
            half = srcs[k].shape[0] // 2
            other = pl.ds((1 - c) * half, half)
            for p, (px, py) in enumerate(peers):
                theirs = outs[k].at[2 * px + py, other]
                pltpu.make_async_remote_copy(
                    src_ref=theirs, dst_ref=theirs, send_sem=d2d_send.at[k, p], recv_sem=d2d_recv.at[k, p],
                    device_id=(x, y, 1 - c), device_id_type=MESH).wait_recv()
        for cp in pending:
            cp.wait_send()

    sem = lambda *shape: pltpu.SemaphoreType.DMA(shape)
    return pl.pallas_call(
        body, name=name, in_specs=[_ANY] * nk, out_specs=[_ANY] * nk,
        out_shape=[jax.ShapeDtypeStruct((4,) + s.shape, s.dtype) for s in shards],
        scratch_shapes=[sem(nk, 3), sem(nk, 3), sem(nk, 3), sem(nk, 3)],
    )(*shards)


def grads_to_sibling(parts, *, name):
    nk = len(parts)

    def body(*refs):
        srcs, outs = refs[:nk], refs[nk:2 * nk]
        send_sems, recv_sems = refs[2 * nk:]
        x, y, c = _place()
        sends = []
        for k in range(nk):
            half = srcs[k].shape[1] // 2
            cp = pltpu.make_async_remote_copy(
                src_ref=srcs[k].at[:, pl.ds((1 - c) * half, half), :], dst_ref=outs[k], send_sem=send_sems.at[k],
                recv_sem=recv_sems.at[k], device_id=(x, y, 1 - c), device_id_type=MESH)
            cp.start()
            sends.append(cp)
        for cp in sends:
            cp.wait_recv()
        for cp in sends:
            cp.wait_send()

    return pl.pallas_call(
        body, name=name, in_specs=[_ANY] * nk, out_specs=[_ANY] * nk,
        out_shape=[jax.ShapeDtypeStruct((4, p.shape[1] // 2, p.shape[2]), p.dtype) for p in parts],
        scratch_shapes=[pltpu.SemaphoreType.DMA((nk,)), pltpu.SemaphoreType.DMA((nk,))],
    )(*parts)


def pair_sum(part, theirs, core, *, name):
    _, rows, cols = part.shape
    half = rows // 2
    tile = _pick(half, max(16, (1 << 20) // (4 * cols) // 16 * 16), 16)
    per = half // tile

    def body(c_ref, p_ref, t_ref, o_ref):
        o_ref[...] = (p_ref[...].astype(f32) + t_ref[...].astype(f32)).astype(bf16)

    grid_spec = pltpu.PrefetchScalarGridSpec(
        num_scalar_prefetch=1, grid=(4, per),
        in_specs=[pl.BlockSpec((1, tile, cols), lambda j, i, c_ref: (j, c_ref[0] * per + i, 0)),
                  pl.BlockSpec((1, tile, cols), lambda j, i, c_ref: (j, i, 0))],
        out_specs=pl.BlockSpec((1, tile, cols), lambda j, i, c_ref: (j, i, 0)))
    return pl.pallas_call(
        body, name=name, grid_spec=grid_spec, out_shape=jax.ShapeDtypeStruct((4, half, cols), bf16),
        compiler_params=_cparams(("parallel", "parallel")),
    )(core, part, theirs)


def chips_all_to_all(parts, *, name):
    nk = len(parts)

    def body(*refs):
        srcs, outs = refs[:nk], refs[nk:2 * nk]
        send_sems, recv_sems, local_sems = refs[2 * nk:]
        x, y, c = _place()
        me = 2 * x + y
        peers = _peer_chips(x, y)
        sends, locals_ = [], []
        for k in range(nk):
            local = pltpu.make_async_copy(srcs[k].at[me], outs[k].at[me], local_sems.at[k])
            local.start()
            locals_.append(local)
            for p, (px, py) in enumerate(peers):
                cp = pltpu.make_async_remote_copy(
                    src_ref=srcs[k].at[2 * px + py], dst_ref=outs[k].at[me], send_sem=send_sems.at[k, p],
                    recv_sem=recv_sems.at[k, p], device_id=(px, py, c), device_id_type=MESH)
                cp.start()
                sends.append(cp)
        for k in range(nk):
            for p, (px, py) in enumerate(peers):
                pltpu.make_async_remote_copy(
                    src_ref=srcs[k].at[me], dst_ref=outs[k].at[2 * px + py], send_sem=send_sems.at[k, p],
                    recv_sem=recv_sems.at[k, p], device_id=(px, py, c), device_id_type=MESH).wait_recv()
        for cp in sends:
            cp.wait_send()
        for cp in locals_:
            cp.wait()

    sem = lambda *shape: pltpu.SemaphoreType.DMA(shape)
    return pl.pallas_call(
        body, name=name, in_specs=[_ANY] * nk, out_specs=[_ANY] * nk,
        out_shape=[jax.ShapeDtypeStruct(p.shape, p.dtype) for p in parts],
        scratch_shapes=[sem(nk, 3), sem(nk, 3), sem(nk)],
    )(*parts)


def join_halves(bufs, layout, *, name):
    nk, nb = len(layout), len(bufs)

    def body(*refs):
        outs = refs[nb:2 * nb]
        send_sems, recv_sems = refs[2 * nb:]
        x, y, c = _place()
        pending = []
        for k, (o, off, rows) in enumerate(layout):
            half = rows // 2
            mine = outs[o].at[pl.ds(off + c * half, half), :]
            cp = pltpu.make_async_remote_copy(
                src_ref=mine, dst_ref=mine, send_sem=send_sems.at[k], recv_sem=recv_sems.at[k],
                device_id=(x, y, 1 - c), device_id_type=MESH)
            cp.start()
            pending.append(cp)
        for k, (o, off, rows) in enumerate(layout):
            half = rows // 2
            theirs = outs[o].at[pl.ds(off + (1 - c) * half, half), :]
            pltpu.make_async_remote_copy(
                src_ref=theirs, dst_ref=theirs, send_sem=send_sems.at[k], recv_sem=recv_sems.at[k],
                device_id=(x, y, 1 - c), device_id_type=MESH).wait_recv()
        for cp in pending:
            cp.wait_send()

    return pl.pallas_call(
        body, name=name, in_specs=[_ANY] * nb, out_specs=[_ANY] * nb,
        out_shape=[jax.ShapeDtypeStruct(b.shape, b.dtype) for b in bufs],
        input_output_aliases={o: o for o in range(nb)},
        scratch_shapes=[pltpu.SemaphoreType.DMA((nk,)), pltpu.SemaphoreType.DMA((nk,))],
    )(*bufs)


def place_slab(dest, src, index, *, name):
    rows, cols = src.shape
    tile = _pick(rows, max(16, (1 << 20) // (src.dtype.itemsize * cols) // 16 * 16), 16)

    def body(i_ref, s_ref, d_ref, o_ref):
        del i_ref, d_ref
        o_ref[0] = s_ref[...]

    grid_spec = pltpu.PrefetchScalarGridSpec(
        num_scalar_prefetch=1, grid=(rows // tile,),
        in_specs=[pl.BlockSpec((tile, cols), lambda i, idx: (i, 0)), _ANY],
        out_specs=pl.BlockSpec((1, tile, cols), lambda i, idx: (idx[0], i, 0)))
    return pl.pallas_call(
        body, name=name, grid_spec=grid_spec, out_shape=jax.ShapeDtypeStruct(dest.shape, dest.dtype),
        input_output_aliases={2: 0}, compiler_params=_cparams(("parallel",)),
    )(index, src, dest)


def all_reduce_small(src, *, name):
    rows, cols = src.shape

    def body(s_ref, o_ref, buf, send_sems, recv_sems):
        x, y, c = _place()
        me = 4 * x + 2 * y + c
        buf[me] = s_ref[...]
        sends = []
        for msk in range(1, 8):
            px = x ^ (msk >> 2)
            py = y ^ ((msk >> 1) & 1)
            pc = c ^ (msk & 1)
            cp = pltpu.make_async_remote_copy(
                src_ref=s_ref, dst_ref=buf.at[me], send_sem=send_sems.at[msk - 1], recv_sem=recv_sems.at[msk - 1],
                device_id=(px, py, pc), device_id_type=MESH)
            cp.start()
            sends.append(cp)
        for msk in range(1, 8):
            px = x ^ (msk >> 2)
            py = y ^ ((msk >> 1) & 1)
            pc = c ^ (msk & 1)
            pltpu.make_async_remote_copy(
                src_ref=s_ref, dst_ref=buf.at[4 * px + 2 * py + pc], send_sem=send_sems.at[msk - 1],
                recv_sem=recv_sems.at[msk - 1], device_id=(px, py, pc), device_id_type=MESH).wait_recv()
        for cp in sends:
            cp.wait_send()
        acc = buf[0]
        for d in range(1, 8):
            acc = acc + buf[d]
        o_ref[...] = acc

    vm = pl.BlockSpec(memory_space=pltpu.VMEM)
    return pl.pallas_call(
        body, name=name, in_specs=[vm], out_specs=vm, out_shape=jax.ShapeDtypeStruct((rows, cols), f32),
        scratch_shapes=[pltpu.VMEM((8, rows, cols), f32), pltpu.SemaphoreType.DMA((7,)),
                        pltpu.SemaphoreType.DMA((7,))],
        compiler_params=pltpu.CompilerParams(vmem_limit_bytes=VMEM_LIMIT),
    )(src)


def sum4_into(parts, dest, core, *, layer, total_rows, name):
    _, rows, cols = parts.shape
    tile = _pick(rows, max(16, (1 << 20) // (4 * cols) // 16 * 16), 16)
    per = rows // tile

    def body(c_ref, p_ref, *rest):
        del c_ref
        p = [p_ref[j].astype(f32) for j in range(4)]
        rest[-1][...] = ((p[0] + p[1]) + p[2]) + p[3]

    grid_spec = pltpu.PrefetchScalarGridSpec(
        num_scalar_prefetch=1, grid=(per,),
        in_specs=[pl.BlockSpec((4, tile, cols), lambda i, c_ref: (0, i, 0))] + ([] if dest is None else [_ANY]),
        out_specs=pl.BlockSpec((tile, cols), lambda i, c_ref: ((2 * layer + c_ref[0]) * per + i, 0)))
    return pl.pallas_call(
        body, name=name, grid_spec=grid_spec, out_shape=jax.ShapeDtypeStruct((total_rows, cols), f32),
        input_output_aliases={} if dest is None else {2: 0}, compiler_params=_cparams(("parallel",)),
    )(core, parts, *([] if dest is None else [dest]))


def _consts():
    idx = np.arange(RW)
    bd = (idx[:, None] // RN == idx[None, :] // RN).astype(np.float32)
    place = np.zeros((128, DQK), np.float32)
    place[np.arange(ROPE), NOPE + np.arange(ROPE)] = 1.0
    rot = np.zeros((DQK, DQK), np.float32)
    half = ROPE // 2
    rot[NOPE + half + np.arange(half), NOPE + np.arange(half)] = -1.0
    rot[NOPE + np.arange(half), NOPE + half + np.arange(half)] = 1.0
    return jnp.asarray(bd), jnp.asarray(place), jnp.asarray(rot)


def _rope_tables(positions):
    freqs = ROPE_THETA ** (-(jnp.arange(ROPE // 2, dtype=f32) * 2.0 / ROPE))
    ang = positions.astype(f32)[:, None] * freqs
    cos, sin = jnp.cos(ang), jnp.sin(ang)
    ones = jnp.ones((positions.shape[0], NOPE), f32)
    return (jnp.concatenate([ones, cos, cos], axis=-1), jnp.concatenate([0.0 * ones, sin, sin], axis=-1))


def derive_layer(w, l):
    w_in = w["w_in"][l]
    pad = jnp.zeros((D, MLA_PAD - MLA_COLS), w_in.dtype)
    wkv = w["mla_wkv_b"][l].reshape(KVL, MLA_H, NOPE + DV)
    wk = jnp.concatenate([wkv[:, :, :NOPE], jnp.zeros((KVL, MLA_H, ROPE), wkv.dtype)], axis=-1)
    return dict(
        gate=w_in[:, :GATE], mla=jnp.concatenate([w_in[:, GATE:GATE + MLA_COLS], pad], axis=1),
        rw=w_in[:, GATE + MLA_COLS:GATE + MLA_COLS + 4 * RW], cv=w_in[:, GATE + MLA_COLS + 4 * RW:],
        wq=w["mla_wq_b"][l].reshape(QL, MLA_H, DQK).transpose(1, 0, 2),
        wk=wk.transpose(1, 0, 2), wv=wkv[:, :, NOPE:].transpose(1, 0, 2),
        wo=w["mla_w_o"][l].reshape(MLA_H, DV, D),
        rwo=w["rwkv_w_o"][l], cvo=w["conv_w_o"][l], out=w["w_out"][l], up=w["w_up"][l], down=w["w_down"][l],
    )


W_IN_WINDOW_TILE = (0, 10, 21, 31)
W_IN_WINDOW = 1664
W_IN_SHARD = 1384


def w_in_window_cols(win, chip):
    gap = MLA_PAD - MLA_COLS
    branches = []
    for j in range(4):
        lo, hi = W_IN_SHARD * j, W_IN_SHARD * (j + 1)
        base = 128 * W_IN_WINDOW_TILE[j]
        cut = GATE + MLA_COLS
        if hi <= cut:
            branches.append(lambda w, a=lo - base: w[:, a:a + W_IN_SHARD])
        elif lo >= cut:
            branches.append(lambda w, a=lo + gap - base: w[:, a:a + W_IN_SHARD])
        else:
            branches.append(lambda w, a=lo - base, n1=cut - lo, b=cut + gap - base, n2=hi - cut:
                            jnp.concatenate([w[:, a:a + n1], w[:, b:b + n2]], axis=1))
    return lax.switch(chip, branches, win)


def chip_major_grads(g):
    padded = jnp.concatenate([g["gate"], g["mla"], g["rw"], g["cv"]], axis=1)
    w_in = jnp.stack([padded[:, 128 * t:128 * t + W_IN_WINDOW] for t in W_IN_WINDOW_TILE])
    heads = MLA_H // 4
    wq = g["wq"].reshape(4, heads, QL, DQK).transpose(0, 2, 1, 3).reshape(4, QL, heads * DQK)
    wkv = jnp.concatenate([g["wk"][:, :, :NOPE], g["wv"]], axis=-1)
    wkv = wkv.reshape(4, heads, KVL, NOPE + DV).transpose(0, 2, 1, 3).reshape(4, KVL, heads * (NOPE + DV))
    return dict(
        w_in=w_in, mla_wq_b=wq, mla_wkv_b=wkv, mla_w_o=g["wo"].reshape(4, MLA_H * DV, D // 4),
        rwkv_w_o=g["rwo"], conv_w_o=g["cvo"], w_out=g["out"].reshape(4, D // 4, D), w_up=g["up"],
        w_down=g["down"].reshape(4, DFF // 4, D),
    )


def _row(v):
    return v.reshape(1, -1)


def local_step(x, positions, target, w, sm):
    s_len = x.shape[0]
    t_row = _pick(s_len, 256, 8)
    t_wide = _pick(s_len, 128, 8)
    bd, place, rot = _consts()
    cos, sin = _rope_tables(positions)
    sds = lambda *shape: jax.ShapeDtypeStruct(shape, f32)
    sdb = lambda *shape: jax.ShapeDtypeStruct(shape, bf16)
    saved = []
    v_first = None
    for l in range(DEPTH):
        tag = f"l{l}_"
        lw = derive_layer(w, l)
        vres = l > 0
        p_norm1 = [_row(sm["attn_norm"][l])]
        (h,) = rows_fwd(_fn_norm, [x], p_norm1, [], [sds(s_len, D)], tile=t_row, name=tag + "norm1")
        gate = mm(h, lw["gate"], name=tag + "proj_gate")
        mla = mm(h, lw["mla"], name=tag + "proj_mla")
        rwc = mm(h, lw["rw"], name=tag + "proj_rwkv")
        cvc = mm(h, lw["cv"], name=tag + "proj_conv")
        p_mla = [_row(sm["mla_q_a_norm"][l]), _row(sm["mla_kv_a_norm"][l])]
        qn, kvn, kpe = rows_fwd(_fn_mla_prep, [mla], p_mla, [], [sdb(s_len, QL), sdb(s_len, KVL), sds(s_len, 128)],
                                tile=t_row, name=tag + "mla_prep")
        q_raw = mm(qn, lw["wq"], b_batched=True, name=tag + "q_proj")
        kn_pad = mm(kvn, lw["wk"], b_batched=True, name=tag + "k_proj")
        vv = mm(kvn, lw["wv"], b_batched=True, name=tag + "v_proj")
        p_qk = [_row(sm["mla_q_norm"][l]), _row(sm["mla_k_norm"][l])]
        q, k = rows_fwd(_fn_qk_post, [q_raw, kn_pad, kpe, cos, sin], p_qk, [place, rot],
                        [sds(MLA_H, s_len, DQK), sds(MLA_H, s_len, DQK)], tile=t_wide, name=tag + "qk_post")
        o = attn_fwd(q, k, vv, tq=_pick(s_len, 256, 8), name=tag + "attn")
        o_a = mm(o, lw["wo"], a_batched=True, b_batched=True, reduce_batch=True, name=tag + "o_a")
        p_rw = [_row(sm["rwkv_mu"][l]), _row(sm["rwkv_w0"][l]), w["rwkv_w2"][l], _row(sm["rwkv_a0"][l]),
                w["rwkv_a2"][l], w["rwkv_g2"][l], _row(sm["rwkv_k_k"][l]), _row(sm["rwkv_k_a"][l])]
        rw_rows, rw_halos = [rwc], (0,)
        if vres:
            p_rw += [w["rwkv_v1"][l - 1], _row(sm["rwkv_v_mu"][l - 1]), _row(sm["rwkv_v0"][l - 1]), w["rwkv_v2"][l - 1]]
            rw_rows, rw_halos = [rwc, h, v_first], (0, 1)
        fn_prep = _make_fn_rwkv_prep(vres)
        r, ld, k2, v, an, bn, g = rows_fwd(fn_prep, rw_rows, p_rw, [bd], [sds(s_len, RW)] * 7, tile=t_row,
                                           name=tag + "rwkv_prep", halos=rw_halos)
        if not vres:
            v_first = v
        y, states = wkv_fwd(r, ld, k2, v, an, bn, name=tag + "wkv")
        p_post = [_row(sm["rwkv_ln_w"][l]), _row(sm["rwkv_ln_b"][l]), _row(sm["rwkv_r_k"][l])]
        (yb,) = rows_fwd(_fn_rwkv_post, [y, r, k2, v, g], p_post, [bd], [sdb(s_len, RW)], tile=t_row,
                         name=tag + "rwkv_post")
        o_b = mm(yb, lw["rwo"], name=tag + "o_b")
        p_cv = [w["conv_w"][l][q:q + 1] for q in range(3)]
        (yc,) = rows_fwd(_fn_conv, [cvc], p_cv, [], [sdb(s_len, CW)], tile=t_row, name=tag + "conv", halos=(0,))
        o_c = mm(yc, lw["cvo"], name=tag + "o_c")
        (merged,) = rows_fwd(_fn_merge, [gate, o_a, o_b, o_c], [], [], [sdb(s_len, D)], tile=t_wide,
                             name=tag + "merge")
        x1 = mm(merged, lw["out"], add=x, name=tag + "out_proj")
        p_norm2 = [_row(sm["mlp_norm"][l])]
        (h2,) = rows_fwd(_fn_norm, [x1], p_norm2, [], [sdb(s_len, D)], tile=t_row, name=tag + "norm2")
        up, act = mm(h2, lw["up"], relu2_out=True, name=tag + "up")
        x2 = mm(act, lw["down"], add=x1, name=tag + "down")
        saved.append(dict(lw=lw, x=x, h=h, gate=gate, mla=mla, rwc=rwc, cvc=cvc, qn=qn, kvn=kvn, kpe=kpe,
                          q_raw=q_raw, kn_pad=kn_pad, vv=vv, q=q, k=k, o=o, o_a=o_a, r=r, ld=ld, k2=k2, v=v,
                          an=an, bn=bn, g=g, y=y, states=states, yb=yb, o_b=o_b, yc=yc, o_c=o_c, merged=merged,
                          x1=x1, h2=h2, up=up, act=act, p_norm1=p_norm1, p_mla=p_mla, p_qk=p_qk, p_rw=p_rw,
                          p_post=p_post, p_cv=p_cv, p_norm2=p_norm2, rw_rows=rw_rows, rw_halos=rw_halos,
                          fn_prep=fn_prep, v_first=v_first if vres else None))
        x = x2

    loss, dx = loss_head(x, target, tile=t_row, name="loss_head")

    grads = {n: [None] * (DEPTH - 1 if n in ("rwkv_v1", "rwkv_v_mu", "rwkv_v0", "rwkv_v2") else DEPTH)
             for n in WEIGHTS}
    dv_first = None
    for l in reversed(range(DEPTH)):
        tag = f"b{l}_"
        sv = saved[l]
        lw = sv["lw"]
        vres = l > 0
        g_down = mm(sv["act"], dx, ta=True, out_dtype=bf16, name=tag + "g_down")
        dup = mm(dx, lw["down"], tb=True, act_grad=sv["up"], out_dtype=bf16, name=tag + "d_up")
        g_up = mm(sv["h2"], dup, ta=True, n_split=4, out_dtype=bf16, name=tag + "g_up")
        dh2 = mm(dup, lw["up"], tb=True, name=tag + "d_h2")
        (dx1,), (g_n2,) = rows_bwd(_fn_norm, [sv["x1"]], sv["p_norm2"], [], [[dh2]], tile=t_row,
                                   name=tag + "norm2", extra={0: [dx]})
        g_out = mm(sv["merged"], dx1, ta=True, out_dtype=bf16, name=tag + "g_out")
        dmerged = mm(dx1, lw["out"], tb=True, name=tag + "d_merged")
        (dgate, do_a, do_b, do_c), _ = rows_bwd(_fn_merge, [sv["gate"], sv["o_a"], sv["o_b"], sv["o_c"]], [], [],
                                                [[dmerged]], tile=t_wide, name=tag + "merge",
                                                grad_dtypes=[bf16] * 4)
        g_cvo = mm(sv["yc"], do_c, ta=True, n_split=4, out_dtype=bf16, name=tag + "g_cvo")
        dyc = mm(do_c, lw["cvo"], tb=True, name=tag + "d_yc")
        (dcvc,), g_cw = rows_bwd(_fn_conv, [sv["cvc"]], sv["p_cv"], [], [[dyc]], tile=t_row, name=tag + "conv",
                                    halos=(0,))
        g_rwo = mm(sv["yb"], do_b, ta=True, n_split=4, out_dtype=bf16, name=tag + "g_rwo")
        dyb = mm(do_b, lw["rwo"], tb=True, name=tag + "d_yb")
        (dy, dr_p, dk_p, dv_p, dg), g_post = rows_bwd(
            _fn_rwkv_post, [sv["y"], sv["r"], sv["k2"], sv["v"], sv["g"]], sv["p_post"], [bd], [[dyb]], tile=t_row,
            name=tag + "rwkv_post")
        dr_s, dld, dk_s, dv_s, dan, dbn = wkv_bwd(sv["r"], sv["ld"], sv["k2"], sv["v"], sv["an"], sv["bn"],
                                                  sv["states"], dy, name=tag + "wkv")
        dv_list = [dv_s, dv_p] + ([dv_first] if (not vres and dv_first is not None) else [])
        d_prep, g_prep = rows_bwd(
            sv["fn_prep"], sv["rw_rows"], sv["p_rw"], [bd],
            [[dr_s, dr_p], [dld], [dk_s, dk_p], dv_list, [dan], [dbn], [dg]], tile=t_row, name=tag + "rwkv_prep",
            halos=sv["rw_halos"])
        drwc = d_prep[0]
        dh_extra = []
        if vres:
            dh_extra = [d_prep[1]]
            dv_first = d_prep[2]
        g_wo = mm(sv["o"], do_a, ta=True, a_batched=True, n_split=4, tk=s_len, out_dtype=bf16, name=tag + "g_wo")
        do = mm(do_a, lw["wo"], tb=True, b_batched=True, name=tag + "d_o")
        dq, dk, dvv = attn_bwd(sv["q"], sv["k"], sv["vv"], do, tq=_pick(s_len, 256, 8), name=tag + "attn")
        (dq_raw, dkn_pad, dkpe), g_qk = rows_bwd(
            _fn_qk_post, [sv["q_raw"], sv["kn_pad"], sv["kpe"], cos, sin], sv["p_qk"], [place, rot], [[dq], [dk]],
            tile=t_wide, name=tag + "qk_post", grad_rows=[0, 1, 2], grad_dtypes=[bf16, bf16, f32])
        g_wq = mm(sv["qn"], dq_raw, ta=True, b_batched=True, tk=s_len, out_dtype=bf16, name=tag + "g_wq")
        g_wk = mm(sv["kvn"], dkn_pad, ta=True, b_batched=True, tk=s_len, out_dtype=bf16, name=tag + "g_wk")
        g_wv = mm(sv["kvn"], dvv, ta=True, b_batched=True, tk=s_len, out_dtype=bf16, name=tag + "g_wv")
        dqn = mm(dq_raw, lw["wq"], tb=True, a_batched=True, b_batched=True, reduce_batch=True, name=tag + "d_qn")
        dkvn = mm(dkn_pad, lw["wk"], tb=True, a_batched=True, b_batched=True, reduce_batch=True, name=tag + "d_kvn_k")
        dkvn = mm(dvv, lw["wv"], tb=True, a_batched=True, b_batched=True, reduce_batch=True, add=dkvn,
                  name=tag + "d_kvn_v")
        (dmla,), g_mla = rows_bwd(_fn_mla_prep, [sv["mla"]], sv["p_mla"], [], [[dqn], [dkvn], [dkpe]], tile=t_row,
                                  name=tag + "mla_prep", grad_dtypes=[bf16])
        g_gate = mm(sv["h"], dgate, ta=True, out_dtype=bf16, name=tag + "g_gate")
        g_mlaw = mm(sv["h"], dmla, ta=True, out_dtype=bf16, name=tag + "g_mla")
        g_rw = mm(sv["h"], drwc, ta=True, out_dtype=bf16, name=tag + "g_rw")
        g_cv = mm(sv["h"], dcvc, ta=True, out_dtype=bf16, name=tag + "g_cv")
        dh = mm(dgate, lw["gate"], tb=True, name=tag + "d_h_gate")
        dh = mm(dmla, lw["mla"], tb=True, add=dh, name=tag + "d_h_mla")
        dh = mm(drwc, lw["rw"], tb=True, add=dh, name=tag + "d_h_rw")
        dh = mm(dcvc, lw["cv"], tb=True, add=dh, name=tag + "d_h_cv")
        (dx,), (g_n1,) = rows_bwd(_fn_norm, [sv["x"]], sv["p_norm1"], [], [[dh] + dh_extra], tile=t_row,
                                  name=tag + "norm1", extra={0: [dx1]})
        slabs = chip_major_grads(dict(gate=g_gate, mla=g_mlaw, rw=g_rw, cv=g_cv, wq=g_wq, wk=g_wk, wv=g_wv, wo=g_wo,
                                      rwo=g_rwo, cvo=g_cvo, out=g_out, up=g_up, down=g_down))
        for n, val in slabs.items():
            grads[n][l] = val
        grads["attn_norm"][l], grads["mlp_norm"][l] = g_n1, g_n2
        grads["mla_q_a_norm"][l], grads["mla_kv_a_norm"][l] = g_mla
        grads["mla_q_norm"][l], grads["mla_k_norm"][l] = g_qk
        grads["rwkv_ln_w"][l], grads["rwkv_ln_b"][l], grads["rwkv_r_k"][l] = g_post
        for n, val in zip(["rwkv_mu", "rwkv_w0", "rwkv_w2", "rwkv_a0", "rwkv_a2", "rwkv_g2", "rwkv_k_k", "rwkv_k_a"],
                          g_prep[:8]):
            grads[n][l] = val
        if vres:
            for n, val in zip(["rwkv_v1", "rwkv_v_mu", "rwkv_v0", "rwkv_v2"], g_prep[8:12]):
                grads[n][l - 1] = val
        grads["conv_w"][l] = jnp.concatenate(g_cw, axis=0)
    return loss, dx, grads


def _split3(a):
    hi = a.astype(bf16)
    r1 = a - hi.astype(f32)
    mid = r1.astype(bf16)
    lo = (r1 - mid.astype(f32)).astype(bf16)
    return hi, mid, lo


def _shard_axis(name):
    return 1 if name in ROW_SHARDED else 2


def _pack(pieces, width, dtype, row_align):
    flat = jnp.concatenate([p.reshape(-1).astype(dtype) for p in pieces])
    rows = -(-flat.shape[0] // width)
    rows = -(-rows // row_align) * row_align
    return jnp.pad(flat, (0, rows * width - flat.shape[0])).reshape(rows, width)


def _unpack(flat2d, shapes):
    flat = flat2d.reshape(-1)
    out, off = [], 0
    for shp in shapes:
        n = int(np.prod(shp))
        out.append(flat[off:off + n].reshape(shp))
        off += n
    return out


def kernel(x, positions, attn_norm, w_in, mla_q_a_norm, mla_wq_b, mla_kv_a_norm, mla_wkv_b, mla_q_norm, mla_k_norm, mla_w_o, rwkv_mu, rwkv_w0, rwkv_w2, rwkv_a0, rwkv_a2, rwkv_g2, rwkv_k_k, rwkv_k_a, rwkv_r_k, rwkv_ln_w, rwkv_ln_b, rwkv_w_o, rwkv_v1, rwkv_v_mu, rwkv_v0, rwkv_v2, conv_w, conv_w_o, w_out, mlp_norm, w_up, w_down, loss_target, m_attn_norm, m_w_in, m_mla_q_a_norm, m_mla_wq_b, m_mla_kv_a_norm, m_mla_wkv_b, m_mla_q_norm, m_mla_k_norm, m_mla_w_o, m_rwkv_mu, m_rwkv_w0, m_rwkv_w2, m_rwkv_a0, m_rwkv_a2, m_rwkv_g2, m_rwkv_k_k, m_rwkv_k_a, m_rwkv_r_k, m_rwkv_ln_w, m_rwkv_ln_b, m_rwkv_w_o, m_rwkv_v1, m_rwkv_v_mu, m_rwkv_v0, m_rwkv_v2, m_conv_w, m_conv_w_o, m_w_out, m_mlp_norm, m_w_up, m_w_down, v_attn_norm, v_w_in, v_mla_q_a_norm, v_mla_wq_b, v_mla_kv_a_norm, v_mla_wkv_b, v_mla_q_norm, v_mla_k_norm, v_mla_w_o, v_rwkv_mu, v_rwkv_w0, v_rwkv_w2, v_rwkv_a0, v_rwkv_a2, v_rwkv_g2, v_rwkv_k_k, v_rwkv_k_a, v_rwkv_r_k, v_rwkv_ln_w, v_rwkv_ln_b, v_rwkv_w_o, v_rwkv_v1, v_rwkv_v_mu, v_rwkv_v0, v_rwkv_v2, v_conv_w, v_conv_w_o, v_w_out, v_mlp_norm, v_w_up, v_w_down):
    args = dict(locals())
    wts = {n: args[n] for n in WEIGHTS}
    mom = {n: args["m_" + n] for n in WEIGHTS}
    var = {n: args["v_" + n] for n in WEIGHTS}
    chip = 2 * lax.axis_index("x") + lax.axis_index("y")
    core = lax.axis_index("c").astype(jnp.int32).reshape(1)

    med_names = [n for n in MED if n != "conv_w"]
    med_pieces = [wts[n] for n in med_names] + list(_split3(wts["conv_w"]))
    med_shapes = [p.shape for p in med_pieces]
    shards = [wts[n].astype(bf16).reshape(-1, wts[n].shape[-1]) for n in BIG] + [_pack(med_pieces, 128, bf16, 32)]
    got = gather_weights(shards, name="gather_weights")
    chip_idx = chip.astype(jnp.int32).reshape(1)
    got = [place_slab(g, s, chip_idx, name=f"place_own_{q}") for q, (g, s) in enumerate(zip(got, shards))]
    whole = {}
    for q, n in enumerate(BIG):
        depth, rows, cols = wts[n].shape
        by_chip = got[q].reshape(4, depth, rows, cols)
        if n in ROW_SHARDED:
            whole[n] = by_chip.transpose(1, 0, 2, 3).reshape(depth, 4 * rows, cols)
        else:
            whole[n] = by_chip.transpose(1, 2, 0, 3).reshape(depth, rows, 4 * cols)
    per_chip = [_unpack(got[len(BIG)][j], med_shapes) for j in range(4)]
    for q, n in enumerate(med_names):
        whole[n] = jnp.concatenate([per_chip[j][q] for j in range(4)], axis=_shard_axis(n)).astype(f32)
    base = len(med_names)
    cw_parts = [jnp.concatenate([per_chip[j][base + t] for j in range(4)], axis=2).astype(f32) for t in range(3)]
    whole["conv_w"] = (cw_parts[0] + cw_parts[1]) + cw_parts[2]
    small = {n: wts[n] for n in SMALL}
    small["rwkv_r_k"] = wts["rwkv_r_k"].reshape(DEPTH, RW)

    loss, grad_x, grads = local_step(x[0], positions[0], loss_target[0], whole, small)
    loss = lax.psum(loss, ("x", "y", "c"))

    sm_names = SMALL + MED
    sm_grads = [jnp.stack(grads[n]) for n in sm_names]
    sm_shapes = [g.shape for g in sm_grads]
    sm_sum = _unpack(all_reduce_small(_pack(sm_grads, 128, f32, 8), name="reduce_small"), sm_shapes)
    gsum = {}
    for n, g in zip(sm_names, sm_sum):
        if n in MED:
            ax = _shard_axis(n)
            width = wts[n].shape[ax]
            g = lax.dynamic_slice_in_dim(g, chip * width, width, axis=ax)
        gsum[n] = g.reshape(wts[n].shape)
    slabs = [grads[n][l] for n in BIG for l in range(DEPTH)]
    labels = [f"{n}_{l}" for n in BIG for l in range(DEPTH)]
    from_sibling = grads_to_sibling(slabs, name="grads_to_sibling")
    chip_sums = [pair_sum(s, t, core, name="pair_sum_" + lb) for s, t, lb in zip(slabs, from_sibling, labels)]
    arrived = chips_all_to_all(chip_sums, name="scatter_grads")
    bufs, layout = [], []
    for q in range(len(BIG)):
        rows = slabs[q * DEPTH].shape[1]
        buf = None
        for l in range(DEPTH):
            buf = sum4_into(arrived[q * DEPTH + l], buf, core, layer=l, total_rows=DEPTH * rows,
                            name="sum_chips_" + labels[q * DEPTH + l])
            layout.append((q, l * rows, rows))
        bufs.append(buf)
    reduced = join_halves(bufs, layout, name="join_halves")

    out_g, out_d, out_m, out_v = {}, {}, {}, {}
    for q, n in enumerate(BIG):
        shp = wts[n].shape
        as2d = lambda a: a.reshape(-1, shp[-1])
        g2d = w_in_window_cols(reduced[q], chip) if n == "w_in" else reduced[q]
        res = adamw(as2d(wts[n]), g2d, as2d(mom[n]), as2d(var[n]), name="adamw_" + n)
        out_g[n], out_d[n], out_m[n], out_v[n] = [r.reshape(shp) for r in res]
    sm_all = SMALL + MED
    sm_shapes2 = [wts[n].shape for n in sm_all]
    res = adamw(_pack([wts[n] for n in sm_all], 128, f32, 8), _pack([gsum[n] for n in sm_all], 128, f32, 8),
                _pack([mom[n] for n in sm_all], 128, f32, 8), _pack([var[n] for n in sm_all], 128, f32, 8),
                name="adamw_small")
    for tgt, flat in zip((out_g, out_d, out_m, out_v), res):
        for n, val in zip(sm_all, _unpack(flat, sm_shapes2)):
            tgt[n] = val
    return (loss, grad_x[None], *[out_g[n] for n in WEIGHTS], *[out_d[n] for n in WEIGHTS],
            *[out_m[n] for n in WEIGHTS], *[out_v[n] for n in WEIGHTS])
```

```python
import functools

import jax
import jax.numpy as jnp
import numpy as np
from jax import lax
from jax.experimental import pallas as pl
from jax.experimental.pallas import tpu as pltpu

f32, bf16 = jnp.float32, jnp.bfloat16
HI = lax.Precision.HIGHEST
MESH = pl.DeviceIdType.MESH

D = 1024
DEPTH = 2
MLA_H, NOPE, ROPE, DQK, DV = 8, 64, 32, 96, 64
QL, KVL = 384, 256
RW, RH, RN = 256, 4, 64
DL, AL, GL, MVL = 64, 64, 128, 32
CW = 256
DFF = 4096
GATE = 3 * D
MLA_COLS = QL + KVL + ROPE
MLA_PAD = 768
NORM_EPS = 1e-6
GN_EPS = 64e-5
ROPE_THETA = 10000.0
LR, B1, B2, EPS, WD, STEP = 0.001, 0.9, 0.999, 1e-08, 0.01, 10

VMEM_LIMIT = 52 * 1024 * 1024
WKV_CHUNK = 64
WKV_CHUNKS_PER_STEP = 2
ATTN_SEGMENTS = 4

BIG = ["w_in", "mla_wq_b", "mla_wkv_b", "mla_w_o", "rwkv_w_o", "conv_w_o", "w_out", "w_up", "w_down"]
MED = ["rwkv_w2", "rwkv_a2", "rwkv_g2", "rwkv_v1", "rwkv_v2", "conv_w"]
ROW_SHARDED = {"w_out", "w_down", "rwkv_v1"}
SMALL = ["attn_norm", "mla_q_a_norm", "mla_kv_a_norm", "mla_q_norm", "mla_k_norm", "rwkv_mu", "rwkv_w0",
         "rwkv_a0", "rwkv_k_k", "rwkv_k_a", "rwkv_r_k", "rwkv_ln_w", "rwkv_ln_b", "rwkv_v_mu", "rwkv_v0",
         "mlp_norm"]
WEIGHTS = ["attn_norm", "w_in", "mla_q_a_norm", "mla_wq_b", "mla_kv_a_norm", "mla_wkv_b", "mla_q_norm",
           "mla_k_norm", "mla_w_o", "rwkv_mu", "rwkv_w0", "rwkv_w2", "rwkv_a0", "rwkv_a2", "rwkv_g2",
           "rwkv_k_k", "rwkv_k_a", "rwkv_r_k", "rwkv_ln_w", "rwkv_ln_b", "rwkv_w_o", "rwkv_v1", "rwkv_v_mu",
           "rwkv_v0", "rwkv_v2", "conv_w", "conv_w_o", "w_out", "mlp_norm", "w_up", "w_down"]


def _cparams(sem=None):
    return pltpu.CompilerParams(dimension_semantics=sem, vmem_limit_bytes=VMEM_LIMIT)


def _pick(dim, pref, align):
    if dim <= pref:
        return dim
    t = (pref // align) * align
    while t >= align:
        if dim % t == 0:
            return t
        t -= align
    return dim


def _bdot(a, b, dims):
    return lax.dot_general(a.astype(bf16), b.astype(bf16), (dims, ((), ())), preferred_element_type=f32)


@jax.custom_vjp
def _mm(a, b):
    return _bdot(a, b, ((1,), (0,)))


def _mm_fwd(a, b):
    return _mm(a, b), (a, b)


def _mm_bwd(res, g):
    a, b = res
    return _bdot(g, b, ((1,), (1,))), _bdot(a, g, ((0,), (0,)))


_mm.defvjp(_mm_fwd, _mm_bwd)


@jax.custom_vjp
def _mm_nt(a, b):
    return _bdot(a, b, ((1,), (1,)))


def _mm_nt_fwd(a, b):
    return _mm_nt(a, b), (a, b)


def _mm_nt_bwd(res, g):
    a, b = res
    return _bdot(g, b, ((1,), (0,))), _bdot(g, a, ((0,), (0,)))


_mm_nt.defvjp(_mm_nt_fwd, _mm_nt_bwd)


_NN, _NT, _TN = ((1,), (0,)), ((1,), (1,)), ((0,), (0,))


def _dg(a, b, dims):
    return lax.dot_general(a, b, (dims, ((), ())), preferred_element_type=f32)


def _bf16_pieces(x, count):
    out, rest = [], x
    for q in range(count):
        piece = rest.astype(bf16)
        out.append(piece)
        if q + 1 < count:
            rest = rest - piece.astype(f32)
    return out


def _dot3(a, b, dims):
    (ah, al), (bh, bl) = _bf16_pieces(a, 2), _bf16_pieces(b, 2)
    return _dg(ah, bh, dims) + (_dg(ah, bl, dims) + _dg(al, bh, dims))


@jax.custom_vjp
def _hdot(a, b):
    return _dot3(a, b, _NN)


@jax.custom_vjp
def _hdot_nt(a, b):
    return _dot3(a, b, _NT)


@jax.custom_vjp
def _hdot_tn(a, b):
    return _dot3(a, b, _TN)


_hdot.defvjp(lambda a, b: (_hdot(a, b), (a, b)), lambda res, g: (_hdot_nt(g, res[1]), _hdot_tn(res[0], g)))
_hdot_nt.defvjp(lambda a, b: (_hdot_nt(a, b), (a, b)), lambda res, g: (_hdot(g, res[1]), _hdot_tn(g, res[0])))
_hdot_tn.defvjp(lambda a, b: (_hdot_tn(a, b), (a, b)), lambda res, g: (_hdot_nt(res[1], g), _hdot(res[0], g)))


_BNN, _BNT, _BTN = ((2,), (1,)), ((2,), (2,)), ((1,), (1,))


def _bdg(a, b, dims):
    return lax.dot_general(a, b, (dims, ((0,), (0,))), preferred_element_type=f32)


def _bdot3(a, b, dims):
    (ah, al), (bh, bl) = _bf16_pieces(a, 2), _bf16_pieces(b, 2)
    return _bdg(ah, bh, dims) + (_bdg(ah, bl, dims) + _bdg(al, bh, dims))


@jax.custom_vjp
def _hbnn(a, b):
    return _bdot3(a, b, _BNN)


@jax.custom_vjp
def _hbnt(a, b):
    return _bdot3(a, b, _BNT)


@jax.custom_vjp
def _hbtn(a, b):
    return _bdot3(a, b, _BTN)


_hbnn.defvjp(lambda a, b: (_hbnn(a, b), (a, b)), lambda res, g: (_hbnt(g, res[1]), _hbtn(res[0], g)))
_hbnt.defvjp(lambda a, b: (_hbnt(a, b), (a, b)), lambda res, g: (_hbnn(g, res[1]), _hbtn(g, res[0])))
_hbtn.defvjp(lambda a, b: (_hbtn(a, b), (a, b)), lambda res, g: (_hbnt(res[1], g), _hbnn(res[0], g)))


@functools.partial(jax.custom_vjp, nondiff_argnums=(2,))
def _exact_bl(m, x, transposed):
    mb = m.astype(bf16)
    hi, mid, lo = _bf16_pieces(x, 3)
    dims = _BTN if transposed else _BNN
    return (_bdg(mb, hi, dims) + _bdg(mb, mid, dims)) + _bdg(mb, lo, dims)


_exact_bl.defvjp(lambda m, x, transposed: (_exact_bl(m, x, transposed), m),
                 lambda transposed, m, g: (jnp.zeros_like(m), _exact_bl(m, g, not transposed)))


@functools.partial(jax.custom_vjp, nondiff_argnums=(2,))
def _exact_l(m, x, transposed):
    mb = m.astype(bf16)
    hi, mid, lo = _bf16_pieces(x, 3)
    dims = _TN if transposed else _NN
    return (_dg(mb, hi, dims) + _dg(mb, mid, dims)) + _dg(mb, lo, dims)


_exact_l.defvjp(lambda m, x, transposed: (_exact_l(m, x, transposed), m),
                lambda transposed, m, g: (jnp.zeros_like(m), _exact_l(m, g, not transposed)))


@functools.partial(jax.custom_vjp, nondiff_argnums=(2,))
def _exact_r(x, m, transposed):
    mb = m.astype(bf16)
    hi, mid, lo = _bf16_pieces(x, 3)
    dims = _NT if transposed else _NN
    return (_dg(hi, mb, dims) + _dg(mid, mb, dims)) + _dg(lo, mb, dims)


_exact_r.defvjp(lambda x, m, transposed: (_exact_r(x, m, transposed), m),
                lambda transposed, m, g: (_exact_r(g, m, not transposed), jnp.zeros_like(m)))


def _rms(x, g, eps=NORM_EPS):
    return x * lax.rsqrt(jnp.mean(x * x, axis=-1, keepdims=True) + eps) * g


def _sigmoid(x):
    return 1.0 / (1.0 + jnp.exp(-x))


def _softplus(x):
    return jnp.maximum(x, 0.0) + jnp.log(1.0 + jnp.exp(-jnp.maximum(x, -x)))


def _lane_split(x, sizes):
    bounds = np.cumsum([0] + list(sizes))

    @jax.custom_vjp
    def split(v):
        return tuple(v[..., int(bounds[q]):int(bounds[q + 1])] for q in range(len(sizes)))

    split.defvjp(lambda v: (split(v), None), lambda _, g: (jnp.concatenate(g, axis=-1),))
    return split(x)


def _row_split(x, sizes):
    bounds = np.cumsum([0] + list(sizes))

    @jax.custom_vjp
    def split(v):
        return tuple(v[..., int(bounds[q]):int(bounds[q + 1]), :] for q in range(len(sizes)))

    split.defvjp(lambda v: (split(v), None), lambda _, g: (jnp.concatenate(g, axis=-2),))
    return split(x)


def _unstack(x):
    @jax.custom_vjp
    def unstack(v):
        return tuple(v[q] for q in range(v.shape[0]))

    unstack.defvjp(lambda v: (unstack(v), None), lambda _, g: (jnp.stack(g, axis=0),))
    return unstack(x)


def _shift_mats(t, k):
    r = lax.broadcasted_iota(jnp.int32, (t, t), 0)
    c = lax.broadcasted_iota(jnp.int32, (t, t), 1)
    inner = (r - c == k).astype(f32)
    r8 = lax.broadcasted_iota(jnp.int32, (t, 8), 0)
    c8 = lax.broadcasted_iota(jnp.int32, (t, 8), 1)
    edge = (c8 - r8 == 8 - k).astype(f32)
    return inner, edge


def _shift(x, halo, k):
    inner, edge = _shift_mats(x.shape[0], k)
    return _exact_l(inner, x, False) + jnp.dot(edge, halo, precision=HI, preferred_element_type=f32)


def mm(a, b, *, name, ta=False, tb=False, a_batched=False, b_batched=False, reduce_batch=False, add=None,
       act_grad=None, relu2_out=False, n_split=1, out_dtype=f32, tm=1024, tn=1024, tk=1024):
    ash, bsh = a.shape[-2:], b.shape[-2:]
    (k_, m_) = ash if ta else ash[::-1]
    (k2_, n_) = bsh[::-1] if tb else bsh
    assert k_ == k2_, (a.shape, b.shape, ta, tb)
    hb = a.shape[0] if a_batched else (b.shape[0] if b_batched else 1)
    batched_out = (a_batched or b_batched) and not reduce_batch
    h_out = hb if batched_out else 1
    h_red = hb if reduce_batch else 1
    tm = _pick(m_, tm, 128)
    tn = _pick(n_ // n_split, tn, 128)
    tk = _pick(k_, tk, 128)
    nm, nn, nk = m_ // tm, n_ // tn, k_ // tk

    def a_map(i, j, ho, hr, kk):
        blk = (kk, i) if ta else (i, kk)
        return ((ho if batched_out else hr),) + blk if a_batched else blk

    def b_map(i, j, ho, hr, kk):
        blk = (j, kk) if tb else (kk, j)
        return ((ho if batched_out else hr),) + blk if b_batched else blk

    a_blk = (tk, tm) if ta else (tm, tk)
    b_blk = (tn, tk) if tb else (tk, tn)
    in_specs = [pl.BlockSpec(((1,) + a_blk) if a_batched else a_blk, a_map),
                pl.BlockSpec(((1,) + b_blk) if b_batched else b_blk, b_map)]
    args = [a, b]
    for extra in (add, act_grad):
        if extra is not None:
            in_specs.append(pl.BlockSpec((tm, tn), lambda i, j, ho, hr, kk: (i, j)))
            args.append(extra)
    if n_split > 1:
        per = n_ // n_split // tn
        if batched_out:
            out_spec = pl.BlockSpec((1, 1, tm, tn), lambda i, j, ho, hr, kk: (j // per, ho, i, j % per))
            out_shape = jax.ShapeDtypeStruct((n_split, hb, m_, n_ // n_split), out_dtype)
        else:
            out_spec = pl.BlockSpec((1, tm, tn), lambda i, j, ho, hr, kk: (j // per, i, j % per))
            out_shape = jax.ShapeDtypeStruct((n_split, m_, n_ // n_split), out_dtype)
    elif batched_out:
        out_spec = pl.BlockSpec((1, tm, tn), lambda i, j, ho, hr, kk: (ho, i, j))
        out_shape = jax.ShapeDtypeStruct((hb, m_, n_), out_dtype)
    else:
        out_spec = pl.BlockSpec((tm, tn), lambda i, j, ho, hr, kk: (i, j))
        out_shape = jax.ShapeDtypeStruct((m_, n_), out_dtype)
    lead = (0,) * (int(batched_out) + int(n_split > 1))
    dims = ((0,) if ta else (1,), (1,) if tb else (0,))
    has_add, has_act = add is not None, act_grad is not None

    def body(*refs):
        a_ref, b_ref = refs[0], refs[1]
        pos = 2
        add_ref = act_ref = None
        if has_add:
            add_ref = refs[pos]
            pos += 1
        if has_act:
            act_ref = refs[pos]
            pos += 1
        o_ref, acc_ref = refs[pos], refs[-1]
        hr, kk = pl.program_id(3), pl.program_id(4)
        first = jnp.logical_and(hr == 0, kk == 0)
        last = jnp.logical_and(hr == h_red - 1, kk == nk - 1)
        av = a_ref[0] if a_batched else a_ref[...]
        bv = b_ref[0] if b_batched else b_ref[...]
        p = _bdot(av, bv, dims)

        @pl.when(first)
        def _():
            acc_ref[...] = p

        @pl.when(jnp.logical_not(first))
        def _():
            acc_ref[...] += p

        @pl.when(last)
        def _():
            r = acc_ref[...]
            if has_act:
                r = r * (2.0 * jnp.maximum(act_ref[...], 0.0))
            if has_add:
                r = r + add_ref[...]
            if lead:
                o_ref[lead] = r.astype(out_dtype)
            else:
                o_ref[...] = r.astype(out_dtype)
            if relu2_out:
                refs[pos + 1][...] = jnp.square(jnp.maximum(r, 0.0)).astype(bf16)

    if relu2_out:
        assert not lead
        out_spec = [out_spec, out_spec]
        out_shape = [out_shape, jax.ShapeDtypeStruct(out_shape.shape, bf16)]
    return pl.pallas_call(
        body, name=name, grid=(nm, nn, h_out, h_red, nk), in_specs=in_specs, out_specs=out_spec,
        out_shape=out_shape, scratch_shapes=[pltpu.VMEM((tm, tn), f32)],
        compiler_params=_cparams(("parallel", "parallel", "parallel", "arbitrary", "arbitrary")),
    )(*args)


def _row_spec(arr, tile, idx):
    if arr.ndim == 2:
        return pl.BlockSpec((tile, arr.shape[1]), lambda i: (idx(i), 0))
    return pl.BlockSpec((arr.shape[0], tile, arr.shape[2]), lambda i: (0, idx(i), 0))


def _halo_spec(arr, tile, idx):
    per = tile // 8
    return pl.BlockSpec((8, arr.shape[1]), lambda i: (jnp.maximum(idx(i) * per - 1, 0), 0))


def _full_spec(arr):
    nd = arr.ndim
    return pl.BlockSpec(arr.shape, lambda i: (0,) * nd)


def _load_f32(ref):
    val = ref[...]
    return val.astype(f32) if val.dtype == bf16 else val


def rows_fwd(fn, rows, params, consts, out_shapes, *, tile, name, halos=()):
    s_len = rows[0].shape[-2]
    n = s_len // tile
    nr, nh, npar, nc = len(rows), len(halos), len(params), len(consts)
    ident = lambda i: i
    in_specs = ([_row_spec(r, tile, ident) for r in rows] + [_halo_spec(rows[h], tile, ident) for h in halos]
                + [_full_spec(p) for p in params] + [_full_spec(c) for c in consts])
    out_specs = [_row_spec(o, tile, ident) for o in out_shapes]

    def body(*refs):
        i = pl.program_id(0)
        rv = [_load_f32(r) for r in refs[:nr]]
        keep = (i > 0).astype(f32)
        hv = [r[...] * keep for r in refs[nr:nr + nh]]
        pv = [r[...] for r in refs[nr + nh:nr + nh + npar]]
        cv = [r[...] for r in refs[nr + nh + npar:nr + nh + npar + nc]]
        outs = fn(rv, hv, pv, cv)
        for o_ref, o in zip(refs[nr + nh + npar + nc:], outs):
            o_ref[...] = o.astype(o_ref.dtype)

    return pl.pallas_call(
        body, name=name, grid=(n,), in_specs=in_specs, out_specs=out_specs, out_shape=list(out_shapes),
        compiler_params=_cparams(("arbitrary",)),
    )(*rows, *[rows[h] for h in halos], *params, *consts)


def rows_bwd(fn, rows, params, consts, douts, *, tile, name, halos=(), grad_rows=None, extra=None,
             grad_dtypes=None):
    s_len = rows[0].shape[-2]
    n = s_len // tile
    nr, nh, npar, nc = len(rows), len(halos), len(params), len(consts)
    grad_rows = list(range(nr)) if grad_rows is None else list(grad_rows)
    extra = extra or {}
    assert all(h in grad_rows for h in halos)
    rev = lambda i: n - 1 - i
    dflat = [d for ds in douts for d in ds]
    dcount = [len(ds) for ds in douts]
    eflat = [e for g in grad_rows for e in extra.get(g, [])]
    ecount = [len(extra.get(g, [])) for g in grad_rows]
    in_specs = ([_row_spec(r, tile, rev) for r in rows] + [_halo_spec(rows[h], tile, rev) for h in halos]
                + [_full_spec(p) for p in params] + [_full_spec(c) for c in consts]
                + [_row_spec(d, tile, rev) for d in dflat] + [_row_spec(e, tile, rev) for e in eflat])
    grad_dtypes = [f32] * len(grad_rows) if grad_dtypes is None else list(grad_dtypes)
    assert all(grad_dtypes[q] == f32 for q, g in enumerate(grad_rows) if g in halos)
    out_shapes = ([jax.ShapeDtypeStruct(rows[g].shape, dt) for g, dt in zip(grad_rows, grad_dtypes)]
                  + [jax.ShapeDtypeStruct(p.shape, f32) for p in params])
    out_specs = [_row_spec(rows[g], tile, rev) for g in grad_rows] + [_full_spec(p) for p in params]
    scratch = [pltpu.VMEM((8, rows[h].shape[1]), f32) for h in halos]
    n_in = nr + nh + npar + nc + len(dflat) + len(eflat)
    n_out = len(grad_rows) + npar

    def body(*refs):
        i = pl.program_id(0)
        rv = [_load_f32(r) for r in refs[:nr]]
        keep = (i < n - 1).astype(f32)
        hv = [r[...] * keep for r in refs[nr:nr + nh]]
        pv = [r[...] for r in refs[nr + nh:nr + nh + npar]]
        pos = nr + nh + npar
        cv = [r[...] for r in refs[pos:pos + nc]]
        pos += nc
        dv = []
        for cnt in dcount:
            acc = _load_f32(refs[pos])
            for q in range(1, cnt):
                acc = acc + _load_f32(refs[pos + q])
            dv.append(acc)
            pos += cnt
        ev = []
        for cnt in ecount:
            ev.append([_load_f32(refs[pos + q]) for q in range(cnt)])
            pos += cnt
        out_refs = refs[n_in:n_in + n_out]
        carry_refs = refs[n_in + n_out:]

        def f(gr, gh, gp):
            full = list(rv)
            for g, val in zip(grad_rows, gr):
                full[g] = val
            return tuple(fn(full, gh, gp, cv))

        _, vjp = jax.vjp(f, [rv[g] for g in grad_rows], hv, pv)
        d_rows, d_halos, d_params = vjp(tuple(dv))

        @pl.when(i == 0)
        def _():
            for c_ref in carry_refs:
                c_ref[...] = jnp.zeros_like(c_ref)
            for p_ref in out_refs[len(grad_rows):]:
                p_ref[...] = jnp.zeros_like(p_ref)

        for q, g in enumerate(grad_rows):
            val = d_rows[q]
            for e in ev[q]:
                val = val + e
            out_refs[q][...] = val.astype(out_refs[q].dtype)
            if g in halos:
                hq = list(halos).index(g)
                out_refs[q][tile - 8:tile, :] += carry_refs[hq][...]
                carry_refs[hq][...] = d_halos[hq]
        for p_ref, dp in zip(out_refs[len(grad_rows):], d_params):
            p_ref[...] += dp

    res = pl.pallas_call(
        body, name=name, grid=(n,), in_specs=in_specs, out_specs=out_specs, out_shape=out_shapes,
        scratch_shapes=scratch, compiler_params=_cparams(("arbitrary",)),
    )(*rows, *[rows[h] for h in halos], *params, *consts, *dflat, *eflat)
    return list(res[:len(grad_rows)]), list(res[len(grad_rows):])


def _fn_norm(rows, halos, params, consts):
    return (_rms(rows[0], params[0]),)


def _fn_mla_prep(rows, halos, params, consts):
    cq, ckv, kpe = _lane_split(rows[0], (QL, KVL, MLA_PAD - QL - KVL))
    return _rms(cq, params[0]), _rms(ckv, params[1]), kpe


def _rope(x, cos, sin, rot):
    return x * cos + _exact_r(x, rot, False) * sin


def _fn_qk_post(rows, halos, params, consts):
    q_raw, kn_pad, kpe, cos, sin = rows
    q_norm, k_norm = params
    place, rot = consts
    kpe96 = _exact_r(kpe, place, False)
    qs = [_rope(_rms(qh, q_norm), cos, sin, rot) for qh in _unstack(q_raw)]
    ks = [_rope(_rms(kh + kpe96, k_norm), cos, sin, rot) for kh in _unstack(kn_pad)]
    return jnp.stack(qs, axis=0), jnp.stack(ks, axis=0)


def _seg(x, bd):
    return _exact_r(x, bd, False)


def _make_fn_rwkv_prep(vres):
    def fn(rows, halos, params, consts):
        cols = rows[0]
        bd = consts[0]
        mu, w0, w2, a0, a2, g2, k_k, k_a = params[:8]
        prev = _shift(cols, halos[0], 1)
        c = cols + (prev - cols) * mu
        r, k, v, xw, xa, xg = _lane_split(c, (RW, RW, RW, DL, AL, GL))
        log_w = -_softplus(-(w0 + _mm(jnp.tanh(xw), w2))) - 0.5
        ld = -jnp.exp(log_w)
        a = _sigmoid(a0 + _mm(xa, a2))
        g = _mm(_sigmoid(xg), g2)
        if vres:
            hcur, v_first = rows[1], rows[2]
            v1, v_mu, v0, v2 = params[8:12]
            xv = _mm(hcur, v1)
            xv_prev = _shift(xv, _mm(halos[1], v1), 1)
            xv = xv + (xv_prev - xv) * v_mu
            v = v + (v_first - v) * _sigmoid(v0 + _mm(xv, v2))
        kk = k * k_k
        kk = kk / jnp.maximum(jnp.sqrt(_seg(kk * kk, bd)), 1e-12)
        k2 = k * (1.0 + (a - 1.0) * k_a)
        return r, ld, k2, v, -kk, kk * a, g
    return fn


def _fn_rwkv_post(rows, halos, params, consts):
    y, r, k2, v, g = rows
    ln_w, ln_b, r_k = params
    bd = consts[0]
    mean = _seg(y, bd) * (1.0 / RN)
    d = y - mean
    var = _seg(d * d, bd) * (1.0 / RN)
    yn = d * lax.rsqrt(var + GN_EPS) * ln_w + ln_b
    bonus = _seg(r * k2 * r_k, bd) * v
    return ((yn + bonus) * g,)


def _fn_conv(rows, halos, params, consts):
    cols, halo = rows[0], halos[0]
    w0, w1, w2 = params
    b, c, x = _lane_split(cols, (CW, CW, CW))
    _, ch, xh = _lane_split(halo, (CW, CW, CW))
    u, uh = c * x, ch * xh
    return (b * (w0 * _shift(u, uh, 2) + w1 * _shift(u, uh, 1) + w2 * u),)


def _fn_merge(rows, halos, params, consts):
    gate, o_a, o_b, o_c = rows
    g_a, g_b, g_c = _lane_split(gate, (D, D, D))
    return (_sigmoid(g_a) * o_a + _sigmoid(g_b) * o_b + _sigmoid(g_c) * o_c,)


def _attn_block(q, k, v, q0):
    tq, s_len = q.shape[0], k.shape[0]
    s = _mm_nt(q, k) * (DQK ** -0.5)
    row = q0 + lax.broadcasted_iota(jnp.int32, (tq, s_len), 0)
    col = lax.broadcasted_iota(jnp.int32, (tq, s_len), 1)
    s = jnp.where(row >= col, s, -1e30)
    m = lax.stop_gradient(jnp.max(s, axis=-1, keepdims=True))
    e = jnp.exp(s - m)
    p = e / jnp.sum(e, axis=-1, keepdims=True)
    return _mm(p, v)


def _attn_segments(s_len, tq):
    per = max(1, s_len // tq // ATTN_SEGMENTS)
    return [(first, per, (first + per) * tq) for first in range(0, s_len // tq, per)]


def attn_fwd(q, k, v, *, tq, name):
    h, s_len, _ = q.shape
    outs = []
    for seg, (first, nq, kend) in enumerate(_attn_segments(s_len, tq)):
        def body(q_ref, k_ref, v_ref, o_ref, first=first):
            q0 = (first + pl.program_id(1)) * tq
            o_ref[0] = _attn_block(q_ref[0], k_ref[0], v_ref[0], q0).astype(o_ref.dtype)

        outs.append(pl.pallas_call(
            body, name=f"{name}_{seg}", grid=(h, nq),
            in_specs=[pl.BlockSpec((1, tq, DQK), lambda hh, i, first=first: (hh, first + i, 0)),
                      pl.BlockSpec((1, kend, DQK), lambda hh, i: (hh, 0, 0)),
                      pl.BlockSpec((1, kend, DV), lambda hh, i: (hh, 0, 0))],
            out_specs=pl.BlockSpec((1, tq, DV), lambda hh, i: (hh, i, 0)),
            out_shape=jax.ShapeDtypeStruct((h, nq * tq, DV), bf16),
            compiler_params=_cparams(("parallel", "arbitrary")),
        )(q, k, v))
    return jnp.concatenate(outs, axis=1)


def attn_bwd(q, k, v, do, *, tq, name):
    h, s_len, _ = q.shape
    dqs, dk_acc, dv_acc = [], None, None
    for seg, (first, nq, kend) in reversed(list(enumerate(_attn_segments(s_len, tq)))):
        carried = dk_acc is not None

        def body(*refs, first=first, carried=carried):
            q_ref, k_ref, v_ref, do_ref = refs[:4]
            dq_ref, dk_ref, dv_ref = refs[-3:]
            i = pl.program_id(1)
            _, vjp = jax.vjp(functools.partial(_attn_block, q0=(first + i) * tq), q_ref[0], k_ref[0], v_ref[0])
            dq, dk, dv = vjp(do_ref[0])
            dq_ref[0] = dq

            @pl.when(i == 0)
            def _():
                dk_ref[0] = dk + refs[4][0] if carried else dk
                dv_ref[0] = dv + refs[5][0] if carried else dv

            @pl.when(i > 0)
            def _():
                dk_ref[0] += dk
                dv_ref[0] += dv

        key_specs = [pl.BlockSpec((1, kend, DQK), lambda hh, i: (hh, 0, 0)),
                     pl.BlockSpec((1, kend, DV), lambda hh, i: (hh, 0, 0))]
        dq, dk_acc, dv_acc = pl.pallas_call(
            body, name=f"{name}_{seg}", grid=(h, nq),
            in_specs=[pl.BlockSpec((1, tq, DQK), lambda hh, i, first=first: (hh, first + i, 0))] + key_specs
            + [pl.BlockSpec((1, tq, DV), lambda hh, i, first=first: (hh, first + i, 0))]
            + (key_specs if carried else []),
            out_specs=[pl.BlockSpec((1, tq, DQK), lambda hh, i: (hh, i, 0))] + key_specs,
            out_shape=[jax.ShapeDtypeStruct((h, nq * tq, DQK), f32), jax.ShapeDtypeStruct((h, s_len, DQK), f32),
                       jax.ShapeDtypeStruct((h, s_len, DV), f32)],
            input_output_aliases={4: 1, 5: 2} if carried else {},
            compiler_params=_cparams(("parallel", "arbitrary")),
        )(q, k, v, do, *([dk_acc, dv_acc] if carried else []))
        dqs.append(dq)
    return jnp.concatenate(dqs[::-1], axis=1), dk_acc, dv_acc


def _wkv_local(r, ld, k, v, a, b):
    nb, c, n = r.shape
    ri = lax.broadcasted_iota(jnp.int32, (c, c), 0)
    ci = lax.broadcasted_iota(jnp.int32, (c, c), 1)
    tri = jnp.broadcast_to((ri >= ci).astype(f32)[None], (nb, c, c))
    cum = _exact_bl(tri, ld, False)
    tot = jnp.sum(ld, axis=1, keepdims=True)
    w_incl, w_excl, w_inv, w_rest = jnp.exp(cum), jnp.exp(cum - ld), jnp.exp(-cum), jnp.exp(tot - cum)
    ab, rb, bb, kb = a * w_excl, r * w_incl, b * w_inv, k * w_inv
    bw, kw = b * w_rest, k * w_rest
    r2 = lax.broadcasted_iota(jnp.int32, (2 * c, 2 * c), 0)
    c2 = lax.broadcasted_iota(jnp.int32, (2 * c, 2 * c), 1)
    t_of, s_of = jnp.where(r2 >= c, r2 - c, r2), jnp.where(c2 >= c, c2 - c, c2)
    keep = jnp.logical_or(t_of > s_of, jnp.logical_and(r2 >= c, t_of == s_of))
    pair = jnp.where(keep[None], _hbnt(jnp.concatenate([ab, rb], axis=1), jnp.concatenate([bb, kb], axis=1)), 0.0)
    on_b, on_k = _lane_split(pair, (c, c))
    l_ab, m_rb = _row_split(on_b, (c, c))
    l_ak_v, m_rk_v = _row_split(_hbnn(on_k, v), (c, c))
    x = jnp.concatenate([ab, l_ak_v], axis=-1)
    lp, span = l_ab, 1
    while span < c:
        x = x + _hbnn(lp, x)
        span *= 2
        if span < c:
            lp = _hbnn(lp, lp)
    via_b_r, via_b_y = _lane_split(_hbnn(m_rb, x), (n, n))
    r_hat = rb + via_b_r
    y0 = via_b_y + m_rk_v
    from_b_g, from_b_z = _row_split(_hbtn(x, bw), (n, n))
    eye = lax.broadcasted_iota(jnp.int32, (n, n), 0) == lax.broadcasted_iota(jnp.int32, (n, n), 1)
    g = jnp.where(eye[None], jnp.exp(tot), 0.0) + from_b_g
    z = from_b_z + _hbtn(v, kw)
    return r_hat, y0, g, z


def _head(h):
    return slice(RN * h, RN * (h + 1))


def _load_chunk_heads(ref, c, per):
    return jnp.stack([ref[c * q:c * (q + 1), _head(h)] for q in range(per) for h in range(RH)], axis=0)


def _store_chunk_heads(ref, val, c, per):
    for q in range(per):
        ref[c * q:c * (q + 1), :] = jnp.concatenate([val[q * RH + h] for h in range(RH)], axis=-1)


def wkv_fwd(r, ld, k, v, a, b, *, name):
    s_len = r.shape[0]
    c, per = WKV_CHUNK, WKV_CHUNKS_PER_STEP
    n = s_len // c
    rows = pl.BlockSpec((c * per, RW), lambda i: (i, 0))
    mats = pl.BlockSpec((per, RH, RN, RN), lambda i: (i, 0, 0, 0))
    rows_t, mats_t = jax.ShapeDtypeStruct((s_len, RW), f32), jax.ShapeDtypeStruct((n, RH, RN, RN), f32)

    def local_body(r_ref, ld_ref, k_ref, v_ref, a_ref, b_ref, rh_ref, y0_ref, g_ref, z_ref):
        r_hat, y0, g, z = _wkv_local(*[_load_chunk_heads(ref, c, per)
                                       for ref in (r_ref, ld_ref, k_ref, v_ref, a_ref, b_ref)])
        _store_chunk_heads(rh_ref, r_hat, c, per)
        _store_chunk_heads(y0_ref, y0, c, per)
        g_ref[...] = g.reshape(per, RH, RN, RN)
        z_ref[...] = z.reshape(per, RH, RN, RN)

    r_hat, y0, g, z = pl.pallas_call(
        local_body, name=name + "_local", grid=(n // per,), in_specs=[rows] * 6, out_specs=[rows, rows, mats, mats],
        out_shape=[rows_t, rows_t, mats_t, mats_t], compiler_params=_cparams(("parallel",)),
    )(r, ld, k, v, a, b)

    def scan_body(g_ref, z_ref, st_ref, s_sc):
        s_sc[...] = jnp.zeros_like(s_sc)

        @pl.loop(0, n)
        def _(i):
            s0 = s_sc[...]
            st_ref[i] = s0
            s_sc[...] = _hbnn(s0, g_ref[i]) + z_ref[i]

    vm = pl.BlockSpec(memory_space=pltpu.VMEM)
    states = pl.pallas_call(
        scan_body, name=name + "_scan", in_specs=[vm, vm], out_specs=vm, out_shape=mats_t,
        scratch_shapes=[pltpu.VMEM((RH, RN, RN), f32)],
        compiler_params=pltpu.CompilerParams(vmem_limit_bytes=VMEM_LIMIT),
    )(g, z)

    def out_body(rh_ref, y0_ref, st_ref, y_ref):
        y = _hbnt(_load_chunk_heads(rh_ref, c, per), st_ref[...].reshape(per * RH, RN, RN))
        _store_chunk_heads(y_ref, y, c, per)
        y_ref[...] += y0_ref[...]

    y = pl.pallas_call(
        out_body, name=name + "_out", grid=(n // per,), in_specs=[rows, rows, mats], out_specs=rows,
        out_shape=rows_t, compiler_params=_cparams(("parallel",)),
    )(r_hat, y0, states)
    return y, dict(r_hat=r_hat, g=g, states=states)


def wkv_bwd(r, ld, k, v, a, b, saved, dy, *, name):
    s_len = r.shape[0]
    c, per = WKV_CHUNK, WKV_CHUNKS_PER_STEP
    n = s_len // c
    rows = pl.BlockSpec((c * per, RW), lambda i: (i, 0))
    mats = pl.BlockSpec((per, RH, RN, RN), lambda i: (i, 0, 0, 0))
    rows_t, mats_t = jax.ShapeDtypeStruct((s_len, RW), f32), jax.ShapeDtypeStruct((n, RH, RN, RN), f32)

    def out_body(dy_ref, rh_ref, st_ref, drh_ref, dsy_ref):
        dyb = _load_chunk_heads(dy_ref, c, per)
        _store_chunk_heads(drh_ref, _hbnn(dyb, st_ref[...].reshape(per * RH, RN, RN)), c, per)
        dsy_ref[...] = _hbtn(dyb, _load_chunk_heads(rh_ref, c, per)).reshape(per, RH, RN, RN)

    d_rhat, ds_y = pl.pallas_call(
        out_body, name=name + "_out", grid=(n // per,), in_specs=[rows, rows, mats], out_specs=[rows, mats],
        out_shape=[rows_t, mats_t], compiler_params=_cparams(("parallel",)),
    )(dy, saved["r_hat"], saved["states"])

    def scan_body(dsy_ref, g_ref, st_ref, dg_ref, dz_ref, ds_sc):
        ds_sc[...] = jnp.zeros_like(ds_sc)

        @pl.loop(0, n)
        def _(i):
            cidx = n - 1 - i
            ds_next = ds_sc[...]
            dz_ref[cidx] = ds_next
            dg_ref[cidx] = _hbtn(st_ref[cidx], ds_next)
            ds_sc[...] = dsy_ref[cidx] + _hbnt(ds_next, g_ref[cidx])

    vm = pl.BlockSpec(memory_space=pltpu.VMEM)
    d_g, d_z = pl.pallas_call(
        scan_body, name=name + "_scan", in_specs=[vm, vm, vm], out_specs=[vm, vm], out_shape=[mats_t, mats_t],
        scratch_shapes=[pltpu.VMEM((RH, RN, RN), f32)],
        compiler_params=pltpu.CompilerParams(vmem_limit_bytes=VMEM_LIMIT),
    )(ds_y, saved["g"], saved["states"])

    def local_body(r_ref, ld_ref, k_ref, v_ref, a_ref, b_ref, drh_ref, dy_ref, dg_ref, dz_ref, *out_refs):
        _, vjp = jax.vjp(_wkv_local, *[_load_chunk_heads(ref, c, per)
                                       for ref in (r_ref, ld_ref, k_ref, v_ref, a_ref, b_ref)])
        grads = vjp((_load_chunk_heads(drh_ref, c, per), _load_chunk_heads(dy_ref, c, per),
                     dg_ref[...].reshape(per * RH, RN, RN), dz_ref[...].reshape(per * RH, RN, RN)))
        for o_ref, val in zip(out_refs, grads):
            _store_chunk_heads(o_ref, val, c, per)

    return pl.pallas_call(
        local_body, name=name + "_local", grid=(n // per,), in_specs=[rows] * 8 + [mats, mats], out_specs=[rows] * 6,
        out_shape=[rows_t] * 6, compiler_params=_cparams(("parallel",)),
    )(r, ld, k, v, a, b, d_rhat, dy, d_g, d_z)


def loss_head(y, target, *, tile, name):
    s_len, d = y.shape
    n = s_len // tile

    def body(y_ref, t_ref, dy_ref, l_ref):
        err = y_ref[...] - t_ref[...]
        dy_ref[...] = err * (1.0 / d)
        part = 0.5 * jnp.sum(jnp.mean(err * err, axis=-1, keepdims=True), axis=0, keepdims=True)

        @pl.when(pl.program_id(0) == 0)
        def _():
            l_ref[...] = jnp.zeros_like(l_ref)

        l_ref[...] += jnp.broadcast_to(part, l_ref.shape)

    bs = pl.BlockSpec((tile, d), lambda i: (i, 0))
    dy, l = pl.pallas_call(
        body, name=name, grid=(n,), in_specs=[bs, bs],
        out_specs=[bs, pl.BlockSpec((8, 128), lambda i: (0, 0))],
        out_shape=[jax.ShapeDtypeStruct((s_len, d), f32), jax.ShapeDtypeStruct((8, 128), f32)],
        compiler_params=_cparams(("arbitrary",)),
    )(y, target)
    return l[0, 0], dy


def adamw(w, g, m, v, *, name):
    gs = g if isinstance(g, (list, tuple)) else [g]
    rows, cols = w.shape
    tile = _pick(rows, max(8, (2 * 1024 * 1024 // (4 * cols)) // 8 * 8), 8)
    c1 = 1.0 - B1 ** STEP
    c2 = 1.0 - B2 ** STEP
    ng = len(gs)

    def body(*refs):
        w_ref, m_ref, v_ref = refs[0], refs[1 + ng], refs[2 + ng]
        g_out, d_out, m_out, v_out = refs[3 + ng:]
        gv = refs[1][...]
        for q in range(1, ng):
            gv = gv + refs[1 + q][...]
        mn = B1 * m_ref[...] + (1.0 - B1) * gv
        vn = B2 * v_ref[...] + (1.0 - B2) * (gv * gv)
        d_out[...] = -LR * ((mn / c1) / (jnp.sqrt(vn / c2) + EPS) + WD * w_ref[...])
        g_out[...] = gv
        m_out[...] = mn
        v_out[...] = vn

    bs = pl.BlockSpec((tile, cols), lambda i: (i, 0))
    return pl.pallas_call(
        body, name=name, grid=(rows // tile,), in_specs=[bs] * (3 + ng), out_specs=[bs] * 4,
        out_shape=[jax.ShapeDtypeStruct((rows, cols), f32)] * 4, compiler_params=_cparams(("parallel",)),
    )(w, *gs, m, v)


def _place():
    return lax.axis_index("x"), lax.axis_index("y"), lax.axis_index("c")


_ANY = pl.BlockSpec(memory_space=pl.ANY)


def _peer_chips(x, y):
    return [(1 - x, y), (x, 1 - y), (1 - x, 1 - y)]


def gather_weights(shards, *, name):
    nk = len(shards)

    def body(*refs):
        srcs, outs = refs[:nk], refs[nk:2 * nk]
        ici_send, ici_recv, d2d_send, d2d_recv = refs[2 * nk + 1:]
        x, y, c = _place()
        me = 2 * x + y
        peers = _peer_chips(x, y)
        pending = []
        for k in range(nk):
            half = srcs[k].shape[0] // 2
            mine = pl.ds(c * half, half)
            for p, (px, py) in enumerate(peers):
                cp = pltpu.make_async_remote_copy(
                    src_ref=srcs[k].at[mine], dst_ref=outs[k].at[me, mine], send_sem=ici_send.at[k, p],
                    recv_sem=ici_recv.at[k, p], device_id=(px, py, c), device_id_type=MESH)
                cp.start()
                pending.append(cp)
        for k in range(nk):
            half = srcs[k].shape[0] // 2
            mine = pl.ds(c * half, half)
            for p, (px, py) in enumerate(peers):
                landed = outs[k].at[2 * px + py, mine]
                pltpu.make_async_remote_copy(
                    src_ref=srcs[k].at[mine], dst_ref=landed, send_sem=ici_send.at[k, p], recv_sem=ici_recv.at[k, p],
                    device_id=(px, py, c), device_id_type=MESH).wait_recv()
                fwd = pltpu.make_async_remote_copy(
                    src_ref=landed, dst_ref=landed, send_sem=d2d_send.at[k, p], recv_sem=d2d_recv.at[k, p],
                    device_id=(x, y, 1 - c), device_id_type=MESH)
                fwd.start()
                pending.append(fwd)
        for k in range(nk):
            half = srcs[k].shape[0] // 2
            other = pl.ds((1 - c) * half, half)
            for p, (px, py) in enumerate(peers):
                theirs = outs[k].at[2 * px + py, other]
                pltpu.make_async_remote_copy(
                    src_ref=theirs, dst_ref=theirs, send_sem=d2d_send.at[k, p], recv_sem=d2d_recv.at[k, p],
                    device_id=(x, y, 1 - c), device_id_type=MESH).wait_recv()
        for cp in pending:
            cp.wait_send()
        refs[2 * nk][...] = jnp.zeros_like(refs[2 * nk])

    sem = lambda *shape: pltpu.SemaphoreType.DMA(shape)
    res = pl.pallas_call(
        body, name=name, in_specs=[_ANY] * nk, out_specs=[_ANY] * nk + [pl.BlockSpec(memory_space=pltpu.VMEM)],
        out_shape=[jax.ShapeDtypeStruct((4,) + s.shape, s.dtype) for s in shards]
        + [jax.ShapeDtypeStruct((8, 128), f32)],
        scratch_shapes=[sem(nk, 3), sem(nk, 3), sem(nk, 3), sem(nk, 3)],
    )(*shards)
    return list(res[:nk]), res[nk]


_HBM = pl.BlockSpec(memory_space=pltpu.HBM)
_SEM = pl.BlockSpec(memory_space=pltpu.SEMAPHORE)


def _ici_half_copies(srcs, lands, send_sems, recv_sems, incoming):
    x, y, c = _place()
    me = 2 * x + y
    out = []
    for k in range(len(srcs)):
        half = srcs[k].shape[0] // 2
        mine = pl.ds(c * half, half)
        for p, (px, py) in enumerate(_peer_chips(x, y)):
            out.append(pltpu.make_async_remote_copy(
                src_ref=srcs[k].at[mine], dst_ref=lands[k].at[(2 * px + py) if incoming else me, mine],
                send_sem=send_sems.at[3 * k + p], recv_sem=recv_sems.at[3 * k + p], device_id=(px, py, c),
                device_id_type=MESH))
    return out


def gather_start(shards, *, name):
    nk = len(shards)

    def body(*refs):
        srcs, lands = refs[:nk], refs[nk:2 * nk]
        send_sems, recv_sems = refs[2 * nk], refs[2 * nk + 1]
        token = refs[-1]
        for outgoing in _ici_half_copies(srcs, lands, send_sems, recv_sems, incoming=False):
            outgoing.start()
        token[...] = jnp.zeros_like(token)

    lands = [pltpu.with_memory_space_constraint(lax.empty((4,) + s.shape, s.dtype), pltpu.HBM) for s in shards]
    res = pl.pallas_call(
        body, name=name,
        out_shape=(pltpu.SemaphoreType.DMA((3 * nk,)), pltpu.SemaphoreType.DMA((3 * nk,)),
                   *[pltpu.HBM(s.shape, s.dtype) for s in shards], *[pltpu.HBM(z.shape, z.dtype) for z in lands],
                   jax.ShapeDtypeStruct((8, 128), f32)),
        in_specs=[_HBM] * (2 * nk), out_specs=(_SEM, _SEM, *[_HBM] * (2 * nk), pl.BlockSpec(memory_space=pltpu.VMEM)),
        input_output_aliases={k: 2 + k for k in range(2 * nk)},
        compiler_params=pltpu.CompilerParams(has_side_effects=pltpu.SideEffectType.DATAFLOW_SIDE_EFFECTING),
    )(*[pltpu.with_memory_space_constraint(s, pltpu.HBM) for s in shards], *lands)
    return res[0], res[1], list(res[2:2 + nk]), list(res[2 + nk:2 + 2 * nk]), res[-1]


def gather_wait(send_sems, recv_sems, shards, lands, after, *, name):
    nk = len(shards)

    def body(*refs):
        srcs, zones = refs[:nk], refs[nk:2 * nk]
        for outgoing in _ici_half_copies(srcs, zones, refs[2 * nk], refs[2 * nk + 1], incoming=False):
            outgoing.wait_send()
        for landed in _ici_half_copies(srcs, zones, refs[2 * nk], refs[2 * nk + 1], incoming=True):
            landed.wait_recv()

    res = pl.pallas_call(
        body, name=name,
        out_shape=(*[pltpu.HBM(s.shape, s.dtype) for s in shards], *[pltpu.HBM(z.shape, z.dtype) for z in lands]),
        in_specs=[_HBM] * (2 * nk) + [_SEM, _SEM, _ANY], out_specs=tuple([_HBM] * (2 * nk)),
        input_output_aliases={k: k for k in range(2 * nk)},
        compiler_params=pltpu.CompilerParams(has_side_effects=pltpu.SideEffectType.DATAFLOW_SIDE_EFFECTING),
    )(*shards, *lands, send_sems, recv_sems, after)
    return list(res[:nk]), list(res[nk:])


def gather_forward(lands, *, name):
    nk = len(lands)

    def body(*refs):
        zones = refs[nk:2 * nk]
        send_sems, recv_sems = refs[2 * nk:]
        x, y, c = _place()
        sends = []
        for k in range(nk):
            half = zones[k].shape[1] // 2
            for p, (px, py) in enumerate(_peer_chips(x, y)):
                landed = zones[k].at[2 * px + py, pl.ds(c * half, half)]
                cp = pltpu.make_async_remote_copy(
                    src_ref=landed, dst_ref=landed, send_sem=send_sems.at[k, p], recv_sem=recv_sems.at[k, p],
                    device_id=(x, y, 1 - c), device_id_type=MESH)
                cp.start()
                sends.append(cp)
        for k in range(nk):
            half = zones[k].shape[1] // 2
            for p, (px, py) in enumerate(_peer_chips(x, y)):
                theirs = zones[k].at[2 * px + py, pl.ds((1 - c) * half, half)]
                pltpu.make_async_remote_copy(
                    src_ref=theirs, dst_ref=theirs, send_sem=send_sems.at[k, p], recv_sem=recv_sems.at[k, p],
                    device_id=(x, y, 1 - c), device_id_type=MESH).wait_recv()
        for cp in sends:
            cp.wait_send()

    return pl.pallas_call(
        body, name=name, in_specs=[_ANY] * nk, out_specs=[_ANY] * nk,
        out_shape=[jax.ShapeDtypeStruct(z.shape, z.dtype) for z in lands],
        input_output_aliases={k: k for k in range(nk)},
        scratch_shapes=[pltpu.SemaphoreType.DMA((nk, 3)), pltpu.SemaphoreType.DMA((nk, 3))],
    )(*lands)


def grads_to_sibling(parts, *, name):
    nk = len(parts)

    def body(*refs):
        srcs, outs = refs[:nk], refs[nk:2 * nk]
        send_sems, recv_sems = refs[2 * nk:]
        x, y, c = _place()
        sends = []
        for k in range(nk):
            half = srcs[k].shape[1] // 2
            cp = pltpu.make_async_remote_copy(
                src_ref=srcs[k].at[:, pl.ds((1 - c) * half, half), :], dst_ref=outs[k], send_sem=send_sems.at[k],
                recv_sem=recv_sems.at[k], device_id=(x, y, 1 - c), device_id_type=MESH)
            cp.start()
            sends.append(cp)
        for cp in sends:
            cp.wait_recv()
        for cp in sends:
            cp.wait_send()

    return pl.pallas_call(
        body, name=name, in_specs=[_ANY] * nk, out_specs=[_ANY] * nk,
        out_shape=[jax.ShapeDtypeStruct((4, p.shape[1] // 2, p.shape[2]), p.dtype) for p in parts],
        scratch_shapes=[pltpu.SemaphoreType.DMA((nk,)), pltpu.SemaphoreType.DMA((nk,))],
    )(*parts)


def pair_sum(part, theirs, core, *, name):
    _, rows, cols = part.shape
    half = rows // 2
    tile = _pick(half, max(16, (1 << 20) // (4 * cols) // 16 * 16), 16)
    per = half // tile

    def body(c_ref, p_ref, t_ref, o_ref):
        o_ref[...] = (p_ref[...].astype(f32) + t_ref[...].astype(f32)).astype(bf16)

    grid_spec = pltpu.PrefetchScalarGridSpec(
        num_scalar_prefetch=1, grid=(4, per),
        in_specs=[pl.BlockSpec((1, tile, cols), lambda j, i, c_ref: (j, c_ref[0] * per + i, 0)),
                  pl.BlockSpec((1, tile, cols), lambda j, i, c_ref: (j, i, 0))],
        out_specs=pl.BlockSpec((1, tile, cols), lambda j, i, c_ref: (j, i, 0)))
    return pl.pallas_call(
        body, name=name, grid_spec=grid_spec, out_shape=jax.ShapeDtypeStruct((4, half, cols), bf16),
        compiler_params=_cparams(("parallel", "parallel")),
    )(core, part, theirs)


def chips_all_to_all(parts, *, name):
    nk = len(parts)

    def body(*refs):
        srcs, outs = refs[:nk], refs[nk:2 * nk]
        send_sems, recv_sems, local_sems = refs[2 * nk:]
        x, y, c = _place()
        me = 2 * x + y
        peers = _peer_chips(x, y)
        sends, locals_ = [], []
        for k in range(nk):
            local = pltpu.make_async_copy(srcs[k].at[me], outs[k].at[me], local_sems.at[k])
            local.start()
            locals_.append(local)
            for p, (px, py) in enumerate(peers):
                cp = pltpu.make_async_remote_copy(
                    src_ref=srcs[k].at[2 * px + py], dst_ref=outs[k].at[me], send_sem=send_sems.at[k, p],
                    recv_sem=recv_sems.at[k, p], device_id=(px, py, c), device_id_type=MESH)
                cp.start()
                sends.append(cp)
        for k in range(nk):
            for p, (px, py) in enumerate(peers):
                pltpu.make_async_remote_copy(
                    src_ref=srcs[k].at[me], dst_ref=outs[k].at[2 * px + py], send_sem=send_sems.at[k, p],
                    recv_sem=recv_sems.at[k, p], device_id=(px, py, c), device_id_type=MESH).wait_recv()
        for cp in sends:
            cp.wait_send()
        for cp in locals_:
            cp.wait()

    sem = lambda *shape: pltpu.SemaphoreType.DMA(shape)
    return pl.pallas_call(
        body, name=name, in_specs=[_ANY] * nk, out_specs=[_ANY] * nk,
        out_shape=[jax.ShapeDtypeStruct(p.shape, p.dtype) for p in parts],
        scratch_shapes=[sem(nk, 3), sem(nk, 3), sem(nk)],
    )(*parts)


def join_halves(bufs, layout, *, name):
    nk, nb = len(layout), len(bufs)

    def body(*refs):
        outs = refs[nb:2 * nb]
        send_sems, recv_sems = refs[2 * nb:]
        x, y, c = _place()
        pending = []
        for k, (o, off, rows) in enumerate(layout):
            half = rows // 2
            mine = outs[o].at[pl.ds(off + c * half, half), :]
            cp = pltpu.make_async_remote_copy(
                src_ref=mine, dst_ref=mine, send_sem=send_sems.at[k], recv_sem=recv_sems.at[k],
                device_id=(x, y, 1 - c), device_id_type=MESH)
            cp.start()
            pending.append(cp)
        for k, (o, off, rows) in enumerate(layout):
            half = rows // 2
            theirs = outs[o].at[pl.ds(off + (1 - c) * half, half), :]
            pltpu.make_async_remote_copy(
                src_ref=theirs, dst_ref=theirs, send_sem=send_sems.at[k], recv_sem=recv_sems.at[k],
                device_id=(x, y, 1 - c), device_id_type=MESH).wait_recv()
        for cp in pending:
            cp.wait_send()

    return pl.pallas_call(
        body, name=name, in_specs=[_ANY] * nb, out_specs=[_ANY] * nb,
        out_shape=[jax.ShapeDtypeStruct(b.shape, b.dtype) for b in bufs],
        input_output_aliases={o: o for o in range(nb)},
        scratch_shapes=[pltpu.SemaphoreType.DMA((nk,)), pltpu.SemaphoreType.DMA((nk,))],
    )(*bufs)


def place_slab(dest, src, index, *, name):
    rows, cols = src.shape
    tile = _pick(rows, max(16, (1 << 20) // (src.dtype.itemsize * cols) // 16 * 16), 16)

    def body(i_ref, s_ref, d_ref, o_ref):
        del i_ref, d_ref
        o_ref[0] = s_ref[...]

    grid_spec = pltpu.PrefetchScalarGridSpec(
        num_scalar_prefetch=1, grid=(rows // tile,),
        in_specs=[pl.BlockSpec((tile, cols), lambda i, idx: (i, 0)), _ANY],
        out_specs=pl.BlockSpec((1, tile, cols), lambda i, idx: (idx[0], i, 0)))
    return pl.pallas_call(
        body, name=name, grid_spec=grid_spec, out_shape=jax.ShapeDtypeStruct(dest.shape, dest.dtype),
        input_output_aliases={2: 0}, compiler_params=_cparams(("parallel",)),
    )(index, src, dest)


def all_reduce_small(src, *, name):
    rows, cols = src.shape

    def body(s_ref, o_ref, buf, send_sems, recv_sems):
        x, y, c = _place()
        me = 4 * x + 2 * y + c
        buf[me] = s_ref[...]
        sends = []
        for msk in range(1, 8):
            px = x ^ (msk >> 2)
            py = y ^ ((msk >> 1) & 1)
            pc = c ^ (msk & 1)
            cp = pltpu.make_async_remote_copy(
                src_ref=s_ref, dst_ref=buf.at[me], send_sem=send_sems.at[msk - 1], recv_sem=recv_sems.at[msk - 1],
                device_id=(px, py, pc), device_id_type=MESH)
            cp.start()
            sends.append(cp)
        for msk in range(1, 8):
            px = x ^ (msk >> 2)
            py = y ^ ((msk >> 1) & 1)
            pc = c ^ (msk & 1)
            pltpu.make_async_remote_copy(
                src_ref=s_ref, dst_ref=buf.at[4 * px + 2 * py + pc], send_sem=send_sems.at[msk - 1],
                recv_sem=recv_sems.at[msk - 1], device_id=(px, py, pc), device_id_type=MESH).wait_recv()
        for cp in sends:
            cp.wait_send()
        acc = buf[0]
        for d in range(1, 8):
            acc = acc + buf[d]
        o_ref[...] = acc

    vm = pl.BlockSpec(memory_space=pltpu.VMEM)
    return pl.pallas_call(
        body, name=name, in_specs=[vm], out_specs=vm, out_shape=jax.ShapeDtypeStruct((rows, cols), f32),
        scratch_shapes=[pltpu.VMEM((8, rows, cols), f32), pltpu.SemaphoreType.DMA((7,)),
                        pltpu.SemaphoreType.DMA((7,))],
        compiler_params=pltpu.CompilerParams(vmem_limit_bytes=VMEM_LIMIT),
    )(src)


def sum4_into(parts, dest, core, *, layer, total_rows, name):
    _, rows, cols = parts.shape
    tile = _pick(rows, max(16, (1 << 20) // (4 * cols) // 16 * 16), 16)
    per = rows // tile

    def body(c_ref, p_ref, *rest):
        del c_ref
        p = [p_ref[j].astype(f32) for j in range(4)]
        rest[-1][...] = ((p[0] + p[1]) + p[2]) + p[3]

    grid_spec = pltpu.PrefetchScalarGridSpec(
        num_scalar_prefetch=1, grid=(per,),
        in_specs=[pl.BlockSpec((4, tile, cols), lambda i, c_ref: (0, i, 0))] + ([] if dest is None else [_ANY]),
        out_specs=pl.BlockSpec((tile, cols), lambda i, c_ref: ((2 * layer + c_ref[0]) * per + i, 0)))
    return pl.pallas_call(
        body, name=name, grid_spec=grid_spec, out_shape=jax.ShapeDtypeStruct((total_rows, cols), f32),
        input_output_aliases={} if dest is None else {2: 0}, compiler_params=_cparams(("parallel",)),
    )(core, parts, *([] if dest is None else [dest]))


def _consts():
    idx = np.arange(RW)
    bd = (idx[:, None] // RN == idx[None, :] // RN).astype(np.float32)
    place = np.zeros((128, DQK), np.float32)
    place[np.arange(ROPE), NOPE + np.arange(ROPE)] = 1.0
    rot = np.zeros((DQK, DQK), np.float32)
    half = ROPE // 2
    rot[NOPE + half + np.arange(half), NOPE + np.arange(half)] = -1.0
    rot[NOPE + np.arange(half), NOPE + half + np.arange(half)] = 1.0
    return jnp.asarray(bd), jnp.asarray(place), jnp.asarray(rot)


def _rope_tables(positions):
    freqs = ROPE_THETA ** (-(jnp.arange(ROPE // 2, dtype=f32) * 2.0 / ROPE))
    ang = positions.astype(f32)[:, None] * freqs
    cos, sin = jnp.cos(ang), jnp.sin(ang)
    ones = jnp.ones((positions.shape[0], NOPE), f32)
    return (jnp.concatenate([ones, cos, cos], axis=-1), jnp.concatenate([0.0 * ones, sin, sin], axis=-1))


def derive_layer(w):
    w_in = w["w_in"]
    pad = jnp.zeros((D, MLA_PAD - MLA_COLS), w_in.dtype)
    wkv = w["mla_wkv_b"].reshape(KVL, MLA_H, NOPE + DV)
    wk = jnp.concatenate([wkv[:, :, :NOPE], jnp.zeros((KVL, MLA_H, ROPE), wkv.dtype)], axis=-1)
    return dict(
        gate=w_in[:, :GATE], mla=jnp.concatenate([w_in[:, GATE:GATE + MLA_COLS], pad], axis=1),
        rw=w_in[:, GATE + MLA_COLS:GATE + MLA_COLS + 4 * RW], cv=w_in[:, GATE + MLA_COLS + 4 * RW:],
        wq=w["mla_wq_b"].reshape(QL, MLA_H, DQK).transpose(1, 0, 2),
        wk=wk.transpose(1, 0, 2), wv=wkv[:, :, NOPE:].transpose(1, 0, 2),
        wo=w["mla_w_o"].reshape(MLA_H, DV, D),
        rwo=w["rwkv_w_o"], cvo=w["conv_w_o"], out=w["w_out"], up=w["w_up"], down=w["w_down"],
    )


W_IN_WINDOW_TILE = (0, 10, 21, 31)
W_IN_WINDOW = 1664
W_IN_SHARD = 1384


def w_in_window_cols(win, chip):
    gap = MLA_PAD - MLA_COLS
    branches = []
    for j in range(4):
        lo, hi = W_IN_SHARD * j, W_IN_SHARD * (j + 1)
        base = 128 * W_IN_WINDOW_TILE[j]
        cut = GATE + MLA_COLS
        if hi <= cut:
            branches.append(lambda w, a=lo - base: w[:, a:a + W_IN_SHARD])
        elif lo >= cut:
            branches.append(lambda w, a=lo + gap - base: w[:, a:a + W_IN_SHARD])
        else:
            branches.append(lambda w, a=lo - base, n1=cut - lo, b=cut + gap - base, n2=hi - cut:
                            jnp.concatenate([w[:, a:a + n1], w[:, b:b + n2]], axis=1))
    return lax.switch(chip, branches, win)


def chip_major_grads(g):
    padded = jnp.concatenate([g["gate"], g["mla"], g["rw"], g["cv"]], axis=1)
    w_in = jnp.stack([padded[:, 128 * t:128 * t + W_IN_WINDOW] for t in W_IN_WINDOW_TILE])
    heads = MLA_H // 4
    wq = g["wq"].reshape(4, heads, QL, DQK).transpose(0, 2, 1, 3).reshape(4, QL, heads * DQK)
    wkv = jnp.concatenate([g["wk"][:, :, :NOPE], g["wv"]], axis=-1)
    wkv = wkv.reshape(4, heads, KVL, NOPE + DV).transpose(0, 2, 1, 3).reshape(4, KVL, heads * (NOPE + DV))
    return dict(
        w_in=w_in, mla_wq_b=wq, mla_wkv_b=wkv, mla_w_o=g["wo"].reshape(4, MLA_H * DV, D // 4),
        rwkv_w_o=g["rwo"], conv_w_o=g["cvo"], w_out=g["out"].reshape(4, D // 4, D), w_up=g["up"],
        w_down=g["down"].reshape(4, DFF // 4, D),
    )


def _row(v):
    return v.reshape(1, -1)


def local_step(x, positions, target, w, sm, big_of=None):
    if big_of is None:
        big_of = lambda l, _x: {n: w[n][l] for n in BIG}
    s_len = x.shape[0]
    t_row = _pick(s_len, 256, 8)
    t_wide = _pick(s_len, 128, 8)
    bd, place, rot = _consts()
    cos, sin = _rope_tables(positions)
    sds = lambda *shape: jax.ShapeDtypeStruct(shape, f32)
    sdb = lambda *shape: jax.ShapeDtypeStruct(shape, bf16)
    saved = []
    v_first = None
    for l in range(DEPTH):
        tag = f"l{l}_"
        lw = derive_layer(big_of(l, x))
        vres = l > 0
        p_norm1 = [_row(sm["attn_norm"][l])]
        (h,) = rows_fwd(_fn_norm, [x], p_norm1, [], [sds(s_len, D)], tile=t_row, name=tag + "norm1")
        gate = mm(h, lw["gate"], name=tag + "proj_gate")
        mla = mm(h, lw["mla"], name=tag + "proj_mla")
        rwc = mm(h, lw["rw"], name=tag + "proj_rwkv")
        cvc = mm(h, lw["cv"], name=tag + "proj_conv")
        p_mla = [_row(sm["mla_q_a_norm"][l]), _row(sm["mla_kv_a_norm"][l])]
        qn, kvn, kpe = rows_fwd(_fn_mla_prep, [mla], p_mla, [], [sdb(s_len, QL), sdb(s_len, KVL), sds(s_len, 128)],
                                tile=t_row, name=tag + "mla_prep")
        q_raw = mm(qn, lw["wq"], b_batched=True, name=tag + "q_proj")
        kn_pad = mm(kvn, lw["wk"], b_batched=True, name=tag + "k_proj")
        vv = mm(kvn, lw["wv"], b_batched=True, name=tag + "v_proj")
        p_qk = [_row(sm["mla_q_norm"][l]), _row(sm["mla_k_norm"][l])]
        q, k = rows_fwd(_fn_qk_post, [q_raw, kn_pad, kpe, cos, sin], p_qk, [place, rot],
                        [sds(MLA_H, s_len, DQK), sds(MLA_H, s_len, DQK)], tile=t_wide, name=tag + "qk_post")
        o = attn_fwd(q, k, vv, tq=_pick(s_len, 256, 8), name=tag + "attn")
        o_a = mm(o, lw["wo"], a_batched=True, b_batched=True, reduce_batch=True, name=tag + "o_a")
        p_rw = [_row(sm["rwkv_mu"][l]), _row(sm["rwkv_w0"][l]), w["rwkv_w2"][l], _row(sm["rwkv_a0"][l]),
                w["rwkv_a2"][l], w["rwkv_g2"][l], _row(sm["rwkv_k_k"][l]), _row(sm["rwkv_k_a"][l])]
        rw_rows, rw_halos = [rwc], (0,)
        if vres:
            p_rw += [w["rwkv_v1"][l - 1], _row(sm["rwkv_v_mu"][l - 1]), _row(sm["rwkv_v0"][l - 1]), w["rwkv_v2"][l - 1]]
            rw_rows, rw_halos = [rwc, h, v_first], (0, 1)
        fn_prep = _make_fn_rwkv_prep(vres)
        r, ld, k2, v, an, bn, g = rows_fwd(fn_prep, rw_rows, p_rw, [bd], [sds(s_len, RW)] * 7, tile=t_row,
                                           name=tag + "rwkv_prep", halos=rw_halos)
        if not vres:
            v_first = v
        y, states = wkv_fwd(r, ld, k2, v, an, bn, name=tag + "wkv")
        p_post = [_row(sm["rwkv_ln_w"][l]), _row(sm["rwkv_ln_b"][l]), _row(sm["rwkv_r_k"][l])]
        (yb,) = rows_fwd(_fn_rwkv_post, [y, r, k2, v, g], p_post, [bd], [sdb(s_len, RW)], tile=t_row,
                         name=tag + "rwkv_post")
        o_b = mm(yb, lw["rwo"], name=tag + "o_b")
        p_cv = [w["conv_w"][l][q:q + 1] for q in range(3)]
        (yc,) = rows_fwd(_fn_conv, [cvc], p_cv, [], [sdb(s_len, CW)], tile=t_row, name=tag + "conv", halos=(0,))
        o_c = mm(yc, lw["cvo"], name=tag + "o_c")
        (merged,) = rows_fwd(_fn_merge, [gate, o_a, o_b, o_c], [], [], [sdb(s_len, D)], tile=t_wide,
                             name=tag + "merge")
        x1 = mm(merged, lw["out"], add=x, name=tag + "out_proj")
        p_norm2 = [_row(sm["mlp_norm"][l])]
        (h2,) = rows_fwd(_fn_norm, [x1], p_norm2, [], [sdb(s_len, D)], tile=t_row, name=tag + "norm2")
        up, act = mm(h2, lw["up"], relu2_out=True, name=tag + "up")
        x2 = mm(act, lw["down"], add=x1, name=tag + "down")
        saved.append(dict(lw=lw, x=x, h=h, gate=gate, mla=mla, rwc=rwc, cvc=cvc, qn=qn, kvn=kvn, kpe=kpe,
                          q_raw=q_raw, kn_pad=kn_pad, vv=vv, q=q, k=k, o=o, o_a=o_a, r=r, ld=ld, k2=k2, v=v,
                          an=an, bn=bn, g=g, y=y, states=states, yb=yb, o_b=o_b, yc=yc, o_c=o_c, merged=merged,
                          x1=x1, h2=h2, up=up, act=act, p_norm1=p_norm1, p_mla=p_mla, p_qk=p_qk, p_rw=p_rw,
                          p_post=p_post, p_cv=p_cv, p_norm2=p_norm2, rw_rows=rw_rows, rw_halos=rw_halos,
                          fn_prep=fn_prep, v_first=v_first if vres else None))
        x = x2

    loss, dx = loss_head(x, target, tile=t_row, name="loss_head")

    grads = {n: [None] * (DEPTH - 1 if n in ("rwkv_v1", "rwkv_v_mu", "rwkv_v0", "rwkv_v2") else DEPTH)
             for n in WEIGHTS}
    dv_first = None
    for l in reversed(range(DEPTH)):
        tag = f"b{l}_"
        sv = saved[l]
        lw = sv["lw"]
        vres = l > 0
        g_down = mm(sv["act"], dx, ta=True, out_dtype=bf16, name=tag + "g_down")
        dup = mm(dx, lw["down"], tb=True, act_grad=sv["up"], out_dtype=bf16, name=tag + "d_up")
        g_up = mm(sv["h2"], dup, ta=True, n_split=4, out_dtype=bf16, name=tag + "g_up")
        dh2 = mm(dup, lw["up"], tb=True, name=tag + "d_h2")
        (dx1,), (g_n2,) = rows_bwd(_fn_norm, [sv["x1"]], sv["p_norm2"], [], [[dh2]], tile=t_row,
                                   name=tag + "norm2", extra={0: [dx]})
        g_out = mm(sv["merged"], dx1, ta=True, out_dtype=bf16, name=tag + "g_out")
        dmerged = mm(dx1, lw["out"], tb=True, name=tag + "d_merged")
        (dgate, do_a, do_b, do_c), _ = rows_bwd(_fn_merge, [sv["gate"], sv["o_a"], sv["o_b"], sv["o_c"]], [], [],
                                                [[dmerged]], tile=t_wide, name=tag + "merge",
                                                grad_dtypes=[bf16] * 4)
        g_cvo = mm(sv["yc"], do_c, ta=True, n_split=4, out_dtype=bf16, name=tag + "g_cvo")
        dyc = mm(do_c, lw["cvo"], tb=True, name=tag + "d_yc")
        (dcvc,), g_cw = rows_bwd(_fn_conv, [sv["cvc"]], sv["p_cv"], [], [[dyc]], tile=t_row, name=tag + "conv",
                                    halos=(0,))
        g_rwo = mm(sv["yb"], do_b, ta=True, n_split=4, out_dtype=bf16, name=tag + "g_rwo")
        dyb = mm(do_b, lw["rwo"], tb=True, name=tag + "d_yb")
        (dy, dr_p, dk_p, dv_p, dg), g_post = rows_bwd(
            _fn_rwkv_post, [sv["y"], sv["r"], sv["k2"], sv["v"], sv["g"]], sv["p_post"], [bd], [[dyb]], tile=t_row,
            name=tag + "rwkv_post")
        dr_s, dld, dk_s, dv_s, dan, dbn = wkv_bwd(sv["r"], sv["ld"], sv["k2"], sv["v"], sv["an"], sv["bn"],
                                                  sv["states"], dy, name=tag + "wkv")
        dv_list = [dv_s, dv_p] + ([dv_first] if (not vres and dv_first is not None) else [])
        d_prep, g_prep = rows_bwd(
            sv["fn_prep"], sv["rw_rows"], sv["p_rw"], [bd],
            [[dr_s, dr_p], [dld], [dk_s, dk_p], dv_list, [dan], [dbn], [dg]], tile=t_row, name=tag + "rwkv_prep",
            halos=sv["rw_halos"])
        drwc = d_prep[0]
        dh_extra = []
        if vres:
            dh_extra = [d_prep[1]]
            dv_first = d_prep[2]
        g_wo = mm(sv["o"], do_a, ta=True, a_batched=True, n_split=4, tk=s_len, out_dtype=bf16, name=tag + "g_wo")
        do = mm(do_a, lw["wo"], tb=True, b_batched=True, name=tag + "d_o")
        dq, dk, dvv = attn_bwd(sv["q"], sv["k"], sv["vv"], do, tq=_pick(s_len, 256, 8), name=tag + "attn")
        (dq_raw, dkn_pad, dkpe), g_qk = rows_bwd(
            _fn_qk_post, [sv["q_raw"], sv["kn_pad"], sv["kpe"], cos, sin], sv["p_qk"], [place, rot], [[dq], [dk]],
            tile=t_wide, name=tag + "qk_post", grad_rows=[0, 1, 2], grad_dtypes=[bf16, bf16, f32])
        g_wq = mm(sv["qn"], dq_raw, ta=True, b_batched=True, tk=s_len, out_dtype=bf16, name=tag + "g_wq")
        g_wk = mm(sv["kvn"], dkn_pad, ta=True, b_batched=True, tk=s_len, out_dtype=bf16, name=tag + "g_wk")
        g_wv = mm(sv["kvn"], dvv, ta=True, b_batched=True, tk=s_len, out_dtype=bf16, name=tag + "g_wv")
        dqn = mm(dq_raw, lw["wq"], tb=True, a_batched=True, b_batched=True, reduce_batch=True, name=tag + "d_qn")
        dkvn = mm(dkn_pad, lw["wk"], tb=True, a_batched=True, b_batched=True, reduce_batch=True, name=tag + "d_kvn_k")
        dkvn = mm(dvv, lw["wv"], tb=True, a_batched=True, b_batched=True, reduce_batch=True, add=dkvn,
                  name=tag + "d_kvn_v")
        (dmla,), g_mla = rows_bwd(_fn_mla_prep, [sv["mla"]], sv["p_mla"], [], [[dqn], [dkvn], [dkpe]], tile=t_row,
                                  name=tag + "mla_prep", grad_dtypes=[bf16])
        g_gate = mm(sv["h"], dgate, ta=True, out_dtype=bf16, name=tag + "g_gate")
        g_mlaw = mm(sv["h"], dmla, ta=True, out_dtype=bf16, name=tag + "g_mla")
        g_rw = mm(sv["h"], drwc, ta=True, out_dtype=bf16, name=tag + "g_rw")
        g_cv = mm(sv["h"], dcvc, ta=True, out_dtype=bf16, name=tag + "g_cv")
        dh = mm(dgate, lw["gate"], tb=True, name=tag + "d_h_gate")
        dh = mm(dmla, lw["mla"], tb=True, add=dh, name=tag + "d_h_mla")
        dh = mm(drwc, lw["rw"], tb=True, add=dh, name=tag + "d_h_rw")
        dh = mm(dcvc, lw["cv"], tb=True, add=dh, name=tag + "d_h_cv")
        (dx,), (g_n1,) = rows_bwd(_fn_norm, [sv["x"]], sv["p_norm1"], [], [[dh] + dh_extra], tile=t_row,
                                  name=tag + "norm1", extra={0: [dx1]})
        slabs = chip_major_grads(dict(gate=g_gate, mla=g_mlaw, rw=g_rw, cv=g_cv, wq=g_wq, wk=g_wk, wv=g_wv, wo=g_wo,
                                      rwo=g_rwo, cvo=g_cvo, out=g_out, up=g_up, down=g_down))
        for n, val in slabs.items():
            grads[n][l] = val
        grads["attn_norm"][l], grads["mlp_norm"][l] = g_n1, g_n2
        grads["mla_q_a_norm"][l], grads["mla_kv_a_norm"][l] = g_mla
        grads["mla_q_norm"][l], grads["mla_k_norm"][l] = g_qk
        grads["rwkv_ln_w"][l], grads["rwkv_ln_b"][l], grads["rwkv_r_k"][l] = g_post
        for n, val in zip(["rwkv_mu", "rwkv_w0", "rwkv_w2", "rwkv_a0", "rwkv_a2", "rwkv_g2", "rwkv_k_k", "rwkv_k_a"],
                          g_prep[:8]):
            grads[n][l] = val
        if vres:
            for n, val in zip(["rwkv_v1", "rwkv_v_mu", "rwkv_v0", "rwkv_v2"], g_prep[8:12]):
                grads[n][l - 1] = val
        grads["conv_w"][l] = jnp.concatenate(g_cw, axis=0)
    return loss, dx, grads


def _split3(a):
    hi = a.astype(bf16)
    r1 = a - hi.astype(f32)
    mid = r1.astype(bf16)
    lo = (r1 - mid.astype(f32)).astype(bf16)
    return hi, mid, lo


def _shard_axis(name):
    return 1 if name in ROW_SHARDED else 2


def _pack(pieces, width, dtype, row_align):
    flat = jnp.concatenate([p.reshape(-1).astype(dtype) for p in pieces])
    rows = -(-flat.shape[0] // width)
    rows = -(-rows // row_align) * row_align
    return jnp.pad(flat, (0, rows * width - flat.shape[0])).reshape(rows, width)


def _unpack(flat2d, shapes):
    flat = flat2d.reshape(-1)
    out, off = [], 0
    for shp in shapes:
        n = int(np.prod(shp))
        out.append(flat[off:off + n].reshape(shp))
        off += n
    return out


def kernel(x, positions, attn_norm, w_in, mla_q_a_norm, mla_wq_b, mla_kv_a_norm, mla_wkv_b, mla_q_norm, mla_k_norm, mla_w_o, rwkv_mu, rwkv_w0, rwkv_w2, rwkv_a0, rwkv_a2, rwkv_g2, rwkv_k_k, rwkv_k_a, rwkv_r_k, rwkv_ln_w, rwkv_ln_b, rwkv_w_o, rwkv_v1, rwkv_v_mu, rwkv_v0, rwkv_v2, conv_w, conv_w_o, w_out, mlp_norm, w_up, w_down, loss_target, m_attn_norm, m_w_in, m_mla_q_a_norm, m_mla_wq_b, m_mla_kv_a_norm, m_mla_wkv_b, m_mla_q_norm, m_mla_k_norm, m_mla_w_o, m_rwkv_mu, m_rwkv_w0, m_rwkv_w2, m_rwkv_a0, m_rwkv_a2, m_rwkv_g2, m_rwkv_k_k, m_rwkv_k_a, m_rwkv_r_k, m_rwkv_ln_w, m_rwkv_ln_b, m_rwkv_w_o, m_rwkv_v1, m_rwkv_v_mu, m_rwkv_v0, m_rwkv_v2, m_conv_w, m_conv_w_o, m_w_out, m_mlp_norm, m_w_up, m_w_down, v_attn_norm, v_w_in, v_mla_q_a_norm, v_mla_wq_b, v_mla_kv_a_norm, v_mla_wkv_b, v_mla_q_norm, v_mla_k_norm, v_mla_w_o, v_rwkv_mu, v_rwkv_w0, v_rwkv_w2, v_rwkv_a0, v_rwkv_a2, v_rwkv_g2, v_rwkv_k_k, v_rwkv_k_a, v_rwkv_r_k, v_rwkv_ln_w, v_rwkv_ln_b, v_rwkv_w_o, v_rwkv_v1, v_rwkv_v_mu, v_rwkv_v0, v_rwkv_v2, v_conv_w, v_conv_w_o, v_w_out, v_mlp_norm, v_w_up, v_w_down):
    args = dict(locals())
    wts = {n: args[n] for n in WEIGHTS}
    mom = {n: args["m_" + n] for n in WEIGHTS}
    var = {n: args["v_" + n] for n in WEIGHTS}
    chip = 2 * lax.axis_index("x") + lax.axis_index("y")
    core = lax.axis_index("c").astype(jnp.int32).reshape(1)

    med_names = [n for n in MED if n != "conv_w"]
    med_pieces = [wts[n] for n in med_names] + list(_split3(wts["conv_w"]))
    med_shapes = [p.shape for p in med_pieces]
    chip_idx = chip.astype(jnp.int32).reshape(1)
    shards0 = [wts[n][0].astype(bf16) for n in BIG] + [_pack(med_pieces, 128, bf16, 32)]
    got0, gathered = gather_weights(shards0, name="gather_weights_l0")
    got0 = [place_slab(g, s, chip_idx, name=f"place_own_l0_{q}") for q, (g, s) in enumerate(zip(got0, shards0))]
    shards1 = [wts[n][1].astype(bf16) for n in BIG]
    shards1[0] = shards1[0] + gathered[0, 0].astype(bf16)
    in_flight = gather_start(shards1, name="gather_l1_start")

    def whole_of(slabs):
        out = {}
        for n, by_chip in zip(BIG, slabs):
            _, rows, cols = by_chip.shape
            if n in ROW_SHARDED:
                out[n] = by_chip.reshape(4 * rows, cols)
            else:
                out[n] = by_chip.transpose(1, 0, 2).reshape(rows, 4 * cols)
        return out

    def big_of(l, x_l):
        if l == 0:
            return whole_of(got0)
        send_sems, recv_sems, thru, lands, _ = in_flight
        thru, lands = gather_wait(send_sems, recv_sems, thru, lands, x_l, name="gather_l1_wait")
        lands = gather_forward(lands, name="gather_l1_forward")
        return whole_of([place_slab(g, s, chip_idx, name=f"place_own_l1_{q}")
                         for q, (g, s) in enumerate(zip(lands, thru))])

    whole = {}
    per_chip = [_unpack(got0[len(BIG)][j], med_shapes) for j in range(4)]
    for q, n in enumerate(med_names):
        whole[n] = jnp.concatenate([per_chip[j][q] for j in range(4)], axis=_shard_axis(n)).astype(f32)
    base = len(med_names)
    cw_parts = [jnp.concatenate([per_chip[j][base + t] for j in range(4)], axis=2).astype(f32) for t in range(3)]
    whole["conv_w"] = (cw_parts[0] + cw_parts[1]) + cw_parts[2]
    small = {n: wts[n] for n in SMALL}
    small["rwkv_r_k"] = wts["rwkv_r_k"].reshape(DEPTH, RW)

    x0 = x[0] + in_flight[4][0, 0]
    loss, grad_x, grads = local_step(x0, positions[0], loss_target[0], whole, small, big_of)
    loss = lax.psum(loss, ("x", "y", "c"))

    sm_names = SMALL + MED
    sm_grads = [jnp.stack(grads[n]) for n in sm_names]
    sm_shapes = [g.shape for g in sm_grads]
    sm_sum = _unpack(all_reduce_small(_pack(sm_grads, 128, f32, 8), name="reduce_small"), sm_shapes)
    gsum = {}
    for n, g in zip(sm_names, sm_sum):
        if n in MED:
            ax = _shard_axis(n)
            width = wts[n].shape[ax]
            g = lax.dynamic_slice_in_dim(g, chip * width, width, axis=ax)
        gsum[n] = g.reshape(wts[n].shape)
    slabs = [grads[n][l] for n in BIG for l in range(DEPTH)]
    labels = [f"{n}_{l}" for n in BIG for l in range(DEPTH)]
    from_sibling = grads_to_sibling(slabs, name="grads_to_sibling")
    chip_sums = [pair_sum(s, t, core, name="pair_sum_" + lb) for s, t, lb in zip(slabs, from_sibling, labels)]
    arrived = chips_all_to_all(chip_sums, name="scatter_grads")
    bufs, layout = [], []
    for q in range(len(BIG)):
        rows = slabs[q * DEPTH].shape[1]
        buf = None
        for l in range(DEPTH):
            buf = sum4_into(arrived[q * DEPTH + l], buf, core, layer=l, total_rows=DEPTH * rows,
                            name="sum_chips_" + labels[q * DEPTH + l])
            layout.append((q, l * rows, rows))
        bufs.append(buf)
    reduced = join_halves(bufs, layout, name="join_halves")

    out_g, out_d, out_m, out_v = {}, {}, {}, {}
    for q, n in enumerate(BIG):
        shp = wts[n].shape
        as2d = lambda a: a.reshape(-1, shp[-1])
        g2d = w_in_window_cols(reduced[q], chip) if n == "w_in" else reduced[q]
        res = adamw(as2d(wts[n]), g2d, as2d(mom[n]), as2d(var[n]), name="adamw_" + n)
        out_g[n], out_d[n], out_m[n], out_v[n] = [r.reshape(shp) for r in res]
    sm_all = SMALL + MED
    sm_shapes2 = [wts[n].shape for n in sm_all]
    res = adamw(_pack([wts[n] for n in sm_all], 128, f32, 8), _pack([gsum[n] for n in sm_all], 128, f32, 8),
                _pack([mom[n] for n in sm_all], 128, f32, 8), _pack([var[n] for n in sm_all], 128, f32, 8),
                name="adamw_small")
    for tgt, flat in zip((out_g, out_d, out_m, out_v), res):
        for n, val in zip(sm_all, _unpack(flat, sm_shapes2)):
            tgt[n] = val
    return (loss, grad_x[None], *[out_g[n] for n in WEIGHTS], *[out_d[n] for n in WEIGHTS],
            *[out_m[n] for n in WEIGHTS], *[out_v[n] for n in WEIGHTS])
```

```python
import functools

import jax
import jax.numpy as jnp
import numpy as np
from jax import lax
from jax.experimental import pallas as pl
from jax.experimental.pallas import tpu as pltpu

f32, bf16 = jnp.float32, jnp.bfloat16
HI = lax.Precision.HIGHEST
MESH = pl.DeviceIdType.MESH

D = 1024
DEPTH = 2
MLA_H, NOPE, ROPE, DQK, DV = 8, 64, 32, 96, 64
QL, KVL = 384, 256
RW, RH, RN = 256, 4, 64
DL, AL, GL, MVL = 64, 64, 128, 32
CW = 256
DFF = 4096
GATE = 3 * D
MLA_COLS = QL + KVL + ROPE
MLA_PAD = 768
NORM_EPS = 1e-6
GN_EPS = 64e-5
ROPE_THETA = 10000.0
LR, B1, B2, EPS, WD, STEP = 0.001, 0.9, 0.999, 1e-08, 0.01, 10

VMEM_LIMIT = 52 * 1024 * 1024
WKV_CHUNK = 64
WKV_CHUNKS_PER_STEP = 2
ATTN_SEGMENTS = 4

BIG = ["w_in", "mla_wq_b", "mla_wkv_b", "mla_w_o", "rwkv_w_o", "conv_w_o", "w_out", "w_up", "w_down"]
MED = ["rwkv_w2", "rwkv_a2", "rwkv_g2", "rwkv_v1", "rwkv_v2", "conv_w"]
ROW_SHARDED = {"w_out", "w_down", "rwkv_v1"}
SMALL = ["attn_norm", "mla_q_a_norm", "mla_kv_a_norm", "mla_q_norm", "mla_k_norm", "rwkv_mu", "rwkv_w0",
         "rwkv_a0", "rwkv_k_k", "rwkv_k_a", "rwkv_r_k", "rwkv_ln_w", "rwkv_ln_b", "rwkv_v_mu", "rwkv_v0",
         "mlp_norm"]
WEIGHTS = ["attn_norm", "w_in", "mla_q_a_norm", "mla_wq_b", "mla_kv_a_norm", "mla_wkv_b", "mla_q_norm",
           "mla_k_norm", "mla_w_o", "rwkv_mu", "rwkv_w0", "rwkv_w2", "rwkv_a0", "rwkv_a2", "rwkv_g2",
           "rwkv_k_k", "rwkv_k_a", "rwkv_r_k", "rwkv_ln_w", "rwkv_ln_b", "rwkv_w_o", "rwkv_v1", "rwkv_v_mu",
           "rwkv_v0", "rwkv_v2", "conv_w", "conv_w_o", "w_out", "mlp_norm", "w_up", "w_down"]


def _cparams(sem=None):
    return pltpu.CompilerParams(dimension_semantics=sem, vmem_limit_bytes=VMEM_LIMIT)


def _pick(dim, pref, align):
    if dim <= pref:
        return dim
    t = (pref // align) * align
    while t >= align:
        if dim % t == 0:
            return t
        t -= align
    return dim


def _bdot(a, b, dims):
    return lax.dot_general(a.astype(bf16), b.astype(bf16), (dims, ((), ())), preferred_element_type=f32)


@jax.custom_vjp
def _mm(a, b):
    return _bdot(a, b, ((1,), (0,)))


def _mm_fwd(a, b):
    return _mm(a, b), (a, b)


def _mm_bwd(res, g):
    a, b = res
    return _bdot(g, b, ((1,), (1,))), _bdot(a, g, ((0,), (0,)))


_mm.defvjp(_mm_fwd, _mm_bwd)


@jax.custom_vjp
def _mm_nt(a, b):
    return _bdot(a, b, ((1,), (1,)))


def _mm_nt_fwd(a, b):
    return _mm_nt(a, b), (a, b)


def _mm_nt_bwd(res, g):
    a, b = res
    return _bdot(g, b, ((1,), (0,))), _bdot(g, a, ((0,), (0,)))


_mm_nt.defvjp(_mm_nt_fwd, _mm_nt_bwd)


_NN, _NT, _TN = ((1,), (0,)), ((1,), (1,)), ((0,), (0,))


def _dg(a, b, dims):
    return lax.dot_general(a, b, (dims, ((), ())), preferred_element_type=f32)


def _bf16_pieces(x, count):
    out, rest = [], x
    for q in range(count):
        piece = rest.astype(bf16)
        out.append(piece)
        if q + 1 < count:
            rest = rest - piece.astype(f32)
    return out


def _dot3(a, b, dims):
    (ah, al), (bh, bl) = _bf16_pieces(a, 2), _bf16_pieces(b, 2)
    return _dg(ah, bh, dims) + (_dg(ah, bl, dims) + _dg(al, bh, dims))


@jax.custom_vjp
def _hdot(a, b):
    return _dot3(a, b, _NN)


@jax.custom_vjp
def _hdot_nt(a, b):
    return _dot3(a, b, _NT)


@jax.custom_vjp
def _hdot_tn(a, b):
    return _dot3(a, b, _TN)


_hdot.defvjp(lambda a, b: (_hdot(a, b), (a, b)), lambda res, g: (_hdot_nt(g, res[1]), _hdot_tn(res[0], g)))
_hdot_nt.defvjp(lambda a, b: (_hdot_nt(a, b), (a, b)), lambda res, g: (_hdot(g, res[1]), _hdot_tn(g, res[0])))
_hdot_tn.defvjp(lambda a, b: (_hdot_tn(a, b), (a, b)), lambda res, g: (_hdot_nt(res[1], g), _hdot(res[0], g)))


_BNN, _BNT, _BTN = ((2,), (1,)), ((2,), (2,)), ((1,), (1,))


def _bdg(a, b, dims):
    return lax.dot_general(a, b, (dims, ((0,), (0,))), preferred_element_type=f32)


def _bdot3(a, b, dims):
    (ah, al), (bh, bl) = _bf16_pieces(a, 2), _bf16_pieces(b, 2)
    return _bdg(ah, bh, dims) + (_bdg(ah, bl, dims) + _bdg(al, bh, dims))


@jax.custom_vjp
def _hbnn(a, b):
    return _bdot3(a, b, _BNN)


@jax.custom_vjp
def _hbnt(a, b):
    return _bdot3(a, b, _BNT)


@jax.custom_vjp
def _hbtn(a, b):
    return _bdot3(a, b, _BTN)


_hbnn.defvjp(lambda a, b: (_hbnn(a, b), (a, b)), lambda res, g: (_hbnt(g, res[1]), _hbtn(res[0], g)))
_hbnt.defvjp(lambda a, b: (_hbnt(a, b), (a, b)), lambda res, g: (_hbnn(g, res[1]), _hbtn(g, res[0])))
_hbtn.defvjp(lambda a, b: (_hbtn(a, b), (a, b)), lambda res, g: (_hbnt(res[1], g), _hbnn(res[0], g)))


@functools.partial(jax.custom_vjp, nondiff_argnums=(2,))
def _exact_bl(m, x, transposed):
    mb = m.astype(bf16)
    hi, mid, lo = _bf16_pieces(x, 3)
    dims = _BTN if transposed else _BNN
    return (_bdg(mb, hi, dims) + _bdg(mb, mid, dims)) + _bdg(mb, lo, dims)


_exact_bl.defvjp(lambda m, x, transposed: (_exact_bl(m, x, transposed), m),
                 lambda transposed, m, g: (jnp.zeros_like(m), _exact_bl(m, g, not transposed)))


@functools.partial(jax.custom_vjp, nondiff_argnums=(2,))
def _exact_l(m, x, transposed):
    mb = m.astype(bf16)
    hi, mid, lo = _bf16_pieces(x, 3)
    dims = _TN if transposed else _NN
    return (_dg(mb, hi, dims) + _dg(mb, mid, dims)) + _dg(mb, lo, dims)


_exact_l.defvjp(lambda m, x, transposed: (_exact_l(m, x, transposed), m),
                lambda transposed, m, g: (jnp.zeros_like(m), _exact_l(m, g, not transposed)))


@functools.partial(jax.custom_vjp, nondiff_argnums=(2,))
def _exact_r(x, m, transposed):
    mb = m.astype(bf16)
    hi, mid, lo = _bf16_pieces(x, 3)
    dims = _NT if transposed else _NN
    return (_dg(hi, mb, dims) + _dg(mid, mb, dims)) + _dg(lo, mb, dims)


_exact_r.defvjp(lambda x, m, transposed: (_exact_r(x, m, transposed), m),
                lambda transposed, m, g: (_exact_r(g, m, not transposed), jnp.zeros_like(m)))


def _rms(x, g, eps=NORM_EPS):
    return x * lax.rsqrt(jnp.mean(x * x, axis=-1, keepdims=True) + eps) * g


def _sigmoid(x):
    return 1.0 / (1.0 + jnp.exp(-x))


def _softplus(x):
    return jnp.maximum(x, 0.0) + jnp.log(1.0 + jnp.exp(-jnp.maximum(x, -x)))


def _lane_split(x, sizes):
    bounds = np.cumsum([0] + list(sizes))

    @jax.custom_vjp
    def split(v):
        return tuple(v[..., int(bounds[q]):int(bounds[q + 1])] for q in range(len(sizes)))

    split.defvjp(lambda v: (split(v), None), lambda _, g: (jnp.concatenate(g, axis=-1),))
    return split(x)


def _row_split(x, sizes):
    bounds = np.cumsum([0] + list(sizes))

    @jax.custom_vjp
    def split(v):
        return tuple(v[..., int(bounds[q]):int(bounds[q + 1]), :] for q in range(len(sizes)))

    split.defvjp(lambda v: (split(v), None), lambda _, g: (jnp.concatenate(g, axis=-2),))
    return split(x)


def _unstack(x):
    @jax.custom_vjp
    def unstack(v):
        return tuple(v[q] for q in range(v.shape[0]))

    unstack.defvjp(lambda v: (unstack(v), None), lambda _, g: (jnp.stack(g, axis=0),))
    return unstack(x)


def _shift_mats(t, k):
    r = lax.broadcasted_iota(jnp.int32, (t, t), 0)
    c = lax.broadcasted_iota(jnp.int32, (t, t), 1)
    inner = (r - c == k).astype(f32)
    r8 = lax.broadcasted_iota(jnp.int32, (t, 8), 0)
    c8 = lax.broadcasted_iota(jnp.int32, (t, 8), 1)
    edge = (c8 - r8 == 8 - k).astype(f32)
    return inner, edge


def _shift(x, halo, k):
    inner, edge = _shift_mats(x.shape[0], k)
    return _exact_l(inner, x, False) + jnp.dot(edge, halo, precision=HI, preferred_element_type=f32)


def mm(a, b, *, name, ta=False, tb=False, a_batched=False, b_batched=False, reduce_batch=False, add=None,
       act_grad=None, relu2_out=False, n_split=1, out_dtype=f32, tm=1024, tn=1024, tk=1024):
    ash, bsh = a.shape[-2:], b.shape[-2:]
    (k_, m_) = ash if ta else ash[::-1]
    (k2_, n_) = bsh[::-1] if tb else bsh
    assert k_ == k2_, (a.shape, b.shape, ta, tb)
    hb = a.shape[0] if a_batched else (b.shape[0] if b_batched else 1)
    batched_out = (a_batched or b_batched) and not reduce_batch
    h_out = hb if batched_out else 1
    h_red = hb if reduce_batch else 1
    tm = _pick(m_, tm, 128)
    tn = _pick(n_ // n_split, tn, 128)
    tk = _pick(k_, tk, 128)
    nm, nn, nk = m_ // tm, n_ // tn, k_ // tk

    def a_map(i, j, ho, hr, kk):
        blk = (kk, i) if ta else (i, kk)
        return ((ho if batched_out else hr),) + blk if a_batched else blk

    def b_map(i, j, ho, hr, kk):
        blk = (j, kk) if tb else (kk, j)
        return ((ho if batched_out else hr),) + blk if b_batched else blk

    a_blk = (tk, tm) if ta else (tm, tk)
    b_blk = (tn, tk) if tb else (tk, tn)
    in_specs = [pl.BlockSpec(((1,) + a_blk) if a_batched else a_blk, a_map),
                pl.BlockSpec(((1,) + b_blk) if b_batched else b_blk, b_map)]
    args = [a, b]
    for extra in (add, act_grad):
        if extra is not None:
            in_specs.append(pl.BlockSpec((tm, tn), lambda i, j, ho, hr, kk: (i, j)))
            args.append(extra)
    if n_split > 1:
        per = n_ // n_split // tn
        if batched_out:
            out_spec = pl.BlockSpec((1, 1, tm, tn), lambda i, j, ho, hr, kk: (j // per, ho, i, j % per))
            out_shape = jax.ShapeDtypeStruct((n_split, hb, m_, n_ // n_split), out_dtype)
        else:
            out_spec = pl.BlockSpec((1, tm, tn), lambda i, j, ho, hr, kk: (j // per, i, j % per))
            out_shape = jax.ShapeDtypeStruct((n_split, m_, n_ // n_split), out_dtype)
    elif batched_out:
        out_spec = pl.BlockSpec((1, tm, tn), lambda i, j, ho, hr, kk: (ho, i, j))
        out_shape = jax.ShapeDtypeStruct((hb, m_, n_), out_dtype)
    else:
        out_spec = pl.BlockSpec((tm, tn), lambda i, j, ho, hr, kk: (i, j))
        out_shape = jax.ShapeDtypeStruct((m_, n_), out_dtype)
    lead = (0,) * (int(batched_out) + int(n_split > 1))
    dims = ((0,) if ta else (1,), (1,) if tb else (0,))
    has_add, has_act = add is not None, act_grad is not None

    def body(*refs):
        a_ref, b_ref = refs[0], refs[1]
        pos = 2
        add_ref = act_ref = None
        if has_add:
            add_ref = refs[pos]
            pos += 1
        if has_act:
            act_ref = refs[pos]
            pos += 1
        o_ref, acc_ref = refs[pos], refs[-1]
        hr, kk = pl.program_id(3), pl.program_id(4)
        first = jnp.logical_and(hr == 0, kk == 0)
        last = jnp.logical_and(hr == h_red - 1, kk == nk - 1)
        av = a_ref[0] if a_batched else a_ref[...]
        bv = b_ref[0] if b_batched else b_ref[...]
        p = _bdot(av, bv, dims)

        @pl.when(first)
        def _():
            acc_ref[...] = p

        @pl.when(jnp.logical_not(first))
        def _():
            acc_ref[...] += p

        @pl.when(last)
        def _():
            r = acc_ref[...]
            if has_act:
                r = r * (2.0 * jnp.maximum(act_ref[...], 0.0))
            if has_add:
                r = r + add_ref[...]
            if lead:
                o_ref[lead] = r.astype(out_dtype)
            else:
                o_ref[...] = r.astype(out_dtype)
            if relu2_out:
                refs[pos + 1][...] = jnp.square(jnp.maximum(r, 0.0)).astype(bf16)

    if relu2_out:
        assert not lead
        out_spec = [out_spec, out_spec]
        out_shape = [out_shape, jax.ShapeDtypeStruct(out_shape.shape, bf16)]
    return pl.pallas_call(
        body, name=name, grid=(nm, nn, h_out, h_red, nk), in_specs=in_specs, out_specs=out_spec,
        out_shape=out_shape, scratch_shapes=[pltpu.VMEM((tm, tn), f32)],
        compiler_params=_cparams(("parallel", "parallel", "parallel", "arbitrary", "arbitrary")),
    )(*args)


def _row_spec(arr, tile, idx):
    if arr.ndim == 2:
        return pl.BlockSpec((tile, arr.shape[1]), lambda i: (idx(i), 0))
    return pl.BlockSpec((arr.shape[0], tile, arr.shape[2]), lambda i: (0, idx(i), 0))


def _halo_spec(arr, tile, idx):
    per = tile // 8
    return pl.BlockSpec((8, arr.shape[1]), lambda i: (jnp.maximum(idx(i) * per - 1, 0), 0))


def _full_spec(arr):
    nd = arr.ndim
    return pl.BlockSpec(arr.shape, lambda i: (0,) * nd)


def _load_f32(ref):
    val = ref[...]
    return val.astype(f32) if val.dtype == bf16 else val


def rows_fwd(fn, rows, params, consts, out_shapes, *, tile, name, halos=()):
    s_len = rows[0].shape[-2]
    n = s_len // tile
    nr, nh, npar, nc = len(rows), len(halos), len(params), len(consts)
    ident = lambda i: i
    in_specs = ([_row_spec(r, tile, ident) for r in rows] + [_halo_spec(rows[h], tile, ident) for h in halos]
                + [_full_spec(p) for p in params] + [_full_spec(c) for c in consts])
    out_specs = [_row_spec(o, tile, ident) for o in out_shapes]

    def body(*refs):
        i = pl.program_id(0)
        rv = [_load_f32(r) for r in refs[:nr]]
        keep = (i > 0).astype(f32)
        hv = [r[...] * keep for r in refs[nr:nr + nh]]
        pv = [r[...] for r in refs[nr + nh:nr + nh + npar]]
        cv = [r[...] for r in refs[nr + nh + npar:nr + nh + npar + nc]]
        outs = fn(rv, hv, pv, cv)
        for o_ref, o in zip(refs[nr + nh + npar + nc:], outs):
            o_ref[...] = o.astype(o_ref.dtype)

    return pl.pallas_call(
        body, name=name, grid=(n,), in_specs=in_specs, out_specs=out_specs, out_shape=list(out_shapes),
        compiler_params=_cparams(("arbitrary",)),
    )(*rows, *[rows[h] for h in halos], *params, *consts)


def rows_bwd(fn, rows, params, consts, douts, *, tile, name, halos=(), grad_rows=None, extra=None,
             grad_dtypes=None):
    s_len = rows[0].shape[-2]
    n = s_len // tile
    nr, nh, npar, nc = len(rows), len(halos), len(params), len(consts)
    grad_rows = list(range(nr)) if grad_rows is None else list(grad_rows)
    extra = extra or {}
    assert all(h in grad_rows for h in halos)
    rev = lambda i: n - 1 - i
    dflat = [d for ds in douts for d in ds]
    dcount = [len(ds) for ds in douts]
    eflat = [e for g in grad_rows for e in extra.get(g, [])]
    ecount = [len(extra.get(g, [])) for g in grad_rows]
    in_specs = ([_row_spec(r, tile, rev) for r in rows] + [_halo_spec(rows[h], tile, rev) for h in halos]
                + [_full_spec(p) for p in params] + [_full_spec(c) for c in consts]
                + [_row_spec(d, tile, rev) for d in dflat] + [_row_spec(e, tile, rev) for e in eflat])
    grad_dtypes = [f32] * len(grad_rows) if grad_dtypes is None else list(grad_dtypes)
    assert all(grad_dtypes[q] == f32 for q, g in enumerate(grad_rows) if g in halos)
    out_shapes = ([jax.ShapeDtypeStruct(rows[g].shape, dt) for g, dt in zip(grad_rows, grad_dtypes)]
                  + [jax.ShapeDtypeStruct(p.shape, f32) for p in params])
    out_specs = [_row_spec(rows[g], tile, rev) for g in grad_rows] + [_full_spec(p) for p in params]
    scratch = [pltpu.VMEM((8, rows[h].shape[1]), f32) for h in halos]
    n_in = nr + nh + npar + nc + len(dflat) + len(eflat)
    n_out = len(grad_rows) + npar

    def body(*refs):
        i = pl.program_id(0)
        rv = [_load_f32(r) for r in refs[:nr]]
        keep = (i < n - 1).astype(f32)
        hv = [r[...] * keep for r in refs[nr:nr + nh]]
        pv = [r[...] for r in refs[nr + nh:nr + nh + npar]]
        pos = nr + nh + npar
        cv = [r[...] for r in refs[pos:pos + nc]]
        pos += nc
        dv = []
        for cnt in dcount:
            acc = _load_f32(refs[pos])
            for q in range(1, cnt):
                acc = acc + _load_f32(refs[pos + q])
            dv.append(acc)
            pos += cnt
        ev = []
        for cnt in ecount:
            ev.append([_load_f32(refs[pos + q]) for q in range(cnt)])
            pos += cnt
        out_refs = refs[n_in:n_in + n_out]
        carry_refs = refs[n_in + n_out:]

        def f(gr, gh, gp):
            full = list(rv)
            for g, val in zip(grad_rows, gr):
                full[g] = val
            return tuple(fn(full, gh, gp, cv))

        _, vjp = jax.vjp(f, [rv[g] for g in grad_rows], hv, pv)
        d_rows, d_halos, d_params = vjp(tuple(dv))

        @pl.when(i == 0)
        def _():
            for c_ref in carry_refs:
                c_ref[...] = jnp.zeros_like(c_ref)
            for p_ref in out_refs[len(grad_rows):]:
                p_ref[...] = jnp.zeros_like(p_ref)

        for q, g in enumerate(grad_rows):
            val = d_rows[q]
            for e in ev[q]:
                val = val + e
            out_refs[q][...] = val.astype(out_refs[q].dtype)
            if g in halos:
                hq = list(halos).index(g)
                out_refs[q][tile - 8:tile, :] += carry_refs[hq][...]
                carry_refs[hq][...] = d_halos[hq]
        for p_ref, dp in zip(out_refs[len(grad_rows):], d_params):
            p_ref[...] += dp

    res = pl.pallas_call(
        body, name=name, grid=(n,), in_specs=in_specs, out_specs=out_specs, out_shape=out_shapes,
        scratch_shapes=scratch, compiler_params=_cparams(("arbitrary",)),
    )(*rows, *[rows[h] for h in halos], *params, *consts, *dflat, *eflat)
    return list(res[:len(grad_rows)]), list(res[len(grad_rows):])


def _fn_norm(rows, halos, params, consts):
    return (_rms(rows[0], params[0]),)


def _fn_mla_prep(rows, halos, params, consts):
    cq, ckv, kpe = _lane_split(rows[0], (QL, KVL, MLA_PAD - QL - KVL))
    return _rms(cq, params[0]), _rms(ckv, params[1]), kpe


def _rope(x, cos, sin, rot):
    return x * cos + _exact_r(x, rot, False) * sin


def _fn_qk_post(rows, halos, params, consts):
    q_raw, kn_pad, kpe, cos, sin = rows
    q_norm, k_norm = params
    place, rot = consts
    kpe96 = _exact_r(kpe, place, False)
    qs = [_rope(_rms(qh, q_norm), cos, sin, rot) for qh in _unstack(q_raw)]
    ks = [_rope(_rms(kh + kpe96, k_norm), cos, sin, rot) for kh in _unstack(kn_pad)]
    return jnp.stack(qs, axis=0), jnp.stack(ks, axis=0)


def _seg(x, bd):
    return _exact_r(x, bd, False)


def _make_fn_rwkv_prep(vres):
    def fn(rows, halos, params, consts):
        cols = rows[0]
        bd = consts[0]
        mu, w0, w2, a0, a2, g2, k_k, k_a = params[:8]
        prev = _shift(cols, halos[0], 1)
        c = cols + (prev - cols) * mu
        r, k, v, xw, xa, xg = _lane_split(c, (RW, RW, RW, DL, AL, GL))
        log_w = -_softplus(-(w0 + _mm(jnp.tanh(xw), w2))) - 0.5
        ld = -jnp.exp(log_w)
        a = _sigmoid(a0 + _mm(xa, a2))
        g = _mm(_sigmoid(xg), g2)
        if vres:
            hcur, v_first = rows[1], rows[2]
            v1, v_mu, v0, v2 = params[8:12]
            xv = _mm(hcur, v1)
            xv_prev = _shift(xv, _mm(halos[1], v1), 1)
            xv = xv + (xv_prev - xv) * v_mu
            v = v + (v_first - v) * _sigmoid(v0 + _mm(xv, v2))
        kk = k * k_k
        kk = kk / jnp.maximum(jnp.sqrt(_seg(kk * kk, bd)), 1e-12)
        k2 = k * (1.0 + (a - 1.0) * k_a)
        return r, ld, k2, v, -kk, kk * a, g
    return fn


def _fn_rwkv_post(rows, halos, params, consts):
    y, r, k2, v, g = rows
    ln_w, ln_b, r_k = params
    bd = consts[0]
    mean = _seg(y, bd) * (1.0 / RN)
    d = y - mean
    var = _seg(d * d, bd) * (1.0 / RN)
    yn = d * lax.rsqrt(var + GN_EPS) * ln_w + ln_b
    bonus = _seg(r * k2 * r_k, bd) * v
    return ((yn + bonus) * g,)


def _fn_conv(rows, halos, params, consts):
    cols, halo = rows[0], halos[0]
    w0, w1, w2 = params
    b, c, x = _lane_split(cols, (CW, CW, CW))
    _, ch, xh = _lane_split(halo, (CW, CW, CW))
    u, uh = c * x, ch * xh
    return (b * (w0 * _shift(u, uh, 2) + w1 * _shift(u, uh, 1) + w2 * u),)


def _fn_merge(rows, halos, params, consts):
    gate, o_a, o_b, o_c = rows
    g_a, g_b, g_c = _lane_split(gate, (D, D, D))
    return (_sigmoid(g_a) * o_a + _sigmoid(g_b) * o_b + _sigmoid(g_c) * o_c,)


def _attn_block(q, k, v, q0):
    tq, s_len = q.shape[0], k.shape[0]
    s = _mm_nt(q, k) * (DQK ** -0.5)
    row = q0 + lax.broadcasted_iota(jnp.int32, (tq, s_len), 0)
    col = lax.broadcasted_iota(jnp.int32, (tq, s_len), 1)
    s = jnp.where(row >= col, s, -1e30)
    m = lax.stop_gradient(jnp.max(s, axis=-1, keepdims=True))
    e = jnp.exp(s - m)
    p = e / jnp.sum(e, axis=-1, keepdims=True)
    return _mm(p, v)


def _attn_segments(s_len, tq):
    per = max(1, s_len // tq // ATTN_SEGMENTS)
    return [(first, per, (first + per) * tq) for first in range(0, s_len // tq, per)]


def attn_fwd(q, k, v, *, tq, name):
    h, s_len, _ = q.shape
    outs = []
    for seg, (first, nq, kend) in enumerate(_attn_segments(s_len, tq)):
        def body(q_ref, k_ref, v_ref, o_ref, first=first):
            q0 = (first + pl.program_id(1)) * tq
            o_ref[0] = _attn_block(q_ref[0], k_ref[0], v_ref[0], q0).astype(o_ref.dtype)

        outs.append(pl.pallas_call(
            body, name=f"{name}_{seg}", grid=(h, nq),
            in_specs=[pl.BlockSpec((1, tq, DQK), lambda hh, i, first=first: (hh, first + i, 0)),
                      pl.BlockSpec((1, kend, DQK), lambda hh, i: (hh, 0, 0)),
                      pl.BlockSpec((1, kend, DV), lambda hh, i: (hh, 0, 0))],
            out_specs=pl.BlockSpec((1, tq, DV), lambda hh, i: (hh, i, 0)),
            out_shape=jax.ShapeDtypeStruct((h, nq * tq, DV), bf16),
            compiler_params=_cparams(("parallel", "arbitrary")),
        )(q, k, v))
    return jnp.concatenate(outs, axis=1)


def attn_bwd(q, k, v, do, *, tq, name):
    h, s_len, _ = q.shape
    dqs, dk_acc, dv_acc = [], None, None
    for seg, (first, nq, kend) in reversed(list(enumerate(_attn_segments(s_len, tq)))):
        carried = dk_acc is not None

        def body(*refs, first=first, carried=carried):
            q_ref, k_ref, v_ref, do_ref = refs[:4]
            dq_ref, dk_ref, dv_ref = refs[-3:]
            i = pl.program_id(1)
            _, vjp = jax.vjp(functools.partial(_attn_block, q0=(first + i) * tq), q_ref[0], k_ref[0], v_ref[0])
            dq, dk, dv = vjp(do_ref[0])
            dq_ref[0] = dq

            @pl.when(i == 0)
            def _():
                dk_ref[0] = dk + refs[4][0] if carried else dk
                dv_ref[0] = dv + refs[5][0] if carried else dv

            @pl.when(i > 0)
            def _():
                dk_ref[0] += dk
                dv_ref[0] += dv

        key_specs = [pl.BlockSpec((1, kend, DQK), lambda hh, i: (hh, 0, 0)),
                     pl.BlockSpec((1, kend, DV), lambda hh, i: (hh, 0, 0))]
        dq, dk_acc, dv_acc = pl.pallas_call(
            body, name=f"{name}_{seg}", grid=(h, nq),
            in_specs=[pl.BlockSpec((1, tq, DQK), lambda hh, i, first=first: (hh, first + i, 0))] + key_specs
            + [pl.BlockSpec((1, tq, DV), lambda hh, i, first=first: (hh, first + i, 0))]
            + (key_specs if carried else []),
            out_specs=[pl.BlockSpec((1, tq, DQK), lambda hh, i: (hh, i, 0))] + key_specs,
            out_shape=[jax.ShapeDtypeStruct((h, nq * tq, DQK), f32), jax.ShapeDtypeStruct((h, s_len, DQK), f32),
                       jax.ShapeDtypeStruct((h, s_len, DV), f32)],
            input_output_aliases={4: 1, 5: 2} if carried else {},
            compiler_params=_cparams(("parallel", "arbitrary")),
        )(q, k, v, do, *([dk_acc, dv_acc] if carried else []))
        dqs.append(dq)
    return jnp.concatenate(dqs[::-1], axis=1), dk_acc, dv_acc


def _wkv_local(r, ld, k, v, a, b):
    nb, c, n = r.shape
    ri = lax.broadcasted_iota(jnp.int32, (c, c), 0)
    ci = lax.broadcasted_iota(jnp.int32, (c, c), 1)
    tri = jnp.broadcast_to((ri >= ci).astype(f32)[None], (nb, c, c))
    cum = _exact_bl(tri, ld, False)
    tot = jnp.sum(ld, axis=1, keepdims=True)
    w_incl, w_excl, w_inv, w_rest = jnp.exp(cum), jnp.exp(cum - ld), jnp.exp(-cum), jnp.exp(tot - cum)
    ab, rb, bb, kb = a * w_excl, r * w_incl, b * w_inv, k * w_inv
    bw, kw = b * w_rest, k * w_rest
    r2 = lax.broadcasted_iota(jnp.int32, (2 * c, 2 * c), 0)
    c2 = lax.broadcasted_iota(jnp.int32, (2 * c, 2 * c), 1)
    t_of, s_of = jnp.where(r2 >= c, r2 - c, r2), jnp.where(c2 >= c, c2 - c, c2)
    keep = jnp.logical_or(t_of > s_of, jnp.logical_and(r2 >= c, t_of == s_of))
    pair = jnp.where(keep[None], _hbnt(jnp.concatenate([ab, rb], axis=1), jnp.concatenate([bb, kb], axis=1)), 0.0)
    on_b, on_k = _lane_split(pair, (c, c))
    l_ab, m_rb = _row_split(on_b, (c, c))
    l_ak_v, m_rk_v = _row_split(_hbnn(on_k, v), (c, c))
    x = jnp.concatenate([ab, l_ak_v], axis=-1)
    lp, span = l_ab, 1
    while span < c:
        x = x + _hbnn(lp, x)
        span *= 2
        if span < c:
            lp = _hbnn(lp, lp)
    via_b_r, via_b_y = _lane_split(_hbnn(m_rb, x), (n, n))
    r_hat = rb + via_b_r
    y0 = via_b_y + m_rk_v
    from_b_g, from_b_z = _row_split(_hbtn(x, bw), (n, n))
    eye = lax.broadcasted_iota(jnp.int32, (n, n), 0) == lax.broadcasted_iota(jnp.int32, (n, n), 1)
    g = jnp.where(eye[None], jnp.exp(tot), 0.0) + from_b_g
    z = from_b_z + _hbtn(v, kw)
    return r_hat, y0, g, z


def _head(h):
    return slice(RN * h, RN * (h + 1))


def _load_chunk_heads(ref, c, per):
    return jnp.stack([ref[c * q:c * (q + 1), _head(h)] for q in range(per) for h in range(RH)], axis=0)


def _store_chunk_heads(ref, val, c, per):
    for q in range(per):
        ref[c * q:c * (q + 1), :] = jnp.concatenate([val[q * RH + h] for h in range(RH)], axis=-1)


def wkv_fwd(r, ld, k, v, a, b, *, name):
    s_len = r.shape[0]
    c, per = WKV_CHUNK, WKV_CHUNKS_PER_STEP
    n = s_len // c
    rows = pl.BlockSpec((c * per, RW), lambda i: (i, 0))
    mats = pl.BlockSpec((per, RH, RN, RN), lambda i: (i, 0, 0, 0))
    rows_t, mats_t = jax.ShapeDtypeStruct((s_len, RW), f32), jax.ShapeDtypeStruct((n, RH, RN, RN), f32)

    def local_body(r_ref, ld_ref, k_ref, v_ref, a_ref, b_ref, rh_ref, y0_ref, g_ref, z_ref):
        r_hat, y0, g, z = _wkv_local(*[_load_chunk_heads(ref, c, per)
                                       for ref in (r_ref, ld_ref, k_ref, v_ref, a_ref, b_ref)])
        _store_chunk_heads(rh_ref, r_hat, c, per)
        _store_chunk_heads(y0_ref, y0, c, per)
        g_ref[...] = g.reshape(per, RH, RN, RN)
        z_ref[...] = z.reshape(per, RH, RN, RN)

    r_hat, y0, g, z = pl.pallas_call(
        local_body, name=name + "_local", grid=(n // per,), in_specs=[rows] * 6, out_specs=[rows, rows, mats, mats],
        out_shape=[rows_t, rows_t, mats_t, mats_t], compiler_params=_cparams(("parallel",)),
    )(r, ld, k, v, a, b)

    def scan_body(g_ref, z_ref, st_ref, s_sc):
        s_sc[...] = jnp.zeros_like(s_sc)

        @pl.loop(0, n)
        def _(i):
            s0 = s_sc[...]
            st_ref[i] = s0
            s_sc[...] = _hbnn(s0, g_ref[i]) + z_ref[i]

    vm = pl.BlockSpec(memory_space=pltpu.VMEM)
    states = pl.pallas_call(
        scan_body, name=name + "_scan", in_specs=[vm, vm], out_specs=vm, out_shape=mats_t,
        scratch_shapes=[pltpu.VMEM((RH, RN, RN), f32)],
        compiler_params=pltpu.CompilerParams(vmem_limit_bytes=VMEM_LIMIT),
    )(g, z)

    def out_body(rh_ref, y0_ref, st_ref, y_ref):
        y = _hbnt(_load_chunk_heads(rh_ref, c, per), st_ref[...].reshape(per * RH, RN, RN))
        _store_chunk_heads(y_ref, y, c, per)
        y_ref[...] += y0_ref[...]

    y = pl.pallas_call(
        out_body, name=name + "_out", grid=(n // per,), in_specs=[rows, rows, mats], out_specs=rows,
        out_shape=rows_t, compiler_params=_cparams(("parallel",)),
    )(r_hat, y0, states)
    return y, dict(r_hat=r_hat, g=g, states=states)


def wkv_bwd(r, ld, k, v, a, b, saved, dy, *, name):
    s_len = r.shape[0]
    c, per = WKV_CHUNK, WKV_CHUNKS_PER_STEP
    n = s_len // c
    rows = pl.BlockSpec((c * per, RW), lambda i: (i, 0))
    mats = pl.BlockSpec((per, RH, RN, RN), lambda i: (i, 0, 0, 0))
    rows_t, mats_t = jax.ShapeDtypeStruct((s_len, RW), f32), jax.ShapeDtypeStruct((n, RH, RN, RN), f32)

    def out_body(dy_ref, rh_ref, st_ref, drh_ref, dsy_ref):
        dyb = _load_chunk_heads(dy_ref, c, per)
        _store_chunk_heads(drh_ref, _hbnn(dyb, st_ref[...].reshape(per * RH, RN, RN)), c, per)
        dsy_ref[...] = _hbtn(dyb, _load_chunk_heads(rh_ref, c, per)).reshape(per, RH, RN, RN)

    d_rhat, ds_y = pl.pallas_call(
        out_body, name=name + "_out", grid=(n // per,), in_specs=[rows, rows, mats], out_specs=[rows, mats],
        out_shape=[rows_t, mats_t], compiler_params=_cparams(("parallel",)),
    )(dy, saved["r_hat"], saved["states"])

    def scan_body(dsy_ref, g_ref, st_ref, dg_ref, dz_ref, ds_sc):
        ds_sc[...] = jnp.zeros_like(ds_sc)

        @pl.loop(0, n)
        def _(i):
            cidx = n - 1 - i
            ds_next = ds_sc[...]
            dz_ref[cidx] = ds_next
            dg_ref[cidx] = _hbtn(st_ref[cidx], ds_next)
            ds_sc[...] = dsy_ref[cidx] + _hbnt(ds_next, g_ref[cidx])

    vm = pl.BlockSpec(memory_space=pltpu.VMEM)
    d_g, d_z = pl.pallas_call(
        scan_body, name=name + "_scan", in_specs=[vm, vm, vm], out_specs=[vm, vm], out_shape=[mats_t, mats_t],
        scratch_shapes=[pltpu.VMEM((RH, RN, RN), f32)],
        compiler_params=pltpu.CompilerParams(vmem_limit_bytes=VMEM_LIMIT),
    )(ds_y, saved["g"], saved["states"])

    def local_body(r_ref, ld_ref, k_ref, v_ref, a_ref, b_ref, drh_ref, dy_ref, dg_ref, dz_ref, *out_refs):
        _, vjp = jax.vjp(_wkv_local, *[_load_chunk_heads(ref, c, per)
                                       for ref in (r_ref, ld_ref, k_ref, v_ref, a_ref, b_ref)])
        grads = vjp((_load_chunk_heads(drh_ref, c, per), _load_chunk_heads(dy_ref, c, per),
                     dg_ref[...].reshape(per * RH, RN, RN), dz_ref[...].reshape(per * RH, RN, RN)))
        for o_ref, val in zip(out_refs, grads):
            _store_chunk_heads(o_ref, val, c, per)

    return pl.pallas_call(
        local_body, name=name + "_local", grid=(n // per,), in_specs=[rows] * 8 + [mats, mats], out_specs=[rows] * 6,
        out_shape=[rows_t] * 6, compiler_params=_cparams(("parallel",)),
    )(r, ld, k, v, a, b, d_rhat, dy, d_g, d_z)


def loss_head(y, target, *, tile, name):
    s_len, d = y.shape
    n = s_len // tile

    def body(y_ref, t_ref, dy_ref, l_ref):
        err = y_ref[...] - t_ref[...]
        dy_ref[...] = err * (1.0 / d)
        part = 0.5 * jnp.sum(jnp.mean(err * err, axis=-1, keepdims=True), axis=0, keepdims=True)

        @pl.when(pl.program_id(0) == 0)
        def _():
            l_ref[...] = jnp.zeros_like(l_ref)

        l_ref[...] += jnp.broadcast_to(part, l_ref.shape)

    bs = pl.BlockSpec((tile, d), lambda i: (i, 0))
    dy, l = pl.pallas_call(
        body, name=name, grid=(n,), in_specs=[bs, bs],
        out_specs=[bs, pl.BlockSpec((8, 128), lambda i: (0, 0))],
        out_shape=[jax.ShapeDtypeStruct((s_len, d), f32), jax.ShapeDtypeStruct((8, 128), f32)],
        compiler_params=_cparams(("arbitrary",)),
    )(y, target)
    return l[0, 0], dy


def adamw(w, g, m, v, *, name):
    gs = g if isinstance(g, (list, tuple)) else [g]
    rows, cols = w.shape
    tile = _pick(rows, max(8, (2 * 1024 * 1024 // (4 * cols)) // 8 * 8), 8)
    c1 = 1.0 - B1 ** STEP
    c2 = 1.0 - B2 ** STEP
    ng = len(gs)

    def body(*refs):
        w_ref, m_ref, v_ref = refs[0], refs[1 + ng], refs[2 + ng]
        g_out, d_out, m_out, v_out = refs[3 + ng:]
        gv = refs[1][...]
        for q in range(1, ng):
            gv = gv + refs[1 + q][...]
        mn = B1 * m_ref[...] + (1.0 - B1) * gv
        vn = B2 * v_ref[...] + (1.0 - B2) * (gv * gv)
        d_out[...] = -LR * ((mn / c1) / (jnp.sqrt(vn / c2) + EPS) + WD * w_ref[...])
        g_out[...] = gv
        m_out[...] = mn
        v_out[...] = vn

    bs = pl.BlockSpec((tile, cols), lambda i: (i, 0))
    return pl.pallas_call(
        body, name=name, grid=(rows // tile,), in_specs=[bs] * (3 + ng), out_specs=[bs] * 4,
        out_shape=[jax.ShapeDtypeStruct((rows, cols), f32)] * 4, compiler_params=_cparams(("parallel",)),
    )(w, *gs, m, v)


def _place():
    return lax.axis_index("x"), lax.axis_index("y"), lax.axis_index("c")


_ANY = pl.BlockSpec(memory_space=pl.ANY)


def _peer_chips(x, y):
    return [(1 - x, y), (x, 1 - y), (1 - x, 1 - y)]


def gather_weights(shards, *, name):
    nk = len(shards)

    def body(*refs):
        srcs, outs = refs[:nk], refs[nk:2 * nk]
        ici_send, ici_recv, d2d_send, d2d_recv = refs[2 * nk + 1:]
        x, y, c = _place()
        me = 2 * x + y
        peers = _peer_chips(x, y)
        pending = []
        for k in range(nk):
            half = srcs[k].shape[0] // 2
            mine = pl.ds(c * half, half)
            for p, (px, py) in enumerate(peers):
                cp = pltpu.make_async_remote_copy(
                    src_ref=srcs[k].at[mine], dst_ref=outs[k].at[me, mine], send_sem=ici_send.at[k, p],
                    recv_sem=ici_recv.at[k, p], device_id=(px, py, c), device_id_type=MESH)
                cp.start()
                pending.append(cp)
        for k in range(nk):
            half = srcs[k].shape[0] // 2
            mine = pl.ds(c * half, half)
            for p, (px, py) in enumerate(peers):
                landed = outs[k].at[2 * px + py, mine]
                pltpu.make_async_remote_copy(
                    src_ref=srcs[k].at[mine], dst_ref=landed, send_sem=ici_send.at[k, p], recv_sem=ici_recv.at[k, p],
                    device_id=(px, py, c), device_id_type=MESH).wait_recv()
                fwd = pltpu.make_async_remote_copy(
                    src_ref=landed, dst_ref=landed, send_sem=d2d_send.at[k, p], recv_sem=d2d_recv.at[k, p],
                    device_id=(x, y, 1 - c), device_id_type=MESH)
                fwd.start()
                pending.append(fwd)
        for k in range(nk):
            half = srcs[k].shape[0] // 2
            other = pl.ds((1 - c) * half, half)
            for p, (px, py) in enumerate(peers):
                theirs = outs[k].at[2 * px + py, other]
                pltpu.make_async_remote_copy(
                    src_ref=theirs, dst_ref=theirs, send_sem=d2d_send.at[k, p], recv_sem=d2d_recv.at[k, p],
                    device_id=(x, y, 1 - c), device_id_type=MESH).wait_recv()
        for cp in pending:
            cp.wait_send()
        refs[2 * nk][...] = jnp.zeros_like(refs[2 * nk])

    sem = lambda *shape: pltpu.SemaphoreType.DMA(shape)
    res = pl.pallas_call(
        body, name=name, in_specs=[_ANY] * nk, out_specs=[_ANY] * nk + [pl.BlockSpec(memory_space=pltpu.VMEM)],
        out_shape=[jax.ShapeDtypeStruct((4,) + s.shape, s.dtype) for s in shards]
        + [jax.ShapeDtypeStruct((8, 128), f32)],
        scratch_shapes=[sem(nk, 3), sem(nk, 3), sem(nk, 3), sem(nk, 3)],
    )(*shards)
    return list(res[:nk]), res[nk]


_HBM = pl.BlockSpec(memory_space=pltpu.HBM)
_SEM = pl.BlockSpec(memory_space=pltpu.SEMAPHORE)


def _ici_half_copies(srcs, lands, send_sems, recv_sems, incoming):
    x, y, c = _place()
    me = 2 * x + y
    out = []
    for k in range(len(srcs)):
        half = srcs[k].shape[0] // 2
        mine = pl.ds(c * half, half)
        for p, (px, py) in enumerate(_peer_chips(x, y)):
            out.append(pltpu.make_async_remote_copy(
                src_ref=srcs[k].at[mine], dst_ref=lands[k].at[(2 * px + py) if incoming else me, mine],
                send_sem=send_sems.at[3 * k + p], recv_sem=recv_sems.at[3 * k + p], device_id=(px, py, c),
                device_id_type=MESH))
    return out


def gather_start(shards, *, name):
    nk = len(shards)

    def body(*refs):
        srcs, lands = refs[:nk], refs[nk:2 * nk]
        send_sems, recv_sems = refs[2 * nk], refs[2 * nk + 1]
        token = refs[-1]
        for outgoing in _ici_half_copies(srcs, lands, send_sems, recv_sems, incoming=False):
            outgoing.start()
        token[...] = jnp.zeros_like(token)

    lands = [pltpu.with_memory_space_constraint(lax.empty((4,) + s.shape, s.dtype), pltpu.HBM) for s in shards]
    res = pl.pallas_call(
        body, name=name,
        out_shape=(pltpu.SemaphoreType.DMA((3 * nk,)), pltpu.SemaphoreType.DMA((3 * nk,)),
                   *[pltpu.HBM(s.shape, s.dtype) for s in shards], *[pltpu.HBM(z.shape, z.dtype) for z in lands],
                   jax.ShapeDtypeStruct((8, 128), f32)),
        in_specs=[_HBM] * (2 * nk), out_specs=(_SEM, _SEM, *[_HBM] * (2 * nk), pl.BlockSpec(memory_space=pltpu.VMEM)),
        input_output_aliases={k: 2 + k for k in range(2 * nk)},
        compiler_params=pltpu.CompilerParams(has_side_effects=pltpu.SideEffectType.DATAFLOW_SIDE_EFFECTING),
    )(*[pltpu.with_memory_space_constraint(s, pltpu.HBM) for s in shards], *lands)
    return res[0], res[1], list(res[2:2 + nk]), list(res[2 + nk:2 + 2 * nk]), res[-1]


def gather_wait(send_sems, recv_sems, shards, lands, after, *, name):
    nk = len(shards)

    def body(*refs):
        srcs, zones = refs[:nk], refs[nk:2 * nk]
        for outgoing in _ici_half_copies(srcs, zones, refs[2 * nk], refs[2 * nk + 1], incoming=False):
            outgoing.wait_send()
        for landed in _ici_half_copies(srcs, zones, refs[2 * nk], refs[2 * nk + 1], incoming=True):
            landed.wait_recv()

    res = pl.pallas_call(
        body, name=name,
        out_shape=(*[pltpu.HBM(s.shape, s.dtype) for s in shards], *[pltpu.HBM(z.shape, z.dtype) for z in lands]),
        in_specs=[_HBM] * (2 * nk) + [_SEM, _SEM, _ANY], out_specs=tuple([_HBM] * (2 * nk)),
        input_output_aliases={k: k for k in range(2 * nk)},
        compiler_params=pltpu.CompilerParams(has_side_effects=pltpu.SideEffectType.DATAFLOW_SIDE_EFFECTING),
    )(*shards, *lands, send_sems, recv_sems, after)
    return list(res[:nk]), list(res[nk:])


def gather_forward(lands, *, name):
    nk = len(lands)

    def body(*refs):
        zones = refs[nk:2 * nk]
        send_sems, recv_sems = refs[2 * nk:]
        x, y, c = _place()
        sends = []
        for k in range(nk):
            half = zones[k].shape[1] // 2
            for p, (px, py) in enumerate(_peer_chips(x, y)):
                landed = zones[k].at[2 * px + py, pl.ds(c * half, half)]
                cp = pltpu.make_async_remote_copy(
                    src_ref=landed, dst_ref=landed, send_sem=send_sems.at[k, p], recv_sem=recv_sems.at[k, p],
                    device_id=(x, y, 1 - c), device_id_type=MESH)
                cp.start()
                sends.append(cp)
        for k in range(nk):
            half = zones[k].shape[1] // 2
            for p, (px, py) in enumerate(_peer_chips(x, y)):
                theirs = zones[k].at[2 * px + py, pl.ds((1 - c) * half, half)]
                pltpu.make_async_remote_copy(
                    src_ref=theirs, dst_ref=theirs, send_sem=send_sems.at[k, p], recv_sem=recv_sems.at[k, p],
                    device_id=(x, y, 1 - c), device_id_type=MESH).wait_recv()
        for cp in sends:
            cp.wait_send()

    return pl.pallas_call(
        body, name=name, in_specs=[_ANY] * nk, out_specs=[_ANY] * nk,
        out_shape=[jax.ShapeDtypeStruct(z.shape, z.dtype) for z in lands],
        input_output_aliases={k: k for k in range(nk)},
        scratch_shapes=[pltpu.SemaphoreType.DMA((nk, 3)), pltpu.SemaphoreType.DMA((nk, 3))],
    )(*lands)


def grads_to_sibling(parts, *, name):
    nk = len(parts)

    def body(*refs):
        srcs, outs = refs[:nk], refs[nk:2 * nk]
        send_sems, recv_sems = refs[2 * nk:]
        x, y, c = _place()
        sends = []
        for k in range(nk):
            half = srcs[k].shape[1] // 2
            cp = pltpu.make_async_remote_copy(
                src_ref=srcs[k].at[:, pl.ds((1 - c) * half, half), :], dst_ref=outs[k], send_sem=send_sems.at[k],
                recv_sem=recv_sems.at[k], device_id=(x, y, 1 - c), device_id_type=MESH)
            cp.start()
            sends.append(cp)
        for cp in sends:
            cp.wait_recv()
        for cp in sends:
            cp.wait_send()

    return pl.pallas_call(
        body, name=name, in_specs=[_ANY] * nk, out_specs=[_ANY] * nk,
        out_shape=[jax.ShapeDtypeStruct((4, p.shape[1] // 2, p.shape[2]), p.dtype) for p in parts],
        scratch_shapes=[pltpu.SemaphoreType.DMA((nk,)), pltpu.SemaphoreType.DMA((nk,))],
    )(*parts)


def pair_sum(part, theirs, core, *, name):
    _, rows, cols = part.shape
    half = rows // 2
    tile = _pick(half, max(16, (1 << 20) // (4 * cols) // 16 * 16), 16)
    per = half // tile

    def body(c_ref, p_ref, t_ref, o_ref):
        o_ref[...] = (p_ref[...].astype(f32) + t_ref[...].astype(f32)).astype(bf16)

    grid_spec = pltpu.PrefetchScalarGridSpec(
        num_scalar_prefetch=1, grid=(4, per),
        in_specs=[pl.BlockSpec((1, tile, cols), lambda j, i, c_ref: (j, c_ref[0] * per + i, 0)),
                  pl.BlockSpec((1, tile, cols), lambda j, i, c_ref: (j, i, 0))],
        out_specs=pl.BlockSpec((1, tile, cols), lambda j, i, c_ref: (j, i, 0)))
    return pl.pallas_call(
        body, name=name, grid_spec=grid_spec, out_shape=jax.ShapeDtypeStruct((4, half, cols), bf16),
        compiler_params=_cparams(("parallel", "parallel")),
    )(core, part, theirs)


def _all_to_all_copies(srcs, lands, send_sems, recv_sems, incoming):
    x, y, c = _place()
    me = 2 * x + y
    out = []
    for k in range(len(srcs)):
        for p, (px, py) in enumerate(_peer_chips(x, y)):
            peer = 2 * px + py
            out.append(pltpu.make_async_remote_copy(
                src_ref=srcs[k].at[peer], dst_ref=lands[k].at[peer if incoming else me],
                send_sem=send_sems.at[3 * k + p], recv_sem=recv_sems.at[3 * k + p], device_id=(px, py, c),
                device_id_type=MESH))
    return out


def scatter_start(parts, *, name):
    nk = len(parts)

    def body(*refs):
        srcs, lands = refs[:nk], refs[nk:2 * nk]
        for outgoing in _all_to_all_copies(srcs, lands, refs[2 * nk], refs[2 * nk + 1], incoming=False):
            outgoing.start()
        refs[-1][...] = jnp.zeros_like(refs[-1])

    lands = [pltpu.with_memory_space_constraint(lax.empty(p.shape, p.dtype), pltpu.HBM) for p in parts]
    res = pl.pallas_call(
        body, name=name,
        out_shape=(pltpu.SemaphoreType.DMA((3 * nk,)), pltpu.SemaphoreType.DMA((3 * nk,)),
                   *[pltpu.HBM(p.shape, p.dtype) for p in parts], *[pltpu.HBM(p.shape, p.dtype) for p in parts],
                   jax.ShapeDtypeStruct((8, 128), f32)),
        in_specs=[_HBM] * (2 * nk), out_specs=(_SEM, _SEM, *[_HBM] * (2 * nk), pl.BlockSpec(memory_space=pltpu.VMEM)),
        input_output_aliases={k: 2 + k for k in range(2 * nk)},
        compiler_params=pltpu.CompilerParams(has_side_effects=pltpu.SideEffectType.DATAFLOW_SIDE_EFFECTING),
    )(*[pltpu.with_memory_space_constraint(p, pltpu.HBM) for p in parts], *lands)
    return res[0], res[1], list(res[2:2 + nk]), list(res[2 + nk:2 + 2 * nk]), res[-1]


def scatter_wait(send_sems, recv_sems, parts, lands, after, *, name):
    nk = len(parts)

    def body(*refs):
        srcs, zones = refs[:nk], refs[nk:2 * nk]
        for outgoing in _all_to_all_copies(srcs, zones, refs[2 * nk], refs[2 * nk + 1], incoming=False):
            outgoing.wait_send()
        for landed in _all_to_all_copies(srcs, zones, refs[2 * nk], refs[2 * nk + 1], incoming=True):
            landed.wait_recv()

    res = pl.pallas_call(
        body, name=name,
        out_shape=(*[pltpu.HBM(p.shape, p.dtype) for p in parts], *[pltpu.HBM(z.shape, z.dtype) for z in lands]),
        in_specs=[_HBM] * (2 * nk) + [_SEM, _SEM, _ANY], out_specs=tuple([_HBM] * (2 * nk)),
        input_output_aliases={k: k for k in range(2 * nk)},
        compiler_params=pltpu.CompilerParams(has_side_effects=pltpu.SideEffectType.DATAFLOW_SIDE_EFFECTING),
    )(*parts, *lands, send_sems, recv_sems, after)
    return list(res[:nk]), list(res[nk:])


def join_halves(bufs, layout, *, name):
    nk, nb = len(layout), len(bufs)

    def body(*refs):
        outs = refs[nb:2 * nb]
        send_sems, recv_sems = refs[2 * nb:]
        x, y, c = _place()
        pending = []
        for k, (o, off, rows) in enumerate(layout):
            half = rows // 2
            mine = outs[o].at[pl.ds(off + c * half, half), :]
            cp = pltpu.make_async_remote_copy(
                src_ref=mine, dst_ref=mine, send_sem=send_sems.at[k], recv_sem=recv_sems.at[k],
                device_id=(x, y, 1 - c), device_id_type=MESH)
            cp.start()
            pending.append(cp)
        for k, (o, off, rows) in enumerate(layout):
            half = rows // 2
            theirs = outs[o].at[pl.ds(off + (1 - c) * half, half), :]
            pltpu.make_async_remote_copy(
                src_ref=theirs, dst_ref=theirs, send_sem=send_sems.at[k], recv_sem=recv_sems.at[k],
                device_id=(x, y, 1 - c), device_id_type=MESH).wait_recv()
        for cp in pending:
            cp.wait_send()

    return pl.pallas_call(
        body, name=name, in_specs=[_ANY] * nb, out_specs=[_ANY] * nb,
        out_shape=[jax.ShapeDtypeStruct(b.shape, b.dtype) for b in bufs],
        input_output_aliases={o: o for o in range(nb)},
        scratch_shapes=[pltpu.SemaphoreType.DMA((nk,)), pltpu.SemaphoreType.DMA((nk,))],
    )(*bufs)


def place_slab(dest, src, index, *, name):
    rows, cols = src.shape
    tile = _pick(rows, max(16, (1 << 20) // (src.dtype.itemsize * cols) // 16 * 16), 16)

    def body(i_ref, s_ref, d_ref, o_ref):
        del i_ref, d_ref
        o_ref[0] = s_ref[...]

    grid_spec = pltpu.PrefetchScalarGridSpec(
        num_scalar_prefetch=1, grid=(rows // tile,),
        in_specs=[pl.BlockSpec((tile, cols), lambda i, idx: (i, 0)), _ANY],
        out_specs=pl.BlockSpec((1, tile, cols), lambda i, idx: (idx[0], i, 0)))
    return pl.pallas_call(
        body, name=name, grid_spec=grid_spec, out_shape=jax.ShapeDtypeStruct(dest.shape, dest.dtype),
        input_output_aliases={2: 0}, compiler_params=_cparams(("parallel",)),
    )(index, src, dest)


def all_reduce_small(src, *, name):
    rows, cols = src.shape

    def body(s_ref, o_ref, buf, send_sems, recv_sems):
        x, y, c = _place()
        me = 4 * x + 2 * y + c
        buf[me] = s_ref[...]
        sends = []
        for msk in range(1, 8):
            px = x ^ (msk >> 2)
            py = y ^ ((msk >> 1) & 1)
            pc = c ^ (msk & 1)
            cp = pltpu.make_async_remote_copy(
                src_ref=s_ref, dst_ref=buf.at[me], send_sem=send_sems.at[msk - 1], recv_sem=recv_sems.at[msk - 1],
                device_id=(px, py, pc), device_id_type=MESH)
            cp.start()
            sends.append(cp)
        for msk in range(1, 8):
            px = x ^ (msk >> 2)
            py = y ^ ((msk >> 1) & 1)
            pc = c ^ (msk & 1)
            pltpu.make_async_remote_copy(
                src_ref=s_ref, dst_ref=buf.at[4 * px + 2 * py + pc], send_sem=send_sems.at[msk - 1],
                recv_sem=recv_sems.at[msk - 1], device_id=(px, py, pc), device_id_type=MESH).wait_recv()
        for cp in sends:
            cp.wait_send()
        acc = buf[0]
        for d in range(1, 8):
            acc = acc + buf[d]
        o_ref[...] = acc

    vm = pl.BlockSpec(memory_space=pltpu.VMEM)
    return pl.pallas_call(
        body, name=name, in_specs=[vm], out_specs=vm, out_shape=jax.ShapeDtypeStruct((rows, cols), f32),
        scratch_shapes=[pltpu.VMEM((8, rows, cols), f32), pltpu.SemaphoreType.DMA((7,)),
                        pltpu.SemaphoreType.DMA((7,))],
        compiler_params=pltpu.CompilerParams(vmem_limit_bytes=VMEM_LIMIT),
    )(src)


def sum4_into(arrived, own, dest, where, *, layer, total_rows, name):
    _, rows, cols = arrived.shape
    tile = _pick(rows, max(16, (1 << 20) // (4 * cols) // 16 * 16), 16)
    per = rows // tile

    def body(w_ref, a_ref, own_ref, *rest):
        mine = own_ref[0].astype(f32)
        p = [jnp.where(w_ref[1] == j, mine, a_ref[j].astype(f32)) for j in range(4)]
        rest[-1][...] = ((p[0] + p[1]) + p[2]) + p[3]

    grid_spec = pltpu.PrefetchScalarGridSpec(
        num_scalar_prefetch=1, grid=(per,),
        in_specs=[pl.BlockSpec((4, tile, cols), lambda i, w_ref: (0, i, 0)),
                  pl.BlockSpec((1, tile, cols), lambda i, w_ref: (w_ref[1], i, 0))] + ([] if dest is None else [_ANY]),
        out_specs=pl.BlockSpec((tile, cols), lambda i, w_ref: ((2 * layer + w_ref[0]) * per + i, 0)))
    return pl.pallas_call(
        body, name=name, grid_spec=grid_spec, out_shape=jax.ShapeDtypeStruct((total_rows, cols), f32),
        input_output_aliases={} if dest is None else {3: 0}, compiler_params=_cparams(("parallel",)),
    )(where, arrived, own, *([] if dest is None else [dest]))


def _consts():
    idx = np.arange(RW)
    bd = (idx[:, None] // RN == idx[None, :] // RN).astype(np.float32)
    place = np.zeros((128, DQK), np.float32)
    place[np.arange(ROPE), NOPE + np.arange(ROPE)] = 1.0
    rot = np.zeros((DQK, DQK), np.float32)
    half = ROPE // 2
    rot[NOPE + half + np.arange(half), NOPE + np.arange(half)] = -1.0
    rot[NOPE + np.arange(half), NOPE + half + np.arange(half)] = 1.0
    return jnp.asarray(bd), jnp.asarray(place), jnp.asarray(rot)


def _rope_tables(positions):
    freqs = ROPE_THETA ** (-(jnp.arange(ROPE // 2, dtype=f32) * 2.0 / ROPE))
    ang = positions.astype(f32)[:, None] * freqs
    cos, sin = jnp.cos(ang), jnp.sin(ang)
    ones = jnp.ones((positions.shape[0], NOPE), f32)
    return (jnp.concatenate([ones, cos, cos], axis=-1), jnp.concatenate([0.0 * ones, sin, sin], axis=-1))


STAGES = (("w_in",), ("mla_wq_b", "mla_wkv_b", "mla_w_o", "rwkv_w_o", "conv_w_o", "w_out"), ("w_up", "w_down"))


def derive_stage(stage, w):
    if stage == 0:
        w_in = w["w_in"]
        pad = jnp.zeros((D, MLA_PAD - MLA_COLS), w_in.dtype)
        return dict(gate=w_in[:, :GATE], mla=jnp.concatenate([w_in[:, GATE:GATE + MLA_COLS], pad], axis=1),
                    rw=w_in[:, GATE + MLA_COLS:GATE + MLA_COLS + 4 * RW], cv=w_in[:, GATE + MLA_COLS + 4 * RW:])
    if stage == 1:
        wkv = w["mla_wkv_b"].reshape(KVL, MLA_H, NOPE + DV)
        wk = jnp.concatenate([wkv[:, :, :NOPE], jnp.zeros((KVL, MLA_H, ROPE), wkv.dtype)], axis=-1)
        return dict(wq=w["mla_wq_b"].reshape(QL, MLA_H, DQK).transpose(1, 0, 2), wk=wk.transpose(1, 0, 2),
                    wv=wkv[:, :, NOPE:].transpose(1, 0, 2), wo=w["mla_w_o"].reshape(MLA_H, DV, D),
                    rwo=w["rwkv_w_o"], cvo=w["conv_w_o"], out=w["w_out"])
    return dict(up=w["w_up"], down=w["w_down"])


W_IN_WINDOW_TILE = (0, 10, 21, 31)
W_IN_WINDOW = 1664
W_IN_SHARD = 1384


def w_in_window_cols(win, chip):
    gap = MLA_PAD - MLA_COLS
    branches = []
    for j in range(4):
        lo, hi = W_IN_SHARD * j, W_IN_SHARD * (j + 1)
        base = 128 * W_IN_WINDOW_TILE[j]
        cut = GATE + MLA_COLS
        if hi <= cut:
            branches.append(lambda w, a=lo - base: w[:, a:a + W_IN_SHARD])
        elif lo >= cut:
            branches.append(lambda w, a=lo + gap - base: w[:, a:a + W_IN_SHARD])
        else:
            branches.append(lambda w, a=lo - base, n1=cut - lo, b=cut + gap - base, n2=hi - cut:
                            jnp.concatenate([w[:, a:a + n1], w[:, b:b + n2]], axis=1))
    return lax.switch(chip, branches, win)


def chip_major_grads(stage, g):
    if stage == 0:
        padded = jnp.concatenate([g["gate"], g["mla"], g["rw"], g["cv"]], axis=1)
        return dict(w_in=jnp.stack([padded[:, 128 * t:128 * t + W_IN_WINDOW] for t in W_IN_WINDOW_TILE]))
    if stage == 1:
        heads = MLA_H // 4
        wq = g["wq"].reshape(4, heads, QL, DQK).transpose(0, 2, 1, 3).reshape(4, QL, heads * DQK)
        wkv = jnp.concatenate([g["wk"][:, :, :NOPE], g["wv"]], axis=-1)
        wkv = wkv.reshape(4, heads, KVL, NOPE + DV).transpose(0, 2, 1, 3).reshape(4, KVL, heads * (NOPE + DV))
        return dict(mla_wq_b=wq, mla_wkv_b=wkv, mla_w_o=g["wo"].reshape(4, MLA_H * DV, D // 4),
                    rwkv_w_o=g["rwo"], conv_w_o=g["cvo"], w_out=g["out"].reshape(4, D // 4, D))
    return dict(w_up=g["up"], w_down=g["down"].reshape(4, DFF // 4, D))


def _row(v):
    return v.reshape(1, -1)


def local_step(x, positions, target, w, sm, big_of=None, on_grads=None):
    if big_of is None:
        big_of = lambda l, stage, _after: {n: w[n][l] for n in STAGES[stage]}
    if on_grads is None:
        on_grads = lambda l, stage, slabs: None
    s_len = x.shape[0]
    t_row = _pick(s_len, 256, 8)
    t_wide = _pick(s_len, 128, 8)
    bd, place, rot = _consts()
    cos, sin = _rope_tables(positions)
    sds = lambda *shape: jax.ShapeDtypeStruct(shape, f32)
    sdb = lambda *shape: jax.ShapeDtypeStruct(shape, bf16)
    saved = []
    v_first = None
    for l in range(DEPTH):
        tag = f"l{l}_"
        lw = derive_stage(0, big_of(l, 0, x))
        vres = l > 0
        p_norm1 = [_row(sm["attn_norm"][l])]
        (h,) = rows_fwd(_fn_norm, [x], p_norm1, [], [sds(s_len, D)], tile=t_row, name=tag + "norm1")
        gate = mm(h, lw["gate"], name=tag + "proj_gate")
        mla = mm(h, lw["mla"], name=tag + "proj_mla")
        rwc = mm(h, lw["rw"], name=tag + "proj_rwkv")
        cvc = mm(h, lw["cv"], name=tag + "proj_conv")
        lw.update(derive_stage(1, big_of(l, 1, cvc)))
        p_mla = [_row(sm["mla_q_a_norm"][l]), _row(sm["mla_kv_a_norm"][l])]
        qn, kvn, kpe = rows_fwd(_fn_mla_prep, [mla], p_mla, [], [sdb(s_len, QL), sdb(s_len, KVL), sds(s_len, 128)],
                                tile=t_row, name=tag + "mla_prep")
        q_raw = mm(qn, lw["wq"], b_batched=True, name=tag + "q_proj")
        kn_pad = mm(kvn, lw["wk"], b_batched=True, name=tag + "k_proj")
        vv = mm(kvn, lw["wv"], b_batched=True, name=tag + "v_proj")
        p_qk = [_row(sm["mla_q_norm"][l]), _row(sm["mla_k_norm"][l])]
        q, k = rows_fwd(_fn_qk_post, [q_raw, kn_pad, kpe, cos, sin], p_qk, [place, rot],
                        [sds(MLA_H, s_len, DQK), sds(MLA_H, s_len, DQK)], tile=t_wide, name=tag + "qk_post")
        o = attn_fwd(q, k, vv, tq=_pick(s_len, 256, 8), name=tag + "attn")
        o_a = mm(o, lw["wo"], a_batched=True, b_batched=True, reduce_batch=True, name=tag + "o_a")
        p_rw = [_row(sm["rwkv_mu"][l]), _row(sm["rwkv_w0"][l]), w["rwkv_w2"][l], _row(sm["rwkv_a0"][l]),
                w["rwkv_a2"][l], w["rwkv_g2"][l], _row(sm["rwkv_k_k"][l]), _row(sm["rwkv_k_a"][l])]
        rw_rows, rw_halos = [rwc], (0,)
        if vres:
            p_rw += [w["rwkv_v1"][l - 1], _row(sm["rwkv_v_mu"][l - 1]), _row(sm["rwkv_v0"][l - 1]), w["rwkv_v2"][l - 1]]
            rw_rows, rw_halos = [rwc, h, v_first], (0, 1)
        fn_prep = _make_fn_rwkv_prep(vres)
        r, ld, k2, v, an, bn, g = rows_fwd(fn_prep, rw_rows, p_rw, [bd], [sds(s_len, RW)] * 7, tile=t_row,
                                           name=tag + "rwkv_prep", halos=rw_halos)
        if not vres:
            v_first = v
        y, states = wkv_fwd(r, ld, k2, v, an, bn, name=tag + "wkv")
        p_post = [_row(sm["rwkv_ln_w"][l]), _row(sm["rwkv_ln_b"][l]), _row(sm["rwkv_r_k"][l])]
        (yb,) = rows_fwd(_fn_rwkv_post, [y, r, k2, v, g], p_post, [bd], [sdb(s_len, RW)], tile=t_row,
                         name=tag + "rwkv_post")
        o_b = mm(yb, lw["rwo"], name=tag + "o_b")
        p_cv = [w["conv_w"][l][q:q + 1] for q in range(3)]
        (yc,) = rows_fwd(_fn_conv, [cvc], p_cv, [], [sdb(s_len, CW)], tile=t_row, name=tag + "conv", halos=(0,))
        o_c = mm(yc, lw["cvo"], name=tag + "o_c")
        (merged,) = rows_fwd(_fn_merge, [gate, o_a, o_b, o_c], [], [], [sdb(s_len, D)], tile=t_wide,
                             name=tag + "merge")
        x1 = mm(merged, lw["out"], add=x, name=tag + "out_proj")
        lw.update(derive_stage(2, big_of(l, 2, x1)))
        p_norm2 = [_row(sm["mlp_norm"][l])]
        (h2,) = rows_fwd(_fn_norm, [x1], p_norm2, [], [sdb(s_len, D)], tile=t_row, name=tag + "norm2")
        up, act = mm(h2, lw["up"], relu2_out=True, name=tag + "up")
        x2 = mm(act, lw["down"], add=x1, name=tag + "down")
        saved.append(dict(lw=lw, x=x, h=h, gate=gate, mla=mla, rwc=rwc, cvc=cvc, qn=qn, kvn=kvn, kpe=kpe,
                          q_raw=q_raw, kn_pad=kn_pad, vv=vv, q=q, k=k, o=o, o_a=o_a, r=r, ld=ld, k2=k2, v=v,
                          an=an, bn=bn, g=g, y=y, states=states, yb=yb, o_b=o_b, yc=yc, o_c=o_c, merged=merged,
                          x1=x1, h2=h2, up=up, act=act, p_norm1=p_norm1, p_mla=p_mla, p_qk=p_qk, p_rw=p_rw,
                          p_post=p_post, p_cv=p_cv, p_norm2=p_norm2, rw_rows=rw_rows, rw_halos=rw_halos,
                          fn_prep=fn_prep, v_first=v_first if vres else None))
        x = x2

    loss, dx = loss_head(x, target, tile=t_row, name="loss_head")

    grads = {n: [None] * (DEPTH - 1 if n in ("rwkv_v1", "rwkv_v_mu", "rwkv_v0", "rwkv_v2") else DEPTH)
             for n in WEIGHTS}
    dv_first = None
    for l in reversed(range(DEPTH)):
        tag = f"b{l}_"
        sv = saved[l]
        lw = sv["lw"]
        vres = l > 0
        g_down = mm(sv["act"], dx, ta=True, out_dtype=bf16, name=tag + "g_down")
        dup = mm(dx, lw["down"], tb=True, act_grad=sv["up"], out_dtype=bf16, name=tag + "d_up")
        g_up = mm(sv["h2"], dup, ta=True, n_split=4, out_dtype=bf16, name=tag + "g_up")
        dh2 = mm(dup, lw["up"], tb=True, name=tag + "d_h2")
        slabs = chip_major_grads(2, dict(up=g_up, down=g_down))
        token = on_grads(l, 2, slabs)
        if token is not None:
            dh2 = dh2 + token[0, 0]
        (dx1,), (g_n2,) = rows_bwd(_fn_norm, [sv["x1"]], sv["p_norm2"], [], [[dh2]], tile=t_row,
                                   name=tag + "norm2", extra={0: [dx]})
        g_out = mm(sv["merged"], dx1, ta=True, out_dtype=bf16, name=tag + "g_out")
        dmerged = mm(dx1, lw["out"], tb=True, name=tag + "d_merged")
        (dgate, do_a, do_b, do_c), _ = rows_bwd(_fn_merge, [sv["gate"], sv["o_a"], sv["o_b"], sv["o_c"]], [], [],
                                                [[dmerged]], tile=t_wide, name=tag + "merge",
                                                grad_dtypes=[bf16] * 4)
        g_cvo = mm(sv["yc"], do_c, ta=True, n_split=4, out_dtype=bf16, name=tag + "g_cvo")
        dyc = mm(do_c, lw["cvo"], tb=True, name=tag + "d_yc")
        (dcvc,), g_cw = rows_bwd(_fn_conv, [sv["cvc"]], sv["p_cv"], [], [[dyc]], tile=t_row, name=tag + "conv",
                                    halos=(0,))
        g_rwo = mm(sv["yb"], do_b, ta=True, n_split=4, out_dtype=bf16, name=tag + "g_rwo")
        dyb = mm(do_b, lw["rwo"], tb=True, name=tag + "d_yb")
        (dy, dr_p, dk_p, dv_p, dg), g_post = rows_bwd(
            _fn_rwkv_post, [sv["y"], sv["r"], sv["k2"], sv["v"], sv["g"]], sv["p_post"], [bd], [[dyb]], tile=t_row,
            name=tag + "rwkv_post")
        dr_s, dld, dk_s, dv_s, dan, dbn = wkv_bwd(sv["r"], sv["ld"], sv["k2"], sv["v"], sv["an"], sv["bn"],
                                                  sv["states"], dy, name=tag + "wkv")
        dv_list = [dv_s, dv_p] + ([dv_first] if (not vres and dv_first is not None) else [])
        d_prep, g_prep = rows_bwd(
            sv["fn_prep"], sv["rw_rows"], sv["p_rw"], [bd],
            [[dr_s, dr_p], [dld], [dk_s, dk_p], dv_list, [dan], [dbn], [dg]], tile=t_row, name=tag + "rwkv_prep",
            halos=sv["rw_halos"])
        drwc = d_prep[0]
        dh_extra = []
        if vres:
            dh_extra = [d_prep[1]]
            dv_first = d_prep[2]
        g_wo = mm(sv["o"], do_a, ta=True, a_batched=True, n_split=4, tk=s_len, out_dtype=bf16, name=tag + "g_wo")
        do = mm(do_a, lw["wo"], tb=True, b_batched=True, name=tag + "d_o")
        dq, dk, dvv = attn_bwd(sv["q"], sv["k"], sv["vv"], do, tq=_pick(s_len, 256, 8), name=tag + "attn")
        (dq_raw, dkn_pad, dkpe), g_qk = rows_bwd(
            _fn_qk_post, [sv["q_raw"], sv["kn_pad"], sv["kpe"], cos, sin], sv["p_qk"], [place, rot], [[dq], [dk]],
            tile=t_wide, name=tag + "qk_post", grad_rows=[0, 1, 2], grad_dtypes=[bf16, bf16, f32])
        g_wq = mm(sv["qn"], dq_raw, ta=True, b_batched=True, tk=s_len, out_dtype=bf16, name=tag + "g_wq")
        g_wk = mm(sv["kvn"], dkn_pad, ta=True, b_batched=True, tk=s_len, out_dtype=bf16, name=tag + "g_wk")
        g_wv = mm(sv["kvn"], dvv, ta=True, b_batched=True, tk=s_len, out_dtype=bf16, name=tag + "g_wv")
        dqn = mm(dq_raw, lw["wq"], tb=True, a_batched=True, b_batched=True, reduce_batch=True, name=tag + "d_qn")
        dkvn = mm(dkn_pad, lw["wk"], tb=True, a_batched=True, b_batched=True, reduce_batch=True, name=tag + "d_kvn_k")
        dkvn = mm(dvv, lw["wv"], tb=True, a_batched=True, b_batched=True, reduce_batch=True, add=dkvn,
                  name=tag + "d_kvn_v")
        slabs.update(chip_major_grads(1, dict(wq=g_wq, wk=g_wk, wv=g_wv, wo=g_wo, rwo=g_rwo, cvo=g_cvo, out=g_out)))
        token = on_grads(l, 1, {n: slabs[n] for n in STAGES[1]})
        if token is not None:
            dqn = dqn + token[0, 0]
        (dmla,), g_mla = rows_bwd(_fn_mla_prep, [sv["mla"]], sv["p_mla"], [], [[dqn], [dkvn], [dkpe]], tile=t_row,
                                  name=tag + "mla_prep", grad_dtypes=[bf16])
        g_gate = mm(sv["h"], dgate, ta=True, out_dtype=bf16, name=tag + "g_gate")
        g_mlaw = mm(sv["h"], dmla, ta=True, out_dtype=bf16, name=tag + "g_mla")
        g_rw = mm(sv["h"], drwc, ta=True, out_dtype=bf16, name=tag + "g_rw")
        g_cv = mm(sv["h"], dcvc, ta=True, out_dtype=bf16, name=tag + "g_cv")
        dh = mm(dgate, lw["gate"], tb=True, name=tag + "d_h_gate")
        dh = mm(dmla, lw["mla"], tb=True, add=dh, name=tag + "d_h_mla")
        dh = mm(drwc, lw["rw"], tb=True, add=dh, name=tag + "d_h_rw")
        dh = mm(dcvc, lw["cv"], tb=True, add=dh, name=tag + "d_h_cv")
        (dx,), (g_n1,) = rows_bwd(_fn_norm, [sv["x"]], sv["p_norm1"], [], [[dh] + dh_extra], tile=t_row,
                                  name=tag + "norm1", extra={0: [dx1]})
        slabs.update(chip_major_grads(0, dict(gate=g_gate, mla=g_mlaw, rw=g_rw, cv=g_cv)))
        token = on_grads(l, 0, {n: slabs[n] for n in STAGES[0]})
        if token is not None:
            dx = dx + token[0, 0]
        for n, val in slabs.items():
            grads[n][l] = val
        grads["attn_norm"][l], grads["mlp_norm"][l] = g_n1, g_n2
        grads["mla_q_a_norm"][l], grads["mla_kv_a_norm"][l] = g_mla
        grads["mla_q_norm"][l], grads["mla_k_norm"][l] = g_qk
        grads["rwkv_ln_w"][l], grads["rwkv_ln_b"][l], grads["rwkv_r_k"][l] = g_post
        for n, val in zip(["rwkv_mu", "rwkv_w0", "rwkv_w2", "rwkv_a0", "rwkv_a2", "rwkv_g2", "rwkv_k_k", "rwkv_k_a"],
                          g_prep[:8]):
            grads[n][l] = val
        if vres:
            for n, val in zip(["rwkv_v1", "rwkv_v_mu", "rwkv_v0", "rwkv_v2"], g_prep[8:12]):
                grads[n][l - 1] = val
        grads["conv_w"][l] = jnp.concatenate(g_cw, axis=0)
    return loss, dx, grads


def _split3(a):
    hi = a.astype(bf16)
    r1 = a - hi.astype(f32)
    mid = r1.astype(bf16)
    lo = (r1 - mid.astype(f32)).astype(bf16)
    return hi, mid, lo


def _shard_axis(name):
    return 1 if name in ROW_SHARDED else 2


def _pack(pieces, width, dtype, row_align):
    flat = jnp.concatenate([p.reshape(-1).astype(dtype) for p in pieces])
    rows = -(-flat.shape[0] // width)
    rows = -(-rows // row_align) * row_align
    return jnp.pad(flat, (0, rows * width - flat.shape[0])).reshape(rows, width)


def _unpack(flat2d, shapes):
    flat = flat2d.reshape(-1)
    out, off = [], 0
    for shp in shapes:
        n = int(np.prod(shp))
        out.append(flat[off:off + n].reshape(shp))
        off += n
    return out


def kernel(x, positions, attn_norm, w_in, mla_q_a_norm, mla_wq_b, mla_kv_a_norm, mla_wkv_b, mla_q_norm, mla_k_norm, mla_w_o, rwkv_mu, rwkv_w0, rwkv_w2, rwkv_a0, rwkv_a2, rwkv_g2, rwkv_k_k, rwkv_k_a, rwkv_r_k, rwkv_ln_w, rwkv_ln_b, rwkv_w_o, rwkv_v1, rwkv_v_mu, rwkv_v0, rwkv_v2, conv_w, conv_w_o, w_out, mlp_norm, w_up, w_down, loss_target, m_attn_norm, m_w_in, m_mla_q_a_norm, m_mla_wq_b, m_mla_kv_a_norm, m_mla_wkv_b, m_mla_q_norm, m_mla_k_norm, m_mla_w_o, m_rwkv_mu, m_rwkv_w0, m_rwkv_w2, m_rwkv_a0, m_rwkv_a2, m_rwkv_g2, m_rwkv_k_k, m_rwkv_k_a, m_rwkv_r_k, m_rwkv_ln_w, m_rwkv_ln_b, m_rwkv_w_o, m_rwkv_v1, m_rwkv_v_mu, m_rwkv_v0, m_rwkv_v2, m_conv_w, m_conv_w_o, m_w_out, m_mlp_norm, m_w_up, m_w_down, v_attn_norm, v_w_in, v_mla_q_a_norm, v_mla_wq_b, v_mla_kv_a_norm, v_mla_wkv_b, v_mla_q_norm, v_mla_k_norm, v_mla_w_o, v_rwkv_mu, v_rwkv_w0, v_rwkv_w2, v_rwkv_a0, v_rwkv_a2, v_rwkv_g2, v_rwkv_k_k, v_rwkv_k_a, v_rwkv_r_k, v_rwkv_ln_w, v_rwkv_ln_b, v_rwkv_w_o, v_rwkv_v1, v_rwkv_v_mu, v_rwkv_v0, v_rwkv_v2, v_conv_w, v_conv_w_o, v_w_out, v_mlp_norm, v_w_up, v_w_down):
    args = dict(locals())
    wts = {n: args[n] for n in WEIGHTS}
    mom = {n: args["m_" + n] for n in WEIGHTS}
    var = {n: args["v_" + n] for n in WEIGHTS}
    chip = 2 * lax.axis_index("x") + lax.axis_index("y")
    core = lax.axis_index("c").astype(jnp.int32).reshape(1)

    med_names = [n for n in MED if n != "conv_w"]
    med_pieces = [wts[n] for n in med_names] + list(_split3(wts["conv_w"]))
    med_shapes = [p.shape for p in med_pieces]
    chip_idx = chip.astype(jnp.int32).reshape(1)
    shards_first = [wts[n][0].astype(bf16) for n in STAGES[0]] + [_pack(med_pieces, 128, bf16, 32)]
    got_first, token = gather_weights(shards_first, name="gather_l0_s0")
    got_first = [place_slab(g, s, chip_idx, name=f"place_own_l0_s0_{q}")
                 for q, (g, s) in enumerate(zip(got_first, shards_first))]
    in_flight = {}
    for key, names, l in (("l0_s1", STAGES[1], 0), ("l0_s2", STAGES[2], 0), ("l1", BIG, 1)):
        group = [wts[n][l].astype(bf16) for n in names]
        group[0] = group[0] + token[0, 0].astype(bf16)
        in_flight[key] = (names, gather_start(group, name="gather_start_" + key))
        token = in_flight[key][1][4]

    def whole_of(names, slabs):
        out = {}
        for n, by_chip in zip(names, slabs):
            _, rows, cols = by_chip.shape
            if n in ROW_SHARDED:
                out[n] = by_chip.reshape(4 * rows, cols)
            else:
                out[n] = by_chip.transpose(1, 0, 2).reshape(rows, 4 * cols)
        return out

    landed = {}

    def big_of(l, stage, after):
        if (l, stage) == (0, 0):
            return whole_of(STAGES[0], got_first)
        key = "l1" if l == 1 else f"l0_s{stage}"
        if key not in landed:
            names, (send_sems, recv_sems, thru, lands, _) = in_flight[key]
            thru, lands = gather_wait(send_sems, recv_sems, thru, lands, after, name="gather_wait_" + key)
            lands = gather_forward(lands, name="gather_forward_" + key)
            landed[key] = whole_of(names, [place_slab(g, s, chip_idx, name=f"place_own_{key}_{q}")
                                           for q, (g, s) in enumerate(zip(lands, thru))])
        return {n: landed[key][n] for n in STAGES[stage]}

    whole = {}
    per_chip = [_unpack(got_first[len(STAGES[0])][j], med_shapes) for j in range(4)]
    for q, n in enumerate(med_names):
        whole[n] = jnp.concatenate([per_chip[j][q] for j in range(4)], axis=_shard_axis(n)).astype(f32)
    base = len(med_names)
    cw_parts = [jnp.concatenate([per_chip[j][base + t] for j in range(4)], axis=2).astype(f32) for t in range(3)]
    whole["conv_w"] = (cw_parts[0] + cw_parts[1]) + cw_parts[2]
    small = {n: wts[n] for n in SMALL}
    small["rwkv_r_k"] = wts["rwkv_r_k"].reshape(DEPTH, RW)

    exchanges = []

    def on_grads(l, stage, slabs):
        names = STAGES[stage]
        tag = f"l{l}_s{stage}"
        parts = [slabs[n] for n in names]
        from_sibling = grads_to_sibling(parts, name="grads_to_sibling_" + tag)
        chip_sums = [pair_sum(s, t, core, name=f"pair_sum_{n}_{l}") for n, s, t in zip(names, parts, from_sibling)]
        started = scatter_start(chip_sums, name="scatter_start_" + tag)
        exchanges.append((l, names, tag, started))
        return started[4]

    x0 = x[0] + token[0, 0]
    loss, grad_x, grads = local_step(x0, positions[0], loss_target[0], whole, small, big_of, on_grads)
    loss = lax.psum(loss, ("x", "y", "c"))

    sm_names = SMALL + MED
    sm_grads = [jnp.stack(grads[n]) for n in sm_names]
    sm_shapes = [g.shape for g in sm_grads]
    sm_sum = _unpack(all_reduce_small(_pack(sm_grads, 128, f32, 8), name="reduce_small"), sm_shapes)
    gsum = {}
    for n, g in zip(sm_names, sm_sum):
        if n in MED:
            ax = _shard_axis(n)
            width = wts[n].shape[ax]
            g = lax.dynamic_slice_in_dim(g, chip * width, width, axis=ax)
        gsum[n] = g.reshape(wts[n].shape)
    where = jnp.stack([lax.axis_index("c"), chip]).astype(jnp.int32)
    bufs, layout = {}, []
    for l, names, tag, (send_sems, recv_sems, thru, lands, _) in exchanges:
        own, arrived = scatter_wait(send_sems, recv_sems, thru, lands, grad_x, name="scatter_wait_" + tag)
        for n, mine, theirs in zip(names, own, arrived):
            rows = 2 * theirs.shape[1]
            bufs[n] = sum4_into(theirs, mine, bufs.get(n), where, layer=l, total_rows=DEPTH * rows,
                                name=f"sum_chips_{n}_{l}")
            layout.append((BIG.index(n), l * rows, rows))
    reduced = join_halves([bufs[n] for n in BIG], layout, name="join_halves")

    out_g, out_d, out_m, out_v = {}, {}, {}, {}
    for q, n in enumerate(BIG):
        shp = wts[n].shape
        as2d = lambda a: a.reshape(-1, shp[-1])
        g2d = w_in_window_cols(reduced[q], chip) if n == "w_in" else reduced[q]
        res = adamw(as2d(wts[n]), g2d, as2d(mom[n]), as2d(var[n]), name="adamw_" + n)
        out_g[n], out_d[n], out_m[n], out_v[n] = [r.reshape(shp) for r in res]
    sm_all = SMALL + MED
    sm_shapes2 = [wts[n].shape for n in sm_all]
    res = adamw(_pack([wts[n] for n in sm_all], 128, f32, 8), _pack([gsum[n] for n in sm_all], 128, f32, 8),
                _pack([mom[n] for n in sm_all], 128, f32, 8), _pack([var[n] for n in sm_all], 128, f32, 8),
                name="adamw_small")
    for tgt, flat in zip((out_g, out_d, out_m, out_v), res):
        for n, val in zip(sm_all, _unpack(flat, sm_shapes2)):
            tgt[n] = val
    return (loss, grad_x[None], *[out_g[n] for n in WEIGHTS], *[out_d[n] for n in WEIGHTS],
            *[out_m[n] for n in WEIGHTS], *[out_v[n] for n in WEIGHTS])
```

```python
import functools

import jax
import jax.numpy as jnp
import numpy as np
from jax import lax
from jax.experimental import pallas as pl
from jax.experimental.pallas import tpu as pltpu

f32, bf16 = jnp.float32, jnp.bfloat16
HI = lax.Precision.HIGHEST
MESH = pl.DeviceIdType.MESH

D = 1024
DEPTH = 2
MLA_H, NOPE, ROPE, DQK, DV = 8, 64, 32, 96, 64
QL, KVL = 384, 256
RW, RH, RN = 256, 4, 64
DL, AL, GL, MVL = 64, 64, 128, 32
CW = 256
DFF = 4096
GATE = 3 * D
MLA_COLS = QL + KVL + ROPE
MLA_PAD = 768
NORM_EPS = 1e-6
GN_EPS = 64e-5
ROPE_THETA = 10000.0
LR, B1, B2, EPS, WD, STEP = 0.001, 0.9, 0.999, 1e-08, 0.01, 10

VMEM_LIMIT = 52 * 1024 * 1024
WKV_CHUNK = 64
WKV_CHUNKS_PER_STEP = 4
ATTN_SEGMENTS = 4

BIG = ["w_in", "mla_wq_b", "mla_wkv_b", "mla_w_o", "rwkv_w_o", "conv_w_o", "w_out", "w_up", "w_down"]
MED = ["rwkv_w2", "rwkv_a2", "rwkv_g2", "rwkv_v1", "rwkv_v2", "conv_w"]
ROW_SHARDED = {"w_out", "w_down", "rwkv_v1"}
SMALL = ["attn_norm", "mla_q_a_norm", "mla_kv_a_norm", "mla_q_norm", "mla_k_norm", "rwkv_mu", "rwkv_w0",
         "rwkv_a0", "rwkv_k_k", "rwkv_k_a", "rwkv_r_k", "rwkv_ln_w", "rwkv_ln_b", "rwkv_v_mu", "rwkv_v0",
         "mlp_norm"]
WEIGHTS = ["attn_norm", "w_in", "mla_q_a_norm", "mla_wq_b", "mla_kv_a_norm", "mla_wkv_b", "mla_q_norm",
           "mla_k_norm", "mla_w_o", "rwkv_mu", "rwkv_w0", "rwkv_w2", "rwkv_a0", "rwkv_a2", "rwkv_g2",
           "rwkv_k_k", "rwkv_k_a", "rwkv_r_k", "rwkv_ln_w", "rwkv_ln_b", "rwkv_w_o", "rwkv_v1", "rwkv_v_mu",
           "rwkv_v0", "rwkv_v2", "conv_w", "conv_w_o", "w_out", "mlp_norm", "w_up", "w_down"]


def _cparams(sem=None):
    return pltpu.CompilerParams(dimension_semantics=sem, vmem_limit_bytes=VMEM_LIMIT)


def _pick(dim, pref, align):
    if dim <= pref:
        return dim
    t = (pref // align) * align
    while t >= align:
        if dim % t == 0:
            return t
        t -= align
    return dim


def _bdot(a, b, dims):
    return lax.dot_general(a.astype(bf16), b.astype(bf16), (dims, ((), ())), preferred_element_type=f32)


@jax.custom_vjp
def _mm(a, b):
    return _bdot(a, b, ((1,), (0,)))


def _mm_fwd(a, b):
    return _mm(a, b), (a, b)


def _mm_bwd(res, g):
    a, b = res
    return _bdot(g, b, ((1,), (1,))), _bdot(a, g, ((0,), (0,)))


_mm.defvjp(_mm_fwd, _mm_bwd)


@jax.custom_vjp
def _mm_nt(a, b):
    return _bdot(a, b, ((1,), (1,)))


def _mm_nt_fwd(a, b):
    return _mm_nt(a, b), (a, b)


def _mm_nt_bwd(res, g):
    a, b = res
    return _bdot(g, b, ((1,), (0,))), _bdot(g, a, ((0,), (0,)))


_mm_nt.defvjp(_mm_nt_fwd, _mm_nt_bwd)


_NN, _NT, _TN = ((1,), (0,)), ((1,), (1,)), ((0,), (0,))


def _dg(a, b, dims):
    return lax.dot_general(a, b, (dims, ((), ())), preferred_element_type=f32)


def _bf16_pieces(x, count):
    out, rest = [], x
    for q in range(count):
        piece = rest.astype(bf16)
        out.append(piece)
        if q + 1 < count:
            rest = rest - piece.astype(f32)
    return out


def _dot3(a, b, dims):
    (ah, al), (bh, bl) = _bf16_pieces(a, 2), _bf16_pieces(b, 2)
    return _dg(ah, bh, dims) + (_dg(ah, bl, dims) + _dg(al, bh, dims))


@jax.custom_vjp
def _hdot(a, b):
    return _dot3(a, b, _NN)


@jax.custom_vjp
def _hdot_nt(a, b):
    return _dot3(a, b, _NT)


@jax.custom_vjp
def _hdot_tn(a, b):
    return _dot3(a, b, _TN)


_hdot.defvjp(lambda a, b: (_hdot(a, b), (a, b)), lambda res, g: (_hdot_nt(g, res[1]), _hdot_tn(res[0], g)))
_hdot_nt.defvjp(lambda a, b: (_hdot_nt(a, b), (a, b)), lambda res, g: (_hdot(g, res[1]), _hdot_tn(g, res[0])))
_hdot_tn.defvjp(lambda a, b: (_hdot_tn(a, b), (a, b)), lambda res, g: (_hdot_nt(res[1], g), _hdot(res[0], g)))


_BNN, _BNT, _BTN = ((2,), (1,)), ((2,), (2,)), ((1,), (1,))


def _bdg(a, b, dims):
    return lax.dot_general(a, b, (dims, ((0,), (0,))), preferred_element_type=f32)


def _bdot3(a, b, dims):
    (ah, al), (bh, bl) = _bf16_pieces(a, 2), _bf16_pieces(b, 2)
    return _bdg(ah, bh, dims) + (_bdg(ah, bl, dims) + _bdg(al, bh, dims))


@jax.custom_vjp
def _hbnn(a, b):
    return _bdot3(a, b, _BNN)


@jax.custom_vjp
def _hbnt(a, b):
    return _bdot3(a, b, _BNT)


@jax.custom_vjp
def _hbtn(a, b):
    return _bdot3(a, b, _BTN)


_hbnn.defvjp(lambda a, b: (_hbnn(a, b), (a, b)), lambda res, g: (_hbnt(g, res[1]), _hbtn(res[0], g)))
_hbnt.defvjp(lambda a, b: (_hbnt(a, b), (a, b)), lambda res, g: (_hbnn(g, res[1]), _hbtn(g, res[0])))
_hbtn.defvjp(lambda a, b: (_hbtn(a, b), (a, b)), lambda res, g: (_hbnt(res[1], g), _hbnn(res[0], g)))


@functools.partial(jax.custom_vjp, nondiff_argnums=(2,))
def _exact_bl(m, x, transposed):
    mb = m.astype(bf16)
    hi, mid, lo = _bf16_pieces(x, 3)
    dims = _BTN if transposed else _BNN
    return (_bdg(mb, hi, dims) + _bdg(mb, mid, dims)) + _bdg(mb, lo, dims)


_exact_bl.defvjp(lambda m, x, transposed: (_exact_bl(m, x, transposed), m),
                 lambda transposed, m, g: (jnp.zeros_like(m), _exact_bl(m, g, not transposed)))


@functools.partial(jax.custom_vjp, nondiff_argnums=(2,))
def _exact_l(m, x, transposed):
    mb = m.astype(bf16)
    hi, mid, lo = _bf16_pieces(x, 3)
    dims = _TN if transposed else _NN
    return (_dg(mb, hi, dims) + _dg(mb, mid, dims)) + _dg(mb, lo, dims)


_exact_l.defvjp(lambda m, x, transposed: (_exact_l(m, x, transposed), m),
                lambda transposed, m, g: (jnp.zeros_like(m), _exact_l(m, g, not transposed)))


@functools.partial(jax.custom_vjp, nondiff_argnums=(2,))
def _exact_r(x, m, transposed):
    mb = m.astype(bf16)
    hi, mid, lo = _bf16_pieces(x, 3)
    dims = _NT if transposed else _NN
    return (_dg(hi, mb, dims) + _dg(mid, mb, dims)) + _dg(lo, mb, dims)


_exact_r.defvjp(lambda x, m, transposed: (_exact_r(x, m, transposed), m),
                lambda transposed, m, g: (_exact_r(g, m, not transposed), jnp.zeros_like(m)))


def _rms(x, g, eps=NORM_EPS):
    return x * lax.rsqrt(jnp.mean(x * x, axis=-1, keepdims=True) + eps) * g


def _sigmoid(x):
    return 1.0 / (1.0 + jnp.exp(-x))


def _softplus(x):
    return jnp.maximum(x, 0.0) + jnp.log(1.0 + jnp.exp(-jnp.maximum(x, -x)))


def _lane_split(x, sizes):
    bounds = np.cumsum([0] + list(sizes))

    @jax.custom_vjp
    def split(v):
        return tuple(v[..., int(bounds[q]):int(bounds[q + 1])] for q in range(len(sizes)))

    split.defvjp(lambda v: (split(v), None), lambda _, g: (jnp.concatenate(g, axis=-1),))
    return split(x)


def _row_split(x, sizes):
    bounds = np.cumsum([0] + list(sizes))

    @jax.custom_vjp
    def split(v):
        return tuple(v[..., int(bounds[q]):int(bounds[q + 1]), :] for q in range(len(sizes)))

    split.defvjp(lambda v: (split(v), None), lambda _, g: (jnp.concatenate(g, axis=-2),))
    return split(x)


def _unstack(x):
    @jax.custom_vjp
    def unstack(v):
        return tuple(v[q] for q in range(v.shape[0]))

    unstack.defvjp(lambda v: (unstack(v), None), lambda _, g: (jnp.stack(g, axis=0),))
    return unstack(x)


def _shift_mats(t, k):
    r = lax.broadcasted_iota(jnp.int32, (t, t), 0)
    c = lax.broadcasted_iota(jnp.int32, (t, t), 1)
    inner = (r - c == k).astype(f32)
    r8 = lax.broadcasted_iota(jnp.int32, (t, 8), 0)
    c8 = lax.broadcasted_iota(jnp.int32, (t, 8), 1)
    edge = (c8 - r8 == 8 - k).astype(f32)
    return inner, edge


def _shift(x, halo, k):
    inner, edge = _shift_mats(x.shape[0], k)
    return _exact_l(inner, x, False) + jnp.dot(edge, halo, precision=HI, preferred_element_type=f32)


def mm(a, b, *, name, ta=False, tb=False, a_batched=False, b_batched=False, reduce_batch=False, add=None,
       act_grad=None, relu2_out=False, n_split=1, out_dtype=f32, tm=1024, tn=1024, tk=2048):
    ash, bsh = a.shape[-2:], b.shape[-2:]
    (k_, m_) = ash if ta else ash[::-1]
    (k2_, n_) = bsh[::-1] if tb else bsh
    assert k_ == k2_, (a.shape, b.shape, ta, tb)
    hb = a.shape[0] if a_batched else (b.shape[0] if b_batched else 1)
    batched_out = (a_batched or b_batched) and not reduce_batch
    h_out = hb if batched_out else 1
    h_red = hb if reduce_batch else 1
    tm = _pick(m_, tm, 128)
    tn = _pick(n_ // n_split, tn, 128)
    tk = _pick(k_, tk, 128)
    nm, nn, nk = m_ // tm, n_ // tn, k_ // tk

    def a_map(i, j, ho, hr, kk):
        blk = (kk, i) if ta else (i, kk)
        return ((ho if batched_out else hr),) + blk if a_batched else blk

    def b_map(i, j, ho, hr, kk):
        blk = (j, kk) if tb else (kk, j)
        return ((ho if batched_out else hr),) + blk if b_batched else blk

    a_blk = (tk, tm) if ta else (tm, tk)
    b_blk = (tn, tk) if tb else (tk, tn)
    in_specs = [pl.BlockSpec(((1,) + a_blk) if a_batched else a_blk, a_map),
                pl.BlockSpec(((1,) + b_blk) if b_batched else b_blk, b_map)]
    args = [a, b]
    for extra in (add, act_grad):
        if extra is not None:
            in_specs.append(pl.BlockSpec((tm, tn), lambda i, j, ho, hr, kk: (i, j)))
            args.append(extra)
    if n_split > 1:
        per = n_ // n_split // tn
        if batched_out:
            out_spec = pl.BlockSpec((1, 1, tm, tn), lambda i, j, ho, hr, kk: (j // per, ho, i, j % per))
            out_shape = jax.ShapeDtypeStruct((n_split, hb, m_, n_ // n_split), out_dtype)
        else:
            out_spec = pl.BlockSpec((1, tm, tn), lambda i, j, ho, hr, kk: (j // per, i, j % per))
            out_shape = jax.ShapeDtypeStruct((n_split, m_, n_ // n_split), out_dtype)
    elif batched_out:
        out_spec = pl.BlockSpec((1, tm, tn), lambda i, j, ho, hr, kk: (ho, i, j))
        out_shape = jax.ShapeDtypeStruct((hb, m_, n_), out_dtype)
    else:
        out_spec = pl.BlockSpec((tm, tn), lambda i, j, ho, hr, kk: (i, j))
        out_shape = jax.ShapeDtypeStruct((m_, n_), out_dtype)
    lead = (0,) * (int(batched_out) + int(n_split > 1))
    dims = ((0,) if ta else (1,), (1,) if tb else (0,))
    has_add, has_act = add is not None, act_grad is not None

    def body(*refs):
        a_ref, b_ref = refs[0], refs[1]
        pos = 2
        add_ref = act_ref = None
        if has_add:
            add_ref = refs[pos]
            pos += 1
        if has_act:
            act_ref = refs[pos]
            pos += 1
        o_ref, acc_ref = refs[pos], refs[-1]
        hr, kk = pl.program_id(3), pl.program_id(4)
        first = jnp.logical_and(hr == 0, kk == 0)
        last = jnp.logical_and(hr == h_red - 1, kk == nk - 1)
        av = a_ref[0] if a_batched else a_ref[...]
        bv = b_ref[0] if b_batched else b_ref[...]
        p = _bdot(av, bv, dims)
        single = h_red * nk == 1

        if not single:
            @pl.when(first)
            def _():
                acc_ref[...] = p

            @pl.when(jnp.logical_not(first))
            def _():
                acc_ref[...] += p

        @pl.when(last)
        def _():
            r = p if single else acc_ref[...]
            if has_act:
                r = r * (2.0 * jnp.maximum(act_ref[...], 0.0))
            if has_add:
                r = r + add_ref[...]
            if lead:
                o_ref[lead] = r.astype(out_dtype)
            else:
                o_ref[...] = r.astype(out_dtype)
            if relu2_out:
                refs[pos + 1][...] = jnp.square(jnp.maximum(r, 0.0)).astype(bf16)

    if relu2_out:
        assert not lead
        out_spec = [out_spec, out_spec]
        out_shape = [out_shape, jax.ShapeDtypeStruct(out_shape.shape, bf16)]
    return pl.pallas_call(
        body, name=name, grid=(nm, nn, h_out, h_red, nk), in_specs=in_specs, out_specs=out_spec,
        out_shape=out_shape, scratch_shapes=[pltpu.VMEM((tm, tn), f32)],
        compiler_params=_cparams(("parallel", "parallel", "parallel", "arbitrary", "arbitrary")),
    )(*args)


def _row_spec(arr, tile, idx):
    if arr.ndim == 2:
        return pl.BlockSpec((tile, arr.shape[1]), lambda i: (idx(i), 0))
    return pl.BlockSpec((arr.shape[0], tile, arr.shape[2]), lambda i: (0, idx(i), 0))


def _halo_spec(arr, tile, idx):
    per = tile // 8
    return pl.BlockSpec((8, arr.shape[1]), lambda i: (jnp.maximum(idx(i) * per - 1, 0), 0))


def _full_spec(arr):
    nd = arr.ndim
    return pl.BlockSpec(arr.shape, lambda i: (0,) * nd)


def _load_f32(ref):
    val = ref[...]
    return val.astype(f32) if val.dtype == bf16 else val


def rows_fwd(fn, rows, params, consts, out_shapes, *, tile, name, halos=()):
    s_len = rows[0].shape[-2]
    n = s_len // tile
    nr, nh, npar, nc = len(rows), len(halos), len(params), len(consts)
    ident = lambda i: i
    in_specs = ([_row_spec(r, tile, ident) for r in rows] + [_halo_spec(rows[h], tile, ident) for h in halos]
                + [_full_spec(p) for p in params] + [_full_spec(c) for c in consts])
    out_specs = [_row_spec(o, tile, ident) for o in out_shapes]

    def body(*refs):
        i = pl.program_id(0)
        rv = [_load_f32(r) for r in refs[:nr]]
        keep = (i > 0).astype(f32)
        hv = [r[...] * keep for r in refs[nr:nr + nh]]
        pv = [r[...] for r in refs[nr + nh:nr + nh + npar]]
        cv = [r[...] for r in refs[nr + nh + npar:nr + nh + npar + nc]]
        outs = fn(rv, hv, pv, cv)
        for o_ref, o in zip(refs[nr + nh + npar + nc:], outs):
            o_ref[...] = o.astype(o_ref.dtype)

    return pl.pallas_call(
        body, name=name, grid=(n,), in_specs=in_specs, out_specs=out_specs, out_shape=list(out_shapes),
        compiler_params=_cparams(("arbitrary",)),
    )(*rows, *[rows[h] for h in halos], *params, *consts)


def rows_bwd(fn, rows, params, consts, douts, *, tile, name, halos=(), grad_rows=None, extra=None,
             grad_dtypes=None):
    s_len = rows[0].shape[-2]
    n = s_len // tile
    nr, nh, npar, nc = len(rows), len(halos), len(params), len(consts)
    grad_rows = list(range(nr)) if grad_rows is None else list(grad_rows)
    extra = extra or {}
    assert all(h in grad_rows for h in halos)
    rev = lambda i: n - 1 - i
    dflat = [d for ds in douts for d in ds]
    dcount = [len(ds) for ds in douts]
    eflat = [e for g in grad_rows for e in extra.get(g, [])]
    ecount = [len(extra.get(g, [])) for g in grad_rows]
    in_specs = ([_row_spec(r, tile, rev) for r in rows] + [_halo_spec(rows[h], tile, rev) for h in halos]
                + [_full_spec(p) for p in params] + [_full_spec(c) for c in consts]
                + [_row_spec(d, tile, rev) for d in dflat] + [_row_spec(e, tile, rev) for e in eflat])
    grad_dtypes = [f32] * len(grad_rows) if grad_dtypes is None else list(grad_dtypes)
    assert all(grad_dtypes[q] == f32 for q, g in enumerate(grad_rows) if g in halos)
    out_shapes = ([jax.ShapeDtypeStruct(rows[g].shape, dt) for g, dt in zip(grad_rows, grad_dtypes)]
                  + [jax.ShapeDtypeStruct(p.shape, f32) for p in params])
    out_specs = [_row_spec(rows[g], tile, rev) for g in grad_rows] + [_full_spec(p) for p in params]
    scratch = [pltpu.VMEM((8, rows[h].shape[1]), f32) for h in halos]
    n_in = nr + nh + npar + nc + len(dflat) + len(eflat)
    n_out = len(grad_rows) + npar

    def body(*refs):
        i = pl.program_id(0)
        rv = [_load_f32(r) for r in refs[:nr]]
        keep = (i < n - 1).astype(f32)
        hv = [r[...] * keep for r in refs[nr:nr + nh]]
        pv = [r[...] for r in refs[nr + nh:nr + nh + npar]]
        pos = nr + nh + npar
        cv = [r[...] for r in refs[pos:pos + nc]]
        pos += nc
        dv = []
        for cnt in dcount:
            acc = _load_f32(refs[pos])
            for q in range(1, cnt):
                acc = acc + _load_f32(refs[pos + q])
            dv.append(acc)
            pos += cnt
        ev = []
        for cnt in ecount:
            ev.append([_load_f32(refs[pos + q]) for q in range(cnt)])
            pos += cnt
        out_refs = refs[n_in:n_in + n_out]
        carry_refs = refs[n_in + n_out:]

        def f(gr, gh, gp):
            full = list(rv)
            for g, val in zip(grad_rows, gr):
                full[g] = val
            return tuple(fn(full, gh, gp, cv))

        _, vjp = jax.vjp(f, [rv[g] for g in grad_rows], hv, pv)
        d_rows, d_halos, d_params = vjp(tuple(dv))

        @pl.when(i == 0)
        def _():
            for c_ref in carry_refs:
                c_ref[...] = jnp.zeros_like(c_ref)
            for p_ref in out_refs[len(grad_rows):]:
                p_ref[...] = jnp.zeros_like(p_ref)

        for q, g in enumerate(grad_rows):
            val = d_rows[q]
            for e in ev[q]:
                val = val + e
            out_refs[q][...] = val.astype(out_refs[q].dtype)
            if g in halos:
                hq = list(halos).index(g)
                out_refs[q][tile - 8:tile, :] += carry_refs[hq][...]
                carry_refs[hq][...] = d_halos[hq]
        for p_ref, dp in zip(out_refs[len(grad_rows):], d_params):
            p_ref[...] += dp

    res = pl.pallas_call(
        body, name=name, grid=(n,), in_specs=in_specs, out_specs=out_specs, out_shape=out_shapes,
        scratch_shapes=scratch, compiler_params=_cparams(("arbitrary",)),
    )(*rows, *[rows[h] for h in halos], *params, *consts, *dflat, *eflat)
    return list(res[:len(grad_rows)]), list(res[len(grad_rows):])


def _fn_norm(rows, halos, params, consts):
    return (_rms(rows[0], params[0]),)


def _fn_mla_prep(rows, halos, params, consts):
    cq, ckv, kpe = _lane_split(rows[0], (QL, KVL, MLA_PAD - QL - KVL))
    return _rms(cq, params[0]), _rms(ckv, params[1]), kpe


def _rope(x, cos, sin, rot):
    return x * cos + _exact_r(x, rot, False) * sin


def _fn_qk_post(rows, halos, params, consts):
    q_raw, kn_pad, kpe, cos, sin = rows
    q_norm, k_norm = params
    place, rot = consts
    kpe96 = _exact_r(kpe, place, False)
    qs = [_rope(_rms(qh, q_norm), cos, sin, rot) for qh in _unstack(q_raw)]
    ks = [_rope(_rms(kh + kpe96, k_norm), cos, sin, rot) for kh in _unstack(kn_pad)]
    return jnp.stack(qs, axis=0), jnp.stack(ks, axis=0)


def _seg(x, bd):
    return _exact_r(x, bd, False)


def _make_fn_rwkv_prep(vres):
    def fn(rows, halos, params, consts):
        cols = rows[0]
        bd = consts[0]
        mu, w0, w2, a0, a2, g2, k_k, k_a = params[:8]
        prev = _shift(cols, halos[0], 1)
        c = cols + (prev - cols) * mu
        r, k, v, xw, xa, xg = _lane_split(c, (RW, RW, RW, DL, AL, GL))
        log_w = -_softplus(-(w0 + _mm(jnp.tanh(xw), w2))) - 0.5
        ld = -jnp.exp(log_w)
        a = _sigmoid(a0 + _mm(xa, a2))
        g = _mm(_sigmoid(xg), g2)
        if vres:
            hcur, v_first = rows[1], rows[2]
            v1, v_mu, v0, v2 = params[8:12]
            xv = _mm(hcur, v1)
            xv_prev = _shift(xv, _mm(halos[1], v1), 1)
            xv = xv + (xv_prev - xv) * v_mu
            v = v + (v_first - v) * _sigmoid(v0 + _mm(xv, v2))
        kk = k * k_k
        kk = kk / jnp.maximum(jnp.sqrt(_seg(kk * kk, bd)), 1e-12)
        k2 = k * (1.0 + (a - 1.0) * k_a)
        return r, ld, k2, v, -kk, kk * a, g
    return fn


def _fn_rwkv_post(rows, halos, params, consts):
    y, r, k2, v, g = rows
    ln_w, ln_b, r_k = params
    bd = consts[0]
    mean = _seg(y, bd) * (1.0 / RN)
    d = y - mean
    var = _seg(d * d, bd) * (1.0 / RN)
    yn = d * lax.rsqrt(var + GN_EPS) * ln_w + ln_b
    bonus = _seg(r * k2 * r_k, bd) * v
    return ((yn + bonus) * g,)


def _fn_conv(rows, halos, params, consts):
    cols, halo = rows[0], halos[0]
    w0, w1, w2 = params
    b, c, x = _lane_split(cols, (CW, CW, CW))
    _, ch, xh = _lane_split(halo, (CW, CW, CW))
    u, uh = c * x, ch * xh
    return (b * (w0 * _shift(u, uh, 2) + w1 * _shift(u, uh, 1) + w2 * u),)


def _fn_merge(rows, halos, params, consts):
    gate, o_a, o_b, o_c = rows
    g_a, g_b, g_c = _lane_split(gate, (D, D, D))
    return (_sigmoid(g_a) * o_a + _sigmoid(g_b) * o_b + _sigmoid(g_c) * o_c,)


def _attn_block(q, k, v, q0):
    tq, s_len = q.shape[0], k.shape[0]
    s = _mm_nt(q, k) * (DQK ** -0.5)
    row = q0 + lax.broadcasted_iota(jnp.int32, (tq, s_len), 0)
    col = lax.broadcasted_iota(jnp.int32, (tq, s_len), 1)
    s = jnp.where(row >= col, s, -1e30)
    m = lax.stop_gradient(jnp.max(s, axis=-1, keepdims=True))
    e = jnp.exp(s - m)
    p = e / jnp.sum(e, axis=-1, keepdims=True)
    return _mm(p, v)


def _attn_segments(s_len, tq):
    per = max(1, s_len // tq // ATTN_SEGMENTS)
    return [(first, per, (first + per) * tq) for first in range(0, s_len // tq, per)]


def attn_fwd(q, k, v, *, tq, name):
    h, s_len, _ = q.shape
    outs = []
    for seg, (first, nq, kend) in enumerate(_attn_segments(s_len, tq)):
        def body(q_ref, k_ref, v_ref, o_ref, first=first):
            q0 = (first + pl.program_id(1)) * tq
            o_ref[0] = _attn_block(q_ref[0], k_ref[0], v_ref[0], q0).astype(o_ref.dtype)

        outs.append(pl.pallas_call(
            body, name=f"{name}_{seg}", grid=(h, nq),
            in_specs=[pl.BlockSpec((1, tq, DQK), lambda hh, i, first=first: (hh, first + i, 0)),
                      pl.BlockSpec((1, kend, DQK), lambda hh, i: (hh, 0, 0)),
                      pl.BlockSpec((1, kend, DV), lambda hh, i: (hh, 0, 0))],
            out_specs=pl.BlockSpec((1, tq, DV), lambda hh, i: (hh, i, 0)),
            out_shape=jax.ShapeDtypeStruct((h, nq * tq, DV), bf16),
            compiler_params=_cparams(("parallel", "arbitrary")),
        )(q, k, v))
    return jnp.concatenate(outs, axis=1)


def attn_bwd(q, k, v, do, *, tq, name):
    h, s_len, _ = q.shape
    dqs, dk_acc, dv_acc = [], None, None
    for seg, (first, nq, kend) in reversed(list(enumerate(_attn_segments(s_len, tq)))):
        carried = dk_acc is not None

        def body(*refs, first=first, carried=carried):
            q_ref, k_ref, v_ref, do_ref = refs[:4]
            dq_ref, dk_ref, dv_ref = refs[-3:]
            i = pl.program_id(1)
            _, vjp = jax.vjp(functools.partial(_attn_block, q0=(first + i) * tq), q_ref[0], k_ref[0], v_ref[0])
            dq, dk, dv = vjp(do_ref[0])
            dq_ref[0] = dq

            @pl.when(i == 0)
            def _():
                dk_ref[0] = dk + refs[4][0] if carried else dk
                dv_ref[0] = dv + refs[5][0] if carried else dv

            @pl.when(i > 0)
            def _():
                dk_ref[0] += dk
                dv_ref[0] += dv

        key_specs = [pl.BlockSpec((1, kend, DQK), lambda hh, i: (hh, 0, 0)),
                     pl.BlockSpec((1, kend, DV), lambda hh, i: (hh, 0, 0))]
        dq, dk_acc, dv_acc = pl.pallas_call(
            body, name=f"{name}_{seg}", grid=(h, nq),
            in_specs=[pl.BlockSpec((1, tq, DQK), lambda hh, i, first=first: (hh, first + i, 0))] + key_specs
            + [pl.BlockSpec((1, tq, DV), lambda hh, i, first=first: (hh, first + i, 0))]
            + (key_specs if carried else []),
            out_specs=[pl.BlockSpec((1, tq, DQK), lambda hh, i: (hh, i, 0))] + key_specs,
            out_shape=[jax.ShapeDtypeStruct((h, nq * tq, DQK), f32), jax.ShapeDtypeStruct((h, s_len, DQK), f32),
                       jax.ShapeDtypeStruct((h, s_len, DV), f32)],
            input_output_aliases={4: 1, 5: 2} if carried else {},
            compiler_params=_cparams(("parallel", "arbitrary")),
        )(q, k, v, do, *([dk_acc, dv_acc] if carried else []))
        dqs.append(dq)
    return jnp.concatenate(dqs[::-1], axis=1), dk_acc, dv_acc


def _wkv_local(r, ld, k, v, a, b):
    nb, c, n = r.shape
    ri = lax.broadcasted_iota(jnp.int32, (c, c), 0)
    ci = lax.broadcasted_iota(jnp.int32, (c, c), 1)
    tri = jnp.broadcast_to((ri >= ci).astype(f32)[None], (nb, c, c))
    cum = _exact_bl(tri, ld, False)
    tot = jnp.sum(ld, axis=1, keepdims=True)
    w_incl, w_excl, w_inv, w_rest = jnp.exp(cum), jnp.exp(cum - ld), jnp.exp(-cum), jnp.exp(tot - cum)
    ab, rb, bb, kb = a * w_excl, r * w_incl, b * w_inv, k * w_inv
    bw, kw = b * w_rest, k * w_rest
    r2 = lax.broadcasted_iota(jnp.int32, (2 * c, 2 * c), 0)
    c2 = lax.broadcasted_iota(jnp.int32, (2 * c, 2 * c), 1)
    t_of, s_of = jnp.where(r2 >= c, r2 - c, r2), jnp.where(c2 >= c, c2 - c, c2)
    keep = jnp.logical_or(t_of > s_of, jnp.logical_and(r2 >= c, t_of == s_of))
    pair = jnp.where(keep[None], _hbnt(jnp.concatenate([ab, rb], axis=1), jnp.concatenate([bb, kb], axis=1)), 0.0)
    on_b, on_k = _lane_split(pair, (c, c))
    l_ab, m_rb = _row_split(on_b, (c, c))
    l_ak_v, m_rk_v = _row_split(_hbnn(on_k, v), (c, c))
    x = jnp.concatenate([ab, l_ak_v], axis=-1)
    lp, span = l_ab, 1
    while span < c:
        x = x + _hbnn(lp, x)
        span *= 2
        if span < c:
            lp = _hbnn(lp, lp)
    via_b_r, via_b_y = _lane_split(_hbnn(m_rb, x), (n, n))
    r_hat = rb + via_b_r
    y0 = via_b_y + m_rk_v
    from_b_g, from_b_z = _row_split(_hbtn(x, bw), (n, n))
    eye = lax.broadcasted_iota(jnp.int32, (n, n), 0) == lax.broadcasted_iota(jnp.int32, (n, n), 1)
    g = jnp.where(eye[None], jnp.exp(tot), 0.0) + from_b_g
    z = from_b_z + _hbtn(v, kw)
    return r_hat, y0, g, z


def _head(h):
    return slice(RN * h, RN * (h + 1))


def _load_chunk_heads(ref, c, per):
    return jnp.stack([ref[c * q:c * (q + 1), _head(h)] for q in range(per) for h in range(RH)], axis=0)


def _store_chunk_heads(ref, val, c, per):
    for q in range(per):
        ref[c * q:c * (q + 1), :] = jnp.concatenate([val[q * RH + h] for h in range(RH)], axis=-1)


def wkv_fwd(r, ld, k, v, a, b, *, name):
    s_len = r.shape[0]
    c = WKV_CHUNK
    n = s_len // c
    per = min(WKV_CHUNKS_PER_STEP, n)
    rows = pl.BlockSpec((c * per, RW), lambda i: (i, 0))
    mats = pl.BlockSpec((per, RH, RN, RN), lambda i: (i, 0, 0, 0))
    rows_t, mats_t = jax.ShapeDtypeStruct((s_len, RW), f32), jax.ShapeDtypeStruct((n, RH, RN, RN), f32)

    def local_body(r_ref, ld_ref, k_ref, v_ref, a_ref, b_ref, rh_ref, y0_ref, g_ref, z_ref):
        r_hat, y0, g, z = _wkv_local(*[_load_chunk_heads(ref, c, per)
                                       for ref in (r_ref, ld_ref, k_ref, v_ref, a_ref, b_ref)])
        _store_chunk_heads(rh_ref, r_hat, c, per)
        _store_chunk_heads(y0_ref, y0, c, per)
        g_ref[...] = g.reshape(per, RH, RN, RN)
        z_ref[...] = z.reshape(per, RH, RN, RN)

    r_hat, y0, g, z = pl.pallas_call(
        local_body, name=name + "_local", grid=(n // per,), in_specs=[rows] * 6, out_specs=[rows, rows, mats, mats],
        out_shape=[rows_t, rows_t, mats_t, mats_t], compiler_params=_cparams(("parallel",)),
    )(r, ld, k, v, a, b)

    def scan_body(g_ref, z_ref, st_ref, s_sc):
        s_sc[...] = jnp.zeros_like(s_sc)

        @pl.loop(0, n)
        def _(i):
            s0 = s_sc[...]
            st_ref[i] = s0
            s_sc[...] = _hbnn(s0, g_ref[i]) + z_ref[i]

    vm = pl.BlockSpec(memory_space=pltpu.VMEM)
    states = pl.pallas_call(
        scan_body, name=name + "_scan", in_specs=[vm, vm], out_specs=vm, out_shape=mats_t,
        scratch_shapes=[pltpu.VMEM((RH, RN, RN), f32)],
        compiler_params=pltpu.CompilerParams(vmem_limit_bytes=VMEM_LIMIT),
    )(g, z)

    def out_body(rh_ref, y0_ref, st_ref, y_ref):
        y = _hbnt(_load_chunk_heads(rh_ref, c, per), st_ref[...].reshape(per * RH, RN, RN))
        _store_chunk_heads(y_ref, y, c, per)
        y_ref[...] += y0_ref[...]

    y = pl.pallas_call(
        out_body, name=name + "_out", grid=(n // per,), in_specs=[rows, rows, mats], out_specs=rows,
        out_shape=rows_t, compiler_params=_cparams(("parallel",)),
    )(r_hat, y0, states)
    return y, dict(r_hat=r_hat, g=g, states=states)


def wkv_bwd(r, ld, k, v, a, b, saved, dy, *, name):
    s_len = r.shape[0]
    c = WKV_CHUNK
    n = s_len // c
    per = min(WKV_CHUNKS_PER_STEP, n)
    rows = pl.BlockSpec((c * per, RW), lambda i: (i, 0))
    mats = pl.BlockSpec((per, RH, RN, RN), lambda i: (i, 0, 0, 0))
    rows_t, mats_t = jax.ShapeDtypeStruct((s_len, RW), f32), jax.ShapeDtypeStruct((n, RH, RN, RN), f32)

    def out_body(dy_ref, rh_ref, st_ref, drh_ref, dsy_ref):
        dyb = _load_chunk_heads(dy_ref, c, per)
        _store_chunk_heads(drh_ref, _hbnn(dyb, st_ref[...].reshape(per * RH, RN, RN)), c, per)
        dsy_ref[...] = _hbtn(dyb, _load_chunk_heads(rh_ref, c, per)).reshape(per, RH, RN, RN)

    d_rhat, ds_y = pl.pallas_call(
        out_body, name=name + "_out", grid=(n // per,), in_specs=[rows, rows, mats], out_specs=[rows, mats],
        out_shape=[rows_t, mats_t], compiler_params=_cparams(("parallel",)),
    )(dy, saved["r_hat"], saved["states"])

    def scan_body(dsy_ref, g_ref, st_ref, dg_ref, dz_ref, ds_sc):
        ds_sc[...] = jnp.zeros_like(ds_sc)

        @pl.loop(0, n)
        def _(i):
            cidx = n - 1 - i
            ds_next = ds_sc[...]
            dz_ref[cidx] = ds_next
            dg_ref[cidx] = _hbtn(st_ref[cidx], ds_next)
            ds_sc[...] = dsy_ref[cidx] + _hbnt(ds_next, g_ref[cidx])

    vm = pl.BlockSpec(memory_space=pltpu.VMEM)
    d_g, d_z = pl.pallas_call(
        scan_body, name=name + "_scan", in_specs=[vm, vm, vm], out_specs=[vm, vm], out_shape=[mats_t, mats_t],
        scratch_shapes=[pltpu.VMEM((RH, RN, RN), f32)],
        compiler_params=pltpu.CompilerParams(vmem_limit_bytes=VMEM_LIMIT),
    )(ds_y, saved["g"], saved["states"])

    def local_body(r_ref, ld_ref, k_ref, v_ref, a_ref, b_ref, drh_ref, dy_ref, dg_ref, dz_ref, *out_refs):
        _, vjp = jax.vjp(_wkv_local, *[_load_chunk_heads(ref, c, per)
                                       for ref in (r_ref, ld_ref, k_ref, v_ref, a_ref, b_ref)])
        grads = vjp((_load_chunk_heads(drh_ref, c, per), _load_chunk_heads(dy_ref, c, per),
                     dg_ref[...].reshape(per * RH, RN, RN), dz_ref[...].reshape(per * RH, RN, RN)))
        for o_ref, val in zip(out_refs, grads):
            _store_chunk_heads(o_ref, val, c, per)

    return pl.pallas_call(
        local_body, name=name + "_local", grid=(n // per,), in_specs=[rows] * 8 + [mats, mats], out_specs=[rows] * 6,
        out_shape=[rows_t] * 6, compiler_params=_cparams(("parallel",)),
    )(r, ld, k, v, a, b, d_rhat, dy, d_g, d_z)


def loss_head(y, target, *, tile, name):
    s_len, d = y.shape
    n = s_len // tile

    def body(y_ref, t_ref, dy_ref, l_ref):
        err = y_ref[...] - t_ref[...]
        dy_ref[...] = err * (1.0 / d)
        part = 0.5 * jnp.sum(jnp.mean(err * err, axis=-1, keepdims=True), axis=0, keepdims=True)

        @pl.when(pl.program_id(0) == 0)
        def _():
            l_ref[...] = jnp.zeros_like(l_ref)

        l_ref[...] += jnp.broadcast_to(part, l_ref.shape)

    bs = pl.BlockSpec((tile, d), lambda i: (i, 0))
    dy, l = pl.pallas_call(
        body, name=name, grid=(n,), in_specs=[bs, bs],
        out_specs=[bs, pl.BlockSpec((8, 128), lambda i: (0, 0))],
        out_shape=[jax.ShapeDtypeStruct((s_len, d), f32), jax.ShapeDtypeStruct((8, 128), f32)],
        compiler_params=_cparams(("arbitrary",)),
    )(y, target)
    return l[0, 0], dy


def adamw(w, g, m, v, *, name):
    gs = g if isinstance(g, (list, tuple)) else [g]
    rows, cols = w.shape
    tile = _pick(rows, max(8, (2 * 1024 * 1024 // (4 * cols)) // 8 * 8), 8)
    c1 = 1.0 - B1 ** STEP
    c2 = 1.0 - B2 ** STEP
    ng = len(gs)

    def body(*refs):
        w_ref, m_ref, v_ref = refs[0], refs[1 + ng], refs[2 + ng]
        g_out, d_out, m_out, v_out = refs[3 + ng:]
        gv = refs[1][...]
        for q in range(1, ng):
            gv = gv + refs[1 + q][...]
        mn = B1 * m_ref[...] + (1.0 - B1) * gv
        vn = B2 * v_ref[...] + (1.0 - B2) * (gv * gv)
        d_out[...] = -LR * ((mn / c1) / (jnp.sqrt(vn / c2) + EPS) + WD * w_ref[...])
        g_out[...] = gv
        m_out[...] = mn
        v_out[...] = vn

    bs = pl.BlockSpec((tile, cols), lambda i: (i, 0))
    return pl.pallas_call(
        body, name=name, grid=(rows // tile,), in_specs=[bs] * (3 + ng), out_specs=[bs] * 4,
        out_shape=[jax.ShapeDtypeStruct((rows, cols), f32)] * 4, compiler_params=_cparams(("parallel",)),
    )(w, *gs, m, v)


def _place():
    return lax.axis_index("x"), lax.axis_index("y"), lax.axis_index("c")


_ANY = pl.BlockSpec(memory_space=pl.ANY)


def _peer_chips(x, y):
    return [(1 - x, y), (x, 1 - y), (1 - x, 1 - y)]


def gather_weights(shards, *, name):
    nk = len(shards)

    def body(*refs):
        srcs, outs = refs[:nk], refs[nk:2 * nk]
        ici_send, ici_recv, d2d_send, d2d_recv = refs[2 * nk + 1:]
        x, y, c = _place()
        me = 2 * x + y
        peers = _peer_chips(x, y)
        pending = []
        for k in range(nk):
            half = srcs[k].shape[0] // 2
            mine = pl.ds(c * half, half)
            for p, (px, py) in enumerate(peers):
                cp = pltpu.make_async_remote_copy(
                    src_ref=srcs[k].at[mine], dst_ref=outs[k].at[me, mine], send_sem=ici_send.at[k, p],
                    recv_sem=ici_recv.at[k, p], device_id=(px, py, c), device_id_type=MESH)
                cp.start()
                pending.append(cp)
        for k in range(nk):
            half = srcs[k].shape[0] // 2
            mine = pl.ds(c * half, half)
            for p, (px, py) in enumerate(peers):
                landed = outs[k].at[2 * px + py, mine]
                pltpu.make_async_remote_copy(
                    src_ref=srcs[k].at[mine], dst_ref=landed, send_sem=ici_send.at[k, p], recv_sem=ici_recv.at[k, p],
                    device_id=(px, py, c), device_id_type=MESH).wait_recv()
                fwd = pltpu.make_async_remote_copy(
                    src_ref=landed, dst_ref=landed, send_sem=d2d_send.at[k, p], recv_sem=d2d_recv.at[k, p],
                    device_id=(x, y, 1 - c), device_id_type=MESH)
                fwd.start()
                pending.append(fwd)
        for k in range(nk):
            half = srcs[k].shape[0] // 2
            other = pl.ds((1 - c) * half, half)
            for p, (px, py) in enumerate(peers):
                theirs = outs[k].at[2 * px + py, other]
                pltpu.make_async_remote_copy(
                    src_ref=theirs, dst_ref=theirs, send_sem=d2d_send.at[k, p], recv_sem=d2d_recv.at[k, p],
                    device_id=(x, y, 1 - c), device_id_type=MESH).wait_recv()
        for cp in pending:
            cp.wait_send()
        refs[2 * nk][...] = jnp.zeros_like(refs[2 * nk])

    sem = lambda *shape: pltpu.SemaphoreType.DMA(shape)
    res = pl.pallas_call(
        body, name=name, in_specs=[_ANY] * nk, out_specs=[_ANY] * nk + [pl.BlockSpec(memory_space=pltpu.VMEM)],
        out_shape=[jax.ShapeDtypeStruct((4,) + s.shape, s.dtype) for s in shards]
        + [jax.ShapeDtypeStruct((8, 128), f32)],
        scratch_shapes=[sem(nk, 3), sem(nk, 3), sem(nk, 3), sem(nk, 3)],
    )(*shards)
    return list(res[:nk]), res[nk]


_HBM = pl.BlockSpec(memory_space=pltpu.HBM)
_SEM = pl.BlockSpec(memory_space=pltpu.SEMAPHORE)


def _ici_half_copies(srcs, lands, send_sems, recv_sems, incoming):
    x, y, c = _place()
    me = 2 * x + y
    out = []
    for k in range(len(srcs)):
        half = srcs[k].shape[0] // 2
        mine = pl.ds(c * half, half)
        for p, (px, py) in enumerate(_peer_chips(x, y)):
            out.append(pltpu.make_async_remote_copy(
                src_ref=srcs[k].at[mine], dst_ref=lands[k].at[(2 * px + py) if incoming else me, mine],
                send_sem=send_sems.at[3 * k + p], recv_sem=recv_sems.at[3 * k + p], device_id=(px, py, c),
                device_id_type=MESH))
    return out


def gather_start(shards, *, name):
    nk = len(shards)

    def body(*refs):
        srcs, lands = refs[:nk], refs[nk:2 * nk]
        send_sems, recv_sems = refs[2 * nk], refs[2 * nk + 1]
        token = refs[-1]
        for outgoing in _ici_half_copies(srcs, lands, send_sems, recv_sems, incoming=False):
            outgoing.start()
        token[...] = jnp.zeros_like(token)

    lands = [pltpu.with_memory_space_constraint(lax.empty((4,) + s.shape, s.dtype), pltpu.HBM) for s in shards]
    res = pl.pallas_call(
        body, name=name,
        out_shape=(pltpu.SemaphoreType.DMA((3 * nk,)), pltpu.SemaphoreType.DMA((3 * nk,)),
                   *[pltpu.HBM(s.shape, s.dtype) for s in shards], *[pltpu.HBM(z.shape, z.dtype) for z in lands],
                   jax.ShapeDtypeStruct((8, 128), f32)),
        in_specs=[_HBM] * (2 * nk), out_specs=(_SEM, _SEM, *[_HBM] * (2 * nk), pl.BlockSpec(memory_space=pltpu.VMEM)),
        input_output_aliases={k: 2 + k for k in range(2 * nk)},
        compiler_params=pltpu.CompilerParams(has_side_effects=pltpu.SideEffectType.DATAFLOW_SIDE_EFFECTING),
    )(*[pltpu.with_memory_space_constraint(s, pltpu.HBM) for s in shards], *lands)
    return res[0], res[1], list(res[2:2 + nk]), list(res[2 + nk:2 + 2 * nk]), res[-1]


def gather_wait(send_sems, recv_sems, shards, lands, after, *, name):
    nk = len(shards)

    def body(*refs):
        srcs, zones = refs[:nk], refs[nk:2 * nk]
        for outgoing in _ici_half_copies(srcs, zones, refs[2 * nk], refs[2 * nk + 1], incoming=False):
            outgoing.wait_send()
        for landed in _ici_half_copies(srcs, zones, refs[2 * nk], refs[2 * nk + 1], incoming=True):
            landed.wait_recv()

    res = pl.pallas_call(
        body, name=name,
        out_shape=(*[pltpu.HBM(s.shape, s.dtype) for s in shards], *[pltpu.HBM(z.shape, z.dtype) for z in lands]),
        in_specs=[_HBM] * (2 * nk) + [_SEM, _SEM, _ANY], out_specs=tuple([_HBM] * (2 * nk)),
        input_output_aliases={k: k for k in range(2 * nk)},
        compiler_params=pltpu.CompilerParams(has_side_effects=pltpu.SideEffectType.DATAFLOW_SIDE_EFFECTING),
    )(*shards, *lands, send_sems, recv_sems, after)
    return list(res[:nk]), list(res[nk:])


def gather_forward(lands, *, name):
    nk = len(lands)

    def body(*refs):
        zones = refs[nk:2 * nk]
        send_sems, recv_sems = refs[2 * nk:]
        x, y, c = _place()
        sends = []
        for k in range(nk):
            half = zones[k].shape[1] // 2
            for p, (px, py) in enumerate(_peer_chips(x, y)):
                landed = zones[k].at[2 * px + py, pl.ds(c * half, half)]
                cp = pltpu.make_async_remote_copy(
                    src_ref=landed, dst_ref=landed, send_sem=send_sems.at[k, p], recv_sem=recv_sems.at[k, p],
                    device_id=(x, y, 1 - c), device_id_type=MESH)
                cp.start()
                sends.append(cp)
        for k in range(nk):
            half = zones[k].shape[1] // 2
            for p, (px, py) in enumerate(_peer_chips(x, y)):
                theirs = zones[k].at[2 * px + py, pl.ds((1 - c) * half, half)]
                pltpu.make_async_remote_copy(
                    src_ref=theirs, dst_ref=theirs, send_sem=send_sems.at[k, p], recv_sem=recv_sems.at[k, p],
                    device_id=(x, y, 1 - c), device_id_type=MESH).wait_recv()
        for cp in sends:
            cp.wait_send()

    return pl.pallas_call(
        body, name=name, in_specs=[_ANY] * nk, out_specs=[_ANY] * nk,
        out_shape=[jax.ShapeDtypeStruct(z.shape, z.dtype) for z in lands],
        input_output_aliases={k: k for k in range(nk)},
        scratch_shapes=[pltpu.SemaphoreType.DMA((nk, 3)), pltpu.SemaphoreType.DMA((nk, 3))],
    )(*lands)


def grads_to_sibling(parts, *, name):
    nk = len(parts)

    def body(*refs):
        srcs, outs = refs[:nk], refs[nk:2 * nk]
        send_sems, recv_sems = refs[2 * nk:]
        x, y, c = _place()
        sends = []
        for k in range(nk):
            half = srcs[k].shape[1] // 2
            cp = pltpu.make_async_remote_copy(
                src_ref=srcs[k].at[:, pl.ds((1 - c) * half, half), :], dst_ref=outs[k], send_sem=send_sems.at[k],
                recv_sem=recv_sems.at[k], device_id=(x, y, 1 - c), device_id_type=MESH)
            cp.start()
            sends.append(cp)
        for cp in sends:
            cp.wait_recv()
        for cp in sends:
            cp.wait_send()

    return pl.pallas_call(
        body, name=name, in_specs=[_ANY] * nk, out_specs=[_ANY] * nk,
        out_shape=[jax.ShapeDtypeStruct((4, p.shape[1] // 2, p.shape[2]), p.dtype) for p in parts],
        scratch_shapes=[pltpu.SemaphoreType.DMA((nk,)), pltpu.SemaphoreType.DMA((nk,))],
    )(*parts)


def pair_sum(part, theirs, core, *, name):
    _, rows, cols = part.shape
    half = rows // 2
    tile = _pick(half, max(16, (1 << 20) // (4 * cols) // 16 * 16), 16)
    per = half // tile

    def body(c_ref, p_ref, t_ref, o_ref):
        o_ref[...] = (p_ref[...].astype(f32) + t_ref[...].astype(f32)).astype(bf16)

    grid_spec = pltpu.PrefetchScalarGridSpec(
        num_scalar_prefetch=1, grid=(4, per),
        in_specs=[pl.BlockSpec((1, tile, cols), lambda j, i, c_ref: (j, c_ref[0] * per + i, 0)),
                  pl.BlockSpec((1, tile, cols), lambda j, i, c_ref: (j, i, 0))],
        out_specs=pl.BlockSpec((1, tile, cols), lambda j, i, c_ref: (j, i, 0)))
    return pl.pallas_call(
        body, name=name, grid_spec=grid_spec, out_shape=jax.ShapeDtypeStruct((4, half, cols), bf16),
        compiler_params=_cparams(("parallel", "parallel")),
    )(core, part, theirs)


def _all_to_all_copies(srcs, lands, send_sems, recv_sems, incoming):
    x, y, c = _place()
    me = 2 * x + y
    out = []
    for k in range(len(srcs)):
        for p, (px, py) in enumerate(_peer_chips(x, y)):
            peer = 2 * px + py
            out.append(pltpu.make_async_remote_copy(
                src_ref=srcs[k].at[peer], dst_ref=lands[k].at[peer if incoming else me],
                send_sem=send_sems.at[3 * k + p], recv_sem=recv_sems.at[3 * k + p], device_id=(px, py, c),
                device_id_type=MESH))
    return out


def scatter_start(parts, *, name):
    nk = len(parts)

    def body(*refs):
        srcs, lands = refs[:nk], refs[nk:2 * nk]
        for outgoing in _all_to_all_copies(srcs, lands, refs[2 * nk], refs[2 * nk + 1], incoming=False):
            outgoing.start()
        refs[-1][...] = jnp.zeros_like(refs[-1])

    lands = [pltpu.with_memory_space_constraint(lax.empty(p.shape, p.dtype), pltpu.HBM) for p in parts]
    res = pl.pallas_call(
        body, name=name,
        out_shape=(pltpu.SemaphoreType.DMA((3 * nk,)), pltpu.SemaphoreType.DMA((3 * nk,)),
                   *[pltpu.HBM(p.shape, p.dtype) for p in parts], *[pltpu.HBM(p.shape, p.dtype) for p in parts],
                   jax.ShapeDtypeStruct((8, 128), f32)),
        in_specs=[_HBM] * (2 * nk), out_specs=(_SEM, _SEM, *[_HBM] * (2 * nk), pl.BlockSpec(memory_space=pltpu.VMEM)),
        input_output_aliases={k: 2 + k for k in range(2 * nk)},
        compiler_params=pltpu.CompilerParams(has_side_effects=pltpu.SideEffectType.DATAFLOW_SIDE_EFFECTING),
    )(*[pltpu.with_memory_space_constraint(p, pltpu.HBM) for p in parts], *lands)
    return res[0], res[1], list(res[2:2 + nk]), list(res[2 + nk:2 + 2 * nk]), res[-1]


def scatter_wait(send_sems, recv_sems, parts, lands, after, *, name):
    nk = len(parts)

    def body(*refs):
        srcs, zones = refs[:nk], refs[nk:2 * nk]
        for outgoing in _all_to_all_copies(srcs, zones, refs[2 * nk], refs[2 * nk + 1], incoming=False):
            outgoing.wait_send()
        for landed in _all_to_all_copies(srcs, zones, refs[2 * nk], refs[2 * nk + 1], incoming=True):
            landed.wait_recv()

    res = pl.pallas_call(
        body, name=name,
        out_shape=(*[pltpu.HBM(p.shape, p.dtype) for p in parts], *[pltpu.HBM(z.shape, z.dtype) for z in lands]),
        in_specs=[_HBM] * (2 * nk) + [_SEM, _SEM, _ANY], out_specs=tuple([_HBM] * (2 * nk)),
        input_output_aliases={k: k for k in range(2 * nk)},
        compiler_params=pltpu.CompilerParams(has_side_effects=pltpu.SideEffectType.DATAFLOW_SIDE_EFFECTING),
    )(*parts, *lands, send_sems, recv_sems, after)
    return list(res[:nk]), list(res[nk:])


def join_halves(bufs, layout, *, name):
    nk, nb = len(layout), len(bufs)

    def body(*refs):
        outs = refs[nb:2 * nb]
        send_sems, recv_sems = refs[2 * nb:]
        x, y, c = _place()
        pending = []
        for k, (o, off, rows) in enumerate(layout):
            half = rows // 2
            mine = outs[o].at[pl.ds(off + c * half, half), :]
            cp = pltpu.make_async_remote_copy(
                src_ref=mine, dst_ref=mine, send_sem=send_sems.at[k], recv_sem=recv_sems.at[k],
                device_id=(x, y, 1 - c), device_id_type=MESH)
            cp.start()
            pending.append(cp)
        for k, (o, off, rows) in enumerate(layout):
            half = rows // 2
            theirs = outs[o].at[pl.ds(off + (1 - c) * half, half), :]
            pltpu.make_async_remote_copy(
                src_ref=theirs, dst_ref=theirs, send_sem=send_sems.at[k], recv_sem=recv_sems.at[k],
                device_id=(x, y, 1 - c), device_id_type=MESH).wait_recv()
        for cp in pending:
            cp.wait_send()

    return pl.pallas_call(
        body, name=name, in_specs=[_ANY] * nb, out_specs=[_ANY] * nb,
        out_shape=[jax.ShapeDtypeStruct(b.shape, b.dtype) for b in bufs],
        input_output_aliases={o: o for o in range(nb)},
        scratch_shapes=[pltpu.SemaphoreType.DMA((nk,)), pltpu.SemaphoreType.DMA((nk,))],
    )(*bufs)


def place_slab(dest, src, index, *, name):
    rows, cols = src.shape
    tile = _pick(rows, max(16, (1 << 20) // (src.dtype.itemsize * cols) // 16 * 16), 16)

    def body(i_ref, s_ref, d_ref, o_ref):
        del i_ref, d_ref
        o_ref[0] = s_ref[...]

    grid_spec = pltpu.PrefetchScalarGridSpec(
        num_scalar_prefetch=1, grid=(rows // tile,),
        in_specs=[pl.BlockSpec((tile, cols), lambda i, idx: (i, 0)), _ANY],
        out_specs=pl.BlockSpec((1, tile, cols), lambda i, idx: (idx[0], i, 0)))
    return pl.pallas_call(
        body, name=name, grid_spec=grid_spec, out_shape=jax.ShapeDtypeStruct(dest.shape, dest.dtype),
        input_output_aliases={2: 0}, compiler_params=_cparams(("parallel",)),
    )(index, src, dest)


def all_reduce_small(src, *, name):
    rows, cols = src.shape

    def body(s_ref, o_ref, buf, send_sems, recv_sems):
        x, y, c = _place()
        me = 4 * x + 2 * y + c
        buf[me] = s_ref[...]
        sends = []
        for msk in range(1, 8):
            px = x ^ (msk >> 2)
            py = y ^ ((msk >> 1) & 1)
            pc = c ^ (msk & 1)
            cp = pltpu.make_async_remote_copy(
                src_ref=s_ref, dst_ref=buf.at[me], send_sem=send_sems.at[msk - 1], recv_sem=recv_sems.at[msk - 1],
                device_id=(px, py, pc), device_id_type=MESH)
            cp.start()
            sends.append(cp)
        for msk in range(1, 8):
            px = x ^ (msk >> 2)
            py = y ^ ((msk >> 1) & 1)
            pc = c ^ (msk & 1)
            pltpu.make_async_remote_copy(
                src_ref=s_ref, dst_ref=buf.at[4 * px + 2 * py + pc], send_sem=send_sems.at[msk - 1],
                recv_sem=recv_sems.at[msk - 1], device_id=(px, py, pc), device_id_type=MESH).wait_recv()
        for cp in sends:
            cp.wait_send()
        acc = buf[0]
        for d in range(1, 8):
            acc = acc + buf[d]
        o_ref[...] = acc

    vm = pl.BlockSpec(memory_space=pltpu.VMEM)
    return pl.pallas_call(
        body, name=name, in_specs=[vm], out_specs=vm, out_shape=jax.ShapeDtypeStruct((rows, cols), f32),
        scratch_shapes=[pltpu.VMEM((8, rows, cols), f32), pltpu.SemaphoreType.DMA((7,)),
                        pltpu.SemaphoreType.DMA((7,))],
        compiler_params=pltpu.CompilerParams(vmem_limit_bytes=VMEM_LIMIT),
    )(src)


def sum4_into(arrived, own, dest, where, *, layer, total_rows, name):
    _, rows, cols = arrived.shape
    tile = _pick(rows, max(16, (1 << 20) // (4 * cols) // 16 * 16), 16)
    per = rows // tile

    def body(w_ref, a_ref, own_ref, *rest):
        mine = own_ref[0].astype(f32)
        p = [jnp.where(w_ref[1] == j, mine, a_ref[j].astype(f32)) for j in range(4)]
        rest[-1][...] = ((p[0] + p[1]) + p[2]) + p[3]

    grid_spec = pltpu.PrefetchScalarGridSpec(
        num_scalar_prefetch=1, grid=(per,),
        in_specs=[pl.BlockSpec((4, tile, cols), lambda i, w_ref: (0, i, 0)),
                  pl.BlockSpec((1, tile, cols), lambda i, w_ref: (w_ref[1], i, 0))] + ([] if dest is None else [_ANY]),
        out_specs=pl.BlockSpec((tile, cols), lambda i, w_ref: ((2 * layer + w_ref[0]) * per + i, 0)))
    return pl.pallas_call(
        body, name=name, grid_spec=grid_spec, out_shape=jax.ShapeDtypeStruct((total_rows, cols), f32),
        input_output_aliases={} if dest is None else {3: 0}, compiler_params=_cparams(("parallel",)),
    )(where, arrived, own, *([] if dest is None else [dest]))


def _consts():
    idx = np.arange(RW)
    bd = (idx[:, None] // RN == idx[None, :] // RN).astype(np.float32)
    place = np.zeros((128, DQK), np.float32)
    place[np.arange(ROPE), NOPE + np.arange(ROPE)] = 1.0
    rot = np.zeros((DQK, DQK), np.float32)
    half = ROPE // 2
    rot[NOPE + half + np.arange(half), NOPE + np.arange(half)] = -1.0
    rot[NOPE + np.arange(half), NOPE + half + np.arange(half)] = 1.0
    return jnp.asarray(bd), jnp.asarray(place), jnp.asarray(rot)


def _rope_tables(positions):
    freqs = ROPE_THETA ** (-(jnp.arange(ROPE // 2, dtype=f32) * 2.0 / ROPE))
    ang = positions.astype(f32)[:, None] * freqs
    cos, sin = jnp.cos(ang), jnp.sin(ang)
    ones = jnp.ones((positions.shape[0], NOPE), f32)
    return (jnp.concatenate([ones, cos, cos], axis=-1), jnp.concatenate([0.0 * ones, sin, sin], axis=-1))


STAGES = (("w_in",), ("mla_wq_b", "mla_wkv_b", "mla_w_o", "rwkv_w_o", "conv_w_o", "w_out"), ("w_up", "w_down"))


def derive_stage(stage, w):
    if stage == 0:
        w_in = w["w_in"]
        pad = jnp.zeros((D, MLA_PAD - MLA_COLS), w_in.dtype)
        return dict(gate=w_in[:, :GATE], mla=jnp.concatenate([w_in[:, GATE:GATE + MLA_COLS], pad], axis=1),
                    rw=w_in[:, GATE + MLA_COLS:GATE + MLA_COLS + 4 * RW], cv=w_in[:, GATE + MLA_COLS + 4 * RW:])
    if stage == 1:
        wkv = w["mla_wkv_b"].reshape(KVL, MLA_H, NOPE + DV)
        wk = jnp.concatenate([wkv[:, :, :NOPE], jnp.zeros((KVL, MLA_H, ROPE), wkv.dtype)], axis=-1)
        return dict(wq=w["mla_wq_b"].reshape(QL, MLA_H, DQK).transpose(1, 0, 2), wk=wk.transpose(1, 0, 2),
                    wv=wkv[:, :, NOPE:].transpose(1, 0, 2), wo=w["mla_w_o"].reshape(MLA_H, DV, D),
                    rwo=w["rwkv_w_o"], cvo=w["conv_w_o"], out=w["w_out"])
    return dict(up=w["w_up"], down=w["w_down"])


W_IN_WINDOW_TILE = (0, 10, 21, 31)
W_IN_WINDOW = 1664
W_IN_SHARD = 1384


def w_in_window_cols(win, chip):
    gap = MLA_PAD - MLA_COLS
    branches = []
    for j in range(4):
        lo, hi = W_IN_SHARD * j, W_IN_SHARD * (j + 1)
        base = 128 * W_IN_WINDOW_TILE[j]
        cut = GATE + MLA_COLS
        if hi <= cut:
            branches.append(lambda w, a=lo - base: w[:, a:a + W_IN_SHARD])
        elif lo >= cut:
            branches.append(lambda w, a=lo + gap - base: w[:, a:a + W_IN_SHARD])
        else:
            branches.append(lambda w, a=lo - base, n1=cut - lo, b=cut + gap - base, n2=hi - cut:
                            jnp.concatenate([w[:, a:a + n1], w[:, b:b + n2]], axis=1))
    return lax.switch(chip, branches, win)


def chip_major_grads(stage, g):
    if stage == 0:
        padded = jnp.concatenate([g["gate"], g["mla"], g["rw"], g["cv"]], axis=1)
        return dict(w_in=jnp.stack([padded[:, 128 * t:128 * t + W_IN_WINDOW] for t in W_IN_WINDOW_TILE]))
    if stage == 1:
        heads = MLA_H // 4
        wq = g["wq"].reshape(4, heads, QL, DQK).transpose(0, 2, 1, 3).reshape(4, QL, heads * DQK)
        wkv = jnp.concatenate([g["wk"][:, :, :NOPE], g["wv"]], axis=-1)
        wkv = wkv.reshape(4, heads, KVL, NOPE + DV).transpose(0, 2, 1, 3).reshape(4, KVL, heads * (NOPE + DV))
        return dict(mla_wq_b=wq, mla_wkv_b=wkv, mla_w_o=g["wo"].reshape(4, MLA_H * DV, D // 4),
                    rwkv_w_o=g["rwo"], conv_w_o=g["cvo"], w_out=g["out"].reshape(4, D // 4, D))
    return dict(w_up=g["up"], w_down=g["down"].reshape(4, DFF // 4, D))


def _row(v):
    return v.reshape(1, -1)


def local_step(x, positions, target, w, sm, big_of=None, on_grads=None):
    if big_of is None:
        big_of = lambda l, stage, _after: {n: w[n][l] for n in STAGES[stage]}
    if on_grads is None:
        on_grads = lambda l, stage, slabs: None
    s_len = x.shape[0]
    t_row = _pick(s_len, 256, 8)
    t_wide = _pick(s_len, 128, 8)
    bd, place, rot = _consts()
    cos, sin = _rope_tables(positions)
    sds = lambda *shape: jax.ShapeDtypeStruct(shape, f32)
    sdb = lambda *shape: jax.ShapeDtypeStruct(shape, bf16)
    saved = []
    v_first = None
    for l in range(DEPTH):
        tag = f"l{l}_"
        lw = derive_stage(0, big_of(l, 0, x))
        vres = l > 0
        p_norm1 = [_row(sm["attn_norm"][l])]
        (h,) = rows_fwd(_fn_norm, [x], p_norm1, [], [sds(s_len, D)], tile=t_row, name=tag + "norm1")
        gate = mm(h, lw["gate"], name=tag + "proj_gate")
        mla = mm(h, lw["mla"], name=tag + "proj_mla")
        rwc = mm(h, lw["rw"], name=tag + "proj_rwkv")
        cvc = mm(h, lw["cv"], name=tag + "proj_conv")
        lw.update(derive_stage(1, big_of(l, 1, cvc)))
        p_mla = [_row(sm["mla_q_a_norm"][l]), _row(sm["mla_kv_a_norm"][l])]
        qn, kvn, kpe = rows_fwd(_fn_mla_prep, [mla], p_mla, [], [sdb(s_len, QL), sdb(s_len, KVL), sds(s_len, 128)],
                                tile=t_row, name=tag + "mla_prep")
        q_raw = mm(qn, lw["wq"], b_batched=True, name=tag + "q_proj")
        kn_pad = mm(kvn, lw["wk"], b_batched=True, name=tag + "k_proj")
        vv = mm(kvn, lw["wv"], b_batched=True, name=tag + "v_proj")
        p_qk = [_row(sm["mla_q_norm"][l]), _row(sm["mla_k_norm"][l])]
        q, k = rows_fwd(_fn_qk_post, [q_raw, kn_pad, kpe, cos, sin], p_qk, [place, rot],
                        [sds(MLA_H, s_len, DQK), sds(MLA_H, s_len, DQK)], tile=t_wide, name=tag + "qk_post")
        o = attn_fwd(q, k, vv, tq=_pick(s_len, 256, 8), name=tag + "attn")
        o_a = mm(o, lw["wo"], a_batched=True, b_batched=True, reduce_batch=True, name=tag + "o_a")
        p_rw = [_row(sm["rwkv_mu"][l]), _row(sm["rwkv_w0"][l]), w["rwkv_w2"][l], _row(sm["rwkv_a0"][l]),
                w["rwkv_a2"][l], w["rwkv_g2"][l], _row(sm["rwkv_k_k"][l]), _row(sm["rwkv_k_a"][l])]
        rw_rows, rw_halos = [rwc], (0,)
        if vres:
            p_rw += [w["rwkv_v1"][l - 1], _row(sm["rwkv_v_mu"][l - 1]), _row(sm["rwkv_v0"][l - 1]), w["rwkv_v2"][l - 1]]
            rw_rows, rw_halos = [rwc, h, v_first], (0, 1)
        fn_prep = _make_fn_rwkv_prep(vres)
        r, ld, k2, v, an, bn, g = rows_fwd(fn_prep, rw_rows, p_rw, [bd], [sds(s_len, RW)] * 7, tile=t_row,
                                           name=tag + "rwkv_prep", halos=rw_halos)
        if not vres:
            v_first = v
        y, states = wkv_fwd(r, ld, k2, v, an, bn, name=tag + "wkv")
        p_post = [_row(sm["rwkv_ln_w"][l]), _row(sm["rwkv_ln_b"][l]), _row(sm["rwkv_r_k"][l])]
        (yb,) = rows_fwd(_fn_rwkv_post, [y, r, k2, v, g], p_post, [bd], [sdb(s_len, RW)], tile=t_row,
                         name=tag + "rwkv_post")
        o_b = mm(yb, lw["rwo"], name=tag + "o_b")
        p_cv = [w["conv_w"][l][q:q + 1] for q in range(3)]
        (yc,) = rows_fwd(_fn_conv, [cvc], p_cv, [], [sdb(s_len, CW)], tile=t_row, name=tag + "conv", halos=(0,))
        o_c = mm(yc, lw["cvo"], name=tag + "o_c")
        (merged,) = rows_fwd(_fn_merge, [gate, o_a, o_b, o_c], [], [], [sdb(s_len, D)], tile=t_wide,
                             name=tag + "merge")
        x1 = mm(merged, lw["out"], add=x, name=tag + "out_proj")
        lw.update(derive_stage(2, big_of(l, 2, x1)))
        p_norm2 = [_row(sm["mlp_norm"][l])]
        (h2,) = rows_fwd(_fn_norm, [x1], p_norm2, [], [sdb(s_len, D)], tile=t_row, name=tag + "norm2")
        up, act = mm(h2, lw["up"], relu2_out=True, name=tag + "up")
        x2 = mm(act, lw["down"], add=x1, name=tag + "down")
        saved.append(dict(lw=lw, x=x, h=h, gate=gate, mla=mla, rwc=rwc, cvc=cvc, qn=qn, kvn=kvn, kpe=kpe,
                          q_raw=q_raw, kn_pad=kn_pad, vv=vv, q=q, k=k, o=o, o_a=o_a, r=r, ld=ld, k2=k2, v=v,
                          an=an, bn=bn, g=g, y=y, states=states, yb=yb, o_b=o_b, yc=yc, o_c=o_c, merged=merged,
                          x1=x1, h2=h2, up=up, act=act, p_norm1=p_norm1, p_mla=p_mla, p_qk=p_qk, p_rw=p_rw,
                          p_post=p_post, p_cv=p_cv, p_norm2=p_norm2, rw_rows=rw_rows, rw_halos=rw_halos,
                          fn_prep=fn_prep, v_first=v_first if vres else None))
        x = x2

    loss, dx = loss_head(x, target, tile=t_row, name="loss_head")

    grads = {n: [None] * (DEPTH - 1 if n in ("rwkv_v1", "rwkv_v_mu", "rwkv_v0", "rwkv_v2") else DEPTH)
             for n in WEIGHTS}
    dv_first = None
    for l in reversed(range(DEPTH)):
        tag = f"b{l}_"
        sv = saved[l]
        lw = sv["lw"]
        vres = l > 0
        g_down = mm(sv["act"], dx, ta=True, out_dtype=bf16, name=tag + "g_down")
        dup = mm(dx, lw["down"], tb=True, act_grad=sv["up"], out_dtype=bf16, name=tag + "d_up")
        g_up = mm(sv["h2"], dup, ta=True, n_split=4, out_dtype=bf16, name=tag + "g_up")
        dh2 = mm(dup, lw["up"], tb=True, name=tag + "d_h2")
        slabs = chip_major_grads(2, dict(up=g_up, down=g_down))
        token = on_grads(l, 2, slabs)
        p_norm2 = sv["p_norm2"] if token is None else [sv["p_norm2"][0] + token[0, 0]]
        (dx1,), (g_n2,) = rows_bwd(_fn_norm, [sv["x1"]], p_norm2, [], [[dh2]], tile=t_row,
                                   name=tag + "norm2", extra={0: [dx]})
        g_out = mm(sv["merged"], dx1, ta=True, out_dtype=bf16, name=tag + "g_out")
        dmerged = mm(dx1, lw["out"], tb=True, name=tag + "d_merged")
        (dgate, do_a, do_b, do_c), _ = rows_bwd(_fn_merge, [sv["gate"], sv["o_a"], sv["o_b"], sv["o_c"]], [], [],
                                                [[dmerged]], tile=t_wide, name=tag + "merge",
                                                grad_dtypes=[bf16] * 4)
        g_cvo = mm(sv["yc"], do_c, ta=True, n_split=4, out_dtype=bf16, name=tag + "g_cvo")
        dyc = mm(do_c, lw["cvo"], tb=True, name=tag + "d_yc")
        (dcvc,), g_cw = rows_bwd(_fn_conv, [sv["cvc"]], sv["p_cv"], [], [[dyc]], tile=t_row, name=tag + "conv",
                                    halos=(0,))
        g_rwo = mm(sv["yb"], do_b, ta=True, n_split=4, out_dtype=bf16, name=tag + "g_rwo")
        dyb = mm(do_b, lw["rwo"], tb=True, name=tag + "d_yb")
        (dy, dr_p, dk_p, dv_p, dg), g_post = rows_bwd(
            _fn_rwkv_post, [sv["y"], sv["r"], sv["k2"], sv["v"], sv["g"]], sv["p_post"], [bd], [[dyb]], tile=t_row,
            name=tag + "rwkv_post")
        dr_s, dld, dk_s, dv_s, dan, dbn = wkv_bwd(sv["r"], sv["ld"], sv["k2"], sv["v"], sv["an"], sv["bn"],
                                                  sv["states"], dy, name=tag + "wkv")
        dv_list = [dv_s, dv_p] + ([dv_first] if (not vres and dv_first is not None) else [])
        d_prep, g_prep = rows_bwd(
            sv["fn_prep"], sv["rw_rows"], sv["p_rw"], [bd],
            [[dr_s, dr_p], [dld], [dk_s, dk_p], dv_list, [dan], [dbn], [dg]], tile=t_row, name=tag + "rwkv_prep",
            halos=sv["rw_halos"])
        drwc = d_prep[0]
        dh_extra = []
        if vres:
            dh_extra = [d_prep[1]]
            dv_first = d_prep[2]
        g_wo = mm(sv["o"], do_a, ta=True, a_batched=True, n_split=4, tk=s_len, out_dtype=bf16, name=tag + "g_wo")
        do = mm(do_a, lw["wo"], tb=True, b_batched=True, name=tag + "d_o")
        dq, dk, dvv = attn_bwd(sv["q"], sv["k"], sv["vv"], do, tq=_pick(s_len, 256, 8), name=tag + "attn")
        (dq_raw, dkn_pad, dkpe), g_qk = rows_bwd(
            _fn_qk_post, [sv["q_raw"], sv["kn_pad"], sv["kpe"], cos, sin], sv["p_qk"], [place, rot], [[dq], [dk]],
            tile=t_wide, name=tag + "qk_post", grad_rows=[0, 1, 2], grad_dtypes=[bf16, bf16, f32])
        g_wq = mm(sv["qn"], dq_raw, ta=True, b_batched=True, tk=s_len, out_dtype=bf16, name=tag + "g_wq")
        g_wk = mm(sv["kvn"], dkn_pad, ta=True, b_batched=True, tk=s_len, out_dtype=bf16, name=tag + "g_wk")
        g_wv = mm(sv["kvn"], dvv, ta=True, b_batched=True, tk=s_len, out_dtype=bf16, name=tag + "g_wv")
        dqn = mm(dq_raw, lw["wq"], tb=True, a_batched=True, b_batched=True, reduce_batch=True, name=tag + "d_qn")
        dkvn = mm(dkn_pad, lw["wk"], tb=True, a_batched=True, b_batched=True, reduce_batch=True, name=tag + "d_kvn_k")
        dkvn = mm(dvv, lw["wv"], tb=True, a_batched=True, b_batched=True, reduce_batch=True, add=dkvn,
                  name=tag + "d_kvn_v")
        slabs.update(chip_major_grads(1, dict(wq=g_wq, wk=g_wk, wv=g_wv, wo=g_wo, rwo=g_rwo, cvo=g_cvo, out=g_out)))
        token = on_grads(l, 1, {n: slabs[n] for n in STAGES[1]})
        p_mla = sv["p_mla"] if token is None else [sv["p_mla"][0] + token[0, 0], sv["p_mla"][1]]
        (dmla,), g_mla = rows_bwd(_fn_mla_prep, [sv["mla"]], p_mla, [], [[dqn], [dkvn], [dkpe]], tile=t_row,
                                  name=tag + "mla_prep", grad_dtypes=[bf16])
        g_gate = mm(sv["h"], dgate, ta=True, out_dtype=bf16, name=tag + "g_gate")
        g_mlaw = mm(sv["h"], dmla, ta=True, out_dtype=bf16, name=tag + "g_mla")
        g_rw = mm(sv["h"], drwc, ta=True, out_dtype=bf16, name=tag + "g_rw")
        g_cv = mm(sv["h"], dcvc, ta=True, out_dtype=bf16, name=tag + "g_cv")
        dh = mm(dgate, lw["gate"], tb=True, name=tag + "d_h_gate")
        dh = mm(dmla, lw["mla"], tb=True, add=dh, name=tag + "d_h_mla")
        dh = mm(drwc, lw["rw"], tb=True, add=dh, name=tag + "d_h_rw")
        dh = mm(dcvc, lw["cv"], tb=True, add=dh, name=tag + "d_h_cv")
        (dx,), (g_n1,) = rows_bwd(_fn_norm, [sv["x"]], sv["p_norm1"], [], [[dh] + dh_extra], tile=t_row,
                                  name=tag + "norm1", extra={0: [dx1]})
        slabs.update(chip_major_grads(0, dict(gate=g_gate, mla=g_mlaw, rw=g_rw, cv=g_cv)))
        token = on_grads(l, 0, {n: slabs[n] for n in STAGES[0]})
        if token is not None and l > 0:
            dx = dx + token[0, 0]
        for n, val in slabs.items():
            grads[n][l] = val
        grads["attn_norm"][l], grads["mlp_norm"][l] = g_n1, g_n2
        grads["mla_q_a_norm"][l], grads["mla_kv_a_norm"][l] = g_mla
        grads["mla_q_norm"][l], grads["mla_k_norm"][l] = g_qk
        grads["rwkv_ln_w"][l], grads["rwkv_ln_b"][l], grads["rwkv_r_k"][l] = g_post
        for n, val in zip(["rwkv_mu", "rwkv_w0", "rwkv_w2", "rwkv_a0", "rwkv_a2", "rwkv_g2", "rwkv_k_k", "rwkv_k_a"],
                          g_prep[:8]):
            grads[n][l] = val
        if vres:
            for n, val in zip(["rwkv_v1", "rwkv_v_mu", "rwkv_v0", "rwkv_v2"], g_prep[8:12]):
                grads[n][l - 1] = val
        grads["conv_w"][l] = jnp.concatenate(g_cw, axis=0)
    return loss, dx, grads


def _split3(a):
    hi = a.astype(bf16)
    r1 = a - hi.astype(f32)
    mid = r1.astype(bf16)
    lo = (r1 - mid.astype(f32)).astype(bf16)
    return hi, mid, lo


def _shard_axis(name):
    return 1 if name in ROW_SHARDED else 2


def _pack(pieces, width, dtype, row_align):
    flat = jnp.concatenate([p.reshape(-1).astype(dtype) for p in pieces])
    rows = -(-flat.shape[0] // width)
    rows = -(-rows // row_align) * row_align
    return jnp.pad(flat, (0, rows * width - flat.shape[0])).reshape(rows, width)


def _unpack(flat2d, shapes):
    flat = flat2d.reshape(-1)
    out, off = [], 0
    for shp in shapes:
        n = int(np.prod(shp))
        out.append(flat[off:off + n].reshape(shp))
        off += n
    return out


def kernel(x, positions, attn_norm, w_in, mla_q_a_norm, mla_wq_b, mla_kv_a_norm, mla_wkv_b, mla_q_norm, mla_k_norm, mla_w_o, rwkv_mu, rwkv_w0, rwkv_w2, rwkv_a0, rwkv_a2, rwkv_g2, rwkv_k_k, rwkv_k_a, rwkv_r_k, rwkv_ln_w, rwkv_ln_b, rwkv_w_o, rwkv_v1, rwkv_v_mu, rwkv_v0, rwkv_v2, conv_w, conv_w_o, w_out, mlp_norm, w_up, w_down, loss_target, m_attn_norm, m_w_in, m_mla_q_a_norm, m_mla_wq_b, m_mla_kv_a_norm, m_mla_wkv_b, m_mla_q_norm, m_mla_k_norm, m_mla_w_o, m_rwkv_mu, m_rwkv_w0, m_rwkv_w2, m_rwkv_a0, m_rwkv_a2, m_rwkv_g2, m_rwkv_k_k, m_rwkv_k_a, m_rwkv_r_k, m_rwkv_ln_w, m_rwkv_ln_b, m_rwkv_w_o, m_rwkv_v1, m_rwkv_v_mu, m_rwkv_v0, m_rwkv_v2, m_conv_w, m_conv_w_o, m_w_out, m_mlp_norm, m_w_up, m_w_down, v_attn_norm, v_w_in, v_mla_q_a_norm, v_mla_wq_b, v_mla_kv_a_norm, v_mla_wkv_b, v_mla_q_norm, v_mla_k_norm, v_mla_w_o, v_rwkv_mu, v_rwkv_w0, v_rwkv_w2, v_rwkv_a0, v_rwkv_a2, v_rwkv_g2, v_rwkv_k_k, v_rwkv_k_a, v_rwkv_r_k, v_rwkv_ln_w, v_rwkv_ln_b, v_rwkv_w_o, v_rwkv_v1, v_rwkv_v_mu, v_rwkv_v0, v_rwkv_v2, v_conv_w, v_conv_w_o, v_w_out, v_mlp_norm, v_w_up, v_w_down):
    args = dict(locals())
    wts = {n: args[n] for n in WEIGHTS}
    mom = {n: args["m_" + n] for n in WEIGHTS}
    var = {n: args["v_" + n] for n in WEIGHTS}
    chip = 2 * lax.axis_index("x") + lax.axis_index("y")
    core = lax.axis_index("c").astype(jnp.int32).reshape(1)

    med_names = [n for n in MED if n != "conv_w"]
    med_pieces = [wts[n] for n in med_names] + list(_split3(wts["conv_w"]))
    med_shapes = [p.shape for p in med_pieces]
    chip_idx = chip.astype(jnp.int32).reshape(1)
    shards_first = [wts[n][0].astype(bf16) for n in STAGES[0]] + [_pack(med_pieces, 128, bf16, 32)]
    got_first, token = gather_weights(shards_first, name="gather_l0_s0")
    got_first = [place_slab(g, s, chip_idx, name=f"place_own_l0_s0_{q}")
                 for q, (g, s) in enumerate(zip(got_first, shards_first))]
    in_flight = {}
    for key, names, l in (("l0_s1", STAGES[1], 0), ("l0_s2", STAGES[2], 0), ("l1", BIG, 1)):
        group = [wts[n][l].astype(bf16) for n in names]
        group[0] = group[0] + token[0, 0].astype(bf16)
        in_flight[key] = (names, gather_start(group, name="gather_start_" + key))
        token = in_flight[key][1][4]

    def whole_of(names, slabs):
        out = {}
        for n, by_chip in zip(names, slabs):
            _, rows, cols = by_chip.shape
            if n in ROW_SHARDED:
                out[n] = by_chip.reshape(4 * rows, cols)
            else:
                out[n] = by_chip.transpose(1, 0, 2).reshape(rows, 4 * cols)
        return out

    landed = {}

    def big_of(l, stage, after):
        if (l, stage) == (0, 0):
            return whole_of(STAGES[0], got_first)
        key = "l1" if l == 1 else f"l0_s{stage}"
        if key not in landed:
            names, (send_sems, recv_sems, thru, lands, _) = in_flight[key]
            thru, lands = gather_wait(send_sems, recv_sems, thru, lands, after, name="gather_wait_" + key)
            lands = gather_forward(lands, name="gather_forward_" + key)
            landed[key] = whole_of(names, [place_slab(g, s, chip_idx, name=f"place_own_{key}_{q}")
                                           for q, (g, s) in enumerate(zip(lands, thru))])
        return {n: landed[key][n] for n in STAGES[stage]}

    whole = {}
    per_chip = [_unpack(got_first[len(STAGES[0])][j], med_shapes) for j in range(4)]
    for q, n in enumerate(med_names):
        whole[n] = jnp.concatenate([per_chip[j][q] for j in range(4)], axis=_shard_axis(n)).astype(f32)
    base = len(med_names)
    cw_parts = [jnp.concatenate([per_chip[j][base + t] for j in range(4)], axis=2).astype(f32) for t in range(3)]
    whole["conv_w"] = (cw_parts[0] + cw_parts[1]) + cw_parts[2]
    small = {n: wts[n] for n in SMALL}
    small["rwkv_r_k"] = wts["rwkv_r_k"].reshape(DEPTH, RW)

    exchanges = []

    def on_grads(l, stage, slabs):
        names = STAGES[stage]
        tag = f"l{l}_s{stage}"
        parts = [slabs[n] for n in names]
        from_sibling = grads_to_sibling(parts, name="grads_to_sibling_" + tag)
        chip_sums = [pair_sum(s, t, core, name=f"pair_sum_{n}_{l}") for n, s, t in zip(names, parts, from_sibling)]
        started = scatter_start(chip_sums, name="scatter_start_" + tag)
        exchanges.append((l, names, tag, started))
        return started[4]

    small["attn_norm"] = small["attn_norm"] + token[0, 0]
    loss, grad_x, grads = local_step(x[0], positions[0], loss_target[0], whole, small, big_of, on_grads)
    loss = lax.psum(loss, ("x", "y", "c"))

    sm_names = SMALL + MED
    sm_grads = [jnp.stack(grads[n]) for n in sm_names]
    sm_shapes = [g.shape for g in sm_grads]
    sm_sum = _unpack(all_reduce_small(_pack(sm_grads, 128, f32, 8), name="reduce_small"), sm_shapes)
    gsum = {}
    for n, g in zip(sm_names, sm_sum):
        if n in MED:
            ax = _shard_axis(n)
            width = wts[n].shape[ax]
            g = lax.dynamic_slice_in_dim(g, chip * width, width, axis=ax)
        gsum[n] = g.reshape(wts[n].shape)
    where = jnp.stack([lax.axis_index("c"), chip]).astype(jnp.int32)
    bufs, layout = {}, []
    for l, names, tag, (send_sems, recv_sems, thru, lands, _) in exchanges:
        own, arrived = scatter_wait(send_sems, recv_sems, thru, lands, grad_x, name="scatter_wait_" + tag)
        for n, mine, theirs in zip(names, own, arrived):
            rows = 2 * theirs.shape[1]
            bufs[n] = sum4_into(theirs, mine, bufs.get(n), where, layer=l, total_rows=DEPTH * rows,
                                name=f"sum_chips_{n}_{l}")
            layout.append((BIG.index(n), l * rows, rows))
    reduced = join_halves([bufs[n] for n in BIG], layout, name="join_halves")

    out_g, out_d, out_m, out_v = {}, {}, {}, {}
    for q, n in enumerate(BIG):
        shp = wts[n].shape
        as2d = lambda a: a.reshape(-1, shp[-1])
        g2d = w_in_window_cols(reduced[q], chip) if n == "w_in" else reduced[q]
        res = adamw(as2d(wts[n]), g2d, as2d(mom[n]), as2d(var[n]), name="adamw_" + n)
        out_g[n], out_d[n], out_m[n], out_v[n] = [r.reshape(shp) for r in res]
    sm_all = SMALL + MED
    sm_shapes2 = [wts[n].shape for n in sm_all]
    res = adamw(_pack([wts[n] for n in sm_all], 128, f32, 8), _pack([gsum[n] for n in sm_all], 128, f32, 8),
                _pack([mom[n] for n in sm_all], 128, f32, 8), _pack([var[n] for n in sm_all], 128, f32, 8),
                name="adamw_small")
    for tgt, flat in zip((out_g, out_d, out_m, out_v), res):
        for n, val in zip(sm_all, _unpack(flat, sm_shapes2)):
            tgt[n] = val
    return (loss, grad_x[None], *[out_g[n] for n in WEIGHTS], *[out_d[n] for n in WEIGHTS],
            *[out_m[n] for n in WEIGHTS], *[out_v[n] for n in WEIGHTS])
```

```python
import functools

import jax
import jax.numpy as jnp
import numpy as np
from jax import lax
from jax.experimental import pallas as pl
from jax.experimental.pallas import tpu as pltpu

f32, bf16 = jnp.float32, jnp.bfloat16
HI = lax.Precision.HIGHEST
MESH = pl.DeviceIdType.MESH

D = 1024
DEPTH = 2
MLA_H, NOPE, ROPE, DQK, DV = 8, 64, 32, 96, 64
QL, KVL = 384, 256
RW, RH, RN = 256, 4, 64
DL, AL, GL, MVL = 64, 64, 128, 32
CW = 256
DFF = 4096
GATE = 3 * D
MLA_COLS = QL + KVL + ROPE
MLA_PAD = 768
NORM_EPS = 1e-6
GN_EPS = 64e-5
ROPE_THETA = 10000.0
LR, B1, B2, EPS, WD, STEP = 0.001, 0.9, 0.999, 1e-08, 0.01, 10

VMEM_LIMIT = 52 * 1024 * 1024
WKV_CHUNK = 64
WKV_CHUNKS_PER_STEP = 4
ATTN_SEGMENTS = 8

BIG = ["w_in", "mla_wq_b", "mla_wkv_b", "mla_w_o", "rwkv_w_o", "conv_w_o", "w_out", "w_up", "w_down"]
MED = ["rwkv_w2", "rwkv_a2", "rwkv_g2", "rwkv_v1", "rwkv_v2", "conv_w"]
ROW_SHARDED = {"w_out", "w_down", "rwkv_v1"}
SMALL = ["attn_norm", "mla_q_a_norm", "mla_kv_a_norm", "mla_q_norm", "mla_k_norm", "rwkv_mu", "rwkv_w0",
         "rwkv_a0", "rwkv_k_k", "rwkv_k_a", "rwkv_r_k", "rwkv_ln_w", "rwkv_ln_b", "rwkv_v_mu", "rwkv_v0",
         "mlp_norm"]
WEIGHTS = ["attn_norm", "w_in", "mla_q_a_norm", "mla_wq_b", "mla_kv_a_norm", "mla_wkv_b", "mla_q_norm",
           "mla_k_norm", "mla_w_o", "rwkv_mu", "rwkv_w0", "rwkv_w2", "rwkv_a0", "rwkv_a2", "rwkv_g2",
           "rwkv_k_k", "rwkv_k_a", "rwkv_r_k", "rwkv_ln_w", "rwkv_ln_b", "rwkv_w_o", "rwkv_v1", "rwkv_v_mu",
           "rwkv_v0", "rwkv_v2", "conv_w", "conv_w_o", "w_out", "mlp_norm", "w_up", "w_down"]


def _cparams(sem=None):
    return pltpu.CompilerParams(dimension_semantics=sem, vmem_limit_bytes=VMEM_LIMIT)


def _pick(dim, pref, align):
    if dim <= pref:
        return dim
    t = (pref // align) * align
    while t >= align:
        if dim % t == 0:
            return t
        t -= align
    return dim


def _bdot(a, b, dims):
    return lax.dot_general(a.astype(bf16), b.astype(bf16), (dims, ((), ())), preferred_element_type=f32)


@jax.custom_vjp
def _mm(a, b):
    return _bdot(a, b, ((1,), (0,)))


def _mm_fwd(a, b):
    return _mm(a, b), (a, b)


def _mm_bwd(res, g):
    a, b = res
    return _bdot(g, b, ((1,), (1,))), _bdot(a, g, ((0,), (0,)))


_mm.defvjp(_mm_fwd, _mm_bwd)


@jax.custom_vjp
def _mm_nt(a, b):
    return _bdot(a, b, ((1,), (1,)))


def _mm_nt_fwd(a, b):
    return _mm_nt(a, b), (a, b)


def _mm_nt_bwd(res, g):
    a, b = res
    return _bdot(g, b, ((1,), (0,))), _bdot(g, a, ((0,), (0,)))


_mm_nt.defvjp(_mm_nt_fwd, _mm_nt_bwd)


_NN, _NT, _TN = ((1,), (0,)), ((1,), (1,)), ((0,), (0,))


def _dg(a, b, dims):
    return lax.dot_general(a, b, (dims, ((), ())), preferred_element_type=f32)


def _bf16_pieces(x, count):
    out, rest = [], x
    for q in range(count):
        piece = rest.astype(bf16)
        out.append(piece)
        if q + 1 < count:
            rest = rest - piece.astype(f32)
    return out


def _dot3(a, b, dims):
    (ah, al), (bh, bl) = _bf16_pieces(a, 2), _bf16_pieces(b, 2)
    return _dg(ah, bh, dims) + (_dg(ah, bl, dims) + _dg(al, bh, dims))


@jax.custom_vjp
def _hdot(a, b):
    return _dot3(a, b, _NN)


@jax.custom_vjp
def _hdot_nt(a, b):
    return _dot3(a, b, _NT)


@jax.custom_vjp
def _hdot_tn(a, b):
    return _dot3(a, b, _TN)


_hdot.defvjp(lambda a, b: (_hdot(a, b), (a, b)), lambda res, g: (_hdot_nt(g, res[1]), _hdot_tn(res[0], g)))
_hdot_nt.defvjp(lambda a, b: (_hdot_nt(a, b), (a, b)), lambda res, g: (_hdot(g, res[1]), _hdot_tn(g, res[0])))
_hdot_tn.defvjp(lambda a, b: (_hdot_tn(a, b), (a, b)), lambda res, g: (_hdot_nt(res[1], g), _hdot(res[0], g)))


_BNN, _BNT, _BTN = ((2,), (1,)), ((2,), (2,)), ((1,), (1,))


def _bdg(a, b, dims):
    return lax.dot_general(a, b, (dims, ((0,), (0,))), preferred_element_type=f32)


def _bdot3(a, b, dims):
    (ah, al), (bh, bl) = _bf16_pieces(a, 2), _bf16_pieces(b, 2)
    return _bdg(ah, bh, dims) + (_bdg(ah, bl, dims) + _bdg(al, bh, dims))


@jax.custom_vjp
def _hbnn(a, b):
    return _bdot3(a, b, _BNN)


@jax.custom_vjp
def _hbnt(a, b):
    return _bdot3(a, b, _BNT)


@jax.custom_vjp
def _hbtn(a, b):
    return _bdot3(a, b, _BTN)


_hbnn.defvjp(lambda a, b: (_hbnn(a, b), (a, b)), lambda res, g: (_hbnt(g, res[1]), _hbtn(res[0], g)))
_hbnt.defvjp(lambda a, b: (_hbnt(a, b), (a, b)), lambda res, g: (_hbnn(g, res[1]), _hbtn(g, res[0])))
_hbtn.defvjp(lambda a, b: (_hbtn(a, b), (a, b)), lambda res, g: (_hbnt(res[1], g), _hbnn(res[0], g)))


@functools.partial(jax.custom_vjp, nondiff_argnums=(2,))
def _exact_bl(m, x, transposed):
    mb = m.astype(bf16)
    hi, mid, lo = _bf16_pieces(x, 3)
    dims = _BTN if transposed else _BNN
    return (_bdg(mb, hi, dims) + _bdg(mb, mid, dims)) + _bdg(mb, lo, dims)


_exact_bl.defvjp(lambda m, x, transposed: (_exact_bl(m, x, transposed), m),
                 lambda transposed, m, g: (jnp.zeros_like(m), _exact_bl(m, g, not transposed)))


@functools.partial(jax.custom_vjp, nondiff_argnums=(2,))
def _exact_l(m, x, transposed):
    mb = m.astype(bf16)
    hi, mid, lo = _bf16_pieces(x, 3)
    dims = _TN if transposed else _NN
    return (_dg(mb, hi, dims) + _dg(mb, mid, dims)) + _dg(mb, lo, dims)


_exact_l.defvjp(lambda m, x, transposed: (_exact_l(m, x, transposed), m),
                lambda transposed, m, g: (jnp.zeros_like(m), _exact_l(m, g, not transposed)))


@functools.partial(jax.custom_vjp, nondiff_argnums=(2,))
def _exact_r(x, m, transposed):
    mb = m.astype(bf16)
    hi, mid, lo = _bf16_pieces(x, 3)
    dims = _NT if transposed else _NN
    return (_dg(hi, mb, dims) + _dg(mid, mb, dims)) + _dg(lo, mb, dims)


_exact_r.defvjp(lambda x, m, transposed: (_exact_r(x, m, transposed), m),
                lambda transposed, m, g: (_exact_r(g, m, not transposed), jnp.zeros_like(m)))


def _rms(x, g, eps=NORM_EPS):
    return x * lax.rsqrt(jnp.mean(x * x, axis=-1, keepdims=True) + eps) * g


def _sigmoid(x):
    return 1.0 / (1.0 + jnp.exp(-x))


def _softplus(x):
    return jnp.maximum(x, 0.0) + jnp.log(1.0 + jnp.exp(-jnp.maximum(x, -x)))


def _lane_split(x, sizes):
    bounds = np.cumsum([0] + list(sizes))

    @jax.custom_vjp
    def split(v):
        return tuple(v[..., int(bounds[q]):int(bounds[q + 1])] for q in range(len(sizes)))

    split.defvjp(lambda v: (split(v), None), lambda _, g: (jnp.concatenate(g, axis=-1),))
    return split(x)


def _row_split(x, sizes):
    bounds = np.cumsum([0] + list(sizes))

    @jax.custom_vjp
    def split(v):
        return tuple(v[..., int(bounds[q]):int(bounds[q + 1]), :] for q in range(len(sizes)))

    split.defvjp(lambda v: (split(v), None), lambda _, g: (jnp.concatenate(g, axis=-2),))
    return split(x)


def _unstack(x):
    @jax.custom_vjp
    def unstack(v):
        return tuple(v[q] for q in range(v.shape[0]))

    unstack.defvjp(lambda v: (unstack(v), None), lambda _, g: (jnp.stack(g, axis=0),))
    return unstack(x)


def _shift_mats(t, k):
    r = lax.broadcasted_iota(jnp.int32, (t, t), 0)
    c = lax.broadcasted_iota(jnp.int32, (t, t), 1)
    inner = (r - c == k).astype(f32)
    r8 = lax.broadcasted_iota(jnp.int32, (t, 8), 0)
    c8 = lax.broadcasted_iota(jnp.int32, (t, 8), 1)
    edge = (c8 - r8 == 8 - k).astype(f32)
    return inner, edge


def _shift(x, halo, k):
    inner, edge = _shift_mats(x.shape[0], k)
    return _exact_l(inner, x, False) + jnp.dot(edge, halo, precision=HI, preferred_element_type=f32)


def mm(a, b, *, name, ta=False, tb=False, a_batched=False, b_batched=False, reduce_batch=False, add=None,
       act_grad=None, relu2_out=False, n_split=1, out_dtype=f32, tm=1024, tn=1024, tk=2048):
    ash, bsh = a.shape[-2:], b.shape[-2:]
    (k_, m_) = ash if ta else ash[::-1]
    (k2_, n_) = bsh[::-1] if tb else bsh
    assert k_ == k2_, (a.shape, b.shape, ta, tb)
    hb = a.shape[0] if a_batched else (b.shape[0] if b_batched else 1)
    batched_out = (a_batched or b_batched) and not reduce_batch
    h_out = hb if batched_out else 1
    h_red = hb if reduce_batch else 1
    tm = _pick(m_, tm, 128)
    tn = _pick(n_ // n_split, tn, 128)
    tk = _pick(k_, tk, 128)
    nm, nn, nk = m_ // tm, n_ // tn, k_ // tk

    def a_map(i, j, ho, hr, kk):
        blk = (kk, i) if ta else (i, kk)
        return ((ho if batched_out else hr),) + blk if a_batched else blk

    def b_map(i, j, ho, hr, kk):
        blk = (j, kk) if tb else (kk, j)
        return ((ho if batched_out else hr),) + blk if b_batched else blk

    a_blk = (tk, tm) if ta else (tm, tk)
    b_blk = (tn, tk) if tb else (tk, tn)
    in_specs = [pl.BlockSpec(((1,) + a_blk) if a_batched else a_blk, a_map),
                pl.BlockSpec(((1,) + b_blk) if b_batched else b_blk, b_map)]
    args = [a, b]
    for extra in (add, act_grad):
        if extra is not None:
            in_specs.append(pl.BlockSpec((tm, tn), lambda i, j, ho, hr, kk: (i, j)))
            args.append(extra)
    if n_split > 1:
        per = n_ // n_split // tn
        if batched_out:
            out_spec = pl.BlockSpec((1, 1, tm, tn), lambda i, j, ho, hr, kk: (j // per, ho, i, j % per))
            out_shape = jax.ShapeDtypeStruct((n_split, hb, m_, n_ // n_split), out_dtype)
        else:
            out_spec = pl.BlockSpec((1, tm, tn), lambda i, j, ho, hr, kk: (j // per, i, j % per))
            out_shape = jax.ShapeDtypeStruct((n_split, m_, n_ // n_split), out_dtype)
    elif batched_out:
        out_spec = pl.BlockSpec((1, tm, tn), lambda i, j, ho, hr, kk: (ho, i, j))
        out_shape = jax.ShapeDtypeStruct((hb, m_, n_), out_dtype)
    else:
        out_spec = pl.BlockSpec((tm, tn), lambda i, j, ho, hr, kk: (i, j))
        out_shape = jax.ShapeDtypeStruct((m_, n_), out_dtype)
    lead = (0,) * (int(batched_out) + int(n_split > 1))
    dims = ((0,) if ta else (1,), (1,) if tb else (0,))
    has_add, has_act = add is not None, act_grad is not None

    def body(*refs):
        a_ref, b_ref = refs[0], refs[1]
        pos = 2
        add_ref = act_ref = None
        if has_add:
            add_ref = refs[pos]
            pos += 1
        if has_act:
            act_ref = refs[pos]
            pos += 1
        o_ref, acc_ref = refs[pos], refs[-1]
        hr, kk = pl.program_id(3), pl.program_id(4)
        first = jnp.logical_and(hr == 0, kk == 0)
        last = jnp.logical_and(hr == h_red - 1, kk == nk - 1)
        av = a_ref[0] if a_batched else a_ref[...]
        bv = b_ref[0] if b_batched else b_ref[...]
        p = _bdot(av, bv, dims)
        single = h_red * nk == 1

        if not single:
            @pl.when(first)
            def _():
                acc_ref[...] = p

            @pl.when(jnp.logical_not(first))
            def _():
                acc_ref[...] += p

        @pl.when(last)
        def _():
            r = p if single else acc_ref[...]
            if has_act:
                r = r * (2.0 * jnp.maximum(act_ref[...], 0.0))
            if has_add:
                r = r + add_ref[...]
            if lead:
                o_ref[lead] = r.astype(out_dtype)
            else:
                o_ref[...] = r.astype(out_dtype)
            if relu2_out:
                refs[pos + 1][...] = jnp.square(jnp.maximum(r, 0.0)).astype(bf16)

    if relu2_out:
        assert not lead
        out_spec = [out_spec, out_spec]
        out_shape = [out_shape, jax.ShapeDtypeStruct(out_shape.shape, bf16)]
    return pl.pallas_call(
        body, name=name, grid=(nm, nn, h_out, h_red, nk), in_specs=in_specs, out_specs=out_spec,
        out_shape=out_shape, scratch_shapes=[pltpu.VMEM((tm, tn), f32)],
        compiler_params=_cparams(("parallel", "parallel", "parallel", "arbitrary", "arbitrary")),
    )(*args)


def _row_spec(arr, tile, idx):
    if arr.ndim == 2:
        return pl.BlockSpec((tile, arr.shape[1]), lambda i: (idx(i), 0))
    return pl.BlockSpec((arr.shape[0], tile, arr.shape[2]), lambda i: (0, idx(i), 0))


def _halo_spec(arr, tile, idx):
    per = tile // 8
    return pl.BlockSpec((8, arr.shape[1]), lambda i: (jnp.maximum(idx(i) * per - 1, 0), 0))


def _full_spec(arr):
    nd = arr.ndim
    return pl.BlockSpec(arr.shape, lambda i: (0,) * nd)


def _load_f32(ref):
    val = ref[...]
    return val.astype(f32) if val.dtype == bf16 else val


def rows_fwd(fn, rows, params, consts, out_shapes, *, tile, name, halos=()):
    s_len = rows[0].shape[-2]
    n = s_len // tile
    nr, nh, npar, nc = len(rows), len(halos), len(params), len(consts)
    ident = lambda i: i
    in_specs = ([_row_spec(r, tile, ident) for r in rows] + [_halo_spec(rows[h], tile, ident) for h in halos]
                + [_full_spec(p) for p in params] + [_full_spec(c) for c in consts])
    out_specs = [_row_spec(o, tile, ident) for o in out_shapes]

    def body(*refs):
        i = pl.program_id(0)
        rv = [_load_f32(r) for r in refs[:nr]]
        keep = (i > 0).astype(f32)
        hv = [r[...] * keep for r in refs[nr:nr + nh]]
        pv = [r[...] for r in refs[nr + nh:nr + nh + npar]]
        cv = [r[...] for r in refs[nr + nh + npar:nr + nh + npar + nc]]
        outs = fn(rv, hv, pv, cv)
        for o_ref, o in zip(refs[nr + nh + npar + nc:], outs):
            o_ref[...] = o.astype(o_ref.dtype)

    return pl.pallas_call(
        body, name=name, grid=(n,), in_specs=in_specs, out_specs=out_specs, out_shape=list(out_shapes),
        compiler_params=_cparams(("arbitrary",)),
    )(*rows, *[rows[h] for h in halos], *params, *consts)


def rows_bwd(fn, rows, params, consts, douts, *, tile, name, halos=(), grad_rows=None, extra=None,
             grad_dtypes=None):
    s_len = rows[0].shape[-2]
    n = s_len // tile
    nr, nh, npar, nc = len(rows), len(halos), len(params), len(consts)
    grad_rows = list(range(nr)) if grad_rows is None else list(grad_rows)
    extra = extra or {}
    assert all(h in grad_rows for h in halos)
    rev = lambda i: n - 1 - i
    dflat = [d for ds in douts for d in ds]
    dcount = [len(ds) for ds in douts]
    eflat = [e for g in grad_rows for e in extra.get(g, [])]
    ecount = [len(extra.get(g, [])) for g in grad_rows]
    in_specs = ([_row_spec(r, tile, rev) for r in rows] + [_halo_spec(rows[h], tile, rev) for h in halos]
                + [_full_spec(p) for p in params] + [_full_spec(c) for c in consts]
                + [_row_spec(d, tile, rev) for d in dflat] + [_row_spec(e, tile, rev) for e in eflat])
    grad_dtypes = [f32] * len(grad_rows) if grad_dtypes is None else list(grad_dtypes)
    assert all(grad_dtypes[q] == f32 for q, g in enumerate(grad_rows) if g in halos)
    out_shapes = ([jax.ShapeDtypeStruct(rows[g].shape, dt) for g, dt in zip(grad_rows, grad_dtypes)]
                  + [jax.ShapeDtypeStruct(p.shape, f32) for p in params])
    out_specs = [_row_spec(rows[g], tile, rev) for g in grad_rows] + [_full_spec(p) for p in params]
    scratch = [pltpu.VMEM((8, rows[h].shape[1]), f32) for h in halos]
    n_in = nr + nh + npar + nc + len(dflat) + len(eflat)
    n_out = len(grad_rows) + npar

    def body(*refs):
        i = pl.program_id(0)
        rv = [_load_f32(r) for r in refs[:nr]]
        keep = (i < n - 1).astype(f32)
        hv = [r[...] * keep for r in refs[nr:nr + nh]]
        pv = [r[...] for r in refs[nr + nh:nr + nh + npar]]
        pos = nr + nh + npar
        cv = [r[...] for r in refs[pos:pos + nc]]
        pos += nc
        dv = []
        for cnt in dcount:
            acc = _load_f32(refs[pos])
            for q in range(1, cnt):
                acc = acc + _load_f32(refs[pos + q])
            dv.append(acc)
            pos += cnt
        ev = []
        for cnt in ecount:
            ev.append([_load_f32(refs[pos + q]) for q in range(cnt)])
            pos += cnt
        out_refs = refs[n_in:n_in + n_out]
        carry_refs = refs[n_in + n_out:]

        def f(gr, gh, gp):
            full = list(rv)
            for g, val in zip(grad_rows, gr):
                full[g] = val
            return tuple(fn(full, gh, gp, cv))

        _, vjp = jax.vjp(f, [rv[g] for g in grad_rows], hv, pv)
        d_rows, d_halos, d_params = vjp(tuple(dv))

        @pl.when(i == 0)
        def _():
            for c_ref in carry_refs:
                c_ref[...] = jnp.zeros_like(c_ref)
            for p_ref in out_refs[len(grad_rows):]:
                p_ref[...] = jnp.zeros_like(p_ref)

        for q, g in enumerate(grad_rows):
            val = d_rows[q]
            for e in ev[q]:
                val = val + e
            out_refs[q][...] = val.astype(out_refs[q].dtype)
            if g in halos:
                hq = list(halos).index(g)
                out_refs[q][tile - 8:tile, :] += carry_refs[hq][...]
                carry_refs[hq][...] = d_halos[hq]
        for p_ref, dp in zip(out_refs[len(grad_rows):], d_params):
            p_ref[...] += dp

    res = pl.pallas_call(
        body, name=name, grid=(n,), in_specs=in_specs, out_specs=out_specs, out_shape=out_shapes,
        scratch_shapes=scratch, compiler_params=_cparams(("arbitrary",)),
    )(*rows, *[rows[h] for h in halos], *params, *consts, *dflat, *eflat)
    return list(res[:len(grad_rows)]), list(res[len(grad_rows):])


def _fn_norm(rows, halos, params, consts):
    return (_rms(rows[0], params[0]),)


def _fn_mla_prep(rows, halos, params, consts):
    cq, ckv, kpe = _lane_split(rows[0], (QL, KVL, MLA_PAD - QL - KVL))
    return _rms(cq, params[0]), _rms(ckv, params[1]), kpe


def _rope(x, cos, sin, rot):
    return x * cos + _exact_r(x, rot, False) * sin


def _fn_qk_post(rows, halos, params, consts):
    q_raw, kn_pad, kpe, cos, sin = rows
    q_norm, k_norm = params
    place, rot = consts
    kpe96 = _exact_r(kpe, place, False)
    qs = [_rope(_rms(qh, q_norm), cos, sin, rot) for qh in _unstack(q_raw)]
    ks = [_rope(_rms(kh + kpe96, k_norm), cos, sin, rot) for kh in _unstack(kn_pad)]
    return jnp.stack(qs, axis=0), jnp.stack(ks, axis=0)


def _seg(x, bd):
    return _exact_r(x, bd, False)


def _make_fn_rwkv_prep(vres):
    def fn(rows, halos, params, consts):
        cols = rows[0]
        bd = consts[0]
        mu, w0, w2, a0, a2, g2, k_k, k_a = params[:8]
        prev = _shift(cols, halos[0], 1)
        c = cols + (prev - cols) * mu
        r, k, v, xw, xa, xg = _lane_split(c, (RW, RW, RW, DL, AL, GL))
        log_w = -_softplus(-(w0 + _mm(jnp.tanh(xw), w2))) - 0.5
        ld = -jnp.exp(log_w)
        a = _sigmoid(a0 + _mm(xa, a2))
        g = _mm(_sigmoid(xg), g2)
        if vres:
            hcur, v_first = rows[1], rows[2]
            v1, v_mu, v0, v2 = params[8:12]
            xv = _mm(hcur, v1)
            xv_prev = _shift(xv, _mm(halos[1], v1), 1)
            xv = xv + (xv_prev - xv) * v_mu
            v = v + (v_first - v) * _sigmoid(v0 + _mm(xv, v2))
        kk = k * k_k
        kk = kk / jnp.maximum(jnp.sqrt(_seg(kk * kk, bd)), 1e-12)
        k2 = k * (1.0 + (a - 1.0) * k_a)
        return r, ld, k2, v, -kk, kk * a, g
    return fn


def _fn_rwkv_post(rows, halos, params, consts):
    y, r, k2, v, g = rows
    ln_w, ln_b, r_k = params
    bd = consts[0]
    mean = _seg(y, bd) * (1.0 / RN)
    d = y - mean
    var = _seg(d * d, bd) * (1.0 / RN)
    yn = d * lax.rsqrt(var + GN_EPS) * ln_w + ln_b
    bonus = _seg(r * k2 * r_k, bd) * v
    return ((yn + bonus) * g,)


def _fn_conv(rows, halos, params, consts):
    cols, halo = rows[0], halos[0]
    w0, w1, w2 = params
    b, c, x = _lane_split(cols, (CW, CW, CW))
    _, ch, xh = _lane_split(halo, (CW, CW, CW))
    u, uh = c * x, ch * xh
    return (b * (w0 * _shift(u, uh, 2) + w1 * _shift(u, uh, 1) + w2 * u),)


def _fn_merge(rows, halos, params, consts):
    gate, o_a, o_b, o_c = rows
    g_a, g_b, g_c = _lane_split(gate, (D, D, D))
    return (_sigmoid(g_a) * o_a + _sigmoid(g_b) * o_b + _sigmoid(g_c) * o_c,)


def _attn_block(q, k, v, q0, diagonal_last):
    tq, kend = q.shape[0], k.shape[0]
    s = _mm_nt(q, k) * (DQK ** -0.5)
    if diagonal_last:
        tri = lax.broadcasted_iota(jnp.int32, (tq, tq), 0) >= lax.broadcasted_iota(jnp.int32, (tq, tq), 1)
        if kend > tq:
            before, diag = _lane_split(s, (kend - tq, tq))
            s = jnp.concatenate([before, jnp.where(tri, diag, -1e30)], axis=-1)
        else:
            s = jnp.where(tri, s, -1e30)
    else:
        row = q0 + lax.broadcasted_iota(jnp.int32, (tq, kend), 0)
        col = lax.broadcasted_iota(jnp.int32, (tq, kend), 1)
        s = jnp.where(row >= col, s, -1e30)
    m = lax.stop_gradient(jnp.max(s, axis=-1, keepdims=True))
    e = jnp.exp(s - m)
    p = e / jnp.sum(e, axis=-1, keepdims=True)
    return _mm(p, v)


def _attn_segments(s_len, tq):
    per = max(1, s_len // tq // ATTN_SEGMENTS)
    return [(first, per, (first + per) * tq) for first in range(0, s_len // tq, per)]


def attn_fwd(q, k, v, *, tq, name):
    h, s_len, _ = q.shape
    outs = []
    for seg, (first, nq, kend) in enumerate(_attn_segments(s_len, tq)):
        def body(q_ref, k_ref, v_ref, o_ref, first=first, nq=nq):
            q0 = (first + pl.program_id(1)) * tq
            o_ref[0] = _attn_block(q_ref[0], k_ref[0], v_ref[0], q0, nq == 1).astype(o_ref.dtype)

        outs.append(pl.pallas_call(
            body, name=f"{name}_{seg}", grid=(h, nq),
            in_specs=[pl.BlockSpec((1, tq, DQK), lambda hh, i, first=first: (hh, first + i, 0)),
                      pl.BlockSpec((1, kend, DQK), lambda hh, i: (hh, 0, 0)),
                      pl.BlockSpec((1, kend, DV), lambda hh, i: (hh, 0, 0))],
            out_specs=pl.BlockSpec((1, tq, DV), lambda hh, i: (hh, i, 0)),
            out_shape=jax.ShapeDtypeStruct((h, nq * tq, DV), bf16),
            compiler_params=_cparams(("parallel", "arbitrary")),
        )(q, k, v))
    return jnp.concatenate(outs, axis=1)


def attn_bwd(q, k, v, do, *, tq, name):
    h, s_len, _ = q.shape
    dqs, dk_acc, dv_acc = [], None, None
    for seg, (first, nq, kend) in reversed(list(enumerate(_attn_segments(s_len, tq)))):
        carried = dk_acc is not None

        def body(*refs, first=first, carried=carried, nq=nq):
            q_ref, k_ref, v_ref, do_ref = refs[:4]
            dq_ref, dk_ref, dv_ref = refs[-3:]
            i = pl.program_id(1)
            _, vjp = jax.vjp(functools.partial(_attn_block, q0=(first + i) * tq, diagonal_last=nq == 1),
                             q_ref[0], k_ref[0], v_ref[0])
            dq, dk, dv = vjp(do_ref[0])
            dq_ref[0] = dq

            @pl.when(i == 0)
            def _():
                dk_ref[0] = dk + refs[4][0] if carried else dk
                dv_ref[0] = dv + refs[5][0] if carried else dv

            @pl.when(i > 0)
            def _():
                dk_ref[0] += dk
                dv_ref[0] += dv

        key_specs = [pl.BlockSpec((1, kend, DQK), lambda hh, i: (hh, 0, 0)),
                     pl.BlockSpec((1, kend, DV), lambda hh, i: (hh, 0, 0))]
        dq, dk_acc, dv_acc = pl.pallas_call(
            body, name=f"{name}_{seg}", grid=(h, nq),
            in_specs=[pl.BlockSpec((1, tq, DQK), lambda hh, i, first=first: (hh, first + i, 0))] + key_specs
            + [pl.BlockSpec((1, tq, DV), lambda hh, i, first=first: (hh, first + i, 0))]
            + (key_specs if carried else []),
            out_specs=[pl.BlockSpec((1, tq, DQK), lambda hh, i: (hh, i, 0))] + key_specs,
            out_shape=[jax.ShapeDtypeStruct((h, nq * tq, DQK), f32), jax.ShapeDtypeStruct((h, s_len, DQK), f32),
                       jax.ShapeDtypeStruct((h, s_len, DV), f32)],
            input_output_aliases={4: 1, 5: 2} if carried else {},
            compiler_params=_cparams(("parallel", "arbitrary")),
        )(q, k, v, do, *([dk_acc, dv_acc] if carried else []))
        dqs.append(dq)
    return jnp.concatenate(dqs[::-1], axis=1), dk_acc, dv_acc


def _wkv_local(r, ld, k, v, a, b):
    nb, c, n = r.shape
    ri = lax.broadcasted_iota(jnp.int32, (c, c), 0)
    ci = lax.broadcasted_iota(jnp.int32, (c, c), 1)
    tri = jnp.broadcast_to((ri >= ci).astype(f32)[None], (nb, c, c))
    cum = _exact_bl(tri, ld, False)
    tot = jnp.sum(ld, axis=1, keepdims=True)
    w_incl, w_excl, w_inv, w_rest = jnp.exp(cum), jnp.exp(cum - ld), jnp.exp(-cum), jnp.exp(tot - cum)
    ab, rb, bb, kb = a * w_excl, r * w_incl, b * w_inv, k * w_inv
    bw, kw = b * w_rest, k * w_rest
    r2 = lax.broadcasted_iota(jnp.int32, (2 * c, 2 * c), 0)
    c2 = lax.broadcasted_iota(jnp.int32, (2 * c, 2 * c), 1)
    t_of, s_of = jnp.where(r2 >= c, r2 - c, r2), jnp.where(c2 >= c, c2 - c, c2)
    keep = jnp.logical_or(t_of > s_of, jnp.logical_and(r2 >= c, t_of == s_of))
    pair = jnp.where(keep[None], _hbnt(jnp.concatenate([ab, rb], axis=1), jnp.concatenate([bb, kb], axis=1)), 0.0)
    on_b, on_k = _lane_split(pair, (c, c))
    l_ab, m_rb = _row_split(on_b, (c, c))
    l_ak_v, m_rk_v = _row_split(_hbnn(on_k, v), (c, c))
    x = jnp.concatenate([ab, l_ak_v], axis=-1)
    lp, span = l_ab, 1
    while span < c:
        x = x + _hbnn(lp, x)
        span *= 2
        if span < c:
            lp = _hbnn(lp, lp)
    via_b_r, via_b_y = _lane_split(_hbnn(m_rb, x), (n, n))
    r_hat = rb + via_b_r
    y0 = via_b_y + m_rk_v
    from_b_g, from_b_z = _row_split(_hbtn(x, bw), (n, n))
    eye = lax.broadcasted_iota(jnp.int32, (n, n), 0) == lax.broadcasted_iota(jnp.int32, (n, n), 1)
    g = jnp.where(eye[None], jnp.exp(tot), 0.0) + from_b_g
    z = from_b_z + _hbtn(v, kw)
    return r_hat, y0, g, z


def _head(h):
    return slice(RN * h, RN * (h + 1))


def _load_chunk_heads(ref, c, per):
    return jnp.stack([ref[c * q:c * (q + 1), _head(h)] for q in range(per) for h in range(RH)], axis=0)


def _store_chunk_heads(ref, val, c, per):
    for q in range(per):
        ref[c * q:c * (q + 1), :] = jnp.concatenate([val[q * RH + h] for h in range(RH)], axis=-1)


def wkv_fwd(r, ld, k, v, a, b, *, name):
    s_len = r.shape[0]
    c = WKV_CHUNK
    n = s_len // c
    per = min(WKV_CHUNKS_PER_STEP, n)
    rows = pl.BlockSpec((c * per, RW), lambda i: (i, 0))
    mats = pl.BlockSpec((per, RH, RN, RN), lambda i: (i, 0, 0, 0))
    rows_t, mats_t = jax.ShapeDtypeStruct((s_len, RW), f32), jax.ShapeDtypeStruct((n, RH, RN, RN), f32)

    def local_body(r_ref, ld_ref, k_ref, v_ref, a_ref, b_ref, rh_ref, y0_ref, g_ref, z_ref):
        r_hat, y0, g, z = _wkv_local(*[_load_chunk_heads(ref, c, per)
                                       for ref in (r_ref, ld_ref, k_ref, v_ref, a_ref, b_ref)])
        _store_chunk_heads(rh_ref, r_hat, c, per)
        _store_chunk_heads(y0_ref, y0, c, per)
        g_ref[...] = g.reshape(per, RH, RN, RN)
        z_ref[...] = z.reshape(per, RH, RN, RN)

    r_hat, y0, g, z = pl.pallas_call(
        local_body, name=name + "_local", grid=(n // per,), in_specs=[rows] * 6, out_specs=[rows, rows, mats, mats],
        out_shape=[rows_t, rows_t, mats_t, mats_t], compiler_params=_cparams(("parallel",)),
    )(r, ld, k, v, a, b)

    def scan_body(g_ref, z_ref, st_ref, s_sc):
        s_sc[...] = jnp.zeros_like(s_sc)

        @pl.loop(0, n)
        def _(i):
            s0 = s_sc[...]
            st_ref[i] = s0
            s_sc[...] = _hbnn(s0, g_ref[i]) + z_ref[i]

    vm = pl.BlockSpec(memory_space=pltpu.VMEM)
    states = pl.pallas_call(
        scan_body, name=name + "_scan", in_specs=[vm, vm], out_specs=vm, out_shape=mats_t,
        scratch_shapes=[pltpu.VMEM((RH, RN, RN), f32)],
        compiler_params=pltpu.CompilerParams(vmem_limit_bytes=VMEM_LIMIT),
    )(g, z)

    def out_body(rh_ref, y0_ref, st_ref, y_ref):
        y = _hbnt(_load_chunk_heads(rh_ref, c, per), st_ref[...].reshape(per * RH, RN, RN))
        _store_chunk_heads(y_ref, y, c, per)
        y_ref[...] += y0_ref[...]

    y = pl.pallas_call(
        out_body, name=name + "_out", grid=(n // per,), in_specs=[rows, rows, mats], out_specs=rows,
        out_shape=rows_t, compiler_params=_cparams(("parallel",)),
    )(r_hat, y0, states)
    return y, dict(r_hat=r_hat, g=g, states=states)


def wkv_bwd(r, ld, k, v, a, b, saved, dy, *, name):
    s_len = r.shape[0]
    c = WKV_CHUNK
    n = s_len // c
    per = min(WKV_CHUNKS_PER_STEP, n)
    rows = pl.BlockSpec((c * per, RW), lambda i: (i, 0))
    mats = pl.BlockSpec((per, RH, RN, RN), lambda i: (i, 0, 0, 0))
    rows_t, mats_t = jax.ShapeDtypeStruct((s_len, RW), f32), jax.ShapeDtypeStruct((n, RH, RN, RN), f32)

    def out_body(dy_ref, rh_ref, st_ref, drh_ref, dsy_ref):
        dyb = _load_chunk_heads(dy_ref, c, per)
        _store_chunk_heads(drh_ref, _hbnn(dyb, st_ref[...].reshape(per * RH, RN, RN)), c, per)
        dsy_ref[...] = _hbtn(dyb, _load_chunk_heads(rh_ref, c, per)).reshape(per, RH, RN, RN)

    d_rhat, ds_y = pl.pallas_call(
        out_body, name=name + "_out", grid=(n // per,), in_specs=[rows, rows, mats], out_specs=[rows, mats],
        out_shape=[rows_t, mats_t], compiler_params=_cparams(("parallel",)),
    )(dy, saved["r_hat"], saved["states"])

    def scan_body(dsy_ref, g_ref, st_ref, dg_ref, dz_ref, ds_sc):
        ds_sc[...] = jnp.zeros_like(ds_sc)

        @pl.loop(0, n)
        def _(i):
            cidx = n - 1 - i
            ds_next = ds_sc[...]
            dz_ref[cidx] = ds_next
            dg_ref[cidx] = _hbtn(st_ref[cidx], ds_next)
            ds_sc[...] = dsy_ref[cidx] + _hbnt(ds_next, g_ref[cidx])

    vm = pl.BlockSpec(memory_space=pltpu.VMEM)
    d_g, d_z = pl.pallas_call(
        scan_body, name=name + "_scan", in_specs=[vm, vm, vm], out_specs=[vm, vm], out_shape=[mats_t, mats_t],
        scratch_shapes=[pltpu.VMEM((RH, RN, RN), f32)],
        compiler_params=pltpu.CompilerParams(vmem_limit_bytes=VMEM_LIMIT),
    )(ds_y, saved["g"], saved["states"])

    def local_body(r_ref, ld_ref, k_ref, v_ref, a_ref, b_ref, drh_ref, dy_ref, dg_ref, dz_ref, *out_refs):
        _, vjp = jax.vjp(_wkv_local, *[_load_chunk_heads(ref, c, per)
                                       for ref in (r_ref, ld_ref, k_ref, v_ref, a_ref, b_ref)])
        grads = vjp((_load_chunk_heads(drh_ref, c, per), _load_chunk_heads(dy_ref, c, per),
                     dg_ref[...].reshape(per * RH, RN, RN), dz_ref[...].reshape(per * RH, RN, RN)))
        for o_ref, val in zip(out_refs, grads):
            _store_chunk_heads(o_ref, val, c, per)

    return pl.pallas_call(
        local_body, name=name + "_local", grid=(n // per,), in_specs=[rows] * 8 + [mats, mats], out_specs=[rows] * 6,
        out_shape=[rows_t] * 6, compiler_params=_cparams(("parallel",)),
    )(r, ld, k, v, a, b, d_rhat, dy, d_g, d_z)


def loss_head(y, target, *, tile, name):
    s_len, d = y.shape
    n = s_len // tile

    def body(y_ref, t_ref, dy_ref, l_ref):
        err = y_ref[...] - t_ref[...]
        dy_ref[...] = err * (1.0 / d)
        part = 0.5 * jnp.sum(jnp.mean(err * err, axis=-1, keepdims=True), axis=0, keepdims=True)

        @pl.when(pl.program_id(0) == 0)
        def _():
            l_ref[...] = jnp.zeros_like(l_ref)

        l_ref[...] += jnp.broadcast_to(part, l_ref.shape)

    bs = pl.BlockSpec((tile, d), lambda i: (i, 0))
    dy, l = pl.pallas_call(
        body, name=name, grid=(n,), in_specs=[bs, bs],
        out_specs=[bs, pl.BlockSpec((8, 128), lambda i: (0, 0))],
        out_shape=[jax.ShapeDtypeStruct((s_len, d), f32), jax.ShapeDtypeStruct((8, 128), f32)],
        compiler_params=_cparams(("arbitrary",)),
    )(y, target)
    return l[0, 0], dy


def adamw(w, g, m, v, *, name):
    gs = g if isinstance(g, (list, tuple)) else [g]
    rows, cols = w.shape
    tile = _pick(rows, max(8, (2 * 1024 * 1024 // (4 * cols)) // 8 * 8), 8)
    c1 = 1.0 - B1 ** STEP
    c2 = 1.0 - B2 ** STEP
    ng = len(gs)

    def body(*refs):
        w_ref, m_ref, v_ref = refs[0], refs[1 + ng], refs[2 + ng]
        g_out, d_out, m_out, v_out = refs[3 + ng:]
        gv = refs[1][...]
        for q in range(1, ng):
            gv = gv + refs[1 + q][...]
        mn = B1 * m_ref[...] + (1.0 - B1) * gv
        vn = B2 * v_ref[...] + (1.0 - B2) * (gv * gv)
        d_out[...] = -LR * ((mn / c1) / (jnp.sqrt(vn / c2) + EPS) + WD * w_ref[...])
        g_out[...] = gv
        m_out[...] = mn
        v_out[...] = vn

    bs = pl.BlockSpec((tile, cols), lambda i: (i, 0))
    return pl.pallas_call(
        body, name=name, grid=(rows // tile,), in_specs=[bs] * (3 + ng), out_specs=[bs] * 4,
        out_shape=[jax.ShapeDtypeStruct((rows, cols), f32)] * 4, compiler_params=_cparams(("parallel",)),
    )(w, *gs, m, v)


def _place():
    return lax.axis_index("x"), lax.axis_index("y"), lax.axis_index("c")


_ANY = pl.BlockSpec(memory_space=pl.ANY)


def _peer_chips(x, y):
    return [(1 - x, y), (x, 1 - y), (1 - x, 1 - y)]


def gather_weights(shards, *, name):
    nk = len(shards)

    def body(*refs):
        srcs, outs = refs[:nk], refs[nk:2 * nk]
        ici_send, ici_recv, d2d_send, d2d_recv = refs[2 * nk + 1:]
        x, y, c = _place()
        me = 2 * x + y
        peers = _peer_chips(x, y)
        pending = []
        for k in range(nk):
            half = srcs[k].shape[0] // 2
            mine = pl.ds(c * half, half)
            for p, (px, py) in enumerate(peers):
                cp = pltpu.make_async_remote_copy(
                    src_ref=srcs[k].at[mine], dst_ref=outs[k].at[me, mine], send_sem=ici_send.at[k, p],
                    recv_sem=ici_recv.at[k, p], device_id=(px, py, c), device_id_type=MESH)
                cp.start()
                pending.append(cp)
        for k in range(nk):
            half = srcs[k].shape[0] // 2
            mine = pl.ds(c * half, half)
            for p, (px, py) in enumerate(peers):
                landed = outs[k].at[2 * px + py, mine]
                pltpu.make_async_remote_copy(
                    src_ref=srcs[k].at[mine], dst_ref=landed, send_sem=ici_send.at[k, p], recv_sem=ici_recv.at[k, p],
                    device_id=(px, py, c), device_id_type=MESH).wait_recv()
                fwd = pltpu.make_async_remote_copy(
                    src_ref=landed, dst_ref=landed, send_sem=d2d_send.at[k, p], recv_sem=d2d_recv.at[k, p],
                    device_id=(x, y, 1 - c), device_id_type=MESH)
                fwd.start()
                pending.append(fwd)
        for k in range(nk):
            half = srcs[k].shape[0] // 2
            other = pl.ds((1 - c) * half, half)
            for p, (px, py) in enumerate(peers):
                theirs = outs[k].at[2 * px + py, other]
                pltpu.make_async_remote_copy(
                    src_ref=theirs, dst_ref=theirs, send_sem=d2d_send.at[k, p], recv_sem=d2d_recv.at[k, p],
                    device_id=(x, y, 1 - c), device_id_type=MESH).wait_recv()
        for cp in pending:
            cp.wait_send()
        refs[2 * nk][...] = jnp.zeros_like(refs[2 * nk])

    sem = lambda *shape: pltpu.SemaphoreType.DMA(shape)
    res = pl.pallas_call(
        body, name=name, in_specs=[_ANY] * nk, out_specs=[_ANY] * nk + [pl.BlockSpec(memory_space=pltpu.VMEM)],
        out_shape=[jax.ShapeDtypeStruct((4,) + s.shape, s.dtype) for s in shards]
        + [jax.ShapeDtypeStruct((8, 128), f32)],
        scratch_shapes=[sem(nk, 3), sem(nk, 3), sem(nk, 3), sem(nk, 3)],
    )(*shards)
    return list(res[:nk]), res[nk]


_HBM = pl.BlockSpec(memory_space=pltpu.HBM)
_SEM = pl.BlockSpec(memory_space=pltpu.SEMAPHORE)


def _ici_half_copies(srcs, lands, send_sems, recv_sems, incoming):
    x, y, c = _place()
    me = 2 * x + y
    out = []
    for k in range(len(srcs)):
        half = srcs[k].shape[0] // 2
        mine = pl.ds(c * half, half)
        for p, (px, py) in enumerate(_peer_chips(x, y)):
            out.append(pltpu.make_async_remote_copy(
                src_ref=srcs[k].at[mine], dst_ref=lands[k].at[(2 * px + py) if incoming else me, mine],
                send_sem=send_sems.at[3 * k + p], recv_sem=recv_sems.at[3 * k + p], device_id=(px, py, c),
                device_id_type=MESH))
    return out


def gather_start(shards, *, name):
    nk = len(shards)

    def body(*refs):
        srcs, lands = refs[:nk], refs[nk:2 * nk]
        send_sems, recv_sems = refs[2 * nk], refs[2 * nk + 1]
        token = refs[-1]
        for outgoing in _ici_half_copies(srcs, lands, send_sems, recv_sems, incoming=False):
            outgoing.start()
        token[...] = jnp.zeros_like(token)

    lands = [pltpu.with_memory_space_constraint(lax.empty((4,) + s.shape, s.dtype), pltpu.HBM) for s in shards]
    res = pl.pallas_call(
        body, name=name,
        out_shape=(pltpu.SemaphoreType.DMA((3 * nk,)), pltpu.SemaphoreType.DMA((3 * nk,)),
                   *[pltpu.HBM(s.shape, s.dtype) for s in shards], *[pltpu.HBM(z.shape, z.dtype) for z in lands],
                   jax.ShapeDtypeStruct((8, 128), f32)),
        in_specs=[_HBM] * (2 * nk), out_specs=(_SEM, _SEM, *[_HBM] * (2 * nk), pl.BlockSpec(memory_space=pltpu.VMEM)),
        input_output_aliases={k: 2 + k for k in range(2 * nk)},
        compiler_params=pltpu.CompilerParams(has_side_effects=pltpu.SideEffectType.DATAFLOW_SIDE_EFFECTING),
    )(*[pltpu.with_memory_space_constraint(s, pltpu.HBM) for s in shards], *lands)
    return res[0], res[1], list(res[2:2 + nk]), list(res[2 + nk:2 + 2 * nk]), res[-1]


def gather_wait(send_sems, recv_sems, shards, lands, after, *, name):
    nk = len(shards)

    def body(*refs):
        srcs, zones = refs[:nk], refs[nk:2 * nk]
        for outgoing in _ici_half_copies(srcs, zones, refs[2 * nk], refs[2 * nk + 1], incoming=False):
            outgoing.wait_send()
        for landed in _ici_half_copies(srcs, zones, refs[2 * nk], refs[2 * nk + 1], incoming=True):
            landed.wait_recv()

    res = pl.pallas_call(
        body, name=name,
        out_shape=(*[pltpu.HBM(s.shape, s.dtype) for s in shards], *[pltpu.HBM(z.shape, z.dtype) for z in lands]),
        in_specs=[_HBM] * (2 * nk) + [_SEM, _SEM, _ANY], out_specs=tuple([_HBM] * (2 * nk)),
        input_output_aliases={k: k for k in range(2 * nk)},
        compiler_params=pltpu.CompilerParams(has_side_effects=pltpu.SideEffectType.DATAFLOW_SIDE_EFFECTING),
    )(*shards, *lands, send_sems, recv_sems, after)
    return list(res[:nk]), list(res[nk:])


def gather_forward(lands, *, name):
    nk = len(lands)

    def body(*refs):
        zones = refs[nk:2 * nk]
        send_sems, recv_sems = refs[2 * nk:]
        x, y, c = _place()
        sends = []
        for k in range(nk):
            half = zones[k].shape[1] // 2
            for p, (px, py) in enumerate(_peer_chips(x, y)):
                landed = zones[k].at[2 * px + py, pl.ds(c * half, half)]
                cp = pltpu.make_async_remote_copy(
                    src_ref=landed, dst_ref=landed, send_sem=send_sems.at[k, p], recv_sem=recv_sems.at[k, p],
                    device_id=(x, y, 1 - c), device_id_type=MESH)
                cp.start()
                sends.append(cp)
        for k in range(nk):
            half = zones[k].shape[1] // 2
            for p, (px, py) in enumerate(_peer_chips(x, y)):
                theirs = zones[k].at[2 * px + py, pl.ds((1 - c) * half, half)]
                pltpu.make_async_remote_copy(
                    src_ref=theirs, dst_ref=theirs, send_sem=send_sems.at[k, p], recv_sem=recv_sems.at[k, p],
                    device_id=(x, y, 1 - c), device_id_type=MESH).wait_recv()
        for cp in sends:
            cp.wait_send()

    return pl.pallas_call(
        body, name=name, in_specs=[_ANY] * nk, out_specs=[_ANY] * nk,
        out_shape=[jax.ShapeDtypeStruct(z.shape, z.dtype) for z in lands],
        input_output_aliases={k: k for k in range(nk)},
        scratch_shapes=[pltpu.SemaphoreType.DMA((nk, 3)), pltpu.SemaphoreType.DMA((nk, 3))],
    )(*lands)


def grads_to_sibling(parts, *, name):
    nk = len(parts)

    def body(*refs):
        srcs, outs = refs[:nk], refs[nk:2 * nk]
        send_sems, recv_sems = refs[2 * nk:]
        x, y, c = _place()
        sends = []
        for k in range(nk):
            half = srcs[k].shape[1] // 2
            cp = pltpu.make_async_remote_copy(
                src_ref=srcs[k].at[:, pl.ds((1 - c) * half, half), :], dst_ref=outs[k], send_sem=send_sems.at[k],
                recv_sem=recv_sems.at[k], device_id=(x, y, 1 - c), device_id_type=MESH)
            cp.start()
            sends.append(cp)
        for cp in sends:
            cp.wait_recv()
        for cp in sends:
            cp.wait_send()

    return pl.pallas_call(
        body, name=name, in_specs=[_ANY] * nk, out_specs=[_ANY] * nk,
        out_shape=[jax.ShapeDtypeStruct((4, p.shape[1] // 2, p.shape[2]), p.dtype) for p in parts],
        scratch_shapes=[pltpu.SemaphoreType.DMA((nk,)), pltpu.SemaphoreType.DMA((nk,))],
    )(*parts)


def pair_sum(part, theirs, core, *, name):
    _, rows, cols = part.shape
    half = rows // 2
    tile = _pick(half, max(16, (1 << 20) // (4 * cols) // 16 * 16), 16)
    per = half // tile

    def body(c_ref, p_ref, t_ref, o_ref):
        o_ref[...] = (p_ref[...].astype(f32) + t_ref[...].astype(f32)).astype(bf16)

    grid_spec = pltpu.PrefetchScalarGridSpec(
        num_scalar_prefetch=1, grid=(4, per),
        in_specs=[pl.BlockSpec((1, tile, cols), lambda j, i, c_ref: (j, c_ref[0] * per + i, 0)),
                  pl.BlockSpec((1, tile, cols), lambda j, i, c_ref: (j, i, 0))],
        out_specs=pl.BlockSpec((1, tile, cols), lambda j, i, c_ref: (j, i, 0)))
    return pl.pallas_call(
        body, name=name, grid_spec=grid_spec, out_shape=jax.ShapeDtypeStruct((4, half, cols), bf16),
        compiler_params=_cparams(("parallel", "parallel")),
    )(core, part, theirs)


def _all_to_all_copies(srcs, lands, send_sems, recv_sems, incoming):
    x, y, c = _place()
    me = 2 * x + y
    out = []
    for k in range(len(srcs)):
        for p, (px, py) in enumerate(_peer_chips(x, y)):
            peer = 2 * px + py
            out.append(pltpu.make_async_remote_copy(
                src_ref=srcs[k].at[peer], dst_ref=lands[k].at[peer if incoming else me],
                send_sem=send_sems.at[3 * k + p], recv_sem=recv_sems.at[3 * k + p], device_id=(px, py, c),
                device_id_type=MESH))
    return out


def scatter_start(parts, *, name):
    nk = len(parts)

    def body(*refs):
        srcs, lands = refs[:nk], refs[nk:2 * nk]
        for outgoing in _all_to_all_copies(srcs, lands, refs[2 * nk], refs[2 * nk + 1], incoming=False):
            outgoing.start()
        refs[-1][...] = jnp.zeros_like(refs[-1])

    lands = [pltpu.with_memory_space_constraint(lax.empty(p.shape, p.dtype), pltpu.HBM) for p in parts]
    res = pl.pallas_call(
        body, name=name,
        out_shape=(pltpu.SemaphoreType.DMA((3 * nk,)), pltpu.SemaphoreType.DMA((3 * nk,)),
                   *[pltpu.HBM(p.shape, p.dtype) for p in parts], *[pltpu.HBM(p.shape, p.dtype) for p in parts],
                   jax.ShapeDtypeStruct((8, 128), f32)),
        in_specs=[_HBM] * (2 * nk), out_specs=(_SEM, _SEM, *[_HBM] * (2 * nk), pl.BlockSpec(memory_space=pltpu.VMEM)),
        input_output_aliases={k: 2 + k for k in range(2 * nk)},
        compiler_params=pltpu.CompilerParams(has_side_effects=pltpu.SideEffectType.DATAFLOW_SIDE_EFFECTING),
    )(*[pltpu.with_memory_space_constraint(p, pltpu.HBM) for p in parts], *lands)
    return res[0], res[1], list(res[2:2 + nk]), list(res[2 + nk:2 + 2 * nk]), res[-1]


def scatter_wait(send_sems, recv_sems, parts, lands, after, *, name):
    nk = len(parts)

    def body(*refs):
        srcs, zones = refs[:nk], refs[nk:2 * nk]
        for outgoing in _all_to_all_copies(srcs, zones, refs[2 * nk], refs[2 * nk + 1], incoming=False):
            outgoing.wait_send()
        for landed in _all_to_all_copies(srcs, zones, refs[2 * nk], refs[2 * nk + 1], incoming=True):
            landed.wait_recv()

    res = pl.pallas_call(
        body, name=name,
        out_shape=(*[pltpu.HBM(p.shape, p.dtype) for p in parts], *[pltpu.HBM(z.shape, z.dtype) for z in lands]),
        in_specs=[_HBM] * (2 * nk) + [_SEM, _SEM, _ANY], out_specs=tuple([_HBM] * (2 * nk)),
        input_output_aliases={k: k for k in range(2 * nk)},
        compiler_params=pltpu.CompilerParams(has_side_effects=pltpu.SideEffectType.DATAFLOW_SIDE_EFFECTING),
    )(*parts, *lands, send_sems, recv_sems, after)
    return list(res[:nk]), list(res[nk:])


def join_halves(bufs, layout, *, name):
    nk, nb = len(layout), len(bufs)

    def body(*refs):
        outs = refs[nb:2 * nb]
        send_sems, recv_sems = refs[2 * nb:]
        x, y, c = _place()
        pending = []
        for k, (o, off, rows) in enumerate(layout):
            half = rows // 2
            mine = outs[o].at[pl.ds(off + c * half, half), :]
            cp = pltpu.make_async_remote_copy(
                src_ref=mine, dst_ref=mine, send_sem=send_sems.at[k], recv_sem=recv_sems.at[k],
                device_id=(x, y, 1 - c), device_id_type=MESH)
            cp.start()
            pending.append(cp)
        for k, (o, off, rows) in enumerate(layout):
            half = rows // 2
            theirs = outs[o].at[pl.ds(off + (1 - c) * half, half), :]
            pltpu.make_async_remote_copy(
                src_ref=theirs, dst_ref=theirs, send_sem=send_sems.at[k], recv_sem=recv_sems.at[k],
                device_id=(x, y, 1 - c), device_id_type=MESH).wait_recv()
        for cp in pending:
            cp.wait_send()

    return pl.pallas_call(
        body, name=name, in_specs=[_ANY] * nb, out_specs=[_ANY] * nb,
        out_shape=[jax.ShapeDtypeStruct(b.shape, b.dtype) for b in bufs],
        input_output_aliases={o: o for o in range(nb)},
        scratch_shapes=[pltpu.SemaphoreType.DMA((nk,)), pltpu.SemaphoreType.DMA((nk,))],
    )(*bufs)


def place_slab(dest, src, index, *, name):
    rows, cols = src.shape
    tile = _pick(rows, max(16, (1 << 20) // (src.dtype.itemsize * cols) // 16 * 16), 16)

    def body(i_ref, s_ref, d_ref, o_ref):
        del i_ref, d_ref
        o_ref[0] = s_ref[...]

    grid_spec = pltpu.PrefetchScalarGridSpec(
        num_scalar_prefetch=1, grid=(rows // tile,),
        in_specs=[pl.BlockSpec((tile, cols), lambda i, idx: (i, 0)), _ANY],
        out_specs=pl.BlockSpec((1, tile, cols), lambda i, idx: (idx[0], i, 0)))
    return pl.pallas_call(
        body, name=name, grid_spec=grid_spec, out_shape=jax.ShapeDtypeStruct(dest.shape, dest.dtype),
        input_output_aliases={2: 0}, compiler_params=_cparams(("parallel",)),
    )(index, src, dest)


def _broadcast_copies(src, land, send_sems, recv_sems, incoming):
    x, y, c = _place()
    me = 4 * x + 2 * y + c
    out = []
    for m in range(1, 8):
        px, py, pc = x ^ (m >> 2), y ^ ((m >> 1) & 1), c ^ (m & 1)
        out.append(pltpu.make_async_remote_copy(
            src_ref=src, dst_ref=land.at[(4 * px + 2 * py + pc) if incoming else me], send_sem=send_sems.at[m - 1],
            recv_sem=recv_sems.at[m - 1], device_id=(px, py, pc), device_id_type=MESH))
    return out


def broadcast_start(src, *, name):
    def body(s_ref, l_ref, send_sems, recv_sems, s_thru, l_thru, token):
        for outgoing in _broadcast_copies(s_ref, l_ref, send_sems, recv_sems, incoming=False):
            outgoing.start()
        token[...] = jnp.zeros_like(token)

    land = pltpu.with_memory_space_constraint(lax.empty((8,) + src.shape, src.dtype), pltpu.HBM)
    return pl.pallas_call(
        body, name=name,
        out_shape=(pltpu.SemaphoreType.DMA((7,)), pltpu.SemaphoreType.DMA((7,)), pltpu.HBM(src.shape, src.dtype),
                   pltpu.HBM(land.shape, land.dtype), jax.ShapeDtypeStruct((8, 128), f32)),
        in_specs=[_HBM, _HBM], out_specs=(_SEM, _SEM, _HBM, _HBM, pl.BlockSpec(memory_space=pltpu.VMEM)),
        input_output_aliases={0: 2, 1: 3},
        compiler_params=pltpu.CompilerParams(has_side_effects=pltpu.SideEffectType.DATAFLOW_SIDE_EFFECTING),
    )(pltpu.with_memory_space_constraint(src, pltpu.HBM), land)


def broadcast_wait(send_sems, recv_sems, src, land, after, *, name):
    def body(s_ref, l_ref, send_sems, recv_sems, after_ref, s_out, l_out):
        for outgoing in _broadcast_copies(s_ref, l_ref, send_sems, recv_sems, incoming=False):
            outgoing.wait_send()
        for landed in _broadcast_copies(s_ref, l_ref, send_sems, recv_sems, incoming=True):
            landed.wait_recv()

    return pl.pallas_call(
        body, name=name, out_shape=(pltpu.HBM(src.shape, src.dtype), pltpu.HBM(land.shape, land.dtype)),
        in_specs=[_HBM, _HBM, _SEM, _SEM, _ANY], out_specs=(_HBM, _HBM), input_output_aliases={0: 0, 1: 1},
        compiler_params=pltpu.CompilerParams(has_side_effects=pltpu.SideEffectType.DATAFLOW_SIDE_EFFECTING),
    )(src, land, send_sems, recv_sems, after)


def sum8(land, own, device, *, name):
    _, rows, cols = land.shape

    def body(d_ref, l_ref, o_ref, out_ref):
        mine = o_ref[...]
        acc = jnp.where(d_ref[0] == 0, mine, l_ref[0])
        for d in range(1, 8):
            acc = acc + jnp.where(d_ref[0] == d, mine, l_ref[d])
        out_ref[...] = acc

    grid_spec = pltpu.PrefetchScalarGridSpec(
        num_scalar_prefetch=1, grid=(1,),
        in_specs=[pl.BlockSpec((8, rows, cols), lambda i, d_ref: (0, 0, 0)),
                  pl.BlockSpec((rows, cols), lambda i, d_ref: (0, 0))],
        out_specs=pl.BlockSpec((rows, cols), lambda i, d_ref: (0, 0)))
    return pl.pallas_call(
        body, name=name, grid_spec=grid_spec, out_shape=jax.ShapeDtypeStruct((rows, cols), f32),
        compiler_params=_cparams(("arbitrary",)),
    )(device, land, own)


def sum4_into(arrived, own, dest, where, *, layer, total_rows, name):
    _, rows, cols = arrived.shape
    tile = _pick(rows, max(16, (1 << 20) // (4 * cols) // 16 * 16), 16)
    per = rows // tile

    def body(w_ref, a_ref, own_ref, *rest):
        mine = own_ref[0].astype(f32)
        p = [jnp.where(w_ref[1] == j, mine, a_ref[j].astype(f32)) for j in range(4)]
        rest[-1][...] = ((p[0] + p[1]) + p[2]) + p[3]

    grid_spec = pltpu.PrefetchScalarGridSpec(
        num_scalar_prefetch=1, grid=(per,),
        in_specs=[pl.BlockSpec((4, tile, cols), lambda i, w_ref: (0, i, 0)),
                  pl.BlockSpec((1, tile, cols), lambda i, w_ref: (w_ref[1], i, 0))] + ([] if dest is None else [_ANY]),
        out_specs=pl.BlockSpec((tile, cols), lambda i, w_ref: ((2 * layer + w_ref[0]) * per + i, 0)))
    return pl.pallas_call(
        body, name=name, grid_spec=grid_spec, out_shape=jax.ShapeDtypeStruct((total_rows, cols), f32),
        input_output_aliases={} if dest is None else {3: 0}, compiler_params=_cparams(("parallel",)),
    )(where, arrived, own, *([] if dest is None else [dest]))


def _consts():
    idx = np.arange(RW)
    bd = (idx[:, None] // RN == idx[None, :] // RN).astype(np.float32)
    place = np.zeros((128, DQK), np.float32)
    place[np.arange(ROPE), NOPE + np.arange(ROPE)] = 1.0
    rot = np.zeros((DQK, DQK), np.float32)
    half = ROPE // 2
    rot[NOPE + half + np.arange(half), NOPE + np.arange(half)] = -1.0
    rot[NOPE + np.arange(half), NOPE + half + np.arange(half)] = 1.0
    return jnp.asarray(bd), jnp.asarray(place), jnp.asarray(rot)


def _rope_tables(positions):
    freqs = ROPE_THETA ** (-(jnp.arange(ROPE // 2, dtype=f32) * 2.0 / ROPE))
    ang = positions.astype(f32)[:, None] * freqs
    cos, sin = jnp.cos(ang), jnp.sin(ang)
    ones = jnp.ones((positions.shape[0], NOPE), f32)
    return (jnp.concatenate([ones, cos, cos], axis=-1), jnp.concatenate([0.0 * ones, sin, sin], axis=-1))


STAGES = (("w_in",), ("mla_wq_b", "mla_wkv_b", "mla_w_o", "rwkv_w_o", "conv_w_o", "w_out"), ("w_up", "w_down"))


def derive_stage(stage, w):
    if stage == 0:
        w_in = w["w_in"]
        pad = jnp.zeros((D, MLA_PAD - MLA_COLS), w_in.dtype)
        return dict(gate=w_in[:, :GATE], mla=jnp.concatenate([w_in[:, GATE:GATE + MLA_COLS], pad], axis=1),
                    rw=w_in[:, GATE + MLA_COLS:GATE + MLA_COLS + 4 * RW], cv=w_in[:, GATE + MLA_COLS + 4 * RW:])
    if stage == 1:
        wkv = w["mla_wkv_b"].reshape(KVL, MLA_H, NOPE + DV)
        wk = jnp.concatenate([wkv[:, :, :NOPE], jnp.zeros((KVL, MLA_H, ROPE), wkv.dtype)], axis=-1)
        return dict(wq=w["mla_wq_b"].reshape(QL, MLA_H, DQK).transpose(1, 0, 2), wk=wk.transpose(1, 0, 2),
                    wv=wkv[:, :, NOPE:].transpose(1, 0, 2), wo=w["mla_w_o"].reshape(MLA_H, DV, D),
                    rwo=w["rwkv_w_o"], cvo=w["conv_w_o"], out=w["w_out"])
    return dict(up=w["w_up"], down=w["w_down"])


W_IN_WINDOW_TILE = (0, 10, 21, 31)
W_IN_WINDOW = 1664
W_IN_SHARD = 1384


def w_in_window_cols(win, chip):
    gap = MLA_PAD - MLA_COLS
    branches = []
    for j in range(4):
        lo, hi = W_IN_SHARD * j, W_IN_SHARD * (j + 1)
        base = 128 * W_IN_WINDOW_TILE[j]
        cut = GATE + MLA_COLS
        if hi <= cut:
            branches.append(lambda w, a=lo - base: w[:, a:a + W_IN_SHARD])
        elif lo >= cut:
            branches.append(lambda w, a=lo + gap - base: w[:, a:a + W_IN_SHARD])
        else:
            branches.append(lambda w, a=lo - base, n1=cut - lo, b=cut + gap - base, n2=hi - cut:
                            jnp.concatenate([w[:, a:a + n1], w[:, b:b + n2]], axis=1))
    return lax.switch(chip, branches, win)


def chip_major_grads(stage, g):
    if stage == 0:
        padded = jnp.concatenate([g["gate"], g["mla"], g["rw"], g["cv"]], axis=1)
        return dict(w_in=jnp.stack([padded[:, 128 * t:128 * t + W_IN_WINDOW] for t in W_IN_WINDOW_TILE]))
    if stage == 1:
        heads = MLA_H // 4
        wq = g["wq"].reshape(4, heads, QL, DQK).transpose(0, 2, 1, 3).reshape(4, QL, heads * DQK)
        wkv = jnp.concatenate([g["wk"][:, :, :NOPE], g["wv"]], axis=-1)
        wkv = wkv.reshape(4, heads, KVL, NOPE + DV).transpose(0, 2, 1, 3).reshape(4, KVL, heads * (NOPE + DV))
        return dict(mla_wq_b=wq, mla_wkv_b=wkv, mla_w_o=g["wo"].reshape(4, MLA_H * DV, D // 4),
                    rwkv_w_o=g["rwo"], conv_w_o=g["cvo"], w_out=g["out"].reshape(4, D // 4, D))
    return dict(w_up=g["up"], w_down=g["down"].reshape(4, DFF // 4, D))


def _row(v):
    return v.reshape(1, -1)


def local_step(x, positions, target, w, sm, big_of=None, on_grads=None, on_small=None):
    if big_of is None:
        big_of = lambda l, stage, _after: {n: w[n][l] for n in STAGES[stage]}
    if on_grads is None:
        on_grads = lambda l, stage, slabs: None
    if on_small is None:
        on_small = lambda l, layer_small: None
    s_len = x.shape[0]
    t_row = _pick(s_len, 256, 8)
    t_wide = _pick(s_len, 128, 8)
    bd, place, rot = _consts()
    cos, sin = _rope_tables(positions)
    sds = lambda *shape: jax.ShapeDtypeStruct(shape, f32)
    sdb = lambda *shape: jax.ShapeDtypeStruct(shape, bf16)
    saved = []
    v_first = None
    for l in range(DEPTH):
        tag = f"l{l}_"
        lw = derive_stage(0, big_of(l, 0, x))
        vres = l > 0
        p_norm1 = [_row(sm["attn_norm"][l])]
        (h,) = rows_fwd(_fn_norm, [x], p_norm1, [], [sds(s_len, D)], tile=t_row, name=tag + "norm1")
        gate = mm(h, lw["gate"], name=tag + "proj_gate")
        mla = mm(h, lw["mla"], name=tag + "proj_mla")
        rwc = mm(h, lw["rw"], name=tag + "proj_rwkv")
        cvc = mm(h, lw["cv"], name=tag + "proj_conv")
        lw.update(derive_stage(1, big_of(l, 1, cvc)))
        p_mla = [_row(sm["mla_q_a_norm"][l]), _row(sm["mla_kv_a_norm"][l])]
        qn, kvn, kpe = rows_fwd(_fn_mla_prep, [mla], p_mla, [], [sdb(s_len, QL), sdb(s_len, KVL), sds(s_len, 128)],
                                tile=t_row, name=tag + "mla_prep")
        q_raw = mm(qn, lw["wq"], b_batched=True, name=tag + "q_proj")
        kn_pad = mm(kvn, lw["wk"], b_batched=True, name=tag + "k_proj")
        vv = mm(kvn, lw["wv"], b_batched=True, name=tag + "v_proj")
        p_qk = [_row(sm["mla_q_norm"][l]), _row(sm["mla_k_norm"][l])]
        q, k = rows_fwd(_fn_qk_post, [q_raw, kn_pad, kpe, cos, sin], p_qk, [place, rot],
                        [sds(MLA_H, s_len, DQK), sds(MLA_H, s_len, DQK)], tile=t_wide, name=tag + "qk_post")
        o = attn_fwd(q, k, vv, tq=_pick(s_len, 256, 8), name=tag + "attn")
        o_a = mm(o, lw["wo"], a_batched=True, b_batched=True, reduce_batch=True, name=tag + "o_a")
        p_rw = [_row(sm["rwkv_mu"][l]), _row(sm["rwkv_w0"][l]), w["rwkv_w2"][l], _row(sm["rwkv_a0"][l]),
                w["rwkv_a2"][l], w["rwkv_g2"][l], _row(sm["rwkv_k_k"][l]), _row(sm["rwkv_k_a"][l])]
        rw_rows, rw_halos = [rwc], (0,)
        if vres:
            p_rw += [w["rwkv_v1"][l - 1], _row(sm["rwkv_v_mu"][l - 1]), _row(sm["rwkv_v0"][l - 1]), w["rwkv_v2"][l - 1]]
            rw_rows, rw_halos = [rwc, h, v_first], (0, 1)
        fn_prep = _make_fn_rwkv_prep(vres)
        r, ld, k2, v, an, bn, g = rows_fwd(fn_prep, rw_rows, p_rw, [bd], [sds(s_len, RW)] * 7, tile=t_row,
                                           name=tag + "rwkv_prep", halos=rw_halos)
        if not vres:
            v_first = v
        y, states = wkv_fwd(r, ld, k2, v, an, bn, name=tag + "wkv")
        p_post = [_row(sm["rwkv_ln_w"][l]), _row(sm["rwkv_ln_b"][l]), _row(sm["rwkv_r_k"][l])]
        (yb,) = rows_fwd(_fn_rwkv_post, [y, r, k2, v, g], p_post, [bd], [sdb(s_len, RW)], tile=t_row,
                         name=tag + "rwkv_post")
        o_b = mm(yb, lw["rwo"], name=tag + "o_b")
        p_cv = [w["conv_w"][l][q:q + 1] for q in range(3)]
        (yc,) = rows_fwd(_fn_conv, [cvc], p_cv, [], [sdb(s_len, CW)], tile=t_row, name=tag + "conv", halos=(0,))
        o_c = mm(yc, lw["cvo"], name=tag + "o_c")
        (merged,) = rows_fwd(_fn_merge, [gate, o_a, o_b, o_c], [], [], [sdb(s_len, D)], tile=t_wide,
                             name=tag + "merge")
        x1 = mm(merged, lw["out"], add=x, name=tag + "out_proj")
        lw.update(derive_stage(2, big_of(l, 2, x1)))
        p_norm2 = [_row(sm["mlp_norm"][l])]
        (h2,) = rows_fwd(_fn_norm, [x1], p_norm2, [], [sdb(s_len, D)], tile=t_row, name=tag + "norm2")
        up, act = mm(h2, lw["up"], relu2_out=True, name=tag + "up")
        x2 = mm(act, lw["down"], add=x1, name=tag + "down")
        saved.append(dict(lw=lw, x=x, h=h, gate=gate, mla=mla, rwc=rwc, cvc=cvc, qn=qn, kvn=kvn, kpe=kpe,
                          q_raw=q_raw, kn_pad=kn_pad, vv=vv, q=q, k=k, o=o, o_a=o_a, r=r, ld=ld, k2=k2, v=v,
                          an=an, bn=bn, g=g, y=y, states=states, yb=yb, o_b=o_b, yc=yc, o_c=o_c, merged=merged,
                          x1=x1, h2=h2, up=up, act=act, p_norm1=p_norm1, p_mla=p_mla, p_qk=p_qk, p_rw=p_rw,
                          p_post=p_post, p_cv=p_cv, p_norm2=p_norm2, rw_rows=rw_rows, rw_halos=rw_halos,
                          fn_prep=fn_prep, v_first=v_first if vres else None))
        x = x2

    loss, dx = loss_head(x, target, tile=t_row, name="loss_head")

    grads = {n: [None] * (DEPTH - 1 if n in ("rwkv_v1", "rwkv_v_mu", "rwkv_v0", "rwkv_v2") else DEPTH)
             for n in WEIGHTS}
    dv_first = None
    for l in reversed(range(DEPTH)):
        tag = f"b{l}_"
        sv = saved[l]
        lw = sv["lw"]
        vres = l > 0
        g_down = mm(sv["act"], dx, ta=True, out_dtype=bf16, name=tag + "g_down")
        dup = mm(dx, lw["down"], tb=True, act_grad=sv["up"], out_dtype=bf16, name=tag + "d_up")
        g_up = mm(sv["h2"], dup, ta=True, n_split=4, out_dtype=bf16, name=tag + "g_up")
        dh2 = mm(dup, lw["up"], tb=True, name=tag + "d_h2")
        slabs = chip_major_grads(2, dict(up=g_up, down=g_down))
        token = on_grads(l, 2, slabs)
        p_norm2 = sv["p_norm2"] if token is None else [sv["p_norm2"][0] + token[0, 0]]
        (dx1,), (g_n2,) = rows_bwd(_fn_norm, [sv["x1"]], p_norm2, [], [[dh2]], tile=t_row,
                                   name=tag + "norm2", extra={0: [dx]})
        g_out = mm(sv["merged"], dx1, ta=True, out_dtype=bf16, name=tag + "g_out")
        dmerged = mm(dx1, lw["out"], tb=True, name=tag + "d_merged")
        (dgate, do_a, do_b, do_c), _ = rows_bwd(_fn_merge, [sv["gate"], sv["o_a"], sv["o_b"], sv["o_c"]], [], [],
                                                [[dmerged]], tile=t_wide, name=tag + "merge",
                                                grad_dtypes=[bf16] * 4)
        g_cvo = mm(sv["yc"], do_c, ta=True, n_split=4, out_dtype=bf16, name=tag + "g_cvo")
        dyc = mm(do_c, lw["cvo"], tb=True, name=tag + "d_yc")
        (dcvc,), g_cw = rows_bwd(_fn_conv, [sv["cvc"]], sv["p_cv"], [], [[dyc]], tile=t_row, name=tag + "conv",
                                    halos=(0,))
        g_rwo = mm(sv["yb"], do_b, ta=True, n_split=4, out_dtype=bf16, name=tag + "g_rwo")
        dyb = mm(do_b, lw["rwo"], tb=True, name=tag + "d_yb")
        (dy, dr_p, dk_p, dv_p, dg), g_post = rows_bwd(
            _fn_rwkv_post, [sv["y"], sv["r"], sv["k2"], sv["v"], sv["g"]], sv["p_post"], [bd], [[dyb]], tile=t_row,
            name=tag + "rwkv_post")
        dr_s, dld, dk_s, dv_s, dan, dbn = wkv_bwd(sv["r"], sv["ld"], sv["k2"], sv["v"], sv["an"], sv["bn"],
                                                  sv["states"], dy, name=tag + "wkv")
        dv_list = [dv_s, dv_p] + ([dv_first] if (not vres and dv_first is not None) else [])
        d_prep, g_prep = rows_bwd(
            sv["fn_prep"], sv["rw_rows"], sv["p_rw"], [bd],
            [[dr_s, dr_p], [dld], [dk_s, dk_p], dv_list, [dan], [dbn], [dg]], tile=t_row, name=tag + "rwkv_prep",
            halos=sv["rw_halos"])
        drwc = d_prep[0]
        dh_extra = []
        if vres:
            dh_extra = [d_prep[1]]
            dv_first = d_prep[2]
        g_wo = mm(sv["o"], do_a, ta=True, a_batched=True, n_split=4, tk=s_len, out_dtype=bf16, name=tag + "g_wo")
        do = mm(do_a, lw["wo"], tb=True, b_batched=True, name=tag + "d_o")
        dq, dk, dvv = attn_bwd(sv["q"], sv["k"], sv["vv"], do, tq=_pick(s_len, 256, 8), name=tag + "attn")
        (dq_raw, dkn_pad, dkpe), g_qk = rows_bwd(
            _fn_qk_post, [sv["q_raw"], sv["kn_pad"], sv["kpe"], cos, sin], sv["p_qk"], [place, rot], [[dq], [dk]],
            tile=t_wide, name=tag + "qk_post", grad_rows=[0, 1, 2], grad_dtypes=[bf16, bf16, f32])
        g_wq = mm(sv["qn"], dq_raw, ta=True, b_batched=True, tk=s_len, out_dtype=bf16, name=tag + "g_wq")
        g_wk = mm(sv["kvn"], dkn_pad, ta=True, b_batched=True, tk=s_len, out_dtype=bf16, name=tag + "g_wk")
        g_wv = mm(sv["kvn"], dvv, ta=True, b_batched=True, tk=s_len, out_dtype=bf16, name=tag + "g_wv")
        dqn = mm(dq_raw, lw["wq"], tb=True, a_batched=True, b_batched=True, reduce_batch=True, name=tag + "d_qn")
        dkvn = mm(dkn_pad, lw["wk"], tb=True, a_batched=True, b_batched=True, reduce_batch=True, name=tag + "d_kvn_k")
        dkvn = mm(dvv, lw["wv"], tb=True, a_batched=True, b_batched=True, reduce_batch=True, add=dkvn,
                  name=tag + "d_kvn_v")
        slabs.update(chip_major_grads(1, dict(wq=g_wq, wk=g_wk, wv=g_wv, wo=g_wo, rwo=g_rwo, cvo=g_cvo, out=g_out)))
        token = on_grads(l, 1, {n: slabs[n] for n in STAGES[1]})
        p_mla = sv["p_mla"] if token is None else [sv["p_mla"][0] + token[0, 0], sv["p_mla"][1]]
        (dmla,), g_mla = rows_bwd(_fn_mla_prep, [sv["mla"]], p_mla, [], [[dqn], [dkvn], [dkpe]], tile=t_row,
                                  name=tag + "mla_prep", grad_dtypes=[bf16])
        g_gate = mm(sv["h"], dgate, ta=True, out_dtype=bf16, name=tag + "g_gate")
        g_mlaw = mm(sv["h"], dmla, ta=True, out_dtype=bf16, name=tag + "g_mla")
        g_rw = mm(sv["h"], drwc, ta=True, out_dtype=bf16, name=tag + "g_rw")
        g_cv = mm(sv["h"], dcvc, ta=True, out_dtype=bf16, name=tag + "g_cv")
        dh = mm(dgate, lw["gate"], tb=True, name=tag + "d_h_gate")
        dh = mm(dmla, lw["mla"], tb=True, add=dh, name=tag + "d_h_mla")
        dh = mm(drwc, lw["rw"], tb=True, add=dh, name=tag + "d_h_rw")
        dh = mm(dcvc, lw["cv"], tb=True, add=dh, name=tag + "d_h_cv")
        (dx,), (g_n1,) = rows_bwd(_fn_norm, [sv["x"]], sv["p_norm1"], [], [[dh] + dh_extra], tile=t_row,
                                  name=tag + "norm1", extra={0: [dx1]})
        slabs.update(chip_major_grads(0, dict(gate=g_gate, mla=g_mlaw, rw=g_rw, cv=g_cv)))
        token = on_grads(l, 0, {n: slabs[n] for n in STAGES[0]})
        if token is not None and l > 0:
            dx = dx + token[0, 0]
        for n, val in slabs.items():
            grads[n][l] = val
        layer_small = [("attn_norm", l, g_n1), ("mlp_norm", l, g_n2), ("mla_q_a_norm", l, g_mla[0]),
                       ("mla_kv_a_norm", l, g_mla[1]), ("mla_q_norm", l, g_qk[0]), ("mla_k_norm", l, g_qk[1]),
                       ("rwkv_ln_w", l, g_post[0]), ("rwkv_ln_b", l, g_post[1]), ("rwkv_r_k", l, g_post[2]),
                       ("conv_w", l, jnp.concatenate(g_cw, axis=0))]
        layer_small += list(zip(["rwkv_mu", "rwkv_w0", "rwkv_w2", "rwkv_a0", "rwkv_a2", "rwkv_g2", "rwkv_k_k",
                                 "rwkv_k_a"], [l] * 8, g_prep[:8]))
        if vres:
            layer_small += list(zip(["rwkv_v1", "rwkv_v_mu", "rwkv_v0", "rwkv_v2"], [l - 1] * 4, g_prep[8:12]))
        for n, index, val in layer_small:
            grads[n][index] = val
        token = on_small(l, layer_small)
        if token is not None and l > 0:
            dx = dx + token[0, 0]
    return loss, dx, grads


def _split3(a):
    hi = a.astype(bf16)
    r1 = a - hi.astype(f32)
    mid = r1.astype(bf16)
    lo = (r1 - mid.astype(f32)).astype(bf16)
    return hi, mid, lo


def _shard_axis(name):
    return 1 if name in ROW_SHARDED else 2


def _pack(pieces, width, dtype, row_align):
    flat = jnp.concatenate([p.reshape(-1).astype(dtype) for p in pieces])
    rows = -(-flat.shape[0] // width)
    rows = -(-rows // row_align) * row_align
    return jnp.pad(flat, (0, rows * width - flat.shape[0])).reshape(rows, width)


def _unpack(flat2d, shapes):
    flat = flat2d.reshape(-1)
    out, off = [], 0
    for shp in shapes:
        n = int(np.prod(shp))
        out.append(flat[off:off + n].reshape(shp))
        off += n
    return out


def kernel(x, positions, attn_norm, w_in, mla_q_a_norm, mla_wq_b, mla_kv_a_norm, mla_wkv_b, mla_q_norm, mla_k_norm, mla_w_o, rwkv_mu, rwkv_w0, rwkv_w2, rwkv_a0, rwkv_a2, rwkv_g2, rwkv_k_k, rwkv_k_a, rwkv_r_k, rwkv_ln_w, rwkv_ln_b, rwkv_w_o, rwkv_v1, rwkv_v_mu, rwkv_v0, rwkv_v2, conv_w, conv_w_o, w_out, mlp_norm, w_up, w_down, loss_target, m_attn_norm, m_w_in, m_mla_q_a_norm, m_mla_wq_b, m_mla_kv_a_norm, m_mla_wkv_b, m_mla_q_norm, m_mla_k_norm, m_mla_w_o, m_rwkv_mu, m_rwkv_w0, m_rwkv_w2, m_rwkv_a0, m_rwkv_a2, m_rwkv_g2, m_rwkv_k_k, m_rwkv_k_a, m_rwkv_r_k, m_rwkv_ln_w, m_rwkv_ln_b, m_rwkv_w_o, m_rwkv_v1, m_rwkv_v_mu, m_rwkv_v0, m_rwkv_v2, m_conv_w, m_conv_w_o, m_w_out, m_mlp_norm, m_w_up, m_w_down, v_attn_norm, v_w_in, v_mla_q_a_norm, v_mla_wq_b, v_mla_kv_a_norm, v_mla_wkv_b, v_mla_q_norm, v_mla_k_norm, v_mla_w_o, v_rwkv_mu, v_rwkv_w0, v_rwkv_w2, v_rwkv_a0, v_rwkv_a2, v_rwkv_g2, v_rwkv_k_k, v_rwkv_k_a, v_rwkv_r_k, v_rwkv_ln_w, v_rwkv_ln_b, v_rwkv_w_o, v_rwkv_v1, v_rwkv_v_mu, v_rwkv_v0, v_rwkv_v2, v_conv_w, v_conv_w_o, v_w_out, v_mlp_norm, v_w_up, v_w_down):
    args = dict(locals())
    wts = {n: args[n] for n in WEIGHTS}
    mom = {n: args["m_" + n] for n in WEIGHTS}
    var = {n: args["v_" + n] for n in WEIGHTS}
    chip = 2 * lax.axis_index("x") + lax.axis_index("y")
    core = lax.axis_index("c").astype(jnp.int32).reshape(1)

    med_names = [n for n in MED if n != "conv_w"]
    med_pieces = [wts[n] for n in med_names] + list(_split3(wts["conv_w"]))
    med_shapes = [p.shape for p in med_pieces]
    chip_idx = chip.astype(jnp.int32).reshape(1)
    shards_first = [wts[n][0].astype(bf16) for n in STAGES[0]] + [_pack(med_pieces, 128, bf16, 32)]
    got_first, token = gather_weights(shards_first, name="gather_l0_s0")
    got_first = [place_slab(g, s, chip_idx, name=f"place_own_l0_s0_{q}")
                 for q, (g, s) in enumerate(zip(got_first, shards_first))]
    in_flight = {}
    for key, names, l in (("l0_s1", STAGES[1], 0), ("l0_s2", STAGES[2], 0), ("l1", BIG, 1)):
        group = [wts[n][l].astype(bf16) for n in names]
        group[0] = group[0] + token[0, 0].astype(bf16)
        in_flight[key] = (names, gather_start(group, name="gather_start_" + key))
        token = in_flight[key][1][4]

    def whole_of(names, slabs):
        out = {}
        for n, by_chip in zip(names, slabs):
            _, rows, cols = by_chip.shape
            if n in ROW_SHARDED:
                out[n] = by_chip.reshape(4 * rows, cols)
            else:
                out[n] = by_chip.transpose(1, 0, 2).reshape(rows, 4 * cols)
        return out

    landed = {}

    def big_of(l, stage, after):
        if (l, stage) == (0, 0):
            return whole_of(STAGES[0], got_first)
        key = "l1" if l == 1 else f"l0_s{stage}"
        if key not in landed:
            names, (send_sems, recv_sems, thru, lands, _) = in_flight[key]
            thru, lands = gather_wait(send_sems, recv_sems, thru, lands, after, name="gather_wait_" + key)
            lands = gather_forward(lands, name="gather_forward_" + key)
            landed[key] = whole_of(names, [place_slab(g, s, chip_idx, name=f"place_own_{key}_{q}")
                                           for q, (g, s) in enumerate(zip(lands, thru))])
        return {n: landed[key][n] for n in STAGES[stage]}

    whole = {}
    per_chip = [_unpack(got_first[len(STAGES[0])][j], med_shapes) for j in range(4)]
    for q, n in enumerate(med_names):
        whole[n] = jnp.concatenate([per_chip[j][q] for j in range(4)], axis=_shard_axis(n)).astype(f32)
    base = len(med_names)
    cw_parts = [jnp.concatenate([per_chip[j][base + t] for j in range(4)], axis=2).astype(f32) for t in range(3)]
    whole["conv_w"] = (cw_parts[0] + cw_parts[1]) + cw_parts[2]
    small = {n: wts[n] for n in SMALL}
    small["rwkv_r_k"] = wts["rwkv_r_k"].reshape(DEPTH, RW)

    exchanges = []

    def on_grads(l, stage, slabs):
        names = STAGES[stage]
        tag = f"l{l}_s{stage}"
        parts = [slabs[n] for n in names]
        from_sibling = grads_to_sibling(parts, name="grads_to_sibling_" + tag)
        chip_sums = [pair_sum(s, t, core, name=f"pair_sum_{n}_{l}") for n, s, t in zip(names, parts, from_sibling)]
        started = scatter_start(chip_sums, name="scatter_start_" + tag)
        exchanges.append((l, names, tag, started))
        return started[4]

    broadcasts = []

    def on_small(l, layer_small):
        values = [val for _, _, val in layer_small]
        started = broadcast_start(_pack(values, 128, f32, 8), name=f"small_start_l{l}")
        broadcasts.append((l, [(n, index, val.shape) for n, index, val in layer_small], started))
        return started[4]

    small["attn_norm"] = small["attn_norm"] + token[0, 0]
    loss, grad_x, grads = local_step(x[0], positions[0], loss_target[0], whole, small, big_of, on_grads, on_small)
    loss = lax.psum(loss, ("x", "y", "c"))

    device = (4 * lax.axis_index("x") + 2 * lax.axis_index("y") + lax.axis_index("c")).astype(jnp.int32).reshape(1)
    summed = {n: [None] * len(grads[n]) for n in SMALL + MED}
    for l, entries, (send_sems, recv_sems, thru, land, _) in broadcasts:
        own, land = broadcast_wait(send_sems, recv_sems, thru, land, grad_x, name=f"small_wait_l{l}")
        total = sum8(land, own, device, name=f"small_sum_l{l}")
        for (n, index, _), val in zip(entries, _unpack(total, [shape for _, _, shape in entries])):
            summed[n][index] = val
    gsum = {}
    for n in SMALL + MED:
        g = jnp.stack(summed[n])
        if n in MED:
            ax = _shard_axis(n)
            width = wts[n].shape[ax]
            g = lax.dynamic_slice_in_dim(g, chip * width, width, axis=ax)
        gsum[n] = g.reshape(wts[n].shape)
    where = jnp.stack([lax.axis_index("c"), chip]).astype(jnp.int32)
    bufs, layout = {}, []
    for l, names, tag, (send_sems, recv_sems, thru, lands, _) in exchanges:
        own, arrived = scatter_wait(send_sems, recv_sems, thru, lands, grad_x, name="scatter_wait_" + tag)
        for n, mine, theirs in zip(names, own, arrived):
            rows = 2 * theirs.shape[1]
            bufs[n] = sum4_into(theirs, mine, bufs.get(n), where, layer=l, total_rows=DEPTH * rows,
                                name=f"sum_chips_{n}_{l}")
            layout.append((BIG.index(n), l * rows, rows))
    reduced = join_halves([bufs[n] for n in BIG], layout, name="join_halves")

    out_g, out_d, out_m, out_v = {}, {}, {}, {}
    for q, n in enumerate(BIG):
        shp = wts[n].shape
        as2d = lambda a: a.reshape(-1, shp[-1])
        g2d = w_in_window_cols(reduced[q], chip) if n == "w_in" else reduced[q]
        res = adamw(as2d(wts[n]), g2d, as2d(mom[n]), as2d(var[n]), name="adamw_" + n)
        out_g[n], out_d[n], out_m[n], out_v[n] = [r.reshape(shp) for r in res]
    sm_all = SMALL + MED
    sm_shapes2 = [wts[n].shape for n in sm_all]
    res = adamw(_pack([wts[n] for n in sm_all], 128, f32, 8), _pack([gsum[n] for n in sm_all], 128, f32, 8),
                _pack([mom[n] for n in sm_all], 128, f32, 8), _pack([var[n] for n in sm_all], 128, f32, 8),
                name="adamw_small")
    for tgt, flat in zip((out_g, out_d, out_m, out_v), res):
        for n, val in zip(sm_all, _unpack(flat, sm_shapes2)):
            tgt[n] = val
    return (loss, grad_x[None], *[out_g[n] for n in WEIGHTS], *[out_d[n] for n in WEIGHTS],
            *[out_m[n] for n in WEIGHTS], *[out_v[n] for n in WEIGHTS])
```

```python
import functools

import jax
import jax.numpy as jnp
import numpy as np
from jax import lax
from jax.experimental import pallas as pl
from jax.experimental.pallas import tpu as pltpu

f32, bf16 = jnp.float32, jnp.bfloat16
HI = lax.Precision.HIGHEST
MESH = pl.DeviceIdType.MESH

D = 1024
DEPTH = 2
MLA_H, NOPE, ROPE, DQK, DV = 8, 64, 32, 96, 64
QL, KVL = 384, 256
RW, RH, RN = 256, 4, 64
DL, AL, GL, MVL = 64, 64, 128, 32
CW = 256
DFF = 4096
GATE = 3 * D
MLA_COLS = QL + KVL + ROPE
MLA_PAD = 768
NORM_EPS = 1e-6
GN_EPS = 64e-5
ROPE_THETA = 10000.0
LR, B1, B2, EPS, WD, STEP = 0.001, 0.9, 0.999, 1e-08, 0.01, 10

VMEM_LIMIT = 52 * 1024 * 1024
WKV_CHUNK = 64
WKV_CHUNKS_PER_STEP = 4
ATTN_SEGMENTS = 4

BIG = ["w_in", "mla_wq_b", "mla_wkv_b", "mla_w_o", "rwkv_w_o", "conv_w_o", "w_out", "w_up", "w_down"]
MED = ["rwkv_w2", "rwkv_a2", "rwkv_g2", "rwkv_v1", "rwkv_v2", "conv_w"]
ROW_SHARDED = {"w_out", "w_down", "rwkv_v1"}
SMALL = ["attn_norm", "mla_q_a_norm", "mla_kv_a_norm", "mla_q_norm", "mla_k_norm", "rwkv_mu", "rwkv_w0",
         "rwkv_a0", "rwkv_k_k", "rwkv_k_a", "rwkv_r_k", "rwkv_ln_w", "rwkv_ln_b", "rwkv_v_mu", "rwkv_v0",
         "mlp_norm"]
WEIGHTS = ["attn_norm", "w_in", "mla_q_a_norm", "mla_wq_b", "mla_kv_a_norm", "mla_wkv_b", "mla_q_norm",
           "mla_k_norm", "mla_w_o", "rwkv_mu", "rwkv_w0", "rwkv_w2", "rwkv_a0", "rwkv_a2", "rwkv_g2",
           "rwkv_k_k", "rwkv_k_a", "rwkv_r_k", "rwkv_ln_w", "rwkv_ln_b", "rwkv_w_o", "rwkv_v1", "rwkv_v_mu",
           "rwkv_v0", "rwkv_v2", "conv_w", "conv_w_o", "w_out", "mlp_norm", "w_up", "w_down"]


def _cparams(sem=None):
    return pltpu.CompilerParams(dimension_semantics=sem, vmem_limit_bytes=VMEM_LIMIT)


def _pick(dim, pref, align):
    if dim <= pref:
        return dim
    t = (pref // align) * align
    while t >= align:
        if dim % t == 0:
            return t
        t -= align
    return dim


def _bdot(a, b, dims):
    return lax.dot_general(a.astype(bf16), b.astype(bf16), (dims, ((), ())), preferred_element_type=f32)


@jax.custom_vjp
def _mm(a, b):
    return _bdot(a, b, ((1,), (0,)))


def _mm_fwd(a, b):
    return _mm(a, b), (a, b)


def _mm_bwd(res, g):
    a, b = res
    return _bdot(g, b, ((1,), (1,))), _bdot(a, g, ((0,), (0,)))


_mm.defvjp(_mm_fwd, _mm_bwd)


@jax.custom_vjp
def _mm_nt(a, b):
    return _bdot(a, b, ((1,), (1,)))


def _mm_nt_fwd(a, b):
    return _mm_nt(a, b), (a, b)


def _mm_nt_bwd(res, g):
    a, b = res
    return _bdot(g, b, ((1,), (0,))), _bdot(g, a, ((0,), (0,)))


_mm_nt.defvjp(_mm_nt_fwd, _mm_nt_bwd)


_NN, _NT, _TN = ((1,), (0,)), ((1,), (1,)), ((0,), (0,))


def _dg(a, b, dims):
    return lax.dot_general(a, b, (dims, ((), ())), preferred_element_type=f32)


def _bf16_pieces(x, count):
    out, rest = [], x
    for q in range(count):
        piece = rest.astype(bf16)
        out.append(piece)
        if q + 1 < count:
            rest = rest - piece.astype(f32)
    return out


def _dot3(a, b, dims):
    (ah, al), (bh, bl) = _bf16_pieces(a, 2), _bf16_pieces(b, 2)
    return _dg(ah, bh, dims) + (_dg(ah, bl, dims) + _dg(al, bh, dims))


@jax.custom_vjp
def _hdot(a, b):
    return _dot3(a, b, _NN)


@jax.custom_vjp
def _hdot_nt(a, b):
    return _dot3(a, b, _NT)


@jax.custom_vjp
def _hdot_tn(a, b):
    return _dot3(a, b, _TN)


_hdot.defvjp(lambda a, b: (_hdot(a, b), (a, b)), lambda res, g: (_hdot_nt(g, res[1]), _hdot_tn(res[0], g)))
_hdot_nt.defvjp(lambda a, b: (_hdot_nt(a, b), (a, b)), lambda res, g: (_hdot(g, res[1]), _hdot_tn(g, res[0])))
_hdot_tn.defvjp(lambda a, b: (_hdot_tn(a, b), (a, b)), lambda res, g: (_hdot_nt(res[1], g), _hdot(res[0], g)))


_BNN, _BNT, _BTN = ((2,), (1,)), ((2,), (2,)), ((1,), (1,))


def _bdg(a, b, dims):
    return lax.dot_general(a, b, (dims, ((0,), (0,))), preferred_element_type=f32)


def _bdot3(a, b, dims):
    (ah, al), (bh, bl) = _bf16_pieces(a, 2), _bf16_pieces(b, 2)
    return _bdg(ah, bh, dims) + (_bdg(ah, bl, dims) + _bdg(al, bh, dims))


@jax.custom_vjp
def _hbnn(a, b):
    return _bdot3(a, b, _BNN)


@jax.custom_vjp
def _hbnt(a, b):
    return _bdot3(a, b, _BNT)


@jax.custom_vjp
def _hbtn(a, b):
    return _bdot3(a, b, _BTN)


_hbnn.defvjp(lambda a, b: (_hbnn(a, b), (a, b)), lambda res, g: (_hbnt(g, res[1]), _hbtn(res[0], g)))
_hbnt.defvjp(lambda a, b: (_hbnt(a, b), (a, b)), lambda res, g: (_hbnn(g, res[1]), _hbtn(g, res[0])))
_hbtn.defvjp(lambda a, b: (_hbtn(a, b), (a, b)), lambda res, g: (_hbnt(res[1], g), _hbnn(res[0], g)))


@functools.partial(jax.custom_vjp, nondiff_argnums=(2,))
def _exact_bl(m, x, transposed):
    mb = m.astype(bf16)
    hi, mid, lo = _bf16_pieces(x, 3)
    dims = _BTN if transposed else _BNN
    return (_bdg(mb, hi, dims) + _bdg(mb, mid, dims)) + _bdg(mb, lo, dims)


_exact_bl.defvjp(lambda m, x, transposed: (_exact_bl(m, x, transposed), m),
                 lambda transposed, m, g: (jnp.zeros_like(m), _exact_bl(m, g, not transposed)))


@functools.partial(jax.custom_vjp, nondiff_argnums=(2,))
def _exact_l(m, x, transposed):
    mb = m.astype(bf16)
    hi, mid, lo = _bf16_pieces(x, 3)
    dims = _TN if transposed else _NN
    return (_dg(mb, hi, dims) + _dg(mb, mid, dims)) + _dg(mb, lo, dims)


_exact_l.defvjp(lambda m, x, transposed: (_exact_l(m, x, transposed), m),
                lambda transposed, m, g: (jnp.zeros_like(m), _exact_l(m, g, not transposed)))


@functools.partial(jax.custom_vjp, nondiff_argnums=(2,))
def _exact_r(x, m, transposed):
    mb = m.astype(bf16)
    hi, mid, lo = _bf16_pieces(x, 3)
    dims = _NT if transposed else _NN
    return (_dg(hi, mb, dims) + _dg(mid, mb, dims)) + _dg(lo, mb, dims)


_exact_r.defvjp(lambda x, m, transposed: (_exact_r(x, m, transposed), m),
                lambda transposed, m, g: (_exact_r(g, m, not transposed), jnp.zeros_like(m)))


def _rms(x, g, eps=NORM_EPS):
    return x * lax.rsqrt(jnp.mean(x * x, axis=-1, keepdims=True) + eps) * g


def _sigmoid(x):
    return 1.0 / (1.0 + jnp.exp(-x))


def _softplus(x):
    return jnp.maximum(x, 0.0) + jnp.log(1.0 + jnp.exp(-jnp.maximum(x, -x)))


def _lane_split(x, sizes):
    bounds = np.cumsum([0] + list(sizes))

    @jax.custom_vjp
    def split(v):
        return tuple(v[..., int(bounds[q]):int(bounds[q + 1])] for q in range(len(sizes)))

    split.defvjp(lambda v: (split(v), None), lambda _, g: (jnp.concatenate(g, axis=-1),))
    return split(x)


def _row_split(x, sizes):
    bounds = np.cumsum([0] + list(sizes))

    @jax.custom_vjp
    def split(v):
        return tuple(v[..., int(bounds[q]):int(bounds[q + 1]), :] for q in range(len(sizes)))

    split.defvjp(lambda v: (split(v), None), lambda _, g: (jnp.concatenate(g, axis=-2),))
    return split(x)


def _unstack(x):
    @jax.custom_vjp
    def unstack(v):
        return tuple(v[q] for q in range(v.shape[0]))

    unstack.defvjp(lambda v: (unstack(v), None), lambda _, g: (jnp.stack(g, axis=0),))
    return unstack(x)


def _shift_mats(t, k):
    r = lax.broadcasted_iota(jnp.int32, (t, t), 0)
    c = lax.broadcasted_iota(jnp.int32, (t, t), 1)
    inner = (r - c == k).astype(f32)
    r8 = lax.broadcasted_iota(jnp.int32, (t, 8), 0)
    c8 = lax.broadcasted_iota(jnp.int32, (t, 8), 1)
    edge = (c8 - r8 == 8 - k).astype(f32)
    return inner, edge


def _shift(x, halo, k):
    inner, edge = _shift_mats(x.shape[0], k)
    return _exact_l(inner, x, False) + jnp.dot(edge, halo, precision=HI, preferred_element_type=f32)


def mm(a, b, *, name, ta=False, tb=False, a_batched=False, b_batched=False, reduce_batch=False, add=None,
       act_grad=None, relu2_out=False, n_split=1, out_dtype=f32, tm=1024, tn=1024, tk=2048):
    ash, bsh = a.shape[-2:], b.shape[-2:]
    (k_, m_) = ash if ta else ash[::-1]
    (k2_, n_) = bsh[::-1] if tb else bsh
    assert k_ == k2_, (a.shape, b.shape, ta, tb)
    hb = a.shape[0] if a_batched else (b.shape[0] if b_batched else 1)
    batched_out = (a_batched or b_batched) and not reduce_batch
    h_out = hb if batched_out else 1
    h_red = hb if reduce_batch else 1
    tm = _pick(m_, tm, 128)
    tn = _pick(n_ // n_split, tn, 128)
    tk = _pick(k_, tk, 128)
    nm, nn, nk = m_ // tm, n_ // tn, k_ // tk

    def a_map(i, j, ho, hr, kk):
        blk = (kk, i) if ta else (i, kk)
        return ((ho if batched_out else hr),) + blk if a_batched else blk

    def b_map(i, j, ho, hr, kk):
        blk = (j, kk) if tb else (kk, j)
        return ((ho if batched_out else hr),) + blk if b_batched else blk

    a_blk = (tk, tm) if ta else (tm, tk)
    b_blk = (tn, tk) if tb else (tk, tn)
    in_specs = [pl.BlockSpec(((1,) + a_blk) if a_batched else a_blk, a_map),
                pl.BlockSpec(((1,) + b_blk) if b_batched else b_blk, b_map)]
    args = [a, b]
    for extra in (add, act_grad):
        if extra is not None:
            in_specs.append(pl.BlockSpec((tm, tn), lambda i, j, ho, hr, kk: (i, j)))
            args.append(extra)
    if n_split > 1:
        per = n_ // n_split // tn
        if batched_out:
            out_spec = pl.BlockSpec((1, 1, tm, tn), lambda i, j, ho, hr, kk: (j // per, ho, i, j % per))
            out_shape = jax.ShapeDtypeStruct((n_split, hb, m_, n_ // n_split), out_dtype)
        else:
            out_spec = pl.BlockSpec((1, tm, tn), lambda i, j, ho, hr, kk: (j // per, i, j % per))
            out_shape = jax.ShapeDtypeStruct((n_split, m_, n_ // n_split), out_dtype)
    elif batched_out:
        out_spec = pl.BlockSpec((1, tm, tn), lambda i, j, ho, hr, kk: (ho, i, j))
        out_shape = jax.ShapeDtypeStruct((hb, m_, n_), out_dtype)
    else:
        out_spec = pl.BlockSpec((tm, tn), lambda i, j, ho, hr, kk: (i, j))
        out_shape = jax.ShapeDtypeStruct((m_, n_), out_dtype)
    lead = (0,) * (int(batched_out) + int(n_split > 1))
    dims = ((0,) if ta else (1,), (1,) if tb else (0,))
    has_add, has_act = add is not None, act_grad is not None

    def body(*refs):
        a_ref, b_ref = refs[0], refs[1]
        pos = 2
        add_ref = act_ref = None
        if has_add:
            add_ref = refs[pos]
            pos += 1
        if has_act:
            act_ref = refs[pos]
            pos += 1
        o_ref, acc_ref = refs[pos], refs[-1]
        hr, kk = pl.program_id(3), pl.program_id(4)
        first = jnp.logical_and(hr == 0, kk == 0)
        last = jnp.logical_and(hr == h_red - 1, kk == nk - 1)
        av = a_ref[0] if a_batched else a_ref[...]
        bv = b_ref[0] if b_batched else b_ref[...]
        p = _bdot(av, bv, dims)
        single = h_red * nk == 1

        if not single:
            @pl.when(first)
            def _():
                acc_ref[...] = p

            @pl.when(jnp.logical_not(first))
            def _():
                acc_ref[...] += p

        @pl.when(last)
        def _():
            r = p if single else acc_ref[...]
            if has_act:
                r = r * (2.0 * jnp.maximum(act_ref[...], 0.0))
            if has_add:
                r = r + add_ref[...]
            if lead:
                o_ref[lead] = r.astype(out_dtype)
            else:
                o_ref[...] = r.astype(out_dtype)
            if relu2_out:
                refs[pos + 1][...] = jnp.square(jnp.maximum(r, 0.0)).astype(bf16)

    if relu2_out:
        assert not lead
        out_spec = [out_spec, out_spec]
        out_shape = [out_shape, jax.ShapeDtypeStruct(out_shape.shape, bf16)]
    return pl.pallas_call(
        body, name=name, grid=(nm, nn, h_out, h_red, nk), in_specs=in_specs, out_specs=out_spec,
        out_shape=out_shape, scratch_shapes=[pltpu.VMEM((tm, tn), f32)],
        compiler_params=_cparams(("parallel", "parallel", "parallel", "arbitrary", "arbitrary")),
    )(*args)


def _row_spec(arr, tile, idx):
    if arr.ndim == 2:
        return pl.BlockSpec((tile, arr.shape[1]), lambda i: (idx(i), 0))
    return pl.BlockSpec((arr.shape[0], tile, arr.shape[2]), lambda i: (0, idx(i), 0))


def _halo_spec(arr, tile, idx):
    per = tile // 8
    return pl.BlockSpec((8, arr.shape[1]), lambda i: (jnp.maximum(idx(i) * per - 1, 0), 0))


def _full_spec(arr):
    nd = arr.ndim
    return pl.BlockSpec(arr.shape, lambda i: (0,) * nd)


def _load_f32(ref):
    val = ref[...]
    return val.astype(f32) if val.dtype == bf16 else val


def rows_fwd(fn, rows, params, consts, out_shapes, *, tile, name, halos=()):
    s_len = rows[0].shape[-2]
    n = s_len // tile
    nr, nh, npar, nc = len(rows), len(halos), len(params), len(consts)
    ident = lambda i: i
    in_specs = ([_row_spec(r, tile, ident) for r in rows] + [_halo_spec(rows[h], tile, ident) for h in halos]
                + [_full_spec(p) for p in params] + [_full_spec(c) for c in consts])
    out_specs = [_row_spec(o, tile, ident) for o in out_shapes]

    def body(*refs):
        i = pl.program_id(0)
        rv = [_load_f32(r) for r in refs[:nr]]
        keep = (i > 0).astype(f32)
        hv = [r[...] * keep for r in refs[nr:nr + nh]]
        pv = [r[...] for r in refs[nr + nh:nr + nh + npar]]
        cv = [r[...] for r in refs[nr + nh + npar:nr + nh + npar + nc]]
        outs = fn(rv, hv, pv, cv)
        for o_ref, o in zip(refs[nr + nh + npar + nc:], outs):
            o_ref[...] = o.astype(o_ref.dtype)

    return pl.pallas_call(
        body, name=name, grid=(n,), in_specs=in_specs, out_specs=out_specs, out_shape=list(out_shapes),
        compiler_params=_cparams(("arbitrary",)),
    )(*rows, *[rows[h] for h in halos], *params, *consts)


def rows_bwd(fn, rows, params, consts, douts, *, tile, name, halos=(), grad_rows=None, extra=None,
             grad_dtypes=None):
    s_len = rows[0].shape[-2]
    n = s_len // tile
    nr, nh, npar, nc = len(rows), len(halos), len(params), len(consts)
    grad_rows = list(range(nr)) if grad_rows is None else list(grad_rows)
    extra = extra or {}
    assert all(h in grad_rows for h in halos)
    rev = lambda i: n - 1 - i
    dflat = [d for ds in douts for d in ds]
    dcount = [len(ds) for ds in douts]
    eflat = [e for g in grad_rows for e in extra.get(g, [])]
    ecount = [len(extra.get(g, [])) for g in grad_rows]
    in_specs = ([_row_spec(r, tile, rev) for r in rows] + [_halo_spec(rows[h], tile, rev) for h in halos]
                + [_full_spec(p) for p in params] + [_full_spec(c) for c in consts]
                + [_row_spec(d, tile, rev) for d in dflat] + [_row_spec(e, tile, rev) for e in eflat])
    grad_dtypes = [f32] * len(grad_rows) if grad_dtypes is None else list(grad_dtypes)
    assert all(grad_dtypes[q] == f32 for q, g in enumerate(grad_rows) if g in halos)
    out_shapes = ([jax.ShapeDtypeStruct(rows[g].shape, dt) for g, dt in zip(grad_rows, grad_dtypes)]
                  + [jax.ShapeDtypeStruct(p.shape, f32) for p in params])
    out_specs = [_row_spec(rows[g], tile, rev) for g in grad_rows] + [_full_spec(p) for p in params]
    scratch = [pltpu.VMEM((8, rows[h].shape[1]), f32) for h in halos]
    n_in = nr + nh + npar + nc + len(dflat) + len(eflat)
    n_out = len(grad_rows) + npar

    def body(*refs):
        i = pl.program_id(0)
        rv = [_load_f32(r) for r in refs[:nr]]
        keep = (i < n - 1).astype(f32)
        hv = [r[...] * keep for r in refs[nr:nr + nh]]
        pv = [r[...] for r in refs[nr + nh:nr + nh + npar]]
        pos = nr + nh + npar
        cv = [r[...] for r in refs[pos:pos + nc]]
        pos += nc
        dv = []
        for cnt in dcount:
            acc = _load_f32(refs[pos])
            for q in range(1, cnt):
                acc = acc + _load_f32(refs[pos + q])
            dv.append(acc)
            pos += cnt
        ev = []
        for cnt in ecount:
            ev.append([_load_f32(refs[pos + q]) for q in range(cnt)])
            pos += cnt
        out_refs = refs[n_in:n_in + n_out]
        carry_refs = refs[n_in + n_out:]

        def f(gr, gh, gp):
            full = list(rv)
            for g, val in zip(grad_rows, gr):
                full[g] = val
            return tuple(fn(full, gh, gp, cv))

        _, vjp = jax.vjp(f, [rv[g] for g in grad_rows], hv, pv)
        d_rows, d_halos, d_params = vjp(tuple(dv))

        @pl.when(i == 0)
        def _():
            for c_ref in carry_refs:
                c_ref[...] = jnp.zeros_like(c_ref)
            for p_ref in out_refs[len(grad_rows):]:
                p_ref[...] = jnp.zeros_like(p_ref)

        for q, g in enumerate(grad_rows):
            val = d_rows[q]
            for e in ev[q]:
                val = val + e
            out_refs[q][...] = val.astype(out_refs[q].dtype)
            if g in halos:
                hq = list(halos).index(g)
                out_refs[q][tile - 8:tile, :] += carry_refs[hq][...]
                carry_refs[hq][...] = d_halos[hq]
        for p_ref, dp in zip(out_refs[len(grad_rows):], d_params):
            p_ref[...] += dp

    res = pl.pallas_call(
        body, name=name, grid=(n,), in_specs=in_specs, out_specs=out_specs, out_shape=out_shapes,
        scratch_shapes=scratch, compiler_params=_cparams(("arbitrary",)),
    )(*rows, *[rows[h] for h in halos], *params, *consts, *dflat, *eflat)
    return list(res[:len(grad_rows)]), list(res[len(grad_rows):])


def _fn_norm(rows, halos, params, consts):
    return (_rms(rows[0], params[0]),)


def _fn_mla_prep(rows, halos, params, consts):
    cq, ckv, kpe = _lane_split(rows[0], (QL, KVL, MLA_PAD - QL - KVL))
    return _rms(cq, params[0]), _rms(ckv, params[1]), kpe


def _rope(x, cos, sin, rot):
    return x * cos + _exact_r(x, rot, False) * sin


def _fn_qk_post(rows, halos, params, consts):
    q_raw, kn_pad, kpe, cos, sin = rows
    q_norm, k_norm = params
    place, rot = consts
    kpe96 = _exact_r(kpe, place, False)
    qs = [_rope(_rms(qh, q_norm), cos, sin, rot) for qh in _unstack(q_raw)]
    ks = [_rope(_rms(kh + kpe96, k_norm), cos, sin, rot) for kh in _unstack(kn_pad)]
    return jnp.stack(qs, axis=0), jnp.stack(ks, axis=0)


def _seg(x, bd):
    return _exact_r(x, bd, False)


def _make_fn_rwkv_prep(vres):
    def fn(rows, halos, params, consts):
        cols = rows[0]
        bd = consts[0]
        mu, w0, w2, a0, a2, g2, k_k, k_a = params[:8]
        prev = _shift(cols, halos[0], 1)
        c = cols + (prev - cols) * mu
        r, k, v, xw, xa, xg = _lane_split(c, (RW, RW, RW, DL, AL, GL))
        log_w = -_softplus(-(w0 + _mm(jnp.tanh(xw), w2))) - 0.5
        ld = -jnp.exp(log_w)
        a = _sigmoid(a0 + _mm(xa, a2))
        g = _mm(_sigmoid(xg), g2)
        if vres:
            hcur, v_first = rows[1], rows[2]
            v1, v_mu, v0, v2 = params[8:12]
            xv = _mm(hcur, v1)
            xv_prev = _shift(xv, _mm(halos[1], v1), 1)
            xv = xv + (xv_prev - xv) * v_mu
            v = v + (v_first - v) * _sigmoid(v0 + _mm(xv, v2))
        kk = k * k_k
        kk = kk / jnp.maximum(jnp.sqrt(_seg(kk * kk, bd)), 1e-12)
        k2 = k * (1.0 + (a - 1.0) * k_a)
        return r, ld, k2, v, -kk, kk * a, g
    return fn


def _fn_rwkv_post(rows, halos, params, consts):
    y, r, k2, v, g = rows
    ln_w, ln_b, r_k = params
    bd = consts[0]
    mean = _seg(y, bd) * (1.0 / RN)
    d = y - mean
    var = _seg(d * d, bd) * (1.0 / RN)
    yn = d * lax.rsqrt(var + GN_EPS) * ln_w + ln_b
    bonus = _seg(r * k2 * r_k, bd) * v
    return ((yn + bonus) * g,)


def _fn_conv(rows, halos, params, consts):
    cols, halo = rows[0], halos[0]
    w0, w1, w2 = params
    b, c, x = _lane_split(cols, (CW, CW, CW))
    _, ch, xh = _lane_split(halo, (CW, CW, CW))
    u, uh = c * x, ch * xh
    return (b * (w0 * _shift(u, uh, 2) + w1 * _shift(u, uh, 1) + w2 * u),)


def _fn_merge(rows, halos, params, consts):
    gate, o_a, o_b, o_c = rows
    g_a, g_b, g_c = _lane_split(gate, (D, D, D))
    return (_sigmoid(g_a) * o_a + _sigmoid(g_b) * o_b + _sigmoid(g_c) * o_c,)


def _attn_block(q, k, v, q0, diagonal_last):
    tq, kend = q.shape[0], k.shape[0]
    s = _mm_nt(q, k) * (DQK ** -0.5)
    if diagonal_last:
        tri = lax.broadcasted_iota(jnp.int32, (tq, tq), 0) >= lax.broadcasted_iota(jnp.int32, (tq, tq), 1)
        if kend > tq:
            before, diag = _lane_split(s, (kend - tq, tq))
            s = jnp.concatenate([before, jnp.where(tri, diag, -1e30)], axis=-1)
        else:
            s = jnp.where(tri, s, -1e30)
    else:
        row = q0 + lax.broadcasted_iota(jnp.int32, (tq, kend), 0)
        col = lax.broadcasted_iota(jnp.int32, (tq, kend), 1)
        s = jnp.where(row >= col, s, -1e30)
    m = lax.stop_gradient(jnp.max(s, axis=-1, keepdims=True))
    e = jnp.exp(s - m)
    p = e / jnp.sum(e, axis=-1, keepdims=True)
    return _mm(p, v)


def _attn_segments(s_len, tq):
    per = max(1, s_len // tq // ATTN_SEGMENTS)
    return [(first, per, (first + per) * tq) for first in range(0, s_len // tq, per)]


def attn_fwd(q, k, v, *, tq, name):
    h, s_len, _ = q.shape
    outs = []
    for seg, (first, nq, kend) in enumerate(_attn_segments(s_len, tq)):
        def body(q_ref, k_ref, v_ref, o_ref, first=first, nq=nq):
            q0 = (first + pl.program_id(1)) * tq
            o_ref[0] = _attn_block(q_ref[0], k_ref[0], v_ref[0], q0, nq == 1).astype(o_ref.dtype)

        outs.append(pl.pallas_call(
            body, name=f"{name}_{seg}", grid=(h, nq),
            in_specs=[pl.BlockSpec((1, tq, DQK), lambda hh, i, first=first: (hh, first + i, 0)),
                      pl.BlockSpec((1, kend, DQK), lambda hh, i: (hh, 0, 0)),
                      pl.BlockSpec((1, kend, DV), lambda hh, i: (hh, 0, 0))],
            out_specs=pl.BlockSpec((1, tq, DV), lambda hh, i: (hh, i, 0)),
            out_shape=jax.ShapeDtypeStruct((h, nq * tq, DV), bf16),
            compiler_params=_cparams(("parallel", "arbitrary")),
        )(q, k, v))
    return jnp.concatenate(outs, axis=1)


def attn_bwd(q, k, v, do, *, tq, name):
    h, s_len, _ = q.shape
    dqs, dk_acc, dv_acc = [], None, None
    for seg, (first, nq, kend) in reversed(list(enumerate(_attn_segments(s_len, tq)))):
        carried = dk_acc is not None

        def body(*refs, first=first, carried=carried, nq=nq):
            q_ref, k_ref, v_ref, do_ref = refs[:4]
            dq_ref, dk_ref, dv_ref = refs[-3:]
            i = pl.program_id(1)
            _, vjp = jax.vjp(functools.partial(_attn_block, q0=(first + i) * tq, diagonal_last=nq == 1),
                             q_ref[0], k_ref[0], v_ref[0])
            dq, dk, dv = vjp(do_ref[0])
            dq_ref[0] = dq

            @pl.when(i == 0)
            def _():
                dk_ref[0] = dk + refs[4][0] if carried else dk
                dv_ref[0] = dv + refs[5][0] if carried else dv

            @pl.when(i > 0)
            def _():
                dk_ref[0] += dk
                dv_ref[0] += dv

        key_specs = [pl.BlockSpec((1, kend, DQK), lambda hh, i: (hh, 0, 0)),
                     pl.BlockSpec((1, kend, DV), lambda hh, i: (hh, 0, 0))]
        dq, dk_acc, dv_acc = pl.pallas_call(
            body, name=f"{name}_{seg}", grid=(h, nq),
            in_specs=[pl.BlockSpec((1, tq, DQK), lambda hh, i, first=first: (hh, first + i, 0))] + key_specs
            + [pl.BlockSpec((1, tq, DV), lambda hh, i, first=first: (hh, first + i, 0))]
            + (key_specs if carried else []),
            out_specs=[pl.BlockSpec((1, tq, DQK), lambda hh, i: (hh, i, 0))] + key_specs,
            out_shape=[jax.ShapeDtypeStruct((h, nq * tq, DQK), f32), jax.ShapeDtypeStruct((h, s_len, DQK), f32),
                       jax.ShapeDtypeStruct((h, s_len, DV), f32)],
            input_output_aliases={4: 1, 5: 2} if carried else {},
            compiler_params=_cparams(("parallel", "arbitrary")),
        )(q, k, v, do, *([dk_acc, dv_acc] if carried else []))
        dqs.append(dq)
    return jnp.concatenate(dqs[::-1], axis=1), dk_acc, dv_acc


def _wkv_local(r, ld, k, v, a, b):
    nb, c, n = r.shape
    ri = lax.broadcasted_iota(jnp.int32, (c, c), 0)
    ci = lax.broadcasted_iota(jnp.int32, (c, c), 1)
    tri = jnp.broadcast_to((ri >= ci).astype(f32)[None], (nb, c, c))
    cum = _exact_bl(tri, ld, False)
    tot = jnp.sum(ld, axis=1, keepdims=True)
    w_incl, w_excl, w_inv, w_rest = jnp.exp(cum), jnp.exp(cum - ld), jnp.exp(-cum), jnp.exp(tot - cum)
    ab, rb, bb, kb = a * w_excl, r * w_incl, b * w_inv, k * w_inv
    bw, kw = b * w_rest, k * w_rest
    r2 = lax.broadcasted_iota(jnp.int32, (2 * c, 2 * c), 0)
    c2 = lax.broadcasted_iota(jnp.int32, (2 * c, 2 * c), 1)
    t_of, s_of = jnp.where(r2 >= c, r2 - c, r2), jnp.where(c2 >= c, c2 - c, c2)
    keep = jnp.logical_or(t_of > s_of, jnp.logical_and(r2 >= c, t_of == s_of))
    pair = jnp.where(keep[None], _hbnt(jnp.concatenate([ab, rb], axis=1), jnp.concatenate([bb, kb], axis=1)), 0.0)
    on_b, on_k = _lane_split(pair, (c, c))
    l_ab, m_rb = _row_split(on_b, (c, c))
    l_ak_v, m_rk_v = _row_split(_hbnn(on_k, v), (c, c))
    x = jnp.concatenate([ab, l_ak_v], axis=-1)
    lp, span = l_ab, 1
    while span < c:
        x = x + _hbnn(lp, x)
        span *= 2
        if span < c:
            lp = _hbnn(lp, lp)
    via_b_r, via_b_y = _lane_split(_hbnn(m_rb, x), (n, n))
    r_hat = rb + via_b_r
    y0 = via_b_y + m_rk_v
    from_b_g, from_b_z = _row_split(_hbtn(x, bw), (n, n))
    eye = lax.broadcasted_iota(jnp.int32, (n, n), 0) == lax.broadcasted_iota(jnp.int32, (n, n), 1)
    g = jnp.where(eye[None], jnp.exp(tot), 0.0) + from_b_g
    z = from_b_z + _hbtn(v, kw)
    return r_hat, y0, g, z


def _head(h):
    return slice(RN * h, RN * (h + 1))


def _load_chunk_heads(ref, c, per):
    return jnp.stack([ref[c * q:c * (q + 1), _head(h)] for q in range(per) for h in range(RH)], axis=0)


def _store_chunk_heads(ref, val, c, per):
    for q in range(per):
        ref[c * q:c * (q + 1), :] = jnp.concatenate([val[q * RH + h] for h in range(RH)], axis=-1)


def wkv_fwd(r, ld, k, v, a, b, *, name):
    s_len = r.shape[0]
    c = WKV_CHUNK
    n = s_len // c
    per = min(WKV_CHUNKS_PER_STEP, n)
    rows = pl.BlockSpec((c * per, RW), lambda i: (i, 0))
    mats = pl.BlockSpec((per, RH, RN, RN), lambda i: (i, 0, 0, 0))
    rows_t, mats_t = jax.ShapeDtypeStruct((s_len, RW), f32), jax.ShapeDtypeStruct((n, RH, RN, RN), f32)

    def local_body(r_ref, ld_ref, k_ref, v_ref, a_ref, b_ref, rh_ref, y0_ref, g_ref, z_ref):
        r_hat, y0, g, z = _wkv_local(*[_load_chunk_heads(ref, c, per)
                                       for ref in (r_ref, ld_ref, k_ref, v_ref, a_ref, b_ref)])
        _store_chunk_heads(rh_ref, r_hat, c, per)
        _store_chunk_heads(y0_ref, y0, c, per)
        g_ref[...] = g.reshape(per, RH, RN, RN)
        z_ref[...] = z.reshape(per, RH, RN, RN)

    r_hat, y0, g, z = pl.pallas_call(
        local_body, name=name + "_local", grid=(n // per,), in_specs=[rows] * 6, out_specs=[rows, rows, mats, mats],
        out_shape=[rows_t, rows_t, mats_t, mats_t], compiler_params=_cparams(("parallel",)),
    )(r, ld, k, v, a, b)

    def scan_body(g_ref, z_ref, st_ref, s_sc):
        s_sc[...] = jnp.zeros_like(s_sc)

        @pl.loop(0, n)
        def _(i):
            s0 = s_sc[...]
            st_ref[i] = s0
            s_sc[...] = _hbnn(s0, g_ref[i]) + z_ref[i]

    vm = pl.BlockSpec(memory_space=pltpu.VMEM)
    states = pl.pallas_call(
        scan_body, name=name + "_scan", in_specs=[vm, vm], out_specs=vm, out_shape=mats_t,
        scratch_shapes=[pltpu.VMEM((RH, RN, RN), f32)],
        compiler_params=pltpu.CompilerParams(vmem_limit_bytes=VMEM_LIMIT),
    )(g, z)

    def out_body(rh_ref, y0_ref, st_ref, y_ref):
        y = _hbnt(_load_chunk_heads(rh_ref, c, per), st_ref[...].reshape(per * RH, RN, RN))
        _store_chunk_heads(y_ref, y, c, per)
        y_ref[...] += y0_ref[...]

    y = pl.pallas_call(
        out_body, name=name + "_out", grid=(n // per,), in_specs=[rows, rows, mats], out_specs=rows,
        out_shape=rows_t, compiler_params=_cparams(("parallel",)),
    )(r_hat, y0, states)
    return y, dict(r_hat=r_hat, g=g, states=states)


def wkv_bwd(r, ld, k, v, a, b, saved, dy, *, name):
    s_len = r.shape[0]
    c = WKV_CHUNK
    n = s_len // c
    per = min(WKV_CHUNKS_PER_STEP, n)
    rows = pl.BlockSpec((c * per, RW), lambda i: (i, 0))
    mats = pl.BlockSpec((per, RH, RN, RN), lambda i: (i, 0, 0, 0))
    rows_t, mats_t = jax.ShapeDtypeStruct((s_len, RW), f32), jax.ShapeDtypeStruct((n, RH, RN, RN), f32)

    def out_body(dy_ref, rh_ref, st_ref, drh_ref, dsy_ref):
        dyb = _load_chunk_heads(dy_ref, c, per)
        _store_chunk_heads(drh_ref, _hbnn(dyb, st_ref[...].reshape(per * RH, RN, RN)), c, per)
        dsy_ref[...] = _hbtn(dyb, _load_chunk_heads(rh_ref, c, per)).reshape(per, RH, RN, RN)

    d_rhat, ds_y = pl.pallas_call(
        out_body, name=name + "_out", grid=(n // per,), in_specs=[rows, rows, mats], out_specs=[rows, mats],
        out_shape=[rows_t, mats_t], compiler_params=_cparams(("parallel",)),
    )(dy, saved["r_hat"], saved["states"])

    def scan_body(dsy_ref, g_ref, st_ref, dg_ref, dz_ref, ds_sc):
        ds_sc[...] = jnp.zeros_like(ds_sc)

        @pl.loop(0, n)
        def _(i):
            cidx = n - 1 - i
            ds_next = ds_sc[...]
            dz_ref[cidx] = ds_next
            dg_ref[cidx] = _hbtn(st_ref[cidx], ds_next)
            ds_sc[...] = dsy_ref[cidx] + _hbnt(ds_next, g_ref[cidx])

    vm = pl.BlockSpec(memory_space=pltpu.VMEM)
    d_g, d_z = pl.pallas_call(
        scan_body, name=name + "_scan", in_specs=[vm, vm, vm], out_specs=[vm, vm], out_shape=[mats_t, mats_t],
        scratch_shapes=[pltpu.VMEM((RH, RN, RN), f32)],
        compiler_params=pltpu.CompilerParams(vmem_limit_bytes=VMEM_LIMIT),
    )(ds_y, saved["g"], saved["states"])

    def local_body(r_ref, ld_ref, k_ref, v_ref, a_ref, b_ref, drh_ref, dy_ref, dg_ref, dz_ref, *out_refs):
        _, vjp = jax.vjp(_wkv_local, *[_load_chunk_heads(ref, c, per)
                                       for ref in (r_ref, ld_ref, k_ref, v_ref, a_ref, b_ref)])
        grads = vjp((_load_chunk_heads(drh_ref, c, per), _load_chunk_heads(dy_ref, c, per),
                     dg_ref[...].reshape(per * RH, RN, RN), dz_ref[...].reshape(per * RH, RN, RN)))
        for o_ref, val in zip(out_refs, grads):
            _store_chunk_heads(o_ref, val, c, per)

    return pl.pallas_call(
        local_body, name=name + "_local", grid=(n // per,), in_specs=[rows] * 8 + [mats, mats], out_specs=[rows] * 6,
        out_shape=[rows_t] * 6, compiler_params=_cparams(("parallel",)),
    )(r, ld, k, v, a, b, d_rhat, dy, d_g, d_z)


def loss_head(y, target, *, tile, name):
    s_len, d = y.shape
    n = s_len // tile

    def body(y_ref, t_ref, dy_ref, l_ref):
        err = y_ref[...] - t_ref[...]
        dy_ref[...] = err * (1.0 / d)
        part = 0.5 * jnp.sum(jnp.mean(err * err, axis=-1, keepdims=True), axis=0, keepdims=True)

        @pl.when(pl.program_id(0) == 0)
        def _():
            l_ref[...] = jnp.zeros_like(l_ref)

        l_ref[...] += jnp.broadcast_to(part, l_ref.shape)

    bs = pl.BlockSpec((tile, d), lambda i: (i, 0))
    dy, l = pl.pallas_call(
        body, name=name, grid=(n,), in_specs=[bs, bs],
        out_specs=[bs, pl.BlockSpec((8, 128), lambda i: (0, 0))],
        out_shape=[jax.ShapeDtypeStruct((s_len, d), f32), jax.ShapeDtypeStruct((8, 128), f32)],
        compiler_params=_cparams(("arbitrary",)),
    )(y, target)
    return l[0, 0], dy


def adamw(w, g, m, v, *, name):
    gs = g if isinstance(g, (list, tuple)) else [g]
    rows, cols = w.shape
    tile = _pick(rows, max(8, (2 * 1024 * 1024 // (4 * cols)) // 8 * 8), 8)
    c1 = 1.0 - B1 ** STEP
    c2 = 1.0 - B2 ** STEP
    ng = len(gs)

    def body(*refs):
        w_ref, m_ref, v_ref = refs[0], refs[1 + ng], refs[2 + ng]
        g_out, d_out, m_out, v_out = refs[3 + ng:]
        gv = refs[1][...]
        for q in range(1, ng):
            gv = gv + refs[1 + q][...]
        mn = B1 * m_ref[...] + (1.0 - B1) * gv
        vn = B2 * v_ref[...] + (1.0 - B2) * (gv * gv)
        d_out[...] = -LR * ((mn / c1) / (jnp.sqrt(vn / c2) + EPS) + WD * w_ref[...])
        g_out[...] = gv
        m_out[...] = mn
        v_out[...] = vn

    bs = pl.BlockSpec((tile, cols), lambda i: (i, 0))
    return pl.pallas_call(
        body, name=name, grid=(rows // tile,), in_specs=[bs] * (3 + ng), out_specs=[bs] * 4,
        out_shape=[jax.ShapeDtypeStruct((rows, cols), f32)] * 4, compiler_params=_cparams(("parallel",)),
    )(w, *gs, m, v)


def _place():
    return lax.axis_index("x"), lax.axis_index("y"), lax.axis_index("c")


_ANY = pl.BlockSpec(memory_space=pl.ANY)


def _peer_chips(x, y):
    return [(1 - x, y), (x, 1 - y), (1 - x, 1 - y)]


def gather_weights(shards, *, name):
    nk = len(shards)

    def body(*refs):
        srcs, outs = refs[:nk], refs[nk:2 * nk]
        ici_send, ici_recv, d2d_send, d2d_recv = refs[2 * nk + 1:]
        x, y, c = _place()
        me = 2 * x + y
        peers = _peer_chips(x, y)
        pending = []
        for k in range(nk):
            half = srcs[k].shape[0] // 2
            mine = pl.ds(c * half, half)
            for p, (px, py) in enumerate(peers):
                cp = pltpu.make_async_remote_copy(
                    src_ref=srcs[k].at[mine], dst_ref=outs[k].at[me, mine], send_sem=ici_send.at[k, p],
                    recv_sem=ici_recv.at[k, p], device_id=(px, py, c), device_id_type=MESH)
                cp.start()
                pending.append(cp)
        for k in range(nk):
            half = srcs[k].shape[0] // 2
            mine = pl.ds(c * half, half)
            for p, (px, py) in enumerate(peers):
                landed = outs[k].at[2 * px + py, mine]
                pltpu.make_async_remote_copy(
                    src_ref=srcs[k].at[mine], dst_ref=landed, send_sem=ici_send.at[k, p], recv_sem=ici_recv.at[k, p],
                    device_id=(px, py, c), device_id_type=MESH).wait_recv()
                fwd = pltpu.make_async_remote_copy(
                    src_ref=landed, dst_ref=landed, send_sem=d2d_send.at[k, p], recv_sem=d2d_recv.at[k, p],
                    device_id=(x, y, 1 - c), device_id_type=MESH)
                fwd.start()
                pending.append(fwd)
        for k in range(nk):
            half = srcs[k].shape[0] // 2
            other = pl.ds((1 - c) * half, half)
            for p, (px, py) in enumerate(peers):
                theirs = outs[k].at[2 * px + py, other]
                pltpu.make_async_remote_copy(
                    src_ref=theirs, dst_ref=theirs, send_sem=d2d_send.at[k, p], recv_sem=d2d_recv.at[k, p],
                    device_id=(x, y, 1 - c), device_id_type=MESH).wait_recv()
        for cp in pending:
            cp.wait_send()
        refs[2 * nk][...] = jnp.zeros_like(refs[2 * nk])

    sem = lambda *shape: pltpu.SemaphoreType.DMA(shape)
    res = pl.pallas_call(
        body, name=name, in_specs=[_ANY] * nk, out_specs=[_ANY] * nk + [pl.BlockSpec(memory_space=pltpu.VMEM)],
        out_shape=[jax.ShapeDtypeStruct((4,) + s.shape, s.dtype) for s in shards]
        + [jax.ShapeDtypeStruct((8, 128), f32)],
        scratch_shapes=[sem(nk, 3), sem(nk, 3), sem(nk, 3), sem(nk, 3)],
    )(*shards)
    return list(res[:nk]), res[nk]


_HBM = pl.BlockSpec(memory_space=pltpu.HBM)
_SEM = pl.BlockSpec(memory_space=pltpu.SEMAPHORE)


def _ici_half_copies(srcs, lands, send_sems, recv_sems, incoming):
    x, y, c = _place()
    me = 2 * x + y
    out = []
    for k in range(len(srcs)):
        half = srcs[k].shape[0] // 2
        mine = pl.ds(c * half, half)
        for p, (px, py) in enumerate(_peer_chips(x, y)):
            out.append(pltpu.make_async_remote_copy(
                src_ref=srcs[k].at[mine], dst_ref=lands[k].at[(2 * px + py) if incoming else me, mine],
                send_sem=send_sems.at[3 * k + p], recv_sem=recv_sems.at[3 * k + p], device_id=(px, py, c),
                device_id_type=MESH))
    return out


def gather_start(shards, *, name):
    nk = len(shards)

    def body(*refs):
        srcs, lands = refs[:nk], refs[nk:2 * nk]
        send_sems, recv_sems = refs[2 * nk], refs[2 * nk + 1]
        token = refs[-1]
        for outgoing in _ici_half_copies(srcs, lands, send_sems, recv_sems, incoming=False):
            outgoing.start()
        token[...] = jnp.zeros_like(token)

    lands = [pltpu.with_memory_space_constraint(lax.empty((4,) + s.shape, s.dtype), pltpu.HBM) for s in shards]
    res = pl.pallas_call(
        body, name=name,
        out_shape=(pltpu.SemaphoreType.DMA((3 * nk,)), pltpu.SemaphoreType.DMA((3 * nk,)),
                   *[pltpu.HBM(s.shape, s.dtype) for s in shards], *[pltpu.HBM(z.shape, z.dtype) for z in lands],
                   jax.ShapeDtypeStruct((8, 128), f32)),
        in_specs=[_HBM] * (2 * nk), out_specs=(_SEM, _SEM, *[_HBM] * (2 * nk), pl.BlockSpec(memory_space=pltpu.VMEM)),
        input_output_aliases={k: 2 + k for k in range(2 * nk)},
        compiler_params=pltpu.CompilerParams(has_side_effects=pltpu.SideEffectType.DATAFLOW_SIDE_EFFECTING),
    )(*[pltpu.with_memory_space_constraint(s, pltpu.HBM) for s in shards], *lands)
    return res[0], res[1], list(res[2:2 + nk]), list(res[2 + nk:2 + 2 * nk]), res[-1]


def gather_wait(send_sems, recv_sems, shards, lands, after, *, name):
    nk = len(shards)

    def body(*refs):
        srcs, zones = refs[:nk], refs[nk:2 * nk]
        for outgoing in _ici_half_copies(srcs, zones, refs[2 * nk], refs[2 * nk + 1], incoming=False):
            outgoing.wait_send()
        for landed in _ici_half_copies(srcs, zones, refs[2 * nk], refs[2 * nk + 1], incoming=True):
            landed.wait_recv()

    res = pl.pallas_call(
        body, name=name,
        out_shape=(*[pltpu.HBM(s.shape, s.dtype) for s in shards], *[pltpu.HBM(z.shape, z.dtype) for z in lands]),
        in_specs=[_HBM] * (2 * nk) + [_SEM, _SEM, _ANY], out_specs=tuple([_HBM] * (2 * nk)),
        input_output_aliases={k: k for k in range(2 * nk)},
        compiler_params=pltpu.CompilerParams(has_side_effects=pltpu.SideEffectType.DATAFLOW_SIDE_EFFECTING),
    )(*shards, *lands, send_sems, recv_sems, after)
    return list(res[:nk]), list(res[nk:])


def gather_forward(lands, *, name):
    nk = len(lands)

    def body(*refs):
        zones = refs[nk:2 * nk]
        send_sems, recv_sems = refs[2 * nk:]
        x, y, c = _place()
        sends = []
        for k in range(nk):
            half = zones[k].shape[1] // 2
            for p, (px, py) in enumerate(_peer_chips(x, y)):
                landed = zones[k].at[2 * px + py, pl.ds(c * half, half)]
                cp = pltpu.make_async_remote_copy(
                    src_ref=landed, dst_ref=landed, send_sem=send_sems.at[k, p], recv_sem=recv_sems.at[k, p],
                    device_id=(x, y, 1 - c), device_id_type=MESH)
                cp.start()
                sends.append(cp)
        for k in range(nk):
            half = zones[k].shape[1] // 2
            for p, (px, py) in enumerate(_peer_chips(x, y)):
                theirs = zones[k].at[2 * px + py, pl.ds((1 - c) * half, half)]
                pltpu.make_async_remote_copy(
                    src_ref=theirs, dst_ref=theirs, send_sem=send_sems.at[k, p], recv_sem=recv_sems.at[k, p],
                    device_id=(x, y, 1 - c), device_id_type=MESH).wait_recv()
        for cp in sends:
            cp.wait_send()

    return pl.pallas_call(
        body, name=name, in_specs=[_ANY] * nk, out_specs=[_ANY] * nk,
        out_shape=[jax.ShapeDtypeStruct(z.shape, z.dtype) for z in lands],
        input_output_aliases={k: k for k in range(nk)},
        scratch_shapes=[pltpu.SemaphoreType.DMA((nk, 3)), pltpu.SemaphoreType.DMA((nk, 3))],
    )(*lands)


def grads_to_sibling(parts, *, name):
    nk = len(parts)

    def body(*refs):
        srcs, outs = refs[:nk], refs[nk:2 * nk]
        send_sems, recv_sems = refs[2 * nk:]
        x, y, c = _place()
        sends = []
        for k in range(nk):
            half = srcs[k].shape[1] // 2
            cp = pltpu.make_async_remote_copy(
                src_ref=srcs[k].at[:, pl.ds((1 - c) * half, half), :], dst_ref=outs[k], send_sem=send_sems.at[k],
                recv_sem=recv_sems.at[k], device_id=(x, y, 1 - c), device_id_type=MESH)
            cp.start()
            sends.append(cp)
        for cp in sends:
            cp.wait_recv()
        for cp in sends:
            cp.wait_send()

    return pl.pallas_call(
        body, name=name, in_specs=[_ANY] * nk, out_specs=[_ANY] * nk,
        out_shape=[jax.ShapeDtypeStruct((4, p.shape[1] // 2, p.shape[2]), p.dtype) for p in parts],
        scratch_shapes=[pltpu.SemaphoreType.DMA((nk,)), pltpu.SemaphoreType.DMA((nk,))],
    )(*parts)


def pair_sum(part, theirs, core, *, name):
    _, rows, cols = part.shape
    half = rows // 2
    tile = _pick(half, max(16, (1 << 20) // (4 * cols) // 16 * 16), 16)
    per = half // tile

    def body(c_ref, p_ref, t_ref, o_ref):
        o_ref[...] = (p_ref[...].astype(f32) + t_ref[...].astype(f32)).astype(bf16)

    grid_spec = pltpu.PrefetchScalarGridSpec(
        num_scalar_prefetch=1, grid=(4, per),
        in_specs=[pl.BlockSpec((1, tile, cols), lambda j, i, c_ref: (j, c_ref[0] * per + i, 0)),
                  pl.BlockSpec((1, tile, cols), lambda j, i, c_ref: (j, i, 0))],
        out_specs=pl.BlockSpec((1, tile, cols), lambda j, i, c_ref: (j, i, 0)))
    return pl.pallas_call(
        body, name=name, grid_spec=grid_spec, out_shape=jax.ShapeDtypeStruct((4, half, cols), bf16),
        compiler_params=_cparams(("parallel", "parallel")),
    )(core, part, theirs)


def _all_to_all_copies(srcs, lands, send_sems, recv_sems, incoming):
    x, y, c = _place()
    me = 2 * x + y
    out = []
    for k in range(len(srcs)):
        for p, (px, py) in enumerate(_peer_chips(x, y)):
            peer = 2 * px + py
            out.append(pltpu.make_async_remote_copy(
                src_ref=srcs[k].at[peer], dst_ref=lands[k].at[peer if incoming else me],
                send_sem=send_sems.at[3 * k + p], recv_sem=recv_sems.at[3 * k + p], device_id=(px, py, c),
                device_id_type=MESH))
    return out


def scatter_start(parts, *, name):
    nk = len(parts)

    def body(*refs):
        srcs, lands = refs[:nk], refs[nk:2 * nk]
        for outgoing in _all_to_all_copies(srcs, lands, refs[2 * nk], refs[2 * nk + 1], incoming=False):
            outgoing.start()
        refs[-1][...] = jnp.zeros_like(refs[-1])

    lands = [pltpu.with_memory_space_constraint(lax.empty(p.shape, p.dtype), pltpu.HBM) for p in parts]
    res = pl.pallas_call(
        body, name=name,
        out_shape=(pltpu.SemaphoreType.DMA((3 * nk,)), pltpu.SemaphoreType.DMA((3 * nk,)),
                   *[pltpu.HBM(p.shape, p.dtype) for p in parts], *[pltpu.HBM(p.shape, p.dtype) for p in parts],
                   jax.ShapeDtypeStruct((8, 128), f32)),
        in_specs=[_HBM] * (2 * nk), out_specs=(_SEM, _SEM, *[_HBM] * (2 * nk), pl.BlockSpec(memory_space=pltpu.VMEM)),
        input_output_aliases={k: 2 + k for k in range(2 * nk)},
        compiler_params=pltpu.CompilerParams(has_side_effects=pltpu.SideEffectType.DATAFLOW_SIDE_EFFECTING),
    )(*[pltpu.with_memory_space_constraint(p, pltpu.HBM) for p in parts], *lands)
    return res[0], res[1], list(res[2:2 + nk]), list(res[2 + nk:2 + 2 * nk]), res[-1]


def scatter_wait(send_sems, recv_sems, parts, lands, after, *, name):
    nk = len(parts)

    def body(*refs):
        srcs, zones = refs[:nk], refs[nk:2 * nk]
        for outgoing in _all_to_all_copies(srcs, zones, refs[2 * nk], refs[2 * nk + 1], incoming=False):
            outgoing.wait_send()
        for landed in _all_to_all_copies(srcs, zones, refs[2 * nk], refs[2 * nk + 1], incoming=True):
            landed.wait_recv()

    res = pl.pallas_call(
        body, name=name,
        out_shape=(*[pltpu.HBM(p.shape, p.dtype) for p in parts], *[pltpu.HBM(z.shape, z.dtype) for z in lands]),
        in_specs=[_HBM] * (2 * nk) + [_SEM, _SEM, _ANY], out_specs=tuple([_HBM] * (2 * nk)),
        input_output_aliases={k: k for k in range(2 * nk)},
        compiler_params=pltpu.CompilerParams(has_side_effects=pltpu.SideEffectType.DATAFLOW_SIDE_EFFECTING),
    )(*parts, *lands, send_sems, recv_sems, after)
    return list(res[:nk]), list(res[nk:])


def join_halves(bufs, layout, *, name):
    nk, nb = len(layout), len(bufs)

    def body(*refs):
        outs = refs[nb:2 * nb]
        send_sems, recv_sems = refs[2 * nb:]
        x, y, c = _place()
        pending = []
        for k, (o, off, rows) in enumerate(layout):
            half = rows // 2
            mine = outs[o].at[pl.ds(off + c * half, half), :]
            cp = pltpu.make_async_remote_copy(
                src_ref=mine, dst_ref=mine, send_sem=send_sems.at[k], recv_sem=recv_sems.at[k],
                device_id=(x, y, 1 - c), device_id_type=MESH)
            cp.start()
            pending.append(cp)
        for k, (o, off, rows) in enumerate(layout):
            half = rows // 2
            theirs = outs[o].at[pl.ds(off + (1 - c) * half, half), :]
            pltpu.make_async_remote_copy(
                src_ref=theirs, dst_ref=theirs, send_sem=send_sems.at[k], recv_sem=recv_sems.at[k],
                device_id=(x, y, 1 - c), device_id_type=MESH).wait_recv()
        for cp in pending:
            cp.wait_send()

    return pl.pallas_call(
        body, name=name, in_specs=[_ANY] * nb, out_specs=[_ANY] * nb,
        out_shape=[jax.ShapeDtypeStruct(b.shape, b.dtype) for b in bufs],
        input_output_aliases={o: o for o in range(nb)},
        scratch_shapes=[pltpu.SemaphoreType.DMA((nk,)), pltpu.SemaphoreType.DMA((nk,))],
    )(*bufs)


def place_slab(dest, src, index, *, name):
    rows, cols = src.shape
    tile = _pick(rows, max(16, (1 << 20) // (src.dtype.itemsize * cols) // 16 * 16), 16)

    def body(i_ref, s_ref, d_ref, o_ref):
        del i_ref, d_ref
        o_ref[0] = s_ref[...]

    grid_spec = pltpu.PrefetchScalarGridSpec(
        num_scalar_prefetch=1, grid=(rows // tile,),
        in_specs=[pl.BlockSpec((tile, cols), lambda i, idx: (i, 0)), _ANY],
        out_specs=pl.BlockSpec((1, tile, cols), lambda i, idx: (idx[0], i, 0)))
    return pl.pallas_call(
        body, name=name, grid_spec=grid_spec, out_shape=jax.ShapeDtypeStruct(dest.shape, dest.dtype),
        input_output_aliases={2: 0}, compiler_params=_cparams(("parallel",)),
    )(index, src, dest)


def _broadcast_copies(src, land, send_sems, recv_sems, incoming):
    x, y, c = _place()
    me = 4 * x + 2 * y + c
    out = []
    for m in range(1, 8):
        px, py, pc = x ^ (m >> 2), y ^ ((m >> 1) & 1), c ^ (m & 1)
        out.append(pltpu.make_async_remote_copy(
            src_ref=src, dst_ref=land.at[(4 * px + 2 * py + pc) if incoming else me], send_sem=send_sems.at[m - 1],
            recv_sem=recv_sems.at[m - 1], device_id=(px, py, pc), device_id_type=MESH))
    return out


def broadcast_start(src, *, name):
    def body(s_ref, l_ref, send_sems, recv_sems, s_thru, l_thru, token):
        for outgoing in _broadcast_copies(s_ref, l_ref, send_sems, recv_sems, incoming=False):
            outgoing.start()
        token[...] = jnp.zeros_like(token)

    land = pltpu.with_memory_space_constraint(lax.empty((8,) + src.shape, src.dtype), pltpu.HBM)
    return pl.pallas_call(
        body, name=name,
        out_shape=(pltpu.SemaphoreType.DMA((7,)), pltpu.SemaphoreType.DMA((7,)), pltpu.HBM(src.shape, src.dtype),
                   pltpu.HBM(land.shape, land.dtype), jax.ShapeDtypeStruct((8, 128), f32)),
        in_specs=[_HBM, _HBM], out_specs=(_SEM, _SEM, _HBM, _HBM, pl.BlockSpec(memory_space=pltpu.VMEM)),
        input_output_aliases={0: 2, 1: 3},
        compiler_params=pltpu.CompilerParams(has_side_effects=pltpu.SideEffectType.DATAFLOW_SIDE_EFFECTING),
    )(pltpu.with_memory_space_constraint(src, pltpu.HBM), land)


def broadcast_wait(send_sems, recv_sems, src, land, after, *, name):
    def body(s_ref, l_ref, send_sems, recv_sems, after_ref, s_out, l_out):
        for outgoing in _broadcast_copies(s_ref, l_ref, send_sems, recv_sems, incoming=False):
            outgoing.wait_send()
        for landed in _broadcast_copies(s_ref, l_ref, send_sems, recv_sems, incoming=True):
            landed.wait_recv()

    return pl.pallas_call(
        body, name=name, out_shape=(pltpu.HBM(src.shape, src.dtype), pltpu.HBM(land.shape, land.dtype)),
        in_specs=[_HBM, _HBM, _SEM, _SEM, _ANY], out_specs=(_HBM, _HBM), input_output_aliases={0: 0, 1: 1},
        compiler_params=pltpu.CompilerParams(has_side_effects=pltpu.SideEffectType.DATAFLOW_SIDE_EFFECTING),
    )(src, land, send_sems, recv_sems, after)


def sum8(land, own, device, *, name):
    _, rows, cols = land.shape

    def body(d_ref, l_ref, o_ref, out_ref):
        mine = o_ref[...]
        acc = jnp.where(d_ref[0] == 0, mine, l_ref[0])
        for d in range(1, 8):
            acc = acc + jnp.where(d_ref[0] == d, mine, l_ref[d])
        out_ref[...] = acc

    grid_spec = pltpu.PrefetchScalarGridSpec(
        num_scalar_prefetch=1, grid=(1,),
        in_specs=[pl.BlockSpec((8, rows, cols), lambda i, d_ref: (0, 0, 0)),
                  pl.BlockSpec((rows, cols), lambda i, d_ref: (0, 0))],
        out_specs=pl.BlockSpec((rows, cols), lambda i, d_ref: (0, 0)))
    return pl.pallas_call(
        body, name=name, grid_spec=grid_spec, out_shape=jax.ShapeDtypeStruct((rows, cols), f32),
        compiler_params=_cparams(("arbitrary",)),
    )(device, land, own)


def sum4_into(arrived, own, dest, where, *, layer, total_rows, name):
    _, rows, cols = arrived.shape
    tile = _pick(rows, max(16, (1 << 20) // (4 * cols) // 16 * 16), 16)
    per = rows // tile

    def body(w_ref, a_ref, own_ref, *rest):
        mine = own_ref[0].astype(f32)
        p = [jnp.where(w_ref[1] == j, mine, a_ref[j].astype(f32)) for j in range(4)]
        rest[-1][...] = ((p[0] + p[1]) + p[2]) + p[3]

    grid_spec = pltpu.PrefetchScalarGridSpec(
        num_scalar_prefetch=1, grid=(per,),
        in_specs=[pl.BlockSpec((4, tile, cols), lambda i, w_ref: (0, i, 0)),
                  pl.BlockSpec((1, tile, cols), lambda i, w_ref: (w_ref[1], i, 0))] + ([] if dest is None else [_ANY]),
        out_specs=pl.BlockSpec((tile, cols), lambda i, w_ref: ((2 * layer + w_ref[0]) * per + i, 0)))
    return pl.pallas_call(
        body, name=name, grid_spec=grid_spec, out_shape=jax.ShapeDtypeStruct((total_rows, cols), f32),
        input_output_aliases={} if dest is None else {3: 0}, compiler_params=_cparams(("parallel",)),
    )(where, arrived, own, *([] if dest is None else [dest]))


def _consts():
    idx = np.arange(RW)
    bd = (idx[:, None] // RN == idx[None, :] // RN).astype(np.float32)
    place = np.zeros((128, DQK), np.float32)
    place[np.arange(ROPE), NOPE + np.arange(ROPE)] = 1.0
    rot = np.zeros((DQK, DQK), np.float32)
    half = ROPE // 2
    rot[NOPE + half + np.arange(half), NOPE + np.arange(half)] = -1.0
    rot[NOPE + np.arange(half), NOPE + half + np.arange(half)] = 1.0
    return jnp.asarray(bd), jnp.asarray(place), jnp.asarray(rot)


def _rope_tables(positions):
    freqs = ROPE_THETA ** (-(jnp.arange(ROPE // 2, dtype=f32) * 2.0 / ROPE))
    ang = positions.astype(f32)[:, None] * freqs
    cos, sin = jnp.cos(ang), jnp.sin(ang)
    ones = jnp.ones((positions.shape[0], NOPE), f32)
    return (jnp.concatenate([ones, cos, cos], axis=-1), jnp.concatenate([0.0 * ones, sin, sin], axis=-1))


STAGES = (("w_in",), ("mla_wq_b", "mla_wkv_b", "mla_w_o", "rwkv_w_o", "conv_w_o", "w_out"), ("w_up", "w_down"))


def derive_stage(stage, w):
    if stage == 0:
        w_in = w["w_in"]
        pad = jnp.zeros((D, MLA_PAD - MLA_COLS), w_in.dtype)
        return dict(gate=w_in[:, :GATE], mla=jnp.concatenate([w_in[:, GATE:GATE + MLA_COLS], pad], axis=1),
                    rw=w_in[:, GATE + MLA_COLS:GATE + MLA_COLS + 4 * RW], cv=w_in[:, GATE + MLA_COLS + 4 * RW:])
    if stage == 1:
        wkv = w["mla_wkv_b"].reshape(KVL, MLA_H, NOPE + DV)
        wk = jnp.concatenate([wkv[:, :, :NOPE], jnp.zeros((KVL, MLA_H, ROPE), wkv.dtype)], axis=-1)
        return dict(wq=w["mla_wq_b"].reshape(QL, MLA_H, DQK).transpose(1, 0, 2), wk=wk.transpose(1, 0, 2),
                    wv=wkv[:, :, NOPE:].transpose(1, 0, 2), wo=w["mla_w_o"].reshape(MLA_H, DV, D),
                    rwo=w["rwkv_w_o"], cvo=w["conv_w_o"], out=w["w_out"])
    return dict(up=w["w_up"], down=w["w_down"])


W_IN_WINDOW_TILE = (0, 10, 21, 31)
W_IN_WINDOW = 1664
W_IN_SHARD = 1384


def w_in_window_cols(win, chip):
    gap = MLA_PAD - MLA_COLS
    branches = []
    for j in range(4):
        lo, hi = W_IN_SHARD * j, W_IN_SHARD * (j + 1)
        base = 128 * W_IN_WINDOW_TILE[j]
        cut = GATE + MLA_COLS
        if hi <= cut:
            branches.append(lambda w, a=lo - base: w[:, a:a + W_IN_SHARD])
        elif lo >= cut:
            branches.append(lambda w, a=lo + gap - base: w[:, a:a + W_IN_SHARD])
        else:
            branches.append(lambda w, a=lo - base, n1=cut - lo, b=cut + gap - base, n2=hi - cut:
                            jnp.concatenate([w[:, a:a + n1], w[:, b:b + n2]], axis=1))
    return lax.switch(chip, branches, win)


def chip_major_grads(stage, g):
    if stage == 0:
        padded = jnp.concatenate([g["gate"], g["mla"], g["rw"], g["cv"]], axis=1)
        return dict(w_in=jnp.stack([padded[:, 128 * t:128 * t + W_IN_WINDOW] for t in W_IN_WINDOW_TILE]))
    if stage == 1:
        heads = MLA_H // 4
        wq = g["wq"].reshape(4, heads, QL, DQK).transpose(0, 2, 1, 3).reshape(4, QL, heads * DQK)
        wkv = jnp.concatenate([g["wk"][:, :, :NOPE], g["wv"]], axis=-1)
        wkv = wkv.reshape(4, heads, KVL, NOPE + DV).transpose(0, 2, 1, 3).reshape(4, KVL, heads * (NOPE + DV))
        return dict(mla_wq_b=wq, mla_wkv_b=wkv, mla_w_o=g["wo"].reshape(4, MLA_H * DV, D // 4),
                    rwkv_w_o=g["rwo"], conv_w_o=g["cvo"], w_out=g["out"].reshape(4, D // 4, D))
    return dict(w_up=g["up"], w_down=g["down"].reshape(4, DFF // 4, D))


def _row(v):
    return v.reshape(1, -1)


def local_step(x, positions, target, w, sm, big_of=None, on_grads=None, on_small=None):
    if big_of is None:
        big_of = lambda l, stage, _after: {n: w[n][l] for n in STAGES[stage]}
    if on_grads is None:
        on_grads = lambda l, stage, slabs: None
    if on_small is None:
        on_small = lambda l, layer_small: None
    s_len = x.shape[0]
    t_row = _pick(s_len, 256, 8)
    t_wide = _pick(s_len, 128, 8)
    bd, place, rot = _consts()
    cos, sin = _rope_tables(positions)
    sds = lambda *shape: jax.ShapeDtypeStruct(shape, f32)
    sdb = lambda *shape: jax.ShapeDtypeStruct(shape, bf16)
    saved = []
    v_first = None
    for l in range(DEPTH):
        tag = f"l{l}_"
        lw = derive_stage(0, big_of(l, 0, x))
        vres = l > 0
        p_norm1 = [_row(sm["attn_norm"][l])]
        (h,) = rows_fwd(_fn_norm, [x], p_norm1, [], [sds(s_len, D)], tile=t_row, name=tag + "norm1")
        gate = mm(h, lw["gate"], name=tag + "proj_gate")
        mla = mm(h, lw["mla"], name=tag + "proj_mla")
        rwc = mm(h, lw["rw"], name=tag + "proj_rwkv")
        cvc = mm(h, lw["cv"], name=tag + "proj_conv")
        lw.update(derive_stage(1, big_of(l, 1, cvc)))
        p_mla = [_row(sm["mla_q_a_norm"][l]), _row(sm["mla_kv_a_norm"][l])]
        qn, kvn, kpe = rows_fwd(_fn_mla_prep, [mla], p_mla, [], [sdb(s_len, QL), sdb(s_len, KVL), sds(s_len, 128)],
                                tile=t_row, name=tag + "mla_prep")
        q_raw = mm(qn, lw["wq"], b_batched=True, name=tag + "q_proj")
        kn_pad = mm(kvn, lw["wk"], b_batched=True, name=tag + "k_proj")
        vv = mm(kvn, lw["wv"], b_batched=True, name=tag + "v_proj")
        p_qk = [_row(sm["mla_q_norm"][l]), _row(sm["mla_k_norm"][l])]
        q, k = rows_fwd(_fn_qk_post, [q_raw, kn_pad, kpe, cos, sin], p_qk, [place, rot],
                        [sds(MLA_H, s_len, DQK), sds(MLA_H, s_len, DQK)], tile=t_wide, name=tag + "qk_post")
        o = attn_fwd(q, k, vv, tq=_pick(s_len, 256, 8), name=tag + "attn")
        o_a = mm(o, lw["wo"], a_batched=True, b_batched=True, reduce_batch=True, name=tag + "o_a")
        p_rw = [_row(sm["rwkv_mu"][l]), _row(sm["rwkv_w0"][l]), w["rwkv_w2"][l], _row(sm["rwkv_a0"][l]),
                w["rwkv_a2"][l], w["rwkv_g2"][l], _row(sm["rwkv_k_k"][l]), _row(sm["rwkv_k_a"][l])]
        rw_rows, rw_halos = [rwc], (0,)
        if vres:
            p_rw += [w["rwkv_v1"][l - 1], _row(sm["rwkv_v_mu"][l - 1]), _row(sm["rwkv_v0"][l - 1]), w["rwkv_v2"][l - 1]]
            rw_rows, rw_halos = [rwc, h, v_first], (0, 1)
        fn_prep = _make_fn_rwkv_prep(vres)
        r, ld, k2, v, an, bn, g = rows_fwd(fn_prep, rw_rows, p_rw, [bd], [sds(s_len, RW)] * 7, tile=t_row,
                                           name=tag + "rwkv_prep", halos=rw_halos)
        if not vres:
            v_first = v
        y, states = wkv_fwd(r, ld, k2, v, an, bn, name=tag + "wkv")
        p_post = [_row(sm["rwkv_ln_w"][l]), _row(sm["rwkv_ln_b"][l]), _row(sm["rwkv_r_k"][l])]
        (yb,) = rows_fwd(_fn_rwkv_post, [y, r, k2, v, g], p_post, [bd], [sdb(s_len, RW)], tile=t_row,
                         name=tag + "rwkv_post")
        o_b = mm(yb, lw["rwo"], name=tag + "o_b")
        p_cv = [w["conv_w"][l][q:q + 1] for q in range(3)]
        (yc,) = rows_fwd(_fn_conv, [cvc], p_cv, [], [sdb(s_len, CW)], tile=t_row, name=tag + "conv", halos=(0,))
        o_c = mm(yc, lw["cvo"], name=tag + "o_c")
        (merged,) = rows_fwd(_fn_merge, [gate, o_a, o_b, o_c], [], [], [sdb(s_len, D)], tile=t_wide,
                             name=tag + "merge")
        x1 = mm(merged, lw["out"], add=x, name=tag + "out_proj")
        lw.update(derive_stage(2, big_of(l, 2, x1)))
        p_norm2 = [_row(sm["mlp_norm"][l])]
        (h2,) = rows_fwd(_fn_norm, [x1], p_norm2, [], [sdb(s_len, D)], tile=t_row, name=tag + "norm2")
        up, act = mm(h2, lw["up"], relu2_out=True, name=tag + "up")
        x2 = mm(act, lw["down"], add=x1, name=tag + "down")
        saved.append(dict(lw=lw, x=x, h=h, gate=gate, mla=mla, rwc=rwc, cvc=cvc, qn=qn, kvn=kvn, kpe=kpe,
                          q_raw=q_raw, kn_pad=kn_pad, vv=vv, q=q, k=k, o=o, o_a=o_a, r=r, ld=ld, k2=k2, v=v,
                          an=an, bn=bn, g=g, y=y, states=states, yb=yb, o_b=o_b, yc=yc, o_c=o_c, merged=merged,
                          x1=x1, h2=h2, up=up, act=act, p_norm1=p_norm1, p_mla=p_mla, p_qk=p_qk, p_rw=p_rw,
                          p_post=p_post, p_cv=p_cv, p_norm2=p_norm2, rw_rows=rw_rows, rw_halos=rw_halos,
                          fn_prep=fn_prep, v_first=v_first if vres else None))
        x = x2

    loss, dx = loss_head(x, target, tile=t_row, name="loss_head")

    grads = {n: [None] * (DEPTH - 1 if n in ("rwkv_v1", "rwkv_v_mu", "rwkv_v0", "rwkv_v2") else DEPTH)
             for n in WEIGHTS}
    dv_first = None
    for l in reversed(range(DEPTH)):
        tag = f"b{l}_"
        sv = saved[l]
        lw = sv["lw"]
        vres = l > 0
        g_down = mm(sv["act"], dx, ta=True, out_dtype=bf16, name=tag + "g_down")
        dup = mm(dx, lw["down"], tb=True, act_grad=sv["up"], out_dtype=bf16, name=tag + "d_up")
        g_up = mm(sv["h2"], dup, ta=True, n_split=4, out_dtype=bf16, name=tag + "g_up")
        dh2 = mm(dup, lw["up"], tb=True, name=tag + "d_h2")
        slabs = chip_major_grads(2, dict(up=g_up, down=g_down))
        token = on_grads(l, 2, slabs)
        p_norm2 = sv["p_norm2"] if token is None else [sv["p_norm2"][0] + token[0, 0]]
        (dx1,), (g_n2,) = rows_bwd(_fn_norm, [sv["x1"]], p_norm2, [], [[dh2]], tile=t_row,
                                   name=tag + "norm2", extra={0: [dx]})
        g_out = mm(sv["merged"], dx1, ta=True, out_dtype=bf16, name=tag + "g_out")
        dmerged = mm(dx1, lw["out"], tb=True, name=tag + "d_merged")
        (dgate, do_a, do_b, do_c), _ = rows_bwd(_fn_merge, [sv["gate"], sv["o_a"], sv["o_b"], sv["o_c"]], [], [],
                                                [[dmerged]], tile=t_wide, name=tag + "merge",
                                                grad_dtypes=[bf16] * 4)
        g_cvo = mm(sv["yc"], do_c, ta=True, n_split=4, out_dtype=bf16, name=tag + "g_cvo")
        dyc = mm(do_c, lw["cvo"], tb=True, name=tag + "d_yc")
        (dcvc,), g_cw = rows_bwd(_fn_conv, [sv["cvc"]], sv["p_cv"], [], [[dyc]], tile=t_row, name=tag + "conv",
                                    halos=(0,))
        g_rwo = mm(sv["yb"], do_b, ta=True, n_split=4, out_dtype=bf16, name=tag + "g_rwo")
        dyb = mm(do_b, lw["rwo"], tb=True, name=tag + "d_yb")
        (dy, dr_p, dk_p, dv_p, dg), g_post = rows_bwd(
            _fn_rwkv_post, [sv["y"], sv["r"], sv["k2"], sv["v"], sv["g"]], sv["p_post"], [bd], [[dyb]], tile=t_row,
            name=tag + "rwkv_post")
        dr_s, dld, dk_s, dv_s, dan, dbn = wkv_bwd(sv["r"], sv["ld"], sv["k2"], sv["v"], sv["an"], sv["bn"],
                                                  sv["states"], dy, name=tag + "wkv")
        dv_list = [dv_s, dv_p] + ([dv_first] if (not vres and dv_first is not None) else [])
        d_prep, g_prep = rows_bwd(
            sv["fn_prep"], sv["rw_rows"], sv["p_rw"], [bd],
            [[dr_s, dr_p], [dld], [dk_s, dk_p], dv_list, [dan], [dbn], [dg]], tile=t_row, name=tag + "rwkv_prep",
            halos=sv["rw_halos"])
        drwc = d_prep[0]
        dh_extra = []
        if vres:
            dh_extra = [d_prep[1]]
            dv_first = d_prep[2]
        g_wo = mm(sv["o"], do_a, ta=True, a_batched=True, n_split=4, tk=s_len, out_dtype=bf16, name=tag + "g_wo")
        do = mm(do_a, lw["wo"], tb=True, b_batched=True, name=tag + "d_o")
        dq, dk, dvv = attn_bwd(sv["q"], sv["k"], sv["vv"], do, tq=_pick(s_len, 256, 8), name=tag + "attn")
        (dq_raw, dkn_pad, dkpe), g_qk = rows_bwd(
            _fn_qk_post, [sv["q_raw"], sv["kn_pad"], sv["kpe"], cos, sin], sv["p_qk"], [place, rot], [[dq], [dk]],
            tile=t_wide, name=tag + "qk_post", grad_rows=[0, 1, 2], grad_dtypes=[bf16, bf16, f32])
        g_wq = mm(sv["qn"], dq_raw, ta=True, b_batched=True, tk=s_len, out_dtype=bf16, name=tag + "g_wq")
        g_wk = mm(sv["kvn"], dkn_pad, ta=True, b_batched=True, tk=s_len, out_dtype=bf16, name=tag + "g_wk")
        g_wv = mm(sv["kvn"], dvv, ta=True, b_batched=True, tk=s_len, out_dtype=bf16, name=tag + "g_wv")
        dqn = mm(dq_raw, lw["wq"], tb=True, a_batched=True, b_batched=True, reduce_batch=True, name=tag + "d_qn")
        dkvn = mm(dkn_pad, lw["wk"], tb=True, a_batched=True, b_batched=True, reduce_batch=True, name=tag + "d_kvn_k")
        dkvn = mm(dvv, lw["wv"], tb=True, a_batched=True, b_batched=True, reduce_batch=True, add=dkvn,
                  name=tag + "d_kvn_v")
        slabs.update(chip_major_grads(1, dict(wq=g_wq, wk=g_wk, wv=g_wv, wo=g_wo, rwo=g_rwo, cvo=g_cvo, out=g_out)))
        token = on_grads(l, 1, {n: slabs[n] for n in STAGES[1]})
        p_mla = sv["p_mla"] if token is None else [sv["p_mla"][0] + token[0, 0], sv["p_mla"][1]]
        (dmla,), g_mla = rows_bwd(_fn_mla_prep, [sv["mla"]], p_mla, [], [[dqn], [dkvn], [dkpe]], tile=t_row,
                                  name=tag + "mla_prep", grad_dtypes=[bf16])
        g_gate = mm(sv["h"], dgate, ta=True, out_dtype=bf16, name=tag + "g_gate")
        g_mlaw = mm(sv["h"], dmla, ta=True, out_dtype=bf16, name=tag + "g_mla")
        g_rw = mm(sv["h"], drwc, ta=True, out_dtype=bf16, name=tag + "g_rw")
        g_cv = mm(sv["h"], dcvc, ta=True, out_dtype=bf16, name=tag + "g_cv")
        dh = mm(dgate, lw["gate"], tb=True, name=tag + "d_h_gate")
        dh = mm(dmla, lw["mla"], tb=True, add=dh, name=tag + "d_h_mla")
        dh = mm(drwc, lw["rw"], tb=True, add=dh, name=tag + "d_h_rw")
        dh = mm(dcvc, lw["cv"], tb=True, add=dh, name=tag + "d_h_cv")
        (dx,), (g_n1,) = rows_bwd(_fn_norm, [sv["x"]], sv["p_norm1"], [], [[dh] + dh_extra], tile=t_row,
                                  name=tag + "norm1", extra={0: [dx1]})
        slabs.update(chip_major_grads(0, dict(gate=g_gate, mla=g_mlaw, rw=g_rw, cv=g_cv)))
        token = on_grads(l, 0, {n: slabs[n] for n in STAGES[0]})
        if token is not None and l > 0:
            dx = dx + token[0, 0]
        for n, val in slabs.items():
            grads[n][l] = val
        layer_small = [("attn_norm", l, g_n1), ("mlp_norm", l, g_n2), ("mla_q_a_norm", l, g_mla[0]),
                       ("mla_kv_a_norm", l, g_mla[1]), ("mla_q_norm", l, g_qk[0]), ("mla_k_norm", l, g_qk[1]),
                       ("rwkv_ln_w", l, g_post[0]), ("rwkv_ln_b", l, g_post[1]), ("rwkv_r_k", l, g_post[2]),
                       ("conv_w", l, jnp.concatenate(g_cw, axis=0))]
        layer_small += list(zip(["rwkv_mu", "rwkv_w0", "rwkv_w2", "rwkv_a0", "rwkv_a2", "rwkv_g2", "rwkv_k_k",
                                 "rwkv_k_a"], [l] * 8, g_prep[:8]))
        if vres:
            layer_small += list(zip(["rwkv_v1", "rwkv_v_mu", "rwkv_v0", "rwkv_v2"], [l - 1] * 4, g_prep[8:12]))
        for n, index, val in layer_small:
            grads[n][index] = val
        token = on_small(l, layer_small)
        if token is not None and l > 0:
            dx = dx + token[0, 0]
    return loss, dx, grads


def _split3(a):
    hi = a.astype(bf16)
    r1 = a - hi.astype(f32)
    mid = r1.astype(bf16)
    lo = (r1 - mid.astype(f32)).astype(bf16)
    return hi, mid, lo


def _shard_axis(name):
    return 1 if name in ROW_SHARDED else 2


def _pack(pieces, width, dtype, row_align):
    flat = jnp.concatenate([p.reshape(-1).astype(dtype) for p in pieces])
    rows = -(-flat.shape[0] // width)
    rows = -(-rows // row_align) * row_align
    return jnp.pad(flat, (0, rows * width - flat.shape[0])).reshape(rows, width)


def _unpack(flat2d, shapes):
    flat = flat2d.reshape(-1)
    out, off = [], 0
    for shp in shapes:
        n = int(np.prod(shp))
        out.append(flat[off:off + n].reshape(shp))
        off += n
    return out


def kernel(x, positions, attn_norm, w_in, mla_q_a_norm, mla_wq_b, mla_kv_a_norm, mla_wkv_b, mla_q_norm, mla_k_norm, mla_w_o, rwkv_mu, rwkv_w0, rwkv_w2, rwkv_a0, rwkv_a2, rwkv_g2, rwkv_k_k, rwkv_k_a, rwkv_r_k, rwkv_ln_w, rwkv_ln_b, rwkv_w_o, rwkv_v1, rwkv_v_mu, rwkv_v0, rwkv_v2, conv_w, conv_w_o, w_out, mlp_norm, w_up, w_down, loss_target, m_attn_norm, m_w_in, m_mla_q_a_norm, m_mla_wq_b, m_mla_kv_a_norm, m_mla_wkv_b, m_mla_q_norm, m_mla_k_norm, m_mla_w_o, m_rwkv_mu, m_rwkv_w0, m_rwkv_w2, m_rwkv_a0, m_rwkv_a2, m_rwkv_g2, m_rwkv_k_k, m_rwkv_k_a, m_rwkv_r_k, m_rwkv_ln_w, m_rwkv_ln_b, m_rwkv_w_o, m_rwkv_v1, m_rwkv_v_mu, m_rwkv_v0, m_rwkv_v2, m_conv_w, m_conv_w_o, m_w_out, m_mlp_norm, m_w_up, m_w_down, v_attn_norm, v_w_in, v_mla_q_a_norm, v_mla_wq_b, v_mla_kv_a_norm, v_mla_wkv_b, v_mla_q_norm, v_mla_k_norm, v_mla_w_o, v_rwkv_mu, v_rwkv_w0, v_rwkv_w2, v_rwkv_a0, v_rwkv_a2, v_rwkv_g2, v_rwkv_k_k, v_rwkv_k_a, v_rwkv_r_k, v_rwkv_ln_w, v_rwkv_ln_b, v_rwkv_w_o, v_rwkv_v1, v_rwkv_v_mu, v_rwkv_v0, v_rwkv_v2, v_conv_w, v_conv_w_o, v_w_out, v_mlp_norm, v_w_up, v_w_down):
    args = dict(locals())
    wts = {n: args[n] for n in WEIGHTS}
    mom = {n: args["m_" + n] for n in WEIGHTS}
    var = {n: args["v_" + n] for n in WEIGHTS}
    chip = 2 * lax.axis_index("x") + lax.axis_index("y")
    core = lax.axis_index("c").astype(jnp.int32).reshape(1)

    med_names = [n for n in MED if n != "conv_w"]
    med_pieces = [wts[n] for n in med_names] + list(_split3(wts["conv_w"]))
    med_shapes = [p.shape for p in med_pieces]
    chip_idx = chip.astype(jnp.int32).reshape(1)
    shards_first = [wts[n][0].astype(bf16) for n in STAGES[0]] + [_pack(med_pieces, 128, bf16, 32)]
    got_first, token = gather_weights(shards_first, name="gather_l0_s0")
    got_first = [place_slab(g, s, chip_idx, name=f"place_own_l0_s0_{q}")
                 for q, (g, s) in enumerate(zip(got_first, shards_first))]
    in_flight = {}
    for key, names, l in (("l0_s1", STAGES[1], 0), ("l0_s2", STAGES[2], 0), ("l1", BIG, 1)):
        group = [wts[n][l].astype(bf16) for n in names]
        group[0] = group[0] + token[0, 0].astype(bf16)
        in_flight[key] = (names, gather_start(group, name="gather_start_" + key))
        token = in_flight[key][1][4]

    def whole_of(names, slabs):
        out = {}
        for n, by_chip in zip(names, slabs):
            _, rows, cols = by_chip.shape
            if n in ROW_SHARDED:
                out[n] = by_chip.reshape(4 * rows, cols)
            else:
                out[n] = by_chip.transpose(1, 0, 2).reshape(rows, 4 * cols)
        return out

    landed = {}

    def big_of(l, stage, after):
        if (l, stage) == (0, 0):
            return whole_of(STAGES[0], got_first)
        key = "l1" if l == 1 else f"l0_s{stage}"
        if key not in landed:
            names, (send_sems, recv_sems, thru, lands, _) = in_flight[key]
            thru, lands = gather_wait(send_sems, recv_sems, thru, lands, after, name="gather_wait_" + key)
            lands = gather_forward(lands, name="gather_forward_" + key)
            landed[key] = whole_of(names, [place_slab(g, s, chip_idx, name=f"place_own_{key}_{q}")
                                           for q, (g, s) in enumerate(zip(lands, thru))])
        return {n: landed[key][n] for n in STAGES[stage]}

    whole = {}
    per_chip = [_unpack(got_first[len(STAGES[0])][j], med_shapes) for j in range(4)]
    for q, n in enumerate(med_names):
        whole[n] = jnp.concatenate([per_chip[j][q] for j in range(4)], axis=_shard_axis(n)).astype(f32)
    base = len(med_names)
    cw_parts = [jnp.concatenate([per_chip[j][base + t] for j in range(4)], axis=2).astype(f32) for t in range(3)]
    whole["conv_w"] = (cw_parts[0] + cw_parts[1]) + cw_parts[2]
    small = {n: wts[n] for n in SMALL}
    small["rwkv_r_k"] = wts["rwkv_r_k"].reshape(DEPTH, RW)

    exchanges = []

    def on_grads(l, stage, slabs):
        names = STAGES[stage]
        tag = f"l{l}_s{stage}"
        parts = [slabs[n] for n in names]
        from_sibling = grads_to_sibling(parts, name="grads_to_sibling_" + tag)
        chip_sums = [pair_sum(s, t, core, name=f"pair_sum_{n}_{l}") for n, s, t in zip(names, parts, from_sibling)]
        started = scatter_start(chip_sums, name="scatter_start_" + tag)
        exchanges.append((l, names, tag, started))
        return started[4]

    broadcasts = []

    def on_small(l, layer_small):
        values = [val for _, _, val in layer_small]
        started = broadcast_start(_pack(values, 128, f32, 8), name=f"small_start_l{l}")
        broadcasts.append((l, [(n, index, val.shape) for n, index, val in layer_small], started))
        return started[4]

    small["attn_norm"] = small["attn_norm"] + token[0, 0]
    loss, grad_x, grads = local_step(x[0], positions[0], loss_target[0], whole, small, big_of, on_grads, on_small)
    loss = lax.psum(loss, ("x", "y", "c"))

    device = (4 * lax.axis_index("x") + 2 * lax.axis_index("y") + lax.axis_index("c")).astype(jnp.int32).reshape(1)
    summed = {n: [None] * len(grads[n]) for n in SMALL + MED}
    for l, entries, (send_sems, recv_sems, thru, land, _) in broadcasts:
        own, land = broadcast_wait(send_sems, recv_sems, thru, land, grad_x, name=f"small_wait_l{l}")
        total = sum8(land, own, device, name=f"small_sum_l{l}")
        for (n, index, _), val in zip(entries, _unpack(total, [shape for _, _, shape in entries])):
            summed[n][index] = val
    gsum = {}
    for n in SMALL + MED:
        g = jnp.stack(summed[n])
        if n in MED:
            ax = _shard_axis(n)
            width = wts[n].shape[ax]
            g = lax.dynamic_slice_in_dim(g, chip * width, width, axis=ax)
        gsum[n] = g.reshape(wts[n].shape)
    where = jnp.stack([lax.axis_index("c"), chip]).astype(jnp.int32)
    bufs, layout = {}, []
    for l, names, tag, (send_sems, recv_sems, thru, lands, _) in exchanges:
        own, arrived = scatter_wait(send_sems, recv_sems, thru, lands, grad_x, name="scatter_wait_" + tag)
        for n, mine, theirs in zip(names, own, arrived):
            rows = 2 * theirs.shape[1]
            bufs[n] = sum4_into(theirs, mine, bufs.get(n), where, layer=l, total_rows=DEPTH * rows,
                                name=f"sum_chips_{n}_{l}")
            layout.append((BIG.index(n), l * rows, rows))
    reduced = join_halves([bufs[n] for n in BIG], layout, name="join_halves")

    out_g, out_d, out_m, out_v = {}, {}, {}, {}
    for q, n in enumerate(BIG):
        shp = wts[n].shape
        as2d = lambda a: a.reshape(-1, shp[-1])
        g2d = w_in_window_cols(reduced[q], chip) if n == "w_in" else reduced[q]
        res = adamw(as2d(wts[n]), g2d, as2d(mom[n]), as2d(var[n]), name="adamw_" + n)
        out_g[n], out_d[n], out_m[n], out_v[n] = [r.reshape(shp) for r in res]
    sm_all = SMALL + MED
    sm_shapes2 = [wts[n].shape for n in sm_all]
    res = adamw(_pack([wts[n] for n in sm_all], 128, f32, 8), _pack([gsum[n] for n in sm_all], 128, f32, 8),
                _pack([mom[n] for n in sm_all], 128, f32, 8), _pack([var[n] for n in sm_all], 128, f32, 8),
                name="adamw_small")
    for tgt, flat in zip((out_g, out_d, out_m, out_v), res):
        for n, val in zip(sm_all, _unpack(flat, sm_shapes2)):
            tgt[n] = val
    return (loss, grad_x[None], *[out_g[n] for n in WEIGHTS], *[out_d[n] for n in WEIGHTS],
            *[out_m[n] for n in WEIGHTS], *[out_v[n] for n in WEIGHTS])
```

```python
import functools

import jax
import jax.numpy as jnp
import numpy as np
from jax import lax
from jax.experimental import pallas as pl
from jax.experimental.pallas import tpu as pltpu

f32, bf16 = jnp.float32, jnp.bfloat16
HI = lax.Precision.HIGHEST
MESH = pl.DeviceIdType.MESH

D = 1024
DEPTH = 2
MLA_H, NOPE, ROPE, DQK, DV = 8, 64, 32, 96, 64
QL, KVL = 384, 256
RW, RH, RN = 256, 4, 64
DL, AL, GL, MVL = 64, 64, 128, 32
CW = 256
DFF = 4096
GATE = 3 * D
MLA_COLS = QL + KVL + ROPE
MLA_PAD = 768
NORM_EPS = 1e-6
GN_EPS = 64e-5
ROPE_THETA = 10000.0
LR, B1, B2, EPS, WD, STEP = 0.001, 0.9, 0.999, 1e-08, 0.01, 10

VMEM_LIMIT = 52 * 1024 * 1024
WKV_CHUNK = 64
WKV_CHUNKS_PER_STEP = 4
ATTN_SEGMENTS = 4

BIG = ["w_in", "mla_wq_b", "mla_wkv_b", "mla_w_o", "rwkv_w_o", "conv_w_o", "w_out", "w_up", "w_down"]
MED = ["rwkv_w2", "rwkv_a2", "rwkv_g2", "rwkv_v1", "rwkv_v2", "conv_w"]
ROW_SHARDED = {"w_out", "w_down", "rwkv_v1"}
SMALL = ["attn_norm", "mla_q_a_norm", "mla_kv_a_norm", "mla_q_norm", "mla_k_norm", "rwkv_mu", "rwkv_w0",
         "rwkv_a0", "rwkv_k_k", "rwkv_k_a", "rwkv_r_k", "rwkv_ln_w", "rwkv_ln_b", "rwkv_v_mu", "rwkv_v0",
         "mlp_norm"]
WEIGHTS = ["attn_norm", "w_in", "mla_q_a_norm", "mla_wq_b", "mla_kv_a_norm", "mla_wkv_b", "mla_q_norm",
           "mla_k_norm", "mla_w_o", "rwkv_mu", "rwkv_w0", "rwkv_w2", "rwkv_a0", "rwkv_a2", "rwkv_g2",
           "rwkv_k_k", "rwkv_k_a", "rwkv_r_k", "rwkv_ln_w", "rwkv_ln_b", "rwkv_w_o", "rwkv_v1", "rwkv_v_mu",
           "rwkv_v0", "rwkv_v2", "conv_w", "conv_w_o", "w_out", "mlp_norm", "w_up", "w_down"]


def _cparams(sem=None):
    return pltpu.CompilerParams(dimension_semantics=sem, vmem_limit_bytes=VMEM_LIMIT)


def _pick(dim, pref, align):
    if dim <= pref:
        return dim
    t = (pref // align) * align
    while t >= align:
        if dim % t == 0:
            return t
        t -= align
    return dim


def _bdot(a, b, dims):
    return lax.dot_general(a.astype(bf16), b.astype(bf16), (dims, ((), ())), preferred_element_type=f32)


@jax.custom_vjp
def _mm(a, b):
    return _bdot(a, b, ((1,), (0,)))


def _mm_fwd(a, b):
    return _mm(a, b), (a, b)


def _mm_bwd(res, g):
    a, b = res
    return _bdot(g, b, ((1,), (1,))), _bdot(a, g, ((0,), (0,)))


_mm.defvjp(_mm_fwd, _mm_bwd)


@jax.custom_vjp
def _mm_nt(a, b):
    return _bdot(a, b, ((1,), (1,)))


def _mm_nt_fwd(a, b):
    return _mm_nt(a, b), (a, b)


def _mm_nt_bwd(res, g):
    a, b = res
    return _bdot(g, b, ((1,), (0,))), _bdot(g, a, ((0,), (0,)))


_mm_nt.defvjp(_mm_nt_fwd, _mm_nt_bwd)


_NN, _NT, _TN = ((1,), (0,)), ((1,), (1,)), ((0,), (0,))


def _dg(a, b, dims):
    return lax.dot_general(a, b, (dims, ((), ())), preferred_element_type=f32)


def _bf16_pieces(x, count):
    out, rest = [], x
    for q in range(count):
        piece = rest.astype(bf16)
        out.append(piece)
        if q + 1 < count:
            rest = rest - piece.astype(f32)
    return out


def _dot3(a, b, dims):
    (ah, al), (bh, bl) = _bf16_pieces(a, 2), _bf16_pieces(b, 2)
    return _dg(ah, bh, dims) + (_dg(ah, bl, dims) + _dg(al, bh, dims))


@jax.custom_vjp
def _hdot(a, b):
    return _dot3(a, b, _NN)


@jax.custom_vjp
def _hdot_nt(a, b):
    return _dot3(a, b, _NT)


@jax.custom_vjp
def _hdot_tn(a, b):
    return _dot3(a, b, _TN)


_hdot.defvjp(lambda a, b: (_hdot(a, b), (a, b)), lambda res, g: (_hdot_nt(g, res[1]), _hdot_tn(res[0], g)))
_hdot_nt.defvjp(lambda a, b: (_hdot_nt(a, b), (a, b)), lambda res, g: (_hdot(g, res[1]), _hdot_tn(g, res[0])))
_hdot_tn.defvjp(lambda a, b: (_hdot_tn(a, b), (a, b)), lambda res, g: (_hdot_nt(res[1], g), _hdot(res[0], g)))


_BNN, _BNT, _BTN = ((2,), (1,)), ((2,), (2,)), ((1,), (1,))


def _bdg(a, b, dims):
    return lax.dot_general(a, b, (dims, ((0,), (0,))), preferred_element_type=f32)


def _bdot3(a, b, dims):
    (ah, al), (bh, bl) = _bf16_pieces(a, 2), _bf16_pieces(b, 2)
    return _bdg(ah, bh, dims) + (_bdg(ah, bl, dims) + _bdg(al, bh, dims))


@jax.custom_vjp
def _hbnn(a, b):
    return _bdot3(a, b, _BNN)


@jax.custom_vjp
def _hbnt(a, b):
    return _bdot3(a, b, _BNT)


@jax.custom_vjp
def _hbtn(a, b):
    return _bdot3(a, b, _BTN)


_hbnn.defvjp(lambda a, b: (_hbnn(a, b), (a, b)), lambda res, g: (_hbnt(g, res[1]), _hbtn(res[0], g)))
_hbnt.defvjp(lambda a, b: (_hbnt(a, b), (a, b)), lambda res, g: (_hbnn(g, res[1]), _hbtn(g, res[0])))
_hbtn.defvjp(lambda a, b: (_hbtn(a, b), (a, b)), lambda res, g: (_hbnt(res[1], g), _hbnn(res[0], g)))


@functools.partial(jax.custom_vjp, nondiff_argnums=(2,))
def _exact_bl(m, x, transposed):
    mb = m.astype(bf16)
    hi, mid, lo = _bf16_pieces(x, 3)
    dims = _BTN if transposed else _BNN
    return (_bdg(mb, hi, dims) + _bdg(mb, mid, dims)) + _bdg(mb, lo, dims)


_exact_bl.defvjp(lambda m, x, transposed: (_exact_bl(m, x, transposed), m),
                 lambda transposed, m, g: (jnp.zeros_like(m), _exact_bl(m, g, not transposed)))


@functools.partial(jax.custom_vjp, nondiff_argnums=(2,))
def _exact_l(m, x, transposed):
    mb = m.astype(bf16)
    hi, mid, lo = _bf16_pieces(x, 3)
    dims = _TN if transposed else _NN
    return (_dg(mb, hi, dims) + _dg(mb, mid, dims)) + _dg(mb, lo, dims)


_exact_l.defvjp(lambda m, x, transposed: (_exact_l(m, x, transposed), m),
                lambda transposed, m, g: (jnp.zeros_like(m), _exact_l(m, g, not transposed)))


@functools.partial(jax.custom_vjp, nondiff_argnums=(2,))
def _exact_r(x, m, transposed):
    mb = m.astype(bf16)
    hi, mid, lo = _bf16_pieces(x, 3)
    dims = _NT if transposed else _NN
    return (_dg(hi, mb, dims) + _dg(mid, mb, dims)) + _dg(lo, mb, dims)


_exact_r.defvjp(lambda x, m, transposed: (_exact_r(x, m, transposed), m),
                lambda transposed, m, g: (_exact_r(g, m, not transposed), jnp.zeros_like(m)))


def _rms(x, g, eps=NORM_EPS):
    return x * lax.rsqrt(jnp.mean(x * x, axis=-1, keepdims=True) + eps) * g


def _sigmoid(x):
    return 1.0 / (1.0 + jnp.exp(-x))


def _softplus(x):
    return jnp.maximum(x, 0.0) + jnp.log(1.0 + jnp.exp(-jnp.maximum(x, -x)))


def _lane_split(x, sizes):
    bounds = np.cumsum([0] + list(sizes))

    @jax.custom_vjp
    def split(v):
        return tuple(v[..., int(bounds[q]):int(bounds[q + 1])] for q in range(len(sizes)))

    split.defvjp(lambda v: (split(v), None), lambda _, g: (jnp.concatenate(g, axis=-1),))
    return split(x)


def _row_split(x, sizes):
    bounds = np.cumsum([0] + list(sizes))

    @jax.custom_vjp
    def split(v):
        return tuple(v[..., int(bounds[q]):int(bounds[q + 1]), :] for q in range(len(sizes)))

    split.defvjp(lambda v: (split(v), None), lambda _, g: (jnp.concatenate(g, axis=-2),))
    return split(x)


def _unstack(x):
    @jax.custom_vjp
    def unstack(v):
        return tuple(v[q] for q in range(v.shape[0]))

    unstack.defvjp(lambda v: (unstack(v), None), lambda _, g: (jnp.stack(g, axis=0),))
    return unstack(x)


def _shift_mats(t, k):
    r = lax.broadcasted_iota(jnp.int32, (t, t), 0)
    c = lax.broadcasted_iota(jnp.int32, (t, t), 1)
    inner = (r - c == k).astype(f32)
    r8 = lax.broadcasted_iota(jnp.int32, (t, 8), 0)
    c8 = lax.broadcasted_iota(jnp.int32, (t, 8), 1)
    edge = (c8 - r8 == 8 - k).astype(f32)
    return inner, edge


def _shift(x, halo, k):
    inner, edge = _shift_mats(x.shape[0], k)
    return _exact_l(inner, x, False) + jnp.dot(edge, halo, precision=HI, preferred_element_type=f32)


def mm(a, b, *, name, ta=False, tb=False, a_batched=False, b_batched=False, reduce_batch=False, add=None,
       act_grad=None, relu2_out=False, n_split=1, out_dtype=f32, tm=1024, tn=1024, tk=2048):
    ash, bsh = a.shape[-2:], b.shape[-2:]
    (k_, m_) = ash if ta else ash[::-1]
    (k2_, n_) = bsh[::-1] if tb else bsh
    assert k_ == k2_, (a.shape, b.shape, ta, tb)
    hb = a.shape[0] if a_batched else (b.shape[0] if b_batched else 1)
    batched_out = (a_batched or b_batched) and not reduce_batch
    h_out = hb if batched_out else 1
    h_red = hb if reduce_batch else 1
    tm = _pick(m_, tm, 128)
    tn = _pick(n_ // n_split, tn, 128)
    tk = _pick(k_, tk, 128)
    nm, nn, nk = m_ // tm, n_ // tn, k_ // tk

    def a_map(i, j, ho, hr, kk):
        blk = (kk, i) if ta else (i, kk)
        return ((ho if batched_out else hr),) + blk if a_batched else blk

    def b_map(i, j, ho, hr, kk):
        blk = (j, kk) if tb else (kk, j)
        return ((ho if batched_out else hr),) + blk if b_batched else blk

    a_blk = (tk, tm) if ta else (tm, tk)
    b_blk = (tn, tk) if tb else (tk, tn)
    in_specs = [pl.BlockSpec(((1,) + a_blk) if a_batched else a_blk, a_map),
                pl.BlockSpec(((1,) + b_blk) if b_batched else b_blk, b_map)]
    args = [a, b]
    for extra in (add, act_grad):
        if extra is not None:
            in_specs.append(pl.BlockSpec((tm, tn), lambda i, j, ho, hr, kk: (i, j)))
            args.append(extra)
    if n_split > 1:
        per = n_ // n_split // tn
        if batched_out:
            out_spec = pl.BlockSpec((1, 1, tm, tn), lambda i, j, ho, hr, kk: (j // per, ho, i, j % per))
            out_shape = jax.ShapeDtypeStruct((n_split, hb, m_, n_ // n_split), out_dtype)
        else:
            out_spec = pl.BlockSpec((1, tm, tn), lambda i, j, ho, hr, kk: (j // per, i, j % per))
            out_shape = jax.ShapeDtypeStruct((n_split, m_, n_ // n_split), out_dtype)
    elif batched_out:
        out_spec = pl.BlockSpec((1, tm, tn), lambda i, j, ho, hr, kk: (ho, i, j))
        out_shape = jax.ShapeDtypeStruct((hb, m_, n_), out_dtype)
    else:
        out_spec = pl.BlockSpec((tm, tn), lambda i, j, ho, hr, kk: (i, j))
        out_shape = jax.ShapeDtypeStruct((m_, n_), out_dtype)
    lead = (0,) * (int(batched_out) + int(n_split > 1))
    dims = ((0,) if ta else (1,), (1,) if tb else (0,))
    has_add, has_act = add is not None, act_grad is not None

    def body(*refs):
        a_ref, b_ref = refs[0], refs[1]
        pos = 2
        add_ref = act_ref = None
        if has_add:
            add_ref = refs[pos]
            pos += 1
        if has_act:
            act_ref = refs[pos]
            pos += 1
        o_ref, acc_ref = refs[pos], refs[-1]
        hr, kk = pl.program_id(3), pl.program_id(4)
        first = jnp.logical_and(hr == 0, kk == 0)
        last = jnp.logical_and(hr == h_red - 1, kk == nk - 1)
        av = a_ref[0] if a_batched else a_ref[...]
        bv = b_ref[0] if b_batched else b_ref[...]
        p = _bdot(av, bv, dims)
        single = h_red * nk == 1

        if not single:
            @pl.when(first)
            def _():
                acc_ref[...] = p

            @pl.when(jnp.logical_not(first))
            def _():
                acc_ref[...] += p

        @pl.when(last)
        def _():
            r = p if single else acc_ref[...]
            if has_act:
                r = r * (2.0 * jnp.maximum(act_ref[...], 0.0))
            if has_add:
                r = r + add_ref[...]
            if lead:
                o_ref[lead] = r.astype(out_dtype)
            else:
                o_ref[...] = r.astype(out_dtype)
            if relu2_out:
                refs[pos + 1][...] = jnp.square(jnp.maximum(r, 0.0)).astype(bf16)

    if relu2_out:
        assert not lead
        out_spec = [out_spec, out_spec]
        out_shape = [out_shape, jax.ShapeDtypeStruct(out_shape.shape, bf16)]
    return pl.pallas_call(
        body, name=name, grid=(nm, nn, h_out, h_red, nk), in_specs=in_specs, out_specs=out_spec,
        out_shape=out_shape, scratch_shapes=[pltpu.VMEM((tm, tn), f32)],
        compiler_params=_cparams(("parallel", "parallel", "parallel", "arbitrary", "arbitrary")),
    )(*args)


def _row_spec(arr, tile, idx):
    if arr.ndim == 2:
        return pl.BlockSpec((tile, arr.shape[1]), lambda i: (idx(i), 0))
    return pl.BlockSpec((arr.shape[0], tile, arr.shape[2]), lambda i: (0, idx(i), 0))


def _halo_spec(arr, tile, idx):
    per = tile // 8
    return pl.BlockSpec((8, arr.shape[1]), lambda i: (jnp.maximum(idx(i) * per - 1, 0), 0))


def _full_spec(arr):
    nd = arr.ndim
    return pl.BlockSpec(arr.shape, lambda i: (0,) * nd)


def _load_f32(ref):
    val = ref[...]
    return val.astype(f32) if val.dtype == bf16 else val


def rows_fwd(fn, rows, params, consts, out_shapes, *, tile, name, halos=()):
    s_len = rows[0].shape[-2]
    n = s_len // tile
    nr, nh, npar, nc = len(rows), len(halos), len(params), len(consts)
    ident = lambda i: i
    in_specs = ([_row_spec(r, tile, ident) for r in rows] + [_halo_spec(rows[h], tile, ident) for h in halos]
                + [_full_spec(p) for p in params] + [_full_spec(c) for c in consts])
    out_specs = [_row_spec(o, tile, ident) for o in out_shapes]

    def body(*refs):
        i = pl.program_id(0)
        rv = [_load_f32(r) for r in refs[:nr]]
        keep = (i > 0).astype(f32)
        hv = [r[...] * keep for r in refs[nr:nr + nh]]
        pv = [r[...] for r in refs[nr + nh:nr + nh + npar]]
        cv = [r[...] for r in refs[nr + nh + npar:nr + nh + npar + nc]]
        outs = fn(rv, hv, pv, cv)
        for o_ref, o in zip(refs[nr + nh + npar + nc:], outs):
            o_ref[...] = o.astype(o_ref.dtype)

    return pl.pallas_call(
        body, name=name, grid=(n,), in_specs=in_specs, out_specs=out_specs, out_shape=list(out_shapes),
        compiler_params=_cparams(("arbitrary",)),
    )(*rows, *[rows[h] for h in halos], *params, *consts)


def rows_bwd(fn, rows, params, consts, douts, *, tile, name, halos=(), grad_rows=None, extra=None,
             grad_dtypes=None):
    s_len = rows[0].shape[-2]
    n = s_len // tile
    nr, nh, npar, nc = len(rows), len(halos), len(params), len(consts)
    grad_rows = list(range(nr)) if grad_rows is None else list(grad_rows)
    extra = extra or {}
    assert all(h in grad_rows for h in halos)
    rev = lambda i: n - 1 - i
    dflat = [d for ds in douts for d in ds]
    dcount = [len(ds) for ds in douts]
    eflat = [e for g in grad_rows for e in extra.get(g, [])]
    ecount = [len(extra.get(g, [])) for g in grad_rows]
    in_specs = ([_row_spec(r, tile, rev) for r in rows] + [_halo_spec(rows[h], tile, rev) for h in halos]
                + [_full_spec(p) for p in params] + [_full_spec(c) for c in consts]
                + [_row_spec(d, tile, rev) for d in dflat] + [_row_spec(e, tile, rev) for e in eflat])
    grad_dtypes = [f32] * len(grad_rows) if grad_dtypes is None else list(grad_dtypes)
    assert all(grad_dtypes[q] == f32 for q, g in enumerate(grad_rows) if g in halos)
    out_shapes = ([jax.ShapeDtypeStruct(rows[g].shape, dt) for g, dt in zip(grad_rows, grad_dtypes)]
                  + [jax.ShapeDtypeStruct(p.shape, f32) for p in params])
    out_specs = [_row_spec(rows[g], tile, rev) for g in grad_rows] + [_full_spec(p) for p in params]
    scratch = [pltpu.VMEM((8, rows[h].shape[1]), f32) for h in halos]
    n_in = nr + nh + npar + nc + len(dflat) + len(eflat)
    n_out = len(grad_rows) + npar

    def body(*refs):
        i = pl.program_id(0)
        rv = [_load_f32(r) for r in refs[:nr]]
        keep = (i < n - 1).astype(f32)
        hv = [r[...] * keep for r in refs[nr:nr + nh]]
        pv = [r[...] for r in refs[nr + nh:nr + nh + npar]]
        pos = nr + nh + npar
        cv = [r[...] for r in refs[pos:pos + nc]]
        pos += nc
        dv = []
        for cnt in dcount:
            acc = _load_f32(refs[pos])
            for q in range(1, cnt):
                acc = acc + _load_f32(refs[pos + q])
            dv.append(acc)
            pos += cnt
        ev = []
        for cnt in ecount:
            ev.append([_load_f32(refs[pos + q]) for q in range(cnt)])
            pos += cnt
        out_refs = refs[n_in:n_in + n_out]
        carry_refs = refs[n_in + n_out:]

        def f(gr, gh, gp):
            full = list(rv)
            for g, val in zip(grad_rows, gr):
                full[g] = val
            return tuple(fn(full, gh, gp, cv))

        _, vjp = jax.vjp(f, [rv[g] for g in grad_rows], hv, pv)
        d_rows, d_halos, d_params = vjp(tuple(dv))

        @pl.when(i == 0)
        def _():
            for c_ref in carry_refs:
                c_ref[...] = jnp.zeros_like(c_ref)
            for p_ref in out_refs[len(grad_rows):]:
                p_ref[...] = jnp.zeros_like(p_ref)

        for q, g in enumerate(grad_rows):
            val = d_rows[q]
            for e in ev[q]:
                val = val + e
            out_refs[q][...] = val.astype(out_refs[q].dtype)
            if g in halos:
                hq = list(halos).index(g)
                out_refs[q][tile - 8:tile, :] += carry_refs[hq][...]
                carry_refs[hq][...] = d_halos[hq]
        for p_ref, dp in zip(out_refs[len(grad_rows):], d_params):
            p_ref[...] += dp

    res = pl.pallas_call(
        body, name=name, grid=(n,), in_specs=in_specs, out_specs=out_specs, out_shape=out_shapes,
        scratch_shapes=scratch, compiler_params=_cparams(("arbitrary",)),
    )(*rows, *[rows[h] for h in halos], *params, *consts, *dflat, *eflat)
    return list(res[:len(grad_rows)]), list(res[len(grad_rows):])


def _fn_norm(rows, halos, params, consts):
    return (_rms(rows[0], params[0]),)


def _fn_mla_prep(rows, halos, params, consts):
    cq, ckv, kpe = _lane_split(rows[0], (QL, KVL, MLA_PAD - QL - KVL))
    return _rms(cq, params[0]), _rms(ckv, params[1]), kpe


def _rope(x, cos, sin, rot):
    return x * cos + _exact_r(x, rot, False) * sin


def _fn_qk_post(rows, halos, params, consts):
    q_raw, kn_pad, kpe, cos, sin = rows
    q_norm, k_norm = params
    place, rot = consts
    kpe96 = _exact_r(kpe, place, False)
    qs = [_rope(_rms(qh, q_norm), cos, sin, rot) for qh in _unstack(q_raw)]
    ks = [_rope(_rms(kh + kpe96, k_norm), cos, sin, rot) for kh in _unstack(kn_pad)]
    return jnp.stack(qs, axis=0), jnp.stack(ks, axis=0)


def _seg(x, bd):
    return _exact_r(x, bd, False)


def _make_fn_rwkv_prep(vres):
    def fn(rows, halos, params, consts):
        cols = rows[0]
        bd = consts[0]
        mu, w0, w2, a0, a2, g2, k_k, k_a = params[:8]
        prev = _shift(cols, halos[0], 1)
        c = cols + (prev - cols) * mu
        r, k, v, xw, xa, xg = _lane_split(c, (RW, RW, RW, DL, AL, GL))
        log_w = -_softplus(-(w0 + _mm(jnp.tanh(xw), w2))) - 0.5
        ld = -jnp.exp(log_w)
        a = _sigmoid(a0 + _mm(xa, a2))
        g = _mm(_sigmoid(xg), g2)
        if vres:
            hcur, v_first = rows[1], rows[2]
            v1, v_mu, v0, v2 = params[8:12]
            xv = _mm(hcur, v1)
            xv_prev = _shift(xv, _mm(halos[1], v1), 1)
            xv = xv + (xv_prev - xv) * v_mu
            v = v + (v_first - v) * _sigmoid(v0 + _mm(xv, v2))
        kk = k * k_k
        kk = kk / jnp.maximum(jnp.sqrt(_seg(kk * kk, bd)), 1e-12)
        k2 = k * (1.0 + (a - 1.0) * k_a)
        return r, ld, k2, v, -kk, kk * a, g
    return fn


def _fn_rwkv_post(rows, halos, params, consts):
    y, r, k2, v, g = rows
    ln_w, ln_b, r_k = params
    bd = consts[0]
    mean = _seg(y, bd) * (1.0 / RN)
    d = y - mean
    var = _seg(d * d, bd) * (1.0 / RN)
    yn = d * lax.rsqrt(var + GN_EPS) * ln_w + ln_b
    bonus = _seg(r * k2 * r_k, bd) * v
    return ((yn + bonus) * g,)


def _fn_conv(rows, halos, params, consts):
    cols, halo = rows[0], halos[0]
    w0, w1, w2 = params
    b, c, x = _lane_split(cols, (CW, CW, CW))
    _, ch, xh = _lane_split(halo, (CW, CW, CW))
    u, uh = c * x, ch * xh
    return (b * (w0 * _shift(u, uh, 2) + w1 * _shift(u, uh, 1) + w2 * u),)


def _fn_merge(rows, halos, params, consts):
    gate, o_a, o_b, o_c = rows
    g_a, g_b, g_c = _lane_split(gate, (D, D, D))
    return (_sigmoid(g_a) * o_a + _sigmoid(g_b) * o_b + _sigmoid(g_c) * o_c,)


def _attn_block(q, k, v, q0, diagonal_last):
    tq, kend = q.shape[0], k.shape[0]
    s = _mm_nt(q, k) * (DQK ** -0.5)
    if diagonal_last:
        tri = lax.broadcasted_iota(jnp.int32, (tq, tq), 0) >= lax.broadcasted_iota(jnp.int32, (tq, tq), 1)
        if kend > tq:
            before, diag = _lane_split(s, (kend - tq, tq))
            s = jnp.concatenate([before, jnp.where(tri, diag, -1e30)], axis=-1)
        else:
            s = jnp.where(tri, s, -1e30)
    else:
        row = q0 + lax.broadcasted_iota(jnp.int32, (tq, kend), 0)
        col = lax.broadcasted_iota(jnp.int32, (tq, kend), 1)
        s = jnp.where(row >= col, s, -1e30)
    m = lax.stop_gradient(jnp.max(s, axis=-1, keepdims=True))
    e = jnp.exp(s - m)
    p = e / jnp.sum(e, axis=-1, keepdims=True)
    return _mm(p, v)


def _attn_segments(s_len, tq):
    per = max(1, s_len // tq // ATTN_SEGMENTS)
    return [(first, per, (first + per) * tq) for first in range(0, s_len // tq, per)]


def attn_fwd(q, k, v, *, tq, name):
    h, s_len, _ = q.shape
    outs = []
    for seg, (first, nq, kend) in enumerate(_attn_segments(s_len, tq)):
        def body(q_ref, k_ref, v_ref, o_ref, first=first, nq=nq):
            q0 = (first + pl.program_id(1)) * tq
            o_ref[0] = _attn_block(q_ref[0], k_ref[0], v_ref[0], q0, nq == 1).astype(o_ref.dtype)

        outs.append(pl.pallas_call(
            body, name=f"{name}_{seg}", grid=(h, nq),
            in_specs=[pl.BlockSpec((1, tq, DQK), lambda hh, i, first=first: (hh, first + i, 0)),
                      pl.BlockSpec((1, kend, DQK), lambda hh, i: (hh, 0, 0)),
                      pl.BlockSpec((1, kend, DV), lambda hh, i: (hh, 0, 0))],
            out_specs=pl.BlockSpec((1, tq, DV), lambda hh, i: (hh, i, 0)),
            out_shape=jax.ShapeDtypeStruct((h, nq * tq, DV), bf16),
            compiler_params=_cparams(("parallel", "arbitrary")),
        )(q, k, v))
    return jnp.concatenate(outs, axis=1)


def attn_bwd(q, k, v, do, *, tq, name):
    h, s_len, _ = q.shape
    dqs, dk_acc, dv_acc = [], None, None
    for seg, (first, nq, kend) in reversed(list(enumerate(_attn_segments(s_len, tq)))):
        carried = dk_acc is not None

        def body(*refs, first=first, carried=carried, nq=nq):
            q_ref, k_ref, v_ref, do_ref = refs[:4]
            dq_ref, dk_ref, dv_ref = refs[-3:]
            i = pl.program_id(1)
            _, vjp = jax.vjp(functools.partial(_attn_block, q0=(first + i) * tq, diagonal_last=nq == 1),
                             q_ref[0], k_ref[0], v_ref[0])
            dq, dk, dv = vjp(do_ref[0])
            dq_ref[0] = dq

            @pl.when(i == 0)
            def _():
                dk_ref[0] = dk + refs[4][0] if carried else dk
                dv_ref[0] = dv + refs[5][0] if carried else dv

            @pl.when(i > 0)
            def _():
                dk_ref[0] += dk
                dv_ref[0] += dv

        key_specs = [pl.BlockSpec((1, kend, DQK), lambda hh, i: (hh, 0, 0)),
                     pl.BlockSpec((1, kend, DV), lambda hh, i: (hh, 0, 0))]
        dq, dk_acc, dv_acc = pl.pallas_call(
            body, name=f"{name}_{seg}", grid=(h, nq),
            in_specs=[pl.BlockSpec((1, tq, DQK), lambda hh, i, first=first: (hh, first + i, 0))] + key_specs
            + [pl.BlockSpec((1, tq, DV), lambda hh, i, first=first: (hh, first + i, 0))]
            + (key_specs if carried else []),
            out_specs=[pl.BlockSpec((1, tq, DQK), lambda hh, i: (hh, i, 0))] + key_specs,
            out_shape=[jax.ShapeDtypeStruct((h, nq * tq, DQK), f32), jax.ShapeDtypeStruct((h, s_len, DQK), f32),
                       jax.ShapeDtypeStruct((h, s_len, DV), f32)],
            input_output_aliases={4: 1, 5: 2} if carried else {},
            compiler_params=_cparams(("parallel", "arbitrary")),
        )(q, k, v, do, *([dk_acc, dv_acc] if carried else []))
        dqs.append(dq)
    return jnp.concatenate(dqs[::-1], axis=1), dk_acc, dv_acc


def _wkv_local(r, ld, k, v, a, b):
    nb, c, n = r.shape
    ri = lax.broadcasted_iota(jnp.int32, (c, c), 0)
    ci = lax.broadcasted_iota(jnp.int32, (c, c), 1)
    tri = jnp.broadcast_to((ri >= ci).astype(f32)[None], (nb, c, c))
    cum = _exact_bl(tri, ld, False)
    tot = jnp.sum(ld, axis=1, keepdims=True)
    w_incl, w_excl, w_inv, w_rest = jnp.exp(cum), jnp.exp(cum - ld), jnp.exp(-cum), jnp.exp(tot - cum)
    ab, rb, bb, kb = a * w_excl, r * w_incl, b * w_inv, k * w_inv
    bw, kw = b * w_rest, k * w_rest
    r2 = lax.broadcasted_iota(jnp.int32, (2 * c, 2 * c), 0)
    c2 = lax.broadcasted_iota(jnp.int32, (2 * c, 2 * c), 1)
    t_of, s_of = jnp.where(r2 >= c, r2 - c, r2), jnp.where(c2 >= c, c2 - c, c2)
    keep = jnp.logical_or(t_of > s_of, jnp.logical_and(r2 >= c, t_of == s_of))
    pair = jnp.where(keep[None], _hbnt(jnp.concatenate([ab, rb], axis=1), jnp.concatenate([bb, kb], axis=1)), 0.0)
    on_b, on_k = _lane_split(pair, (c, c))
    l_ab, m_rb = _row_split(on_b, (c, c))
    l_ak_v, m_rk_v = _row_split(_hbnn(on_k, v), (c, c))
    x = jnp.concatenate([ab, l_ak_v], axis=-1)
    lp, span = l_ab, 1
    while span < c:
        x = x + _hbnn(lp, x)
        span *= 2
        if span < c:
            lp = _hbnn(lp, lp)
    via_b_r, via_b_y = _lane_split(_hbnn(m_rb, x), (n, n))
    r_hat = rb + via_b_r
    y0 = via_b_y + m_rk_v
    from_b_g, from_b_z = _row_split(_hbtn(x, bw), (n, n))
    eye = lax.broadcasted_iota(jnp.int32, (n, n), 0) == lax.broadcasted_iota(jnp.int32, (n, n), 1)
    g = jnp.where(eye[None], jnp.exp(tot), 0.0) + from_b_g
    z = from_b_z + _hbtn(v, kw)
    return r_hat, y0, g, z


def _head(h):
    return slice(RN * h, RN * (h + 1))


def _load_chunk_heads(ref, c, per):
    return jnp.stack([ref[c * q:c * (q + 1), _head(h)] for q in range(per) for h in range(RH)], axis=0)


def _store_chunk_heads(ref, val, c, per):
    for q in range(per):
        ref[c * q:c * (q + 1), :] = jnp.concatenate([val[q * RH + h] for h in range(RH)], axis=-1)


def wkv_fwd(r, ld, k, v, a, b, *, name):
    s_len = r.shape[0]
    c = WKV_CHUNK
    n = s_len // c
    per = min(WKV_CHUNKS_PER_STEP, n)
    rows = pl.BlockSpec((c * per, RW), lambda i: (i, 0))
    mats = pl.BlockSpec((per, RH, RN, RN), lambda i: (i, 0, 0, 0))
    rows_t, mats_t = jax.ShapeDtypeStruct((s_len, RW), f32), jax.ShapeDtypeStruct((n, RH, RN, RN), f32)

    def local_body(r_ref, ld_ref, k_ref, v_ref, a_ref, b_ref, rh_ref, y0_ref, g_ref, z_ref):
        r_hat, y0, g, z = _wkv_local(*[_load_chunk_heads(ref, c, per)
                                       for ref in (r_ref, ld_ref, k_ref, v_ref, a_ref, b_ref)])
        _store_chunk_heads(rh_ref, r_hat, c, per)
        _store_chunk_heads(y0_ref, y0, c, per)
        g_ref[...] = g.reshape(per, RH, RN, RN)
        z_ref[...] = z.reshape(per, RH, RN, RN)

    r_hat, y0, g, z = pl.pallas_call(
        local_body, name=name + "_local", grid=(n // per,), in_specs=[rows] * 6, out_specs=[rows, rows, mats, mats],
        out_shape=[rows_t, rows_t, mats_t, mats_t], compiler_params=_cparams(("parallel",)),
    )(r, ld, k, v, a, b)

    def scan_body(g_ref, z_ref, st_ref, s_sc):
        s_sc[...] = jnp.zeros_like(s_sc)

        @pl.loop(0, n)
        def _(i):
            s0 = s_sc[...]
            st_ref[i] = s0
            s_sc[...] = _hbnn(s0, g_ref[i]) + z_ref[i]

    vm = pl.BlockSpec(memory_space=pltpu.VMEM)
    states = pl.pallas_call(
        scan_body, name=name + "_scan", in_specs=[vm, vm], out_specs=vm, out_shape=mats_t,
        scratch_shapes=[pltpu.VMEM((RH, RN, RN), f32)],
        compiler_params=pltpu.CompilerParams(vmem_limit_bytes=VMEM_LIMIT),
    )(g, z)

    def out_body(rh_ref, y0_ref, st_ref, y_ref):
        y = _hbnt(_load_chunk_heads(rh_ref, c, per), st_ref[...].reshape(per * RH, RN, RN))
        _store_chunk_heads(y_ref, y, c, per)
        y_ref[...] += y0_ref[...]

    y = pl.pallas_call(
        out_body, name=name + "_out", grid=(n // per,), in_specs=[rows, rows, mats], out_specs=rows,
        out_shape=rows_t, compiler_params=_cparams(("parallel",)),
    )(r_hat, y0, states)
    return y, dict(r_hat=r_hat, g=g, states=states)


def wkv_bwd(r, ld, k, v, a, b, saved, dy, *, name):
    s_len = r.shape[0]
    c = WKV_CHUNK
    n = s_len // c
    per = min(WKV_CHUNKS_PER_STEP, n)
    rows = pl.BlockSpec((c * per, RW), lambda i: (i, 0))
    mats = pl.BlockSpec((per, RH, RN, RN), lambda i: (i, 0, 0, 0))
    rows_t, mats_t = jax.ShapeDtypeStruct((s_len, RW), f32), jax.ShapeDtypeStruct((n, RH, RN, RN), f32)

    def out_body(dy_ref, rh_ref, st_ref, drh_ref, dsy_ref):
        dyb = _load_chunk_heads(dy_ref, c, per)
        _store_chunk_heads(drh_ref, _hbnn(dyb, st_ref[...].reshape(per * RH, RN, RN)), c, per)
        dsy_ref[...] = _hbtn(dyb, _load_chunk_heads(rh_ref, c, per)).reshape(per, RH, RN, RN)

    d_rhat, ds_y = pl.pallas_call(
        out_body, name=name + "_out", grid=(n // per,), in_specs=[rows, rows, mats], out_specs=[rows, mats],
        out_shape=[rows_t, mats_t], compiler_params=_cparams(("parallel",)),
    )(dy, saved["r_hat"], saved["states"])

    def scan_body(dsy_ref, g_ref, st_ref, dg_ref, dz_ref, ds_sc):
        ds_sc[...] = jnp.zeros_like(ds_sc)

        @pl.loop(0, n)
        def _(i):
            cidx = n - 1 - i
            ds_next = ds_sc[...]
            dz_ref[cidx] = ds_next
            dg_ref[cidx] = _hbtn(st_ref[cidx], ds_next)
            ds_sc[...] = dsy_ref[cidx] + _hbnt(ds_next, g_ref[cidx])

    vm = pl.BlockSpec(memory_space=pltpu.VMEM)
    d_g, d_z = pl.pallas_call(
        scan_body, name=name + "_scan", in_specs=[vm, vm, vm], out_specs=[vm, vm], out_shape=[mats_t, mats_t],
        scratch_shapes=[pltpu.VMEM((RH, RN, RN), f32)],
        compiler_params=pltpu.CompilerParams(vmem_limit_bytes=VMEM_LIMIT),
    )(ds_y, saved["g"], saved["states"])

    def local_body(r_ref, ld_ref, k_ref, v_ref, a_ref, b_ref, drh_ref, dy_ref, dg_ref, dz_ref, *out_refs):
        _, vjp = jax.vjp(_wkv_local, *[_load_chunk_heads(ref, c, per)
                                       for ref in (r_ref, ld_ref, k_ref, v_ref, a_ref, b_ref)])
        grads = vjp((_load_chunk_heads(drh_ref, c, per), _load_chunk_heads(dy_ref, c, per),
                     dg_ref[...].reshape(per * RH, RN, RN), dz_ref[...].reshape(per * RH, RN, RN)))
        for o_ref, val in zip(out_refs, grads):
            _store_chunk_heads(o_ref, val, c, per)

    return pl.pallas_call(
        local_body, name=name + "_local", grid=(n // per,), in_specs=[rows] * 8 + [mats, mats], out_specs=[rows] * 6,
        out_shape=[rows_t] * 6, compiler_params=_cparams(("parallel",)),
    )(r, ld, k, v, a, b, d_rhat, dy, d_g, d_z)


def loss_head(y, target, *, tile, name):
    s_len, d = y.shape
    n = s_len // tile

    def body(y_ref, t_ref, dy_ref, l_ref):
        err = y_ref[...] - t_ref[...]
        dy_ref[...] = err * (1.0 / d)
        part = 0.5 * jnp.sum(jnp.mean(err * err, axis=-1, keepdims=True), axis=0, keepdims=True)

        @pl.when(pl.program_id(0) == 0)
        def _():
            l_ref[...] = jnp.zeros_like(l_ref)

        l_ref[...] += jnp.broadcast_to(part, l_ref.shape)

    bs = pl.BlockSpec((tile, d), lambda i: (i, 0))
    dy, l = pl.pallas_call(
        body, name=name, grid=(n,), in_specs=[bs, bs],
        out_specs=[bs, pl.BlockSpec((8, 128), lambda i: (0, 0))],
        out_shape=[jax.ShapeDtypeStruct((s_len, d), f32), jax.ShapeDtypeStruct((8, 128), f32)],
        compiler_params=_cparams(("arbitrary",)),
    )(y, target)
    return l[0, 0], dy


def _adamw_update(w, g, m, v):
    mn = B1 * m + (1.0 - B1) * g
    vn = B2 * v + (1.0 - B2) * (g * g)
    delta = -LR * ((mn / (1.0 - B1 ** STEP)) / (jnp.sqrt(vn / (1.0 - B2 ** STEP)) + EPS) + WD * w)
    return delta, mn, vn


def adamw(w, g, m, v, *, name):
    rows, cols = w.shape
    tile = _pick(rows, max(8, (2 * 1024 * 1024 // (4 * cols)) // 8 * 8), 8)

    def body(w_ref, g_ref, m_ref, v_ref, g_out, d_out, m_out, v_out):
        gv = g_ref[...]
        d_out[...], m_out[...], v_out[...] = _adamw_update(w_ref[...], gv, m_ref[...], v_ref[...])
        g_out[...] = gv

    bs = pl.BlockSpec((tile, cols), lambda i: (i, 0))
    return pl.pallas_call(
        body, name=name, grid=(rows // tile,), in_specs=[bs] * 4, out_specs=[bs] * 4,
        out_shape=[jax.ShapeDtypeStruct((rows, cols), f32)] * 4, compiler_params=_cparams(("parallel",)),
    )(w, g, m, v)


def adamw_many(ws, gs, ms, vs, *, name):
    n = len(ws)

    def body(*refs):
        ins, outs = refs[:4 * n], refs[4 * n:]
        for i in range(n):
            outs[i][...], outs[n + i][...], outs[2 * n + i][...] = _adamw_update(
                ins[i][...], ins[n + i][...], ins[2 * n + i][...], ins[3 * n + i][...])

    vm = pl.BlockSpec(memory_space=pltpu.VMEM)
    res = pl.pallas_call(
        body, name=name, in_specs=[vm] * (4 * n), out_specs=[vm] * (3 * n),
        out_shape=[jax.ShapeDtypeStruct(w.shape, f32) for w in ws] * 3,
        compiler_params=pltpu.CompilerParams(vmem_limit_bytes=VMEM_LIMIT),
    )(*ws, *gs, *ms, *vs)
    return list(res[:n]), list(res[n:2 * n]), list(res[2 * n:])


def _place():
    return lax.axis_index("x"), lax.axis_index("y"), lax.axis_index("c")


_ANY = pl.BlockSpec(memory_space=pl.ANY)


def _peer_chips(x, y):
    return [(1 - x, y), (x, 1 - y), (1 - x, 1 - y)]


def gather_weights(shards, *, name):
    nk = len(shards)

    def body(*refs):
        srcs, outs = refs[:nk], refs[nk:2 * nk]
        ici_send, ici_recv, d2d_send, d2d_recv = refs[2 * nk + 1:]
        x, y, c = _place()
        me = 2 * x + y
        peers = _peer_chips(x, y)
        pending = []
        for k in range(nk):
            half = srcs[k].shape[0] // 2
            mine = pl.ds(c * half, half)
            for p, (px, py) in enumerate(peers):
                cp = pltpu.make_async_remote_copy(
                    src_ref=srcs[k].at[mine], dst_ref=outs[k].at[me, mine], send_sem=ici_send.at[k, p],
                    recv_sem=ici_recv.at[k, p], device_id=(px, py, c), device_id_type=MESH)
                cp.start()
                pending.append(cp)
        for k in range(nk):
            half = srcs[k].shape[0] // 2
            mine = pl.ds(c * half, half)
            for p, (px, py) in enumerate(peers):
                landed = outs[k].at[2 * px + py, mine]
                pltpu.make_async_remote_copy(
                    src_ref=srcs[k].at[mine], dst_ref=landed, send_sem=ici_send.at[k, p], recv_sem=ici_recv.at[k, p],
                    device_id=(px, py, c), device_id_type=MESH).wait_recv()
                fwd = pltpu.make_async_remote_copy(
                    src_ref=landed, dst_ref=landed, send_sem=d2d_send.at[k, p], recv_sem=d2d_recv.at[k, p],
                    device_id=(x, y, 1 - c), device_id_type=MESH)
                fwd.start()
                pending.append(fwd)
        for k in range(nk):
            half = srcs[k].shape[0] // 2
            other = pl.ds((1 - c) * half, half)
            for p, (px, py) in enumerate(peers):
                theirs = outs[k].at[2 * px + py, other]
                pltpu.make_async_remote_copy(
                    src_ref=theirs, dst_ref=theirs, send_sem=d2d_send.at[k, p], recv_sem=d2d_recv.at[k, p],
                    device_id=(x, y, 1 - c), device_id_type=MESH).wait_recv()
        for cp in pending:
            cp.wait_send()
        refs[2 * nk][...] = jnp.zeros_like(refs[2 * nk])

    sem = lambda *shape: pltpu.SemaphoreType.DMA(shape)
    res = pl.pallas_call(
        body, name=name, in_specs=[_ANY] * nk, out_specs=[_ANY] * nk + [pl.BlockSpec(memory_space=pltpu.VMEM)],
        out_shape=[jax.ShapeDtypeStruct((4,) + s.shape, s.dtype) for s in shards]
        + [jax.ShapeDtypeStruct((8, 128), f32)],
        scratch_shapes=[sem(nk, 3), sem(nk, 3), sem(nk, 3), sem(nk, 3)],
    )(*shards)
    return list(res[:nk]), res[nk]


_HBM = pl.BlockSpec(memory_space=pltpu.HBM)
_SEM = pl.BlockSpec(memory_space=pltpu.SEMAPHORE)


def _ici_half_copies(srcs, lands, send_sems, recv_sems, incoming):
    x, y, c = _place()
    me = 2 * x + y
    out = []
    for k in range(len(srcs)):
        half = srcs[k].shape[0] // 2
        mine = pl.ds(c * half, half)
        for p, (px, py) in enumerate(_peer_chips(x, y)):
            out.append(pltpu.make_async_remote_copy(
                src_ref=srcs[k].at[mine], dst_ref=lands[k].at[(2 * px + py) if incoming else me, mine],
                send_sem=send_sems.at[3 * k + p], recv_sem=recv_sems.at[3 * k + p], device_id=(px, py, c),
                device_id_type=MESH))
    return out


def gather_start(shards, *, name):
    nk = len(shards)

    def body(*refs):
        srcs, lands = refs[:nk], refs[nk:2 * nk]
        send_sems, recv_sems = refs[2 * nk], refs[2 * nk + 1]
        token = refs[-1]
        for outgoing in _ici_half_copies(srcs, lands, send_sems, recv_sems, incoming=False):
            outgoing.start()
        token[...] = jnp.zeros_like(token)

    lands = [pltpu.with_memory_space_constraint(lax.empty((4,) + s.shape, s.dtype), pltpu.HBM) for s in shards]
    res = pl.pallas_call(
        body, name=name,
        out_shape=(pltpu.SemaphoreType.DMA((3 * nk,)), pltpu.SemaphoreType.DMA((3 * nk,)),
                   *[pltpu.HBM(s.shape, s.dtype) for s in shards], *[pltpu.HBM(z.shape, z.dtype) for z in lands],
                   jax.ShapeDtypeStruct((8, 128), f32)),
        in_specs=[_HBM] * (2 * nk), out_specs=(_SEM, _SEM, *[_HBM] * (2 * nk), pl.BlockSpec(memory_space=pltpu.VMEM)),
        input_output_aliases={k: 2 + k for k in range(2 * nk)},
        compiler_params=pltpu.CompilerParams(has_side_effects=pltpu.SideEffectType.DATAFLOW_SIDE_EFFECTING),
    )(*[pltpu.with_memory_space_constraint(s, pltpu.HBM) for s in shards], *lands)
    return res[0], res[1], list(res[2:2 + nk]), list(res[2 + nk:2 + 2 * nk]), res[-1]


def gather_wait(send_sems, recv_sems, shards, lands, after, *, name):
    nk = len(shards)

    def body(*refs):
        srcs, zones = refs[:nk], refs[nk:2 * nk]
        for outgoing in _ici_half_copies(srcs, zones, refs[2 * nk], refs[2 * nk + 1], incoming=False):
            outgoing.wait_send()
        for landed in _ici_half_copies(srcs, zones, refs[2 * nk], refs[2 * nk + 1], incoming=True):
            landed.wait_recv()

    res = pl.pallas_call(
        body, name=name,
        out_shape=(*[pltpu.HBM(s.shape, s.dtype) for s in shards], *[pltpu.HBM(z.shape, z.dtype) for z in lands]),
        in_specs=[_HBM] * (2 * nk) + [_SEM, _SEM, _ANY], out_specs=tuple([_HBM] * (2 * nk)),
        input_output_aliases={k: k for k in range(2 * nk)},
        compiler_params=pltpu.CompilerParams(has_side_effects=pltpu.SideEffectType.DATAFLOW_SIDE_EFFECTING),
    )(*shards, *lands, send_sems, recv_sems, after)
    return list(res[:nk]), list(res[nk:])


def gather_forward(lands, *, name):
    nk = len(lands)

    def body(*refs):
        zones = refs[nk:2 * nk]
        send_sems, recv_sems = refs[2 * nk:]
        x, y, c = _place()
        sends = []
        for k in range(nk):
            half = zones[k].shape[1] // 2
            for p, (px, py) in enumerate(_peer_chips(x, y)):
                landed = zones[k].at[2 * px + py, pl.ds(c * half, half)]
                cp = pltpu.make_async_remote_copy(
                    src_ref=landed, dst_ref=landed, send_sem=send_sems.at[k, p], recv_sem=recv_sems.at[k, p],
                    device_id=(x, y, 1 - c), device_id_type=MESH)
                cp.start()
                sends.append(cp)
        for k in range(nk):
            half = zones[k].shape[1] // 2
            for p, (px, py) in enumerate(_peer_chips(x, y)):
                theirs = zones[k].at[2 * px + py, pl.ds((1 - c) * half, half)]
                pltpu.make_async_remote_copy(
                    src_ref=theirs, dst_ref=theirs, send_sem=send_sems.at[k, p], recv_sem=recv_sems.at[k, p],
                    device_id=(x, y, 1 - c), device_id_type=MESH).wait_recv()
        for cp in sends:
            cp.wait_send()

    return pl.pallas_call(
        body, name=name, in_specs=[_ANY] * nk, out_specs=[_ANY] * nk,
        out_shape=[jax.ShapeDtypeStruct(z.shape, z.dtype) for z in lands],
        input_output_aliases={k: k for k in range(nk)},
        scratch_shapes=[pltpu.SemaphoreType.DMA((nk, 3)), pltpu.SemaphoreType.DMA((nk, 3))],
    )(*lands)


def grads_to_sibling(parts, *, name):
    nk = len(parts)

    def body(*refs):
        srcs, outs = refs[:nk], refs[nk:2 * nk]
        send_sems, recv_sems = refs[2 * nk:]
        x, y, c = _place()
        sends = []
        for k in range(nk):
            half = srcs[k].shape[1] // 2
            cp = pltpu.make_async_remote_copy(
                src_ref=srcs[k].at[:, pl.ds((1 - c) * half, half), :], dst_ref=outs[k], send_sem=send_sems.at[k],
                recv_sem=recv_sems.at[k], device_id=(x, y, 1 - c), device_id_type=MESH)
            cp.start()
            sends.append(cp)
        for cp in sends:
            cp.wait_recv()
        for cp in sends:
            cp.wait_send()

    return pl.pallas_call(
        body, name=name, in_specs=[_ANY] * nk, out_specs=[_ANY] * nk,
        out_shape=[jax.ShapeDtypeStruct((4, p.shape[1] // 2, p.shape[2]), p.dtype) for p in parts],
        scratch_shapes=[pltpu.SemaphoreType.DMA((nk,)), pltpu.SemaphoreType.DMA((nk,))],
    )(*parts)


def pair_sum(part, theirs, core, *, name):
    _, rows, cols = part.shape
    half = rows // 2
    tile = _pick(half, max(16, (1 << 20) // (4 * cols) // 16 * 16), 16)
    per = half // tile

    def body(c_ref, p_ref, t_ref, o_ref):
        o_ref[...] = (p_ref[...].astype(f32) + t_ref[...].astype(f32)).astype(bf16)

    grid_spec = pltpu.PrefetchScalarGridSpec(
        num_scalar_prefetch=1, grid=(4, per),
        in_specs=[pl.BlockSpec((1, tile, cols), lambda j, i, c_ref: (j, c_ref[0] * per + i, 0)),
                  pl.BlockSpec((1, tile, cols), lambda j, i, c_ref: (j, i, 0))],
        out_specs=pl.BlockSpec((1, tile, cols), lambda j, i, c_ref: (j, i, 0)))
    return pl.pallas_call(
        body, name=name, grid_spec=grid_spec, out_shape=jax.ShapeDtypeStruct((4, half, cols), bf16),
        compiler_params=_cparams(("parallel", "parallel")),
    )(core, part, theirs)


def _all_to_all_copies(srcs, lands, send_sems, recv_sems, incoming):
    x, y, c = _place()
    me = 2 * x + y
    out = []
    for k in range(len(srcs)):
        for p, (px, py) in enumerate(_peer_chips(x, y)):
            peer = 2 * px + py
            out.append(pltpu.make_async_remote_copy(
                src_ref=srcs[k].at[peer], dst_ref=lands[k].at[peer if incoming else me],
                send_sem=send_sems.at[3 * k + p], recv_sem=recv_sems.at[3 * k + p], device_id=(px, py, c),
                device_id_type=MESH))
    return out


def scatter_start(parts, *, name):
    nk = len(parts)

    def body(*refs):
        srcs, lands = refs[:nk], refs[nk:2 * nk]
        for outgoing in _all_to_all_copies(srcs, lands, refs[2 * nk], refs[2 * nk + 1], incoming=False):
            outgoing.start()
        refs[-1][...] = jnp.zeros_like(refs[-1])

    lands = [pltpu.with_memory_space_constraint(lax.empty(p.shape, p.dtype), pltpu.HBM) for p in parts]
    res = pl.pallas_call(
        body, name=name,
        out_shape=(pltpu.SemaphoreType.DMA((3 * nk,)), pltpu.SemaphoreType.DMA((3 * nk,)),
                   *[pltpu.HBM(p.shape, p.dtype) for p in parts], *[pltpu.HBM(p.shape, p.dtype) for p in parts],
                   jax.ShapeDtypeStruct((8, 128), f32)),
        in_specs=[_HBM] * (2 * nk), out_specs=(_SEM, _SEM, *[_HBM] * (2 * nk), pl.BlockSpec(memory_space=pltpu.VMEM)),
        input_output_aliases={k: 2 + k for k in range(2 * nk)},
        compiler_params=pltpu.CompilerParams(has_side_effects=pltpu.SideEffectType.DATAFLOW_SIDE_EFFECTING),
    )(*[pltpu.with_memory_space_constraint(p, pltpu.HBM) for p in parts], *lands)
    return res[0], res[1], list(res[2:2 + nk]), list(res[2 + nk:2 + 2 * nk]), res[-1]


def scatter_wait(send_sems, recv_sems, parts, lands, after, *, name):
    nk = len(parts)

    def body(*refs):
        srcs, zones = refs[:nk], refs[nk:2 * nk]
        for outgoing in _all_to_all_copies(srcs, zones, refs[2 * nk], refs[2 * nk + 1], incoming=False):
            outgoing.wait_send()
        for landed in _all_to_all_copies(srcs, zones, refs[2 * nk], refs[2 * nk + 1], incoming=True):
            landed.wait_recv()

    res = pl.pallas_call(
        body, name=name,
        out_shape=(*[pltpu.HBM(p.shape, p.dtype) for p in parts], *[pltpu.HBM(z.shape, z.dtype) for z in lands]),
        in_specs=[_HBM] * (2 * nk) + [_SEM, _SEM, _ANY], out_specs=tuple([_HBM] * (2 * nk)),
        input_output_aliases={k: k for k in range(2 * nk)},
        compiler_params=pltpu.CompilerParams(has_side_effects=pltpu.SideEffectType.DATAFLOW_SIDE_EFFECTING),
    )(*parts, *lands, send_sems, recv_sems, after)
    return list(res[:nk]), list(res[nk:])


def join_halves(bufs, layout, *, name):
    nk, nb = len(layout), len(bufs)

    def body(*refs):
        outs = refs[nb:2 * nb]
        send_sems, recv_sems = refs[2 * nb:]
        x, y, c = _place()
        pending = []
        for k, (o, off, rows) in enumerate(layout):
            half = rows // 2
            mine = outs[o].at[pl.ds(off + c * half, half), :]
            cp = pltpu.make_async_remote_copy(
                src_ref=mine, dst_ref=mine, send_sem=send_sems.at[k], recv_sem=recv_sems.at[k],
                device_id=(x, y, 1 - c), device_id_type=MESH)
            cp.start()
            pending.append(cp)
        for k, (o, off, rows) in enumerate(layout):
            half = rows // 2
            theirs = outs[o].at[pl.ds(off + (1 - c) * half, half), :]
            pltpu.make_async_remote_copy(
                src_ref=theirs, dst_ref=theirs, send_sem=send_sems.at[k], recv_sem=recv_sems.at[k],
                device_id=(x, y, 1 - c), device_id_type=MESH).wait_recv()
        for cp in pending:
            cp.wait_send()

    return pl.pallas_call(
        body, name=name, in_specs=[_ANY] * nb, out_specs=[_ANY] * nb,
        out_shape=[jax.ShapeDtypeStruct(b.shape, b.dtype) for b in bufs],
        input_output_aliases={o: o for o in range(nb)},
        scratch_shapes=[pltpu.SemaphoreType.DMA((nk,)), pltpu.SemaphoreType.DMA((nk,))],
    )(*bufs)


def place_slab(dest, src, index, *, name):
    rows, cols = src.shape
    tile = _pick(rows, max(16, (1 << 20) // (src.dtype.itemsize * cols) // 16 * 16), 16)

    def body(i_ref, s_ref, d_ref, o_ref):
        del i_ref, d_ref
        o_ref[0] = s_ref[...]

    grid_spec = pltpu.PrefetchScalarGridSpec(
        num_scalar_prefetch=1, grid=(rows // tile,),
        in_specs=[pl.BlockSpec((tile, cols), lambda i, idx: (i, 0)), _ANY],
        out_specs=pl.BlockSpec((1, tile, cols), lambda i, idx: (idx[0], i, 0)))
    return pl.pallas_call(
        body, name=name, grid_spec=grid_spec, out_shape=jax.ShapeDtypeStruct(dest.shape, dest.dtype),
        input_output_aliases={2: 0}, compiler_params=_cparams(("parallel",)),
    )(index, src, dest)


def _broadcast_copies(src, land, send_sems, recv_sems, incoming):
    x, y, c = _place()
    me = 4 * x + 2 * y + c
    out = []
    for m in range(1, 8):
        px, py, pc = x ^ (m >> 2), y ^ ((m >> 1) & 1), c ^ (m & 1)
        out.append(pltpu.make_async_remote_copy(
            src_ref=src, dst_ref=land.at[(4 * px + 2 * py + pc) if incoming else me], send_sem=send_sems.at[m - 1],
            recv_sem=recv_sems.at[m - 1], device_id=(px, py, pc), device_id_type=MESH))
    return out


def broadcast_start(src, *, name):
    def body(s_ref, l_ref, send_sems, recv_sems, s_thru, l_thru, token):
        for outgoing in _broadcast_copies(s_ref, l_ref, send_sems, recv_sems, incoming=False):
            outgoing.start()
        token[...] = jnp.zeros_like(token)

    land = pltpu.with_memory_space_constraint(lax.empty((8,) + src.shape, src.dtype), pltpu.HBM)
    return pl.pallas_call(
        body, name=name,
        out_shape=(pltpu.SemaphoreType.DMA((7,)), pltpu.SemaphoreType.DMA((7,)), pltpu.HBM(src.shape, src.dtype),
                   pltpu.HBM(land.shape, land.dtype), jax.ShapeDtypeStruct((8, 128), f32)),
        in_specs=[_HBM, _HBM], out_specs=(_SEM, _SEM, _HBM, _HBM, pl.BlockSpec(memory_space=pltpu.VMEM)),
        input_output_aliases={0: 2, 1: 3},
        compiler_params=pltpu.CompilerParams(has_side_effects=pltpu.SideEffectType.DATAFLOW_SIDE_EFFECTING),
    )(pltpu.with_memory_space_constraint(src, pltpu.HBM), land)


def broadcast_wait(send_sems, recv_sems, src, land, after, *, name):
    def body(s_ref, l_ref, send_sems, recv_sems, after_ref, s_out, l_out):
        for outgoing in _broadcast_copies(s_ref, l_ref, send_sems, recv_sems, incoming=False):
            outgoing.wait_send()
        for landed in _broadcast_copies(s_ref, l_ref, send_sems, recv_sems, incoming=True):
            landed.wait_recv()

    return pl.pallas_call(
        body, name=name, out_shape=(pltpu.HBM(src.shape, src.dtype), pltpu.HBM(land.shape, land.dtype)),
        in_specs=[_HBM, _HBM, _SEM, _SEM, _ANY], out_specs=(_HBM, _HBM), input_output_aliases={0: 0, 1: 1},
        compiler_params=pltpu.CompilerParams(has_side_effects=pltpu.SideEffectType.DATAFLOW_SIDE_EFFECTING),
    )(src, land, send_sems, recv_sems, after)


def sum8(land, own, device, *, name):
    _, rows, cols = land.shape

    def body(d_ref, l_ref, o_ref, out_ref):
        mine = o_ref[...]
        acc = jnp.where(d_ref[0] == 0, mine, l_ref[0])
        for d in range(1, 8):
            acc = acc + jnp.where(d_ref[0] == d, mine, l_ref[d])
        out_ref[...] = acc

    grid_spec = pltpu.PrefetchScalarGridSpec(
        num_scalar_prefetch=1, grid=(1,),
        in_specs=[pl.BlockSpec((8, rows, cols), lambda i, d_ref: (0, 0, 0)),
                  pl.BlockSpec((rows, cols), lambda i, d_ref: (0, 0))],
        out_specs=pl.BlockSpec((rows, cols), lambda i, d_ref: (0, 0)))
    return pl.pallas_call(
        body, name=name, grid_spec=grid_spec, out_shape=jax.ShapeDtypeStruct((rows, cols), f32),
        compiler_params=_cparams(("arbitrary",)),
    )(device, land, own)


def sum4_into(arrived, own, dest, where, *, layer, total_rows, name):
    _, rows, cols = arrived.shape
    tile = _pick(rows, max(16, (1 << 20) // (4 * cols) // 16 * 16), 16)
    per = rows // tile

    def body(w_ref, a_ref, own_ref, *rest):
        mine = own_ref[0].astype(f32)
        p = [jnp.where(w_ref[1] == j, mine, a_ref[j].astype(f32)) for j in range(4)]
        rest[-1][...] = ((p[0] + p[1]) + p[2]) + p[3]

    grid_spec = pltpu.PrefetchScalarGridSpec(
        num_scalar_prefetch=1, grid=(per,),
        in_specs=[pl.BlockSpec((4, tile, cols), lambda i, w_ref: (0, i, 0)),
                  pl.BlockSpec((1, tile, cols), lambda i, w_ref: (w_ref[1], i, 0))] + ([] if dest is None else [_ANY]),
        out_specs=pl.BlockSpec((tile, cols), lambda i, w_ref: ((2 * layer + w_ref[0]) * per + i, 0)))
    return pl.pallas_call(
        body, name=name, grid_spec=grid_spec, out_shape=jax.ShapeDtypeStruct((total_rows, cols), f32),
        input_output_aliases={} if dest is None else {3: 0}, compiler_params=_cparams(("parallel",)),
    )(where, arrived, own, *([] if dest is None else [dest]))


def _consts():
    idx = np.arange(RW)
    bd = (idx[:, None] // RN == idx[None, :] // RN).astype(np.float32)
    place = np.zeros((128, DQK), np.float32)
    place[np.arange(ROPE), NOPE + np.arange(ROPE)] = 1.0
    rot = np.zeros((DQK, DQK), np.float32)
    half = ROPE // 2
    rot[NOPE + half + np.arange(half), NOPE + np.arange(half)] = -1.0
    rot[NOPE + np.arange(half), NOPE + half + np.arange(half)] = 1.0
    return jnp.asarray(bd), jnp.asarray(place), jnp.asarray(rot)


def _rope_tables(positions):
    freqs = ROPE_THETA ** (-(jnp.arange(ROPE // 2, dtype=f32) * 2.0 / ROPE))
    ang = positions.astype(f32)[:, None] * freqs
    cos, sin = jnp.cos(ang), jnp.sin(ang)
    ones = jnp.ones((positions.shape[0], NOPE), f32)
    return (jnp.concatenate([ones, cos, cos], axis=-1), jnp.concatenate([0.0 * ones, sin, sin], axis=-1))


STAGES = (("w_in",), ("mla_wq_b", "mla_wkv_b", "mla_w_o", "rwkv_w_o", "conv_w_o", "w_out"), ("w_up", "w_down"))


def derive_stage(stage, w):
    if stage == 0:
        w_in = w["w_in"]
        pad = jnp.zeros((D, MLA_PAD - MLA_COLS), w_in.dtype)
        return dict(gate=w_in[:, :GATE], mla=jnp.concatenate([w_in[:, GATE:GATE + MLA_COLS], pad], axis=1),
                    rw=w_in[:, GATE + MLA_COLS:GATE + MLA_COLS + 4 * RW], cv=w_in[:, GATE + MLA_COLS + 4 * RW:])
    if stage == 1:
        wkv = w["mla_wkv_b"].reshape(KVL, MLA_H, NOPE + DV)
        wk = jnp.concatenate([wkv[:, :, :NOPE], jnp.zeros((KVL, MLA_H, ROPE), wkv.dtype)], axis=-1)
        return dict(wq=w["mla_wq_b"].reshape(QL, MLA_H, DQK).transpose(1, 0, 2), wk=wk.transpose(1, 0, 2),
                    wv=wkv[:, :, NOPE:].transpose(1, 0, 2), wo=w["mla_w_o"].reshape(MLA_H, DV, D),
                    rwo=w["rwkv_w_o"], cvo=w["conv_w_o"], out=w["w_out"])
    return dict(up=w["w_up"], down=w["w_down"])


W_IN_WINDOW_TILE = (0, 10, 21, 31)
W_IN_WINDOW = 1664
W_IN_SHARD = 1384


def w_in_window_cols(win, chip):
    gap = MLA_PAD - MLA_COLS
    branches = []
    for j in range(4):
        lo, hi = W_IN_SHARD * j, W_IN_SHARD * (j + 1)
        base = 128 * W_IN_WINDOW_TILE[j]
        cut = GATE + MLA_COLS
        if hi <= cut:
            branches.append(lambda w, a=lo - base: w[:, a:a + W_IN_SHARD])
        elif lo >= cut:
            branches.append(lambda w, a=lo + gap - base: w[:, a:a + W_IN_SHARD])
        else:
            branches.append(lambda w, a=lo - base, n1=cut - lo, b=cut + gap - base, n2=hi - cut:
                            jnp.concatenate([w[:, a:a + n1], w[:, b:b + n2]], axis=1))
    return lax.switch(chip, branches, win)


def chip_major_grads(stage, g):
    if stage == 0:
        padded = jnp.concatenate([g["gate"], g["mla"], g["rw"], g["cv"]], axis=1)
        return dict(w_in=jnp.stack([padded[:, 128 * t:128 * t + W_IN_WINDOW] for t in W_IN_WINDOW_TILE]))
    if stage == 1:
        heads = MLA_H // 4
        wq = g["wq"].reshape(4, heads, QL, DQK).transpose(0, 2, 1, 3).reshape(4, QL, heads * DQK)
        wkv = jnp.concatenate([g["wk"][:, :, :NOPE], g["wv"]], axis=-1)
        wkv = wkv.reshape(4, heads, KVL, NOPE + DV).transpose(0, 2, 1, 3).reshape(4, KVL, heads * (NOPE + DV))
        return dict(mla_wq_b=wq, mla_wkv_b=wkv, mla_w_o=g["wo"].reshape(4, MLA_H * DV, D // 4),
                    rwkv_w_o=g["rwo"], conv_w_o=g["cvo"], w_out=g["out"].reshape(4, D // 4, D))
    return dict(w_up=g["up"], w_down=g["down"].reshape(4, DFF // 4, D))


def _row(v):
    return v.reshape(1, -1)


def local_step(x, positions, target, w, sm, big_of=None, on_grads=None, on_small=None):
    if big_of is None:
        big_of = lambda l, stage, _after: {n: w[n][l] for n in STAGES[stage]}
    if on_grads is None:
        on_grads = lambda l, stage, slabs: None
    if on_small is None:
        on_small = lambda l, layer_small: None
    s_len = x.shape[0]
    t_row = _pick(s_len, 256, 8)
    t_wide = _pick(s_len, 128, 8)
    bd, place, rot = _consts()
    cos, sin = _rope_tables(positions)
    sds = lambda *shape: jax.ShapeDtypeStruct(shape, f32)
    sdb = lambda *shape: jax.ShapeDtypeStruct(shape, bf16)
    saved = []
    v_first = None
    for l in range(DEPTH):
        tag = f"l{l}_"
        lw = derive_stage(0, big_of(l, 0, x))
        vres = l > 0
        p_norm1 = [_row(sm["attn_norm"][l])]
        (h,) = rows_fwd(_fn_norm, [x], p_norm1, [], [sds(s_len, D)], tile=t_row, name=tag + "norm1")
        gate = mm(h, lw["gate"], name=tag + "proj_gate")
        mla = mm(h, lw["mla"], name=tag + "proj_mla")
        rwc = mm(h, lw["rw"], name=tag + "proj_rwkv")
        cvc = mm(h, lw["cv"], name=tag + "proj_conv")
        lw.update(derive_stage(1, big_of(l, 1, cvc)))
        p_mla = [_row(sm["mla_q_a_norm"][l]), _row(sm["mla_kv_a_norm"][l])]
        qn, kvn, kpe = rows_fwd(_fn_mla_prep, [mla], p_mla, [], [sdb(s_len, QL), sdb(s_len, KVL), sds(s_len, 128)],
                                tile=t_row, name=tag + "mla_prep")
        q_raw = mm(qn, lw["wq"], b_batched=True, name=tag + "q_proj")
        kn_pad = mm(kvn, lw["wk"], b_batched=True, name=tag + "k_proj")
        vv = mm(kvn, lw["wv"], b_batched=True, name=tag + "v_proj")
        p_qk = [_row(sm["mla_q_norm"][l]), _row(sm["mla_k_norm"][l])]
        q, k = rows_fwd(_fn_qk_post, [q_raw, kn_pad, kpe, cos, sin], p_qk, [place, rot],
                        [sds(MLA_H, s_len, DQK), sds(MLA_H, s_len, DQK)], tile=t_wide, name=tag + "qk_post")
        o = attn_fwd(q, k, vv, tq=_pick(s_len, 256, 8), name=tag + "attn")
        o_a = mm(o, lw["wo"], a_batched=True, b_batched=True, reduce_batch=True, name=tag + "o_a")
        p_rw = [_row(sm["rwkv_mu"][l]), _row(sm["rwkv_w0"][l]), w["rwkv_w2"][l], _row(sm["rwkv_a0"][l]),
                w["rwkv_a2"][l], w["rwkv_g2"][l], _row(sm["rwkv_k_k"][l]), _row(sm["rwkv_k_a"][l])]
        rw_rows, rw_halos = [rwc], (0,)
        if vres:
            p_rw += [w["rwkv_v1"][l - 1], _row(sm["rwkv_v_mu"][l - 1]), _row(sm["rwkv_v0"][l - 1]), w["rwkv_v2"][l - 1]]
            rw_rows, rw_halos = [rwc, h, v_first], (0, 1)
        fn_prep = _make_fn_rwkv_prep(vres)
        r, ld, k2, v, an, bn, g = rows_fwd(fn_prep, rw_rows, p_rw, [bd], [sds(s_len, RW)] * 7, tile=t_row,
                                           name=tag + "rwkv_prep", halos=rw_halos)
        if not vres:
            v_first = v
        y, states = wkv_fwd(r, ld, k2, v, an, bn, name=tag + "wkv")
        p_post = [_row(sm["rwkv_ln_w"][l]), _row(sm["rwkv_ln_b"][l]), _row(sm["rwkv_r_k"][l])]
        (yb,) = rows_fwd(_fn_rwkv_post, [y, r, k2, v, g], p_post, [bd], [sdb(s_len, RW)], tile=t_row,
                         name=tag + "rwkv_post")
        o_b = mm(yb, lw["rwo"], name=tag + "o_b")
        p_cv = [w["conv_w"][l][q:q + 1] for q in range(3)]
        (yc,) = rows_fwd(_fn_conv, [cvc], p_cv, [], [sdb(s_len, CW)], tile=t_row, name=tag + "conv", halos=(0,))
        o_c = mm(yc, lw["cvo"], name=tag + "o_c")
        (merged,) = rows_fwd(_fn_merge, [gate, o_a, o_b, o_c], [], [], [sdb(s_len, D)], tile=t_wide,
                             name=tag + "merge")
        x1 = mm(merged, lw["out"], add=x, name=tag + "out_proj")
        lw.update(derive_stage(2, big_of(l, 2, x1)))
        p_norm2 = [_row(sm["mlp_norm"][l])]
        (h2,) = rows_fwd(_fn_norm, [x1], p_norm2, [], [sdb(s_len, D)], tile=t_row, name=tag + "norm2")
        up, act = mm(h2, lw["up"], relu2_out=True, name=tag + "up")
        x2 = mm(act, lw["down"], add=x1, name=tag + "down")
        saved.append(dict(lw=lw, x=x, h=h, gate=gate, mla=mla, rwc=rwc, cvc=cvc, qn=qn, kvn=kvn, kpe=kpe,
                          q_raw=q_raw, kn_pad=kn_pad, vv=vv, q=q, k=k, o=o, o_a=o_a, r=r, ld=ld, k2=k2, v=v,
                          an=an, bn=bn, g=g, y=y, states=states, yb=yb, o_b=o_b, yc=yc, o_c=o_c, merged=merged,
                          x1=x1, h2=h2, up=up, act=act, p_norm1=p_norm1, p_mla=p_mla, p_qk=p_qk, p_rw=p_rw,
                          p_post=p_post, p_cv=p_cv, p_norm2=p_norm2, rw_rows=rw_rows, rw_halos=rw_halos,
                          fn_prep=fn_prep, v_first=v_first if vres else None))
        x = x2

    loss, dx = loss_head(x, target, tile=t_row, name="loss_head")

    grads = {n: [None] * (DEPTH - 1 if n in ("rwkv_v1", "rwkv_v_mu", "rwkv_v0", "rwkv_v2") else DEPTH)
             for n in WEIGHTS}
    dv_first = None
    for l in reversed(range(DEPTH)):
        tag = f"b{l}_"
        sv = saved[l]
        lw = sv["lw"]
        vres = l > 0
        g_down = mm(sv["act"], dx, ta=True, out_dtype=bf16, name=tag + "g_down")
        dup = mm(dx, lw["down"], tb=True, act_grad=sv["up"], out_dtype=bf16, name=tag + "d_up")
        g_up = mm(sv["h2"], dup, ta=True, n_split=4, out_dtype=bf16, name=tag + "g_up")
        dh2 = mm(dup, lw["up"], tb=True, name=tag + "d_h2")
        slabs = chip_major_grads(2, dict(up=g_up, down=g_down))
        token = on_grads(l, 2, slabs)
        p_norm2 = sv["p_norm2"] if token is None else [sv["p_norm2"][0] + token[0, 0]]
        (dx1,), (g_n2,) = rows_bwd(_fn_norm, [sv["x1"]], p_norm2, [], [[dh2]], tile=t_row,
                                   name=tag + "norm2", extra={0: [dx]})
        g_out = mm(sv["merged"], dx1, ta=True, out_dtype=bf16, name=tag + "g_out")
        dmerged = mm(dx1, lw["out"], tb=True, name=tag + "d_merged")
        (dgate, do_a, do_b, do_c), _ = rows_bwd(_fn_merge, [sv["gate"], sv["o_a"], sv["o_b"], sv["o_c"]], [], [],
                                                [[dmerged]], tile=t_wide, name=tag + "merge",
                                                grad_dtypes=[bf16] * 4)
        g_cvo = mm(sv["yc"], do_c, ta=True, n_split=4, out_dtype=bf16, name=tag + "g_cvo")
        dyc = mm(do_c, lw["cvo"], tb=True, name=tag + "d_yc")
        (dcvc,), g_cw = rows_bwd(_fn_conv, [sv["cvc"]], sv["p_cv"], [], [[dyc]], tile=t_row, name=tag + "conv",
                                    halos=(0,))
        g_rwo = mm(sv["yb"], do_b, ta=True, n_split=4, out_dtype=bf16, name=tag + "g_rwo")
        dyb = mm(do_b, lw["rwo"], tb=True, name=tag + "d_yb")
        (dy, dr_p, dk_p, dv_p, dg), g_post = rows_bwd(
            _fn_rwkv_post, [sv["y"], sv["r"], sv["k2"], sv["v"], sv["g"]], sv["p_post"], [bd], [[dyb]], tile=t_row,
            name=tag + "rwkv_post")
        dr_s, dld, dk_s, dv_s, dan, dbn = wkv_bwd(sv["r"], sv["ld"], sv["k2"], sv["v"], sv["an"], sv["bn"],
                                                  sv["states"], dy, name=tag + "wkv")
        dv_list = [dv_s, dv_p] + ([dv_first] if (not vres and dv_first is not None) else [])
        d_prep, g_prep = rows_bwd(
            sv["fn_prep"], sv["rw_rows"], sv["p_rw"], [bd],
            [[dr_s, dr_p], [dld], [dk_s, dk_p], dv_list, [dan], [dbn], [dg]], tile=t_row, name=tag + "rwkv_prep",
            halos=sv["rw_halos"])
        drwc = d_prep[0]
        dh_extra = []
        if vres:
            dh_extra = [d_prep[1]]
            dv_first = d_prep[2]
        g_wo = mm(sv["o"], do_a, ta=True, a_batched=True, n_split=4, tk=s_len, out_dtype=bf16, name=tag + "g_wo")
        do = mm(do_a, lw["wo"], tb=True, b_batched=True, name=tag + "d_o")
        dq, dk, dvv = attn_bwd(sv["q"], sv["k"], sv["vv"], do, tq=_pick(s_len, 256, 8), name=tag + "attn")
        (dq_raw, dkn_pad, dkpe), g_qk = rows_bwd(
            _fn_qk_post, [sv["q_raw"], sv["kn_pad"], sv["kpe"], cos, sin], sv["p_qk"], [place, rot], [[dq], [dk]],
            tile=t_wide, name=tag + "qk_post", grad_rows=[0, 1, 2], grad_dtypes=[bf16, bf16, f32])
        g_wq = mm(sv["qn"], dq_raw, ta=True, b_batched=True, tk=s_len, out_dtype=bf16, name=tag + "g_wq")
        g_wk = mm(sv["kvn"], dkn_pad, ta=True, b_batched=True, tk=s_len, out_dtype=bf16, name=tag + "g_wk")
        g_wv = mm(sv["kvn"], dvv, ta=True, b_batched=True, tk=s_len, out_dtype=bf16, name=tag + "g_wv")
        dqn = mm(dq_raw, lw["wq"], tb=True, a_batched=True, b_batched=True, reduce_batch=True, name=tag + "d_qn")
        dkvn = mm(dkn_pad, lw["wk"], tb=True, a_batched=True, b_batched=True, reduce_batch=True, name=tag + "d_kvn_k")
        dkvn = mm(dvv, lw["wv"], tb=True, a_batched=True, b_batched=True, reduce_batch=True, add=dkvn,
                  name=tag + "d_kvn_v")
        slabs.update(chip_major_grads(1, dict(wq=g_wq, wk=g_wk, wv=g_wv, wo=g_wo, rwo=g_rwo, cvo=g_cvo, out=g_out)))
        token = on_grads(l, 1, {n: slabs[n] for n in STAGES[1]})
        p_mla = sv["p_mla"] if token is None else [sv["p_mla"][0] + token[0, 0], sv["p_mla"][1]]
        (dmla,), g_mla = rows_bwd(_fn_mla_prep, [sv["mla"]], p_mla, [], [[dqn], [dkvn], [dkpe]], tile=t_row,
                                  name=tag + "mla_prep", grad_dtypes=[bf16])
        g_gate = mm(sv["h"], dgate, ta=True, out_dtype=bf16, name=tag + "g_gate")
        g_mlaw = mm(sv["h"], dmla, ta=True, out_dtype=bf16, name=tag + "g_mla")
        g_rw = mm(sv["h"], drwc, ta=True, out_dtype=bf16, name=tag + "g_rw")
        g_cv = mm(sv["h"], dcvc, ta=True, out_dtype=bf16, name=tag + "g_cv")
        dh = mm(dgate, lw["gate"], tb=True, name=tag + "d_h_gate")
        dh = mm(dmla, lw["mla"], tb=True, add=dh, name=tag + "d_h_mla")
        dh = mm(drwc, lw["rw"], tb=True, add=dh, name=tag + "d_h_rw")
        dh = mm(dcvc, lw["cv"], tb=True, add=dh, name=tag + "d_h_cv")
        (dx,), (g_n1,) = rows_bwd(_fn_norm, [sv["x"]], sv["p_norm1"], [], [[dh] + dh_extra], tile=t_row,
                                  name=tag + "norm1", extra={0: [dx1]})
        slabs.update(chip_major_grads(0, dict(gate=g_gate, mla=g_mlaw, rw=g_rw, cv=g_cv)))
        token = on_grads(l, 0, {n: slabs[n] for n in STAGES[0]})
        if token is not None and l > 0:
            dx = dx + token[0, 0]
        for n, val in slabs.items():
            grads[n][l] = val
        layer_small = [("attn_norm", l, g_n1), ("mlp_norm", l, g_n2), ("mla_q_a_norm", l, g_mla[0]),
                       ("mla_kv_a_norm", l, g_mla[1]), ("mla_q_norm", l, g_qk[0]), ("mla_k_norm", l, g_qk[1]),
                       ("rwkv_ln_w", l, g_post[0]), ("rwkv_ln_b", l, g_post[1]), ("rwkv_r_k", l, g_post[2]),
                       ("conv_w", l, jnp.concatenate(g_cw, axis=0))]
        layer_small += list(zip(["rwkv_mu", "rwkv_w0", "rwkv_w2", "rwkv_a0", "rwkv_a2", "rwkv_g2", "rwkv_k_k",
                                 "rwkv_k_a"], [l] * 8, g_prep[:8]))
        if vres:
            layer_small += list(zip(["rwkv_v1", "rwkv_v_mu", "rwkv_v0", "rwkv_v2"], [l - 1] * 4, g_prep[8:12]))
        for n, index, val in layer_small:
            grads[n][index] = val
        if l == 0:
            layer_small.append(("loss", 0, loss.reshape(1, 1)))
        token = on_small(l, layer_small)
        if token is not None and l > 0:
            dx = dx + token[0, 0]
    return loss, dx, grads


def _split3(a):
    hi = a.astype(bf16)
    r1 = a - hi.astype(f32)
    mid = r1.astype(bf16)
    lo = (r1 - mid.astype(f32)).astype(bf16)
    return hi, mid, lo


def _shard_axis(name):
    return 1 if name in ROW_SHARDED else 2


def _pack(pieces, width, dtype, row_align):
    flat = jnp.concatenate([p.reshape(-1).astype(dtype) for p in pieces])
    rows = -(-flat.shape[0] // width)
    rows = -(-rows // row_align) * row_align
    return jnp.pad(flat, (0, rows * width - flat.shape[0])).reshape(rows, width)


def _unpack(flat2d, shapes):
    flat = flat2d.reshape(-1)
    out, off = [], 0
    for shp in shapes:
        n = int(np.prod(shp))
        out.append(flat[off:off + n].reshape(shp))
        off += n
    return out


def kernel(x, positions, attn_norm, w_in, mla_q_a_norm, mla_wq_b, mla_kv_a_norm, mla_wkv_b, mla_q_norm, mla_k_norm, mla_w_o, rwkv_mu, rwkv_w0, rwkv_w2, rwkv_a0, rwkv_a2, rwkv_g2, rwkv_k_k, rwkv_k_a, rwkv_r_k, rwkv_ln_w, rwkv_ln_b, rwkv_w_o, rwkv_v1, rwkv_v_mu, rwkv_v0, rwkv_v2, conv_w, conv_w_o, w_out, mlp_norm, w_up, w_down, loss_target, m_attn_norm, m_w_in, m_mla_q_a_norm, m_mla_wq_b, m_mla_kv_a_norm, m_mla_wkv_b, m_mla_q_norm, m_mla_k_norm, m_mla_w_o, m_rwkv_mu, m_rwkv_w0, m_rwkv_w2, m_rwkv_a0, m_rwkv_a2, m_rwkv_g2, m_rwkv_k_k, m_rwkv_k_a, m_rwkv_r_k, m_rwkv_ln_w, m_rwkv_ln_b, m_rwkv_w_o, m_rwkv_v1, m_rwkv_v_mu, m_rwkv_v0, m_rwkv_v2, m_conv_w, m_conv_w_o, m_w_out, m_mlp_norm, m_w_up, m_w_down, v_attn_norm, v_w_in, v_mla_q_a_norm, v_mla_wq_b, v_mla_kv_a_norm, v_mla_wkv_b, v_mla_q_norm, v_mla_k_norm, v_mla_w_o, v_rwkv_mu, v_rwkv_w0, v_rwkv_w2, v_rwkv_a0, v_rwkv_a2, v_rwkv_g2, v_rwkv_k_k, v_rwkv_k_a, v_rwkv_r_k, v_rwkv_ln_w, v_rwkv_ln_b, v_rwkv_w_o, v_rwkv_v1, v_rwkv_v_mu, v_rwkv_v0, v_rwkv_v2, v_conv_w, v_conv_w_o, v_w_out, v_mlp_norm, v_w_up, v_w_down):
    args = dict(locals())
    wts = {n: args[n] for n in WEIGHTS}
    mom = {n: args["m_" + n] for n in WEIGHTS}
    var = {n: args["v_" + n] for n in WEIGHTS}
    chip = 2 * lax.axis_index("x") + lax.axis_index("y")
    core = lax.axis_index("c").astype(jnp.int32).reshape(1)

    med_names = [n for n in MED if n != "conv_w"]
    med_pieces = [wts[n] for n in med_names] + list(_split3(wts["conv_w"]))
    med_shapes = [p.shape for p in med_pieces]
    chip_idx = chip.astype(jnp.int32).reshape(1)
    shards_first = [wts[n][0].astype(bf16) for n in STAGES[0]] + [_pack(med_pieces, 128, bf16, 32)]
    got_first, token = gather_weights(shards_first, name="gather_l0_s0")
    got_first = [place_slab(g, s, chip_idx, name=f"place_own_l0_s0_{q}")
                 for q, (g, s) in enumerate(zip(got_first, shards_first))]
    in_flight = {}
    for key, names, l in (("l0_s1", STAGES[1], 0), ("l0_s2", STAGES[2], 0), ("l1", BIG, 1)):
        group = [wts[n][l].astype(bf16) for n in names]
        group[0] = group[0] + token[0, 0].astype(bf16)
        in_flight[key] = (names, gather_start(group, name="gather_start_" + key))
        token = in_flight[key][1][4]

    def whole_of(names, slabs):
        out = {}
        for n, by_chip in zip(names, slabs):
            _, rows, cols = by_chip.shape
            if n in ROW_SHARDED:
                out[n] = by_chip.reshape(4 * rows, cols)
            else:
                out[n] = by_chip.transpose(1, 0, 2).reshape(rows, 4 * cols)
        return out

    landed = {}

    def big_of(l, stage, after):
        if (l, stage) == (0, 0):
            return whole_of(STAGES[0], got_first)
        key = "l1" if l == 1 else f"l0_s{stage}"
        if key not in landed:
            names, (send_sems, recv_sems, thru, lands, _) = in_flight[key]
            thru, lands = gather_wait(send_sems, recv_sems, thru, lands, after, name="gather_wait_" + key)
            lands = gather_forward(lands, name="gather_forward_" + key)
            landed[key] = whole_of(names, [place_slab(g, s, chip_idx, name=f"place_own_{key}_{q}")
                                           for q, (g, s) in enumerate(zip(lands, thru))])
        return {n: landed[key][n] for n in STAGES[stage]}

    whole = {}
    per_chip = [_unpack(got_first[len(STAGES[0])][j], med_shapes) for j in range(4)]
    for q, n in enumerate(med_names):
        whole[n] = jnp.concatenate([per_chip[j][q] for j in range(4)], axis=_shard_axis(n)).astype(f32)
    base = len(med_names)
    cw_parts = [jnp.concatenate([per_chip[j][base + t] for j in range(4)], axis=2).astype(f32) for t in range(3)]
    whole["conv_w"] = (cw_parts[0] + cw_parts[1]) + cw_parts[2]
    small = {n: wts[n] for n in SMALL}
    small["rwkv_r_k"] = wts["rwkv_r_k"].reshape(DEPTH, RW)

    exchanges = []

    def on_grads(l, stage, slabs):
        names = STAGES[stage]
        tag = f"l{l}_s{stage}"
        parts = [slabs[n] for n in names]
        from_sibling = grads_to_sibling(parts, name="grads_to_sibling_" + tag)
        chip_sums = [pair_sum(s, t, core, name=f"pair_sum_{n}_{l}") for n, s, t in zip(names, parts, from_sibling)]
        started = scatter_start(chip_sums, name="scatter_start_" + tag)
        exchanges.append((l, names, tag, started))
        return started[4]

    broadcasts = []

    def on_small(l, layer_small):
        values = [val for _, _, val in layer_small]
        started = broadcast_start(_pack(values, 128, f32, 8), name=f"small_start_l{l}")
        broadcasts.append((l, [(n, index, val.shape) for n, index, val in layer_small], started))
        return started[4]

    small["attn_norm"] = small["attn_norm"] + token[0, 0]
    _, grad_x, grads = local_step(x[0], positions[0], loss_target[0], whole, small, big_of, on_grads, on_small)

    device = (4 * lax.axis_index("x") + 2 * lax.axis_index("y") + lax.axis_index("c")).astype(jnp.int32).reshape(1)
    summed = {n: [None] * len(grads[n]) for n in SMALL + MED}
    summed["loss"] = [None]
    for l, entries, (send_sems, recv_sems, thru, land, _) in broadcasts:
        own, land = broadcast_wait(send_sems, recv_sems, thru, land, grad_x, name=f"small_wait_l{l}")
        total = sum8(land, own, device, name=f"small_sum_l{l}")
        for (n, index, _), val in zip(entries, _unpack(total, [shape for _, _, shape in entries])):
            summed[n][index] = val
    gsum = {}
    for n in SMALL + MED:
        g = jnp.stack(summed[n])
        if n in MED:
            ax = _shard_axis(n)
            width = wts[n].shape[ax]
            g = lax.dynamic_slice_in_dim(g, chip * width, width, axis=ax)
        gsum[n] = g.reshape(wts[n].shape)
    where = jnp.stack([lax.axis_index("c"), chip]).astype(jnp.int32)
    bufs, layout = {}, []
    for l, names, tag, (send_sems, recv_sems, thru, lands, _) in exchanges:
        own, arrived = scatter_wait(send_sems, recv_sems, thru, lands, grad_x, name="scatter_wait_" + tag)
        for n, mine, theirs in zip(names, own, arrived):
            rows = 2 * theirs.shape[1]
            bufs[n] = sum4_into(theirs, mine, bufs.get(n), where, layer=l, total_rows=DEPTH * rows,
                                name=f"sum_chips_{n}_{l}")
            layout.append((BIG.index(n), l * rows, rows))
    reduced = join_halves([bufs[n] for n in BIG], layout, name="join_halves")

    out_g, out_d, out_m, out_v = {}, {}, {}, {}
    for q, n in enumerate(BIG):
        shp = wts[n].shape
        as2d = lambda a: a.reshape(-1, shp[-1])
        g2d = w_in_window_cols(reduced[q], chip) if n == "w_in" else reduced[q]
        res = adamw(as2d(wts[n]), g2d, as2d(mom[n]), as2d(var[n]), name="adamw_" + n)
        out_g[n], out_d[n], out_m[n], out_v[n] = [r.reshape(shp) for r in res]
    sm_all = SMALL + MED
    flat2d = lambda a: a.reshape(-1, a.shape[-1])
    res = adamw_many([flat2d(wts[n]) for n in sm_all], [flat2d(gsum[n]) for n in sm_all],
                     [flat2d(mom[n]) for n in sm_all], [flat2d(var[n]) for n in sm_all], name="adamw_small")
    for tgt, vals in zip((out_d, out_m, out_v), res):
        for n, val in zip(sm_all, vals):
            tgt[n] = val.reshape(wts[n].shape)
    out_g.update({n: gsum[n] for n in sm_all})
    loss = summed["loss"][0].reshape(())
    return (loss, grad_x[None], *[out_g[n] for n in WEIGHTS], *[out_d[n] for n in WEIGHTS],
            *[out_m[n] for n in WEIGHTS], *[out_v[n] for n in WEIGHTS])
```

```python
import functools

import jax
import jax.numpy as jnp
import numpy as np
from jax import lax
from jax.experimental import pallas as pl
from jax.experimental.pallas import tpu as pltpu

f32, bf16 = jnp.float32, jnp.bfloat16
HI = lax.Precision.HIGHEST
MESH = pl.DeviceIdType.MESH

D = 1024
DEPTH = 2
MLA_H, NOPE, ROPE, DQK, DV = 8, 64, 32, 96, 64
QL, KVL = 384, 256
RW, RH, RN = 256, 4, 64
DL, AL, GL, MVL = 64, 64, 128, 32
CW = 256
DFF = 4096
GATE = 3 * D
MLA_COLS = QL + KVL + ROPE
MLA_PAD = 768
NORM_EPS = 1e-6
GN_EPS = 64e-5
ROPE_THETA = 10000.0
LR, B1, B2, EPS, WD, STEP = 0.001, 0.9, 0.999, 1e-08, 0.01, 10

VMEM_LIMIT = 52 * 1024 * 1024
WKV_CHUNK = 64
WKV_CHUNKS_PER_STEP = 4
ATTN_SEGMENTS = 4

BIG = ["w_in", "mla_wq_b", "mla_wkv_b", "mla_w_o", "rwkv_w_o", "conv_w_o", "w_out", "w_up", "w_down"]
MED = ["rwkv_w2", "rwkv_a2", "rwkv_g2", "rwkv_v1", "rwkv_v2", "conv_w"]
ROW_SHARDED = {"w_out", "w_down", "rwkv_v1"}
SMALL = ["attn_norm", "mla_q_a_norm", "mla_kv_a_norm", "mla_q_norm", "mla_k_norm", "rwkv_mu", "rwkv_w0",
         "rwkv_a0", "rwkv_k_k", "rwkv_k_a", "rwkv_r_k", "rwkv_ln_w", "rwkv_ln_b", "rwkv_v_mu", "rwkv_v0",
         "mlp_norm"]
WEIGHTS = ["attn_norm", "w_in", "mla_q_a_norm", "mla_wq_b", "mla_kv_a_norm", "mla_wkv_b", "mla_q_norm",
           "mla_k_norm", "mla_w_o", "rwkv_mu", "rwkv_w0", "rwkv_w2", "rwkv_a0", "rwkv_a2", "rwkv_g2",
           "rwkv_k_k", "rwkv_k_a", "rwkv_r_k", "rwkv_ln_w", "rwkv_ln_b", "rwkv_w_o", "rwkv_v1", "rwkv_v_mu",
           "rwkv_v0", "rwkv_v2", "conv_w", "conv_w_o", "w_out", "mlp_norm", "w_up", "w_down"]


def _cparams(sem=None):
    return pltpu.CompilerParams(dimension_semantics=sem, vmem_limit_bytes=VMEM_LIMIT)


def _pick(dim, pref, align):
    if dim <= pref:
        return dim
    t = (pref // align) * align
    while t >= align:
        if dim % t == 0:
            return t
        t -= align
    return dim


def _bdot(a, b, dims):
    return lax.dot_general(a.astype(bf16), b.astype(bf16), (dims, ((), ())), preferred_element_type=f32)


@jax.custom_vjp
def _mm(a, b):
    return _bdot(a, b, ((1,), (0,)))


def _mm_fwd(a, b):
    return _mm(a, b), (a, b)


def _mm_bwd(res, g):
    a, b = res
    return _bdot(g, b, ((1,), (1,))), _bdot(a, g, ((0,), (0,)))


_mm.defvjp(_mm_fwd, _mm_bwd)


@jax.custom_vjp
def _mm_nt(a, b):
    return _bdot(a, b, ((1,), (1,)))


def _mm_nt_fwd(a, b):
    return _mm_nt(a, b), (a, b)


def _mm_nt_bwd(res, g):
    a, b = res
    return _bdot(g, b, ((1,), (0,))), _bdot(g, a, ((0,), (0,)))


_mm_nt.defvjp(_mm_nt_fwd, _mm_nt_bwd)


_NN, _NT, _TN = ((1,), (0,)), ((1,), (1,)), ((0,), (0,))


def _dg(a, b, dims):
    return lax.dot_general(a, b, (dims, ((), ())), preferred_element_type=f32)


def _bf16_pieces(x, count):
    out, rest = [], x
    for q in range(count):
        piece = rest.astype(bf16)
        out.append(piece)
        if q + 1 < count:
            rest = rest - piece.astype(f32)
    return out


def _dot3(a, b, dims):
    (ah, al), (bh, bl) = _bf16_pieces(a, 2), _bf16_pieces(b, 2)
    return _dg(ah, bh, dims) + (_dg(ah, bl, dims) + _dg(al, bh, dims))


@jax.custom_vjp
def _hdot(a, b):
    return _dot3(a, b, _NN)


@jax.custom_vjp
def _hdot_nt(a, b):
    return _dot3(a, b, _NT)


@jax.custom_vjp
def _hdot_tn(a, b):
    return _dot3(a, b, _TN)


_hdot.defvjp(lambda a, b: (_hdot(a, b), (a, b)), lambda res, g: (_hdot_nt(g, res[1]), _hdot_tn(res[0], g)))
_hdot_nt.defvjp(lambda a, b: (_hdot_nt(a, b), (a, b)), lambda res, g: (_hdot(g, res[1]), _hdot_tn(g, res[0])))
_hdot_tn.defvjp(lambda a, b: (_hdot_tn(a, b), (a, b)), lambda res, g: (_hdot_nt(res[1], g), _hdot(res[0], g)))


_BNN, _BNT, _BTN = ((2,), (1,)), ((2,), (2,)), ((1,), (1,))


def _bdg(a, b, dims):
    return lax.dot_general(a, b, (dims, ((0,), (0,))), preferred_element_type=f32)


def _bdot3(a, b, dims):
    (ah, al), (bh, bl) = _bf16_pieces(a, 2), _bf16_pieces(b, 2)
    return _bdg(ah, bh, dims) + (_bdg(ah, bl, dims) + _bdg(al, bh, dims))


@jax.custom_vjp
def _hbnn(a, b):
    return _bdot3(a, b, _BNN)


@jax.custom_vjp
def _hbnt(a, b):
    return _bdot3(a, b, _BNT)


@jax.custom_vjp
def _hbtn(a, b):
    return _bdot3(a, b, _BTN)


_hbnn.defvjp(lambda a, b: (_hbnn(a, b), (a, b)), lambda res, g: (_hbnt(g, res[1]), _hbtn(res[0], g)))
_hbnt.defvjp(lambda a, b: (_hbnt(a, b), (a, b)), lambda res, g: (_hbnn(g, res[1]), _hbtn(g, res[0])))
_hbtn.defvjp(lambda a, b: (_hbtn(a, b), (a, b)), lambda res, g: (_hbnt(res[1], g), _hbnn(res[0], g)))


@functools.partial(jax.custom_vjp, nondiff_argnums=(2,))
def _exact_bl(m, x, transposed):
    mb = m.astype(bf16)
    hi, mid, lo = _bf16_pieces(x, 3)
    dims = _BTN if transposed else _BNN
    return (_bdg(mb, hi, dims) + _bdg(mb, mid, dims)) + _bdg(mb, lo, dims)


_exact_bl.defvjp(lambda m, x, transposed: (_exact_bl(m, x, transposed), m),
                 lambda transposed, m, g: (jnp.zeros_like(m), _exact_bl(m, g, not transposed)))


@functools.partial(jax.custom_vjp, nondiff_argnums=(2,))
def _exact_l(m, x, transposed):
    mb = m.astype(bf16)
    hi, mid, lo = _bf16_pieces(x, 3)
    dims = _TN if transposed else _NN
    return (_dg(mb, hi, dims) + _dg(mb, mid, dims)) + _dg(mb, lo, dims)


_exact_l.defvjp(lambda m, x, transposed: (_exact_l(m, x, transposed), m),
                lambda transposed, m, g: (jnp.zeros_like(m), _exact_l(m, g, not transposed)))


@functools.partial(jax.custom_vjp, nondiff_argnums=(2,))
def _exact_r(x, m, transposed):
    mb = m.astype(bf16)
    hi, mid, lo = _bf16_pieces(x, 3)
    dims = _NT if transposed else _NN
    return (_dg(hi, mb, dims) + _dg(mid, mb, dims)) + _dg(lo, mb, dims)


_exact_r.defvjp(lambda x, m, transposed: (_exact_r(x, m, transposed), m),
                lambda transposed, m, g: (_exact_r(g, m, not transposed), jnp.zeros_like(m)))


def _rms(x, g, eps=NORM_EPS):
    return x * lax.rsqrt(jnp.mean(x * x, axis=-1, keepdims=True) + eps) * g


def _sigmoid(x):
    return 1.0 / (1.0 + jnp.exp(-x))


def _softplus(x):
    return jnp.maximum(x, 0.0) + jnp.log(1.0 + jnp.exp(-jnp.maximum(x, -x)))


def _lane_split(x, sizes):
    bounds = np.cumsum([0] + list(sizes))

    @jax.custom_vjp
    def split(v):
        return tuple(v[..., int(bounds[q]):int(bounds[q + 1])] for q in range(len(sizes)))

    split.defvjp(lambda v: (split(v), None), lambda _, g: (jnp.concatenate(g, axis=-1),))
    return split(x)


def _row_split(x, sizes):
    bounds = np.cumsum([0] + list(sizes))

    @jax.custom_vjp
    def split(v):
        return tuple(v[..., int(bounds[q]):int(bounds[q + 1]), :] for q in range(len(sizes)))

    split.defvjp(lambda v: (split(v), None), lambda _, g: (jnp.concatenate(g, axis=-2),))
    return split(x)


def _shift_mats(t, k):
    r = lax.broadcasted_iota(jnp.int32, (t, t), 0)
    c = lax.broadcasted_iota(jnp.int32, (t, t), 1)
    inner = (r - c == k).astype(f32)
    r8 = lax.broadcasted_iota(jnp.int32, (t, 8), 0)
    c8 = lax.broadcasted_iota(jnp.int32, (t, 8), 1)
    edge = (c8 - r8 == 8 - k).astype(f32)
    return inner, edge


def _shift(x, halo, k):
    inner, edge = _shift_mats(x.shape[0], k)
    return _exact_l(inner, x, False) + jnp.dot(edge, halo, precision=HI, preferred_element_type=f32)


def mm(a, b, *, name, ta=False, tb=False, a_batched=False, b_batched=False, reduce_batch=False, add=None,
       act_grad=None, relu2_out=False, n_split=1, out_dtype=f32, tm=1024, tn=1024, tk=2048):
    ash, bsh = a.shape[-2:], b.shape[-2:]
    (k_, m_) = ash if ta else ash[::-1]
    (k2_, n_) = bsh[::-1] if tb else bsh
    assert k_ == k2_, (a.shape, b.shape, ta, tb)
    hb = a.shape[0] if a_batched else (b.shape[0] if b_batched else 1)
    batched_out = (a_batched or b_batched) and not reduce_batch
    h_out = hb if batched_out else 1
    h_red = hb if reduce_batch else 1
    tm = _pick(m_, tm, 128)
    tn = _pick(n_ // n_split, tn, 128)
    tk = _pick(k_, tk, 128)
    nm, nn, nk = m_ // tm, n_ // tn, k_ // tk

    def a_map(i, j, ho, hr, kk):
        blk = (kk, i) if ta else (i, kk)
        return ((ho if batched_out else hr),) + blk if a_batched else blk

    def b_map(i, j, ho, hr, kk):
        blk = (j, kk) if tb else (kk, j)
        return ((ho if batched_out else hr),) + blk if b_batched else blk

    a_blk = (tk, tm) if ta else (tm, tk)
    b_blk = (tn, tk) if tb else (tk, tn)
    in_specs = [pl.BlockSpec(((1,) + a_blk) if a_batched else a_blk, a_map),
                pl.BlockSpec(((1,) + b_blk) if b_batched else b_blk, b_map)]
    args = [a, b]
    for extra in (add, act_grad):
        if extra is not None:
            in_specs.append(pl.BlockSpec((tm, tn), lambda i, j, ho, hr, kk: (i, j)))
            args.append(extra)
    if n_split > 1:
        per = n_ // n_split // tn
        if batched_out:
            out_spec = pl.BlockSpec((1, 1, tm, tn), lambda i, j, ho, hr, kk: (j // per, ho, i, j % per))
            out_shape = jax.ShapeDtypeStruct((n_split, hb, m_, n_ // n_split), out_dtype)
        else:
            out_spec = pl.BlockSpec((1, tm, tn), lambda i, j, ho, hr, kk: (j // per, i, j % per))
            out_shape = jax.ShapeDtypeStruct((n_split, m_, n_ // n_split), out_dtype)
    elif batched_out:
        out_spec = pl.BlockSpec((1, tm, tn), lambda i, j, ho, hr, kk: (ho, i, j))
        out_shape = jax.ShapeDtypeStruct((hb, m_, n_), out_dtype)
    else:
        out_spec = pl.BlockSpec((tm, tn), lambda i, j, ho, hr, kk: (i, j))
        out_shape = jax.ShapeDtypeStruct((m_, n_), out_dtype)
    lead = (0,) * (int(batched_out) + int(n_split > 1))
    dims = ((0,) if ta else (1,), (1,) if tb else (0,))
    has_add, has_act = add is not None, act_grad is not None

    def body(*refs):
        a_ref, b_ref = refs[0], refs[1]
        pos = 2
        add_ref = act_ref = None
        if has_add:
            add_ref = refs[pos]
            pos += 1
        if has_act:
            act_ref = refs[pos]
            pos += 1
        o_ref, acc_ref = refs[pos], refs[-1]
        hr, kk = pl.program_id(3), pl.program_id(4)
        first = jnp.logical_and(hr == 0, kk == 0)
        last = jnp.logical_and(hr == h_red - 1, kk == nk - 1)
        av = a_ref[0] if a_batched else a_ref[...]
        bv = b_ref[0] if b_batched else b_ref[...]
        p = _bdot(av, bv, dims)
        single = h_red * nk == 1

        if not single:
            @pl.when(first)
            def _():
                acc_ref[...] = p

            @pl.when(jnp.logical_not(first))
            def _():
                acc_ref[...] += p

        @pl.when(last)
        def _():
            r = p if single else acc_ref[...]
            if has_act:
                r = r * (2.0 * jnp.maximum(act_ref[...], 0.0))
            if has_add:
                r = r + add_ref[...]
            if lead:
                o_ref[lead] = r.astype(out_dtype)
            else:
                o_ref[...] = r.astype(out_dtype)
            if relu2_out:
                refs[pos + 1][...] = jnp.square(jnp.maximum(r, 0.0)).astype(bf16)

    if relu2_out:
        assert not lead
        out_spec = [out_spec, out_spec]
        out_shape = [out_shape, jax.ShapeDtypeStruct(out_shape.shape, bf16)]
    return pl.pallas_call(
        body, name=name, grid=(nm, nn, h_out, h_red, nk), in_specs=in_specs, out_specs=out_spec,
        out_shape=out_shape, scratch_shapes=[pltpu.VMEM((tm, tn), f32)],
        compiler_params=_cparams(("parallel", "parallel", "parallel", "arbitrary", "arbitrary")),
    )(*args)


def _row_spec(arr, tile, idx):
    if arr.ndim == 2:
        return pl.BlockSpec((tile, arr.shape[1]), lambda i: (idx(i), 0))
    return pl.BlockSpec((arr.shape[0], tile, arr.shape[2]), lambda i: (0, idx(i), 0))


def _halo_spec(arr, tile, idx):
    per = tile // 8
    return pl.BlockSpec((8, arr.shape[1]), lambda i: (jnp.maximum(idx(i) * per - 1, 0), 0))


def _full_spec(arr):
    nd = arr.ndim
    return pl.BlockSpec(arr.shape, lambda i: (0,) * nd)


def _load_f32(ref):
    val = ref[...]
    return val.astype(f32) if val.dtype == bf16 else val


def rows_fwd(fn, rows, params, consts, out_shapes, *, tile, name, halos=()):
    s_len = rows[0].shape[-2]
    n = s_len // tile
    nr, nh, npar, nc = len(rows), len(halos), len(params), len(consts)
    ident = lambda i: i
    in_specs = ([_row_spec(r, tile, ident) for r in rows] + [_halo_spec(rows[h], tile, ident) for h in halos]
                + [_full_spec(p) for p in params] + [_full_spec(c) for c in consts])
    out_specs = [_row_spec(o, tile, ident) for o in out_shapes]

    def body(*refs):
        i = pl.program_id(0)
        rv = [_load_f32(r) for r in refs[:nr]]
        keep = (i > 0).astype(f32)
        hv = [r[...] * keep for r in refs[nr:nr + nh]]
        pv = [r[...] for r in refs[nr + nh:nr + nh + npar]]
        cv = [r[...] for r in refs[nr + nh + npar:nr + nh + npar + nc]]
        outs = fn(rv, hv, pv, cv)
        for o_ref, o in zip(refs[nr + nh + npar + nc:], outs):
            o_ref[...] = o.astype(o_ref.dtype)

    return pl.pallas_call(
        body, name=name, grid=(n,), in_specs=in_specs, out_specs=out_specs, out_shape=list(out_shapes),
        compiler_params=_cparams(("arbitrary",)),
    )(*rows, *[rows[h] for h in halos], *params, *consts)


def rows_bwd(fn, rows, params, consts, douts, *, tile, name, halos=(), grad_rows=None, extra=None,
             grad_dtypes=None):
    s_len = rows[0].shape[-2]
    n = s_len // tile
    nr, nh, npar, nc = len(rows), len(halos), len(params), len(consts)
    grad_rows = list(range(nr)) if grad_rows is None else list(grad_rows)
    extra = extra or {}
    assert all(h in grad_rows for h in halos)
    rev = lambda i: n - 1 - i
    dflat = [d for ds in douts for d in ds]
    dcount = [len(ds) for ds in douts]
    eflat = [e for g in grad_rows for e in extra.get(g, [])]
    ecount = [len(extra.get(g, [])) for g in grad_rows]
    in_specs = ([_row_spec(r, tile, rev) for r in rows] + [_halo_spec(rows[h], tile, rev) for h in halos]
                + [_full_spec(p) for p in params] + [_full_spec(c) for c in consts]
                + [_row_spec(d, tile, rev) for d in dflat] + [_row_spec(e, tile, rev) for e in eflat])
    grad_dtypes = [f32] * len(grad_rows) if grad_dtypes is None else list(grad_dtypes)
    assert all(grad_dtypes[q] == f32 for q, g in enumerate(grad_rows) if g in halos)
    out_shapes = ([jax.ShapeDtypeStruct(rows[g].shape, dt) for g, dt in zip(grad_rows, grad_dtypes)]
                  + [jax.ShapeDtypeStruct(p.shape, f32) for p in params])
    out_specs = [_row_spec(rows[g], tile, rev) for g in grad_rows] + [_full_spec(p) for p in params]
    scratch = [pltpu.VMEM((8, rows[h].shape[1]), f32) for h in halos]
    n_in = nr + nh + npar + nc + len(dflat) + len(eflat)
    n_out = len(grad_rows) + npar

    def body(*refs):
        i = pl.program_id(0)
        rv = [_load_f32(r) for r in refs[:nr]]
        keep = (i < n - 1).astype(f32)
        hv = [r[...] * keep for r in refs[nr:nr + nh]]
        pv = [r[...] for r in refs[nr + nh:nr + nh + npar]]
        pos = nr + nh + npar
        cv = [r[...] for r in refs[pos:pos + nc]]
        pos += nc
        dv = []
        for cnt in dcount:
            acc = _load_f32(refs[pos])
            for q in range(1, cnt):
                acc = acc + _load_f32(refs[pos + q])
            dv.append(acc)
            pos += cnt
        ev = []
        for cnt in ecount:
            ev.append([_load_f32(refs[pos + q]) for q in range(cnt)])
            pos += cnt
        out_refs = refs[n_in:n_in + n_out]
        carry_refs = refs[n_in + n_out:]

        def f(gr, gh, gp):
            full = list(rv)
            for g, val in zip(grad_rows, gr):
                full[g] = val
            return tuple(fn(full, gh, gp, cv))

        _, vjp = jax.vjp(f, [rv[g] for g in grad_rows], hv, pv)
        d_rows, d_halos, d_params = vjp(tuple(dv))

        @pl.when(i == 0)
        def _():
            for c_ref in carry_refs:
                c_ref[...] = jnp.zeros_like(c_ref)
            for p_ref in out_refs[len(grad_rows):]:
                p_ref[...] = jnp.zeros_like(p_ref)

        for q, g in enumerate(grad_rows):
            val = d_rows[q]
            for e in ev[q]:
                val = val + e
            out_refs[q][...] = val.astype(out_refs[q].dtype)
            if g in halos:
                hq = list(halos).index(g)
                out_refs[q][tile - 8:tile, :] += carry_refs[hq][...]
                carry_refs[hq][...] = d_halos[hq]
        for p_ref, dp in zip(out_refs[len(grad_rows):], d_params):
            p_ref[...] += dp

    res = pl.pallas_call(
        body, name=name, grid=(n,), in_specs=in_specs, out_specs=out_specs, out_shape=out_shapes,
        scratch_shapes=scratch, compiler_params=_cparams(("arbitrary",)),
    )(*rows, *[rows[h] for h in halos], *params, *consts, *dflat, *eflat)
    return list(res[:len(grad_rows)]), list(res[len(grad_rows):])


def _fn_norm(rows, halos, params, consts):
    return (_rms(rows[0], params[0]),)


def _fn_mla_prep(rows, halos, params, consts):
    cq, ckv, kpe = _lane_split(rows[0], (QL, KVL, MLA_PAD - QL - KVL))
    return _rms(cq, params[0]), _rms(ckv, params[1]), kpe


def _rope(x, cos, sin, rot):
    return x * cos + _exact_r(x, rot, False) * sin


def _fn_qk_post(rows, halos, params, consts):
    q_flat, kv_flat, kpe, cos, sin = rows
    q_norm, k_norm = params
    (rot,) = consts
    q_heads = _lane_split(q_flat, (DQK,) * MLA_H)
    kv_heads = _lane_split(kv_flat, (NOPE, DV) * MLA_H)
    k_pe, _ = _lane_split(kpe, (ROPE, kpe.shape[1] - ROPE))
    qs = [_rope(_rms(qh, q_norm), cos, sin, rot) for qh in q_heads]
    ks = [_rope(_rms(jnp.concatenate([kv_heads[2 * h], k_pe], axis=-1), k_norm), cos, sin, rot)
          for h in range(MLA_H)]
    vs = [kv_heads[2 * h + 1] for h in range(MLA_H)]
    return jnp.stack(qs, axis=0), jnp.stack(ks, axis=0), jnp.stack(vs, axis=0)


def _seg(x, bd):
    return _exact_r(x, bd, False)


def _make_fn_rwkv_prep(vres):
    def fn(rows, halos, params, consts):
        cols = rows[0]
        bd = consts[0]
        mu, w0, w2, a0, a2, g2, k_k, k_a = params[:8]
        prev = _shift(cols, halos[0], 1)
        c = cols + (prev - cols) * mu
        r, k, v, xw, xa, xg = _lane_split(c, (RW, RW, RW, DL, AL, GL))
        log_w = -_softplus(-(w0 + _mm(jnp.tanh(xw), w2))) - 0.5
        ld = -jnp.exp(log_w)
        a = _sigmoid(a0 + _mm(xa, a2))
        g = _mm(_sigmoid(xg), g2)
        if vres:
            hcur, v_first = rows[1], rows[2]
            v1, v_mu, v0, v2 = params[8:12]
            xv = _mm(hcur, v1)
            xv_prev = _shift(xv, _mm(halos[1], v1), 1)
            xv = xv + (xv_prev - xv) * v_mu
            v = v + (v_first - v) * _sigmoid(v0 + _mm(xv, v2))
        kk = k * k_k
        kk = kk / jnp.maximum(jnp.sqrt(_seg(kk * kk, bd)), 1e-12)
        k2 = k * (1.0 + (a - 1.0) * k_a)
        return r, ld, k2, v, -kk, kk * a, g
    return fn


def _fn_rwkv_post(rows, halos, params, consts):
    y, r, k2, v, g = rows
    ln_w, ln_b, r_k = params
    bd = consts[0]
    mean = _seg(y, bd) * (1.0 / RN)
    d = y - mean
    var = _seg(d * d, bd) * (1.0 / RN)
    yn = d * lax.rsqrt(var + GN_EPS) * ln_w + ln_b
    bonus = _seg(r * k2 * r_k, bd) * v
    return ((yn + bonus) * g,)


def _fn_conv(rows, halos, params, consts):
    cols, halo = rows[0], halos[0]
    w0, w1, w2 = params
    b, c, x = _lane_split(cols, (CW, CW, CW))
    _, ch, xh = _lane_split(halo, (CW, CW, CW))
    u, uh = c * x, ch * xh
    return (b * (w0 * _shift(u, uh, 2) + w1 * _shift(u, uh, 1) + w2 * u),)


def _fn_merge(rows, halos, params, consts):
    gate, o_a, o_b, o_c = rows
    g_a, g_b, g_c = _lane_split(gate, (D, D, D))
    return (_sigmoid(g_a) * o_a + _sigmoid(g_b) * o_b + _sigmoid(g_c) * o_c,)


def _attn_block(q, k, v, q0, diagonal_last):
    tq, kend = q.shape[0], k.shape[0]
    s = _mm_nt(q, k) * (DQK ** -0.5)
    if diagonal_last:
        tri = lax.broadcasted_iota(jnp.int32, (tq, tq), 0) >= lax.broadcasted_iota(jnp.int32, (tq, tq), 1)
        if kend > tq:
            before, diag = _lane_split(s, (kend - tq, tq))
            s = jnp.concatenate([before, jnp.where(tri, diag, -1e30)], axis=-1)
        else:
            s = jnp.where(tri, s, -1e30)
    else:
        row = q0 + lax.broadcasted_iota(jnp.int32, (tq, kend), 0)
        col = lax.broadcasted_iota(jnp.int32, (tq, kend), 1)
        s = jnp.where(row >= col, s, -1e30)
    m = lax.stop_gradient(jnp.max(s, axis=-1, keepdims=True))
    e = jnp.exp(s - m)
    p = e / jnp.sum(e, axis=-1, keepdims=True)
    return _mm(p, v)


def _attn_segments(s_len, tq):
    per = max(1, s_len // tq // ATTN_SEGMENTS)
    return [(first, per, (first + per) * tq) for first in range(0, s_len // tq, per)]


HEAD_PAIR = 2


def attn_fwd(q, k, v, *, tq, name):
    h, s_len, _ = q.shape
    outs = []
    for seg, (first, nq, kend) in enumerate(_attn_segments(s_len, tq)):
        def body(q_ref, k_ref, v_ref, o_ref, first=first, nq=nq):
            q0 = (first + pl.program_id(1)) * tq
            o = [_attn_block(q_ref[j], k_ref[j], v_ref[j], q0, nq == 1) for j in range(HEAD_PAIR)]
            o_ref[...] = jnp.concatenate(o, axis=-1).astype(o_ref.dtype)

        outs.append(pl.pallas_call(
            body, name=f"{name}_{seg}", grid=(h // HEAD_PAIR, nq),
            in_specs=[pl.BlockSpec((HEAD_PAIR, tq, DQK), lambda hp, i, first=first: (hp, first + i, 0)),
                      pl.BlockSpec((HEAD_PAIR, kend, DQK), lambda hp, i: (hp, 0, 0)),
                      pl.BlockSpec((HEAD_PAIR, kend, DV), lambda hp, i: (hp, 0, 0))],
            out_specs=pl.BlockSpec((tq, HEAD_PAIR * DV), lambda hp, i: (i, hp)),
            out_shape=jax.ShapeDtypeStruct((nq * tq, h * DV), bf16),
            compiler_params=_cparams(("parallel", "arbitrary")),
        )(q, k, v))
    return jnp.concatenate(outs, axis=0)


def attn_bwd(q, k, v, do, *, tq, name):
    h, s_len, _ = q.shape
    dqs, dk_acc, dv_acc = [], None, None
    for seg, (first, nq, kend) in reversed(list(enumerate(_attn_segments(s_len, tq)))):
        carried = dk_acc is not None

        def body(*refs, first=first, carried=carried, nq=nq):
            q_ref, k_ref, v_ref, do_ref = refs[:4]
            dq_ref, dk_ref, dv_ref = refs[-3:]
            i = pl.program_id(1)
            do_heads = _lane_split(do_ref[...], (DV,) * HEAD_PAIR)
            for j in range(HEAD_PAIR):
                _, vjp = jax.vjp(functools.partial(_attn_block, q0=(first + i) * tq, diagonal_last=nq == 1),
                                 q_ref[j], k_ref[j], v_ref[j])
                dq, dk, dv = vjp(do_heads[j])
                dq_ref[j] = dq

                @pl.when(i == 0)
                def _():
                    dk_ref[j] = dk + refs[4][j] if carried else dk
                    dv_ref[j] = dv + refs[5][j] if carried else dv

                @pl.when(i > 0)
                def _():
                    dk_ref[j] += dk
                    dv_ref[j] += dv

        key_specs = [pl.BlockSpec((HEAD_PAIR, kend, DQK), lambda hp, i: (hp, 0, 0)),
                     pl.BlockSpec((HEAD_PAIR, kend, DV), lambda hp, i: (hp, 0, 0))]
        dq, dk_acc, dv_acc = pl.pallas_call(
            body, name=f"{name}_{seg}", grid=(h // HEAD_PAIR, nq),
            in_specs=[pl.BlockSpec((HEAD_PAIR, tq, DQK), lambda hp, i, first=first: (hp, first + i, 0))] + key_specs
            + [pl.BlockSpec((tq, HEAD_PAIR * DV), lambda hp, i, first=first: (first + i, hp))]
            + (key_specs if carried else []),
            out_specs=[pl.BlockSpec((HEAD_PAIR, tq, DQK), lambda hp, i: (hp, i, 0))] + key_specs,
            out_shape=[jax.ShapeDtypeStruct((h, nq * tq, DQK), f32), jax.ShapeDtypeStruct((h, s_len, DQK), f32),
                       jax.ShapeDtypeStruct((h, s_len, DV), f32)],
            input_output_aliases={4: 1, 5: 2} if carried else {},
            compiler_params=_cparams(("parallel", "arbitrary")),
        )(q, k, v, do, *([dk_acc, dv_acc] if carried else []))
        dqs.append(dq)
    return jnp.concatenate(dqs[::-1], axis=1), dk_acc, dv_acc


def _wkv_local(r, ld, k, v, a, b):
    nb, c, n = r.shape
    ri = lax.broadcasted_iota(jnp.int32, (c, c), 0)
    ci = lax.broadcasted_iota(jnp.int32, (c, c), 1)
    tri = jnp.broadcast_to((ri >= ci).astype(f32)[None], (nb, c, c))
    cum = _exact_bl(tri, ld, False)
    tot = jnp.sum(ld, axis=1, keepdims=True)
    w_incl, w_excl, w_inv, w_rest = jnp.exp(cum), jnp.exp(cum - ld), jnp.exp(-cum), jnp.exp(tot - cum)
    ab, rb, bb, kb = a * w_excl, r * w_incl, b * w_inv, k * w_inv
    bw, kw = b * w_rest, k * w_rest
    r2 = lax.broadcasted_iota(jnp.int32, (2 * c, 2 * c), 0)
    c2 = lax.broadcasted_iota(jnp.int32, (2 * c, 2 * c), 1)
    t_of, s_of = jnp.where(r2 >= c, r2 - c, r2), jnp.where(c2 >= c, c2 - c, c2)
    keep = jnp.logical_or(t_of > s_of, jnp.logical_and(r2 >= c, t_of == s_of))
    pair = jnp.where(keep[None], _hbnt(jnp.concatenate([ab, rb], axis=1), jnp.concatenate([bb, kb], axis=1)), 0.0)
    on_b, on_k = _lane_split(pair, (c, c))
    l_ab, m_rb = _row_split(on_b, (c, c))
    l_ak_v, m_rk_v = _row_split(_hbnn(on_k, v), (c, c))
    x = jnp.concatenate([ab, l_ak_v], axis=-1)
    lp, span = l_ab, 1
    while span < c:
        x = x + _hbnn(lp, x)
        span *= 2
        if span < c:
            lp = _hbnn(lp, lp)
    via_b_r, via_b_y = _lane_split(_hbnn(m_rb, x), (n, n))
    r_hat = rb + via_b_r
    y0 = via_b_y + m_rk_v
    from_b_g, from_b_z = _row_split(_hbtn(x, bw), (n, n))
    eye = lax.broadcasted_iota(jnp.int32, (n, n), 0) == lax.broadcasted_iota(jnp.int32, (n, n), 1)
    g = jnp.where(eye[None], jnp.exp(tot), 0.0) + from_b_g
    z = from_b_z + _hbtn(v, kw)
    return r_hat, y0, g, z


def _head(h):
    return slice(RN * h, RN * (h + 1))


def _load_chunk_heads(ref, c, per):
    return jnp.stack([ref[c * q:c * (q + 1), _head(h)] for q in range(per) for h in range(RH)], axis=0)


def _store_chunk_heads(ref, val, c, per):
    for q in range(per):
        ref[c * q:c * (q + 1), :] = jnp.concatenate([val[q * RH + h] for h in range(RH)], axis=-1)


def wkv_fwd(r, ld, k, v, a, b, *, name):
    s_len = r.shape[0]
    c = WKV_CHUNK
    n = s_len // c
    per = min(WKV_CHUNKS_PER_STEP, n)
    rows = pl.BlockSpec((c * per, RW), lambda i: (i, 0))
    mats = pl.BlockSpec((per, RH, RN, RN), lambda i: (i, 0, 0, 0))
    rows_t, mats_t = jax.ShapeDtypeStruct((s_len, RW), f32), jax.ShapeDtypeStruct((n, RH, RN, RN), f32)

    def local_body(r_ref, ld_ref, k_ref, v_ref, a_ref, b_ref, rh_ref, y0_ref, g_ref, z_ref):
        r_hat, y0, g, z = _wkv_local(*[_load_chunk_heads(ref, c, per)
                                       for ref in (r_ref, ld_ref, k_ref, v_ref, a_ref, b_ref)])
        _store_chunk_heads(rh_ref, r_hat, c, per)
        _store_chunk_heads(y0_ref, y0, c, per)
        g_ref[...] = g.reshape(per, RH, RN, RN)
        z_ref[...] = z.reshape(per, RH, RN, RN)

    r_hat, y0, g, z = pl.pallas_call(
        local_body, name=name + "_local", grid=(n // per,), in_specs=[rows] * 6, out_specs=[rows, rows, mats, mats],
        out_shape=[rows_t, rows_t, mats_t, mats_t], compiler_params=_cparams(("parallel",)),
    )(r, ld, k, v, a, b)

    def scan_body(g_ref, z_ref, st_ref, s_sc):
        s_sc[...] = jnp.zeros_like(s_sc)

        @pl.loop(0, n)
        def _(i):
            s0 = s_sc[...]
            st_ref[i] = s0
            s_sc[...] = _hbnn(s0, g_ref[i]) + z_ref[i]

    vm = pl.BlockSpec(memory_space=pltpu.VMEM)
    states = pl.pallas_call(
        scan_body, name=name + "_scan", in_specs=[vm, vm], out_specs=vm, out_shape=mats_t,
        scratch_shapes=[pltpu.VMEM((RH, RN, RN), f32)],
        compiler_params=pltpu.CompilerParams(vmem_limit_bytes=VMEM_LIMIT),
    )(g, z)

    def out_body(rh_ref, y0_ref, st_ref, y_ref):
        y = _hbnt(_load_chunk_heads(rh_ref, c, per), st_ref[...].reshape(per * RH, RN, RN))
        _store_chunk_heads(y_ref, y, c, per)
        y_ref[...] += y0_ref[...]

    y = pl.pallas_call(
        out_body, name=name + "_out", grid=(n // per,), in_specs=[rows, rows, mats], out_specs=rows,
        out_shape=rows_t, compiler_params=_cparams(("parallel",)),
    )(r_hat, y0, states)
    return y, dict(r_hat=r_hat, g=g, states=states)


def wkv_bwd(r, ld, k, v, a, b, saved, dy, *, name):
    s_len = r.shape[0]
    c = WKV_CHUNK
    n = s_len // c
    per = min(WKV_CHUNKS_PER_STEP, n)
    rows = pl.BlockSpec((c * per, RW), lambda i: (i, 0))
    mats = pl.BlockSpec((per, RH, RN, RN), lambda i: (i, 0, 0, 0))
    rows_t, mats_t = jax.ShapeDtypeStruct((s_len, RW), f32), jax.ShapeDtypeStruct((n, RH, RN, RN), f32)

    def out_body(dy_ref, rh_ref, st_ref, drh_ref, dsy_ref):
        dyb = _load_chunk_heads(dy_ref, c, per)
        _store_chunk_heads(drh_ref, _hbnn(dyb, st_ref[...].reshape(per * RH, RN, RN)), c, per)
        dsy_ref[...] = _hbtn(dyb, _load_chunk_heads(rh_ref, c, per)).reshape(per, RH, RN, RN)

    d_rhat, ds_y = pl.pallas_call(
        out_body, name=name + "_out", grid=(n // per,), in_specs=[rows, rows, mats], out_specs=[rows, mats],
        out_shape=[rows_t, mats_t], compiler_params=_cparams(("parallel",)),
    )(dy, saved["r_hat"], saved["states"])

    def scan_body(dsy_ref, g_ref, st_ref, dg_ref, dz_ref, ds_sc):
        ds_sc[...] = jnp.zeros_like(ds_sc)

        @pl.loop(0, n)
        def _(i):
            cidx = n - 1 - i
            ds_next = ds_sc[...]
            dz_ref[cidx] = ds_next
            dg_ref[cidx] = _hbtn(st_ref[cidx], ds_next)
            ds_sc[...] = dsy_ref[cidx] + _hbnt(ds_next, g_ref[cidx])

    vm = pl.BlockSpec(memory_space=pltpu.VMEM)
    d_g, d_z = pl.pallas_call(
        scan_body, name=name + "_scan", in_specs=[vm, vm, vm], out_specs=[vm, vm], out_shape=[mats_t, mats_t],
        scratch_shapes=[pltpu.VMEM((RH, RN, RN), f32)],
        compiler_params=pltpu.CompilerParams(vmem_limit_bytes=VMEM_LIMIT),
    )(ds_y, saved["g"], saved["states"])

    def local_body(r_ref, ld_ref, k_ref, v_ref, a_ref, b_ref, drh_ref, dy_ref, dg_ref, dz_ref, *out_refs):
        _, vjp = jax.vjp(_wkv_local, *[_load_chunk_heads(ref, c, per)
                                       for ref in (r_ref, ld_ref, k_ref, v_ref, a_ref, b_ref)])
        grads = vjp((_load_chunk_heads(drh_ref, c, per), _load_chunk_heads(dy_ref, c, per),
                     dg_ref[...].reshape(per * RH, RN, RN), dz_ref[...].reshape(per * RH, RN, RN)))
        for o_ref, val in zip(out_refs, grads):
            _store_chunk_heads(o_ref, val, c, per)

    return pl.pallas_call(
        local_body, name=name + "_local", grid=(n // per,), in_specs=[rows] * 8 + [mats, mats], out_specs=[rows] * 6,
        out_shape=[rows_t] * 6, compiler_params=_cparams(("parallel",)),
    )(r, ld, k, v, a, b, d_rhat, dy, d_g, d_z)


def loss_head(y, target, *, tile, name):
    s_len, d = y.shape
    n = s_len // tile

    def body(y_ref, t_ref, dy_ref, l_ref):
        err = y_ref[...] - t_ref[...]
        dy_ref[...] = err * (1.0 / d)
        part = 0.5 * jnp.sum(jnp.mean(err * err, axis=-1, keepdims=True), axis=0, keepdims=True)

        @pl.when(pl.program_id(0) == 0)
        def _():
            l_ref[...] = jnp.zeros_like(l_ref)

        l_ref[...] += jnp.broadcast_to(part, l_ref.shape)

    bs = pl.BlockSpec((tile, d), lambda i: (i, 0))
    dy, l = pl.pallas_call(
        body, name=name, grid=(n,), in_specs=[bs, bs],
        out_specs=[bs, pl.BlockSpec((8, 128), lambda i: (0, 0))],
        out_shape=[jax.ShapeDtypeStruct((s_len, d), f32), jax.ShapeDtypeStruct((8, 128), f32)],
        compiler_params=_cparams(("arbitrary",)),
    )(y, target)
    return l[0, 0], dy


def _adamw_update(w, g, m, v):
    mn = B1 * m + (1.0 - B1) * g
    vn = B2 * v + (1.0 - B2) * (g * g)
    delta = -LR * ((mn / (1.0 - B1 ** STEP)) / (jnp.sqrt(vn / (1.0 - B2 ** STEP)) + EPS) + WD * w)
    return delta, mn, vn


def adamw(w, g, m, v, *, name):
    rows, cols = w.shape
    tile = _pick(rows, max(8, (2 * 1024 * 1024 // (4 * cols)) // 8 * 8), 8)

    def body(w_ref, g_ref, m_ref, v_ref, g_out, d_out, m_out, v_out):
        gv = g_ref[...]
        d_out[...], m_out[...], v_out[...] = _adamw_update(w_ref[...], gv, m_ref[...], v_ref[...])
        g_out[...] = gv

    bs = pl.BlockSpec((tile, cols), lambda i: (i, 0))
    return pl.pallas_call(
        body, name=name, grid=(rows // tile,), in_specs=[bs] * 4, out_specs=[bs] * 4,
        out_shape=[jax.ShapeDtypeStruct((rows, cols), f32)] * 4, compiler_params=_cparams(("parallel",)),
    )(w, g, m, v)


def adamw_many(ws, gs, ms, vs, *, name):
    n = len(ws)

    def body(*refs):
        ins, outs = refs[:4 * n], refs[4 * n:]
        for i in range(n):
            outs[i][...], outs[n + i][...], outs[2 * n + i][...] = _adamw_update(
                ins[i][...], ins[n + i][...], ins[2 * n + i][...], ins[3 * n + i][...])

    vm = pl.BlockSpec(memory_space=pltpu.VMEM)
    res = pl.pallas_call(
        body, name=name, in_specs=[vm] * (4 * n), out_specs=[vm] * (3 * n),
        out_shape=[jax.ShapeDtypeStruct(w.shape, f32) for w in ws] * 3,
        compiler_params=pltpu.CompilerParams(vmem_limit_bytes=VMEM_LIMIT),
    )(*ws, *gs, *ms, *vs)
    return list(res[:n]), list(res[n:2 * n]), list(res[2 * n:])


def _place():
    return lax.axis_index("x"), lax.axis_index("y"), lax.axis_index("c")


_ANY = pl.BlockSpec(memory_space=pl.ANY)


def _peer_chips(x, y):
    return [(1 - x, y), (x, 1 - y), (1 - x, 1 - y)]


def gather_weights(shards, *, name):
    nk = len(shards)

    def body(*refs):
        srcs, outs = refs[:nk], refs[nk:2 * nk]
        ici_send, ici_recv, d2d_send, d2d_recv = refs[2 * nk + 1:]
        x, y, c = _place()
        me = 2 * x + y
        peers = _peer_chips(x, y)
        pending = []
        for k in range(nk):
            half = srcs[k].shape[0] // 2
            mine = pl.ds(c * half, half)
            for p, (px, py) in enumerate(peers):
                cp = pltpu.make_async_remote_copy(
                    src_ref=srcs[k].at[mine], dst_ref=outs[k].at[me, mine], send_sem=ici_send.at[k, p],
                    recv_sem=ici_recv.at[k, p], device_id=(px, py, c), device_id_type=MESH)
                cp.start()
                pending.append(cp)
        for k in range(nk):
            half = srcs[k].shape[0] // 2
            mine = pl.ds(c * half, half)
            for p, (px, py) in enumerate(peers):
                landed = outs[k].at[2 * px + py, mine]
                pltpu.make_async_remote_copy(
                    src_ref=srcs[k].at[mine], dst_ref=landed, send_sem=ici_send.at[k, p], recv_sem=ici_recv.at[k, p],
                    device_id=(px, py, c), device_id_type=MESH).wait_recv()
                fwd = pltpu.make_async_remote_copy(
                    src_ref=landed, dst_ref=landed, send_sem=d2d_send.at[k, p], recv_sem=d2d_recv.at[k, p],
                    device_id=(x, y, 1 - c), device_id_type=MESH)
                fwd.start()
                pending.append(fwd)
        for k in range(nk):
            half = srcs[k].shape[0] // 2
            other = pl.ds((1 - c) * half, half)
            for p, (px, py) in enumerate(peers):
                theirs = outs[k].at[2 * px + py, other]
                pltpu.make_async_remote_copy(
                    src_ref=theirs, dst_ref=theirs, send_sem=d2d_send.at[k, p], recv_sem=d2d_recv.at[k, p],
                    device_id=(x, y, 1 - c), device_id_type=MESH).wait_recv()
        for cp in pending:
            cp.wait_send()
        refs[2 * nk][...] = jnp.zeros_like(refs[2 * nk])

    sem = lambda *shape: pltpu.SemaphoreType.DMA(shape)
    res = pl.pallas_call(
        body, name=name, in_specs=[_ANY] * nk, out_specs=[_ANY] * nk + [pl.BlockSpec(memory_space=pltpu.VMEM)],
        out_shape=[jax.ShapeDtypeStruct((4,) + s.shape, s.dtype) for s in shards]
        + [jax.ShapeDtypeStruct((8, 128), f32)],
        scratch_shapes=[sem(nk, 3), sem(nk, 3), sem(nk, 3), sem(nk, 3)],
    )(*shards)
    return list(res[:nk]), res[nk]


_HBM = pl.BlockSpec(memory_space=pltpu.HBM)
_SEM = pl.BlockSpec(memory_space=pltpu.SEMAPHORE)


def _ici_half_copies(srcs, lands, send_sems, recv_sems, incoming):
    x, y, c = _place()
    me = 2 * x + y
    out = []
    for k in range(len(srcs)):
        half = srcs[k].shape[0] // 2
        mine = pl.ds(c * half, half)
        for p, (px, py) in enumerate(_peer_chips(x, y)):
            out.append(pltpu.make_async_remote_copy(
                src_ref=srcs[k].at[mine], dst_ref=lands[k].at[(2 * px + py) if incoming else me, mine],
                send_sem=send_sems.at[3 * k + p], recv_sem=recv_sems.at[3 * k + p], device_id=(px, py, c),
                device_id_type=MESH))
    return out


def gather_start(shards, *, name):
    nk = len(shards)

    def body(*refs):
        srcs, lands = refs[:nk], refs[nk:2 * nk]
        send_sems, recv_sems = refs[2 * nk], refs[2 * nk + 1]
        token = refs[-1]
        for outgoing in _ici_half_copies(srcs, lands, send_sems, recv_sems, incoming=False):
            outgoing.start()
        token[...] = jnp.zeros_like(token)

    lands = [pltpu.with_memory_space_constraint(lax.empty((4,) + s.shape, s.dtype), pltpu.HBM) for s in shards]
    res = pl.pallas_call(
        body, name=name,
        out_shape=(pltpu.SemaphoreType.DMA((3 * nk,)), pltpu.SemaphoreType.DMA((3 * nk,)),
                   *[pltpu.HBM(s.shape, s.dtype) for s in shards], *[pltpu.HBM(z.shape, z.dtype) for z in lands],
                   jax.ShapeDtypeStruct((8, 128), f32)),
        in_specs=[_HBM] * (2 * nk), out_specs=(_SEM, _SEM, *[_HBM] * (2 * nk), pl.BlockSpec(memory_space=pltpu.VMEM)),
        input_output_aliases={k: 2 + k for k in range(2 * nk)},
        compiler_params=pltpu.CompilerParams(has_side_effects=pltpu.SideEffectType.DATAFLOW_SIDE_EFFECTING),
    )(*[pltpu.with_memory_space_constraint(s, pltpu.HBM) for s in shards], *lands)
    return res[0], res[1], list(res[2:2 + nk]), list(res[2 + nk:2 + 2 * nk]), res[-1]


def gather_wait(send_sems, recv_sems, shards, lands, after, *, name):
    nk = len(shards)

    def body(*refs):
        srcs, zones = refs[:nk], refs[nk:2 * nk]
        for outgoing in _ici_half_copies(srcs, zones, refs[2 * nk], refs[2 * nk + 1], incoming=False):
            outgoing.wait_send()
        for landed in _ici_half_copies(srcs, zones, refs[2 * nk], refs[2 * nk + 1], incoming=True):
            landed.wait_recv()

    res = pl.pallas_call(
        body, name=name,
        out_shape=(*[pltpu.HBM(s.shape, s.dtype) for s in shards], *[pltpu.HBM(z.shape, z.dtype) for z in lands]),
        in_specs=[_HBM] * (2 * nk) + [_SEM, _SEM, _ANY], out_specs=tuple([_HBM] * (2 * nk)),
        input_output_aliases={k: k for k in range(2 * nk)},
        compiler_params=pltpu.CompilerParams(has_side_effects=pltpu.SideEffectType.DATAFLOW_SIDE_EFFECTING),
    )(*shards, *lands, send_sems, recv_sems, after)
    return list(res[:nk]), list(res[nk:])


def gather_forward(lands, *, name):
    nk = len(lands)

    def body(*refs):
        zones = refs[nk:2 * nk]
        send_sems, recv_sems = refs[2 * nk:]
        x, y, c = _place()
        sends = []
        for k in range(nk):
            half = zones[k].shape[1] // 2
            for p, (px, py) in enumerate(_peer_chips(x, y)):
                landed = zones[k].at[2 * px + py, pl.ds(c * half, half)]
                cp = pltpu.make_async_remote_copy(
                    src_ref=landed, dst_ref=landed, send_sem=send_sems.at[k, p], recv_sem=recv_sems.at[k, p],
                    device_id=(x, y, 1 - c), device_id_type=MESH)
                cp.start()
                sends.append(cp)
        for k in range(nk):
            half = zones[k].shape[1] // 2
            for p, (px, py) in enumerate(_peer_chips(x, y)):
                theirs = zones[k].at[2 * px + py, pl.ds((1 - c) * half, half)]
                pltpu.make_async_remote_copy(
                    src_ref=theirs, dst_ref=theirs, send_sem=send_sems.at[k, p], recv_sem=recv_sems.at[k, p],
                    device_id=(x, y, 1 - c), device_id_type=MESH).wait_recv()
        for cp in sends:
            cp.wait_send()

    return pl.pallas_call(
        body, name=name, in_specs=[_ANY] * nk, out_specs=[_ANY] * nk,
        out_shape=[jax.ShapeDtypeStruct(z.shape, z.dtype) for z in lands],
        input_output_aliases={k: k for k in range(nk)},
        scratch_shapes=[pltpu.SemaphoreType.DMA((nk, 3)), pltpu.SemaphoreType.DMA((nk, 3))],
    )(*lands)


def grads_to_sibling(parts, *, name):
    nk = len(parts)

    def body(*refs):
        srcs, outs = refs[:nk], refs[nk:2 * nk]
        send_sems, recv_sems = refs[2 * nk:]
        x, y, c = _place()
        sends = []
        for k in range(nk):
            half = srcs[k].shape[1] // 2
            cp = pltpu.make_async_remote_copy(
                src_ref=srcs[k].at[:, pl.ds((1 - c) * half, half), :], dst_ref=outs[k], send_sem=send_sems.at[k],
                recv_sem=recv_sems.at[k], device_id=(x, y, 1 - c), device_id_type=MESH)
            cp.start()
            sends.append(cp)
        for cp in sends:
            cp.wait_recv()
        for cp in sends:
            cp.wait_send()

    return pl.pallas_call(
        body, name=name, in_specs=[_ANY] * nk, out_specs=[_ANY] * nk,
        out_shape=[jax.ShapeDtypeStruct((4, p.shape[1] // 2, p.shape[2]), p.dtype) for p in parts],
        scratch_shapes=[pltpu.SemaphoreType.DMA((nk,)), pltpu.SemaphoreType.DMA((nk,))],
    )(*parts)


def pair_sum(part, theirs, core, *, name):
    _, rows, cols = part.shape
    half = rows // 2
    tile = _pick(half, max(16, (1 << 20) // (4 * cols) // 16 * 16), 16)
    per = half // tile

    def body(c_ref, p_ref, t_ref, o_ref):
        o_ref[...] = (p_ref[...].astype(f32) + t_ref[...].astype(f32)).astype(bf16)

    grid_spec = pltpu.PrefetchScalarGridSpec(
        num_scalar_prefetch=1, grid=(4, per),
        in_specs=[pl.BlockSpec((1, tile, cols), lambda j, i, c_ref: (j, c_ref[0] * per + i, 0)),
                  pl.BlockSpec((1, tile, cols), lambda j, i, c_ref: (j, i, 0))],
        out_specs=pl.BlockSpec((1, tile, cols), lambda j, i, c_ref: (j, i, 0)))
    return pl.pallas_call(
        body, name=name, grid_spec=grid_spec, out_shape=jax.ShapeDtypeStruct((4, half, cols), bf16),
        compiler_params=_cparams(("parallel", "parallel")),
    )(core, part, theirs)


def _all_to_all_copies(srcs, lands, send_sems, recv_sems, incoming):
    x, y, c = _place()
    me = 2 * x + y
    out = []
    for k in range(len(srcs)):
        for p, (px, py) in enumerate(_peer_chips(x, y)):
            peer = 2 * px + py
            out.append(pltpu.make_async_remote_copy(
                src_ref=srcs[k].at[peer], dst_ref=lands[k].at[peer if incoming else me],
                send_sem=send_sems.at[3 * k + p], recv_sem=recv_sems.at[3 * k + p], device_id=(px, py, c),
                device_id_type=MESH))
    return out


def scatter_start(parts, *, name):
    nk = len(parts)

    def body(*refs):
        srcs, lands = refs[:nk], refs[nk:2 * nk]
        for outgoing in _all_to_all_copies(srcs, lands, refs[2 * nk], refs[2 * nk + 1], incoming=False):
            outgoing.start()
        refs[-1][...] = jnp.zeros_like(refs[-1])

    lands = [pltpu.with_memory_space_constraint(lax.empty(p.shape, p.dtype), pltpu.HBM) for p in parts]
    res = pl.pallas_call(
        body, name=name,
        out_shape=(pltpu.SemaphoreType.DMA((3 * nk,)), pltpu.SemaphoreType.DMA((3 * nk,)),
                   *[pltpu.HBM(p.shape, p.dtype) for p in parts], *[pltpu.HBM(p.shape, p.dtype) for p in parts],
                   jax.ShapeDtypeStruct((8, 128), f32)),
        in_specs=[_HBM] * (2 * nk), out_specs=(_SEM, _SEM, *[_HBM] * (2 * nk), pl.BlockSpec(memory_space=pltpu.VMEM)),
        input_output_aliases={k: 2 + k for k in range(2 * nk)},
        compiler_params=pltpu.CompilerParams(has_side_effects=pltpu.SideEffectType.DATAFLOW_SIDE_EFFECTING),
    )(*[pltpu.with_memory_space_constraint(p, pltpu.HBM) for p in parts], *lands)
    return res[0], res[1], list(res[2:2 + nk]), list(res[2 + nk:2 + 2 * nk]), res[-1]


def scatter_wait(send_sems, recv_sems, parts, lands, after, *, name):
    nk = len(parts)

    def body(*refs):
        srcs, zones = refs[:nk], refs[nk:2 * nk]
        for outgoing in _all_to_all_copies(srcs, zones, refs[2 * nk], refs[2 * nk + 1], incoming=False):
            outgoing.wait_send()
        for landed in _all_to_all_copies(srcs, zones, refs[2 * nk], refs[2 * nk + 1], incoming=True):
            landed.wait_recv()

    res = pl.pallas_call(
        body, name=name,
        out_shape=(*[pltpu.HBM(p.shape, p.dtype) for p in parts], *[pltpu.HBM(z.shape, z.dtype) for z in lands]),
        in_specs=[_HBM] * (2 * nk) + [_SEM, _SEM, _ANY], out_specs=tuple([_HBM] * (2 * nk)),
        input_output_aliases={k: k for k in range(2 * nk)},
        compiler_params=pltpu.CompilerParams(has_side_effects=pltpu.SideEffectType.DATAFLOW_SIDE_EFFECTING),
    )(*parts, *lands, send_sems, recv_sems, after)
    return list(res[:nk]), list(res[nk:])


def join_halves(bufs, layout, *, name):
    nk, nb = len(layout), len(bufs)

    def body(*refs):
        outs = refs[nb:2 * nb]
        send_sems, recv_sems = refs[2 * nb:]
        x, y, c = _place()
        pending = []
        for k, (o, off, rows) in enumerate(layout):
            half = rows // 2
            mine = outs[o].at[pl.ds(off + c * half, half), :]
            cp = pltpu.make_async_remote_copy(
                src_ref=mine, dst_ref=mine, send_sem=send_sems.at[k], recv_sem=recv_sems.at[k],
                device_id=(x, y, 1 - c), device_id_type=MESH)
            cp.start()
            pending.append(cp)
        for k, (o, off, rows) in enumerate(layout):
            half = rows // 2
            theirs = outs[o].at[pl.ds(off + (1 - c) * half, half), :]
            pltpu.make_async_remote_copy(
                src_ref=theirs, dst_ref=theirs, send_sem=send_sems.at[k], recv_sem=recv_sems.at[k],
                device_id=(x, y, 1 - c), device_id_type=MESH).wait_recv()
        for cp in pending:
            cp.wait_send()

    return pl.pallas_call(
        body, name=name, in_specs=[_ANY] * nb, out_specs=[_ANY] * nb,
        out_shape=[jax.ShapeDtypeStruct(b.shape, b.dtype) for b in bufs],
        input_output_aliases={o: o for o in range(nb)},
        scratch_shapes=[pltpu.SemaphoreType.DMA((nk,)), pltpu.SemaphoreType.DMA((nk,))],
    )(*bufs)


def place_slab(dest, src, index, *, name):
    rows, cols = src.shape
    tile = _pick(rows, max(16, (1 << 20) // (src.dtype.itemsize * cols) // 16 * 16), 16)

    def body(i_ref, s_ref, d_ref, o_ref):
        del i_ref, d_ref
        o_ref[0] = s_ref[...]

    grid_spec = pltpu.PrefetchScalarGridSpec(
        num_scalar_prefetch=1, grid=(rows // tile,),
        in_specs=[pl.BlockSpec((tile, cols), lambda i, idx: (i, 0)), _ANY],
        out_specs=pl.BlockSpec((1, tile, cols), lambda i, idx: (idx[0], i, 0)))
    return pl.pallas_call(
        body, name=name, grid_spec=grid_spec, out_shape=jax.ShapeDtypeStruct(dest.shape, dest.dtype),
        input_output_aliases={2: 0}, compiler_params=_cparams(("parallel",)),
    )(index, src, dest)


def _broadcast_copies(src, land, send_sems, recv_sems, incoming):
    x, y, c = _place()
    me = 4 * x + 2 * y + c
    out = []
    for m in range(1, 8):
        px, py, pc = x ^ (m >> 2), y ^ ((m >> 1) & 1), c ^ (m & 1)
        out.append(pltpu.make_async_remote_copy(
            src_ref=src, dst_ref=land.at[(4 * px + 2 * py + pc) if incoming else me], send_sem=send_sems.at[m - 1],
            recv_sem=recv_sems.at[m - 1], device_id=(px, py, pc), device_id_type=MESH))
    return out


def broadcast_start(src, *, name):
    def body(s_ref, l_ref, send_sems, recv_sems, s_thru, l_thru, token):
        for outgoing in _broadcast_copies(s_ref, l_ref, send_sems, recv_sems, incoming=False):
            outgoing.start()
        token[...] = jnp.zeros_like(token)

    land = pltpu.with_memory_space_constraint(lax.empty((8,) + src.shape, src.dtype), pltpu.HBM)
    return pl.pallas_call(
        body, name=name,
        out_shape=(pltpu.SemaphoreType.DMA((7,)), pltpu.SemaphoreType.DMA((7,)), pltpu.HBM(src.shape, src.dtype),
                   pltpu.HBM(land.shape, land.dtype), jax.ShapeDtypeStruct((8, 128), f32)),
        in_specs=[_HBM, _HBM], out_specs=(_SEM, _SEM, _HBM, _HBM, pl.BlockSpec(memory_space=pltpu.VMEM)),
        input_output_aliases={0: 2, 1: 3},
        compiler_params=pltpu.CompilerParams(has_side_effects=pltpu.SideEffectType.DATAFLOW_SIDE_EFFECTING),
    )(pltpu.with_memory_space_constraint(src, pltpu.HBM), land)


def broadcast_wait(send_sems, recv_sems, src, land, after, *, name):
    def body(s_ref, l_ref, send_sems, recv_sems, after_ref, s_out, l_out):
        for outgoing in _broadcast_copies(s_ref, l_ref, send_sems, recv_sems, incoming=False):
            outgoing.wait_send()
        for landed in _broadcast_copies(s_ref, l_ref, send_sems, recv_sems, incoming=True):
            landed.wait_recv()

    return pl.pallas_call(
        body, name=name, out_shape=(pltpu.HBM(src.shape, src.dtype), pltpu.HBM(land.shape, land.dtype)),
        in_specs=[_HBM, _HBM, _SEM, _SEM, _ANY], out_specs=(_HBM, _HBM), input_output_aliases={0: 0, 1: 1},
        compiler_params=pltpu.CompilerParams(has_side_effects=pltpu.SideEffectType.DATAFLOW_SIDE_EFFECTING),
    )(src, land, send_sems, recv_sems, after)


def sum8(land, own, device, *, name):
    _, rows, cols = land.shape

    def body(d_ref, l_ref, o_ref, out_ref):
        mine = o_ref[...]
        acc = jnp.where(d_ref[0] == 0, mine, l_ref[0])
        for d in range(1, 8):
            acc = acc + jnp.where(d_ref[0] == d, mine, l_ref[d])
        out_ref[...] = acc

    grid_spec = pltpu.PrefetchScalarGridSpec(
        num_scalar_prefetch=1, grid=(1,),
        in_specs=[pl.BlockSpec((8, rows, cols), lambda i, d_ref: (0, 0, 0)),
                  pl.BlockSpec((rows, cols), lambda i, d_ref: (0, 0))],
        out_specs=pl.BlockSpec((rows, cols), lambda i, d_ref: (0, 0)))
    return pl.pallas_call(
        body, name=name, grid_spec=grid_spec, out_shape=jax.ShapeDtypeStruct((rows, cols), f32),
        compiler_params=_cparams(("arbitrary",)),
    )(device, land, own)


def sum4_into(arrived, own, dest, where, *, layer, total_rows, name):
    _, rows, cols = arrived.shape
    tile = _pick(rows, max(16, (1 << 20) // (4 * cols) // 16 * 16), 16)
    per = rows // tile

    def body(w_ref, a_ref, own_ref, *rest):
        mine = own_ref[0].astype(f32)
        p = [jnp.where(w_ref[1] == j, mine, a_ref[j].astype(f32)) for j in range(4)]
        rest[-1][...] = ((p[0] + p[1]) + p[2]) + p[3]

    grid_spec = pltpu.PrefetchScalarGridSpec(
        num_scalar_prefetch=1, grid=(per,),
        in_specs=[pl.BlockSpec((4, tile, cols), lambda i, w_ref: (0, i, 0)),
                  pl.BlockSpec((1, tile, cols), lambda i, w_ref: (w_ref[1], i, 0))] + ([] if dest is None else [_ANY]),
        out_specs=pl.BlockSpec((tile, cols), lambda i, w_ref: ((2 * layer + w_ref[0]) * per + i, 0)))
    return pl.pallas_call(
        body, name=name, grid_spec=grid_spec, out_shape=jax.ShapeDtypeStruct((total_rows, cols), f32),
        input_output_aliases={} if dest is None else {3: 0}, compiler_params=_cparams(("parallel",)),
    )(where, arrived, own, *([] if dest is None else [dest]))


def _consts():
    idx = np.arange(RW)
    bd = (idx[:, None] // RN == idx[None, :] // RN).astype(np.float32)
    rot = np.zeros((DQK, DQK), np.float32)
    half = ROPE // 2
    rot[NOPE + half + np.arange(half), NOPE + np.arange(half)] = -1.0
    rot[NOPE + np.arange(half), NOPE + half + np.arange(half)] = 1.0
    return jnp.asarray(bd), jnp.asarray(rot)


def _rope_tables(positions):
    freqs = ROPE_THETA ** (-(jnp.arange(ROPE // 2, dtype=f32) * 2.0 / ROPE))
    ang = positions.astype(f32)[:, None] * freqs
    cos, sin = jnp.cos(ang), jnp.sin(ang)
    ones = jnp.ones((positions.shape[0], NOPE), f32)
    return (jnp.concatenate([ones, cos, cos], axis=-1), jnp.concatenate([0.0 * ones, sin, sin], axis=-1))


STAGES = (("w_in",), ("mla_wq_b", "mla_wkv_b", "mla_w_o", "rwkv_w_o", "conv_w_o", "w_out"), ("w_up", "w_down"))


def derive_stage(stage, w):
    if stage == 0:
        w_in = w["w_in"]
        pad = jnp.zeros((D, MLA_PAD - MLA_COLS), w_in.dtype)
        return dict(gate=w_in[:, :GATE], mla=jnp.concatenate([w_in[:, GATE:GATE + MLA_COLS], pad], axis=1),
                    rw=w_in[:, GATE + MLA_COLS:GATE + MLA_COLS + 4 * RW], cv=w_in[:, GATE + MLA_COLS + 4 * RW:])
    if stage == 1:
        return dict(wq=w["mla_wq_b"], wkv=w["mla_wkv_b"], wo=w["mla_w_o"], rwo=w["rwkv_w_o"], cvo=w["conv_w_o"],
                    out=w["w_out"])
    return dict(up=w["w_up"], down=w["w_down"])


W_IN_WINDOW_TILE = (0, 10, 21, 31)
W_IN_WINDOW = 1664
W_IN_SHARD = 1384


def w_in_window_cols(win, chip):
    gap = MLA_PAD - MLA_COLS
    branches = []
    for j in range(4):
        lo, hi = W_IN_SHARD * j, W_IN_SHARD * (j + 1)
        base = 128 * W_IN_WINDOW_TILE[j]
        cut = GATE + MLA_COLS
        if hi <= cut:
            branches.append(lambda w, a=lo - base: w[:, a:a + W_IN_SHARD])
        elif lo >= cut:
            branches.append(lambda w, a=lo + gap - base: w[:, a:a + W_IN_SHARD])
        else:
            branches.append(lambda w, a=lo - base, n1=cut - lo, b=cut + gap - base, n2=hi - cut:
                            jnp.concatenate([w[:, a:a + n1], w[:, b:b + n2]], axis=1))
    return lax.switch(chip, branches, win)


def chip_major_grads(stage, g):
    if stage == 0:
        padded = jnp.concatenate([g["gate"], g["mla"], g["rw"], g["cv"]], axis=1)
        return dict(w_in=jnp.stack([padded[:, 128 * t:128 * t + W_IN_WINDOW] for t in W_IN_WINDOW_TILE]))
    if stage == 1:
        return dict(mla_wq_b=g["wq"].reshape(QL, 4, -1).transpose(1, 0, 2), mla_wkv_b=g["wkv"], mla_w_o=g["wo"],
                    rwkv_w_o=g["rwo"], conv_w_o=g["cvo"], w_out=g["out"].reshape(4, D // 4, D))
    return dict(w_up=g["up"], w_down=g["down"].reshape(4, DFF // 4, D))


def _row(v):
    return v.reshape(1, -1)


def local_step(x, positions, target, w, sm, big_of=None, on_grads=None, on_small=None):
    if big_of is None:
        big_of = lambda l, stage, _after: {n: w[n][l] for n in STAGES[stage]}
    if on_grads is None:
        on_grads = lambda l, stage, slabs: None
    if on_small is None:
        on_small = lambda l, layer_small: None
    s_len = x.shape[0]
    t_row = _pick(s_len, 256, 8)
    t_wide = _pick(s_len, 128, 8)
    bd, rot = _consts()
    cos, sin = _rope_tables(positions)
    sds = lambda *shape: jax.ShapeDtypeStruct(shape, f32)
    sdb = lambda *shape: jax.ShapeDtypeStruct(shape, bf16)
    saved = []
    v_first = None
    for l in range(DEPTH):
        tag = f"l{l}_"
        lw = derive_stage(0, big_of(l, 0, x))
        vres = l > 0
        p_norm1 = [_row(sm["attn_norm"][l])]
        (h,) = rows_fwd(_fn_norm, [x], p_norm1, [], [sds(s_len, D)], tile=t_row, name=tag + "norm1")
        gate = mm(h, lw["gate"], name=tag + "proj_gate")
        mla = mm(h, lw["mla"], name=tag + "proj_mla")
        rwc = mm(h, lw["rw"], name=tag + "proj_rwkv")
        cvc = mm(h, lw["cv"], name=tag + "proj_conv")
        lw.update(derive_stage(1, big_of(l, 1, cvc)))
        p_mla = [_row(sm["mla_q_a_norm"][l]), _row(sm["mla_kv_a_norm"][l])]
        qn, kvn, kpe = rows_fwd(_fn_mla_prep, [mla], p_mla, [], [sdb(s_len, QL), sdb(s_len, KVL), sds(s_len, 128)],
                                tile=t_row, name=tag + "mla_prep")
        q_flat = mm(qn, lw["wq"], name=tag + "q_proj")
        kv_flat = mm(kvn, lw["wkv"], name=tag + "kv_proj")
        p_qk = [_row(sm["mla_q_norm"][l]), _row(sm["mla_k_norm"][l])]
        q, k, vv = rows_fwd(_fn_qk_post, [q_flat, kv_flat, kpe, cos, sin], p_qk, [rot],
                            [sds(MLA_H, s_len, DQK), sds(MLA_H, s_len, DQK), sds(MLA_H, s_len, DV)], tile=t_wide,
                            name=tag + "qk_post")
        o = attn_fwd(q, k, vv, tq=_pick(s_len, 256, 8), name=tag + "attn")
        o_a = mm(o, lw["wo"], name=tag + "o_a")
        p_rw = [_row(sm["rwkv_mu"][l]), _row(sm["rwkv_w0"][l]), w["rwkv_w2"][l], _row(sm["rwkv_a0"][l]),
                w["rwkv_a2"][l], w["rwkv_g2"][l], _row(sm["rwkv_k_k"][l]), _row(sm["rwkv_k_a"][l])]
        rw_rows, rw_halos = [rwc], (0,)
        if vres:
            p_rw += [w["rwkv_v1"][l - 1], _row(sm["rwkv_v_mu"][l - 1]), _row(sm["rwkv_v0"][l - 1]), w["rwkv_v2"][l - 1]]
            rw_rows, rw_halos = [rwc, h, v_first], (0, 1)
        fn_prep = _make_fn_rwkv_prep(vres)
        r, ld, k2, v, an, bn, g = rows_fwd(fn_prep, rw_rows, p_rw, [bd], [sds(s_len, RW)] * 7, tile=t_row,
                                           name=tag + "rwkv_prep", halos=rw_halos)
        if not vres:
            v_first = v
        y, states = wkv_fwd(r, ld, k2, v, an, bn, name=tag + "wkv")
        p_post = [_row(sm["rwkv_ln_w"][l]), _row(sm["rwkv_ln_b"][l]), _row(sm["rwkv_r_k"][l])]
        (yb,) = rows_fwd(_fn_rwkv_post, [y, r, k2, v, g], p_post, [bd], [sdb(s_len, RW)], tile=t_row,
                         name=tag + "rwkv_post")
        o_b = mm(yb, lw["rwo"], name=tag + "o_b")
        p_cv = [w["conv_w"][l][q:q + 1] for q in range(3)]
        (yc,) = rows_fwd(_fn_conv, [cvc], p_cv, [], [sdb(s_len, CW)], tile=t_row, name=tag + "conv", halos=(0,))
        o_c = mm(yc, lw["cvo"], name=tag + "o_c")
        (merged,) = rows_fwd(_fn_merge, [gate, o_a, o_b, o_c], [], [], [sdb(s_len, D)], tile=t_wide,
                             name=tag + "merge")
        x1 = mm(merged, lw["out"], add=x, name=tag + "out_proj")
        lw.update(derive_stage(2, big_of(l, 2, x1)))
        p_norm2 = [_row(sm["mlp_norm"][l])]
        (h2,) = rows_fwd(_fn_norm, [x1], p_norm2, [], [sdb(s_len, D)], tile=t_row, name=tag + "norm2")
        up, act = mm(h2, lw["up"], relu2_out=True, name=tag + "up")
        x2 = mm(act, lw["down"], add=x1, name=tag + "down")
        saved.append(dict(lw=lw, x=x, h=h, gate=gate, mla=mla, rwc=rwc, cvc=cvc, qn=qn, kvn=kvn, kpe=kpe,
                          q_flat=q_flat, kv_flat=kv_flat, vv=vv, q=q, k=k, o=o, o_a=o_a, r=r, ld=ld, k2=k2, v=v,
                          an=an, bn=bn, g=g, y=y, states=states, yb=yb, o_b=o_b, yc=yc, o_c=o_c, merged=merged,
                          x1=x1, h2=h2, up=up, act=act, p_norm1=p_norm1, p_mla=p_mla, p_qk=p_qk, p_rw=p_rw,
                          p_post=p_post, p_cv=p_cv, p_norm2=p_norm2, rw_rows=rw_rows, rw_halos=rw_halos,
                          fn_prep=fn_prep, v_first=v_first if vres else None))
        x = x2

    loss, dx = loss_head(x, target, tile=t_row, name="loss_head")

    grads = {n: [None] * (DEPTH - 1 if n in ("rwkv_v1", "rwkv_v_mu", "rwkv_v0", "rwkv_v2") else DEPTH)
             for n in WEIGHTS}
    dv_first = None
    for l in reversed(range(DEPTH)):
        tag = f"b{l}_"
        sv = saved[l]
        lw = sv["lw"]
        vres = l > 0
        g_down = mm(sv["act"], dx, ta=True, out_dtype=bf16, name=tag + "g_down")
        dup = mm(dx, lw["down"], tb=True, act_grad=sv["up"], out_dtype=bf16, name=tag + "d_up")
        g_up = mm(sv["h2"], dup, ta=True, n_split=4, out_dtype=bf16, name=tag + "g_up")
        dh2 = mm(dup, lw["up"], tb=True, name=tag + "d_h2")
        slabs = chip_major_grads(2, dict(up=g_up, down=g_down))
        token = on_grads(l, 2, slabs)
        p_norm2 = sv["p_norm2"] if token is None else [sv["p_norm2"][0] + token[0, 0]]
        (dx1,), (g_n2,) = rows_bwd(_fn_norm, [sv["x1"]], p_norm2, [], [[dh2]], tile=t_row,
                                   name=tag + "norm2", extra={0: [dx]})
        g_out = mm(sv["merged"], dx1, ta=True, out_dtype=bf16, name=tag + "g_out")
        dmerged = mm(dx1, lw["out"], tb=True, name=tag + "d_merged")
        (dgate, do_a, do_b, do_c), _ = rows_bwd(_fn_merge, [sv["gate"], sv["o_a"], sv["o_b"], sv["o_c"]], [], [],
                                                [[dmerged]], tile=t_wide, name=tag + "merge",
                                                grad_dtypes=[bf16] * 4)
        g_cvo = mm(sv["yc"], do_c, ta=True, n_split=4, out_dtype=bf16, name=tag + "g_cvo")
        dyc = mm(do_c, lw["cvo"], tb=True, name=tag + "d_yc")
        (dcvc,), g_cw = rows_bwd(_fn_conv, [sv["cvc"]], sv["p_cv"], [], [[dyc]], tile=t_row, name=tag + "conv",
                                    halos=(0,))
        g_rwo = mm(sv["yb"], do_b, ta=True, n_split=4, out_dtype=bf16, name=tag + "g_rwo")
        dyb = mm(do_b, lw["rwo"], tb=True, name=tag + "d_yb")
        (dy, dr_p, dk_p, dv_p, dg), g_post = rows_bwd(
            _fn_rwkv_post, [sv["y"], sv["r"], sv["k2"], sv["v"], sv["g"]], sv["p_post"], [bd], [[dyb]], tile=t_row,
            name=tag + "rwkv_post")
        dr_s, dld, dk_s, dv_s, dan, dbn = wkv_bwd(sv["r"], sv["ld"], sv["k2"], sv["v"], sv["an"], sv["bn"],
                                                  sv["states"], dy, name=tag + "wkv")
        dv_list = [dv_s, dv_p] + ([dv_first] if (not vres and dv_first is not None) else [])
        d_prep, g_prep = rows_bwd(
            sv["fn_prep"], sv["rw_rows"], sv["p_rw"], [bd],
            [[dr_s, dr_p], [dld], [dk_s, dk_p], dv_list, [dan], [dbn], [dg]], tile=t_row, name=tag + "rwkv_prep",
            halos=sv["rw_halos"])
        drwc = d_prep[0]
        dh_extra = []
        if vres:
            dh_extra = [d_prep[1]]
            dv_first = d_prep[2]
        g_wo = mm(sv["o"], do_a, ta=True, n_split=4, out_dtype=bf16, name=tag + "g_wo")
        do = mm(do_a, lw["wo"], tb=True, name=tag + "d_o")
        dq, dk, dvv = attn_bwd(sv["q"], sv["k"], sv["vv"], do, tq=_pick(s_len, 256, 8), name=tag + "attn")
        (dq_flat, dkv_flat, dkpe), g_qk = rows_bwd(
            _fn_qk_post, [sv["q_flat"], sv["kv_flat"], sv["kpe"], cos, sin], sv["p_qk"], [rot], [[dq], [dk], [dvv]],
            tile=t_wide, name=tag + "qk_post", grad_rows=[0, 1, 2], grad_dtypes=[bf16, bf16, f32])
        g_wq = mm(sv["qn"], dq_flat, ta=True, out_dtype=bf16, name=tag + "g_wq")
        g_wkv = mm(sv["kvn"], dkv_flat, ta=True, n_split=4, out_dtype=bf16, name=tag + "g_wkv")
        dqn = mm(dq_flat, lw["wq"], tb=True, name=tag + "d_qn")
        dkvn = mm(dkv_flat, lw["wkv"], tb=True, name=tag + "d_kvn")
        slabs.update(chip_major_grads(1, dict(wq=g_wq, wkv=g_wkv, wo=g_wo, rwo=g_rwo, cvo=g_cvo, out=g_out)))
        token = on_grads(l, 1, {n: slabs[n] for n in STAGES[1]})
        p_mla = sv["p_mla"] if token is None else [sv["p_mla"][0] + token[0, 0], sv["p_mla"][1]]
        (dmla,), g_mla = rows_bwd(_fn_mla_prep, [sv["mla"]], p_mla, [], [[dqn], [dkvn], [dkpe]], tile=t_row,
                                  name=tag + "mla_prep", grad_dtypes=[bf16])
        g_gate = mm(sv["h"], dgate, ta=True, out_dtype=bf16, name=tag + "g_gate")
        g_mlaw = mm(sv["h"], dmla, ta=True, out_dtype=bf16, name=tag + "g_mla")
        g_rw = mm(sv["h"], drwc, ta=True, out_dtype=bf16, name=tag + "g_rw")
        g_cv = mm(sv["h"], dcvc, ta=True, out_dtype=bf16, name=tag + "g_cv")
        dh = mm(dgate, lw["gate"], tb=True, name=tag + "d_h_gate")
        dh = mm(dmla, lw["mla"], tb=True, add=dh, name=tag + "d_h_mla")
        dh = mm(drwc, lw["rw"], tb=True, add=dh, name=tag + "d_h_rw")
        dh = mm(dcvc, lw["cv"], tb=True, add=dh, name=tag + "d_h_cv")
        (dx,), (g_n1,) = rows_bwd(_fn_norm, [sv["x"]], sv["p_norm1"], [], [[dh] + dh_extra], tile=t_row,
                                  name=tag + "norm1", extra={0: [dx1]})
        slabs.update(chip_major_grads(0, dict(gate=g_gate, mla=g_mlaw, rw=g_rw, cv=g_cv)))
        token = on_grads(l, 0, {n: slabs[n] for n in STAGES[0]})
        if token is not None and l > 0:
            dx = dx + token[0, 0]
        for n, val in slabs.items():
            grads[n][l] = val
        layer_small = [("attn_norm", l, g_n1), ("mlp_norm", l, g_n2), ("mla_q_a_norm", l, g_mla[0]),
                       ("mla_kv_a_norm", l, g_mla[1]), ("mla_q_norm", l, g_qk[0]), ("mla_k_norm", l, g_qk[1]),
                       ("rwkv_ln_w", l, g_post[0]), ("rwkv_ln_b", l, g_post[1]), ("rwkv_r_k", l, g_post[2]),
                       ("conv_w", l, jnp.concatenate(g_cw, axis=0))]
        layer_small += list(zip(["rwkv_mu", "rwkv_w0", "rwkv_w2", "rwkv_a0", "rwkv_a2", "rwkv_g2", "rwkv_k_k",
                                 "rwkv_k_a"], [l] * 8, g_prep[:8]))
        if vres:
            layer_small += list(zip(["rwkv_v1", "rwkv_v_mu", "rwkv_v0", "rwkv_v2"], [l - 1] * 4, g_prep[8:12]))
        for n, index, val in layer_small:
            grads[n][index] = val
        if l == 0:
            layer_small.append(("loss", 0, loss.reshape(1, 1)))
        token = on_small(l, layer_small)
        if token is not None and l > 0:
            dx = dx + token[0, 0]
    return loss, dx, grads


def _split3(a):
    hi = a.astype(bf16)
    r1 = a - hi.astype(f32)
    mid = r1.astype(bf16)
    lo = (r1 - mid.astype(f32)).astype(bf16)
    return hi, mid, lo


def _shard_axis(name):
    return 1 if name in ROW_SHARDED else 2


def _pack(pieces, width, dtype, row_align):
    flat = jnp.concatenate([p.reshape(-1).astype(dtype) for p in pieces])
    rows = -(-flat.shape[0] // width)
    rows = -(-rows // row_align) * row_align
    return jnp.pad(flat, (0, rows * width - flat.shape[0])).reshape(rows, width)


def _unpack(flat2d, shapes):
    flat = flat2d.reshape(-1)
    out, off = [], 0
    for shp in shapes:
        n = int(np.prod(shp))
        out.append(flat[off:off + n].reshape(shp))
        off += n
    return out


def kernel(x, positions, attn_norm, w_in, mla_q_a_norm, mla_wq_b, mla_kv_a_norm, mla_wkv_b, mla_q_norm, mla_k_norm, mla_w_o, rwkv_mu, rwkv_w0, rwkv_w2, rwkv_a0, rwkv_a2, rwkv_g2, rwkv_k_k, rwkv_k_a, rwkv_r_k, rwkv_ln_w, rwkv_ln_b, rwkv_w_o, rwkv_v1, rwkv_v_mu, rwkv_v0, rwkv_v2, conv_w, conv_w_o, w_out, mlp_norm, w_up, w_down, loss_target, m_attn_norm, m_w_in, m_mla_q_a_norm, m_mla_wq_b, m_mla_kv_a_norm, m_mla_wkv_b, m_mla_q_norm, m_mla_k_norm, m_mla_w_o, m_rwkv_mu, m_rwkv_w0, m_rwkv_w2, m_rwkv_a0, m_rwkv_a2, m_rwkv_g2, m_rwkv_k_k, m_rwkv_k_a, m_rwkv_r_k, m_rwkv_ln_w, m_rwkv_ln_b, m_rwkv_w_o, m_rwkv_v1, m_rwkv_v_mu, m_rwkv_v0, m_rwkv_v2, m_conv_w, m_conv_w_o, m_w_out, m_mlp_norm, m_w_up, m_w_down, v_attn_norm, v_w_in, v_mla_q_a_norm, v_mla_wq_b, v_mla_kv_a_norm, v_mla_wkv_b, v_mla_q_norm, v_mla_k_norm, v_mla_w_o, v_rwkv_mu, v_rwkv_w0, v_rwkv_w2, v_rwkv_a0, v_rwkv_a2, v_rwkv_g2, v_rwkv_k_k, v_rwkv_k_a, v_rwkv_r_k, v_rwkv_ln_w, v_rwkv_ln_b, v_rwkv_w_o, v_rwkv_v1, v_rwkv_v_mu, v_rwkv_v0, v_rwkv_v2, v_conv_w, v_conv_w_o, v_w_out, v_mlp_norm, v_w_up, v_w_down):
    args = dict(locals())
    wts = {n: args[n] for n in WEIGHTS}
    mom = {n: args["m_" + n] for n in WEIGHTS}
    var = {n: args["v_" + n] for n in WEIGHTS}
    chip = 2 * lax.axis_index("x") + lax.axis_index("y")
    core = lax.axis_index("c").astype(jnp.int32).reshape(1)

    med_names = [n for n in MED if n != "conv_w"]
    med_pieces = [wts[n] for n in med_names] + list(_split3(wts["conv_w"]))
    med_shapes = [p.shape for p in med_pieces]
    chip_idx = chip.astype(jnp.int32).reshape(1)
    shards_first = [wts[n][0].astype(bf16) for n in STAGES[0]] + [_pack(med_pieces, 128, bf16, 32)]
    got_first, token = gather_weights(shards_first, name="gather_l0_s0")
    got_first = [place_slab(g, s, chip_idx, name=f"place_own_l0_s0_{q}")
                 for q, (g, s) in enumerate(zip(got_first, shards_first))]
    in_flight = {}
    for key, names, l in (("l0_s1", STAGES[1], 0), ("l0_s2", STAGES[2], 0), ("l1", BIG, 1)):
        group = [wts[n][l].astype(bf16) for n in names]
        group[0] = group[0] + token[0, 0].astype(bf16)
        in_flight[key] = (names, gather_start(group, name="gather_start_" + key))
        token = in_flight[key][1][4]

    def whole_of(names, slabs):
        out = {}
        for n, by_chip in zip(names, slabs):
            _, rows, cols = by_chip.shape
            if n in ROW_SHARDED:
                out[n] = by_chip.reshape(4 * rows, cols)
            else:
                out[n] = by_chip.transpose(1, 0, 2).reshape(rows, 4 * cols)
        return out

    landed = {}

    def big_of(l, stage, after):
        if (l, stage) == (0, 0):
            return whole_of(STAGES[0], got_first)
        key = "l1" if l == 1 else f"l0_s{stage}"
        if key not in landed:
            names, (send_sems, recv_sems, thru, lands, _) = in_flight[key]
            thru, lands = gather_wait(send_sems, recv_sems, thru, lands, after, name="gather_wait_" + key)
            lands = gather_forward(lands, name="gather_forward_" + key)
            landed[key] = whole_of(names, [place_slab(g, s, chip_idx, name=f"place_own_{key}_{q}")
                                           for q, (g, s) in enumerate(zip(lands, thru))])
        return {n: landed[key][n] for n in STAGES[stage]}

    whole = {}
    per_chip = [_unpack(got_first[len(STAGES[0])][j], med_shapes) for j in range(4)]
    for q, n in enumerate(med_names):
        whole[n] = jnp.concatenate([per_chip[j][q] for j in range(4)], axis=_shard_axis(n)).astype(f32)
    base = len(med_names)
    cw_parts = [jnp.concatenate([per_chip[j][base + t] for j in range(4)], axis=2).astype(f32) for t in range(3)]
    whole["conv_w"] = (cw_parts[0] + cw_parts[1]) + cw_parts[2]
    small = {n: wts[n] for n in SMALL}
    small["rwkv_r_k"] = wts["rwkv_r_k"].reshape(DEPTH, RW)

    exchanges = []

    def on_grads(l, stage, slabs):
        names = STAGES[stage]
        tag = f"l{l}_s{stage}"
        parts = [slabs[n] for n in names]
        from_sibling = grads_to_sibling(parts, name="grads_to_sibling_" + tag)
        chip_sums = [pair_sum(s, t, core, name=f"pair_sum_{n}_{l}") for n, s, t in zip(names, parts, from_sibling)]
        started = scatter_start(chip_sums, name="scatter_start_" + tag)
        exchanges.append((l, names, tag, started))
        return started[4]

    broadcasts = []

    def on_small(l, layer_small):
        values = [val for _, _, val in layer_small]
        started = broadcast_start(_pack(values, 128, f32, 8), name=f"small_start_l{l}")
        broadcasts.append((l, [(n, index, val.shape) for n, index, val in layer_small], started))
        return started[4]

    small["attn_norm"] = small["attn_norm"] + token[0, 0]
    _, grad_x, grads = local_step(x[0], positions[0], loss_target[0], whole, small, big_of, on_grads, on_small)

    device = (4 * lax.axis_index("x") + 2 * lax.axis_index("y") + lax.axis_index("c")).astype(jnp.int32).reshape(1)
    summed = {n: [None] * len(grads[n]) for n in SMALL + MED}
    summed["loss"] = [None]
    for l, entries, (send_sems, recv_sems, thru, land, _) in broadcasts:
        own, land = broadcast_wait(send_sems, recv_sems, thru, land, grad_x, name=f"small_wait_l{l}")
        total = sum8(land, own, device, name=f"small_sum_l{l}")
        for (n, index, _), val in zip(entries, _unpack(total, [shape for _, _, shape in entries])):
            summed[n][index] = val
    gsum = {}
    for n in SMALL + MED:
        g = jnp.stack(summed[n])
        if n in MED:
            ax = _shard_axis(n)
            width = wts[n].shape[ax]
            g = lax.dynamic_slice_in_dim(g, chip * width, width, axis=ax)
        gsum[n] = g.reshape(wts[n].shape)
    where = jnp.stack([lax.axis_index("c"), chip]).astype(jnp.int32)
    bufs, layout = {}, []
    for l, names, tag, (send_sems, recv_sems, thru, lands, _) in exchanges:
        own, arrived = scatter_wait(send_sems, recv_sems, thru, lands, grad_x, name="scatter_wait_" + tag)
        for n, mine, theirs in zip(names, own, arrived):
            rows = 2 * theirs.shape[1]
            bufs[n] = sum4_into(theirs, mine, bufs.get(n), where, layer=l, total_rows=DEPTH * rows,
                                name=f"sum_chips_{n}_{l}")
            layout.append((BIG.index(n), l * rows, rows))
    reduced = join_halves([bufs[n] for n in BIG], layout, name="join_halves")

    out_g, out_d, out_m, out_v = {}, {}, {}, {}
    for q, n in enumerate(BIG):
        shp = wts[n].shape
        as2d = lambda a: a.reshape(-1, shp[-1])
        g2d = w_in_window_cols(reduced[q], chip) if n == "w_in" else reduced[q]
        res = adamw(as2d(wts[n]), g2d, as2d(mom[n]), as2d(var[n]), name="adamw_" + n)
        out_g[n], out_d[n], out_m[n], out_v[n] = [r.reshape(shp) for r in res]
    sm_all = SMALL + MED
    flat2d = lambda a: a.reshape(-1, a.shape[-1])
    res = adamw_many([flat2d(wts[n]) for n in sm_all], [flat2d(gsum[n]) for n in sm_all],
                     [flat2d(mom[n]) for n in sm_all], [flat2d(var[n]) for n in sm_all], name="adamw_small")
    for tgt, vals in zip((out_d, out_m, out_v), res):
        for n, val in zip(sm_all, vals):
            tgt[n] = val.reshape(wts[n].shape)
    out_g.update({n: gsum[n] for n in sm_all})
    loss = summed["loss"][0].reshape(())
    return (loss, grad_x[None], *[out_g[n] for n in WEIGHTS], *[out_d[n] for n in WEIGHTS],
            *[out_m[n] for n in WEIGHTS], *[out_v[n] for n in WEIGHTS])
```

```python
import functools

import jax
import jax.numpy as jnp
import numpy as np
from jax import lax
from jax.experimental import pallas as pl
from jax.experimental.pallas import tpu as pltpu

f32, bf16 = jnp.float32, jnp.bfloat16
HI = lax.Precision.HIGHEST
MESH = pl.DeviceIdType.MESH

D = 1024
DEPTH = 2
MLA_H, NOPE, ROPE, DQK, DV = 8, 64, 32, 96, 64
QL, KVL = 384, 256
RW, RH, RN = 256, 4, 64
DL, AL, GL, MVL = 64, 64, 128, 32
CW = 256
DFF = 4096
GATE = 3 * D
MLA_COLS = QL + KVL + ROPE
MLA_PAD = 768
Q_HEAD_PAD = 128
NORM_EPS = 1e-6
GN_EPS = 64e-5
ROPE_THETA = 10000.0
LR, B1, B2, EPS, WD, STEP = 0.001, 0.9, 0.999, 1e-08, 0.01, 10

VMEM_LIMIT = 52 * 1024 * 1024
WKV_CHUNK = 64
WKV_CHUNKS_PER_STEP = 4
ATTN_SEGMENTS = 4

BIG = ["w_in", "mla_wq_b", "mla_wkv_b", "mla_w_o", "rwkv_w_o", "conv_w_o", "w_out", "w_up", "w_down"]
MED = ["rwkv_w2", "rwkv_a2", "rwkv_g2", "rwkv_v1", "rwkv_v2", "conv_w"]
ROW_SHARDED = {"w_out", "w_down", "rwkv_v1"}
SMALL = ["attn_norm", "mla_q_a_norm", "mla_kv_a_norm", "mla_q_norm", "mla_k_norm", "rwkv_mu", "rwkv_w0",
         "rwkv_a0", "rwkv_k_k", "rwkv_k_a", "rwkv_r_k", "rwkv_ln_w", "rwkv_ln_b", "rwkv_v_mu", "rwkv_v0",
         "mlp_norm"]
WEIGHTS = ["attn_norm", "w_in", "mla_q_a_norm", "mla_wq_b", "mla_kv_a_norm", "mla_wkv_b", "mla_q_norm",
           "mla_k_norm", "mla_w_o", "rwkv_mu", "rwkv_w0", "rwkv_w2", "rwkv_a0", "rwkv_a2", "rwkv_g2",
           "rwkv_k_k", "rwkv_k_a", "rwkv_r_k", "rwkv_ln_w", "rwkv_ln_b", "rwkv_w_o", "rwkv_v1", "rwkv_v_mu",
           "rwkv_v0", "rwkv_v2", "conv_w", "conv_w_o", "w_out", "mlp_norm", "w_up", "w_down"]


def _cparams(sem=None):
    return pltpu.CompilerParams(dimension_semantics=sem, vmem_limit_bytes=VMEM_LIMIT)


def _pick(dim, pref, align):
    if dim <= pref:
        return dim
    t = (pref // align) * align
    while t >= align:
        if dim % t == 0:
            return t
        t -= align
    return dim


def _bdot(a, b, dims):
    return lax.dot_general(a.astype(bf16), b.astype(bf16), (dims, ((), ())), preferred_element_type=f32)


@jax.custom_vjp
def _mm(a, b):
    return _bdot(a, b, ((1,), (0,)))


def _mm_fwd(a, b):
    return _mm(a, b), (a, b)


def _mm_bwd(res, g):
    a, b = res
    return _bdot(g, b, ((1,), (1,))), _bdot(a, g, ((0,), (0,)))


_mm.defvjp(_mm_fwd, _mm_bwd)


@jax.custom_vjp
def _mm_nt(a, b):
    return _bdot(a, b, ((1,), (1,)))


def _mm_nt_fwd(a, b):
    return _mm_nt(a, b), (a, b)


def _mm_nt_bwd(res, g):
    a, b = res
    return _bdot(g, b, ((1,), (0,))), _bdot(g, a, ((0,), (0,)))


_mm_nt.defvjp(_mm_nt_fwd, _mm_nt_bwd)


_NN, _NT, _TN = ((1,), (0,)), ((1,), (1,)), ((0,), (0,))


def _dg(a, b, dims):
    return lax.dot_general(a, b, (dims, ((), ())), preferred_element_type=f32)


def _bf16_pieces(x, count):
    out, rest = [], x
    for q in range(count):
        piece = rest.astype(bf16)
        out.append(piece)
        if q + 1 < count:
            rest = rest - piece.astype(f32)
    return out


def _dot3(a, b, dims):
    (ah, al), (bh, bl) = _bf16_pieces(a, 2), _bf16_pieces(b, 2)
    return _dg(ah, bh, dims) + (_dg(ah, bl, dims) + _dg(al, bh, dims))


@jax.custom_vjp
def _hdot(a, b):
    return _dot3(a, b, _NN)


@jax.custom_vjp
def _hdot_nt(a, b):
    return _dot3(a, b, _NT)


@jax.custom_vjp
def _hdot_tn(a, b):
    return _dot3(a, b, _TN)


_hdot.defvjp(lambda a, b: (_hdot(a, b), (a, b)), lambda res, g: (_hdot_nt(g, res[1]), _hdot_tn(res[0], g)))
_hdot_nt.defvjp(lambda a, b: (_hdot_nt(a, b), (a, b)), lambda res, g: (_hdot(g, res[1]), _hdot_tn(g, res[0])))
_hdot_tn.defvjp(lambda a, b: (_hdot_tn(a, b), (a, b)), lambda res, g: (_hdot_nt(res[1], g), _hdot(res[0], g)))


_BNN, _BNT, _BTN = ((2,), (1,)), ((2,), (2,)), ((1,), (1,))


def _bdg(a, b, dims):
    return lax.dot_general(a, b, (dims, ((0,), (0,))), preferred_element_type=f32)


def _bdot3(a, b, dims):
    (ah, al), (bh, bl) = _bf16_pieces(a, 2), _bf16_pieces(b, 2)
    return _bdg(ah, bh, dims) + (_bdg(ah, bl, dims) + _bdg(al, bh, dims))


@jax.custom_vjp
def _hbnn(a, b):
    return _bdot3(a, b, _BNN)


@jax.custom_vjp
def _hbnt(a, b):
    return _bdot3(a, b, _BNT)


@jax.custom_vjp
def _hbtn(a, b):
    return _bdot3(a, b, _BTN)


_hbnn.defvjp(lambda a, b: (_hbnn(a, b), (a, b)), lambda res, g: (_hbnt(g, res[1]), _hbtn(res[0], g)))
_hbnt.defvjp(lambda a, b: (_hbnt(a, b), (a, b)), lambda res, g: (_hbnn(g, res[1]), _hbtn(g, res[0])))
_hbtn.defvjp(lambda a, b: (_hbtn(a, b), (a, b)), lambda res, g: (_hbnt(res[1], g), _hbnn(res[0], g)))


@functools.partial(jax.custom_vjp, nondiff_argnums=(2,))
def _exact_bl(m, x, transposed):
    mb = m.astype(bf16)
    hi, mid, lo = _bf16_pieces(x, 3)
    dims = _BTN if transposed else _BNN
    return (_bdg(mb, hi, dims) + _bdg(mb, mid, dims)) + _bdg(mb, lo, dims)


_exact_bl.defvjp(lambda m, x, transposed: (_exact_bl(m, x, transposed), m),
                 lambda transposed, m, g: (jnp.zeros_like(m), _exact_bl(m, g, not transposed)))


@functools.partial(jax.custom_vjp, nondiff_argnums=(2,))
def _exact_l(m, x, transposed):
    mb = m.astype(bf16)
    hi, mid, lo = _bf16_pieces(x, 3)
    dims = _TN if transposed else _NN
    return (_dg(mb, hi, dims) + _dg(mb, mid, dims)) + _dg(mb, lo, dims)


_exact_l.defvjp(lambda m, x, transposed: (_exact_l(m, x, transposed), m),
                lambda transposed, m, g: (jnp.zeros_like(m), _exact_l(m, g, not transposed)))


@functools.partial(jax.custom_vjp, nondiff_argnums=(2,))
def _exact_r(x, m, transposed):
    mb = m.astype(bf16)
    hi, mid, lo = _bf16_pieces(x, 3)
    dims = _NT if transposed else _NN
    return (_dg(hi, mb, dims) + _dg(mid, mb, dims)) + _dg(lo, mb, dims)


_exact_r.defvjp(lambda x, m, transposed: (_exact_r(x, m, transposed), m),
                lambda transposed, m, g: (_exact_r(g, m, not transposed), jnp.zeros_like(m)))


def _rms(x, g, eps=NORM_EPS):
    return x * lax.rsqrt(jnp.mean(x * x, axis=-1, keepdims=True) + eps) * g


def _sigmoid(x):
    return 1.0 / (1.0 + jnp.exp(-x))


def _softplus(x):
    return jnp.maximum(x, 0.0) + jnp.log(1.0 + jnp.exp(-jnp.maximum(x, -x)))


def _lane_split(x, sizes):
    bounds = np.cumsum([0] + list(sizes))

    @jax.custom_vjp
    def split(v):
        return tuple(v[..., int(bounds[q]):int(bounds[q + 1])] for q in range(len(sizes)))

    split.defvjp(lambda v: (split(v), None), lambda _, g: (jnp.concatenate(g, axis=-1),))
    return split(x)


def _row_split(x, sizes):
    bounds = np.cumsum([0] + list(sizes))

    @jax.custom_vjp
    def split(v):
        return tuple(v[..., int(bounds[q]):int(bounds[q + 1]), :] for q in range(len(sizes)))

    split.defvjp(lambda v: (split(v), None), lambda _, g: (jnp.concatenate(g, axis=-2),))
    return split(x)


def _shift_mats(t, k):
    r = lax.broadcasted_iota(jnp.int32, (t, t), 0)
    c = lax.broadcasted_iota(jnp.int32, (t, t), 1)
    inner = (r - c == k).astype(f32)
    r8 = lax.broadcasted_iota(jnp.int32, (t, 8), 0)
    c8 = lax.broadcasted_iota(jnp.int32, (t, 8), 1)
    edge = (c8 - r8 == 8 - k).astype(f32)
    return inner, edge


def _shift(x, halo, k):
    inner, edge = _shift_mats(x.shape[0], k)
    return _exact_l(inner, x, False) + jnp.dot(edge, halo, precision=HI, preferred_element_type=f32)


def mm(a, b, *, name, ta=False, tb=False, a_batched=False, b_batched=False, reduce_batch=False, add=None,
       act_grad=None, relu2_out=False, n_split=1, out_dtype=f32, tm=1024, tn=1024, tk=2048):
    ash, bsh = a.shape[-2:], b.shape[-2:]
    (k_, m_) = ash if ta else ash[::-1]
    (k2_, n_) = bsh[::-1] if tb else bsh
    assert k_ == k2_, (a.shape, b.shape, ta, tb)
    hb = a.shape[0] if a_batched else (b.shape[0] if b_batched else 1)
    batched_out = (a_batched or b_batched) and not reduce_batch
    h_out = hb if batched_out else 1
    h_red = hb if reduce_batch else 1
    tm = _pick(m_, tm, 128)
    tn = _pick(n_ // n_split, tn, 128)
    tk = _pick(k_, tk, 128)
    nm, nn, nk = m_ // tm, n_ // tn, k_ // tk

    def a_map(i, j, ho, hr, kk):
        blk = (kk, i) if ta else (i, kk)
        return ((ho if batched_out else hr),) + blk if a_batched else blk

    def b_map(i, j, ho, hr, kk):
        blk = (j, kk) if tb else (kk, j)
        return ((ho if batched_out else hr),) + blk if b_batched else blk

    a_blk = (tk, tm) if ta else (tm, tk)
    b_blk = (tn, tk) if tb else (tk, tn)
    in_specs = [pl.BlockSpec(((1,) + a_blk) if a_batched else a_blk, a_map),
                pl.BlockSpec(((1,) + b_blk) if b_batched else b_blk, b_map)]
    args = [a, b]
    for extra in (add, act_grad):
        if extra is not None:
            in_specs.append(pl.BlockSpec((tm, tn), lambda i, j, ho, hr, kk: (i, j)))
            args.append(extra)
    if n_split > 1:
        per = n_ // n_split // tn
        if batched_out:
            out_spec = pl.BlockSpec((1, 1, tm, tn), lambda i, j, ho, hr, kk: (j // per, ho, i, j % per))
            out_shape = jax.ShapeDtypeStruct((n_split, hb, m_, n_ // n_split), out_dtype)
        else:
            out_spec = pl.BlockSpec((1, tm, tn), lambda i, j, ho, hr, kk: (j // per, i, j % per))
            out_shape = jax.ShapeDtypeStruct((n_split, m_, n_ // n_split), out_dtype)
    elif batched_out:
        out_spec = pl.BlockSpec((1, tm, tn), lambda i, j, ho, hr, kk: (ho, i, j))
        out_shape = jax.ShapeDtypeStruct((hb, m_, n_), out_dtype)
    else:
        out_spec = pl.BlockSpec((tm, tn), lambda i, j, ho, hr, kk: (i, j))
        out_shape = jax.ShapeDtypeStruct((m_, n_), out_dtype)
    lead = (0,) * (int(batched_out) + int(n_split > 1))
    dims = ((0,) if ta else (1,), (1,) if tb else (0,))
    has_add, has_act = add is not None, act_grad is not None

    def body(*refs):
        a_ref, b_ref = refs[0], refs[1]
        pos = 2
        add_ref = act_ref = None
        if has_add:
            add_ref = refs[pos]
            pos += 1
        if has_act:
            act_ref = refs[pos]
            pos += 1
        o_ref, acc_ref = refs[pos], refs[-1]
        hr, kk = pl.program_id(3), pl.program_id(4)
        first = jnp.logical_and(hr == 0, kk == 0)
        last = jnp.logical_and(hr == h_red - 1, kk == nk - 1)
        av = a_ref[0] if a_batched else a_ref[...]
        bv = b_ref[0] if b_batched else b_ref[...]
        p = _bdot(av, bv, dims)
        single = h_red * nk == 1

        if not single:
            @pl.when(first)
            def _():
                acc_ref[...] = p

            @pl.when(jnp.logical_not(first))
            def _():
                acc_ref[...] += p

        @pl.when(last)
        def _():
            r = p if single else acc_ref[...]
            if has_act:
                r = r * (2.0 * jnp.maximum(act_ref[...], 0.0))
            if has_add:
                r = r + add_ref[...]
            if lead:
                o_ref[lead] = r.astype(out_dtype)
            else:
                o_ref[...] = r.astype(out_dtype)
            if relu2_out:
                refs[pos + 1][...] = jnp.square(jnp.maximum(r, 0.0)).astype(bf16)

    if relu2_out:
        assert not lead
        out_spec = [out_spec, out_spec]
        out_shape = [out_shape, jax.ShapeDtypeStruct(out_shape.shape, bf16)]
    return pl.pallas_call(
        body, name=name, grid=(nm, nn, h_out, h_red, nk), in_specs=in_specs, out_specs=out_spec,
        out_shape=out_shape, scratch_shapes=[pltpu.VMEM((tm, tn), f32)],
        compiler_params=_cparams(("parallel", "parallel", "parallel", "arbitrary", "arbitrary")),
    )(*args)


def _row_spec(arr, tile, idx):
    if arr.ndim == 2:
        return pl.BlockSpec((tile, arr.shape[1]), lambda i: (idx(i), 0))
    return pl.BlockSpec((arr.shape[0], tile, arr.shape[2]), lambda i: (0, idx(i), 0))


def _halo_spec(arr, tile, idx):
    per = tile // 8
    return pl.BlockSpec((8, arr.shape[1]), lambda i: (jnp.maximum(idx(i) * per - 1, 0), 0))


def _full_spec(arr):
    nd = arr.ndim
    return pl.BlockSpec(arr.shape, lambda i: (0,) * nd)


def _load_f32(ref):
    val = ref[...]
    return val.astype(f32) if val.dtype == bf16 else val


def rows_fwd(fn, rows, params, consts, out_shapes, *, tile, name, halos=()):
    s_len = rows[0].shape[-2]
    n = s_len // tile
    nr, nh, npar, nc = len(rows), len(halos), len(params), len(consts)
    ident = lambda i: i
    in_specs = ([_row_spec(r, tile, ident) for r in rows] + [_halo_spec(rows[h], tile, ident) for h in halos]
                + [_full_spec(p) for p in params] + [_full_spec(c) for c in consts])
    out_specs = [_row_spec(o, tile, ident) for o in out_shapes]

    def body(*refs):
        i = pl.program_id(0)
        rv = [_load_f32(r) for r in refs[:nr]]
        keep = (i > 0).astype(f32)
        hv = [r[...] * keep for r in refs[nr:nr + nh]]
        pv = [r[...] for r in refs[nr + nh:nr + nh + npar]]
        cv = [r[...] for r in refs[nr + nh + npar:nr + nh + npar + nc]]
        outs = fn(rv, hv, pv, cv)
        for o_ref, o in zip(refs[nr + nh + npar + nc:], outs):
            o_ref[...] = o.astype(o_ref.dtype)

    return pl.pallas_call(
        body, name=name, grid=(n,), in_specs=in_specs, out_specs=out_specs, out_shape=list(out_shapes),
        compiler_params=_cparams(("arbitrary",)),
    )(*rows, *[rows[h] for h in halos], *params, *consts)


def rows_bwd(fn, rows, params, consts, douts, *, tile, name, halos=(), grad_rows=None, extra=None,
             grad_dtypes=None):
    s_len = rows[0].shape[-2]
    n = s_len // tile
    nr, nh, npar, nc = len(rows), len(halos), len(params), len(consts)
    grad_rows = list(range(nr)) if grad_rows is None else list(grad_rows)
    extra = extra or {}
    assert all(h in grad_rows for h in halos)
    rev = lambda i: n - 1 - i
    dflat = [d for ds in douts for d in ds]
    dcount = [len(ds) for ds in douts]
    eflat = [e for g in grad_rows for e in extra.get(g, [])]
    ecount = [len(extra.get(g, [])) for g in grad_rows]
    in_specs = ([_row_spec(r, tile, rev) for r in rows] + [_halo_spec(rows[h], tile, rev) for h in halos]
                + [_full_spec(p) for p in params] + [_full_spec(c) for c in consts]
                + [_row_spec(d, tile, rev) for d in dflat] + [_row_spec(e, tile, rev) for e in eflat])
    grad_dtypes = [f32] * len(grad_rows) if grad_dtypes is None else list(grad_dtypes)
    assert all(grad_dtypes[q] == f32 for q, g in enumerate(grad_rows) if g in halos)
    out_shapes = ([jax.ShapeDtypeStruct(rows[g].shape, dt) for g, dt in zip(grad_rows, grad_dtypes)]
                  + [jax.ShapeDtypeStruct(p.shape, f32) for p in params])
    out_specs = [_row_spec(rows[g], tile, rev) for g in grad_rows] + [_full_spec(p) for p in params]
    scratch = [pltpu.VMEM((8, rows[h].shape[1]), f32) for h in halos]
    n_in = nr + nh + npar + nc + len(dflat) + len(eflat)
    n_out = len(grad_rows) + npar

    def body(*refs):
        i = pl.program_id(0)
        rv = [_load_f32(r) for r in refs[:nr]]
        keep = (i < n - 1).astype(f32)
        hv = [r[...] * keep for r in refs[nr:nr + nh]]
        pv = [r[...] for r in refs[nr + nh:nr + nh + npar]]
        pos = nr + nh + npar
        cv = [r[...] for r in refs[pos:pos + nc]]
        pos += nc
        dv = []
        for cnt in dcount:
            acc = _load_f32(refs[pos])
            for q in range(1, cnt):
                acc = acc + _load_f32(refs[pos + q])
            dv.append(acc)
            pos += cnt
        ev = []
        for cnt in ecount:
            ev.append([_load_f32(refs[pos + q]) for q in range(cnt)])
            pos += cnt
        out_refs = refs[n_in:n_in + n_out]
        carry_refs = refs[n_in + n_out:]

        def f(gr, gh, gp):
            full = list(rv)
            for g, val in zip(grad_rows, gr):
                full[g] = val
            return tuple(fn(full, gh, gp, cv))

        _, vjp = jax.vjp(f, [rv[g] for g in grad_rows], hv, pv)
        d_rows, d_halos, d_params = vjp(tuple(dv))

        @pl.when(i == 0)
        def _():
            for c_ref in carry_refs:
                c_ref[...] = jnp.zeros_like(c_ref)
            for p_ref in out_refs[len(grad_rows):]:
                p_ref[...] = jnp.zeros_like(p_ref)

        for q, g in enumerate(grad_rows):
            val = d_rows[q]
            for e in ev[q]:
                val = val + e
            out_refs[q][...] = val.astype(out_refs[q].dtype)
            if g in halos:
                hq = list(halos).index(g)
                out_refs[q][tile - 8:tile, :] += carry_refs[hq][...]
                carry_refs[hq][...] = d_halos[hq]
        for p_ref, dp in zip(out_refs[len(grad_rows):], d_params):
            p_ref[...] += dp

    res = pl.pallas_call(
        body, name=name, grid=(n,), in_specs=in_specs, out_specs=out_specs, out_shape=out_shapes,
        scratch_shapes=scratch, compiler_params=_cparams(("arbitrary",)),
    )(*rows, *[rows[h] for h in halos], *params, *consts, *dflat, *eflat)
    return list(res[:len(grad_rows)]), list(res[len(grad_rows):])


def _fn_norm(rows, halos, params, consts):
    return (_rms(rows[0], params[0]),)


def _fn_mla_prep(rows, halos, params, consts):
    cq, ckv, kpe = _lane_split(rows[0], (QL, KVL, MLA_PAD - QL - KVL))
    return _rms(cq, params[0]), _rms(ckv, params[1]), kpe


def _rope(x, cos, sin, rot):
    return x * cos + _exact_r(x, rot, False) * sin


def _fn_qk_post(rows, halos, params, consts):
    q_flat, kv_flat, kpe, cos, sin = rows
    q_norm, k_norm = params
    (rot,) = consts
    q_heads = _lane_split(q_flat, (DQK, Q_HEAD_PAD - DQK) * MLA_H)[::2]
    kv_heads = _lane_split(kv_flat, (NOPE, DV) * MLA_H)
    k_pe, _ = _lane_split(kpe, (ROPE, kpe.shape[1] - ROPE))
    qs = [_rope(_rms(qh, q_norm), cos, sin, rot) for qh in q_heads]
    ks = [_rope(_rms(jnp.concatenate([kv_heads[2 * h], k_pe], axis=-1), k_norm), cos, sin, rot)
          for h in range(MLA_H)]
    vs = [kv_heads[2 * h + 1] for h in range(MLA_H)]
    return jnp.stack(qs, axis=0), jnp.stack(ks, axis=0), jnp.stack(vs, axis=0)


def _seg(x, bd):
    return _exact_r(x, bd, False)


def _make_fn_rwkv_prep(vres):
    def fn(rows, halos, params, consts):
        cols = rows[0]
        bd = consts[0]
        mu, w0, w2, a0, a2, g2, k_k, k_a = params[:8]
        prev = _shift(cols, halos[0], 1)
        c = cols + (prev - cols) * mu
        r, k, v, xw, xa, xg = _lane_split(c, (RW, RW, RW, DL, AL, GL))
        log_w = -_softplus(-(w0 + _mm(jnp.tanh(xw), w2))) - 0.5
        ld = -jnp.exp(log_w)
        a = _sigmoid(a0 + _mm(xa, a2))
        g = _mm(_sigmoid(xg), g2)
        if vres:
            hcur, v_first = rows[1], rows[2]
            v1, v_mu, v0, v2 = params[8:12]
            xv = _mm(hcur, v1)
            xv_prev = _shift(xv, _mm(halos[1], v1), 1)
            xv = xv + (xv_prev - xv) * v_mu
            v = v + (v_first - v) * _sigmoid(v0 + _mm(xv, v2))
        kk = k * k_k
        kk = kk / jnp.maximum(jnp.sqrt(_seg(kk * kk, bd)), 1e-12)
        k2 = k * (1.0 + (a - 1.0) * k_a)
        return r, ld, k2, v, -kk, kk * a, g
    return fn


def _fn_rwkv_post(rows, halos, params, consts):
    y, r, k2, v, g = rows
    ln_w, ln_b, r_k = params
    bd = consts[0]
    mean = _seg(y, bd) * (1.0 / RN)
    d = y - mean
    var = _seg(d * d, bd) * (1.0 / RN)
    yn = d * lax.rsqrt(var + GN_EPS) * ln_w + ln_b
    bonus = _seg(r * k2 * r_k, bd) * v
    return ((yn + bonus) * g,)


def _fn_conv(rows, halos, params, consts):
    cols, halo = rows[0], halos[0]
    w0, w1, w2 = params
    b, c, x = _lane_split(cols, (CW, CW, CW))
    _, ch, xh = _lane_split(halo, (CW, CW, CW))
    u, uh = c * x, ch * xh
    return (b * (w0 * _shift(u, uh, 2) + w1 * _shift(u, uh, 1) + w2 * u),)


def _fn_merge(rows, halos, params, consts):
    gate, o_a, o_b, o_c = rows
    g_a, g_b, g_c = _lane_split(gate, (D, D, D))
    return (_sigmoid(g_a) * o_a + _sigmoid(g_b) * o_b + _sigmoid(g_c) * o_c,)


def _attn_block(q, k, v, q0, diagonal_last):
    tq, kend = q.shape[0], k.shape[0]
    s = _mm_nt(q, k) * (DQK ** -0.5)
    if diagonal_last:
        tri = lax.broadcasted_iota(jnp.int32, (tq, tq), 0) >= lax.broadcasted_iota(jnp.int32, (tq, tq), 1)
        if kend > tq:
            before, diag = _lane_split(s, (kend - tq, tq))
            s = jnp.concatenate([before, jnp.where(tri, diag, -1e30)], axis=-1)
        else:
            s = jnp.where(tri, s, -1e30)
    else:
        row = q0 + lax.broadcasted_iota(jnp.int32, (tq, kend), 0)
        col = lax.broadcasted_iota(jnp.int32, (tq, kend), 1)
        s = jnp.where(row >= col, s, -1e30)
    m = lax.stop_gradient(jnp.max(s, axis=-1, keepdims=True))
    e = jnp.exp(s - m)
    p = e / jnp.sum(e, axis=-1, keepdims=True)
    return _mm(p, v)


def _attn_segments(s_len, tq):
    per = max(1, s_len // tq // ATTN_SEGMENTS)
    return [(first, per, (first + per) * tq) for first in range(0, s_len // tq, per)]


HEAD_PAIR = 2


def attn_fwd(q, k, v, *, tq, name):
    h, s_len, _ = q.shape
    outs = []
    for seg, (first, nq, kend) in enumerate(_attn_segments(s_len, tq)):
        def body(q_ref, k_ref, v_ref, o_ref, first=first, nq=nq):
            q0 = (first + pl.program_id(1)) * tq
            o = [_attn_block(q_ref[j], k_ref[j], v_ref[j], q0, nq == 1) for j in range(HEAD_PAIR)]
            o_ref[...] = jnp.concatenate(o, axis=-1).astype(o_ref.dtype)

        outs.append(pl.pallas_call(
            body, name=f"{name}_{seg}", grid=(h // HEAD_PAIR, nq),
            in_specs=[pl.BlockSpec((HEAD_PAIR, tq, DQK), lambda hp, i, first=first: (hp, first + i, 0)),
                      pl.BlockSpec((HEAD_PAIR, kend, DQK), lambda hp, i: (hp, 0, 0)),
                      pl.BlockSpec((HEAD_PAIR, kend, DV), lambda hp, i: (hp, 0, 0))],
            out_specs=pl.BlockSpec((tq, HEAD_PAIR * DV), lambda hp, i: (i, hp)),
            out_shape=jax.ShapeDtypeStruct((nq * tq, h * DV), bf16),
            compiler_params=_cparams(("parallel", "arbitrary")),
        )(q, k, v))
    return jnp.concatenate(outs, axis=0)


def attn_bwd(q, k, v, do, *, tq, name):
    h, s_len, _ = q.shape
    dqs, dk_acc, dv_acc = [], None, None
    for seg, (first, nq, kend) in reversed(list(enumerate(_attn_segments(s_len, tq)))):
        carried = dk_acc is not None

        def body(*refs, first=first, carried=carried, nq=nq):
            q_ref, k_ref, v_ref, do_ref = refs[:4]
            dq_ref, dk_ref, dv_ref = refs[-3:]
            i = pl.program_id(1)
            do_heads = _lane_split(do_ref[...], (DV,) * HEAD_PAIR)
            for j in range(HEAD_PAIR):
                _, vjp = jax.vjp(functools.partial(_attn_block, q0=(first + i) * tq, diagonal_last=nq == 1),
                                 q_ref[j], k_ref[j], v_ref[j])
                dq, dk, dv = vjp(do_heads[j])
                dq_ref[j] = dq

                @pl.when(i == 0)
                def _():
                    dk_ref[j] = dk + refs[4][j] if carried else dk
                    dv_ref[j] = dv + refs[5][j] if carried else dv

                @pl.when(i > 0)
                def _():
                    dk_ref[j] += dk
                    dv_ref[j] += dv

        key_specs = [pl.BlockSpec((HEAD_PAIR, kend, DQK), lambda hp, i: (hp, 0, 0)),
                     pl.BlockSpec((HEAD_PAIR, kend, DV), lambda hp, i: (hp, 0, 0))]
        dq, dk_acc, dv_acc = pl.pallas_call(
            body, name=f"{name}_{seg}", grid=(h // HEAD_PAIR, nq),
            in_specs=[pl.BlockSpec((HEAD_PAIR, tq, DQK), lambda hp, i, first=first: (hp, first + i, 0))] + key_specs
            + [pl.BlockSpec((tq, HEAD_PAIR * DV), lambda hp, i, first=first: (first + i, hp))]
            + (key_specs if carried else []),
            out_specs=[pl.BlockSpec((HEAD_PAIR, tq, DQK), lambda hp, i: (hp, i, 0))] + key_specs,
            out_shape=[jax.ShapeDtypeStruct((h, nq * tq, DQK), f32), jax.ShapeDtypeStruct((h, s_len, DQK), f32),
                       jax.ShapeDtypeStruct((h, s_len, DV), f32)],
            input_output_aliases={4: 1, 5: 2} if carried else {},
            compiler_params=_cparams(("parallel", "arbitrary")),
        )(q, k, v, do, *([dk_acc, dv_acc] if carried else []))
        dqs.append(dq)
    return jnp.concatenate(dqs[::-1], axis=1), dk_acc, dv_acc


def _wkv_local(r, ld, k, v, a, b):
    nb, c, n = r.shape
    ri = lax.broadcasted_iota(jnp.int32, (c, c), 0)
    ci = lax.broadcasted_iota(jnp.int32, (c, c), 1)
    tri = jnp.broadcast_to((ri >= ci).astype(f32)[None], (nb, c, c))
    cum = _exact_bl(tri, ld, False)
    tot = jnp.sum(ld, axis=1, keepdims=True)
    w_incl, w_excl, w_inv, w_rest = jnp.exp(cum), jnp.exp(cum - ld), jnp.exp(-cum), jnp.exp(tot - cum)
    ab, rb, bb, kb = a * w_excl, r * w_incl, b * w_inv, k * w_inv
    bw, kw = b * w_rest, k * w_rest
    r2 = lax.broadcasted_iota(jnp.int32, (2 * c, 2 * c), 0)
    c2 = lax.broadcasted_iota(jnp.int32, (2 * c, 2 * c), 1)
    t_of, s_of = jnp.where(r2 >= c, r2 - c, r2), jnp.where(c2 >= c, c2 - c, c2)
    keep = jnp.logical_or(t_of > s_of, jnp.logical_and(r2 >= c, t_of == s_of))
    pair = jnp.where(keep[None], _hbnt(jnp.concatenate([ab, rb], axis=1), jnp.concatenate([bb, kb], axis=1)), 0.0)
    on_b, on_k = _lane_split(pair, (c, c))
    l_ab, m_rb = _row_split(on_b, (c, c))
    l_ak_v, m_rk_v = _row_split(_hbnn(on_k, v), (c, c))
    x = jnp.concatenate([ab, l_ak_v], axis=-1)
    lp, span = l_ab, 1
    while span < c:
        x = x + _hbnn(lp, x)
        span *= 2
        if span < c:
            lp = _hbnn(lp, lp)
    via_b_r, via_b_y = _lane_split(_hbnn(m_rb, x), (n, n))
    r_hat = rb + via_b_r
    y0 = via_b_y + m_rk_v
    from_b_g, from_b_z = _row_split(_hbtn(x, bw), (n, n))
    eye = lax.broadcasted_iota(jnp.int32, (n, n), 0) == lax.broadcasted_iota(jnp.int32, (n, n), 1)
    g = jnp.where(eye[None], jnp.exp(tot), 0.0) + from_b_g
    z = from_b_z + _hbtn(v, kw)
    return r_hat, y0, g, z


def _head(h):
    return slice(RN * h, RN * (h + 1))


def _load_chunk_heads(ref, c, per):
    return jnp.stack([ref[c * q:c * (q + 1), _head(h)] for q in range(per) for h in range(RH)], axis=0)


def _store_chunk_heads(ref, val, c, per):
    for q in range(per):
        ref[c * q:c * (q + 1), :] = jnp.concatenate([val[q * RH + h] for h in range(RH)], axis=-1)


def wkv_fwd(r, ld, k, v, a, b, *, name):
    s_len = r.shape[0]
    c = WKV_CHUNK
    n = s_len // c
    per = min(WKV_CHUNKS_PER_STEP, n)
    rows = pl.BlockSpec((c * per, RW), lambda i: (i, 0))
    mats = pl.BlockSpec((per, RH, RN, RN), lambda i: (i, 0, 0, 0))
    rows_t, mats_t = jax.ShapeDtypeStruct((s_len, RW), f32), jax.ShapeDtypeStruct((n, RH, RN, RN), f32)

    def local_body(r_ref, ld_ref, k_ref, v_ref, a_ref, b_ref, rh_ref, y0_ref, g_ref, z_ref):
        r_hat, y0, g, z = _wkv_local(*[_load_chunk_heads(ref, c, per)
                                       for ref in (r_ref, ld_ref, k_ref, v_ref, a_ref, b_ref)])
        _store_chunk_heads(rh_ref, r_hat, c, per)
        _store_chunk_heads(y0_ref, y0, c, per)
        g_ref[...] = g.reshape(per, RH, RN, RN)
        z_ref[...] = z.reshape(per, RH, RN, RN)

    r_hat, y0, g, z = pl.pallas_call(
        local_body, name=name + "_local", grid=(n // per,), in_specs=[rows] * 6, out_specs=[rows, rows, mats, mats],
        out_shape=[rows_t, rows_t, mats_t, mats_t], compiler_params=_cparams(("parallel",)),
    )(r, ld, k, v, a, b)

    def scan_body(g_ref, z_ref, st_ref, s_sc):
        s_sc[...] = jnp.zeros_like(s_sc)

        @pl.loop(0, n)
        def _(i):
            s0 = s_sc[...]
            st_ref[i] = s0
            s_sc[...] = _hbnn(s0, g_ref[i]) + z_ref[i]

    vm = pl.BlockSpec(memory_space=pltpu.VMEM)
    states = pl.pallas_call(
        scan_body, name=name + "_scan", in_specs=[vm, vm], out_specs=vm, out_shape=mats_t,
        scratch_shapes=[pltpu.VMEM((RH, RN, RN), f32)],
        compiler_params=pltpu.CompilerParams(vmem_limit_bytes=VMEM_LIMIT),
    )(g, z)

    def out_body(rh_ref, y0_ref, st_ref, y_ref):
        y = _hbnt(_load_chunk_heads(rh_ref, c, per), st_ref[...].reshape(per * RH, RN, RN))
        _store_chunk_heads(y_ref, y, c, per)
        y_ref[...] += y0_ref[...]

    y = pl.pallas_call(
        out_body, name=name + "_out", grid=(n // per,), in_specs=[rows, rows, mats], out_specs=rows,
        out_shape=rows_t, compiler_params=_cparams(("parallel",)),
    )(r_hat, y0, states)
    return y, dict(r_hat=r_hat, g=g, states=states)


def wkv_bwd(r, ld, k, v, a, b, saved, dy, *, name):
    s_len = r.shape[0]
    c = WKV_CHUNK
    n = s_len // c
    per = min(WKV_CHUNKS_PER_STEP, n)
    rows = pl.BlockSpec((c * per, RW), lambda i: (i, 0))
    mats = pl.BlockSpec((per, RH, RN, RN), lambda i: (i, 0, 0, 0))
    rows_t, mats_t = jax.ShapeDtypeStruct((s_len, RW), f32), jax.ShapeDtypeStruct((n, RH, RN, RN), f32)

    def out_body(dy_ref, rh_ref, st_ref, drh_ref, dsy_ref):
        dyb = _load_chunk_heads(dy_ref, c, per)
        _store_chunk_heads(drh_ref, _hbnn(dyb, st_ref[...].reshape(per * RH, RN, RN)), c, per)
        dsy_ref[...] = _hbtn(dyb, _load_chunk_heads(rh_ref, c, per)).reshape(per, RH, RN, RN)

    d_rhat, ds_y = pl.pallas_call(
        out_body, name=name + "_out", grid=(n // per,), in_specs=[rows, rows, mats], out_specs=[rows, mats],
        out_shape=[rows_t, mats_t], compiler_params=_cparams(("parallel",)),
    )(dy, saved["r_hat"], saved["states"])

    def scan_body(dsy_ref, g_ref, st_ref, dg_ref, dz_ref, ds_sc):
        ds_sc[...] = jnp.zeros_like(ds_sc)

        @pl.loop(0, n)
        def _(i):
            cidx = n - 1 - i
            ds_next = ds_sc[...]
            dz_ref[cidx] = ds_next
            dg_ref[cidx] = _hbtn(st_ref[cidx], ds_next)
            ds_sc[...] = dsy_ref[cidx] + _hbnt(ds_next, g_ref[cidx])

    vm = pl.BlockSpec(memory_space=pltpu.VMEM)
    d_g, d_z = pl.pallas_call(
        scan_body, name=name + "_scan", in_specs=[vm, vm, vm], out_specs=[vm, vm], out_shape=[mats_t, mats_t],
        scratch_shapes=[pltpu.VMEM((RH, RN, RN), f32)],
        compiler_params=pltpu.CompilerParams(vmem_limit_bytes=VMEM_LIMIT),
    )(ds_y, saved["g"], saved["states"])

    def local_body(r_ref, ld_ref, k_ref, v_ref, a_ref, b_ref, drh_ref, dy_ref, dg_ref, dz_ref, *out_refs):
        _, vjp = jax.vjp(_wkv_local, *[_load_chunk_heads(ref, c, per)
                                       for ref in (r_ref, ld_ref, k_ref, v_ref, a_ref, b_ref)])
        grads = vjp((_load_chunk_heads(drh_ref, c, per), _load_chunk_heads(dy_ref, c, per),
                     dg_ref[...].reshape(per * RH, RN, RN), dz_ref[...].reshape(per * RH, RN, RN)))
        for o_ref, val in zip(out_refs, grads):
            _store_chunk_heads(o_ref, val, c, per)

    return pl.pallas_call(
        local_body, name=name + "_local", grid=(n // per,), in_specs=[rows] * 8 + [mats, mats], out_specs=[rows] * 6,
        out_shape=[rows_t] * 6, compiler_params=_cparams(("parallel",)),
    )(r, ld, k, v, a, b, d_rhat, dy, d_g, d_z)


def loss_head(y, target, *, tile, name):
    s_len, d = y.shape
    n = s_len // tile

    def body(y_ref, t_ref, dy_ref, l_ref):
        err = y_ref[...] - t_ref[...]
        dy_ref[...] = err * (1.0 / d)
        part = 0.5 * jnp.sum(jnp.mean(err * err, axis=-1, keepdims=True), axis=0, keepdims=True)

        @pl.when(pl.program_id(0) == 0)
        def _():
            l_ref[...] = jnp.zeros_like(l_ref)

        l_ref[...] += jnp.broadcast_to(part, l_ref.shape)

    bs = pl.BlockSpec((tile, d), lambda i: (i, 0))
    dy, l = pl.pallas_call(
        body, name=name, grid=(n,), in_specs=[bs, bs],
        out_specs=[bs, pl.BlockSpec((8, 128), lambda i: (0, 0))],
        out_shape=[jax.ShapeDtypeStruct((s_len, d), f32), jax.ShapeDtypeStruct((8, 128), f32)],
        compiler_params=_cparams(("arbitrary",)),
    )(y, target)
    return l[0, 0], dy


def _adamw_update(w, g, m, v):
    mn = B1 * m + (1.0 - B1) * g
    vn = B2 * v + (1.0 - B2) * (g * g)
    delta = -LR * ((mn / (1.0 - B1 ** STEP)) / (jnp.sqrt(vn / (1.0 - B2 ** STEP)) + EPS) + WD * w)
    return delta, mn, vn


def adamw(w, g, m, v, *, name):
    rows, cols = w.shape
    tile = _pick(rows, max(8, (2 * 1024 * 1024 // (4 * cols)) // 8 * 8), 8)

    def body(w_ref, g_ref, m_ref, v_ref, g_out, d_out, m_out, v_out):
        gv = g_ref[...]
        d_out[...], m_out[...], v_out[...] = _adamw_update(w_ref[...], gv, m_ref[...], v_ref[...])
        g_out[...] = gv

    bs = pl.BlockSpec((tile, cols), lambda i: (i, 0))
    return pl.pallas_call(
        body, name=name, grid=(rows // tile,), in_specs=[bs] * 4, out_specs=[bs] * 4,
        out_shape=[jax.ShapeDtypeStruct((rows, cols), f32)] * 4, compiler_params=_cparams(("parallel",)),
    )(w, g, m, v)


def adamw_many(ws, gs, ms, vs, *, name):
    n = len(ws)

    def body(*refs):
        ins, outs = refs[:4 * n], refs[4 * n:]
        for i in range(n):
            outs[i][...], outs[n + i][...], outs[2 * n + i][...] = _adamw_update(
                ins[i][...], ins[n + i][...], ins[2 * n + i][...], ins[3 * n + i][...])

    vm = pl.BlockSpec(memory_space=pltpu.VMEM)
    res = pl.pallas_call(
        body, name=name, in_specs=[vm] * (4 * n), out_specs=[vm] * (3 * n),
        out_shape=[jax.ShapeDtypeStruct(w.shape, f32) for w in ws] * 3,
        compiler_params=pltpu.CompilerParams(vmem_limit_bytes=VMEM_LIMIT),
    )(*ws, *gs, *ms, *vs)
    return list(res[:n]), list(res[n:2 * n]), list(res[2 * n:])


def _place():
    return lax.axis_index("x"), lax.axis_index("y"), lax.axis_index("c")


_ANY = pl.BlockSpec(memory_space=pl.ANY)


def _peer_chips(x, y):
    return [(1 - x, y), (x, 1 - y), (1 - x, 1 - y)]


def gather_weights(shards, *, name):
    nk = len(shards)

    def body(*refs):
        srcs, outs = refs[:nk], refs[nk:2 * nk]
        ici_send, ici_recv, d2d_send, d2d_recv = refs[2 * nk + 1:]
        x, y, c = _place()
        me = 2 * x + y
        peers = _peer_chips(x, y)
        pending = []
        for k in range(nk):
            half = srcs[k].shape[0] // 2
            mine = pl.ds(c * half, half)
            for p, (px, py) in enumerate(peers):
                cp = pltpu.make_async_remote_copy(
                    src_ref=srcs[k].at[mine], dst_ref=outs[k].at[me, mine], send_sem=ici_send.at[k, p],
                    recv_sem=ici_recv.at[k, p], device_id=(px, py, c), device_id_type=MESH)
                cp.start()
                pending.append(cp)
        for k in range(nk):
            half = srcs[k].shape[0] // 2
            mine = pl.ds(c * half, half)
            for p, (px, py) in enumerate(peers):
                landed = outs[k].at[2 * px + py, mine]
                pltpu.make_async_remote_copy(
                    src_ref=srcs[k].at[mine], dst_ref=landed, send_sem=ici_send.at[k, p], recv_sem=ici_recv.at[k, p],
                    device_id=(px, py, c), device_id_type=MESH).wait_recv()
                fwd = pltpu.make_async_remote_copy(
                    src_ref=landed, dst_ref=landed, send_sem=d2d_send.at[k, p], recv_sem=d2d_recv.at[k, p],
                    device_id=(x, y, 1 - c), device_id_type=MESH)
                fwd.start()
                pending.append(fwd)
        for k in range(nk):
            half = srcs[k].shape[0] // 2
            other = pl.ds((1 - c) * half, half)
            for p, (px, py) in enumerate(peers):
                theirs = outs[k].at[2 * px + py, other]
                pltpu.make_async_remote_copy(
                    src_ref=theirs, dst_ref=theirs, send_sem=d2d_send.at[k, p], recv_sem=d2d_recv.at[k, p],
                    device_id=(x, y, 1 - c), device_id_type=MESH).wait_recv()
        for cp in pending:
            cp.wait_send()
        refs[2 * nk][...] = jnp.zeros_like(refs[2 * nk])

    sem = lambda *shape: pltpu.SemaphoreType.DMA(shape)
    res = pl.pallas_call(
        body, name=name, in_specs=[_ANY] * nk, out_specs=[_ANY] * nk + [pl.BlockSpec(memory_space=pltpu.VMEM)],
        out_shape=[jax.ShapeDtypeStruct((4,) + s.shape, s.dtype) for s in shards]
        + [jax.ShapeDtypeStruct((8, 128), f32)],
        scratch_shapes=[sem(nk, 3), sem(nk, 3), sem(nk, 3), sem(nk, 3)],
    )(*shards)
    return list(res[:nk]), res[nk]


_HBM = pl.BlockSpec(memory_space=pltpu.HBM)
_SEM = pl.BlockSpec(memory_space=pltpu.SEMAPHORE)


def _ici_half_copies(srcs, lands, send_sems, recv_sems, incoming):
    x, y, c = _place()
    me = 2 * x + y
    out = []
    for k in range(len(srcs)):
        half = srcs[k].shape[0] // 2
        mine = pl.ds(c * half, half)
        for p, (px, py) in enumerate(_peer_chips(x, y)):
            out.append(pltpu.make_async_remote_copy(
                src_ref=srcs[k].at[mine], dst_ref=lands[k].at[(2 * px + py) if incoming else me, mine],
                send_sem=send_sems.at[3 * k + p], recv_sem=recv_sems.at[3 * k + p], device_id=(px, py, c),
                device_id_type=MESH))
    return out


def gather_start(shards, *, name):
    nk = len(shards)

    def body(*refs):
        srcs, lands = refs[:nk], refs[nk:2 * nk]
        send_sems, recv_sems = refs[2 * nk], refs[2 * nk + 1]
        token = refs[-1]
        for outgoing in _ici_half_copies(srcs, lands, send_sems, recv_sems, incoming=False):
            outgoing.start()
        token[...] = jnp.zeros_like(token)

    lands = [pltpu.with_memory_space_constraint(lax.empty((4,) + s.shape, s.dtype), pltpu.HBM) for s in shards]
    res = pl.pallas_call(
        body, name=name,
        out_shape=(pltpu.SemaphoreType.DMA((3 * nk,)), pltpu.SemaphoreType.DMA((3 * nk,)),
                   *[pltpu.HBM(s.shape, s.dtype) for s in shards], *[pltpu.HBM(z.shape, z.dtype) for z in lands],
                   jax.ShapeDtypeStruct((8, 128), f32)),
        in_specs=[_HBM] * (2 * nk), out_specs=(_SEM, _SEM, *[_HBM] * (2 * nk), pl.BlockSpec(memory_space=pltpu.VMEM)),
        input_output_aliases={k: 2 + k for k in range(2 * nk)},
        compiler_params=pltpu.CompilerParams(has_side_effects=pltpu.SideEffectType.DATAFLOW_SIDE_EFFECTING),
    )(*[pltpu.with_memory_space_constraint(s, pltpu.HBM) for s in shards], *lands)
    return res[0], res[1], list(res[2:2 + nk]), list(res[2 + nk:2 + 2 * nk]), res[-1]


def gather_wait(send_sems, recv_sems, shards, lands, after, *, name):
    nk = len(shards)

    def body(*refs):
        srcs, zones = refs[:nk], refs[nk:2 * nk]
        for outgoing in _ici_half_copies(srcs, zones, refs[2 * nk], refs[2 * nk + 1], incoming=False):
            outgoing.wait_send()
        for landed in _ici_half_copies(srcs, zones, refs[2 * nk], refs[2 * nk + 1], incoming=True):
            landed.wait_recv()

    res = pl.pallas_call(
        body, name=name,
        out_shape=(*[pltpu.HBM(s.shape, s.dtype) for s in shards], *[pltpu.HBM(z.shape, z.dtype) for z in lands]),
        in_specs=[_HBM] * (2 * nk) + [_SEM, _SEM, _ANY], out_specs=tuple([_HBM] * (2 * nk)),
        input_output_aliases={k: k for k in range(2 * nk)},
        compiler_params=pltpu.CompilerParams(has_side_effects=pltpu.SideEffectType.DATAFLOW_SIDE_EFFECTING),
    )(*shards, *lands, send_sems, recv_sems, after)
    return list(res[:nk]), list(res[nk:])


def gather_forward(lands, *, name):
    nk = len(lands)

    def body(*refs):
        zones = refs[nk:2 * nk]
        send_sems, recv_sems = refs[2 * nk:]
        x, y, c = _place()
        sends = []
        for k in range(nk):
            half = zones[k].shape[1] // 2
            for p, (px, py) in enumerate(_peer_chips(x, y)):
                landed = zones[k].at[2 * px + py, pl.ds(c * half, half)]
                cp = pltpu.make_async_remote_copy(
                    src_ref=landed, dst_ref=landed, send_sem=send_sems.at[k, p], recv_sem=recv_sems.at[k, p],
                    device_id=(x, y, 1 - c), device_id_type=MESH)
                cp.start()
                sends.append(cp)
        for k in range(nk):
            half = zones[k].shape[1] // 2
            for p, (px, py) in enumerate(_peer_chips(x, y)):
                theirs = zones[k].at[2 * px + py, pl.ds((1 - c) * half, half)]
                pltpu.make_async_remote_copy(
                    src_ref=theirs, dst_ref=theirs, send_sem=send_sems.at[k, p], recv_sem=recv_sems.at[k, p],
                    device_id=(x, y, 1 - c), device_id_type=MESH).wait_recv()
        for cp in sends:
            cp.wait_send()

    return pl.pallas_call(
        body, name=name, in_specs=[_ANY] * nk, out_specs=[_ANY] * nk,
        out_shape=[jax.ShapeDtypeStruct(z.shape, z.dtype) for z in lands],
        input_output_aliases={k: k for k in range(nk)},
        scratch_shapes=[pltpu.SemaphoreType.DMA((nk, 3)), pltpu.SemaphoreType.DMA((nk, 3))],
    )(*lands)


def grads_to_sibling(parts, *, name):
    nk = len(parts)

    def body(*refs):
        srcs, outs = refs[:nk], refs[nk:2 * nk]
        send_sems, recv_sems = refs[2 * nk:]
        x, y, c = _place()
        sends = []
        for k in range(nk):
            half = srcs[k].shape[1] // 2
            cp = pltpu.make_async_remote_copy(
                src_ref=srcs[k].at[:, pl.ds((1 - c) * half, half), :], dst_ref=outs[k], send_sem=send_sems.at[k],
                recv_sem=recv_sems.at[k], device_id=(x, y, 1 - c), device_id_type=MESH)
            cp.start()
            sends.append(cp)
        for cp in sends:
            cp.wait_recv()
        for cp in sends:
            cp.wait_send()

    return pl.pallas_call(
        body, name=name, in_specs=[_ANY] * nk, out_specs=[_ANY] * nk,
        out_shape=[jax.ShapeDtypeStruct((4, p.shape[1] // 2, p.shape[2]), p.dtype) for p in parts],
        scratch_shapes=[pltpu.SemaphoreType.DMA((nk,)), pltpu.SemaphoreType.DMA((nk,))],
    )(*parts)


def pair_sum(part, theirs, core, *, name):
    _, rows, cols = part.shape
    half = rows // 2
    tile = _pick(half, max(16, (1 << 20) // (4 * cols) // 16 * 16), 16)
    per = half // tile

    def body(c_ref, p_ref, t_ref, o_ref):
        o_ref[...] = (p_ref[...].astype(f32) + t_ref[...].astype(f32)).astype(bf16)

    grid_spec = pltpu.PrefetchScalarGridSpec(
        num_scalar_prefetch=1, grid=(4, per),
        in_specs=[pl.BlockSpec((1, tile, cols), lambda j, i, c_ref: (j, c_ref[0] * per + i, 0)),
                  pl.BlockSpec((1, tile, cols), lambda j, i, c_ref: (j, i, 0))],
        out_specs=pl.BlockSpec((1, tile, cols), lambda j, i, c_ref: (j, i, 0)))
    return pl.pallas_call(
        body, name=name, grid_spec=grid_spec, out_shape=jax.ShapeDtypeStruct((4, half, cols), bf16),
        compiler_params=_cparams(("parallel", "parallel")),
    )(core, part, theirs)


def _all_to_all_copies(srcs, lands, send_sems, recv_sems, incoming):
    x, y, c = _place()
    me = 2 * x + y
    out = []
    for k in range(len(srcs)):
        for p, (px, py) in enumerate(_peer_chips(x, y)):
            peer = 2 * px + py
            out.append(pltpu.make_async_remote_copy(
                src_ref=srcs[k].at[peer], dst_ref=lands[k].at[peer if incoming else me],
                send_sem=send_sems.at[3 * k + p], recv_sem=recv_sems.at[3 * k + p], device_id=(px, py, c),
                device_id_type=MESH))
    return out


def scatter_start(parts, *, name):
    nk = len(parts)

    def body(*refs):
        srcs, lands = refs[:nk], refs[nk:2 * nk]
        for outgoing in _all_to_all_copies(srcs, lands, refs[2 * nk], refs[2 * nk + 1], incoming=False):
            outgoing.start()
        refs[-1][...] = jnp.zeros_like(refs[-1])

    lands = [pltpu.with_memory_space_constraint(lax.empty(p.shape, p.dtype), pltpu.HBM) for p in parts]
    res = pl.pallas_call(
        body, name=name,
        out_shape=(pltpu.SemaphoreType.DMA((3 * nk,)), pltpu.SemaphoreType.DMA((3 * nk,)),
                   *[pltpu.HBM(p.shape, p.dtype) for p in parts], *[pltpu.HBM(p.shape, p.dtype) for p in parts],
                   jax.ShapeDtypeStruct((8, 128), f32)),
        in_specs=[_HBM] * (2 * nk), out_specs=(_SEM, _SEM, *[_HBM] * (2 * nk), pl.BlockSpec(memory_space=pltpu.VMEM)),
        input_output_aliases={k: 2 + k for k in range(2 * nk)},
        compiler_params=pltpu.CompilerParams(has_side_effects=pltpu.SideEffectType.DATAFLOW_SIDE_EFFECTING),
    )(*[pltpu.with_memory_space_constraint(p, pltpu.HBM) for p in parts], *lands)
    return res[0], res[1], list(res[2:2 + nk]), list(res[2 + nk:2 + 2 * nk]), res[-1]


def scatter_wait(send_sems, recv_sems, parts, lands, after, *, name):
    nk = len(parts)

    def body(*refs):
        srcs, zones = refs[:nk], refs[nk:2 * nk]
        for outgoing in _all_to_all_copies(srcs, zones, refs[2 * nk], refs[2 * nk + 1], incoming=False):
            outgoing.wait_send()
        for landed in _all_to_all_copies(srcs, zones, refs[2 * nk], refs[2 * nk + 1], incoming=True):
            landed.wait_recv()

    res = pl.pallas_call(
        body, name=name,
        out_shape=(*[pltpu.HBM(p.shape, p.dtype) for p in parts], *[pltpu.HBM(z.shape, z.dtype) for z in lands]),
        in_specs=[_HBM] * (2 * nk) + [_SEM, _SEM, _ANY], out_specs=tuple([_HBM] * (2 * nk)),
        input_output_aliases={k: k for k in range(2 * nk)},
        compiler_params=pltpu.CompilerParams(has_side_effects=pltpu.SideEffectType.DATAFLOW_SIDE_EFFECTING),
    )(*parts, *lands, send_sems, recv_sems, after)
    return list(res[:nk]), list(res[nk:])


def join_halves(bufs, layout, *, name):
    nk, nb = len(layout), len(bufs)

    def body(*refs):
        outs = refs[nb:2 * nb]
        send_sems, recv_sems = refs[2 * nb:]
        x, y, c = _place()
        pending = []
        for k, (o, off, rows) in enumerate(layout):
            half = rows // 2
            mine = outs[o].at[pl.ds(off + c * half, half), :]
            cp = pltpu.make_async_remote_copy(
                src_ref=mine, dst_ref=mine, send_sem=send_sems.at[k], recv_sem=recv_sems.at[k],
                device_id=(x, y, 1 - c), device_id_type=MESH)
            cp.start()
            pending.append(cp)
        for k, (o, off, rows) in enumerate(layout):
            half = rows // 2
            theirs = outs[o].at[pl.ds(off + (1 - c) * half, half), :]
            pltpu.make_async_remote_copy(
                src_ref=theirs, dst_ref=theirs, send_sem=send_sems.at[k], recv_sem=recv_sems.at[k],
                device_id=(x, y, 1 - c), device_id_type=MESH).wait_recv()
        for cp in pending:
            cp.wait_send()

    return pl.pallas_call(
        body, name=name, in_specs=[_ANY] * nb, out_specs=[_ANY] * nb,
        out_shape=[jax.ShapeDtypeStruct(b.shape, b.dtype) for b in bufs],
        input_output_aliases={o: o for o in range(nb)},
        scratch_shapes=[pltpu.SemaphoreType.DMA((nk,)), pltpu.SemaphoreType.DMA((nk,))],
    )(*bufs)


def place_slab(dest, src, index, *, name):
    rows, cols = src.shape
    tile = _pick(rows, max(16, (1 << 20) // (src.dtype.itemsize * cols) // 16 * 16), 16)

    def body(i_ref, s_ref, d_ref, o_ref):
        del i_ref, d_ref
        o_ref[0] = s_ref[...]

    grid_spec = pltpu.PrefetchScalarGridSpec(
        num_scalar_prefetch=1, grid=(rows // tile,),
        in_specs=[pl.BlockSpec((tile, cols), lambda i, idx: (i, 0)), _ANY],
        out_specs=pl.BlockSpec((1, tile, cols), lambda i, idx: (idx[0], i, 0)))
    return pl.pallas_call(
        body, name=name, grid_spec=grid_spec, out_shape=jax.ShapeDtypeStruct(dest.shape, dest.dtype),
        input_output_aliases={2: 0}, compiler_params=_cparams(("parallel",)),
    )(index, src, dest)


def _broadcast_copies(src, land, send_sems, recv_sems, incoming):
    x, y, c = _place()
    me = 4 * x + 2 * y + c
    out = []
    for m in range(1, 8):
        px, py, pc = x ^ (m >> 2), y ^ ((m >> 1) & 1), c ^ (m & 1)
        out.append(pltpu.make_async_remote_copy(
            src_ref=src, dst_ref=land.at[(4 * px + 2 * py + pc) if incoming else me], send_sem=send_sems.at[m - 1],
            recv_sem=recv_sems.at[m - 1], device_id=(px, py, pc), device_id_type=MESH))
    return out


def broadcast_start(src, *, name):
    def body(s_ref, l_ref, send_sems, recv_sems, s_thru, l_thru, token):
        for outgoing in _broadcast_copies(s_ref, l_ref, send_sems, recv_sems, incoming=False):
            outgoing.start()
        token[...] = jnp.zeros_like(token)

    land = pltpu.with_memory_space_constraint(lax.empty((8,) + src.shape, src.dtype), pltpu.HBM)
    return pl.pallas_call(
        body, name=name,
        out_shape=(pltpu.SemaphoreType.DMA((7,)), pltpu.SemaphoreType.DMA((7,)), pltpu.HBM(src.shape, src.dtype),
                   pltpu.HBM(land.shape, land.dtype), jax.ShapeDtypeStruct((8, 128), f32)),
        in_specs=[_HBM, _HBM], out_specs=(_SEM, _SEM, _HBM, _HBM, pl.BlockSpec(memory_space=pltpu.VMEM)),
        input_output_aliases={0: 2, 1: 3},
        compiler_params=pltpu.CompilerParams(has_side_effects=pltpu.SideEffectType.DATAFLOW_SIDE_EFFECTING),
    )(pltpu.with_memory_space_constraint(src, pltpu.HBM), land)


def broadcast_wait(send_sems, recv_sems, src, land, after, *, name):
    def body(s_ref, l_ref, send_sems, recv_sems, after_ref, s_out, l_out):
        for outgoing in _broadcast_copies(s_ref, l_ref, send_sems, recv_sems, incoming=False):
            outgoing.wait_send()
        for landed in _broadcast_copies(s_ref, l_ref, send_sems, recv_sems, incoming=True):
            landed.wait_recv()

    return pl.pallas_call(
        body, name=name, out_shape=(pltpu.HBM(src.shape, src.dtype), pltpu.HBM(land.shape, land.dtype)),
        in_specs=[_HBM, _HBM, _SEM, _SEM, _ANY], out_specs=(_HBM, _HBM), input_output_aliases={0: 0, 1: 1},
        compiler_params=pltpu.CompilerParams(has_side_effects=pltpu.SideEffectType.DATAFLOW_SIDE_EFFECTING),
    )(src, land, send_sems, recv_sems, after)


def sum8(land, own, device, *, name):
    _, rows, cols = land.shape

    def body(d_ref, l_ref, o_ref, out_ref):
        mine = o_ref[...]
        acc = jnp.where(d_ref[0] == 0, mine, l_ref[0])
        for d in range(1, 8):
            acc = acc + jnp.where(d_ref[0] == d, mine, l_ref[d])
        out_ref[...] = acc

    grid_spec = pltpu.PrefetchScalarGridSpec(
        num_scalar_prefetch=1, grid=(1,),
        in_specs=[pl.BlockSpec((8, rows, cols), lambda i, d_ref: (0, 0, 0)),
                  pl.BlockSpec((rows, cols), lambda i, d_ref: (0, 0))],
        out_specs=pl.BlockSpec((rows, cols), lambda i, d_ref: (0, 0)))
    return pl.pallas_call(
        body, name=name, grid_spec=grid_spec, out_shape=jax.ShapeDtypeStruct((rows, cols), f32),
        compiler_params=_cparams(("arbitrary",)),
    )(device, land, own)


def sum4_into(arrived, own, dest, where, *, layer, total_rows, name):
    _, rows, cols = arrived.shape
    tile = _pick(rows, max(16, (1 << 20) // (4 * cols) // 16 * 16), 16)
    per = rows // tile

    def body(w_ref, a_ref, own_ref, *rest):
        mine = own_ref[0].astype(f32)
        p = [jnp.where(w_ref[1] == j, mine, a_ref[j].astype(f32)) for j in range(4)]
        rest[-1][...] = ((p[0] + p[1]) + p[2]) + p[3]

    grid_spec = pltpu.PrefetchScalarGridSpec(
        num_scalar_prefetch=1, grid=(per,),
        in_specs=[pl.BlockSpec((4, tile, cols), lambda i, w_ref: (0, i, 0)),
                  pl.BlockSpec((1, tile, cols), lambda i, w_ref: (w_ref[1], i, 0))] + ([] if dest is None else [_ANY]),
        out_specs=pl.BlockSpec((tile, cols), lambda i, w_ref: ((2 * layer + w_ref[0]) * per + i, 0)))
    return pl.pallas_call(
        body, name=name, grid_spec=grid_spec, out_shape=jax.ShapeDtypeStruct((total_rows, cols), f32),
        input_output_aliases={} if dest is None else {3: 0}, compiler_params=_cparams(("parallel",)),
    )(where, arrived, own, *([] if dest is None else [dest]))


def _consts():
    idx = np.arange(RW)
    bd = (idx[:, None] // RN == idx[None, :] // RN).astype(np.float32)
    rot = np.zeros((DQK, DQK), np.float32)
    half = ROPE // 2
    rot[NOPE + half + np.arange(half), NOPE + np.arange(half)] = -1.0
    rot[NOPE + np.arange(half), NOPE + half + np.arange(half)] = 1.0
    return jnp.asarray(bd), jnp.asarray(rot)


def _rope_tables(positions):
    freqs = ROPE_THETA ** (-(jnp.arange(ROPE // 2, dtype=f32) * 2.0 / ROPE))
    ang = positions.astype(f32)[:, None] * freqs
    cos, sin = jnp.cos(ang), jnp.sin(ang)
    ones = jnp.ones((positions.shape[0], NOPE), f32)
    return (jnp.concatenate([ones, cos, cos], axis=-1), jnp.concatenate([0.0 * ones, sin, sin], axis=-1))


STAGES = (("w_in",), ("mla_wq_b", "mla_wkv_b", "mla_w_o", "rwkv_w_o", "conv_w_o", "w_out"), ("w_up", "w_down"))


def derive_stage(stage, w):
    if stage == 0:
        w_in = w["w_in"]
        pad = jnp.zeros((D, MLA_PAD - MLA_COLS), w_in.dtype)
        return dict(gate=w_in[:, :GATE], mla=jnp.concatenate([w_in[:, GATE:GATE + MLA_COLS], pad], axis=1),
                    rw=w_in[:, GATE + MLA_COLS:GATE + MLA_COLS + 4 * RW], cv=w_in[:, GATE + MLA_COLS + 4 * RW:])
    if stage == 1:
        wq = jnp.pad(w["mla_wq_b"].reshape(QL, MLA_H, DQK), ((0, 0), (0, 0), (0, Q_HEAD_PAD - DQK)))
        return dict(wq=wq.reshape(QL, MLA_H * Q_HEAD_PAD), wkv=w["mla_wkv_b"], wo=w["mla_w_o"], rwo=w["rwkv_w_o"],
                    cvo=w["conv_w_o"], out=w["w_out"])
    return dict(up=w["w_up"], down=w["w_down"])


W_IN_WINDOW_TILE = (0, 10, 21, 31)
W_IN_WINDOW = 1664
W_IN_SHARD = 1384


def w_in_window_cols(win, chip):
    gap = MLA_PAD - MLA_COLS
    branches = []
    for j in range(4):
        lo, hi = W_IN_SHARD * j, W_IN_SHARD * (j + 1)
        base = 128 * W_IN_WINDOW_TILE[j]
        cut = GATE + MLA_COLS
        if hi <= cut:
            branches.append(lambda w, a=lo - base: w[:, a:a + W_IN_SHARD])
        elif lo >= cut:
            branches.append(lambda w, a=lo + gap - base: w[:, a:a + W_IN_SHARD])
        else:
            branches.append(lambda w, a=lo - base, n1=cut - lo, b=cut + gap - base, n2=hi - cut:
                            jnp.concatenate([w[:, a:a + n1], w[:, b:b + n2]], axis=1))
    return lax.switch(chip, branches, win)


def chip_major_grads(stage, g):
    if stage == 0:
        padded = jnp.concatenate([g["gate"], g["mla"], g["rw"], g["cv"]], axis=1)
        return dict(w_in=jnp.stack([padded[:, 128 * t:128 * t + W_IN_WINDOW] for t in W_IN_WINDOW_TILE]))
    if stage == 1:
        wq = g["wq"].reshape(QL, MLA_H, Q_HEAD_PAD)[:, :, :DQK].reshape(QL, 4, -1).transpose(1, 0, 2)
        return dict(mla_wq_b=wq, mla_wkv_b=g["wkv"], mla_w_o=g["wo"], rwkv_w_o=g["rwo"], conv_w_o=g["cvo"],
                    w_out=g["out"].reshape(4, D // 4, D))
    return dict(w_up=g["up"], w_down=g["down"].reshape(4, DFF // 4, D))


def _row(v):
    return v.reshape(1, -1)


def local_step(x, positions, target, w, sm, big_of=None, on_grads=None, on_small=None):
    if big_of is None:
        big_of = lambda l, stage, _after: {n: w[n][l] for n in STAGES[stage]}
    if on_grads is None:
        on_grads = lambda l, stage, slabs: None
    if on_small is None:
        on_small = lambda l, layer_small: None
    s_len = x.shape[0]
    t_row = _pick(s_len, 256, 8)
    t_wide = _pick(s_len, 128, 8)
    bd, rot = _consts()
    cos, sin = _rope_tables(positions)
    sds = lambda *shape: jax.ShapeDtypeStruct(shape, f32)
    sdb = lambda *shape: jax.ShapeDtypeStruct(shape, bf16)
    saved = []
    v_first = None
    for l in range(DEPTH):
        tag = f"l{l}_"
        lw = derive_stage(0, big_of(l, 0, x))
        vres = l > 0
        p_norm1 = [_row(sm["attn_norm"][l])]
        (h,) = rows_fwd(_fn_norm, [x], p_norm1, [], [sds(s_len, D)], tile=t_row, name=tag + "norm1")
        gate = mm(h, lw["gate"], name=tag + "proj_gate")
        mla = mm(h, lw["mla"], name=tag + "proj_mla")
        rwc = mm(h, lw["rw"], name=tag + "proj_rwkv")
        cvc = mm(h, lw["cv"], name=tag + "proj_conv")
        lw.update(derive_stage(1, big_of(l, 1, cvc)))
        p_mla = [_row(sm["mla_q_a_norm"][l]), _row(sm["mla_kv_a_norm"][l])]
        qn, kvn, kpe = rows_fwd(_fn_mla_prep, [mla], p_mla, [], [sdb(s_len, QL), sdb(s_len, KVL), sds(s_len, 128)],
                                tile=t_row, name=tag + "mla_prep")
        q_flat = mm(qn, lw["wq"], name=tag + "q_proj")
        kv_flat = mm(kvn, lw["wkv"], name=tag + "kv_proj")
        p_qk = [_row(sm["mla_q_norm"][l]), _row(sm["mla_k_norm"][l])]
        q, k, vv = rows_fwd(_fn_qk_post, [q_flat, kv_flat, kpe, cos, sin], p_qk, [rot],
                            [sds(MLA_H, s_len, DQK), sds(MLA_H, s_len, DQK), sds(MLA_H, s_len, DV)], tile=t_wide,
                            name=tag + "qk_post")
        o = attn_fwd(q, k, vv, tq=_pick(s_len, 256, 8), name=tag + "attn")
        o_a = mm(o, lw["wo"], name=tag + "o_a")
        p_rw = [_row(sm["rwkv_mu"][l]), _row(sm["rwkv_w0"][l]), w["rwkv_w2"][l], _row(sm["rwkv_a0"][l]),
                w["rwkv_a2"][l], w["rwkv_g2"][l], _row(sm["rwkv_k_k"][l]), _row(sm["rwkv_k_a"][l])]
        rw_rows, rw_halos = [rwc], (0,)
        if vres:
            p_rw += [w["rwkv_v1"][l - 1], _row(sm["rwkv_v_mu"][l - 1]), _row(sm["rwkv_v0"][l - 1]), w["rwkv_v2"][l - 1]]
            rw_rows, rw_halos = [rwc, h, v_first], (0, 1)
        fn_prep = _make_fn_rwkv_prep(vres)
        r, ld, k2, v, an, bn, g = rows_fwd(fn_prep, rw_rows, p_rw, [bd], [sds(s_len, RW)] * 7, tile=t_row,
                                           name=tag + "rwkv_prep", halos=rw_halos)
        if not vres:
            v_first = v
        y, states = wkv_fwd(r, ld, k2, v, an, bn, name=tag + "wkv")
        p_post = [_row(sm["rwkv_ln_w"][l]), _row(sm["rwkv_ln_b"][l]), _row(sm["rwkv_r_k"][l])]
        (yb,) = rows_fwd(_fn_rwkv_post, [y, r, k2, v, g], p_post, [bd], [sdb(s_len, RW)], tile=t_row,
                         name=tag + "rwkv_post")
        o_b = mm(yb, lw["rwo"], name=tag + "o_b")
        p_cv = [w["conv_w"][l][q:q + 1] for q in range(3)]
        (yc,) = rows_fwd(_fn_conv, [cvc], p_cv, [], [sdb(s_len, CW)], tile=t_row, name=tag + "conv", halos=(0,))
        o_c = mm(yc, lw["cvo"], name=tag + "o_c")
        (merged,) = rows_fwd(_fn_merge, [gate, o_a, o_b, o_c], [], [], [sdb(s_len, D)], tile=t_wide,
                             name=tag + "merge")
        x1 = mm(merged, lw["out"], add=x, name=tag + "out_proj")
        lw.update(derive_stage(2, big_of(l, 2, x1)))
        p_norm2 = [_row(sm["mlp_norm"][l])]
        (h2,) = rows_fwd(_fn_norm, [x1], p_norm2, [], [sdb(s_len, D)], tile=t_row, name=tag + "norm2")
        up, act = mm(h2, lw["up"], relu2_out=True, name=tag + "up")
        x2 = mm(act, lw["down"], add=x1, name=tag + "down")
        saved.append(dict(lw=lw, x=x, h=h, gate=gate, mla=mla, rwc=rwc, cvc=cvc, qn=qn, kvn=kvn, kpe=kpe,
                          q_flat=q_flat, kv_flat=kv_flat, vv=vv, q=q, k=k, o=o, o_a=o_a, r=r, ld=ld, k2=k2, v=v,
                          an=an, bn=bn, g=g, y=y, states=states, yb=yb, o_b=o_b, yc=yc, o_c=o_c, merged=merged,
                          x1=x1, h2=h2, up=up, act=act, p_norm1=p_norm1, p_mla=p_mla, p_qk=p_qk, p_rw=p_rw,
                          p_post=p_post, p_cv=p_cv, p_norm2=p_norm2, rw_rows=rw_rows, rw_halos=rw_halos,
                          fn_prep=fn_prep, v_first=v_first if vres else None))
        x = x2

    loss, dx = loss_head(x, target, tile=t_row, name="loss_head")

    grads = {n: [None] * (DEPTH - 1 if n in ("rwkv_v1", "rwkv_v_mu", "rwkv_v0", "rwkv_v2") else DEPTH)
             for n in WEIGHTS}
    dv_first = None
    for l in reversed(range(DEPTH)):
        tag = f"b{l}_"
        sv = saved[l]
        lw = sv["lw"]
        vres = l > 0
        g_down = mm(sv["act"], dx, ta=True, out_dtype=bf16, name=tag + "g_down")
        dup = mm(dx, lw["down"], tb=True, act_grad=sv["up"], out_dtype=bf16, name=tag + "d_up")
        g_up = mm(sv["h2"], dup, ta=True, n_split=4, out_dtype=bf16, name=tag + "g_up")
        dh2 = mm(dup, lw["up"], tb=True, name=tag + "d_h2")
        slabs = chip_major_grads(2, dict(up=g_up, down=g_down))
        token = on_grads(l, 2, slabs)
        p_norm2 = sv["p_norm2"] if token is None else [sv["p_norm2"][0] + token[0, 0]]
        (dx1,), (g_n2,) = rows_bwd(_fn_norm, [sv["x1"]], p_norm2, [], [[dh2]], tile=t_row,
                                   name=tag + "norm2", extra={0: [dx]})
        g_out = mm(sv["merged"], dx1, ta=True, out_dtype=bf16, name=tag + "g_out")
        dmerged = mm(dx1, lw["out"], tb=True, name=tag + "d_merged")
        (dgate, do_a, do_b, do_c), _ = rows_bwd(_fn_merge, [sv["gate"], sv["o_a"], sv["o_b"], sv["o_c"]], [], [],
                                                [[dmerged]], tile=t_wide, name=tag + "merge",
                                                grad_dtypes=[bf16] * 4)
        g_cvo = mm(sv["yc"], do_c, ta=True, n_split=4, out_dtype=bf16, name=tag + "g_cvo")
        dyc = mm(do_c, lw["cvo"], tb=True, name=tag + "d_yc")
        (dcvc,), g_cw = rows_bwd(_fn_conv, [sv["cvc"]], sv["p_cv"], [], [[dyc]], tile=t_row, name=tag + "conv",
                                    halos=(0,))
        g_rwo = mm(sv["yb"], do_b, ta=True, n_split=4, out_dtype=bf16, name=tag + "g_rwo")
        dyb = mm(do_b, lw["rwo"], tb=True, name=tag + "d_yb")
        (dy, dr_p, dk_p, dv_p, dg), g_post = rows_bwd(
            _fn_rwkv_post, [sv["y"], sv["r"], sv["k2"], sv["v"], sv["g"]], sv["p_post"], [bd], [[dyb]], tile=t_row,
            name=tag + "rwkv_post")
        dr_s, dld, dk_s, dv_s, dan, dbn = wkv_bwd(sv["r"], sv["ld"], sv["k2"], sv["v"], sv["an"], sv["bn"],
                                                  sv["states"], dy, name=tag + "wkv")
        dv_list = [dv_s, dv_p] + ([dv_first] if (not vres and dv_first is not None) else [])
        d_prep, g_prep = rows_bwd(
            sv["fn_prep"], sv["rw_rows"], sv["p_rw"], [bd],
            [[dr_s, dr_p], [dld], [dk_s, dk_p], dv_list, [dan], [dbn], [dg]], tile=t_row, name=tag + "rwkv_prep",
            halos=sv["rw_halos"])
        drwc = d_prep[0]
        dh_extra = []
        if vres:
            dh_extra = [d_prep[1]]
            dv_first = d_prep[2]
        g_wo = mm(sv["o"], do_a, ta=True, n_split=4, out_dtype=bf16, name=tag + "g_wo")
        do = mm(do_a, lw["wo"], tb=True, name=tag + "d_o")
        dq, dk, dvv = attn_bwd(sv["q"], sv["k"], sv["vv"], do, tq=_pick(s_len, 256, 8), name=tag + "attn")
        (dq_flat, dkv_flat, dkpe), g_qk = rows_bwd(
            _fn_qk_post, [sv["q_flat"], sv["kv_flat"], sv["kpe"], cos, sin], sv["p_qk"], [rot], [[dq], [dk], [dvv]],
            tile=t_wide, name=tag + "qk_post", grad_rows=[0, 1, 2], grad_dtypes=[bf16, bf16, f32])
        g_wq = mm(sv["qn"], dq_flat, ta=True, out_dtype=bf16, name=tag + "g_wq")
        g_wkv = mm(sv["kvn"], dkv_flat, ta=True, n_split=4, out_dtype=bf16, name=tag + "g_wkv")
        dqn = mm(dq_flat, lw["wq"], tb=True, name=tag + "d_qn")
        dkvn = mm(dkv_flat, lw["wkv"], tb=True, name=tag + "d_kvn")
        slabs.update(chip_major_grads(1, dict(wq=g_wq, wkv=g_wkv, wo=g_wo, rwo=g_rwo, cvo=g_cvo, out=g_out)))
        token = on_grads(l, 1, {n: slabs[n] for n in STAGES[1]})
        p_mla = sv["p_mla"] if token is None else [sv["p_mla"][0] + token[0, 0], sv["p_mla"][1]]
        (dmla,), g_mla = rows_bwd(_fn_mla_prep, [sv["mla"]], p_mla, [], [[dqn], [dkvn], [dkpe]], tile=t_row,
                                  name=tag + "mla_prep", grad_dtypes=[bf16])
        g_gate = mm(sv["h"], dgate, ta=True, out_dtype=bf16, name=tag + "g_gate")
        g_mlaw = mm(sv["h"], dmla, ta=True, out_dtype=bf16, name=tag + "g_mla")
        g_rw = mm(sv["h"], drwc, ta=True, out_dtype=bf16, name=tag + "g_rw")
        g_cv = mm(sv["h"], dcvc, ta=True, out_dtype=bf16, name=tag + "g_cv")
        dh = mm(dgate, lw["gate"], tb=True, name=tag + "d_h_gate")
        dh = mm(dmla, lw["mla"], tb=True, add=dh, name=tag + "d_h_mla")
        dh = mm(drwc, lw["rw"], tb=True, add=dh, name=tag + "d_h_rw")
        dh = mm(dcvc, lw["cv"], tb=True, add=dh, name=tag + "d_h_cv")
        (dx,), (g_n1,) = rows_bwd(_fn_norm, [sv["x"]], sv["p_norm1"], [], [[dh] + dh_extra], tile=t_row,
                                  name=tag + "norm1", extra={0: [dx1]})
        slabs.update(chip_major_grads(0, dict(gate=g_gate, mla=g_mlaw, rw=g_rw, cv=g_cv)))
        token = on_grads(l, 0, {n: slabs[n] for n in STAGES[0]})
        if token is not None and l > 0:
            dx = dx + token[0, 0]
        for n, val in slabs.items():
            grads[n][l] = val
        layer_small = [("attn_norm", l, g_n1), ("mlp_norm", l, g_n2), ("mla_q_a_norm", l, g_mla[0]),
                       ("mla_kv_a_norm", l, g_mla[1]), ("mla_q_norm", l, g_qk[0]), ("mla_k_norm", l, g_qk[1]),
                       ("rwkv_ln_w", l, g_post[0]), ("rwkv_ln_b", l, g_post[1]), ("rwkv_r_k", l, g_post[2]),
                       ("conv_w", l, jnp.concatenate(g_cw, axis=0))]
        layer_small += list(zip(["rwkv_mu", "rwkv_w0", "rwkv_w2", "rwkv_a0", "rwkv_a2", "rwkv_g2", "rwkv_k_k",
                                 "rwkv_k_a"], [l] * 8, g_prep[:8]))
        if vres:
            layer_small += list(zip(["rwkv_v1", "rwkv_v_mu", "rwkv_v0", "rwkv_v2"], [l - 1] * 4, g_prep[8:12]))
        for n, index, val in layer_small:
            grads[n][index] = val
        if l == 0:
            layer_small.append(("loss", 0, loss.reshape(1, 1)))
        token = on_small(l, layer_small)
        if token is not None and l > 0:
            dx = dx + token[0, 0]
    return loss, dx, grads


def _split3(a):
    hi = a.astype(bf16)
    r1 = a - hi.astype(f32)
    mid = r1.astype(bf16)
    lo = (r1 - mid.astype(f32)).astype(bf16)
    return hi, mid, lo


def _shard_axis(name):
    return 1 if name in ROW_SHARDED else 2


def _pack(pieces, width, dtype, row_align):
    flat = jnp.concatenate([p.reshape(-1).astype(dtype) for p in pieces])
    rows = -(-flat.shape[0] // width)
    rows = -(-rows // row_align) * row_align
    return jnp.pad(flat, (0, rows * width - flat.shape[0])).reshape(rows, width)


def _unpack(flat2d, shapes):
    flat = flat2d.reshape(-1)
    out, off = [], 0
    for shp in shapes:
        n = int(np.prod(shp))
        out.append(flat[off:off + n].reshape(shp))
        off += n
    return out


def kernel(x, positions, attn_norm, w_in, mla_q_a_norm, mla_wq_b, mla_kv_a_norm, mla_wkv_b, mla_q_norm, mla_k_norm, mla_w_o, rwkv_mu, rwkv_w0, rwkv_w2, rwkv_a0, rwkv_a2, rwkv_g2, rwkv_k_k, rwkv_k_a, rwkv_r_k, rwkv_ln_w, rwkv_ln_b, rwkv_w_o, rwkv_v1, rwkv_v_mu, rwkv_v0, rwkv_v2, conv_w, conv_w_o, w_out, mlp_norm, w_up, w_down, loss_target, m_attn_norm, m_w_in, m_mla_q_a_norm, m_mla_wq_b, m_mla_kv_a_norm, m_mla_wkv_b, m_mla_q_norm, m_mla_k_norm, m_mla_w_o, m_rwkv_mu, m_rwkv_w0, m_rwkv_w2, m_rwkv_a0, m_rwkv_a2, m_rwkv_g2, m_rwkv_k_k, m_rwkv_k_a, m_rwkv_r_k, m_rwkv_ln_w, m_rwkv_ln_b, m_rwkv_w_o, m_rwkv_v1, m_rwkv_v_mu, m_rwkv_v0, m_rwkv_v2, m_conv_w, m_conv_w_o, m_w_out, m_mlp_norm, m_w_up, m_w_down, v_attn_norm, v_w_in, v_mla_q_a_norm, v_mla_wq_b, v_mla_kv_a_norm, v_mla_wkv_b, v_mla_q_norm, v_mla_k_norm, v_mla_w_o, v_rwkv_mu, v_rwkv_w0, v_rwkv_w2, v_rwkv_a0, v_rwkv_a2, v_rwkv_g2, v_rwkv_k_k, v_rwkv_k_a, v_rwkv_r_k, v_rwkv_ln_w, v_rwkv_ln_b, v_rwkv_w_o, v_rwkv_v1, v_rwkv_v_mu, v_rwkv_v0, v_rwkv_v2, v_conv_w, v_conv_w_o, v_w_out, v_mlp_norm, v_w_up, v_w_down):
    args = dict(locals())
    wts = {n: args[n] for n in WEIGHTS}
    mom = {n: args["m_" + n] for n in WEIGHTS}
    var = {n: args["v_" + n] for n in WEIGHTS}
    chip = 2 * lax.axis_index("x") + lax.axis_index("y")
    core = lax.axis_index("c").astype(jnp.int32).reshape(1)

    med_names = [n for n in MED if n != "conv_w"]
    med_pieces = [wts[n] for n in med_names] + list(_split3(wts["conv_w"]))
    med_shapes = [p.shape for p in med_pieces]
    chip_idx = chip.astype(jnp.int32).reshape(1)
    shards_first = [wts[n][0].astype(bf16) for n in STAGES[0]] + [_pack(med_pieces, 128, bf16, 32)]
    got_first, token = gather_weights(shards_first, name="gather_l0_s0")
    got_first = [place_slab(g, s, chip_idx, name=f"place_own_l0_s0_{q}")
                 for q, (g, s) in enumerate(zip(got_first, shards_first))]
    in_flight = {}
    for key, names, l in (("l0_s1", STAGES[1], 0), ("l0_s2", STAGES[2], 0), ("l1", BIG, 1)):
        group = [wts[n][l].astype(bf16) for n in names]
        group[0] = group[0] + token[0, 0].astype(bf16)
        in_flight[key] = (names, gather_start(group, name="gather_start_" + key))
        token = in_flight[key][1][4]

    def whole_of(names, slabs):
        out = {}
        for n, by_chip in zip(names, slabs):
            _, rows, cols = by_chip.shape
            if n in ROW_SHARDED:
                out[n] = by_chip.reshape(4 * rows, cols)
            else:
                out[n] = by_chip.transpose(1, 0, 2).reshape(rows, 4 * cols)
        return out

    landed = {}

    def big_of(l, stage, after):
        if (l, stage) == (0, 0):
            return whole_of(STAGES[0], got_first)
        key = "l1" if l == 1 else f"l0_s{stage}"
        if key not in landed:
            names, (send_sems, recv_sems, thru, lands, _) = in_flight[key]
            thru, lands = gather_wait(send_sems, recv_sems, thru, lands, after, name="gather_wait_" + key)
            lands = gather_forward(lands, name="gather_forward_" + key)
            landed[key] = whole_of(names, [place_slab(g, s, chip_idx, name=f"place_own_{key}_{q}")
                                           for q, (g, s) in enumerate(zip(lands, thru))])
        return {n: landed[key][n] for n in STAGES[stage]}

    whole = {}
    per_chip = [_unpack(got_first[len(STAGES[0])][j], med_shapes) for j in range(4)]
    for q, n in enumerate(med_names):
        whole[n] = jnp.concatenate([per_chip[j][q] for j in range(4)], axis=_shard_axis(n)).astype(f32)
    base = len(med_names)
    cw_parts = [jnp.concatenate([per_chip[j][base + t] for j in range(4)], axis=2).astype(f32) for t in range(3)]
    whole["conv_w"] = (cw_parts[0] + cw_parts[1]) + cw_parts[2]
    small = {n: wts[n] for n in SMALL}
    small["rwkv_r_k"] = wts["rwkv_r_k"].reshape(DEPTH, RW)

    exchanges = []

    def on_grads(l, stage, slabs):
        names = STAGES[stage]
        tag = f"l{l}_s{stage}"
        parts = [slabs[n] for n in names]
        from_sibling = grads_to_sibling(parts, name="grads_to_sibling_" + tag)
        chip_sums = [pair_sum(s, t, core, name=f"pair_sum_{n}_{l}") for n, s, t in zip(names, parts, from_sibling)]
        started = scatter_start(chip_sums, name="scatter_start_" + tag)
        exchanges.append((l, names, tag, started))
        return started[4]

    broadcasts = []

    def on_small(l, layer_small):
        values = [val for _, _, val in layer_small]
        started = broadcast_start(_pack(values, 128, f32, 8), name=f"small_start_l{l}")
        broadcasts.append((l, [(n, index, val.shape) for n, index, val in layer_small], started))
        return started[4]

    small["attn_norm"] = small["attn_norm"] + token[0, 0]
    _, grad_x, grads = local_step(x[0], positions[0], loss_target[0], whole, small, big_of, on_grads, on_small)

    device = (4 * lax.axis_index("x") + 2 * lax.axis_index("y") + lax.axis_index("c")).astype(jnp.int32).reshape(1)
    summed = {n: [None] * len(grads[n]) for n in SMALL + MED}
    summed["loss"] = [None]
    for l, entries, (send_sems, recv_sems, thru, land, _) in broadcasts:
        own, land = broadcast_wait(send_sems, recv_sems, thru, land, grad_x, name=f"small_wait_l{l}")
        total = sum8(land, own, device, name=f"small_sum_l{l}")
        for (n, index, _), val in zip(entries, _unpack(total, [shape for _, _, shape in entries])):
            summed[n][index] = val
    gsum = {}
    for n in SMALL + MED:
        g = jnp.stack(summed[n])
        if n in MED:
            ax = _shard_axis(n)
            width = wts[n].shape[ax]
            g = lax.dynamic_slice_in_dim(g, chip * width, width, axis=ax)
        gsum[n] = g.reshape(wts[n].shape)
    where = jnp.stack([lax.axis_index("c"), chip]).astype(jnp.int32)
    bufs, layout = {}, []
    for l, names, tag, (send_sems, recv_sems, thru, lands, _) in exchanges:
        own, arrived = scatter_wait(send_sems, recv_sems, thru, lands, grad_x, name="scatter_wait_" + tag)
        for n, mine, theirs in zip(names, own, arrived):
            rows = 2 * theirs.shape[1]
            bufs[n] = sum4_into(theirs, mine, bufs.get(n), where, layer=l, total_rows=DEPTH * rows,
                                name=f"sum_chips_{n}_{l}")
            layout.append((BIG.index(n), l * rows, rows))
    reduced = join_halves([bufs[n] for n in BIG], layout, name="join_halves")

    out_g, out_d, out_m, out_v = {}, {}, {}, {}
    for q, n in enumerate(BIG):
        shp = wts[n].shape
        as2d = lambda a: a.reshape(-1, shp[-1])
        g2d = w_in_window_cols(reduced[q], chip) if n == "w_in" else reduced[q]
        res = adamw(as2d(wts[n]), g2d, as2d(mom[n]), as2d(var[n]), name="adamw_" + n)
        out_g[n], out_d[n], out_m[n], out_v[n] = [r.reshape(shp) for r in res]
    sm_all = SMALL + MED
    flat2d = lambda a: a.reshape(-1, a.shape[-1])
    res = adamw_many([flat2d(wts[n]) for n in sm_all], [flat2d(gsum[n]) for n in sm_all],
                     [flat2d(mom[n]) for n in sm_all], [flat2d(var[n]) for n in sm_all], name="adamw_small")
    for tgt, vals in zip((out_d, out_m, out_v), res):
        for n, val in zip(sm_all, vals):
            tgt[n] = val.reshape(wts[n].shape)
    out_g.update({n: gsum[n] for n in sm_all})
    loss = summed["loss"][0].reshape(())
    return (loss, grad_x[None], *[out_g[n] for n in WEIGHTS], *[out_d[n] for n in WEIGHTS],
            *[out_m[n] for n in WEIGHTS], *[out_v[n] for n in WEIGHTS])
```

```python
import functools

import jax
import jax.numpy as jnp
import numpy as np
from jax import lax
from jax.experimental import pallas as pl
from jax.experimental.pallas import tpu as pltpu

f32, bf16 = jnp.float32, jnp.bfloat16
HI = lax.Precision.HIGHEST
MESH = pl.DeviceIdType.MESH

D = 1024
DEPTH = 2
MLA_H, NOPE, ROPE, DQK, DV = 8, 64, 32, 96, 64
QL, KVL = 384, 256
RW, RH, RN = 256, 4, 64
DL, AL, GL, MVL = 64, 64, 128, 32
CW = 256
DFF = 4096
GATE = 3 * D
MLA_COLS = QL + KVL + ROPE
MLA_PAD = 768
Q_HEAD_PAD = 128
NORM_EPS = 1e-6
GN_EPS = 64e-5
ROPE_THETA = 10000.0
LR, B1, B2, EPS, WD, STEP = 0.001, 0.9, 0.999, 1e-08, 0.01, 10

VMEM_LIMIT = 52 * 1024 * 1024
WKV_CHUNK = 64
WKV_CHUNKS_PER_STEP = 4
ATTN_SEGMENTS = 4

BIG = ["w_in", "mla_wq_b", "mla_wkv_b", "mla_w_o", "rwkv_w_o", "conv_w_o", "w_out", "w_up", "w_down"]
MED = ["rwkv_w2", "rwkv_a2", "rwkv_g2", "rwkv_v1", "rwkv_v2", "conv_w"]
ROW_SHARDED = {"w_out", "w_down", "rwkv_v1"}
SMALL = ["attn_norm", "mla_q_a_norm", "mla_kv_a_norm", "mla_q_norm", "mla_k_norm", "rwkv_mu", "rwkv_w0",
         "rwkv_a0", "rwkv_k_k", "rwkv_k_a", "rwkv_r_k", "rwkv_ln_w", "rwkv_ln_b", "rwkv_v_mu", "rwkv_v0",
         "mlp_norm"]
WEIGHTS = ["attn_norm", "w_in", "mla_q_a_norm", "mla_wq_b", "mla_kv_a_norm", "mla_wkv_b", "mla_q_norm",
           "mla_k_norm", "mla_w_o", "rwkv_mu", "rwkv_w0", "rwkv_w2", "rwkv_a0", "rwkv_a2", "rwkv_g2",
           "rwkv_k_k", "rwkv_k_a", "rwkv_r_k", "rwkv_ln_w", "rwkv_ln_b", "rwkv_w_o", "rwkv_v1", "rwkv_v_mu",
           "rwkv_v0", "rwkv_v2", "conv_w", "conv_w_o", "w_out", "mlp_norm", "w_up", "w_down"]


def _cparams(sem=None):
    return pltpu.CompilerParams(dimension_semantics=sem, vmem_limit_bytes=VMEM_LIMIT)


def _pick(dim, pref, align):
    if dim <= pref:
        return dim
    t = (pref // align) * align
    while t >= align:
        if dim % t == 0:
            return t
        t -= align
    return dim


def _bdot(a, b, dims):
    return lax.dot_general(a.astype(bf16), b.astype(bf16), (dims, ((), ())), preferred_element_type=f32)


@jax.custom_vjp
def _mm(a, b):
    return _bdot(a, b, ((1,), (0,)))


def _mm_fwd(a, b):
    return _mm(a, b), (a, b)


def _mm_bwd(res, g):
    a, b = res
    return _bdot(g, b, ((1,), (1,))), _bdot(a, g, ((0,), (0,)))


_mm.defvjp(_mm_fwd, _mm_bwd)


@jax.custom_vjp
def _mm_nt(a, b):
    return _bdot(a, b, ((1,), (1,)))


def _mm_nt_fwd(a, b):
    return _mm_nt(a, b), (a, b)


def _mm_nt_bwd(res, g):
    a, b = res
    return _bdot(g, b, ((1,), (0,))), _bdot(g, a, ((0,), (0,)))


_mm_nt.defvjp(_mm_nt_fwd, _mm_nt_bwd)


_NN, _NT, _TN = ((1,), (0,)), ((1,), (1,)), ((0,), (0,))


def _dg(a, b, dims):
    return lax.dot_general(a, b, (dims, ((), ())), preferred_element_type=f32)


def _bf16_pieces(x, count):
    out, rest = [], x
    for q in range(count):
        piece = rest.astype(bf16)
        out.append(piece)
        if q + 1 < count:
            rest = rest - piece.astype(f32)
    return out


def _dot3(a, b, dims):
    (ah, al), (bh, bl) = _bf16_pieces(a, 2), _bf16_pieces(b, 2)
    return _dg(ah, bh, dims) + (_dg(ah, bl, dims) + _dg(al, bh, dims))


@jax.custom_vjp
def _hdot(a, b):
    return _dot3(a, b, _NN)


@jax.custom_vjp
def _hdot_nt(a, b):
    return _dot3(a, b, _NT)


@jax.custom_vjp
def _hdot_tn(a, b):
    return _dot3(a, b, _TN)


_hdot.defvjp(lambda a, b: (_hdot(a, b), (a, b)), lambda res, g: (_hdot_nt(g, res[1]), _hdot_tn(res[0], g)))
_hdot_nt.defvjp(lambda a, b: (_hdot_nt(a, b), (a, b)), lambda res, g: (_hdot(g, res[1]), _hdot_tn(g, res[0])))
_hdot_tn.defvjp(lambda a, b: (_hdot_tn(a, b), (a, b)), lambda res, g: (_hdot_nt(res[1], g), _hdot(res[0], g)))


_BNN, _BNT, _BTN = ((2,), (1,)), ((2,), (2,)), ((1,), (1,))


def _bdg(a, b, dims):
    return lax.dot_general(a, b, (dims, ((0,), (0,))), preferred_element_type=f32)


def _bdot3(a, b, dims):
    (ah, al), (bh, bl) = _bf16_pieces(a, 2), _bf16_pieces(b, 2)
    return _bdg(ah, bh, dims) + (_bdg(ah, bl, dims) + _bdg(al, bh, dims))


@jax.custom_vjp
def _hbnn(a, b):
    return _bdot3(a, b, _BNN)


@jax.custom_vjp
def _hbnt(a, b):
    return _bdot3(a, b, _BNT)


@jax.custom_vjp
def _hbtn(a, b):
    return _bdot3(a, b, _BTN)


_hbnn.defvjp(lambda a, b: (_hbnn(a, b), (a, b)), lambda res, g: (_hbnt(g, res[1]), _hbtn(res[0], g)))
_hbnt.defvjp(lambda a, b: (_hbnt(a, b), (a, b)), lambda res, g: (_hbnn(g, res[1]), _hbtn(g, res[0])))
_hbtn.defvjp(lambda a, b: (_hbtn(a, b), (a, b)), lambda res, g: (_hbnt(res[1], g), _hbnn(res[0], g)))


@functools.partial(jax.custom_vjp, nondiff_argnums=(2,))
def _exact_bl(m, x, transposed):
    mb = m.astype(bf16)
    hi, mid, lo = _bf16_pieces(x, 3)
    dims = _BTN if transposed else _BNN
    return (_bdg(mb, hi, dims) + _bdg(mb, mid, dims)) + _bdg(mb, lo, dims)


_exact_bl.defvjp(lambda m, x, transposed: (_exact_bl(m, x, transposed), m),
                 lambda transposed, m, g: (jnp.zeros_like(m), _exact_bl(m, g, not transposed)))


@functools.partial(jax.custom_vjp, nondiff_argnums=(2,))
def _exact_l(m, x, transposed):
    mb = m.astype(bf16)
    hi, mid, lo = _bf16_pieces(x, 3)
    dims = _TN if transposed else _NN
    return (_dg(mb, hi, dims) + _dg(mb, mid, dims)) + _dg(mb, lo, dims)


_exact_l.defvjp(lambda m, x, transposed: (_exact_l(m, x, transposed), m),
                lambda transposed, m, g: (jnp.zeros_like(m), _exact_l(m, g, not transposed)))


@functools.partial(jax.custom_vjp, nondiff_argnums=(2,))
def _exact_r(x, m, transposed):
    mb = m.astype(bf16)
    hi, mid, lo = _bf16_pieces(x, 3)
    dims = _NT if transposed else _NN
    return (_dg(hi, mb, dims) + _dg(mid, mb, dims)) + _dg(lo, mb, dims)


_exact_r.defvjp(lambda x, m, transposed: (_exact_r(x, m, transposed), m),
                lambda transposed, m, g: (_exact_r(g, m, not transposed), jnp.zeros_like(m)))


def _rms(x, g, eps=NORM_EPS):
    return x * lax.rsqrt(jnp.mean(x * x, axis=-1, keepdims=True) + eps) * g


def _sigmoid(x):
    return 1.0 / (1.0 + jnp.exp(-x))


def _softplus(x):
    return jnp.maximum(x, 0.0) + jnp.log(1.0 + jnp.exp(-jnp.maximum(x, -x)))


def _lane_split(x, sizes):
    bounds = np.cumsum([0] + list(sizes))

    @jax.custom_vjp
    def split(v):
        return tuple(v[..., int(bounds[q]):int(bounds[q + 1])] for q in range(len(sizes)))

    split.defvjp(lambda v: (split(v), None), lambda _, g: (jnp.concatenate(g, axis=-1),))
    return split(x)


def _row_split(x, sizes):
    bounds = np.cumsum([0] + list(sizes))

    @jax.custom_vjp
    def split(v):
        return tuple(v[..., int(bounds[q]):int(bounds[q + 1]), :] for q in range(len(sizes)))

    split.defvjp(lambda v: (split(v), None), lambda _, g: (jnp.concatenate(g, axis=-2),))
    return split(x)


def _shift_mats(t, k):
    r = lax.broadcasted_iota(jnp.int32, (t, t), 0)
    c = lax.broadcasted_iota(jnp.int32, (t, t), 1)
    inner = (r - c == k).astype(f32)
    r8 = lax.broadcasted_iota(jnp.int32, (t, 8), 0)
    c8 = lax.broadcasted_iota(jnp.int32, (t, 8), 1)
    edge = (c8 - r8 == 8 - k).astype(f32)
    return inner, edge


def _shift(x, halo, k):
    inner, edge = _shift_mats(x.shape[0], k)
    return _exact_l(inner, x, False) + jnp.dot(edge, halo, precision=HI, preferred_element_type=f32)


def mm(a, b, *, name, ta=False, tb=False, a_batched=False, b_batched=False, reduce_batch=False, add=None,
       act_grad=None, relu2_out=False, n_split=1, out_dtype=f32, tm=1024, tn=1024, tk=2048):
    ash, bsh = a.shape[-2:], b.shape[-2:]
    (k_, m_) = ash if ta else ash[::-1]
    (k2_, n_) = bsh[::-1] if tb else bsh
    assert k_ == k2_, (a.shape, b.shape, ta, tb)
    hb = a.shape[0] if a_batched else (b.shape[0] if b_batched else 1)
    batched_out = (a_batched or b_batched) and not reduce_batch
    h_out = hb if batched_out else 1
    h_red = hb if reduce_batch else 1
    tm = _pick(m_, tm, 128)
    tn = _pick(n_ // n_split, tn, 128)
    tk = _pick(k_, tk, 128)
    nm, nn, nk = m_ // tm, n_ // tn, k_ // tk

    def a_map(i, j, ho, hr, kk):
        blk = (kk, i) if ta else (i, kk)
        return ((ho if batched_out else hr),) + blk if a_batched else blk

    def b_map(i, j, ho, hr, kk):
        blk = (j, kk) if tb else (kk, j)
        return ((ho if batched_out else hr),) + blk if b_batched else blk

    a_blk = (tk, tm) if ta else (tm, tk)
    b_blk = (tn, tk) if tb else (tk, tn)
    in_specs = [pl.BlockSpec(((1,) + a_blk) if a_batched else a_blk, a_map),
                pl.BlockSpec(((1,) + b_blk) if b_batched else b_blk, b_map)]
    args = [a, b]
    for extra in (add, act_grad):
        if extra is not None:
            in_specs.append(pl.BlockSpec((tm, tn), lambda i, j, ho, hr, kk: (i, j)))
            args.append(extra)
    if n_split > 1:
        per = n_ // n_split // tn
        if batched_out:
            out_spec = pl.BlockSpec((1, 1, tm, tn), lambda i, j, ho, hr, kk: (j // per, ho, i, j % per))
            out_shape = jax.ShapeDtypeStruct((n_split, hb, m_, n_ // n_split), out_dtype)
        else:
            out_spec = pl.BlockSpec((1, tm, tn), lambda i, j, ho, hr, kk: (j // per, i, j % per))
            out_shape = jax.ShapeDtypeStruct((n_split, m_, n_ // n_split), out_dtype)
    elif batched_out:
        out_spec = pl.BlockSpec((1, tm, tn), lambda i, j, ho, hr, kk: (ho, i, j))
        out_shape = jax.ShapeDtypeStruct((hb, m_, n_), out_dtype)
    else:
        out_spec = pl.BlockSpec((tm, tn), lambda i, j, ho, hr, kk: (i, j))
        out_shape = jax.ShapeDtypeStruct((m_, n_), out_dtype)
    lead = (0,) * (int(batched_out) + int(n_split > 1))
    dims = ((0,) if ta else (1,), (1,) if tb else (0,))
    has_add, has_act = add is not None, act_grad is not None

    def body(*refs):
        a_ref, b_ref = refs[0], refs[1]
        pos = 2
        add_ref = act_ref = None
        if has_add:
            add_ref = refs[pos]
            pos += 1
        if has_act:
            act_ref = refs[pos]
            pos += 1
        o_ref, acc_ref = refs[pos], refs[-1]
        hr, kk = pl.program_id(3), pl.program_id(4)
        first = jnp.logical_and(hr == 0, kk == 0)
        last = jnp.logical_and(hr == h_red - 1, kk == nk - 1)
        av = a_ref[0] if a_batched else a_ref[...]
        bv = b_ref[0] if b_batched else b_ref[...]
        p = _bdot(av, bv, dims)
        single = h_red * nk == 1

        if not single:
            @pl.when(first)
            def _():
                acc_ref[...] = p

            @pl.when(jnp.logical_not(first))
            def _():
                acc_ref[...] += p

        @pl.when(last)
        def _():
            r = p if single else acc_ref[...]
            if has_act:
                r = r * (2.0 * jnp.maximum(act_ref[...], 0.0))
            if has_add:
                r = r + add_ref[...]
            if lead:
                o_ref[lead] = r.astype(out_dtype)
            else:
                o_ref[...] = r.astype(out_dtype)
            if relu2_out:
                refs[pos + 1][...] = jnp.square(jnp.maximum(r, 0.0)).astype(bf16)

    if relu2_out:
        assert not lead
        out_spec = [out_spec, out_spec]
        out_shape = [out_shape, jax.ShapeDtypeStruct(out_shape.shape, bf16)]
    return pl.pallas_call(
        body, name=name, grid=(nm, nn, h_out, h_red, nk), in_specs=in_specs, out_specs=out_spec,
        out_shape=out_shape, scratch_shapes=[pltpu.VMEM((tm, tn), f32)],
        compiler_params=_cparams(("parallel", "parallel", "parallel", "arbitrary", "arbitrary")),
    )(*args)


def _row_spec(arr, tile, idx):
    if arr.ndim == 2:
        return pl.BlockSpec((tile, arr.shape[1]), lambda i: (idx(i), 0))
    return pl.BlockSpec((arr.shape[0], tile, arr.shape[2]), lambda i: (0, idx(i), 0))


def _halo_spec(arr, tile, idx):
    per = tile // 8
    return pl.BlockSpec((8, arr.shape[1]), lambda i: (jnp.maximum(idx(i) * per - 1, 0), 0))


def _full_spec(arr):
    nd = arr.ndim
    return pl.BlockSpec(arr.shape, lambda i: (0,) * nd)


def _load_f32(ref):
    val = ref[...]
    return val.astype(f32) if val.dtype == bf16 else val


def rows_fwd(fn, rows, params, consts, out_shapes, *, tile, name, halos=()):
    s_len = rows[0].shape[-2]
    n = s_len // tile
    nr, nh, npar, nc = len(rows), len(halos), len(params), len(consts)
    ident = lambda i: i
    in_specs = ([_row_spec(r, tile, ident) for r in rows] + [_halo_spec(rows[h], tile, ident) for h in halos]
                + [_full_spec(p) for p in params] + [_full_spec(c) for c in consts])
    out_specs = [_row_spec(o, tile, ident) for o in out_shapes]

    def body(*refs):
        i = pl.program_id(0)
        rv = [_load_f32(r) for r in refs[:nr]]
        keep = (i > 0).astype(f32)
        hv = [r[...] * keep for r in refs[nr:nr + nh]]
        pv = [r[...] for r in refs[nr + nh:nr + nh + npar]]
        cv = [r[...] for r in refs[nr + nh + npar:nr + nh + npar + nc]]
        outs = fn(rv, hv, pv, cv)
        for o_ref, o in zip(refs[nr + nh + npar + nc:], outs):
            o_ref[...] = o.astype(o_ref.dtype)

    return pl.pallas_call(
        body, name=name, grid=(n,), in_specs=in_specs, out_specs=out_specs, out_shape=list(out_shapes),
        compiler_params=_cparams(("arbitrary",)),
    )(*rows, *[rows[h] for h in halos], *params, *consts)


def rows_bwd(fn, rows, params, consts, douts, *, tile, name, halos=(), grad_rows=None, extra=None,
             grad_dtypes=None):
    s_len = rows[0].shape[-2]
    n = s_len // tile
    nr, nh, npar, nc = len(rows), len(halos), len(params), len(consts)
    grad_rows = list(range(nr)) if grad_rows is None else list(grad_rows)
    extra = extra or {}
    assert all(h in grad_rows for h in halos)
    rev = lambda i: n - 1 - i
    dflat = [d for ds in douts for d in ds]
    dcount = [len(ds) for ds in douts]
    eflat = [e for g in grad_rows for e in extra.get(g, [])]
    ecount = [len(extra.get(g, [])) for g in grad_rows]
    in_specs = ([_row_spec(r, tile, rev) for r in rows] + [_halo_spec(rows[h], tile, rev) for h in halos]
                + [_full_spec(p) for p in params] + [_full_spec(c) for c in consts]
                + [_row_spec(d, tile, rev) for d in dflat] + [_row_spec(e, tile, rev) for e in eflat])
    grad_dtypes = [f32] * len(grad_rows) if grad_dtypes is None else list(grad_dtypes)
    assert all(grad_dtypes[q] == f32 for q, g in enumerate(grad_rows) if g in halos)
    out_shapes = ([jax.ShapeDtypeStruct(rows[g].shape, dt) for g, dt in zip(grad_rows, grad_dtypes)]
                  + [jax.ShapeDtypeStruct(p.shape, f32) for p in params])
    out_specs = [_row_spec(rows[g], tile, rev) for g in grad_rows] + [_full_spec(p) for p in params]
    scratch = [pltpu.VMEM((8, rows[h].shape[1]), f32) for h in halos]
    n_in = nr + nh + npar + nc + len(dflat) + len(eflat)
    n_out = len(grad_rows) + npar

    def body(*refs):
        i = pl.program_id(0)
        rv = [_load_f32(r) for r in refs[:nr]]
        keep = (i < n - 1).astype(f32)
        hv = [r[...] * keep for r in refs[nr:nr + nh]]
        pv = [r[...] for r in refs[nr + nh:nr + nh + npar]]
        pos = nr + nh + npar
        cv = [r[...] for r in refs[pos:pos + nc]]
        pos += nc
        dv = []
        for cnt in dcount:
            acc = _load_f32(refs[pos])
            for q in range(1, cnt):
                acc = acc + _load_f32(refs[pos + q])
            dv.append(acc)
            pos += cnt
        ev = []
        for cnt in ecount:
            ev.append([_load_f32(refs[pos + q]) for q in range(cnt)])
            pos += cnt
        out_refs = refs[n_in:n_in + n_out]
        carry_refs = refs[n_in + n_out:]

        def f(gr, gh, gp):
            full = list(rv)
            for g, val in zip(grad_rows, gr):
                full[g] = val
            return tuple(fn(full, gh, gp, cv))

        _, vjp = jax.vjp(f, [rv[g] for g in grad_rows], hv, pv)
        d_rows, d_halos, d_params = vjp(tuple(dv))

        @pl.when(i == 0)
        def _():
            for c_ref in carry_refs:
                c_ref[...] = jnp.zeros_like(c_ref)
            for p_ref in out_refs[len(grad_rows):]:
                p_ref[...] = jnp.zeros_like(p_ref)

        for q, g in enumerate(grad_rows):
            val = d_rows[q]
            for e in ev[q]:
                val = val + e
            out_refs[q][...] = val.astype(out_refs[q].dtype)
            if g in halos:
                hq = list(halos).index(g)
                out_refs[q][tile - 8:tile, :] += carry_refs[hq][...]
                carry_refs[hq][...] = d_halos[hq]
        for p_ref, dp in zip(out_refs[len(grad_rows):], d_params):
            p_ref[...] += dp

    res = pl.pallas_call(
        body, name=name, grid=(n,), in_specs=in_specs, out_specs=out_specs, out_shape=out_shapes,
        scratch_shapes=scratch, compiler_params=_cparams(("arbitrary",)),
    )(*rows, *[rows[h] for h in halos], *params, *consts, *dflat, *eflat)
    return list(res[:len(grad_rows)]), list(res[len(grad_rows):])


def _fn_norm(rows, halos, params, consts):
    return (_rms(rows[0], params[0]),)


def _fn_mla_prep(rows, halos, params, consts):
    cq, ckv, kpe = _lane_split(rows[0], (QL, KVL, MLA_PAD - QL - KVL))
    return _rms(cq, params[0]), _rms(ckv, params[1]), kpe


def _rope(x, cos, sin, rot):
    return x * cos + _exact_r(x, rot, False) * sin


def _fn_qk_post(rows, halos, params, consts):
    q_flat, kv_flat, kpe, cos, sin = rows
    q_norm, k_norm = params
    (rot,) = consts
    q_heads = _lane_split(q_flat, (DQK, Q_HEAD_PAD - DQK) * MLA_H)[::2]
    kv_heads = _lane_split(kv_flat, (NOPE, DV) * MLA_H)
    k_pe, _ = _lane_split(kpe, (ROPE, kpe.shape[1] - ROPE))
    qs = [_rope(_rms(qh, q_norm), cos, sin, rot) for qh in q_heads]
    ks = [_rope(_rms(jnp.concatenate([kv_heads[2 * h], k_pe], axis=-1), k_norm), cos, sin, rot)
          for h in range(MLA_H)]
    vs = [kv_heads[2 * h + 1] for h in range(MLA_H)]
    return jnp.stack(qs, axis=0), jnp.stack(ks, axis=0), jnp.stack(vs, axis=0)


def _seg(x, bd):
    return _exact_r(x, bd, False)


def _make_fn_rwkv_prep(vres):
    def fn(rows, halos, params, consts):
        cols = rows[0]
        bd = consts[0]
        mu, w0, w2, a0, a2, g2, k_k, k_a = params[:8]
        prev = _shift(cols, halos[0], 1)
        c = cols + (prev - cols) * mu
        r, k, v, xw, xa, xg = _lane_split(c, (RW, RW, RW, DL, AL, GL))
        log_w = -_softplus(-(w0 + _mm(jnp.tanh(xw), w2))) - 0.5
        ld = -jnp.exp(log_w)
        a = _sigmoid(a0 + _mm(xa, a2))
        g = _mm(_sigmoid(xg), g2)
        if vres:
            hcur, v_first = rows[1], rows[2]
            v1, v_mu, v0, v2 = params[8:12]
            xv = _mm(hcur, v1)
            xv_prev = _shift(xv, _mm(halos[1], v1), 1)
            xv = xv + (xv_prev - xv) * v_mu
            v = v + (v_first - v) * _sigmoid(v0 + _mm(xv, v2))
        kk = k * k_k
        kk = kk / jnp.maximum(jnp.sqrt(_seg(kk * kk, bd)), 1e-12)
        k2 = k * (1.0 + (a - 1.0) * k_a)
        return r, ld, k2, v, -kk, kk * a, g
    return fn


def _fn_rwkv_post(rows, halos, params, consts):
    y, r, k2, v, g = rows
    ln_w, ln_b, r_k = params
    bd = consts[0]
    mean = _seg(y, bd) * (1.0 / RN)
    d = y - mean
    var = _seg(d * d, bd) * (1.0 / RN)
    yn = d * lax.rsqrt(var + GN_EPS) * ln_w + ln_b
    bonus = _seg(r * k2 * r_k, bd) * v
    return ((yn + bonus) * g,)


def _fn_conv(rows, halos, params, consts):
    cols, halo = rows[0], halos[0]
    w0, w1, w2 = params
    b, c, x = _lane_split(cols, (CW, CW, CW))
    _, ch, xh = _lane_split(halo, (CW, CW, CW))
    u, uh = c * x, ch * xh
    return (b * (w0 * _shift(u, uh, 2) + w1 * _shift(u, uh, 1) + w2 * u),)


def _fn_merge(rows, halos, params, consts):
    gate, o_a, o_b, o_c = rows
    g_a, g_b, g_c = _lane_split(gate, (D, D, D))
    return (_sigmoid(g_a) * o_a + _sigmoid(g_b) * o_b + _sigmoid(g_c) * o_c,)


def _attn_block(q, k, v, q0, diagonal_last):
    tq, kend = q.shape[0], k.shape[0]
    s = _mm_nt(q, k) * (DQK ** -0.5)
    if diagonal_last:
        tri = lax.broadcasted_iota(jnp.int32, (tq, tq), 0) >= lax.broadcasted_iota(jnp.int32, (tq, tq), 1)
        if kend > tq:
            before, diag = _lane_split(s, (kend - tq, tq))
            s = jnp.concatenate([before, jnp.where(tri, diag, -1e30)], axis=-1)
        else:
            s = jnp.where(tri, s, -1e30)
    else:
        row = q0 + lax.broadcasted_iota(jnp.int32, (tq, kend), 0)
        col = lax.broadcasted_iota(jnp.int32, (tq, kend), 1)
        s = jnp.where(row >= col, s, -1e30)
    m = lax.stop_gradient(jnp.max(s, axis=-1, keepdims=True))
    e = jnp.exp(s - m)
    p = e / jnp.sum(e, axis=-1, keepdims=True)
    return _mm(p, v)


def _attn_segments(s_len, tq):
    per = max(1, s_len // tq // ATTN_SEGMENTS)
    return [(first, per, (first + per) * tq) for first in range(0, s_len // tq, per)]


HEAD_PAIR = 2


def attn_fwd(q, k, v, *, tq, name):
    h, s_len, _ = q.shape
    outs = []
    for seg, (first, nq, kend) in enumerate(_attn_segments(s_len, tq)):
        def body(q_ref, k_ref, v_ref, o_ref, first=first, nq=nq):
            q0 = (first + pl.program_id(1)) * tq
            o = [_attn_block(q_ref[j], k_ref[j], v_ref[j], q0, nq == 1) for j in range(HEAD_PAIR)]
            o_ref[...] = jnp.concatenate(o, axis=-1).astype(o_ref.dtype)

        outs.append(pl.pallas_call(
            body, name=f"{name}_{seg}", grid=(h // HEAD_PAIR, nq),
            in_specs=[pl.BlockSpec((HEAD_PAIR, tq, DQK), lambda hp, i, first=first: (hp, first + i, 0)),
                      pl.BlockSpec((HEAD_PAIR, kend, DQK), lambda hp, i: (hp, 0, 0)),
                      pl.BlockSpec((HEAD_PAIR, kend, DV), lambda hp, i: (hp, 0, 0))],
            out_specs=pl.BlockSpec((tq, HEAD_PAIR * DV), lambda hp, i: (i, hp)),
            out_shape=jax.ShapeDtypeStruct((nq * tq, h * DV), bf16),
            compiler_params=_cparams(("parallel", "arbitrary")),
        )(q, k, v))
    return jnp.concatenate(outs, axis=0)


def attn_bwd(q, k, v, do, *, tq, name):
    h, s_len, _ = q.shape
    dqs, dk_acc, dv_acc = [], None, None
    for seg, (first, nq, kend) in reversed(list(enumerate(_attn_segments(s_len, tq)))):
        carried = dk_acc is not None

        def body(*refs, first=first, carried=carried, nq=nq):
            q_ref, k_ref, v_ref, do_ref = refs[:4]
            dq_ref, dk_ref, dv_ref = refs[-3:]
            i = pl.program_id(1)
            do_heads = _lane_split(do_ref[...], (DV,) * HEAD_PAIR)
            for j in range(HEAD_PAIR):
                _, vjp = jax.vjp(functools.partial(_attn_block, q0=(first + i) * tq, diagonal_last=nq == 1),
                                 q_ref[j], k_ref[j], v_ref[j])
                dq, dk, dv = vjp(do_heads[j])
                dq_ref[j] = dq

                @pl.when(i == 0)
                def _():
                    dk_ref[j] = dk + refs[4][j] if carried else dk
                    dv_ref[j] = dv + refs[5][j] if carried else dv

                @pl.when(i > 0)
                def _():
                    dk_ref[j] += dk
                    dv_ref[j] += dv

        key_specs = [pl.BlockSpec((HEAD_PAIR, kend, DQK), lambda hp, i: (hp, 0, 0)),
                     pl.BlockSpec((HEAD_PAIR, kend, DV), lambda hp, i: (hp, 0, 0))]
        dq, dk_acc, dv_acc = pl.pallas_call(
            body, name=f"{name}_{seg}", grid=(h // HEAD_PAIR, nq),
            in_specs=[pl.BlockSpec((HEAD_PAIR, tq, DQK), lambda hp, i, first=first: (hp, first + i, 0))] + key_specs
            + [pl.BlockSpec((tq, HEAD_PAIR * DV), lambda hp, i, first=first: (first + i, hp))]
            + (key_specs if carried else []),
            out_specs=[pl.BlockSpec((HEAD_PAIR, tq, DQK), lambda hp, i: (hp, i, 0))] + key_specs,
            out_shape=[jax.ShapeDtypeStruct((h, nq * tq, DQK), f32), jax.ShapeDtypeStruct((h, s_len, DQK), f32),
                       jax.ShapeDtypeStruct((h, s_len, DV), f32)],
            input_output_aliases={4: 1, 5: 2} if carried else {},
            compiler_params=_cparams(("parallel", "arbitrary")),
        )(q, k, v, do, *([dk_acc, dv_acc] if carried else []))
        dqs.append(dq)
    return jnp.concatenate(dqs[::-1], axis=1), dk_acc, dv_acc


def _wkv_local(r, ld, k, v, a, b):
    nb, c, n = r.shape
    ri = lax.broadcasted_iota(jnp.int32, (c, c), 0)
    ci = lax.broadcasted_iota(jnp.int32, (c, c), 1)
    tri = jnp.broadcast_to((ri >= ci).astype(f32)[None], (nb, c, c))
    cum = _exact_bl(tri, ld, False)
    tot = jnp.sum(ld, axis=1, keepdims=True)
    w_incl, w_excl, w_inv, w_rest = jnp.exp(cum), jnp.exp(cum - ld), jnp.exp(-cum), jnp.exp(tot - cum)
    ab, rb, bb, kb = a * w_excl, r * w_incl, b * w_inv, k * w_inv
    bw, kw = b * w_rest, k * w_rest
    r2 = lax.broadcasted_iota(jnp.int32, (2 * c, 2 * c), 0)
    c2 = lax.broadcasted_iota(jnp.int32, (2 * c, 2 * c), 1)
    t_of, s_of = jnp.where(r2 >= c, r2 - c, r2), jnp.where(c2 >= c, c2 - c, c2)
    keep = jnp.logical_or(t_of > s_of, jnp.logical_and(r2 >= c, t_of == s_of))
    pair = jnp.where(keep[None], _hbnt(jnp.concatenate([ab, rb], axis=1), jnp.concatenate([bb, kb], axis=1)), 0.0)
    on_b, on_k = _lane_split(pair, (c, c))
    l_ab, m_rb = _row_split(on_b, (c, c))
    l_ak_v, m_rk_v = _row_split(_hbnn(on_k, v), (c, c))
    x = jnp.concatenate([ab, l_ak_v], axis=-1)
    lp, span = l_ab, 1
    while span < c:
        x = x + _hbnn(lp, x)
        span *= 2
        if span < c:
            lp = _hbnn(lp, lp)
    via_b_r, via_b_y = _lane_split(_hbnn(m_rb, x), (n, n))
    r_hat = rb + via_b_r
    y0 = via_b_y + m_rk_v
    from_b_g, from_b_z = _row_split(_hbtn(x, bw), (n, n))
    eye = lax.broadcasted_iota(jnp.int32, (n, n), 0) == lax.broadcasted_iota(jnp.int32, (n, n), 1)
    g = jnp.where(eye[None], jnp.exp(tot), 0.0) + from_b_g
    z = from_b_z + _hbtn(v, kw)
    return r_hat, y0, g, z


def _head(h):
    return slice(RN * h, RN * (h + 1))


def _load_chunk_heads(ref, c, per):
    return jnp.stack([ref[c * q:c * (q + 1), _head(h)] for q in range(per) for h in range(RH)], axis=0)


def _store_chunk_heads(ref, val, c, per):
    for q in range(per):
        ref[c * q:c * (q + 1), :] = jnp.concatenate([val[q * RH + h] for h in range(RH)], axis=-1)


def wkv_fwd(r, ld, k, v, a, b, *, name):
    s_len = r.shape[0]
    c = WKV_CHUNK
    n = s_len // c
    per = min(WKV_CHUNKS_PER_STEP, n)
    rows = pl.BlockSpec((c * per, RW), lambda i: (i, 0))
    mats = pl.BlockSpec((per, RH, RN, RN), lambda i: (i, 0, 0, 0))
    rows_t, mats_t = jax.ShapeDtypeStruct((s_len, RW), f32), jax.ShapeDtypeStruct((n, RH, RN, RN), f32)

    def local_body(r_ref, ld_ref, k_ref, v_ref, a_ref, b_ref, rh_ref, y0_ref, g_ref, z_ref):
        r_hat, y0, g, z = _wkv_local(*[_load_chunk_heads(ref, c, per)
                                       for ref in (r_ref, ld_ref, k_ref, v_ref, a_ref, b_ref)])
        _store_chunk_heads(rh_ref, r_hat, c, per)
        _store_chunk_heads(y0_ref, y0, c, per)
        g_ref[...] = g.reshape(per, RH, RN, RN)
        z_ref[...] = z.reshape(per, RH, RN, RN)

    r_hat, y0, g, z = pl.pallas_call(
        local_body, name=name + "_local", grid=(n // per,), in_specs=[rows] * 6, out_specs=[rows, rows, mats, mats],
        out_shape=[rows_t, rows_t, mats_t, mats_t], compiler_params=_cparams(("parallel",)),
    )(r, ld, k, v, a, b)

    def scan_body(g_ref, z_ref, st_ref, s_sc):
        s_sc[...] = jnp.zeros_like(s_sc)

        @pl.loop(0, n)
        def _(i):
            s0 = s_sc[...]
            st_ref[i] = s0
            s_sc[...] = _hbnn(s0, g_ref[i]) + z_ref[i]

    vm = pl.BlockSpec(memory_space=pltpu.VMEM)
    states = pl.pallas_call(
        scan_body, name=name + "_scan", in_specs=[vm, vm], out_specs=vm, out_shape=mats_t,
        scratch_shapes=[pltpu.VMEM((RH, RN, RN), f32)],
        compiler_params=pltpu.CompilerParams(vmem_limit_bytes=VMEM_LIMIT),
    )(g, z)

    def out_body(rh_ref, y0_ref, st_ref, y_ref):
        y = _hbnt(_load_chunk_heads(rh_ref, c, per), st_ref[...].reshape(per * RH, RN, RN))
        _store_chunk_heads(y_ref, y, c, per)
        y_ref[...] += y0_ref[...]

    y = pl.pallas_call(
        out_body, name=name + "_out", grid=(n // per,), in_specs=[rows, rows, mats], out_specs=rows,
        out_shape=rows_t, compiler_params=_cparams(("parallel",)),
    )(r_hat, y0, states)
    return y, dict(r_hat=r_hat, g=g, states=states)


def wkv_bwd(r, ld, k, v, a, b, saved, dy, *, name):
    s_len = r.shape[0]
    c = WKV_CHUNK
    n = s_len // c
    per = min(WKV_CHUNKS_PER_STEP, n)
    rows = pl.BlockSpec((c * per, RW), lambda i: (i, 0))
    mats = pl.BlockSpec((per, RH, RN, RN), lambda i: (i, 0, 0, 0))
    rows_t, mats_t = jax.ShapeDtypeStruct((s_len, RW), f32), jax.ShapeDtypeStruct((n, RH, RN, RN), f32)

    def out_body(dy_ref, rh_ref, st_ref, drh_ref, dsy_ref):
        dyb = _load_chunk_heads(dy_ref, c, per)
        _store_chunk_heads(drh_ref, _hbnn(dyb, st_ref[...].reshape(per * RH, RN, RN)), c, per)
        dsy_ref[...] = _hbtn(dyb, _load_chunk_heads(rh_ref, c, per)).reshape(per, RH, RN, RN)

    d_rhat, ds_y = pl.pallas_call(
        out_body, name=name + "_out", grid=(n // per,), in_specs=[rows, rows, mats], out_specs=[rows, mats],
        out_shape=[rows_t, mats_t], compiler_params=_cparams(("parallel",)),
    )(dy, saved["r_hat"], saved["states"])

    def scan_body(dsy_ref, g_ref, st_ref, dg_ref, dz_ref, ds_sc):
        ds_sc[...] = jnp.zeros_like(ds_sc)

        @pl.loop(0, n)
        def _(i):
            cidx = n - 1 - i
            ds_next = ds_sc[...]
            dz_ref[cidx] = ds_next
            dg_ref[cidx] = _hbtn(st_ref[cidx], ds_next)
            ds_sc[...] = dsy_ref[cidx] + _hbnt(ds_next, g_ref[cidx])

    vm = pl.BlockSpec(memory_space=pltpu.VMEM)
    d_g, d_z = pl.pallas_call(
        scan_body, name=name + "_scan", in_specs=[vm, vm, vm], out_specs=[vm, vm], out_shape=[mats_t, mats_t],
        scratch_shapes=[pltpu.VMEM((RH, RN, RN), f32)],
        compiler_params=pltpu.CompilerParams(vmem_limit_bytes=VMEM_LIMIT),
    )(ds_y, saved["g"], saved["states"])

    def local_body(r_ref, ld_ref, k_ref, v_ref, a_ref, b_ref, drh_ref, dy_ref, dg_ref, dz_ref, *out_refs):
        _, vjp = jax.vjp(_wkv_local, *[_load_chunk_heads(ref, c, per)
                                       for ref in (r_ref, ld_ref, k_ref, v_ref, a_ref, b_ref)])
        grads = vjp((_load_chunk_heads(drh_ref, c, per), _load_chunk_heads(dy_ref, c, per),
                     dg_ref[...].reshape(per * RH, RN, RN), dz_ref[...].reshape(per * RH, RN, RN)))
        for o_ref, val in zip(out_refs, grads):
            _store_chunk_heads(o_ref, val, c, per)

    return pl.pallas_call(
        local_body, name=name + "_local", grid=(n // per,), in_specs=[rows] * 8 + [mats, mats], out_specs=[rows] * 6,
        out_shape=[rows_t] * 6, compiler_params=_cparams(("parallel",)),
    )(r, ld, k, v, a, b, d_rhat, dy, d_g, d_z)


def loss_head(y, target, *, tile, name):
    s_len, d = y.shape
    n = s_len // tile

    def body(y_ref, t_ref, dy_ref, l_ref):
        err = y_ref[...] - t_ref[...]
        dy_ref[...] = err * (1.0 / d)
        part = 0.5 * jnp.sum(jnp.mean(err * err, axis=-1, keepdims=True), axis=0, keepdims=True)

        @pl.when(pl.program_id(0) == 0)
        def _():
            l_ref[...] = jnp.zeros_like(l_ref)

        l_ref[...] += jnp.broadcast_to(part, l_ref.shape)

    bs = pl.BlockSpec((tile, d), lambda i: (i, 0))
    dy, l = pl.pallas_call(
        body, name=name, grid=(n,), in_specs=[bs, bs],
        out_specs=[bs, pl.BlockSpec((8, 128), lambda i: (0, 0))],
        out_shape=[jax.ShapeDtypeStruct((s_len, d), f32), jax.ShapeDtypeStruct((8, 128), f32)],
        compiler_params=_cparams(("arbitrary",)),
    )(y, target)
    return l[0, 0], dy


def _adamw_update(w, g, m, v):
    mn = B1 * m + (1.0 - B1) * g
    vn = B2 * v + (1.0 - B2) * (g * g)
    delta = -LR * ((mn / (1.0 - B1 ** STEP)) / (jnp.sqrt(vn / (1.0 - B2 ** STEP)) + EPS) + WD * w)
    return delta, mn, vn


def adamw(w, g, m, v, *, name):
    rows, cols = w.shape
    tile = _pick(rows, max(8, (2 * 1024 * 1024 // (4 * cols)) // 8 * 8), 8)

    def body(w_ref, g_ref, m_ref, v_ref, g_out, d_out, m_out, v_out):
        gv = g_ref[...]
        d_out[...], m_out[...], v_out[...] = _adamw_update(w_ref[...], gv, m_ref[...], v_ref[...])
        g_out[...] = gv

    bs = pl.BlockSpec((tile, cols), lambda i: (i, 0))
    return pl.pallas_call(
        body, name=name, grid=(rows // tile,), in_specs=[bs] * 4, out_specs=[bs] * 4,
        out_shape=[jax.ShapeDtypeStruct((rows, cols), f32)] * 4, compiler_params=_cparams(("parallel",)),
    )(w, g, m, v)


def adamw_many(ws, gs, ms, vs, *, name):
    n = len(ws)

    def body(*refs):
        ins, outs = refs[:4 * n], refs[4 * n:]
        for i in range(n):
            outs[i][...], outs[n + i][...], outs[2 * n + i][...] = _adamw_update(
                ins[i][...], ins[n + i][...], ins[2 * n + i][...], ins[3 * n + i][...])

    vm = pl.BlockSpec(memory_space=pltpu.VMEM)
    res = pl.pallas_call(
        body, name=name, in_specs=[vm] * (4 * n), out_specs=[vm] * (3 * n),
        out_shape=[jax.ShapeDtypeStruct(w.shape, f32) for w in ws] * 3,
        compiler_params=pltpu.CompilerParams(vmem_limit_bytes=VMEM_LIMIT),
    )(*ws, *gs, *ms, *vs)
    return list(res[:n]), list(res[n:2 * n]), list(res[2 * n:])


def _place():
    return lax.axis_index("x"), lax.axis_index("y"), lax.axis_index("c")


_ANY = pl.BlockSpec(memory_space=pl.ANY)


def _peer_chips(x, y):
    return [(1 - x, y), (x, 1 - y), (1 - x, 1 - y)]


def gather_weights(shards, *, name):
    nk = len(shards)

    def body(*refs):
        srcs, outs = refs[:nk], refs[nk:2 * nk]
        ici_send, ici_recv, d2d_send, d2d_recv = refs[2 * nk + 1:]
        x, y, c = _place()
        me = 2 * x + y
        peers = _peer_chips(x, y)
        pending = []
        for k in range(nk):
            half = srcs[k].shape[0] // 2
            mine = pl.ds(c * half, half)
            for p, (px, py) in enumerate(peers):
                cp = pltpu.make_async_remote_copy(
                    src_ref=srcs[k].at[mine], dst_ref=outs[k].at[me, mine], send_sem=ici_send.at[k, p],
                    recv_sem=ici_recv.at[k, p], device_id=(px, py, c), device_id_type=MESH)
                cp.start()
                pending.append(cp)
        for k in range(nk):
            half = srcs[k].shape[0] // 2
            mine = pl.ds(c * half, half)
            for p, (px, py) in enumerate(peers):
                landed = outs[k].at[2 * px + py, mine]
                pltpu.make_async_remote_copy(
                    src_ref=srcs[k].at[mine], dst_ref=landed, send_sem=ici_send.at[k, p], recv_sem=ici_recv.at[k, p],
                    device_id=(px, py, c), device_id_type=MESH).wait_recv()
                fwd = pltpu.make_async_remote_copy(
                    src_ref=landed, dst_ref=landed, send_sem=d2d_send.at[k, p], recv_sem=d2d_recv.at[k, p],
                    device_id=(x, y, 1 - c), device_id_type=MESH)
                fwd.start()
                pending.append(fwd)
        for k in range(nk):
            half = srcs[k].shape[0] // 2
            other = pl.ds((1 - c) * half, half)
            for p, (px, py) in enumerate(peers):
                theirs = outs[k].at[2 * px + py, other]
                pltpu.make_async_remote_copy(
                    src_ref=theirs, dst_ref=theirs, send_sem=d2d_send.at[k, p], recv_sem=d2d_recv.at[k, p],
                    device_id=(x, y, 1 - c), device_id_type=MESH).wait_recv()
        for cp in pending:
            cp.wait_send()
        refs[2 * nk][...] = jnp.zeros_like(refs[2 * nk])

    sem = lambda *shape: pltpu.SemaphoreType.DMA(shape)
    res = pl.pallas_call(
        body, name=name, in_specs=[_ANY] * nk, out_specs=[_ANY] * nk + [pl.BlockSpec(memory_space=pltpu.VMEM)],
        out_shape=[jax.ShapeDtypeStruct((4,) + s.shape, s.dtype) for s in shards]
        + [jax.ShapeDtypeStruct((8, 128), f32)],
        scratch_shapes=[sem(nk, 3), sem(nk, 3), sem(nk, 3), sem(nk, 3)],
    )(*shards)
    return list(res[:nk]), res[nk]


_HBM = pl.BlockSpec(memory_space=pltpu.HBM)
_SEM = pl.BlockSpec(memory_space=pltpu.SEMAPHORE)


def _ici_half_copies(srcs, lands, send_sems, recv_sems, incoming):
    x, y, c = _place()
    me = 2 * x + y
    out = []
    for k in range(len(srcs)):
        half = srcs[k].shape[0] // 2
        mine = pl.ds(c * half, half)
        for p, (px, py) in enumerate(_peer_chips(x, y)):
            out.append(pltpu.make_async_remote_copy(
                src_ref=srcs[k].at[mine], dst_ref=lands[k].at[(2 * px + py) if incoming else me, mine],
                send_sem=send_sems.at[3 * k + p], recv_sem=recv_sems.at[3 * k + p], device_id=(px, py, c),
                device_id_type=MESH))
    return out


def gather_start(shards, *, name):
    nk = len(shards)

    def body(*refs):
        srcs, lands = refs[:nk], refs[nk:2 * nk]
        send_sems, recv_sems = refs[2 * nk], refs[2 * nk + 1]
        token = refs[-1]
        for outgoing in _ici_half_copies(srcs, lands, send_sems, recv_sems, incoming=False):
            outgoing.start()
        token[...] = jnp.zeros_like(token)

    lands = [pltpu.with_memory_space_constraint(lax.empty((4,) + s.shape, s.dtype), pltpu.HBM) for s in shards]
    res = pl.pallas_call(
        body, name=name,
        out_shape=(pltpu.SemaphoreType.DMA((3 * nk,)), pltpu.SemaphoreType.DMA((3 * nk,)),
                   *[pltpu.HBM(s.shape, s.dtype) for s in shards], *[pltpu.HBM(z.shape, z.dtype) for z in lands],
                   jax.ShapeDtypeStruct((8, 128), f32)),
        in_specs=[_HBM] * (2 * nk), out_specs=(_SEM, _SEM, *[_HBM] * (2 * nk), pl.BlockSpec(memory_space=pltpu.VMEM)),
        input_output_aliases={k: 2 + k for k in range(2 * nk)},
        compiler_params=pltpu.CompilerParams(has_side_effects=pltpu.SideEffectType.DATAFLOW_SIDE_EFFECTING),
    )(*[pltpu.with_memory_space_constraint(s, pltpu.HBM) for s in shards], *lands)
    return res[0], res[1], list(res[2:2 + nk]), list(res[2 + nk:2 + 2 * nk]), res[-1]


def gather_wait(send_sems, recv_sems, shards, lands, after, *, name):
    nk = len(shards)

    def body(*refs):
        srcs, zones = refs[:nk], refs[nk:2 * nk]
        for outgoing in _ici_half_copies(srcs, zones, refs[2 * nk], refs[2 * nk + 1], incoming=False):
            outgoing.wait_send()
        for landed in _ici_half_copies(srcs, zones, refs[2 * nk], refs[2 * nk + 1], incoming=True):
            landed.wait_recv()

    res = pl.pallas_call(
        body, name=name,
        out_shape=(*[pltpu.HBM(s.shape, s.dtype) for s in shards], *[pltpu.HBM(z.shape, z.dtype) for z in lands]),
        in_specs=[_HBM] * (2 * nk) + [_SEM, _SEM, _ANY], out_specs=tuple([_HBM] * (2 * nk)),
        input_output_aliases={k: k for k in range(2 * nk)},
        compiler_params=pltpu.CompilerParams(has_side_effects=pltpu.SideEffectType.DATAFLOW_SIDE_EFFECTING),
    )(*shards, *lands, send_sems, recv_sems, after)
    return list(res[:nk]), list(res[nk:])


def gather_forward(lands, *, name):
    nk = len(lands)

    def body(*refs):
        zones = refs[nk:2 * nk]
        send_sems, recv_sems = refs[2 * nk:]
        x, y, c = _place()
        sends = []
        for k in range(nk):
            half = zones[k].shape[1] // 2
            for p, (px, py) in enumerate(_peer_chips(x, y)):
                landed = zones[k].at[2 * px + py, pl.ds(c * half, half)]
                cp = pltpu.make_async_remote_copy(
                    src_ref=landed, dst_ref=landed, send_sem=send_sems.at[k, p], recv_sem=recv_sems.at[k, p],
                    device_id=(x, y, 1 - c), device_id_type=MESH)
                cp.start()
                sends.append(cp)
        for k in range(nk):
            half = zones[k].shape[1] // 2
            for p, (px, py) in enumerate(_peer_chips(x, y)):
                theirs = zones[k].at[2 * px + py, pl.ds((1 - c) * half, half)]
                pltpu.make_async_remote_copy(
                    src_ref=theirs, dst_ref=theirs, send_sem=send_sems.at[k, p], recv_sem=recv_sems.at[k, p],
                    device_id=(x, y, 1 - c), device_id_type=MESH).wait_recv()
        for cp in sends:
            cp.wait_send()

    return pl.pallas_call(
        body, name=name, in_specs=[_ANY] * nk, out_specs=[_ANY] * nk,
        out_shape=[jax.ShapeDtypeStruct(z.shape, z.dtype) for z in lands],
        input_output_aliases={k: k for k in range(nk)},
        scratch_shapes=[pltpu.SemaphoreType.DMA((nk, 3)), pltpu.SemaphoreType.DMA((nk, 3))],
    )(*lands)


def grads_to_sibling(parts, *, name):
    nk = len(parts)

    def body(*refs):
        srcs, outs = refs[:nk], refs[nk:2 * nk]
        send_sems, recv_sems = refs[2 * nk:]
        x, y, c = _place()
        sends = []
        for k in range(nk):
            half = srcs[k].shape[1] // 2
            cp = pltpu.make_async_remote_copy(
                src_ref=srcs[k].at[:, pl.ds((1 - c) * half, half), :], dst_ref=outs[k], send_sem=send_sems.at[k],
                recv_sem=recv_sems.at[k], device_id=(x, y, 1 - c), device_id_type=MESH)
            cp.start()
            sends.append(cp)
        for cp in sends:
            cp.wait_recv()
        for cp in sends:
            cp.wait_send()

    return pl.pallas_call(
        body, name=name, in_specs=[_ANY] * nk, out_specs=[_ANY] * nk,
        out_shape=[jax.ShapeDtypeStruct((4, p.shape[1] // 2, p.shape[2]), p.dtype) for p in parts],
        scratch_shapes=[pltpu.SemaphoreType.DMA((nk,)), pltpu.SemaphoreType.DMA((nk,))],
    )(*parts)


def pair_sum(part, theirs, core, *, name):
    _, rows, cols = part.shape
    half = rows // 2
    tile = _pick(half, max(16, (1 << 20) // (4 * cols) // 16 * 16), 16)
    per = half // tile

    def body(c_ref, p_ref, t_ref, o_ref):
        o_ref[...] = (p_ref[...].astype(f32) + t_ref[...].astype(f32)).astype(bf16)

    grid_spec = pltpu.PrefetchScalarGridSpec(
        num_scalar_prefetch=1, grid=(4, per),
        in_specs=[pl.BlockSpec((1, tile, cols), lambda j, i, c_ref: (j, c_ref[0] * per + i, 0)),
                  pl.BlockSpec((1, tile, cols), lambda j, i, c_ref: (j, i, 0))],
        out_specs=pl.BlockSpec((1, tile, cols), lambda j, i, c_ref: (j, i, 0)))
    return pl.pallas_call(
        body, name=name, grid_spec=grid_spec, out_shape=jax.ShapeDtypeStruct((4, half, cols), bf16),
        compiler_params=_cparams(("parallel", "parallel")),
    )(core, part, theirs)


def _all_to_all_copies(srcs, lands, send_sems, recv_sems, incoming):
    x, y, c = _place()
    me = 2 * x + y
    out = []
    for k in range(len(srcs)):
        for p, (px, py) in enumerate(_peer_chips(x, y)):
            peer = 2 * px + py
            out.append(pltpu.make_async_remote_copy(
                src_ref=srcs[k].at[peer], dst_ref=lands[k].at[peer if incoming else me],
                send_sem=send_sems.at[3 * k + p], recv_sem=recv_sems.at[3 * k + p], device_id=(px, py, c),
                device_id_type=MESH))
    return out


def scatter_start(parts, *, name):
    nk = len(parts)

    def body(*refs):
        srcs, lands = refs[:nk], refs[nk:2 * nk]
        for outgoing in _all_to_all_copies(srcs, lands, refs[2 * nk], refs[2 * nk + 1], incoming=False):
            outgoing.start()
        refs[-1][...] = jnp.zeros_like(refs[-1])

    lands = [pltpu.with_memory_space_constraint(lax.empty(p.shape, p.dtype), pltpu.HBM) for p in parts]
    res = pl.pallas_call(
        body, name=name,
        out_shape=(pltpu.SemaphoreType.DMA((3 * nk,)), pltpu.SemaphoreType.DMA((3 * nk,)),
                   *[pltpu.HBM(p.shape, p.dtype) for p in parts], *[pltpu.HBM(p.shape, p.dtype) for p in parts],
                   jax.ShapeDtypeStruct((8, 128), f32)),
        in_specs=[_HBM] * (2 * nk), out_specs=(_SEM, _SEM, *[_HBM] * (2 * nk), pl.BlockSpec(memory_space=pltpu.VMEM)),
        input_output_aliases={k: 2 + k for k in range(2 * nk)},
        compiler_params=pltpu.CompilerParams(has_side_effects=pltpu.SideEffectType.DATAFLOW_SIDE_EFFECTING),
    )(*[pltpu.with_memory_space_constraint(p, pltpu.HBM) for p in parts], *lands)
    return res[0], res[1], list(res[2:2 + nk]), list(res[2 + nk:2 + 2 * nk]), res[-1]


def scatter_wait(send_sems, recv_sems, parts, lands, after, *, name):
    nk = len(parts)

    def body(*refs):
        srcs, zones = refs[:nk], refs[nk:2 * nk]
        for outgoing in _all_to_all_copies(srcs, zones, refs[2 * nk], refs[2 * nk + 1], incoming=False):
            outgoing.wait_send()
        for landed in _all_to_all_copies(srcs, zones, refs[2 * nk], refs[2 * nk + 1], incoming=True):
            landed.wait_recv()

    res = pl.pallas_call(
        body, name=name,
        out_shape=(*[pltpu.HBM(p.shape, p.dtype) for p in parts], *[pltpu.HBM(z.shape, z.dtype) for z in lands]),
        in_specs=[_HBM] * (2 * nk) + [_SEM, _SEM, _ANY], out_specs=tuple([_HBM] * (2 * nk)),
        input_output_aliases={k: k for k in range(2 * nk)},
        compiler_params=pltpu.CompilerParams(has_side_effects=pltpu.SideEffectType.DATAFLOW_SIDE_EFFECTING),
    )(*parts, *lands, send_sems, recv_sems, after)
    return list(res[:nk]), list(res[nk:])


def join_halves(bufs, layout, *, name):
    nk, nb = len(layout), len(bufs)

    def body(*refs):
        outs = refs[nb:2 * nb]
        send_sems, recv_sems = refs[2 * nb:]
        x, y, c = _place()
        pending = []
        for k, (o, off, rows) in enumerate(layout):
            half = rows // 2
            mine = outs[o].at[pl.ds(off + c * half, half), :]
            cp = pltpu.make_async_remote_copy(
                src_ref=mine, dst_ref=mine, send_sem=send_sems.at[k], recv_sem=recv_sems.at[k],
                device_id=(x, y, 1 - c), device_id_type=MESH)
            cp.start()
            pending.append(cp)
        for k, (o, off, rows) in enumerate(layout):
            half = rows // 2
            theirs = outs[o].at[pl.ds(off + (1 - c) * half, half), :]
            pltpu.make_async_remote_copy(
                src_ref=theirs, dst_ref=theirs, send_sem=send_sems.at[k], recv_sem=recv_sems.at[k],
                device_id=(x, y, 1 - c), device_id_type=MESH).wait_recv()
        for cp in pending:
            cp.wait_send()

    return pl.pallas_call(
        body, name=name, in_specs=[_ANY] * nb, out_specs=[_ANY] * nb,
        out_shape=[jax.ShapeDtypeStruct(b.shape, b.dtype) for b in bufs],
        input_output_aliases={o: o for o in range(nb)},
        scratch_shapes=[pltpu.SemaphoreType.DMA((nk,)), pltpu.SemaphoreType.DMA((nk,))],
    )(*bufs)


def _broadcast_copies(src, land, send_sems, recv_sems, incoming):
    x, y, c = _place()
    me = 4 * x + 2 * y + c
    out = []
    for m in range(1, 8):
        px, py, pc = x ^ (m >> 2), y ^ ((m >> 1) & 1), c ^ (m & 1)
        out.append(pltpu.make_async_remote_copy(
            src_ref=src, dst_ref=land.at[(4 * px + 2 * py + pc) if incoming else me], send_sem=send_sems.at[m - 1],
            recv_sem=recv_sems.at[m - 1], device_id=(px, py, pc), device_id_type=MESH))
    return out


def broadcast_start(src, *, name):
    def body(s_ref, l_ref, send_sems, recv_sems, s_thru, l_thru, token):
        for outgoing in _broadcast_copies(s_ref, l_ref, send_sems, recv_sems, incoming=False):
            outgoing.start()
        token[...] = jnp.zeros_like(token)

    land = pltpu.with_memory_space_constraint(lax.empty((8,) + src.shape, src.dtype), pltpu.HBM)
    return pl.pallas_call(
        body, name=name,
        out_shape=(pltpu.SemaphoreType.DMA((7,)), pltpu.SemaphoreType.DMA((7,)), pltpu.HBM(src.shape, src.dtype),
                   pltpu.HBM(land.shape, land.dtype), jax.ShapeDtypeStruct((8, 128), f32)),
        in_specs=[_HBM, _HBM], out_specs=(_SEM, _SEM, _HBM, _HBM, pl.BlockSpec(memory_space=pltpu.VMEM)),
        input_output_aliases={0: 2, 1: 3},
        compiler_params=pltpu.CompilerParams(has_side_effects=pltpu.SideEffectType.DATAFLOW_SIDE_EFFECTING),
    )(pltpu.with_memory_space_constraint(src, pltpu.HBM), land)


def broadcast_wait(send_sems, recv_sems, src, land, after, *, name):
    def body(s_ref, l_ref, send_sems, recv_sems, after_ref, s_out, l_out):
        for outgoing in _broadcast_copies(s_ref, l_ref, send_sems, recv_sems, incoming=False):
            outgoing.wait_send()
        for landed in _broadcast_copies(s_ref, l_ref, send_sems, recv_sems, incoming=True):
            landed.wait_recv()

    return pl.pallas_call(
        body, name=name, out_shape=(pltpu.HBM(src.shape, src.dtype), pltpu.HBM(land.shape, land.dtype)),
        in_specs=[_HBM, _HBM, _SEM, _SEM, _ANY], out_specs=(_HBM, _HBM), input_output_aliases={0: 0, 1: 1},
        compiler_params=pltpu.CompilerParams(has_side_effects=pltpu.SideEffectType.DATAFLOW_SIDE_EFFECTING),
    )(src, land, send_sems, recv_sems, after)


def sum8(land, own, device, *, name):
    _, rows, cols = land.shape

    def body(d_ref, l_ref, o_ref, out_ref):
        mine = o_ref[...]
        acc = jnp.where(d_ref[0] == 0, mine, l_ref[0])
        for d in range(1, 8):
            acc = acc + jnp.where(d_ref[0] == d, mine, l_ref[d])
        out_ref[...] = acc

    grid_spec = pltpu.PrefetchScalarGridSpec(
        num_scalar_prefetch=1, grid=(1,),
        in_specs=[pl.BlockSpec((8, rows, cols), lambda i, d_ref: (0, 0, 0)),
                  pl.BlockSpec((rows, cols), lambda i, d_ref: (0, 0))],
        out_specs=pl.BlockSpec((rows, cols), lambda i, d_ref: (0, 0)))
    return pl.pallas_call(
        body, name=name, grid_spec=grid_spec, out_shape=jax.ShapeDtypeStruct((rows, cols), f32),
        compiler_params=_cparams(("arbitrary",)),
    )(device, land, own)


def sum4_into(arrived, own, dest, where, *, layer, total_rows, name):
    _, rows, cols = arrived.shape
    tile = _pick(rows, max(16, (1 << 20) // (4 * cols) // 16 * 16), 16)
    per = rows // tile

    def body(w_ref, a_ref, own_ref, *rest):
        mine = own_ref[0].astype(f32)
        p = [jnp.where(w_ref[1] == j, mine, a_ref[j].astype(f32)) for j in range(4)]
        rest[-1][...] = ((p[0] + p[1]) + p[2]) + p[3]

    grid_spec = pltpu.PrefetchScalarGridSpec(
        num_scalar_prefetch=1, grid=(per,),
        in_specs=[pl.BlockSpec((4, tile, cols), lambda i, w_ref: (0, i, 0)),
                  pl.BlockSpec((1, tile, cols), lambda i, w_ref: (w_ref[1], i, 0))] + ([] if dest is None else [_ANY]),
        out_specs=pl.BlockSpec((tile, cols), lambda i, w_ref: ((2 * layer + w_ref[0]) * per + i, 0)))
    return pl.pallas_call(
        body, name=name, grid_spec=grid_spec, out_shape=jax.ShapeDtypeStruct((total_rows, cols), f32),
        input_output_aliases={} if dest is None else {3: 0}, compiler_params=_cparams(("parallel",)),
    )(where, arrived, own, *([] if dest is None else [dest]))


def _consts():
    idx = np.arange(RW)
    bd = (idx[:, None] // RN == idx[None, :] // RN).astype(np.float32)
    rot = np.zeros((DQK, DQK), np.float32)
    half = ROPE // 2
    rot[NOPE + half + np.arange(half), NOPE + np.arange(half)] = -1.0
    rot[NOPE + np.arange(half), NOPE + half + np.arange(half)] = 1.0
    return jnp.asarray(bd), jnp.asarray(rot)


def _rope_tables(positions):
    freqs = ROPE_THETA ** (-(jnp.arange(ROPE // 2, dtype=f32) * 2.0 / ROPE))
    ang = positions.astype(f32)[:, None] * freqs
    cos, sin = jnp.cos(ang), jnp.sin(ang)
    ones = jnp.ones((positions.shape[0], NOPE), f32)
    return (jnp.concatenate([ones, cos, cos], axis=-1), jnp.concatenate([0.0 * ones, sin, sin], axis=-1))


STAGES = (("w_in",), ("mla_wq_b", "mla_wkv_b", "mla_w_o", "rwkv_w_o", "conv_w_o", "w_out"), ("w_up", "w_down"))


def derive_stage(stage, w):
    if stage == 0:
        w_in = w["w_in"]
        pad = jnp.zeros((D, MLA_PAD - MLA_COLS), w_in.dtype)
        return dict(gate=w_in[:, :GATE], mla=jnp.concatenate([w_in[:, GATE:GATE + MLA_COLS], pad], axis=1),
                    rw=w_in[:, GATE + MLA_COLS:GATE + MLA_COLS + 4 * RW], cv=w_in[:, GATE + MLA_COLS + 4 * RW:])
    if stage == 1:
        wq = jnp.pad(w["mla_wq_b"].reshape(QL, MLA_H, DQK), ((0, 0), (0, 0), (0, Q_HEAD_PAD - DQK)))
        return dict(wq=wq.reshape(QL, MLA_H * Q_HEAD_PAD), wkv=w["mla_wkv_b"], wo=w["mla_w_o"], rwo=w["rwkv_w_o"],
                    cvo=w["conv_w_o"], out=w["w_out"])
    return dict(up=w["w_up"], down=w["w_down"])


W_IN_WINDOW_TILE = (0, 10, 21, 31)
W_IN_WINDOW = 1664
W_IN_SHARD = 1384


def w_in_window_cols(win, chip):
    gap = MLA_PAD - MLA_COLS
    branches = []
    for j in range(4):
        lo, hi = W_IN_SHARD * j, W_IN_SHARD * (j + 1)
        base = 128 * W_IN_WINDOW_TILE[j]
        cut = GATE + MLA_COLS
        if hi <= cut:
            branches.append(lambda w, a=lo - base: w[:, a:a + W_IN_SHARD])
        elif lo >= cut:
            branches.append(lambda w, a=lo + gap - base: w[:, a:a + W_IN_SHARD])
        else:
            branches.append(lambda w, a=lo - base, n1=cut - lo, b=cut + gap - base, n2=hi - cut:
                            jnp.concatenate([w[:, a:a + n1], w[:, b:b + n2]], axis=1))
    return lax.switch(chip, branches, win)


def chip_major_grads(stage, g):
    if stage == 0:
        padded = jnp.concatenate([g["gate"], g["mla"], g["rw"], g["cv"]], axis=1)
        return dict(w_in=jnp.stack([padded[:, 128 * t:128 * t + W_IN_WINDOW] for t in W_IN_WINDOW_TILE]))
    if stage == 1:
        wq = g["wq"].reshape(QL, MLA_H, Q_HEAD_PAD)[:, :, :DQK].reshape(QL, 4, -1).transpose(1, 0, 2)
        return dict(mla_wq_b=wq, mla_wkv_b=g["wkv"], mla_w_o=g["wo"], rwkv_w_o=g["rwo"], conv_w_o=g["cvo"],
                    w_out=g["out"].reshape(4, D // 4, D))
    return dict(w_up=g["up"], w_down=g["down"].reshape(4, DFF // 4, D))


def _row(v):
    return v.reshape(1, -1)


def local_step(x, positions, target, w, sm, big_of=None, on_grads=None, on_small=None):
    if big_of is None:
        big_of = lambda l, stage, _after: {n: w[n][l] for n in STAGES[stage]}
    if on_grads is None:
        on_grads = lambda l, stage, slabs: None
    if on_small is None:
        on_small = lambda l, layer_small: None
    s_len = x.shape[0]
    t_row = _pick(s_len, 256, 8)
    t_wide = _pick(s_len, 128, 8)
    bd, rot = _consts()
    cos, sin = _rope_tables(positions)
    sds = lambda *shape: jax.ShapeDtypeStruct(shape, f32)
    sdb = lambda *shape: jax.ShapeDtypeStruct(shape, bf16)
    saved = []
    v_first = None
    for l in range(DEPTH):
        tag = f"l{l}_"
        lw = derive_stage(0, big_of(l, 0, x))
        vres = l > 0
        p_norm1 = [_row(sm["attn_norm"][l])]
        (h,) = rows_fwd(_fn_norm, [x], p_norm1, [], [sds(s_len, D)], tile=t_row, name=tag + "norm1")
        gate = mm(h, lw["gate"], name=tag + "proj_gate")
        mla = mm(h, lw["mla"], name=tag + "proj_mla")
        rwc = mm(h, lw["rw"], name=tag + "proj_rwkv")
        cvc = mm(h, lw["cv"], name=tag + "proj_conv")
        lw.update(derive_stage(1, big_of(l, 1, cvc)))
        p_mla = [_row(sm["mla_q_a_norm"][l]), _row(sm["mla_kv_a_norm"][l])]
        qn, kvn, kpe = rows_fwd(_fn_mla_prep, [mla], p_mla, [], [sdb(s_len, QL), sdb(s_len, KVL), sds(s_len, 128)],
                                tile=t_row, name=tag + "mla_prep")
        q_flat = mm(qn, lw["wq"], name=tag + "q_proj")
        kv_flat = mm(kvn, lw["wkv"], name=tag + "kv_proj")
        p_qk = [_row(sm["mla_q_norm"][l]), _row(sm["mla_k_norm"][l])]
        q, k, vv = rows_fwd(_fn_qk_post, [q_flat, kv_flat, kpe, cos, sin], p_qk, [rot],
                            [sds(MLA_H, s_len, DQK), sds(MLA_H, s_len, DQK), sds(MLA_H, s_len, DV)], tile=t_wide,
                            name=tag + "qk_post")
        o = attn_fwd(q, k, vv, tq=_pick(s_len, 256, 8), name=tag + "attn")
        o_a = mm(o, lw["wo"], name=tag + "o_a")
        p_rw = [_row(sm["rwkv_mu"][l]), _row(sm["rwkv_w0"][l]), w["rwkv_w2"][l], _row(sm["rwkv_a0"][l]),
                w["rwkv_a2"][l], w["rwkv_g2"][l], _row(sm["rwkv_k_k"][l]), _row(sm["rwkv_k_a"][l])]
        rw_rows, rw_halos = [rwc], (0,)
        if vres:
            p_rw += [w["rwkv_v1"][l - 1], _row(sm["rwkv_v_mu"][l - 1]), _row(sm["rwkv_v0"][l - 1]), w["rwkv_v2"][l - 1]]
            rw_rows, rw_halos = [rwc, h, v_first], (0, 1)
        fn_prep = _make_fn_rwkv_prep(vres)
        r, ld, k2, v, an, bn, g = rows_fwd(fn_prep, rw_rows, p_rw, [bd], [sds(s_len, RW)] * 7, tile=t_row,
                                           name=tag + "rwkv_prep", halos=rw_halos)
        if not vres:
            v_first = v
        y, states = wkv_fwd(r, ld, k2, v, an, bn, name=tag + "wkv")
        p_post = [_row(sm["rwkv_ln_w"][l]), _row(sm["rwkv_ln_b"][l]), _row(sm["rwkv_r_k"][l])]
        (yb,) = rows_fwd(_fn_rwkv_post, [y, r, k2, v, g], p_post, [bd], [sdb(s_len, RW)], tile=t_row,
                         name=tag + "rwkv_post")
        o_b = mm(yb, lw["rwo"], name=tag + "o_b")
        p_cv = [w["conv_w"][l][q:q + 1] for q in range(3)]
        (yc,) = rows_fwd(_fn_conv, [cvc], p_cv, [], [sdb(s_len, CW)], tile=t_row, name=tag + "conv", halos=(0,))
        o_c = mm(yc, lw["cvo"], name=tag + "o_c")
        (merged,) = rows_fwd(_fn_merge, [gate, o_a, o_b, o_c], [], [], [sdb(s_len, D)], tile=t_wide,
                             name=tag + "merge")
        x1 = mm(merged, lw["out"], add=x, name=tag + "out_proj")
        lw.update(derive_stage(2, big_of(l, 2, x1)))
        p_norm2 = [_row(sm["mlp_norm"][l])]
        (h2,) = rows_fwd(_fn_norm, [x1], p_norm2, [], [sdb(s_len, D)], tile=t_row, name=tag + "norm2")
        up, act = mm(h2, lw["up"], relu2_out=True, name=tag + "up")
        x2 = mm(act, lw["down"], add=x1, name=tag + "down")
        saved.append(dict(lw=lw, x=x, h=h, gate=gate, mla=mla, rwc=rwc, cvc=cvc, qn=qn, kvn=kvn, kpe=kpe,
                          q_flat=q_flat, kv_flat=kv_flat, vv=vv, q=q, k=k, o=o, o_a=o_a, r=r, ld=ld, k2=k2, v=v,
                          an=an, bn=bn, g=g, y=y, states=states, yb=yb, o_b=o_b, yc=yc, o_c=o_c, merged=merged,
                          x1=x1, h2=h2, up=up, act=act, p_norm1=p_norm1, p_mla=p_mla, p_qk=p_qk, p_rw=p_rw,
                          p_post=p_post, p_cv=p_cv, p_norm2=p_norm2, rw_rows=rw_rows, rw_halos=rw_halos,
                          fn_prep=fn_prep, v_first=v_first if vres else None))
        x = x2

    loss, dx = loss_head(x, target, tile=t_row, name="loss_head")

    grads = {n: [None] * (DEPTH - 1 if n in ("rwkv_v1", "rwkv_v_mu", "rwkv_v0", "rwkv_v2") else DEPTH)
             for n in WEIGHTS}
    dv_first = None
    for l in reversed(range(DEPTH)):
        tag = f"b{l}_"
        sv = saved[l]
        lw = sv["lw"]
        vres = l > 0
        g_down = mm(sv["act"], dx, ta=True, out_dtype=bf16, name=tag + "g_down")
        dup = mm(dx, lw["down"], tb=True, act_grad=sv["up"], out_dtype=bf16, name=tag + "d_up")
        g_up = mm(sv["h2"], dup, ta=True, n_split=4, out_dtype=bf16, name=tag + "g_up")
        dh2 = mm(dup, lw["up"], tb=True, name=tag + "d_h2")
        slabs = chip_major_grads(2, dict(up=g_up, down=g_down))
        token = on_grads(l, 2, slabs)
        p_norm2 = sv["p_norm2"] if token is None else [sv["p_norm2"][0] + token[0, 0]]
        (dx1,), (g_n2,) = rows_bwd(_fn_norm, [sv["x1"]], p_norm2, [], [[dh2]], tile=t_row,
                                   name=tag + "norm2", extra={0: [dx]})
        g_out = mm(sv["merged"], dx1, ta=True, out_dtype=bf16, name=tag + "g_out")
        dmerged = mm(dx1, lw["out"], tb=True, name=tag + "d_merged")
        (dgate, do_a, do_b, do_c), _ = rows_bwd(_fn_merge, [sv["gate"], sv["o_a"], sv["o_b"], sv["o_c"]], [], [],
                                                [[dmerged]], tile=t_wide, name=tag + "merge",
                                                grad_dtypes=[bf16] * 4)
        g_cvo = mm(sv["yc"], do_c, ta=True, n_split=4, out_dtype=bf16, name=tag + "g_cvo")
        dyc = mm(do_c, lw["cvo"], tb=True, name=tag + "d_yc")
        (dcvc,), g_cw = rows_bwd(_fn_conv, [sv["cvc"]], sv["p_cv"], [], [[dyc]], tile=t_row, name=tag + "conv",
                                    halos=(0,))
        g_rwo = mm(sv["yb"], do_b, ta=True, n_split=4, out_dtype=bf16, name=tag + "g_rwo")
        dyb = mm(do_b, lw["rwo"], tb=True, name=tag + "d_yb")
        (dy, dr_p, dk_p, dv_p, dg), g_post = rows_bwd(
            _fn_rwkv_post, [sv["y"], sv["r"], sv["k2"], sv["v"], sv["g"]], sv["p_post"], [bd], [[dyb]], tile=t_row,
            name=tag + "rwkv_post")
        dr_s, dld, dk_s, dv_s, dan, dbn = wkv_bwd(sv["r"], sv["ld"], sv["k2"], sv["v"], sv["an"], sv["bn"],
                                                  sv["states"], dy, name=tag + "wkv")
        dv_list = [dv_s, dv_p] + ([dv_first] if (not vres and dv_first is not None) else [])
        d_prep, g_prep = rows_bwd(
            sv["fn_prep"], sv["rw_rows"], sv["p_rw"], [bd],
            [[dr_s, dr_p], [dld], [dk_s, dk_p], dv_list, [dan], [dbn], [dg]], tile=t_row, name=tag + "rwkv_prep",
            halos=sv["rw_halos"])
        drwc = d_prep[0]
        dh_extra = []
        if vres:
            dh_extra = [d_prep[1]]
            dv_first = d_prep[2]
        g_wo = mm(sv["o"], do_a, ta=True, n_split=4, out_dtype=bf16, name=tag + "g_wo")
        do = mm(do_a, lw["wo"], tb=True, name=tag + "d_o")
        dq, dk, dvv = attn_bwd(sv["q"], sv["k"], sv["vv"], do, tq=_pick(s_len, 256, 8), name=tag + "attn")
        (dq_flat, dkv_flat, dkpe), g_qk = rows_bwd(
            _fn_qk_post, [sv["q_flat"], sv["kv_flat"], sv["kpe"], cos, sin], sv["p_qk"], [rot], [[dq], [dk], [dvv]],
            tile=t_wide, name=tag + "qk_post", grad_rows=[0, 1, 2], grad_dtypes=[bf16, bf16, f32])
        g_wq = mm(sv["qn"], dq_flat, ta=True, out_dtype=bf16, name=tag + "g_wq")
        g_wkv = mm(sv["kvn"], dkv_flat, ta=True, n_split=4, out_dtype=bf16, name=tag + "g_wkv")
        dqn = mm(dq_flat, lw["wq"], tb=True, name=tag + "d_qn")
        dkvn = mm(dkv_flat, lw["wkv"], tb=True, name=tag + "d_kvn")
        slabs.update(chip_major_grads(1, dict(wq=g_wq, wkv=g_wkv, wo=g_wo, rwo=g_rwo, cvo=g_cvo, out=g_out)))
        token = on_grads(l, 1, {n: slabs[n] for n in STAGES[1]})
        p_mla = sv["p_mla"] if token is None else [sv["p_mla"][0] + token[0, 0], sv["p_mla"][1]]
        (dmla,), g_mla = rows_bwd(_fn_mla_prep, [sv["mla"]], p_mla, [], [[dqn], [dkvn], [dkpe]], tile=t_row,
                                  name=tag + "mla_prep", grad_dtypes=[bf16])
        g_gate = mm(sv["h"], dgate, ta=True, out_dtype=bf16, name=tag + "g_gate")
        g_mlaw = mm(sv["h"], dmla, ta=True, out_dtype=bf16, name=tag + "g_mla")
        g_rw = mm(sv["h"], drwc, ta=True, out_dtype=bf16, name=tag + "g_rw")
        g_cv = mm(sv["h"], dcvc, ta=True, out_dtype=bf16, name=tag + "g_cv")
        dh = mm(dgate, lw["gate"], tb=True, name=tag + "d_h_gate")
        dh = mm(dmla, lw["mla"], tb=True, add=dh, name=tag + "d_h_mla")
        dh = mm(drwc, lw["rw"], tb=True, add=dh, name=tag + "d_h_rw")
        dh = mm(dcvc, lw["cv"], tb=True, add=dh, name=tag + "d_h_cv")
        (dx,), (g_n1,) = rows_bwd(_fn_norm, [sv["x"]], sv["p_norm1"], [], [[dh] + dh_extra], tile=t_row,
                                  name=tag + "norm1", extra={0: [dx1]})
        slabs.update(chip_major_grads(0, dict(gate=g_gate, mla=g_mlaw, rw=g_rw, cv=g_cv)))
        token = on_grads(l, 0, {n: slabs[n] for n in STAGES[0]})
        if token is not None and l > 0:
            dx = dx + token[0, 0]
        for n, val in slabs.items():
            grads[n][l] = val
        layer_small = [("attn_norm", l, g_n1), ("mlp_norm", l, g_n2), ("mla_q_a_norm", l, g_mla[0]),
                       ("mla_kv_a_norm", l, g_mla[1]), ("mla_q_norm", l, g_qk[0]), ("mla_k_norm", l, g_qk[1]),
                       ("rwkv_ln_w", l, g_post[0]), ("rwkv_ln_b", l, g_post[1]), ("rwkv_r_k", l, g_post[2]),
                       ("conv_w", l, jnp.concatenate(g_cw, axis=0))]
        layer_small += list(zip(["rwkv_mu", "rwkv_w0", "rwkv_w2", "rwkv_a0", "rwkv_a2", "rwkv_g2", "rwkv_k_k",
                                 "rwkv_k_a"], [l] * 8, g_prep[:8]))
        if vres:
            layer_small += list(zip(["rwkv_v1", "rwkv_v_mu", "rwkv_v0", "rwkv_v2"], [l - 1] * 4, g_prep[8:12]))
        for n, index, val in layer_small:
            grads[n][index] = val
        if l == 0:
            layer_small.append(("loss", 0, loss.reshape(1, 1)))
        token = on_small(l, layer_small)
        if token is not None and l > 0:
            dx = dx + token[0, 0]
    return loss, dx, grads


def _split3(a):
    hi = a.astype(bf16)
    r1 = a - hi.astype(f32)
    mid = r1.astype(bf16)
    lo = (r1 - mid.astype(f32)).astype(bf16)
    return hi, mid, lo


def _shard_axis(name):
    return 1 if name in ROW_SHARDED else 2


def _pack(pieces, width, dtype, row_align):
    flat = jnp.concatenate([p.reshape(-1).astype(dtype) for p in pieces])
    rows = -(-flat.shape[0] // width)
    rows = -(-rows // row_align) * row_align
    return jnp.pad(flat, (0, rows * width - flat.shape[0])).reshape(rows, width)


def _unpack(flat2d, shapes):
    flat = flat2d.reshape(-1)
    out, off = [], 0
    for shp in shapes:
        n = int(np.prod(shp))
        out.append(flat[off:off + n].reshape(shp))
        off += n
    return out


def kernel(x, positions, attn_norm, w_in, mla_q_a_norm, mla_wq_b, mla_kv_a_norm, mla_wkv_b, mla_q_norm, mla_k_norm, mla_w_o, rwkv_mu, rwkv_w0, rwkv_w2, rwkv_a0, rwkv_a2, rwkv_g2, rwkv_k_k, rwkv_k_a, rwkv_r_k, rwkv_ln_w, rwkv_ln_b, rwkv_w_o, rwkv_v1, rwkv_v_mu, rwkv_v0, rwkv_v2, conv_w, conv_w_o, w_out, mlp_norm, w_up, w_down, loss_target, m_attn_norm, m_w_in, m_mla_q_a_norm, m_mla_wq_b, m_mla_kv_a_norm, m_mla_wkv_b, m_mla_q_norm, m_mla_k_norm, m_mla_w_o, m_rwkv_mu, m_rwkv_w0, m_rwkv_w2, m_rwkv_a0, m_rwkv_a2, m_rwkv_g2, m_rwkv_k_k, m_rwkv_k_a, m_rwkv_r_k, m_rwkv_ln_w, m_rwkv_ln_b, m_rwkv_w_o, m_rwkv_v1, m_rwkv_v_mu, m_rwkv_v0, m_rwkv_v2, m_conv_w, m_conv_w_o, m_w_out, m_mlp_norm, m_w_up, m_w_down, v_attn_norm, v_w_in, v_mla_q_a_norm, v_mla_wq_b, v_mla_kv_a_norm, v_mla_wkv_b, v_mla_q_norm, v_mla_k_norm, v_mla_w_o, v_rwkv_mu, v_rwkv_w0, v_rwkv_w2, v_rwkv_a0, v_rwkv_a2, v_rwkv_g2, v_rwkv_k_k, v_rwkv_k_a, v_rwkv_r_k, v_rwkv_ln_w, v_rwkv_ln_b, v_rwkv_w_o, v_rwkv_v1, v_rwkv_v_mu, v_rwkv_v0, v_rwkv_v2, v_conv_w, v_conv_w_o, v_w_out, v_mlp_norm, v_w_up, v_w_down):
    args = dict(locals())
    wts = {n: args[n] for n in WEIGHTS}
    mom = {n: args["m_" + n] for n in WEIGHTS}
    var = {n: args["v_" + n] for n in WEIGHTS}
    chip = 2 * lax.axis_index("x") + lax.axis_index("y")
    core = lax.axis_index("c").astype(jnp.int32).reshape(1)

    med_names = [n for n in MED if n != "conv_w"]
    med_pieces = [wts[n] for n in med_names] + list(_split3(wts["conv_w"]))
    med_shapes = [p.shape for p in med_pieces]
    shards_first = [wts[n][0].astype(bf16) for n in STAGES[0]] + [_pack(med_pieces, 128, bf16, 32)]
    got_first, token = gather_weights(shards_first, name="gather_l0_s0")
    is_own = (jnp.arange(4) == chip).reshape(4, 1, 1)

    def with_own(slabs, shards):
        return [jnp.where(is_own, s[None], g) for g, s in zip(slabs, shards)]

    got_first = with_own(got_first, shards_first)
    in_flight = {}
    for key, names, l in (("l0_s1", STAGES[1], 0), ("l0_s2", STAGES[2], 0), ("l1", BIG, 1)):
        group = [wts[n][l].astype(bf16) for n in names]
        group[0] = group[0] + token[0, 0].astype(bf16)
        in_flight[key] = (names, gather_start(group, name="gather_start_" + key))
        token = in_flight[key][1][4]

    def whole_of(names, slabs):
        out = {}
        for n, by_chip in zip(names, slabs):
            _, rows, cols = by_chip.shape
            if n in ROW_SHARDED:
                out[n] = by_chip.reshape(4 * rows, cols)
            else:
                out[n] = by_chip.transpose(1, 0, 2).reshape(rows, 4 * cols)
        return out

    landed = {}

    def big_of(l, stage, after):
        if (l, stage) == (0, 0):
            return whole_of(STAGES[0], got_first)
        key = "l1" if l == 1 else f"l0_s{stage}"
        if key not in landed:
            names, (send_sems, recv_sems, thru, lands, _) = in_flight[key]
            thru, lands = gather_wait(send_sems, recv_sems, thru, lands, after, name="gather_wait_" + key)
            lands = gather_forward(lands, name="gather_forward_" + key)
            landed[key] = whole_of(names, with_own(lands, thru))
        return {n: landed[key][n] for n in STAGES[stage]}

    whole = {}
    per_chip = [_unpack(got_first[len(STAGES[0])][j], med_shapes) for j in range(4)]
    for q, n in enumerate(med_names):
        whole[n] = jnp.concatenate([per_chip[j][q] for j in range(4)], axis=_shard_axis(n)).astype(f32)
    base = len(med_names)
    cw_parts = [jnp.concatenate([per_chip[j][base + t] for j in range(4)], axis=2).astype(f32) for t in range(3)]
    whole["conv_w"] = (cw_parts[0] + cw_parts[1]) + cw_parts[2]
    small = {n: wts[n] for n in SMALL}
    small["rwkv_r_k"] = wts["rwkv_r_k"].reshape(DEPTH, RW)

    exchanges = []

    def on_grads(l, stage, slabs):
        names = STAGES[stage]
        tag = f"l{l}_s{stage}"
        parts = [slabs[n] for n in names]
        from_sibling = grads_to_sibling(parts, name="grads_to_sibling_" + tag)
        chip_sums = [pair_sum(s, t, core, name=f"pair_sum_{n}_{l}") for n, s, t in zip(names, parts, from_sibling)]
        started = scatter_start(chip_sums, name="scatter_start_" + tag)
        exchanges.append((l, names, tag, started))
        return started[4]

    broadcasts = []

    def on_small(l, layer_small):
        values = [val for _, _, val in layer_small]
        started = broadcast_start(_pack(values, 128, f32, 8), name=f"small_start_l{l}")
        broadcasts.append((l, [(n, index, val.shape) for n, index, val in layer_small], started))
        return started[4]

    small["attn_norm"] = small["attn_norm"] + token[0, 0]
    _, grad_x, grads = local_step(x[0], positions[0], loss_target[0], whole, small, big_of, on_grads, on_small)

    device = (4 * lax.axis_index("x") + 2 * lax.axis_index("y") + lax.axis_index("c")).astype(jnp.int32).reshape(1)
    summed = {n: [None] * len(grads[n]) for n in SMALL + MED}
    summed["loss"] = [None]
    for l, entries, (send_sems, recv_sems, thru, land, _) in broadcasts:
        own, land = broadcast_wait(send_sems, recv_sems, thru, land, grad_x, name=f"small_wait_l{l}")
        total = sum8(land, own, device, name=f"small_sum_l{l}")
        for (n, index, _), val in zip(entries, _unpack(total, [shape for _, _, shape in entries])):
            summed[n][index] = val
    gsum = {}
    for n in SMALL + MED:
        g = jnp.stack(summed[n])
        if n in MED:
            ax = _shard_axis(n)
            width = wts[n].shape[ax]
            g = lax.dynamic_slice_in_dim(g, chip * width, width, axis=ax)
        gsum[n] = g.reshape(wts[n].shape)
    where = jnp.stack([lax.axis_index("c"), chip]).astype(jnp.int32)
    bufs, layout = {}, []
    for l, names, tag, (send_sems, recv_sems, thru, lands, _) in exchanges:
        own, arrived = scatter_wait(send_sems, recv_sems, thru, lands, grad_x, name="scatter_wait_" + tag)
        for n, mine, theirs in zip(names, own, arrived):
            rows = 2 * theirs.shape[1]
            bufs[n] = sum4_into(theirs, mine, bufs.get(n), where, layer=l, total_rows=DEPTH * rows,
                                name=f"sum_chips_{n}_{l}")
            layout.append((BIG.index(n), l * rows, rows))
    reduced = join_halves([bufs[n] for n in BIG], layout, name="join_halves")

    out_g, out_d, out_m, out_v = {}, {}, {}, {}
    for q, n in enumerate(BIG):
        shp = wts[n].shape
        as2d = lambda a: a.reshape(-1, shp[-1])
        g2d = w_in_window_cols(reduced[q], chip) if n == "w_in" else reduced[q]
        res = adamw(as2d(wts[n]), g2d, as2d(mom[n]), as2d(var[n]), name="adamw_" + n)
        out_g[n], out_d[n], out_m[n], out_v[n] = [r.reshape(shp) for r in res]
    sm_all = SMALL + MED
    flat2d = lambda a: a.reshape(-1, a.shape[-1])
    res = adamw_many([flat2d(wts[n]) for n in sm_all], [flat2d(gsum[n]) for n in sm_all],
                     [flat2d(mom[n]) for n in sm_all], [flat2d(var[n]) for n in sm_all], name="adamw_small")
    for tgt, vals in zip((out_d, out_m, out_v), res):
        for n, val in zip(sm_all, vals):
            tgt[n] = val.reshape(wts[n].shape)
    out_g.update({n: gsum[n] for n in sm_all})
    loss = summed["loss"][0].reshape(())
    return (loss, grad_x[None], *[out_g[n] for n in WEIGHTS], *[out_d[n] for n in WEIGHTS],
            *[out_m[n] for n in WEIGHTS], *[out_v[n] for n in WEIGHTS])
```

```python
import functools

import jax
import jax.numpy as jnp
import numpy as np
from jax import lax
from jax.experimental import pallas as pl
from jax.experimental.pallas import tpu as pltpu

f32, bf16 = jnp.float32, jnp.bfloat16
HI = lax.Precision.HIGHEST
MESH = pl.DeviceIdType.MESH

D = 1024
DEPTH = 2
MLA_H, NOPE, ROPE, DQK, DV = 8, 64, 32, 96, 64
QL, KVL = 384, 256
RW, RH, RN = 256, 4, 64
DL, AL, GL, MVL = 64, 64, 128, 32
CW = 256
DFF = 4096
GATE = 3 * D
MLA_COLS = QL + KVL + ROPE
MLA_PAD = 768
Q_HEAD_PAD = 128
NORM_EPS = 1e-6
GN_EPS = 64e-5
ROPE_THETA = 10000.0
LR, B1, B2, EPS, WD, STEP = 0.001, 0.9, 0.999, 1e-08, 0.01, 10

VMEM_LIMIT = 52 * 1024 * 1024
WKV_CHUNK = 64
WKV_CHUNKS_PER_STEP = 4
ATTN_SEGMENTS = 4

BIG = ["w_in", "mla_wq_b", "mla_wkv_b", "mla_w_o", "rwkv_w_o", "conv_w_o", "w_out", "w_up", "w_down"]
MED = ["rwkv_w2", "rwkv_a2", "rwkv_g2", "rwkv_v1", "rwkv_v2", "conv_w"]
ROW_SHARDED = {"w_out", "w_down", "rwkv_v1"}
SMALL = ["attn_norm", "mla_q_a_norm", "mla_kv_a_norm", "mla_q_norm", "mla_k_norm", "rwkv_mu", "rwkv_w0",
         "rwkv_a0", "rwkv_k_k", "rwkv_k_a", "rwkv_r_k", "rwkv_ln_w", "rwkv_ln_b", "rwkv_v_mu", "rwkv_v0",
         "mlp_norm"]
WEIGHTS = ["attn_norm", "w_in", "mla_q_a_norm", "mla_wq_b", "mla_kv_a_norm", "mla_wkv_b", "mla_q_norm",
           "mla_k_norm", "mla_w_o", "rwkv_mu", "rwkv_w0", "rwkv_w2", "rwkv_a0", "rwkv_a2", "rwkv_g2",
           "rwkv_k_k", "rwkv_k_a", "rwkv_r_k", "rwkv_ln_w", "rwkv_ln_b", "rwkv_w_o", "rwkv_v1", "rwkv_v_mu",
           "rwkv_v0", "rwkv_v2", "conv_w", "conv_w_o", "w_out", "mlp_norm", "w_up", "w_down"]


def _cparams(sem=None):
    return pltpu.CompilerParams(dimension_semantics=sem, vmem_limit_bytes=VMEM_LIMIT)


def _pick(dim, pref, align):
    if dim <= pref:
        return dim
    t = (pref // align) * align
    while t >= align:
        if dim % t == 0:
            return t
        t -= align
    return dim


def _bdot(a, b, dims):
    return lax.dot_general(a.astype(bf16), b.astype(bf16), (dims, ((), ())), preferred_element_type=f32)


@jax.custom_vjp
def _mm(a, b):
    return _bdot(a, b, ((1,), (0,)))


def _mm_fwd(a, b):
    return _mm(a, b), (a, b)


def _mm_bwd(res, g):
    a, b = res
    return _bdot(g, b, ((1,), (1,))), _bdot(a, g, ((0,), (0,)))


_mm.defvjp(_mm_fwd, _mm_bwd)


@jax.custom_vjp
def _mm_nt(a, b):
    return _bdot(a, b, ((1,), (1,)))


def _mm_nt_fwd(a, b):
    return _mm_nt(a, b), (a, b)


def _mm_nt_bwd(res, g):
    a, b = res
    return _bdot(g, b, ((1,), (0,))), _bdot(g, a, ((0,), (0,)))


_mm_nt.defvjp(_mm_nt_fwd, _mm_nt_bwd)


_NN, _NT, _TN = ((1,), (0,)), ((1,), (1,)), ((0,), (0,))


def _dg(a, b, dims):
    return lax.dot_general(a, b, (dims, ((), ())), preferred_element_type=f32)


def _bf16_pieces(x, count):
    out, rest = [], x
    for q in range(count):
        piece = rest.astype(bf16)
        out.append(piece)
        if q + 1 < count:
            rest = rest - piece.astype(f32)
    return out


def _dot3(a, b, dims):
    (ah, al), (bh, bl) = _bf16_pieces(a, 2), _bf16_pieces(b, 2)
    return _dg(ah, bh, dims) + (_dg(ah, bl, dims) + _dg(al, bh, dims))


@jax.custom_vjp
def _hdot(a, b):
    return _dot3(a, b, _NN)


@jax.custom_vjp
def _hdot_nt(a, b):
    return _dot3(a, b, _NT)


@jax.custom_vjp
def _hdot_tn(a, b):
    return _dot3(a, b, _TN)


_hdot.defvjp(lambda a, b: (_hdot(a, b), (a, b)), lambda res, g: (_hdot_nt(g, res[1]), _hdot_tn(res[0], g)))
_hdot_nt.defvjp(lambda a, b: (_hdot_nt(a, b), (a, b)), lambda res, g: (_hdot(g, res[1]), _hdot_tn(g, res[0])))
_hdot_tn.defvjp(lambda a, b: (_hdot_tn(a, b), (a, b)), lambda res, g: (_hdot_nt(res[1], g), _hdot(res[0], g)))


_BNN, _BNT, _BTN = ((2,), (1,)), ((2,), (2,)), ((1,), (1,))


def _bdg(a, b, dims):
    return lax.dot_general(a, b, (dims, ((0,), (0,))), preferred_element_type=f32)


def _bdot3(a, b, dims):
    (ah, al), (bh, bl) = _bf16_pieces(a, 2), _bf16_pieces(b, 2)
    return _bdg(ah, bh, dims) + (_bdg(ah, bl, dims) + _bdg(al, bh, dims))


@jax.custom_vjp
def _hbnn(a, b):
    return _bdot3(a, b, _BNN)


@jax.custom_vjp
def _hbnt(a, b):
    return _bdot3(a, b, _BNT)


@jax.custom_vjp
def _hbtn(a, b):
    return _bdot3(a, b, _BTN)


_hbnn.defvjp(lambda a, b: (_hbnn(a, b), (a, b)), lambda res, g: (_hbnt(g, res[1]), _hbtn(res[0], g)))
_hbnt.defvjp(lambda a, b: (_hbnt(a, b), (a, b)), lambda res, g: (_hbnn(g, res[1]), _hbtn(g, res[0])))
_hbtn.defvjp(lambda a, b: (_hbtn(a, b), (a, b)), lambda res, g: (_hbnt(res[1], g), _hbnn(res[0], g)))


@functools.partial(jax.custom_vjp, nondiff_argnums=(2,))
def _exact_bl(m, x, transposed):
    mb = m.astype(bf16)
    hi, mid, lo = _bf16_pieces(x, 3)
    dims = _BTN if transposed else _BNN
    return (_bdg(mb, hi, dims) + _bdg(mb, mid, dims)) + _bdg(mb, lo, dims)


_exact_bl.defvjp(lambda m, x, transposed: (_exact_bl(m, x, transposed), m),
                 lambda transposed, m, g: (jnp.zeros_like(m), _exact_bl(m, g, not transposed)))


@functools.partial(jax.custom_vjp, nondiff_argnums=(2,))
def _exact_l(m, x, transposed):
    mb = m.astype(bf16)
    hi, mid, lo = _bf16_pieces(x, 3)
    dims = _TN if transposed else _NN
    return (_dg(mb, hi, dims) + _dg(mb, mid, dims)) + _dg(mb, lo, dims)


_exact_l.defvjp(lambda m, x, transposed: (_exact_l(m, x, transposed), m),
                lambda transposed, m, g: (jnp.zeros_like(m), _exact_l(m, g, not transposed)))


@functools.partial(jax.custom_vjp, nondiff_argnums=(2,))
def _exact_r(x, m, transposed):
    mb = m.astype(bf16)
    hi, mid, lo = _bf16_pieces(x, 3)
    dims = _NT if transposed else _NN
    return (_dg(hi, mb, dims) + _dg(mid, mb, dims)) + _dg(lo, mb, dims)


_exact_r.defvjp(lambda x, m, transposed: (_exact_r(x, m, transposed), m),
                lambda transposed, m, g: (_exact_r(g, m, not transposed), jnp.zeros_like(m)))


def _rms(x, g, eps=NORM_EPS):
    return x * lax.rsqrt(jnp.mean(x * x, axis=-1, keepdims=True) + eps) * g


def _sigmoid(x):
    return 1.0 / (1.0 + jnp.exp(-x))


def _softplus(x):
    return jnp.maximum(x, 0.0) + jnp.log(1.0 + jnp.exp(-jnp.maximum(x, -x)))


def _lane_split(x, sizes):
    bounds = np.cumsum([0] + list(sizes))

    @jax.custom_vjp
    def split(v):
        return tuple(v[..., int(bounds[q]):int(bounds[q + 1])] for q in range(len(sizes)))

    split.defvjp(lambda v: (split(v), None), lambda _, g: (jnp.concatenate(g, axis=-1),))
    return split(x)


def _row_split(x, sizes):
    bounds = np.cumsum([0] + list(sizes))

    @jax.custom_vjp
    def split(v):
        return tuple(v[..., int(bounds[q]):int(bounds[q + 1]), :] for q in range(len(sizes)))

    split.defvjp(lambda v: (split(v), None), lambda _, g: (jnp.concatenate(g, axis=-2),))
    return split(x)


def _shift_mats(t, k):
    r = lax.broadcasted_iota(jnp.int32, (t, t), 0)
    c = lax.broadcasted_iota(jnp.int32, (t, t), 1)
    inner = (r - c == k).astype(f32)
    r8 = lax.broadcasted_iota(jnp.int32, (t, 8), 0)
    c8 = lax.broadcasted_iota(jnp.int32, (t, 8), 1)
    edge = (c8 - r8 == 8 - k).astype(f32)
    return inner, edge


def _shift(x, halo, k):
    inner, edge = _shift_mats(x.shape[0], k)
    return _exact_l(inner, x, False) + jnp.dot(edge, halo, precision=HI, preferred_element_type=f32)


def mm(a, b, *, name, ta=False, tb=False, a_batched=False, b_batched=False, reduce_batch=False, add=None,
       act_grad=None, relu2_out=False, n_split=1, out_dtype=f32, tm=1024, tn=1024, tk=2048):
    ash, bsh = a.shape[-2:], b.shape[-2:]
    (k_, m_) = ash if ta else ash[::-1]
    (k2_, n_) = bsh[::-1] if tb else bsh
    assert k_ == k2_, (a.shape, b.shape, ta, tb)
    hb = a.shape[0] if a_batched else (b.shape[0] if b_batched else 1)
    batched_out = (a_batched or b_batched) and not reduce_batch
    h_out = hb if batched_out else 1
    h_red = hb if reduce_batch else 1
    tm = _pick(m_, tm, 128)
    tn = _pick(n_ // n_split, tn, 128)
    tk = _pick(k_, tk, 128)
    nm, nn, nk = m_ // tm, n_ // tn, k_ // tk

    def a_map(i, j, ho, hr, kk):
        blk = (kk, i) if ta else (i, kk)
        return ((ho if batched_out else hr),) + blk if a_batched else blk

    def b_map(i, j, ho, hr, kk):
        blk = (j, kk) if tb else (kk, j)
        return ((ho if batched_out else hr),) + blk if b_batched else blk

    a_blk = (tk, tm) if ta else (tm, tk)
    b_blk = (tn, tk) if tb else (tk, tn)
    in_specs = [pl.BlockSpec(((1,) + a_blk) if a_batched else a_blk, a_map),
                pl.BlockSpec(((1,) + b_blk) if b_batched else b_blk, b_map)]
    args = [a, b]
    for extra in (add, act_grad):
        if extra is not None:
            in_specs.append(pl.BlockSpec((tm, tn), lambda i, j, ho, hr, kk: (i, j)))
            args.append(extra)
    if n_split > 1:
        per = n_ // n_split // tn
        if batched_out:
            out_spec = pl.BlockSpec((1, 1, tm, tn), lambda i, j, ho, hr, kk: (j // per, ho, i, j % per))
            out_shape = jax.ShapeDtypeStruct((n_split, hb, m_, n_ // n_split), out_dtype)
        else:
            out_spec = pl.BlockSpec((1, tm, tn), lambda i, j, ho, hr, kk: (j // per, i, j % per))
            out_shape = jax.ShapeDtypeStruct((n_split, m_, n_ // n_split), out_dtype)
    elif batched_out:
        out_spec = pl.BlockSpec((1, tm, tn), lambda i, j, ho, hr, kk: (ho, i, j))
        out_shape = jax.ShapeDtypeStruct((hb, m_, n_), out_dtype)
    else:
        out_spec = pl.BlockSpec((tm, tn), lambda i, j, ho, hr, kk: (i, j))
        out_shape = jax.ShapeDtypeStruct((m_, n_), out_dtype)
    lead = (0,) * (int(batched_out) + int(n_split > 1))
    dims = ((0,) if ta else (1,), (1,) if tb else (0,))
    has_add, has_act = add is not None, act_grad is not None

    def body(*refs):
        a_ref, b_ref = refs[0], refs[1]
        pos = 2
        add_ref = act_ref = None
        if has_add:
            add_ref = refs[pos]
            pos += 1
        if has_act:
            act_ref = refs[pos]
            pos += 1
        o_ref, acc_ref = refs[pos], refs[-1]
        hr, kk = pl.program_id(3), pl.program_id(4)
        first = jnp.logical_and(hr == 0, kk == 0)
        last = jnp.logical_and(hr == h_red - 1, kk == nk - 1)
        av = a_ref[0] if a_batched else a_ref[...]
        bv = b_ref[0] if b_batched else b_ref[...]
        p = _bdot(av, bv, dims)
        single = h_red * nk == 1

        if not single:
            @pl.when(first)
            def _():
                acc_ref[...] = p

            @pl.when(jnp.logical_not(first))
            def _():
                acc_ref[...] += p

        @pl.when(last)
        def _():
            r = p if single else acc_ref[...]
            if has_act:
                r = r * (2.0 * jnp.maximum(act_ref[...], 0.0))
            if has_add:
                r = r + add_ref[...]
            if lead:
                o_ref[lead] = r.astype(out_dtype)
            else:
                o_ref[...] = r.astype(out_dtype)
            if relu2_out:
                refs[pos + 1][...] = jnp.square(jnp.maximum(r, 0.0)).astype(bf16)

    if relu2_out:
        assert not lead
        out_spec = [out_spec, out_spec]
        out_shape = [out_shape, jax.ShapeDtypeStruct(out_shape.shape, bf16)]
    return pl.pallas_call(
        body, name=name, grid=(nm, nn, h_out, h_red, nk), in_specs=in_specs, out_specs=out_spec,
        out_shape=out_shape, scratch_shapes=[pltpu.VMEM((tm, tn), f32)],
        compiler_params=_cparams(("parallel", "parallel", "parallel", "arbitrary", "arbitrary")),
    )(*args)


def _row_spec(arr, tile, idx):
    if arr.ndim == 2:
        return pl.BlockSpec((tile, arr.shape[1]), lambda i: (idx(i), 0))
    return pl.BlockSpec((arr.shape[0], tile, arr.shape[2]), lambda i: (0, idx(i), 0))


def _halo_spec(arr, tile, idx):
    per = tile // 8
    return pl.BlockSpec((8, arr.shape[1]), lambda i: (jnp.maximum(idx(i) * per - 1, 0), 0))


def _full_spec(arr):
    nd = arr.ndim
    return pl.BlockSpec(arr.shape, lambda i: (0,) * nd)


def _load_f32(ref):
    val = ref[...]
    return val.astype(f32) if val.dtype == bf16 else val


def rows_fwd(fn, rows, params, consts, out_shapes, *, tile, name, halos=()):
    s_len = rows[0].shape[-2]
    n = s_len // tile
    nr, nh, npar, nc = len(rows), len(halos), len(params), len(consts)
    ident = lambda i: i
    in_specs = ([_row_spec(r, tile, ident) for r in rows] + [_halo_spec(rows[h], tile, ident) for h in halos]
                + [_full_spec(p) for p in params] + [_full_spec(c) for c in consts])
    out_specs = [_row_spec(o, tile, ident) for o in out_shapes]

    def body(*refs):
        i = pl.program_id(0)
        rv = [_load_f32(r) for r in refs[:nr]]
        keep = (i > 0).astype(f32)
        hv = [r[...] * keep for r in refs[nr:nr + nh]]
        pv = [r[...] for r in refs[nr + nh:nr + nh + npar]]
        cv = [r[...] for r in refs[nr + nh + npar:nr + nh + npar + nc]]
        outs = fn(rv, hv, pv, cv)
        for o_ref, o in zip(refs[nr + nh + npar + nc:], outs):
            o_ref[...] = o.astype(o_ref.dtype)

    return pl.pallas_call(
        body, name=name, grid=(n,), in_specs=in_specs, out_specs=out_specs, out_shape=list(out_shapes),
        compiler_params=_cparams(("arbitrary",)),
    )(*rows, *[rows[h] for h in halos], *params, *consts)


def rows_bwd(fn, rows, params, consts, douts, *, tile, name, halos=(), grad_rows=None, extra=None,
             grad_dtypes=None):
    s_len = rows[0].shape[-2]
    n = s_len // tile
    nr, nh, npar, nc = len(rows), len(halos), len(params), len(consts)
    grad_rows = list(range(nr)) if grad_rows is None else list(grad_rows)
    extra = extra or {}
    assert all(h in grad_rows for h in halos)
    rev = lambda i: n - 1 - i
    dflat = [d for ds in douts for d in ds]
    dcount = [len(ds) for ds in douts]
    eflat = [e for g in grad_rows for e in extra.get(g, [])]
    ecount = [len(extra.get(g, [])) for g in grad_rows]
    in_specs = ([_row_spec(r, tile, rev) for r in rows] + [_halo_spec(rows[h], tile, rev) for h in halos]
                + [_full_spec(p) for p in params] + [_full_spec(c) for c in consts]
                + [_row_spec(d, tile, rev) for d in dflat] + [_row_spec(e, tile, rev) for e in eflat])
    grad_dtypes = [f32] * len(grad_rows) if grad_dtypes is None else list(grad_dtypes)
    assert all(grad_dtypes[q] == f32 for q, g in enumerate(grad_rows) if g in halos)
    out_shapes = ([jax.ShapeDtypeStruct(rows[g].shape, dt) for g, dt in zip(grad_rows, grad_dtypes)]
                  + [jax.ShapeDtypeStruct(p.shape, f32) for p in params])
    out_specs = [_row_spec(rows[g], tile, rev) for g in grad_rows] + [_full_spec(p) for p in params]
    scratch = [pltpu.VMEM((8, rows[h].shape[1]), f32) for h in halos]
    n_in = nr + nh + npar + nc + len(dflat) + len(eflat)
    n_out = len(grad_rows) + npar

    def body(*refs):
        i = pl.program_id(0)
        rv = [_load_f32(r) for r in refs[:nr]]
        keep = (i < n - 1).astype(f32)
        hv = [r[...] * keep for r in refs[nr:nr + nh]]
        pv = [r[...] for r in refs[nr + nh:nr + nh + npar]]
        pos = nr + nh + npar
        cv = [r[...] for r in refs[pos:pos + nc]]
        pos += nc
        dv = []
        for cnt in dcount:
            acc = _load_f32(refs[pos])
            for q in range(1, cnt):
                acc = acc + _load_f32(refs[pos + q])
            dv.append(acc)
            pos += cnt
        ev = []
        for cnt in ecount:
            ev.append([_load_f32(refs[pos + q]) for q in range(cnt)])
            pos += cnt
        out_refs = refs[n_in:n_in + n_out]
        carry_refs = refs[n_in + n_out:]

        def f(gr, gh, gp):
            full = list(rv)
            for g, val in zip(grad_rows, gr):
                full[g] = val
            return tuple(fn(full, gh, gp, cv))

        _, vjp = jax.vjp(f, [rv[g] for g in grad_rows], hv, pv)
        d_rows, d_halos, d_params = vjp(tuple(dv))

        @pl.when(i == 0)
        def _():
            for c_ref in carry_refs:
                c_ref[...] = jnp.zeros_like(c_ref)
            for p_ref in out_refs[len(grad_rows):]:
                p_ref[...] = jnp.zeros_like(p_ref)

        for q, g in enumerate(grad_rows):
            val = d_rows[q]
            for e in ev[q]:
                val = val + e
            out_refs[q][...] = val.astype(out_refs[q].dtype)
            if g in halos:
                hq = list(halos).index(g)
                out_refs[q][tile - 8:tile, :] += carry_refs[hq][...]
                carry_refs[hq][...] = d_halos[hq]
        for p_ref, dp in zip(out_refs[len(grad_rows):], d_params):
            p_ref[...] += dp

    res = pl.pallas_call(
        body, name=name, grid=(n,), in_specs=in_specs, out_specs=out_specs, out_shape=out_shapes,
        scratch_shapes=scratch, compiler_params=_cparams(("arbitrary",)),
    )(*rows, *[rows[h] for h in halos], *params, *consts, *dflat, *eflat)
    return list(res[:len(grad_rows)]), list(res[len(grad_rows):])


def _fn_norm(rows, halos, params, consts):
    return (_rms(rows[0], params[0]),)


def _fn_mla_prep(rows, halos, params, consts):
    cq, ckv, kpe = _lane_split(rows[0], (QL, KVL, MLA_PAD - QL - KVL))
    return _rms(cq, params[0]), _rms(ckv, params[1]), kpe


def _rope(x, cos, sin, rot):
    return x * cos + _exact_r(x, rot, False) * sin


def _fn_qk_post(rows, halos, params, consts):
    q_flat, kv_flat, kpe, cos, sin = rows
    q_norm, k_norm = params
    (rot,) = consts
    q_heads = _lane_split(q_flat, (DQK, Q_HEAD_PAD - DQK) * MLA_H)[::2]
    kv_heads = _lane_split(kv_flat, (NOPE, DV) * MLA_H)
    k_pe, _ = _lane_split(kpe, (ROPE, kpe.shape[1] - ROPE))
    qs = [_rope(_rms(qh, q_norm), cos, sin, rot) for qh in q_heads]
    ks = [_rope(_rms(jnp.concatenate([kv_heads[2 * h], k_pe], axis=-1), k_norm), cos, sin, rot)
          for h in range(MLA_H)]
    vs = [kv_heads[2 * h + 1] for h in range(MLA_H)]
    return jnp.stack(qs, axis=0), jnp.stack(ks, axis=0), jnp.stack(vs, axis=0)


def _seg(x, bd):
    return _exact_r(x, bd, False)


def _make_fn_rwkv_prep(vres):
    def fn(rows, halos, params, consts):
        cols = rows[0]
        bd = consts[0]
        mu, w0, w2, a0, a2, g2, k_k, k_a = params[:8]
        prev = _shift(cols, halos[0], 1)
        c = cols + (prev - cols) * mu
        r, k, v, xw, xa, xg = _lane_split(c, (RW, RW, RW, DL, AL, GL))
        log_w = -_softplus(-(w0 + _mm(jnp.tanh(xw), w2))) - 0.5
        ld = -jnp.exp(log_w)
        a = _sigmoid(a0 + _mm(xa, a2))
        g = _mm(_sigmoid(xg), g2)
        if vres:
            hcur, v_first = rows[1], rows[2]
            v1, v_mu, v0, v2 = params[8:12]
            xv = _mm(hcur, v1)
            xv_prev = _shift(xv, _mm(halos[1], v1), 1)
            xv = xv + (xv_prev - xv) * v_mu
            v = v + (v_first - v) * _sigmoid(v0 + _mm(xv, v2))
        kk = k * k_k
        kk = kk / jnp.maximum(jnp.sqrt(_seg(kk * kk, bd)), 1e-12)
        k2 = k * (1.0 + (a - 1.0) * k_a)
        return r, ld, k2, v, -kk, kk * a, g
    return fn


def _fn_rwkv_post(rows, halos, params, consts):
    y, r, k2, v, g = rows
    ln_w, ln_b, r_k = params
    bd = consts[0]
    mean = _seg(y, bd) * (1.0 / RN)
    d = y - mean
    var = _seg(d * d, bd) * (1.0 / RN)
    yn = d * lax.rsqrt(var + GN_EPS) * ln_w + ln_b
    bonus = _seg(r * k2 * r_k, bd) * v
    return ((yn + bonus) * g,)


def _fn_conv(rows, halos, params, consts):
    cols, halo = rows[0], halos[0]
    w0, w1, w2 = params
    b, c, x = _lane_split(cols, (CW, CW, CW))
    _, ch, xh = _lane_split(halo, (CW, CW, CW))
    u, uh = c * x, ch * xh
    return (b * (w0 * _shift(u, uh, 2) + w1 * _shift(u, uh, 1) + w2 * u),)


def _fn_merge(rows, halos, params, consts):
    gate, o_a, o_b, o_c = rows
    g_a, g_b, g_c = _lane_split(gate, (D, D, D))
    return (_sigmoid(g_a) * o_a + _sigmoid(g_b) * o_b + _sigmoid(g_c) * o_c,)


def _attn_block(q, k, v, q0, diagonal_last):
    tq, kend = q.shape[0], k.shape[0]
    s = _mm_nt(q, k) * (DQK ** -0.5)
    if diagonal_last:
        tri = lax.broadcasted_iota(jnp.int32, (tq, tq), 0) >= lax.broadcasted_iota(jnp.int32, (tq, tq), 1)
        if kend > tq:
            before, diag = _lane_split(s, (kend - tq, tq))
            s = jnp.concatenate([before, jnp.where(tri, diag, -1e30)], axis=-1)
        else:
            s = jnp.where(tri, s, -1e30)
    else:
        row = q0 + lax.broadcasted_iota(jnp.int32, (tq, kend), 0)
        col = lax.broadcasted_iota(jnp.int32, (tq, kend), 1)
        s = jnp.where(row >= col, s, -1e30)
    m = lax.stop_gradient(jnp.max(s, axis=-1, keepdims=True))
    e = jnp.exp(s - m)
    p = e / jnp.sum(e, axis=-1, keepdims=True)
    return _mm(p, v)


def _attn_segments(s_len, tq):
    per = max(1, s_len // tq // ATTN_SEGMENTS)
    return [(first, per, (first + per) * tq) for first in range(0, s_len // tq, per)]


HEAD_PAIR = 2


def attn_fwd(q, k, v, *, tq, name):
    h, s_len, _ = q.shape
    outs = []
    for seg, (first, nq, kend) in enumerate(_attn_segments(s_len, tq)):
        def body(q_ref, k_ref, v_ref, o_ref, first=first, nq=nq):
            q0 = (first + pl.program_id(1)) * tq
            o = [_attn_block(q_ref[j], k_ref[j], v_ref[j], q0, nq == 1) for j in range(HEAD_PAIR)]
            o_ref[...] = jnp.concatenate(o, axis=-1).astype(o_ref.dtype)

        outs.append(pl.pallas_call(
            body, name=f"{name}_{seg}", grid=(h // HEAD_PAIR, nq),
            in_specs=[pl.BlockSpec((HEAD_PAIR, tq, DQK), lambda hp, i, first=first: (hp, first + i, 0)),
                      pl.BlockSpec((HEAD_PAIR, kend, DQK), lambda hp, i: (hp, 0, 0)),
                      pl.BlockSpec((HEAD_PAIR, kend, DV), lambda hp, i: (hp, 0, 0))],
            out_specs=pl.BlockSpec((tq, HEAD_PAIR * DV), lambda hp, i: (i, hp)),
            out_shape=jax.ShapeDtypeStruct((nq * tq, h * DV), bf16),
            compiler_params=_cparams(("parallel", "arbitrary")),
        )(q, k, v))
    return jnp.concatenate(outs, axis=0)


def attn_bwd(q, k, v, do, *, tq, name):
    h, s_len, _ = q.shape
    dqs, dk_acc, dv_acc = [], None, None
    for seg, (first, nq, kend) in reversed(list(enumerate(_attn_segments(s_len, tq)))):
        carried = dk_acc is not None

        def body(*refs, first=first, carried=carried, nq=nq):
            q_ref, k_ref, v_ref, do_ref = refs[:4]
            dq_ref, dk_ref, dv_ref = refs[-3:]
            i = pl.program_id(1)
            do_heads = _lane_split(do_ref[...], (DV,) * HEAD_PAIR)
            for j in range(HEAD_PAIR):
                _, vjp = jax.vjp(functools.partial(_attn_block, q0=(first + i) * tq, diagonal_last=nq == 1),
                                 q_ref[j], k_ref[j], v_ref[j])
                dq, dk, dv = vjp(do_heads[j])
                dq_ref[j] = dq

                @pl.when(i == 0)
                def _():
                    dk_ref[j] = dk + refs[4][j] if carried else dk
                    dv_ref[j] = dv + refs[5][j] if carried else dv

                @pl.when(i > 0)
                def _():
                    dk_ref[j] += dk
                    dv_ref[j] += dv

        key_specs = [pl.BlockSpec((HEAD_PAIR, kend, DQK), lambda hp, i: (hp, 0, 0)),
                     pl.BlockSpec((HEAD_PAIR, kend, DV), lambda hp, i: (hp, 0, 0))]
        dq, dk_acc, dv_acc = pl.pallas_call(
            body, name=f"{name}_{seg}", grid=(h // HEAD_PAIR, nq),
            in_specs=[pl.BlockSpec((HEAD_PAIR, tq, DQK), lambda hp, i, first=first: (hp, first + i, 0))] + key_specs
            + [pl.BlockSpec((tq, HEAD_PAIR * DV), lambda hp, i, first=first: (first + i, hp))]
            + (key_specs if carried else []),
            out_specs=[pl.BlockSpec((HEAD_PAIR, tq, DQK), lambda hp, i: (hp, i, 0))] + key_specs,
            out_shape=[jax.ShapeDtypeStruct((h, nq * tq, DQK), f32), jax.ShapeDtypeStruct((h, s_len, DQK), f32),
                       jax.ShapeDtypeStruct((h, s_len, DV), f32)],
            input_output_aliases={4: 1, 5: 2} if carried else {},
            compiler_params=_cparams(("parallel", "arbitrary")),
        )(q, k, v, do, *([dk_acc, dv_acc] if carried else []))
        dqs.append(dq)
    return jnp.concatenate(dqs[::-1], axis=1), dk_acc, dv_acc


def _wkv_local(r, ld, k, v, a, b):
    nb, c, n = r.shape
    ri = lax.broadcasted_iota(jnp.int32, (c, c), 0)
    ci = lax.broadcasted_iota(jnp.int32, (c, c), 1)
    tri = jnp.broadcast_to((ri >= ci).astype(f32)[None], (nb, c, c))
    cum = _exact_bl(tri, ld, False)
    tot = jnp.sum(ld, axis=1, keepdims=True)
    w_incl, w_excl, w_inv, w_rest = jnp.exp(cum), jnp.exp(cum - ld), jnp.exp(-cum), jnp.exp(tot - cum)
    ab, rb, bb, kb = a * w_excl, r * w_incl, b * w_inv, k * w_inv
    bw, kw = b * w_rest, k * w_rest
    r2 = lax.broadcasted_iota(jnp.int32, (2 * c, 2 * c), 0)
    c2 = lax.broadcasted_iota(jnp.int32, (2 * c, 2 * c), 1)
    t_of, s_of = jnp.where(r2 >= c, r2 - c, r2), jnp.where(c2 >= c, c2 - c, c2)
    keep = jnp.logical_or(t_of > s_of, jnp.logical_and(r2 >= c, t_of == s_of))
    pair = jnp.where(keep[None], _hbnt(jnp.concatenate([ab, rb], axis=1), jnp.concatenate([bb, kb], axis=1)), 0.0)
    on_b, on_k = _lane_split(pair, (c, c))
    l_ab, m_rb = _row_split(on_b, (c, c))
    l_ak_v, m_rk_v = _row_split(_hbnn(on_k, v), (c, c))
    x = jnp.concatenate([ab, l_ak_v], axis=-1)
    lp, span = l_ab, 1
    while span < c:
        x = x + _hbnn(lp, x)
        span *= 2
        if span < c:
            lp = _hbnn(lp, lp)
    via_b_r, via_b_y = _lane_split(_hbnn(m_rb, x), (n, n))
    r_hat = rb + via_b_r
    y0 = via_b_y + m_rk_v
    from_b_g, from_b_z = _row_split(_hbtn(x, bw), (n, n))
    eye = lax.broadcasted_iota(jnp.int32, (n, n), 0) == lax.broadcasted_iota(jnp.int32, (n, n), 1)
    g = jnp.where(eye[None], jnp.exp(tot), 0.0) + from_b_g
    z = from_b_z + _hbtn(v, kw)
    return r_hat, y0, g, z


def _head(h):
    return slice(RN * h, RN * (h + 1))


def _load_chunk_heads(ref, c, per):
    return jnp.stack([ref[c * q:c * (q + 1), _head(h)] for q in range(per) for h in range(RH)], axis=0)


def _store_chunk_heads(ref, val, c, per):
    for q in range(per):
        ref[c * q:c * (q + 1), :] = jnp.concatenate([val[q * RH + h] for h in range(RH)], axis=-1)


def wkv_fwd(r, ld, k, v, a, b, *, name):
    s_len = r.shape[0]
    c = WKV_CHUNK
    n = s_len // c
    per = min(WKV_CHUNKS_PER_STEP, n)
    rows = pl.BlockSpec((c * per, RW), lambda i: (i, 0))
    mats = pl.BlockSpec((per, RH, RN, RN), lambda i: (i, 0, 0, 0))
    rows_t, mats_t = jax.ShapeDtypeStruct((s_len, RW), f32), jax.ShapeDtypeStruct((n, RH, RN, RN), f32)

    def local_body(r_ref, ld_ref, k_ref, v_ref, a_ref, b_ref, rh_ref, y0_ref, g_ref, z_ref):
        r_hat, y0, g, z = _wkv_local(*[_load_chunk_heads(ref, c, per)
                                       for ref in (r_ref, ld_ref, k_ref, v_ref, a_ref, b_ref)])
        _store_chunk_heads(rh_ref, r_hat, c, per)
        _store_chunk_heads(y0_ref, y0, c, per)
        g_ref[...] = g.reshape(per, RH, RN, RN)
        z_ref[...] = z.reshape(per, RH, RN, RN)

    r_hat, y0, g, z = pl.pallas_call(
        local_body, name=name + "_local", grid=(n // per,), in_specs=[rows] * 6, out_specs=[rows, rows, mats, mats],
        out_shape=[rows_t, rows_t, mats_t, mats_t], compiler_params=_cparams(("parallel",)),
    )(r, ld, k, v, a, b)

    def scan_body(g_ref, z_ref, st_ref, s_sc):
        s_sc[...] = jnp.zeros_like(s_sc)

        @pl.loop(0, n)
        def _(i):
            s0 = s_sc[...]
            st_ref[i] = s0
            s_sc[...] = _hbnn(s0, g_ref[i]) + z_ref[i]

    vm = pl.BlockSpec(memory_space=pltpu.VMEM)
    states = pl.pallas_call(
        scan_body, name=name + "_scan", in_specs=[vm, vm], out_specs=vm, out_shape=mats_t,
        scratch_shapes=[pltpu.VMEM((RH, RN, RN), f32)],
        compiler_params=pltpu.CompilerParams(vmem_limit_bytes=VMEM_LIMIT),
    )(g, z)

    def out_body(rh_ref, y0_ref, st_ref, y_ref):
        y = _hbnt(_load_chunk_heads(rh_ref, c, per), st_ref[...].reshape(per * RH, RN, RN))
        _store_chunk_heads(y_ref, y, c, per)
        y_ref[...] += y0_ref[...]

    y = pl.pallas_call(
        out_body, name=name + "_out", grid=(n // per,), in_specs=[rows, rows, mats], out_specs=rows,
        out_shape=rows_t, compiler_params=_cparams(("parallel",)),
    )(r_hat, y0, states)
    return y, dict(r_hat=r_hat, g=g, states=states)


def wkv_bwd(r, ld, k, v, a, b, saved, dy, *, name):
    s_len = r.shape[0]
    c = WKV_CHUNK
    n = s_len // c
    per = min(WKV_CHUNKS_PER_STEP, n)
    rows = pl.BlockSpec((c * per, RW), lambda i: (i, 0))
    mats = pl.BlockSpec((per, RH, RN, RN), lambda i: (i, 0, 0, 0))
    rows_t, mats_t = jax.ShapeDtypeStruct((s_len, RW), f32), jax.ShapeDtypeStruct((n, RH, RN, RN), f32)

    def out_body(dy_ref, rh_ref, st_ref, drh_ref, dsy_ref):
        dyb = _load_chunk_heads(dy_ref, c, per)
        _store_chunk_heads(drh_ref, _hbnn(dyb, st_ref[...].reshape(per * RH, RN, RN)), c, per)
        dsy_ref[...] = _hbtn(dyb, _load_chunk_heads(rh_ref, c, per)).reshape(per, RH, RN, RN)

    d_rhat, ds_y = pl.pallas_call(
        out_body, name=name + "_out", grid=(n // per,), in_specs=[rows, rows, mats], out_specs=[rows, mats],
        out_shape=[rows_t, mats_t], compiler_params=_cparams(("parallel",)),
    )(dy, saved["r_hat"], saved["states"])

    def scan_body(dsy_ref, g_ref, st_ref, dg_ref, dz_ref, ds_sc):
        ds_sc[...] = jnp.zeros_like(ds_sc)

        @pl.loop(0, n)
        def _(i):
            cidx = n - 1 - i
            ds_next = ds_sc[...]
            dz_ref[cidx] = ds_next
            dg_ref[cidx] = _hbtn(st_ref[cidx], ds_next)
            ds_sc[...] = dsy_ref[cidx] + _hbnt(ds_next, g_ref[cidx])

    vm = pl.BlockSpec(memory_space=pltpu.VMEM)
    d_g, d_z = pl.pallas_call(
        scan_body, name=name + "_scan", in_specs=[vm, vm, vm], out_specs=[vm, vm], out_shape=[mats_t, mats_t],
        scratch_shapes=[pltpu.VMEM((RH, RN, RN), f32)],
        compiler_params=pltpu.CompilerParams(vmem_limit_bytes=VMEM_LIMIT),
    )(ds_y, saved["g"], saved["states"])

    def local_body(r_ref, ld_ref, k_ref, v_ref, a_ref, b_ref, drh_ref, dy_ref, dg_ref, dz_ref, *out_refs):
        _, vjp = jax.vjp(_wkv_local, *[_load_chunk_heads(ref, c, per)
                                       for ref in (r_ref, ld_ref, k_ref, v_ref, a_ref, b_ref)])
        grads = vjp((_load_chunk_heads(drh_ref, c, per), _load_chunk_heads(dy_ref, c, per),
                     dg_ref[...].reshape(per * RH, RN, RN), dz_ref[...].reshape(per * RH, RN, RN)))
        for o_ref, val in zip(out_refs, grads):
            _store_chunk_heads(o_ref, val, c, per)

    return pl.pallas_call(
        local_body, name=name + "_local", grid=(n // per,), in_specs=[rows] * 8 + [mats, mats], out_specs=[rows] * 6,
        out_shape=[rows_t] * 6, compiler_params=_cparams(("parallel",)),
    )(r, ld, k, v, a, b, d_rhat, dy, d_g, d_z)


def loss_head(y, target, *, tile, name):
    s_len, d = y.shape
    n = s_len // tile

    def body(y_ref, t_ref, dy_ref, l_ref):
        err = y_ref[...] - t_ref[...]
        dy_ref[...] = err * (1.0 / d)
        part = 0.5 * jnp.sum(jnp.mean(err * err, axis=-1, keepdims=True), axis=0, keepdims=True)

        @pl.when(pl.program_id(0) == 0)
        def _():
            l_ref[...] = jnp.zeros_like(l_ref)

        l_ref[...] += jnp.broadcast_to(part, l_ref.shape)

    bs = pl.BlockSpec((tile, d), lambda i: (i, 0))
    dy, l = pl.pallas_call(
        body, name=name, grid=(n,), in_specs=[bs, bs],
        out_specs=[bs, pl.BlockSpec((8, 128), lambda i: (0, 0))],
        out_shape=[jax.ShapeDtypeStruct((s_len, d), f32), jax.ShapeDtypeStruct((8, 128), f32)],
        compiler_params=_cparams(("arbitrary",)),
    )(y, target)
    return l[0, 0], dy


def _adamw_update(w, g, m, v):
    mn = B1 * m + (1.0 - B1) * g
    vn = B2 * v + (1.0 - B2) * (g * g)
    delta = -LR * ((mn / (1.0 - B1 ** STEP)) / (jnp.sqrt(vn / (1.0 - B2 ** STEP)) + EPS) + WD * w)
    return delta, mn, vn


def adamw(w, g, m, v, *, name):
    rows, cols = w.shape
    tile = _pick(rows, max(8, (2 * 1024 * 1024 // (4 * cols)) // 8 * 8), 8)

    def body(w_ref, g_ref, m_ref, v_ref, g_out, d_out, m_out, v_out):
        gv = g_ref[...]
        d_out[...], m_out[...], v_out[...] = _adamw_update(w_ref[...], gv, m_ref[...], v_ref[...])
        g_out[...] = gv

    bs = pl.BlockSpec((tile, cols), lambda i: (i, 0))
    return pl.pallas_call(
        body, name=name, grid=(rows // tile,), in_specs=[bs] * 4, out_specs=[bs] * 4,
        out_shape=[jax.ShapeDtypeStruct((rows, cols), f32)] * 4, compiler_params=_cparams(("parallel",)),
    )(w, g, m, v)


def adamw_many(ws, gs, ms, vs, *, name):
    n = len(ws)

    def body(*refs):
        ins, outs = refs[:4 * n], refs[4 * n:]
        for i in range(n):
            outs[i][...], outs[n + i][...], outs[2 * n + i][...] = _adamw_update(
                ins[i][...], ins[n + i][...], ins[2 * n + i][...], ins[3 * n + i][...])

    vm = pl.BlockSpec(memory_space=pltpu.VMEM)
    res = pl.pallas_call(
        body, name=name, in_specs=[vm] * (4 * n), out_specs=[vm] * (3 * n),
        out_shape=[jax.ShapeDtypeStruct(w.shape, f32) for w in ws] * 3,
        compiler_params=pltpu.CompilerParams(vmem_limit_bytes=VMEM_LIMIT),
    )(*ws, *gs, *ms, *vs)
    return list(res[:n]), list(res[n:2 * n]), list(res[2 * n:])


def _place():
    return lax.axis_index("x"), lax.axis_index("y"), lax.axis_index("c")


_ANY = pl.BlockSpec(memory_space=pl.ANY)


def _peer_chips(x, y):
    return [(1 - x, y), (x, 1 - y), (1 - x, 1 - y)]


def gather_weights(shards, *, name):
    nk = len(shards)

    def body(*refs):
        srcs, outs = refs[:nk], refs[nk:2 * nk]
        ici_send, ici_recv, d2d_send, d2d_recv = refs[2 * nk + 1:]
        x, y, c = _place()
        me = 2 * x + y
        peers = _peer_chips(x, y)
        pending = []
        for k in range(nk):
            half = srcs[k].shape[0] // 2
            mine = pl.ds(c * half, half)
            for p, (px, py) in enumerate(peers):
                cp = pltpu.make_async_remote_copy(
                    src_ref=srcs[k].at[mine], dst_ref=outs[k].at[me, mine], send_sem=ici_send.at[k, p],
                    recv_sem=ici_recv.at[k, p], device_id=(px, py, c), device_id_type=MESH)
                cp.start()
                pending.append(cp)
        for k in range(nk):
            half = srcs[k].shape[0] // 2
            mine = pl.ds(c * half, half)
            for p, (px, py) in enumerate(peers):
                landed = outs[k].at[2 * px + py, mine]
                pltpu.make_async_remote_copy(
                    src_ref=srcs[k].at[mine], dst_ref=landed, send_sem=ici_send.at[k, p], recv_sem=ici_recv.at[k, p],
                    device_id=(px, py, c), device_id_type=MESH).wait_recv()
                fwd = pltpu.make_async_remote_copy(
                    src_ref=landed, dst_ref=landed, send_sem=d2d_send.at[k, p], recv_sem=d2d_recv.at[k, p],
                    device_id=(x, y, 1 - c), device_id_type=MESH)
                fwd.start()
                pending.append(fwd)
        for k in range(nk):
            half = srcs[k].shape[0] // 2
            other = pl.ds((1 - c) * half, half)
            for p, (px, py) in enumerate(peers):
                theirs = outs[k].at[2 * px + py, other]
                pltpu.make_async_remote_copy(
                    src_ref=theirs, dst_ref=theirs, send_sem=d2d_send.at[k, p], recv_sem=d2d_recv.at[k, p],
                    device_id=(x, y, 1 - c), device_id_type=MESH).wait_recv()
        for cp in pending:
            cp.wait_send()
        refs[2 * nk][...] = jnp.zeros_like(refs[2 * nk])

    sem = lambda *shape: pltpu.SemaphoreType.DMA(shape)
    res = pl.pallas_call(
        body, name=name, in_specs=[_ANY] * nk, out_specs=[_ANY] * nk + [pl.BlockSpec(memory_space=pltpu.VMEM)],
        out_shape=[jax.ShapeDtypeStruct((4,) + s.shape, s.dtype) for s in shards]
        + [jax.ShapeDtypeStruct((8, 128), f32)],
        scratch_shapes=[sem(nk, 3), sem(nk, 3), sem(nk, 3), sem(nk, 3)],
    )(*shards)
    return list(res[:nk]), res[nk]


_HBM = pl.BlockSpec(memory_space=pltpu.HBM)
_SEM = pl.BlockSpec(memory_space=pltpu.SEMAPHORE)


def _ici_half_copies(srcs, lands, send_sems, recv_sems, incoming):
    x, y, c = _place()
    me = 2 * x + y
    out = []
    for k in range(len(srcs)):
        half = srcs[k].shape[0] // 2
        mine = pl.ds(c * half, half)
        for p, (px, py) in enumerate(_peer_chips(x, y)):
            out.append(pltpu.make_async_remote_copy(
                src_ref=srcs[k].at[mine], dst_ref=lands[k].at[(2 * px + py) if incoming else me, mine],
                send_sem=send_sems.at[3 * k + p], recv_sem=recv_sems.at[3 * k + p], device_id=(px, py, c),
                device_id_type=MESH))
    return out


def gather_start(shards, *, name):
    nk = len(shards)

    def body(*refs):
        srcs, lands = refs[:nk], refs[nk:2 * nk]
        send_sems, recv_sems = refs[2 * nk], refs[2 * nk + 1]
        token = refs[-1]
        for outgoing in _ici_half_copies(srcs, lands, send_sems, recv_sems, incoming=False):
            outgoing.start()
        token[...] = jnp.zeros_like(token)

    lands = [pltpu.with_memory_space_constraint(lax.empty((4,) + s.shape, s.dtype), pltpu.HBM) for s in shards]
    res = pl.pallas_call(
        body, name=name,
        out_shape=(pltpu.SemaphoreType.DMA((3 * nk,)), pltpu.SemaphoreType.DMA((3 * nk,)),
                   *[pltpu.HBM(s.shape, s.dtype) for s in shards], *[pltpu.HBM(z.shape, z.dtype) for z in lands],
                   jax.ShapeDtypeStruct((8, 128), f32)),
        in_specs=[_HBM] * (2 * nk), out_specs=(_SEM, _SEM, *[_HBM] * (2 * nk), pl.BlockSpec(memory_space=pltpu.VMEM)),
        input_output_aliases={k: 2 + k for k in range(2 * nk)},
        compiler_params=pltpu.CompilerParams(has_side_effects=pltpu.SideEffectType.DATAFLOW_SIDE_EFFECTING),
    )(*[pltpu.with_memory_space_constraint(s, pltpu.HBM) for s in shards], *lands)
    return res[0], res[1], list(res[2:2 + nk]), list(res[2 + nk:2 + 2 * nk]), res[-1]


def gather_wait(send_sems, recv_sems, shards, lands, after, *, name):
    nk = len(shards)

    def body(*refs):
        srcs, zones = refs[:nk], refs[nk:2 * nk]
        for outgoing in _ici_half_copies(srcs, zones, refs[2 * nk], refs[2 * nk + 1], incoming=False):
            outgoing.wait_send()
        for landed in _ici_half_copies(srcs, zones, refs[2 * nk], refs[2 * nk + 1], incoming=True):
            landed.wait_recv()

    res = pl.pallas_call(
        body, name=name,
        out_shape=(*[pltpu.HBM(s.shape, s.dtype) for s in shards], *[pltpu.HBM(z.shape, z.dtype) for z in lands]),
        in_specs=[_HBM] * (2 * nk) + [_SEM, _SEM, _ANY], out_specs=tuple([_HBM] * (2 * nk)),
        input_output_aliases={k: k for k in range(2 * nk)},
        compiler_params=pltpu.CompilerParams(has_side_effects=pltpu.SideEffectType.DATAFLOW_SIDE_EFFECTING),
    )(*shards, *lands, send_sems, recv_sems, after)
    return list(res[:nk]), list(res[nk:])


def gather_forward(lands, *, name):
    nk = len(lands)

    def body(*refs):
        zones = refs[nk:2 * nk]
        send_sems, recv_sems = refs[2 * nk:]
        x, y, c = _place()
        sends = []
        for k in range(nk):
            half = zones[k].shape[1] // 2
            for p, (px, py) in enumerate(_peer_chips(x, y)):
                landed = zones[k].at[2 * px + py, pl.ds(c * half, half)]
                cp = pltpu.make_async_remote_copy(
                    src_ref=landed, dst_ref=landed, send_sem=send_sems.at[k, p], recv_sem=recv_sems.at[k, p],
                    device_id=(x, y, 1 - c), device_id_type=MESH)
                cp.start()
                sends.append(cp)
        for k in range(nk):
            half = zones[k].shape[1] // 2
            for p, (px, py) in enumerate(_peer_chips(x, y)):
                theirs = zones[k].at[2 * px + py, pl.ds((1 - c) * half, half)]
                pltpu.make_async_remote_copy(
                    src_ref=theirs, dst_ref=theirs, send_sem=send_sems.at[k, p], recv_sem=recv_sems.at[k, p],
                    device_id=(x, y, 1 - c), device_id_type=MESH).wait_recv()
        for cp in sends:
            cp.wait_send()

    return pl.pallas_call(
        body, name=name, in_specs=[_ANY] * nk, out_specs=[_ANY] * nk,
        out_shape=[jax.ShapeDtypeStruct(z.shape, z.dtype) for z in lands],
        input_output_aliases={k: k for k in range(nk)},
        scratch_shapes=[pltpu.SemaphoreType.DMA((nk, 3)), pltpu.SemaphoreType.DMA((nk, 3))],
    )(*lands)


def _to_sibling_copies(srcs, lands, send_sems, recv_sems):
    x, y, c = _place()
    out = []
    for k in range(len(srcs)):
        half = srcs[k].shape[1] // 2
        out.append(pltpu.make_async_remote_copy(
            src_ref=srcs[k].at[:, pl.ds((1 - c) * half, half), :], dst_ref=lands[k], send_sem=send_sems.at[k],
            recv_sem=recv_sems.at[k], device_id=(x, y, 1 - c), device_id_type=MESH))
    return out


def sibling_start(parts, *, name):
    nk = len(parts)

    def body(*refs):
        for cp in _to_sibling_copies(refs[:nk], refs[nk:2 * nk], refs[2 * nk], refs[2 * nk + 1]):
            cp.start()
        refs[-1][...] = jnp.zeros_like(refs[-1])

    lands = [pltpu.with_memory_space_constraint(lax.empty((4, p.shape[1] // 2, p.shape[2]), p.dtype), pltpu.HBM)
             for p in parts]
    res = pl.pallas_call(
        body, name=name,
        out_shape=(pltpu.SemaphoreType.DMA((nk,)), pltpu.SemaphoreType.DMA((nk,)),
                   *[pltpu.HBM(p.shape, p.dtype) for p in parts], *[pltpu.HBM(z.shape, z.dtype) for z in lands],
                   jax.ShapeDtypeStruct((8, 128), f32)),
        in_specs=[_HBM] * (2 * nk), out_specs=(_SEM, _SEM, *[_HBM] * (2 * nk), pl.BlockSpec(memory_space=pltpu.VMEM)),
        input_output_aliases={k: 2 + k for k in range(2 * nk)},
        compiler_params=pltpu.CompilerParams(has_side_effects=pltpu.SideEffectType.DATAFLOW_SIDE_EFFECTING),
    )(*[pltpu.with_memory_space_constraint(p, pltpu.HBM) for p in parts], *lands)
    return res[0], res[1], list(res[2:2 + nk]), list(res[2 + nk:2 + 2 * nk]), res[-1]


def sibling_wait(send_sems, recv_sems, parts, lands, after, *, name):
    nk = len(parts)

    def body(*refs):
        copies = _to_sibling_copies(refs[:nk], refs[nk:2 * nk], refs[2 * nk], refs[2 * nk + 1])
        for cp in copies:
            cp.wait_send()
        for cp in copies:
            cp.wait_recv()

    res = pl.pallas_call(
        body, name=name,
        out_shape=(*[pltpu.HBM(p.shape, p.dtype) for p in parts], *[pltpu.HBM(z.shape, z.dtype) for z in lands]),
        in_specs=[_HBM] * (2 * nk) + [_SEM, _SEM, _ANY], out_specs=tuple([_HBM] * (2 * nk)),
        input_output_aliases={k: k for k in range(2 * nk)},
        compiler_params=pltpu.CompilerParams(has_side_effects=pltpu.SideEffectType.DATAFLOW_SIDE_EFFECTING),
    )(*parts, *lands, send_sems, recv_sems, after)
    return list(res[:nk]), list(res[nk:])


def pair_sum(part, theirs, core, *, name):
    _, rows, cols = part.shape
    half = rows // 2
    tile = _pick(half, max(16, (1 << 20) // (4 * cols) // 16 * 16), 16)
    per = half // tile

    def body(c_ref, p_ref, t_ref, o_ref):
        o_ref[...] = (p_ref[...].astype(f32) + t_ref[...].astype(f32)).astype(bf16)

    grid_spec = pltpu.PrefetchScalarGridSpec(
        num_scalar_prefetch=1, grid=(4, per),
        in_specs=[pl.BlockSpec((1, tile, cols), lambda j, i, c_ref: (j, c_ref[0] * per + i, 0)),
                  pl.BlockSpec((1, tile, cols), lambda j, i, c_ref: (j, i, 0))],
        out_specs=pl.BlockSpec((1, tile, cols), lambda j, i, c_ref: (j, i, 0)))
    return pl.pallas_call(
        body, name=name, grid_spec=grid_spec, out_shape=jax.ShapeDtypeStruct((4, half, cols), bf16),
        compiler_params=_cparams(("parallel", "parallel")),
    )(core, part, theirs)


def _all_to_all_copies(srcs, lands, send_sems, recv_sems, incoming):
    x, y, c = _place()
    me = 2 * x + y
    out = []
    for k in range(len(srcs)):
        for p, (px, py) in enumerate(_peer_chips(x, y)):
            peer = 2 * px + py
            out.append(pltpu.make_async_remote_copy(
                src_ref=srcs[k].at[peer], dst_ref=lands[k].at[peer if incoming else me],
                send_sem=send_sems.at[3 * k + p], recv_sem=recv_sems.at[3 * k + p], device_id=(px, py, c),
                device_id_type=MESH))
    return out


def scatter_start(parts, *, name):
    nk = len(parts)

    def body(*refs):
        srcs, lands = refs[:nk], refs[nk:2 * nk]
        for outgoing in _all_to_all_copies(srcs, lands, refs[2 * nk], refs[2 * nk + 1], incoming=False):
            outgoing.start()
        refs[-1][...] = jnp.zeros_like(refs[-1])

    lands = [pltpu.with_memory_space_constraint(lax.empty(p.shape, p.dtype), pltpu.HBM) for p in parts]
    res = pl.pallas_call(
        body, name=name,
        out_shape=(pltpu.SemaphoreType.DMA((3 * nk,)), pltpu.SemaphoreType.DMA((3 * nk,)),
                   *[pltpu.HBM(p.shape, p.dtype) for p in parts], *[pltpu.HBM(p.shape, p.dtype) for p in parts],
                   jax.ShapeDtypeStruct((8, 128), f32)),
        in_specs=[_HBM] * (2 * nk), out_specs=(_SEM, _SEM, *[_HBM] * (2 * nk), pl.BlockSpec(memory_space=pltpu.VMEM)),
        input_output_aliases={k: 2 + k for k in range(2 * nk)},
        compiler_params=pltpu.CompilerParams(has_side_effects=pltpu.SideEffectType.DATAFLOW_SIDE_EFFECTING),
    )(*[pltpu.with_memory_space_constraint(p, pltpu.HBM) for p in parts], *lands)
    return res[0], res[1], list(res[2:2 + nk]), list(res[2 + nk:2 + 2 * nk]), res[-1]


def scatter_wait(send_sems, recv_sems, parts, lands, after, *, name):
    nk = len(parts)

    def body(*refs):
        srcs, zones = refs[:nk], refs[nk:2 * nk]
        for outgoing in _all_to_all_copies(srcs, zones, refs[2 * nk], refs[2 * nk + 1], incoming=False):
            outgoing.wait_send()
        for landed in _all_to_all_copies(srcs, zones, refs[2 * nk], refs[2 * nk + 1], incoming=True):
            landed.wait_recv()

    res = pl.pallas_call(
        body, name=name,
        out_shape=(*[pltpu.HBM(p.shape, p.dtype) for p in parts], *[pltpu.HBM(z.shape, z.dtype) for z in lands]),
        in_specs=[_HBM] * (2 * nk) + [_SEM, _SEM, _ANY], out_specs=tuple([_HBM] * (2 * nk)),
        input_output_aliases={k: k for k in range(2 * nk)},
        compiler_params=pltpu.CompilerParams(has_side_effects=pltpu.SideEffectType.DATAFLOW_SIDE_EFFECTING),
    )(*parts, *lands, send_sems, recv_sems, after)
    return list(res[:nk]), list(res[nk:])


def join_halves(bufs, layout, *, name):
    nk, nb = len(layout), len(bufs)

    def body(*refs):
        outs = refs[nb:2 * nb]
        send_sems, recv_sems = refs[2 * nb:]
        x, y, c = _place()
        pending = []
        for k, (o, off, rows) in enumerate(layout):
            half = rows // 2
            mine = outs[o].at[pl.ds(off + c * half, half), :]
            cp = pltpu.make_async_remote_copy(
                src_ref=mine, dst_ref=mine, send_sem=send_sems.at[k], recv_sem=recv_sems.at[k],
                device_id=(x, y, 1 - c), device_id_type=MESH)
            cp.start()
            pending.append(cp)
        for k, (o, off, rows) in enumerate(layout):
            half = rows // 2
            theirs = outs[o].at[pl.ds(off + (1 - c) * half, half), :]
            pltpu.make_async_remote_copy(
                src_ref=theirs, dst_ref=theirs, send_sem=send_sems.at[k], recv_sem=recv_sems.at[k],
                device_id=(x, y, 1 - c), device_id_type=MESH).wait_recv()
        for cp in pending:
            cp.wait_send()

    return pl.pallas_call(
        body, name=name, in_specs=[_ANY] * nb, out_specs=[_ANY] * nb,
        out_shape=[jax.ShapeDtypeStruct(b.shape, b.dtype) for b in bufs],
        input_output_aliases={o: o for o in range(nb)},
        scratch_shapes=[pltpu.SemaphoreType.DMA((nk,)), pltpu.SemaphoreType.DMA((nk,))],
    )(*bufs)


def place_slab(dest, src, index, *, name):
    rows, cols = src.shape
    tile = _pick(rows, max(16, (1 << 20) // (src.dtype.itemsize * cols) // 16 * 16), 16)

    def body(i_ref, s_ref, d_ref, o_ref):
        del i_ref, d_ref
        o_ref[0] = s_ref[...]

    grid_spec = pltpu.PrefetchScalarGridSpec(
        num_scalar_prefetch=1, grid=(rows // tile,),
        in_specs=[pl.BlockSpec((tile, cols), lambda i, idx: (i, 0)), _ANY],
        out_specs=pl.BlockSpec((1, tile, cols), lambda i, idx: (idx[0], i, 0)))
    return pl.pallas_call(
        body, name=name, grid_spec=grid_spec, out_shape=jax.ShapeDtypeStruct(dest.shape, dest.dtype),
        input_output_aliases={2: 0}, compiler_params=_cparams(("parallel",)),
    )(index, src, dest)


def _broadcast_copies(src, land, send_sems, recv_sems, incoming):
    x, y, c = _place()
    me = 4 * x + 2 * y + c
    out = []
    for m in range(1, 8):
        px, py, pc = x ^ (m >> 2), y ^ ((m >> 1) & 1), c ^ (m & 1)
        out.append(pltpu.make_async_remote_copy(
            src_ref=src, dst_ref=land.at[(4 * px + 2 * py + pc) if incoming else me], send_sem=send_sems.at[m - 1],
            recv_sem=recv_sems.at[m - 1], device_id=(px, py, pc), device_id_type=MESH))
    return out


def broadcast_start(src, *, name):
    def body(s_ref, l_ref, send_sems, recv_sems, s_thru, l_thru, token):
        for outgoing in _broadcast_copies(s_ref, l_ref, send_sems, recv_sems, incoming=False):
            outgoing.start()
        token[...] = jnp.zeros_like(token)

    land = pltpu.with_memory_space_constraint(lax.empty((8,) + src.shape, src.dtype), pltpu.HBM)
    return pl.pallas_call(
        body, name=name,
        out_shape=(pltpu.SemaphoreType.DMA((7,)), pltpu.SemaphoreType.DMA((7,)), pltpu.HBM(src.shape, src.dtype),
                   pltpu.HBM(land.shape, land.dtype), jax.ShapeDtypeStruct((8, 128), f32)),
        in_specs=[_HBM, _HBM], out_specs=(_SEM, _SEM, _HBM, _HBM, pl.BlockSpec(memory_space=pltpu.VMEM)),
        input_output_aliases={0: 2, 1: 3},
        compiler_params=pltpu.CompilerParams(has_side_effects=pltpu.SideEffectType.DATAFLOW_SIDE_EFFECTING),
    )(pltpu.with_memory_space_constraint(src, pltpu.HBM), land)


def broadcast_wait(send_sems, recv_sems, src, land, after, *, name):
    def body(s_ref, l_ref, send_sems, recv_sems, after_ref, s_out, l_out):
        for outgoing in _broadcast_copies(s_ref, l_ref, send_sems, recv_sems, incoming=False):
            outgoing.wait_send()
        for landed in _broadcast_copies(s_ref, l_ref, send_sems, recv_sems, incoming=True):
            landed.wait_recv()

    return pl.pallas_call(
        body, name=name, out_shape=(pltpu.HBM(src.shape, src.dtype), pltpu.HBM(land.shape, land.dtype)),
        in_specs=[_HBM, _HBM, _SEM, _SEM, _ANY], out_specs=(_HBM, _HBM), input_output_aliases={0: 0, 1: 1},
        compiler_params=pltpu.CompilerParams(has_side_effects=pltpu.SideEffectType.DATAFLOW_SIDE_EFFECTING),
    )(src, land, send_sems, recv_sems, after)


def sum8(land, own, device, *, name):
    _, rows, cols = land.shape

    def body(d_ref, l_ref, o_ref, out_ref):
        mine = o_ref[...]
        acc = jnp.where(d_ref[0] == 0, mine, l_ref[0])
        for d in range(1, 8):
            acc = acc + jnp.where(d_ref[0] == d, mine, l_ref[d])
        out_ref[...] = acc

    grid_spec = pltpu.PrefetchScalarGridSpec(
        num_scalar_prefetch=1, grid=(1,),
        in_specs=[pl.BlockSpec((8, rows, cols), lambda i, d_ref: (0, 0, 0)),
                  pl.BlockSpec((rows, cols), lambda i, d_ref: (0, 0))],
        out_specs=pl.BlockSpec((rows, cols), lambda i, d_ref: (0, 0)))
    return pl.pallas_call(
        body, name=name, grid_spec=grid_spec, out_shape=jax.ShapeDtypeStruct((rows, cols), f32),
        compiler_params=_cparams(("arbitrary",)),
    )(device, land, own)


def sum4_into(arrived, own, dest, where, *, layer, total_rows, name):
    _, rows, cols = arrived.shape
    tile = _pick(rows, max(16, (1 << 20) // (4 * cols) // 16 * 16), 16)
    per = rows // tile

    def body(w_ref, a_ref, own_ref, *rest):
        mine = own_ref[0].astype(f32)
        p = [jnp.where(w_ref[1] == j, mine, a_ref[j].astype(f32)) for j in range(4)]
        rest[-1][...] = ((p[0] + p[1]) + p[2]) + p[3]

    grid_spec = pltpu.PrefetchScalarGridSpec(
        num_scalar_prefetch=1, grid=(per,),
        in_specs=[pl.BlockSpec((4, tile, cols), lambda i, w_ref: (0, i, 0)),
                  pl.BlockSpec((1, tile, cols), lambda i, w_ref: (w_ref[1], i, 0))] + ([] if dest is None else [_ANY]),
        out_specs=pl.BlockSpec((tile, cols), lambda i, w_ref: ((2 * layer + w_ref[0]) * per + i, 0)))
    return pl.pallas_call(
        body, name=name, grid_spec=grid_spec, out_shape=jax.ShapeDtypeStruct((total_rows, cols), f32),
        input_output_aliases={} if dest is None else {3: 0}, compiler_params=_cparams(("parallel",)),
    )(where, arrived, own, *([] if dest is None else [dest]))


def _consts():
    idx = np.arange(RW)
    bd = (idx[:, None] // RN == idx[None, :] // RN).astype(np.float32)
    rot = np.zeros((DQK, DQK), np.float32)
    half = ROPE // 2
    rot[NOPE + half + np.arange(half), NOPE + np.arange(half)] = -1.0
    rot[NOPE + np.arange(half), NOPE + half + np.arange(half)] = 1.0
    return jnp.asarray(bd), jnp.asarray(rot)


def _rope_tables(positions):
    freqs = ROPE_THETA ** (-(jnp.arange(ROPE // 2, dtype=f32) * 2.0 / ROPE))
    ang = positions.astype(f32)[:, None] * freqs
    cos, sin = jnp.cos(ang), jnp.sin(ang)
    ones = jnp.ones((positions.shape[0], NOPE), f32)
    return (jnp.concatenate([ones, cos, cos], axis=-1), jnp.concatenate([0.0 * ones, sin, sin], axis=-1))


STAGES = (("w_in",), ("mla_wq_b", "mla_wkv_b", "mla_w_o", "rwkv_w_o", "conv_w_o", "w_out"), ("w_up", "w_down"))


def derive_stage(stage, w):
    if stage == 0:
        w_in = w["w_in"]
        pad = jnp.zeros((D, MLA_PAD - MLA_COLS), w_in.dtype)
        return dict(gate=w_in[:, :GATE], mla=jnp.concatenate([w_in[:, GATE:GATE + MLA_COLS], pad], axis=1),
                    rw=w_in[:, GATE + MLA_COLS:GATE + MLA_COLS + 4 * RW], cv=w_in[:, GATE + MLA_COLS + 4 * RW:])
    if stage == 1:
        wq = jnp.pad(w["mla_wq_b"].reshape(QL, MLA_H, DQK), ((0, 0), (0, 0), (0, Q_HEAD_PAD - DQK)))
        return dict(wq=wq.reshape(QL, MLA_H * Q_HEAD_PAD), wkv=w["mla_wkv_b"], wo=w["mla_w_o"], rwo=w["rwkv_w_o"],
                    cvo=w["conv_w_o"], out=w["w_out"])
    return dict(up=w["w_up"], down=w["w_down"])


W_IN_WINDOW_TILE = (0, 10, 21, 31)
W_IN_WINDOW = 1664
W_IN_SHARD = 1384


def w_in_window_cols(win, chip):
    gap = MLA_PAD - MLA_COLS
    branches = []
    for j in range(4):
        lo, hi = W_IN_SHARD * j, W_IN_SHARD * (j + 1)
        base = 128 * W_IN_WINDOW_TILE[j]
        cut = GATE + MLA_COLS
        if hi <= cut:
            branches.append(lambda w, a=lo - base: w[:, a:a + W_IN_SHARD])
        elif lo >= cut:
            branches.append(lambda w, a=lo + gap - base: w[:, a:a + W_IN_SHARD])
        else:
            branches.append(lambda w, a=lo - base, n1=cut - lo, b=cut + gap - base, n2=hi - cut:
                            jnp.concatenate([w[:, a:a + n1], w[:, b:b + n2]], axis=1))
    return lax.switch(chip, branches, win)


def chip_major_grads(stage, g):
    if stage == 0:
        padded = jnp.concatenate([g["gate"], g["mla"], g["rw"], g["cv"]], axis=1)
        return dict(w_in=jnp.stack([padded[:, 128 * t:128 * t + W_IN_WINDOW] for t in W_IN_WINDOW_TILE]))
    if stage == 1:
        wq = g["wq"].reshape(QL, MLA_H, Q_HEAD_PAD)[:, :, :DQK].reshape(QL, 4, -1).transpose(1, 0, 2)
        return dict(mla_wq_b=wq, mla_wkv_b=g["wkv"], mla_w_o=g["wo"], rwkv_w_o=g["rwo"], conv_w_o=g["cvo"],
                    w_out=g["out"].reshape(4, D // 4, D))
    return dict(w_up=g["up"], w_down=g["down"].reshape(4, DFF // 4, D))


def _row(v):
    return v.reshape(1, -1)


def local_step(x, positions, target, w, sm, big_of=None, on_grads=None, on_small=None):
    if big_of is None:
        big_of = lambda l, stage, _after: {n: w[n][l] for n in STAGES[stage]}
    if on_grads is None:
        on_grads = lambda l, stage, slabs: None
    if on_small is None:
        on_small = lambda l, layer_small: None
    s_len = x.shape[0]
    t_row = _pick(s_len, 256, 8)
    t_wide = _pick(s_len, 128, 8)
    bd, rot = _consts()
    cos, sin = _rope_tables(positions)
    sds = lambda *shape: jax.ShapeDtypeStruct(shape, f32)
    sdb = lambda *shape: jax.ShapeDtypeStruct(shape, bf16)
    saved = []
    v_first = None
    for l in range(DEPTH):
        tag = f"l{l}_"
        lw = derive_stage(0, big_of(l, 0, x))
        vres = l > 0
        p_norm1 = [_row(sm["attn_norm"][l])]
        (h,) = rows_fwd(_fn_norm, [x], p_norm1, [], [sds(s_len, D)], tile=t_row, name=tag + "norm1")
        gate = mm(h, lw["gate"], name=tag + "proj_gate")
        mla = mm(h, lw["mla"], name=tag + "proj_mla")
        rwc = mm(h, lw["rw"], name=tag + "proj_rwkv")
        cvc = mm(h, lw["cv"], name=tag + "proj_conv")
        lw.update(derive_stage(1, big_of(l, 1, cvc)))
        p_mla = [_row(sm["mla_q_a_norm"][l]), _row(sm["mla_kv_a_norm"][l])]
        qn, kvn, kpe = rows_fwd(_fn_mla_prep, [mla], p_mla, [], [sdb(s_len, QL), sdb(s_len, KVL), sds(s_len, 128)],
                                tile=t_row, name=tag + "mla_prep")
        q_flat = mm(qn, lw["wq"], name=tag + "q_proj")
        kv_flat = mm(kvn, lw["wkv"], name=tag + "kv_proj")
        p_qk = [_row(sm["mla_q_norm"][l]), _row(sm["mla_k_norm"][l])]
        q, k, vv = rows_fwd(_fn_qk_post, [q_flat, kv_flat, kpe, cos, sin], p_qk, [rot],
                            [sds(MLA_H, s_len, DQK), sds(MLA_H, s_len, DQK), sds(MLA_H, s_len, DV)], tile=t_wide,
                            name=tag + "qk_post")
        o = attn_fwd(q, k, vv, tq=_pick(s_len, 256, 8), name=tag + "attn")
        o_a = mm(o, lw["wo"], name=tag + "o_a")
        p_rw = [_row(sm["rwkv_mu"][l]), _row(sm["rwkv_w0"][l]), w["rwkv_w2"][l], _row(sm["rwkv_a0"][l]),
                w["rwkv_a2"][l], w["rwkv_g2"][l], _row(sm["rwkv_k_k"][l]), _row(sm["rwkv_k_a"][l])]
        rw_rows, rw_halos = [rwc], (0,)
        if vres:
            p_rw += [w["rwkv_v1"][l - 1], _row(sm["rwkv_v_mu"][l - 1]), _row(sm["rwkv_v0"][l - 1]), w["rwkv_v2"][l - 1]]
            rw_rows, rw_halos = [rwc, h, v_first], (0, 1)
        fn_prep = _make_fn_rwkv_prep(vres)
        r, ld, k2, v, an, bn, g = rows_fwd(fn_prep, rw_rows, p_rw, [bd], [sds(s_len, RW)] * 7, tile=t_row,
                                           name=tag + "rwkv_prep", halos=rw_halos)
        if not vres:
            v_first = v
        y, states = wkv_fwd(r, ld, k2, v, an, bn, name=tag + "wkv")
        p_post = [_row(sm["rwkv_ln_w"][l]), _row(sm["rwkv_ln_b"][l]), _row(sm["rwkv_r_k"][l])]
        (yb,) = rows_fwd(_fn_rwkv_post, [y, r, k2, v, g], p_post, [bd], [sdb(s_len, RW)], tile=t_row,
                         name=tag + "rwkv_post")
        o_b = mm(yb, lw["rwo"], name=tag + "o_b")
        p_cv = [w["conv_w"][l][q:q + 1] for q in range(3)]
        (yc,) = rows_fwd(_fn_conv, [cvc], p_cv, [], [sdb(s_len, CW)], tile=t_row, name=tag + "conv", halos=(0,))
        o_c = mm(yc, lw["cvo"], name=tag + "o_c")
        (merged,) = rows_fwd(_fn_merge, [gate, o_a, o_b, o_c], [], [], [sdb(s_len, D)], tile=t_wide,
                             name=tag + "merge")
        x1 = mm(merged, lw["out"], add=x, name=tag + "out_proj")
        lw.update(derive_stage(2, big_of(l, 2, x1)))
        p_norm2 = [_row(sm["mlp_norm"][l])]
        (h2,) = rows_fwd(_fn_norm, [x1], p_norm2, [], [sdb(s_len, D)], tile=t_row, name=tag + "norm2")
        up, act = mm(h2, lw["up"], relu2_out=True, name=tag + "up")
        x2 = mm(act, lw["down"], add=x1, name=tag + "down")
        saved.append(dict(lw=lw, x=x, h=h, gate=gate, mla=mla, rwc=rwc, cvc=cvc, qn=qn, kvn=kvn, kpe=kpe,
                          q_flat=q_flat, kv_flat=kv_flat, vv=vv, q=q, k=k, o=o, o_a=o_a, r=r, ld=ld, k2=k2, v=v,
                          an=an, bn=bn, g=g, y=y, states=states, yb=yb, o_b=o_b, yc=yc, o_c=o_c, merged=merged,
                          x1=x1, h2=h2, up=up, act=act, p_norm1=p_norm1, p_mla=p_mla, p_qk=p_qk, p_rw=p_rw,
                          p_post=p_post, p_cv=p_cv, p_norm2=p_norm2, rw_rows=rw_rows, rw_halos=rw_halos,
                          fn_prep=fn_prep, v_first=v_first if vres else None))
        x = x2

    loss, dx = loss_head(x, target, tile=t_row, name="loss_head")

    grads = {n: [None] * (DEPTH - 1 if n in ("rwkv_v1", "rwkv_v_mu", "rwkv_v0", "rwkv_v2") else DEPTH)
             for n in WEIGHTS}
    dv_first = None
    for l in reversed(range(DEPTH)):
        tag = f"b{l}_"
        sv = saved[l]
        lw = sv["lw"]
        vres = l > 0
        g_down = mm(sv["act"], dx, ta=True, out_dtype=bf16, name=tag + "g_down")
        dup = mm(dx, lw["down"], tb=True, act_grad=sv["up"], out_dtype=bf16, name=tag + "d_up")
        g_up = mm(sv["h2"], dup, ta=True, n_split=4, out_dtype=bf16, name=tag + "g_up")
        dh2 = mm(dup, lw["up"], tb=True, name=tag + "d_h2")
        slabs = chip_major_grads(2, dict(up=g_up, down=g_down))
        token = on_grads(l, 2, slabs)
        p_norm2 = sv["p_norm2"] if token is None else [sv["p_norm2"][0] + token[0, 0]]
        (dx1,), (g_n2,) = rows_bwd(_fn_norm, [sv["x1"]], p_norm2, [], [[dh2]], tile=t_row,
                                   name=tag + "norm2", extra={0: [dx]})
        g_out = mm(sv["merged"], dx1, ta=True, out_dtype=bf16, name=tag + "g_out")
        dmerged = mm(dx1, lw["out"], tb=True, name=tag + "d_merged")
        (dgate, do_a, do_b, do_c), _ = rows_bwd(_fn_merge, [sv["gate"], sv["o_a"], sv["o_b"], sv["o_c"]], [], [],
                                                [[dmerged]], tile=t_wide, name=tag + "merge",
                                                grad_dtypes=[bf16] * 4)
        g_cvo = mm(sv["yc"], do_c, ta=True, n_split=4, out_dtype=bf16, name=tag + "g_cvo")
        dyc = mm(do_c, lw["cvo"], tb=True, name=tag + "d_yc")
        (dcvc,), g_cw = rows_bwd(_fn_conv, [sv["cvc"]], sv["p_cv"], [], [[dyc]], tile=t_row, name=tag + "conv",
                                    halos=(0,))
        g_rwo = mm(sv["yb"], do_b, ta=True, n_split=4, out_dtype=bf16, name=tag + "g_rwo")
        dyb = mm(do_b, lw["rwo"], tb=True, name=tag + "d_yb")
        (dy, dr_p, dk_p, dv_p, dg), g_post = rows_bwd(
            _fn_rwkv_post, [sv["y"], sv["r"], sv["k2"], sv["v"], sv["g"]], sv["p_post"], [bd], [[dyb]], tile=t_row,
            name=tag + "rwkv_post")
        dr_s, dld, dk_s, dv_s, dan, dbn = wkv_bwd(sv["r"], sv["ld"], sv["k2"], sv["v"], sv["an"], sv["bn"],
                                                  sv["states"], dy, name=tag + "wkv")
        dv_list = [dv_s, dv_p] + ([dv_first] if (not vres and dv_first is not None) else [])
        d_prep, g_prep = rows_bwd(
            sv["fn_prep"], sv["rw_rows"], sv["p_rw"], [bd],
            [[dr_s, dr_p], [dld], [dk_s, dk_p], dv_list, [dan], [dbn], [dg]], tile=t_row, name=tag + "rwkv_prep",
            halos=sv["rw_halos"])
        drwc = d_prep[0]
        dh_extra = []
        if vres:
            dh_extra = [d_prep[1]]
            dv_first = d_prep[2]
        g_wo = mm(sv["o"], do_a, ta=True, n_split=4, out_dtype=bf16, name=tag + "g_wo")
        do = mm(do_a, lw["wo"], tb=True, name=tag + "d_o")
        dq, dk, dvv = attn_bwd(sv["q"], sv["k"], sv["vv"], do, tq=_pick(s_len, 256, 8), name=tag + "attn")
        (dq_flat, dkv_flat, dkpe), g_qk = rows_bwd(
            _fn_qk_post, [sv["q_flat"], sv["kv_flat"], sv["kpe"], cos, sin], sv["p_qk"], [rot], [[dq], [dk], [dvv]],
            tile=t_wide, name=tag + "qk_post", grad_rows=[0, 1, 2], grad_dtypes=[bf16, bf16, f32])
        g_wq = mm(sv["qn"], dq_flat, ta=True, out_dtype=bf16, name=tag + "g_wq")
        g_wkv = mm(sv["kvn"], dkv_flat, ta=True, n_split=4, out_dtype=bf16, name=tag + "g_wkv")
        dqn = mm(dq_flat, lw["wq"], tb=True, name=tag + "d_qn")
        dkvn = mm(dkv_flat, lw["wkv"], tb=True, name=tag + "d_kvn")
        slabs.update(chip_major_grads(1, dict(wq=g_wq, wkv=g_wkv, wo=g_wo, rwo=g_rwo, cvo=g_cvo, out=g_out)))
        token = on_grads(l, 1, {n: slabs[n] for n in STAGES[1]})
        p_mla = sv["p_mla"] if token is None else [sv["p_mla"][0] + token[0, 0], sv["p_mla"][1]]
        (dmla,), g_mla = rows_bwd(_fn_mla_prep, [sv["mla"]], p_mla, [], [[dqn], [dkvn], [dkpe]], tile=t_row,
                                  name=tag + "mla_prep", grad_dtypes=[bf16])
        g_gate = mm(sv["h"], dgate, ta=True, out_dtype=bf16, name=tag + "g_gate")
        g_mlaw = mm(sv["h"], dmla, ta=True, out_dtype=bf16, name=tag + "g_mla")
        g_rw = mm(sv["h"], drwc, ta=True, out_dtype=bf16, name=tag + "g_rw")
        g_cv = mm(sv["h"], dcvc, ta=True, out_dtype=bf16, name=tag + "g_cv")
        dh = mm(dgate, lw["gate"], tb=True, name=tag + "d_h_gate")
        dh = mm(dmla, lw["mla"], tb=True, add=dh, name=tag + "d_h_mla")
        dh = mm(drwc, lw["rw"], tb=True, add=dh, name=tag + "d_h_rw")
        dh = mm(dcvc, lw["cv"], tb=True, add=dh, name=tag + "d_h_cv")
        (dx,), (g_n1,) = rows_bwd(_fn_norm, [sv["x"]], sv["p_norm1"], [], [[dh] + dh_extra], tile=t_row,
                                  name=tag + "norm1", extra={0: [dx1]})
        slabs.update(chip_major_grads(0, dict(gate=g_gate, mla=g_mlaw, rw=g_rw, cv=g_cv)))
        token = on_grads(l, 0, {n: slabs[n] for n in STAGES[0]})
        if token is not None and l > 0:
            dx = dx + token[0, 0]
        for n, val in slabs.items():
            grads[n][l] = val
        layer_small = [("attn_norm", l, g_n1), ("mlp_norm", l, g_n2), ("mla_q_a_norm", l, g_mla[0]),
                       ("mla_kv_a_norm", l, g_mla[1]), ("mla_q_norm", l, g_qk[0]), ("mla_k_norm", l, g_qk[1]),
                       ("rwkv_ln_w", l, g_post[0]), ("rwkv_ln_b", l, g_post[1]), ("rwkv_r_k", l, g_post[2]),
                       ("conv_w", l, jnp.concatenate(g_cw, axis=0))]
        layer_small += list(zip(["rwkv_mu", "rwkv_w0", "rwkv_w2", "rwkv_a0", "rwkv_a2", "rwkv_g2", "rwkv_k_k",
                                 "rwkv_k_a"], [l] * 8, g_prep[:8]))
        if vres:
            layer_small += list(zip(["rwkv_v1", "rwkv_v_mu", "rwkv_v0", "rwkv_v2"], [l - 1] * 4, g_prep[8:12]))
        for n, index, val in layer_small:
            grads[n][index] = val
        if l == 0:
            layer_small.append(("loss", 0, loss.reshape(1, 1)))
        token = on_small(l, layer_small)
        if token is not None and l > 0:
            dx = dx + token[0, 0]
    return loss, dx, grads


def _split3(a):
    hi = a.astype(bf16)
    r1 = a - hi.astype(f32)
    mid = r1.astype(bf16)
    lo = (r1 - mid.astype(f32)).astype(bf16)
    return hi, mid, lo


def _shard_axis(name):
    return 1 if name in ROW_SHARDED else 2


def _pack(pieces, width, dtype, row_align):
    flat = jnp.concatenate([p.reshape(-1).astype(dtype) for p in pieces])
    rows = -(-flat.shape[0] // width)
    rows = -(-rows // row_align) * row_align
    return jnp.pad(flat, (0, rows * width - flat.shape[0])).reshape(rows, width)


def _unpack(flat2d, shapes):
    flat = flat2d.reshape(-1)
    out, off = [], 0
    for shp in shapes:
        n = int(np.prod(shp))
        out.append(flat[off:off + n].reshape(shp))
        off += n
    return out


def kernel(x, positions, attn_norm, w_in, mla_q_a_norm, mla_wq_b, mla_kv_a_norm, mla_wkv_b, mla_q_norm, mla_k_norm, mla_w_o, rwkv_mu, rwkv_w0, rwkv_w2, rwkv_a0, rwkv_a2, rwkv_g2, rwkv_k_k, rwkv_k_a, rwkv_r_k, rwkv_ln_w, rwkv_ln_b, rwkv_w_o, rwkv_v1, rwkv_v_mu, rwkv_v0, rwkv_v2, conv_w, conv_w_o, w_out, mlp_norm, w_up, w_down, loss_target, m_attn_norm, m_w_in, m_mla_q_a_norm, m_mla_wq_b, m_mla_kv_a_norm, m_mla_wkv_b, m_mla_q_norm, m_mla_k_norm, m_mla_w_o, m_rwkv_mu, m_rwkv_w0, m_rwkv_w2, m_rwkv_a0, m_rwkv_a2, m_rwkv_g2, m_rwkv_k_k, m_rwkv_k_a, m_rwkv_r_k, m_rwkv_ln_w, m_rwkv_ln_b, m_rwkv_w_o, m_rwkv_v1, m_rwkv_v_mu, m_rwkv_v0, m_rwkv_v2, m_conv_w, m_conv_w_o, m_w_out, m_mlp_norm, m_w_up, m_w_down, v_attn_norm, v_w_in, v_mla_q_a_norm, v_mla_wq_b, v_mla_kv_a_norm, v_mla_wkv_b, v_mla_q_norm, v_mla_k_norm, v_mla_w_o, v_rwkv_mu, v_rwkv_w0, v_rwkv_w2, v_rwkv_a0, v_rwkv_a2, v_rwkv_g2, v_rwkv_k_k, v_rwkv_k_a, v_rwkv_r_k, v_rwkv_ln_w, v_rwkv_ln_b, v_rwkv_w_o, v_rwkv_v1, v_rwkv_v_mu, v_rwkv_v0, v_rwkv_v2, v_conv_w, v_conv_w_o, v_w_out, v_mlp_norm, v_w_up, v_w_down):
    args = dict(locals())
    wts = {n: args[n] for n in WEIGHTS}
    mom = {n: args["m_" + n] for n in WEIGHTS}
    var = {n: args["v_" + n] for n in WEIGHTS}
    chip = 2 * lax.axis_index("x") + lax.axis_index("y")
    core = lax.axis_index("c").astype(jnp.int32).reshape(1)

    med_names = [n for n in MED if n != "conv_w"]
    med_pieces = [wts[n] for n in med_names] + list(_split3(wts["conv_w"]))
    med_shapes = [p.shape for p in med_pieces]
    chip_idx = chip.astype(jnp.int32).reshape(1)
    shards_first = [wts[n][0].astype(bf16) for n in STAGES[0]] + [_pack(med_pieces, 128, bf16, 32)]
    got_first, token = gather_weights(shards_first, name="gather_l0_s0")
    got_first = [place_slab(g, s, chip_idx, name=f"place_own_l0_s0_{q}")
                 for q, (g, s) in enumerate(zip(got_first, shards_first))]
    in_flight = {}
    for key, names, l in (("l0_s1", STAGES[1], 0), ("l0_s2", STAGES[2], 0), ("l1", BIG, 1)):
        group = [wts[n][l].astype(bf16) for n in names]
        group[0] = group[0] + token[0, 0].astype(bf16)
        in_flight[key] = (names, gather_start(group, name="gather_start_" + key))
        token = in_flight[key][1][4]

    def whole_of(names, slabs):
        out = {}
        for n, by_chip in zip(names, slabs):
            _, rows, cols = by_chip.shape
            if n in ROW_SHARDED:
                out[n] = by_chip.reshape(4 * rows, cols)
            else:
                out[n] = by_chip.transpose(1, 0, 2).reshape(rows, 4 * cols)
        return out

    landed = {}

    def big_of(l, stage, after):
        if (l, stage) == (0, 0):
            return whole_of(STAGES[0], got_first)
        key = "l1" if l == 1 else f"l0_s{stage}"
        if key not in landed:
            names, (send_sems, recv_sems, thru, lands, _) = in_flight[key]
            thru, lands = gather_wait(send_sems, recv_sems, thru, lands, after, name="gather_wait_" + key)
            lands = gather_forward(lands, name="gather_forward_" + key)
            landed[key] = whole_of(names, [place_slab(g, s, chip_idx, name=f"place_own_{key}_{q}")
                                           for q, (g, s) in enumerate(zip(lands, thru))])
        return {n: landed[key][n] for n in STAGES[stage]}

    whole = {}
    per_chip = [_unpack(got_first[len(STAGES[0])][j], med_shapes) for j in range(4)]
    for q, n in enumerate(med_names):
        whole[n] = jnp.concatenate([per_chip[j][q] for j in range(4)], axis=_shard_axis(n)).astype(f32)
    base = len(med_names)
    cw_parts = [jnp.concatenate([per_chip[j][base + t] for j in range(4)], axis=2).astype(f32) for t in range(3)]
    whole["conv_w"] = (cw_parts[0] + cw_parts[1]) + cw_parts[2]
    small = {n: wts[n] for n in SMALL}
    small["rwkv_r_k"] = wts["rwkv_r_k"].reshape(DEPTH, RW)

    exchanges, to_sibling = [], []

    def chip_sums_on_their_way(after):
        tokens = []
        while to_sibling:
            l, names, tag, (send_sems, recv_sems, thru, lands, _) = to_sibling.pop(0)
            parts, theirs = sibling_wait(send_sems, recv_sems, thru, lands, after, name="sibling_wait_" + tag)
            chip_sums = [pair_sum(s, t, core, name=f"pair_sum_{n}_{l}") for n, s, t in zip(names, parts, theirs)]
            started = scatter_start(chip_sums, name="scatter_start_" + tag)
            exchanges.append((l, names, tag, started))
            tokens.append(started[4])
        return tokens

    def on_grads(l, stage, slabs):
        names = STAGES[stage]
        tag = f"l{l}_s{stage}"
        parts = [slabs[n] for n in names]
        tokens = chip_sums_on_their_way(parts[0])
        started = sibling_start(parts, name="sibling_start_" + tag)
        to_sibling.append((l, names, tag, started))
        token = started[4]
        for extra in tokens:
            token = token + extra
        return token

    broadcasts = []

    def on_small(l, layer_small):
        values = [val for _, _, val in layer_small]
        started = broadcast_start(_pack(values, 128, f32, 8), name=f"small_start_l{l}")
        broadcasts.append((l, [(n, index, val.shape) for n, index, val in layer_small], started))
        return started[4]

    small["attn_norm"] = small["attn_norm"] + token[0, 0]
    _, grad_x, grads = local_step(x[0], positions[0], loss_target[0], whole, small, big_of, on_grads, on_small)
    chip_sums_on_their_way(grad_x)

    device = (4 * lax.axis_index("x") + 2 * lax.axis_index("y") + lax.axis_index("c")).astype(jnp.int32).reshape(1)
    summed = {n: [None] * len(grads[n]) for n in SMALL + MED}
    summed["loss"] = [None]
    for l, entries, (send_sems, recv_sems, thru, land, _) in broadcasts:
        own, land = broadcast_wait(send_sems, recv_sems, thru, land, grad_x, name=f"small_wait_l{l}")
        total = sum8(land, own, device, name=f"small_sum_l{l}")
        for (n, index, _), val in zip(entries, _unpack(total, [shape for _, _, shape in entries])):
            summed[n][index] = val
    gsum = {}
    for n in SMALL + MED:
        g = jnp.stack(summed[n])
        if n in MED:
            ax = _shard_axis(n)
            width = wts[n].shape[ax]
            g = lax.dynamic_slice_in_dim(g, chip * width, width, axis=ax)
        gsum[n] = g.reshape(wts[n].shape)
    where = jnp.stack([lax.axis_index("c"), chip]).astype(jnp.int32)
    bufs, layout = {}, []
    for l, names, tag, (send_sems, recv_sems, thru, lands, _) in exchanges:
        own, arrived = scatter_wait(send_sems, recv_sems, thru, lands, grad_x, name="scatter_wait_" + tag)
        for n, mine, theirs in zip(names, own, arrived):
            rows = 2 * theirs.shape[1]
            bufs[n] = sum4_into(theirs, mine, bufs.get(n), where, layer=l, total_rows=DEPTH * rows,
                                name=f"sum_chips_{n}_{l}")
            layout.append((BIG.index(n), l * rows, rows))
    reduced = join_halves([bufs[n] for n in BIG], layout, name="join_halves")

    out_g, out_d, out_m, out_v = {}, {}, {}, {}
    for q, n in enumerate(BIG):
        shp = wts[n].shape
        as2d = lambda a: a.reshape(-1, shp[-1])
        g2d = w_in_window_cols(reduced[q], chip) if n == "w_in" else reduced[q]
        res = adamw(as2d(wts[n]), g2d, as2d(mom[n]), as2d(var[n]), name="adamw_" + n)
        out_g[n], out_d[n], out_m[n], out_v[n] = [r.reshape(shp) for r in res]
    sm_all = SMALL + MED
    flat2d = lambda a: a.reshape(-1, a.shape[-1])
    res = adamw_many([flat2d(wts[n]) for n in sm_all], [flat2d(gsum[n]) for n in sm_all],
                     [flat2d(mom[n]) for n in sm_all], [flat2d(var[n]) for n in sm_all], name="adamw_small")
    for tgt, vals in zip((out_d, out_m, out_v), res):
        for n, val in zip(sm_all, vals):
            tgt[n] = val.reshape(wts[n].shape)
    out_g.update({n: gsum[n] for n in sm_all})
    loss = summed["loss"][0].reshape(())
    return (loss, grad_x[None], *[out_g[n] for n in WEIGHTS], *[out_d[n] for n in WEIGHTS],
            *[out_m[n] for n in WEIGHTS], *[out_v[n] for n in WEIGHTS])
```

```python
import functools

import jax
import jax.numpy as jnp
import numpy as np
from jax import lax
from jax.experimental import pallas as pl
from jax.experimental.pallas import tpu as pltpu

f32, bf16 = jnp.float32, jnp.bfloat16
HI = lax.Precision.HIGHEST
MESH = pl.DeviceIdType.MESH

D = 1024
DEPTH = 2
MLA_H, NOPE, ROPE, DQK, DV = 8, 64, 32, 96, 64
QL, KVL = 384, 256
RW, RH, RN = 256, 4, 64
DL, AL, GL, MVL = 64, 64, 128, 32
CW = 256
DFF = 4096
GATE = 3 * D
MLA_COLS = QL + KVL + ROPE
MLA_PAD = 768
Q_HEAD_PAD = 128
NORM_EPS = 1e-6
GN_EPS = 64e-5
ROPE_THETA = 10000.0
LR, B1, B2, EPS, WD, STEP = 0.001, 0.9, 0.999, 1e-08, 0.01, 10

VMEM_LIMIT = 52 * 1024 * 1024
WKV_CHUNK = 64
WKV_CHUNKS_PER_STEP = 4
ATTN_SEGMENTS = 4
ATTN_Q_BLOCK = 512

BIG = ["w_in", "mla_wq_b", "mla_wkv_b", "mla_w_o", "rwkv_w_o", "conv_w_o", "w_out", "w_up", "w_down"]
MED = ["rwkv_w2", "rwkv_a2", "rwkv_g2", "rwkv_v1", "rwkv_v2", "conv_w"]
ROW_SHARDED = {"w_out", "w_down", "rwkv_v1"}
SMALL = ["attn_norm", "mla_q_a_norm", "mla_kv_a_norm", "mla_q_norm", "mla_k_norm", "rwkv_mu", "rwkv_w0",
         "rwkv_a0", "rwkv_k_k", "rwkv_k_a", "rwkv_r_k", "rwkv_ln_w", "rwkv_ln_b", "rwkv_v_mu", "rwkv_v0",
         "mlp_norm"]
WEIGHTS = ["attn_norm", "w_in", "mla_q_a_norm", "mla_wq_b", "mla_kv_a_norm", "mla_wkv_b", "mla_q_norm",
           "mla_k_norm", "mla_w_o", "rwkv_mu", "rwkv_w0", "rwkv_w2", "rwkv_a0", "rwkv_a2", "rwkv_g2",
           "rwkv_k_k", "rwkv_k_a", "rwkv_r_k", "rwkv_ln_w", "rwkv_ln_b", "rwkv_w_o", "rwkv_v1", "rwkv_v_mu",
           "rwkv_v0", "rwkv_v2", "conv_w", "conv_w_o", "w_out", "mlp_norm", "w_up", "w_down"]


def _cparams(sem=None):
    return pltpu.CompilerParams(dimension_semantics=sem, vmem_limit_bytes=VMEM_LIMIT)


def _pick(dim, pref, align):
    if dim <= pref:
        return dim
    t = (pref // align) * align
    while t >= align:
        if dim % t == 0:
            return t
        t -= align
    return dim


def _bdot(a, b, dims):
    return lax.dot_general(a.astype(bf16), b.astype(bf16), (dims, ((), ())), preferred_element_type=f32)


@jax.custom_vjp
def _mm(a, b):
    return _bdot(a, b, ((1,), (0,)))


def _mm_fwd(a, b):
    return _mm(a, b), (a, b)


def _mm_bwd(res, g):
    a, b = res
    return _bdot(g, b, ((1,), (1,))), _bdot(a, g, ((0,), (0,)))


_mm.defvjp(_mm_fwd, _mm_bwd)


@jax.custom_vjp
def _mm_nt(a, b):
    return _bdot(a, b, ((1,), (1,)))


def _mm_nt_fwd(a, b):
    return _mm_nt(a, b), (a, b)


def _mm_nt_bwd(res, g):
    a, b = res
    return _bdot(g, b, ((1,), (0,))), _bdot(g, a, ((0,), (0,)))


_mm_nt.defvjp(_mm_nt_fwd, _mm_nt_bwd)


_NN, _NT, _TN = ((1,), (0,)), ((1,), (1,)), ((0,), (0,))


def _dg(a, b, dims):
    return lax.dot_general(a, b, (dims, ((), ())), preferred_element_type=f32)


def _bf16_pieces(x, count):
    out, rest = [], x
    for q in range(count):
        piece = rest.astype(bf16)
        out.append(piece)
        if q + 1 < count:
            rest = rest - piece.astype(f32)
    return out


def _dot3(a, b, dims):
    (ah, al), (bh, bl) = _bf16_pieces(a, 2), _bf16_pieces(b, 2)
    return _dg(ah, bh, dims) + (_dg(ah, bl, dims) + _dg(al, bh, dims))


@jax.custom_vjp
def _hdot(a, b):
    return _dot3(a, b, _NN)


@jax.custom_vjp
def _hdot_nt(a, b):
    return _dot3(a, b, _NT)


@jax.custom_vjp
def _hdot_tn(a, b):
    return _dot3(a, b, _TN)


_hdot.defvjp(lambda a, b: (_hdot(a, b), (a, b)), lambda res, g: (_hdot_nt(g, res[1]), _hdot_tn(res[0], g)))
_hdot_nt.defvjp(lambda a, b: (_hdot_nt(a, b), (a, b)), lambda res, g: (_hdot(g, res[1]), _hdot_tn(g, res[0])))
_hdot_tn.defvjp(lambda a, b: (_hdot_tn(a, b), (a, b)), lambda res, g: (_hdot_nt(res[1], g), _hdot(res[0], g)))


_BNN, _BNT, _BTN = ((2,), (1,)), ((2,), (2,)), ((1,), (1,))


def _bdg(a, b, dims):
    return lax.dot_general(a, b, (dims, ((0,), (0,))), preferred_element_type=f32)


def _bdot3(a, b, dims):
    (ah, al), (bh, bl) = _bf16_pieces(a, 2), _bf16_pieces(b, 2)
    return _bdg(ah, bh, dims) + (_bdg(ah, bl, dims) + _bdg(al, bh, dims))


@jax.custom_vjp
def _hbnn(a, b):
    return _bdot3(a, b, _BNN)


@jax.custom_vjp
def _hbnt(a, b):
    return _bdot3(a, b, _BNT)


@jax.custom_vjp
def _hbtn(a, b):
    return _bdot3(a, b, _BTN)


_hbnn.defvjp(lambda a, b: (_hbnn(a, b), (a, b)), lambda res, g: (_hbnt(g, res[1]), _hbtn(res[0], g)))
_hbnt.defvjp(lambda a, b: (_hbnt(a, b), (a, b)), lambda res, g: (_hbnn(g, res[1]), _hbtn(g, res[0])))
_hbtn.defvjp(lambda a, b: (_hbtn(a, b), (a, b)), lambda res, g: (_hbnt(res[1], g), _hbnn(res[0], g)))


@functools.partial(jax.custom_vjp, nondiff_argnums=(2,))
def _exact_bl(m, x, transposed):
    mb = m.astype(bf16)
    hi, mid, lo = _bf16_pieces(x, 3)
    dims = _BTN if transposed else _BNN
    return (_bdg(mb, hi, dims) + _bdg(mb, mid, dims)) + _bdg(mb, lo, dims)


_exact_bl.defvjp(lambda m, x, transposed: (_exact_bl(m, x, transposed), m),
                 lambda transposed, m, g: (jnp.zeros_like(m), _exact_bl(m, g, not transposed)))


@functools.partial(jax.custom_vjp, nondiff_argnums=(2,))
def _exact_l(m, x, transposed):
    mb = m.astype(bf16)
    hi, mid, lo = _bf16_pieces(x, 3)
    dims = _TN if transposed else _NN
    return (_dg(mb, hi, dims) + _dg(mb, mid, dims)) + _dg(mb, lo, dims)


_exact_l.defvjp(lambda m, x, transposed: (_exact_l(m, x, transposed), m),
                lambda transposed, m, g: (jnp.zeros_like(m), _exact_l(m, g, not transposed)))


@functools.partial(jax.custom_vjp, nondiff_argnums=(2,))
def _exact_r(x, m, transposed):
    mb = m.astype(bf16)
    hi, mid, lo = _bf16_pieces(x, 3)
    dims = _NT if transposed else _NN
    return (_dg(hi, mb, dims) + _dg(mid, mb, dims)) + _dg(lo, mb, dims)


_exact_r.defvjp(lambda x, m, transposed: (_exact_r(x, m, transposed), m),
                lambda transposed, m, g: (_exact_r(g, m, not transposed), jnp.zeros_like(m)))


def _rms(x, g, eps=NORM_EPS):
    return x * lax.rsqrt(jnp.mean(x * x, axis=-1, keepdims=True) + eps) * g


def _sigmoid(x):
    return 1.0 / (1.0 + jnp.exp(-x))


def _softplus(x):
    return jnp.maximum(x, 0.0) + jnp.log(1.0 + jnp.exp(-jnp.maximum(x, -x)))


def _lane_split(x, sizes):
    bounds = np.cumsum([0] + list(sizes))

    @jax.custom_vjp
    def split(v):
        return tuple(v[..., int(bounds[q]):int(bounds[q + 1])] for q in range(len(sizes)))

    split.defvjp(lambda v: (split(v), None), lambda _, g: (jnp.concatenate(g, axis=-1),))
    return split(x)


def _row_split(x, sizes):
    bounds = np.cumsum([0] + list(sizes))

    @jax.custom_vjp
    def split(v):
        return tuple(v[..., int(bounds[q]):int(bounds[q + 1]), :] for q in range(len(sizes)))

    split.defvjp(lambda v: (split(v), None), lambda _, g: (jnp.concatenate(g, axis=-2),))
    return split(x)


def _shift_mats(t, k):
    r = lax.broadcasted_iota(jnp.int32, (t, t), 0)
    c = lax.broadcasted_iota(jnp.int32, (t, t), 1)
    inner = (r - c == k).astype(f32)
    r8 = lax.broadcasted_iota(jnp.int32, (t, 8), 0)
    c8 = lax.broadcasted_iota(jnp.int32, (t, 8), 1)
    edge = (c8 - r8 == 8 - k).astype(f32)
    return inner, edge


def _shift(x, halo, k):
    inner, edge = _shift_mats(x.shape[0], k)
    return _exact_l(inner, x, False) + jnp.dot(edge, halo, precision=HI, preferred_element_type=f32)


def mm(a, b, *, name, ta=False, tb=False, a_batched=False, b_batched=False, reduce_batch=False, add=None,
       act_grad=None, relu2_out=False, n_split=1, out_dtype=f32, tm=1024, tn=1024, tk=2048):
    ash, bsh = a.shape[-2:], b.shape[-2:]
    (k_, m_) = ash if ta else ash[::-1]
    (k2_, n_) = bsh[::-1] if tb else bsh
    assert k_ == k2_, (a.shape, b.shape, ta, tb)
    hb = a.shape[0] if a_batched else (b.shape[0] if b_batched else 1)
    batched_out = (a_batched or b_batched) and not reduce_batch
    h_out = hb if batched_out else 1
    h_red = hb if reduce_batch else 1
    tm = _pick(m_, tm, 128)
    tn = _pick(n_ // n_split, tn, 128)
    tk = _pick(k_, tk, 128)
    nm, nn, nk = m_ // tm, n_ // tn, k_ // tk

    def a_map(i, j, ho, hr, kk):
        blk = (kk, i) if ta else (i, kk)
        return ((ho if batched_out else hr),) + blk if a_batched else blk

    def b_map(i, j, ho, hr, kk):
        blk = (j, kk) if tb else (kk, j)
        return ((ho if batched_out else hr),) + blk if b_batched else blk

    a_blk = (tk, tm) if ta else (tm, tk)
    b_blk = (tn, tk) if tb else (tk, tn)
    in_specs = [pl.BlockSpec(((1,) + a_blk) if a_batched else a_blk, a_map),
                pl.BlockSpec(((1,) + b_blk) if b_batched else b_blk, b_map)]
    args = [a, b]
    for extra in (add, act_grad):
        if extra is not None:
            in_specs.append(pl.BlockSpec((tm, tn), lambda i, j, ho, hr, kk: (i, j)))
            args.append(extra)
    if n_split > 1:
        per = n_ // n_split // tn
        if batched_out:
            out_spec = pl.BlockSpec((1, 1, tm, tn), lambda i, j, ho, hr, kk: (j // per, ho, i, j % per))
            out_shape = jax.ShapeDtypeStruct((n_split, hb, m_, n_ // n_split), out_dtype)
        else:
            out_spec = pl.BlockSpec((1, tm, tn), lambda i, j, ho, hr, kk: (j // per, i, j % per))
            out_shape = jax.ShapeDtypeStruct((n_split, m_, n_ // n_split), out_dtype)
    elif batched_out:
        out_spec = pl.BlockSpec((1, tm, tn), lambda i, j, ho, hr, kk: (ho, i, j))
        out_shape = jax.ShapeDtypeStruct((hb, m_, n_), out_dtype)
    else:
        out_spec = pl.BlockSpec((tm, tn), lambda i, j, ho, hr, kk: (i, j))
        out_shape = jax.ShapeDtypeStruct((m_, n_), out_dtype)
    lead = (0,) * (int(batched_out) + int(n_split > 1))
    dims = ((0,) if ta else (1,), (1,) if tb else (0,))
    has_add, has_act = add is not None, act_grad is not None

    def body(*refs):
        a_ref, b_ref = refs[0], refs[1]
        pos = 2
        add_ref = act_ref = None
        if has_add:
            add_ref = refs[pos]
            pos += 1
        if has_act:
            act_ref = refs[pos]
            pos += 1
        o_ref, acc_ref = refs[pos], refs[-1]
        hr, kk = pl.program_id(3), pl.program_id(4)
        first = jnp.logical_and(hr == 0, kk == 0)
        last = jnp.logical_and(hr == h_red - 1, kk == nk - 1)
        av = a_ref[0] if a_batched else a_ref[...]
        bv = b_ref[0] if b_batched else b_ref[...]
        p = _bdot(av, bv, dims)
        single = h_red * nk == 1

        if not single:
            @pl.when(first)
            def _():
                acc_ref[...] = p

            @pl.when(jnp.logical_not(first))
            def _():
                acc_ref[...] += p

        @pl.when(last)
        def _():
            r = p if single else acc_ref[...]
            if has_act:
                r = r * (2.0 * jnp.maximum(act_ref[...], 0.0))
            if has_add:
                r = r + add_ref[...]
            if lead:
                o_ref[lead] = r.astype(out_dtype)
            else:
                o_ref[...] = r.astype(out_dtype)
            if relu2_out:
                refs[pos + 1][...] = jnp.square(jnp.maximum(r, 0.0)).astype(bf16)

    if relu2_out:
        assert not lead
        out_spec = [out_spec, out_spec]
        out_shape = [out_shape, jax.ShapeDtypeStruct(out_shape.shape, bf16)]
    return pl.pallas_call(
        body, name=name, grid=(nm, nn, h_out, h_red, nk), in_specs=in_specs, out_specs=out_spec,
        out_shape=out_shape, scratch_shapes=[pltpu.VMEM((tm, tn), f32)],
        compiler_params=_cparams(("parallel", "parallel", "parallel", "arbitrary", "arbitrary")),
    )(*args)


def _row_spec(arr, tile, idx):
    if arr.ndim == 2:
        return pl.BlockSpec((tile, arr.shape[1]), lambda i: (idx(i), 0))
    return pl.BlockSpec((arr.shape[0], tile, arr.shape[2]), lambda i: (0, idx(i), 0))


def _halo_spec(arr, tile, idx):
    per = tile // 8
    return pl.BlockSpec((8, arr.shape[1]), lambda i: (jnp.maximum(idx(i) * per - 1, 0), 0))


def _full_spec(arr):
    nd = arr.ndim
    return pl.BlockSpec(arr.shape, lambda i: (0,) * nd)


def _load_f32(ref):
    val = ref[...]
    return val.astype(f32) if val.dtype == bf16 else val


def rows_fwd(fn, rows, params, consts, out_shapes, *, tile, name, halos=()):
    s_len = rows[0].shape[-2]
    n = s_len // tile
    nr, nh, npar, nc = len(rows), len(halos), len(params), len(consts)
    ident = lambda i: i
    in_specs = ([_row_spec(r, tile, ident) for r in rows] + [_halo_spec(rows[h], tile, ident) for h in halos]
                + [_full_spec(p) for p in params] + [_full_spec(c) for c in consts])
    out_specs = [_row_spec(o, tile, ident) for o in out_shapes]

    def body(*refs):
        i = pl.program_id(0)
        rv = [_load_f32(r) for r in refs[:nr]]
        keep = (i > 0).astype(f32)
        hv = [r[...] * keep for r in refs[nr:nr + nh]]
        pv = [r[...] for r in refs[nr + nh:nr + nh + npar]]
        cv = [r[...] for r in refs[nr + nh + npar:nr + nh + npar + nc]]
        outs = fn(rv, hv, pv, cv)
        for o_ref, o in zip(refs[nr + nh + npar + nc:], outs):
            o_ref[...] = o.astype(o_ref.dtype)

    return pl.pallas_call(
        body, name=name, grid=(n,), in_specs=in_specs, out_specs=out_specs, out_shape=list(out_shapes),
        compiler_params=_cparams(("arbitrary",)),
    )(*rows, *[rows[h] for h in halos], *params, *consts)


def rows_bwd(fn, rows, params, consts, douts, *, tile, name, halos=(), grad_rows=None, extra=None,
             grad_dtypes=None):
    s_len = rows[0].shape[-2]
    n = s_len // tile
    nr, nh, npar, nc = len(rows), len(halos), len(params), len(consts)
    grad_rows = list(range(nr)) if grad_rows is None else list(grad_rows)
    extra = extra or {}
    assert all(h in grad_rows for h in halos)
    rev = lambda i: n - 1 - i
    dflat = [d for ds in douts for d in ds]
    dcount = [len(ds) for ds in douts]
    eflat = [e for g in grad_rows for e in extra.get(g, [])]
    ecount = [len(extra.get(g, [])) for g in grad_rows]
    in_specs = ([_row_spec(r, tile, rev) for r in rows] + [_halo_spec(rows[h], tile, rev) for h in halos]
                + [_full_spec(p) for p in params] + [_full_spec(c) for c in consts]
                + [_row_spec(d, tile, rev) for d in dflat] + [_row_spec(e, tile, rev) for e in eflat])
    grad_dtypes = [f32] * len(grad_rows) if grad_dtypes is None else list(grad_dtypes)
    assert all(grad_dtypes[q] == f32 for q, g in enumerate(grad_rows) if g in halos)
    out_shapes = ([jax.ShapeDtypeStruct(rows[g].shape, dt) for g, dt in zip(grad_rows, grad_dtypes)]
                  + [jax.ShapeDtypeStruct(p.shape, f32) for p in params])
    out_specs = [_row_spec(rows[g], tile, rev) for g in grad_rows] + [_full_spec(p) for p in params]
    scratch = [pltpu.VMEM((8, rows[h].shape[1]), f32) for h in halos]
    n_in = nr + nh + npar + nc + len(dflat) + len(eflat)
    n_out = len(grad_rows) + npar

    def body(*refs):
        i = pl.program_id(0)
        rv = [_load_f32(r) for r in refs[:nr]]
        keep = (i < n - 1).astype(f32)
        hv = [r[...] * keep for r in refs[nr:nr + nh]]
        pv = [r[...] for r in refs[nr + nh:nr + nh + npar]]
        pos = nr + nh + npar
        cv = [r[...] for r in refs[pos:pos + nc]]
        pos += nc
        dv = []
        for cnt in dcount:
            acc = _load_f32(refs[pos])
            for q in range(1, cnt):
                acc = acc + _load_f32(refs[pos + q])
            dv.append(acc)
            pos += cnt
        ev = []
        for cnt in ecount:
            ev.append([_load_f32(refs[pos + q]) for q in range(cnt)])
            pos += cnt
        out_refs = refs[n_in:n_in + n_out]
        carry_refs = refs[n_in + n_out:]

        def f(gr, gh, gp):
            full = list(rv)
            for g, val in zip(grad_rows, gr):
                full[g] = val
            return tuple(fn(full, gh, gp, cv))

        _, vjp = jax.vjp(f, [rv[g] for g in grad_rows], hv, pv)
        d_rows, d_halos, d_params = vjp(tuple(dv))

        @pl.when(i == 0)
        def _():
            for c_ref in carry_refs:
                c_ref[...] = jnp.zeros_like(c_ref)
            for p_ref in out_refs[len(grad_rows):]:
                p_ref[...] = jnp.zeros_like(p_ref)

        for q, g in enumerate(grad_rows):
            val = d_rows[q]
            for e in ev[q]:
                val = val + e
            out_refs[q][...] = val.astype(out_refs[q].dtype)
            if g in halos:
                hq = list(halos).index(g)
                out_refs[q][tile - 8:tile, :] += carry_refs[hq][...]
                carry_refs[hq][...] = d_halos[hq]
        for p_ref, dp in zip(out_refs[len(grad_rows):], d_params):
            p_ref[...] += dp

    res = pl.pallas_call(
        body, name=name, grid=(n,), in_specs=in_specs, out_specs=out_specs, out_shape=out_shapes,
        scratch_shapes=scratch, compiler_params=_cparams(("arbitrary",)),
    )(*rows, *[rows[h] for h in halos], *params, *consts, *dflat, *eflat)
    return list(res[:len(grad_rows)]), list(res[len(grad_rows):])


def _fn_norm(rows, halos, params, consts):
    return (_rms(rows[0], params[0]),)


def _fn_mla_prep(rows, halos, params, consts):
    cq, ckv, kpe = _lane_split(rows[0], (QL, KVL, MLA_PAD - QL - KVL))
    return _rms(cq, params[0]), _rms(ckv, params[1]), kpe


def _rope(x, cos, sin, rot):
    return x * cos + _exact_r(x, rot, False) * sin


def _fn_qk_post(rows, halos, params, consts):
    q_flat, kv_flat, kpe, cos, sin = rows
    q_norm, k_norm = params
    (rot,) = consts
    q_heads = _lane_split(q_flat, (DQK, Q_HEAD_PAD - DQK) * MLA_H)[::2]
    kv_heads = _lane_split(kv_flat, (NOPE, DV) * MLA_H)
    k_pe, _ = _lane_split(kpe, (ROPE, kpe.shape[1] - ROPE))
    qs = [_rope(_rms(qh, q_norm), cos, sin, rot) for qh in q_heads]
    ks = [_rope(_rms(jnp.concatenate([kv_heads[2 * h], k_pe], axis=-1), k_norm), cos, sin, rot)
          for h in range(MLA_H)]
    vs = [kv_heads[2 * h + 1] for h in range(MLA_H)]
    return jnp.stack(qs, axis=0), jnp.stack(ks, axis=0), jnp.stack(vs, axis=0)


def _seg(x, bd):
    return _exact_r(x, bd, False)


def _make_fn_rwkv_prep(vres):
    def fn(rows, halos, params, consts):
        cols = rows[0]
        bd = consts[0]
        mu, w0, w2, a0, a2, g2, k_k, k_a = params[:8]
        prev = _shift(cols, halos[0], 1)
        c = cols + (prev - cols) * mu
        r, k, v, xw, xa, xg = _lane_split(c, (RW, RW, RW, DL, AL, GL))
        log_w = -_softplus(-(w0 + _mm(jnp.tanh(xw), w2))) - 0.5
        ld = -jnp.exp(log_w)
        a = _sigmoid(a0 + _mm(xa, a2))
        g = _mm(_sigmoid(xg), g2)
        if vres:
            hcur, v_first = rows[1], rows[2]
            v1, v_mu, v0, v2 = params[8:12]
            xv = _mm(hcur, v1)
            xv_prev = _shift(xv, _mm(halos[1], v1), 1)
            xv = xv + (xv_prev - xv) * v_mu
            v = v + (v_first - v) * _sigmoid(v0 + _mm(xv, v2))
        kk = k * k_k
        kk = kk / jnp.maximum(jnp.sqrt(_seg(kk * kk, bd)), 1e-12)
        k2 = k * (1.0 + (a - 1.0) * k_a)
        return r, ld, k2, v, -kk, kk * a, g
    return fn


def _fn_rwkv_post(rows, halos, params, consts):
    y, r, k2, v, g = rows
    ln_w, ln_b, r_k = params
    bd = consts[0]
    mean = _seg(y, bd) * (1.0 / RN)
    d = y - mean
    var = _seg(d * d, bd) * (1.0 / RN)
    yn = d * lax.rsqrt(var + GN_EPS) * ln_w + ln_b
    bonus = _seg(r * k2 * r_k, bd) * v
    return ((yn + bonus) * g,)


def _fn_conv(rows, halos, params, consts):
    cols, halo = rows[0], halos[0]
    w0, w1, w2 = params
    b, c, x = _lane_split(cols, (CW, CW, CW))
    _, ch, xh = _lane_split(halo, (CW, CW, CW))
    u, uh = c * x, ch * xh
    return (b * (w0 * _shift(u, uh, 2) + w1 * _shift(u, uh, 1) + w2 * u),)


def _fn_merge(rows, halos, params, consts):
    gate, o_a, o_b, o_c = rows
    g_a, g_b, g_c = _lane_split(gate, (D, D, D))
    return (_sigmoid(g_a) * o_a + _sigmoid(g_b) * o_b + _sigmoid(g_c) * o_c,)


def _attn_block(q, k, v, q0, diagonal_last):
    tq, kend = q.shape[0], k.shape[0]
    s = _mm_nt(q, k) * (DQK ** -0.5)
    if diagonal_last:
        tri = lax.broadcasted_iota(jnp.int32, (tq, tq), 0) >= lax.broadcasted_iota(jnp.int32, (tq, tq), 1)
        if kend > tq:
            before, diag = _lane_split(s, (kend - tq, tq))
            s = jnp.concatenate([before, jnp.where(tri, diag, -1e30)], axis=-1)
        else:
            s = jnp.where(tri, s, -1e30)
    else:
        row = q0 + lax.broadcasted_iota(jnp.int32, (tq, kend), 0)
        col = lax.broadcasted_iota(jnp.int32, (tq, kend), 1)
        s = jnp.where(row >= col, s, -1e30)
    m = lax.stop_gradient(jnp.max(s, axis=-1, keepdims=True))
    e = jnp.exp(s - m)
    p = e / jnp.sum(e, axis=-1, keepdims=True)
    return _mm(p, v)


def _attn_segments(s_len, tq):
    per = max(1, s_len // tq // ATTN_SEGMENTS)
    return [(first, per, (first + per) * tq) for first in range(0, s_len // tq, per)]


HEAD_PAIR = 2


def attn_fwd(q, k, v, *, tq, name):
    h, s_len, _ = q.shape
    outs = []
    for seg, (first, nq, kend) in enumerate(_attn_segments(s_len, tq)):
        def body(q_ref, k_ref, v_ref, o_ref, first=first, nq=nq):
            q0 = (first + pl.program_id(1)) * tq
            o = [_attn_block(q_ref[j], k_ref[j], v_ref[j], q0, nq == 1) for j in range(HEAD_PAIR)]
            o_ref[...] = jnp.concatenate(o, axis=-1).astype(o_ref.dtype)

        outs.append(pl.pallas_call(
            body, name=f"{name}_{seg}", grid=(h // HEAD_PAIR, nq),
            in_specs=[pl.BlockSpec((HEAD_PAIR, tq, DQK), lambda hp, i, first=first: (hp, first + i, 0)),
                      pl.BlockSpec((HEAD_PAIR, kend, DQK), lambda hp, i: (hp, 0, 0)),
                      pl.BlockSpec((HEAD_PAIR, kend, DV), lambda hp, i: (hp, 0, 0))],
            out_specs=pl.BlockSpec((tq, HEAD_PAIR * DV), lambda hp, i: (i, hp)),
            out_shape=jax.ShapeDtypeStruct((nq * tq, h * DV), bf16),
            compiler_params=_cparams(("parallel", "arbitrary")),
        )(q, k, v))
    return jnp.concatenate(outs, axis=0)


def attn_bwd(q, k, v, do, *, tq, name):
    h, s_len, _ = q.shape
    dqs, dk_acc, dv_acc = [], None, None
    for seg, (first, nq, kend) in reversed(list(enumerate(_attn_segments(s_len, tq)))):
        carried = dk_acc is not None

        def body(*refs, first=first, carried=carried, nq=nq):
            q_ref, k_ref, v_ref, do_ref = refs[:4]
            dq_ref, dk_ref, dv_ref = refs[-3:]
            i = pl.program_id(1)
            do_heads = _lane_split(do_ref[...], (DV,) * HEAD_PAIR)
            for j in range(HEAD_PAIR):
                _, vjp = jax.vjp(functools.partial(_attn_block, q0=(first + i) * tq, diagonal_last=nq == 1),
                                 q_ref[j], k_ref[j], v_ref[j])
                dq, dk, dv = vjp(do_heads[j])
                dq_ref[j] = dq

                @pl.when(i == 0)
                def _():
                    dk_ref[j] = dk + refs[4][j] if carried else dk
                    dv_ref[j] = dv + refs[5][j] if carried else dv

                @pl.when(i > 0)
                def _():
                    dk_ref[j] += dk
                    dv_ref[j] += dv

        key_specs = [pl.BlockSpec((HEAD_PAIR, kend, DQK), lambda hp, i: (hp, 0, 0)),
                     pl.BlockSpec((HEAD_PAIR, kend, DV), lambda hp, i: (hp, 0, 0))]
        dq, dk_acc, dv_acc = pl.pallas_call(
            body, name=f"{name}_{seg}", grid=(h // HEAD_PAIR, nq),
            in_specs=[pl.BlockSpec((HEAD_PAIR, tq, DQK), lambda hp, i, first=first: (hp, first + i, 0))] + key_specs
            + [pl.BlockSpec((tq, HEAD_PAIR * DV), lambda hp, i, first=first: (first + i, hp))]
            + (key_specs if carried else []),
            out_specs=[pl.BlockSpec((HEAD_PAIR, tq, DQK), lambda hp, i: (hp, i, 0))] + key_specs,
            out_shape=[jax.ShapeDtypeStruct((h, nq * tq, DQK), f32), jax.ShapeDtypeStruct((h, s_len, DQK), f32),
                       jax.ShapeDtypeStruct((h, s_len, DV), f32)],
            input_output_aliases={4: 1, 5: 2} if carried else {},
            compiler_params=_cparams(("parallel", "arbitrary")),
        )(q, k, v, do, *([dk_acc, dv_acc] if carried else []))
        dqs.append(dq)
    return jnp.concatenate(dqs[::-1], axis=1), dk_acc, dv_acc


def _wkv_local(r, ld, k, v, a, b):
    nb, c, n = r.shape
    ri = lax.broadcasted_iota(jnp.int32, (c, c), 0)
    ci = lax.broadcasted_iota(jnp.int32, (c, c), 1)
    tri = jnp.broadcast_to((ri >= ci).astype(f32)[None], (nb, c, c))
    cum = _exact_bl(tri, ld, False)
    tot = jnp.sum(ld, axis=1, keepdims=True)
    w_incl, w_excl, w_inv, w_rest = jnp.exp(cum), jnp.exp(cum - ld), jnp.exp(-cum), jnp.exp(tot - cum)
    ab, rb, bb, kb = a * w_excl, r * w_incl, b * w_inv, k * w_inv
    bw, kw = b * w_rest, k * w_rest
    r2 = lax.broadcasted_iota(jnp.int32, (2 * c, 2 * c), 0)
    c2 = lax.broadcasted_iota(jnp.int32, (2 * c, 2 * c), 1)
    t_of, s_of = jnp.where(r2 >= c, r2 - c, r2), jnp.where(c2 >= c, c2 - c, c2)
    keep = jnp.logical_or(t_of > s_of, jnp.logical_and(r2 >= c, t_of == s_of))
    pair = jnp.where(keep[None], _hbnt(jnp.concatenate([ab, rb], axis=1), jnp.concatenate([bb, kb], axis=1)), 0.0)
    on_b, on_k = _lane_split(pair, (c, c))
    l_ab, m_rb = _row_split(on_b, (c, c))
    l_ak_v, m_rk_v = _row_split(_hbnn(on_k, v), (c, c))
    x = jnp.concatenate([ab, l_ak_v], axis=-1)
    lp, span = l_ab, 1
    while span < c:
        x = x + _hbnn(lp, x)
        span *= 2
        if span < c:
            lp = _hbnn(lp, lp)
    via_b_r, via_b_y = _lane_split(_hbnn(m_rb, x), (n, n))
    r_hat = rb + via_b_r
    y0 = via_b_y + m_rk_v
    from_b_g, from_b_z = _row_split(_hbtn(x, bw), (n, n))
    eye = lax.broadcasted_iota(jnp.int32, (n, n), 0) == lax.broadcasted_iota(jnp.int32, (n, n), 1)
    g = jnp.where(eye[None], jnp.exp(tot), 0.0) + from_b_g
    z = from_b_z + _hbtn(v, kw)
    return r_hat, y0, g, z


def _head(h):
    return slice(RN * h, RN * (h + 1))


def _load_chunk_heads(ref, c, per):
    return jnp.stack([ref[c * q:c * (q + 1), _head(h)] for q in range(per) for h in range(RH)], axis=0)


def _store_chunk_heads(ref, val, c, per):
    for q in range(per):
        ref[c * q:c * (q + 1), :] = jnp.concatenate([val[q * RH + h] for h in range(RH)], axis=-1)


def wkv_fwd(r, ld, k, v, a, b, *, name):
    s_len = r.shape[0]
    c = WKV_CHUNK
    n = s_len // c
    per = min(WKV_CHUNKS_PER_STEP, n)
    rows = pl.BlockSpec((c * per, RW), lambda i: (i, 0))
    mats = pl.BlockSpec((per, RH, RN, RN), lambda i: (i, 0, 0, 0))
    rows_t, mats_t = jax.ShapeDtypeStruct((s_len, RW), f32), jax.ShapeDtypeStruct((n, RH, RN, RN), f32)

    def local_body(r_ref, ld_ref, k_ref, v_ref, a_ref, b_ref, rh_ref, y0_ref, g_ref, z_ref):
        r_hat, y0, g, z = _wkv_local(*[_load_chunk_heads(ref, c, per)
                                       for ref in (r_ref, ld_ref, k_ref, v_ref, a_ref, b_ref)])
        _store_chunk_heads(rh_ref, r_hat, c, per)
        _store_chunk_heads(y0_ref, y0, c, per)
        g_ref[...] = g.reshape(per, RH, RN, RN)
        z_ref[...] = z.reshape(per, RH, RN, RN)

    r_hat, y0, g, z = pl.pallas_call(
        local_body, name=name + "_local", grid=(n // per,), in_specs=[rows] * 6, out_specs=[rows, rows, mats, mats],
        out_shape=[rows_t, rows_t, mats_t, mats_t], compiler_params=_cparams(("parallel",)),
    )(r, ld, k, v, a, b)

    def scan_body(g_ref, z_ref, st_ref, s_sc):
        s_sc[...] = jnp.zeros_like(s_sc)

        @pl.loop(0, n)
        def _(i):
            s0 = s_sc[...]
            st_ref[i] = s0
            s_sc[...] = _hbnn(s0, g_ref[i]) + z_ref[i]

    vm = pl.BlockSpec(memory_space=pltpu.VMEM)
    states = pl.pallas_call(
        scan_body, name=name + "_scan", in_specs=[vm, vm], out_specs=vm, out_shape=mats_t,
        scratch_shapes=[pltpu.VMEM((RH, RN, RN), f32)],
        compiler_params=pltpu.CompilerParams(vmem_limit_bytes=VMEM_LIMIT),
    )(g, z)

    def out_body(rh_ref, y0_ref, st_ref, y_ref):
        y = _hbnt(_load_chunk_heads(rh_ref, c, per), st_ref[...].reshape(per * RH, RN, RN))
        _store_chunk_heads(y_ref, y, c, per)
        y_ref[...] += y0_ref[...]

    y = pl.pallas_call(
        out_body, name=name + "_out", grid=(n // per,), in_specs=[rows, rows, mats], out_specs=rows,
        out_shape=rows_t, compiler_params=_cparams(("parallel",)),
    )(r_hat, y0, states)
    return y, dict(r_hat=r_hat, g=g, states=states)


def wkv_bwd(r, ld, k, v, a, b, saved, dy, *, name):
    s_len = r.shape[0]
    c = WKV_CHUNK
    n = s_len // c
    per = min(WKV_CHUNKS_PER_STEP, n)
    rows = pl.BlockSpec((c * per, RW), lambda i: (i, 0))
    mats = pl.BlockSpec((per, RH, RN, RN), lambda i: (i, 0, 0, 0))
    rows_t, mats_t = jax.ShapeDtypeStruct((s_len, RW), f32), jax.ShapeDtypeStruct((n, RH, RN, RN), f32)

    def out_body(dy_ref, rh_ref, st_ref, drh_ref, dsy_ref):
        dyb = _load_chunk_heads(dy_ref, c, per)
        _store_chunk_heads(drh_ref, _hbnn(dyb, st_ref[...].reshape(per * RH, RN, RN)), c, per)
        dsy_ref[...] = _hbtn(dyb, _load_chunk_heads(rh_ref, c, per)).reshape(per, RH, RN, RN)

    d_rhat, ds_y = pl.pallas_call(
        out_body, name=name + "_out", grid=(n // per,), in_specs=[rows, rows, mats], out_specs=[rows, mats],
        out_shape=[rows_t, mats_t], compiler_params=_cparams(("parallel",)),
    )(dy, saved["r_hat"], saved["states"])

    def scan_body(dsy_ref, g_ref, st_ref, dg_ref, dz_ref, ds_sc):
        ds_sc[...] = jnp.zeros_like(ds_sc)

        @pl.loop(0, n)
        def _(i):
            cidx = n - 1 - i
            ds_next = ds_sc[...]
            dz_ref[cidx] = ds_next
            dg_ref[cidx] = _hbtn(st_ref[cidx], ds_next)
            ds_sc[...] = dsy_ref[cidx] + _hbnt(ds_next, g_ref[cidx])

    vm = pl.BlockSpec(memory_space=pltpu.VMEM)
    d_g, d_z = pl.pallas_call(
        scan_body, name=name + "_scan", in_specs=[vm, vm, vm], out_specs=[vm, vm], out_shape=[mats_t, mats_t],
        scratch_shapes=[pltpu.VMEM((RH, RN, RN), f32)],
        compiler_params=pltpu.CompilerParams(vmem_limit_bytes=VMEM_LIMIT),
    )(ds_y, saved["g"], saved["states"])

    def local_body(r_ref, ld_ref, k_ref, v_ref, a_ref, b_ref, drh_ref, dy_ref, dg_ref, dz_ref, *out_refs):
        _, vjp = jax.vjp(_wkv_local, *[_load_chunk_heads(ref, c, per)
                                       for ref in (r_ref, ld_ref, k_ref, v_ref, a_ref, b_ref)])
        grads = vjp((_load_chunk_heads(drh_ref, c, per), _load_chunk_heads(dy_ref, c, per),
                     dg_ref[...].reshape(per * RH, RN, RN), dz_ref[...].reshape(per * RH, RN, RN)))
        for o_ref, val in zip(out_refs, grads):
            _store_chunk_heads(o_ref, val, c, per)

    return pl.pallas_call(
        local_body, name=name + "_local", grid=(n // per,), in_specs=[rows] * 8 + [mats, mats], out_specs=[rows] * 6,
        out_shape=[rows_t] * 6, compiler_params=_cparams(("parallel",)),
    )(r, ld, k, v, a, b, d_rhat, dy, d_g, d_z)


def loss_head(y, target, *, tile, name):
    s_len, d = y.shape
    n = s_len // tile

    def body(y_ref, t_ref, dy_ref, l_ref):
        err = y_ref[...] - t_ref[...]
        dy_ref[...] = err * (1.0 / d)
        part = 0.5 * jnp.sum(jnp.mean(err * err, axis=-1, keepdims=True), axis=0, keepdims=True)

        @pl.when(pl.program_id(0) == 0)
        def _():
            l_ref[...] = jnp.zeros_like(l_ref)

        l_ref[...] += jnp.broadcast_to(part, l_ref.shape)

    bs = pl.BlockSpec((tile, d), lambda i: (i, 0))
    dy, l = pl.pallas_call(
        body, name=name, grid=(n,), in_specs=[bs, bs],
        out_specs=[bs, pl.BlockSpec((8, 128), lambda i: (0, 0))],
        out_shape=[jax.ShapeDtypeStruct((s_len, d), f32), jax.ShapeDtypeStruct((8, 128), f32)],
        compiler_params=_cparams(("arbitrary",)),
    )(y, target)
    return l[0, 0], dy


def _adamw_update(w, g, m, v):
    mn = B1 * m + (1.0 - B1) * g
    vn = B2 * v + (1.0 - B2) * (g * g)
    delta = -LR * ((mn / (1.0 - B1 ** STEP)) / (jnp.sqrt(vn / (1.0 - B2 ** STEP)) + EPS) + WD * w)
    return delta, mn, vn


def adamw(w, g, m, v, *, name):
    rows, cols = w.shape
    tile = _pick(rows, max(8, (2 * 1024 * 1024 // (4 * cols)) // 8 * 8), 8)

    def body(w_ref, g_ref, m_ref, v_ref, g_out, d_out, m_out, v_out):
        gv = g_ref[...]
        d_out[...], m_out[...], v_out[...] = _adamw_update(w_ref[...], gv, m_ref[...], v_ref[...])
        g_out[...] = gv

    bs = pl.BlockSpec((tile, cols), lambda i: (i, 0))
    return pl.pallas_call(
        body, name=name, grid=(rows // tile,), in_specs=[bs] * 4, out_specs=[bs] * 4,
        out_shape=[jax.ShapeDtypeStruct((rows, cols), f32)] * 4, compiler_params=_cparams(("parallel",)),
    )(w, g, m, v)


def adamw_many(ws, gs, ms, vs, *, name):
    n = len(ws)

    def body(*refs):
        ins, outs = refs[:4 * n], refs[4 * n:]
        for i in range(n):
            outs[i][...], outs[n + i][...], outs[2 * n + i][...] = _adamw_update(
                ins[i][...], ins[n + i][...], ins[2 * n + i][...], ins[3 * n + i][...])

    vm = pl.BlockSpec(memory_space=pltpu.VMEM)
    res = pl.pallas_call(
        body, name=name, in_specs=[vm] * (4 * n), out_specs=[vm] * (3 * n),
        out_shape=[jax.ShapeDtypeStruct(w.shape, f32) for w in ws] * 3,
        compiler_params=pltpu.CompilerParams(vmem_limit_bytes=VMEM_LIMIT),
    )(*ws, *gs, *ms, *vs)
    return list(res[:n]), list(res[n:2 * n]), list(res[2 * n:])


def _place():
    return lax.axis_index("x"), lax.axis_index("y"), lax.axis_index("c")


_ANY = pl.BlockSpec(memory_space=pl.ANY)


def _peer_chips(x, y):
    return [(1 - x, y), (x, 1 - y), (1 - x, 1 - y)]


def gather_weights(shards, *, name):
    nk = len(shards)

    def body(*refs):
        srcs, outs = refs[:nk], refs[nk:2 * nk]
        ici_send, ici_recv, d2d_send, d2d_recv = refs[2 * nk + 1:]
        x, y, c = _place()
        me = 2 * x + y
        peers = _peer_chips(x, y)
        pending = []
        for k in range(nk):
            half = srcs[k].shape[0] // 2
            mine = pl.ds(c * half, half)
            for p, (px, py) in enumerate(peers):
                cp = pltpu.make_async_remote_copy(
                    src_ref=srcs[k].at[mine], dst_ref=outs[k].at[me, mine], send_sem=ici_send.at[k, p],
                    recv_sem=ici_recv.at[k, p], device_id=(px, py, c), device_id_type=MESH)
                cp.start()
                pending.append(cp)
        for k in range(nk):
            half = srcs[k].shape[0] // 2
            mine = pl.ds(c * half, half)
            for p, (px, py) in enumerate(peers):
                landed = outs[k].at[2 * px + py, mine]
                pltpu.make_async_remote_copy(
                    src_ref=srcs[k].at[mine], dst_ref=landed, send_sem=ici_send.at[k, p], recv_sem=ici_recv.at[k, p],
                    device_id=(px, py, c), device_id_type=MESH).wait_recv()
                fwd = pltpu.make_async_remote_copy(
                    src_ref=landed, dst_ref=landed, send_sem=d2d_send.at[k, p], recv_sem=d2d_recv.at[k, p],
                    device_id=(x, y, 1 - c), device_id_type=MESH)
                fwd.start()
                pending.append(fwd)
        for k in range(nk):
            half = srcs[k].shape[0] // 2
            other = pl.ds((1 - c) * half, half)
            for p, (px, py) in enumerate(peers):
                theirs = outs[k].at[2 * px + py, other]
                pltpu.make_async_remote_copy(
                    src_ref=theirs, dst_ref=theirs, send_sem=d2d_send.at[k, p], recv_sem=d2d_recv.at[k, p],
                    device_id=(x, y, 1 - c), device_id_type=MESH).wait_recv()
        for cp in pending:
            cp.wait_send()
        refs[2 * nk][...] = jnp.zeros_like(refs[2 * nk])

    sem = lambda *shape: pltpu.SemaphoreType.DMA(shape)
    res = pl.pallas_call(
        body, name=name, in_specs=[_ANY] * nk, out_specs=[_ANY] * nk + [pl.BlockSpec(memory_space=pltpu.VMEM)],
        out_shape=[jax.ShapeDtypeStruct((4,) + s.shape, s.dtype) for s in shards]
        + [jax.ShapeDtypeStruct((8, 128), f32)],
        scratch_shapes=[sem(nk, 3), sem(nk, 3), sem(nk, 3), sem(nk, 3)],
    )(*shards)
    return list(res[:nk]), res[nk]


_HBM = pl.BlockSpec(memory_space=pltpu.HBM)
_SEM = pl.BlockSpec(memory_space=pltpu.SEMAPHORE)


def _ici_half_copies(srcs, lands, send_sems, recv_sems, incoming):
    x, y, c = _place()
    me = 2 * x + y
    out = []
    for k in range(len(srcs)):
        half = srcs[k].shape[0] // 2
        mine = pl.ds(c * half, half)
        for p, (px, py) in enumerate(_peer_chips(x, y)):
            out.append(pltpu.make_async_remote_copy(
                src_ref=srcs[k].at[mine], dst_ref=lands[k].at[(2 * px + py) if incoming else me, mine],
                send_sem=send_sems.at[3 * k + p], recv_sem=recv_sems.at[3 * k + p], device_id=(px, py, c),
                device_id_type=MESH))
    return out


def gather_start(shards, *, name):
    nk = len(shards)

    def body(*refs):
        srcs, lands = refs[:nk], refs[nk:2 * nk]
        send_sems, recv_sems = refs[2 * nk], refs[2 * nk + 1]
        token = refs[-1]
        for outgoing in _ici_half_copies(srcs, lands, send_sems, recv_sems, incoming=False):
            outgoing.start()
        token[...] = jnp.zeros_like(token)

    lands = [pltpu.with_memory_space_constraint(lax.empty((4,) + s.shape, s.dtype), pltpu.HBM) for s in shards]
    res = pl.pallas_call(
        body, name=name,
        out_shape=(pltpu.SemaphoreType.DMA((3 * nk,)), pltpu.SemaphoreType.DMA((3 * nk,)),
                   *[pltpu.HBM(s.shape, s.dtype) for s in shards], *[pltpu.HBM(z.shape, z.dtype) for z in lands],
                   jax.ShapeDtypeStruct((8, 128), f32)),
        in_specs=[_HBM] * (2 * nk), out_specs=(_SEM, _SEM, *[_HBM] * (2 * nk), pl.BlockSpec(memory_space=pltpu.VMEM)),
        input_output_aliases={k: 2 + k for k in range(2 * nk)},
        compiler_params=pltpu.CompilerParams(has_side_effects=pltpu.SideEffectType.DATAFLOW_SIDE_EFFECTING),
    )(*[pltpu.with_memory_space_constraint(s, pltpu.HBM) for s in shards], *lands)
    return res[0], res[1], list(res[2:2 + nk]), list(res[2 + nk:2 + 2 * nk]), res[-1]


def gather_wait(send_sems, recv_sems, shards, lands, after, *, name):
    nk = len(shards)

    def body(*refs):
        srcs, zones = refs[:nk], refs[nk:2 * nk]
        for outgoing in _ici_half_copies(srcs, zones, refs[2 * nk], refs[2 * nk + 1], incoming=False):
            outgoing.wait_send()
        for landed in _ici_half_copies(srcs, zones, refs[2 * nk], refs[2 * nk + 1], incoming=True):
            landed.wait_recv()

    res = pl.pallas_call(
        body, name=name,
        out_shape=(*[pltpu.HBM(s.shape, s.dtype) for s in shards], *[pltpu.HBM(z.shape, z.dtype) for z in lands]),
        in_specs=[_HBM] * (2 * nk) + [_SEM, _SEM, _ANY], out_specs=tuple([_HBM] * (2 * nk)),
        input_output_aliases={k: k for k in range(2 * nk)},
        compiler_params=pltpu.CompilerParams(has_side_effects=pltpu.SideEffectType.DATAFLOW_SIDE_EFFECTING),
    )(*shards, *lands, send_sems, recv_sems, after)
    return list(res[:nk]), list(res[nk:])


def gather_forward(lands, *, name):
    nk = len(lands)

    def body(*refs):
        zones = refs[nk:2 * nk]
        send_sems, recv_sems = refs[2 * nk:]
        x, y, c = _place()
        sends = []
        for k in range(nk):
            half = zones[k].shape[1] // 2
            for p, (px, py) in enumerate(_peer_chips(x, y)):
                landed = zones[k].at[2 * px + py, pl.ds(c * half, half)]
                cp = pltpu.make_async_remote_copy(
                    src_ref=landed, dst_ref=landed, send_sem=send_sems.at[k, p], recv_sem=recv_sems.at[k, p],
                    device_id=(x, y, 1 - c), device_id_type=MESH)
                cp.start()
                sends.append(cp)
        for k in range(nk):
            half = zones[k].shape[1] // 2
            for p, (px, py) in enumerate(_peer_chips(x, y)):
                theirs = zones[k].at[2 * px + py, pl.ds((1 - c) * half, half)]
                pltpu.make_async_remote_copy(
                    src_ref=theirs, dst_ref=theirs, send_sem=send_sems.at[k, p], recv_sem=recv_sems.at[k, p],
                    device_id=(x, y, 1 - c), device_id_type=MESH).wait_recv()
        for cp in sends:
            cp.wait_send()

    return pl.pallas_call(
        body, name=name, in_specs=[_ANY] * nk, out_specs=[_ANY] * nk,
        out_shape=[jax.ShapeDtypeStruct(z.shape, z.dtype) for z in lands],
        input_output_aliases={k: k for k in range(nk)},
        scratch_shapes=[pltpu.SemaphoreType.DMA((nk, 3)), pltpu.SemaphoreType.DMA((nk, 3))],
    )(*lands)


def _to_sibling_copies(srcs, lands, send_sems, recv_sems):
    x, y, c = _place()
    out = []
    for k in range(len(srcs)):
        half = srcs[k].shape[1] // 2
        out.append(pltpu.make_async_remote_copy(
            src_ref=srcs[k].at[:, pl.ds((1 - c) * half, half), :], dst_ref=lands[k], send_sem=send_sems.at[k],
            recv_sem=recv_sems.at[k], device_id=(x, y, 1 - c), device_id_type=MESH))
    return out


def sibling_start(parts, *, name):
    nk = len(parts)

    def body(*refs):
        for cp in _to_sibling_copies(refs[:nk], refs[nk:2 * nk], refs[2 * nk], refs[2 * nk + 1]):
            cp.start()
        refs[-1][...] = jnp.zeros_like(refs[-1])

    lands = [pltpu.with_memory_space_constraint(lax.empty((4, p.shape[1] // 2, p.shape[2]), p.dtype), pltpu.HBM)
             for p in parts]
    res = pl.pallas_call(
        body, name=name,
        out_shape=(pltpu.SemaphoreType.DMA((nk,)), pltpu.SemaphoreType.DMA((nk,)),
                   *[pltpu.HBM(p.shape, p.dtype) for p in parts], *[pltpu.HBM(z.shape, z.dtype) for z in lands],
                   jax.ShapeDtypeStruct((8, 128), f32)),
        in_specs=[_HBM] * (2 * nk), out_specs=(_SEM, _SEM, *[_HBM] * (2 * nk), pl.BlockSpec(memory_space=pltpu.VMEM)),
        input_output_aliases={k: 2 + k for k in range(2 * nk)},
        compiler_params=pltpu.CompilerParams(has_side_effects=pltpu.SideEffectType.DATAFLOW_SIDE_EFFECTING),
    )(*[pltpu.with_memory_space_constraint(p, pltpu.HBM) for p in parts], *lands)
    return res[0], res[1], list(res[2:2 + nk]), list(res[2 + nk:2 + 2 * nk]), res[-1]


def sibling_wait(send_sems, recv_sems, parts, lands, after, *, name):
    nk = len(parts)

    def body(*refs):
        copies = _to_sibling_copies(refs[:nk], refs[nk:2 * nk], refs[2 * nk], refs[2 * nk + 1])
        for cp in copies:
            cp.wait_send()
        for cp in copies:
            cp.wait_recv()

    res = pl.pallas_call(
        body, name=name,
        out_shape=(*[pltpu.HBM(p.shape, p.dtype) for p in parts], *[pltpu.HBM(z.shape, z.dtype) for z in lands]),
        in_specs=[_HBM] * (2 * nk) + [_SEM, _SEM, _ANY], out_specs=tuple([_HBM] * (2 * nk)),
        input_output_aliases={k: k for k in range(2 * nk)},
        compiler_params=pltpu.CompilerParams(has_side_effects=pltpu.SideEffectType.DATAFLOW_SIDE_EFFECTING),
    )(*parts, *lands, send_sems, recv_sems, after)
    return list(res[:nk]), list(res[nk:])


def pair_sum(part, theirs, core, *, name):
    _, rows, cols = part.shape
    half = rows // 2
    tile = _pick(half, max(16, (1 << 20) // (4 * cols) // 16 * 16), 16)
    per = half // tile

    def body(c_ref, p_ref, t_ref, o_ref):
        o_ref[...] = (p_ref[...].astype(f32) + t_ref[...].astype(f32)).astype(bf16)

    grid_spec = pltpu.PrefetchScalarGridSpec(
        num_scalar_prefetch=1, grid=(4, per),
        in_specs=[pl.BlockSpec((1, tile, cols), lambda j, i, c_ref: (j, c_ref[0] * per + i, 0)),
                  pl.BlockSpec((1, tile, cols), lambda j, i, c_ref: (j, i, 0))],
        out_specs=pl.BlockSpec((1, tile, cols), lambda j, i, c_ref: (j, i, 0)))
    return pl.pallas_call(
        body, name=name, grid_spec=grid_spec, out_shape=jax.ShapeDtypeStruct((4, half, cols), bf16),
        compiler_params=_cparams(("parallel", "parallel")),
    )(core, part, theirs)


def _all_to_all_copies(srcs, lands, send_sems, recv_sems, incoming):
    x, y, c = _place()
    me = 2 * x + y
    out = []
    for k in range(len(srcs)):
        for p, (px, py) in enumerate(_peer_chips(x, y)):
            peer = 2 * px + py
            out.append(pltpu.make_async_remote_copy(
                src_ref=srcs[k].at[peer], dst_ref=lands[k].at[peer if incoming else me],
                send_sem=send_sems.at[3 * k + p], recv_sem=recv_sems.at[3 * k + p], device_id=(px, py, c),
                device_id_type=MESH))
    return out


def scatter_start(parts, *, name):
    nk = len(parts)

    def body(*refs):
        srcs, lands = refs[:nk], refs[nk:2 * nk]
        for outgoing in _all_to_all_copies(srcs, lands, refs[2 * nk], refs[2 * nk + 1], incoming=False):
            outgoing.start()
        refs[-1][...] = jnp.zeros_like(refs[-1])

    lands = [pltpu.with_memory_space_constraint(lax.empty(p.shape, p.dtype), pltpu.HBM) for p in parts]
    res = pl.pallas_call(
        body, name=name,
        out_shape=(pltpu.SemaphoreType.DMA((3 * nk,)), pltpu.SemaphoreType.DMA((3 * nk,)),
                   *[pltpu.HBM(p.shape, p.dtype) for p in parts], *[pltpu.HBM(p.shape, p.dtype) for p in parts],
                   jax.ShapeDtypeStruct((8, 128), f32)),
        in_specs=[_HBM] * (2 * nk), out_specs=(_SEM, _SEM, *[_HBM] * (2 * nk), pl.BlockSpec(memory_space=pltpu.VMEM)),
        input_output_aliases={k: 2 + k for k in range(2 * nk)},
        compiler_params=pltpu.CompilerParams(has_side_effects=pltpu.SideEffectType.DATAFLOW_SIDE_EFFECTING),
    )(*[pltpu.with_memory_space_constraint(p, pltpu.HBM) for p in parts], *lands)
    return res[0], res[1], list(res[2:2 + nk]), list(res[2 + nk:2 + 2 * nk]), res[-1]


def scatter_wait(send_sems, recv_sems, parts, lands, after, *, name):
    nk = len(parts)

    def body(*refs):
        srcs, zones = refs[:nk], refs[nk:2 * nk]
        for outgoing in _all_to_all_copies(srcs, zones, refs[2 * nk], refs[2 * nk + 1], incoming=False):
            outgoing.wait_send()
        for landed in _all_to_all_copies(srcs, zones, refs[2 * nk], refs[2 * nk + 1], incoming=True):
            landed.wait_recv()

    res = pl.pallas_call(
        body, name=name,
        out_shape=(*[pltpu.HBM(p.shape, p.dtype) for p in parts], *[pltpu.HBM(z.shape, z.dtype) for z in lands]),
        in_specs=[_HBM] * (2 * nk) + [_SEM, _SEM, _ANY], out_specs=tuple([_HBM] * (2 * nk)),
        input_output_aliases={k: k for k in range(2 * nk)},
        compiler_params=pltpu.CompilerParams(has_side_effects=pltpu.SideEffectType.DATAFLOW_SIDE_EFFECTING),
    )(*parts, *lands, send_sems, recv_sems, after)
    return list(res[:nk]), list(res[nk:])


def join_halves(bufs, layout, *, name):
    nk, nb = len(layout), len(bufs)

    def body(*refs):
        outs = refs[nb:2 * nb]
        send_sems, recv_sems = refs[2 * nb:]
        x, y, c = _place()
        pending = []
        for k, (o, off, rows) in enumerate(layout):
            half = rows // 2
            mine = outs[o].at[pl.ds(off + c * half, half), :]
            cp = pltpu.make_async_remote_copy(
                src_ref=mine, dst_ref=mine, send_sem=send_sems.at[k], recv_sem=recv_sems.at[k],
                device_id=(x, y, 1 - c), device_id_type=MESH)
            cp.start()
            pending.append(cp)
        for k, (o, off, rows) in enumerate(layout):
            half = rows // 2
            theirs = outs[o].at[pl.ds(off + (1 - c) * half, half), :]
            pltpu.make_async_remote_copy(
                src_ref=theirs, dst_ref=theirs, send_sem=send_sems.at[k], recv_sem=recv_sems.at[k],
                device_id=(x, y, 1 - c), device_id_type=MESH).wait_recv()
        for cp in pending:
            cp.wait_send()

    return pl.pallas_call(
        body, name=name, in_specs=[_ANY] * nb, out_specs=[_ANY] * nb,
        out_shape=[jax.ShapeDtypeStruct(b.shape, b.dtype) for b in bufs],
        input_output_aliases={o: o for o in range(nb)},
        scratch_shapes=[pltpu.SemaphoreType.DMA((nk,)), pltpu.SemaphoreType.DMA((nk,))],
    )(*bufs)


def place_slab(dest, src, index, *, name):
    rows, cols = src.shape
    tile = _pick(rows, max(16, (1 << 20) // (src.dtype.itemsize * cols) // 16 * 16), 16)

    def body(i_ref, s_ref, d_ref, o_ref):
        del i_ref, d_ref
        o_ref[0] = s_ref[...]

    grid_spec = pltpu.PrefetchScalarGridSpec(
        num_scalar_prefetch=1, grid=(rows // tile,),
        in_specs=[pl.BlockSpec((tile, cols), lambda i, idx: (i, 0)), _ANY],
        out_specs=pl.BlockSpec((1, tile, cols), lambda i, idx: (idx[0], i, 0)))
    return pl.pallas_call(
        body, name=name, grid_spec=grid_spec, out_shape=jax.ShapeDtypeStruct(dest.shape, dest.dtype),
        input_output_aliases={2: 0}, compiler_params=_cparams(("parallel",)),
    )(index, src, dest)


def _broadcast_copies(src, land, send_sems, recv_sems, incoming):
    x, y, c = _place()
    me = 4 * x + 2 * y + c
    out = []
    for m in range(1, 8):
        px, py, pc = x ^ (m >> 2), y ^ ((m >> 1) & 1), c ^ (m & 1)
        out.append(pltpu.make_async_remote_copy(
            src_ref=src, dst_ref=land.at[(4 * px + 2 * py + pc) if incoming else me], send_sem=send_sems.at[m - 1],
            recv_sem=recv_sems.at[m - 1], device_id=(px, py, pc), device_id_type=MESH))
    return out


def broadcast_start(src, *, name):
    def body(s_ref, l_ref, send_sems, recv_sems, s_thru, l_thru, token):
        for outgoing in _broadcast_copies(s_ref, l_ref, send_sems, recv_sems, incoming=False):
            outgoing.start()
        token[...] = jnp.zeros_like(token)

    land = pltpu.with_memory_space_constraint(lax.empty((8,) + src.shape, src.dtype), pltpu.HBM)
    return pl.pallas_call(
        body, name=name,
        out_shape=(pltpu.SemaphoreType.DMA((7,)), pltpu.SemaphoreType.DMA((7,)), pltpu.HBM(src.shape, src.dtype),
                   pltpu.HBM(land.shape, land.dtype), jax.ShapeDtypeStruct((8, 128), f32)),
        in_specs=[_HBM, _HBM], out_specs=(_SEM, _SEM, _HBM, _HBM, pl.BlockSpec(memory_space=pltpu.VMEM)),
        input_output_aliases={0: 2, 1: 3},
        compiler_params=pltpu.CompilerParams(has_side_effects=pltpu.SideEffectType.DATAFLOW_SIDE_EFFECTING),
    )(pltpu.with_memory_space_constraint(src, pltpu.HBM), land)


def broadcast_wait(send_sems, recv_sems, src, land, after, *, name):
    def body(s_ref, l_ref, send_sems, recv_sems, after_ref, s_out, l_out):
        for outgoing in _broadcast_copies(s_ref, l_ref, send_sems, recv_sems, incoming=False):
            outgoing.wait_send()
        for landed in _broadcast_copies(s_ref, l_ref, send_sems, recv_sems, incoming=True):
            landed.wait_recv()

    return pl.pallas_call(
        body, name=name, out_shape=(pltpu.HBM(src.shape, src.dtype), pltpu.HBM(land.shape, land.dtype)),
        in_specs=[_HBM, _HBM, _SEM, _SEM, _ANY], out_specs=(_HBM, _HBM), input_output_aliases={0: 0, 1: 1},
        compiler_params=pltpu.CompilerParams(has_side_effects=pltpu.SideEffectType.DATAFLOW_SIDE_EFFECTING),
    )(src, land, send_sems, recv_sems, after)


def sum8(land, own, device, *, name):
    _, rows, cols = land.shape

    def body(d_ref, l_ref, o_ref, out_ref):
        mine = o_ref[...]
        acc = jnp.where(d_ref[0] == 0, mine, l_ref[0])
        for d in range(1, 8):
            acc = acc + jnp.where(d_ref[0] == d, mine, l_ref[d])
        out_ref[...] = acc

    grid_spec = pltpu.PrefetchScalarGridSpec(
        num_scalar_prefetch=1, grid=(1,),
        in_specs=[pl.BlockSpec((8, rows, cols), lambda i, d_ref: (0, 0, 0)),
                  pl.BlockSpec((rows, cols), lambda i, d_ref: (0, 0))],
        out_specs=pl.BlockSpec((rows, cols), lambda i, d_ref: (0, 0)))
    return pl.pallas_call(
        body, name=name, grid_spec=grid_spec, out_shape=jax.ShapeDtypeStruct((rows, cols), f32),
        compiler_params=_cparams(("arbitrary",)),
    )(device, land, own)


def sum4_into(arrived, own, dest, where, *, layer, total_rows, name):
    _, rows, cols = arrived.shape
    tile = _pick(rows, max(16, (1 << 20) // (4 * cols) // 16 * 16), 16)
    per = rows // tile

    def body(w_ref, a_ref, own_ref, *rest):
        mine = own_ref[0].astype(f32)
        p = [jnp.where(w_ref[1] == j, mine, a_ref[j].astype(f32)) for j in range(4)]
        rest[-1][...] = ((p[0] + p[1]) + p[2]) + p[3]

    grid_spec = pltpu.PrefetchScalarGridSpec(
        num_scalar_prefetch=1, grid=(per,),
        in_specs=[pl.BlockSpec((4, tile, cols), lambda i, w_ref: (0, i, 0)),
                  pl.BlockSpec((1, tile, cols), lambda i, w_ref: (w_ref[1], i, 0))] + ([] if dest is None else [_ANY]),
        out_specs=pl.BlockSpec((tile, cols), lambda i, w_ref: ((2 * layer + w_ref[0]) * per + i, 0)))
    return pl.pallas_call(
        body, name=name, grid_spec=grid_spec, out_shape=jax.ShapeDtypeStruct((total_rows, cols), f32),
        input_output_aliases={} if dest is None else {3: 0}, compiler_params=_cparams(("parallel",)),
    )(where, arrived, own, *([] if dest is None else [dest]))


def _consts():
    idx = np.arange(RW)
    bd = (idx[:, None] // RN == idx[None, :] // RN).astype(np.float32)
    rot = np.zeros((DQK, DQK), np.float32)
    half = ROPE // 2
    rot[NOPE + half + np.arange(half), NOPE + np.arange(half)] = -1.0
    rot[NOPE + np.arange(half), NOPE + half + np.arange(half)] = 1.0
    return jnp.asarray(bd), jnp.asarray(rot)


def _rope_tables(positions):
    freqs = ROPE_THETA ** (-(jnp.arange(ROPE // 2, dtype=f32) * 2.0 / ROPE))
    ang = positions.astype(f32)[:, None] * freqs
    cos, sin = jnp.cos(ang), jnp.sin(ang)
    ones = jnp.ones((positions.shape[0], NOPE), f32)
    return (jnp.concatenate([ones, cos, cos], axis=-1), jnp.concatenate([0.0 * ones, sin, sin], axis=-1))


STAGES = (("w_in",), ("mla_wq_b", "mla_wkv_b", "mla_w_o", "rwkv_w_o", "conv_w_o", "w_out"), ("w_up", "w_down"))


def derive_stage(stage, w):
    if stage == 0:
        w_in = w["w_in"]
        pad = jnp.zeros((D, MLA_PAD - MLA_COLS), w_in.dtype)
        return dict(gate=w_in[:, :GATE], mla=jnp.concatenate([w_in[:, GATE:GATE + MLA_COLS], pad], axis=1),
                    rw=w_in[:, GATE + MLA_COLS:GATE + MLA_COLS + 4 * RW], cv=w_in[:, GATE + MLA_COLS + 4 * RW:])
    if stage == 1:
        wq = jnp.pad(w["mla_wq_b"].reshape(QL, MLA_H, DQK), ((0, 0), (0, 0), (0, Q_HEAD_PAD - DQK)))
        return dict(wq=wq.reshape(QL, MLA_H * Q_HEAD_PAD), wkv=w["mla_wkv_b"], wo=w["mla_w_o"], rwo=w["rwkv_w_o"],
                    cvo=w["conv_w_o"], out=w["w_out"])
    return dict(up=w["w_up"], down=w["w_down"])


W_IN_WINDOW_TILE = (0, 10, 21, 31)
W_IN_WINDOW = 1664
W_IN_SHARD = 1384


def w_in_window_cols(win, chip):
    gap = MLA_PAD - MLA_COLS
    branches = []
    for j in range(4):
        lo, hi = W_IN_SHARD * j, W_IN_SHARD * (j + 1)
        base = 128 * W_IN_WINDOW_TILE[j]
        cut = GATE + MLA_COLS
        if hi <= cut:
            branches.append(lambda w, a=lo - base: w[:, a:a + W_IN_SHARD])
        elif lo >= cut:
            branches.append(lambda w, a=lo + gap - base: w[:, a:a + W_IN_SHARD])
        else:
            branches.append(lambda w, a=lo - base, n1=cut - lo, b=cut + gap - base, n2=hi - cut:
                            jnp.concatenate([w[:, a:a + n1], w[:, b:b + n2]], axis=1))
    return lax.switch(chip, branches, win)


def chip_major_grads(stage, g):
    if stage == 0:
        padded = jnp.concatenate([g["gate"], g["mla"], g["rw"], g["cv"]], axis=1)
        return dict(w_in=jnp.stack([padded[:, 128 * t:128 * t + W_IN_WINDOW] for t in W_IN_WINDOW_TILE]))
    if stage == 1:
        wq = g["wq"].reshape(QL, MLA_H, Q_HEAD_PAD)[:, :, :DQK].reshape(QL, 4, -1).transpose(1, 0, 2)
        return dict(mla_wq_b=wq, mla_wkv_b=g["wkv"], mla_w_o=g["wo"], rwkv_w_o=g["rwo"], conv_w_o=g["cvo"],
                    w_out=g["out"].reshape(4, D // 4, D))
    return dict(w_up=g["up"], w_down=g["down"].reshape(4, DFF // 4, D))


def _row(v):
    return v.reshape(1, -1)


def local_step(x, positions, target, w, sm, big_of=None, on_grads=None, on_small=None):
    if big_of is None:
        big_of = lambda l, stage, _after: {n: w[n][l] for n in STAGES[stage]}
    if on_grads is None:
        on_grads = lambda l, stage, slabs: None
    if on_small is None:
        on_small = lambda l, layer_small: None
    s_len = x.shape[0]
    t_row = _pick(s_len, 256, 8)
    t_wide = _pick(s_len, 128, 8)
    bd, rot = _consts()
    cos, sin = _rope_tables(positions)
    sds = lambda *shape: jax.ShapeDtypeStruct(shape, f32)
    sdb = lambda *shape: jax.ShapeDtypeStruct(shape, bf16)
    saved = []
    v_first = None
    for l in range(DEPTH):
        tag = f"l{l}_"
        lw = derive_stage(0, big_of(l, 0, x))
        vres = l > 0
        p_norm1 = [_row(sm["attn_norm"][l])]
        (h,) = rows_fwd(_fn_norm, [x], p_norm1, [], [sds(s_len, D)], tile=t_row, name=tag + "norm1")
        gate = mm(h, lw["gate"], name=tag + "proj_gate")
        mla = mm(h, lw["mla"], name=tag + "proj_mla")
        rwc = mm(h, lw["rw"], name=tag + "proj_rwkv")
        cvc = mm(h, lw["cv"], name=tag + "proj_conv")
        lw.update(derive_stage(1, big_of(l, 1, cvc)))
        p_mla = [_row(sm["mla_q_a_norm"][l]), _row(sm["mla_kv_a_norm"][l])]
        qn, kvn, kpe = rows_fwd(_fn_mla_prep, [mla], p_mla, [], [sdb(s_len, QL), sdb(s_len, KVL), sds(s_len, 128)],
                                tile=t_row, name=tag + "mla_prep")
        q_flat = mm(qn, lw["wq"], name=tag + "q_proj")
        kv_flat = mm(kvn, lw["wkv"], name=tag + "kv_proj")
        p_qk = [_row(sm["mla_q_norm"][l]), _row(sm["mla_k_norm"][l])]
        q, k, vv = rows_fwd(_fn_qk_post, [q_flat, kv_flat, kpe, cos, sin], p_qk, [rot],
                            [sds(MLA_H, s_len, DQK), sds(MLA_H, s_len, DQK), sds(MLA_H, s_len, DV)], tile=t_wide,
                            name=tag + "qk_post")
        o = attn_fwd(q, k, vv, tq=_pick(s_len, ATTN_Q_BLOCK, 8), name=tag + "attn")
        o_a = mm(o, lw["wo"], name=tag + "o_a")
        p_rw = [_row(sm["rwkv_mu"][l]), _row(sm["rwkv_w0"][l]), w["rwkv_w2"][l], _row(sm["rwkv_a0"][l]),
                w["rwkv_a2"][l], w["rwkv_g2"][l], _row(sm["rwkv_k_k"][l]), _row(sm["rwkv_k_a"][l])]
        rw_rows, rw_halos = [rwc], (0,)
        if vres:
            p_rw += [w["rwkv_v1"][l - 1], _row(sm["rwkv_v_mu"][l - 1]), _row(sm["rwkv_v0"][l - 1]), w["rwkv_v2"][l - 1]]
            rw_rows, rw_halos = [rwc, h, v_first], (0, 1)
        fn_prep = _make_fn_rwkv_prep(vres)
        r, ld, k2, v, an, bn, g = rows_fwd(fn_prep, rw_rows, p_rw, [bd], [sds(s_len, RW)] * 7, tile=t_row,
                                           name=tag + "rwkv_prep", halos=rw_halos)
        if not vres:
            v_first = v
        y, states = wkv_fwd(r, ld, k2, v, an, bn, name=tag + "wkv")
        p_post = [_row(sm["rwkv_ln_w"][l]), _row(sm["rwkv_ln_b"][l]), _row(sm["rwkv_r_k"][l])]
        (yb,) = rows_fwd(_fn_rwkv_post, [y, r, k2, v, g], p_post, [bd], [sdb(s_len, RW)], tile=t_row,
                         name=tag + "rwkv_post")
        o_b = mm(yb, lw["rwo"], name=tag + "o_b")
        p_cv = [w["conv_w"][l][q:q + 1] for q in range(3)]
        (yc,) = rows_fwd(_fn_conv, [cvc], p_cv, [], [sdb(s_len, CW)], tile=t_row, name=tag + "conv", halos=(0,))
        o_c = mm(yc, lw["cvo"], name=tag + "o_c")
        (merged,) = rows_fwd(_fn_merge, [gate, o_a, o_b, o_c], [], [], [sdb(s_len, D)], tile=t_wide,
                             name=tag + "merge")
        x1 = mm(merged, lw["out"], add=x, name=tag + "out_proj")
        lw.update(derive_stage(2, big_of(l, 2, x1)))
        p_norm2 = [_row(sm["mlp_norm"][l])]
        (h2,) = rows_fwd(_fn_norm, [x1], p_norm2, [], [sdb(s_len, D)], tile=t_row, name=tag + "norm2")
        up, act = mm(h2, lw["up"], relu2_out=True, name=tag + "up")
        x2 = mm(act, lw["down"], add=x1, name=tag + "down")
        saved.append(dict(lw=lw, x=x, h=h, gate=gate, mla=mla, rwc=rwc, cvc=cvc, qn=qn, kvn=kvn, kpe=kpe,
                          q_flat=q_flat, kv_flat=kv_flat, vv=vv, q=q, k=k, o=o, o_a=o_a, r=r, ld=ld, k2=k2, v=v,
                          an=an, bn=bn, g=g, y=y, states=states, yb=yb, o_b=o_b, yc=yc, o_c=o_c, merged=merged,
                          x1=x1, h2=h2, up=up, act=act, p_norm1=p_norm1, p_mla=p_mla, p_qk=p_qk, p_rw=p_rw,
                          p_post=p_post, p_cv=p_cv, p_norm2=p_norm2, rw_rows=rw_rows, rw_halos=rw_halos,
                          fn_prep=fn_prep, v_first=v_first if vres else None))
        x = x2

    loss, dx = loss_head(x, target, tile=t_row, name="loss_head")

    grads = {n: [None] * (DEPTH - 1 if n in ("rwkv_v1", "rwkv_v_mu", "rwkv_v0", "rwkv_v2") else DEPTH)
             for n in WEIGHTS}
    dv_first = None
    for l in reversed(range(DEPTH)):
        tag = f"b{l}_"
        sv = saved[l]
        lw = sv["lw"]
        vres = l > 0
        g_down = mm(sv["act"], dx, ta=True, out_dtype=bf16, name=tag + "g_down")
        dup = mm(dx, lw["down"], tb=True, act_grad=sv["up"], out_dtype=bf16, name=tag + "d_up")
        g_up = mm(sv["h2"], dup, ta=True, n_split=4, out_dtype=bf16, name=tag + "g_up")
        dh2 = mm(dup, lw["up"], tb=True, name=tag + "d_h2")
        slabs = chip_major_grads(2, dict(up=g_up, down=g_down))
        token = on_grads(l, 2, slabs)
        p_norm2 = sv["p_norm2"] if token is None else [sv["p_norm2"][0] + token[0, 0]]
        (dx1,), (g_n2,) = rows_bwd(_fn_norm, [sv["x1"]], p_norm2, [], [[dh2]], tile=t_row,
                                   name=tag + "norm2", extra={0: [dx]})
        g_out = mm(sv["merged"], dx1, ta=True, out_dtype=bf16, name=tag + "g_out")
        dmerged = mm(dx1, lw["out"], tb=True, name=tag + "d_merged")
        (dgate, do_a, do_b, do_c), _ = rows_bwd(_fn_merge, [sv["gate"], sv["o_a"], sv["o_b"], sv["o_c"]], [], [],
                                                [[dmerged]], tile=t_wide, name=tag + "merge",
                                                grad_dtypes=[bf16] * 4)
        g_cvo = mm(sv["yc"], do_c, ta=True, n_split=4, out_dtype=bf16, name=tag + "g_cvo")
        dyc = mm(do_c, lw["cvo"], tb=True, name=tag + "d_yc")
        (dcvc,), g_cw = rows_bwd(_fn_conv, [sv["cvc"]], sv["p_cv"], [], [[dyc]], tile=t_row, name=tag + "conv",
                                    halos=(0,))
        g_rwo = mm(sv["yb"], do_b, ta=True, n_split=4, out_dtype=bf16, name=tag + "g_rwo")
        dyb = mm(do_b, lw["rwo"], tb=True, name=tag + "d_yb")
        (dy, dr_p, dk_p, dv_p, dg), g_post = rows_bwd(
            _fn_rwkv_post, [sv["y"], sv["r"], sv["k2"], sv["v"], sv["g"]], sv["p_post"], [bd], [[dyb]], tile=t_row,
            name=tag + "rwkv_post")
        dr_s, dld, dk_s, dv_s, dan, dbn = wkv_bwd(sv["r"], sv["ld"], sv["k2"], sv["v"], sv["an"], sv["bn"],
                                                  sv["states"], dy, name=tag + "wkv")
        dv_list = [dv_s, dv_p] + ([dv_first] if (not vres and dv_first is not None) else [])
        d_prep, g_prep = rows_bwd(
            sv["fn_prep"], sv["rw_rows"], sv["p_rw"], [bd],
            [[dr_s, dr_p], [dld], [dk_s, dk_p], dv_list, [dan], [dbn], [dg]], tile=t_row, name=tag + "rwkv_prep",
            halos=sv["rw_halos"])
        drwc = d_prep[0]
        dh_extra = []
        if vres:
            dh_extra = [d_prep[1]]
            dv_first = d_prep[2]
        g_wo = mm(sv["o"], do_a, ta=True, n_split=4, out_dtype=bf16, name=tag + "g_wo")
        do = mm(do_a, lw["wo"], tb=True, name=tag + "d_o")
        dq, dk, dvv = attn_bwd(sv["q"], sv["k"], sv["vv"], do, tq=_pick(s_len, ATTN_Q_BLOCK, 8), name=tag + "attn")
        (dq_flat, dkv_flat, dkpe), g_qk = rows_bwd(
            _fn_qk_post, [sv["q_flat"], sv["kv_flat"], sv["kpe"], cos, sin], sv["p_qk"], [rot], [[dq], [dk], [dvv]],
            tile=t_wide, name=tag + "qk_post", grad_rows=[0, 1, 2], grad_dtypes=[bf16, bf16, f32])
        g_wq = mm(sv["qn"], dq_flat, ta=True, out_dtype=bf16, name=tag + "g_wq")
        g_wkv = mm(sv["kvn"], dkv_flat, ta=True, n_split=4, out_dtype=bf16, name=tag + "g_wkv")
        dqn = mm(dq_flat, lw["wq"], tb=True, name=tag + "d_qn")
        dkvn = mm(dkv_flat, lw["wkv"], tb=True, name=tag + "d_kvn")
        slabs.update(chip_major_grads(1, dict(wq=g_wq, wkv=g_wkv, wo=g_wo, rwo=g_rwo, cvo=g_cvo, out=g_out)))
        token = on_grads(l, 1, {n: slabs[n] for n in STAGES[1]})
        p_mla = sv["p_mla"] if token is None else [sv["p_mla"][0] + token[0, 0], sv["p_mla"][1]]
        (dmla,), g_mla = rows_bwd(_fn_mla_prep, [sv["mla"]], p_mla, [], [[dqn], [dkvn], [dkpe]], tile=t_row,
                                  name=tag + "mla_prep", grad_dtypes=[bf16])
        g_gate = mm(sv["h"], dgate, ta=True, out_dtype=bf16, name=tag + "g_gate")
        g_mlaw = mm(sv["h"], dmla, ta=True, out_dtype=bf16, name=tag + "g_mla")
        g_rw = mm(sv["h"], drwc, ta=True, out_dtype=bf16, name=tag + "g_rw")
        g_cv = mm(sv["h"], dcvc, ta=True, out_dtype=bf16, name=tag + "g_cv")
        dh = mm(dgate, lw["gate"], tb=True, name=tag + "d_h_gate")
        dh = mm(dmla, lw["mla"], tb=True, add=dh, name=tag + "d_h_mla")
        dh = mm(drwc, lw["rw"], tb=True, add=dh, name=tag + "d_h_rw")
        dh = mm(dcvc, lw["cv"], tb=True, add=dh, name=tag + "d_h_cv")
        (dx,), (g_n1,) = rows_bwd(_fn_norm, [sv["x"]], sv["p_norm1"], [], [[dh] + dh_extra], tile=t_row,
                                  name=tag + "norm1", extra={0: [dx1]})
        slabs.update(chip_major_grads(0, dict(gate=g_gate, mla=g_mlaw, rw=g_rw, cv=g_cv)))
        token = on_grads(l, 0, {n: slabs[n] for n in STAGES[0]})
        if token is not None and l > 0:
            dx = dx + token[0, 0]
        for n, val in slabs.items():
            grads[n][l] = val
        layer_small = [("attn_norm", l, g_n1), ("mlp_norm", l, g_n2), ("mla_q_a_norm", l, g_mla[0]),
                       ("mla_kv_a_norm", l, g_mla[1]), ("mla_q_norm", l, g_qk[0]), ("mla_k_norm", l, g_qk[1]),
                       ("rwkv_ln_w", l, g_post[0]), ("rwkv_ln_b", l, g_post[1]), ("rwkv_r_k", l, g_post[2]),
                       ("conv_w", l, jnp.concatenate(g_cw, axis=0))]
        layer_small += list(zip(["rwkv_mu", "rwkv_w0", "rwkv_w2", "rwkv_a0", "rwkv_a2", "rwkv_g2", "rwkv_k_k",
                                 "rwkv_k_a"], [l] * 8, g_prep[:8]))
        if vres:
            layer_small += list(zip(["rwkv_v1", "rwkv_v_mu", "rwkv_v0", "rwkv_v2"], [l - 1] * 4, g_prep[8:12]))
        for n, index, val in layer_small:
            grads[n][index] = val
        if l == 0:
            layer_small.append(("loss", 0, loss.reshape(1, 1)))
        token = on_small(l, layer_small)
        if token is not None and l > 0:
            dx = dx + token[0, 0]
    return loss, dx, grads


def _split3(a):
    hi = a.astype(bf16)
    r1 = a - hi.astype(f32)
    mid = r1.astype(bf16)
    lo = (r1 - mid.astype(f32)).astype(bf16)
    return hi, mid, lo


def _shard_axis(name):
    return 1 if name in ROW_SHARDED else 2


def _pack(pieces, width, dtype, row_align):
    flat = jnp.concatenate([p.reshape(-1).astype(dtype) for p in pieces])
    rows = -(-flat.shape[0] // width)
    rows = -(-rows // row_align) * row_align
    return jnp.pad(flat, (0, rows * width - flat.shape[0])).reshape(rows, width)


def _unpack(flat2d, shapes):
    flat = flat2d.reshape(-1)
    out, off = [], 0
    for shp in shapes:
        n = int(np.prod(shp))
        out.append(flat[off:off + n].reshape(shp))
        off += n
    return out


def kernel(x, positions, attn_norm, w_in, mla_q_a_norm, mla_wq_b, mla_kv_a_norm, mla_wkv_b, mla_q_norm, mla_k_norm, mla_w_o, rwkv_mu, rwkv_w0, rwkv_w2, rwkv_a0, rwkv_a2, rwkv_g2, rwkv_k_k, rwkv_k_a, rwkv_r_k, rwkv_ln_w, rwkv_ln_b, rwkv_w_o, rwkv_v1, rwkv_v_mu, rwkv_v0, rwkv_v2, conv_w, conv_w_o, w_out, mlp_norm, w_up, w_down, loss_target, m_attn_norm, m_w_in, m_mla_q_a_norm, m_mla_wq_b, m_mla_kv_a_norm, m_mla_wkv_b, m_mla_q_norm, m_mla_k_norm, m_mla_w_o, m_rwkv_mu, m_rwkv_w0, m_rwkv_w2, m_rwkv_a0, m_rwkv_a2, m_rwkv_g2, m_rwkv_k_k, m_rwkv_k_a, m_rwkv_r_k, m_rwkv_ln_w, m_rwkv_ln_b, m_rwkv_w_o, m_rwkv_v1, m_rwkv_v_mu, m_rwkv_v0, m_rwkv_v2, m_conv_w, m_conv_w_o, m_w_out, m_mlp_norm, m_w_up, m_w_down, v_attn_norm, v_w_in, v_mla_q_a_norm, v_mla_wq_b, v_mla_kv_a_norm, v_mla_wkv_b, v_mla_q_norm, v_mla_k_norm, v_mla_w_o, v_rwkv_mu, v_rwkv_w0, v_rwkv_w2, v_rwkv_a0, v_rwkv_a2, v_rwkv_g2, v_rwkv_k_k, v_rwkv_k_a, v_rwkv_r_k, v_rwkv_ln_w, v_rwkv_ln_b, v_rwkv_w_o, v_rwkv_v1, v_rwkv_v_mu, v_rwkv_v0, v_rwkv_v2, v_conv_w, v_conv_w_o, v_w_out, v_mlp_norm, v_w_up, v_w_down):
    args = dict(locals())
    wts = {n: args[n] for n in WEIGHTS}
    mom = {n: args["m_" + n] for n in WEIGHTS}
    var = {n: args["v_" + n] for n in WEIGHTS}
    chip = 2 * lax.axis_index("x") + lax.axis_index("y")
    core = lax.axis_index("c").astype(jnp.int32).reshape(1)

    med_names = [n for n in MED if n != "conv_w"]
    med_pieces = [wts[n] for n in med_names] + list(_split3(wts["conv_w"]))
    med_shapes = [p.shape for p in med_pieces]
    chip_idx = chip.astype(jnp.int32).reshape(1)
    shards_first = [wts[n][0].astype(bf16) for n in STAGES[0]] + [_pack(med_pieces, 128, bf16, 32)]
    got_first, token = gather_weights(shards_first, name="gather_l0_s0")
    got_first = [place_slab(g, s, chip_idx, name=f"place_own_l0_s0_{q}")
                 for q, (g, s) in enumerate(zip(got_first, shards_first))]
    in_flight = {}
    for key, names, l in (("l0_s1", STAGES[1], 0), ("l0_s2", STAGES[2], 0), ("l1", BIG, 1)):
        group = [wts[n][l].astype(bf16) for n in names]
        group[0] = group[0] + token[0, 0].astype(bf16)
        in_flight[key] = (names, gather_start(group, name="gather_start_" + key))
        token = in_flight[key][1][4]

    def whole_of(names, slabs):
        out = {}
        for n, by_chip in zip(names, slabs):
            _, rows, cols = by_chip.shape
            if n in ROW_SHARDED:
                out[n] = by_chip.reshape(4 * rows, cols)
            else:
                out[n] = by_chip.transpose(1, 0, 2).reshape(rows, 4 * cols)
        return out

    landed = {}

    def big_of(l, stage, after):
        if (l, stage) == (0, 0):
            return whole_of(STAGES[0], got_first)
        key = "l1" if l == 1 else f"l0_s{stage}"
        if key not in landed:
            names, (send_sems, recv_sems, thru, lands, _) = in_flight[key]
            thru, lands = gather_wait(send_sems, recv_sems, thru, lands, after, name="gather_wait_" + key)
            lands = gather_forward(lands, name="gather_forward_" + key)
            landed[key] = whole_of(names, [place_slab(g, s, chip_idx, name=f"place_own_{key}_{q}")
                                           for q, (g, s) in enumerate(zip(lands, thru))])
        return {n: landed[key][n] for n in STAGES[stage]}

    whole = {}
    per_chip = [_unpack(got_first[len(STAGES[0])][j], med_shapes) for j in range(4)]
    for q, n in enumerate(med_names):
        whole[n] = jnp.concatenate([per_chip[j][q] for j in range(4)], axis=_shard_axis(n)).astype(f32)
    base = len(med_names)
    cw_parts = [jnp.concatenate([per_chip[j][base + t] for j in range(4)], axis=2).astype(f32) for t in range(3)]
    whole["conv_w"] = (cw_parts[0] + cw_parts[1]) + cw_parts[2]
    small = {n: wts[n] for n in SMALL}
    small["rwkv_r_k"] = wts["rwkv_r_k"].reshape(DEPTH, RW)

    exchanges, to_sibling = [], []

    def chip_sums_on_their_way(after):
        tokens = []
        while to_sibling:
            l, names, tag, (send_sems, recv_sems, thru, lands, _) = to_sibling.pop(0)
            parts, theirs = sibling_wait(send_sems, recv_sems, thru, lands, after, name="sibling_wait_" + tag)
            chip_sums = [pair_sum(s, t, core, name=f"pair_sum_{n}_{l}") for n, s, t in zip(names, parts, theirs)]
            started = scatter_start(chip_sums, name="scatter_start_" + tag)
            exchanges.append((l, names, tag, started))
            tokens.append(started[4])
        return tokens

    def on_grads(l, stage, slabs):
        names = STAGES[stage]
        tag = f"l{l}_s{stage}"
        parts = [slabs[n] for n in names]
        tokens = chip_sums_on_their_way(parts[0])
        started = sibling_start(parts, name="sibling_start_" + tag)
        to_sibling.append((l, names, tag, started))
        token = started[4]
        for extra in tokens:
            token = token + extra
        return token

    broadcasts = []

    def on_small(l, layer_small):
        values = [val for _, _, val in layer_small]
        started = broadcast_start(_pack(values, 128, f32, 8), name=f"small_start_l{l}")
        broadcasts.append((l, [(n, index, val.shape) for n, index, val in layer_small], started))
        return started[4]

    small["attn_norm"] = small["attn_norm"] + token[0, 0]
    _, grad_x, grads = local_step(x[0], positions[0], loss_target[0], whole, small, big_of, on_grads, on_small)
    chip_sums_on_their_way(grad_x)

    device = (4 * lax.axis_index("x") + 2 * lax.axis_index("y") + lax.axis_index("c")).astype(jnp.int32).reshape(1)
    summed = {n: [None] * len(grads[n]) for n in SMALL + MED}
    summed["loss"] = [None]
    for l, entries, (send_sems, recv_sems, thru, land, _) in broadcasts:
        own, land = broadcast_wait(send_sems, recv_sems, thru, land, grad_x, name=f"small_wait_l{l}")
        total = sum8(land, own, device, name=f"small_sum_l{l}")
        for (n, index, _), val in zip(entries, _unpack(total, [shape for _, _, shape in entries])):
            summed[n][index] = val
    gsum = {}
    for n in SMALL + MED:
        g = jnp.stack(summed[n])
        if n in MED:
            ax = _shard_axis(n)
            width = wts[n].shape[ax]
            g = lax.dynamic_slice_in_dim(g, chip * width, width, axis=ax)
        gsum[n] = g.reshape(wts[n].shape)
    where = jnp.stack([lax.axis_index("c"), chip]).astype(jnp.int32)
    bufs, layout = {}, []
    for l, names, tag, (send_sems, recv_sems, thru, lands, _) in exchanges:
        own, arrived = scatter_wait(send_sems, recv_sems, thru, lands, grad_x, name="scatter_wait_" + tag)
        for n, mine, theirs in zip(names, own, arrived):
            rows = 2 * theirs.shape[1]
            bufs[n] = sum4_into(theirs, mine, bufs.get(n), where, layer=l, total_rows=DEPTH * rows,
                                name=f"sum_chips_{n}_{l}")
            layout.append((BIG.index(n), l * rows, rows))
    reduced = join_halves([bufs[n] for n in BIG], layout, name="join_halves")

    out_g, out_d, out_m, out_v = {}, {}, {}, {}
    for q, n in enumerate(BIG):
        shp = wts[n].shape
        as2d = lambda a: a.reshape(-1, shp[-1])
        g2d = w_in_window_cols(reduced[q], chip) if n == "w_in" else reduced[q]
        res = adamw(as2d(wts[n]), g2d, as2d(mom[n]), as2d(var[n]), name="adamw_" + n)
        out_g[n], out_d[n], out_m[n], out_v[n] = [r.reshape(shp) for r in res]
    sm_all = SMALL + MED
    flat2d = lambda a: a.reshape(-1, a.shape[-1])
    res = adamw_many([flat2d(wts[n]) for n in sm_all], [flat2d(gsum[n]) for n in sm_all],
                     [flat2d(mom[n]) for n in sm_all], [flat2d(var[n]) for n in sm_all], name="adamw_small")
    for tgt, vals in zip((out_d, out_m, out_v), res):
        for n, val in zip(sm_all, vals):
            tgt[n] = val.reshape(wts[n].shape)
    out_g.update({n: gsum[n] for n in sm_all})
    loss = summed["loss"][0].reshape(())
    return (loss, grad_x[None], *[out_g[n] for n in WEIGHTS], *[out_d[n] for n in WEIGHTS],
            *[out_m[n] for n in WEIGHTS], *[out_v[n] for n in WEIGHTS])
```

```python
import functools

import jax
import jax.numpy as jnp
import numpy as np
from jax import lax
from jax.experimental import pallas as pl
from jax.experimental.pallas import tpu as pltpu

f32, bf16 = jnp.float32, jnp.bfloat16
HI = lax.Precision.HIGHEST
MESH = pl.DeviceIdType.MESH

D = 1024
DEPTH = 2
MLA_H, NOPE, ROPE, DQK, DV = 8, 64, 32, 96, 64
QL, KVL = 384, 256
RW, RH, RN = 256, 4, 64
DL, AL, GL, MVL = 64, 64, 128, 32
CW = 256
DFF = 4096
GATE = 3 * D
MLA_COLS = QL + KVL + ROPE
MLA_PAD = 768
Q_HEAD_PAD = 128
NORM_EPS = 1e-6
GN_EPS = 64e-5
ROPE_THETA = 10000.0
LR, B1, B2, EPS, WD, STEP = 0.001, 0.9, 0.999, 1e-08, 0.01, 10

VMEM_LIMIT = 52 * 1024 * 1024
WKV_CHUNK = 64
WKV_CHUNKS_PER_STEP = 4
ATTN_SEGMENTS = 4
ATTN_Q_BLOCK = 512

BIG = ["w_in", "mla_wq_b", "mla_wkv_b", "mla_w_o", "rwkv_w_o", "conv_w_o", "w_out", "w_up", "w_down"]
MED = ["rwkv_w2", "rwkv_a2", "rwkv_g2", "rwkv_v1", "rwkv_v2", "conv_w"]
ROW_SHARDED = {"w_out", "w_down", "rwkv_v1"}
SMALL = ["attn_norm", "mla_q_a_norm", "mla_kv_a_norm", "mla_q_norm", "mla_k_norm", "rwkv_mu", "rwkv_w0",
         "rwkv_a0", "rwkv_k_k", "rwkv_k_a", "rwkv_r_k", "rwkv_ln_w", "rwkv_ln_b", "rwkv_v_mu", "rwkv_v0",
         "mlp_norm"]
WEIGHTS = ["attn_norm", "w_in", "mla_q_a_norm", "mla_wq_b", "mla_kv_a_norm", "mla_wkv_b", "mla_q_norm",
           "mla_k_norm", "mla_w_o", "rwkv_mu", "rwkv_w0", "rwkv_w2", "rwkv_a0", "rwkv_a2", "rwkv_g2",
           "rwkv_k_k", "rwkv_k_a", "rwkv_r_k", "rwkv_ln_w", "rwkv_ln_b", "rwkv_w_o", "rwkv_v1", "rwkv_v_mu",
           "rwkv_v0", "rwkv_v2", "conv_w", "conv_w_o", "w_out", "mlp_norm", "w_up", "w_down"]


def _cparams(sem=None):
    return pltpu.CompilerParams(dimension_semantics=sem, vmem_limit_bytes=VMEM_LIMIT)


def _pick(dim, pref, align):
    if dim <= pref:
        return dim
    t = (pref // align) * align
    while t >= align:
        if dim % t == 0:
            return t
        t -= align
    return dim


def _bdot(a, b, dims):
    return lax.dot_general(a.astype(bf16), b.astype(bf16), (dims, ((), ())), preferred_element_type=f32)


@jax.custom_vjp
def _mm(a, b):
    return _bdot(a, b, ((1,), (0,)))


def _mm_fwd(a, b):
    return _mm(a, b), (a, b)


def _mm_bwd(res, g):
    a, b = res
    return _bdot(g, b, ((1,), (1,))), _bdot(a, g, ((0,), (0,)))


_mm.defvjp(_mm_fwd, _mm_bwd)


@jax.custom_vjp
def _mm_nt(a, b):
    return _bdot(a, b, ((1,), (1,)))


def _mm_nt_fwd(a, b):
    return _mm_nt(a, b), (a, b)


def _mm_nt_bwd(res, g):
    a, b = res
    return _bdot(g, b, ((1,), (0,))), _bdot(g, a, ((0,), (0,)))


_mm_nt.defvjp(_mm_nt_fwd, _mm_nt_bwd)


_NN, _NT, _TN = ((1,), (0,)), ((1,), (1,)), ((0,), (0,))


def _dg(a, b, dims):
    return lax.dot_general(a, b, (dims, ((), ())), preferred_element_type=f32)


def _bf16_pieces(x, count):
    out, rest = [], x
    for q in range(count):
        piece = rest.astype(bf16)
        out.append(piece)
        if q + 1 < count:
            rest = rest - piece.astype(f32)
    return out


def _dot3(a, b, dims):
    (ah, al), (bh, bl) = _bf16_pieces(a, 2), _bf16_pieces(b, 2)
    return _dg(ah, bh, dims) + (_dg(ah, bl, dims) + _dg(al, bh, dims))


@jax.custom_vjp
def _hdot(a, b):
    return _dot3(a, b, _NN)


@jax.custom_vjp
def _hdot_nt(a, b):
    return _dot3(a, b, _NT)


@jax.custom_vjp
def _hdot_tn(a, b):
    return _dot3(a, b, _TN)


_hdot.defvjp(lambda a, b: (_hdot(a, b), (a, b)), lambda res, g: (_hdot_nt(g, res[1]), _hdot_tn(res[0], g)))
_hdot_nt.defvjp(lambda a, b: (_hdot_nt(a, b), (a, b)), lambda res, g: (_hdot(g, res[1]), _hdot_tn(g, res[0])))
_hdot_tn.defvjp(lambda a, b: (_hdot_tn(a, b), (a, b)), lambda res, g: (_hdot_nt(res[1], g), _hdot(res[0], g)))


_BNN, _BNT, _BTN = ((2,), (1,)), ((2,), (2,)), ((1,), (1,))


def _bdg(a, b, dims):
    return lax.dot_general(a, b, (dims, ((0,), (0,))), preferred_element_type=f32)


def _bdot3(a, b, dims):
    (ah, al), (bh, bl) = _bf16_pieces(a, 2), _bf16_pieces(b, 2)
    return _bdg(ah, bh, dims) + (_bdg(ah, bl, dims) + _bdg(al, bh, dims))


@jax.custom_vjp
def _hbnn(a, b):
    return _bdot3(a, b, _BNN)


@jax.custom_vjp
def _hbnt(a, b):
    return _bdot3(a, b, _BNT)


@jax.custom_vjp
def _hbtn(a, b):
    return _bdot3(a, b, _BTN)


_hbnn.defvjp(lambda a, b: (_hbnn(a, b), (a, b)), lambda res, g: (_hbnt(g, res[1]), _hbtn(res[0], g)))
_hbnt.defvjp(lambda a, b: (_hbnt(a, b), (a, b)), lambda res, g: (_hbnn(g, res[1]), _hbtn(g, res[0])))
_hbtn.defvjp(lambda a, b: (_hbtn(a, b), (a, b)), lambda res, g: (_hbnt(res[1], g), _hbnn(res[0], g)))


@functools.partial(jax.custom_vjp, nondiff_argnums=(2,))
def _exact_bl(m, x, transposed):
    mb = m.astype(bf16)
    hi, mid, lo = _bf16_pieces(x, 3)
    dims = _BTN if transposed else _BNN
    return (_bdg(mb, hi, dims) + _bdg(mb, mid, dims)) + _bdg(mb, lo, dims)


_exact_bl.defvjp(lambda m, x, transposed: (_exact_bl(m, x, transposed), m),
                 lambda transposed, m, g: (jnp.zeros_like(m), _exact_bl(m, g, not transposed)))


@functools.partial(jax.custom_vjp, nondiff_argnums=(2,))
def _exact_l(m, x, transposed):
    mb = m.astype(bf16)
    hi, mid, lo = _bf16_pieces(x, 3)
    dims = _TN if transposed else _NN
    return (_dg(mb, hi, dims) + _dg(mb, mid, dims)) + _dg(mb, lo, dims)


_exact_l.defvjp(lambda m, x, transposed: (_exact_l(m, x, transposed), m),
                lambda transposed, m, g: (jnp.zeros_like(m), _exact_l(m, g, not transposed)))


@functools.partial(jax.custom_vjp, nondiff_argnums=(2,))
def _exact_r(x, m, transposed):
    mb = m.astype(bf16)
    hi, mid, lo = _bf16_pieces(x, 3)
    dims = _NT if transposed else _NN
    return (_dg(hi, mb, dims) + _dg(mid, mb, dims)) + _dg(lo, mb, dims)


_exact_r.defvjp(lambda x, m, transposed: (_exact_r(x, m, transposed), m),
                lambda transposed, m, g: (_exact_r(g, m, not transposed), jnp.zeros_like(m)))


def _rms(x, g, eps=NORM_EPS):
    return x * lax.rsqrt(jnp.mean(x * x, axis=-1, keepdims=True) + eps) * g


def _sigmoid(x):
    return 1.0 / (1.0 + jnp.exp(-x))


def _softplus(x):
    return jnp.maximum(x, 0.0) + jnp.log(1.0 + jnp.exp(-jnp.maximum(x, -x)))


def _lane_split(x, sizes):
    bounds = np.cumsum([0] + list(sizes))

    @jax.custom_vjp
    def split(v):
        return tuple(v[..., int(bounds[q]):int(bounds[q + 1])] for q in range(len(sizes)))

    split.defvjp(lambda v: (split(v), None), lambda _, g: (jnp.concatenate(g, axis=-1),))
    return split(x)


def _row_split(x, sizes):
    bounds = np.cumsum([0] + list(sizes))

    @jax.custom_vjp
    def split(v):
        return tuple(v[..., int(bounds[q]):int(bounds[q + 1]), :] for q in range(len(sizes)))

    split.defvjp(lambda v: (split(v), None), lambda _, g: (jnp.concatenate(g, axis=-2),))
    return split(x)


def _shift_mats(t, k):
    r = lax.broadcasted_iota(jnp.int32, (t, t), 0)
    c = lax.broadcasted_iota(jnp.int32, (t, t), 1)
    inner = (r - c == k).astype(f32)
    r8 = lax.broadcasted_iota(jnp.int32, (t, 8), 0)
    c8 = lax.broadcasted_iota(jnp.int32, (t, 8), 1)
    edge = (c8 - r8 == 8 - k).astype(f32)
    return inner, edge


def _shift(x, halo, k):
    inner, edge = _shift_mats(x.shape[0], k)
    return _exact_l(inner, x, False) + jnp.dot(edge, halo, precision=HI, preferred_element_type=f32)


def mm(a, b, *, name, ta=False, tb=False, a_batched=False, b_batched=False, reduce_batch=False, add=None,
       act_grad=None, relu2_out=False, n_split=1, out_dtype=f32, tm=1024, tn=1024, tk=2048):
    ash, bsh = a.shape[-2:], b.shape[-2:]
    (k_, m_) = ash if ta else ash[::-1]
    (k2_, n_) = bsh[::-1] if tb else bsh
    assert k_ == k2_, (a.shape, b.shape, ta, tb)
    hb = a.shape[0] if a_batched else (b.shape[0] if b_batched else 1)
    batched_out = (a_batched or b_batched) and not reduce_batch
    h_out = hb if batched_out else 1
    h_red = hb if reduce_batch else 1
    tm = _pick(m_, tm, 128)
    tn = _pick(n_ // n_split, tn, 128)
    tk = _pick(k_, tk, 128)
    nm, nn, nk = m_ // tm, n_ // tn, k_ // tk

    def a_map(i, j, ho, hr, kk):
        blk = (kk, i) if ta else (i, kk)
        return ((ho if batched_out else hr),) + blk if a_batched else blk

    def b_map(i, j, ho, hr, kk):
        blk = (j, kk) if tb else (kk, j)
        return ((ho if batched_out else hr),) + blk if b_batched else blk

    a_blk = (tk, tm) if ta else (tm, tk)
    b_blk = (tn, tk) if tb else (tk, tn)
    in_specs = [pl.BlockSpec(((1,) + a_blk) if a_batched else a_blk, a_map),
                pl.BlockSpec(((1,) + b_blk) if b_batched else b_blk, b_map)]
    args = [a, b]
    for extra in (add, act_grad):
        if extra is not None:
            in_specs.append(pl.BlockSpec((tm, tn), lambda i, j, ho, hr, kk: (i, j)))
            args.append(extra)
    if n_split > 1:
        per = n_ // n_split // tn
        if batched_out:
            out_spec = pl.BlockSpec((1, 1, tm, tn), lambda i, j, ho, hr, kk: (j // per, ho, i, j % per))
            out_shape = jax.ShapeDtypeStruct((n_split, hb, m_, n_ // n_split), out_dtype)
        else:
            out_spec = pl.BlockSpec((1, tm, tn), lambda i, j, ho, hr, kk: (j // per, i, j % per))
            out_shape = jax.ShapeDtypeStruct((n_split, m_, n_ // n_split), out_dtype)
    elif batched_out:
        out_spec = pl.BlockSpec((1, tm, tn), lambda i, j, ho, hr, kk: (ho, i, j))
        out_shape = jax.ShapeDtypeStruct((hb, m_, n_), out_dtype)
    else:
        out_spec = pl.BlockSpec((tm, tn), lambda i, j, ho, hr, kk: (i, j))
        out_shape = jax.ShapeDtypeStruct((m_, n_), out_dtype)
    lead = (0,) * (int(batched_out) + int(n_split > 1))
    dims = ((0,) if ta else (1,), (1,) if tb else (0,))
    has_add, has_act = add is not None, act_grad is not None

    def body(*refs):
        a_ref, b_ref = refs[0], refs[1]
        pos = 2
        add_ref = act_ref = None
        if has_add:
            add_ref = refs[pos]
            pos += 1
        if has_act:
            act_ref = refs[pos]
            pos += 1
        o_ref, acc_ref = refs[pos], refs[-1]
        hr, kk = pl.program_id(3), pl.program_id(4)
        first = jnp.logical_and(hr == 0, kk == 0)
        last = jnp.logical_and(hr == h_red - 1, kk == nk - 1)
        av = a_ref[0] if a_batched else a_ref[...]
        bv = b_ref[0] if b_batched else b_ref[...]
        p = _bdot(av, bv, dims)
        single = h_red * nk == 1

        if not single:
            @pl.when(first)
            def _():
                acc_ref[...] = p

            @pl.when(jnp.logical_not(first))
            def _():
                acc_ref[...] += p

        @pl.when(last)
        def _():
            r = p if single else acc_ref[...]
            if has_act:
                r = r * (2.0 * jnp.maximum(act_ref[...], 0.0))
            if has_add:
                r = r + add_ref[...]
            if lead:
                o_ref[lead] = r.astype(out_dtype)
            else:
                o_ref[...] = r.astype(out_dtype)
            if relu2_out:
                refs[pos + 1][...] = jnp.square(jnp.maximum(r, 0.0)).astype(bf16)

    if relu2_out:
        assert not lead
        out_spec = [out_spec, out_spec]
        out_shape = [out_shape, jax.ShapeDtypeStruct(out_shape.shape, bf16)]
    return pl.pallas_call(
        body, name=name, grid=(nm, nn, h_out, h_red, nk), in_specs=in_specs, out_specs=out_spec,
        out_shape=out_shape, scratch_shapes=[pltpu.VMEM((tm, tn), f32)],
        compiler_params=_cparams(("parallel", "parallel", "parallel", "arbitrary", "arbitrary")),
    )(*args)


def _row_spec(arr, tile, idx):
    if arr.ndim == 2:
        return pl.BlockSpec((tile, arr.shape[1]), lambda i: (idx(i), 0))
    return pl.BlockSpec((arr.shape[0], tile, arr.shape[2]), lambda i: (0, idx(i), 0))


def _halo_spec(arr, tile, idx):
    per = tile // 8
    return pl.BlockSpec((8, arr.shape[1]), lambda i: (jnp.maximum(idx(i) * per - 1, 0), 0))


def _full_spec(arr):
    nd = arr.ndim
    return pl.BlockSpec(arr.shape, lambda i: (0,) * nd)


def _load_f32(ref):
    val = ref[...]
    return val.astype(f32) if val.dtype == bf16 else val


def rows_fwd(fn, rows, params, consts, out_shapes, *, tile, name, halos=()):
    s_len = rows[0].shape[-2]
    n = s_len // tile
    nr, nh, npar, nc = len(rows), len(halos), len(params), len(consts)
    ident = lambda i: i
    in_specs = ([_row_spec(r, tile, ident) for r in rows] + [_halo_spec(rows[h], tile, ident) for h in halos]
                + [_full_spec(p) for p in params] + [_full_spec(c) for c in consts])
    out_specs = [_row_spec(o, tile, ident) for o in out_shapes]

    def body(*refs):
        i = pl.program_id(0)
        rv = [_load_f32(r) for r in refs[:nr]]
        keep = (i > 0).astype(f32)
        hv = [r[...] * keep for r in refs[nr:nr + nh]]
        pv = [r[...] for r in refs[nr + nh:nr + nh + npar]]
        cv = [r[...] for r in refs[nr + nh + npar:nr + nh + npar + nc]]
        outs = fn(rv, hv, pv, cv)
        for o_ref, o in zip(refs[nr + nh + npar + nc:], outs):
            o_ref[...] = o.astype(o_ref.dtype)

    return pl.pallas_call(
        body, name=name, grid=(n,), in_specs=in_specs, out_specs=out_specs, out_shape=list(out_shapes),
        compiler_params=_cparams(("arbitrary",)),
    )(*rows, *[rows[h] for h in halos], *params, *consts)


def rows_bwd(fn, rows, params, consts, douts, *, tile, name, halos=(), grad_rows=None, extra=None,
             grad_dtypes=None):
    s_len = rows[0].shape[-2]
    n = s_len // tile
    nr, nh, npar, nc = len(rows), len(halos), len(params), len(consts)
    grad_rows = list(range(nr)) if grad_rows is None else list(grad_rows)
    extra = extra or {}
    assert all(h in grad_rows for h in halos)
    rev = lambda i: n - 1 - i
    dflat = [d for ds in douts for d in ds]
    dcount = [len(ds) for ds in douts]
    eflat = [e for g in grad_rows for e in extra.get(g, [])]
    ecount = [len(extra.get(g, [])) for g in grad_rows]
    in_specs = ([_row_spec(r, tile, rev) for r in rows] + [_halo_spec(rows[h], tile, rev) for h in halos]
                + [_full_spec(p) for p in params] + [_full_spec(c) for c in consts]
                + [_row_spec(d, tile, rev) for d in dflat] + [_row_spec(e, tile, rev) for e in eflat])
    grad_dtypes = [f32] * len(grad_rows) if grad_dtypes is None else list(grad_dtypes)
    assert all(grad_dtypes[q] == f32 for q, g in enumerate(grad_rows) if g in halos)
    out_shapes = ([jax.ShapeDtypeStruct(rows[g].shape, dt) for g, dt in zip(grad_rows, grad_dtypes)]
                  + [jax.ShapeDtypeStruct(p.shape, f32) for p in params])
    out_specs = [_row_spec(rows[g], tile, rev) for g in grad_rows] + [_full_spec(p) for p in params]
    scratch = [pltpu.VMEM((8, rows[h].shape[1]), f32) for h in halos]
    n_in = nr + nh + npar + nc + len(dflat) + len(eflat)
    n_out = len(grad_rows) + npar

    def body(*refs):
        i = pl.program_id(0)
        rv = [_load_f32(r) for r in refs[:nr]]
        keep = (i < n - 1).astype(f32)
        hv = [r[...] * keep for r in refs[nr:nr + nh]]
        pv = [r[...] for r in refs[nr + nh:nr + nh + npar]]
        pos = nr + nh + npar
        cv = [r[...] for r in refs[pos:pos + nc]]
        pos += nc
        dv = []
        for cnt in dcount:
            acc = _load_f32(refs[pos])
            for q in range(1, cnt):
                acc = acc + _load_f32(refs[pos + q])
            dv.append(acc)
            pos += cnt
        ev = []
        for cnt in ecount:
            ev.append([_load_f32(refs[pos + q]) for q in range(cnt)])
            pos += cnt
        out_refs = refs[n_in:n_in + n_out]
        carry_refs = refs[n_in + n_out:]

        def f(gr, gh, gp):
            full = list(rv)
            for g, val in zip(grad_rows, gr):
                full[g] = val
            return tuple(fn(full, gh, gp, cv))

        _, vjp = jax.vjp(f, [rv[g] for g in grad_rows], hv, pv)
        d_rows, d_halos, d_params = vjp(tuple(dv))

        @pl.when(i == 0)
        def _():
            for c_ref in carry_refs:
                c_ref[...] = jnp.zeros_like(c_ref)
            for p_ref in out_refs[len(grad_rows):]:
                p_ref[...] = jnp.zeros_like(p_ref)

        for q, g in enumerate(grad_rows):
            val = d_rows[q]
            for e in ev[q]:
                val = val + e
            out_refs[q][...] = val.astype(out_refs[q].dtype)
            if g in halos:
                hq = list(halos).index(g)
                out_refs[q][tile - 8:tile, :] += carry_refs[hq][...]
                carry_refs[hq][...] = d_halos[hq]
        for p_ref, dp in zip(out_refs[len(grad_rows):], d_params):
            p_ref[...] += dp

    res = pl.pallas_call(
        body, name=name, grid=(n,), in_specs=in_specs, out_specs=out_specs, out_shape=out_shapes,
        scratch_shapes=scratch, compiler_params=_cparams(("arbitrary",)),
    )(*rows, *[rows[h] for h in halos], *params, *consts, *dflat, *eflat)
    return list(res[:len(grad_rows)]), list(res[len(grad_rows):])


def _fn_norm(rows, halos, params, consts):
    return (_rms(rows[0], params[0]),)


def _fn_mla_prep(rows, halos, params, consts):
    cq, ckv, kpe = _lane_split(rows[0], (QL, KVL, MLA_PAD - QL - KVL))
    return _rms(cq, params[0]), _rms(ckv, params[1]), kpe


def _rope(x, cos, sin, rot):
    return x * cos + _exact_r(x, rot, False) * sin


def _fn_qk_post(rows, halos, params, consts):
    q_flat, kv_flat, kpe, cos, sin = rows
    q_norm, k_norm = params
    (rot,) = consts
    q_heads = _lane_split(q_flat, (DQK, Q_HEAD_PAD - DQK) * MLA_H)[::2]
    kv_heads = _lane_split(kv_flat, (NOPE, DV) * MLA_H)
    k_pe, _ = _lane_split(kpe, (ROPE, kpe.shape[1] - ROPE))
    qs = [_rope(_rms(qh, q_norm), cos, sin, rot) for qh in q_heads]
    ks = [_rope(_rms(jnp.concatenate([kv_heads[2 * h], k_pe], axis=-1), k_norm), cos, sin, rot)
          for h in range(MLA_H)]
    vs = [kv_heads[2 * h + 1] for h in range(MLA_H)]
    return jnp.stack(qs, axis=0), jnp.stack(ks, axis=0), jnp.stack(vs, axis=0)


def _seg(x, bd):
    return _exact_r(x, bd, False)


def _make_fn_rwkv_prep(vres):
    def fn(rows, halos, params, consts):
        cols = rows[0]
        bd = consts[0]
        mu, w0, w2, a0, a2, g2, k_k, k_a = params[:8]
        prev = _shift(cols, halos[0], 1)
        c = cols + (prev - cols) * mu
        r, k, v, xw, xa, xg = _lane_split(c, (RW, RW, RW, DL, AL, GL))
        log_w = -_softplus(-(w0 + _mm(jnp.tanh(xw), w2))) - 0.5
        ld = -jnp.exp(log_w)
        a = _sigmoid(a0 + _mm(xa, a2))
        g = _mm(_sigmoid(xg), g2)
        if vres:
            hcur, v_first = rows[1], rows[2]
            v1, v_mu, v0, v2 = params[8:12]
            xv = _mm(hcur, v1)
            xv_prev = _shift(xv, _mm(halos[1], v1), 1)
            xv = xv + (xv_prev - xv) * v_mu
            v = v + (v_first - v) * _sigmoid(v0 + _mm(xv, v2))
        kk = k * k_k
        kk = kk / jnp.maximum(jnp.sqrt(_seg(kk * kk, bd)), 1e-12)
        k2 = k * (1.0 + (a - 1.0) * k_a)
        return r, ld, k2, v, -kk, kk * a, g
    return fn


def _fn_rwkv_post(rows, halos, params, consts):
    y, r, k2, v, g = rows
    ln_w, ln_b, r_k = params
    bd = consts[0]
    mean = _seg(y, bd) * (1.0 / RN)
    d = y - mean
    var = _seg(d * d, bd) * (1.0 / RN)
    yn = d * lax.rsqrt(var + GN_EPS) * ln_w + ln_b
    bonus = _seg(r * k2 * r_k, bd) * v
    return ((yn + bonus) * g,)


def _fn_conv(rows, halos, params, consts):
    cols, halo = rows[0], halos[0]
    w0, w1, w2 = params
    b, c, x = _lane_split(cols, (CW, CW, CW))
    _, ch, xh = _lane_split(halo, (CW, CW, CW))
    u, uh = c * x, ch * xh
    return (b * (w0 * _shift(u, uh, 2) + w1 * _shift(u, uh, 1) + w2 * u),)


def _fn_merge(rows, halos, params, consts):
    gate, o_a, o_b, o_c = rows
    g_a, g_b, g_c = _lane_split(gate, (D, D, D))
    return (_sigmoid(g_a) * o_a + _sigmoid(g_b) * o_b + _sigmoid(g_c) * o_c,)


def _attn_block(q, k, v, q0, diagonal_last):
    tq, kend = q.shape[0], k.shape[0]
    s = _mm_nt(q, k) * (DQK ** -0.5)
    if diagonal_last:
        tri = lax.broadcasted_iota(jnp.int32, (tq, tq), 0) >= lax.broadcasted_iota(jnp.int32, (tq, tq), 1)
        if kend > tq:
            before, diag = _lane_split(s, (kend - tq, tq))
            s = jnp.concatenate([before, jnp.where(tri, diag, -1e30)], axis=-1)
        else:
            s = jnp.where(tri, s, -1e30)
    else:
        row = q0 + lax.broadcasted_iota(jnp.int32, (tq, kend), 0)
        col = lax.broadcasted_iota(jnp.int32, (tq, kend), 1)
        s = jnp.where(row >= col, s, -1e30)
    m = lax.stop_gradient(jnp.max(s, axis=-1, keepdims=True))
    e = jnp.exp(s - m)
    p = e * (1.0 / jnp.sum(e, axis=-1, keepdims=True))
    return _mm(p, v)


def _attn_segments(s_len, tq):
    per = max(1, s_len // tq // ATTN_SEGMENTS)
    return [(first, per, (first + per) * tq) for first in range(0, s_len // tq, per)]


HEAD_PAIR = 2


def attn_fwd(q, k, v, *, tq, name):
    h, s_len, _ = q.shape
    outs = []
    for seg, (first, nq, kend) in enumerate(_attn_segments(s_len, tq)):
        def body(q_ref, k_ref, v_ref, o_ref, first=first, nq=nq):
            q0 = (first + pl.program_id(1)) * tq
            o = [_attn_block(q_ref[j], k_ref[j], v_ref[j], q0, nq == 1) for j in range(HEAD_PAIR)]
            o_ref[...] = jnp.concatenate(o, axis=-1).astype(o_ref.dtype)

        outs.append(pl.pallas_call(
            body, name=f"{name}_{seg}", grid=(h // HEAD_PAIR, nq),
            in_specs=[pl.BlockSpec((HEAD_PAIR, tq, DQK), lambda hp, i, first=first: (hp, first + i, 0)),
                      pl.BlockSpec((HEAD_PAIR, kend, DQK), lambda hp, i: (hp, 0, 0)),
                      pl.BlockSpec((HEAD_PAIR, kend, DV), lambda hp, i: (hp, 0, 0))],
            out_specs=pl.BlockSpec((tq, HEAD_PAIR * DV), lambda hp, i: (i, hp)),
            out_shape=jax.ShapeDtypeStruct((nq * tq, h * DV), bf16),
            compiler_params=_cparams(("parallel", "arbitrary")),
        )(q, k, v))
    return jnp.concatenate(outs, axis=0)


def attn_bwd(q, k, v, do, *, tq, name):
    h, s_len, _ = q.shape
    dqs, dk_acc, dv_acc = [], None, None
    for seg, (first, nq, kend) in reversed(list(enumerate(_attn_segments(s_len, tq)))):
        carried = dk_acc is not None

        def body(*refs, first=first, carried=carried, nq=nq):
            q_ref, k_ref, v_ref, do_ref = refs[:4]
            dq_ref, dk_ref, dv_ref = refs[-3:]
            i = pl.program_id(1)
            do_heads = _lane_split(do_ref[...], (DV,) * HEAD_PAIR)
            for j in range(HEAD_PAIR):
                _, vjp = jax.vjp(functools.partial(_attn_block, q0=(first + i) * tq, diagonal_last=nq == 1),
                                 q_ref[j], k_ref[j], v_ref[j])
                dq, dk, dv = vjp(do_heads[j])
                dq_ref[j] = dq

                @pl.when(i == 0)
                def _():
                    dk_ref[j] = dk + refs[4][j] if carried else dk
                    dv_ref[j] = dv + refs[5][j] if carried else dv

                @pl.when(i > 0)
                def _():
                    dk_ref[j] += dk
                    dv_ref[j] += dv

        key_specs = [pl.BlockSpec((HEAD_PAIR, kend, DQK), lambda hp, i: (hp, 0, 0)),
                     pl.BlockSpec((HEAD_PAIR, kend, DV), lambda hp, i: (hp, 0, 0))]
        dq, dk_acc, dv_acc = pl.pallas_call(
            body, name=f"{name}_{seg}", grid=(h // HEAD_PAIR, nq),
            in_specs=[pl.BlockSpec((HEAD_PAIR, tq, DQK), lambda hp, i, first=first: (hp, first + i, 0))] + key_specs
            + [pl.BlockSpec((tq, HEAD_PAIR * DV), lambda hp, i, first=first: (first + i, hp))]
            + (key_specs if carried else []),
            out_specs=[pl.BlockSpec((HEAD_PAIR, tq, DQK), lambda hp, i: (hp, i, 0))] + key_specs,
            out_shape=[jax.ShapeDtypeStruct((h, nq * tq, DQK), f32), jax.ShapeDtypeStruct((h, s_len, DQK), f32),
                       jax.ShapeDtypeStruct((h, s_len, DV), f32)],
            input_output_aliases={4: 1, 5: 2} if carried else {},
            compiler_params=_cparams(("parallel", "arbitrary")),
        )(q, k, v, do, *([dk_acc, dv_acc] if carried else []))
        dqs.append(dq)
    return jnp.concatenate(dqs[::-1], axis=1), dk_acc, dv_acc


def _wkv_local(r, ld, k, v, a, b):
    nb, c, n = r.shape
    ri = lax.broadcasted_iota(jnp.int32, (c, c), 0)
    ci = lax.broadcasted_iota(jnp.int32, (c, c), 1)
    tri = jnp.broadcast_to((ri >= ci).astype(f32)[None], (nb, c, c))
    cum = _exact_bl(tri, ld, False)
    tot = jnp.sum(ld, axis=1, keepdims=True)
    w_incl, w_excl, w_inv, w_rest = jnp.exp(cum), jnp.exp(cum - ld), jnp.exp(-cum), jnp.exp(tot - cum)
    ab, rb, bb, kb = a * w_excl, r * w_incl, b * w_inv, k * w_inv
    bw, kw = b * w_rest, k * w_rest
    r2 = lax.broadcasted_iota(jnp.int32, (2 * c, 2 * c), 0)
    c2 = lax.broadcasted_iota(jnp.int32, (2 * c, 2 * c), 1)
    t_of, s_of = jnp.where(r2 >= c, r2 - c, r2), jnp.where(c2 >= c, c2 - c, c2)
    keep = jnp.logical_or(t_of > s_of, jnp.logical_and(r2 >= c, t_of == s_of))
    pair = jnp.where(keep[None], _hbnt(jnp.concatenate([ab, rb], axis=1), jnp.concatenate([bb, kb], axis=1)), 0.0)
    on_b, on_k = _lane_split(pair, (c, c))
    l_ab, m_rb = _row_split(on_b, (c, c))
    l_ak_v, m_rk_v = _row_split(_hbnn(on_k, v), (c, c))
    x = jnp.concatenate([ab, l_ak_v], axis=-1)
    lp, span = l_ab, 1
    while span < c:
        x = x + _hbnn(lp, x)
        span *= 2
        if span < c:
            lp = _hbnn(lp, lp)
    via_b_r, via_b_y = _lane_split(_hbnn(m_rb, x), (n, n))
    r_hat = rb + via_b_r
    y0 = via_b_y + m_rk_v
    from_b_g, from_b_z = _row_split(_hbtn(x, bw), (n, n))
    eye = lax.broadcasted_iota(jnp.int32, (n, n), 0) == lax.broadcasted_iota(jnp.int32, (n, n), 1)
    g = jnp.where(eye[None], jnp.exp(tot), 0.0) + from_b_g
    z = from_b_z + _hbtn(v, kw)
    return r_hat, y0, g, z


def _head(h):
    return slice(RN * h, RN * (h + 1))


def _load_chunk_heads(ref, c, per):
    return jnp.stack([ref[c * q:c * (q + 1), _head(h)] for q in range(per) for h in range(RH)], axis=0)


def _store_chunk_heads(ref, val, c, per):
    for q in range(per):
        ref[c * q:c * (q + 1), :] = jnp.concatenate([val[q * RH + h] for h in range(RH)], axis=-1)


def wkv_fwd(r, ld, k, v, a, b, *, name):
    s_len = r.shape[0]
    c = WKV_CHUNK
    n = s_len // c
    per = min(WKV_CHUNKS_PER_STEP, n)
    rows = pl.BlockSpec((c * per, RW), lambda i: (i, 0))
    mats = pl.BlockSpec((per, RH, RN, RN), lambda i: (i, 0, 0, 0))
    rows_t, mats_t = jax.ShapeDtypeStruct((s_len, RW), f32), jax.ShapeDtypeStruct((n, RH, RN, RN), f32)

    def local_body(r_ref, ld_ref, k_ref, v_ref, a_ref, b_ref, rh_ref, y0_ref, g_ref, z_ref):
        r_hat, y0, g, z = _wkv_local(*[_load_chunk_heads(ref, c, per)
                                       for ref in (r_ref, ld_ref, k_ref, v_ref, a_ref, b_ref)])
        _store_chunk_heads(rh_ref, r_hat, c, per)
        _store_chunk_heads(y0_ref, y0, c, per)
        g_ref[...] = g.reshape(per, RH, RN, RN)
        z_ref[...] = z.reshape(per, RH, RN, RN)

    r_hat, y0, g, z = pl.pallas_call(
        local_body, name=name + "_local", grid=(n // per,), in_specs=[rows] * 6, out_specs=[rows, rows, mats, mats],
        out_shape=[rows_t, rows_t, mats_t, mats_t], compiler_params=_cparams(("parallel",)),
    )(r, ld, k, v, a, b)

    def scan_body(g_ref, z_ref, st_ref, s_sc):
        s_sc[...] = jnp.zeros_like(s_sc)

        @pl.loop(0, n)
        def _(i):
            s0 = s_sc[...]
            st_ref[i] = s0
            s_sc[...] = _hbnn(s0, g_ref[i]) + z_ref[i]

    vm = pl.BlockSpec(memory_space=pltpu.VMEM)
    states = pl.pallas_call(
        scan_body, name=name + "_scan", in_specs=[vm, vm], out_specs=vm, out_shape=mats_t,
        scratch_shapes=[pltpu.VMEM((RH, RN, RN), f32)],
        compiler_params=pltpu.CompilerParams(vmem_limit_bytes=VMEM_LIMIT),
    )(g, z)

    def out_body(rh_ref, y0_ref, st_ref, y_ref):
        y = _hbnt(_load_chunk_heads(rh_ref, c, per), st_ref[...].reshape(per * RH, RN, RN))
        _store_chunk_heads(y_ref, y, c, per)
        y_ref[...] += y0_ref[...]

    y = pl.pallas_call(
        out_body, name=name + "_out", grid=(n // per,), in_specs=[rows, rows, mats], out_specs=rows,
        out_shape=rows_t, compiler_params=_cparams(("parallel",)),
    )(r_hat, y0, states)
    return y, dict(r_hat=r_hat, g=g, states=states)


def wkv_bwd(r, ld, k, v, a, b, saved, dy, *, name):
    s_len = r.shape[0]
    c = WKV_CHUNK
    n = s_len // c
    per = min(WKV_CHUNKS_PER_STEP, n)
    rows = pl.BlockSpec((c * per, RW), lambda i: (i, 0))
    mats = pl.BlockSpec((per, RH, RN, RN), lambda i: (i, 0, 0, 0))
    rows_t, mats_t = jax.ShapeDtypeStruct((s_len, RW), f32), jax.ShapeDtypeStruct((n, RH, RN, RN), f32)

    def out_body(dy_ref, rh_ref, st_ref, drh_ref, dsy_ref):
        dyb = _load_chunk_heads(dy_ref, c, per)
        _store_chunk_heads(drh_ref, _hbnn(dyb, st_ref[...].reshape(per * RH, RN, RN)), c, per)
        dsy_ref[...] = _hbtn(dyb, _load_chunk_heads(rh_ref, c, per)).reshape(per, RH, RN, RN)

    d_rhat, ds_y = pl.pallas_call(
        out_body, name=name + "_out", grid=(n // per,), in_specs=[rows, rows, mats], out_specs=[rows, mats],
        out_shape=[rows_t, mats_t], compiler_params=_cparams(("parallel",)),
    )(dy, saved["r_hat"], saved["states"])

    def scan_body(dsy_ref, g_ref, st_ref, dg_ref, dz_ref, ds_sc):
        ds_sc[...] = jnp.zeros_like(ds_sc)

        @pl.loop(0, n)
        def _(i):
            cidx = n - 1 - i
            ds_next = ds_sc[...]
            dz_ref[cidx] = ds_next
            dg_ref[cidx] = _hbtn(st_ref[cidx], ds_next)
            ds_sc[...] = dsy_ref[cidx] + _hbnt(ds_next, g_ref[cidx])

    vm = pl.BlockSpec(memory_space=pltpu.VMEM)
    d_g, d_z = pl.pallas_call(
        scan_body, name=name + "_scan", in_specs=[vm, vm, vm], out_specs=[vm, vm], out_shape=[mats_t, mats_t],
        scratch_shapes=[pltpu.VMEM((RH, RN, RN), f32)],
        compiler_params=pltpu.CompilerParams(vmem_limit_bytes=VMEM_LIMIT),
    )(ds_y, saved["g"], saved["states"])

    def local_body(r_ref, ld_ref, k_ref, v_ref, a_ref, b_ref, drh_ref, dy_ref, dg_ref, dz_ref, *out_refs):
        _, vjp = jax.vjp(_wkv_local, *[_load_chunk_heads(ref, c, per)
                                       for ref in (r_ref, ld_ref, k_ref, v_ref, a_ref, b_ref)])
        grads = vjp((_load_chunk_heads(drh_ref, c, per), _load_chunk_heads(dy_ref, c, per),
                     dg_ref[...].reshape(per * RH, RN, RN), dz_ref[...].reshape(per * RH, RN, RN)))
        for o_ref, val in zip(out_refs, grads):
            _store_chunk_heads(o_ref, val, c, per)

    return pl.pallas_call(
        local_body, name=name + "_local", grid=(n // per,), in_specs=[rows] * 8 + [mats, mats], out_specs=[rows] * 6,
        out_shape=[rows_t] * 6, compiler_params=_cparams(("parallel",)),
    )(r, ld, k, v, a, b, d_rhat, dy, d_g, d_z)


def loss_head(y, target, *, tile, name):
    s_len, d = y.shape
    n = s_len // tile

    def body(y_ref, t_ref, dy_ref, l_ref):
        err = y_ref[...] - t_ref[...]
        dy_ref[...] = err * (1.0 / d)
        part = 0.5 * jnp.sum(jnp.mean(err * err, axis=-1, keepdims=True), axis=0, keepdims=True)

        @pl.when(pl.program_id(0) == 0)
        def _():
            l_ref[...] = jnp.zeros_like(l_ref)

        l_ref[...] += jnp.broadcast_to(part, l_ref.shape)

    bs = pl.BlockSpec((tile, d), lambda i: (i, 0))
    dy, l = pl.pallas_call(
        body, name=name, grid=(n,), in_specs=[bs, bs],
        out_specs=[bs, pl.BlockSpec((8, 128), lambda i: (0, 0))],
        out_shape=[jax.ShapeDtypeStruct((s_len, d), f32), jax.ShapeDtypeStruct((8, 128), f32)],
        compiler_params=_cparams(("arbitrary",)),
    )(y, target)
    return l[0, 0], dy


def _adamw_update(w, g, m, v):
    mn = B1 * m + (1.0 - B1) * g
    vn = B2 * v + (1.0 - B2) * (g * g)
    delta = -LR * ((mn / (1.0 - B1 ** STEP)) / (jnp.sqrt(vn / (1.0 - B2 ** STEP)) + EPS) + WD * w)
    return delta, mn, vn


def adamw(w, g, m, v, *, name):
    rows, cols = w.shape
    tile = _pick(rows, max(8, (2 * 1024 * 1024 // (4 * cols)) // 8 * 8), 8)

    def body(w_ref, g_ref, m_ref, v_ref, g_out, d_out, m_out, v_out):
        gv = g_ref[...]
        d_out[...], m_out[...], v_out[...] = _adamw_update(w_ref[...], gv, m_ref[...], v_ref[...])
        g_out[...] = gv

    bs = pl.BlockSpec((tile, cols), lambda i: (i, 0))
    return pl.pallas_call(
        body, name=name, grid=(rows // tile,), in_specs=[bs] * 4, out_specs=[bs] * 4,
        out_shape=[jax.ShapeDtypeStruct((rows, cols), f32)] * 4, compiler_params=_cparams(("parallel",)),
    )(w, g, m, v)


def adamw_many(ws, gs, ms, vs, *, name):
    n = len(ws)

    def body(*refs):
        ins, outs = refs[:4 * n], refs[4 * n:]
        for i in range(n):
            outs[i][...], outs[n + i][...], outs[2 * n + i][...] = _adamw_update(
                ins[i][...], ins[n + i][...], ins[2 * n + i][...], ins[3 * n + i][...])

    vm = pl.BlockSpec(memory_space=pltpu.VMEM)
    res = pl.pallas_call(
        body, name=name, in_specs=[vm] * (4 * n), out_specs=[vm] * (3 * n),
        out_shape=[jax.ShapeDtypeStruct(w.shape, f32) for w in ws] * 3,
        compiler_params=pltpu.CompilerParams(vmem_limit_bytes=VMEM_LIMIT),
    )(*ws, *gs, *ms, *vs)
    return list(res[:n]), list(res[n:2 * n]), list(res[2 * n:])


def _place():
    return lax.axis_index("x"), lax.axis_index("y"), lax.axis_index("c")


_ANY = pl.BlockSpec(memory_space=pl.ANY)


def _peer_chips(x, y):
    return [(1 - x, y), (x, 1 - y), (1 - x, 1 - y)]


def gather_weights(shards, *, name):
    nk = len(shards)

    def body(*refs):
        srcs, outs = refs[:nk], refs[nk:2 * nk]
        ici_send, ici_recv, d2d_send, d2d_recv = refs[2 * nk + 1:]
        x, y, c = _place()
        me = 2 * x + y
        peers = _peer_chips(x, y)
        pending = []
        for k in range(nk):
            half = srcs[k].shape[0] // 2
            mine = pl.ds(c * half, half)
            for p, (px, py) in enumerate(peers):
                cp = pltpu.make_async_remote_copy(
                    src_ref=srcs[k].at[mine], dst_ref=outs[k].at[me, mine], send_sem=ici_send.at[k, p],
                    recv_sem=ici_recv.at[k, p], device_id=(px, py, c), device_id_type=MESH)
                cp.start()
                pending.append(cp)
        for k in range(nk):
            half = srcs[k].shape[0] // 2
            mine = pl.ds(c * half, half)
            for p, (px, py) in enumerate(peers):
                landed = outs[k].at[2 * px + py, mine]
                pltpu.make_async_remote_copy(
                    src_ref=srcs[k].at[mine], dst_ref=landed, send_sem=ici_send.at[k, p], recv_sem=ici_recv.at[k, p],
                    device_id=(px, py, c), device_id_type=MESH).wait_recv()
                fwd = pltpu.make_async_remote_copy(
                    src_ref=landed, dst_ref=landed, send_sem=d2d_send.at[k, p], recv_sem=d2d_recv.at[k, p],
                    device_id=(x, y, 1 - c), device_id_type=MESH)
                fwd.start()
                pending.append(fwd)
        for k in range(nk):
            half = srcs[k].shape[0] // 2
            other = pl.ds((1 - c) * half, half)
            for p, (px, py) in enumerate(peers):
                theirs = outs[k].at[2 * px + py, other]
                pltpu.make_async_remote_copy(
                    src_ref=theirs, dst_ref=theirs, send_sem=d2d_send.at[k, p], recv_sem=d2d_recv.at[k, p],
                    device_id=(x, y, 1 - c), device_id_type=MESH).wait_recv()
        for cp in pending:
            cp.wait_send()
        refs[2 * nk][...] = jnp.zeros_like(refs[2 * nk])

    sem = lambda *shape: pltpu.SemaphoreType.DMA(shape)
    res = pl.pallas_call(
        body, name=name, in_specs=[_ANY] * nk, out_specs=[_ANY] * nk + [pl.BlockSpec(memory_space=pltpu.VMEM)],
        out_shape=[jax.ShapeDtypeStruct((4,) + s.shape, s.dtype) for s in shards]
        + [jax.ShapeDtypeStruct((8, 128), f32)],
        scratch_shapes=[sem(nk, 3), sem(nk, 3), sem(nk, 3), sem(nk, 3)],
    )(*shards)
    return list(res[:nk]), res[nk]


_HBM = pl.BlockSpec(memory_space=pltpu.HBM)
_SEM = pl.BlockSpec(memory_space=pltpu.SEMAPHORE)


def _ici_half_copies(srcs, lands, send_sems, recv_sems, incoming):
    x, y, c = _place()
    me = 2 * x + y
    out = []
    for k in range(len(srcs)):
        half = srcs[k].shape[0] // 2
        mine = pl.ds(c * half, half)
        for p, (px, py) in enumerate(_peer_chips(x, y)):
            out.append(pltpu.make_async_remote_copy(
                src_ref=srcs[k].at[mine], dst_ref=lands[k].at[(2 * px + py) if incoming else me, mine],
                send_sem=send_sems.at[3 * k + p], recv_sem=recv_sems.at[3 * k + p], device_id=(px, py, c),
                device_id_type=MESH))
    return out


def gather_start(shards, *, name):
    nk = len(shards)

    def body(*refs):
        srcs, lands = refs[:nk], refs[nk:2 * nk]
        send_sems, recv_sems = refs[2 * nk], refs[2 * nk + 1]
        token = refs[-1]
        for outgoing in _ici_half_copies(srcs, lands, send_sems, recv_sems, incoming=False):
            outgoing.start()
        token[...] = jnp.zeros_like(token)

    lands = [pltpu.with_memory_space_constraint(lax.empty((4,) + s.shape, s.dtype), pltpu.HBM) for s in shards]
    res = pl.pallas_call(
        body, name=name,
        out_shape=(pltpu.SemaphoreType.DMA((3 * nk,)), pltpu.SemaphoreType.DMA((3 * nk,)),
                   *[pltpu.HBM(s.shape, s.dtype) for s in shards], *[pltpu.HBM(z.shape, z.dtype) for z in lands],
                   jax.ShapeDtypeStruct((8, 128), f32)),
        in_specs=[_HBM] * (2 * nk), out_specs=(_SEM, _SEM, *[_HBM] * (2 * nk), pl.BlockSpec(memory_space=pltpu.VMEM)),
        input_output_aliases={k: 2 + k for k in range(2 * nk)},
        compiler_params=pltpu.CompilerParams(has_side_effects=pltpu.SideEffectType.DATAFLOW_SIDE_EFFECTING),
    )(*[pltpu.with_memory_space_constraint(s, pltpu.HBM) for s in shards], *lands)
    return res[0], res[1], list(res[2:2 + nk]), list(res[2 + nk:2 + 2 * nk]), res[-1]


def gather_wait(send_sems, recv_sems, shards, lands, after, *, name):
    nk = len(shards)

    def body(*refs):
        srcs, zones = refs[:nk], refs[nk:2 * nk]
        for outgoing in _ici_half_copies(srcs, zones, refs[2 * nk], refs[2 * nk + 1], incoming=False):
            outgoing.wait_send()
        for landed in _ici_half_copies(srcs, zones, refs[2 * nk], refs[2 * nk + 1], incoming=True):
            landed.wait_recv()

    res = pl.pallas_call(
        body, name=name,
        out_shape=(*[pltpu.HBM(s.shape, s.dtype) for s in shards], *[pltpu.HBM(z.shape, z.dtype) for z in lands]),
        in_specs=[_HBM] * (2 * nk) + [_SEM, _SEM, _ANY], out_specs=tuple([_HBM] * (2 * nk)),
        input_output_aliases={k: k for k in range(2 * nk)},
        compiler_params=pltpu.CompilerParams(has_side_effects=pltpu.SideEffectType.DATAFLOW_SIDE_EFFECTING),
    )(*shards, *lands, send_sems, recv_sems, after)
    return list(res[:nk]), list(res[nk:])


def gather_forward(lands, *, name):
    nk = len(lands)

    def body(*refs):
        zones = refs[nk:2 * nk]
        send_sems, recv_sems = refs[2 * nk:]
        x, y, c = _place()
        sends = []
        for k in range(nk):
            half = zones[k].shape[1] // 2
            for p, (px, py) in enumerate(_peer_chips(x, y)):
                landed = zones[k].at[2 * px + py, pl.ds(c * half, half)]
                cp = pltpu.make_async_remote_copy(
                    src_ref=landed, dst_ref=landed, send_sem=send_sems.at[k, p], recv_sem=recv_sems.at[k, p],
                    device_id=(x, y, 1 - c), device_id_type=MESH)
                cp.start()
                sends.append(cp)
        for k in range(nk):
            half = zones[k].shape[1] // 2
            for p, (px, py) in enumerate(_peer_chips(x, y)):
                theirs = zones[k].at[2 * px + py, pl.ds((1 - c) * half, half)]
                pltpu.make_async_remote_copy(
                    src_ref=theirs, dst_ref=theirs, send_sem=send_sems.at[k, p], recv_sem=recv_sems.at[k, p],
                    device_id=(x, y, 1 - c), device_id_type=MESH).wait_recv()
        for cp in sends:
            cp.wait_send()

    return pl.pallas_call(
        body, name=name, in_specs=[_ANY] * nk, out_specs=[_ANY] * nk,
        out_shape=[jax.ShapeDtypeStruct(z.shape, z.dtype) for z in lands],
        input_output_aliases={k: k for k in range(nk)},
        scratch_shapes=[pltpu.SemaphoreType.DMA((nk, 3)), pltpu.SemaphoreType.DMA((nk, 3))],
    )(*lands)


def _to_sibling_copies(srcs, lands, send_sems, recv_sems):
    x, y, c = _place()
    out = []
    for k in range(len(srcs)):
        half = srcs[k].shape[1] // 2
        out.append(pltpu.make_async_remote_copy(
            src_ref=srcs[k].at[:, pl.ds((1 - c) * half, half), :], dst_ref=lands[k], send_sem=send_sems.at[k],
            recv_sem=recv_sems.at[k], device_id=(x, y, 1 - c), device_id_type=MESH))
    return out


def sibling_start(parts, *, name):
    nk = len(parts)

    def body(*refs):
        for cp in _to_sibling_copies(refs[:nk], refs[nk:2 * nk], refs[2 * nk], refs[2 * nk + 1]):
            cp.start()
        refs[-1][...] = jnp.zeros_like(refs[-1])

    lands = [pltpu.with_memory_space_constraint(lax.empty((4, p.shape[1] // 2, p.shape[2]), p.dtype), pltpu.HBM)
             for p in parts]
    res = pl.pallas_call(
        body, name=name,
        out_shape=(pltpu.SemaphoreType.DMA((nk,)), pltpu.SemaphoreType.DMA((nk,)),
                   *[pltpu.HBM(p.shape, p.dtype) for p in parts], *[pltpu.HBM(z.shape, z.dtype) for z in lands],
                   jax.ShapeDtypeStruct((8, 128), f32)),
        in_specs=[_HBM] * (2 * nk), out_specs=(_SEM, _SEM, *[_HBM] * (2 * nk), pl.BlockSpec(memory_space=pltpu.VMEM)),
        input_output_aliases={k: 2 + k for k in range(2 * nk)},
        compiler_params=pltpu.CompilerParams(has_side_effects=pltpu.SideEffectType.DATAFLOW_SIDE_EFFECTING),
    )(*[pltpu.with_memory_space_constraint(p, pltpu.HBM) for p in parts], *lands)
    return res[0], res[1], list(res[2:2 + nk]), list(res[2 + nk:2 + 2 * nk]), res[-1]


def sibling_wait(send_sems, recv_sems, parts, lands, after, *, name):
    nk = len(parts)

    def body(*refs):
        copies = _to_sibling_copies(refs[:nk], refs[nk:2 * nk], refs[2 * nk], refs[2 * nk + 1])
        for cp in copies:
            cp.wait_send()
        for cp in copies:
            cp.wait_recv()

    res = pl.pallas_call(
        body, name=name,
        out_shape=(*[pltpu.HBM(p.shape, p.dtype) for p in parts], *[pltpu.HBM(z.shape, z.dtype) for z in lands]),
        in_specs=[_HBM] * (2 * nk) + [_SEM, _SEM, _ANY], out_specs=tuple([_HBM] * (2 * nk)),
        input_output_aliases={k: k for k in range(2 * nk)},
        compiler_params=pltpu.CompilerParams(has_side_effects=pltpu.SideEffectType.DATAFLOW_SIDE_EFFECTING),
    )(*parts, *lands, send_sems, recv_sems, after)
    return list(res[:nk]), list(res[nk:])


def pair_sum(part, theirs, core, *, name):
    _, rows, cols = part.shape
    half = rows // 2
    tile = _pick(half, max(16, (1 << 20) // (4 * cols) // 16 * 16), 16)
    per = half // tile

    def body(c_ref, p_ref, t_ref, o_ref):
        o_ref[...] = (p_ref[...].astype(f32) + t_ref[...].astype(f32)).astype(bf16)

    grid_spec = pltpu.PrefetchScalarGridSpec(
        num_scalar_prefetch=1, grid=(4, per),
        in_specs=[pl.BlockSpec((1, tile, cols), lambda j, i, c_ref: (j, c_ref[0] * per + i, 0)),
                  pl.BlockSpec((1, tile, cols), lambda j, i, c_ref: (j, i, 0))],
        out_specs=pl.BlockSpec((1, tile, cols), lambda j, i, c_ref: (j, i, 0)))
    return pl.pallas_call(
        body, name=name, grid_spec=grid_spec, out_shape=jax.ShapeDtypeStruct((4, half, cols), bf16),
        compiler_params=_cparams(("parallel", "parallel")),
    )(core, part, theirs)


def _all_to_all_copies(srcs, lands, send_sems, recv_sems, incoming):
    x, y, c = _place()
    me = 2 * x + y
    out = []
    for k in range(len(srcs)):
        for p, (px, py) in enumerate(_peer_chips(x, y)):
            peer = 2 * px + py
            out.append(pltpu.make_async_remote_copy(
                src_ref=srcs[k].at[peer], dst_ref=lands[k].at[peer if incoming else me],
                send_sem=send_sems.at[3 * k + p], recv_sem=recv_sems.at[3 * k + p], device_id=(px, py, c),
                device_id_type=MESH))
    return out


def scatter_start(parts, *, name):
    nk = len(parts)

    def body(*refs):
        srcs, lands = refs[:nk], refs[nk:2 * nk]
        for outgoing in _all_to_all_copies(srcs, lands, refs[2 * nk], refs[2 * nk + 1], incoming=False):
            outgoing.start()
        refs[-1][...] = jnp.zeros_like(refs[-1])

    lands = [pltpu.with_memory_space_constraint(lax.empty(p.shape, p.dtype), pltpu.HBM) for p in parts]
    res = pl.pallas_call(
        body, name=name,
        out_shape=(pltpu.SemaphoreType.DMA((3 * nk,)), pltpu.SemaphoreType.DMA((3 * nk,)),
                   *[pltpu.HBM(p.shape, p.dtype) for p in parts], *[pltpu.HBM(p.shape, p.dtype) for p in parts],
                   jax.ShapeDtypeStruct((8, 128), f32)),
        in_specs=[_HBM] * (2 * nk), out_specs=(_SEM, _SEM, *[_HBM] * (2 * nk), pl.BlockSpec(memory_space=pltpu.VMEM)),
        input_output_aliases={k: 2 + k for k in range(2 * nk)},
        compiler_params=pltpu.CompilerParams(has_side_effects=pltpu.SideEffectType.DATAFLOW_SIDE_EFFECTING),
    )(*[pltpu.with_memory_space_constraint(p, pltpu.HBM) for p in parts], *lands)
    return res[0], res[1], list(res[2:2 + nk]), list(res[2 + nk:2 + 2 * nk]), res[-1]


def scatter_wait(send_sems, recv_sems, parts, lands, after, *, name):
    nk = len(parts)

    def body(*refs):
        srcs, zones = refs[:nk], refs[nk:2 * nk]
        for outgoing in _all_to_all_copies(srcs, zones, refs[2 * nk], refs[2 * nk + 1], incoming=False):
            outgoing.wait_send()
        for landed in _all_to_all_copies(srcs, zones, refs[2 * nk], refs[2 * nk + 1], incoming=True):
            landed.wait_recv()

    res = pl.pallas_call(
        body, name=name,
        out_shape=(*[pltpu.HBM(p.shape, p.dtype) for p in parts], *[pltpu.HBM(z.shape, z.dtype) for z in lands]),
        in_specs=[_HBM] * (2 * nk) + [_SEM, _SEM, _ANY], out_specs=tuple([_HBM] * (2 * nk)),
        input_output_aliases={k: k for k in range(2 * nk)},
        compiler_params=pltpu.CompilerParams(has_side_effects=pltpu.SideEffectType.DATAFLOW_SIDE_EFFECTING),
    )(*parts, *lands, send_sems, recv_sems, after)
    return list(res[:nk]), list(res[nk:])


def join_halves(bufs, layout, *, name):
    nk, nb = len(layout), len(bufs)

    def body(*refs):
        outs = refs[nb:2 * nb]
        send_sems, recv_sems = refs[2 * nb:]
        x, y, c = _place()
        pending = []
        for k, (o, off, rows) in enumerate(layout):
            half = rows // 2
            mine = outs[o].at[pl.ds(off + c * half, half), :]
            cp = pltpu.make_async_remote_copy(
                src_ref=mine, dst_ref=mine, send_sem=send_sems.at[k], recv_sem=recv_sems.at[k],
                device_id=(x, y, 1 - c), device_id_type=MESH)
            cp.start()
            pending.append(cp)
        for k, (o, off, rows) in enumerate(layout):
            half = rows // 2
            theirs = outs[o].at[pl.ds(off + (1 - c) * half, half), :]
            pltpu.make_async_remote_copy(
                src_ref=theirs, dst_ref=theirs, send_sem=send_sems.at[k], recv_sem=recv_sems.at[k],
                device_id=(x, y, 1 - c), device_id_type=MESH).wait_recv()
        for cp in pending:
            cp.wait_send()

    return pl.pallas_call(
        body, name=name, in_specs=[_ANY] * nb, out_specs=[_ANY] * nb,
        out_shape=[jax.ShapeDtypeStruct(b.shape, b.dtype) for b in bufs],
        input_output_aliases={o: o for o in range(nb)},
        scratch_shapes=[pltpu.SemaphoreType.DMA((nk,)), pltpu.SemaphoreType.DMA((nk,))],
    )(*bufs)


def place_slab(dest, src, index, *, name):
    rows, cols = src.shape
    tile = _pick(rows, max(16, (1 << 20) // (src.dtype.itemsize * cols) // 16 * 16), 16)

    def body(i_ref, s_ref, d_ref, o_ref):
        del i_ref, d_ref
        o_ref[0] = s_ref[...]

    grid_spec = pltpu.PrefetchScalarGridSpec(
        num_scalar_prefetch=1, grid=(rows // tile,),
        in_specs=[pl.BlockSpec((tile, cols), lambda i, idx: (i, 0)), _ANY],
        out_specs=pl.BlockSpec((1, tile, cols), lambda i, idx: (idx[0], i, 0)))
    return pl.pallas_call(
        body, name=name, grid_spec=grid_spec, out_shape=jax.ShapeDtypeStruct(dest.shape, dest.dtype),
        input_output_aliases={2: 0}, compiler_params=_cparams(("parallel",)),
    )(index, src, dest)


def _broadcast_copies(src, land, send_sems, recv_sems, incoming):
    x, y, c = _place()
    me = 4 * x + 2 * y + c
    out = []
    for m in range(1, 8):
        px, py, pc = x ^ (m >> 2), y ^ ((m >> 1) & 1), c ^ (m & 1)
        out.append(pltpu.make_async_remote_copy(
            src_ref=src, dst_ref=land.at[(4 * px + 2 * py + pc) if incoming else me], send_sem=send_sems.at[m - 1],
            recv_sem=recv_sems.at[m - 1], device_id=(px, py, pc), device_id_type=MESH))
    return out


def broadcast_start(src, *, name):
    def body(s_ref, l_ref, send_sems, recv_sems, s_thru, l_thru, token):
        for outgoing in _broadcast_copies(s_ref, l_ref, send_sems, recv_sems, incoming=False):
            outgoing.start()
        token[...] = jnp.zeros_like(token)

    land = pltpu.with_memory_space_constraint(lax.empty((8,) + src.shape, src.dtype), pltpu.HBM)
    return pl.pallas_call(
        body, name=name,
        out_shape=(pltpu.SemaphoreType.DMA((7,)), pltpu.SemaphoreType.DMA((7,)), pltpu.HBM(src.shape, src.dtype),
                   pltpu.HBM(land.shape, land.dtype), jax.ShapeDtypeStruct((8, 128), f32)),
        in_specs=[_HBM, _HBM], out_specs=(_SEM, _SEM, _HBM, _HBM, pl.BlockSpec(memory_space=pltpu.VMEM)),
        input_output_aliases={0: 2, 1: 3},
        compiler_params=pltpu.CompilerParams(has_side_effects=pltpu.SideEffectType.DATAFLOW_SIDE_EFFECTING),
    )(pltpu.with_memory_space_constraint(src, pltpu.HBM), land)


def broadcast_wait(send_sems, recv_sems, src, land, after, *, name):
    def body(s_ref, l_ref, send_sems, recv_sems, after_ref, s_out, l_out):
        for outgoing in _broadcast_copies(s_ref, l_ref, send_sems, recv_sems, incoming=False):
            outgoing.wait_send()
        for landed in _broadcast_copies(s_ref, l_ref, send_sems, recv_sems, incoming=True):
            landed.wait_recv()

    return pl.pallas_call(
        body, name=name, out_shape=(pltpu.HBM(src.shape, src.dtype), pltpu.HBM(land.shape, land.dtype)),
        in_specs=[_HBM, _HBM, _SEM, _SEM, _ANY], out_specs=(_HBM, _HBM), input_output_aliases={0: 0, 1: 1},
        compiler_params=pltpu.CompilerParams(has_side_effects=pltpu.SideEffectType.DATAFLOW_SIDE_EFFECTING),
    )(src, land, send_sems, recv_sems, after)


def sum8(land, own, device, *, name):
    _, rows, cols = land.shape

    def body(d_ref, l_ref, o_ref, out_ref):
        mine = o_ref[...]
        acc = jnp.where(d_ref[0] == 0, mine, l_ref[0])
        for d in range(1, 8):
            acc = acc + jnp.where(d_ref[0] == d, mine, l_ref[d])
        out_ref[...] = acc

    grid_spec = pltpu.PrefetchScalarGridSpec(
        num_scalar_prefetch=1, grid=(1,),
        in_specs=[pl.BlockSpec((8, rows, cols), lambda i, d_ref: (0, 0, 0)),
                  pl.BlockSpec((rows, cols), lambda i, d_ref: (0, 0))],
        out_specs=pl.BlockSpec((rows, cols), lambda i, d_ref: (0, 0)))
    return pl.pallas_call(
        body, name=name, grid_spec=grid_spec, out_shape=jax.ShapeDtypeStruct((rows, cols), f32),
        compiler_params=_cparams(("arbitrary",)),
    )(device, land, own)


def sum4_into(arrived, own, dest, where, *, layer, total_rows, name):
    _, rows, cols = arrived.shape
    tile = _pick(rows, max(16, (1 << 20) // (4 * cols) // 16 * 16), 16)
    per = rows // tile

    def body(w_ref, a_ref, own_ref, *rest):
        mine = own_ref[0].astype(f32)
        p = [jnp.where(w_ref[1] == j, mine, a_ref[j].astype(f32)) for j in range(4)]
        rest[-1][...] = ((p[0] + p[1]) + p[2]) + p[3]

    grid_spec = pltpu.PrefetchScalarGridSpec(
        num_scalar_prefetch=1, grid=(per,),
        in_specs=[pl.BlockSpec((4, tile, cols), lambda i, w_ref: (0, i, 0)),
                  pl.BlockSpec((1, tile, cols), lambda i, w_ref: (w_ref[1], i, 0))] + ([] if dest is None else [_ANY]),
        out_specs=pl.BlockSpec((tile, cols), lambda i, w_ref: ((2 * layer + w_ref[0]) * per + i, 0)))
    return pl.pallas_call(
        body, name=name, grid_spec=grid_spec, out_shape=jax.ShapeDtypeStruct((total_rows, cols), f32),
        input_output_aliases={} if dest is None else {3: 0}, compiler_params=_cparams(("parallel",)),
    )(where, arrived, own, *([] if dest is None else [dest]))


def _consts():
    idx = np.arange(RW)
    bd = (idx[:, None] // RN == idx[None, :] // RN).astype(np.float32)
    rot = np.zeros((DQK, DQK), np.float32)
    half = ROPE // 2
    rot[NOPE + half + np.arange(half), NOPE + np.arange(half)] = -1.0
    rot[NOPE + np.arange(half), NOPE + half + np.arange(half)] = 1.0
    return jnp.asarray(bd), jnp.asarray(rot)


def _rope_tables(positions):
    freqs = ROPE_THETA ** (-(jnp.arange(ROPE // 2, dtype=f32) * 2.0 / ROPE))
    ang = positions.astype(f32)[:, None] * freqs
    cos, sin = jnp.cos(ang), jnp.sin(ang)
    ones = jnp.ones((positions.shape[0], NOPE), f32)
    return (jnp.concatenate([ones, cos, cos], axis=-1), jnp.concatenate([0.0 * ones, sin, sin], axis=-1))


STAGES = (("w_in",), ("mla_wq_b", "mla_wkv_b", "mla_w_o", "rwkv_w_o", "conv_w_o", "w_out"), ("w_up", "w_down"))


def derive_stage(stage, w):
    if stage == 0:
        w_in = w["w_in"]
        pad = jnp.zeros((D, MLA_PAD - MLA_COLS), w_in.dtype)
        return dict(gate=w_in[:, :GATE], mla=jnp.concatenate([w_in[:, GATE:GATE + MLA_COLS], pad], axis=1),
                    rw=w_in[:, GATE + MLA_COLS:GATE + MLA_COLS + 4 * RW], cv=w_in[:, GATE + MLA_COLS + 4 * RW:])
    if stage == 1:
        wq = jnp.pad(w["mla_wq_b"].reshape(QL, MLA_H, DQK), ((0, 0), (0, 0), (0, Q_HEAD_PAD - DQK)))
        return dict(wq=wq.reshape(QL, MLA_H * Q_HEAD_PAD), wkv=w["mla_wkv_b"], wo=w["mla_w_o"], rwo=w["rwkv_w_o"],
                    cvo=w["conv_w_o"], out=w["w_out"])
    return dict(up=w["w_up"], down=w["w_down"])


W_IN_WINDOW_TILE = (0, 10, 21, 31)
W_IN_WINDOW = 1664
W_IN_SHARD = 1384


def w_in_window_cols(win, chip):
    gap = MLA_PAD - MLA_COLS
    branches = []
    for j in range(4):
        lo, hi = W_IN_SHARD * j, W_IN_SHARD * (j + 1)
        base = 128 * W_IN_WINDOW_TILE[j]
        cut = GATE + MLA_COLS
        if hi <= cut:
            branches.append(lambda w, a=lo - base: w[:, a:a + W_IN_SHARD])
        elif lo >= cut:
            branches.append(lambda w, a=lo + gap - base: w[:, a:a + W_IN_SHARD])
        else:
            branches.append(lambda w, a=lo - base, n1=cut - lo, b=cut + gap - base, n2=hi - cut:
                            jnp.concatenate([w[:, a:a + n1], w[:, b:b + n2]], axis=1))
    return lax.switch(chip, branches, win)


def chip_major_grads(stage, g):
    if stage == 0:
        padded = jnp.concatenate([g["gate"], g["mla"], g["rw"], g["cv"]], axis=1)
        return dict(w_in=jnp.stack([padded[:, 128 * t:128 * t + W_IN_WINDOW] for t in W_IN_WINDOW_TILE]))
    if stage == 1:
        wq = g["wq"].reshape(QL, MLA_H, Q_HEAD_PAD)[:, :, :DQK].reshape(QL, 4, -1).transpose(1, 0, 2)
        return dict(mla_wq_b=wq, mla_wkv_b=g["wkv"], mla_w_o=g["wo"], rwkv_w_o=g["rwo"], conv_w_o=g["cvo"],
                    w_out=g["out"].reshape(4, D // 4, D))
    return dict(w_up=g["up"], w_down=g["down"].reshape(4, DFF // 4, D))


def _row(v):
    return v.reshape(1, -1)


def local_step(x, positions, target, w, sm, big_of=None, on_grads=None, on_small=None):
    if big_of is None:
        big_of = lambda l, stage, _after: {n: w[n][l] for n in STAGES[stage]}
    if on_grads is None:
        on_grads = lambda l, stage, slabs: None
    if on_small is None:
        on_small = lambda l, layer_small: None
    s_len = x.shape[0]
    t_row = _pick(s_len, 256, 8)
    t_wide = _pick(s_len, 128, 8)
    bd, rot = _consts()
    cos, sin = _rope_tables(positions)
    sds = lambda *shape: jax.ShapeDtypeStruct(shape, f32)
    sdb = lambda *shape: jax.ShapeDtypeStruct(shape, bf16)
    saved = []
    v_first = None
    for l in range(DEPTH):
        tag = f"l{l}_"
        lw = derive_stage(0, big_of(l, 0, x))
        vres = l > 0
        p_norm1 = [_row(sm["attn_norm"][l])]
        (h,) = rows_fwd(_fn_norm, [x], p_norm1, [], [sds(s_len, D)], tile=t_row, name=tag + "norm1")
        gate = mm(h, lw["gate"], name=tag + "proj_gate")
        mla = mm(h, lw["mla"], name=tag + "proj_mla")
        rwc = mm(h, lw["rw"], name=tag + "proj_rwkv")
        cvc = mm(h, lw["cv"], name=tag + "proj_conv")
        lw.update(derive_stage(1, big_of(l, 1, cvc)))
        p_mla = [_row(sm["mla_q_a_norm"][l]), _row(sm["mla_kv_a_norm"][l])]
        qn, kvn, kpe = rows_fwd(_fn_mla_prep, [mla], p_mla, [], [sdb(s_len, QL), sdb(s_len, KVL), sds(s_len, 128)],
                                tile=t_row, name=tag + "mla_prep")
        q_flat = mm(qn, lw["wq"], name=tag + "q_proj")
        kv_flat = mm(kvn, lw["wkv"], name=tag + "kv_proj")
        p_qk = [_row(sm["mla_q_norm"][l]), _row(sm["mla_k_norm"][l])]
        q, k, vv = rows_fwd(_fn_qk_post, [q_flat, kv_flat, kpe, cos, sin], p_qk, [rot],
                            [sds(MLA_H, s_len, DQK), sds(MLA_H, s_len, DQK), sds(MLA_H, s_len, DV)], tile=t_wide,
                            name=tag + "qk_post")
        o = attn_fwd(q, k, vv, tq=_pick(s_len, ATTN_Q_BLOCK, 8), name=tag + "attn")
        o_a = mm(o, lw["wo"], name=tag + "o_a")
        p_rw = [_row(sm["rwkv_mu"][l]), _row(sm["rwkv_w0"][l]), w["rwkv_w2"][l], _row(sm["rwkv_a0"][l]),
                w["rwkv_a2"][l], w["rwkv_g2"][l], _row(sm["rwkv_k_k"][l]), _row(sm["rwkv_k_a"][l])]
        rw_rows, rw_halos = [rwc], (0,)
        if vres:
            p_rw += [w["rwkv_v1"][l - 1], _row(sm["rwkv_v_mu"][l - 1]), _row(sm["rwkv_v0"][l - 1]), w["rwkv_v2"][l - 1]]
            rw_rows, rw_halos = [rwc, h, v_first], (0, 1)
        fn_prep = _make_fn_rwkv_prep(vres)
        r, ld, k2, v, an, bn, g = rows_fwd(fn_prep, rw_rows, p_rw, [bd], [sds(s_len, RW)] * 7, tile=t_row,
                                           name=tag + "rwkv_prep", halos=rw_halos)
        if not vres:
            v_first = v
        y, states = wkv_fwd(r, ld, k2, v, an, bn, name=tag + "wkv")
        p_post = [_row(sm["rwkv_ln_w"][l]), _row(sm["rwkv_ln_b"][l]), _row(sm["rwkv_r_k"][l])]
        (yb,) = rows_fwd(_fn_rwkv_post, [y, r, k2, v, g], p_post, [bd], [sdb(s_len, RW)], tile=t_row,
                         name=tag + "rwkv_post")
        o_b = mm(yb, lw["rwo"], name=tag + "o_b")
        p_cv = [w["conv_w"][l][q:q + 1] for q in range(3)]
        (yc,) = rows_fwd(_fn_conv, [cvc], p_cv, [], [sdb(s_len, CW)], tile=t_row, name=tag + "conv", halos=(0,))
        o_c = mm(yc, lw["cvo"], name=tag + "o_c")
        (merged,) = rows_fwd(_fn_merge, [gate, o_a, o_b, o_c], [], [], [sdb(s_len, D)], tile=t_wide,
                             name=tag + "merge")
        x1 = mm(merged, lw["out"], add=x, name=tag + "out_proj")
        lw.update(derive_stage(2, big_of(l, 2, x1)))
        p_norm2 = [_row(sm["mlp_norm"][l])]
        (h2,) = rows_fwd(_fn_norm, [x1], p_norm2, [], [sdb(s_len, D)], tile=t_row, name=tag + "norm2")
        up, act = mm(h2, lw["up"], relu2_out=True, name=tag + "up")
        x2 = mm(act, lw["down"], add=x1, name=tag + "down")
        saved.append(dict(lw=lw, x=x, h=h, gate=gate, mla=mla, rwc=rwc, cvc=cvc, qn=qn, kvn=kvn, kpe=kpe,
                          q_flat=q_flat, kv_flat=kv_flat, vv=vv, q=q, k=k, o=o, o_a=o_a, r=r, ld=ld, k2=k2, v=v,
                          an=an, bn=bn, g=g, y=y, states=states, yb=yb, o_b=o_b, yc=yc, o_c=o_c, merged=merged,
                          x1=x1, h2=h2, up=up, act=act, p_norm1=p_norm1, p_mla=p_mla, p_qk=p_qk, p_rw=p_rw,
                          p_post=p_post, p_cv=p_cv, p_norm2=p_norm2, rw_rows=rw_rows, rw_halos=rw_halos,
                          fn_prep=fn_prep, v_first=v_first if vres else None))
        x = x2

    loss, dx = loss_head(x, target, tile=t_row, name="loss_head")

    grads = {n: [None] * (DEPTH - 1 if n in ("rwkv_v1", "rwkv_v_mu", "rwkv_v0", "rwkv_v2") else DEPTH)
             for n in WEIGHTS}
    dv_first = None
    for l in reversed(range(DEPTH)):
        tag = f"b{l}_"
        sv = saved[l]
        lw = sv["lw"]
        vres = l > 0
        g_down = mm(sv["act"], dx, ta=True, out_dtype=bf16, name=tag + "g_down")
        dup = mm(dx, lw["down"], tb=True, act_grad=sv["up"], out_dtype=bf16, name=tag + "d_up")
        g_up = mm(sv["h2"], dup, ta=True, n_split=4, out_dtype=bf16, name=tag + "g_up")
        dh2 = mm(dup, lw["up"], tb=True, name=tag + "d_h2")
        slabs = chip_major_grads(2, dict(up=g_up, down=g_down))
        token = on_grads(l, 2, slabs)
        p_norm2 = sv["p_norm2"] if token is None else [sv["p_norm2"][0] + token[0, 0]]
        (dx1,), (g_n2,) = rows_bwd(_fn_norm, [sv["x1"]], p_norm2, [], [[dh2]], tile=t_row,
                                   name=tag + "norm2", extra={0: [dx]})
        g_out = mm(sv["merged"], dx1, ta=True, out_dtype=bf16, name=tag + "g_out")
        dmerged = mm(dx1, lw["out"], tb=True, name=tag + "d_merged")
        (dgate, do_a, do_b, do_c), _ = rows_bwd(_fn_merge, [sv["gate"], sv["o_a"], sv["o_b"], sv["o_c"]], [], [],
                                                [[dmerged]], tile=t_wide, name=tag + "merge",
                                                grad_dtypes=[bf16] * 4)
        g_cvo = mm(sv["yc"], do_c, ta=True, n_split=4, out_dtype=bf16, name=tag + "g_cvo")
        dyc = mm(do_c, lw["cvo"], tb=True, name=tag + "d_yc")
        (dcvc,), g_cw = rows_bwd(_fn_conv, [sv["cvc"]], sv["p_cv"], [], [[dyc]], tile=t_row, name=tag + "conv",
                                    halos=(0,))
        g_rwo = mm(sv["yb"], do_b, ta=True, n_split=4, out_dtype=bf16, name=tag + "g_rwo")
        dyb = mm(do_b, lw["rwo"], tb=True, name=tag + "d_yb")
        (dy, dr_p, dk_p, dv_p, dg), g_post = rows_bwd(
            _fn_rwkv_post, [sv["y"], sv["r"], sv["k2"], sv["v"], sv["g"]], sv["p_post"], [bd], [[dyb]], tile=t_row,
            name=tag + "rwkv_post")
        dr_s, dld, dk_s, dv_s, dan, dbn = wkv_bwd(sv["r"], sv["ld"], sv["k2"], sv["v"], sv["an"], sv["bn"],
                                                  sv["states"], dy, name=tag + "wkv")
        dv_list = [dv_s, dv_p] + ([dv_first] if (not vres and dv_first is not None) else [])
        d_prep, g_prep = rows_bwd(
            sv["fn_prep"], sv["rw_rows"], sv["p_rw"], [bd],
            [[dr_s, dr_p], [dld], [dk_s, dk_p], dv_list, [dan], [dbn], [dg]], tile=t_row, name=tag + "rwkv_prep",
            halos=sv["rw_halos"])
        drwc = d_prep[0]
        dh_extra = []
        if vres:
            dh_extra = [d_prep[1]]
            dv_first = d_prep[2]
        g_wo = mm(sv["o"], do_a, ta=True, n_split=4, out_dtype=bf16, name=tag + "g_wo")
        do = mm(do_a, lw["wo"], tb=True, name=tag + "d_o")
        dq, dk, dvv = attn_bwd(sv["q"], sv["k"], sv["vv"], do, tq=_pick(s_len, ATTN_Q_BLOCK, 8), name=tag + "attn")
        (dq_flat, dkv_flat, dkpe), g_qk = rows_bwd(
            _fn_qk_post, [sv["q_flat"], sv["kv_flat"], sv["kpe"], cos, sin], sv["p_qk"], [rot], [[dq], [dk], [dvv]],
            tile=t_wide, name=tag + "qk_post", grad_rows=[0, 1, 2], grad_dtypes=[bf16, bf16, f32])
        g_wq = mm(sv["qn"], dq_flat, ta=True, out_dtype=bf16, name=tag + "g_wq")
        g_wkv = mm(sv["kvn"], dkv_flat, ta=True, n_split=4, out_dtype=bf16, name=tag + "g_wkv")
        dqn = mm(dq_flat, lw["wq"], tb=True, name=tag + "d_qn")
        dkvn = mm(dkv_flat, lw["wkv"], tb=True, name=tag + "d_kvn")
        slabs.update(chip_major_grads(1, dict(wq=g_wq, wkv=g_wkv, wo=g_wo, rwo=g_rwo, cvo=g_cvo, out=g_out)))
        token = on_grads(l, 1, {n: slabs[n] for n in STAGES[1]})
        p_mla = sv["p_mla"] if token is None else [sv["p_mla"][0] + token[0, 0], sv["p_mla"][1]]
        (dmla,), g_mla = rows_bwd(_fn_mla_prep, [sv["mla"]], p_mla, [], [[dqn], [dkvn], [dkpe]], tile=t_row,
                                  name=tag + "mla_prep", grad_dtypes=[bf16])
        g_gate = mm(sv["h"], dgate, ta=True, out_dtype=bf16, name=tag + "g_gate")
        g_mlaw = mm(sv["h"], dmla, ta=True, out_dtype=bf16, name=tag + "g_mla")
        g_rw = mm(sv["h"], drwc, ta=True, out_dtype=bf16, name=tag + "g_rw")
        g_cv = mm(sv["h"], dcvc, ta=True, out_dtype=bf16, name=tag + "g_cv")
        dh = mm(dgate, lw["gate"], tb=True, name=tag + "d_h_gate")
        dh = mm(dmla, lw["mla"], tb=True, add=dh, name=tag + "d_h_mla")
        dh = mm(drwc, lw["rw"], tb=True, add=dh, name=tag + "d_h_rw")
        dh = mm(dcvc, lw["cv"], tb=True, add=dh, name=tag + "d_h_cv")
        (dx,), (g_n1,) = rows_bwd(_fn_norm, [sv["x"]], sv["p_norm1"], [], [[dh] + dh_extra], tile=t_row,
                                  name=tag + "norm1", extra={0: [dx1]})
        slabs.update(chip_major_grads(0, dict(gate=g_gate, mla=g_mlaw, rw=g_rw, cv=g_cv)))
        token = on_grads(l, 0, {n: slabs[n] for n in STAGES[0]})
        if token is not None and l > 0:
            dx = dx + token[0, 0]
        for n, val in slabs.items():
            grads[n][l] = val
        layer_small = [("attn_norm", l, g_n1), ("mlp_norm", l, g_n2), ("mla_q_a_norm", l, g_mla[0]),
                       ("mla_kv_a_norm", l, g_mla[1]), ("mla_q_norm", l, g_qk[0]), ("mla_k_norm", l, g_qk[1]),
                       ("rwkv_ln_w", l, g_post[0]), ("rwkv_ln_b", l, g_post[1]), ("rwkv_r_k", l, g_post[2]),
                       ("conv_w", l, jnp.concatenate(g_cw, axis=0))]
        layer_small += list(zip(["rwkv_mu", "rwkv_w0", "rwkv_w2", "rwkv_a0", "rwkv_a2", "rwkv_g2", "rwkv_k_k",
                                 "rwkv_k_a"], [l] * 8, g_prep[:8]))
        if vres:
            layer_small += list(zip(["rwkv_v1", "rwkv_v_mu", "rwkv_v0", "rwkv_v2"], [l - 1] * 4, g_prep[8:12]))
        for n, index, val in layer_small:
            grads[n][index] = val
        if l == 0:
            layer_small.append(("loss", 0, loss.reshape(1, 1)))
        token = on_small(l, layer_small)
        if token is not None and l > 0:
            dx = dx + token[0, 0]
    return loss, dx, grads


def _split3(a):
    hi = a.astype(bf16)
    r1 = a - hi.astype(f32)
    mid = r1.astype(bf16)
    lo = (r1 - mid.astype(f32)).astype(bf16)
    return hi, mid, lo


def _shard_axis(name):
    return 1 if name in ROW_SHARDED else 2


def _pack(pieces, width, dtype, row_align):
    flat = jnp.concatenate([p.reshape(-1).astype(dtype) for p in pieces])
    rows = -(-flat.shape[0] // width)
    rows = -(-rows // row_align) * row_align
    return jnp.pad(flat, (0, rows * width - flat.shape[0])).reshape(rows, width)


def _unpack(flat2d, shapes):
    flat = flat2d.reshape(-1)
    out, off = [], 0
    for shp in shapes:
        n = int(np.prod(shp))
        out.append(flat[off:off + n].reshape(shp))
        off += n
    return out


def kernel(x, positions, attn_norm, w_in, mla_q_a_norm, mla_wq_b, mla_kv_a_norm, mla_wkv_b, mla_q_norm, mla_k_norm, mla_w_o, rwkv_mu, rwkv_w0, rwkv_w2, rwkv_a0, rwkv_a2, rwkv_g2, rwkv_k_k, rwkv_k_a, rwkv_r_k, rwkv_ln_w, rwkv_ln_b, rwkv_w_o, rwkv_v1, rwkv_v_mu, rwkv_v0, rwkv_v2, conv_w, conv_w_o, w_out, mlp_norm, w_up, w_down, loss_target, m_attn_norm, m_w_in, m_mla_q_a_norm, m_mla_wq_b, m_mla_kv_a_norm, m_mla_wkv_b, m_mla_q_norm, m_mla_k_norm, m_mla_w_o, m_rwkv_mu, m_rwkv_w0, m_rwkv_w2, m_rwkv_a0, m_rwkv_a2, m_rwkv_g2, m_rwkv_k_k, m_rwkv_k_a, m_rwkv_r_k, m_rwkv_ln_w, m_rwkv_ln_b, m_rwkv_w_o, m_rwkv_v1, m_rwkv_v_mu, m_rwkv_v0, m_rwkv_v2, m_conv_w, m_conv_w_o, m_w_out, m_mlp_norm, m_w_up, m_w_down, v_attn_norm, v_w_in, v_mla_q_a_norm, v_mla_wq_b, v_mla_kv_a_norm, v_mla_wkv_b, v_mla_q_norm, v_mla_k_norm, v_mla_w_o, v_rwkv_mu, v_rwkv_w0, v_rwkv_w2, v_rwkv_a0, v_rwkv_a2, v_rwkv_g2, v_rwkv_k_k, v_rwkv_k_a, v_rwkv_r_k, v_rwkv_ln_w, v_rwkv_ln_b, v_rwkv_w_o, v_rwkv_v1, v_rwkv_v_mu, v_rwkv_v0, v_rwkv_v2, v_conv_w, v_conv_w_o, v_w_out, v_mlp_norm, v_w_up, v_w_down):
    args = dict(locals())
    wts = {n: args[n] for n in WEIGHTS}
    mom = {n: args["m_" + n] for n in WEIGHTS}
    var = {n: args["v_" + n] for n in WEIGHTS}
    chip = 2 * lax.axis_index("x") + lax.axis_index("y")
    core = lax.axis_index("c").astype(jnp.int32).reshape(1)

    med_names = [n for n in MED if n != "conv_w"]
    med_pieces = [wts[n] for n in med_names] + list(_split3(wts["conv_w"]))
    med_shapes = [p.shape for p in med_pieces]
    chip_idx = chip.astype(jnp.int32).reshape(1)
    shards_first = [wts[n][0].astype(bf16) for n in STAGES[0]] + [_pack(med_pieces, 128, bf16, 32)]
    got_first, token = gather_weights(shards_first, name="gather_l0_s0")
    got_first = [place_slab(g, s, chip_idx, name=f"place_own_l0_s0_{q}")
                 for q, (g, s) in enumerate(zip(got_first, shards_first))]
    in_flight = {}
    for key, names, l in (("l0_s1", STAGES[1], 0), ("l0_s2", STAGES[2], 0), ("l1", BIG, 1)):
        group = [wts[n][l].astype(bf16) for n in names]
        group[0] = group[0] + token[0, 0].astype(bf16)
        in_flight[key] = (names, gather_start(group, name="gather_start_" + key))
        token = in_flight[key][1][4]

    def whole_of(names, slabs):
        out = {}
        for n, by_chip in zip(names, slabs):
            _, rows, cols = by_chip.shape
            if n in ROW_SHARDED:
                out[n] = by_chip.reshape(4 * rows, cols)
            else:
                out[n] = by_chip.transpose(1, 0, 2).reshape(rows, 4 * cols)
        return out

    landed = {}

    def big_of(l, stage, after):
        if (l, stage) == (0, 0):
            return whole_of(STAGES[0], got_first)
        key = "l1" if l == 1 else f"l0_s{stage}"
        if key not in landed:
            names, (send_sems, recv_sems, thru, lands, _) = in_flight[key]
            thru, lands = gather_wait(send_sems, recv_sems, thru, lands, after, name="gather_wait_" + key)
            lands = gather_forward(lands, name="gather_forward_" + key)
            landed[key] = whole_of(names, [place_slab(g, s, chip_idx, name=f"place_own_{key}_{q}")
                                           for q, (g, s) in enumerate(zip(lands, thru))])
        return {n: landed[key][n] for n in STAGES[stage]}

    whole = {}
    per_chip = [_unpack(got_first[len(STAGES[0])][j], med_shapes) for j in range(4)]
    for q, n in enumerate(med_names):
        whole[n] = jnp.concatenate([per_chip[j][q] for j in range(4)], axis=_shard_axis(n)).astype(f32)
    base = len(med_names)
    cw_parts = [jnp.concatenate([per_chip[j][base + t] for j in range(4)], axis=2).astype(f32) for t in range(3)]
    whole["conv_w"] = (cw_parts[0] + cw_parts[1]) + cw_parts[2]
    small = {n: wts[n] for n in SMALL}
    small["rwkv_r_k"] = wts["rwkv_r_k"].reshape(DEPTH, RW)

    exchanges, to_sibling = [], []

    def chip_sums_on_their_way(after):
        tokens = []
        while to_sibling:
            l, names, tag, (send_sems, recv_sems, thru, lands, _) = to_sibling.pop(0)
            parts, theirs = sibling_wait(send_sems, recv_sems, thru, lands, after, name="sibling_wait_" + tag)
            chip_sums = [pair_sum(s, t, core, name=f"pair_sum_{n}_{l}") for n, s, t in zip(names, parts, theirs)]
            started = scatter_start(chip_sums, name="scatter_start_" + tag)
            exchanges.append((l, names, tag, started))
            tokens.append(started[4])
        return tokens

    def on_grads(l, stage, slabs):
        names = STAGES[stage]
        tag = f"l{l}_s{stage}"
        parts = [slabs[n] for n in names]
        tokens = chip_sums_on_their_way(parts[0])
        started = sibling_start(parts, name="sibling_start_" + tag)
        to_sibling.append((l, names, tag, started))
        token = started[4]
        for extra in tokens:
            token = token + extra
        return token

    broadcasts = []

    def on_small(l, layer_small):
        values = [val for _, _, val in layer_small]
        started = broadcast_start(_pack(values, 128, f32, 8), name=f"small_start_l{l}")
        broadcasts.append((l, [(n, index, val.shape) for n, index, val in layer_small], started))
        return started[4]

    small["attn_norm"] = small["attn_norm"] + token[0, 0]
    _, grad_x, grads = local_step(x[0], positions[0], loss_target[0], whole, small, big_of, on_grads, on_small)
    chip_sums_on_their_way(grad_x)

    device = (4 * lax.axis_index("x") + 2 * lax.axis_index("y") + lax.axis_index("c")).astype(jnp.int32).reshape(1)
    summed = {n: [None] * len(grads[n]) for n in SMALL + MED}
    summed["loss"] = [None]
    for l, entries, (send_sems, recv_sems, thru, land, _) in broadcasts:
        own, land = broadcast_wait(send_sems, recv_sems, thru, land, grad_x, name=f"small_wait_l{l}")
        total = sum8(land, own, device, name=f"small_sum_l{l}")
        for (n, index, _), val in zip(entries, _unpack(total, [shape for _, _, shape in entries])):
            summed[n][index] = val
    gsum = {}
    for n in SMALL + MED:
        g = jnp.stack(summed[n])
        if n in MED:
            ax = _shard_axis(n)
            width = wts[n].shape[ax]
            g = lax.dynamic_slice_in_dim(g, chip * width, width, axis=ax)
        gsum[n] = g.reshape(wts[n].shape)
    where = jnp.stack([lax.axis_index("c"), chip]).astype(jnp.int32)
    bufs, layout = {}, []
    for l, names, tag, (send_sems, recv_sems, thru, lands, _) in exchanges:
        own, arrived = scatter_wait(send_sems, recv_sems, thru, lands, grad_x, name="scatter_wait_" + tag)
        for n, mine, theirs in zip(names, own, arrived):
            rows = 2 * theirs.shape[1]
            bufs[n] = sum4_into(theirs, mine, bufs.get(n), where, layer=l, total_rows=DEPTH * rows,
                                name=f"sum_chips_{n}_{l}")
            layout.append((BIG.index(n), l * rows, rows))
    reduced = join_halves([bufs[n] for n in BIG], layout, name="join_halves")

    out_g, out_d, out_m, out_v = {}, {}, {}, {}
    for q, n in enumerate(BIG):
        shp = wts[n].shape
        as2d = lambda a: a.reshape(-1, shp[-1])
        g2d = w_in_window_cols(reduced[q], chip) if n == "w_in" else reduced[q]
        res = adamw(as2d(wts[n]), g2d, as2d(mom[n]), as2d(var[n]), name="adamw_" + n)
        out_g[n], out_d[n], out_m[n], out_v[n] = [r.reshape(shp) for r in res]
    sm_all = SMALL + MED
    flat2d = lambda a: a.reshape(-1, a.shape[-1])
    res = adamw_many([flat2d(wts[n]) for n in sm_all], [flat2d(gsum[n]) for n in sm_all],
                     [flat2d(mom[n]) for n in sm_all], [flat2d(var[n]) for n in sm_all], name="adamw_small")
    for tgt, vals in zip((out_d, out_m, out_v), res):
        for n, val in zip(sm_all, vals):
            tgt[n] = val.reshape(wts[n].shape)
    out_g.update({n: gsum[n] for n in sm_all})
    loss = summed["loss"][0].reshape(())
    return (loss, grad_x[None], *[out_g[n] for n in WEIGHTS], *[out_d[n] for n in WEIGHTS],
            *[out_m[n] for n in WEIGHTS], *[out_v[n] for n in WEIGHTS])
```

```python
import functools

import jax
import jax.numpy as jnp
import numpy as np
from jax import lax
from jax.experimental import pallas as pl
from jax.experimental.pallas import tpu as pltpu

f32, bf16 = jnp.float32, jnp.bfloat16
HI = lax.Precision.HIGHEST
MESH = pl.DeviceIdType.MESH

D = 1024
DEPTH = 2
MLA_H, NOPE, ROPE, DQK, DV = 8, 64, 32, 96, 64
QL, KVL = 384, 256
RW, RH, RN = 256, 4, 64
DL, AL, GL, MVL = 64, 64, 128, 32
CW = 256
DFF = 4096
GATE = 3 * D
MLA_COLS = QL + KVL + ROPE
MLA_PAD = 768
Q_HEAD_PAD = 128
NORM_EPS = 1e-6
GN_EPS = 64e-5
ROPE_THETA = 10000.0
LR, B1, B2, EPS, WD, STEP = 0.001, 0.9, 0.999, 1e-08, 0.01, 10

VMEM_LIMIT = 52 * 1024 * 1024
WKV_CHUNK = 64
WKV_CHUNKS_PER_STEP = 4
ATTN_SEGMENTS = 4
ATTN_Q_BLOCK = 512

BIG = ["w_in", "mla_wq_b", "mla_wkv_b", "mla_w_o", "rwkv_w_o", "conv_w_o", "w_out", "w_up", "w_down"]
MED = ["rwkv_w2", "rwkv_a2", "rwkv_g2", "rwkv_v1", "rwkv_v2", "conv_w"]
ROW_SHARDED = {"w_out", "w_down", "rwkv_v1"}
SMALL = ["attn_norm", "mla_q_a_norm", "mla_kv_a_norm", "mla_q_norm", "mla_k_norm", "rwkv_mu", "rwkv_w0",
         "rwkv_a0", "rwkv_k_k", "rwkv_k_a", "rwkv_r_k", "rwkv_ln_w", "rwkv_ln_b", "rwkv_v_mu", "rwkv_v0",
         "mlp_norm"]
WEIGHTS = ["attn_norm", "w_in", "mla_q_a_norm", "mla_wq_b", "mla_kv_a_norm", "mla_wkv_b", "mla_q_norm",
           "mla_k_norm", "mla_w_o", "rwkv_mu", "rwkv_w0", "rwkv_w2", "rwkv_a0", "rwkv_a2", "rwkv_g2",
           "rwkv_k_k", "rwkv_k_a", "rwkv_r_k", "rwkv_ln_w", "rwkv_ln_b", "rwkv_w_o", "rwkv_v1", "rwkv_v_mu",
           "rwkv_v0", "rwkv_v2", "conv_w", "conv_w_o", "w_out", "mlp_norm", "w_up", "w_down"]


def _cparams(sem=None):
    return pltpu.CompilerParams(dimension_semantics=sem, vmem_limit_bytes=VMEM_LIMIT)


def _pick(dim, pref, align):
    if dim <= pref:
        return dim
    t = (pref // align) * align
    while t >= align:
        if dim % t == 0:
            return t
        t -= align
    return dim


def _bdot(a, b, dims):
    return lax.dot_general(a.astype(bf16), b.astype(bf16), (dims, ((), ())), preferred_element_type=f32)


@jax.custom_vjp
def _mm(a, b):
    return _bdot(a, b, ((1,), (0,)))


def _mm_fwd(a, b):
    return _mm(a, b), (a, b)


def _mm_bwd(res, g):
    a, b = res
    return _bdot(g, b, ((1,), (1,))), _bdot(a, g, ((0,), (0,)))


_mm.defvjp(_mm_fwd, _mm_bwd)


@jax.custom_vjp
def _mm_nt(a, b):
    return _bdot(a, b, ((1,), (1,)))


def _mm_nt_fwd(a, b):
    return _mm_nt(a, b), (a, b)


def _mm_nt_bwd(res, g):
    a, b = res
    return _bdot(g, b, ((1,), (0,))), _bdot(g, a, ((0,), (0,)))


_mm_nt.defvjp(_mm_nt_fwd, _mm_nt_bwd)


_NN, _NT, _TN = ((1,), (0,)), ((1,), (1,)), ((0,), (0,))


def _dg(a, b, dims):
    return lax.dot_general(a, b, (dims, ((), ())), preferred_element_type=f32)


def _bf16_pieces(x, count):
    out, rest = [], x
    for q in range(count):
        piece = rest.astype(bf16)
        out.append(piece)
        if q + 1 < count:
            rest = rest - piece.astype(f32)
    return out


def _dot3(a, b, dims):
    (ah, al), (bh, bl) = _bf16_pieces(a, 2), _bf16_pieces(b, 2)
    return _dg(ah, bh, dims) + (_dg(ah, bl, dims) + _dg(al, bh, dims))


@jax.custom_vjp
def _hdot(a, b):
    return _dot3(a, b, _NN)


@jax.custom_vjp
def _hdot_nt(a, b):
    return _dot3(a, b, _NT)


@jax.custom_vjp
def _hdot_tn(a, b):
    return _dot3(a, b, _TN)


_hdot.defvjp(lambda a, b: (_hdot(a, b), (a, b)), lambda res, g: (_hdot_nt(g, res[1]), _hdot_tn(res[0], g)))
_hdot_nt.defvjp(lambda a, b: (_hdot_nt(a, b), (a, b)), lambda res, g: (_hdot(g, res[1]), _hdot_tn(g, res[0])))
_hdot_tn.defvjp(lambda a, b: (_hdot_tn(a, b), (a, b)), lambda res, g: (_hdot_nt(res[1], g), _hdot(res[0], g)))


_BNN, _BNT, _BTN = ((2,), (1,)), ((2,), (2,)), ((1,), (1,))


def _bdg(a, b, dims):
    return lax.dot_general(a, b, (dims, ((0,), (0,))), preferred_element_type=f32)


def _bdot3(a, b, dims):
    (ah, al), (bh, bl) = _bf16_pieces(a, 2), _bf16_pieces(b, 2)
    return _bdg(ah, bh, dims) + (_bdg(ah, bl, dims) + _bdg(al, bh, dims))


@jax.custom_vjp
def _hbnn(a, b):
    return _bdot3(a, b, _BNN)


@jax.custom_vjp
def _hbnt(a, b):
    return _bdot3(a, b, _BNT)


@jax.custom_vjp
def _hbtn(a, b):
    return _bdot3(a, b, _BTN)


_hbnn.defvjp(lambda a, b: (_hbnn(a, b), (a, b)), lambda res, g: (_hbnt(g, res[1]), _hbtn(res[0], g)))
_hbnt.defvjp(lambda a, b: (_hbnt(a, b), (a, b)), lambda res, g: (_hbnn(g, res[1]), _hbtn(g, res[0])))
_hbtn.defvjp(lambda a, b: (_hbtn(a, b), (a, b)), lambda res, g: (_hbnt(res[1], g), _hbnn(res[0], g)))


@functools.partial(jax.custom_vjp, nondiff_argnums=(2,))
def _exact_bl(m, x, transposed):
    mb = m.astype(bf16)
    hi, mid, lo = _bf16_pieces(x, 3)
    dims = _BTN if transposed else _BNN
    return (_bdg(mb, hi, dims) + _bdg(mb, mid, dims)) + _bdg(mb, lo, dims)


_exact_bl.defvjp(lambda m, x, transposed: (_exact_bl(m, x, transposed), m),
                 lambda transposed, m, g: (jnp.zeros_like(m), _exact_bl(m, g, not transposed)))


@functools.partial(jax.custom_vjp, nondiff_argnums=(2,))
def _exact_l(m, x, transposed):
    mb = m.astype(bf16)
    hi, mid, lo = _bf16_pieces(x, 3)
    dims = _TN if transposed else _NN
    return (_dg(mb, hi, dims) + _dg(mb, mid, dims)) + _dg(mb, lo, dims)


_exact_l.defvjp(lambda m, x, transposed: (_exact_l(m, x, transposed), m),
                lambda transposed, m, g: (jnp.zeros_like(m), _exact_l(m, g, not transposed)))


@functools.partial(jax.custom_vjp, nondiff_argnums=(2,))
def _exact_r(x, m, transposed):
    mb = m.astype(bf16)
    hi, mid, lo = _bf16_pieces(x, 3)
    dims = _NT if transposed else _NN
    return (_dg(hi, mb, dims) + _dg(mid, mb, dims)) + _dg(lo, mb, dims)


_exact_r.defvjp(lambda x, m, transposed: (_exact_r(x, m, transposed), m),
                lambda transposed, m, g: (_exact_r(g, m, not transposed), jnp.zeros_like(m)))


def _rms(x, g, eps=NORM_EPS):
    return x * lax.rsqrt(jnp.mean(x * x, axis=-1, keepdims=True) + eps) * g


def _sigmoid(x):
    return 1.0 / (1.0 + jnp.exp(-x))


def _softplus(x):
    return jnp.maximum(x, 0.0) + jnp.log(1.0 + jnp.exp(-jnp.maximum(x, -x)))


def _lane_split(x, sizes):
    bounds = np.cumsum([0] + list(sizes))

    @jax.custom_vjp
    def split(v):
        return tuple(v[..., int(bounds[q]):int(bounds[q + 1])] for q in range(len(sizes)))

    split.defvjp(lambda v: (split(v), None), lambda _, g: (jnp.concatenate(g, axis=-1),))
    return split(x)


def _row_split(x, sizes):
    bounds = np.cumsum([0] + list(sizes))

    @jax.custom_vjp
    def split(v):
        return tuple(v[..., int(bounds[q]):int(bounds[q + 1]), :] for q in range(len(sizes)))

    split.defvjp(lambda v: (split(v), None), lambda _, g: (jnp.concatenate(g, axis=-2),))
    return split(x)


def _shift_mats(t, k):
    r = lax.broadcasted_iota(jnp.int32, (t, t), 0)
    c = lax.broadcasted_iota(jnp.int32, (t, t), 1)
    inner = (r - c == k).astype(f32)
    r8 = lax.broadcasted_iota(jnp.int32, (t, 8), 0)
    c8 = lax.broadcasted_iota(jnp.int32, (t, 8), 1)
    edge = (c8 - r8 == 8 - k).astype(f32)
    return inner, edge


def _shift(x, halo, k):
    inner, edge = _shift_mats(x.shape[0], k)
    return _exact_l(inner, x, False) + jnp.dot(edge, halo, precision=HI, preferred_element_type=f32)


def mm(a, b, *, name, ta=False, tb=False, a_batched=False, b_batched=False, reduce_batch=False, add=None,
       act_grad=None, relu2_out=False, n_split=1, out_dtype=f32, tm=1024, tn=1024, tk=2048):
    ash, bsh = a.shape[-2:], b.shape[-2:]
    (k_, m_) = ash if ta else ash[::-1]
    (k2_, n_) = bsh[::-1] if tb else bsh
    assert k_ == k2_, (a.shape, b.shape, ta, tb)
    hb = a.shape[0] if a_batched else (b.shape[0] if b_batched else 1)
    batched_out = (a_batched or b_batched) and not reduce_batch
    h_out = hb if batched_out else 1
    h_red = hb if reduce_batch else 1
    tm = _pick(m_, tm, 128)
    tn = _pick(n_ // n_split, tn, 128)
    tk = _pick(k_, tk, 128)
    nm, nn, nk = m_ // tm, n_ // tn, k_ // tk

    def a_map(i, j, ho, hr, kk):
        blk = (kk, i) if ta else (i, kk)
        return ((ho if batched_out else hr),) + blk if a_batched else blk

    def b_map(i, j, ho, hr, kk):
        blk = (j, kk) if tb else (kk, j)
        return ((ho if batched_out else hr),) + blk if b_batched else blk

    a_blk = (tk, tm) if ta else (tm, tk)
    b_blk = (tn, tk) if tb else (tk, tn)
    in_specs = [pl.BlockSpec(((1,) + a_blk) if a_batched else a_blk, a_map),
                pl.BlockSpec(((1,) + b_blk) if b_batched else b_blk, b_map)]
    args = [a, b]
    for extra in (add, act_grad):
        if extra is not None:
            in_specs.append(pl.BlockSpec((tm, tn), lambda i, j, ho, hr, kk: (i, j)))
            args.append(extra)
    if n_split > 1:
        per = n_ // n_split // tn
        if batched_out:
            out_spec = pl.BlockSpec((1, 1, tm, tn), lambda i, j, ho, hr, kk: (j // per, ho, i, j % per))
            out_shape = jax.ShapeDtypeStruct((n_split, hb, m_, n_ // n_split), out_dtype)
        else:
            out_spec = pl.BlockSpec((1, tm, tn), lambda i, j, ho, hr, kk: (j // per, i, j % per))
            out_shape = jax.ShapeDtypeStruct((n_split, m_, n_ // n_split), out_dtype)
    elif batched_out:
        out_spec = pl.BlockSpec((1, tm, tn), lambda i, j, ho, hr, kk: (ho, i, j))
        out_shape = jax.ShapeDtypeStruct((hb, m_, n_), out_dtype)
    else:
        out_spec = pl.BlockSpec((tm, tn), lambda i, j, ho, hr, kk: (i, j))
        out_shape = jax.ShapeDtypeStruct((m_, n_), out_dtype)
    lead = (0,) * (int(batched_out) + int(n_split > 1))
    dims = ((0,) if ta else (1,), (1,) if tb else (0,))
    has_add, has_act = add is not None, act_grad is not None

    def body(*refs):
        a_ref, b_ref = refs[0], refs[1]
        pos = 2
        add_ref = act_ref = None
        if has_add:
            add_ref = refs[pos]
            pos += 1
        if has_act:
            act_ref = refs[pos]
            pos += 1
        o_ref, acc_ref = refs[pos], refs[-1]
        hr, kk = pl.program_id(3), pl.program_id(4)
        first = jnp.logical_and(hr == 0, kk == 0)
        last = jnp.logical_and(hr == h_red - 1, kk == nk - 1)
        av = a_ref[0] if a_batched else a_ref[...]
        bv = b_ref[0] if b_batched else b_ref[...]
        p = _bdot(av, bv, dims)
        single = h_red * nk == 1

        if not single:
            @pl.when(first)
            def _():
                acc_ref[...] = p

            @pl.when(jnp.logical_not(first))
            def _():
                acc_ref[...] += p

        @pl.when(last)
        def _():
            r = p if single else acc_ref[...]
            if has_act:
                r = r * (2.0 * jnp.maximum(act_ref[...], 0.0))
            if has_add:
                r = r + add_ref[...]
            if lead:
                o_ref[lead] = r.astype(out_dtype)
            else:
                o_ref[...] = r.astype(out_dtype)
            if relu2_out:
                refs[pos + 1][...] = jnp.square(jnp.maximum(r, 0.0)).astype(bf16)

    if relu2_out:
        assert not lead
        out_spec = [out_spec, out_spec]
        out_shape = [out_shape, jax.ShapeDtypeStruct(out_shape.shape, bf16)]
    return pl.pallas_call(
        body, name=name, grid=(nm, nn, h_out, h_red, nk), in_specs=in_specs, out_specs=out_spec,
        out_shape=out_shape, scratch_shapes=[pltpu.VMEM((tm, tn), f32)],
        compiler_params=_cparams(("parallel", "parallel", "parallel", "arbitrary", "arbitrary")),
    )(*args)


def _row_spec(arr, tile, idx):
    if arr.ndim == 2:
        return pl.BlockSpec((tile, arr.shape[1]), lambda i: (idx(i), 0))
    return pl.BlockSpec((arr.shape[0], tile, arr.shape[2]), lambda i: (0, idx(i), 0))


def _halo_spec(arr, tile, idx):
    per = tile // 8
    return pl.BlockSpec((8, arr.shape[1]), lambda i: (jnp.maximum(idx(i) * per - 1, 0), 0))


def _full_spec(arr):
    nd = arr.ndim
    return pl.BlockSpec(arr.shape, lambda i: (0,) * nd)


def _load_f32(ref):
    val = ref[...]
    return val.astype(f32) if val.dtype == bf16 else val


def rows_fwd(fn, rows, params, consts, out_shapes, *, tile, name, halos=()):
    s_len = rows[0].shape[-2]
    n = s_len // tile
    nr, nh, npar, nc = len(rows), len(halos), len(params), len(consts)
    ident = lambda i: i
    in_specs = ([_row_spec(r, tile, ident) for r in rows] + [_halo_spec(rows[h], tile, ident) for h in halos]
                + [_full_spec(p) for p in params] + [_full_spec(c) for c in consts])
    out_specs = [_row_spec(o, tile, ident) for o in out_shapes]

    def body(*refs):
        i = pl.program_id(0)
        rv = [_load_f32(r) for r in refs[:nr]]
        keep = (i > 0).astype(f32)
        hv = [r[...] * keep for r in refs[nr:nr + nh]]
        pv = [r[...] for r in refs[nr + nh:nr + nh + npar]]
        cv = [r[...] for r in refs[nr + nh + npar:nr + nh + npar + nc]]
        outs = fn(rv, hv, pv, cv)
        for o_ref, o in zip(refs[nr + nh + npar + nc:], outs):
            o_ref[...] = o.astype(o_ref.dtype)

    return pl.pallas_call(
        body, name=name, grid=(n,), in_specs=in_specs, out_specs=out_specs, out_shape=list(out_shapes),
        compiler_params=_cparams(("arbitrary",)),
    )(*rows, *[rows[h] for h in halos], *params, *consts)


def rows_bwd(fn, rows, params, consts, douts, *, tile, name, halos=(), grad_rows=None, extra=None,
             grad_dtypes=None):
    s_len = rows[0].shape[-2]
    n = s_len // tile
    nr, nh, npar, nc = len(rows), len(halos), len(params), len(consts)
    grad_rows = list(range(nr)) if grad_rows is None else list(grad_rows)
    extra = extra or {}
    assert all(h in grad_rows for h in halos)
    rev = lambda i: n - 1 - i
    dflat = [d for ds in douts for d in ds]
    dcount = [len(ds) for ds in douts]
    eflat = [e for g in grad_rows for e in extra.get(g, [])]
    ecount = [len(extra.get(g, [])) for g in grad_rows]
    in_specs = ([_row_spec(r, tile, rev) for r in rows] + [_halo_spec(rows[h], tile, rev) for h in halos]
                + [_full_spec(p) for p in params] + [_full_spec(c) for c in consts]
                + [_row_spec(d, tile, rev) for d in dflat] + [_row_spec(e, tile, rev) for e in eflat])
    grad_dtypes = [f32] * len(grad_rows) if grad_dtypes is None else list(grad_dtypes)
    assert all(grad_dtypes[q] == f32 for q, g in enumerate(grad_rows) if g in halos)
    out_shapes = ([jax.ShapeDtypeStruct(rows[g].shape, dt) for g, dt in zip(grad_rows, grad_dtypes)]
                  + [jax.ShapeDtypeStruct(p.shape, f32) for p in params])
    out_specs = [_row_spec(rows[g], tile, rev) for g in grad_rows] + [_full_spec(p) for p in params]
    scratch = [pltpu.VMEM((8, rows[h].shape[1]), f32) for h in halos]
    n_in = nr + nh + npar + nc + len(dflat) + len(eflat)
    n_out = len(grad_rows) + npar

    def body(*refs):
        i = pl.program_id(0)
        rv = [_load_f32(r) for r in refs[:nr]]
        keep = (i < n - 1).astype(f32)
        hv = [r[...] * keep for r in refs[nr:nr + nh]]
        pv = [r[...] for r in refs[nr + nh:nr + nh + npar]]
        pos = nr + nh + npar
        cv = [r[...] for r in refs[pos:pos + nc]]
        pos += nc
        dv = []
        for cnt in dcount:
            acc = _load_f32(refs[pos])
            for q in range(1, cnt):
                acc = acc + _load_f32(refs[pos + q])
            dv.append(acc)
            pos += cnt
        ev = []
        for cnt in ecount:
            ev.append([_load_f32(refs[pos + q]) for q in range(cnt)])
            pos += cnt
        out_refs = refs[n_in:n_in + n_out]
        carry_refs = refs[n_in + n_out:]

        def f(gr, gh, gp):
            full = list(rv)
            for g, val in zip(grad_rows, gr):
                full[g] = val
            return tuple(fn(full, gh, gp, cv))

        _, vjp = jax.vjp(f, [rv[g] for g in grad_rows], hv, pv)
        d_rows, d_halos, d_params = vjp(tuple(dv))

        @pl.when(i == 0)
        def _():
            for c_ref in carry_refs:
                c_ref[...] = jnp.zeros_like(c_ref)
            for p_ref in out_refs[len(grad_rows):]:
                p_ref[...] = jnp.zeros_like(p_ref)

        for q, g in enumerate(grad_rows):
            val = d_rows[q]
            for e in ev[q]:
                val = val + e
            out_refs[q][...] = val.astype(out_refs[q].dtype)
            if g in halos:
                hq = list(halos).index(g)
                out_refs[q][tile - 8:tile, :] += carry_refs[hq][...]
                carry_refs[hq][...] = d_halos[hq]
        for p_ref, dp in zip(out_refs[len(grad_rows):], d_params):
            p_ref[...] += dp

    res = pl.pallas_call(
        body, name=name, grid=(n,), in_specs=in_specs, out_specs=out_specs, out_shape=out_shapes,
        scratch_shapes=scratch, compiler_params=_cparams(("arbitrary",)),
    )(*rows, *[rows[h] for h in halos], *params, *consts, *dflat, *eflat)
    return list(res[:len(grad_rows)]), list(res[len(grad_rows):])


def _fn_norm(rows, halos, params, consts):
    return (_rms(rows[0], params[0]),)


def _fn_mla_prep(rows, halos, params, consts):
    cq, ckv, kpe = _lane_split(rows[0], (QL, KVL, MLA_PAD - QL - KVL))
    return _rms(cq, params[0]), _rms(ckv, params[1]), kpe


def _rope(x, cos, sin, rot):
    return x * cos + _exact_r(x, rot, False) * sin


def _fn_qk_post(rows, halos, params, consts):
    q_flat, kv_flat, kpe, cos, sin = rows
    q_norm, k_norm = params
    (rot,) = consts
    q_heads = _lane_split(q_flat, (DQK, Q_HEAD_PAD - DQK) * MLA_H)[::2]
    kv_heads = _lane_split(kv_flat, (NOPE, DV) * MLA_H)
    k_pe, _ = _lane_split(kpe, (ROPE, kpe.shape[1] - ROPE))
    qs = [_rope(_rms(qh, q_norm), cos, sin, rot) for qh in q_heads]
    ks = [_rope(_rms(jnp.concatenate([kv_heads[2 * h], k_pe], axis=-1), k_norm), cos, sin, rot)
          for h in range(MLA_H)]
    vs = [kv_heads[2 * h + 1] for h in range(MLA_H)]
    return jnp.stack(qs, axis=0), jnp.stack(ks, axis=0), jnp.stack(vs, axis=0)


def _seg(x, bd):
    return _exact_r(x, bd, False)


def _make_fn_rwkv_prep(vres):
    def fn(rows, halos, params, consts):
        cols = rows[0]
        bd = consts[0]
        mu, w0, w2, a0, a2, g2, k_k, k_a = params[:8]
        prev = _shift(cols, halos[0], 1)
        c = cols + (prev - cols) * mu
        r, k, v, xw, xa, xg = _lane_split(c, (RW, RW, RW, DL, AL, GL))
        log_w = -_softplus(-(w0 + _mm(jnp.tanh(xw), w2))) - 0.5
        ld = -jnp.exp(log_w)
        a = _sigmoid(a0 + _mm(xa, a2))
        g = _mm(_sigmoid(xg), g2)
        if vres:
            hcur, v_first = rows[1], rows[2]
            v1, v_mu, v0, v2 = params[8:12]
            xv = _mm(hcur, v1)
            xv_prev = _shift(xv, _mm(halos[1], v1), 1)
            xv = xv + (xv_prev - xv) * v_mu
            v = v + (v_first - v) * _sigmoid(v0 + _mm(xv, v2))
        kk = k * k_k
        kk = kk / jnp.maximum(jnp.sqrt(_seg(kk * kk, bd)), 1e-12)
        k2 = k * (1.0 + (a - 1.0) * k_a)
        return r, ld, k2, v, -kk, kk * a, g
    return fn


def _fn_rwkv_post(rows, halos, params, consts):
    y, r, k2, v, g = rows
    ln_w, ln_b, r_k = params
    bd = consts[0]
    mean = _seg(y, bd) * (1.0 / RN)
    d = y - mean
    var = _seg(d * d, bd) * (1.0 / RN)
    yn = d * lax.rsqrt(var + GN_EPS) * ln_w + ln_b
    bonus = _seg(r * k2 * r_k, bd) * v
    return ((yn + bonus) * g,)


def _fn_conv(rows, halos, params, consts):
    cols, halo = rows[0], halos[0]
    w0, w1, w2 = params
    b, c, x = _lane_split(cols, (CW, CW, CW))
    _, ch, xh = _lane_split(halo, (CW, CW, CW))
    u, uh = c * x, ch * xh
    return (b * (w0 * _shift(u, uh, 2) + w1 * _shift(u, uh, 1) + w2 * u),)


def _fn_merge(rows, halos, params, consts):
    gate, o_a, o_b, o_c = rows
    g_a, g_b, g_c = _lane_split(gate, (D, D, D))
    return (_sigmoid(g_a) * o_a + _sigmoid(g_b) * o_b + _sigmoid(g_c) * o_c,)


def _attn_block(q, k, v, q0, diagonal_last):
    tq, kend = q.shape[0], k.shape[0]
    s = _mm_nt(q, k) * (DQK ** -0.5)
    if diagonal_last:
        tri = lax.broadcasted_iota(jnp.int32, (tq, tq), 0) >= lax.broadcasted_iota(jnp.int32, (tq, tq), 1)
        if kend > tq:
            before, diag = _lane_split(s, (kend - tq, tq))
            s = jnp.concatenate([before, jnp.where(tri, diag, -1e30)], axis=-1)
        else:
            s = jnp.where(tri, s, -1e30)
    else:
        row = q0 + lax.broadcasted_iota(jnp.int32, (tq, kend), 0)
        col = lax.broadcasted_iota(jnp.int32, (tq, kend), 1)
        s = jnp.where(row >= col, s, -1e30)
    m = lax.stop_gradient(jnp.max(s, axis=-1, keepdims=True))
    e = jnp.exp(s - m)
    p = e / jnp.sum(e, axis=-1, keepdims=True)
    return _mm(p, v)


def _attn_segments(s_len, tq):
    per = max(1, s_len // tq // ATTN_SEGMENTS)
    return [(first, per, (first + per) * tq) for first in range(0, s_len // tq, per)]


HEAD_PAIR = 2


def attn_fwd(q, k, v, *, tq, name):
    h, s_len, _ = q.shape
    outs = []
    for seg, (first, nq, kend) in enumerate(_attn_segments(s_len, tq)):
        def body(q_ref, k_ref, v_ref, o_ref, first=first, nq=nq):
            q0 = (first + pl.program_id(1)) * tq
            o = [_attn_block(q_ref[j], k_ref[j], v_ref[j], q0, nq == 1) for j in range(HEAD_PAIR)]
            o_ref[...] = jnp.concatenate(o, axis=-1).astype(o_ref.dtype)

        outs.append(pl.pallas_call(
            body, name=f"{name}_{seg}", grid=(h // HEAD_PAIR, nq),
            in_specs=[pl.BlockSpec((HEAD_PAIR, tq, DQK), lambda hp, i, first=first: (hp, first + i, 0)),
                      pl.BlockSpec((HEAD_PAIR, kend, DQK), lambda hp, i: (hp, 0, 0)),
                      pl.BlockSpec((HEAD_PAIR, kend, DV), lambda hp, i: (hp, 0, 0))],
            out_specs=pl.BlockSpec((tq, HEAD_PAIR * DV), lambda hp, i: (i, hp)),
            out_shape=jax.ShapeDtypeStruct((nq * tq, h * DV), bf16),
            compiler_params=_cparams(("parallel", "arbitrary")),
        )(q, k, v))
    return jnp.concatenate(outs, axis=0)


def attn_bwd(q, k, v, do, *, tq, name):
    h, s_len, _ = q.shape
    dqs, dk_acc, dv_acc = [], None, None
    for seg, (first, nq, kend) in reversed(list(enumerate(_attn_segments(s_len, tq)))):
        carried = dk_acc is not None

        def body(*refs, first=first, carried=carried, nq=nq):
            q_ref, k_ref, v_ref, do_ref = refs[:4]
            dq_ref, dk_ref, dv_ref = refs[-3:]
            i = pl.program_id(1)
            do_heads = _lane_split(do_ref[...], (DV,) * HEAD_PAIR)
            for j in range(HEAD_PAIR):
                _, vjp = jax.vjp(functools.partial(_attn_block, q0=(first + i) * tq, diagonal_last=nq == 1),
                                 q_ref[j], k_ref[j], v_ref[j])
                dq, dk, dv = vjp(do_heads[j])
                dq_ref[j] = dq

                @pl.when(i == 0)
                def _():
                    dk_ref[j] = dk + refs[4][j] if carried else dk
                    dv_ref[j] = dv + refs[5][j] if carried else dv

                @pl.when(i > 0)
                def _():
                    dk_ref[j] += dk
                    dv_ref[j] += dv

        key_specs = [pl.BlockSpec((HEAD_PAIR, kend, DQK), lambda hp, i: (hp, 0, 0)),
                     pl.BlockSpec((HEAD_PAIR, kend, DV), lambda hp, i: (hp, 0, 0))]
        dq, dk_acc, dv_acc = pl.pallas_call(
            body, name=f"{name}_{seg}", grid=(h // HEAD_PAIR, nq),
            in_specs=[pl.BlockSpec((HEAD_PAIR, tq, DQK), lambda hp, i, first=first: (hp, first + i, 0))] + key_specs
            + [pl.BlockSpec((tq, HEAD_PAIR * DV), lambda hp, i, first=first: (first + i, hp))]
            + (key_specs if carried else []),
            out_specs=[pl.BlockSpec((HEAD_PAIR, tq, DQK), lambda hp, i: (hp, i, 0))] + key_specs,
            out_shape=[jax.ShapeDtypeStruct((h, nq * tq, DQK), f32), jax.ShapeDtypeStruct((h, s_len, DQK), f32),
                       jax.ShapeDtypeStruct((h, s_len, DV), f32)],
            input_output_aliases={4: 1, 5: 2} if carried else {},
            compiler_params=_cparams(("parallel", "arbitrary")),
        )(q, k, v, do, *([dk_acc, dv_acc] if carried else []))
        dqs.append(dq)
    return jnp.concatenate(dqs[::-1], axis=1), dk_acc, dv_acc


def _wkv_local(r, ld, k, v, a, b):
    nb, c, n = r.shape
    ri = lax.broadcasted_iota(jnp.int32, (c, c), 0)
    ci = lax.broadcasted_iota(jnp.int32, (c, c), 1)
    tri = jnp.broadcast_to((ri >= ci).astype(f32)[None], (nb, c, c))
    cum = _exact_bl(tri, ld, False)
    tot = jnp.sum(ld, axis=1, keepdims=True)
    w_incl, w_excl, w_inv, w_rest = jnp.exp(cum), jnp.exp(cum - ld), jnp.exp(-cum), jnp.exp(tot - cum)
    ab, rb, bb, kb = a * w_excl, r * w_incl, b * w_inv, k * w_inv
    bw, kw = b * w_rest, k * w_rest
    r2 = lax.broadcasted_iota(jnp.int32, (2 * c, 2 * c), 0)
    c2 = lax.broadcasted_iota(jnp.int32, (2 * c, 2 * c), 1)
    t_of, s_of = jnp.where(r2 >= c, r2 - c, r2), jnp.where(c2 >= c, c2 - c, c2)
    keep = jnp.logical_or(t_of > s_of, jnp.logical_and(r2 >= c, t_of == s_of))
    pair = jnp.where(keep[None], _hbnt(jnp.concatenate([ab, rb], axis=1), jnp.concatenate([bb, kb], axis=1)), 0.0)
    on_b, on_k = _lane_split(pair, (c, c))
    l_ab, m_rb = _row_split(on_b, (c, c))
    l_ak_v, m_rk_v = _row_split(_hbnn(on_k, v), (c, c))
    x = jnp.concatenate([ab, l_ak_v], axis=-1)
    lp, span = l_ab, 1
    while span < c:
        x = x + _hbnn(lp, x)
        span *= 2
        if span < c:
            lp = _hbnn(lp, lp)
    via_b_r, via_b_y = _lane_split(_hbnn(m_rb, x), (n, n))
    r_hat = rb + via_b_r
    y0 = via_b_y + m_rk_v
    from_b_g, from_b_z = _row_split(_hbtn(x, bw), (n, n))
    eye = lax.broadcasted_iota(jnp.int32, (n, n), 0) == lax.broadcasted_iota(jnp.int32, (n, n), 1)
    g = jnp.where(eye[None], jnp.exp(tot), 0.0) + from_b_g
    z = from_b_z + _hbtn(v, kw)
    return r_hat, y0, g, z


def _head(h):
    return slice(RN * h, RN * (h + 1))


def _load_chunk_heads(ref, c, per):
    return jnp.stack([ref[c * q:c * (q + 1), _head(h)] for q in range(per) for h in range(RH)], axis=0)


def _store_chunk_heads(ref, val, c, per):
    for q in range(per):
        ref[c * q:c * (q + 1), :] = jnp.concatenate([val[q * RH + h] for h in range(RH)], axis=-1)


def wkv_fwd(r, ld, k, v, a, b, *, name):
    s_len = r.shape[0]
    c = WKV_CHUNK
    n = s_len // c
    per = min(WKV_CHUNKS_PER_STEP, n)
    rows = pl.BlockSpec((c * per, RW), lambda i: (i, 0))
    mats = pl.BlockSpec((per, RH, RN, RN), lambda i: (i, 0, 0, 0))
    rows_t, mats_t = jax.ShapeDtypeStruct((s_len, RW), f32), jax.ShapeDtypeStruct((n, RH, RN, RN), f32)

    def local_body(r_ref, ld_ref, k_ref, v_ref, a_ref, b_ref, rh_ref, y0_ref, g_ref, z_ref):
        r_hat, y0, g, z = _wkv_local(*[_load_chunk_heads(ref, c, per)
                                       for ref in (r_ref, ld_ref, k_ref, v_ref, a_ref, b_ref)])
        _store_chunk_heads(rh_ref, r_hat, c, per)
        _store_chunk_heads(y0_ref, y0, c, per)
        g_ref[...] = g.reshape(per, RH, RN, RN)
        z_ref[...] = z.reshape(per, RH, RN, RN)

    r_hat, y0, g, z = pl.pallas_call(
        local_body, name=name + "_local", grid=(n // per,), in_specs=[rows] * 6, out_specs=[rows, rows, mats, mats],
        out_shape=[rows_t, rows_t, mats_t, mats_t], compiler_params=_cparams(("parallel",)),
    )(r, ld, k, v, a, b)

    def scan_body(g_ref, z_ref, st_ref, s_sc):
        s_sc[...] = jnp.zeros_like(s_sc)

        @pl.loop(0, n)
        def _(i):
            s0 = s_sc[...]
            st_ref[i] = s0
            s_sc[...] = _hbnn(s0, g_ref[i]) + z_ref[i]

    vm = pl.BlockSpec(memory_space=pltpu.VMEM)
    states = pl.pallas_call(
        scan_body, name=name + "_scan", in_specs=[vm, vm], out_specs=vm, out_shape=mats_t,
        scratch_shapes=[pltpu.VMEM((RH, RN, RN), f32)],
        compiler_params=pltpu.CompilerParams(vmem_limit_bytes=VMEM_LIMIT),
    )(g, z)

    def out_body(rh_ref, y0_ref, st_ref, y_ref):
        y = _hbnt(_load_chunk_heads(rh_ref, c, per), st_ref[...].reshape(per * RH, RN, RN))
        _store_chunk_heads(y_ref, y, c, per)
        y_ref[...] += y0_ref[...]

    y = pl.pallas_call(
        out_body, name=name + "_out", grid=(n // per,), in_specs=[rows, rows, mats], out_specs=rows,
        out_shape=rows_t, compiler_params=_cparams(("parallel",)),
    )(r_hat, y0, states)
    return y, dict(r_hat=r_hat, g=g, states=states)


def wkv_bwd(r, ld, k, v, a, b, saved, dy, *, name):
    s_len = r.shape[0]
    c = WKV_CHUNK
    n = s_len // c
    per = min(WKV_CHUNKS_PER_STEP, n)
    rows = pl.BlockSpec((c * per, RW), lambda i: (i, 0))
    mats = pl.BlockSpec((per, RH, RN, RN), lambda i: (i, 0, 0, 0))
    rows_t, mats_t = jax.ShapeDtypeStruct((s_len, RW), f32), jax.ShapeDtypeStruct((n, RH, RN, RN), f32)

    def out_body(dy_ref, rh_ref, st_ref, drh_ref, dsy_ref):
        dyb = _load_chunk_heads(dy_ref, c, per)
        _store_chunk_heads(drh_ref, _hbnn(dyb, st_ref[...].reshape(per * RH, RN, RN)), c, per)
        dsy_ref[...] = _hbtn(dyb, _load_chunk_heads(rh_ref, c, per)).reshape(per, RH, RN, RN)

    d_rhat, ds_y = pl.pallas_call(
        out_body, name=name + "_out", grid=(n // per,), in_specs=[rows, rows, mats], out_specs=[rows, mats],
        out_shape=[rows_t, mats_t], compiler_params=_cparams(("parallel",)),
    )(dy, saved["r_hat"], saved["states"])

    def scan_body(dsy_ref, g_ref, st_ref, dg_ref, dz_ref, ds_sc):
        ds_sc[...] = jnp.zeros_like(ds_sc)

        @pl.loop(0, n)
        def _(i):
            cidx = n - 1 - i
            ds_next = ds_sc[...]
            dz_ref[cidx] = ds_next
            dg_ref[cidx] = _hbtn(st_ref[cidx], ds_next)
            ds_sc[...] = dsy_ref[cidx] + _hbnt(ds_next, g_ref[cidx])

    vm = pl.BlockSpec(memory_space=pltpu.VMEM)
    d_g, d_z = pl.pallas_call(
        scan_body, name=name + "_scan", in_specs=[vm, vm, vm], out_specs=[vm, vm], out_shape=[mats_t, mats_t],
        scratch_shapes=[pltpu.VMEM((RH, RN, RN), f32)],
        compiler_params=pltpu.CompilerParams(vmem_limit_bytes=VMEM_LIMIT),
    )(ds_y, saved["g"], saved["states"])

    def local_body(r_ref, ld_ref, k_ref, v_ref, a_ref, b_ref, drh_ref, dy_ref, dg_ref, dz_ref, *out_refs):
        _, vjp = jax.vjp(_wkv_local, *[_load_chunk_heads(ref, c, per)
                                       for ref in (r_ref, ld_ref, k_ref, v_ref, a_ref, b_ref)])
        grads = vjp((_load_chunk_heads(drh_ref, c, per), _load_chunk_heads(dy_ref, c, per),
                     dg_ref[...].reshape(per * RH, RN, RN), dz_ref[...].reshape(per * RH, RN, RN)))
        for o_ref, val in zip(out_refs, grads):
            _store_chunk_heads(o_ref, val, c, per)

    return pl.pallas_call(
        local_body, name=name + "_local", grid=(n // per,), in_specs=[rows] * 8 + [mats, mats], out_specs=[rows] * 6,
        out_shape=[rows_t] * 6, compiler_params=_cparams(("parallel",)),
    )(r, ld, k, v, a, b, d_rhat, dy, d_g, d_z)


def loss_head(y, target, *, tile, name):
    s_len, d = y.shape
    n = s_len // tile

    def body(y_ref, t_ref, dy_ref, l_ref):
        err = y_ref[...] - t_ref[...]
        dy_ref[...] = err * (1.0 / d)
        part = 0.5 * jnp.sum(jnp.mean(err * err, axis=-1, keepdims=True), axis=0, keepdims=True)

        @pl.when(pl.program_id(0) == 0)
        def _():
            l_ref[...] = jnp.zeros_like(l_ref)

        l_ref[...] += jnp.broadcast_to(part, l_ref.shape)

    bs = pl.BlockSpec((tile, d), lambda i: (i, 0))
    dy, l = pl.pallas_call(
        body, name=name, grid=(n,), in_specs=[bs, bs],
        out_specs=[bs, pl.BlockSpec((8, 128), lambda i: (0, 0))],
        out_shape=[jax.ShapeDtypeStruct((s_len, d), f32), jax.ShapeDtypeStruct((8, 128), f32)],
        compiler_params=_cparams(("arbitrary",)),
    )(y, target)
    return l[0, 0], dy


def _adamw_update(w, g, m, v):
    mn = B1 * m + (1.0 - B1) * g
    vn = B2 * v + (1.0 - B2) * (g * g)
    delta = -LR * ((mn / (1.0 - B1 ** STEP)) / (jnp.sqrt(vn / (1.0 - B2 ** STEP)) + EPS) + WD * w)
    return delta, mn, vn


def adamw(w, g, m, v, *, name):
    rows, cols = w.shape
    tile = _pick(rows, max(8, (2 * 1024 * 1024 // (4 * cols)) // 8 * 8), 8)

    def body(w_ref, g_ref, m_ref, v_ref, g_out, d_out, m_out, v_out):
        gv = g_ref[...]
        d_out[...], m_out[...], v_out[...] = _adamw_update(w_ref[...], gv, m_ref[...], v_ref[...])
        g_out[...] = gv

    bs = pl.BlockSpec((tile, cols), lambda i: (i, 0))
    return pl.pallas_call(
        body, name=name, grid=(rows // tile,), in_specs=[bs] * 4, out_specs=[bs] * 4,
        out_shape=[jax.ShapeDtypeStruct((rows, cols), f32)] * 4, compiler_params=_cparams(("parallel",)),
    )(w, g, m, v)


def adamw_many(ws, gs, ms, vs, *, name):
    n = len(ws)

    def body(*refs):
        ins, outs = refs[:4 * n], refs[4 * n:]
        for i in range(n):
            outs[i][...], outs[n + i][...], outs[2 * n + i][...] = _adamw_update(
                ins[i][...], ins[n + i][...], ins[2 * n + i][...], ins[3 * n + i][...])

    vm = pl.BlockSpec(memory_space=pltpu.VMEM)
    res = pl.pallas_call(
        body, name=name, in_specs=[vm] * (4 * n), out_specs=[vm] * (3 * n),
        out_shape=[jax.ShapeDtypeStruct(w.shape, f32) for w in ws] * 3,
        compiler_params=pltpu.CompilerParams(vmem_limit_bytes=VMEM_LIMIT),
    )(*ws, *gs, *ms, *vs)
    return list(res[:n]), list(res[n:2 * n]), list(res[2 * n:])


def _place():
    return lax.axis_index("x"), lax.axis_index("y"), lax.axis_index("c")


_ANY = pl.BlockSpec(memory_space=pl.ANY)


def _peer_chips(x, y):
    return [(1 - x, y), (x, 1 - y), (1 - x, 1 - y)]


def gather_weights(shards, *, name):
    nk = len(shards)

    def body(*refs):
        srcs, outs = refs[:nk], refs[nk:2 * nk]
        ici_send, ici_recv, d2d_send, d2d_recv = refs[2 * nk + 1:]
        x, y, c = _place()
        me = 2 * x + y
        peers = _peer_chips(x, y)
        pending = []
        for k in range(nk):
            half = srcs[k].shape[0] // 2
            mine = pl.ds(c * half, half)
            for p, (px, py) in enumerate(peers):
                cp = pltpu.make_async_remote_copy(
                    src_ref=srcs[k].at[mine], dst_ref=outs[k].at[me, mine], send_sem=ici_send.at[k, p],
                    recv_sem=ici_recv.at[k, p], device_id=(px, py, c), device_id_type=MESH)
                cp.start()
                pending.append(cp)
        for k in range(nk):
            half = srcs[k].shape[0] // 2
            mine = pl.ds(c * half, half)
            for p, (px, py) in enumerate(peers):
                landed = outs[k].at[2 * px + py, mine]
                pltpu.make_async_remote_copy(
                    src_ref=srcs[k].at[mine], dst_ref=landed, send_sem=ici_send.at[k, p], recv_sem=ici_recv.at[k, p],
                    device_id=(px, py, c), device_id_type=MESH).wait_recv()
                fwd = pltpu.make_async_remote_copy(
                    src_ref=landed, dst_ref=landed, send_sem=d2d_send.at[k, p], recv_sem=d2d_recv.at[k, p],
                    device_id=(x, y, 1 - c), device_id_type=MESH)
                fwd.start()
                pending.append(fwd)
        for k in range(nk):
            half = srcs[k].shape[0] // 2
            other = pl.ds((1 - c) * half, half)
            for p, (px, py) in enumerate(peers):
                theirs = outs[k].at[2 * px + py, other]
                pltpu.make_async_remote_copy(
                    src_ref=theirs, dst_ref=theirs, send_sem=d2d_send.at[k, p], recv_sem=d2d_recv.at[k, p],
                    device_id=(x, y, 1 - c), device_id_type=MESH).wait_recv()
        for cp in pending:
            cp.wait_send()
        refs[2 * nk][...] = jnp.zeros_like(refs[2 * nk])

    sem = lambda *shape: pltpu.SemaphoreType.DMA(shape)
    res = pl.pallas_call(
        body, name=name, in_specs=[_ANY] * nk, out_specs=[_ANY] * nk + [pl.BlockSpec(memory_space=pltpu.VMEM)],
        out_shape=[jax.ShapeDtypeStruct((4,) + s.shape, s.dtype) for s in shards]
        + [jax.ShapeDtypeStruct((8, 128), f32)],
        scratch_shapes=[sem(nk, 3), sem(nk, 3), sem(nk, 3), sem(nk, 3)],
    )(*shards)
    return list(res[:nk]), res[nk]


_HBM = pl.BlockSpec(memory_space=pltpu.HBM)
_SEM = pl.BlockSpec(memory_space=pltpu.SEMAPHORE)


def _ici_half_copies(srcs, lands, send_sems, recv_sems, incoming):
    x, y, c = _place()
    me = 2 * x + y
    out = []
    for k in range(len(srcs)):
        half = srcs[k].shape[0] // 2
        mine = pl.ds(c * half, half)
        for p, (px, py) in enumerate(_peer_chips(x, y)):
            out.append(pltpu.make_async_remote_copy(
                src_ref=srcs[k].at[mine], dst_ref=lands[k].at[(2 * px + py) if incoming else me, mine],
                send_sem=send_sems.at[3 * k + p], recv_sem=recv_sems.at[3 * k + p], device_id=(px, py, c),
                device_id_type=MESH))
    return out


def gather_start(shards, *, name):
    nk = len(shards)

    def body(*refs):
        srcs, lands = refs[:nk], refs[nk:2 * nk]
        send_sems, recv_sems = refs[2 * nk], refs[2 * nk + 1]
        token = refs[-1]
        for outgoing in _ici_half_copies(srcs, lands, send_sems, recv_sems, incoming=False):
            outgoing.start()
        token[...] = jnp.zeros_like(token)

    lands = [pltpu.with_memory_space_constraint(lax.empty((4,) + s.shape, s.dtype), pltpu.HBM) for s in shards]
    res = pl.pallas_call(
        body, name=name,
        out_shape=(pltpu.SemaphoreType.DMA((3 * nk,)), pltpu.SemaphoreType.DMA((3 * nk,)),
                   *[pltpu.HBM(s.shape, s.dtype) for s in shards], *[pltpu.HBM(z.shape, z.dtype) for z in lands],
                   jax.ShapeDtypeStruct((8, 128), f32)),
        in_specs=[_HBM] * (2 * nk), out_specs=(_SEM, _SEM, *[_HBM] * (2 * nk), pl.BlockSpec(memory_space=pltpu.VMEM)),
        input_output_aliases={k: 2 + k for k in range(2 * nk)},
        compiler_params=pltpu.CompilerParams(has_side_effects=pltpu.SideEffectType.DATAFLOW_SIDE_EFFECTING),
    )(*[pltpu.with_memory_space_constraint(s, pltpu.HBM) for s in shards], *lands)
    return res[0], res[1], list(res[2:2 + nk]), list(res[2 + nk:2 + 2 * nk]), res[-1]


def gather_wait(send_sems, recv_sems, shards, lands, after, *, name):
    nk = len(shards)

    def body(*refs):
        srcs, zones = refs[:nk], refs[nk:2 * nk]
        for outgoing in _ici_half_copies(srcs, zones, refs[2 * nk], refs[2 * nk + 1], incoming=False):
            outgoing.wait_send()
        for landed in _ici_half_copies(srcs, zones, refs[2 * nk], refs[2 * nk + 1], incoming=True):
            landed.wait_recv()

    res = pl.pallas_call(
        body, name=name,
        out_shape=(*[pltpu.HBM(s.shape, s.dtype) for s in shards], *[pltpu.HBM(z.shape, z.dtype) for z in lands]),
        in_specs=[_HBM] * (2 * nk) + [_SEM, _SEM, _ANY], out_specs=tuple([_HBM] * (2 * nk)),
        input_output_aliases={k: k for k in range(2 * nk)},
        compiler_params=pltpu.CompilerParams(has_side_effects=pltpu.SideEffectType.DATAFLOW_SIDE_EFFECTING),
    )(*shards, *lands, send_sems, recv_sems, after)
    return list(res[:nk]), list(res[nk:])


def gather_forward(lands, *, name):
    nk = len(lands)

    def body(*refs):
        zones = refs[nk:2 * nk]
        send_sems, recv_sems = refs[2 * nk:]
        x, y, c = _place()
        sends = []
        for k in range(nk):
            half = zones[k].shape[1] // 2
            for p, (px, py) in enumerate(_peer_chips(x, y)):
                landed = zones[k].at[2 * px + py, pl.ds(c * half, half)]
                cp = pltpu.make_async_remote_copy(
                    src_ref=landed, dst_ref=landed, send_sem=send_sems.at[k, p], recv_sem=recv_sems.at[k, p],
                    device_id=(x, y, 1 - c), device_id_type=MESH)
                cp.start()
                sends.append(cp)
        for k in range(nk):
            half = zones[k].shape[1] // 2
            for p, (px, py) in enumerate(_peer_chips(x, y)):
                theirs = zones[k].at[2 * px + py, pl.ds((1 - c) * half, half)]
                pltpu.make_async_remote_copy(
                    src_ref=theirs, dst_ref=theirs, send_sem=send_sems.at[k, p], recv_sem=recv_sems.at[k, p],
                    device_id=(x, y, 1 - c), device_id_type=MESH).wait_recv()
        for cp in sends:
            cp.wait_send()

    return pl.pallas_call(
        body, name=name, in_specs=[_ANY] * nk, out_specs=[_ANY] * nk,
        out_shape=[jax.ShapeDtypeStruct(z.shape, z.dtype) for z in lands],
        input_output_aliases={k: k for k in range(nk)},
        scratch_shapes=[pltpu.SemaphoreType.DMA((nk, 3)), pltpu.SemaphoreType.DMA((nk, 3))],
    )(*lands)


def _to_sibling_copies(srcs, lands, send_sems, recv_sems):
    x, y, c = _place()
    out = []
    for k in range(len(srcs)):
        half = srcs[k].shape[1] // 2
        out.append(pltpu.make_async_remote_copy(
            src_ref=srcs[k].at[:, pl.ds((1 - c) * half, half), :], dst_ref=lands[k], send_sem=send_sems.at[k],
            recv_sem=recv_sems.at[k], device_id=(x, y, 1 - c), device_id_type=MESH))
    return out


def sibling_start(parts, *, name):
    nk = len(parts)

    def body(*refs):
        for cp in _to_sibling_copies(refs[:nk], refs[nk:2 * nk], refs[2 * nk], refs[2 * nk + 1]):
            cp.start()
        refs[-1][...] = jnp.zeros_like(refs[-1])

    lands = [pltpu.with_memory_space_constraint(lax.empty((4, p.shape[1] // 2, p.shape[2]), p.dtype), pltpu.HBM)
             for p in parts]
    res = pl.pallas_call(
        body, name=name,
        out_shape=(pltpu.SemaphoreType.DMA((nk,)), pltpu.SemaphoreType.DMA((nk,)),
                   *[pltpu.HBM(p.shape, p.dtype) for p in parts], *[pltpu.HBM(z.shape, z.dtype) for z in lands],
                   jax.ShapeDtypeStruct((8, 128), f32)),
        in_specs=[_HBM] * (2 * nk), out_specs=(_SEM, _SEM, *[_HBM] * (2 * nk), pl.BlockSpec(memory_space=pltpu.VMEM)),
        input_output_aliases={k: 2 + k for k in range(2 * nk)},
        compiler_params=pltpu.CompilerParams(has_side_effects=pltpu.SideEffectType.DATAFLOW_SIDE_EFFECTING),
    )(*[pltpu.with_memory_space_constraint(p, pltpu.HBM) for p in parts], *lands)
    return res[0], res[1], list(res[2:2 + nk]), list(res[2 + nk:2 + 2 * nk]), res[-1]


def sibling_wait(send_sems, recv_sems, parts, lands, after, *, name):
    nk = len(parts)

    def body(*refs):
        copies = _to_sibling_copies(refs[:nk], refs[nk:2 * nk], refs[2 * nk], refs[2 * nk + 1])
        for cp in copies:
            cp.wait_send()
        for cp in copies:
            cp.wait_recv()

    res = pl.pallas_call(
        body, name=name,
        out_shape=(*[pltpu.HBM(p.shape, p.dtype) for p in parts], *[pltpu.HBM(z.shape, z.dtype) for z in lands]),
        in_specs=[_HBM] * (2 * nk) + [_SEM, _SEM, _ANY], out_specs=tuple([_HBM] * (2 * nk)),
        input_output_aliases={k: k for k in range(2 * nk)},
        compiler_params=pltpu.CompilerParams(has_side_effects=pltpu.SideEffectType.DATAFLOW_SIDE_EFFECTING),
    )(*parts, *lands, send_sems, recv_sems, after)
    return list(res[:nk]), list(res[nk:])


def pair_sum(part, theirs, core, *, name):
    _, rows, cols = part.shape
    half = rows // 2
    tile = _pick(half, max(16, (1 << 20) // (4 * cols) // 16 * 16), 16)
    per = half // tile

    def body(c_ref, p_ref, t_ref, o_ref):
        o_ref[...] = (p_ref[...].astype(f32) + t_ref[...].astype(f32)).astype(bf16)

    grid_spec = pltpu.PrefetchScalarGridSpec(
        num_scalar_prefetch=1, grid=(4, per),
        in_specs=[pl.BlockSpec((1, tile, cols), lambda j, i, c_ref: (j, c_ref[0] * per + i, 0)),
                  pl.BlockSpec((1, tile, cols), lambda j, i, c_ref: (j, i, 0))],
        out_specs=pl.BlockSpec((1, tile, cols), lambda j, i, c_ref: (j, i, 0)))
    return pl.pallas_call(
        body, name=name, grid_spec=grid_spec, out_shape=jax.ShapeDtypeStruct((4, half, cols), bf16),
        compiler_params=_cparams(("parallel", "parallel")),
    )(core, part, theirs)


def _all_to_all_copies(srcs, lands, send_sems, recv_sems, incoming):
    x, y, c = _place()
    me = 2 * x + y
    out = []
    for k in range(len(srcs)):
        for p, (px, py) in enumerate(_peer_chips(x, y)):
            peer = 2 * px + py
            out.append(pltpu.make_async_remote_copy(
                src_ref=srcs[k].at[peer], dst_ref=lands[k].at[peer if incoming else me],
                send_sem=send_sems.at[3 * k + p], recv_sem=recv_sems.at[3 * k + p], device_id=(px, py, c),
                device_id_type=MESH))
    return out


def scatter_start(parts, *, name):
    nk = len(parts)

    def body(*refs):
        srcs, lands = refs[:nk], refs[nk:2 * nk]
        for outgoing in _all_to_all_copies(srcs, lands, refs[2 * nk], refs[2 * nk + 1], incoming=False):
            outgoing.start()
        refs[-1][...] = jnp.zeros_like(refs[-1])

    lands = [pltpu.with_memory_space_constraint(lax.empty(p.shape, p.dtype), pltpu.HBM) for p in parts]
    res = pl.pallas_call(
        body, name=name,
        out_shape=(pltpu.SemaphoreType.DMA((3 * nk,)), pltpu.SemaphoreType.DMA((3 * nk,)),
                   *[pltpu.HBM(p.shape, p.dtype) for p in parts], *[pltpu.HBM(p.shape, p.dtype) for p in parts],
                   jax.ShapeDtypeStruct((8, 128), f32)),
        in_specs=[_HBM] * (2 * nk), out_specs=(_SEM, _SEM, *[_HBM] * (2 * nk), pl.BlockSpec(memory_space=pltpu.VMEM)),
        input_output_aliases={k: 2 + k for k in range(2 * nk)},
        compiler_params=pltpu.CompilerParams(has_side_effects=pltpu.SideEffectType.DATAFLOW_SIDE_EFFECTING),
    )(*[pltpu.with_memory_space_constraint(p, pltpu.HBM) for p in parts], *lands)
    return res[0], res[1], list(res[2:2 + nk]), list(res[2 + nk:2 + 2 * nk]), res[-1]


def scatter_wait(send_sems, recv_sems, parts, lands, after, *, name):
    nk = len(parts)

    def body(*refs):
        srcs, zones = refs[:nk], refs[nk:2 * nk]
        for outgoing in _all_to_all_copies(srcs, zones, refs[2 * nk], refs[2 * nk + 1], incoming=False):
            outgoing.wait_send()
        for landed in _all_to_all_copies(srcs, zones, refs[2 * nk], refs[2 * nk + 1], incoming=True):
            landed.wait_recv()

    res = pl.pallas_call(
        body, name=name,
        out_shape=(*[pltpu.HBM(p.shape, p.dtype) for p in parts], *[pltpu.HBM(z.shape, z.dtype) for z in lands]),
        in_specs=[_HBM] * (2 * nk) + [_SEM, _SEM, _ANY], out_specs=tuple([_HBM] * (2 * nk)),
        input_output_aliases={k: k for k in range(2 * nk)},
        compiler_params=pltpu.CompilerParams(has_side_effects=pltpu.SideEffectType.DATAFLOW_SIDE_EFFECTING),
    )(*parts, *lands, send_sems, recv_sems, after)
    return list(res[:nk]), list(res[nk:])


def join_halves(bufs, layout, *, name):
    nk, nb = len(layout), len(bufs)

    def body(*refs):
        outs = refs[nb:2 * nb]
        send_sems, recv_sems = refs[2 * nb:]
        x, y, c = _place()
        pending = []
        for k, (o, off, rows) in enumerate(layout):
            half = rows // 2
            mine = outs[o].at[pl.ds(off + c * half, half), :]
            cp = pltpu.make_async_remote_copy(
                src_ref=mine, dst_ref=mine, send_sem=send_sems.at[k], recv_sem=recv_sems.at[k],
                device_id=(x, y, 1 - c), device_id_type=MESH)
            cp.start()
            pending.append(cp)
        for k, (o, off, rows) in enumerate(layout):
            half = rows // 2
            theirs = outs[o].at[pl.ds(off + (1 - c) * half, half), :]
            pltpu.make_async_remote_copy(
                src_ref=theirs, dst_ref=theirs, send_sem=send_sems.at[k], recv_sem=recv_sems.at[k],
                device_id=(x, y, 1 - c), device_id_type=MESH).wait_recv()
        for cp in pending:
            cp.wait_send()

    return pl.pallas_call(
        body, name=name, in_specs=[_ANY] * nb, out_specs=[_ANY] * nb,
        out_shape=[jax.ShapeDtypeStruct(b.shape, b.dtype) for b in bufs],
        input_output_aliases={o: o for o in range(nb)},
        scratch_shapes=[pltpu.SemaphoreType.DMA((nk,)), pltpu.SemaphoreType.DMA((nk,))],
    )(*bufs)


def place_slab(dest, src, index, *, name):
    rows, cols = src.shape
    tile = _pick(rows, max(16, (1 << 20) // (src.dtype.itemsize * cols) // 16 * 16), 16)

    def body(i_ref, s_ref, d_ref, o_ref):
        del i_ref, d_ref
        o_ref[0] = s_ref[...]

    grid_spec = pltpu.PrefetchScalarGridSpec(
        num_scalar_prefetch=1, grid=(rows // tile,),
        in_specs=[pl.BlockSpec((tile, cols), lambda i, idx: (i, 0)), _ANY],
        out_specs=pl.BlockSpec((1, tile, cols), lambda i, idx: (idx[0], i, 0)))
    return pl.pallas_call(
        body, name=name, grid_spec=grid_spec, out_shape=jax.ShapeDtypeStruct(dest.shape, dest.dtype),
        input_output_aliases={2: 0}, compiler_params=_cparams(("parallel",)),
    )(index, src, dest)


def _broadcast_copies(src, land, send_sems, recv_sems, incoming):
    x, y, c = _place()
    me = 4 * x + 2 * y + c
    out = []
    for m in range(1, 8):
        px, py, pc = x ^ (m >> 2), y ^ ((m >> 1) & 1), c ^ (m & 1)
        out.append(pltpu.make_async_remote_copy(
            src_ref=src, dst_ref=land.at[(4 * px + 2 * py + pc) if incoming else me], send_sem=send_sems.at[m - 1],
            recv_sem=recv_sems.at[m - 1], device_id=(px, py, pc), device_id_type=MESH))
    return out


def broadcast_start(src, *, name):
    def body(s_ref, l_ref, send_sems, recv_sems, s_thru, l_thru, token):
        for outgoing in _broadcast_copies(s_ref, l_ref, send_sems, recv_sems, incoming=False):
            outgoing.start()
        token[...] = jnp.zeros_like(token)

    land = pltpu.with_memory_space_constraint(lax.empty((8,) + src.shape, src.dtype), pltpu.HBM)
    return pl.pallas_call(
        body, name=name,
        out_shape=(pltpu.SemaphoreType.DMA((7,)), pltpu.SemaphoreType.DMA((7,)), pltpu.HBM(src.shape, src.dtype),
                   pltpu.HBM(land.shape, land.dtype), jax.ShapeDtypeStruct((8, 128), f32)),
        in_specs=[_HBM, _HBM], out_specs=(_SEM, _SEM, _HBM, _HBM, pl.BlockSpec(memory_space=pltpu.VMEM)),
        input_output_aliases={0: 2, 1: 3},
        compiler_params=pltpu.CompilerParams(has_side_effects=pltpu.SideEffectType.DATAFLOW_SIDE_EFFECTING),
    )(pltpu.with_memory_space_constraint(src, pltpu.HBM), land)


def broadcast_wait(send_sems, recv_sems, src, land, after, *, name):
    def body(s_ref, l_ref, send_sems, recv_sems, after_ref, s_out, l_out):
        for outgoing in _broadcast_copies(s_ref, l_ref, send_sems, recv_sems, incoming=False):
            outgoing.wait_send()
        for landed in _broadcast_copies(s_ref, l_ref, send_sems, recv_sems, incoming=True):
            landed.wait_recv()

    return pl.pallas_call(
        body, name=name, out_shape=(pltpu.HBM(src.shape, src.dtype), pltpu.HBM(land.shape, land.dtype)),
        in_specs=[_HBM, _HBM, _SEM, _SEM, _ANY], out_specs=(_HBM, _HBM), input_output_aliases={0: 0, 1: 1},
        compiler_params=pltpu.CompilerParams(has_side_effects=pltpu.SideEffectType.DATAFLOW_SIDE_EFFECTING),
    )(src, land, send_sems, recv_sems, after)


def sum8(land, own, device, *, name):
    _, rows, cols = land.shape

    def body(d_ref, l_ref, o_ref, out_ref):
        mine = o_ref[...]
        acc = jnp.where(d_ref[0] == 0, mine, l_ref[0])
        for d in range(1, 8):
            acc = acc + jnp.where(d_ref[0] == d, mine, l_ref[d])
        out_ref[...] = acc

    grid_spec = pltpu.PrefetchScalarGridSpec(
        num_scalar_prefetch=1, grid=(1,),
        in_specs=[pl.BlockSpec((8, rows, cols), lambda i, d_ref: (0, 0, 0)),
                  pl.BlockSpec((rows, cols), lambda i, d_ref: (0, 0))],
        out_specs=pl.BlockSpec((rows, cols), lambda i, d_ref: (0, 0)))
    return pl.pallas_call(
        body, name=name, grid_spec=grid_spec, out_shape=jax.ShapeDtypeStruct((rows, cols), f32),
        compiler_params=_cparams(("arbitrary",)),
    )(device, land, own)


def sum4_into(arrived, own, dest, where, *, layer, total_rows, name):
    _, rows, cols = arrived.shape
    tile = _pick(rows, max(16, (1 << 20) // (4 * cols) // 16 * 16), 16)
    per = rows // tile

    def body(w_ref, a_ref, own_ref, *rest):
        mine = own_ref[0].astype(f32)
        p = [jnp.where(w_ref[1] == j, mine, a_ref[j].astype(f32)) for j in range(4)]
        rest[-1][...] = ((p[0] + p[1]) + p[2]) + p[3]

    grid_spec = pltpu.PrefetchScalarGridSpec(
        num_scalar_prefetch=1, grid=(per,),
        in_specs=[pl.BlockSpec((4, tile, cols), lambda i, w_ref: (0, i, 0)),
                  pl.BlockSpec((1, tile, cols), lambda i, w_ref: (w_ref[1], i, 0))] + ([] if dest is None else [_ANY]),
        out_specs=pl.BlockSpec((tile, cols), lambda i, w_ref: ((2 * layer + w_ref[0]) * per + i, 0)))
    return pl.pallas_call(
        body, name=name, grid_spec=grid_spec, out_shape=jax.ShapeDtypeStruct((total_rows, cols), f32),
        input_output_aliases={} if dest is None else {3: 0}, compiler_params=_cparams(("parallel",)),
    )(where, arrived, own, *([] if dest is None else [dest]))


def _consts():
    idx = np.arange(RW)
    bd = (idx[:, None] // RN == idx[None, :] // RN).astype(np.float32)
    rot = np.zeros((DQK, DQK), np.float32)
    half = ROPE // 2
    rot[NOPE + half + np.arange(half), NOPE + np.arange(half)] = -1.0
    rot[NOPE + np.arange(half), NOPE + half + np.arange(half)] = 1.0
    return jnp.asarray(bd), jnp.asarray(rot)


def _rope_tables(positions):
    freqs = ROPE_THETA ** (-(jnp.arange(ROPE // 2, dtype=f32) * 2.0 / ROPE))
    ang = positions.astype(f32)[:, None] * freqs
    cos, sin = jnp.cos(ang), jnp.sin(ang)
    ones = jnp.ones((positions.shape[0], NOPE), f32)
    return (jnp.concatenate([ones, cos, cos], axis=-1), jnp.concatenate([0.0 * ones, sin, sin], axis=-1))


STAGES = (("w_in",), ("mla_wq_b", "mla_wkv_b", "mla_w_o", "rwkv_w_o", "conv_w_o", "w_out"), ("w_up", "w_down"))


def derive_stage(stage, w):
    if stage == 0:
        w_in = w["w_in"]
        pad = jnp.zeros((D, MLA_PAD - MLA_COLS), w_in.dtype)
        return dict(gate=w_in[:, :GATE], mla=jnp.concatenate([w_in[:, GATE:GATE + MLA_COLS], pad], axis=1),
                    rw=w_in[:, GATE + MLA_COLS:GATE + MLA_COLS + 4 * RW], cv=w_in[:, GATE + MLA_COLS + 4 * RW:])
    if stage == 1:
        wq = jnp.pad(w["mla_wq_b"].reshape(QL, MLA_H, DQK), ((0, 0), (0, 0), (0, Q_HEAD_PAD - DQK)))
        return dict(wq=wq.reshape(QL, MLA_H * Q_HEAD_PAD), wkv=w["mla_wkv_b"], wo=w["mla_w_o"], rwo=w["rwkv_w_o"],
                    cvo=w["conv_w_o"], out=w["w_out"])
    return dict(up=w["w_up"], down=w["w_down"])


W_IN_WINDOW_TILE = (0, 10, 21, 31)
W_IN_WINDOW = 1664
W_IN_SHARD = 1384


def w_in_window_cols(win, chip):
    gap = MLA_PAD - MLA_COLS
    branches = []
    for j in range(4):
        lo, hi = W_IN_SHARD * j, W_IN_SHARD * (j + 1)
        base = 128 * W_IN_WINDOW_TILE[j]
        cut = GATE + MLA_COLS
        if hi <= cut:
            branches.append(lambda w, a=lo - base: w[:, a:a + W_IN_SHARD])
        elif lo >= cut:
            branches.append(lambda w, a=lo + gap - base: w[:, a:a + W_IN_SHARD])
        else:
            branches.append(lambda w, a=lo - base, n1=cut - lo, b=cut + gap - base, n2=hi - cut:
                            jnp.concatenate([w[:, a:a + n1], w[:, b:b + n2]], axis=1))
    return lax.switch(chip, branches, win)


def chip_major_grads(stage, g):
    if stage == 0:
        padded = jnp.concatenate([g["gate"], g["mla"], g["rw"], g["cv"]], axis=1)
        return dict(w_in=jnp.stack([padded[:, 128 * t:128 * t + W_IN_WINDOW] for t in W_IN_WINDOW_TILE]))
    if stage == 1:
        wq = g["wq"].reshape(QL, MLA_H, Q_HEAD_PAD)[:, :, :DQK].reshape(QL, 4, -1).transpose(1, 0, 2)
        return dict(mla_wq_b=wq, mla_wkv_b=g["wkv"], mla_w_o=g["wo"], rwkv_w_o=g["rwo"], conv_w_o=g["cvo"],
                    w_out=g["out"].reshape(4, D // 4, D))
    return dict(w_up=g["up"], w_down=g["down"].reshape(4, DFF // 4, D))


def _row(v):
    return v.reshape(1, -1)


def local_step(x, positions, target, w, sm, big_of=None, on_grads=None, on_small=None):
    if big_of is None:
        big_of = lambda l, stage, _after: {n: w[n][l] for n in STAGES[stage]}
    if on_grads is None:
        on_grads = lambda l, stage, slabs: None
    if on_small is None:
        on_small = lambda l, layer_small: None
    s_len = x.shape[0]
    t_row = _pick(s_len, 256, 8)
    t_wide = _pick(s_len, 128, 8)
    bd, rot = _consts()
    cos, sin = _rope_tables(positions)
    sds = lambda *shape: jax.ShapeDtypeStruct(shape, f32)
    sdb = lambda *shape: jax.ShapeDtypeStruct(shape, bf16)
    saved = []
    v_first = None
    for l in range(DEPTH):
        tag = f"l{l}_"
        lw = derive_stage(0, big_of(l, 0, x))
        vres = l > 0
        p_norm1 = [_row(sm["attn_norm"][l])]
        (h,) = rows_fwd(_fn_norm, [x], p_norm1, [], [sds(s_len, D)], tile=t_row, name=tag + "norm1")
        gate = mm(h, lw["gate"], name=tag + "proj_gate")
        mla = mm(h, lw["mla"], name=tag + "proj_mla")
        rwc = mm(h, lw["rw"], name=tag + "proj_rwkv")
        cvc = mm(h, lw["cv"], name=tag + "proj_conv")
        lw.update(derive_stage(1, big_of(l, 1, cvc)))
        p_mla = [_row(sm["mla_q_a_norm"][l]), _row(sm["mla_kv_a_norm"][l])]
        qn, kvn, kpe = rows_fwd(_fn_mla_prep, [mla], p_mla, [], [sdb(s_len, QL), sdb(s_len, KVL), sds(s_len, 128)],
                                tile=t_row, name=tag + "mla_prep")
        q_flat = mm(qn, lw["wq"], name=tag + "q_proj")
        kv_flat = mm(kvn, lw["wkv"], name=tag + "kv_proj")
        p_qk = [_row(sm["mla_q_norm"][l]), _row(sm["mla_k_norm"][l])]
        q, k, vv = rows_fwd(_fn_qk_post, [q_flat, kv_flat, kpe, cos, sin], p_qk, [rot],
                            [sds(MLA_H, s_len, DQK), sds(MLA_H, s_len, DQK), sds(MLA_H, s_len, DV)], tile=t_wide,
                            name=tag + "qk_post")
        o = attn_fwd(q, k, vv, tq=_pick(s_len, ATTN_Q_BLOCK, 8), name=tag + "attn")
        o_a = mm(o, lw["wo"], name=tag + "o_a")
        p_rw = [_row(sm["rwkv_mu"][l]), _row(sm["rwkv_w0"][l]), w["rwkv_w2"][l], _row(sm["rwkv_a0"][l]),
                w["rwkv_a2"][l], w["rwkv_g2"][l], _row(sm["rwkv_k_k"][l]), _row(sm["rwkv_k_a"][l])]
        rw_rows, rw_halos = [rwc], (0,)
        if vres:
            p_rw += [w["rwkv_v1"][l - 1], _row(sm["rwkv_v_mu"][l - 1]), _row(sm["rwkv_v0"][l - 1]), w["rwkv_v2"][l - 1]]
            rw_rows, rw_halos = [rwc, h, v_first], (0, 1)
        fn_prep = _make_fn_rwkv_prep(vres)
        r, ld, k2, v, an, bn, g = rows_fwd(fn_prep, rw_rows, p_rw, [bd], [sds(s_len, RW)] * 7, tile=t_row,
                                           name=tag + "rwkv_prep", halos=rw_halos)
        if not vres:
            v_first = v
        y, states = wkv_fwd(r, ld, k2, v, an, bn, name=tag + "wkv")
        p_post = [_row(sm["rwkv_ln_w"][l]), _row(sm["rwkv_ln_b"][l]), _row(sm["rwkv_r_k"][l])]
        (yb,) = rows_fwd(_fn_rwkv_post, [y, r, k2, v, g], p_post, [bd], [sdb(s_len, RW)], tile=t_row,
                         name=tag + "rwkv_post")
        o_b = mm(yb, lw["rwo"], name=tag + "o_b")
        p_cv = [w["conv_w"][l][q:q + 1] for q in range(3)]
        (yc,) = rows_fwd(_fn_conv, [cvc], p_cv, [], [sdb(s_len, CW)], tile=t_row, name=tag + "conv", halos=(0,))
        o_c = mm(yc, lw["cvo"], name=tag + "o_c")
        (merged,) = rows_fwd(_fn_merge, [gate, o_a, o_b, o_c], [], [], [sdb(s_len, D)], tile=t_wide,
                             name=tag + "merge")
        x1 = mm(merged, lw["out"], add=x, name=tag + "out_proj")
        lw.update(derive_stage(2, big_of(l, 2, x1)))
        p_norm2 = [_row(sm["mlp_norm"][l])]
        (h2,) = rows_fwd(_fn_norm, [x1], p_norm2, [], [sdb(s_len, D)], tile=t_row, name=tag + "norm2")
        up, act = mm(h2, lw["up"], relu2_out=True, name=tag + "up")
        x2 = mm(act, lw["down"], add=x1, name=tag + "down")
        saved.append(dict(lw=lw, x=x, h=h, gate=gate, mla=mla, rwc=rwc, cvc=cvc, qn=qn, kvn=kvn, kpe=kpe,
                          q_flat=q_flat, kv_flat=kv_flat, vv=vv, q=q, k=k, o=o, o_a=o_a, r=r, ld=ld, k2=k2, v=v,
                          an=an, bn=bn, g=g, y=y, states=states, yb=yb, o_b=o_b, yc=yc, o_c=o_c, merged=merged,
                          x1=x1, h2=h2, up=up, act=act, p_norm1=p_norm1, p_mla=p_mla, p_qk=p_qk, p_rw=p_rw,
                          p_post=p_post, p_cv=p_cv, p_norm2=p_norm2, rw_rows=rw_rows, rw_halos=rw_halos,
                          fn_prep=fn_prep, v_first=v_first if vres else None))
        x = x2

    loss, dx = loss_head(x, target, tile=t_row, name="loss_head")

    grads = {n: [None] * (DEPTH - 1 if n in ("rwkv_v1", "rwkv_v_mu", "rwkv_v0", "rwkv_v2") else DEPTH)
             for n in WEIGHTS}
    dv_first = None
    for l in reversed(range(DEPTH)):
        tag = f"b{l}_"
        sv = saved[l]
        lw = sv["lw"]
        vres = l > 0
        g_down = mm(sv["act"], dx, ta=True, out_dtype=bf16, name=tag + "g_down")
        dup = mm(dx, lw["down"], tb=True, act_grad=sv["up"], out_dtype=bf16, name=tag + "d_up")
        g_up = mm(sv["h2"], dup, ta=True, n_split=4, out_dtype=bf16, name=tag + "g_up")
        dh2 = mm(dup, lw["up"], tb=True, name=tag + "d_h2")
        slabs = chip_major_grads(2, dict(up=g_up, down=g_down))
        token = on_grads(l, 2, slabs)
        p_norm2 = sv["p_norm2"] if token is None else [sv["p_norm2"][0] + token[0, 0]]
        (dx1,), (g_n2,) = rows_bwd(_fn_norm, [sv["x1"]], p_norm2, [], [[dh2]], tile=t_row,
                                   name=tag + "norm2", extra={0: [dx]})
        g_out = mm(sv["merged"], dx1, ta=True, out_dtype=bf16, name=tag + "g_out")
        dmerged = mm(dx1, lw["out"], tb=True, name=tag + "d_merged")
        (dgate, do_a, do_b, do_c), _ = rows_bwd(_fn_merge, [sv["gate"], sv["o_a"], sv["o_b"], sv["o_c"]], [], [],
                                                [[dmerged]], tile=t_wide, name=tag + "merge",
                                                grad_dtypes=[bf16] * 4)
        g_cvo = mm(sv["yc"], do_c, ta=True, n_split=4, out_dtype=bf16, name=tag + "g_cvo")
        dyc = mm(do_c, lw["cvo"], tb=True, name=tag + "d_yc")
        (dcvc,), g_cw = rows_bwd(_fn_conv, [sv["cvc"]], sv["p_cv"], [], [[dyc]], tile=t_row, name=tag + "conv",
                                    halos=(0,))
        g_rwo = mm(sv["yb"], do_b, ta=True, n_split=4, out_dtype=bf16, name=tag + "g_rwo")
        dyb = mm(do_b, lw["rwo"], tb=True, name=tag + "d_yb")
        (dy, dr_p, dk_p, dv_p, dg), g_post = rows_bwd(
            _fn_rwkv_post, [sv["y"], sv["r"], sv["k2"], sv["v"], sv["g"]], sv["p_post"], [bd], [[dyb]], tile=t_row,
            name=tag + "rwkv_post")
        dr_s, dld, dk_s, dv_s, dan, dbn = wkv_bwd(sv["r"], sv["ld"], sv["k2"], sv["v"], sv["an"], sv["bn"],
                                                  sv["states"], dy, name=tag + "wkv")
        dv_list = [dv_s, dv_p] + ([dv_first] if (not vres and dv_first is not None) else [])
        d_prep, g_prep = rows_bwd(
            sv["fn_prep"], sv["rw_rows"], sv["p_rw"], [bd],
            [[dr_s, dr_p], [dld], [dk_s, dk_p], dv_list, [dan], [dbn], [dg]], tile=t_row, name=tag + "rwkv_prep",
            halos=sv["rw_halos"])
        drwc = d_prep[0]
        dh_extra = []
        if vres:
            dh_extra = [d_prep[1]]
            dv_first = d_prep[2]
        g_wo = mm(sv["o"], do_a, ta=True, n_split=4, out_dtype=bf16, name=tag + "g_wo")
        do = mm(do_a, lw["wo"], tb=True, name=tag + "d_o")
        dq, dk, dvv = attn_bwd(sv["q"], sv["k"], sv["vv"], do, tq=_pick(s_len, ATTN_Q_BLOCK, 8), name=tag + "attn")
        (dq_flat, dkv_flat, dkpe), g_qk = rows_bwd(
            _fn_qk_post, [sv["q_flat"], sv["kv_flat"], sv["kpe"], cos, sin], sv["p_qk"], [rot], [[dq], [dk], [dvv]],
            tile=t_wide, name=tag + "qk_post", grad_rows=[0, 1, 2], grad_dtypes=[bf16, bf16, f32])
        g_wq = mm(sv["qn"], dq_flat, ta=True, out_dtype=bf16, name=tag + "g_wq")
        g_wkv = mm(sv["kvn"], dkv_flat, ta=True, n_split=4, out_dtype=bf16, name=tag + "g_wkv")
        dqn = mm(dq_flat, lw["wq"], tb=True, name=tag + "d_qn")
        dkvn = mm(dkv_flat, lw["wkv"], tb=True, name=tag + "d_kvn")
        slabs.update(chip_major_grads(1, dict(wq=g_wq, wkv=g_wkv, wo=g_wo, rwo=g_rwo, cvo=g_cvo, out=g_out)))
        token = on_grads(l, 1, {n: slabs[n] for n in STAGES[1]})
        p_mla = sv["p_mla"] if token is None else [sv["p_mla"][0] + token[0, 0], sv["p_mla"][1]]
        (dmla,), g_mla = rows_bwd(_fn_mla_prep, [sv["mla"]], p_mla, [], [[dqn], [dkvn], [dkpe]], tile=t_row,
                                  name=tag + "mla_prep", grad_dtypes=[bf16])
        g_gate = mm(sv["h"], dgate, ta=True, out_dtype=bf16, name=tag + "g_gate")
        g_mlaw = mm(sv["h"], dmla, ta=True, out_dtype=bf16, name=tag + "g_mla")
        g_rw = mm(sv["h"], drwc, ta=True, out_dtype=bf16, name=tag + "g_rw")
        g_cv = mm(sv["h"], dcvc, ta=True, out_dtype=bf16, name=tag + "g_cv")
        dh = mm(dgate, lw["gate"], tb=True, name=tag + "d_h_gate")
        dh = mm(dmla, lw["mla"], tb=True, add=dh, name=tag + "d_h_mla")
        dh = mm(drwc, lw["rw"], tb=True, add=dh, name=tag + "d_h_rw")
        dh = mm(dcvc, lw["cv"], tb=True, add=dh, name=tag + "d_h_cv")
        (dx,), (g_n1,) = rows_bwd(_fn_norm, [sv["x"]], sv["p_norm1"], [], [[dh] + dh_extra], tile=t_row,
                                  name=tag + "norm1", extra={0: [dx1]})
        slabs.update(chip_major_grads(0, dict(gate=g_gate, mla=g_mlaw, rw=g_rw, cv=g_cv)))
        token = on_grads(l, 0, {n: slabs[n] for n in STAGES[0]})
        if token is not None and l > 0:
            dx = dx + token[0, 0]
        for n, val in slabs.items():
            grads[n][l] = val
        layer_small = [("attn_norm", l, g_n1), ("mlp_norm", l, g_n2), ("mla_q_a_norm", l, g_mla[0]),
                       ("mla_kv_a_norm", l, g_mla[1]), ("mla_q_norm", l, g_qk[0]), ("mla_k_norm", l, g_qk[1]),
                       ("rwkv_ln_w", l, g_post[0]), ("rwkv_ln_b", l, g_post[1]), ("rwkv_r_k", l, g_post[2]),
                       ("conv_w", l, jnp.concatenate(g_cw, axis=0))]
        layer_small += list(zip(["rwkv_mu", "rwkv_w0", "rwkv_w2", "rwkv_a0", "rwkv_a2", "rwkv_g2", "rwkv_k_k",
                                 "rwkv_k_a"], [l] * 8, g_prep[:8]))
        if vres:
            layer_small += list(zip(["rwkv_v1", "rwkv_v_mu", "rwkv_v0", "rwkv_v2"], [l - 1] * 4, g_prep[8:12]))
        for n, index, val in layer_small:
            grads[n][index] = val
        if l == 0:
            layer_small.append(("loss", 0, loss.reshape(1, 1)))
        token = on_small(l, layer_small)
        if token is not None and l > 0:
            dx = dx + token[0, 0]
    return loss, dx, grads


def _split3(a):
    hi = a.astype(bf16)
    r1 = a - hi.astype(f32)
    mid = r1.astype(bf16)
    lo = (r1 - mid.astype(f32)).astype(bf16)
    return hi, mid, lo


def _shard_axis(name):
    return 1 if name in ROW_SHARDED else 2


def _pack(pieces, width, dtype, row_align):
    flat = jnp.concatenate([p.reshape(-1).astype(dtype) for p in pieces])
    rows = -(-flat.shape[0] // width)
    rows = -(-rows // row_align) * row_align
    return jnp.pad(flat, (0, rows * width - flat.shape[0])).reshape(rows, width)


def _unpack(flat2d, shapes):
    flat = flat2d.reshape(-1)
    out, off = [], 0
    for shp in shapes:
        n = int(np.prod(shp))
        out.append(flat[off:off + n].reshape(shp))
        off += n
    return out


def kernel(x, positions, attn_norm, w_in, mla_q_a_norm, mla_wq_b, mla_kv_a_norm, mla_wkv_b, mla_q_norm, mla_k_norm, mla_w_o, rwkv_mu, rwkv_w0, rwkv_w2, rwkv_a0, rwkv_a2, rwkv_g2, rwkv_k_k, rwkv_k_a, rwkv_r_k, rwkv_ln_w, rwkv_ln_b, rwkv_w_o, rwkv_v1, rwkv_v_mu, rwkv_v0, rwkv_v2, conv_w, conv_w_o, w_out, mlp_norm, w_up, w_down, loss_target, m_attn_norm, m_w_in, m_mla_q_a_norm, m_mla_wq_b, m_mla_kv_a_norm, m_mla_wkv_b, m_mla_q_norm, m_mla_k_norm, m_mla_w_o, m_rwkv_mu, m_rwkv_w0, m_rwkv_w2, m_rwkv_a0, m_rwkv_a2, m_rwkv_g2, m_rwkv_k_k, m_rwkv_k_a, m_rwkv_r_k, m_rwkv_ln_w, m_rwkv_ln_b, m_rwkv_w_o, m_rwkv_v1, m_rwkv_v_mu, m_rwkv_v0, m_rwkv_v2, m_conv_w, m_conv_w_o, m_w_out, m_mlp_norm, m_w_up, m_w_down, v_attn_norm, v_w_in, v_mla_q_a_norm, v_mla_wq_b, v_mla_kv_a_norm, v_mla_wkv_b, v_mla_q_norm, v_mla_k_norm, v_mla_w_o, v_rwkv_mu, v_rwkv_w0, v_rwkv_w2, v_rwkv_a0, v_rwkv_a2, v_rwkv_g2, v_rwkv_k_k, v_rwkv_k_a, v_rwkv_r_k, v_rwkv_ln_w, v_rwkv_ln_b, v_rwkv_w_o, v_rwkv_v1, v_rwkv_v_mu, v_rwkv_v0, v_rwkv_v2, v_conv_w, v_conv_w_o, v_w_out, v_mlp_norm, v_w_up, v_w_down):
    args = dict(locals())
    wts = {n: args[n] for n in WEIGHTS}
    mom = {n: args["m_" + n] for n in WEIGHTS}
    var = {n: args["v_" + n] for n in WEIGHTS}
    chip = 2 * lax.axis_index("x") + lax.axis_index("y")
    core = lax.axis_index("c").astype(jnp.int32).reshape(1)

    med_names = [n for n in MED if n != "conv_w"]
    med_pieces = [wts[n] for n in med_names] + list(_split3(wts["conv_w"]))
    med_shapes = [p.shape for p in med_pieces]
    chip_idx = chip.astype(jnp.int32).reshape(1)
    shards_first = [wts[n][0].astype(bf16) for n in STAGES[0]] + [_pack(med_pieces, 128, bf16, 32)]
    got_first, token = gather_weights(shards_first, name="gather_l0_s0")
    got_first = [place_slab(g, s, chip_idx, name=f"place_own_l0_s0_{q}")
                 for q, (g, s) in enumerate(zip(got_first, shards_first))]
    in_flight = {}
    for key, names, l in (("l0_s1", STAGES[1], 0), ("l0_s2", STAGES[2], 0), ("l1", BIG, 1)):
        group = [wts[n][l].astype(bf16) for n in names]
        group[0] = group[0] + token[0, 0].astype(bf16)
        in_flight[key] = (names, gather_start(group, name="gather_start_" + key))
        token = in_flight[key][1][4]

    def whole_of(names, slabs):
        out = {}
        for n, by_chip in zip(names, slabs):
            _, rows, cols = by_chip.shape
            if n in ROW_SHARDED:
                out[n] = by_chip.reshape(4 * rows, cols)
            else:
                out[n] = by_chip.transpose(1, 0, 2).reshape(rows, 4 * cols)
        return out

    landed = {}

    def big_of(l, stage, after):
        if (l, stage) == (0, 0):
            return whole_of(STAGES[0], got_first)
        key = "l1" if l == 1 else f"l0_s{stage}"
        if key not in landed:
            names, (send_sems, recv_sems, thru, lands, _) = in_flight[key]
            thru, lands = gather_wait(send_sems, recv_sems, thru, lands, after, name="gather_wait_" + key)
            lands = gather_forward(lands, name="gather_forward_" + key)
            landed[key] = whole_of(names, [place_slab(g, s, chip_idx, name=f"place_own_{key}_{q}")
                                           for q, (g, s) in enumerate(zip(lands, thru))])
        return {n: landed[key][n] for n in STAGES[stage]}

    whole = {}
    per_chip = [_unpack(got_first[len(STAGES[0])][j], med_shapes) for j in range(4)]
    for q, n in enumerate(med_names):
        whole[n] = jnp.concatenate([per_chip[j][q] for j in range(4)], axis=_shard_axis(n)).astype(f32)
    base = len(med_names)
    cw_parts = [jnp.concatenate([per_chip[j][base + t] for j in range(4)], axis=2).astype(f32) for t in range(3)]
    whole["conv_w"] = (cw_parts[0] + cw_parts[1]) + cw_parts[2]
    small = {n: wts[n] for n in SMALL}
    small["rwkv_r_k"] = wts["rwkv_r_k"].reshape(DEPTH, RW)

    exchanges, to_sibling = [], []

    def chip_sums_on_their_way(after):
        tokens = []
        while to_sibling:
            l, names, tag, (send_sems, recv_sems, thru, lands, _) = to_sibling.pop(0)
            parts, theirs = sibling_wait(send_sems, recv_sems, thru, lands, after, name="sibling_wait_" + tag)
            chip_sums = [pair_sum(s, t, core, name=f"pair_sum_{n}_{l}") for n, s, t in zip(names, parts, theirs)]
            started = scatter_start(chip_sums, name="scatter_start_" + tag)
            exchanges.append((l, names, tag, started))
            tokens.append(started[4])
        return tokens

    def on_grads(l, stage, slabs):
        names = STAGES[stage]
        tag = f"l{l}_s{stage}"
        parts = [slabs[n] for n in names]
        tokens = chip_sums_on_their_way(parts[0])
        started = sibling_start(parts, name="sibling_start_" + tag)
        to_sibling.append((l, names, tag, started))
        token = started[4]
        for extra in tokens:
            token = token + extra
        return token

    broadcasts = []

    def on_small(l, layer_small):
        values = [val for _, _, val in layer_small]
        started = broadcast_start(_pack(values, 128, f32, 8), name=f"small_start_l{l}")
        broadcasts.append((l, [(n, index, val.shape) for n, index, val in layer_small], started))
        return started[4]

    small["attn_norm"] = small["attn_norm"] + token[0, 0]
    _, grad_x, grads = local_step(x[0], positions[0], loss_target[0], whole, small, big_of, on_grads, on_small)
    chip_sums_on_their_way(grad_x)

    device = (4 * lax.axis_index("x") + 2 * lax.axis_index("y") + lax.axis_index("c")).astype(jnp.int32).reshape(1)
    summed = {n: [None] * len(grads[n]) for n in SMALL + MED}
    summed["loss"] = [None]
    for l, entries, (send_sems, recv_sems, thru, land, _) in broadcasts:
        own, land = broadcast_wait(send_sems, recv_sems, thru, land, grad_x, name=f"small_wait_l{l}")
        total = sum8(land, own, device, name=f"small_sum_l{l}")
        for (n, index, _), val in zip(entries, _unpack(total, [shape for _, _, shape in entries])):
            summed[n][index] = val
    gsum = {}
    for n in SMALL + MED:
        g = jnp.stack(summed[n])
        if n in MED:
            ax = _shard_axis(n)
            width = wts[n].shape[ax]
            g = lax.dynamic_slice_in_dim(g, chip * width, width, axis=ax)
        gsum[n] = g.reshape(wts[n].shape)
    where = jnp.stack([lax.axis_index("c"), chip]).astype(jnp.int32)
    bufs, spans = {}, {n: [] for n in BIG}
    out_g, out_d, out_m, out_v = {}, {}, {}, {}

    def take(groups, after):
        for l, names, tag, (send_sems, recv_sems, thru, lands, _) in groups:
            own, arrived = scatter_wait(send_sems, recv_sems, thru, lands, after, name="scatter_wait_" + tag)
            for n, mine, theirs in zip(names, own, arrived):
                rows = 2 * theirs.shape[1]
                bufs[n] = sum4_into(theirs, mine, bufs.get(n), where, layer=l, total_rows=DEPTH * rows,
                                    name=f"sum_chips_{n}_{l}")
                spans[n].append((l * rows, rows))

    def update(names, tag):
        layout = [(i, off, rows) for i, n in enumerate(names) for off, rows in spans[n]]
        reduced = join_halves([bufs[n] for n in names], layout, name="join_halves_" + tag)
        for n, g2d in zip(names, reduced):
            shp = wts[n].shape
            as2d = lambda a: a.reshape(-1, shp[-1])
            if n == "w_in":
                g2d = w_in_window_cols(g2d, chip)
            res = adamw(as2d(wts[n]), g2d, as2d(mom[n]), as2d(var[n]), name="adamw_" + n)
            out_g[n], out_d[n], out_m[n], out_v[n] = [r.reshape(shp) for r in res]

    others = [n for n in BIG if n not in STAGES[0]]
    take(exchanges[:-1], grad_x)
    update(others, "rest")
    take(exchanges[-1:], out_d[others[-1]])
    update(list(STAGES[0]), "w_in")
    sm_all = SMALL + MED
    flat2d = lambda a: a.reshape(-1, a.shape[-1])
    res = adamw_many([flat2d(wts[n]) for n in sm_all], [flat2d(gsum[n]) for n in sm_all],
                     [flat2d(mom[n]) for n in sm_all], [flat2d(var[n]) for n in sm_all], name="adamw_small")
    for tgt, vals in zip((out_d, out_m, out_v), res):
        for n, val in zip(sm_all, vals):
            tgt[n] = val.reshape(wts[n].shape)
    out_g.update({n: gsum[n] for n in sm_all})
    loss = summed["loss"][0].reshape(())
    return (loss, grad_x[None], *[out_g[n] for n in WEIGHTS], *[out_d[n] for n in WEIGHTS],
            *[out_m[n] for n in WEIGHTS], *[out_v[n] for n in WEIGHTS])
```

```python
import functools

import jax
import jax.numpy as jnp
import numpy as np
from jax import lax
from jax.experimental import pallas as pl
from jax.experimental.pallas import tpu as pltpu

f32, bf16 = jnp.float32, jnp.bfloat16
HI = lax.Precision.HIGHEST
MESH = pl.DeviceIdType.MESH

D = 1024
DEPTH = 2
MLA_H, NOPE, ROPE, DQK, DV = 8, 64, 32, 96, 64
QL, KVL = 384, 256
RW, RH, RN = 256, 4, 64
DL, AL, GL, MVL = 64, 64, 128, 32
CW = 256
DFF = 4096
GATE = 3 * D
MLA_COLS = QL + KVL + ROPE
MLA_PAD = 768
Q_HEAD_PAD = 128
NORM_EPS = 1e-6
GN_EPS = 64e-5
ROPE_THETA = 10000.0
LR, B1, B2, EPS, WD, STEP = 0.001, 0.9, 0.999, 1e-08, 0.01, 10

VMEM_LIMIT = 52 * 1024 * 1024
WKV_CHUNK = 64
WKV_CHUNKS_PER_STEP = 4
ATTN_SEGMENTS = 4
ATTN_Q_BLOCK = 512

BIG = ["w_in", "mla_wq_b", "mla_wkv_b", "mla_w_o", "rwkv_w_o", "conv_w_o", "w_out", "w_up", "w_down"]
MED = ["rwkv_w2", "rwkv_a2", "rwkv_g2", "rwkv_v1", "rwkv_v2", "conv_w"]
ROW_SHARDED = {"w_out", "w_down", "rwkv_v1"}
SMALL = ["attn_norm", "mla_q_a_norm", "mla_kv_a_norm", "mla_q_norm", "mla_k_norm", "rwkv_mu", "rwkv_w0",
         "rwkv_a0", "rwkv_k_k", "rwkv_k_a", "rwkv_r_k", "rwkv_ln_w", "rwkv_ln_b", "rwkv_v_mu", "rwkv_v0",
         "mlp_norm"]
WEIGHTS = ["attn_norm", "w_in", "mla_q_a_norm", "mla_wq_b", "mla_kv_a_norm", "mla_wkv_b", "mla_q_norm",
           "mla_k_norm", "mla_w_o", "rwkv_mu", "rwkv_w0", "rwkv_w2", "rwkv_a0", "rwkv_a2", "rwkv_g2",
           "rwkv_k_k", "rwkv_k_a", "rwkv_r_k", "rwkv_ln_w", "rwkv_ln_b", "rwkv_w_o", "rwkv_v1", "rwkv_v_mu",
           "rwkv_v0", "rwkv_v2", "conv_w", "conv_w_o", "w_out", "mlp_norm", "w_up", "w_down"]


def _cparams(sem=None):
    return pltpu.CompilerParams(dimension_semantics=sem, vmem_limit_bytes=VMEM_LIMIT)


def _pick(dim, pref, align):
    if dim <= pref:
        return dim
    t = (pref // align) * align
    while t >= align:
        if dim % t == 0:
            return t
        t -= align
    return dim


def _bdot(a, b, dims):
    return lax.dot_general(a.astype(bf16), b.astype(bf16), (dims, ((), ())), preferred_element_type=f32)


@jax.custom_vjp
def _mm(a, b):
    return _bdot(a, b, ((1,), (0,)))


def _mm_fwd(a, b):
    return _mm(a, b), (a, b)


def _mm_bwd(res, g):
    a, b = res
    return _bdot(g, b, ((1,), (1,))), _bdot(a, g, ((0,), (0,)))


_mm.defvjp(_mm_fwd, _mm_bwd)


@jax.custom_vjp
def _mm_nt(a, b):
    return _bdot(a, b, ((1,), (1,)))


def _mm_nt_fwd(a, b):
    return _mm_nt(a, b), (a, b)


def _mm_nt_bwd(res, g):
    a, b = res
    return _bdot(g, b, ((1,), (0,))), _bdot(g, a, ((0,), (0,)))


_mm_nt.defvjp(_mm_nt_fwd, _mm_nt_bwd)


_NN, _NT, _TN = ((1,), (0,)), ((1,), (1,)), ((0,), (0,))


def _dg(a, b, dims):
    return lax.dot_general(a, b, (dims, ((), ())), preferred_element_type=f32)


def _bf16_pieces(x, count):
    out, rest = [], x
    for q in range(count):
        piece = rest.astype(bf16)
        out.append(piece)
        if q + 1 < count:
            rest = rest - piece.astype(f32)
    return out


def _dot3(a, b, dims):
    (ah, al), (bh, bl) = _bf16_pieces(a, 2), _bf16_pieces(b, 2)
    return _dg(ah, bh, dims) + (_dg(ah, bl, dims) + _dg(al, bh, dims))


@jax.custom_vjp
def _hdot(a, b):
    return _dot3(a, b, _NN)


@jax.custom_vjp
def _hdot_nt(a, b):
    return _dot3(a, b, _NT)


@jax.custom_vjp
def _hdot_tn(a, b):
    return _dot3(a, b, _TN)


_hdot.defvjp(lambda a, b: (_hdot(a, b), (a, b)), lambda res, g: (_hdot_nt(g, res[1]), _hdot_tn(res[0], g)))
_hdot_nt.defvjp(lambda a, b: (_hdot_nt(a, b), (a, b)), lambda res, g: (_hdot(g, res[1]), _hdot_tn(g, res[0])))
_hdot_tn.defvjp(lambda a, b: (_hdot_tn(a, b), (a, b)), lambda res, g: (_hdot_nt(res[1], g), _hdot(res[0], g)))


_BNN, _BNT, _BTN = ((2,), (1,)), ((2,), (2,)), ((1,), (1,))


def _bdg(a, b, dims):
    return lax.dot_general(a, b, (dims, ((0,), (0,))), preferred_element_type=f32)


def _bdot3(a, b, dims):
    (ah, al), (bh, bl) = _bf16_pieces(a, 2), _bf16_pieces(b, 2)
    return _bdg(ah, bh, dims) + (_bdg(ah, bl, dims) + _bdg(al, bh, dims))


@jax.custom_vjp
def _hbnn(a, b):
    return _bdot3(a, b, _BNN)


@jax.custom_vjp
def _hbnt(a, b):
    return _bdot3(a, b, _BNT)


@jax.custom_vjp
def _hbtn(a, b):
    return _bdot3(a, b, _BTN)


_hbnn.defvjp(lambda a, b: (_hbnn(a, b), (a, b)), lambda res, g: (_hbnt(g, res[1]), _hbtn(res[0], g)))
_hbnt.defvjp(lambda a, b: (_hbnt(a, b), (a, b)), lambda res, g: (_hbnn(g, res[1]), _hbtn(g, res[0])))
_hbtn.defvjp(lambda a, b: (_hbtn(a, b), (a, b)), lambda res, g: (_hbnt(res[1], g), _hbnn(res[0], g)))


@functools.partial(jax.custom_vjp, nondiff_argnums=(2,))
def _exact_bl(m, x, transposed):
    mb = m.astype(bf16)
    hi, mid, lo = _bf16_pieces(x, 3)
    dims = _BTN if transposed else _BNN
    return (_bdg(mb, hi, dims) + _bdg(mb, mid, dims)) + _bdg(mb, lo, dims)


_exact_bl.defvjp(lambda m, x, transposed: (_exact_bl(m, x, transposed), m),
                 lambda transposed, m, g: (jnp.zeros_like(m), _exact_bl(m, g, not transposed)))


@functools.partial(jax.custom_vjp, nondiff_argnums=(2,))
def _exact_l(m, x, transposed):
    mb = m.astype(bf16)
    hi, mid, lo = _bf16_pieces(x, 3)
    dims = _TN if transposed else _NN
    return (_dg(mb, hi, dims) + _dg(mb, mid, dims)) + _dg(mb, lo, dims)


_exact_l.defvjp(lambda m, x, transposed: (_exact_l(m, x, transposed), m),
                lambda transposed, m, g: (jnp.zeros_like(m), _exact_l(m, g, not transposed)))


@functools.partial(jax.custom_vjp, nondiff_argnums=(2,))
def _exact_r(x, m, transposed):
    mb = m.astype(bf16)
    hi, mid, lo = _bf16_pieces(x, 3)
    dims = _NT if transposed else _NN
    return (_dg(hi, mb, dims) + _dg(mid, mb, dims)) + _dg(lo, mb, dims)


_exact_r.defvjp(lambda x, m, transposed: (_exact_r(x, m, transposed), m),
                lambda transposed, m, g: (_exact_r(g, m, not transposed), jnp.zeros_like(m)))


def _rms(x, g, eps=NORM_EPS):
    return x * lax.rsqrt(jnp.mean(x * x, axis=-1, keepdims=True) + eps) * g


def _sigmoid(x):
    return 1.0 / (1.0 + jnp.exp(-x))


def _softplus(x):
    return jnp.maximum(x, 0.0) + jnp.log(1.0 + jnp.exp(-jnp.maximum(x, -x)))


def _lane_split(x, sizes):
    bounds = np.cumsum([0] + list(sizes))

    @jax.custom_vjp
    def split(v):
        return tuple(v[..., int(bounds[q]):int(bounds[q + 1])] for q in range(len(sizes)))

    split.defvjp(lambda v: (split(v), None), lambda _, g: (jnp.concatenate(g, axis=-1),))
    return split(x)


def _row_split(x, sizes):
    bounds = np.cumsum([0] + list(sizes))

    @jax.custom_vjp
    def split(v):
        return tuple(v[..., int(bounds[q]):int(bounds[q + 1]), :] for q in range(len(sizes)))

    split.defvjp(lambda v: (split(v), None), lambda _, g: (jnp.concatenate(g, axis=-2),))
    return split(x)


def _shift_mats(t, k):
    r = lax.broadcasted_iota(jnp.int32, (t, t), 0)
    c = lax.broadcasted_iota(jnp.int32, (t, t), 1)
    inner = (r - c == k).astype(f32)
    r8 = lax.broadcasted_iota(jnp.int32, (t, 8), 0)
    c8 = lax.broadcasted_iota(jnp.int32, (t, 8), 1)
    edge = (c8 - r8 == 8 - k).astype(f32)
    return inner, edge


def _shift(x, halo, k):
    inner, edge = _shift_mats(x.shape[0], k)
    return _exact_l(inner, x, False) + jnp.dot(edge, halo, precision=HI, preferred_element_type=f32)


def mm(a, b, *, name, ta=False, tb=False, a_batched=False, b_batched=False, reduce_batch=False, add=None,
       act_grad=None, relu2_out=False, n_split=1, out_dtype=f32, tm=1024, tn=1024, tk=2048):
    ash, bsh = a.shape[-2:], b.shape[-2:]
    (k_, m_) = ash if ta else ash[::-1]
    (k2_, n_) = bsh[::-1] if tb else bsh
    assert k_ == k2_, (a.shape, b.shape, ta, tb)
    hb = a.shape[0] if a_batched else (b.shape[0] if b_batched else 1)
    batched_out = (a_batched or b_batched) and not reduce_batch
    h_out = hb if batched_out else 1
    h_red = hb if reduce_batch else 1
    tm = _pick(m_, tm, 128)
    tn = _pick(n_ // n_split, tn, 128)
    tk = _pick(k_, tk, 128)
    nm, nn, nk = m_ // tm, n_ // tn, k_ // tk

    def a_map(i, j, ho, hr, kk):
        blk = (kk, i) if ta else (i, kk)
        return ((ho if batched_out else hr),) + blk if a_batched else blk

    def b_map(i, j, ho, hr, kk):
        blk = (j, kk) if tb else (kk, j)
        return ((ho if batched_out else hr),) + blk if b_batched else blk

    a_blk = (tk, tm) if ta else (tm, tk)
    b_blk = (tn, tk) if tb else (tk, tn)
    in_specs = [pl.BlockSpec(((1,) + a_blk) if a_batched else a_blk, a_map),
                pl.BlockSpec(((1,) + b_blk) if b_batched else b_blk, b_map)]
    args = [a, b]
    for extra in (add, act_grad):
        if extra is not None:
            in_specs.append(pl.BlockSpec((tm, tn), lambda i, j, ho, hr, kk: (i, j)))
            args.append(extra)
    if n_split > 1:
        per = n_ // n_split // tn
        if batched_out:
            out_spec = pl.BlockSpec((1, 1, tm, tn), lambda i, j, ho, hr, kk: (j // per, ho, i, j % per))
            out_shape = jax.ShapeDtypeStruct((n_split, hb, m_, n_ // n_split), out_dtype)
        else:
            out_spec = pl.BlockSpec((1, tm, tn), lambda i, j, ho, hr, kk: (j // per, i, j % per))
            out_shape = jax.ShapeDtypeStruct((n_split, m_, n_ // n_split), out_dtype)
    elif batched_out:
        out_spec = pl.BlockSpec((1, tm, tn), lambda i, j, ho, hr, kk: (ho, i, j))
        out_shape = jax.ShapeDtypeStruct((hb, m_, n_), out_dtype)
    else:
        out_spec = pl.BlockSpec((tm, tn), lambda i, j, ho, hr, kk: (i, j))
        out_shape = jax.ShapeDtypeStruct((m_, n_), out_dtype)
    lead = (0,) * (int(batched_out) + int(n_split > 1))
    dims = ((0,) if ta else (1,), (1,) if tb else (0,))
    has_add, has_act = add is not None, act_grad is not None

    def body(*refs):
        a_ref, b_ref = refs[0], refs[1]
        pos = 2
        add_ref = act_ref = None
        if has_add:
            add_ref = refs[pos]
            pos += 1
        if has_act:
            act_ref = refs[pos]
            pos += 1
        o_ref, acc_ref = refs[pos], refs[-1]
        hr, kk = pl.program_id(3), pl.program_id(4)
        first = jnp.logical_and(hr == 0, kk == 0)
        last = jnp.logical_and(hr == h_red - 1, kk == nk - 1)
        av = a_ref[0] if a_batched else a_ref[...]
        bv = b_ref[0] if b_batched else b_ref[...]
        p = _bdot(av, bv, dims)
        single = h_red * nk == 1

        if not single:
            @pl.when(first)
            def _():
                acc_ref[...] = p

            @pl.when(jnp.logical_not(first))
            def _():
                acc_ref[...] += p

        @pl.when(last)
        def _():
            r = p if single else acc_ref[...]
            if has_act:
                r = r * (2.0 * jnp.maximum(act_ref[...], 0.0))
            if has_add:
                r = r + add_ref[...]
            if lead:
                o_ref[lead] = r.astype(out_dtype)
            else:
                o_ref[...] = r.astype(out_dtype)
            if relu2_out:
                refs[pos + 1][...] = jnp.square(jnp.maximum(r, 0.0)).astype(bf16)

    if relu2_out:
        assert not lead
        out_spec = [out_spec, out_spec]
        out_shape = [out_shape, jax.ShapeDtypeStruct(out_shape.shape, bf16)]
    return pl.pallas_call(
        body, name=name, grid=(nm, nn, h_out, h_red, nk), in_specs=in_specs, out_specs=out_spec,
        out_shape=out_shape, scratch_shapes=[pltpu.VMEM((tm, tn), f32)],
        compiler_params=_cparams(("parallel", "parallel", "parallel", "arbitrary", "arbitrary")),
    )(*args)


def _row_spec(arr, tile, idx):
    if arr.ndim == 2:
        return pl.BlockSpec((tile, arr.shape[1]), lambda i: (idx(i), 0))
    return pl.BlockSpec((arr.shape[0], tile, arr.shape[2]), lambda i: (0, idx(i), 0))


def _halo_spec(arr, tile, idx):
    per = tile // 8
    return pl.BlockSpec((8, arr.shape[1]), lambda i: (jnp.maximum(idx(i) * per - 1, 0), 0))


def _full_spec(arr):
    nd = arr.ndim
    return pl.BlockSpec(arr.shape, lambda i: (0,) * nd)


def _load_f32(ref):
    val = ref[...]
    return val.astype(f32) if val.dtype == bf16 else val


def rows_fwd(fn, rows, params, consts, out_shapes, *, tile, name, halos=()):
    s_len = rows[0].shape[-2]
    n = s_len // tile
    nr, nh, npar, nc = len(rows), len(halos), len(params), len(consts)
    ident = lambda i: i
    in_specs = ([_row_spec(r, tile, ident) for r in rows] + [_halo_spec(rows[h], tile, ident) for h in halos]
                + [_full_spec(p) for p in params] + [_full_spec(c) for c in consts])
    out_specs = [_row_spec(o, tile, ident) for o in out_shapes]

    def body(*refs):
        i = pl.program_id(0)
        rv = [_load_f32(r) for r in refs[:nr]]
        keep = (i > 0).astype(f32)
        hv = [r[...] * keep for r in refs[nr:nr + nh]]
        pv = [r[...] for r in refs[nr + nh:nr + nh + npar]]
        cv = [r[...] for r in refs[nr + nh + npar:nr + nh + npar + nc]]
        outs = fn(rv, hv, pv, cv)
        for o_ref, o in zip(refs[nr + nh + npar + nc:], outs):
            o_ref[...] = o.astype(o_ref.dtype)

    return pl.pallas_call(
        body, name=name, grid=(n,), in_specs=in_specs, out_specs=out_specs, out_shape=list(out_shapes),
        compiler_params=_cparams(("arbitrary",)),
    )(*rows, *[rows[h] for h in halos], *params, *consts)


def rows_bwd(fn, rows, params, consts, douts, *, tile, name, halos=(), grad_rows=None, extra=None,
             grad_dtypes=None):
    s_len = rows[0].shape[-2]
    n = s_len // tile
    nr, nh, npar, nc = len(rows), len(halos), len(params), len(consts)
    grad_rows = list(range(nr)) if grad_rows is None else list(grad_rows)
    extra = extra or {}
    assert all(h in grad_rows for h in halos)
    rev = lambda i: n - 1 - i
    dflat = [d for ds in douts for d in ds]
    dcount = [len(ds) for ds in douts]
    eflat = [e for g in grad_rows for e in extra.get(g, [])]
    ecount = [len(extra.get(g, [])) for g in grad_rows]
    in_specs = ([_row_spec(r, tile, rev) for r in rows] + [_halo_spec(rows[h], tile, rev) for h in halos]
                + [_full_spec(p) for p in params] + [_full_spec(c) for c in consts]
                + [_row_spec(d, tile, rev) for d in dflat] + [_row_spec(e, tile, rev) for e in eflat])
    grad_dtypes = [f32] * len(grad_rows) if grad_dtypes is None else list(grad_dtypes)
    assert all(grad_dtypes[q] == f32 for q, g in enumerate(grad_rows) if g in halos)
    out_shapes = ([jax.ShapeDtypeStruct(rows[g].shape, dt) for g, dt in zip(grad_rows, grad_dtypes)]
                  + [jax.ShapeDtypeStruct(p.shape, f32) for p in params])
    out_specs = [_row_spec(rows[g], tile, rev) for g in grad_rows] + [_full_spec(p) for p in params]
    scratch = [pltpu.VMEM((8, rows[h].shape[1]), f32) for h in halos]
    n_in = nr + nh + npar + nc + len(dflat) + len(eflat)
    n_out = len(grad_rows) + npar

    def body(*refs):
        i = pl.program_id(0)
        rv = [_load_f32(r) for r in refs[:nr]]
        keep = (i < n - 1).astype(f32)
        hv = [r[...] * keep for r in refs[nr:nr + nh]]
        pv = [r[...] for r in refs[nr + nh:nr + nh + npar]]
        pos = nr + nh + npar
        cv = [r[...] for r in refs[pos:pos + nc]]
        pos += nc
        dv = []
        for cnt in dcount:
            acc = _load_f32(refs[pos])
            for q in range(1, cnt):
                acc = acc + _load_f32(refs[pos + q])
            dv.append(acc)
            pos += cnt
        ev = []
        for cnt in ecount:
            ev.append([_load_f32(refs[pos + q]) for q in range(cnt)])
            pos += cnt
        out_refs = refs[n_in:n_in + n_out]
        carry_refs = refs[n_in + n_out:]

        def f(gr, gh, gp):
            full = list(rv)
            for g, val in zip(grad_rows, gr):
                full[g] = val
            return tuple(fn(full, gh, gp, cv))

        _, vjp = jax.vjp(f, [rv[g] for g in grad_rows], hv, pv)
        d_rows, d_halos, d_params = vjp(tuple(dv))

        @pl.when(i == 0)
        def _():
            for c_ref in carry_refs:
                c_ref[...] = jnp.zeros_like(c_ref)
            for p_ref in out_refs[len(grad_rows):]:
                p_ref[...] = jnp.zeros_like(p_ref)

        for q, g in enumerate(grad_rows):
            val = d_rows[q]
            for e in ev[q]:
                val = val + e
            out_refs[q][...] = val.astype(out_refs[q].dtype)
            if g in halos:
                hq = list(halos).index(g)
                out_refs[q][tile - 8:tile, :] += carry_refs[hq][...]
                carry_refs[hq][...] = d_halos[hq]
        for p_ref, dp in zip(out_refs[len(grad_rows):], d_params):
            p_ref[...] += dp

    res = pl.pallas_call(
        body, name=name, grid=(n,), in_specs=in_specs, out_specs=out_specs, out_shape=out_shapes,
        scratch_shapes=scratch, compiler_params=_cparams(("arbitrary",)),
    )(*rows, *[rows[h] for h in halos], *params, *consts, *dflat, *eflat)
    return list(res[:len(grad_rows)]), list(res[len(grad_rows):])


def _fn_norm(rows, halos, params, consts):
    return (_rms(rows[0], params[0]),)


def _fn_mla_prep(rows, halos, params, consts):
    cq, ckv, kpe = _lane_split(rows[0], (QL, KVL, MLA_PAD - QL - KVL))
    return _rms(cq, params[0]), _rms(ckv, params[1]), kpe


def _rope(x, cos, sin, rot):
    return x * cos + _exact_r(x, rot, False) * sin


def _fn_qk_post(rows, halos, params, consts):
    q_flat, kv_flat, kpe, cos, sin = rows
    q_norm, k_norm = params
    (rot,) = consts
    q_heads = _lane_split(q_flat, (DQK, Q_HEAD_PAD - DQK) * MLA_H)[::2]
    kv_heads = _lane_split(kv_flat, (NOPE, DV) * MLA_H)
    k_pe, _ = _lane_split(kpe, (ROPE, kpe.shape[1] - ROPE))
    qs = [_rope(_rms(qh, q_norm), cos, sin, rot) for qh in q_heads]
    ks = [_rope(_rms(jnp.concatenate([kv_heads[2 * h], k_pe], axis=-1), k_norm), cos, sin, rot)
          for h in range(MLA_H)]
    vs = [kv_heads[2 * h + 1] for h in range(MLA_H)]
    return jnp.stack(qs, axis=0), jnp.stack(ks, axis=0), jnp.stack(vs, axis=0)


def _seg(x, bd):
    return _exact_r(x, bd, False)


def _make_fn_rwkv_prep(vres):
    def fn(rows, halos, params, consts):
        cols = rows[0]
        bd = consts[0]
        mu, w0, w2, a0, a2, g2, k_k, k_a = params[:8]
        prev = _shift(cols, halos[0], 1)
        c = cols + (prev - cols) * mu
        r, k, v, xw, xa, xg = _lane_split(c, (RW, RW, RW, DL, AL, GL))
        log_w = -_softplus(-(w0 + _mm(jnp.tanh(xw), w2))) - 0.5
        ld = -jnp.exp(log_w)
        a = _sigmoid(a0 + _mm(xa, a2))
        g = _mm(_sigmoid(xg), g2)
        if vres:
            hcur, v_first = rows[1], rows[2]
            v1, v_mu, v0, v2 = params[8:12]
            xv = _mm(hcur, v1)
            xv_prev = _shift(xv, _mm(halos[1], v1), 1)
            xv = xv + (xv_prev - xv) * v_mu
            v = v + (v_first - v) * _sigmoid(v0 + _mm(xv, v2))
        kk = k * k_k
        kk = kk / jnp.maximum(jnp.sqrt(_seg(kk * kk, bd)), 1e-12)
        k2 = k * (1.0 + (a - 1.0) * k_a)
        return r, ld, k2, v, -kk, kk * a, g
    return fn


def _fn_rwkv_post(rows, halos, params, consts):
    y, r, k2, v, g = rows
    ln_w, ln_b, r_k = params
    bd = consts[0]
    mean = _seg(y, bd) * (1.0 / RN)
    d = y - mean
    var = _seg(d * d, bd) * (1.0 / RN)
    yn = d * lax.rsqrt(var + GN_EPS) * ln_w + ln_b
    bonus = _seg(r * k2 * r_k, bd) * v
    return ((yn + bonus) * g,)


def _fn_conv(rows, halos, params, consts):
    cols, halo = rows[0], halos[0]
    w0, w1, w2 = params
    b, c, x = _lane_split(cols, (CW, CW, CW))
    _, ch, xh = _lane_split(halo, (CW, CW, CW))
    u, uh = c * x, ch * xh
    return (b * (w0 * _shift(u, uh, 2) + w1 * _shift(u, uh, 1) + w2 * u),)


def _fn_merge(rows, halos, params, consts):
    gate, o_a, o_b, o_c = rows
    g_a, g_b, g_c = _lane_split(gate, (D, D, D))
    return (_sigmoid(g_a) * o_a + _sigmoid(g_b) * o_b + _sigmoid(g_c) * o_c,)


def _attn_block(q, k, v, q0, diagonal_last):
    tq, kend = q.shape[0], k.shape[0]
    s = _mm_nt(q, k) * (DQK ** -0.5)
    if diagonal_last:
        tri = lax.broadcasted_iota(jnp.int32, (tq, tq), 0) >= lax.broadcasted_iota(jnp.int32, (tq, tq), 1)
        if kend > tq:
            before, diag = _lane_split(s, (kend - tq, tq))
            s = jnp.concatenate([before, jnp.where(tri, diag, -1e30)], axis=-1)
        else:
            s = jnp.where(tri, s, -1e30)
    else:
        row = q0 + lax.broadcasted_iota(jnp.int32, (tq, kend), 0)
        col = lax.broadcasted_iota(jnp.int32, (tq, kend), 1)
        s = jnp.where(row >= col, s, -1e30)
    m = lax.stop_gradient(jnp.max(s, axis=-1, keepdims=True))
    e = jnp.exp(s - m)
    p = e / jnp.sum(e, axis=-1, keepdims=True)
    return _mm(p, v)


def _attn_segments(s_len, tq):
    per = max(1, s_len // tq // ATTN_SEGMENTS)
    return [(first, per, (first + per) * tq) for first in range(0, s_len // tq, per)]


HEAD_PAIR = 2


def attn_fwd(q, k, v, *, tq, name):
    h, s_len, _ = q.shape
    outs = []
    for seg, (first, nq, kend) in enumerate(_attn_segments(s_len, tq)):
        def body(q_ref, k_ref, v_ref, o_ref, first=first, nq=nq):
            q0 = (first + pl.program_id(1)) * tq
            o = [_attn_block(q_ref[j], k_ref[j], v_ref[j], q0, nq == 1) for j in range(HEAD_PAIR)]
            o_ref[...] = jnp.concatenate(o, axis=-1).astype(o_ref.dtype)

        outs.append(pl.pallas_call(
            body, name=f"{name}_{seg}", grid=(h // HEAD_PAIR, nq),
            in_specs=[pl.BlockSpec((HEAD_PAIR, tq, DQK), lambda hp, i, first=first: (hp, first + i, 0)),
                      pl.BlockSpec((HEAD_PAIR, kend, DQK), lambda hp, i: (hp, 0, 0)),
                      pl.BlockSpec((HEAD_PAIR, kend, DV), lambda hp, i: (hp, 0, 0))],
            out_specs=pl.BlockSpec((tq, HEAD_PAIR * DV), lambda hp, i: (i, hp)),
            out_shape=jax.ShapeDtypeStruct((nq * tq, h * DV), bf16),
            compiler_params=_cparams(("parallel", "arbitrary")),
        )(q, k, v))
    return jnp.concatenate(outs, axis=0)


def attn_bwd(q, k, v, do, *, tq, name):
    h, s_len, _ = q.shape
    dqs, dk_acc, dv_acc = [], None, None
    for seg, (first, nq, kend) in reversed(list(enumerate(_attn_segments(s_len, tq)))):
        carried = dk_acc is not None

        def body(*refs, first=first, carried=carried, nq=nq):
            q_ref, k_ref, v_ref, do_ref = refs[:4]
            dq_ref, dk_ref, dv_ref = refs[-3:]
            i = pl.program_id(1)
            do_heads = _lane_split(do_ref[...], (DV,) * HEAD_PAIR)
            for j in range(HEAD_PAIR):
                _, vjp = jax.vjp(functools.partial(_attn_block, q0=(first + i) * tq, diagonal_last=nq == 1),
                                 q_ref[j], k_ref[j], v_ref[j])
                dq, dk, dv = vjp(do_heads[j])
                dq_ref[j] = dq

                @pl.when(i == 0)
                def _():
                    dk_ref[j] = dk + refs[4][j] if carried else dk
                    dv_ref[j] = dv + refs[5][j] if carried else dv

                @pl.when(i > 0)
                def _():
                    dk_ref[j] += dk
                    dv_ref[j] += dv

        key_specs = [pl.BlockSpec((HEAD_PAIR, kend, DQK), lambda hp, i: (hp, 0, 0)),
                     pl.BlockSpec((HEAD_PAIR, kend, DV), lambda hp, i: (hp, 0, 0))]
        dq, dk_acc, dv_acc = pl.pallas_call(
            body, name=f"{name}_{seg}", grid=(h // HEAD_PAIR, nq),
            in_specs=[pl.BlockSpec((HEAD_PAIR, tq, DQK), lambda hp, i, first=first: (hp, first + i, 0))] + key_specs
            + [pl.BlockSpec((tq, HEAD_PAIR * DV), lambda hp, i, first=first: (first + i, hp))]
            + (key_specs if carried else []),
            out_specs=[pl.BlockSpec((HEAD_PAIR, tq, DQK), lambda hp, i: (hp, i, 0))] + key_specs,
            out_shape=[jax.ShapeDtypeStruct((h, nq * tq, DQK), f32), jax.ShapeDtypeStruct((h, s_len, DQK), f32),
                       jax.ShapeDtypeStruct((h, s_len, DV), f32)],
            input_output_aliases={4: 1, 5: 2} if carried else {},
            compiler_params=_cparams(("parallel", "arbitrary")),
        )(q, k, v, do, *([dk_acc, dv_acc] if carried else []))
        dqs.append(dq)
    return jnp.concatenate(dqs[::-1], axis=1), dk_acc, dv_acc


def _wkv_local(r, ld, k, v, a, b):
    nb, c, n = r.shape
    ri = lax.broadcasted_iota(jnp.int32, (c, c), 0)
    ci = lax.broadcasted_iota(jnp.int32, (c, c), 1)
    tri = jnp.broadcast_to((ri >= ci).astype(f32)[None], (nb, c, c))
    cum = _exact_bl(tri, ld, False)
    tot = jnp.sum(ld, axis=1, keepdims=True)
    w_incl, w_excl, w_inv, w_rest = jnp.exp(cum), jnp.exp(cum - ld), jnp.exp(-cum), jnp.exp(tot - cum)
    ab, rb, bb, kb = a * w_excl, r * w_incl, b * w_inv, k * w_inv
    bw, kw = b * w_rest, k * w_rest
    r2 = lax.broadcasted_iota(jnp.int32, (2 * c, 2 * c), 0)
    c2 = lax.broadcasted_iota(jnp.int32, (2 * c, 2 * c), 1)
    t_of, s_of = jnp.where(r2 >= c, r2 - c, r2), jnp.where(c2 >= c, c2 - c, c2)
    keep = jnp.logical_or(t_of > s_of, jnp.logical_and(r2 >= c, t_of == s_of))
    pair = jnp.where(keep[None], _hbnt(jnp.concatenate([ab, rb], axis=1), jnp.concatenate([bb, kb], axis=1)), 0.0)
    on_b, on_k = _lane_split(pair, (c, c))
    l_ab, m_rb = _row_split(on_b, (c, c))
    l_ak_v, m_rk_v = _row_split(_hbnn(on_k, v), (c, c))
    x = jnp.concatenate([ab, l_ak_v], axis=-1)
    lp, span = l_ab, 1
    while span < c:
        x = x + _hbnn(lp, x)
        span *= 2
        if span < c:
            lp = _hbnn(lp, lp)
    via_b_r, via_b_y = _lane_split(_hbnn(m_rb, x), (n, n))
    r_hat = rb + via_b_r
    y0 = via_b_y + m_rk_v
    from_b_g, from_b_z = _row_split(_hbtn(x, bw), (n, n))
    eye = lax.broadcasted_iota(jnp.int32, (n, n), 0) == lax.broadcasted_iota(jnp.int32, (n, n), 1)
    g = jnp.where(eye[None], jnp.exp(tot), 0.0) + from_b_g
    z = from_b_z + _hbtn(v, kw)
    return r_hat, y0, g, z


def _head(h):
    return slice(RN * h, RN * (h + 1))


def _load_chunk_heads(ref, c, per):
    return jnp.stack([ref[c * q:c * (q + 1), _head(h)] for q in range(per) for h in range(RH)], axis=0)


def _store_chunk_heads(ref, val, c, per):
    for q in range(per):
        ref[c * q:c * (q + 1), :] = jnp.concatenate([val[q * RH + h] for h in range(RH)], axis=-1)


def wkv_fwd(r, ld, k, v, a, b, *, name):
    s_len = r.shape[0]
    c = WKV_CHUNK
    n = s_len // c
    per = min(WKV_CHUNKS_PER_STEP, n)
    rows = pl.BlockSpec((c * per, RW), lambda i: (i, 0))
    mats = pl.BlockSpec((per, RH, RN, RN), lambda i: (i, 0, 0, 0))
    rows_t, mats_t = jax.ShapeDtypeStruct((s_len, RW), f32), jax.ShapeDtypeStruct((n, RH, RN, RN), f32)

    def local_body(r_ref, ld_ref, k_ref, v_ref, a_ref, b_ref, rh_ref, y0_ref, g_ref, z_ref):
        r_hat, y0, g, z = _wkv_local(*[_load_chunk_heads(ref, c, per)
                                       for ref in (r_ref, ld_ref, k_ref, v_ref, a_ref, b_ref)])
        _store_chunk_heads(rh_ref, r_hat, c, per)
        _store_chunk_heads(y0_ref, y0, c, per)
        g_ref[...] = g.reshape(per, RH, RN, RN)
        z_ref[...] = z.reshape(per, RH, RN, RN)

    r_hat, y0, g, z = pl.pallas_call(
        local_body, name=name + "_local", grid=(n // per,), in_specs=[rows] * 6, out_specs=[rows, rows, mats, mats],
        out_shape=[rows_t, rows_t, mats_t, mats_t], compiler_params=_cparams(("parallel",)),
    )(r, ld, k, v, a, b)

    def scan_body(g_ref, z_ref, st_ref, s_sc):
        s_sc[...] = jnp.zeros_like(s_sc)

        @pl.loop(0, n)
        def _(i):
            s0 = s_sc[...]
            st_ref[i] = s0
            s_sc[...] = _hbnn(s0, g_ref[i]) + z_ref[i]

    vm = pl.BlockSpec(memory_space=pltpu.VMEM)
    states = pl.pallas_call(
        scan_body, name=name + "_scan", in_specs=[vm, vm], out_specs=vm, out_shape=mats_t,
        scratch_shapes=[pltpu.VMEM((RH, RN, RN), f32)],
        compiler_params=pltpu.CompilerParams(vmem_limit_bytes=VMEM_LIMIT),
    )(g, z)

    def out_body(rh_ref, y0_ref, st_ref, y_ref):
        y = _hbnt(_load_chunk_heads(rh_ref, c, per), st_ref[...].reshape(per * RH, RN, RN))
        _store_chunk_heads(y_ref, y, c, per)
        y_ref[...] += y0_ref[...]

    y = pl.pallas_call(
        out_body, name=name + "_out", grid=(n // per,), in_specs=[rows, rows, mats], out_specs=rows,
        out_shape=rows_t, compiler_params=_cparams(("parallel",)),
    )(r_hat, y0, states)
    return y, dict(r_hat=r_hat, g=g, states=states)


def wkv_bwd(r, ld, k, v, a, b, saved, dy, *, name):
    s_len = r.shape[0]
    c = WKV_CHUNK
    n = s_len // c
    per = min(WKV_CHUNKS_PER_STEP, n)
    rows = pl.BlockSpec((c * per, RW), lambda i: (i, 0))
    mats = pl.BlockSpec((per, RH, RN, RN), lambda i: (i, 0, 0, 0))
    rows_t, mats_t = jax.ShapeDtypeStruct((s_len, RW), f32), jax.ShapeDtypeStruct((n, RH, RN, RN), f32)

    def out_body(dy_ref, rh_ref, st_ref, drh_ref, dsy_ref):
        dyb = _load_chunk_heads(dy_ref, c, per)
        _store_chunk_heads(drh_ref, _hbnn(dyb, st_ref[...].reshape(per * RH, RN, RN)), c, per)
        dsy_ref[...] = _hbtn(dyb, _load_chunk_heads(rh_ref, c, per)).reshape(per, RH, RN, RN)

    d_rhat, ds_y = pl.pallas_call(
        out_body, name=name + "_out", grid=(n // per,), in_specs=[rows, rows, mats], out_specs=[rows, mats],
        out_shape=[rows_t, mats_t], compiler_params=_cparams(("parallel",)),
    )(dy, saved["r_hat"], saved["states"])

    def scan_body(dsy_ref, g_ref, st_ref, dg_ref, dz_ref, ds_sc):
        ds_sc[...] = jnp.zeros_like(ds_sc)

        @pl.loop(0, n)
        def _(i):
            cidx = n - 1 - i
            ds_next = ds_sc[...]
            dz_ref[cidx] = ds_next
            dg_ref[cidx] = _hbtn(st_ref[cidx], ds_next)
            ds_sc[...] = dsy_ref[cidx] + _hbnt(ds_next, g_ref[cidx])

    vm = pl.BlockSpec(memory_space=pltpu.VMEM)
    d_g, d_z = pl.pallas_call(
        scan_body, name=name + "_scan", in_specs=[vm, vm, vm], out_specs=[vm, vm], out_shape=[mats_t, mats_t],
        scratch_shapes=[pltpu.VMEM((RH, RN, RN), f32)],
        compiler_params=pltpu.CompilerParams(vmem_limit_bytes=VMEM_LIMIT),
    )(ds_y, saved["g"], saved["states"])

    def local_body(r_ref, ld_ref, k_ref, v_ref, a_ref, b_ref, drh_ref, dy_ref, dg_ref, dz_ref, *out_refs):
        _, vjp = jax.vjp(_wkv_local, *[_load_chunk_heads(ref, c, per)
                                       for ref in (r_ref, ld_ref, k_ref, v_ref, a_ref, b_ref)])
        grads = vjp((_load_chunk_heads(drh_ref, c, per), _load_chunk_heads(dy_ref, c, per),
                     dg_ref[...].reshape(per * RH, RN, RN), dz_ref[...].reshape(per * RH, RN, RN)))
        for o_ref, val in zip(out_refs, grads):
            _store_chunk_heads(o_ref, val, c, per)

    return pl.pallas_call(
        local_body, name=name + "_local", grid=(n // per,), in_specs=[rows] * 8 + [mats, mats], out_specs=[rows] * 6,
        out_shape=[rows_t] * 6, compiler_params=_cparams(("parallel",)),
    )(r, ld, k, v, a, b, d_rhat, dy, d_g, d_z)


def loss_head(y, target, *, tile, name):
    s_len, d = y.shape
    n = s_len // tile

    def body(y_ref, t_ref, dy_ref, l_ref):
        err = y_ref[...] - t_ref[...]
        dy_ref[...] = err * (1.0 / d)
        part = 0.5 * jnp.sum(jnp.mean(err * err, axis=-1, keepdims=True), axis=0, keepdims=True)

        @pl.when(pl.program_id(0) == 0)
        def _():
            l_ref[...] = jnp.zeros_like(l_ref)

        l_ref[...] += jnp.broadcast_to(part, l_ref.shape)

    bs = pl.BlockSpec((tile, d), lambda i: (i, 0))
    dy, l = pl.pallas_call(
        body, name=name, grid=(n,), in_specs=[bs, bs],
        out_specs=[bs, pl.BlockSpec((8, 128), lambda i: (0, 0))],
        out_shape=[jax.ShapeDtypeStruct((s_len, d), f32), jax.ShapeDtypeStruct((8, 128), f32)],
        compiler_params=_cparams(("arbitrary",)),
    )(y, target)
    return l[0, 0], dy


def _adamw_update(w, g, m, v):
    mn = B1 * m + (1.0 - B1) * g
    vn = B2 * v + (1.0 - B2) * (g * g)
    delta = -LR * ((mn / (1.0 - B1 ** STEP)) / (jnp.sqrt(vn / (1.0 - B2 ** STEP)) + EPS) + WD * w)
    return delta, mn, vn


def adamw(w, g, m, v, *, name):
    rows, cols = w.shape
    tile = _pick(rows, max(8, (2 * 1024 * 1024 // (4 * cols)) // 8 * 8), 8)

    def body(w_ref, g_ref, m_ref, v_ref, g_out, d_out, m_out, v_out):
        gv = g_ref[...]
        d_out[...], m_out[...], v_out[...] = _adamw_update(w_ref[...], gv, m_ref[...], v_ref[...])
        g_out[...] = gv

    bs = pl.BlockSpec((tile, cols), lambda i: (i, 0))
    return pl.pallas_call(
        body, name=name, grid=(rows // tile,), in_specs=[bs] * 4, out_specs=[bs] * 4,
        out_shape=[jax.ShapeDtypeStruct((rows, cols), f32)] * 4, compiler_params=_cparams(("parallel",)),
    )(w, g, m, v)


def adamw_many(ws, gs, ms, vs, *, name):
    n = len(ws)

    def body(*refs):
        ins, outs = refs[:4 * n], refs[4 * n:]
        for i in range(n):
            outs[i][...], outs[n + i][...], outs[2 * n + i][...] = _adamw_update(
                ins[i][...], ins[n + i][...], ins[2 * n + i][...], ins[3 * n + i][...])

    vm = pl.BlockSpec(memory_space=pltpu.VMEM)
    res = pl.pallas_call(
        body, name=name, in_specs=[vm] * (4 * n), out_specs=[vm] * (3 * n),
        out_shape=[jax.ShapeDtypeStruct(w.shape, f32) for w in ws] * 3,
        compiler_params=pltpu.CompilerParams(vmem_limit_bytes=VMEM_LIMIT),
    )(*ws, *gs, *ms, *vs)
    return list(res[:n]), list(res[n:2 * n]), list(res[2 * n:])


def _place():
    return lax.axis_index("x"), lax.axis_index("y"), lax.axis_index("c")


_ANY = pl.BlockSpec(memory_space=pl.ANY)


def _peer_chips(x, y):
    return [(1 - x, y), (x, 1 - y), (1 - x, 1 - y)]


def gather_weights(shards, *, name):
    nk = len(shards)

    def body(*refs):
        srcs, outs = refs[:nk], refs[nk:2 * nk]
        ici_send, ici_recv, d2d_send, d2d_recv = refs[2 * nk + 1:]
        x, y, c = _place()
        me = 2 * x + y
        peers = _peer_chips(x, y)
        pending = []
        for k in range(nk):
            half = srcs[k].shape[0] // 2
            mine = pl.ds(c * half, half)
            for p, (px, py) in enumerate(peers):
                cp = pltpu.make_async_remote_copy(
                    src_ref=srcs[k].at[mine], dst_ref=outs[k].at[me, mine], send_sem=ici_send.at[k, p],
                    recv_sem=ici_recv.at[k, p], device_id=(px, py, c), device_id_type=MESH)
                cp.start()
                pending.append(cp)
        for k in range(nk):
            half = srcs[k].shape[0] // 2
            mine = pl.ds(c * half, half)
            for p, (px, py) in enumerate(peers):
                landed = outs[k].at[2 * px + py, mine]
                pltpu.make_async_remote_copy(
                    src_ref=srcs[k].at[mine], dst_ref=landed, send_sem=ici_send.at[k, p], recv_sem=ici_recv.at[k, p],
                    device_id=(px, py, c), device_id_type=MESH).wait_recv()
                fwd = pltpu.make_async_remote_copy(
                    src_ref=landed, dst_ref=landed, send_sem=d2d_send.at[k, p], recv_sem=d2d_recv.at[k, p],
                    device_id=(x, y, 1 - c), device_id_type=MESH)
                fwd.start()
                pending.append(fwd)
        for k in range(nk):
            half = srcs[k].shape[0] // 2
            other = pl.ds((1 - c) * half, half)
            for p, (px, py) in enumerate(peers):
                theirs = outs[k].at[2 * px + py, other]
                pltpu.make_async_remote_copy(
                    src_ref=theirs, dst_ref=theirs, send_sem=d2d_send.at[k, p], recv_sem=d2d_recv.at[k, p],
                    device_id=(x, y, 1 - c), device_id_type=MESH).wait_recv()
        for cp in pending:
            cp.wait_send()
        refs[2 * nk][...] = jnp.zeros_like(refs[2 * nk])

    sem = lambda *shape: pltpu.SemaphoreType.DMA(shape)
    res = pl.pallas_call(
        body, name=name, in_specs=[_ANY] * nk, out_specs=[_ANY] * nk + [pl.BlockSpec(memory_space=pltpu.VMEM)],
        out_shape=[jax.ShapeDtypeStruct((4,) + s.shape, s.dtype) for s in shards]
        + [jax.ShapeDtypeStruct((8, 128), f32)],
        scratch_shapes=[sem(nk, 3), sem(nk, 3), sem(nk, 3), sem(nk, 3)],
    )(*shards)
    return list(res[:nk]), res[nk]


_HBM = pl.BlockSpec(memory_space=pltpu.HBM)
_SEM = pl.BlockSpec(memory_space=pltpu.SEMAPHORE)


def _ici_half_copies(srcs, lands, send_sems, recv_sems, incoming):
    x, y, c = _place()
    me = 2 * x + y
    out = []
    for k in range(len(srcs)):
        half = srcs[k].shape[0] // 2
        mine = pl.ds(c * half, half)
        for p, (px, py) in enumerate(_peer_chips(x, y)):
            out.append(pltpu.make_async_remote_copy(
                src_ref=srcs[k].at[mine], dst_ref=lands[k].at[(2 * px + py) if incoming else me, mine],
                send_sem=send_sems.at[3 * k + p], recv_sem=recv_sems.at[3 * k + p], device_id=(px, py, c),
                device_id_type=MESH))
    return out


def gather_start(shards, *, name):
    nk = len(shards)

    def body(*refs):
        srcs, lands = refs[:nk], refs[nk:2 * nk]
        send_sems, recv_sems = refs[2 * nk], refs[2 * nk + 1]
        token = refs[-1]
        for outgoing in _ici_half_copies(srcs, lands, send_sems, recv_sems, incoming=False):
            outgoing.start()
        token[...] = jnp.zeros_like(token)

    lands = [pltpu.with_memory_space_constraint(lax.empty((4,) + s.shape, s.dtype), pltpu.HBM) for s in shards]
    res = pl.pallas_call(
        body, name=name,
        out_shape=(pltpu.SemaphoreType.DMA((3 * nk,)), pltpu.SemaphoreType.DMA((3 * nk,)),
                   *[pltpu.HBM(s.shape, s.dtype) for s in shards], *[pltpu.HBM(z.shape, z.dtype) for z in lands],
                   jax.ShapeDtypeStruct((8, 128), f32)),
        in_specs=[_HBM] * (2 * nk), out_specs=(_SEM, _SEM, *[_HBM] * (2 * nk), pl.BlockSpec(memory_space=pltpu.VMEM)),
        input_output_aliases={k: 2 + k for k in range(2 * nk)},
        compiler_params=pltpu.CompilerParams(has_side_effects=pltpu.SideEffectType.DATAFLOW_SIDE_EFFECTING),
    )(*[pltpu.with_memory_space_constraint(s, pltpu.HBM) for s in shards], *lands)
    return res[0], res[1], list(res[2:2 + nk]), list(res[2 + nk:2 + 2 * nk]), res[-1]


def gather_wait(send_sems, recv_sems, shards, lands, after, *, name):
    nk = len(shards)

    def body(*refs):
        srcs, zones = refs[:nk], refs[nk:2 * nk]
        for outgoing in _ici_half_copies(srcs, zones, refs[2 * nk], refs[2 * nk + 1], incoming=False):
            outgoing.wait_send()
        for landed in _ici_half_copies(srcs, zones, refs[2 * nk], refs[2 * nk + 1], incoming=True):
            landed.wait_recv()

    res = pl.pallas_call(
        body, name=name,
        out_shape=(*[pltpu.HBM(s.shape, s.dtype) for s in shards], *[pltpu.HBM(z.shape, z.dtype) for z in lands]),
        in_specs=[_HBM] * (2 * nk) + [_SEM, _SEM, _ANY], out_specs=tuple([_HBM] * (2 * nk)),
        input_output_aliases={k: k for k in range(2 * nk)},
        compiler_params=pltpu.CompilerParams(has_side_effects=pltpu.SideEffectType.DATAFLOW_SIDE_EFFECTING),
    )(*shards, *lands, send_sems, recv_sems, after)
    return list(res[:nk]), list(res[nk:])


def gather_forward(lands, *, name):
    nk = len(lands)

    def body(*refs):
        zones = refs[nk:2 * nk]
        send_sems, recv_sems = refs[2 * nk:]
        x, y, c = _place()
        sends = []
        for k in range(nk):
            half = zones[k].shape[1] // 2
            for p, (px, py) in enumerate(_peer_chips(x, y)):
                landed = zones[k].at[2 * px + py, pl.ds(c * half, half)]
                cp = pltpu.make_async_remote_copy(
                    src_ref=landed, dst_ref=landed, send_sem=send_sems.at[k, p], recv_sem=recv_sems.at[k, p],
                    device_id=(x, y, 1 - c), device_id_type=MESH)
                cp.start()
                sends.append(cp)
        for k in range(nk):
            half = zones[k].shape[1] // 2
            for p, (px, py) in enumerate(_peer_chips(x, y)):
                theirs = zones[k].at[2 * px + py, pl.ds((1 - c) * half, half)]
                pltpu.make_async_remote_copy(
                    src_ref=theirs, dst_ref=theirs, send_sem=send_sems.at[k, p], recv_sem=recv_sems.at[k, p],
                    device_id=(x, y, 1 - c), device_id_type=MESH).wait_recv()
        for cp in sends:
            cp.wait_send()

    return pl.pallas_call(
        body, name=name, in_specs=[_ANY] * nk, out_specs=[_ANY] * nk,
        out_shape=[jax.ShapeDtypeStruct(z.shape, z.dtype) for z in lands],
        input_output_aliases={k: k for k in range(nk)},
        scratch_shapes=[pltpu.SemaphoreType.DMA((nk, 3)), pltpu.SemaphoreType.DMA((nk, 3))],
    )(*lands)


def _to_sibling_copies(srcs, lands, send_sems, recv_sems):
    x, y, c = _place()
    out = []
    for k in range(len(srcs)):
        half = srcs[k].shape[1] // 2
        out.append(pltpu.make_async_remote_copy(
            src_ref=srcs[k].at[:, pl.ds((1 - c) * half, half), :], dst_ref=lands[k], send_sem=send_sems.at[k],
            recv_sem=recv_sems.at[k], device_id=(x, y, 1 - c), device_id_type=MESH))
    return out


def sibling_start(parts, *, name):
    nk = len(parts)

    def body(*refs):
        for cp in _to_sibling_copies(refs[:nk], refs[nk:2 * nk], refs[2 * nk], refs[2 * nk + 1]):
            cp.start()
        refs[-1][...] = jnp.zeros_like(refs[-1])

    lands = [pltpu.with_memory_space_constraint(lax.empty((4, p.shape[1] // 2, p.shape[2]), p.dtype), pltpu.HBM)
             for p in parts]
    res = pl.pallas_call(
        body, name=name,
        out_shape=(pltpu.SemaphoreType.DMA((nk,)), pltpu.SemaphoreType.DMA((nk,)),
                   *[pltpu.HBM(p.shape, p.dtype) for p in parts], *[pltpu.HBM(z.shape, z.dtype) for z in lands],
                   jax.ShapeDtypeStruct((8, 128), f32)),
        in_specs=[_HBM] * (2 * nk), out_specs=(_SEM, _SEM, *[_HBM] * (2 * nk), pl.BlockSpec(memory_space=pltpu.VMEM)),
        input_output_aliases={k: 2 + k for k in range(2 * nk)},
        compiler_params=pltpu.CompilerParams(has_side_effects=pltpu.SideEffectType.DATAFLOW_SIDE_EFFECTING),
    )(*[pltpu.with_memory_space_constraint(p, pltpu.HBM) for p in parts], *lands)
    return res[0], res[1], list(res[2:2 + nk]), list(res[2 + nk:2 + 2 * nk]), res[-1]


def sibling_wait(send_sems, recv_sems, parts, lands, after, *, name):
    nk = len(parts)

    def body(*refs):
        copies = _to_sibling_copies(refs[:nk], refs[nk:2 * nk], refs[2 * nk], refs[2 * nk + 1])
        for cp in copies:
            cp.wait_send()
        for cp in copies:
            cp.wait_recv()

    res = pl.pallas_call(
        body, name=name,
        out_shape=(*[pltpu.HBM(p.shape, p.dtype) for p in parts], *[pltpu.HBM(z.shape, z.dtype) for z in lands]),
        in_specs=[_HBM] * (2 * nk) + [_SEM, _SEM, _ANY], out_specs=tuple([_HBM] * (2 * nk)),
        input_output_aliases={k: k for k in range(2 * nk)},
        compiler_params=pltpu.CompilerParams(has_side_effects=pltpu.SideEffectType.DATAFLOW_SIDE_EFFECTING),
    )(*parts, *lands, send_sems, recv_sems, after)
    return list(res[:nk]), list(res[nk:])


def pair_sum(part, theirs, core, *, name):
    _, rows, cols = part.shape
    half = rows // 2
    tile = _pick(half, max(16, (1 << 20) // (4 * cols) // 16 * 16), 16)
    per = half // tile

    def body(c_ref, p_ref, t_ref, o_ref):
        o_ref[...] = (p_ref[...].astype(f32) + t_ref[...].astype(f32)).astype(bf16)

    grid_spec = pltpu.PrefetchScalarGridSpec(
        num_scalar_prefetch=1, grid=(4, per),
        in_specs=[pl.BlockSpec((1, tile, cols), lambda j, i, c_ref: (j, c_ref[0] * per + i, 0)),
                  pl.BlockSpec((1, tile, cols), lambda j, i, c_ref: (j, i, 0))],
        out_specs=pl.BlockSpec((1, tile, cols), lambda j, i, c_ref: (j, i, 0)))
    return pl.pallas_call(
        body, name=name, grid_spec=grid_spec, out_shape=jax.ShapeDtypeStruct((4, half, cols), bf16),
        compiler_params=_cparams(("parallel", "parallel")),
    )(core, part, theirs)


def _all_to_all_copies(srcs, lands, send_sems, recv_sems, incoming):
    x, y, c = _place()
    me = 2 * x + y
    out = []
    for k in range(len(srcs)):
        for p, (px, py) in enumerate(_peer_chips(x, y)):
            peer = 2 * px + py
            out.append(pltpu.make_async_remote_copy(
                src_ref=srcs[k].at[peer], dst_ref=lands[k].at[peer if incoming else me],
                send_sem=send_sems.at[3 * k + p], recv_sem=recv_sems.at[3 * k + p], device_id=(px, py, c),
                device_id_type=MESH))
    return out


def scatter_start(parts, *, name):
    nk = len(parts)

    def body(*refs):
        srcs, lands = refs[:nk], refs[nk:2 * nk]
        for outgoing in _all_to_all_copies(srcs, lands, refs[2 * nk], refs[2 * nk + 1], incoming=False):
            outgoing.start()
        refs[-1][...] = jnp.zeros_like(refs[-1])

    lands = [pltpu.with_memory_space_constraint(lax.empty(p.shape, p.dtype), pltpu.HBM) for p in parts]
    res = pl.pallas_call(
        body, name=name,
        out_shape=(pltpu.SemaphoreType.DMA((3 * nk,)), pltpu.SemaphoreType.DMA((3 * nk,)),
                   *[pltpu.HBM(p.shape, p.dtype) for p in parts], *[pltpu.HBM(p.shape, p.dtype) for p in parts],
                   jax.ShapeDtypeStruct((8, 128), f32)),
        in_specs=[_HBM] * (2 * nk), out_specs=(_SEM, _SEM, *[_HBM] * (2 * nk), pl.BlockSpec(memory_space=pltpu.VMEM)),
        input_output_aliases={k: 2 + k for k in range(2 * nk)},
        compiler_params=pltpu.CompilerParams(has_side_effects=pltpu.SideEffectType.DATAFLOW_SIDE_EFFECTING),
    )(*[pltpu.with_memory_space_constraint(p, pltpu.HBM) for p in parts], *lands)
    return res[0], res[1], list(res[2:2 + nk]), list(res[2 + nk:2 + 2 * nk]), res[-1]


def scatter_wait(send_sems, recv_sems, parts, lands, after, *, name):
    nk = len(parts)

    def body(*refs):
        srcs, zones = refs[:nk], refs[nk:2 * nk]
        for outgoing in _all_to_all_copies(srcs, zones, refs[2 * nk], refs[2 * nk + 1], incoming=False):
            outgoing.wait_send()
        for landed in _all_to_all_copies(srcs, zones, refs[2 * nk], refs[2 * nk + 1], incoming=True):
            landed.wait_recv()

    res = pl.pallas_call(
        body, name=name,
        out_shape=(*[pltpu.HBM(p.shape, p.dtype) for p in parts], *[pltpu.HBM(z.shape, z.dtype) for z in lands]),
        in_specs=[_HBM] * (2 * nk) + [_SEM, _SEM, _ANY], out_specs=tuple([_HBM] * (2 * nk)),
        input_output_aliases={k: k for k in range(2 * nk)},
        compiler_params=pltpu.CompilerParams(has_side_effects=pltpu.SideEffectType.DATAFLOW_SIDE_EFFECTING),
    )(*parts, *lands, send_sems, recv_sems, after)
    return list(res[:nk]), list(res[nk:])


def join_halves(bufs, layout, *, name):
    nk, nb = len(layout), len(bufs)

    def body(*refs):
        outs = refs[nb:2 * nb]
        send_sems, recv_sems = refs[2 * nb:]
        x, y, c = _place()
        pending = []
        for k, (o, off, rows) in enumerate(layout):
            half = rows // 2
            mine = outs[o].at[pl.ds(off + c * half, half), :]
            cp = pltpu.make_async_remote_copy(
                src_ref=mine, dst_ref=mine, send_sem=send_sems.at[k], recv_sem=recv_sems.at[k],
                device_id=(x, y, 1 - c), device_id_type=MESH)
            cp.start()
            pending.append(cp)
        for k, (o, off, rows) in enumerate(layout):
            half = rows // 2
            theirs = outs[o].at[pl.ds(off + (1 - c) * half, half), :]
            pltpu.make_async_remote_copy(
                src_ref=theirs, dst_ref=theirs, send_sem=send_sems.at[k], recv_sem=recv_sems.at[k],
                device_id=(x, y, 1 - c), device_id_type=MESH).wait_recv()
        for cp in pending:
            cp.wait_send()

    return pl.pallas_call(
        body, name=name, in_specs=[_ANY] * nb, out_specs=[_ANY] * nb,
        out_shape=[jax.ShapeDtypeStruct(b.shape, b.dtype) for b in bufs],
        input_output_aliases={o: o for o in range(nb)},
        scratch_shapes=[pltpu.SemaphoreType.DMA((nk,)), pltpu.SemaphoreType.DMA((nk,))],
    )(*bufs)


def place_slab(dest, src, index, *, name):
    rows, cols = src.shape
    tile = _pick(rows, max(16, (1 << 20) // (src.dtype.itemsize * cols) // 16 * 16), 16)

    def body(i_ref, s_ref, d_ref, o_ref):
        del i_ref, d_ref
        o_ref[0] = s_ref[...]

    grid_spec = pltpu.PrefetchScalarGridSpec(
        num_scalar_prefetch=1, grid=(rows // tile,),
        in_specs=[pl.BlockSpec((tile, cols), lambda i, idx: (i, 0)), _ANY],
        out_specs=pl.BlockSpec((1, tile, cols), lambda i, idx: (idx[0], i, 0)))
    return pl.pallas_call(
        body, name=name, grid_spec=grid_spec, out_shape=jax.ShapeDtypeStruct(dest.shape, dest.dtype),
        input_output_aliases={2: 0}, compiler_params=_cparams(("parallel",)),
    )(index, src, dest)


def _broadcast_copies(src, land, send_sems, recv_sems, incoming):
    x, y, c = _place()
    me = 4 * x + 2 * y + c
    out = []
    for m in range(1, 8):
        px, py, pc = x ^ (m >> 2), y ^ ((m >> 1) & 1), c ^ (m & 1)
        out.append(pltpu.make_async_remote_copy(
            src_ref=src, dst_ref=land.at[(4 * px + 2 * py + pc) if incoming else me], send_sem=send_sems.at[m - 1],
            recv_sem=recv_sems.at[m - 1], device_id=(px, py, pc), device_id_type=MESH))
    return out


def broadcast_start(src, *, name):
    def body(s_ref, l_ref, send_sems, recv_sems, s_thru, l_thru, token):
        for outgoing in _broadcast_copies(s_ref, l_ref, send_sems, recv_sems, incoming=False):
            outgoing.start()
        token[...] = jnp.zeros_like(token)

    land = pltpu.with_memory_space_constraint(lax.empty((8,) + src.shape, src.dtype), pltpu.HBM)
    return pl.pallas_call(
        body, name=name,
        out_shape=(pltpu.SemaphoreType.DMA((7,)), pltpu.SemaphoreType.DMA((7,)), pltpu.HBM(src.shape, src.dtype),
                   pltpu.HBM(land.shape, land.dtype), jax.ShapeDtypeStruct((8, 128), f32)),
        in_specs=[_HBM, _HBM], out_specs=(_SEM, _SEM, _HBM, _HBM, pl.BlockSpec(memory_space=pltpu.VMEM)),
        input_output_aliases={0: 2, 1: 3},
        compiler_params=pltpu.CompilerParams(has_side_effects=pltpu.SideEffectType.DATAFLOW_SIDE_EFFECTING),
    )(pltpu.with_memory_space_constraint(src, pltpu.HBM), land)


def broadcast_wait(send_sems, recv_sems, src, land, after, *, name):
    def body(s_ref, l_ref, send_sems, recv_sems, after_ref, s_out, l_out):
        for outgoing in _broadcast_copies(s_ref, l_ref, send_sems, recv_sems, incoming=False):
            outgoing.wait_send()
        for landed in _broadcast_copies(s_ref, l_ref, send_sems, recv_sems, incoming=True):
            landed.wait_recv()

    return pl.pallas_call(
        body, name=name, out_shape=(pltpu.HBM(src.shape, src.dtype), pltpu.HBM(land.shape, land.dtype)),
        in_specs=[_HBM, _HBM, _SEM, _SEM, _ANY], out_specs=(_HBM, _HBM), input_output_aliases={0: 0, 1: 1},
        compiler_params=pltpu.CompilerParams(has_side_effects=pltpu.SideEffectType.DATAFLOW_SIDE_EFFECTING),
    )(src, land, send_sems, recv_sems, after)


def sum8(land, own, device, *, name):
    _, rows, cols = land.shape

    def body(d_ref, l_ref, o_ref, out_ref):
        mine = o_ref[...]
        acc = jnp.where(d_ref[0] == 0, mine, l_ref[0])
        for d in range(1, 8):
            acc = acc + jnp.where(d_ref[0] == d, mine, l_ref[d])
        out_ref[...] = acc

    grid_spec = pltpu.PrefetchScalarGridSpec(
        num_scalar_prefetch=1, grid=(1,),
        in_specs=[pl.BlockSpec((8, rows, cols), lambda i, d_ref: (0, 0, 0)),
                  pl.BlockSpec((rows, cols), lambda i, d_ref: (0, 0))],
        out_specs=pl.BlockSpec((rows, cols), lambda i, d_ref: (0, 0)))
    return pl.pallas_call(
        body, name=name, grid_spec=grid_spec, out_shape=jax.ShapeDtypeStruct((rows, cols), f32),
        compiler_params=_cparams(("arbitrary",)),
    )(device, land, own)


def sum4_into(arrived, own, dest, where, *, layer, total_rows, name):
    _, rows, cols = arrived.shape
    tile = _pick(rows, max(16, (1 << 20) // (4 * cols) // 16 * 16), 16)
    per = rows // tile

    def body(w_ref, a_ref, own_ref, *rest):
        mine = own_ref[0].astype(f32)
        p = [jnp.where(w_ref[1] == j, mine, a_ref[j].astype(f32)) for j in range(4)]
        rest[-1][...] = ((p[0] + p[1]) + p[2]) + p[3]

    grid_spec = pltpu.PrefetchScalarGridSpec(
        num_scalar_prefetch=1, grid=(per,),
        in_specs=[pl.BlockSpec((4, tile, cols), lambda i, w_ref: (0, i, 0)),
                  pl.BlockSpec((1, tile, cols), lambda i, w_ref: (w_ref[1], i, 0))] + ([] if dest is None else [_ANY]),
        out_specs=pl.BlockSpec((tile, cols), lambda i, w_ref: ((2 * layer + w_ref[0]) * per + i, 0)))
    return pl.pallas_call(
        body, name=name, grid_spec=grid_spec, out_shape=jax.ShapeDtypeStruct((total_rows, cols), f32),
        input_output_aliases={} if dest is None else {3: 0}, compiler_params=_cparams(("parallel",)),
    )(where, arrived, own, *([] if dest is None else [dest]))


def _consts():
    idx = np.arange(RW)
    bd = (idx[:, None] // RN == idx[None, :] // RN).astype(np.float32)
    rot = np.zeros((DQK, DQK), np.float32)
    half = ROPE // 2
    rot[NOPE + half + np.arange(half), NOPE + np.arange(half)] = -1.0
    rot[NOPE + np.arange(half), NOPE + half + np.arange(half)] = 1.0
    return jnp.asarray(bd), jnp.asarray(rot)


def _rope_tables(positions):
    freqs = ROPE_THETA ** (-(jnp.arange(ROPE // 2, dtype=f32) * 2.0 / ROPE))
    ang = positions.astype(f32)[:, None] * freqs
    cos, sin = jnp.cos(ang), jnp.sin(ang)
    ones = jnp.ones((positions.shape[0], NOPE), f32)
    return (jnp.concatenate([ones, cos, cos], axis=-1), jnp.concatenate([0.0 * ones, sin, sin], axis=-1))


STAGES = (("w_in",), ("mla_wq_b", "mla_wkv_b", "mla_w_o", "rwkv_w_o", "conv_w_o", "w_out"), ("w_up", "w_down"))


def derive_stage(stage, w):
    if stage == 0:
        w_in = w["w_in"]
        pad = jnp.zeros((D, MLA_PAD - MLA_COLS), w_in.dtype)
        return dict(gate=w_in[:, :GATE], mla=jnp.concatenate([w_in[:, GATE:GATE + MLA_COLS], pad], axis=1),
                    rw=w_in[:, GATE + MLA_COLS:GATE + MLA_COLS + 4 * RW], cv=w_in[:, GATE + MLA_COLS + 4 * RW:])
    if stage == 1:
        wq = jnp.pad(w["mla_wq_b"].reshape(QL, MLA_H, DQK), ((0, 0), (0, 0), (0, Q_HEAD_PAD - DQK)))
        return dict(wq=wq.reshape(QL, MLA_H * Q_HEAD_PAD), wkv=w["mla_wkv_b"], wo=w["mla_w_o"], rwo=w["rwkv_w_o"],
                    cvo=w["conv_w_o"], out=w["w_out"])
    return dict(up=w["w_up"], down=w["w_down"])


W_IN_WINDOW_TILE = (0, 10, 21, 31)
W_IN_WINDOW = 1664
W_IN_SHARD = 1384


def w_in_window_cols(win, chip):
    gap = MLA_PAD - MLA_COLS
    branches = []
    for j in range(4):
        lo, hi = W_IN_SHARD * j, W_IN_SHARD * (j + 1)
        base = 128 * W_IN_WINDOW_TILE[j]
        cut = GATE + MLA_COLS
        if hi <= cut:
            branches.append(lambda w, a=lo - base: w[:, a:a + W_IN_SHARD])
        elif lo >= cut:
            branches.append(lambda w, a=lo + gap - base: w[:, a:a + W_IN_SHARD])
        else:
            branches.append(lambda w, a=lo - base, n1=cut - lo, b=cut + gap - base, n2=hi - cut:
                            jnp.concatenate([w[:, a:a + n1], w[:, b:b + n2]], axis=1))
    return lax.switch(chip, branches, win)


def chip_major_grads(stage, g):
    if stage == 0:
        padded = jnp.concatenate([g["gate"], g["mla"], g["rw"], g["cv"]], axis=1)
        return dict(w_in=jnp.stack([padded[:, 128 * t:128 * t + W_IN_WINDOW] for t in W_IN_WINDOW_TILE]))
    if stage == 1:
        wq = g["wq"].reshape(QL, MLA_H, Q_HEAD_PAD)[:, :, :DQK].reshape(QL, 4, -1).transpose(1, 0, 2)
        return dict(mla_wq_b=wq, mla_wkv_b=g["wkv"], mla_w_o=g["wo"], rwkv_w_o=g["rwo"], conv_w_o=g["cvo"],
                    w_out=g["out"].reshape(4, D // 4, D))
    return dict(w_up=g["up"], w_down=g["down"].reshape(4, DFF // 4, D))


def _row(v):
    return v.reshape(1, -1)


def local_step(x, positions, target, w, sm, big_of=None, on_grads=None, on_small=None):
    if big_of is None:
        big_of = lambda l, stage, _after: {n: w[n][l] for n in STAGES[stage]}
    if on_grads is None:
        on_grads = lambda l, stage, slabs: None
    if on_small is None:
        on_small = lambda l, layer_small: None
    s_len = x.shape[0]
    t_row = _pick(s_len, 256, 8)
    t_wide = _pick(s_len, 256, 8)
    bd, rot = _consts()
    cos, sin = _rope_tables(positions)
    sds = lambda *shape: jax.ShapeDtypeStruct(shape, f32)
    sdb = lambda *shape: jax.ShapeDtypeStruct(shape, bf16)
    saved = []
    v_first = None
    for l in range(DEPTH):
        tag = f"l{l}_"
        lw = derive_stage(0, big_of(l, 0, x))
        vres = l > 0
        p_norm1 = [_row(sm["attn_norm"][l])]
        (h,) = rows_fwd(_fn_norm, [x], p_norm1, [], [sds(s_len, D)], tile=t_row, name=tag + "norm1")
        gate = mm(h, lw["gate"], name=tag + "proj_gate")
        mla = mm(h, lw["mla"], name=tag + "proj_mla")
        rwc = mm(h, lw["rw"], name=tag + "proj_rwkv")
        cvc = mm(h, lw["cv"], name=tag + "proj_conv")
        lw.update(derive_stage(1, big_of(l, 1, cvc)))
        p_mla = [_row(sm["mla_q_a_norm"][l]), _row(sm["mla_kv_a_norm"][l])]
        qn, kvn, kpe = rows_fwd(_fn_mla_prep, [mla], p_mla, [], [sdb(s_len, QL), sdb(s_len, KVL), sds(s_len, 128)],
                                tile=t_row, name=tag + "mla_prep")
        q_flat = mm(qn, lw["wq"], name=tag + "q_proj")
        kv_flat = mm(kvn, lw["wkv"], name=tag + "kv_proj")
        p_qk = [_row(sm["mla_q_norm"][l]), _row(sm["mla_k_norm"][l])]
        q, k, vv = rows_fwd(_fn_qk_post, [q_flat, kv_flat, kpe, cos, sin], p_qk, [rot],
                            [sds(MLA_H, s_len, DQK), sds(MLA_H, s_len, DQK), sds(MLA_H, s_len, DV)], tile=t_wide,
                            name=tag + "qk_post")
        o = attn_fwd(q, k, vv, tq=_pick(s_len, ATTN_Q_BLOCK, 8), name=tag + "attn")
        o_a = mm(o, lw["wo"], name=tag + "o_a")
        p_rw = [_row(sm["rwkv_mu"][l]), _row(sm["rwkv_w0"][l]), w["rwkv_w2"][l], _row(sm["rwkv_a0"][l]),
                w["rwkv_a2"][l], w["rwkv_g2"][l], _row(sm["rwkv_k_k"][l]), _row(sm["rwkv_k_a"][l])]
        rw_rows, rw_halos = [rwc], (0,)
        if vres:
            p_rw += [w["rwkv_v1"][l - 1], _row(sm["rwkv_v_mu"][l - 1]), _row(sm["rwkv_v0"][l - 1]), w["rwkv_v2"][l - 1]]
            rw_rows, rw_halos = [rwc, h, v_first], (0, 1)
        fn_prep = _make_fn_rwkv_prep(vres)
        r, ld, k2, v, an, bn, g = rows_fwd(fn_prep, rw_rows, p_rw, [bd], [sds(s_len, RW)] * 7, tile=t_row,
                                           name=tag + "rwkv_prep", halos=rw_halos)
        if not vres:
            v_first = v
        y, states = wkv_fwd(r, ld, k2, v, an, bn, name=tag + "wkv")
        p_post = [_row(sm["rwkv_ln_w"][l]), _row(sm["rwkv_ln_b"][l]), _row(sm["rwkv_r_k"][l])]
        (yb,) = rows_fwd(_fn_rwkv_post, [y, r, k2, v, g], p_post, [bd], [sdb(s_len, RW)], tile=t_row,
                         name=tag + "rwkv_post")
        o_b = mm(yb, lw["rwo"], name=tag + "o_b")
        p_cv = [w["conv_w"][l][q:q + 1] for q in range(3)]
        (yc,) = rows_fwd(_fn_conv, [cvc], p_cv, [], [sdb(s_len, CW)], tile=t_row, name=tag + "conv", halos=(0,))
        o_c = mm(yc, lw["cvo"], name=tag + "o_c")
        (merged,) = rows_fwd(_fn_merge, [gate, o_a, o_b, o_c], [], [], [sdb(s_len, D)], tile=t_wide,
                             name=tag + "merge")
        x1 = mm(merged, lw["out"], add=x, name=tag + "out_proj")
        lw.update(derive_stage(2, big_of(l, 2, x1)))
        p_norm2 = [_row(sm["mlp_norm"][l])]
        (h2,) = rows_fwd(_fn_norm, [x1], p_norm2, [], [sdb(s_len, D)], tile=t_row, name=tag + "norm2")
        up, act = mm(h2, lw["up"], relu2_out=True, name=tag + "up")
        x2 = mm(act, lw["down"], add=x1, name=tag + "down")
        saved.append(dict(lw=lw, x=x, h=h, gate=gate, mla=mla, rwc=rwc, cvc=cvc, qn=qn, kvn=kvn, kpe=kpe,
                          q_flat=q_flat, kv_flat=kv_flat, vv=vv, q=q, k=k, o=o, o_a=o_a, r=r, ld=ld, k2=k2, v=v,
                          an=an, bn=bn, g=g, y=y, states=states, yb=yb, o_b=o_b, yc=yc, o_c=o_c, merged=merged,
                          x1=x1, h2=h2, up=up, act=act, p_norm1=p_norm1, p_mla=p_mla, p_qk=p_qk, p_rw=p_rw,
                          p_post=p_post, p_cv=p_cv, p_norm2=p_norm2, rw_rows=rw_rows, rw_halos=rw_halos,
                          fn_prep=fn_prep, v_first=v_first if vres else None))
        x = x2

    loss, dx = loss_head(x, target, tile=t_row, name="loss_head")

    grads = {n: [None] * (DEPTH - 1 if n in ("rwkv_v1", "rwkv_v_mu", "rwkv_v0", "rwkv_v2") else DEPTH)
             for n in WEIGHTS}
    dv_first = None
    for l in reversed(range(DEPTH)):
        tag = f"b{l}_"
        sv = saved[l]
        lw = sv["lw"]
        vres = l > 0
        g_down = mm(sv["act"], dx, ta=True, out_dtype=bf16, name=tag + "g_down")
        dup = mm(dx, lw["down"], tb=True, act_grad=sv["up"], out_dtype=bf16, name=tag + "d_up")
        g_up = mm(sv["h2"], dup, ta=True, n_split=4, out_dtype=bf16, name=tag + "g_up")
        dh2 = mm(dup, lw["up"], tb=True, name=tag + "d_h2")
        slabs = chip_major_grads(2, dict(up=g_up, down=g_down))
        token = on_grads(l, 2, slabs)
        p_norm2 = sv["p_norm2"] if token is None else [sv["p_norm2"][0] + token[0, 0]]
        (dx1,), (g_n2,) = rows_bwd(_fn_norm, [sv["x1"]], p_norm2, [], [[dh2]], tile=t_row,
                                   name=tag + "norm2", extra={0: [dx]})
        g_out = mm(sv["merged"], dx1, ta=True, out_dtype=bf16, name=tag + "g_out")
        dmerged = mm(dx1, lw["out"], tb=True, name=tag + "d_merged")
        (dgate, do_a, do_b, do_c), _ = rows_bwd(_fn_merge, [sv["gate"], sv["o_a"], sv["o_b"], sv["o_c"]], [], [],
                                                [[dmerged]], tile=t_wide, name=tag + "merge",
                                                grad_dtypes=[bf16] * 4)
        g_cvo = mm(sv["yc"], do_c, ta=True, n_split=4, out_dtype=bf16, name=tag + "g_cvo")
        dyc = mm(do_c, lw["cvo"], tb=True, name=tag + "d_yc")
        (dcvc,), g_cw = rows_bwd(_fn_conv, [sv["cvc"]], sv["p_cv"], [], [[dyc]], tile=t_row, name=tag + "conv",
                                    halos=(0,))
        g_rwo = mm(sv["yb"], do_b, ta=True, n_split=4, out_dtype=bf16, name=tag + "g_rwo")
        dyb = mm(do_b, lw["rwo"], tb=True, name=tag + "d_yb")
        (dy, dr_p, dk_p, dv_p, dg), g_post = rows_bwd(
            _fn_rwkv_post, [sv["y"], sv["r"], sv["k2"], sv["v"], sv["g"]], sv["p_post"], [bd], [[dyb]], tile=t_row,
            name=tag + "rwkv_post")
        dr_s, dld, dk_s, dv_s, dan, dbn = wkv_bwd(sv["r"], sv["ld"], sv["k2"], sv["v"], sv["an"], sv["bn"],
                                                  sv["states"], dy, name=tag + "wkv")
        dv_list = [dv_s, dv_p] + ([dv_first] if (not vres and dv_first is not None) else [])
        d_prep, g_prep = rows_bwd(
            sv["fn_prep"], sv["rw_rows"], sv["p_rw"], [bd],
            [[dr_s, dr_p], [dld], [dk_s, dk_p], dv_list, [dan], [dbn], [dg]], tile=t_row, name=tag + "rwkv_prep",
            halos=sv["rw_halos"])
        drwc = d_prep[0]
        dh_extra = []
        if vres:
            dh_extra = [d_prep[1]]
            dv_first = d_prep[2]
        g_wo = mm(sv["o"], do_a, ta=True, n_split=4, out_dtype=bf16, name=tag + "g_wo")
        do = mm(do_a, lw["wo"], tb=True, name=tag + "d_o")
        dq, dk, dvv = attn_bwd(sv["q"], sv["k"], sv["vv"], do, tq=_pick(s_len, ATTN_Q_BLOCK, 8), name=tag + "attn")
        (dq_flat, dkv_flat, dkpe), g_qk = rows_bwd(
            _fn_qk_post, [sv["q_flat"], sv["kv_flat"], sv["kpe"], cos, sin], sv["p_qk"], [rot], [[dq], [dk], [dvv]],
            tile=t_wide, name=tag + "qk_post", grad_rows=[0, 1, 2], grad_dtypes=[bf16, bf16, f32])
        g_wq = mm(sv["qn"], dq_flat, ta=True, out_dtype=bf16, name=tag + "g_wq")
        g_wkv = mm(sv["kvn"], dkv_flat, ta=True, n_split=4, out_dtype=bf16, name=tag + "g_wkv")
        dqn = mm(dq_flat, lw["wq"], tb=True, name=tag + "d_qn")
        dkvn = mm(dkv_flat, lw["wkv"], tb=True, name=tag + "d_kvn")
        slabs.update(chip_major_grads(1, dict(wq=g_wq, wkv=g_wkv, wo=g_wo, rwo=g_rwo, cvo=g_cvo, out=g_out)))
        token = on_grads(l, 1, {n: slabs[n] for n in STAGES[1]})
        p_mla = sv["p_mla"] if token is None else [sv["p_mla"][0] + token[0, 0], sv["p_mla"][1]]
        (dmla,), g_mla = rows_bwd(_fn_mla_prep, [sv["mla"]], p_mla, [], [[dqn], [dkvn], [dkpe]], tile=t_row,
                                  name=tag + "mla_prep", grad_dtypes=[bf16])
        g_gate = mm(sv["h"], dgate, ta=True, out_dtype=bf16, name=tag + "g_gate")
        g_mlaw = mm(sv["h"], dmla, ta=True, out_dtype=bf16, name=tag + "g_mla")
        g_rw = mm(sv["h"], drwc, ta=True, out_dtype=bf16, name=tag + "g_rw")
        g_cv = mm(sv["h"], dcvc, ta=True, out_dtype=bf16, name=tag + "g_cv")
        dh = mm(dgate, lw["gate"], tb=True, name=tag + "d_h_gate")
        dh = mm(dmla, lw["mla"], tb=True, add=dh, name=tag + "d_h_mla")
        dh = mm(drwc, lw["rw"], tb=True, add=dh, name=tag + "d_h_rw")
        dh = mm(dcvc, lw["cv"], tb=True, add=dh, name=tag + "d_h_cv")
        (dx,), (g_n1,) = rows_bwd(_fn_norm, [sv["x"]], sv["p_norm1"], [], [[dh] + dh_extra], tile=t_row,
                                  name=tag + "norm1", extra={0: [dx1]})
        slabs.update(chip_major_grads(0, dict(gate=g_gate, mla=g_mlaw, rw=g_rw, cv=g_cv)))
        token = on_grads(l, 0, {n: slabs[n] for n in STAGES[0]})
        if token is not None and l > 0:
            dx = dx + token[0, 0]
        for n, val in slabs.items():
            grads[n][l] = val
        layer_small = [("attn_norm", l, g_n1), ("mlp_norm", l, g_n2), ("mla_q_a_norm", l, g_mla[0]),
                       ("mla_kv_a_norm", l, g_mla[1]), ("mla_q_norm", l, g_qk[0]), ("mla_k_norm", l, g_qk[1]),
                       ("rwkv_ln_w", l, g_post[0]), ("rwkv_ln_b", l, g_post[1]), ("rwkv_r_k", l, g_post[2]),
                       ("conv_w", l, jnp.concatenate(g_cw, axis=0))]
        layer_small += list(zip(["rwkv_mu", "rwkv_w0", "rwkv_w2", "rwkv_a0", "rwkv_a2", "rwkv_g2", "rwkv_k_k",
                                 "rwkv_k_a"], [l] * 8, g_prep[:8]))
        if vres:
            layer_small += list(zip(["rwkv_v1", "rwkv_v_mu", "rwkv_v0", "rwkv_v2"], [l - 1] * 4, g_prep[8:12]))
        for n, index, val in layer_small:
            grads[n][index] = val
        if l == 0:
            layer_small.append(("loss", 0, loss.reshape(1, 1)))
        token = on_small(l, layer_small)
        if token is not None and l > 0:
            dx = dx + token[0, 0]
    return loss, dx, grads


def _split3(a):
    hi = a.astype(bf16)
    r1 = a - hi.astype(f32)
    mid = r1.astype(bf16)
    lo = (r1 - mid.astype(f32)).astype(bf16)
    return hi, mid, lo


def _shard_axis(name):
    return 1 if name in ROW_SHARDED else 2


def _pack(pieces, width, dtype, row_align):
    flat = jnp.concatenate([p.reshape(-1).astype(dtype) for p in pieces])
    rows = -(-flat.shape[0] // width)
    rows = -(-rows // row_align) * row_align
    return jnp.pad(flat, (0, rows * width - flat.shape[0])).reshape(rows, width)


def _unpack(flat2d, shapes):
    flat = flat2d.reshape(-1)
    out, off = [], 0
    for shp in shapes:
        n = int(np.prod(shp))
        out.append(flat[off:off + n].reshape(shp))
        off += n
    return out


def kernel(x, positions, attn_norm, w_in, mla_q_a_norm, mla_wq_b, mla_kv_a_norm, mla_wkv_b, mla_q_norm, mla_k_norm, mla_w_o, rwkv_mu, rwkv_w0, rwkv_w2, rwkv_a0, rwkv_a2, rwkv_g2, rwkv_k_k, rwkv_k_a, rwkv_r_k, rwkv_ln_w, rwkv_ln_b, rwkv_w_o, rwkv_v1, rwkv_v_mu, rwkv_v0, rwkv_v2, conv_w, conv_w_o, w_out, mlp_norm, w_up, w_down, loss_target, m_attn_norm, m_w_in, m_mla_q_a_norm, m_mla_wq_b, m_mla_kv_a_norm, m_mla_wkv_b, m_mla_q_norm, m_mla_k_norm, m_mla_w_o, m_rwkv_mu, m_rwkv_w0, m_rwkv_w2, m_rwkv_a0, m_rwkv_a2, m_rwkv_g2, m_rwkv_k_k, m_rwkv_k_a, m_rwkv_r_k, m_rwkv_ln_w, m_rwkv_ln_b, m_rwkv_w_o, m_rwkv_v1, m_rwkv_v_mu, m_rwkv_v0, m_rwkv_v2, m_conv_w, m_conv_w_o, m_w_out, m_mlp_norm, m_w_up, m_w_down, v_attn_norm, v_w_in, v_mla_q_a_norm, v_mla_wq_b, v_mla_kv_a_norm, v_mla_wkv_b, v_mla_q_norm, v_mla_k_norm, v_mla_w_o, v_rwkv_mu, v_rwkv_w0, v_rwkv_w2, v_rwkv_a0, v_rwkv_a2, v_rwkv_g2, v_rwkv_k_k, v_rwkv_k_a, v_rwkv_r_k, v_rwkv_ln_w, v_rwkv_ln_b, v_rwkv_w_o, v_rwkv_v1, v_rwkv_v_mu, v_rwkv_v0, v_rwkv_v2, v_conv_w, v_conv_w_o, v_w_out, v_mlp_norm, v_w_up, v_w_down):
    args = dict(locals())
    wts = {n: args[n] for n in WEIGHTS}
    mom = {n: args["m_" + n] for n in WEIGHTS}
    var = {n: args["v_" + n] for n in WEIGHTS}
    chip = 2 * lax.axis_index("x") + lax.axis_index("y")
    core = lax.axis_index("c").astype(jnp.int32).reshape(1)

    med_names = [n for n in MED if n != "conv_w"]
    med_pieces = [wts[n] for n in med_names] + list(_split3(wts["conv_w"]))
    med_shapes = [p.shape for p in med_pieces]
    chip_idx = chip.astype(jnp.int32).reshape(1)
    shards_first = [wts[n][0].astype(bf16) for n in STAGES[0]] + [_pack(med_pieces, 128, bf16, 32)]
    got_first, token = gather_weights(shards_first, name="gather_l0_s0")
    got_first = [place_slab(g, s, chip_idx, name=f"place_own_l0_s0_{q}")
                 for q, (g, s) in enumerate(zip(got_first, shards_first))]
    in_flight = {}
    for key, names, l in (("l0_s1", STAGES[1], 0), ("l0_s2", STAGES[2], 0), ("l1", BIG, 1)):
        group = [wts[n][l].astype(bf16) for n in names]
        group[0] = group[0] + token[0, 0].astype(bf16)
        in_flight[key] = (names, gather_start(group, name="gather_start_" + key))
        token = in_flight[key][1][4]

    def whole_of(names, slabs):
        out = {}
        for n, by_chip in zip(names, slabs):
            _, rows, cols = by_chip.shape
            if n in ROW_SHARDED:
                out[n] = by_chip.reshape(4 * rows, cols)
            else:
                out[n] = by_chip.transpose(1, 0, 2).reshape(rows, 4 * cols)
        return out

    landed = {}

    def big_of(l, stage, after):
        if (l, stage) == (0, 0):
            return whole_of(STAGES[0], got_first)
        key = "l1" if l == 1 else f"l0_s{stage}"
        if key not in landed:
            names, (send_sems, recv_sems, thru, lands, _) = in_flight[key]
            thru, lands = gather_wait(send_sems, recv_sems, thru, lands, after, name="gather_wait_" + key)
            lands = gather_forward(lands, name="gather_forward_" + key)
            landed[key] = whole_of(names, [place_slab(g, s, chip_idx, name=f"place_own_{key}_{q}")
                                           for q, (g, s) in enumerate(zip(lands, thru))])
        return {n: landed[key][n] for n in STAGES[stage]}

    whole = {}
    per_chip = [_unpack(got_first[len(STAGES[0])][j], med_shapes) for j in range(4)]
    for q, n in enumerate(med_names):
        whole[n] = jnp.concatenate([per_chip[j][q] for j in range(4)], axis=_shard_axis(n)).astype(f32)
    base = len(med_names)
    cw_parts = [jnp.concatenate([per_chip[j][base + t] for j in range(4)], axis=2).astype(f32) for t in range(3)]
    whole["conv_w"] = (cw_parts[0] + cw_parts[1]) + cw_parts[2]
    small = {n: wts[n] for n in SMALL}
    small["rwkv_r_k"] = wts["rwkv_r_k"].reshape(DEPTH, RW)

    exchanges, to_sibling = [], []

    def chip_sums_on_their_way(after):
        tokens = []
        while to_sibling:
            l, names, tag, (send_sems, recv_sems, thru, lands, _) = to_sibling.pop(0)
            parts, theirs = sibling_wait(send_sems, recv_sems, thru, lands, after, name="sibling_wait_" + tag)
            chip_sums = [pair_sum(s, t, core, name=f"pair_sum_{n}_{l}") for n, s, t in zip(names, parts, theirs)]
            started = scatter_start(chip_sums, name="scatter_start_" + tag)
            exchanges.append((l, names, tag, started))
            tokens.append(started[4])
        return tokens

    def on_grads(l, stage, slabs):
        names = STAGES[stage]
        tag = f"l{l}_s{stage}"
        parts = [slabs[n] for n in names]
        tokens = chip_sums_on_their_way(parts[0])
        started = sibling_start(parts, name="sibling_start_" + tag)
        to_sibling.append((l, names, tag, started))
        token = started[4]
        for extra in tokens:
            token = token + extra
        return token

    broadcasts = []

    def on_small(l, layer_small):
        values = [val for _, _, val in layer_small]
        started = broadcast_start(_pack(values, 128, f32, 8), name=f"small_start_l{l}")
        broadcasts.append((l, [(n, index, val.shape) for n, index, val in layer_small], started))
        return started[4]

    small["attn_norm"] = small["attn_norm"] + token[0, 0]
    _, grad_x, grads = local_step(x[0], positions[0], loss_target[0], whole, small, big_of, on_grads, on_small)
    chip_sums_on_their_way(grad_x)

    device = (4 * lax.axis_index("x") + 2 * lax.axis_index("y") + lax.axis_index("c")).astype(jnp.int32).reshape(1)
    summed = {n: [None] * len(grads[n]) for n in SMALL + MED}
    summed["loss"] = [None]
    for l, entries, (send_sems, recv_sems, thru, land, _) in broadcasts:
        own, land = broadcast_wait(send_sems, recv_sems, thru, land, grad_x, name=f"small_wait_l{l}")
        total = sum8(land, own, device, name=f"small_sum_l{l}")
        for (n, index, _), val in zip(entries, _unpack(total, [shape for _, _, shape in entries])):
            summed[n][index] = val
    gsum = {}
    for n in SMALL + MED:
        g = jnp.stack(summed[n])
        if n in MED:
            ax = _shard_axis(n)
            width = wts[n].shape[ax]
            g = lax.dynamic_slice_in_dim(g, chip * width, width, axis=ax)
        gsum[n] = g.reshape(wts[n].shape)
    where = jnp.stack([lax.axis_index("c"), chip]).astype(jnp.int32)
    bufs, layout = {}, []
    for l, names, tag, (send_sems, recv_sems, thru, lands, _) in exchanges:
        own, arrived = scatter_wait(send_sems, recv_sems, thru, lands, grad_x, name="scatter_wait_" + tag)
        for n, mine, theirs in zip(names, own, arrived):
            rows = 2 * theirs.shape[1]
            bufs[n] = sum4_into(theirs, mine, bufs.get(n), where, layer=l, total_rows=DEPTH * rows,
                                name=f"sum_chips_{n}_{l}")
            layout.append((BIG.index(n), l * rows, rows))
    reduced = join_halves([bufs[n] for n in BIG], layout, name="join_halves")

    out_g, out_d, out_m, out_v = {}, {}, {}, {}
    for q, n in enumerate(BIG):
        shp = wts[n].shape
        as2d = lambda a: a.reshape(-1, shp[-1])
        g2d = w_in_window_cols(reduced[q], chip) if n == "w_in" else reduced[q]
        res = adamw(as2d(wts[n]), g2d, as2d(mom[n]), as2d(var[n]), name="adamw_" + n)
        out_g[n], out_d[n], out_m[n], out_v[n] = [r.reshape(shp) for r in res]
    sm_all = SMALL + MED
    flat2d = lambda a: a.reshape(-1, a.shape[-1])
    res = adamw_many([flat2d(wts[n]) for n in sm_all], [flat2d(gsum[n]) for n in sm_all],
                     [flat2d(mom[n]) for n in sm_all], [flat2d(var[n]) for n in sm_all], name="adamw_small")
    for tgt, vals in zip((out_d, out_m, out_v), res):
        for n, val in zip(sm_all, vals):
            tgt[n] = val.reshape(wts[n].shape)
    out_g.update({n: gsum[n] for n in sm_all})
    loss = summed["loss"][0].reshape(())
    return (loss, grad_x[None], *[out_g[n] for n in WEIGHTS], *[out_d[n] for n in WEIGHTS],
            *[out_m[n] for n in WEIGHTS], *[out_v[n] for n in WEIGHTS])
```

```python
import functools

import jax
import jax.numpy as jnp
import numpy as np
from jax import lax
from jax.experimental import pallas as pl
from jax.experimental.pallas import tpu as pltpu

f32, bf16 = jnp.float32, jnp.bfloat16
HI = lax.Precision.HIGHEST
MESH = pl.DeviceIdType.MESH

D = 1024
DEPTH = 2
MLA_H, NOPE, ROPE, DQK, DV = 8, 64, 32, 96, 64
QL, KVL = 384, 256
RW, RH, RN = 256, 4, 64
DL, AL, GL, MVL = 64, 64, 128, 32
CW = 256
DFF = 4096
GATE = 3 * D
MLA_COLS = QL + KVL + ROPE
MLA_PAD = 768
Q_HEAD_PAD = 128
NORM_EPS = 1e-6
GN_EPS = 64e-5
ROPE_THETA = 10000.0
LR, B1, B2, EPS, WD, STEP = 0.001, 0.9, 0.999, 1e-08, 0.01, 10

VMEM_LIMIT = 52 * 1024 * 1024
WKV_CHUNK = 64
WKV_CHUNKS_PER_STEP = 4
ATTN_SEGMENTS = 4
ATTN_Q_BLOCK = 512

BIG = ["w_in", "mla_wq_b", "mla_wkv_b", "mla_w_o", "rwkv_w_o", "conv_w_o", "w_out", "w_up", "w_down"]
MED = ["rwkv_w2", "rwkv_a2", "rwkv_g2", "rwkv_v1", "rwkv_v2", "conv_w"]
ROW_SHARDED = {"w_out", "w_down", "rwkv_v1"}
SMALL = ["attn_norm", "mla_q_a_norm", "mla_kv_a_norm", "mla_q_norm", "mla_k_norm", "rwkv_mu", "rwkv_w0",
         "rwkv_a0", "rwkv_k_k", "rwkv_k_a", "rwkv_r_k", "rwkv_ln_w", "rwkv_ln_b", "rwkv_v_mu", "rwkv_v0",
         "mlp_norm"]
WEIGHTS = ["attn_norm", "w_in", "mla_q_a_norm", "mla_wq_b", "mla_kv_a_norm", "mla_wkv_b", "mla_q_norm",
           "mla_k_norm", "mla_w_o", "rwkv_mu", "rwkv_w0", "rwkv_w2", "rwkv_a0", "rwkv_a2", "rwkv_g2",
           "rwkv_k_k", "rwkv_k_a", "rwkv_r_k", "rwkv_ln_w", "rwkv_ln_b", "rwkv_w_o", "rwkv_v1", "rwkv_v_mu",
           "rwkv_v0", "rwkv_v2", "conv_w", "conv_w_o", "w_out", "mlp_norm", "w_up", "w_down"]


def _cparams(sem=None):
    return pltpu.CompilerParams(dimension_semantics=sem, vmem_limit_bytes=VMEM_LIMIT)


def _pick(dim, pref, align):
    if dim <= pref:
        return dim
    t = (pref // align) * align
    while t >= align:
        if dim % t == 0:
            return t
        t -= align
    return dim


def _bdot(a, b, dims):
    return lax.dot_general(a.astype(bf16), b.astype(bf16), (dims, ((), ())), preferred_element_type=f32)


@jax.custom_vjp
def _mm(a, b):
    return _bdot(a, b, ((1,), (0,)))


def _mm_fwd(a, b):
    return _mm(a, b), (a, b)


def _mm_bwd(res, g):
    a, b = res
    return _bdot(g, b, ((1,), (1,))), _bdot(a, g, ((0,), (0,)))


_mm.defvjp(_mm_fwd, _mm_bwd)


@jax.custom_vjp
def _mm_nt(a, b):
    return _bdot(a, b, ((1,), (1,)))


def _mm_nt_fwd(a, b):
    return _mm_nt(a, b), (a, b)


def _mm_nt_bwd(res, g):
    a, b = res
    return _bdot(g, b, ((1,), (0,))), _bdot(g, a, ((0,), (0,)))


_mm_nt.defvjp(_mm_nt_fwd, _mm_nt_bwd)


_NN, _NT, _TN = ((1,), (0,)), ((1,), (1,)), ((0,), (0,))


def _dg(a, b, dims):
    return lax.dot_general(a, b, (dims, ((), ())), preferred_element_type=f32)


def _bf16_pieces(x, count):
    out, rest = [], x
    for q in range(count):
        piece = rest.astype(bf16)
        out.append(piece)
        if q + 1 < count:
            rest = rest - piece.astype(f32)
    return out


def _dot3(a, b, dims):
    (ah, al), (bh, bl) = _bf16_pieces(a, 2), _bf16_pieces(b, 2)
    return _dg(ah, bh, dims) + (_dg(ah, bl, dims) + _dg(al, bh, dims))


@jax.custom_vjp
def _hdot(a, b):
    return _dot3(a, b, _NN)


@jax.custom_vjp
def _hdot_nt(a, b):
    return _dot3(a, b, _NT)


@jax.custom_vjp
def _hdot_tn(a, b):
    return _dot3(a, b, _TN)


_hdot.defvjp(lambda a, b: (_hdot(a, b), (a, b)), lambda res, g: (_hdot_nt(g, res[1]), _hdot_tn(res[0], g)))
_hdot_nt.defvjp(lambda a, b: (_hdot_nt(a, b), (a, b)), lambda res, g: (_hdot(g, res[1]), _hdot_tn(g, res[0])))
_hdot_tn.defvjp(lambda a, b: (_hdot_tn(a, b), (a, b)), lambda res, g: (_hdot_nt(res[1], g), _hdot(res[0], g)))


_BNN, _BNT, _BTN = ((2,), (1,)), ((2,), (2,)), ((1,), (1,))


def _bdg(a, b, dims):
    return lax.dot_general(a, b, (dims, ((0,), (0,))), preferred_element_type=f32)


def _bdot3(a, b, dims):
    (ah, al), (bh, bl) = _bf16_pieces(a, 2), _bf16_pieces(b, 2)
    return _bdg(ah, bh, dims) + (_bdg(ah, bl, dims) + _bdg(al, bh, dims))


@jax.custom_vjp
def _hbnn(a, b):
    return _bdot3(a, b, _BNN)


@jax.custom_vjp
def _hbnt(a, b):
    return _bdot3(a, b, _BNT)


@jax.custom_vjp
def _hbtn(a, b):
    return _bdot3(a, b, _BTN)


_hbnn.defvjp(lambda a, b: (_hbnn(a, b), (a, b)), lambda res, g: (_hbnt(g, res[1]), _hbtn(res[0], g)))
_hbnt.defvjp(lambda a, b: (_hbnt(a, b), (a, b)), lambda res, g: (_hbnn(g, res[1]), _hbtn(g, res[0])))
_hbtn.defvjp(lambda a, b: (_hbtn(a, b), (a, b)), lambda res, g: (_hbnt(res[1], g), _hbnn(res[0], g)))


@functools.partial(jax.custom_vjp, nondiff_argnums=(2,))
def _exact_bl(m, x, transposed):
    mb = m.astype(bf16)
    hi, mid, lo = _bf16_pieces(x, 3)
    dims = _BTN if transposed else _BNN
    return (_bdg(mb, hi, dims) + _bdg(mb, mid, dims)) + _bdg(mb, lo, dims)


_exact_bl.defvjp(lambda m, x, transposed: (_exact_bl(m, x, transposed), m),
                 lambda transposed, m, g: (jnp.zeros_like(m), _exact_bl(m, g, not transposed)))


@functools.partial(jax.custom_vjp, nondiff_argnums=(2,))
def _exact_l(m, x, transposed):
    mb = m.astype(bf16)
    hi, mid, lo = _bf16_pieces(x, 3)
    dims = _TN if transposed else _NN
    return (_dg(mb, hi, dims) + _dg(mb, mid, dims)) + _dg(mb, lo, dims)


_exact_l.defvjp(lambda m, x, transposed: (_exact_l(m, x, transposed), m),
                lambda transposed, m, g: (jnp.zeros_like(m), _exact_l(m, g, not transposed)))


@functools.partial(jax.custom_vjp, nondiff_argnums=(2,))
def _exact_r(x, m, transposed):
    mb = m.astype(bf16)
    hi, mid, lo = _bf16_pieces(x, 3)
    dims = _NT if transposed else _NN
    return (_dg(hi, mb, dims) + _dg(mid, mb, dims)) + _dg(lo, mb, dims)


_exact_r.defvjp(lambda x, m, transposed: (_exact_r(x, m, transposed), m),
                lambda transposed, m, g: (_exact_r(g, m, not transposed), jnp.zeros_like(m)))


def _rms(x, g, eps=NORM_EPS):
    return x * lax.rsqrt(jnp.mean(x * x, axis=-1, keepdims=True) + eps) * g


def _sigmoid(x):
    return 1.0 / (1.0 + jnp.exp(-x))


def _softplus(x):
    return jnp.maximum(x, 0.0) + jnp.log(1.0 + jnp.exp(-jnp.maximum(x, -x)))


def _lane_split(x, sizes):
    bounds = np.cumsum([0] + list(sizes))

    @jax.custom_vjp
    def split(v):
        return tuple(v[..., int(bounds[q]):int(bounds[q + 1])] for q in range(len(sizes)))

    split.defvjp(lambda v: (split(v), None), lambda _, g: (jnp.concatenate(g, axis=-1),))
    return split(x)


def _row_split(x, sizes):
    bounds = np.cumsum([0] + list(sizes))

    @jax.custom_vjp
    def split(v):
        return tuple(v[..., int(bounds[q]):int(bounds[q + 1]), :] for q in range(len(sizes)))

    split.defvjp(lambda v: (split(v), None), lambda _, g: (jnp.concatenate(g, axis=-2),))
    return split(x)


def _shift_mats(t, k):
    r = lax.broadcasted_iota(jnp.int32, (t, t), 0)
    c = lax.broadcasted_iota(jnp.int32, (t, t), 1)
    inner = (r - c == k).astype(f32)
    r8 = lax.broadcasted_iota(jnp.int32, (t, 8), 0)
    c8 = lax.broadcasted_iota(jnp.int32, (t, 8), 1)
    edge = (c8 - r8 == 8 - k).astype(f32)
    return inner, edge


def _shift(x, halo, k):
    inner, edge = _shift_mats(x.shape[0], k)
    return _exact_l(inner, x, False) + jnp.dot(edge, halo, precision=HI, preferred_element_type=f32)


def mm(a, b, *, name, ta=False, tb=False, a_batched=False, b_batched=False, reduce_batch=False, add=None,
       act_grad=None, relu2_out=False, n_split=1, out_dtype=f32, tm=1024, tn=1024, tk=2048):
    ash, bsh = a.shape[-2:], b.shape[-2:]
    (k_, m_) = ash if ta else ash[::-1]
    (k2_, n_) = bsh[::-1] if tb else bsh
    assert k_ == k2_, (a.shape, b.shape, ta, tb)
    hb = a.shape[0] if a_batched else (b.shape[0] if b_batched else 1)
    batched_out = (a_batched or b_batched) and not reduce_batch
    h_out = hb if batched_out else 1
    h_red = hb if reduce_batch else 1
    tm = _pick(m_, tm, 128)
    tn = _pick(n_ // n_split, tn, 128)
    tk = _pick(k_, tk, 128)
    nm, nn, nk = m_ // tm, n_ // tn, k_ // tk

    def a_map(i, j, ho, hr, kk):
        blk = (kk, i) if ta else (i, kk)
        return ((ho if batched_out else hr),) + blk if a_batched else blk

    def b_map(i, j, ho, hr, kk):
        blk = (j, kk) if tb else (kk, j)
        return ((ho if batched_out else hr),) + blk if b_batched else blk

    a_blk = (tk, tm) if ta else (tm, tk)
    b_blk = (tn, tk) if tb else (tk, tn)
    in_specs = [pl.BlockSpec(((1,) + a_blk) if a_batched else a_blk, a_map),
                pl.BlockSpec(((1,) + b_blk) if b_batched else b_blk, b_map)]
    args = [a, b]
    for extra in (add, act_grad):
        if extra is not None:
            in_specs.append(pl.BlockSpec((tm, tn), lambda i, j, ho, hr, kk: (i, j)))
            args.append(extra)
    if n_split > 1:
        per = n_ // n_split // tn
        if batched_out:
            out_spec = pl.BlockSpec((1, 1, tm, tn), lambda i, j, ho, hr, kk: (j // per, ho, i, j % per))
            out_shape = jax.ShapeDtypeStruct((n_split, hb, m_, n_ // n_split), out_dtype)
        else:
            out_spec = pl.BlockSpec((1, tm, tn), lambda i, j, ho, hr, kk: (j // per, i, j % per))
            out_shape = jax.ShapeDtypeStruct((n_split, m_, n_ // n_split), out_dtype)
    elif batched_out:
        out_spec = pl.BlockSpec((1, tm, tn), lambda i, j, ho, hr, kk: (ho, i, j))
        out_shape = jax.ShapeDtypeStruct((hb, m_, n_), out_dtype)
    else:
        out_spec = pl.BlockSpec((tm, tn), lambda i, j, ho, hr, kk: (i, j))
        out_shape = jax.ShapeDtypeStruct((m_, n_), out_dtype)
    lead = (0,) * (int(batched_out) + int(n_split > 1))
    dims = ((0,) if ta else (1,), (1,) if tb else (0,))
    has_add, has_act = add is not None, act_grad is not None

    def body(*refs):
        a_ref, b_ref = refs[0], refs[1]
        pos = 2
        add_ref = act_ref = None
        if has_add:
            add_ref = refs[pos]
            pos += 1
        if has_act:
            act_ref = refs[pos]
            pos += 1
        o_ref, acc_ref = refs[pos], refs[-1]
        hr, kk = pl.program_id(3), pl.program_id(4)
        first = jnp.logical_and(hr == 0, kk == 0)
        last = jnp.logical_and(hr == h_red - 1, kk == nk - 1)
        av = a_ref[0] if a_batched else a_ref[...]
        bv = b_ref[0] if b_batched else b_ref[...]
        p = _bdot(av, bv, dims)
        single = h_red * nk == 1

        if not single:
            @pl.when(first)
            def _():
                acc_ref[...] = p

            @pl.when(jnp.logical_not(first))
            def _():
                acc_ref[...] += p

        @pl.when(last)
        def _():
            r = p if single else acc_ref[...]
            if has_act:
                r = r * (2.0 * jnp.maximum(act_ref[...], 0.0))
            if has_add:
                r = r + add_ref[...]
            if lead:
                o_ref[lead] = r.astype(out_dtype)
            else:
                o_ref[...] = r.astype(out_dtype)
            if relu2_out:
                refs[pos + 1][...] = jnp.square(jnp.maximum(r, 0.0)).astype(bf16)

    if relu2_out:
        assert not lead
        out_spec = [out_spec, out_spec]
        out_shape = [out_shape, jax.ShapeDtypeStruct(out_shape.shape, bf16)]
    return pl.pallas_call(
        body, name=name, grid=(nm, nn, h_out, h_red, nk), in_specs=in_specs, out_specs=out_spec,
        out_shape=out_shape, scratch_shapes=[pltpu.VMEM((tm, tn), f32)],
        compiler_params=_cparams(("parallel", "parallel", "parallel", "arbitrary", "arbitrary")),
    )(*args)


def _row_spec(arr, tile, idx):
    if arr.ndim == 2:
        return pl.BlockSpec((tile, arr.shape[1]), lambda i: (idx(i), 0))
    return pl.BlockSpec((arr.shape[0], tile, arr.shape[2]), lambda i: (0, idx(i), 0))


def _halo_spec(arr, tile, idx):
    per = tile // 8
    return pl.BlockSpec((8, arr.shape[1]), lambda i: (jnp.maximum(idx(i) * per - 1, 0), 0))


def _full_spec(arr):
    nd = arr.ndim
    return pl.BlockSpec(arr.shape, lambda i: (0,) * nd)


def _load_f32(ref):
    val = ref[...]
    return val.astype(f32) if val.dtype == bf16 else val


def rows_fwd(fn, rows, params, consts, out_shapes, *, tile, name, halos=()):
    s_len = rows[0].shape[-2]
    n = s_len // tile
    nr, nh, npar, nc = len(rows), len(halos), len(params), len(consts)
    ident = lambda i: i
    in_specs = ([_row_spec(r, tile, ident) for r in rows] + [_halo_spec(rows[h], tile, ident) for h in halos]
                + [_full_spec(p) for p in params] + [_full_spec(c) for c in consts])
    out_specs = [_row_spec(o, tile, ident) for o in out_shapes]

    def body(*refs):
        i = pl.program_id(0)
        rv = [_load_f32(r) for r in refs[:nr]]
        keep = (i > 0).astype(f32)
        hv = [r[...] * keep for r in refs[nr:nr + nh]]
        pv = [r[...] for r in refs[nr + nh:nr + nh + npar]]
        cv = [r[...] for r in refs[nr + nh + npar:nr + nh + npar + nc]]
        outs = fn(rv, hv, pv, cv)
        for o_ref, o in zip(refs[nr + nh + npar + nc:], outs):
            o_ref[...] = o.astype(o_ref.dtype)

    return pl.pallas_call(
        body, name=name, grid=(n,), in_specs=in_specs, out_specs=out_specs, out_shape=list(out_shapes),
        compiler_params=_cparams(("arbitrary",)),
    )(*rows, *[rows[h] for h in halos], *params, *consts)


def rows_bwd(fn, rows, params, consts, douts, *, tile, name, halos=(), grad_rows=None, extra=None,
             grad_dtypes=None):
    s_len = rows[0].shape[-2]
    n = s_len // tile
    nr, nh, npar, nc = len(rows), len(halos), len(params), len(consts)
    grad_rows = list(range(nr)) if grad_rows is None else list(grad_rows)
    extra = extra or {}
    assert all(h in grad_rows for h in halos)
    rev = lambda i: n - 1 - i
    dflat = [d for ds in douts for d in ds]
    dcount = [len(ds) for ds in douts]
    eflat = [e for g in grad_rows for e in extra.get(g, [])]
    ecount = [len(extra.get(g, [])) for g in grad_rows]
    in_specs = ([_row_spec(r, tile, rev) for r in rows] + [_halo_spec(rows[h], tile, rev) for h in halos]
                + [_full_spec(p) for p in params] + [_full_spec(c) for c in consts]
                + [_row_spec(d, tile, rev) for d in dflat] + [_row_spec(e, tile, rev) for e in eflat])
    grad_dtypes = [f32] * len(grad_rows) if grad_dtypes is None else list(grad_dtypes)
    assert all(grad_dtypes[q] == f32 for q, g in enumerate(grad_rows) if g in halos)
    out_shapes = ([jax.ShapeDtypeStruct(rows[g].shape, dt) for g, dt in zip(grad_rows, grad_dtypes)]
                  + [jax.ShapeDtypeStruct(p.shape, f32) for p in params])
    out_specs = [_row_spec(rows[g], tile, rev) for g in grad_rows] + [_full_spec(p) for p in params]
    scratch = [pltpu.VMEM((8, rows[h].shape[1]), f32) for h in halos]
    n_in = nr + nh + npar + nc + len(dflat) + len(eflat)
    n_out = len(grad_rows) + npar

    def body(*refs):
        i = pl.program_id(0)
        rv = [_load_f32(r) for r in refs[:nr]]
        keep = (i < n - 1).astype(f32)
        hv = [r[...] * keep for r in refs[nr:nr + nh]]
        pv = [r[...] for r in refs[nr + nh:nr + nh + npar]]
        pos = nr + nh + npar
        cv = [r[...] for r in refs[pos:pos + nc]]
        pos += nc
        dv = []
        for cnt in dcount:
            acc = _load_f32(refs[pos])
            for q in range(1, cnt):
                acc = acc + _load_f32(refs[pos + q])
            dv.append(acc)
            pos += cnt
        ev = []
        for cnt in ecount:
            ev.append([_load_f32(refs[pos + q]) for q in range(cnt)])
            pos += cnt
        out_refs = refs[n_in:n_in + n_out]
        carry_refs = refs[n_in + n_out:]

        def f(gr, gh, gp):
            full = list(rv)
            for g, val in zip(grad_rows, gr):
                full[g] = val
            return tuple(fn(full, gh, gp, cv))

        _, vjp = jax.vjp(f, [rv[g] for g in grad_rows], hv, pv)
        d_rows, d_halos, d_params = vjp(tuple(dv))

        @pl.when(i == 0)
        def _():
            for c_ref in carry_refs:
                c_ref[...] = jnp.zeros_like(c_ref)
            for p_ref in out_refs[len(grad_rows):]:
                p_ref[...] = jnp.zeros_like(p_ref)

        for q, g in enumerate(grad_rows):
            val = d_rows[q]
            for e in ev[q]:
                val = val + e
            out_refs[q][...] = val.astype(out_refs[q].dtype)
            if g in halos:
                hq = list(halos).index(g)
                out_refs[q][tile - 8:tile, :] += carry_refs[hq][...]
                carry_refs[hq][...] = d_halos[hq]
        for p_ref, dp in zip(out_refs[len(grad_rows):], d_params):
            p_ref[...] += dp

    res = pl.pallas_call(
        body, name=name, grid=(n,), in_specs=in_specs, out_specs=out_specs, out_shape=out_shapes,
        scratch_shapes=scratch, compiler_params=_cparams(("arbitrary",)),
    )(*rows, *[rows[h] for h in halos], *params, *consts, *dflat, *eflat)
    return list(res[:len(grad_rows)]), list(res[len(grad_rows):])


def _fn_norm(rows, halos, params, consts):
    return (_rms(rows[0], params[0]),)


def _fn_mla_prep(rows, halos, params, consts):
    cq, ckv, kpe = _lane_split(rows[0], (QL, KVL, MLA_PAD - QL - KVL))
    return _rms(cq, params[0]), _rms(ckv, params[1]), kpe


def _rope(x, cos, sin, rot):
    return x * cos + _exact_r(x, rot, False) * sin


def _fn_qk_post(rows, halos, params, consts):
    q_flat, kv_flat, kpe, cos, sin = rows
    q_norm, k_norm = params
    (rot,) = consts
    q_heads = _lane_split(q_flat, (DQK, Q_HEAD_PAD - DQK) * MLA_H)[::2]
    kv_heads = _lane_split(kv_flat, (NOPE, DV) * MLA_H)
    k_pe, _ = _lane_split(kpe, (ROPE, kpe.shape[1] - ROPE))
    qs = [_rope(_rms(qh, q_norm), cos, sin, rot) for qh in q_heads]
    ks = [_rope(_rms(jnp.concatenate([kv_heads[2 * h], k_pe], axis=-1), k_norm), cos, sin, rot)
          for h in range(MLA_H)]
    vs = [kv_heads[2 * h + 1] for h in range(MLA_H)]
    return jnp.stack(qs, axis=0), jnp.stack(ks, axis=0), jnp.stack(vs, axis=0)


def _seg(x, bd):
    return _exact_r(x, bd, False)


def _make_fn_rwkv_prep(vres):
    def fn(rows, halos, params, consts):
        cols = rows[0]
        bd = consts[0]
        mu, w0, w2, a0, a2, g2, k_k, k_a = params[:8]
        prev = _shift(cols, halos[0], 1)
        c = cols + (prev - cols) * mu
        r, k, v, xw, xa, xg = _lane_split(c, (RW, RW, RW, DL, AL, GL))
        log_w = -_softplus(-(w0 + _mm(jnp.tanh(xw), w2))) - 0.5
        ld = -jnp.exp(log_w)
        a = _sigmoid(a0 + _mm(xa, a2))
        g = _mm(_sigmoid(xg), g2)
        if vres:
            hcur, v_first = rows[1], rows[2]
            v1, v_mu, v0, v2 = params[8:12]
            xv = _mm(hcur, v1)
            xv_prev = _shift(xv, _mm(halos[1], v1), 1)
            xv = xv + (xv_prev - xv) * v_mu
            v = v + (v_first - v) * _sigmoid(v0 + _mm(xv, v2))
        kk = k * k_k
        kk = kk / jnp.maximum(jnp.sqrt(_seg(kk * kk, bd)), 1e-12)
        k2 = k * (1.0 + (a - 1.0) * k_a)
        return r, ld, k2, v, -kk, kk * a, g
    return fn


def _fn_rwkv_post(rows, halos, params, consts):
    y, r, k2, v, g = rows
    ln_w, ln_b, r_k = params
    bd = consts[0]
    mean = _seg(y, bd) * (1.0 / RN)
    d = y - mean
    var = _seg(d * d, bd) * (1.0 / RN)
    yn = d * lax.rsqrt(var + GN_EPS) * ln_w + ln_b
    bonus = _seg(r * k2 * r_k, bd) * v
    return ((yn + bonus) * g,)


def _fn_conv(rows, halos, params, consts):
    cols, halo = rows[0], halos[0]
    w0, w1, w2 = params
    b, c, x = _lane_split(cols, (CW, CW, CW))
    _, ch, xh = _lane_split(halo, (CW, CW, CW))
    u, uh = c * x, ch * xh
    return (b * (w0 * _shift(u, uh, 2) + w1 * _shift(u, uh, 1) + w2 * u),)


def _fn_merge(rows, halos, params, consts):
    gate, o_a, o_b, o_c = rows
    g_a, g_b, g_c = _lane_split(gate, (D, D, D))
    return (_sigmoid(g_a) * o_a + _sigmoid(g_b) * o_b + _sigmoid(g_c) * o_c,)


def _attn_block(q, k, v, q0, diagonal_last):
    tq, kend = q.shape[0], k.shape[0]
    s = _mm_nt(q, k) * (DQK ** -0.5)
    if diagonal_last:
        tri = lax.broadcasted_iota(jnp.int32, (tq, tq), 0) >= lax.broadcasted_iota(jnp.int32, (tq, tq), 1)
        if kend > tq:
            before, diag = _lane_split(s, (kend - tq, tq))
            s = jnp.concatenate([before, jnp.where(tri, diag, -1e30)], axis=-1)
        else:
            s = jnp.where(tri, s, -1e30)
    else:
        row = q0 + lax.broadcasted_iota(jnp.int32, (tq, kend), 0)
        col = lax.broadcasted_iota(jnp.int32, (tq, kend), 1)
        s = jnp.where(row >= col, s, -1e30)
    m = lax.stop_gradient(jnp.max(s, axis=-1, keepdims=True))
    e = jnp.exp(s - m)
    p = e / jnp.sum(e, axis=-1, keepdims=True)
    return _mm(p, v)


def _attn_segments(s_len, tq):
    per = max(1, s_len // tq // ATTN_SEGMENTS)
    return [(first, per, (first + per) * tq) for first in range(0, s_len // tq, per)]


HEAD_PAIR = 2


def attn_fwd(q, k, v, *, tq, name):
    h, s_len, _ = q.shape
    outs = []
    for seg, (first, nq, kend) in enumerate(_attn_segments(s_len, tq)):
        def body(q_ref, k_ref, v_ref, o_ref, first=first, nq=nq):
            q0 = (first + pl.program_id(1)) * tq
            o = [_attn_block(q_ref[j], k_ref[j], v_ref[j], q0, nq == 1) for j in range(HEAD_PAIR)]
            o_ref[...] = jnp.concatenate(o, axis=-1).astype(o_ref.dtype)

        outs.append(pl.pallas_call(
            body, name=f"{name}_{seg}", grid=(h // HEAD_PAIR, nq),
            in_specs=[pl.BlockSpec((HEAD_PAIR, tq, DQK), lambda hp, i, first=first: (hp, first + i, 0)),
                      pl.BlockSpec((HEAD_PAIR, kend, DQK), lambda hp, i: (hp, 0, 0)),
                      pl.BlockSpec((HEAD_PAIR, kend, DV), lambda hp, i: (hp, 0, 0))],
            out_specs=pl.BlockSpec((tq, HEAD_PAIR * DV), lambda hp, i: (i, hp)),
            out_shape=jax.ShapeDtypeStruct((nq * tq, h * DV), bf16),
            compiler_params=_cparams(("parallel", "arbitrary")),
        )(q, k, v))
    return jnp.concatenate(outs, axis=0)


def attn_bwd(q, k, v, do, *, tq, name):
    h, s_len, _ = q.shape
    dqs, dk_acc, dv_acc = [], None, None
    for seg, (first, nq, kend) in reversed(list(enumerate(_attn_segments(s_len, tq)))):
        carried = dk_acc is not None

        def body(*refs, first=first, carried=carried, nq=nq):
            q_ref, k_ref, v_ref, do_ref = refs[:4]
            dq_ref, dk_ref, dv_ref = refs[-3:]
            i = pl.program_id(1)
            do_heads = _lane_split(do_ref[...], (DV,) * HEAD_PAIR)
            for j in range(HEAD_PAIR):
                _, vjp = jax.vjp(functools.partial(_attn_block, q0=(first + i) * tq, diagonal_last=nq == 1),
                                 q_ref[j], k_ref[j], v_ref[j])
                dq, dk, dv = vjp(do_heads[j])
                dq_ref[j] = dq

                @pl.when(i == 0)
                def _():
                    dk_ref[j] = dk + refs[4][j] if carried else dk
                    dv_ref[j] = dv + refs[5][j] if carried else dv

                @pl.when(i > 0)
                def _():
                    dk_ref[j] += dk
                    dv_ref[j] += dv

        key_specs = [pl.BlockSpec((HEAD_PAIR, kend, DQK), lambda hp, i: (hp, 0, 0)),
                     pl.BlockSpec((HEAD_PAIR, kend, DV), lambda hp, i: (hp, 0, 0))]
        dq, dk_acc, dv_acc = pl.pallas_call(
            body, name=f"{name}_{seg}", grid=(h // HEAD_PAIR, nq),
            in_specs=[pl.BlockSpec((HEAD_PAIR, tq, DQK), lambda hp, i, first=first: (hp, first + i, 0))] + key_specs
            + [pl.BlockSpec((tq, HEAD_PAIR * DV), lambda hp, i, first=first: (first + i, hp))]
            + (key_specs if carried else []),
            out_specs=[pl.BlockSpec((HEAD_PAIR, tq, DQK), lambda hp, i: (hp, i, 0))] + key_specs,
            out_shape=[jax.ShapeDtypeStruct((h, nq * tq, DQK), f32), jax.ShapeDtypeStruct((h, s_len, DQK), f32),
                       jax.ShapeDtypeStruct((h, s_len, DV), f32)],
            input_output_aliases={4: 1, 5: 2} if carried else {},
            compiler_params=_cparams(("parallel", "arbitrary")),
        )(q, k, v, do, *([dk_acc, dv_acc] if carried else []))
        dqs.append(dq)
    return jnp.concatenate(dqs[::-1], axis=1), dk_acc, dv_acc


def _wkv_local(r, ld, k, v, a, b):
    nb, c, n = r.shape
    ri = lax.broadcasted_iota(jnp.int32, (c, c), 0)
    ci = lax.broadcasted_iota(jnp.int32, (c, c), 1)
    tri = jnp.broadcast_to((ri >= ci).astype(f32)[None], (nb, c, c))
    cum = _exact_bl(tri, ld, False)
    tot = jnp.sum(ld, axis=1, keepdims=True)
    w_incl, w_excl, w_inv, w_rest = jnp.exp(cum), jnp.exp(cum - ld), jnp.exp(-cum), jnp.exp(tot - cum)
    ab, rb, bb, kb = a * w_excl, r * w_incl, b * w_inv, k * w_inv
    bw, kw = b * w_rest, k * w_rest
    r2 = lax.broadcasted_iota(jnp.int32, (2 * c, 2 * c), 0)
    c2 = lax.broadcasted_iota(jnp.int32, (2 * c, 2 * c), 1)
    t_of, s_of = jnp.where(r2 >= c, r2 - c, r2), jnp.where(c2 >= c, c2 - c, c2)
    keep = jnp.logical_or(t_of > s_of, jnp.logical_and(r2 >= c, t_of == s_of))
    pair = jnp.where(keep[None], _hbnt(jnp.concatenate([ab, rb], axis=1), jnp.concatenate([bb, kb], axis=1)), 0.0)
    on_b, on_k = _lane_split(pair, (c, c))
    l_ab, m_rb = _row_split(on_b, (c, c))
    l_ak_v, m_rk_v = _row_split(_hbnn(on_k, v), (c, c))
    x = jnp.concatenate([ab, l_ak_v], axis=-1)
    lp, span = l_ab, 1
    while span < c:
        x = x + _hbnn(lp, x)
        span *= 2
        if span < c:
            lp = _hbnn(lp, lp)
    via_b_r, via_b_y = _lane_split(_hbnn(m_rb, x), (n, n))
    r_hat = rb + via_b_r
    y0 = via_b_y + m_rk_v
    from_b_g, from_b_z = _row_split(_hbtn(x, bw), (n, n))
    eye = lax.broadcasted_iota(jnp.int32, (n, n), 0) == lax.broadcasted_iota(jnp.int32, (n, n), 1)
    g = jnp.where(eye[None], jnp.exp(tot), 0.0) + from_b_g
    z = from_b_z + _hbtn(v, kw)
    return r_hat, y0, g, z


def _head(h):
    return slice(RN * h, RN * (h + 1))


def _load_chunk_heads(ref, c, per):
    return jnp.stack([ref[c * q:c * (q + 1), _head(h)] for q in range(per) for h in range(RH)], axis=0)


def _store_chunk_heads(ref, val, c, per):
    for q in range(per):
        ref[c * q:c * (q + 1), :] = jnp.concatenate([val[q * RH + h] for h in range(RH)], axis=-1)


def wkv_fwd(r, ld, k, v, a, b, *, name):
    s_len = r.shape[0]
    c = WKV_CHUNK
    n = s_len // c
    per = min(WKV_CHUNKS_PER_STEP, n)
    rows = pl.BlockSpec((c * per, RW), lambda i: (i, 0))
    mats = pl.BlockSpec((per, RH, RN, RN), lambda i: (i, 0, 0, 0))
    rows_t, mats_t = jax.ShapeDtypeStruct((s_len, RW), f32), jax.ShapeDtypeStruct((n, RH, RN, RN), f32)

    def local_body(r_ref, ld_ref, k_ref, v_ref, a_ref, b_ref, rh_ref, y0_ref, g_ref, z_ref):
        r_hat, y0, g, z = _wkv_local(*[_load_chunk_heads(ref, c, per)
                                       for ref in (r_ref, ld_ref, k_ref, v_ref, a_ref, b_ref)])
        _store_chunk_heads(rh_ref, r_hat, c, per)
        _store_chunk_heads(y0_ref, y0, c, per)
        g_ref[...] = g.reshape(per, RH, RN, RN)
        z_ref[...] = z.reshape(per, RH, RN, RN)

    r_hat, y0, g, z = pl.pallas_call(
        local_body, name=name + "_local", grid=(n // per,), in_specs=[rows] * 6, out_specs=[rows, rows, mats, mats],
        out_shape=[rows_t, rows_t, mats_t, mats_t], compiler_params=_cparams(("parallel",)),
    )(r, ld, k, v, a, b)

    def scan_body(g_ref, z_ref, st_ref, s_sc):
        s_sc[...] = jnp.zeros_like(s_sc)

        @pl.loop(0, n)
        def _(i):
            s0 = s_sc[...]
            st_ref[i] = s0
            s_sc[...] = _hbnn(s0, g_ref[i]) + z_ref[i]

    vm = pl.BlockSpec(memory_space=pltpu.VMEM)
    states = pl.pallas_call(
        scan_body, name=name + "_scan", in_specs=[vm, vm], out_specs=vm, out_shape=mats_t,
        scratch_shapes=[pltpu.VMEM((RH, RN, RN), f32)],
        compiler_params=pltpu.CompilerParams(vmem_limit_bytes=VMEM_LIMIT),
    )(g, z)

    def out_body(rh_ref, y0_ref, st_ref, y_ref):
        y = _hbnt(_load_chunk_heads(rh_ref, c, per), st_ref[...].reshape(per * RH, RN, RN))
        _store_chunk_heads(y_ref, y, c, per)
        y_ref[...] += y0_ref[...]

    y = pl.pallas_call(
        out_body, name=name + "_out", grid=(n // per,), in_specs=[rows, rows, mats], out_specs=rows,
        out_shape=rows_t, compiler_params=_cparams(("parallel",)),
    )(r_hat, y0, states)
    return y, dict(r_hat=r_hat, g=g, states=states)


def wkv_bwd(r, ld, k, v, a, b, saved, dy, *, name):
    s_len = r.shape[0]
    c = WKV_CHUNK
    n = s_len // c
    per = min(WKV_CHUNKS_PER_STEP, n)
    rows = pl.BlockSpec((c * per, RW), lambda i: (i, 0))
    mats = pl.BlockSpec((per, RH, RN, RN), lambda i: (i, 0, 0, 0))
    rows_t, mats_t = jax.ShapeDtypeStruct((s_len, RW), f32), jax.ShapeDtypeStruct((n, RH, RN, RN), f32)

    def out_body(dy_ref, rh_ref, st_ref, drh_ref, dsy_ref):
        dyb = _load_chunk_heads(dy_ref, c, per)
        _store_chunk_heads(drh_ref, _hbnn(dyb, st_ref[...].reshape(per * RH, RN, RN)), c, per)
        dsy_ref[...] = _hbtn(dyb, _load_chunk_heads(rh_ref, c, per)).reshape(per, RH, RN, RN)

    d_rhat, ds_y = pl.pallas_call(
        out_body, name=name + "_out", grid=(n // per,), in_specs=[rows, rows, mats], out_specs=[rows, mats],
        out_shape=[rows_t, mats_t], compiler_params=_cparams(("parallel",)),
    )(dy, saved["r_hat"], saved["states"])

    def scan_body(dsy_ref, g_ref, st_ref, dg_ref, dz_ref, ds_sc):
        ds_sc[...] = jnp.zeros_like(ds_sc)

        @pl.loop(0, n)
        def _(i):
            cidx = n - 1 - i
            ds_next = ds_sc[...]
            dz_ref[cidx] = ds_next
            dg_ref[cidx] = _hbtn(st_ref[cidx], ds_next)
            ds_sc[...] = dsy_ref[cidx] + _hbnt(ds_next, g_ref[cidx])

    vm = pl.BlockSpec(memory_space=pltpu.VMEM)
    d_g, d_z = pl.pallas_call(
        scan_body, name=name + "_scan", in_specs=[vm, vm, vm], out_specs=[vm, vm], out_shape=[mats_t, mats_t],
        scratch_shapes=[pltpu.VMEM((RH, RN, RN), f32)],
        compiler_params=pltpu.CompilerParams(vmem_limit_bytes=VMEM_LIMIT),
    )(ds_y, saved["g"], saved["states"])

    def local_body(r_ref, ld_ref, k_ref, v_ref, a_ref, b_ref, drh_ref, dy_ref, dg_ref, dz_ref, *out_refs):
        _, vjp = jax.vjp(_wkv_local, *[_load_chunk_heads(ref, c, per)
                                       for ref in (r_ref, ld_ref, k_ref, v_ref, a_ref, b_ref)])
        grads = vjp((_load_chunk_heads(drh_ref, c, per), _load_chunk_heads(dy_ref, c, per),
                     dg_ref[...].reshape(per * RH, RN, RN), dz_ref[...].reshape(per * RH, RN, RN)))
        for o_ref, val in zip(out_refs, grads):
            _store_chunk_heads(o_ref, val, c, per)

    return pl.pallas_call(
        local_body, name=name + "_local", grid=(n // per,), in_specs=[rows] * 8 + [mats, mats], out_specs=[rows] * 6,
        out_shape=[rows_t] * 6, compiler_params=_cparams(("parallel",)),
    )(r, ld, k, v, a, b, d_rhat, dy, d_g, d_z)


def loss_head(y, target, *, tile, name):
    s_len, d = y.shape
    n = s_len // tile

    def body(y_ref, t_ref, dy_ref, l_ref):
        err = y_ref[...] - t_ref[...]
        dy_ref[...] = err * (1.0 / d)
        part = 0.5 * jnp.sum(jnp.mean(err * err, axis=-1, keepdims=True), axis=0, keepdims=True)

        @pl.when(pl.program_id(0) == 0)
        def _():
            l_ref[...] = jnp.zeros_like(l_ref)

        l_ref[...] += jnp.broadcast_to(part, l_ref.shape)

    bs = pl.BlockSpec((tile, d), lambda i: (i, 0))
    dy, l = pl.pallas_call(
        body, name=name, grid=(n,), in_specs=[bs, bs],
        out_specs=[bs, pl.BlockSpec((8, 128), lambda i: (0, 0))],
        out_shape=[jax.ShapeDtypeStruct((s_len, d), f32), jax.ShapeDtypeStruct((8, 128), f32)],
        compiler_params=_cparams(("arbitrary",)),
    )(y, target)
    return l[0, 0], dy


def _adamw_update(w, g, m, v):
    mn = B1 * m + (1.0 - B1) * g
    vn = B2 * v + (1.0 - B2) * (g * g)
    delta = -LR * ((mn / (1.0 - B1 ** STEP)) / (jnp.sqrt(vn / (1.0 - B2 ** STEP)) + EPS) + WD * w)
    return delta, mn, vn


def adamw(w, g, m, v, *, name):
    rows, cols = w.shape
    tile = _pick(rows, max(8, (2 * 1024 * 1024 // (4 * cols)) // 8 * 8), 8)

    def body(w_ref, g_ref, m_ref, v_ref, g_out, d_out, m_out, v_out):
        gv = g_ref[...]
        d_out[...], m_out[...], v_out[...] = _adamw_update(w_ref[...], gv, m_ref[...], v_ref[...])
        g_out[...] = gv

    bs = pl.BlockSpec((tile, cols), lambda i: (i, 0))
    return pl.pallas_call(
        body, name=name, grid=(rows // tile,), in_specs=[bs] * 4, out_specs=[bs] * 4,
        out_shape=[jax.ShapeDtypeStruct((rows, cols), f32)] * 4, compiler_params=_cparams(("parallel",)),
    )(w, g, m, v)


def adamw_many(ws, gs, ms, vs, *, name):
    n = len(ws)

    def body(*refs):
        ins, outs = refs[:4 * n], refs[4 * n:]
        for i in range(n):
            outs[i][...], outs[n + i][...], outs[2 * n + i][...] = _adamw_update(
                ins[i][...], ins[n + i][...], ins[2 * n + i][...], ins[3 * n + i][...])

    vm = pl.BlockSpec(memory_space=pltpu.VMEM)
    res = pl.pallas_call(
        body, name=name, in_specs=[vm] * (4 * n), out_specs=[vm] * (3 * n),
        out_shape=[jax.ShapeDtypeStruct(w.shape, f32) for w in ws] * 3,
        compiler_params=pltpu.CompilerParams(vmem_limit_bytes=VMEM_LIMIT),
    )(*ws, *gs, *ms, *vs)
    return list(res[:n]), list(res[n:2 * n]), list(res[2 * n:])


def _place():
    return lax.axis_index("x"), lax.axis_index("y"), lax.axis_index("c")


_ANY = pl.BlockSpec(memory_space=pl.ANY)


def _peer_chips(x, y):
    return [(1 - x, y), (x, 1 - y), (1 - x, 1 - y)]


def gather_weights(shards, *, name):
    nk = len(shards)

    def body(*refs):
        srcs, outs = refs[:nk], refs[nk:2 * nk]
        ici_send, ici_recv, d2d_send, d2d_recv = refs[2 * nk + 1:]
        x, y, c = _place()
        me = 2 * x + y
        peers = _peer_chips(x, y)
        pending = []
        for k in range(nk):
            half = srcs[k].shape[0] // 2
            mine = pl.ds(c * half, half)
            for p, (px, py) in enumerate(peers):
                cp = pltpu.make_async_remote_copy(
                    src_ref=srcs[k].at[mine], dst_ref=outs[k].at[me, mine], send_sem=ici_send.at[k, p],
                    recv_sem=ici_recv.at[k, p], device_id=(px, py, c), device_id_type=MESH)
                cp.start()
                pending.append(cp)
        for k in range(nk):
            half = srcs[k].shape[0] // 2
            mine = pl.ds(c * half, half)
            for p, (px, py) in enumerate(peers):
                landed = outs[k].at[2 * px + py, mine]
                pltpu.make_async_remote_copy(
                    src_ref=srcs[k].at[mine], dst_ref=landed, send_sem=ici_send.at[k, p], recv_sem=ici_recv.at[k, p],
                    device_id=(px, py, c), device_id_type=MESH).wait_recv()
                fwd = pltpu.make_async_remote_copy(
                    src_ref=landed, dst_ref=landed, send_sem=d2d_send.at[k, p], recv_sem=d2d_recv.at[k, p],
                    device_id=(x, y, 1 - c), device_id_type=MESH)
                fwd.start()
                pending.append(fwd)
        for k in range(nk):
            half = srcs[k].shape[0] // 2
            other = pl.ds((1 - c) * half, half)
            for p, (px, py) in enumerate(peers):
                theirs = outs[k].at[2 * px + py, other]
                pltpu.make_async_remote_copy(
                    src_ref=theirs, dst_ref=theirs, send_sem=d2d_send.at[k, p], recv_sem=d2d_recv.at[k, p],
                    device_id=(x, y, 1 - c), device_id_type=MESH).wait_recv()
        for cp in pending:
            cp.wait_send()
        refs[2 * nk][...] = jnp.zeros_like(refs[2 * nk])

    sem = lambda *shape: pltpu.SemaphoreType.DMA(shape)
    res = pl.pallas_call(
        body, name=name, in_specs=[_ANY] * nk, out_specs=[_ANY] * nk + [pl.BlockSpec(memory_space=pltpu.VMEM)],
        out_shape=[jax.ShapeDtypeStruct((4,) + s.shape, s.dtype) for s in shards]
        + [jax.ShapeDtypeStruct((8, 128), f32)],
        scratch_shapes=[sem(nk, 3), sem(nk, 3), sem(nk, 3), sem(nk, 3)],
    )(*shards)
    return list(res[:nk]), res[nk]


_HBM = pl.BlockSpec(memory_space=pltpu.HBM)
_SEM = pl.BlockSpec(memory_space=pltpu.SEMAPHORE)


def _ici_half_copies(srcs, lands, send_sems, recv_sems, incoming):
    x, y, c = _place()
    me = 2 * x + y
    out = []
    for k in range(len(srcs)):
        half = srcs[k].shape[0] // 2
        mine = pl.ds(c * half, half)
        for p, (px, py) in enumerate(_peer_chips(x, y)):
            out.append(pltpu.make_async_remote_copy(
                src_ref=srcs[k].at[mine], dst_ref=lands[k].at[(2 * px + py) if incoming else me, mine],
                send_sem=send_sems.at[3 * k + p], recv_sem=recv_sems.at[3 * k + p], device_id=(px, py, c),
                device_id_type=MESH))
    return out


def gather_start(shards, *, name):
    nk = len(shards)

    def body(*refs):
        srcs, lands = refs[:nk], refs[nk:2 * nk]
        send_sems, recv_sems = refs[2 * nk], refs[2 * nk + 1]
        token = refs[-1]
        for outgoing in _ici_half_copies(srcs, lands, send_sems, recv_sems, incoming=False):
            outgoing.start()
        token[...] = jnp.zeros_like(token)

    lands = [pltpu.with_memory_space_constraint(lax.empty((4,) + s.shape, s.dtype), pltpu.HBM) for s in shards]
    res = pl.pallas_call(
        body, name=name,
        out_shape=(pltpu.SemaphoreType.DMA((3 * nk,)), pltpu.SemaphoreType.DMA((3 * nk,)),
                   *[pltpu.HBM(s.shape, s.dtype) for s in shards], *[pltpu.HBM(z.shape, z.dtype) for z in lands],
                   jax.ShapeDtypeStruct((8, 128), f32)),
        in_specs=[_HBM] * (2 * nk), out_specs=(_SEM, _SEM, *[_HBM] * (2 * nk), pl.BlockSpec(memory_space=pltpu.VMEM)),
        input_output_aliases={k: 2 + k for k in range(2 * nk)},
        compiler_params=pltpu.CompilerParams(has_side_effects=pltpu.SideEffectType.DATAFLOW_SIDE_EFFECTING),
    )(*[pltpu.with_memory_space_constraint(s, pltpu.HBM) for s in shards], *lands)
    return res[0], res[1], list(res[2:2 + nk]), list(res[2 + nk:2 + 2 * nk]), res[-1]


def gather_wait(send_sems, recv_sems, shards, lands, after, *, name):
    nk = len(shards)

    def body(*refs):
        srcs, zones = refs[:nk], refs[nk:2 * nk]
        for outgoing in _ici_half_copies(srcs, zones, refs[2 * nk], refs[2 * nk + 1], incoming=False):
            outgoing.wait_send()
        for landed in _ici_half_copies(srcs, zones, refs[2 * nk], refs[2 * nk + 1], incoming=True):
            landed.wait_recv()

    res = pl.pallas_call(
        body, name=name,
        out_shape=(*[pltpu.HBM(s.shape, s.dtype) for s in shards], *[pltpu.HBM(z.shape, z.dtype) for z in lands]),
        in_specs=[_HBM] * (2 * nk) + [_SEM, _SEM, _ANY], out_specs=tuple([_HBM] * (2 * nk)),
        input_output_aliases={k: k for k in range(2 * nk)},
        compiler_params=pltpu.CompilerParams(has_side_effects=pltpu.SideEffectType.DATAFLOW_SIDE_EFFECTING),
    )(*shards, *lands, send_sems, recv_sems, after)
    return list(res[:nk]), list(res[nk:])


def gather_forward(lands, *, name):
    nk = len(lands)

    def body(*refs):
        zones = refs[nk:2 * nk]
        send_sems, recv_sems = refs[2 * nk:]
        x, y, c = _place()
        sends = []
        for k in range(nk):
            half = zones[k].shape[1] // 2
            for p, (px, py) in enumerate(_peer_chips(x, y)):
                landed = zones[k].at[2 * px + py, pl.ds(c * half, half)]
                cp = pltpu.make_async_remote_copy(
                    src_ref=landed, dst_ref=landed, send_sem=send_sems.at[k, p], recv_sem=recv_sems.at[k, p],
                    device_id=(x, y, 1 - c), device_id_type=MESH)
                cp.start()
                sends.append(cp)
        for k in range(nk):
            half = zones[k].shape[1] // 2
            for p, (px, py) in enumerate(_peer_chips(x, y)):
                theirs = zones[k].at[2 * px + py, pl.ds((1 - c) * half, half)]
                pltpu.make_async_remote_copy(
                    src_ref=theirs, dst_ref=theirs, send_sem=send_sems.at[k, p], recv_sem=recv_sems.at[k, p],
                    device_id=(x, y, 1 - c), device_id_type=MESH).wait_recv()
        for cp in sends:
            cp.wait_send()

    return pl.pallas_call(
        body, name=name, in_specs=[_ANY] * nk, out_specs=[_ANY] * nk,
        out_shape=[jax.ShapeDtypeStruct(z.shape, z.dtype) for z in lands],
        input_output_aliases={k: k for k in range(nk)},
        scratch_shapes=[pltpu.SemaphoreType.DMA((nk, 3)), pltpu.SemaphoreType.DMA((nk, 3))],
    )(*lands)


def _to_sibling_copies(srcs, lands, send_sems, recv_sems):
    x, y, c = _place()
    out = []
    for k in range(len(srcs)):
        half = srcs[k].shape[1] // 2
        out.append(pltpu.make_async_remote_copy(
            src_ref=srcs[k].at[:, pl.ds((1 - c) * half, half), :], dst_ref=lands[k], send_sem=send_sems.at[k],
            recv_sem=recv_sems.at[k], device_id=(x, y, 1 - c), device_id_type=MESH))
    return out


def sibling_start(parts, *, name):
    nk = len(parts)

    def body(*refs):
        for cp in _to_sibling_copies(refs[:nk], refs[nk:2 * nk], refs[2 * nk], refs[2 * nk + 1]):
            cp.start()
        refs[-1][...] = jnp.zeros_like(refs[-1])

    lands = [pltpu.with_memory_space_constraint(lax.empty((4, p.shape[1] // 2, p.shape[2]), p.dtype), pltpu.HBM)
             for p in parts]
    res = pl.pallas_call(
        body, name=name,
        out_shape=(pltpu.SemaphoreType.DMA((nk,)), pltpu.SemaphoreType.DMA((nk,)),
                   *[pltpu.HBM(p.shape, p.dtype) for p in parts], *[pltpu.HBM(z.shape, z.dtype) for z in lands],
                   jax.ShapeDtypeStruct((8, 128), f32)),
        in_specs=[_HBM] * (2 * nk), out_specs=(_SEM, _SEM, *[_HBM] * (2 * nk), pl.BlockSpec(memory_space=pltpu.VMEM)),
        input_output_aliases={k: 2 + k for k in range(2 * nk)},
        compiler_params=pltpu.CompilerParams(has_side_effects=pltpu.SideEffectType.DATAFLOW_SIDE_EFFECTING),
    )(*[pltpu.with_memory_space_constraint(p, pltpu.HBM) for p in parts], *lands)
    return res[0], res[1], list(res[2:2 + nk]), list(res[2 + nk:2 + 2 * nk]), res[-1]


def sibling_wait(send_sems, recv_sems, parts, lands, after, *, name):
    nk = len(parts)

    def body(*refs):
        copies = _to_sibling_copies(refs[:nk], refs[nk:2 * nk], refs[2 * nk], refs[2 * nk + 1])
        for cp in copies:
            cp.wait_send()
        for cp in copies:
            cp.wait_recv()

    res = pl.pallas_call(
        body, name=name,
        out_shape=(*[pltpu.HBM(p.shape, p.dtype) for p in parts], *[pltpu.HBM(z.shape, z.dtype) for z in lands]),
        in_specs=[_HBM] * (2 * nk) + [_SEM, _SEM, _ANY], out_specs=tuple([_HBM] * (2 * nk)),
        input_output_aliases={k: k for k in range(2 * nk)},
        compiler_params=pltpu.CompilerParams(has_side_effects=pltpu.SideEffectType.DATAFLOW_SIDE_EFFECTING),
    )(*parts, *lands, send_sems, recv_sems, after)
    return list(res[:nk]), list(res[nk:])


def pair_sum(part, theirs, core, *, name):
    _, rows, cols = part.shape
    half = rows // 2
    tile = _pick(half, max(16, (1 << 20) // (4 * cols) // 16 * 16), 16)
    per = half // tile

    def body(c_ref, p_ref, t_ref, o_ref):
        o_ref[...] = (p_ref[...].astype(f32) + t_ref[...].astype(f32)).astype(bf16)

    grid_spec = pltpu.PrefetchScalarGridSpec(
        num_scalar_prefetch=1, grid=(4, per),
        in_specs=[pl.BlockSpec((1, tile, cols), lambda j, i, c_ref: (j, c_ref[0] * per + i, 0)),
                  pl.BlockSpec((1, tile, cols), lambda j, i, c_ref: (j, i, 0))],
        out_specs=pl.BlockSpec((1, tile, cols), lambda j, i, c_ref: (j, i, 0)))
    return pl.pallas_call(
        body, name=name, grid_spec=grid_spec, out_shape=jax.ShapeDtypeStruct((4, half, cols), bf16),
        compiler_params=_cparams(("parallel", "parallel")),
    )(core, part, theirs)


def _all_to_all_copies(srcs, lands, send_sems, recv_sems, incoming):
    x, y, c = _place()
    me = 2 * x + y
    out = []
    for k in range(len(srcs)):
        for p, (px, py) in enumerate(_peer_chips(x, y)):
            peer = 2 * px + py
            out.append(pltpu.make_async_remote_copy(
                src_ref=srcs[k].at[peer], dst_ref=lands[k].at[peer if incoming else me],
                send_sem=send_sems.at[3 * k + p], recv_sem=recv_sems.at[3 * k + p], device_id=(px, py, c),
                device_id_type=MESH))
    return out


def scatter_start(parts, *, name):
    nk = len(parts)

    def body(*refs):
        srcs, lands = refs[:nk], refs[nk:2 * nk]
        for outgoing in _all_to_all_copies(srcs, lands, refs[2 * nk], refs[2 * nk + 1], incoming=False):
            outgoing.start()
        refs[-1][...] = jnp.zeros_like(refs[-1])

    lands = [pltpu.with_memory_space_constraint(lax.empty(p.shape, p.dtype), pltpu.HBM) for p in parts]
    res = pl.pallas_call(
        body, name=name,
        out_shape=(pltpu.SemaphoreType.DMA((3 * nk,)), pltpu.SemaphoreType.DMA((3 * nk,)),
                   *[pltpu.HBM(p.shape, p.dtype) for p in parts], *[pltpu.HBM(p.shape, p.dtype) for p in parts],
                   jax.ShapeDtypeStruct((8, 128), f32)),
        in_specs=[_HBM] * (2 * nk), out_specs=(_SEM, _SEM, *[_HBM] * (2 * nk), pl.BlockSpec(memory_space=pltpu.VMEM)),
        input_output_aliases={k: 2 + k for k in range(2 * nk)},
        compiler_params=pltpu.CompilerParams(has_side_effects=pltpu.SideEffectType.DATAFLOW_SIDE_EFFECTING),
    )(*[pltpu.with_memory_space_constraint(p, pltpu.HBM) for p in parts], *lands)
    return res[0], res[1], list(res[2:2 + nk]), list(res[2 + nk:2 + 2 * nk]), res[-1]


def scatter_wait(send_sems, recv_sems, parts, lands, after, *, name):
    nk = len(parts)

    def body(*refs):
        srcs, zones = refs[:nk], refs[nk:2 * nk]
        for outgoing in _all_to_all_copies(srcs, zones, refs[2 * nk], refs[2 * nk + 1], incoming=False):
            outgoing.wait_send()
        for landed in _all_to_all_copies(srcs, zones, refs[2 * nk], refs[2 * nk + 1], incoming=True):
            landed.wait_recv()

    res = pl.pallas_call(
        body, name=name,
        out_shape=(*[pltpu.HBM(p.shape, p.dtype) for p in parts], *[pltpu.HBM(z.shape, z.dtype) for z in lands]),
        in_specs=[_HBM] * (2 * nk) + [_SEM, _SEM, _ANY], out_specs=tuple([_HBM] * (2 * nk)),
        input_output_aliases={k: k for k in range(2 * nk)},
        compiler_params=pltpu.CompilerParams(has_side_effects=pltpu.SideEffectType.DATAFLOW_SIDE_EFFECTING),
    )(*parts, *lands, send_sems, recv_sems, after)
    return list(res[:nk]), list(res[nk:])


def join_halves(bufs, layout, *, name):
    nk, nb = len(layout), len(bufs)

    def body(*refs):
        outs = refs[nb:2 * nb]
        send_sems, recv_sems = refs[2 * nb:]
        x, y, c = _place()
        pending = []
        for k, (o, off, rows) in enumerate(layout):
            half = rows // 2
            mine = outs[o].at[pl.ds(off + c * half, half), :]
            cp = pltpu.make_async_remote_copy(
                src_ref=mine, dst_ref=mine, send_sem=send_sems.at[k], recv_sem=recv_sems.at[k],
                device_id=(x, y, 1 - c), device_id_type=MESH)
            cp.start()
            pending.append(cp)
        for k, (o, off, rows) in enumerate(layout):
            half = rows // 2
            theirs = outs[o].at[pl.ds(off + (1 - c) * half, half), :]
            pltpu.make_async_remote_copy(
                src_ref=theirs, dst_ref=theirs, send_sem=send_sems.at[k], recv_sem=recv_sems.at[k],
                device_id=(x, y, 1 - c), device_id_type=MESH).wait_recv()
        for cp in pending:
            cp.wait_send()

    return pl.pallas_call(
        body, name=name, in_specs=[_ANY] * nb, out_specs=[_ANY] * nb,
        out_shape=[jax.ShapeDtypeStruct(b.shape, b.dtype) for b in bufs],
        input_output_aliases={o: o for o in range(nb)},
        scratch_shapes=[pltpu.SemaphoreType.DMA((nk,)), pltpu.SemaphoreType.DMA((nk,))],
    )(*bufs)


def place_slab(dest, src, index, *, name):
    rows, cols = src.shape
    tile = _pick(rows, max(16, (1 << 20) // (src.dtype.itemsize * cols) // 16 * 16), 16)

    def body(i_ref, s_ref, d_ref, o_ref):
        del i_ref, d_ref
        o_ref[0] = s_ref[...]

    grid_spec = pltpu.PrefetchScalarGridSpec(
        num_scalar_prefetch=1, grid=(rows // tile,),
        in_specs=[pl.BlockSpec((tile, cols), lambda i, idx: (i, 0)), _ANY],
        out_specs=pl.BlockSpec((1, tile, cols), lambda i, idx: (idx[0], i, 0)))
    return pl.pallas_call(
        body, name=name, grid_spec=grid_spec, out_shape=jax.ShapeDtypeStruct(dest.shape, dest.dtype),
        input_output_aliases={2: 0}, compiler_params=_cparams(("parallel",)),
    )(index, src, dest)


def _broadcast_copies(src, land, send_sems, recv_sems, incoming):
    x, y, c = _place()
    me = 4 * x + 2 * y + c
    out = []
    for m in range(1, 8):
        px, py, pc = x ^ (m >> 2), y ^ ((m >> 1) & 1), c ^ (m & 1)
        out.append(pltpu.make_async_remote_copy(
            src_ref=src, dst_ref=land.at[(4 * px + 2 * py + pc) if incoming else me], send_sem=send_sems.at[m - 1],
            recv_sem=recv_sems.at[m - 1], device_id=(px, py, pc), device_id_type=MESH))
    return out


def broadcast_start(src, *, name):
    def body(s_ref, l_ref, send_sems, recv_sems, s_thru, l_thru, token):
        for outgoing in _broadcast_copies(s_ref, l_ref, send_sems, recv_sems, incoming=False):
            outgoing.start()
        token[...] = jnp.zeros_like(token)

    land = pltpu.with_memory_space_constraint(lax.empty((8,) + src.shape, src.dtype), pltpu.HBM)
    return pl.pallas_call(
        body, name=name,
        out_shape=(pltpu.SemaphoreType.DMA((7,)), pltpu.SemaphoreType.DMA((7,)), pltpu.HBM(src.shape, src.dtype),
                   pltpu.HBM(land.shape, land.dtype), jax.ShapeDtypeStruct((8, 128), f32)),
        in_specs=[_HBM, _HBM], out_specs=(_SEM, _SEM, _HBM, _HBM, pl.BlockSpec(memory_space=pltpu.VMEM)),
        input_output_aliases={0: 2, 1: 3},
        compiler_params=pltpu.CompilerParams(has_side_effects=pltpu.SideEffectType.DATAFLOW_SIDE_EFFECTING),
    )(pltpu.with_memory_space_constraint(src, pltpu.HBM), land)


def broadcast_wait(send_sems, recv_sems, src, land, after, *, name):
    def body(s_ref, l_ref, send_sems, recv_sems, after_ref, s_out, l_out):
        for outgoing in _broadcast_copies(s_ref, l_ref, send_sems, recv_sems, incoming=False):
            outgoing.wait_send()
        for landed in _broadcast_copies(s_ref, l_ref, send_sems, recv_sems, incoming=True):
            landed.wait_recv()

    return pl.pallas_call(
        body, name=name, out_shape=(pltpu.HBM(src.shape, src.dtype), pltpu.HBM(land.shape, land.dtype)),
        in_specs=[_HBM, _HBM, _SEM, _SEM, _ANY], out_specs=(_HBM, _HBM), input_output_aliases={0: 0, 1: 1},
        compiler_params=pltpu.CompilerParams(has_side_effects=pltpu.SideEffectType.DATAFLOW_SIDE_EFFECTING),
    )(src, land, send_sems, recv_sems, after)


def sum8(land, own, device, *, name):
    _, rows, cols = land.shape

    def body(d_ref, l_ref, o_ref, out_ref):
        mine = o_ref[...]
        acc = jnp.where(d_ref[0] == 0, mine, l_ref[0])
        for d in range(1, 8):
            acc = acc + jnp.where(d_ref[0] == d, mine, l_ref[d])
        out_ref[...] = acc

    grid_spec = pltpu.PrefetchScalarGridSpec(
        num_scalar_prefetch=1, grid=(1,),
        in_specs=[pl.BlockSpec((8, rows, cols), lambda i, d_ref: (0, 0, 0)),
                  pl.BlockSpec((rows, cols), lambda i, d_ref: (0, 0))],
        out_specs=pl.BlockSpec((rows, cols), lambda i, d_ref: (0, 0)))
    return pl.pallas_call(
        body, name=name, grid_spec=grid_spec, out_shape=jax.ShapeDtypeStruct((rows, cols), f32),
        compiler_params=_cparams(("arbitrary",)),
    )(device, land, own)


def sum4_into(arrived, own, dest, where, *, layer, total_rows, name):
    _, rows, cols = arrived.shape
    tile = _pick(rows, max(16, (1 << 20) // (4 * cols) // 16 * 16), 16)
    per = rows // tile

    def body(w_ref, a_ref, own_ref, *rest):
        mine = own_ref[0].astype(f32)
        p = [jnp.where(w_ref[1] == j, mine, a_ref[j].astype(f32)) for j in range(4)]
        rest[-1][...] = ((p[0] + p[1]) + p[2]) + p[3]

    grid_spec = pltpu.PrefetchScalarGridSpec(
        num_scalar_prefetch=1, grid=(per,),
        in_specs=[pl.BlockSpec((4, tile, cols), lambda i, w_ref: (0, i, 0)),
                  pl.BlockSpec((1, tile, cols), lambda i, w_ref: (w_ref[1], i, 0))] + ([] if dest is None else [_ANY]),
        out_specs=pl.BlockSpec((tile, cols), lambda i, w_ref: ((2 * layer + w_ref[0]) * per + i, 0)))
    return pl.pallas_call(
        body, name=name, grid_spec=grid_spec, out_shape=jax.ShapeDtypeStruct((total_rows, cols), f32),
        input_output_aliases={} if dest is None else {3: 0}, compiler_params=_cparams(("parallel",)),
    )(where, arrived, own, *([] if dest is None else [dest]))


def _consts():
    idx = np.arange(RW)
    bd = (idx[:, None] // RN == idx[None, :] // RN).astype(np.float32)
    rot = np.zeros((DQK, DQK), np.float32)
    half = ROPE // 2
    rot[NOPE + half + np.arange(half), NOPE + np.arange(half)] = -1.0
    rot[NOPE + np.arange(half), NOPE + half + np.arange(half)] = 1.0
    return jnp.asarray(bd), jnp.asarray(rot)


def _rope_tables(positions):
    freqs = ROPE_THETA ** (-(jnp.arange(ROPE // 2, dtype=f32) * 2.0 / ROPE))
    ang = positions.astype(f32)[:, None] * freqs
    cos, sin = jnp.cos(ang), jnp.sin(ang)
    ones = jnp.ones((positions.shape[0], NOPE), f32)
    return (jnp.concatenate([ones, cos, cos], axis=-1), jnp.concatenate([0.0 * ones, sin, sin], axis=-1))


STAGES = (("w_in",), ("mla_wq_b", "mla_wkv_b", "mla_w_o", "rwkv_w_o", "conv_w_o", "w_out"), ("w_up", "w_down"))


def derive_stage(stage, w):
    if stage == 0:
        w_in = w["w_in"]
        pad = jnp.zeros((D, MLA_PAD - MLA_COLS), w_in.dtype)
        return dict(gate=w_in[:, :GATE], mla=jnp.concatenate([w_in[:, GATE:GATE + MLA_COLS], pad], axis=1),
                    rw=w_in[:, GATE + MLA_COLS:GATE + MLA_COLS + 4 * RW], cv=w_in[:, GATE + MLA_COLS + 4 * RW:])
    if stage == 1:
        wq = jnp.pad(w["mla_wq_b"].reshape(QL, MLA_H, DQK), ((0, 0), (0, 0), (0, Q_HEAD_PAD - DQK)))
        return dict(wq=wq.reshape(QL, MLA_H * Q_HEAD_PAD), wkv=w["mla_wkv_b"], wo=w["mla_w_o"], rwo=w["rwkv_w_o"],
                    cvo=w["conv_w_o"], out=w["w_out"])
    return dict(up=w["w_up"], down=w["w_down"])


W_IN_WINDOW_TILE = (0, 10, 21, 31)
W_IN_WINDOW = 1664
W_IN_SHARD = 1384


def w_in_window_cols(win, chip):
    gap = MLA_PAD - MLA_COLS
    branches = []
    for j in range(4):
        lo, hi = W_IN_SHARD * j, W_IN_SHARD * (j + 1)
        base = 128 * W_IN_WINDOW_TILE[j]
        cut = GATE + MLA_COLS
        if hi <= cut:
            branches.append(lambda w, a=lo - base: w[:, a:a + W_IN_SHARD])
        elif lo >= cut:
            branches.append(lambda w, a=lo + gap - base: w[:, a:a + W_IN_SHARD])
        else:
            branches.append(lambda w, a=lo - base, n1=cut - lo, b=cut + gap - base, n2=hi - cut:
                            jnp.concatenate([w[:, a:a + n1], w[:, b:b + n2]], axis=1))
    return lax.switch(chip, branches, win)


def chip_major_grads(stage, g):
    if stage == 0:
        padded = jnp.concatenate([g["gate"], g["mla"], g["rw"], g["cv"]], axis=1)
        return dict(w_in=jnp.stack([padded[:, 128 * t:128 * t + W_IN_WINDOW] for t in W_IN_WINDOW_TILE]))
    if stage == 1:
        wq = g["wq"].reshape(QL, MLA_H, Q_HEAD_PAD)[:, :, :DQK].reshape(QL, 4, -1).transpose(1, 0, 2)
        return dict(mla_wq_b=wq, mla_wkv_b=g["wkv"], mla_w_o=g["wo"], rwkv_w_o=g["rwo"], conv_w_o=g["cvo"],
                    w_out=g["out"].reshape(4, D // 4, D))
    return dict(w_up=g["up"], w_down=g["down"].reshape(4, DFF // 4, D))


def _row(v):
    return v.reshape(1, -1)


def local_step(x, positions, target, w, sm, big_of=None, on_grads=None, on_small=None):
    if big_of is None:
        big_of = lambda l, stage, _after: {n: w[n][l] for n in STAGES[stage]}
    if on_grads is None:
        on_grads = lambda l, stage, slabs: None
    if on_small is None:
        on_small = lambda l, layer_small: None
    s_len = x.shape[0]
    t_row = _pick(s_len, 256, 8)
    t_wide = _pick(s_len, 256, 8)
    t_norm = _pick(s_len, 512, 8)
    bd, rot = _consts()
    cos, sin = _rope_tables(positions)
    sds = lambda *shape: jax.ShapeDtypeStruct(shape, f32)
    sdb = lambda *shape: jax.ShapeDtypeStruct(shape, bf16)
    saved = []
    v_first = None
    for l in range(DEPTH):
        tag = f"l{l}_"
        lw = derive_stage(0, big_of(l, 0, x))
        vres = l > 0
        p_norm1 = [_row(sm["attn_norm"][l])]
        (h,) = rows_fwd(_fn_norm, [x], p_norm1, [], [sds(s_len, D)], tile=t_norm, name=tag + "norm1")
        gate = mm(h, lw["gate"], name=tag + "proj_gate")
        mla = mm(h, lw["mla"], name=tag + "proj_mla")
        rwc = mm(h, lw["rw"], name=tag + "proj_rwkv")
        cvc = mm(h, lw["cv"], name=tag + "proj_conv")
        lw.update(derive_stage(1, big_of(l, 1, cvc)))
        p_mla = [_row(sm["mla_q_a_norm"][l]), _row(sm["mla_kv_a_norm"][l])]
        qn, kvn, kpe = rows_fwd(_fn_mla_prep, [mla], p_mla, [], [sdb(s_len, QL), sdb(s_len, KVL), sds(s_len, 128)],
                                tile=t_row, name=tag + "mla_prep")
        q_flat = mm(qn, lw["wq"], name=tag + "q_proj")
        kv_flat = mm(kvn, lw["wkv"], name=tag + "kv_proj")
        p_qk = [_row(sm["mla_q_norm"][l]), _row(sm["mla_k_norm"][l])]
        q, k, vv = rows_fwd(_fn_qk_post, [q_flat, kv_flat, kpe, cos, sin], p_qk, [rot],
                            [sds(MLA_H, s_len, DQK), sds(MLA_H, s_len, DQK), sds(MLA_H, s_len, DV)], tile=t_wide,
                            name=tag + "qk_post")
        o = attn_fwd(q, k, vv, tq=_pick(s_len, ATTN_Q_BLOCK, 8), name=tag + "attn")
        o_a = mm(o, lw["wo"], name=tag + "o_a")
        p_rw = [_row(sm["rwkv_mu"][l]), _row(sm["rwkv_w0"][l]), w["rwkv_w2"][l], _row(sm["rwkv_a0"][l]),
                w["rwkv_a2"][l], w["rwkv_g2"][l], _row(sm["rwkv_k_k"][l]), _row(sm["rwkv_k_a"][l])]
        rw_rows, rw_halos = [rwc], (0,)
        if vres:
            p_rw += [w["rwkv_v1"][l - 1], _row(sm["rwkv_v_mu"][l - 1]), _row(sm["rwkv_v0"][l - 1]), w["rwkv_v2"][l - 1]]
            rw_rows, rw_halos = [rwc, h, v_first], (0, 1)
        fn_prep = _make_fn_rwkv_prep(vres)
        r, ld, k2, v, an, bn, g = rows_fwd(fn_prep, rw_rows, p_rw, [bd], [sds(s_len, RW)] * 7, tile=t_row,
                                           name=tag + "rwkv_prep", halos=rw_halos)
        if not vres:
            v_first = v
        y, states = wkv_fwd(r, ld, k2, v, an, bn, name=tag + "wkv")
        p_post = [_row(sm["rwkv_ln_w"][l]), _row(sm["rwkv_ln_b"][l]), _row(sm["rwkv_r_k"][l])]
        (yb,) = rows_fwd(_fn_rwkv_post, [y, r, k2, v, g], p_post, [bd], [sdb(s_len, RW)], tile=t_row,
                         name=tag + "rwkv_post")
        o_b = mm(yb, lw["rwo"], name=tag + "o_b")
        p_cv = [w["conv_w"][l][q:q + 1] for q in range(3)]
        (yc,) = rows_fwd(_fn_conv, [cvc], p_cv, [], [sdb(s_len, CW)], tile=t_row, name=tag + "conv", halos=(0,))
        o_c = mm(yc, lw["cvo"], name=tag + "o_c")
        (merged,) = rows_fwd(_fn_merge, [gate, o_a, o_b, o_c], [], [], [sdb(s_len, D)], tile=t_wide,
                             name=tag + "merge")
        x1 = mm(merged, lw["out"], add=x, name=tag + "out_proj")
        lw.update(derive_stage(2, big_of(l, 2, x1)))
        p_norm2 = [_row(sm["mlp_norm"][l])]
        (h2,) = rows_fwd(_fn_norm, [x1], p_norm2, [], [sdb(s_len, D)], tile=t_norm, name=tag + "norm2")
        up, act = mm(h2, lw["up"], relu2_out=True, name=tag + "up")
        x2 = mm(act, lw["down"], add=x1, name=tag + "down")
        saved.append(dict(lw=lw, x=x, h=h, gate=gate, mla=mla, rwc=rwc, cvc=cvc, qn=qn, kvn=kvn, kpe=kpe,
                          q_flat=q_flat, kv_flat=kv_flat, vv=vv, q=q, k=k, o=o, o_a=o_a, r=r, ld=ld, k2=k2, v=v,
                          an=an, bn=bn, g=g, y=y, states=states, yb=yb, o_b=o_b, yc=yc, o_c=o_c, merged=merged,
                          x1=x1, h2=h2, up=up, act=act, p_norm1=p_norm1, p_mla=p_mla, p_qk=p_qk, p_rw=p_rw,
                          p_post=p_post, p_cv=p_cv, p_norm2=p_norm2, rw_rows=rw_rows, rw_halos=rw_halos,
                          fn_prep=fn_prep, v_first=v_first if vres else None))
        x = x2

    loss, dx = loss_head(x, target, tile=t_norm, name="loss_head")

    grads = {n: [None] * (DEPTH - 1 if n in ("rwkv_v1", "rwkv_v_mu", "rwkv_v0", "rwkv_v2") else DEPTH)
             for n in WEIGHTS}
    dv_first = None
    for l in reversed(range(DEPTH)):
        tag = f"b{l}_"
        sv = saved[l]
        lw = sv["lw"]
        vres = l > 0
        g_down = mm(sv["act"], dx, ta=True, out_dtype=bf16, name=tag + "g_down")
        dup = mm(dx, lw["down"], tb=True, act_grad=sv["up"], out_dtype=bf16, name=tag + "d_up")
        g_up = mm(sv["h2"], dup, ta=True, n_split=4, out_dtype=bf16, name=tag + "g_up")
        dh2 = mm(dup, lw["up"], tb=True, name=tag + "d_h2")
        slabs = chip_major_grads(2, dict(up=g_up, down=g_down))
        token = on_grads(l, 2, slabs)
        p_norm2 = sv["p_norm2"] if token is None else [sv["p_norm2"][0] + token[0, 0]]
        (dx1,), (g_n2,) = rows_bwd(_fn_norm, [sv["x1"]], p_norm2, [], [[dh2]], tile=t_norm,
                                   name=tag + "norm2", extra={0: [dx]})
        g_out = mm(sv["merged"], dx1, ta=True, out_dtype=bf16, name=tag + "g_out")
        dmerged = mm(dx1, lw["out"], tb=True, name=tag + "d_merged")
        (dgate, do_a, do_b, do_c), _ = rows_bwd(_fn_merge, [sv["gate"], sv["o_a"], sv["o_b"], sv["o_c"]], [], [],
                                                [[dmerged]], tile=t_wide, name=tag + "merge",
                                                grad_dtypes=[bf16] * 4)
        g_cvo = mm(sv["yc"], do_c, ta=True, n_split=4, out_dtype=bf16, name=tag + "g_cvo")
        dyc = mm(do_c, lw["cvo"], tb=True, name=tag + "d_yc")
        (dcvc,), g_cw = rows_bwd(_fn_conv, [sv["cvc"]], sv["p_cv"], [], [[dyc]], tile=t_row, name=tag + "conv",
                                    halos=(0,))
        g_rwo = mm(sv["yb"], do_b, ta=True, n_split=4, out_dtype=bf16, name=tag + "g_rwo")
        dyb = mm(do_b, lw["rwo"], tb=True, name=tag + "d_yb")
        (dy, dr_p, dk_p, dv_p, dg), g_post = rows_bwd(
            _fn_rwkv_post, [sv["y"], sv["r"], sv["k2"], sv["v"], sv["g"]], sv["p_post"], [bd], [[dyb]], tile=t_row,
            name=tag + "rwkv_post")
        dr_s, dld, dk_s, dv_s, dan, dbn = wkv_bwd(sv["r"], sv["ld"], sv["k2"], sv["v"], sv["an"], sv["bn"],
                                                  sv["states"], dy, name=tag + "wkv")
        dv_list = [dv_s, dv_p] + ([dv_first] if (not vres and dv_first is not None) else [])
        d_prep, g_prep = rows_bwd(
            sv["fn_prep"], sv["rw_rows"], sv["p_rw"], [bd],
            [[dr_s, dr_p], [dld], [dk_s, dk_p], dv_list, [dan], [dbn], [dg]], tile=t_row, name=tag + "rwkv_prep",
            halos=sv["rw_halos"])
        drwc = d_prep[0]
        dh_extra = []
        if vres:
            dh_extra = [d_prep[1]]
            dv_first = d_prep[2]
        g_wo = mm(sv["o"], do_a, ta=True, n_split=4, out_dtype=bf16, name=tag + "g_wo")
        do = mm(do_a, lw["wo"], tb=True, name=tag + "d_o")
        dq, dk, dvv = attn_bwd(sv["q"], sv["k"], sv["vv"], do, tq=_pick(s_len, ATTN_Q_BLOCK, 8), name=tag + "attn")
        (dq_flat, dkv_flat, dkpe), g_qk = rows_bwd(
            _fn_qk_post, [sv["q_flat"], sv["kv_flat"], sv["kpe"], cos, sin], sv["p_qk"], [rot], [[dq], [dk], [dvv]],
            tile=t_wide, name=tag + "qk_post", grad_rows=[0, 1, 2], grad_dtypes=[bf16, bf16, f32])
        g_wq = mm(sv["qn"], dq_flat, ta=True, out_dtype=bf16, name=tag + "g_wq")
        g_wkv = mm(sv["kvn"], dkv_flat, ta=True, n_split=4, out_dtype=bf16, name=tag + "g_wkv")
        dqn = mm(dq_flat, lw["wq"], tb=True, name=tag + "d_qn")
        dkvn = mm(dkv_flat, lw["wkv"], tb=True, name=tag + "d_kvn")
        slabs.update(chip_major_grads(1, dict(wq=g_wq, wkv=g_wkv, wo=g_wo, rwo=g_rwo, cvo=g_cvo, out=g_out)))
        token = on_grads(l, 1, {n: slabs[n] for n in STAGES[1]})
        p_mla = sv["p_mla"] if token is None else [sv["p_mla"][0] + token[0, 0], sv["p_mla"][1]]
        (dmla,), g_mla = rows_bwd(_fn_mla_prep, [sv["mla"]], p_mla, [], [[dqn], [dkvn], [dkpe]], tile=t_row,
                                  name=tag + "mla_prep", grad_dtypes=[bf16])
        g_gate = mm(sv["h"], dgate, ta=True, out_dtype=bf16, name=tag + "g_gate")
        g_mlaw = mm(sv["h"], dmla, ta=True, out_dtype=bf16, name=tag + "g_mla")
        g_rw = mm(sv["h"], drwc, ta=True, out_dtype=bf16, name=tag + "g_rw")
        g_cv = mm(sv["h"], dcvc, ta=True, out_dtype=bf16, name=tag + "g_cv")
        dh = mm(dgate, lw["gate"], tb=True, name=tag + "d_h_gate")
        dh = mm(dmla, lw["mla"], tb=True, add=dh, name=tag + "d_h_mla")
        dh = mm(drwc, lw["rw"], tb=True, add=dh, name=tag + "d_h_rw")
        dh = mm(dcvc, lw["cv"], tb=True, add=dh, name=tag + "d_h_cv")
        (dx,), (g_n1,) = rows_bwd(_fn_norm, [sv["x"]], sv["p_norm1"], [], [[dh] + dh_extra], tile=t_norm,
                                  name=tag + "norm1", extra={0: [dx1]})
        slabs.update(chip_major_grads(0, dict(gate=g_gate, mla=g_mlaw, rw=g_rw, cv=g_cv)))
        token = on_grads(l, 0, {n: slabs[n] for n in STAGES[0]})
        if token is not None and l > 0:
            dx = dx + token[0, 0]
        for n, val in slabs.items():
            grads[n][l] = val
        layer_small = [("attn_norm", l, g_n1), ("mlp_norm", l, g_n2), ("mla_q_a_norm", l, g_mla[0]),
                       ("mla_kv_a_norm", l, g_mla[1]), ("mla_q_norm", l, g_qk[0]), ("mla_k_norm", l, g_qk[1]),
                       ("rwkv_ln_w", l, g_post[0]), ("rwkv_ln_b", l, g_post[1]), ("rwkv_r_k", l, g_post[2]),
                       ("conv_w", l, jnp.concatenate(g_cw, axis=0))]
        layer_small += list(zip(["rwkv_mu", "rwkv_w0", "rwkv_w2", "rwkv_a0", "rwkv_a2", "rwkv_g2", "rwkv_k_k",
                                 "rwkv_k_a"], [l] * 8, g_prep[:8]))
        if vres:
            layer_small += list(zip(["rwkv_v1", "rwkv_v_mu", "rwkv_v0", "rwkv_v2"], [l - 1] * 4, g_prep[8:12]))
        for n, index, val in layer_small:
            grads[n][index] = val
        if l == 0:
            layer_small.append(("loss", 0, loss.reshape(1, 1)))
        token = on_small(l, layer_small)
        if token is not None and l > 0:
            dx = dx + token[0, 0]
    return loss, dx, grads


def _split3(a):
    hi = a.astype(bf16)
    r1 = a - hi.astype(f32)
    mid = r1.astype(bf16)
    lo = (r1 - mid.astype(f32)).astype(bf16)
    return hi, mid, lo


def _shard_axis(name):
    return 1 if name in ROW_SHARDED else 2


def _pack(pieces, width, dtype, row_align):
    flat = jnp.concatenate([p.reshape(-1).astype(dtype) for p in pieces])
    rows = -(-flat.shape[0] // width)
    rows = -(-rows // row_align) * row_align
    return jnp.pad(flat, (0, rows * width - flat.shape[0])).reshape(rows, width)


def _unpack(flat2d, shapes):
    flat = flat2d.reshape(-1)
    out, off = [], 0
    for shp in shapes:
        n = int(np.prod(shp))
        out.append(flat[off:off + n].reshape(shp))
        off += n
    return out


def kernel(x, positions, attn_norm, w_in, mla_q_a_norm, mla_wq_b, mla_kv_a_norm, mla_wkv_b, mla_q_norm, mla_k_norm, mla_w_o, rwkv_mu, rwkv_w0, rwkv_w2, rwkv_a0, rwkv_a2, rwkv_g2, rwkv_k_k, rwkv_k_a, rwkv_r_k, rwkv_ln_w, rwkv_ln_b, rwkv_w_o, rwkv_v1, rwkv_v_mu, rwkv_v0, rwkv_v2, conv_w, conv_w_o, w_out, mlp_norm, w_up, w_down, loss_target, m_attn_norm, m_w_in, m_mla_q_a_norm, m_mla_wq_b, m_mla_kv_a_norm, m_mla_wkv_b, m_mla_q_norm, m_mla_k_norm, m_mla_w_o, m_rwkv_mu, m_rwkv_w0, m_rwkv_w2, m_rwkv_a0, m_rwkv_a2, m_rwkv_g2, m_rwkv_k_k, m_rwkv_k_a, m_rwkv_r_k, m_rwkv_ln_w, m_rwkv_ln_b, m_rwkv_w_o, m_rwkv_v1, m_rwkv_v_mu, m_rwkv_v0, m_rwkv_v2, m_conv_w, m_conv_w_o, m_w_out, m_mlp_norm, m_w_up, m_w_down, v_attn_norm, v_w_in, v_mla_q_a_norm, v_mla_wq_b, v_mla_kv_a_norm, v_mla_wkv_b, v_mla_q_norm, v_mla_k_norm, v_mla_w_o, v_rwkv_mu, v_rwkv_w0, v_rwkv_w2, v_rwkv_a0, v_rwkv_a2, v_rwkv_g2, v_rwkv_k_k, v_rwkv_k_a, v_rwkv_r_k, v_rwkv_ln_w, v_rwkv_ln_b, v_rwkv_w_o, v_rwkv_v1, v_rwkv_v_mu, v_rwkv_v0, v_rwkv_v2, v_conv_w, v_conv_w_o, v_w_out, v_mlp_norm, v_w_up, v_w_down):
    args = dict(locals())
    wts = {n: args[n] for n in WEIGHTS}
    mom = {n: args["m_" + n] for n in WEIGHTS}
    var = {n: args["v_" + n] for n in WEIGHTS}
    chip = 2 * lax.axis_index("x") + lax.axis_index("y")
    core = lax.axis_index("c").astype(jnp.int32).reshape(1)

    med_names = [n for n in MED if n != "conv_w"]
    med_pieces = [wts[n] for n in med_names] + list(_split3(wts["conv_w"]))
    med_shapes = [p.shape for p in med_pieces]
    chip_idx = chip.astype(jnp.int32).reshape(1)
    shards_first = [wts[n][0].astype(bf16) for n in STAGES[0]] + [_pack(med_pieces, 128, bf16, 32)]
    got_first, token = gather_weights(shards_first, name="gather_l0_s0")
    got_first = [place_slab(g, s, chip_idx, name=f"place_own_l0_s0_{q}")
                 for q, (g, s) in enumerate(zip(got_first, shards_first))]
    in_flight = {}
    for key, names, l in (("l0_s1", STAGES[1], 0), ("l0_s2", STAGES[2], 0), ("l1", BIG, 1)):
        group = [wts[n][l].astype(bf16) for n in names]
        group[0] = group[0] + token[0, 0].astype(bf16)
        in_flight[key] = (names, gather_start(group, name="gather_start_" + key))
        token = in_flight[key][1][4]

    def whole_of(names, slabs):
        out = {}
        for n, by_chip in zip(names, slabs):
            _, rows, cols = by_chip.shape
            if n in ROW_SHARDED:
                out[n] = by_chip.reshape(4 * rows, cols)
            else:
                out[n] = by_chip.transpose(1, 0, 2).reshape(rows, 4 * cols)
        return out

    landed = {}

    def big_of(l, stage, after):
        if (l, stage) == (0, 0):
            return whole_of(STAGES[0], got_first)
        key = "l1" if l == 1 else f"l0_s{stage}"
        if key not in landed:
            names, (send_sems, recv_sems, thru, lands, _) = in_flight[key]
            thru, lands = gather_wait(send_sems, recv_sems, thru, lands, after, name="gather_wait_" + key)
            lands = gather_forward(lands, name="gather_forward_" + key)
            landed[key] = whole_of(names, [place_slab(g, s, chip_idx, name=f"place_own_{key}_{q}")
                                           for q, (g, s) in enumerate(zip(lands, thru))])
        return {n: landed[key][n] for n in STAGES[stage]}

    whole = {}
    per_chip = [_unpack(got_first[len(STAGES[0])][j], med_shapes) for j in range(4)]
    for q, n in enumerate(med_names):
        whole[n] = jnp.concatenate([per_chip[j][q] for j in range(4)], axis=_shard_axis(n)).astype(f32)
    base = len(med_names)
    cw_parts = [jnp.concatenate([per_chip[j][base + t] for j in range(4)], axis=2).astype(f32) for t in range(3)]
    whole["conv_w"] = (cw_parts[0] + cw_parts[1]) + cw_parts[2]
    small = {n: wts[n] for n in SMALL}
    small["rwkv_r_k"] = wts["rwkv_r_k"].reshape(DEPTH, RW)

    exchanges, to_sibling = [], []

    def chip_sums_on_their_way(after):
        tokens = []
        while to_sibling:
            l, names, tag, (send_sems, recv_sems, thru, lands, _) = to_sibling.pop(0)
            parts, theirs = sibling_wait(send_sems, recv_sems, thru, lands, after, name="sibling_wait_" + tag)
            chip_sums = [pair_sum(s, t, core, name=f"pair_sum_{n}_{l}") for n, s, t in zip(names, parts, theirs)]
            started = scatter_start(chip_sums, name="scatter_start_" + tag)
            exchanges.append((l, names, tag, started))
            tokens.append(started[4])
        return tokens

    def on_grads(l, stage, slabs):
        names = STAGES[stage]
        tag = f"l{l}_s{stage}"
        parts = [slabs[n] for n in names]
        tokens = chip_sums_on_their_way(parts[0])
        started = sibling_start(parts, name="sibling_start_" + tag)
        to_sibling.append((l, names, tag, started))
        token = started[4]
        for extra in tokens:
            token = token + extra
        return token

    broadcasts = []

    def on_small(l, layer_small):
        values = [val for _, _, val in layer_small]
        started = broadcast_start(_pack(values, 128, f32, 8), name=f"small_start_l{l}")
        broadcasts.append((l, [(n, index, val.shape) for n, index, val in layer_small], started))
        return started[4]

    small["attn_norm"] = small["attn_norm"] + token[0, 0]
    _, grad_x, grads = local_step(x[0], positions[0], loss_target[0], whole, small, big_of, on_grads, on_small)
    chip_sums_on_their_way(grad_x)

    device = (4 * lax.axis_index("x") + 2 * lax.axis_index("y") + lax.axis_index("c")).astype(jnp.int32).reshape(1)
    summed = {n: [None] * len(grads[n]) for n in SMALL + MED}
    summed["loss"] = [None]
    for l, entries, (send_sems, recv_sems, thru, land, _) in broadcasts:
        own, land = broadcast_wait(send_sems, recv_sems, thru, land, grad_x, name=f"small_wait_l{l}")
        total = sum8(land, own, device, name=f"small_sum_l{l}")
        for (n, index, _), val in zip(entries, _unpack(total, [shape for _, _, shape in entries])):
            summed[n][index] = val
    gsum = {}
    for n in SMALL + MED:
        g = jnp.stack(summed[n])
        if n in MED:
            ax = _shard_axis(n)
            width = wts[n].shape[ax]
            g = lax.dynamic_slice_in_dim(g, chip * width, width, axis=ax)
        gsum[n] = g.reshape(wts[n].shape)
    where = jnp.stack([lax.axis_index("c"), chip]).astype(jnp.int32)
    bufs, layout = {}, []
    for l, names, tag, (send_sems, recv_sems, thru, lands, _) in exchanges:
        own, arrived = scatter_wait(send_sems, recv_sems, thru, lands, grad_x, name="scatter_wait_" + tag)
        for n, mine, theirs in zip(names, own, arrived):
            rows = 2 * theirs.shape[1]
            bufs[n] = sum4_into(theirs, mine, bufs.get(n), where, layer=l, total_rows=DEPTH * rows,
                                name=f"sum_chips_{n}_{l}")
            layout.append((BIG.index(n), l * rows, rows))
    reduced = join_halves([bufs[n] for n in BIG], layout, name="join_halves")

    out_g, out_d, out_m, out_v = {}, {}, {}, {}
    for q, n in enumerate(BIG):
        shp = wts[n].shape
        as2d = lambda a: a.reshape(-1, shp[-1])
        g2d = w_in_window_cols(reduced[q], chip) if n == "w_in" else reduced[q]
        res = adamw(as2d(wts[n]), g2d, as2d(mom[n]), as2d(var[n]), name="adamw_" + n)
        out_g[n], out_d[n], out_m[n], out_v[n] = [r.reshape(shp) for r in res]
    sm_all = SMALL + MED
    flat2d = lambda a: a.reshape(-1, a.shape[-1])
    res = adamw_many([flat2d(wts[n]) for n in sm_all], [flat2d(gsum[n]) for n in sm_all],
                     [flat2d(mom[n]) for n in sm_all], [flat2d(var[n]) for n in sm_all], name="adamw_small")
    for tgt, vals in zip((out_d, out_m, out_v), res):
        for n, val in zip(sm_all, vals):
            tgt[n] = val.reshape(wts[n].shape)
    out_g.update({n: gsum[n] for n in sm_all})
    loss = summed["loss"][0].reshape(())
    return (loss, grad_x[None], *[out_g[n] for n in WEIGHTS], *[out_d[n] for n in WEIGHTS],
            *[out_m[n] for n in WEIGHTS], *[out_v[n] for n in WEIGHTS])
```
